```python
import math
import jax, jax.numpy as jnp
from jax import lax
import numpy as np

D_MODEL = 2048
BATCH = 8
SEQ = 2048
DEPTH = 1

MIX_WIDTH = D_MODEL
GDN_WIDTH = MIX_WIDTH // 2
POOL_WIDTH = MIX_WIDTH - GDN_WIDTH
GDN_HEAD_DIM = 128
GDN_HEADS = GDN_WIDTH // GDN_HEAD_DIM
CONV_K = 4
CHUNK = 64
POOL_WINDOWS = (2, 4, 8, 16)
POOL_GROUPS = len(POOL_WINDOWS)
POOL_GROUP_DIM = POOL_WIDTH // POOL_GROUPS
MEM_LEN = 256
XATTN_HEADS = 4
XATTN_HEAD_DIM = D_MODEL // XATTN_HEADS
D_FF = 4 * D_MODEL
IN_COLS = 4 * GDN_WIDTH + 2 * GDN_HEADS + POOL_WIDTH
DEEPNORM_ALPHA = (2.0 * DEPTH) ** 0.25
DEEPNORM_BETA = (8.0 * DEPTH) ** -0.25
LN_EPS = 1e-5
NORM_EPS = 1e-6

kernel_name = "hybrid_gdn_pool_deepnorm_layer"


def layer_norm(x, g, b):
    xf = x.astype(jnp.float32)
    mu = jnp.mean(xf, axis=-1, keepdims=True)
    xc = xf - mu
    var = jnp.mean(xc * xc, axis=-1, keepdims=True)
    y = xc * lax.rsqrt(var + LN_EPS) * g.astype(jnp.float32) + b.astype(jnp.float32)
    return y.astype(x.dtype)


def l2norm(x):
    return x * lax.rsqrt(jnp.sum(x * x, axis=-1, keepdims=True) + NORM_EPS)


def causal_dwconv(x, w):
    c = x.shape[-1]
    return lax.conv_general_dilated(
        x, w.astype(x.dtype)[:, None, :], window_strides=(1,), padding=[(CONV_K - 1, 0)],
        dimension_numbers=("NWC", "WIO", "NWC"), feature_group_count=c)


def chunk_gated_delta_rule(q, k, v, g, beta):
    bsz, t_len, h, dk = q.shape
    dv = v.shape[-1]
    n = t_len // CHUNK

    def to_chunks(u):
        return u.reshape(bsz, n, CHUNK, h, u.shape[-1]).transpose(1, 0, 3, 2, 4)

    q = to_chunks(q * (dk ** -0.5))
    k = to_chunks(k)
    v = to_chunks(v)
    g = g.reshape(bsz, n, CHUNK, h).transpose(1, 0, 3, 2)
    beta = beta.reshape(bsz, n, CHUNK, h).transpose(1, 0, 3, 2)
    g = jnp.cumsum(g, axis=-1)

    idx = jnp.arange(CHUNK)
    lower_incl = idx[:, None] >= idx[None, :]
    strict = idx[:, None] > idx[None, :]
    diff = g[..., :, None] - g[..., None, :]
    decay = jnp.where(lower_incl, jnp.exp(jnp.where(lower_incl, diff, 0.0)), 0.0)

    k_beta = k * beta[..., None]
    v_beta = v * beta[..., None]
    L = jnp.where(strict, jnp.einsum("nbhcd,nbhmd->nbhcm", k_beta, k) * decay, 0.0)
    eye = jnp.eye(CHUNK, dtype=jnp.float32)
    rhs = jnp.concatenate([v_beta, k_beta * jnp.exp(g)[..., None]], axis=-1)
    sol = lax.linalg.triangular_solve(eye + L, rhs, left_side=True, lower=True, unit_diagonal=True)
    u, w = sol[..., :dv], sol[..., dv:]
    attn_intra = jnp.where(lower_incl, jnp.einsum("nbhcd,nbhmd->nbhcm", q, k) * decay, 0.0)

    def step(state, inp):
        q_c, k_c, u_c, w_c, g_c, a_c = inp
        v_new = u_c - jnp.einsum("bhck,bhkv->bhcv", w_c, state)
        o = (jnp.einsum("bhck,bhkv->bhcv", q_c * jnp.exp(g_c)[..., None], state)
             + jnp.einsum("bhcm,bhmv->bhcv", a_c, v_new))
        g_last = g_c[..., -1]
        k_dec = k_c * jnp.exp(g_last[..., None] - g_c)[..., None]
        state = state * jnp.exp(g_last)[..., None, None] + jnp.einsum("bhck,bhcv->bhkv", k_dec, v_new)
        return state, o

    s0 = jnp.zeros((bsz, h, dk, dv), jnp.float32)
    _, o = lax.scan(step, s0, (q, k, u, w, g, attn_intra))
    return o.transpose(1, 0, 3, 2, 4).reshape(bsz, t_len, h, dv)


def gated_deltanet(qkv, z, b, a, conv_w, a_log, dt_bias, norm_w):
    bsz, t_len, _ = qkv.shape
    qkv = jax.nn.silu(causal_dwconv(qkv, conv_w)).astype(jnp.float32)
    q, k, v = jnp.split(qkv, 3, axis=-1)
    shp = (bsz, t_len, GDN_HEADS, GDN_HEAD_DIM)
    q = l2norm(q.reshape(shp))
    k = l2norm(k.reshape(shp))
    v = v.reshape(shp)
    beta = jax.nn.sigmoid(b.astype(jnp.float32))
    g = -jnp.exp(a_log.astype(jnp.float32)) * jax.nn.softplus(
        a.astype(jnp.float32) + dt_bias.astype(jnp.float32))
    o = chunk_gated_delta_rule(q, k, v, g, beta)
    o = o * lax.rsqrt(jnp.mean(o * o, axis=-1, keepdims=True) + NORM_EPS) * norm_w.astype(jnp.float32)
    o = o * jax.nn.silu(z.astype(jnp.float32).reshape(shp))
    return o.reshape(bsz, t_len, GDN_WIDTH).astype(z.dtype)


def multiscale_pool(p, pool_w, pool_scale):
    bsz, t_len, _ = p.shape
    pg = p.astype(jnp.float32).reshape(bsz, t_len, POOL_GROUPS, POOL_GROUP_DIM)
    cs = jnp.cumsum(pg, axis=1)
    pos = jnp.arange(t_len)
    means = []
    for gi, win in enumerate(POOL_WINDOWS):
        c = cs[:, :, gi]
        lag = jnp.pad(c[:, : t_len - win], ((0, 0), (win, 0), (0, 0)))
        cnt = jnp.minimum(pos + 1, win).astype(jnp.float32)[None, :, None]
        means.append((c - lag) / cnt)
    pooled = jnp.stack(means, axis=2) - pg
    mixed = jnp.einsum("btgc,gcd->btgd", pooled.astype(p.dtype), pool_w)
    return mixed.reshape(bsz, t_len, POOL_WIDTH) * pool_scale


def memory_cross_attention(h, mem, wq, wk, wv, wo):
    bsz, t_len, _ = h.shape
    q = (h @ wq).reshape(bsz, t_len, XATTN_HEADS, XATTN_HEAD_DIM)
    k = (mem @ wk).reshape(bsz, mem.shape[1], XATTN_HEADS, XATTN_HEAD_DIM)
    v = (mem @ wv).reshape(bsz, mem.shape[1], XATTN_HEADS, XATTN_HEAD_DIM)
    s = jnp.einsum("bqhd,bmhd->bhqm", q, k).astype(jnp.float32) * (XATTN_HEAD_DIM ** -0.5)
    p = jax.nn.softmax(s, axis=-1).astype(v.dtype)
    o = jnp.einsum("bhqm,bmhd->bqhd", p, v).reshape(bsz, t_len, D_MODEL)
    return o @ wo


def _fwd_setup_inputs(seed: int = 0) -> dict:
    key = jax.random.key(seed)
    ks = jax.random.split(key, 24)
    f32 = jnp.float32
    nrm = lambda k, shape, scale: jax.random.normal(k, shape, f32) * scale
    x = nrm(ks[0], (BATCH, SEQ, D_MODEL), 1.0)
    mem = nrm(ks[1], (BATCH, MEM_LEN, D_MODEL), 1.0)
    w_in = nrm(ks[2], (DEPTH, D_MODEL, IN_COLS), D_MODEL ** -0.5)
    conv_w = nrm(ks[3], (DEPTH, CONV_K, 3 * GDN_WIDTH), CONV_K ** -0.5)
    a_log = jnp.log(jax.random.uniform(ks[4], (DEPTH, GDN_HEADS), f32, 1.0, 16.0))
    dt = jnp.exp(jax.random.uniform(ks[5], (DEPTH, GDN_HEADS), f32, math.log(1e-3), math.log(1e-1)))
    dt_bias = dt + jnp.log(-jnp.expm1(-dt))
    gdn_norm_w = 1.0 + nrm(ks[6], (DEPTH, GDN_HEAD_DIM), 0.02)
    pool_w = nrm(ks[7], (DEPTH, POOL_GROUPS, POOL_GROUP_DIM, POOL_GROUP_DIM), POOL_GROUP_DIM ** -0.5)
    pool_scale = 1.0 + nrm(ks[8], (DEPTH, POOL_WIDTH), 0.1)
    w_out = nrm(ks[9], (DEPTH, MIX_WIDTH, D_MODEL), MIX_WIDTH ** -0.5 * DEEPNORM_BETA)
    ln1_g = 1.0 + nrm(ks[10], (DEPTH, D_MODEL), 0.02)
    ln1_b = nrm(ks[11], (DEPTH, D_MODEL), 0.02)
    xq_w = nrm(ks[12], (DEPTH, D_MODEL, D_MODEL), D_MODEL ** -0.5)
    xk_w = nrm(ks[13], (DEPTH, D_MODEL, D_MODEL), D_MODEL ** -0.5)
    xv_w = nrm(ks[14], (DEPTH, D_MODEL, D_MODEL), D_MODEL ** -0.5)
    xo_w = nrm(ks[15], (DEPTH, D_MODEL, D_MODEL), D_MODEL ** -0.5 * DEEPNORM_BETA)
    ln2_g = 1.0 + nrm(ks[16], (DEPTH, D_MODEL), 0.02)
    ln2_b = nrm(ks[17], (DEPTH, D_MODEL), 0.02)
    w_up = nrm(ks[18], (DEPTH, D_MODEL, D_FF), D_MODEL ** -0.5)
    w_down = nrm(ks[19], (DEPTH, D_FF, D_MODEL), D_FF ** -0.5 * DEEPNORM_BETA)
    ln3_g = 1.0 + nrm(ks[20], (DEPTH, D_MODEL), 0.02)
    ln3_b = nrm(ks[21], (DEPTH, D_MODEL), 0.02)
    return {"x": x, "mem": mem, "w_in": w_in, "conv_w": conv_w, "a_log": a_log, "dt_bias": dt_bias,
            "gdn_norm_w": gdn_norm_w, "pool_w": pool_w, "pool_scale": pool_scale, "w_out": w_out,
            "ln1_g": ln1_g, "ln1_b": ln1_b, "xq_w": xq_w, "xk_w": xk_w, "xv_w": xv_w, "xo_w": xo_w,
            "ln2_g": ln2_g, "ln2_b": ln2_b, "w_up": w_up, "w_down": w_down, "ln3_g": ln3_g, "ln3_b": ln3_b}


def _fwd_reference(x, mem, w_in, conv_w, a_log, dt_bias, gdn_norm_w, pool_w, pool_scale, w_out,
              ln1_g, ln1_b, xq_w, xk_w, xv_w, xo_w, ln2_g, ln2_b, w_up, w_down, ln3_g, ln3_b):
    W, H = GDN_WIDTH, GDN_HEADS
    h = x
    for l in range(DEPTH):
        proj = h @ w_in[l]
        qkv = proj[..., : 3 * W]
        z = proj[..., 3 * W: 4 * W]
        b = proj[..., 4 * W: 4 * W + H]
        a = proj[..., 4 * W + H: 4 * W + 2 * H]
        p = proj[..., 4 * W + 2 * H:]
        o_gdn = gated_deltanet(qkv, z, b, a, conv_w[l], a_log[l], dt_bias[l], gdn_norm_w[l])
        o_pool = multiscale_pool(p, pool_w[l], pool_scale[l])
        mix = jnp.concatenate([o_gdn, o_pool], axis=-1) @ w_out[l]
        h = layer_norm(DEEPNORM_ALPHA * h + mix, ln1_g[l], ln1_b[l])
        xa = memory_cross_attention(h, mem, xq_w[l], xk_w[l], xv_w[l], xo_w[l])
        h = layer_norm(DEEPNORM_ALPHA * h + xa, ln2_g[l], ln2_b[l])
        ff = jnp.square(jax.nn.relu(h @ w_up[l])) @ w_down[l]
        h = layer_norm(DEEPNORM_ALPHA * h + ff, ln3_g[l], ln3_b[l])
    return h


import jax as _jax
import jax.numpy as _jnp

TWIN_FORMAT = 'train_step'
FWD_PARAMS = ['x', 'mem', 'w_in', 'conv_w', 'a_log', 'dt_bias', 'gdn_norm_w', 'pool_w', 'pool_scale', 'w_out', 'ln1_g', 'ln1_b', 'xq_w', 'xk_w', 'xv_w', 'xo_w', 'ln2_g', 'ln2_b', 'w_up', 'w_down', 'ln3_g', 'ln3_b']
TWIN_WEIGHTS = ['w_in', 'conv_w', 'a_log', 'dt_bias', 'gdn_norm_w', 'pool_w', 'pool_scale', 'w_out', 'ln1_g', 'ln1_b', 'xq_w', 'xk_w', 'xv_w', 'xo_w', 'ln2_g', 'ln2_b', 'w_up', 'w_down', 'ln3_g', 'ln3_b']
TWIN_DIFF_INPUT = 'x'
TWIN_INPUTS = ['x', 'mem', 'w_in', 'conv_w', 'a_log', 'dt_bias', 'gdn_norm_w', 'pool_w', 'pool_scale', 'w_out', 'ln1_g', 'ln1_b', 'xq_w', 'xk_w', 'xv_w', 'xo_w', 'ln2_g', 'ln2_b', 'w_up', 'w_down', 'ln3_g', 'ln3_b', 'loss_target', 'm_w_in', 'm_conv_w', 'm_a_log', 'm_dt_bias', 'm_gdn_norm_w', 'm_pool_w', 'm_pool_scale', 'm_w_out', 'm_ln1_g', 'm_ln1_b', 'm_xq_w', 'm_xk_w', 'm_xv_w', 'm_xo_w', 'm_ln2_g', 'm_ln2_b', 'm_w_up', 'm_w_down', 'm_ln3_g', 'm_ln3_b', 'v_w_in', 'v_conv_w', 'v_a_log', 'v_dt_bias', 'v_gdn_norm_w', 'v_pool_w', 'v_pool_scale', 'v_w_out', 'v_ln1_g', 'v_ln1_b', 'v_xq_w', 'v_xk_w', 'v_xv_w', 'v_xo_w', 'v_ln2_g', 'v_ln2_b', 'v_w_up', 'v_w_down', 'v_ln3_g', 'v_ln3_b']
TWIN_OUTPUTS = ['loss', 'grad_x', 'grad_w_in', 'grad_conv_w', 'grad_a_log', 'grad_dt_bias', 'grad_gdn_norm_w', 'grad_pool_w', 'grad_pool_scale', 'grad_w_out', 'grad_ln1_g', 'grad_ln1_b', 'grad_xq_w', 'grad_xk_w', 'grad_xv_w', 'grad_xo_w', 'grad_ln2_g', 'grad_ln2_b', 'grad_w_up', 'grad_w_down', 'grad_ln3_g', 'grad_ln3_b', 'delta_w_in', 'delta_conv_w', 'delta_a_log', 'delta_dt_bias', 'delta_gdn_norm_w', 'delta_pool_w', 'delta_pool_scale', 'delta_w_out', 'delta_ln1_g', 'delta_ln1_b', 'delta_xq_w', 'delta_xk_w', 'delta_xv_w', 'delta_xo_w', 'delta_ln2_g', 'delta_ln2_b', 'delta_w_up', 'delta_w_down', 'delta_ln3_g', 'delta_ln3_b', 'new_m_w_in', 'new_m_conv_w', 'new_m_a_log', 'new_m_dt_bias', 'new_m_gdn_norm_w', 'new_m_pool_w', 'new_m_pool_scale', 'new_m_w_out', 'new_m_ln1_g', 'new_m_ln1_b', 'new_m_xq_w', 'new_m_xk_w', 'new_m_xv_w', 'new_m_xo_w', 'new_m_ln2_g', 'new_m_ln2_b', 'new_m_w_up', 'new_m_w_down', 'new_m_ln3_g', 'new_m_ln3_b', 'new_v_w_in', 'new_v_conv_w', 'new_v_a_log', 'new_v_dt_bias', 'new_v_gdn_norm_w', 'new_v_pool_w', 'new_v_pool_scale', 'new_v_w_out', 'new_v_ln1_g', 'new_v_ln1_b', 'new_v_xq_w', 'new_v_xk_w', 'new_v_xv_w', 'new_v_xo_w', 'new_v_ln2_g', 'new_v_ln2_b', 'new_v_w_up', 'new_v_w_down', 'new_v_ln3_g', 'new_v_ln3_b']
TWIN_LEAF_KINDS = {'loss': 'loss', 'grad_x': 'grad_x', 'grad_w_in': 'grad_w', 'grad_conv_w': 'grad_w', 'grad_a_log': 'grad_w', 'grad_dt_bias': 'grad_w', 'grad_gdn_norm_w': 'grad_w', 'grad_pool_w': 'grad_w', 'grad_pool_scale': 'grad_w', 'grad_w_out': 'grad_w', 'grad_ln1_g': 'grad_w', 'grad_ln1_b': 'grad_w', 'grad_xq_w': 'grad_w', 'grad_xk_w': 'grad_w', 'grad_xv_w': 'grad_w', 'grad_xo_w': 'grad_w', 'grad_ln2_g': 'grad_w', 'grad_ln2_b': 'grad_w', 'grad_w_up': 'grad_w', 'grad_w_down': 'grad_w', 'grad_ln3_g': 'grad_w', 'grad_ln3_b': 'grad_w', 'delta_w_in': 'delta_w', 'delta_conv_w': 'delta_w', 'delta_a_log': 'delta_w', 'delta_dt_bias': 'delta_w', 'delta_gdn_norm_w': 'delta_w', 'delta_pool_w': 'delta_w', 'delta_pool_scale': 'delta_w', 'delta_w_out': 'delta_w', 'delta_ln1_g': 'delta_w', 'delta_ln1_b': 'delta_w', 'delta_xq_w': 'delta_w', 'delta_xk_w': 'delta_w', 'delta_xv_w': 'delta_w', 'delta_xo_w': 'delta_w', 'delta_ln2_g': 'delta_w', 'delta_ln2_b': 'delta_w', 'delta_w_up': 'delta_w', 'delta_w_down': 'delta_w', 'delta_ln3_g': 'delta_w', 'delta_ln3_b': 'delta_w', 'new_m_w_in': 'new_m', 'new_m_conv_w': 'new_m', 'new_m_a_log': 'new_m', 'new_m_dt_bias': 'new_m', 'new_m_gdn_norm_w': 'new_m', 'new_m_pool_w': 'new_m', 'new_m_pool_scale': 'new_m', 'new_m_w_out': 'new_m', 'new_m_ln1_g': 'new_m', 'new_m_ln1_b': 'new_m', 'new_m_xq_w': 'new_m', 'new_m_xk_w': 'new_m', 'new_m_xv_w': 'new_m', 'new_m_xo_w': 'new_m', 'new_m_ln2_g': 'new_m', 'new_m_ln2_b': 'new_m', 'new_m_w_up': 'new_m', 'new_m_w_down': 'new_m', 'new_m_ln3_g': 'new_m', 'new_m_ln3_b': 'new_m', 'new_v_w_in': 'new_v', 'new_v_conv_w': 'new_v', 'new_v_a_log': 'new_v', 'new_v_dt_bias': 'new_v', 'new_v_gdn_norm_w': 'new_v', 'new_v_pool_w': 'new_v', 'new_v_pool_scale': 'new_v', 'new_v_w_out': 'new_v', 'new_v_ln1_g': 'new_v', 'new_v_ln1_b': 'new_v', 'new_v_xq_w': 'new_v', 'new_v_xk_w': 'new_v', 'new_v_xv_w': 'new_v', 'new_v_xo_w': 'new_v', 'new_v_ln2_g': 'new_v', 'new_v_ln2_b': 'new_v', 'new_v_w_up': 'new_v', 'new_v_w_down': 'new_v', 'new_v_ln3_g': 'new_v', 'new_v_ln3_b': 'new_v'}


def _forward(args):
    return _fwd_reference(*[args[k] for k in FWD_PARAMS])


def _output_shape():
    out = _jax.eval_shape(lambda: _forward(_fwd_setup_inputs(0)))
    return out.shape, out.dtype

N_MICROBATCH = 1
ADAM_LR = 0.001
ADAM_B1 = 0.9
ADAM_B2 = 0.999
ADAM_EPS = 1e-08
ADAM_WD = 0.01
ADAM_STEP = 10
PER_EXAMPLE_BATCH_AXIS = {'x': 0, 'mem': 0, 'loss_target': 0}
SHARED_INPUTS = []
_WEIGHT_DTYPES = {'w_in': _jnp.float32, 'conv_w': _jnp.float32, 'a_log': _jnp.float32, 'dt_bias': _jnp.float32, 'gdn_norm_w': _jnp.float32, 'pool_w': _jnp.float32, 'pool_scale': _jnp.float32, 'w_out': _jnp.float32, 'ln1_g': _jnp.float32, 'ln1_b': _jnp.float32, 'xq_w': _jnp.float32, 'xk_w': _jnp.float32, 'xv_w': _jnp.float32, 'xo_w': _jnp.float32, 'ln2_g': _jnp.float32, 'ln2_b': _jnp.float32, 'w_up': _jnp.float32, 'w_down': _jnp.float32, 'ln3_g': _jnp.float32, 'ln3_b': _jnp.float32}
MOMENT_SCALE = {'w_in': 1.809243e-02, 'conv_w': 1.370532e-02, 'a_log': 5.530538e-02, 'dt_bias': 5.329453e-02, 'gdn_norm_w': 5.278787e-02, 'pool_w': 2.701032e-02, 'pool_scale': 2.704190e-02, 'w_out': 3.941709e-02, 'ln1_g': 2.050628e-01, 'ln1_b': 1.493540e-01, 'xq_w': 3.369184e-03, 'xk_w': 3.378759e-03, 'xv_w': 3.859174e-03, 'xo_w': 6.403223e-03, 'ln2_g': 2.057305e-01, 'ln2_b': 1.498628e-01, 'w_up': 1.896073e-02, 'w_down': 7.150828e-02, 'ln3_g': 8.025125e+00, 'ln3_b': 1.687652e+00}


def _to_microbatches(a, axis):
    t = _jnp.moveaxis(a, axis, 0)
    t = t.reshape((N_MICROBATCH, t.shape[0] // N_MICROBATCH) + t.shape[1:])
    return _jnp.moveaxis(t, 1, axis + 1)


def setup_inputs(seed: int = 0) -> dict:
    inp = _fwd_setup_inputs(seed)
    key = _jax.random.fold_in(_jax.random.key(seed), 7919)
    shape, _ = _output_shape()
    out = dict(inp)
    out["loss_target"] = _jax.random.normal(_jax.random.fold_in(key, 0), shape, _jnp.float32)
    for i, name in enumerate(TWIN_WEIGHTS):
        w = inp[name].astype(_jnp.float32)
        if MOMENT_SCALE is None:
            s = _jnp.sqrt(_jnp.mean(_jnp.square(w)) + 1e-30)
        else:
            s = MOMENT_SCALE[name]
        km, kv = _jax.random.split(_jax.random.fold_in(key, i + 1))
        out[name] = w
        out["m_" + name] = s * _jax.random.normal(km, w.shape, _jnp.float32)
        out["v_" + name] = (s * s) * _jax.random.uniform(kv, w.shape, _jnp.float32, 0.5, 1.5)
    if N_MICROBATCH > 1:
        for name, axis in PER_EXAMPLE_BATCH_AXIS.items():
            out[name] = _to_microbatches(out[name], axis)
    return {'x': out['x'], 'mem': out['mem'], 'w_in': out['w_in'], 'conv_w': out['conv_w'], 'a_log': out['a_log'], 'dt_bias': out['dt_bias'], 'gdn_norm_w': out['gdn_norm_w'], 'pool_w': out['pool_w'], 'pool_scale': out['pool_scale'], 'w_out': out['w_out'], 'ln1_g': out['ln1_g'], 'ln1_b': out['ln1_b'], 'xq_w': out['xq_w'], 'xk_w': out['xk_w'], 'xv_w': out['xv_w'], 'xo_w': out['xo_w'], 'ln2_g': out['ln2_g'], 'ln2_b': out['ln2_b'], 'w_up': out['w_up'], 'w_down': out['w_down'], 'ln3_g': out['ln3_g'], 'ln3_b': out['ln3_b'], 'loss_target': out['loss_target'], 'm_w_in': out['m_w_in'], 'm_conv_w': out['m_conv_w'], 'm_a_log': out['m_a_log'], 'm_dt_bias': out['m_dt_bias'], 'm_gdn_norm_w': out['m_gdn_norm_w'], 'm_pool_w': out['m_pool_w'], 'm_pool_scale': out['m_pool_scale'], 'm_w_out': out['m_w_out'], 'm_ln1_g': out['m_ln1_g'], 'm_ln1_b': out['m_ln1_b'], 'm_xq_w': out['m_xq_w'], 'm_xk_w': out['m_xk_w'], 'm_xv_w': out['m_xv_w'], 'm_xo_w': out['m_xo_w'], 'm_ln2_g': out['m_ln2_g'], 'm_ln2_b': out['m_ln2_b'], 'm_w_up': out['m_w_up'], 'm_w_down': out['m_w_down'], 'm_ln3_g': out['m_ln3_g'], 'm_ln3_b': out['m_ln3_b'], 'v_w_in': out['v_w_in'], 'v_conv_w': out['v_conv_w'], 'v_a_log': out['v_a_log'], 'v_dt_bias': out['v_dt_bias'], 'v_gdn_norm_w': out['v_gdn_norm_w'], 'v_pool_w': out['v_pool_w'], 'v_pool_scale': out['v_pool_scale'], 'v_w_out': out['v_w_out'], 'v_ln1_g': out['v_ln1_g'], 'v_ln1_b': out['v_ln1_b'], 'v_xq_w': out['v_xq_w'], 'v_xk_w': out['v_xk_w'], 'v_xv_w': out['v_xv_w'], 'v_xo_w': out['v_xo_w'], 'v_ln2_g': out['v_ln2_g'], 'v_ln2_b': out['v_ln2_b'], 'v_w_up': out['v_w_up'], 'v_w_down': out['v_w_down'], 'v_ln3_g': out['v_ln3_g'], 'v_ln3_b': out['v_ln3_b']}


def _loss(weights, diff, rest, loss_target):
    with _jax.named_scope("forward"):
        args = {**rest, TWIN_DIFF_INPUT: diff, **{k: w.astype(_WEIGHT_DTYPES[k]) for k, w in weights.items()}}
        y = _forward(args)
    with _jax.named_scope("loss_head"):
        err = _jnp.square(y.astype(_jnp.float32) - loss_target)
        return 0.5 * _jnp.sum(_jnp.mean(err, axis=-1)) if err.ndim else 0.5 * err


def _adamw(w, g, m, v):
    m = ADAM_B1 * m + (1.0 - ADAM_B1) * g
    v = ADAM_B2 * v + (1.0 - ADAM_B2) * _jnp.square(g)
    m_hat = m / (1.0 - ADAM_B1 ** ADAM_STEP)
    v_hat = v / (1.0 - ADAM_B2 ** ADAM_STEP)
    delta = -ADAM_LR * (m_hat / (_jnp.sqrt(v_hat) + ADAM_EPS) + ADAM_WD * w)
    return delta, m, v


def reference(x, mem, w_in, conv_w, a_log, dt_bias, gdn_norm_w, pool_w, pool_scale, w_out, ln1_g, ln1_b, xq_w, xk_w, xv_w, xo_w, ln2_g, ln2_b, w_up, w_down, ln3_g, ln3_b, loss_target, m_w_in, m_conv_w, m_a_log, m_dt_bias, m_gdn_norm_w, m_pool_w, m_pool_scale, m_w_out, m_ln1_g, m_ln1_b, m_xq_w, m_xk_w, m_xv_w, m_xo_w, m_ln2_g, m_ln2_b, m_w_up, m_w_down, m_ln3_g, m_ln3_b, v_w_in, v_conv_w, v_a_log, v_dt_bias, v_gdn_norm_w, v_pool_w, v_pool_scale, v_w_out, v_ln1_g, v_ln1_b, v_xq_w, v_xk_w, v_xv_w, v_xo_w, v_ln2_g, v_ln2_b, v_w_up, v_w_down, v_ln3_g, v_ln3_b):
    given = dict(x=x, mem=mem, w_in=w_in, conv_w=conv_w, a_log=a_log, dt_bias=dt_bias, gdn_norm_w=gdn_norm_w, pool_w=pool_w, pool_scale=pool_scale, w_out=w_out, ln1_g=ln1_g, ln1_b=ln1_b, xq_w=xq_w, xk_w=xk_w, xv_w=xv_w, xo_w=xo_w, ln2_g=ln2_g, ln2_b=ln2_b, w_up=w_up, w_down=w_down, ln3_g=ln3_g, ln3_b=ln3_b, loss_target=loss_target, m_w_in=m_w_in, m_conv_w=m_conv_w, m_a_log=m_a_log, m_dt_bias=m_dt_bias, m_gdn_norm_w=m_gdn_norm_w, m_pool_w=m_pool_w, m_pool_scale=m_pool_scale, m_w_out=m_w_out, m_ln1_g=m_ln1_g, m_ln1_b=m_ln1_b, m_xq_w=m_xq_w, m_xk_w=m_xk_w, m_xv_w=m_xv_w, m_xo_w=m_xo_w, m_ln2_g=m_ln2_g, m_ln2_b=m_ln2_b, m_w_up=m_w_up, m_w_down=m_w_down, m_ln3_g=m_ln3_g, m_ln3_b=m_ln3_b, v_w_in=v_w_in, v_conv_w=v_conv_w, v_a_log=v_a_log, v_dt_bias=v_dt_bias, v_gdn_norm_w=v_gdn_norm_w, v_pool_w=v_pool_w, v_pool_scale=v_pool_scale, v_w_out=v_w_out, v_ln1_g=v_ln1_g, v_ln1_b=v_ln1_b, v_xq_w=v_xq_w, v_xk_w=v_xk_w, v_xv_w=v_xv_w, v_xo_w=v_xo_w, v_ln2_g=v_ln2_g, v_ln2_b=v_ln2_b, v_w_up=v_w_up, v_w_down=v_w_down, v_ln3_g=v_ln3_g, v_ln3_b=v_ln3_b)
    weights = {n: given[n] for n in TWIN_WEIGHTS}
    shared = {n: given[n] for n in SHARED_INPUTS}
    per_example = {n: given[n] for n in ['x', 'mem']}
    grad_fn = _jax.value_and_grad(_loss, argnums=(0, 1))

    def one_microbatch(ex, loss_target):
        ex = dict(ex)
        diff = ex.pop(TWIN_DIFF_INPUT)
        return grad_fn(weights, diff, {**shared, **ex}, loss_target)

    if N_MICROBATCH == 1:
        loss, (grad_w, grad_x) = one_microbatch(per_example, given["loss_target"])
    else:
        def body(carry, xs):
            loss_sum, grad_sum = carry
            l_k, (gw_k, gx_k) = one_microbatch(xs[0], xs[1])
            with _jax.named_scope("update"):
                return (loss_sum + l_k, _jax.tree.map(_jnp.add, grad_sum, gw_k)), gx_k

        init = (_jnp.zeros((), _jnp.float32), _jax.tree.map(_jnp.zeros_like, weights))
        (loss, grad_w), grad_x = _jax.lax.scan(body, init, (per_example, given["loss_target"]))
    with _jax.named_scope("update"):
        delta_w, new_m, new_v = {}, {}, {}
        for n in TWIN_WEIGHTS:
            delta_w[n], new_m[n], new_v[n] = _adamw(weights[n], grad_w[n], given["m_" + n], given["v_" + n])
    return (loss, grad_x, *[grad_w[n] for n in TWIN_WEIGHTS], *[delta_w[n] for n in TWIN_WEIGHTS],
            *[new_m[n] for n in TWIN_WEIGHTS], *[new_v[n] for n in TWIN_WEIGHTS])
```

```python
import functools
import math

import jax
import jax.numpy as jnp
from jax import lax
from jax.experimental import pallas as pl
from jax.experimental.pallas import tpu as pltpu

F32 = jnp.float32
BF16 = jnp.bfloat16
MESH = pl.DeviceIdType.MESH

N_DEV = 8
D_MODEL = 2048
GDN_WIDTH = 1024
GDN_HEADS = 8
HEAD_DIM = 128
CONV_K = 4
CHUNK = 64
POOL_GROUPS = 4
POOL_GROUP_DIM = 256
MEM_LEN = 256
XATTN_HEADS = 4
XATTN_HEAD_DIM = 512
D_FF = 8192
IN_COLS = 5136
ALPHA = 2.0 ** 0.25
LN_EPS = 1e-5
NORM_EPS = 1e-6

LANE = 128
QKV_COLS = 3 * GDN_WIDTH
Z_OFF = QKV_COLS
BA_OFF = 4 * GDN_WIDTH
POOL_OFF = BA_OFF + 2 * LANE
PROJ_COLS = POOL_OFF + GDN_WIDTH
Z_BLK = Z_OFF // LANE
BA_BLK = BA_OFF // LANE
POOL_BLK = POOL_OFF // POOL_GROUP_DIM

ADAM_LR = 0.001
ADAM_B1 = 0.9
ADAM_B2 = 0.999
ADAM_EPS = 1e-08
ADAM_WD = 0.01
ADAM_STEP = 10

VMEM_LIMIT_BYTES = 48 * 1024 * 1024


def _params(*sem):
    return pltpu.CompilerParams(dimension_semantics=sem if sem else None, vmem_limit_bytes=VMEM_LIMIT_BYTES)


def _make_dots(cast, precision):
    def dg(a, b, ca, cb):
        if cast is not None:
            a = a.astype(cast)
            b = b.astype(cast)
        return lax.dot_general(a, b, (((ca,), (cb,)), ((), ())), precision=precision, preferred_element_type=F32)

    def nn_(a, b):
        return dg(a, b, 1, 0)

    def nt_(a, b):
        return dg(a, b, 1, 1)

    def tn_(a, b):
        return dg(a, b, 0, 0)

    @jax.custom_vjp
    def nn(a, b):
        return nn_(a, b)

    nn.defvjp(lambda a, b: (nn_(a, b), (a, b)), lambda r, g: (nt_(g, r[1]), tn_(r[0], g)))

    @jax.custom_vjp
    def nt(a, b):
        return nt_(a, b)

    nt.defvjp(lambda a, b: (nt_(a, b), (a, b)), lambda r, g: (nn_(g, r[1]), tn_(g, r[0])))

    @jax.custom_vjp
    def tn(a, b):
        return tn_(a, b)

    tn.defvjp(lambda a, b: (tn_(a, b), (a, b)), lambda r, g: (nt_(r[1], g), nn_(r[0], g)))

    return (nn_, nt_, tn_), (nn, nt, tn)


_BDOT_PLAIN, _BDOT_VJP = _make_dots(BF16, None)
_FDOT_PLAIN, _FDOT_VJP = _make_dots(None, lax.Precision.HIGHEST)


def _mm(a, b, *, ta=False, tb=False, out_dtype=F32, tm=512, tn=512, tk=1024, add=None, add_scale=1.0, name):
    m, k = (a.shape[1], a.shape[0]) if ta else a.shape
    n = b.shape[0] if tb else b.shape[1]
    tm, tn, tk = min(tm, m), min(tn, n), min(tk, k)
    assert m % tm == 0 and n % tn == 0 and k % tk == 0, (name, m, n, k)
    nk = k // tk
    dims = (((0 if ta else 1,), (1 if tb else 0,)), ((), ()))

    def body(*refs):
        if add is None:
            a_ref, b_ref, o_ref = refs[:3]
            c_ref = None
            scr = refs[3:]
        else:
            a_ref, b_ref, c_ref, o_ref = refs[:4]
            scr = refs[4:]
        r = lax.dot_general(a_ref[...].astype(BF16), b_ref[...].astype(BF16), dims, preferred_element_type=F32)

        def finish(v):
            if c_ref is not None:
                v = v + add_scale * c_ref[...]
            o_ref[...] = v.astype(out_dtype)

        if nk == 1:
            finish(r)
        else:
            acc = scr[0]
            kk = pl.program_id(2)

            @pl.when(kk == 0)
            def _():
                acc[...] = r

            @pl.when(kk > 0)
            def _():
                acc[...] += r

            @pl.when(kk == nk - 1)
            def _():
                finish(acc[...])

    a_spec = pl.BlockSpec((tk, tm), lambda i, j, kk: (kk, i)) if ta else pl.BlockSpec((tm, tk), lambda i, j, kk: (i, kk))
    b_spec = pl.BlockSpec((tn, tk), lambda i, j, kk: (j, kk)) if tb else pl.BlockSpec((tk, tn), lambda i, j, kk: (kk, j))
    o_spec = pl.BlockSpec((tm, tn), lambda i, j, kk: (i, j))
    in_specs = [a_spec, b_spec]
    args = [a, b]
    if add is not None:
        in_specs.append(o_spec)
        args.append(add)
    return pl.pallas_call(
        body, grid=(m // tm, n // tn, nk), in_specs=in_specs, out_specs=o_spec,
        out_shape=jax.ShapeDtypeStruct((m, n), out_dtype),
        scratch_shapes=[pltpu.VMEM((tm, tn), F32)] if nk > 1 else [],
        compiler_params=_params("parallel", "parallel", "arbitrary"), name=name,
    )(*args)


def _shift_down(v, s):
    if s == 0:
        return v
    row = lax.broadcasted_iota(jnp.int32, v.shape, 0)
    return jnp.where(row >= s, pltpu.roll(v, s, axis=0), 0.0)


def _shift_up(v, s):
    if s == 0:
        return v
    t = v.shape[0]
    row = lax.broadcasted_iota(jnp.int32, v.shape, 0)
    return jnp.where(row < t - s, pltpu.roll(v, t - s, axis=0), 0.0)


def _post_col(j):
    return (j % GDN_HEADS) * 3 + j // GDN_HEADS


def _gdn_prep_fwd(proj, conv_w):
    t = proj.shape[0]

    def body(x_ref, w_ref, o_ref):
        j = pl.program_id(0)
        x = x_ref[...]
        y = jnp.zeros_like(x)
        for tap in range(CONV_K):
            y = y + w_ref[tap:tap + 1, :] * _shift_down(x, CONV_K - 1 - tap)
        c = y * jax.nn.sigmoid(y)
        nrm = c * lax.rsqrt(jnp.sum(c * c, axis=1, keepdims=True) + NORM_EPS)
        o_ref[...] = jnp.where(j < 2 * GDN_HEADS, nrm, c)

    return pl.pallas_call(
        body, grid=(QKV_COLS // LANE,),
        in_specs=[pl.BlockSpec((t, LANE), lambda j: (0, j)), pl.BlockSpec((CONV_K, LANE), lambda j: (0, j))],
        out_specs=pl.BlockSpec((t, LANE), lambda j: (0, _post_col(j))),
        out_shape=jax.ShapeDtypeStruct((t, QKV_COLS), F32),
        compiler_params=_params("parallel"), name="gdn_prep_fwd",
    )(proj, conv_w)


def _gdn_prep_bwd(proj, conv_w, dpost, dproj):
    t = proj.shape[0]

    def body(x_ref, w_ref, d_ref, _, dx_ref, dw_ref):
        j = pl.program_id(0)
        x = x_ref[...]
        xs = [_shift_down(x, CONV_K - 1 - tap) for tap in range(CONV_K)]
        y = jnp.zeros_like(x)
        for tap in range(CONV_K):
            y = y + w_ref[tap:tap + 1, :] * xs[tap]
        sig = jax.nn.sigmoid(y)
        c = y * sig
        r = lax.rsqrt(jnp.sum(c * c, axis=1, keepdims=True) + NORM_EPS)
        nrm = c * r
        d = d_ref[...]
        dc_norm = r * (d - nrm * jnp.sum(d * nrm, axis=1, keepdims=True))
        dc = jnp.where(j < 2 * GDN_HEADS, dc_norm, d)
        dy = dc * (sig * (1.0 + y * (1.0 - sig)))
        dx = jnp.zeros_like(x)
        for tap in range(CONV_K):
            dx = dx + _shift_up(w_ref[tap:tap + 1, :] * dy, CONV_K - 1 - tap)
            dw_ref[tap:tap + 1, :] = jnp.sum(dy * xs[tap], axis=0, keepdims=True)
        dx_ref[...] = dx

    return pl.pallas_call(
        body, grid=(QKV_COLS // LANE,),
        in_specs=[pl.BlockSpec((t, LANE), lambda j: (0, j)), pl.BlockSpec((CONV_K, LANE), lambda j: (0, j)),
                  pl.BlockSpec((t, LANE), lambda j: (0, _post_col(j))), pl.BlockSpec(memory_space=pl.ANY)],
        out_specs=[pl.BlockSpec((t, LANE), lambda j: (0, j)), pl.BlockSpec((CONV_K, LANE), lambda j: (0, j))],
        out_shape=[jax.ShapeDtypeStruct(dproj.shape, F32), jax.ShapeDtypeStruct((CONV_K, QKV_COLS), F32)],
        input_output_aliases={3: 0},
        compiler_params=_params("parallel"), name="gdn_prep_bwd",
    )(proj, conv_w, dpost, dproj)


def _softplus(v):
    return jnp.maximum(v, 0.0) + jnp.log(1.0 + jnp.exp(-jnp.abs(v)))


def _tri_inv(low, nn):
    r = lax.broadcasted_iota(jnp.int32, (CHUNK, CHUNK), 0)
    c = lax.broadcasted_iota(jnp.int32, (CHUNK, CHUNK), 1)
    eye = (r == c).astype(F32)
    same_blk = lax.shift_right_logical(r, 4) == lax.shift_right_logical(c, 4)
    diag = jnp.where(same_blk, low, 0.0)
    off = low - diag
    n1 = -diag
    n2 = nn(n1, n1)
    n4 = nn(n2, n2)
    n8 = nn(n4, n4)
    inv_d = nn(nn(nn(eye + n1, eye + n2), eye + n4), eye + n8)
    m1 = nn(inv_d, off)
    m2 = nn(m1, m1)
    return nn(nn(eye - m1, eye + m2), inv_d)


def _chunk_fn(qkv, ba, alog_row, dtb_row, state, h, bdots, fdots):
    nn, nt, tn = bdots
    fnn = fdots[0]
    q = qkv[:, 0:HEAD_DIM] * (HEAD_DIM ** -0.5)
    k = qkv[:, HEAD_DIM:2 * HEAD_DIM]
    v = qkv[:, 2 * HEAD_DIM:3 * HEAD_DIM]
    lane = lax.broadcasted_iota(jnp.int32, ba.shape, 1)
    bg = jnp.where(lane < GDN_HEADS, jax.nn.sigmoid(ba), -jnp.exp(alog_row) * _softplus(ba + dtb_row))
    beta = jnp.sum(jnp.where(lane == h, bg, 0.0), axis=1, keepdims=True)
    g = jnp.sum(jnp.where(lane == h + GDN_HEADS, bg, 0.0), axis=1, keepdims=True)

    r = lax.broadcasted_iota(jnp.int32, (CHUNK, CHUNK), 0)
    c = lax.broadcasted_iota(jnp.int32, (CHUNK, CHUNK), 1)
    incl = r >= c
    strict = r > c
    eye = r == c

    def to_row(col):
        return jnp.sum(jnp.where(eye, col, 0.0), axis=0, keepdims=True)

    gc = jnp.sum(jnp.where(incl, to_row(g), 0.0), axis=1, keepdims=True)
    diff = gc - to_row(gc)
    decay = jnp.where(incl, jnp.exp(jnp.where(incl, diff, 0.0)), 0.0)
    k_beta = k * beta
    v_beta = v * beta
    low = jnp.where(strict, nt(k_beta, k) * decay, 0.0)
    t_inv = _tri_inv(low, fnn)
    eg = jnp.exp(gc)
    u = fnn(t_inv, v_beta)
    w = fnn(t_inv, k_beta * eg)
    attn = jnp.where(incl, nt(q, k) * decay, 0.0)
    v_new = u - nn(w, state)
    o = nn(q * eg, state) + nn(attn, v_new)
    last = lax.broadcasted_iota(jnp.int32, (CHUNK, 1), 0) == CHUNK - 1
    g_last = jnp.sum(jnp.where(last, gc, 0.0), axis=0, keepdims=True)
    k_dec = k * jnp.exp(g_last - gc)
    new_state = state * jnp.exp(g_last) + tn(k_dec, v_new)
    return o, new_state


def _gdn_scan_fwd(post, proj, alog_row, dtb_row):
    t = post.shape[0]
    n_chunks = t // CHUNK

    def body(qkv_ref, ba_ref, al_ref, dt_ref, o_ref, save_ref, state_ref):
        n = pl.program_id(0)
        h = pl.program_id(1)

        @pl.when(n == 0)
        def _():
            state_ref[h] = jnp.zeros((HEAD_DIM, HEAD_DIM), F32)

        state = state_ref[h]
        save_ref[0, 0] = state
        o, new_state = _chunk_fn(qkv_ref[...], ba_ref[...], al_ref[...], dt_ref[...], state, h, _BDOT_PLAIN, _FDOT_PLAIN)
        o_ref[...] = o
        state_ref[h] = new_state

    return pl.pallas_call(
        body, grid=(n_chunks, GDN_HEADS),
        in_specs=[pl.BlockSpec((CHUNK, 3 * HEAD_DIM), lambda n, h: (n, h)),
                  pl.BlockSpec((CHUNK, LANE), lambda n, h: (n, BA_BLK)),
                  pl.BlockSpec((1, LANE), lambda n, h: (0, 0)), pl.BlockSpec((1, LANE), lambda n, h: (0, 0))],
        out_specs=[pl.BlockSpec((CHUNK, HEAD_DIM), lambda n, h: (n, h)),
                   pl.BlockSpec((1, 1, HEAD_DIM, HEAD_DIM), lambda n, h: (h, n, 0, 0))],
        out_shape=[jax.ShapeDtypeStruct((t, GDN_WIDTH), F32),
                   jax.ShapeDtypeStruct((GDN_HEADS, n_chunks, HEAD_DIM, HEAD_DIM), F32)],
        scratch_shapes=[pltpu.VMEM((GDN_HEADS, HEAD_DIM, HEAD_DIM), F32)],
        compiler_params=_params("arbitrary", "arbitrary"), name="gdn_scan_fwd",
    )(post, proj, alog_row, dtb_row)


def _gdn_scan_bwd(post, proj, alog_row, dtb_row, saved, do, dproj):
    t = post.shape[0]
    n_chunks = t // CHUNK
    last = n_chunks - 1

    def body(qkv_ref, ba_ref, al_ref, dt_ref, save_ref, do_ref, _, dqkv_ref, dba_ref, dal_ref, ddt_ref, dstate_ref):
        n = pl.program_id(0)
        h = pl.program_id(1)

        @pl.when(n == 0)
        def _():
            dstate_ref[h] = jnp.zeros((HEAD_DIM, HEAD_DIM), F32)

        @pl.when((n == 0) & (h == 0))
        def _():
            dal_ref[...] = jnp.zeros_like(dal_ref)
            ddt_ref[...] = jnp.zeros_like(ddt_ref)

        def f(qkv, ba, al, dt, state):
            return _chunk_fn(qkv, ba, al, dt, state, h, _BDOT_VJP, _FDOT_VJP)

        _, vjp = jax.vjp(f, qkv_ref[...], ba_ref[...], al_ref[...], dt_ref[...], save_ref[0, 0])
        dqkv, dba, dal, ddt, dstate = vjp((do_ref[...], dstate_ref[h]))
        dqkv_ref[...] = dqkv

        @pl.when(h == 0)
        def _():
            dba_ref[:, 0:LANE] = dba
            dba_ref[:, LANE:2 * LANE] = jnp.zeros_like(dba)

        @pl.when(h > 0)
        def _():
            dba_ref[:, 0:LANE] += dba

        dal_ref[...] += dal
        ddt_ref[...] += ddt
        dstate_ref[h] = dstate

    return pl.pallas_call(
        body, grid=(n_chunks, GDN_HEADS),
        in_specs=[pl.BlockSpec((CHUNK, 3 * HEAD_DIM), lambda n, h: (last - n, h)),
                  pl.BlockSpec((CHUNK, LANE), lambda n, h: (last - n, BA_BLK)),
                  pl.BlockSpec((1, LANE), lambda n, h: (0, 0)), pl.BlockSpec((1, LANE), lambda n, h: (0, 0)),
                  pl.BlockSpec((1, 1, HEAD_DIM, HEAD_DIM), lambda n, h: (h, last - n, 0, 0)),
                  pl.BlockSpec((CHUNK, HEAD_DIM), lambda n, h: (last - n, h)),
                  pl.BlockSpec(memory_space=pl.ANY)],
        out_specs=[pl.BlockSpec((CHUNK, 3 * HEAD_DIM), lambda n, h: (last - n, h)),
                   pl.BlockSpec((CHUNK, 2 * LANE), lambda n, h: (last - n, BA_BLK // 2)),
                   pl.BlockSpec((1, LANE), lambda n, h: (0, 0)), pl.BlockSpec((1, LANE), lambda n, h: (0, 0))],
        out_shape=[jax.ShapeDtypeStruct((t, QKV_COLS), F32), jax.ShapeDtypeStruct(dproj.shape, F32),
                   jax.ShapeDtypeStruct((1, LANE), F32), jax.ShapeDtypeStruct((1, LANE), F32)],
        input_output_aliases={6: 1},
        scratch_shapes=[pltpu.VMEM((GDN_HEADS, HEAD_DIM, HEAD_DIM), F32)],
        compiler_params=_params("arbitrary", "arbitrary"), name="gdn_scan_bwd",
    )(post, proj, alog_row, dtb_row, saved, do, dproj)


def _onorm_fn(o, z, w):
    return o * lax.rsqrt(jnp.mean(o * o, axis=1, keepdims=True) + NORM_EPS) * w * (z * jax.nn.sigmoid(z))


def _onorm_fwd(o_raw, proj, norm_w, mixin, tm=512):
    t = o_raw.shape[0]
    tm = min(tm, t)

    def body(o_ref, z_ref, w_ref, _, out_ref):
        out_ref[...] = _onorm_fn(o_ref[...], z_ref[...], w_ref[...]).astype(out_ref.dtype)

    return pl.pallas_call(
        body, grid=(t // tm, GDN_HEADS),
        in_specs=[pl.BlockSpec((tm, LANE), lambda i, h: (i, h)), pl.BlockSpec((tm, LANE), lambda i, h: (i, Z_BLK + h)),
                  pl.BlockSpec((1, LANE), lambda i, h: (0, 0)), pl.BlockSpec(memory_space=pl.ANY)],
        out_specs=pl.BlockSpec((tm, LANE), lambda i, h: (i, h)),
        out_shape=jax.ShapeDtypeStruct(mixin.shape, mixin.dtype), input_output_aliases={3: 0},
        compiler_params=_params("parallel", "parallel"), name="gdn_onorm_fwd",
    )(o_raw, proj, norm_w, mixin)


def _onorm_bwd(o_raw, proj, norm_w, dmixin, dproj, tm=512):
    t = o_raw.shape[0]
    tm = min(tm, t)

    def body(o_ref, z_ref, w_ref, d_ref, _, do_ref, dz_ref, dw_ref):
        @pl.when((pl.program_id(0) == 0) & (pl.program_id(1) == 0))
        def _():
            dw_ref[...] = jnp.zeros_like(dw_ref)

        _, vjp = jax.vjp(_onorm_fn, o_ref[...], z_ref[...], w_ref[...])
        do, dz, dw = vjp(d_ref[...])
        do_ref[...] = do
        dz_ref[...] = dz
        dw_ref[...] += dw

    return pl.pallas_call(
        body, grid=(t // tm, GDN_HEADS),
        in_specs=[pl.BlockSpec((tm, LANE), lambda i, h: (i, h)), pl.BlockSpec((tm, LANE), lambda i, h: (i, Z_BLK + h)),
                  pl.BlockSpec((1, LANE), lambda i, h: (0, 0)), pl.BlockSpec((tm, LANE), lambda i, h: (i, h)),
                  pl.BlockSpec(memory_space=pl.ANY)],
        out_specs=[pl.BlockSpec((tm, LANE), lambda i, h: (i, h)), pl.BlockSpec((tm, LANE), lambda i, h: (i, Z_BLK + h)),
                   pl.BlockSpec((1, LANE), lambda i, h: (0, 0))],
        out_shape=[jax.ShapeDtypeStruct((t, GDN_WIDTH), F32), jax.ShapeDtypeStruct(dproj.shape, F32),
                   jax.ShapeDtypeStruct((1, LANE), F32)],
        input_output_aliases={4: 1},
        compiler_params=_params("arbitrary", "arbitrary"), name="gdn_onorm_bwd",
    )(o_raw, proj, norm_w, dmixin, dproj)


def _pool_select(levels, gi):
    out = levels[-1]
    for lvl in range(len(levels) - 2, -1, -1):
        out = jnp.where(gi == lvl, levels[lvl], out)
    return out


def _pool_count(shape, gi):
    pos = lax.broadcasted_iota(jnp.int32, shape, 0)
    win = lax.shift_left(jnp.int32(2), gi)
    return jnp.minimum(pos + 1, win).astype(F32)


def _pooled(p, gi):
    acc = p
    levels = []
    for lvl in range(POOL_GROUPS):
        acc = acc + _shift_down(acc, 1 << lvl)
        levels.append(acc)
    return _pool_select(levels, gi) / _pool_count(p.shape, gi) - p


def _pool_fwd(proj, pool_w, pool_scale):
    t = proj.shape[0]

    def body(p_ref, w_ref, s_ref, out_ref):
        gi = pl.program_id(0)
        pooled = _pooled(p_ref[...], gi)
        out_ref[...] = (_BDOT_PLAIN[0](pooled, w_ref[0]) * s_ref[0]).astype(out_ref.dtype)

    return pl.pallas_call(
        body, grid=(POOL_GROUPS,),
        in_specs=[pl.BlockSpec((t, POOL_GROUP_DIM), lambda g: (0, POOL_BLK + g)),
                  pl.BlockSpec((1, POOL_GROUP_DIM, POOL_GROUP_DIM), lambda g: (g, 0, 0)),
                  pl.BlockSpec((1, 1, POOL_GROUP_DIM), lambda g: (g, 0, 0))],
        out_specs=pl.BlockSpec((t, POOL_GROUP_DIM), lambda g: (0, GDN_WIDTH // POOL_GROUP_DIM + g)),
        out_shape=jax.ShapeDtypeStruct((t, 2 * GDN_WIDTH), BF16),
        compiler_params=_params("parallel"), name="pool_fwd",
    )(proj, pool_w, pool_scale)


def _pool_bwd(proj, pool_w, pool_scale, dmixin):
    t = proj.shape[0]
    nn, nt, tn = _BDOT_PLAIN

    def body(p_ref, w_ref, s_ref, d_ref, dp_ref, dw_ref, ds_ref):
        gi = pl.program_id(0)
        p = p_ref[...]
        pooled = _pooled(p, gi)
        mixed = nn(pooled, w_ref[0])
        d = d_ref[...]
        ds_ref[0] = jnp.sum(d * mixed, axis=0, keepdims=True)
        dmixed = d * s_ref[0]
        dw_ref[0] = tn(pooled, dmixed)
        dpooled = nt(dmixed, w_ref[0])
        acc = dpooled / _pool_count(p.shape, gi)
        levels = []
        for lvl in range(POOL_GROUPS):
            acc = acc + _shift_up(acc, 1 << lvl)
            levels.append(acc)
        dp_ref[...] = _pool_select(levels, gi) - dpooled

    return pl.pallas_call(
        body, grid=(POOL_GROUPS,),
        in_specs=[pl.BlockSpec((t, POOL_GROUP_DIM), lambda g: (0, POOL_BLK + g)),
                  pl.BlockSpec((1, POOL_GROUP_DIM, POOL_GROUP_DIM), lambda g: (g, 0, 0)),
                  pl.BlockSpec((1, 1, POOL_GROUP_DIM), lambda g: (g, 0, 0)),
                  pl.BlockSpec((t, POOL_GROUP_DIM), lambda g: (0, GDN_WIDTH // POOL_GROUP_DIM + g))],
        out_specs=[pl.BlockSpec((t, POOL_GROUP_DIM), lambda g: (0, POOL_BLK + g)),
                   pl.BlockSpec((1, POOL_GROUP_DIM, POOL_GROUP_DIM), lambda g: (g, 0, 0)),
                   pl.BlockSpec((1, 1, POOL_GROUP_DIM), lambda g: (g, 0, 0))],
        out_shape=[jax.ShapeDtypeStruct((t, PROJ_COLS), F32),
                   jax.ShapeDtypeStruct((POOL_GROUPS, POOL_GROUP_DIM, POOL_GROUP_DIM), F32),
                   jax.ShapeDtypeStruct((POOL_GROUPS, 1, POOL_GROUP_DIM), F32)],
        compiler_params=_params("parallel"), name="pool_bwd",
    )(proj, pool_w, pool_scale, dmixin)


def _ln_stats(s):
    mu = jnp.mean(s, axis=1, keepdims=True)
    xc = s - mu
    var = jnp.mean(xc * xc, axis=1, keepdims=True)
    rstd = lax.rsqrt(var + LN_EPS)
    return xc * rstd, rstd


def _ln_fwd(h_in, y, g, b, *, name, tm=256):
    t, d = h_in.shape
    tm = min(tm, t)

    def body(h_ref, y_ref, g_ref, b_ref, o_ref):
        xhat, _ = _ln_stats(ALPHA * h_ref[...] + y_ref[...])
        o_ref[...] = xhat * g_ref[...] + b_ref[...]

    row = pl.BlockSpec((tm, d), lambda i: (i, 0))
    vec = pl.BlockSpec((1, d), lambda i: (0, 0))
    return pl.pallas_call(
        body, grid=(t // tm,), in_specs=[row, row, vec, vec], out_specs=row,
        out_shape=jax.ShapeDtypeStruct((t, d), F32), compiler_params=_params("parallel"), name=name,
    )(h_in, y, g, b)


def _ln_loss_fwd(h_in, y, g, b, target, *, name, tm=256):
    t, d = h_in.shape
    tm = min(tm, t)

    def body(h_ref, y_ref, g_ref, b_ref, t_ref, dy_ref, sq_ref):
        @pl.when(pl.program_id(0) == 0)
        def _():
            sq_ref[...] = jnp.zeros_like(sq_ref)

        xhat, _ = _ln_stats(ALPHA * h_ref[...] + y_ref[...])
        err = xhat * g_ref[...] + b_ref[...] - t_ref[...]
        dy_ref[...] = err * (1.0 / d)
        sq_ref[...] += jnp.sum(jnp.sum(err * err, axis=1, keepdims=True), axis=0, keepdims=True)

    row = pl.BlockSpec((tm, d), lambda i: (i, 0))
    vec = pl.BlockSpec((1, d), lambda i: (0, 0))
    return pl.pallas_call(
        body, grid=(t // tm,), in_specs=[row, row, vec, vec, row],
        out_specs=[row, pl.BlockSpec((1, LANE), lambda i: (0, 0))],
        out_shape=[jax.ShapeDtypeStruct((t, d), F32), jax.ShapeDtypeStruct((1, LANE), F32)],
        compiler_params=_params("arbitrary"), name=name,
    )(h_in, y, g, b, target)


def _ln_bwd(h_in, y, g, d_a, d_b, *, name, tm=256):
    t, d = h_in.shape
    tm = min(tm, t)
    has_b = d_b is not None

    def body(*refs):
        if has_b:
            h_ref, y_ref, g_ref, da_ref, db_ref, ds_ref, dg_ref, dbias_ref = refs
        else:
            h_ref, y_ref, g_ref, da_ref, ds_ref, dg_ref, dbias_ref = refs

        @pl.when(pl.program_id(0) == 0)
        def _():
            dg_ref[...] = jnp.zeros_like(dg_ref)
            dbias_ref[...] = jnp.zeros_like(dbias_ref)

        xhat, rstd = _ln_stats(ALPHA * h_ref[...] + y_ref[...])
        dout = da_ref[...]
        if has_b:
            dout = dout + ALPHA * db_ref[...]
        dxhat = dout * g_ref[...]
        m1 = jnp.mean(dxhat, axis=1, keepdims=True)
        m2 = jnp.mean(dxhat * xhat, axis=1, keepdims=True)
        ds_ref[...] = rstd * (dxhat - m1 - xhat * m2)
        dg_ref[...] += jnp.sum(dout * xhat, axis=0, keepdims=True)
        dbias_ref[...] += jnp.sum(dout, axis=0, keepdims=True)

    row = pl.BlockSpec((tm, d), lambda i: (i, 0))
    vec = pl.BlockSpec((1, d), lambda i: (0, 0))
    args = [h_in, y, g, d_a] + ([d_b] if has_b else [])
    return pl.pallas_call(
        body, grid=(t // tm,), in_specs=[row, row, vec, row] + ([row] if has_b else []),
        out_specs=[row, vec, vec],
        out_shape=[jax.ShapeDtypeStruct((t, d), F32), jax.ShapeDtypeStruct((1, d), F32), jax.ShapeDtypeStruct((1, d), F32)],
        compiler_params=_params("arbitrary"), name=name,
    )(*args)


def _attn_fn(q, k, v, dots):
    nn, nt, _ = dots
    s = nt(q, k) * (XATTN_HEAD_DIM ** -0.5)
    s = s - lax.stop_gradient(jnp.max(s, axis=1, keepdims=True))
    e = jnp.exp(s)
    p = e / jnp.sum(e, axis=1, keepdims=True)
    return nn(p, v)


def _attn_fwd(q, k, v, tq=512):
    t = q.shape[0]
    tq = min(tq, t)

    def body(q_ref, k_ref, v_ref, o_ref):
        o_ref[...] = _attn_fn(q_ref[...], k_ref[...], v_ref[...], _BDOT_PLAIN)

    qs = pl.BlockSpec((tq, XATTN_HEAD_DIM), lambda h, i: (i, h))
    ks = pl.BlockSpec((MEM_LEN, XATTN_HEAD_DIM), lambda h, i: (0, h))
    return pl.pallas_call(
        body, grid=(XATTN_HEADS, t // tq), in_specs=[qs, ks, ks], out_specs=qs,
        out_shape=jax.ShapeDtypeStruct(q.shape, F32), compiler_params=_params("parallel", "parallel"), name="xattn_fwd",
    )(q, k, v)


def _attn_bwd(q, k, v, do, tq=512):
    t = q.shape[0]
    tq = min(tq, t)

    def body(q_ref, k_ref, v_ref, do_ref, dq_ref, dk_ref, dv_ref):
        @pl.when(pl.program_id(1) == 0)
        def _():
            dk_ref[...] = jnp.zeros_like(dk_ref)
            dv_ref[...] = jnp.zeros_like(dv_ref)

        _, vjp = jax.vjp(lambda a, b, c: _attn_fn(a, b, c, _BDOT_VJP), q_ref[...], k_ref[...], v_ref[...])
        dq, dk, dv = vjp(do_ref[...])
        dq_ref[...] = dq
        dk_ref[...] += dk
        dv_ref[...] += dv

    qs = pl.BlockSpec((tq, XATTN_HEAD_DIM), lambda h, i: (i, h))
    ks = pl.BlockSpec((MEM_LEN, XATTN_HEAD_DIM), lambda h, i: (0, h))
    return pl.pallas_call(
        body, grid=(XATTN_HEADS, t // tq), in_specs=[qs, ks, ks, qs], out_specs=[qs, ks, ks],
        out_shape=[jax.ShapeDtypeStruct(q.shape, F32), jax.ShapeDtypeStruct(k.shape, F32), jax.ShapeDtypeStruct(v.shape, F32)],
        compiler_params=_params("parallel", "arbitrary"), name="xattn_bwd",
    )(q, k, v, do)


def _act_fwd(u, tm=256):
    t, f = u.shape
    tm = min(tm, t)

    def body(u_ref, a_ref):
        r = jnp.maximum(u_ref[...], 0.0)
        a_ref[...] = (r * r).astype(BF16)

    spec = pl.BlockSpec((tm, f), lambda i: (i, 0))
    return pl.pallas_call(body, grid=(t // tm,), in_specs=[spec], out_specs=spec,
                          out_shape=jax.ShapeDtypeStruct((t, f), BF16), compiler_params=_params("parallel"),
                          name="relu2_fwd")(u)


def _act_bwd(u, da, tm=256):
    t, f = u.shape
    tm = min(tm, t)

    def body(u_ref, da_ref, du_ref):
        du_ref[...] = (da_ref[...] * (2.0 * jnp.maximum(u_ref[...], 0.0))).astype(BF16)

    spec = pl.BlockSpec((tm, f), lambda i: (i, 0))
    return pl.pallas_call(body, grid=(t // tm,), in_specs=[spec, spec], out_specs=spec,
                          out_shape=jax.ShapeDtypeStruct((t, f), BF16), compiler_params=_params("parallel"),
                          name="relu2_bwd")(u, da)


def _local_step(x, mem, target, w):
    proj = _mm(x, w["w_in"], tn=768, name="mm_in_proj")
    post = _gdn_prep_fwd(proj, w["conv_w"])
    o_raw, saved = _gdn_scan_fwd(post, proj, w["alog_row"], w["dtb_row"])
    mixin = _pool_fwd(proj, w["pool_w"], w["pool_scale"])
    mixin = _onorm_fwd(o_raw, proj, w["gdn_norm_w"], mixin)
    mix = _mm(mixin, w["w_out"], name="mm_out_proj")
    h1 = _ln_fwd(x, mix, w["ln1_g"], w["ln1_b"], name="ln1_fwd")
    xq = _mm(h1, w["xq_w"], name="mm_xq")
    xk = _mm(mem, w["xk_w"], name="mm_xk")
    xv = _mm(mem, w["xv_w"], name="mm_xv")
    xo = _attn_fwd(xq, xk, xv)
    xa = _mm(xo, w["xo_w"], name="mm_xo")
    h2 = _ln_fwd(h1, xa, w["ln2_g"], w["ln2_b"], name="ln2_fwd")
    u = _mm(h2, w["w_up"], name="mm_up")
    act = _act_fwd(u)
    ff = _mm(act, w["w_down"], name="mm_down")
    dy, sq = _ln_loss_fwd(h2, ff, w["ln3_g"], w["ln3_b"], target, name="ln3_loss_fwd")

    g = {}
    ds3, g["ln3_g"], g["ln3_b"] = _ln_bwd(h2, ff, w["ln3_g"], dy, None, name="ln3_bwd")
    g["w_down"] = _mm(act, ds3, ta=True, out_dtype=BF16, name="mm_gw_down")
    dact = _mm(ds3, w["w_down"], tb=True, name="mm_dact")
    du = _act_bwd(u, dact)
    g["w_up"] = _mm(h2, du, ta=True, out_dtype=BF16, name="mm_gw_up")
    dh2 = _mm(du, w["w_up"], tb=True, name="mm_dh2")
    ds2, g["ln2_g"], g["ln2_b"] = _ln_bwd(h1, xa, w["ln2_g"], dh2, ds3, name="ln2_bwd")
    g["xo_w"] = _mm(xo, ds2, ta=True, out_dtype=BF16, name="mm_gw_xo")
    dxo = _mm(ds2, w["xo_w"], tb=True, name="mm_dxo")
    dxq, dxk, dxv = _attn_bwd(xq, xk, xv, dxo)
    g["xq_w"] = _mm(h1, dxq, ta=True, out_dtype=BF16, name="mm_gw_xq")
    g["xk_w"] = _mm(mem, dxk, ta=True, out_dtype=BF16, name="mm_gw_xk")
    g["xv_w"] = _mm(mem, dxv, ta=True, out_dtype=BF16, name="mm_gw_xv")
    dh1 = _mm(dxq, w["xq_w"], tb=True, name="mm_dh1")
    ds1, g["ln1_g"], g["ln1_b"] = _ln_bwd(x, mix, w["ln1_g"], dh1, ds2, name="ln1_bwd")
    g["w_out"] = _mm(mixin, ds1, ta=True, out_dtype=BF16, name="mm_gw_out")
    dmixin = _mm(ds1, w["w_out"], tb=True, name="mm_dmixin")
    dproj, g["pool_w"], g["pool_scale"] = _pool_bwd(proj, w["pool_w"], w["pool_scale"], dmixin)
    do_raw, dproj, g["gdn_norm_w"] = _onorm_bwd(o_raw, proj, w["gdn_norm_w"], dmixin, dproj)
    dpost, dproj, g["alog_row"], g["dtb_row"] = _gdn_scan_bwd(post, proj, w["alog_row"], w["dtb_row"], saved, do_raw, dproj)
    dproj, g["conv_w"] = _gdn_prep_bwd(proj, w["conv_w"], dpost, dproj)
    g["w_in"] = _mm(x, dproj, ta=True, out_dtype=BF16, tn=768, name="mm_gw_in")
    grad_x = _mm(dproj, w["w_in"], tb=True, tk=768, add=ds1, add_scale=ALPHA, name="mm_dx")
    return sq, grad_x, g


_MATRICES = ("w_in", "pool_w", "w_out", "xq_w", "xk_w", "xv_w", "xo_w", "w_up", "w_down")
_VECTORS = ("a_log", "dt_bias", "gdn_norm_w", "pool_scale", "ln1_g", "ln1_b", "ln2_g", "ln2_b", "ln3_g", "ln3_b")
_BA_SPLIT = BA_OFF + 2 * GDN_HEADS


def _lane_row(v, offset):
    return jnp.zeros((1, LANE), F32).at[0, offset:offset + v.shape[0]].set(v)


def _prepare_weights(full):
    w_in = full["w_in"]
    zeros = jnp.zeros((w_in.shape[0], POOL_OFF - _BA_SPLIT), w_in.dtype)
    w = {
        "w_in": jnp.concatenate([w_in[:, :_BA_SPLIT], zeros, w_in[:, _BA_SPLIT:]], axis=1),
        "conv_w": full["conv_w"],
        "alog_row": _lane_row(full["a_log"], GDN_HEADS),
        "dtb_row": _lane_row(full["dt_bias"], GDN_HEADS),
        "gdn_norm_w": full["gdn_norm_w"].reshape(1, LANE),
        "pool_w": full["pool_w"],
        "pool_scale": full["pool_scale"].reshape(POOL_GROUPS, 1, POOL_GROUP_DIM),
    }
    for n in ("w_out", "xq_w", "xk_w", "xv_w", "xo_w", "w_up", "w_down"):
        w[n] = full[n]
    for n in ("ln1_g", "ln1_b", "ln2_g", "ln2_b", "ln3_g", "ln3_b"):
        w[n] = full[n].reshape(1, D_MODEL)
    return w


def _finish_grads(g):
    out = {n: g[n] for n in ("conv_w", "pool_w", "w_out", "xq_w", "xk_w", "xv_w", "xo_w", "w_up", "w_down")}
    out["w_in"] = jnp.concatenate([g["w_in"][:, :_BA_SPLIT], g["w_in"][:, POOL_OFF:]], axis=1)
    out["a_log"] = g["alog_row"][0, GDN_HEADS:2 * GDN_HEADS]
    out["dt_bias"] = g["dtb_row"][0, GDN_HEADS:2 * GDN_HEADS]
    out["gdn_norm_w"] = g["gdn_norm_w"].reshape(LANE)
    out["pool_scale"] = g["pool_scale"].reshape(POOL_GROUPS * POOL_GROUP_DIM)
    for n in ("ln1_g", "ln1_b", "ln2_g", "ln2_b", "ln3_g", "ln3_b"):
        out[n] = g[n].reshape(D_MODEL)
    return out


def _adamw_math(w, g, m, v):
    m = ADAM_B1 * m + (1.0 - ADAM_B1) * g
    v = ADAM_B2 * v + (1.0 - ADAM_B2) * (g * g)
    m_hat = m / (1.0 - ADAM_B1 ** ADAM_STEP)
    v_hat = v / (1.0 - ADAM_B2 ** ADAM_STEP)
    delta = -ADAM_LR * (m_hat / (jnp.sqrt(v_hat) + ADAM_EPS) + ADAM_WD * w)
    return delta, m, v


def _adamw_shard(parts, w, m, v, *, tr, name):
    s, r, c = parts.shape
    tr = min(tr, r)
    assert r % tr == 0, (name, r, tr)

    def body(p_ref, w_ref, m_ref, v_ref, g_ref, d_ref, nm_ref, nv_ref):
        g = p_ref[0].astype(F32)
        for i in range(1, s):
            g = g + p_ref[i].astype(F32)
        delta, nm, nv = _adamw_math(w_ref[...], g, m_ref[...], v_ref[...])
        g_ref[...] = g
        d_ref[...] = delta
        nm_ref[...] = nm
        nv_ref[...] = nv

    blk = pl.BlockSpec((tr, c), lambda i: (i, 0))
    out = jax.ShapeDtypeStruct((r, c), F32)
    return pl.pallas_call(
        body, grid=(r // tr,), in_specs=[pl.BlockSpec((s, tr, c), lambda i: (0, i, 0)), blk, blk, blk],
        out_specs=[blk, blk, blk, blk], out_shape=[out, out, out, out],
        compiler_params=_params("parallel"), name=name,
    )(parts, w, m, v)


def _place():
    return lax.axis_index("x"), lax.axis_index("y"), lax.axis_index("c")


def _slot(px, py, pc):
    return 4 * px + 2 * py + pc


_HBM = pl.BlockSpec(memory_space=pltpu.HBM)


def _all_gather(shards, *, name):
    n = len(shards)

    def body(*refs):
        ins, outs = refs[:n], refs[n:2 * n]
        send_sems, recv_sems, local_sems = refs[2 * n:]
        x, y, c = _place()
        me = _slot(x, y, c)
        sibling = (x, y, 1 - c)
        chips = [(1 - x, y), (x, 1 - y), (1 - x, 1 - y)]

        def copy(w, k, slot, to, src=None):
            dst = outs[w].at[slot]
            return pltpu.make_async_remote_copy(
                src_ref=dst if src is None else src, dst_ref=dst, send_sem=send_sems.at[w, k], recv_sem=recv_sems.at[w, k],
                device_id=to, device_id_type=MESH)

        local = [pltpu.make_async_copy(ins[w], outs[w].at[me], local_sems.at[w]) for w in range(n)]
        for cp in local:
            cp.start()
        first = []
        for w in range(n):
            first.append(copy(w, 0, me, sibling, src=ins[w]))
            for j, chip in enumerate(chips):
                first.append(copy(w, 1 + j, me, (*chip, c), src=ins[w]))
        for cp in first:
            cp.start()
        passed = []
        for w in range(n):
            for j, chip in enumerate(chips):
                copy(w, 1 + j, _slot(*chip, c), (x, y, c)).wait_recv()
                fwd = copy(w, 4 + j, _slot(*chip, c), sibling)
                fwd.start()
                passed.append(fwd)
        for w in range(n):
            copy(w, 0, _slot(x, y, 1 - c), (x, y, c)).wait_recv()
            for j, chip in enumerate(chips):
                copy(w, 4 + j, _slot(*chip, 1 - c), (x, y, c)).wait_recv()
        for cp in first + passed:
            cp.wait_send()
        for cp in local:
            cp.wait()

    return pl.pallas_call(
        body, in_specs=[_HBM] * n, out_specs=[_HBM] * n,
        out_shape=[jax.ShapeDtypeStruct((N_DEV, *s.shape), s.dtype) for s in shards],
        scratch_shapes=[pltpu.SemaphoreType.DMA((n, 7)), pltpu.SemaphoreType.DMA((n, 7)), pltpu.SemaphoreType.DMA((n,))],
        name=name,
    )(*shards)


def _peer(k, x, y, c):
    return (1 - x if k & 4 else x, 1 - y if k & 2 else y, 1 - c if k & 1 else c)


def _scatter_to_owners(chunked, *, name):
    n = len(chunked)

    def body(*refs):
        ins, outs = refs[:n], refs[n:2 * n]
        send_sems, recv_sems, local_sems = refs[2 * n:]
        x, y, c = _place()
        me = _slot(x, y, c)
        local = [pltpu.make_async_copy(ins[w].at[me], outs[w].at[me], local_sems.at[w]) for w in range(n)]
        for cp in local:
            cp.start()
        sends = []
        for w in range(n):
            for k in range(1, N_DEV):
                peer = _peer(k, x, y, c)
                sends.append(pltpu.make_async_remote_copy(
                    src_ref=ins[w].at[_slot(*peer)], dst_ref=outs[w].at[me], send_sem=send_sems.at[w, k - 1],
                    recv_sem=recv_sems.at[w, k - 1], device_id=peer, device_id_type=MESH))
        for cp in sends:
            cp.start()
        for w in range(n):
            for k in range(1, N_DEV):
                peer = _peer(k, x, y, c)
                pltpu.make_async_remote_copy(
                    src_ref=ins[w].at[_slot(*peer)], dst_ref=outs[w].at[_slot(*peer)], send_sem=send_sems.at[w, k - 1],
                    recv_sem=recv_sems.at[w, k - 1], device_id=peer, device_id_type=MESH).wait_recv()
        for cp in sends:
            cp.wait_send()
        for cp in local:
            cp.wait()

    return pl.pallas_call(
        body, in_specs=[_HBM] * n, out_specs=[_HBM] * n,
        out_shape=[jax.ShapeDtypeStruct(a.shape, a.dtype) for a in chunked],
        scratch_shapes=[pltpu.SemaphoreType.DMA((n, 7)), pltpu.SemaphoreType.DMA((n, 7)), pltpu.SemaphoreType.DMA((n,))],
        name=name,
    )(*chunked)


def _small_allreduce_adamw(gvec, wvec, mvec, vvec):
    rows, length = gvec.shape

    def body(g_ref, w_ref, m_ref, v_ref, gs_ref, d_ref, nm_ref, nv_ref, slots, send_sems, recv_sems):
        x, y, c = _place()
        me = _slot(x, y, c)
        slots[me] = g_ref[...]
        sends = []
        for k in range(1, N_DEV):
            peer = _peer(k, x, y, c)
            sends.append(pltpu.make_async_remote_copy(
                src_ref=g_ref, dst_ref=slots.at[me], send_sem=send_sems.at[k - 1], recv_sem=recv_sems.at[k - 1],
                device_id=peer, device_id_type=MESH))
        for cp in sends:
            cp.start()
        for k in range(1, N_DEV):
            peer = _peer(k, x, y, c)
            pltpu.make_async_remote_copy(
                src_ref=g_ref, dst_ref=slots.at[_slot(*peer)], send_sem=send_sems.at[k - 1], recv_sem=recv_sems.at[k - 1],
                device_id=peer, device_id_type=MESH).wait_recv()
        for cp in sends:
            cp.wait_send()
        g = slots[0]
        for s in range(1, N_DEV):
            g = g + slots[s]
        delta, nm, nv = _adamw_math(w_ref[...], g, m_ref[...], v_ref[...])
        gs_ref[...] = g
        d_ref[...] = delta
        nm_ref[...] = nm
        nv_ref[...] = nv

    vmem = pl.BlockSpec(memory_space=pltpu.VMEM)
    out = jax.ShapeDtypeStruct((rows, length), F32)
    return pl.pallas_call(
        body, in_specs=[vmem] * 4, out_specs=[vmem] * 4, out_shape=[out] * 4,
        scratch_shapes=[pltpu.VMEM((N_DEV, rows, length), F32), pltpu.SemaphoreType.DMA((N_DEV - 1,)),
                        pltpu.SemaphoreType.DMA((N_DEV - 1,))],
        name="small_allreduce_adamw",
    )(gvec, wvec, mvec, vvec)


_SMALL_SEGMENTS = (("a_log", GDN_HEADS), ("dt_bias", GDN_HEADS), ("gdn_norm_w", HEAD_DIM), ("pool_scale", GDN_WIDTH),
                   ("ln1_g", D_MODEL), ("ln1_b", D_MODEL), ("ln2_g", D_MODEL), ("ln2_b", D_MODEL),
                   ("ln3_g", D_MODEL), ("ln3_b", D_MODEL), ("conv_w", CONV_K * QKV_COLS))
_SMALL_ROWS = 8
_SMALL_LEN = -(-sum(sz for _, sz in _SMALL_SEGMENTS) // (_SMALL_ROWS * LANE)) * LANE


def _pack_small(vals):
    parts = [vals[n].reshape(-1).astype(F32) if n in vals else jnp.zeros((sz,), F32) for n, sz in _SMALL_SEGMENTS]
    flat = jnp.concatenate(parts)
    flat = jnp.pad(flat, (0, _SMALL_ROWS * _SMALL_LEN - flat.shape[0]))
    return flat.reshape(_SMALL_ROWS, _SMALL_LEN)


def _unpack_small(vec):
    flat = vec.reshape(-1)
    out, off = {}, 0
    for n, sz in _SMALL_SEGMENTS:
        out[n] = flat[off:off + sz]
        off += sz
    return out


_WEIGHT_ORDER = ("w_in", "conv_w", "a_log", "dt_bias", "gdn_norm_w", "pool_w", "pool_scale", "w_out", "ln1_g", "ln1_b",
                 "xq_w", "xk_w", "xv_w", "xo_w", "ln2_g", "ln2_b", "w_up", "w_down", "ln3_g", "ln3_b")
_ADAM_ROWS = {"w_in": 256, "pool_w": 128, "w_out": 128, "xq_w": 128, "xk_w": 128, "xv_w": 128, "xo_w": 128,
              "w_up": 128, "w_down": 128}


def _shard2d(name, a):
    return a.reshape(-1, a.shape[-1]) if name == "pool_w" else a


def _gathered_to_full(name, gth):
    if name in ("w_in", "w_up", "conv_w"):
        return jnp.transpose(gth, (1, 0, 2)).reshape(gth.shape[1], N_DEV * gth.shape[2])
    if name == "pool_w":
        g4 = gth.reshape(N_DEV, POOL_GROUPS, POOL_GROUP_DIM // N_DEV, POOL_GROUP_DIM)
        return jnp.transpose(g4, (1, 0, 2, 3)).reshape(POOL_GROUPS, POOL_GROUP_DIM, POOL_GROUP_DIM)
    return gth.reshape(N_DEV * gth.shape[1], gth.shape[2])


def _full_to_chunks(name, full):
    if name in ("w_in", "w_up"):
        r, cols = full.shape
        return jnp.transpose(full.reshape(r, N_DEV, cols // N_DEV), (1, 0, 2))
    if name == "pool_w":
        g4 = full.reshape(POOL_GROUPS, N_DEV, POOL_GROUP_DIM // N_DEV, POOL_GROUP_DIM)
        return jnp.transpose(g4, (1, 0, 2, 3)).reshape(N_DEV, POOL_GROUPS * POOL_GROUP_DIM // N_DEV, POOL_GROUP_DIM)
    return full.reshape(N_DEV, full.shape[0] // N_DEV, full.shape[1])


def kernel(x, mem, w_in, conv_w, a_log, dt_bias, gdn_norm_w, pool_w, pool_scale, w_out, ln1_g, ln1_b, xq_w, xk_w, xv_w, xo_w, ln2_g, ln2_b, w_up, w_down, ln3_g, ln3_b, loss_target, m_w_in, m_conv_w, m_a_log, m_dt_bias, m_gdn_norm_w, m_pool_w, m_pool_scale, m_w_out, m_ln1_g, m_ln1_b, m_xq_w, m_xk_w, m_xv_w, m_xo_w, m_ln2_g, m_ln2_b, m_w_up, m_w_down, m_ln3_g, m_ln3_b, v_w_in, v_conv_w, v_a_log, v_dt_bias, v_gdn_norm_w, v_pool_w, v_pool_scale, v_w_out, v_ln1_g, v_ln1_b, v_xq_w, v_xk_w, v_xv_w, v_xo_w, v_ln2_g, v_ln2_b, v_w_up, v_w_down, v_ln3_g, v_ln3_b):
    args = dict(locals())
    wt = {n: args[n][0] for n in _WEIGHT_ORDER}
    mo = {n: args["m_" + n][0] for n in _WEIGHT_ORDER}
    vo = {n: args["v_" + n][0] for n in _WEIGHT_ORDER}

    gather_names = _MATRICES + ("conv_w",)
    shards = [_shard2d(n, wt[n]).astype(BF16 if n in _MATRICES else F32) for n in gather_names]
    gathered = _all_gather(shards, name="all_gather_weights")
    full = {n: _gathered_to_full(n, gth) for n, gth in zip(gather_names, gathered)}
    for n in _VECTORS:
        full[n] = wt[n]
    w = _prepare_weights(full)

    sq, grad_x, g = _local_step(x[0], mem[0], loss_target[0], w)
    grads = _finish_grads(g)

    chunked = [_full_to_chunks(n, grads[n].astype(BF16)) for n in _MATRICES]
    received = _scatter_to_owners(chunked, name="scatter_weight_grads")
    small = {n: grads[n] for n, _ in _SMALL_SEGMENTS}
    gs, ds, ms, vs = _small_allreduce_adamw(
        _pack_small(small), _pack_small({n: wt[n] for n in _VECTORS}), _pack_small({n: mo[n] for n in _VECTORS}),
        _pack_small({n: vo[n] for n in _VECTORS}))
    gs, ds, ms, vs = _unpack_small(gs), _unpack_small(ds), _unpack_small(ms), _unpack_small(vs)

    out = {}
    for n, parts in zip(_MATRICES, received):
        res = _adamw_shard(parts, _shard2d(n, wt[n]), _shard2d(n, mo[n]), _shard2d(n, vo[n]), tr=_ADAM_ROWS[n],
                           name="adamw_" + n)
        out[n] = [r.reshape(args[n].shape) for r in res]
    cols = conv_w.shape[-1]
    me = _slot(*_place())
    conv_full = gs["conv_w"].reshape(CONV_K, QKV_COLS)
    conv_mine = lax.dynamic_slice(conv_full, (0, me * cols), (CONV_K, cols))
    res = _adamw_shard(conv_mine[None], wt["conv_w"], mo["conv_w"], vo["conv_w"], tr=CONV_K, name="adamw_conv_w")
    out["conv_w"] = [r.reshape(conv_w.shape) for r in res]
    for n in _VECTORS:
        out[n] = [t[n].reshape(args[n].shape) for t in (gs, ds, ms, vs)]

    loss = lax.psum(0.5 * sq[0, 0] / D_MODEL, ("x", "y", "c"))
    return (loss, grad_x[None], *[out[n][0] for n in _WEIGHT_ORDER], *[out[n][1] for n in _WEIGHT_ORDER],
            *[out[n][2] for n in _WEIGHT_ORDER], *[out[n][3] for n in _WEIGHT_ORDER])
```

```python
import functools
import math

import jax
import jax.numpy as jnp
from jax import lax
from jax.experimental import pallas as pl
from jax.experimental.pallas import tpu as pltpu

F32 = jnp.float32
BF16 = jnp.bfloat16
MESH = pl.DeviceIdType.MESH

N_DEV = 8
D_MODEL = 2048
GDN_WIDTH = 1024
GDN_HEADS = 8
HEAD_DIM = 128
CONV_K = 4
CHUNK = 64
POOL_GROUPS = 4
POOL_GROUP_DIM = 256
MEM_LEN = 256
XATTN_HEADS = 4
XATTN_HEAD_DIM = 512
D_FF = 8192
IN_COLS = 5136
ALPHA = 2.0 ** 0.25
LN_EPS = 1e-5
NORM_EPS = 1e-6

LANE = 128
QKV_COLS = 3 * GDN_WIDTH
Z_OFF = QKV_COLS
BA_OFF = 4 * GDN_WIDTH
POOL_OFF = BA_OFF + 2 * LANE
PROJ_COLS = POOL_OFF + GDN_WIDTH
Z_BLK = Z_OFF // LANE
BA_BLK = BA_OFF // LANE
POOL_BLK = POOL_OFF // POOL_GROUP_DIM

ADAM_LR = 0.001
ADAM_B1 = 0.9
ADAM_B2 = 0.999
ADAM_EPS = 1e-08
ADAM_WD = 0.01
ADAM_STEP = 10

VMEM_LIMIT_BYTES = 48 * 1024 * 1024


def _params(*sem):
    return pltpu.CompilerParams(dimension_semantics=sem if sem else None, vmem_limit_bytes=VMEM_LIMIT_BYTES)


def _make_dots(cast, precision):
    def dg(a, b, ca, cb):
        if cast is not None:
            a = a.astype(cast)
            b = b.astype(cast)
        return lax.dot_general(a, b, (((ca,), (cb,)), ((), ())), precision=precision, preferred_element_type=F32)

    def nn_(a, b):
        return dg(a, b, 1, 0)

    def nt_(a, b):
        return dg(a, b, 1, 1)

    def tn_(a, b):
        return dg(a, b, 0, 0)

    @jax.custom_vjp
    def nn(a, b):
        return nn_(a, b)

    nn.defvjp(lambda a, b: (nn_(a, b), (a, b)), lambda r, g: (nt_(g, r[1]), tn_(r[0], g)))

    @jax.custom_vjp
    def nt(a, b):
        return nt_(a, b)

    nt.defvjp(lambda a, b: (nt_(a, b), (a, b)), lambda r, g: (nn_(g, r[1]), tn_(g, r[0])))

    @jax.custom_vjp
    def tn(a, b):
        return tn_(a, b)

    tn.defvjp(lambda a, b: (tn_(a, b), (a, b)), lambda r, g: (nt_(r[1], g), nn_(r[0], g)))

    return (nn_, nt_, tn_), (nn, nt, tn)


_BDOT_PLAIN, _BDOT_VJP = _make_dots(BF16, None)
_FDOT_PLAIN, _FDOT_VJP = _make_dots(None, lax.Precision.HIGHEST)


def _mm(a, b, *, ta=False, tb=False, out_dtype=F32, tm=None, tn=512, tk=None, epi=None, extra=None, add_scale=1.0,
        b_chunks=False, o_chunks=False, name):
    m, k = (a.shape[1], a.shape[0]) if ta else a.shape
    if b_chunks:
        n, kb = (b.shape[1], N_DEV * b.shape[2]) if tb else (N_DEV * b.shape[2], b.shape[1])
    else:
        n, kb = b.shape if tb else (b.shape[1], b.shape[0])
    assert kb == k, (name, a.shape, b.shape)
    tm, tn, tk = min(tm or m, m), min(tn, n), min(tk or k, k)
    assert m % tm == 0 and n % tn == 0 and k % tk == 0, (name, m, n, k)
    nk = k // tk
    dims = (((0 if ta else 1,), (1 if tb else 0,)), ((), ()))
    n_extra = 0 if epi in (None, "relu2") else 1
    n_out = 2 if epi == "relu2" else 1
    if epi in ("relu2", "mul2r"):
        out_dtype = BF16

    def body(*refs):
        a_ref, b_ref = refs[:2]
        c_ref = refs[2] if n_extra else None
        o_refs = refs[2 + n_extra:2 + n_extra + n_out]
        scr = refs[2 + n_extra + n_out:]
        r = lax.dot_general(a_ref[...].astype(BF16), b_ref[...].astype(BF16), dims, preferred_element_type=F32)

        def finish(v):
            if epi == "add":
                o_refs[0][...] = (v + add_scale * c_ref[...]).astype(out_dtype)
            elif epi == "relu2":
                p = jnp.maximum(v, 0.0)
                o_refs[0][...] = (p * p).astype(BF16)
                o_refs[1][...] = p.astype(BF16)
            elif epi == "mul2r":
                o_refs[0][...] = (v * (2.0 * c_ref[...].astype(F32))).astype(BF16)
            else:
                o_refs[0][...] = v.astype(out_dtype)

        if nk == 1:
            finish(r)
        else:
            acc = scr[0]
            kk = pl.program_id(2)

            @pl.when(kk == 0)
            def _():
                acc[...] = r

            @pl.when(kk > 0)
            def _():
                acc[...] += r

            @pl.when(kk == nk - 1)
            def _():
                finish(acc[...])

    a_spec = pl.BlockSpec((tk, tm), lambda i, j, kk: (kk, i)) if ta else pl.BlockSpec((tm, tk), lambda i, j, kk: (i, kk))
    if b_chunks and tb:
        kc = k // N_DEV // tk
        b_spec = pl.BlockSpec((None, tn, tk), lambda i, j, kk: (kk // kc, j, kk % kc))
    elif b_chunks:
        nc = n // N_DEV // tn
        b_spec = pl.BlockSpec((None, tk, tn), lambda i, j, kk: (j // nc, kk, j % nc))
    elif tb:
        b_spec = pl.BlockSpec((tn, tk), lambda i, j, kk: (j, kk))
    else:
        b_spec = pl.BlockSpec((tk, tn), lambda i, j, kk: (kk, j))
    mn_spec = pl.BlockSpec((tm, tn), lambda i, j, kk: (i, j))
    if o_chunks:
        oc = n // N_DEV // tn
        o_spec = pl.BlockSpec((None, tm, tn), lambda i, j, kk: (j // oc, i, j % oc))
        o_shape = jax.ShapeDtypeStruct((N_DEV, m, n // N_DEV), out_dtype)
    else:
        o_spec, o_shape = mn_spec, jax.ShapeDtypeStruct((m, n), out_dtype)
    res = pl.pallas_call(
        body, grid=(m // tm, n // tn, nk), in_specs=[a_spec, b_spec] + [mn_spec] * n_extra,
        out_specs=[o_spec] * n_out, out_shape=[o_shape] * n_out,
        scratch_shapes=[pltpu.VMEM((tm, tn), F32)] if nk > 1 else [],
        compiler_params=_params("parallel", "parallel", "arbitrary"), name=name,
    )(a, b, *([extra] if n_extra else []))
    return res if n_out > 1 else res[0]


def _cast_bf16(v, *, name, tm=512):
    t, d = v.shape
    tm = min(tm, t)

    def body(v_ref, o_ref):
        o_ref[...] = v_ref[...].astype(BF16)

    spec = pl.BlockSpec((tm, d), lambda i: (i, 0))
    return pl.pallas_call(body, grid=(t // tm,), in_specs=[spec], out_specs=spec,
                          out_shape=jax.ShapeDtypeStruct((t, d), BF16), compiler_params=_params("parallel"), name=name)(v)


def _shift_down(v, s):
    if s == 0:
        return v
    row = lax.broadcasted_iota(jnp.int32, v.shape, 0)
    return jnp.where(row >= s, pltpu.roll(v, s, axis=0), 0.0)


def _shift_up(v, s):
    if s == 0:
        return v
    t = v.shape[0]
    row = lax.broadcasted_iota(jnp.int32, v.shape, 0)
    return jnp.where(row < t - s, pltpu.roll(v, t - s, axis=0), 0.0)


def _post_col(j):
    return (j % GDN_HEADS) * 3 + j // GDN_HEADS


def _gdn_prep_fwd(proj, conv_w):
    t = proj.shape[0]

    def body(x_ref, w_ref, o_ref):
        j = pl.program_id(0)
        x = x_ref[...]
        y = jnp.zeros_like(x)
        for tap in range(CONV_K):
            y = y + w_ref[tap:tap + 1, :] * _shift_down(x, CONV_K - 1 - tap)
        c = y * jax.nn.sigmoid(y)
        nrm = c * lax.rsqrt(jnp.sum(c * c, axis=1, keepdims=True) + NORM_EPS)
        o_ref[...] = jnp.where(j < 2 * GDN_HEADS, nrm, c)

    return pl.pallas_call(
        body, grid=(QKV_COLS // LANE,),
        in_specs=[pl.BlockSpec((t, LANE), lambda j: (0, j)), pl.BlockSpec((CONV_K, LANE), lambda j: (0, j))],
        out_specs=pl.BlockSpec((t, LANE), lambda j: (0, _post_col(j))),
        out_shape=jax.ShapeDtypeStruct((t, QKV_COLS), F32),
        compiler_params=_params("parallel"), name="gdn_prep_fwd",
    )(proj, conv_w)


def _gdn_prep_bwd(proj, conv_w, dpost, dproj):
    t = proj.shape[0]

    def body(x_ref, w_ref, d_ref, _, dx_ref, dw_ref):
        j = pl.program_id(0)
        x = x_ref[...]
        xs = [_shift_down(x, CONV_K - 1 - tap) for tap in range(CONV_K)]
        y = jnp.zeros_like(x)
        for tap in range(CONV_K):
            y = y + w_ref[tap:tap + 1, :] * xs[tap]
        sig = jax.nn.sigmoid(y)
        c = y * sig
        r = lax.rsqrt(jnp.sum(c * c, axis=1, keepdims=True) + NORM_EPS)
        nrm = c * r
        d = d_ref[...]
        dc_norm = r * (d - nrm * jnp.sum(d * nrm, axis=1, keepdims=True))
        dc = jnp.where(j < 2 * GDN_HEADS, dc_norm, d)
        dy = dc * (sig * (1.0 + y * (1.0 - sig)))
        dx = jnp.zeros_like(x)
        for tap in range(CONV_K):
            dx = dx + _shift_up(w_ref[tap:tap + 1, :] * dy, CONV_K - 1 - tap)
            dw_ref[tap:tap + 1, :] = jnp.sum(dy * xs[tap], axis=0, keepdims=True)
        dx_ref[...] = dx.astype(dx_ref.dtype)

    return pl.pallas_call(
        body, grid=(QKV_COLS // LANE,),
        in_specs=[pl.BlockSpec((t, LANE), lambda j: (0, j)), pl.BlockSpec((CONV_K, LANE), lambda j: (0, j)),
                  pl.BlockSpec((t, LANE), lambda j: (0, _post_col(j))), pl.BlockSpec(memory_space=pl.ANY)],
        out_specs=[pl.BlockSpec((t, LANE), lambda j: (0, j)), pl.BlockSpec((CONV_K, LANE), lambda j: (0, j))],
        out_shape=[jax.ShapeDtypeStruct(dproj.shape, dproj.dtype), jax.ShapeDtypeStruct((CONV_K, QKV_COLS), F32)],
        input_output_aliases={3: 0},
        compiler_params=_params("parallel"), name="gdn_prep_bwd",
    )(proj, conv_w, dpost, dproj)


def _softplus(v):
    return jnp.maximum(v, 0.0) + jnp.log(1.0 + jnp.exp(-jnp.abs(v)))


def _tri_inv(low, nn):
    r = lax.broadcasted_iota(jnp.int32, (CHUNK, CHUNK), 0)
    c = lax.broadcasted_iota(jnp.int32, (CHUNK, CHUNK), 1)
    eye = (r == c).astype(F32)
    same_blk = lax.shift_right_logical(r, 4) == lax.shift_right_logical(c, 4)
    diag = jnp.where(same_blk, low, 0.0)
    off = low - diag
    n1 = -diag
    n2 = nn(n1, n1)
    n4 = nn(n2, n2)
    n8 = nn(n4, n4)
    inv_d = nn(nn(nn(eye + n1, eye + n2), eye + n4), eye + n8)
    m1 = nn(inv_d, off)
    m2 = nn(m1, m1)
    return nn(nn(eye - m1, eye + m2), inv_d)


def _chunk_fn(qkv, ba, alog_row, dtb_row, state, h, bdots, fdots):
    nn, nt, tn = bdots
    fnn = fdots[0]
    q = qkv[:, 0:HEAD_DIM] * (HEAD_DIM ** -0.5)
    k = qkv[:, HEAD_DIM:2 * HEAD_DIM]
    v = qkv[:, 2 * HEAD_DIM:3 * HEAD_DIM]
    lane = lax.broadcasted_iota(jnp.int32, ba.shape, 1)
    bg = jnp.where(lane < GDN_HEADS, jax.nn.sigmoid(ba), -jnp.exp(alog_row) * _softplus(ba + dtb_row))
    beta = jnp.sum(jnp.where(lane == h, bg, 0.0), axis=1, keepdims=True)
    g = jnp.sum(jnp.where(lane == h + GDN_HEADS, bg, 0.0), axis=1, keepdims=True)

    r = lax.broadcasted_iota(jnp.int32, (CHUNK, CHUNK), 0)
    c = lax.broadcasted_iota(jnp.int32, (CHUNK, CHUNK), 1)
    incl = r >= c
    strict = r > c
    eye = r == c

    def to_row(col):
        return jnp.sum(jnp.where(eye, col, 0.0), axis=0, keepdims=True)

    gc = jnp.sum(jnp.where(incl, to_row(g), 0.0), axis=1, keepdims=True)
    diff = gc - to_row(gc)
    decay = jnp.where(incl, jnp.exp(jnp.where(incl, diff, 0.0)), 0.0)
    k_beta = k * beta
    v_beta = v * beta
    low = jnp.where(strict, nt(k_beta, k) * decay, 0.0)
    t_inv = _tri_inv(low, fnn)
    eg = jnp.exp(gc)
    u = fnn(t_inv, v_beta)
    w = fnn(t_inv, k_beta * eg)
    attn = jnp.where(incl, nt(q, k) * decay, 0.0)
    v_new = u - nn(w, state)
    o = nn(q * eg, state) + nn(attn, v_new)
    last = lax.broadcasted_iota(jnp.int32, (CHUNK, 1), 0) == CHUNK - 1
    g_last = jnp.sum(jnp.where(last, gc, 0.0), axis=0, keepdims=True)
    k_dec = k * jnp.exp(g_last - gc)
    new_state = state * jnp.exp(g_last) + tn(k_dec, v_new)
    return o, new_state


def _gdn_scan_fwd(post, proj, alog_row, dtb_row):
    t = post.shape[0]
    n_chunks = t // CHUNK

    def body(qkv_ref, ba_ref, al_ref, dt_ref, o_ref, save_ref, state_ref):
        n = pl.program_id(0)
        h = pl.program_id(1)

        @pl.when(n == 0)
        def _():
            state_ref[h] = jnp.zeros((HEAD_DIM, HEAD_DIM), F32)

        state = state_ref[h]
        save_ref[0, 0] = state
        o, new_state = _chunk_fn(qkv_ref[...], ba_ref[...], al_ref[...], dt_ref[...], state, h, _BDOT_PLAIN, _FDOT_PLAIN)
        o_ref[...] = o
        state_ref[h] = new_state

    return pl.pallas_call(
        body, grid=(n_chunks, GDN_HEADS),
        in_specs=[pl.BlockSpec((CHUNK, 3 * HEAD_DIM), lambda n, h: (n, h)),
                  pl.BlockSpec((CHUNK, LANE), lambda n, h: (n, BA_BLK)),
                  pl.BlockSpec((1, LANE), lambda n, h: (0, 0)), pl.BlockSpec((1, LANE), lambda n, h: (0, 0))],
        out_specs=[pl.BlockSpec((CHUNK, HEAD_DIM), lambda n, h: (n, h)),
                   pl.BlockSpec((1, 1, HEAD_DIM, HEAD_DIM), lambda n, h: (h, n, 0, 0))],
        out_shape=[jax.ShapeDtypeStruct((t, GDN_WIDTH), F32),
                   jax.ShapeDtypeStruct((GDN_HEADS, n_chunks, HEAD_DIM, HEAD_DIM), F32)],
        scratch_shapes=[pltpu.VMEM((GDN_HEADS, HEAD_DIM, HEAD_DIM), F32)],
        compiler_params=_params("arbitrary", "arbitrary"), name="gdn_scan_fwd",
    )(post, proj, alog_row, dtb_row)


def _gdn_scan_bwd(post, proj, alog_row, dtb_row, saved, do, dproj):
    t = post.shape[0]
    n_chunks = t // CHUNK
    last = n_chunks - 1

    def body(qkv_ref, ba_ref, al_ref, dt_ref, save_ref, do_ref, _, dqkv_ref, dba_ref, dal_ref, ddt_ref, dstate_ref, dba_acc):
        n = pl.program_id(0)
        h = pl.program_id(1)

        @pl.when(n == 0)
        def _():
            dstate_ref[h] = jnp.zeros((HEAD_DIM, HEAD_DIM), F32)

        @pl.when((n == 0) & (h == 0))
        def _():
            dal_ref[...] = jnp.zeros_like(dal_ref)
            ddt_ref[...] = jnp.zeros_like(ddt_ref)

        def f(qkv, ba, al, dt, state):
            return _chunk_fn(qkv, ba, al, dt, state, h, _BDOT_VJP, _FDOT_VJP)

        _, vjp = jax.vjp(f, qkv_ref[...], ba_ref[...], al_ref[...], dt_ref[...], save_ref[0, 0])
        dqkv, dba, dal, ddt, dstate = vjp((do_ref[...], dstate_ref[h]))
        dqkv_ref[...] = dqkv

        @pl.when(h == 0)
        def _():
            dba_acc[...] = dba

        @pl.when(h > 0)
        def _():
            dba_acc[...] += dba

        @pl.when(h == GDN_HEADS - 1)
        def _():
            dba_ref[:, 0:LANE] = dba_acc[...].astype(dba_ref.dtype)
            dba_ref[:, LANE:2 * LANE] = jnp.zeros((CHUNK, LANE), dba_ref.dtype)

        dal_ref[...] += dal
        ddt_ref[...] += ddt
        dstate_ref[h] = dstate

    return pl.pallas_call(
        body, grid=(n_chunks, GDN_HEADS),
        in_specs=[pl.BlockSpec((CHUNK, 3 * HEAD_DIM), lambda n, h: (last - n, h)),
                  pl.BlockSpec((CHUNK, LANE), lambda n, h: (last - n, BA_BLK)),
                  pl.BlockSpec((1, LANE), lambda n, h: (0, 0)), pl.BlockSpec((1, LANE), lambda n, h: (0, 0)),
                  pl.BlockSpec((1, 1, HEAD_DIM, HEAD_DIM), lambda n, h: (h, last - n, 0, 0)),
                  pl.BlockSpec((CHUNK, HEAD_DIM), lambda n, h: (last - n, h)),
                  pl.BlockSpec(memory_space=pl.ANY)],
        out_specs=[pl.BlockSpec((CHUNK, 3 * HEAD_DIM), lambda n, h: (last - n, h)),
                   pl.BlockSpec((CHUNK, 2 * LANE), lambda n, h: (last - n, BA_BLK // 2)),
                   pl.BlockSpec((1, LANE), lambda n, h: (0, 0)), pl.BlockSpec((1, LANE), lambda n, h: (0, 0))],
        out_shape=[jax.ShapeDtypeStruct((t, QKV_COLS), F32), jax.ShapeDtypeStruct(dproj.shape, dproj.dtype),
                   jax.ShapeDtypeStruct((1, LANE), F32), jax.ShapeDtypeStruct((1, LANE), F32)],
        input_output_aliases={6: 1},
        scratch_shapes=[pltpu.VMEM((GDN_HEADS, HEAD_DIM, HEAD_DIM), F32), pltpu.VMEM((CHUNK, LANE), F32)],
        compiler_params=_params("arbitrary", "arbitrary"), name="gdn_scan_bwd",
    )(post, proj, alog_row, dtb_row, saved, do, dproj)


def _onorm_fn(o, z, w):
    return o * lax.rsqrt(jnp.mean(o * o, axis=1, keepdims=True) + NORM_EPS) * w * (z * jax.nn.sigmoid(z))


def _onorm_fwd(o_raw, proj, norm_w, mixin, tm=512):
    t = o_raw.shape[0]
    tm = min(tm, t)

    def body(o_ref, z_ref, w_ref, _, out_ref):
        out_ref[...] = _onorm_fn(o_ref[...], z_ref[...], w_ref[...]).astype(out_ref.dtype)

    return pl.pallas_call(
        body, grid=(t // tm, GDN_HEADS),
        in_specs=[pl.BlockSpec((tm, LANE), lambda i, h: (i, h)), pl.BlockSpec((tm, LANE), lambda i, h: (i, Z_BLK + h)),
                  pl.BlockSpec((1, LANE), lambda i, h: (0, 0)), pl.BlockSpec(memory_space=pl.ANY)],
        out_specs=pl.BlockSpec((tm, LANE), lambda i, h: (i, h)),
        out_shape=jax.ShapeDtypeStruct(mixin.shape, mixin.dtype), input_output_aliases={3: 0},
        compiler_params=_params("parallel", "parallel"), name="gdn_onorm_fwd",
    )(o_raw, proj, norm_w, mixin)


def _onorm_bwd(o_raw, proj, norm_w, dmixin, dproj, tm=512):
    t = o_raw.shape[0]
    tm = min(tm, t)

    def body(o_ref, z_ref, w_ref, d_ref, _, do_ref, dz_ref, dw_ref):
        @pl.when((pl.program_id(0) == 0) & (pl.program_id(1) == 0))
        def _():
            dw_ref[...] = jnp.zeros_like(dw_ref)

        _, vjp = jax.vjp(_onorm_fn, o_ref[...], z_ref[...], w_ref[...])
        do, dz, dw = vjp(d_ref[...])
        do_ref[...] = do
        dz_ref[...] = dz.astype(dz_ref.dtype)
        dw_ref[...] += dw

    return pl.pallas_call(
        body, grid=(t // tm, GDN_HEADS),
        in_specs=[pl.BlockSpec((tm, LANE), lambda i, h: (i, h)), pl.BlockSpec((tm, LANE), lambda i, h: (i, Z_BLK + h)),
                  pl.BlockSpec((1, LANE), lambda i, h: (0, 0)), pl.BlockSpec((tm, LANE), lambda i, h: (i, h)),
                  pl.BlockSpec(memory_space=pl.ANY)],
        out_specs=[pl.BlockSpec((tm, LANE), lambda i, h: (i, h)), pl.BlockSpec((tm, LANE), lambda i, h: (i, Z_BLK + h)),
                   pl.BlockSpec((1, LANE), lambda i, h: (0, 0))],
        out_shape=[jax.ShapeDtypeStruct((t, GDN_WIDTH), F32), jax.ShapeDtypeStruct(dproj.shape, dproj.dtype),
                   jax.ShapeDtypeStruct((1, LANE), F32)],
        input_output_aliases={4: 1},
        compiler_params=_params("arbitrary", "arbitrary"), name="gdn_onorm_bwd",
    )(o_raw, proj, norm_w, dmixin, dproj)


def _pool_select(levels, gi):
    out = levels[-1]
    for lvl in range(len(levels) - 2, -1, -1):
        out = jnp.where(gi == lvl, levels[lvl], out)
    return out


def _pool_count(shape, gi):
    pos = lax.broadcasted_iota(jnp.int32, shape, 0)
    win = lax.shift_left(jnp.int32(2), gi)
    return jnp.minimum(pos + 1, win).astype(F32)


def _pooled(p, gi):
    acc = p
    levels = []
    for lvl in range(POOL_GROUPS):
        acc = acc + _shift_down(acc, 1 << lvl)
        levels.append(acc)
    return _pool_select(levels, gi) / _pool_count(p.shape, gi) - p


def _pool_fwd(proj, pool_w, pool_scale):
    t = proj.shape[0]

    def body(p_ref, w_ref, s_ref, out_ref):
        gi = pl.program_id(0)
        pooled = _pooled(p_ref[...], gi)
        out_ref[...] = (_BDOT_PLAIN[0](pooled, w_ref[0]) * s_ref[0]).astype(out_ref.dtype)

    return pl.pallas_call(
        body, grid=(POOL_GROUPS,),
        in_specs=[pl.BlockSpec((t, POOL_GROUP_DIM), lambda g: (0, POOL_BLK + g)),
                  pl.BlockSpec((1, POOL_GROUP_DIM, POOL_GROUP_DIM), lambda g: (g, 0, 0)),
                  pl.BlockSpec((1, 1, POOL_GROUP_DIM), lambda g: (g, 0, 0))],
        out_specs=pl.BlockSpec((t, POOL_GROUP_DIM), lambda g: (0, GDN_WIDTH // POOL_GROUP_DIM + g)),
        out_shape=jax.ShapeDtypeStruct((t, 2 * GDN_WIDTH), BF16),
        compiler_params=_params("parallel"), name="pool_fwd",
    )(proj, pool_w, pool_scale)


def _pool_bwd(proj, pool_w, pool_scale, dmixin):
    t = proj.shape[0]
    nn, nt, tn = _BDOT_PLAIN

    def body(p_ref, w_ref, s_ref, d_ref, dp_ref, dw_ref, ds_ref):
        gi = pl.program_id(0)
        p = p_ref[...]
        pooled = _pooled(p, gi)
        mixed = nn(pooled, w_ref[0])
        d = d_ref[...]
        ds_ref[0] = jnp.sum(d * mixed, axis=0, keepdims=True)
        dmixed = d * s_ref[0]
        dw_ref[0] = tn(pooled, dmixed)
        dpooled = nt(dmixed, w_ref[0])
        acc = dpooled / _pool_count(p.shape, gi)
        levels = []
        for lvl in range(POOL_GROUPS):
            acc = acc + _shift_up(acc, 1 << lvl)
            levels.append(acc)
        dp_ref[...] = (_pool_select(levels, gi) - dpooled).astype(dp_ref.dtype)

    return pl.pallas_call(
        body, grid=(POOL_GROUPS,),
        in_specs=[pl.BlockSpec((t, POOL_GROUP_DIM), lambda g: (0, POOL_BLK + g)),
                  pl.BlockSpec((1, POOL_GROUP_DIM, POOL_GROUP_DIM), lambda g: (g, 0, 0)),
                  pl.BlockSpec((1, 1, POOL_GROUP_DIM), lambda g: (g, 0, 0)),
                  pl.BlockSpec((t, POOL_GROUP_DIM), lambda g: (0, GDN_WIDTH // POOL_GROUP_DIM + g))],
        out_specs=[pl.BlockSpec((t, POOL_GROUP_DIM), lambda g: (0, POOL_BLK + g)),
                   pl.BlockSpec((1, POOL_GROUP_DIM, POOL_GROUP_DIM), lambda g: (g, 0, 0)),
                   pl.BlockSpec((1, 1, POOL_GROUP_DIM), lambda g: (g, 0, 0))],
        out_shape=[jax.ShapeDtypeStruct((t, PROJ_COLS), BF16),
                   jax.ShapeDtypeStruct((POOL_GROUPS, POOL_GROUP_DIM, POOL_GROUP_DIM), F32),
                   jax.ShapeDtypeStruct((POOL_GROUPS, 1, POOL_GROUP_DIM), F32)],
        compiler_params=_params("parallel"), name="pool_bwd",
    )(proj, pool_w, pool_scale, dmixin)


def _ln_stats(s):
    mu = jnp.mean(s, axis=1, keepdims=True)
    xc = s - mu
    var = jnp.mean(xc * xc, axis=1, keepdims=True)
    rstd = lax.rsqrt(var + LN_EPS)
    return xc * rstd, rstd


def _ln_fwd(h_in, y, g, b, *, name, tm=256):
    t, d = h_in.shape
    tm = min(tm, t)

    def body(h_ref, y_ref, g_ref, b_ref, o_ref, o16_ref):
        xhat, _ = _ln_stats(ALPHA * h_ref[...] + y_ref[...])
        out = xhat * g_ref[...] + b_ref[...]
        o_ref[...] = out
        o16_ref[...] = out.astype(BF16)

    row = pl.BlockSpec((tm, d), lambda i: (i, 0))
    vec = pl.BlockSpec((1, d), lambda i: (0, 0))
    return pl.pallas_call(
        body, grid=(t // tm,), in_specs=[row, row, vec, vec], out_specs=[row, row],
        out_shape=[jax.ShapeDtypeStruct((t, d), F32), jax.ShapeDtypeStruct((t, d), BF16)],
        compiler_params=_params("parallel"), name=name,
    )(h_in, y, g, b)


def _ln_loss_fwd(h_in, y, g, b, target, *, name, tm=256):
    t, d = h_in.shape
    tm = min(tm, t)

    def body(h_ref, y_ref, g_ref, b_ref, t_ref, dy_ref, sq_ref):
        @pl.when(pl.program_id(0) == 0)
        def _():
            sq_ref[...] = jnp.zeros_like(sq_ref)

        xhat, _ = _ln_stats(ALPHA * h_ref[...] + y_ref[...])
        err = xhat * g_ref[...] + b_ref[...] - t_ref[...]
        dy_ref[...] = err * (1.0 / d)
        sq_ref[...] += jnp.sum(jnp.sum(err * err, axis=1, keepdims=True), axis=0, keepdims=True)

    row = pl.BlockSpec((tm, d), lambda i: (i, 0))
    vec = pl.BlockSpec((1, d), lambda i: (0, 0))
    return pl.pallas_call(
        body, grid=(t // tm,), in_specs=[row, row, vec, vec, row],
        out_specs=[row, pl.BlockSpec((1, LANE), lambda i: (0, 0))],
        out_shape=[jax.ShapeDtypeStruct((t, d), F32), jax.ShapeDtypeStruct((1, LANE), F32)],
        compiler_params=_params("arbitrary"), name=name,
    )(h_in, y, g, b, target)


def _ln_bwd(h_in, y, g, d_a, d_b, *, name, tm=256):
    t, d = h_in.shape
    tm = min(tm, t)
    has_b = d_b is not None

    def body(*refs):
        if has_b:
            h_ref, y_ref, g_ref, da_ref, db_ref, ds_ref, ds16_ref, dg_ref, dbias_ref = refs
        else:
            h_ref, y_ref, g_ref, da_ref, ds_ref, ds16_ref, dg_ref, dbias_ref = refs

        @pl.when(pl.program_id(0) == 0)
        def _():
            dg_ref[...] = jnp.zeros_like(dg_ref)
            dbias_ref[...] = jnp.zeros_like(dbias_ref)

        xhat, rstd = _ln_stats(ALPHA * h_ref[...] + y_ref[...])
        dout = da_ref[...]
        if has_b:
            dout = dout + ALPHA * db_ref[...]
        dxhat = dout * g_ref[...]
        m1 = jnp.mean(dxhat, axis=1, keepdims=True)
        m2 = jnp.mean(dxhat * xhat, axis=1, keepdims=True)
        ds = rstd * (dxhat - m1 - xhat * m2)
        ds_ref[...] = ds
        ds16_ref[...] = ds.astype(BF16)
        dg_ref[...] += jnp.sum(dout * xhat, axis=0, keepdims=True)
        dbias_ref[...] += jnp.sum(dout, axis=0, keepdims=True)

    row = pl.BlockSpec((tm, d), lambda i: (i, 0))
    vec = pl.BlockSpec((1, d), lambda i: (0, 0))
    args = [h_in, y, g, d_a] + ([d_b] if has_b else [])
    return pl.pallas_call(
        body, grid=(t // tm,), in_specs=[row, row, vec, row] + ([row] if has_b else []),
        out_specs=[row, row, vec, vec],
        out_shape=[jax.ShapeDtypeStruct((t, d), F32), jax.ShapeDtypeStruct((t, d), BF16),
                   jax.ShapeDtypeStruct((1, d), F32), jax.ShapeDtypeStruct((1, d), F32)],
        compiler_params=_params("arbitrary"), name=name,
    )(*args)


def _attn_fn(q, k, v, dots):
    nn, nt, _ = dots
    s = nt(q, k) * (XATTN_HEAD_DIM ** -0.5)
    s = s - lax.stop_gradient(jnp.max(s, axis=1, keepdims=True))
    e = jnp.exp(s)
    p = e / jnp.sum(e, axis=1, keepdims=True)
    return nn(p, v)


def _attn_fwd(q, k, v, tq=512):
    t = q.shape[0]
    tq = min(tq, t)

    def body(q_ref, k_ref, v_ref, o_ref):
        o_ref[...] = _attn_fn(q_ref[...], k_ref[...], v_ref[...], _BDOT_PLAIN).astype(BF16)

    qs = pl.BlockSpec((tq, XATTN_HEAD_DIM), lambda h, i: (i, h))
    ks = pl.BlockSpec((MEM_LEN, XATTN_HEAD_DIM), lambda h, i: (0, h))
    return pl.pallas_call(
        body, grid=(XATTN_HEADS, t // tq), in_specs=[qs, ks, ks], out_specs=qs,
        out_shape=jax.ShapeDtypeStruct(q.shape, BF16), compiler_params=_params("parallel", "parallel"), name="xattn_fwd",
    )(q, k, v)


def _attn_bwd(q, k, v, do, tq=512):
    t = q.shape[0]
    tq = min(tq, t)

    def body(q_ref, k_ref, v_ref, do_ref, dq_ref, dk_ref, dv_ref):
        @pl.when(pl.program_id(1) == 0)
        def _():
            dk_ref[...] = jnp.zeros_like(dk_ref)
            dv_ref[...] = jnp.zeros_like(dv_ref)

        _, vjp = jax.vjp(lambda a, b, c: _attn_fn(a, b, c, _BDOT_VJP), q_ref[...].astype(F32), k_ref[...].astype(F32),
                         v_ref[...].astype(F32))
        dq, dk, dv = vjp(do_ref[...].astype(F32))
        dq_ref[...] = dq.astype(BF16)
        dk_ref[...] += dk
        dv_ref[...] += dv

    qs = pl.BlockSpec((tq, XATTN_HEAD_DIM), lambda h, i: (i, h))
    ks = pl.BlockSpec((MEM_LEN, XATTN_HEAD_DIM), lambda h, i: (0, h))
    return pl.pallas_call(
        body, grid=(XATTN_HEADS, t // tq), in_specs=[qs, ks, ks, qs], out_specs=[qs, ks, ks],
        out_shape=[jax.ShapeDtypeStruct(q.shape, BF16), jax.ShapeDtypeStruct(k.shape, F32), jax.ShapeDtypeStruct(v.shape, F32)],
        compiler_params=_params("parallel", "arbitrary"), name="xattn_bwd",
    )(q, k, v, do)


def _local_step(x, mem, target, w):
    x16 = _cast_bf16(x, name="cast_x")
    proj = _mm(x16, w["w_in"], tn=768, name="mm_in_proj")
    post = _gdn_prep_fwd(proj, w["conv_w"])
    o_raw, saved = _gdn_scan_fwd(post, proj, w["alog_row"], w["dtb_row"])
    mixin = _pool_fwd(proj, w["pool_w"], w["pool_scale"])
    mixin = _onorm_fwd(o_raw, proj, w["gdn_norm_w"], mixin)
    mix = _mm(mixin, w["w_out"], name="mm_out_proj")
    h1, h1_16 = _ln_fwd(x, mix, w["ln1_g"], w["ln1_b"], name="ln1_fwd")
    xq = _mm(h1_16, w["xq_w"], out_dtype=BF16, name="mm_xq")
    xk = _mm(mem, w["xk_w"], out_dtype=BF16, name="mm_xk")
    xv = _mm(mem, w["xv_w"], out_dtype=BF16, name="mm_xv")
    xo = _attn_fwd(xq, xk, xv)
    xa = _mm(xo, w["xo_w"], name="mm_xo")
    h2, h2_16 = _ln_fwd(h1, xa, w["ln2_g"], w["ln2_b"], name="ln2_fwd")
    act, relu = _mm(h2_16, w["w_up"], b_chunks=True, epi="relu2", name="mm_up")
    ff = _mm(act, w["w_down"], tn=1024, tk=512, name="mm_down")
    dy, sq = _ln_loss_fwd(h2, ff, w["ln3_g"], w["ln3_b"], target, name="ln3_loss_fwd")

    g = {}
    ds3, ds3_16, g["ln3_g"], g["ln3_b"] = _ln_bwd(h2, ff, w["ln3_g"], dy, None, name="ln3_bwd")
    g["w_down"] = _mm(act, ds3_16, ta=True, out_dtype=BF16, tm=512, tn=D_MODEL, name="mm_gw_down")
    du = _mm(ds3_16, w["w_down"], tb=True, epi="mul2r", extra=relu, name="mm_du")
    g["w_up"] = _mm(h2_16, du, ta=True, out_dtype=BF16, o_chunks=True, name="mm_gw_up")
    dh2 = _mm(du, w["w_up"], tb=True, b_chunks=True, tn=1024, tk=512, name="mm_dh2")
    ds2, ds2_16, g["ln2_g"], g["ln2_b"] = _ln_bwd(h1, xa, w["ln2_g"], dh2, ds3, name="ln2_bwd")
    g["xo_w"] = _mm(xo, ds2_16, ta=True, out_dtype=BF16, name="mm_gw_xo")
    dxo = _mm(ds2_16, w["xo_w"], tb=True, out_dtype=BF16, name="mm_dxo")
    dxq, dxk, dxv = _attn_bwd(xq, xk, xv, dxo)
    g["xq_w"] = _mm(h1_16, dxq, ta=True, out_dtype=BF16, name="mm_gw_xq")
    g["xk_w"] = _mm(mem, dxk, ta=True, out_dtype=BF16, name="mm_gw_xk")
    g["xv_w"] = _mm(mem, dxv, ta=True, out_dtype=BF16, name="mm_gw_xv")
    dh1 = _mm(dxq, w["xq_w"], tb=True, name="mm_dh1")
    ds1, ds1_16, g["ln1_g"], g["ln1_b"] = _ln_bwd(x, mix, w["ln1_g"], dh1, ds2, name="ln1_bwd")
    g["w_out"] = _mm(mixin, ds1_16, ta=True, out_dtype=BF16, name="mm_gw_out")
    dmixin = _mm(ds1_16, w["w_out"], tb=True, name="mm_dmixin")
    dproj, g["pool_w"], g["pool_scale"] = _pool_bwd(proj, w["pool_w"], w["pool_scale"], dmixin)
    do_raw, dproj, g["gdn_norm_w"] = _onorm_bwd(o_raw, proj, w["gdn_norm_w"], dmixin, dproj)
    dpost, dproj, g["alog_row"], g["dtb_row"] = _gdn_scan_bwd(post, proj, w["alog_row"], w["dtb_row"], saved, do_raw, dproj)
    dproj, g["conv_w"] = _gdn_prep_bwd(proj, w["conv_w"], dpost, dproj)
    g["w_in"] = _mm(x16, dproj, ta=True, out_dtype=BF16, tn=768, name="mm_gw_in")
    grad_x = _mm(dproj, w["w_in"], tb=True, tk=768, epi="add", extra=ds1, add_scale=ALPHA, name="mm_dx")
    return sq, grad_x, g


_MATRICES = ("w_in", "pool_w", "w_out", "xq_w", "xk_w", "xv_w", "xo_w", "w_up", "w_down")
_VECTORS = ("a_log", "dt_bias", "gdn_norm_w", "pool_scale", "ln1_g", "ln1_b", "ln2_g", "ln2_b", "ln3_g", "ln3_b")
_BA_SPLIT = BA_OFF + 2 * GDN_HEADS


def _lane_row(v, offset):
    return jnp.zeros((1, LANE), F32).at[0, offset:offset + v.shape[0]].set(v)


def _prepare_weights(full):
    w_in = full["w_in"]
    zeros = jnp.zeros((w_in.shape[0], POOL_OFF - _BA_SPLIT), w_in.dtype)
    w = {
        "w_in": jnp.concatenate([w_in[:, :_BA_SPLIT], zeros, w_in[:, _BA_SPLIT:]], axis=1),
        "conv_w": full["conv_w"],
        "alog_row": _lane_row(full["a_log"], GDN_HEADS),
        "dtb_row": _lane_row(full["dt_bias"], GDN_HEADS),
        "gdn_norm_w": full["gdn_norm_w"].reshape(1, LANE),
        "pool_w": full["pool_w"],
        "pool_scale": full["pool_scale"].reshape(POOL_GROUPS, 1, POOL_GROUP_DIM),
    }
    for n in ("w_out", "xq_w", "xk_w", "xv_w", "xo_w", "w_up", "w_down"):
        w[n] = full[n]
    for n in ("ln1_g", "ln1_b", "ln2_g", "ln2_b", "ln3_g", "ln3_b"):
        w[n] = full[n].reshape(1, D_MODEL)
    return w


def _finish_grads(g):
    out = {n: g[n] for n in ("conv_w", "pool_w", "w_out", "xq_w", "xk_w", "xv_w", "xo_w", "w_up", "w_down")}
    out["w_in"] = jnp.concatenate([g["w_in"][:, :_BA_SPLIT], g["w_in"][:, POOL_OFF:]], axis=1)
    out["a_log"] = g["alog_row"][0, GDN_HEADS:2 * GDN_HEADS]
    out["dt_bias"] = g["dtb_row"][0, GDN_HEADS:2 * GDN_HEADS]
    out["gdn_norm_w"] = g["gdn_norm_w"].reshape(LANE)
    out["pool_scale"] = g["pool_scale"].reshape(POOL_GROUPS * POOL_GROUP_DIM)
    for n in ("ln1_g", "ln1_b", "ln2_g", "ln2_b", "ln3_g", "ln3_b"):
        out[n] = g[n].reshape(D_MODEL)
    return out


def _adamw_math(w, g, m, v):
    m = ADAM_B1 * m + (1.0 - ADAM_B1) * g
    v = ADAM_B2 * v + (1.0 - ADAM_B2) * (g * g)
    m_hat = m / (1.0 - ADAM_B1 ** ADAM_STEP)
    v_hat = v / (1.0 - ADAM_B2 ** ADAM_STEP)
    delta = -ADAM_LR * (m_hat / (jnp.sqrt(v_hat) + ADAM_EPS) + ADAM_WD * w)
    return delta, m, v


def _adamw_shard(parts, w, m, v, *, tr, name):
    s, r, c = parts.shape
    tr = min(tr, r)
    assert r % tr == 0, (name, r, tr)

    def body(p_ref, w_ref, m_ref, v_ref, g_ref, d_ref, nm_ref, nv_ref):
        g = p_ref[0].astype(F32)
        for i in range(1, s):
            g = g + p_ref[i].astype(F32)
        delta, nm, nv = _adamw_math(w_ref[...], g, m_ref[...], v_ref[...])
        g_ref[...] = g
        d_ref[...] = delta
        nm_ref[...] = nm
        nv_ref[...] = nv

    blk = pl.BlockSpec((tr, c), lambda i: (i, 0))
    out = jax.ShapeDtypeStruct((r, c), F32)
    return pl.pallas_call(
        body, grid=(r // tr,), in_specs=[pl.BlockSpec((s, tr, c), lambda i: (0, i, 0)), blk, blk, blk],
        out_specs=[blk, blk, blk, blk], out_shape=[out, out, out, out],
        compiler_params=_params("parallel"), name=name,
    )(parts, w, m, v)


def _place():
    return lax.axis_index("x"), lax.axis_index("y"), lax.axis_index("c")


def _slot(px, py, pc):
    return 4 * px + 2 * py + pc


_HBM = pl.BlockSpec(memory_space=pltpu.HBM)


def _all_gather(shards, *, name):
    n = len(shards)

    def body(*refs):
        ins, outs = refs[:n], refs[n:2 * n]
        send_sems, recv_sems, local_sems = refs[2 * n:]
        x, y, c = _place()
        me = _slot(x, y, c)
        sibling = (x, y, 1 - c)
        chips = [(1 - x, y), (x, 1 - y), (1 - x, 1 - y)]

        def copy(w, k, slot, to, src=None):
            dst = outs[w].at[slot]
            return pltpu.make_async_remote_copy(
                src_ref=dst if src is None else src, dst_ref=dst, send_sem=send_sems.at[w, k], recv_sem=recv_sems.at[w, k],
                device_id=to, device_id_type=MESH)

        local = [pltpu.make_async_copy(ins[w], outs[w].at[me], local_sems.at[w]) for w in range(n)]
        for cp in local:
            cp.start()
        first = []
        for w in range(n):
            first.append(copy(w, 0, me, sibling, src=ins[w]))
            for j, chip in enumerate(chips):
                first.append(copy(w, 1 + j, me, (*chip, c), src=ins[w]))
        for cp in first:
            cp.start()
        passed = []
        for w in range(n):
            for j, chip in enumerate(chips):
                copy(w, 1 + j, _slot(*chip, c), (x, y, c)).wait_recv()
                fwd = copy(w, 4 + j, _slot(*chip, c), sibling)
                fwd.start()
                passed.append(fwd)
        for w in range(n):
            copy(w, 0, _slot(x, y, 1 - c), (x, y, c)).wait_recv()
            for j, chip in enumerate(chips):
                copy(w, 4 + j, _slot(*chip, 1 - c), (x, y, c)).wait_recv()
        for cp in first + passed:
            cp.wait_send()
        for cp in local:
            cp.wait()

    return pl.pallas_call(
        body, in_specs=[_HBM] * n, out_specs=[_HBM] * n,
        out_shape=[jax.ShapeDtypeStruct((N_DEV, *s.shape), s.dtype) for s in shards],
        scratch_shapes=[pltpu.SemaphoreType.DMA((n, 7)), pltpu.SemaphoreType.DMA((n, 7)), pltpu.SemaphoreType.DMA((n,))],
        name=name,
    )(*shards)


def _peer(k, x, y, c):
    return (1 - x if k & 4 else x, 1 - y if k & 2 else y, 1 - c if k & 1 else c)


def _scatter_to_owners(chunked, *, name):
    n = len(chunked)

    def body(*refs):
        ins, outs = refs[:n], refs[n:2 * n]
        send_sems, recv_sems, local_sems = refs[2 * n:]
        x, y, c = _place()
        me = _slot(x, y, c)
        local = [pltpu.make_async_copy(ins[w].at[me], outs[w].at[me], local_sems.at[w]) for w in range(n)]
        for cp in local:
            cp.start()
        sends = []
        for w in range(n):
            for k in range(1, N_DEV):
                peer = _peer(k, x, y, c)
                sends.append(pltpu.make_async_remote_copy(
                    src_ref=ins[w].at[_slot(*peer)], dst_ref=outs[w].at[me], send_sem=send_sems.at[w, k - 1],
                    recv_sem=recv_sems.at[w, k - 1], device_id=peer, device_id_type=MESH))
        for cp in sends:
            cp.start()
        for w in range(n):
            for k in range(1, N_DEV):
                peer = _peer(k, x, y, c)
                pltpu.make_async_remote_copy(
                    src_ref=ins[w].at[_slot(*peer)], dst_ref=outs[w].at[_slot(*peer)], send_sem=send_sems.at[w, k - 1],
                    recv_sem=recv_sems.at[w, k - 1], device_id=peer, device_id_type=MESH).wait_recv()
        for cp in sends:
            cp.wait_send()
        for cp in local:
            cp.wait()

    return pl.pallas_call(
        body, in_specs=[_HBM] * n, out_specs=[_HBM] * n,
        out_shape=[jax.ShapeDtypeStruct(a.shape, a.dtype) for a in chunked],
        scratch_shapes=[pltpu.SemaphoreType.DMA((n, 7)), pltpu.SemaphoreType.DMA((n, 7)), pltpu.SemaphoreType.DMA((n,))],
        name=name,
    )(*chunked)


def _small_allreduce_adamw(gvec, wvec, mvec, vvec):
    rows, length = gvec.shape

    def body(g_ref, w_ref, m_ref, v_ref, gs_ref, d_ref, nm_ref, nv_ref, slots, send_sems, recv_sems):
        x, y, c = _place()
        me = _slot(x, y, c)
        slots[me] = g_ref[...]
        sends = []
        for k in range(1, N_DEV):
            peer = _peer(k, x, y, c)
            sends.append(pltpu.make_async_remote_copy(
                src_ref=g_ref, dst_ref=slots.at[me], send_sem=send_sems.at[k - 1], recv_sem=recv_sems.at[k - 1],
                device_id=peer, device_id_type=MESH))
        for cp in sends:
            cp.start()
        for k in range(1, N_DEV):
            peer = _peer(k, x, y, c)
            pltpu.make_async_remote_copy(
                src_ref=g_ref, dst_ref=slots.at[_slot(*peer)], send_sem=send_sems.at[k - 1], recv_sem=recv_sems.at[k - 1],
                device_id=peer, device_id_type=MESH).wait_recv()
        for cp in sends:
            cp.wait_send()
        g = slots[0]
        for s in range(1, N_DEV):
            g = g + slots[s]
        delta, nm, nv = _adamw_math(w_ref[...], g, m_ref[...], v_ref[...])
        gs_ref[...] = g
        d_ref[...] = delta
        nm_ref[...] = nm
        nv_ref[...] = nv

    vmem = pl.BlockSpec(memory_space=pltpu.VMEM)
    out = jax.ShapeDtypeStruct((rows, length), F32)
    return pl.pallas_call(
        body, in_specs=[vmem] * 4, out_specs=[vmem] * 4, out_shape=[out] * 4,
        scratch_shapes=[pltpu.VMEM((N_DEV, rows, length), F32), pltpu.SemaphoreType.DMA((N_DEV - 1,)),
                        pltpu.SemaphoreType.DMA((N_DEV - 1,))],
        name="small_allreduce_adamw",
    )(gvec, wvec, mvec, vvec)


_SMALL_SEGMENTS = (("a_log", GDN_HEADS), ("dt_bias", GDN_HEADS), ("gdn_norm_w", HEAD_DIM), ("pool_scale", GDN_WIDTH),
                   ("ln1_g", D_MODEL), ("ln1_b", D_MODEL), ("ln2_g", D_MODEL), ("ln2_b", D_MODEL),
                   ("ln3_g", D_MODEL), ("ln3_b", D_MODEL), ("conv_w", CONV_K * QKV_COLS))
_SMALL_ROWS = 8
_SMALL_LEN = -(-sum(sz for _, sz in _SMALL_SEGMENTS) // (_SMALL_ROWS * LANE)) * LANE


def _pack_small(vals):
    parts = [vals[n].reshape(-1).astype(F32) if n in vals else jnp.zeros((sz,), F32) for n, sz in _SMALL_SEGMENTS]
    flat = jnp.concatenate(parts)
    flat = jnp.pad(flat, (0, _SMALL_ROWS * _SMALL_LEN - flat.shape[0]))
    return flat.reshape(_SMALL_ROWS, _SMALL_LEN)


def _unpack_small(vec):
    flat = vec.reshape(-1)
    out, off = {}, 0
    for n, sz in _SMALL_SEGMENTS:
        out[n] = flat[off:off + sz]
        off += sz
    return out


_WEIGHT_ORDER = ("w_in", "conv_w", "a_log", "dt_bias", "gdn_norm_w", "pool_w", "pool_scale", "w_out", "ln1_g", "ln1_b",
                 "xq_w", "xk_w", "xv_w", "xo_w", "ln2_g", "ln2_b", "w_up", "w_down", "ln3_g", "ln3_b")
_ADAM_ROWS = {"w_in": 256, "pool_w": 128, "w_out": 128, "xq_w": 128, "xk_w": 128, "xv_w": 128, "xo_w": 128,
              "w_up": 128, "w_down": 128}


def _shard2d(name, a):
    return a.reshape(-1, a.shape[-1]) if name == "pool_w" else a


def _gathered_to_full(name, gth):
    if name == "w_up":
        return gth
    if name in ("w_in", "conv_w"):
        return jnp.transpose(gth, (1, 0, 2)).reshape(gth.shape[1], N_DEV * gth.shape[2])
    if name == "pool_w":
        g4 = gth.reshape(N_DEV, POOL_GROUPS, POOL_GROUP_DIM // N_DEV, POOL_GROUP_DIM)
        return jnp.transpose(g4, (1, 0, 2, 3)).reshape(POOL_GROUPS, POOL_GROUP_DIM, POOL_GROUP_DIM)
    return gth.reshape(N_DEV * gth.shape[1], gth.shape[2])


def _full_to_chunks(name, full):
    if name == "w_up":
        return full
    if name == "w_in":
        r, cols = full.shape
        return jnp.transpose(full.reshape(r, N_DEV, cols // N_DEV), (1, 0, 2))
    if name == "pool_w":
        g4 = full.reshape(POOL_GROUPS, N_DEV, POOL_GROUP_DIM // N_DEV, POOL_GROUP_DIM)
        return jnp.transpose(g4, (1, 0, 2, 3)).reshape(N_DEV, POOL_GROUPS * POOL_GROUP_DIM // N_DEV, POOL_GROUP_DIM)
    return full.reshape(N_DEV, full.shape[0] // N_DEV, full.shape[1])


def kernel(x, mem, w_in, conv_w, a_log, dt_bias, gdn_norm_w, pool_w, pool_scale, w_out, ln1_g, ln1_b, xq_w, xk_w, xv_w, xo_w, ln2_g, ln2_b, w_up, w_down, ln3_g, ln3_b, loss_target, m_w_in, m_conv_w, m_a_log, m_dt_bias, m_gdn_norm_w, m_pool_w, m_pool_scale, m_w_out, m_ln1_g, m_ln1_b, m_xq_w, m_xk_w, m_xv_w, m_xo_w, m_ln2_g, m_ln2_b, m_w_up, m_w_down, m_ln3_g, m_ln3_b, v_w_in, v_conv_w, v_a_log, v_dt_bias, v_gdn_norm_w, v_pool_w, v_pool_scale, v_w_out, v_ln1_g, v_ln1_b, v_xq_w, v_xk_w, v_xv_w, v_xo_w, v_ln2_g, v_ln2_b, v_w_up, v_w_down, v_ln3_g, v_ln3_b):
    args = dict(locals())
    wt = {n: args[n][0] for n in _WEIGHT_ORDER}
    mo = {n: args["m_" + n][0] for n in _WEIGHT_ORDER}
    vo = {n: args["v_" + n][0] for n in _WEIGHT_ORDER}

    gather_names = _MATRICES + ("conv_w",)
    shards = [_shard2d(n, wt[n]).astype(BF16 if n in _MATRICES else F32) for n in gather_names]
    gathered = _all_gather(shards, name="all_gather_weights")
    full = {n: _gathered_to_full(n, gth) for n, gth in zip(gather_names, gathered)}
    for n in _VECTORS:
        full[n] = wt[n]
    w = _prepare_weights(full)

    sq, grad_x, g = _local_step(x[0], mem[0], loss_target[0], w)
    grads = _finish_grads(g)

    chunked = [_full_to_chunks(n, grads[n].astype(BF16)) for n in _MATRICES]
    received = _scatter_to_owners(chunked, name="scatter_weight_grads")
    small = {n: grads[n] for n, _ in _SMALL_SEGMENTS}
    gs, ds, ms, vs = _small_allreduce_adamw(
        _pack_small(small), _pack_small({n: wt[n] for n in _VECTORS}), _pack_small({n: mo[n] for n in _VECTORS}),
        _pack_small({n: vo[n] for n in _VECTORS}))
    gs, ds, ms, vs = _unpack_small(gs), _unpack_small(ds), _unpack_small(ms), _unpack_small(vs)

    out = {}
    for n, parts in zip(_MATRICES, received):
        res = _adamw_shard(parts, _shard2d(n, wt[n]), _shard2d(n, mo[n]), _shard2d(n, vo[n]), tr=_ADAM_ROWS[n],
                           name="adamw_" + n)
        out[n] = [r.reshape(args[n].shape) for r in res]
    cols = conv_w.shape[-1]
    me = _slot(*_place())
    conv_full = gs["conv_w"].reshape(CONV_K, QKV_COLS)
    conv_mine = lax.dynamic_slice(conv_full, (0, me * cols), (CONV_K, cols))
    res = _adamw_shard(conv_mine[None], wt["conv_w"], mo["conv_w"], vo["conv_w"], tr=CONV_K, name="adamw_conv_w")
    out["conv_w"] = [r.reshape(conv_w.shape) for r in res]
    for n in _VECTORS:
        out[n] = [t[n].reshape(args[n].shape) for t in (gs, ds, ms, vs)]

    loss = lax.psum(0.5 * sq[0, 0] / D_MODEL, ("x", "y", "c"))
    return (loss, grad_x[None], *[out[n][0] for n in _WEIGHT_ORDER], *[out[n][1] for n in _WEIGHT_ORDER],
            *[out[n][2] for n in _WEIGHT_ORDER], *[out[n][3] for n in _WEIGHT_ORDER])
```

```python
import functools
import math

import jax
import jax.numpy as jnp
from jax import lax
from jax.experimental import pallas as pl
from jax.experimental.pallas import tpu as pltpu

F32 = jnp.float32
BF16 = jnp.bfloat16
MESH = pl.DeviceIdType.MESH

N_DEV = 8
D_MODEL = 2048
GDN_WIDTH = 1024
GDN_HEADS = 8
HEAD_DIM = 128
CONV_K = 4
CHUNK = 64
POOL_GROUPS = 4
POOL_GROUP_DIM = 256
MEM_LEN = 256
XATTN_HEADS = 4
XATTN_HEAD_DIM = 512
D_FF = 8192
IN_COLS = 5136
ALPHA = 2.0 ** 0.25
LN_EPS = 1e-5
NORM_EPS = 1e-6

LANE = 128
QKV_COLS = 3 * GDN_WIDTH
Z_OFF = QKV_COLS
BA_OFF = 4 * GDN_WIDTH
POOL_OFF = BA_OFF + 2 * LANE
PROJ_COLS = POOL_OFF + GDN_WIDTH
Z_BLK = Z_OFF // LANE
BA_BLK = BA_OFF // LANE
POOL_BLK = POOL_OFF // POOL_GROUP_DIM

ADAM_LR = 0.001
ADAM_B1 = 0.9
ADAM_B2 = 0.999
ADAM_EPS = 1e-08
ADAM_WD = 0.01
ADAM_STEP = 10

VMEM_LIMIT_BYTES = 48 * 1024 * 1024


def _params(*sem):
    return pltpu.CompilerParams(dimension_semantics=sem if sem else None, vmem_limit_bytes=VMEM_LIMIT_BYTES)


def _make_dots(cast, precision):
    def dg(a, b, ca, cb):
        if cast is not None:
            a = a.astype(cast)
            b = b.astype(cast)
        return lax.dot_general(a, b, (((ca,), (cb,)), ((), ())), precision=precision, preferred_element_type=F32)

    def nn_(a, b):
        return dg(a, b, 1, 0)

    def nt_(a, b):
        return dg(a, b, 1, 1)

    def tn_(a, b):
        return dg(a, b, 0, 0)

    @jax.custom_vjp
    def nn(a, b):
        return nn_(a, b)

    nn.defvjp(lambda a, b: (nn_(a, b), (a, b)), lambda r, g: (nt_(g, r[1]), tn_(r[0], g)))

    @jax.custom_vjp
    def nt(a, b):
        return nt_(a, b)

    nt.defvjp(lambda a, b: (nt_(a, b), (a, b)), lambda r, g: (nn_(g, r[1]), tn_(g, r[0])))

    @jax.custom_vjp
    def tn(a, b):
        return tn_(a, b)

    tn.defvjp(lambda a, b: (tn_(a, b), (a, b)), lambda r, g: (nt_(r[1], g), nn_(r[0], g)))

    return (nn_, nt_, tn_), (nn, nt, tn)


_BDOT_PLAIN, _BDOT_VJP = _make_dots(BF16, None)
_FDOT_PLAIN, _FDOT_VJP = _make_dots(None, lax.Precision.HIGHEST)


def _mm(a, b, *, ta=False, tb=False, out_dtype=F32, tm=None, tn=512, tk=None, epi=None, extra=None, add_scale=1.0,
        b_chunks=False, o_chunks=False, name):
    m, k = (a.shape[1], a.shape[0]) if ta else a.shape
    if b_chunks:
        n, kb = (b.shape[1], N_DEV * b.shape[2]) if tb else (N_DEV * b.shape[2], b.shape[1])
    else:
        n, kb = b.shape if tb else (b.shape[1], b.shape[0])
    assert kb == k, (name, a.shape, b.shape)
    tm, tn, tk = min(tm or m, m), min(tn, n), min(tk or k, k)
    assert m % tm == 0 and n % tn == 0 and k % tk == 0, (name, m, n, k)
    nk = k // tk
    dims = (((0 if ta else 1,), (1 if tb else 0,)), ((), ()))
    n_extra = 0 if epi in (None, "relu2") else 1
    n_out = 2 if epi == "relu2" else 1
    if epi in ("relu2", "mul2r"):
        out_dtype = BF16

    def body(*refs):
        a_ref, b_ref = refs[:2]
        c_ref = refs[2] if n_extra else None
        o_refs = refs[2 + n_extra:2 + n_extra + n_out]
        scr = refs[2 + n_extra + n_out:]
        r = lax.dot_general(a_ref[...].astype(BF16), b_ref[...].astype(BF16), dims, preferred_element_type=F32)

        def finish(v):
            if epi == "add":
                o_refs[0][...] = (v + add_scale * c_ref[...]).astype(out_dtype)
            elif epi == "relu2":
                p = jnp.maximum(v, 0.0)
                o_refs[0][...] = (p * p).astype(BF16)
                o_refs[1][...] = p.astype(BF16)
            elif epi == "mul2r":
                o_refs[0][...] = (v * (2.0 * c_ref[...].astype(F32))).astype(BF16)
            else:
                o_refs[0][...] = v.astype(out_dtype)

        if nk == 1:
            finish(r)
        else:
            acc = scr[0]
            kk = pl.program_id(2)

            @pl.when(kk == 0)
            def _():
                acc[...] = r

            @pl.when(kk > 0)
            def _():
                acc[...] += r

            @pl.when(kk == nk - 1)
            def _():
                finish(acc[...])

    a_spec = pl.BlockSpec((tk, tm), lambda i, j, kk: (kk, i)) if ta else pl.BlockSpec((tm, tk), lambda i, j, kk: (i, kk))
    if b_chunks and tb:
        kc = k // N_DEV // tk
        b_spec = pl.BlockSpec((None, tn, tk), lambda i, j, kk: (kk // kc, j, kk % kc))
    elif b_chunks:
        nc = n // N_DEV // tn
        b_spec = pl.BlockSpec((None, tk, tn), lambda i, j, kk: (j // nc, kk, j % nc))
    elif tb:
        b_spec = pl.BlockSpec((tn, tk), lambda i, j, kk: (j, kk))
    else:
        b_spec = pl.BlockSpec((tk, tn), lambda i, j, kk: (kk, j))
    mn_spec = pl.BlockSpec((tm, tn), lambda i, j, kk: (i, j))
    if o_chunks:
        oc = n // N_DEV // tn
        o_spec = pl.BlockSpec((None, tm, tn), lambda i, j, kk: (j // oc, i, j % oc))
        o_shape = jax.ShapeDtypeStruct((N_DEV, m, n // N_DEV), out_dtype)
    else:
        o_spec, o_shape = mn_spec, jax.ShapeDtypeStruct((m, n), out_dtype)
    res = pl.pallas_call(
        body, grid=(m // tm, n // tn, nk), in_specs=[a_spec, b_spec] + [mn_spec] * n_extra,
        out_specs=[o_spec] * n_out, out_shape=[o_shape] * n_out,
        scratch_shapes=[pltpu.VMEM((tm, tn), F32)] if nk > 1 else [],
        compiler_params=_params("parallel", "parallel", "arbitrary"), name=name,
    )(a, b, *([extra] if n_extra else []))
    return res if n_out > 1 else res[0]


def _cast_bf16(v, *, name, tm=512):
    t, d = v.shape
    tm = min(tm, t)

    def body(v_ref, o_ref):
        o_ref[...] = v_ref[...].astype(BF16)

    spec = pl.BlockSpec((tm, d), lambda i: (i, 0))
    return pl.pallas_call(body, grid=(t // tm,), in_specs=[spec], out_specs=spec,
                          out_shape=jax.ShapeDtypeStruct((t, d), BF16), compiler_params=_params("parallel"), name=name)(v)


def _shift_down(v, s):
    if s == 0:
        return v
    row = lax.broadcasted_iota(jnp.int32, v.shape, 0)
    return jnp.where(row >= s, pltpu.roll(v, s, axis=0), 0.0)


def _shift_up(v, s):
    if s == 0:
        return v
    t = v.shape[0]
    row = lax.broadcasted_iota(jnp.int32, v.shape, 0)
    return jnp.where(row < t - s, pltpu.roll(v, t - s, axis=0), 0.0)


def _post_col(j):
    return (j % GDN_HEADS) * 3 + j // GDN_HEADS


def _gdn_prep_fwd(proj, conv_w):
    t = proj.shape[0]

    def body(x_ref, w_ref, o_ref):
        j = pl.program_id(0)
        x = x_ref[...]
        y = jnp.zeros_like(x)
        for tap in range(CONV_K):
            y = y + w_ref[tap:tap + 1, :] * _shift_down(x, CONV_K - 1 - tap)
        c = y * jax.nn.sigmoid(y)
        nrm = c * lax.rsqrt(jnp.sum(c * c, axis=1, keepdims=True) + NORM_EPS)
        o_ref[...] = jnp.where(j < 2 * GDN_HEADS, nrm, c)

    return pl.pallas_call(
        body, grid=(QKV_COLS // LANE,),
        in_specs=[pl.BlockSpec((t, LANE), lambda j: (0, j)), pl.BlockSpec((CONV_K, LANE), lambda j: (0, j))],
        out_specs=pl.BlockSpec((t, LANE), lambda j: (0, _post_col(j))),
        out_shape=jax.ShapeDtypeStruct((t, QKV_COLS), F32),
        compiler_params=_params("parallel"), name="gdn_prep_fwd",
    )(proj, conv_w)


def _gdn_prep_bwd(proj, conv_w, dpost, dproj):
    t = proj.shape[0]

    def body(x_ref, w_ref, d_ref, _, dx_ref, dw_ref):
        j = pl.program_id(0)
        x = x_ref[...]
        xs = [_shift_down(x, CONV_K - 1 - tap) for tap in range(CONV_K)]
        y = jnp.zeros_like(x)
        for tap in range(CONV_K):
            y = y + w_ref[tap:tap + 1, :] * xs[tap]
        sig = jax.nn.sigmoid(y)
        c = y * sig
        r = lax.rsqrt(jnp.sum(c * c, axis=1, keepdims=True) + NORM_EPS)
        nrm = c * r
        d = d_ref[...]
        dc_norm = r * (d - nrm * jnp.sum(d * nrm, axis=1, keepdims=True))
        dc = jnp.where(j < 2 * GDN_HEADS, dc_norm, d)
        dy = dc * (sig * (1.0 + y * (1.0 - sig)))
        dx = jnp.zeros_like(x)
        for tap in range(CONV_K):
            dx = dx + _shift_up(w_ref[tap:tap + 1, :] * dy, CONV_K - 1 - tap)
            dw_ref[tap:tap + 1, :] = jnp.sum(dy * xs[tap], axis=0, keepdims=True)
        dx_ref[...] = dx.astype(dx_ref.dtype)

    return pl.pallas_call(
        body, grid=(QKV_COLS // LANE,),
        in_specs=[pl.BlockSpec((t, LANE), lambda j: (0, j)), pl.BlockSpec((CONV_K, LANE), lambda j: (0, j)),
                  pl.BlockSpec((t, LANE), lambda j: (0, _post_col(j))), pl.BlockSpec(memory_space=pl.ANY)],
        out_specs=[pl.BlockSpec((t, LANE), lambda j: (0, j)), pl.BlockSpec((CONV_K, LANE), lambda j: (0, j))],
        out_shape=[jax.ShapeDtypeStruct(dproj.shape, dproj.dtype), jax.ShapeDtypeStruct((CONV_K, QKV_COLS), F32)],
        input_output_aliases={3: 0},
        compiler_params=_params("parallel"), name="gdn_prep_bwd",
    )(proj, conv_w, dpost, dproj)


def _softplus(v):
    return jnp.maximum(v, 0.0) + jnp.log(1.0 + jnp.exp(-jnp.abs(v)))


def _tri_inv(low, nn):
    r = lax.broadcasted_iota(jnp.int32, (CHUNK, CHUNK), 0)
    c = lax.broadcasted_iota(jnp.int32, (CHUNK, CHUNK), 1)
    eye = (r == c).astype(F32)
    same_blk = lax.shift_right_logical(r, 4) == lax.shift_right_logical(c, 4)
    diag = jnp.where(same_blk, low, 0.0)
    off = low - diag
    n1 = -diag
    n2 = nn(n1, n1)
    n4 = nn(n2, n2)
    n8 = nn(n4, n4)
    inv_d = nn(nn(nn(eye + n1, eye + n2), eye + n4), eye + n8)
    m1 = nn(inv_d, off)
    m2 = nn(m1, m1)
    return nn(nn(eye - m1, eye + m2), inv_d)


def _chunk_fn(qkv, ba, alog_row, dtb_row, state, h, bdots, fdots):
    nn, nt, tn = bdots
    fnn = fdots[0]
    q = qkv[:, 0:HEAD_DIM] * (HEAD_DIM ** -0.5)
    k = qkv[:, HEAD_DIM:2 * HEAD_DIM]
    v = qkv[:, 2 * HEAD_DIM:3 * HEAD_DIM]
    lane = lax.broadcasted_iota(jnp.int32, ba.shape, 1)
    bg = jnp.where(lane < GDN_HEADS, jax.nn.sigmoid(ba), -jnp.exp(alog_row) * _softplus(ba + dtb_row))
    beta = jnp.sum(jnp.where(lane == h, bg, 0.0), axis=1, keepdims=True)
    g = jnp.sum(jnp.where(lane == h + GDN_HEADS, bg, 0.0), axis=1, keepdims=True)

    r = lax.broadcasted_iota(jnp.int32, (CHUNK, CHUNK), 0)
    c = lax.broadcasted_iota(jnp.int32, (CHUNK, CHUNK), 1)
    incl = r >= c
    strict = r > c
    eye = r == c

    def to_row(col):
        return jnp.sum(jnp.where(eye, col, 0.0), axis=0, keepdims=True)

    gc = jnp.sum(jnp.where(incl, to_row(g), 0.0), axis=1, keepdims=True)
    diff = gc - to_row(gc)
    decay = jnp.where(incl, jnp.exp(jnp.where(incl, diff, 0.0)), 0.0)
    k_beta = k * beta
    v_beta = v * beta
    low = jnp.where(strict, nt(k_beta, k) * decay, 0.0)
    t_inv = _tri_inv(low, fnn)
    eg = jnp.exp(gc)
    u = fnn(t_inv, v_beta)
    w = fnn(t_inv, k_beta * eg)
    attn = jnp.where(incl, nt(q, k) * decay, 0.0)
    v_new = u - nn(w, state)
    o = nn(q * eg, state) + nn(attn, v_new)
    last = lax.broadcasted_iota(jnp.int32, (CHUNK, 1), 0) == CHUNK - 1
    g_last = jnp.sum(jnp.where(last, gc, 0.0), axis=0, keepdims=True)
    k_dec = k * jnp.exp(g_last - gc)
    new_state = state * jnp.exp(g_last) + tn(k_dec, v_new)
    return o, new_state


def _gdn_scan_fwd(post, proj, alog_row, dtb_row):
    t = post.shape[0]
    n_chunks = t // CHUNK

    def body(qkv_ref, ba_ref, al_ref, dt_ref, o_ref, save_ref, state_ref):
        n = pl.program_id(0)
        h = pl.program_id(1)

        @pl.when(n == 0)
        def _():
            state_ref[h] = jnp.zeros((HEAD_DIM, HEAD_DIM), F32)

        state = state_ref[h]
        save_ref[0, 0] = state
        o, new_state = _chunk_fn(qkv_ref[...], ba_ref[...], al_ref[...], dt_ref[...], state, h, _BDOT_PLAIN, _FDOT_PLAIN)
        o_ref[...] = o
        state_ref[h] = new_state

    return pl.pallas_call(
        body, grid=(n_chunks, GDN_HEADS),
        in_specs=[pl.BlockSpec((CHUNK, 3 * HEAD_DIM), lambda n, h: (n, h)),
                  pl.BlockSpec((CHUNK, LANE), lambda n, h: (n, BA_BLK)),
                  pl.BlockSpec((1, LANE), lambda n, h: (0, 0)), pl.BlockSpec((1, LANE), lambda n, h: (0, 0))],
        out_specs=[pl.BlockSpec((CHUNK, HEAD_DIM), lambda n, h: (n, h)),
                   pl.BlockSpec((1, 1, HEAD_DIM, HEAD_DIM), lambda n, h: (h, n, 0, 0))],
        out_shape=[jax.ShapeDtypeStruct((t, GDN_WIDTH), F32),
                   jax.ShapeDtypeStruct((GDN_HEADS, n_chunks, HEAD_DIM, HEAD_DIM), F32)],
        scratch_shapes=[pltpu.VMEM((GDN_HEADS, HEAD_DIM, HEAD_DIM), F32)],
        compiler_params=_params("arbitrary", "arbitrary"), name="gdn_scan_fwd",
    )(post, proj, alog_row, dtb_row)


def _gdn_scan_bwd(post, proj, alog_row, dtb_row, saved, do, dproj):
    t = post.shape[0]
    n_chunks = t // CHUNK
    last = n_chunks - 1

    def body(qkv_ref, ba_ref, al_ref, dt_ref, save_ref, do_ref, _, dqkv_ref, dba_ref, dal_ref, ddt_ref, dstate_ref, dba_acc):
        n = pl.program_id(0)
        h = pl.program_id(1)

        @pl.when(n == 0)
        def _():
            dstate_ref[h] = jnp.zeros((HEAD_DIM, HEAD_DIM), F32)

        @pl.when((n == 0) & (h == 0))
        def _():
            dal_ref[...] = jnp.zeros_like(dal_ref)
            ddt_ref[...] = jnp.zeros_like(ddt_ref)

        def f(qkv, ba, al, dt, state):
            return _chunk_fn(qkv, ba, al, dt, state, h, _BDOT_VJP, _FDOT_VJP)

        _, vjp = jax.vjp(f, qkv_ref[...], ba_ref[...], al_ref[...], dt_ref[...], save_ref[0, 0])
        dqkv, dba, dal, ddt, dstate = vjp((do_ref[...], dstate_ref[h]))
        dqkv_ref[...] = dqkv

        @pl.when(h == 0)
        def _():
            dba_acc[...] = dba

        @pl.when(h > 0)
        def _():
            dba_acc[...] += dba

        @pl.when(h == GDN_HEADS - 1)
        def _():
            dba_ref[:, 0:LANE] = dba_acc[...].astype(dba_ref.dtype)
            dba_ref[:, LANE:2 * LANE] = jnp.zeros((CHUNK, LANE), dba_ref.dtype)

        dal_ref[...] += dal
        ddt_ref[...] += ddt
        dstate_ref[h] = dstate

    return pl.pallas_call(
        body, grid=(n_chunks, GDN_HEADS),
        in_specs=[pl.BlockSpec((CHUNK, 3 * HEAD_DIM), lambda n, h: (last - n, h)),
                  pl.BlockSpec((CHUNK, LANE), lambda n, h: (last - n, BA_BLK)),
                  pl.BlockSpec((1, LANE), lambda n, h: (0, 0)), pl.BlockSpec((1, LANE), lambda n, h: (0, 0)),
                  pl.BlockSpec((1, 1, HEAD_DIM, HEAD_DIM), lambda n, h: (h, last - n, 0, 0)),
                  pl.BlockSpec((CHUNK, HEAD_DIM), lambda n, h: (last - n, h)),
                  pl.BlockSpec(memory_space=pl.ANY)],
        out_specs=[pl.BlockSpec((CHUNK, 3 * HEAD_DIM), lambda n, h: (last - n, h)),
                   pl.BlockSpec((CHUNK, 2 * LANE), lambda n, h: (last - n, BA_BLK // 2)),
                   pl.BlockSpec((1, LANE), lambda n, h: (0, 0)), pl.BlockSpec((1, LANE), lambda n, h: (0, 0))],
        out_shape=[jax.ShapeDtypeStruct((t, QKV_COLS), F32), jax.ShapeDtypeStruct(dproj.shape, dproj.dtype),
                   jax.ShapeDtypeStruct((1, LANE), F32), jax.ShapeDtypeStruct((1, LANE), F32)],
        input_output_aliases={6: 1},
        scratch_shapes=[pltpu.VMEM((GDN_HEADS, HEAD_DIM, HEAD_DIM), F32), pltpu.VMEM((CHUNK, LANE), F32)],
        compiler_params=_params("arbitrary", "arbitrary"), name="gdn_scan_bwd",
    )(post, proj, alog_row, dtb_row, saved, do, dproj)


def _onorm_fn(o, z, w):
    return o * lax.rsqrt(jnp.mean(o * o, axis=1, keepdims=True) + NORM_EPS) * w * (z * jax.nn.sigmoid(z))


def _onorm_fwd(o_raw, proj, norm_w, mixin, tm=512):
    t = o_raw.shape[0]
    tm = min(tm, t)

    def body(o_ref, z_ref, w_ref, _, out_ref):
        out_ref[...] = _onorm_fn(o_ref[...], z_ref[...], w_ref[...]).astype(out_ref.dtype)

    return pl.pallas_call(
        body, grid=(t // tm, GDN_HEADS),
        in_specs=[pl.BlockSpec((tm, LANE), lambda i, h: (i, h)), pl.BlockSpec((tm, LANE), lambda i, h: (i, Z_BLK + h)),
                  pl.BlockSpec((1, LANE), lambda i, h: (0, 0)), pl.BlockSpec(memory_space=pl.ANY)],
        out_specs=pl.BlockSpec((tm, LANE), lambda i, h: (i, h)),
        out_shape=jax.ShapeDtypeStruct(mixin.shape, mixin.dtype), input_output_aliases={3: 0},
        compiler_params=_params("parallel", "parallel"), name="gdn_onorm_fwd",
    )(o_raw, proj, norm_w, mixin)


def _onorm_bwd(o_raw, proj, norm_w, dmixin, dproj, tm=512):
    t = o_raw.shape[0]
    tm = min(tm, t)

    def body(o_ref, z_ref, w_ref, d_ref, _, do_ref, dz_ref, dw_ref):
        @pl.when((pl.program_id(0) == 0) & (pl.program_id(1) == 0))
        def _():
            dw_ref[...] = jnp.zeros_like(dw_ref)

        _, vjp = jax.vjp(_onorm_fn, o_ref[...], z_ref[...], w_ref[...])
        do, dz, dw = vjp(d_ref[...])
        do_ref[...] = do
        dz_ref[...] = dz.astype(dz_ref.dtype)
        dw_ref[...] += dw

    return pl.pallas_call(
        body, grid=(t // tm, GDN_HEADS),
        in_specs=[pl.BlockSpec((tm, LANE), lambda i, h: (i, h)), pl.BlockSpec((tm, LANE), lambda i, h: (i, Z_BLK + h)),
                  pl.BlockSpec((1, LANE), lambda i, h: (0, 0)), pl.BlockSpec((tm, LANE), lambda i, h: (i, h)),
                  pl.BlockSpec(memory_space=pl.ANY)],
        out_specs=[pl.BlockSpec((tm, LANE), lambda i, h: (i, h)), pl.BlockSpec((tm, LANE), lambda i, h: (i, Z_BLK + h)),
                   pl.BlockSpec((1, LANE), lambda i, h: (0, 0))],
        out_shape=[jax.ShapeDtypeStruct((t, GDN_WIDTH), F32), jax.ShapeDtypeStruct(dproj.shape, dproj.dtype),
                   jax.ShapeDtypeStruct((1, LANE), F32)],
        input_output_aliases={4: 1},
        compiler_params=_params("arbitrary", "arbitrary"), name="gdn_onorm_bwd",
    )(o_raw, proj, norm_w, dmixin, dproj)


def _pool_select(levels, gi):
    out = levels[-1]
    for lvl in range(len(levels) - 2, -1, -1):
        out = jnp.where(gi == lvl, levels[lvl], out)
    return out


def _pool_count(shape, gi):
    pos = lax.broadcasted_iota(jnp.int32, shape, 0)
    win = lax.shift_left(jnp.int32(2), gi)
    return jnp.minimum(pos + 1, win).astype(F32)


def _pooled(p, gi):
    acc = p
    levels = []
    for lvl in range(POOL_GROUPS):
        acc = acc + _shift_down(acc, 1 << lvl)
        levels.append(acc)
    return _pool_select(levels, gi) / _pool_count(p.shape, gi) - p


def _pool_fwd(proj, pool_w, pool_scale):
    t = proj.shape[0]

    def body(p_ref, w_ref, s_ref, out_ref):
        gi = pl.program_id(0)
        pooled = _pooled(p_ref[...], gi)
        out_ref[...] = (_BDOT_PLAIN[0](pooled, w_ref[0]) * s_ref[0]).astype(out_ref.dtype)

    return pl.pallas_call(
        body, grid=(POOL_GROUPS,),
        in_specs=[pl.BlockSpec((t, POOL_GROUP_DIM), lambda g: (0, POOL_BLK + g)),
                  pl.BlockSpec((1, POOL_GROUP_DIM, POOL_GROUP_DIM), lambda g: (g, 0, 0)),
                  pl.BlockSpec((1, 1, POOL_GROUP_DIM), lambda g: (g, 0, 0))],
        out_specs=pl.BlockSpec((t, POOL_GROUP_DIM), lambda g: (0, GDN_WIDTH // POOL_GROUP_DIM + g)),
        out_shape=jax.ShapeDtypeStruct((t, 2 * GDN_WIDTH), BF16),
        compiler_params=_params("parallel"), name="pool_fwd",
    )(proj, pool_w, pool_scale)


def _pool_bwd(proj, pool_w, pool_scale, dmixin):
    t = proj.shape[0]
    nn, nt, tn = _BDOT_PLAIN

    def body(p_ref, w_ref, s_ref, d_ref, dp_ref, dw_ref, ds_ref):
        gi = pl.program_id(0)
        p = p_ref[...]
        pooled = _pooled(p, gi)
        mixed = nn(pooled, w_ref[0])
        d = d_ref[...]
        ds_ref[0] = jnp.sum(d * mixed, axis=0, keepdims=True)
        dmixed = d * s_ref[0]
        dw_ref[0] = tn(pooled, dmixed)
        dpooled = nt(dmixed, w_ref[0])
        acc = dpooled / _pool_count(p.shape, gi)
        levels = []
        for lvl in range(POOL_GROUPS):
            acc = acc + _shift_up(acc, 1 << lvl)
            levels.append(acc)
        dp_ref[...] = (_pool_select(levels, gi) - dpooled).astype(dp_ref.dtype)

    return pl.pallas_call(
        body, grid=(POOL_GROUPS,),
        in_specs=[pl.BlockSpec((t, POOL_GROUP_DIM), lambda g: (0, POOL_BLK + g)),
                  pl.BlockSpec((1, POOL_GROUP_DIM, POOL_GROUP_DIM), lambda g: (g, 0, 0)),
                  pl.BlockSpec((1, 1, POOL_GROUP_DIM), lambda g: (g, 0, 0)),
                  pl.BlockSpec((t, POOL_GROUP_DIM), lambda g: (0, GDN_WIDTH // POOL_GROUP_DIM + g))],
        out_specs=[pl.BlockSpec((t, POOL_GROUP_DIM), lambda g: (0, POOL_BLK + g)),
                   pl.BlockSpec((1, POOL_GROUP_DIM, POOL_GROUP_DIM), lambda g: (g, 0, 0)),
                   pl.BlockSpec((1, 1, POOL_GROUP_DIM), lambda g: (g, 0, 0))],
        out_shape=[jax.ShapeDtypeStruct((t, PROJ_COLS), BF16),
                   jax.ShapeDtypeStruct((POOL_GROUPS, POOL_GROUP_DIM, POOL_GROUP_DIM), F32),
                   jax.ShapeDtypeStruct((POOL_GROUPS, 1, POOL_GROUP_DIM), F32)],
        compiler_params=_params("parallel"), name="pool_bwd",
    )(proj, pool_w, pool_scale, dmixin)


def _ln_stats(s):
    mu = jnp.mean(s, axis=1, keepdims=True)
    xc = s - mu
    var = jnp.mean(xc * xc, axis=1, keepdims=True)
    rstd = lax.rsqrt(var + LN_EPS)
    return xc * rstd, rstd


def _ln_fwd(h_in, y, g, b, *, name, tm=256):
    t, d = h_in.shape
    tm = min(tm, t)

    def body(h_ref, y_ref, g_ref, b_ref, o_ref, o16_ref):
        xhat, _ = _ln_stats(ALPHA * h_ref[...] + y_ref[...])
        out = xhat * g_ref[...] + b_ref[...]
        o_ref[...] = out
        o16_ref[...] = out.astype(BF16)

    row = pl.BlockSpec((tm, d), lambda i: (i, 0))
    vec = pl.BlockSpec((1, d), lambda i: (0, 0))
    return pl.pallas_call(
        body, grid=(t // tm,), in_specs=[row, row, vec, vec], out_specs=[row, row],
        out_shape=[jax.ShapeDtypeStruct((t, d), F32), jax.ShapeDtypeStruct((t, d), BF16)],
        compiler_params=_params("parallel"), name=name,
    )(h_in, y, g, b)


def _ln_loss_fwd(h_in, y, g, b, target, *, name, tm=256):
    t, d = h_in.shape
    tm = min(tm, t)

    def body(h_ref, y_ref, g_ref, b_ref, t_ref, dy_ref, sq_ref):
        @pl.when(pl.program_id(0) == 0)
        def _():
            sq_ref[...] = jnp.zeros_like(sq_ref)

        xhat, _ = _ln_stats(ALPHA * h_ref[...] + y_ref[...])
        err = xhat * g_ref[...] + b_ref[...] - t_ref[...]
        dy_ref[...] = err * (1.0 / d)
        sq_ref[...] += jnp.sum(jnp.sum(err * err, axis=1, keepdims=True), axis=0, keepdims=True)

    row = pl.BlockSpec((tm, d), lambda i: (i, 0))
    vec = pl.BlockSpec((1, d), lambda i: (0, 0))
    return pl.pallas_call(
        body, grid=(t // tm,), in_specs=[row, row, vec, vec, row],
        out_specs=[row, pl.BlockSpec((1, LANE), lambda i: (0, 0))],
        out_shape=[jax.ShapeDtypeStruct((t, d), F32), jax.ShapeDtypeStruct((1, LANE), F32)],
        compiler_params=_params("arbitrary"), name=name,
    )(h_in, y, g, b, target)


def _ln_bwd(h_in, y, g, d_a, d_b, *, name, tm=256):
    t, d = h_in.shape
    tm = min(tm, t)
    has_b = d_b is not None

    def body(*refs):
        if has_b:
            h_ref, y_ref, g_ref, da_ref, db_ref, ds_ref, ds16_ref, dg_ref, dbias_ref = refs
        else:
            h_ref, y_ref, g_ref, da_ref, ds_ref, ds16_ref, dg_ref, dbias_ref = refs

        @pl.when(pl.program_id(0) == 0)
        def _():
            dg_ref[...] = jnp.zeros_like(dg_ref)
            dbias_ref[...] = jnp.zeros_like(dbias_ref)

        xhat, rstd = _ln_stats(ALPHA * h_ref[...] + y_ref[...])
        dout = da_ref[...]
        if has_b:
            dout = dout + ALPHA * db_ref[...]
        dxhat = dout * g_ref[...]
        m1 = jnp.mean(dxhat, axis=1, keepdims=True)
        m2 = jnp.mean(dxhat * xhat, axis=1, keepdims=True)
        ds = rstd * (dxhat - m1 - xhat * m2)
        ds_ref[...] = ds
        ds16_ref[...] = ds.astype(BF16)
        dg_ref[...] += jnp.sum(dout * xhat, axis=0, keepdims=True)
        dbias_ref[...] += jnp.sum(dout, axis=0, keepdims=True)

    row = pl.BlockSpec((tm, d), lambda i: (i, 0))
    vec = pl.BlockSpec((1, d), lambda i: (0, 0))
    args = [h_in, y, g, d_a] + ([d_b] if has_b else [])
    return pl.pallas_call(
        body, grid=(t // tm,), in_specs=[row, row, vec, row] + ([row] if has_b else []),
        out_specs=[row, row, vec, vec],
        out_shape=[jax.ShapeDtypeStruct((t, d), F32), jax.ShapeDtypeStruct((t, d), BF16),
                   jax.ShapeDtypeStruct((1, d), F32), jax.ShapeDtypeStruct((1, d), F32)],
        compiler_params=_params("arbitrary"), name=name,
    )(*args)


def _attn_fn(q, k, v, dots):
    nn, nt, _ = dots
    s = nt(q, k) * (XATTN_HEAD_DIM ** -0.5)
    s = s - lax.stop_gradient(jnp.max(s, axis=1, keepdims=True))
    e = jnp.exp(s)
    p = e / jnp.sum(e, axis=1, keepdims=True)
    return nn(p, v)


def _attn_fwd(q, k, v, tq=512):
    t = q.shape[0]
    tq = min(tq, t)

    def body(q_ref, k_ref, v_ref, o_ref):
        o_ref[...] = _attn_fn(q_ref[...], k_ref[...], v_ref[...], _BDOT_PLAIN).astype(BF16)

    qs = pl.BlockSpec((tq, XATTN_HEAD_DIM), lambda h, i: (i, h))
    ks = pl.BlockSpec((MEM_LEN, XATTN_HEAD_DIM), lambda h, i: (0, h))
    return pl.pallas_call(
        body, grid=(XATTN_HEADS, t // tq), in_specs=[qs, ks, ks], out_specs=qs,
        out_shape=jax.ShapeDtypeStruct(q.shape, BF16), compiler_params=_params("parallel", "parallel"), name="xattn_fwd",
    )(q, k, v)


def _attn_bwd(q, k, v, do, tq=512):
    t = q.shape[0]
    tq = min(tq, t)

    def body(q_ref, k_ref, v_ref, do_ref, dq_ref, dk_ref, dv_ref):
        @pl.when(pl.program_id(1) == 0)
        def _():
            dk_ref[...] = jnp.zeros_like(dk_ref)
            dv_ref[...] = jnp.zeros_like(dv_ref)

        _, vjp = jax.vjp(lambda a, b, c: _attn_fn(a, b, c, _BDOT_VJP), q_ref[...].astype(F32), k_ref[...].astype(F32),
                         v_ref[...].astype(F32))
        dq, dk, dv = vjp(do_ref[...].astype(F32))
        dq_ref[...] = dq.astype(BF16)
        dk_ref[...] += dk
        dv_ref[...] += dv

    qs = pl.BlockSpec((tq, XATTN_HEAD_DIM), lambda h, i: (i, h))
    ks = pl.BlockSpec((MEM_LEN, XATTN_HEAD_DIM), lambda h, i: (0, h))
    return pl.pallas_call(
        body, grid=(XATTN_HEADS, t // tq), in_specs=[qs, ks, ks, qs], out_specs=[qs, ks, ks],
        out_shape=[jax.ShapeDtypeStruct(q.shape, BF16), jax.ShapeDtypeStruct(k.shape, F32), jax.ShapeDtypeStruct(v.shape, F32)],
        compiler_params=_params("parallel", "arbitrary"), name="xattn_bwd",
    )(q, k, v, do)


def _local_step(x, mem, target, weights_of, grads_ready):
    def behind(vec, token):
        return vec if token is None else vec + token

    x16 = _cast_bf16(x, name="cast_x")
    w = dict(weights_of("mixer", None))
    proj = _mm(x16, w["w_in"], tn=768, name="mm_in_proj")
    post = _gdn_prep_fwd(proj, w["conv_w"])
    o_raw, saved = _gdn_scan_fwd(post, proj, w["alog_row"], w["dtb_row"])
    mixin = _pool_fwd(proj, w["pool_w"], w["pool_scale"])
    mixin = _onorm_fwd(o_raw, proj, w["gdn_norm_w"], mixin)
    w.update(weights_of("attn", mixin))
    mix = _mm(mixin, w["w_out"], name="mm_out_proj")
    h1, h1_16 = _ln_fwd(x, mix, w["ln1_g"], w["ln1_b"], name="ln1_fwd")
    xq = _mm(h1_16, w["xq_w"], out_dtype=BF16, name="mm_xq")
    xk = _mm(mem, w["xk_w"], out_dtype=BF16, name="mm_xk")
    xv = _mm(mem, w["xv_w"], out_dtype=BF16, name="mm_xv")
    xo = _attn_fwd(xq, xk, xv)
    xa = _mm(xo, w["xo_w"], name="mm_xo")
    h2, h2_16 = _ln_fwd(h1, xa, w["ln2_g"], w["ln2_b"], name="ln2_fwd")
    w.update(weights_of("mlp", h2_16))
    act, relu = _mm(h2_16, w["w_up"], b_chunks=True, epi="relu2", name="mm_up")
    ff = _mm(act, w["w_down"], tn=1024, tk=512, name="mm_down")
    dy, sq = _ln_loss_fwd(h2, ff, w["ln3_g"], w["ln3_b"], target, name="ln3_loss_fwd")

    g = {}
    ds3, ds3_16, g["ln3_g"], g["ln3_b"] = _ln_bwd(h2, ff, w["ln3_g"], dy, None, name="ln3_bwd")
    gw_down = _mm(act, ds3_16, ta=True, out_dtype=BF16, tm=512, tn=D_MODEL, name="mm_gw_down")
    du = _mm(ds3_16, w["w_down"], tb=True, epi="mul2r", extra=relu, name="mm_du")
    gw_up = _mm(h2_16, du, ta=True, out_dtype=BF16, o_chunks=True, name="mm_gw_up")
    token = grads_ready("mlp", {"w_down": gw_down, "w_up": gw_up})
    dh2 = _mm(du, w["w_up"], tb=True, b_chunks=True, tn=1024, tk=512, name="mm_dh2")
    ds2, ds2_16, g["ln2_g"], g["ln2_b"] = _ln_bwd(h1, xa, behind(w["ln2_g"], token), dh2, ds3, name="ln2_bwd")
    gw_xo = _mm(xo, ds2_16, ta=True, out_dtype=BF16, name="mm_gw_xo")
    dxo = _mm(ds2_16, w["xo_w"], tb=True, out_dtype=BF16, name="mm_dxo")
    dxq, dxk, dxv = _attn_bwd(xq, xk, xv, dxo)
    gw_xq = _mm(h1_16, dxq, ta=True, out_dtype=BF16, name="mm_gw_xq")
    gw_xk = _mm(mem, dxk, ta=True, out_dtype=BF16, name="mm_gw_xk")
    gw_xv = _mm(mem, dxv, ta=True, out_dtype=BF16, name="mm_gw_xv")
    token = grads_ready("attn", {"xo_w": gw_xo, "xq_w": gw_xq, "xk_w": gw_xk, "xv_w": gw_xv})
    dh1 = _mm(dxq, w["xq_w"], tb=True, name="mm_dh1")
    ds1, ds1_16, g["ln1_g"], g["ln1_b"] = _ln_bwd(x, mix, behind(w["ln1_g"], token), dh1, ds2, name="ln1_bwd")
    gw_out = _mm(mixin, ds1_16, ta=True, out_dtype=BF16, name="mm_gw_out")
    dmixin = _mm(ds1_16, w["w_out"], tb=True, name="mm_dmixin")
    dproj, gw_pool, g["pool_scale"] = _pool_bwd(proj, w["pool_w"], w["pool_scale"], dmixin)
    token = grads_ready("mix", {"w_out": gw_out, "pool_w": gw_pool})
    do_raw, dproj, g["gdn_norm_w"] = _onorm_bwd(o_raw, proj, behind(w["gdn_norm_w"], token), dmixin, dproj)
    dpost, dproj, g["alog_row"], g["dtb_row"] = _gdn_scan_bwd(post, proj, w["alog_row"], w["dtb_row"], saved, do_raw, dproj)
    dproj, g["conv_w"] = _gdn_prep_bwd(proj, w["conv_w"], dpost, dproj)
    gw_in = _mm(x16, dproj, ta=True, out_dtype=BF16, tn=768, name="mm_gw_in")
    grads_ready("in", {"w_in": gw_in})
    grad_x = _mm(dproj, w["w_in"], tb=True, tk=768, epi="add", extra=ds1, add_scale=ALPHA, name="mm_dx")
    return sq, grad_x, g


_MATRICES = ("w_in", "pool_w", "w_out", "xq_w", "xk_w", "xv_w", "xo_w", "w_up", "w_down")
_VECTORS = ("a_log", "dt_bias", "gdn_norm_w", "pool_scale", "ln1_g", "ln1_b", "ln2_g", "ln2_b", "ln3_g", "ln3_b")
_BA_SPLIT = BA_OFF + 2 * GDN_HEADS


def _lane_row(v, offset):
    return jnp.zeros((1, LANE), F32).at[0, offset:offset + v.shape[0]].set(v)


_GROUP_VECTORS = {"mixer": (), "attn": ("ln1_g", "ln1_b", "ln2_g", "ln2_b"), "mlp": ("ln3_g", "ln3_b")}


def _group_weights(group, full):
    w = {n: full[n].reshape(1, D_MODEL) for n in _GROUP_VECTORS[group]}
    if group == "mixer":
        w_in = full["w_in"]
        zeros = jnp.zeros((w_in.shape[0], POOL_OFF - _BA_SPLIT), w_in.dtype)
        w.update({
            "w_in": jnp.concatenate([w_in[:, :_BA_SPLIT], zeros, w_in[:, _BA_SPLIT:]], axis=1),
            "conv_w": full["conv_w"],
            "alog_row": _lane_row(full["a_log"], GDN_HEADS),
            "dtb_row": _lane_row(full["dt_bias"], GDN_HEADS),
            "gdn_norm_w": full["gdn_norm_w"].reshape(1, LANE),
            "pool_w": full["pool_w"],
            "pool_scale": full["pool_scale"].reshape(POOL_GROUPS, 1, POOL_GROUP_DIM),
        })
    elif group == "attn":
        w.update({n: full[n] for n in ("w_out", "xq_w", "xk_w", "xv_w", "xo_w")})
    else:
        w.update({n: full[n] for n in ("w_up", "w_down")})
    return w


def _unpad_w_in(g):
    return jnp.concatenate([g[:, :_BA_SPLIT], g[:, POOL_OFF:]], axis=1)


def _finish_small_grads(g):
    out = {"conv_w": g["conv_w"]}
    out["a_log"] = g["alog_row"][0, GDN_HEADS:2 * GDN_HEADS]
    out["dt_bias"] = g["dtb_row"][0, GDN_HEADS:2 * GDN_HEADS]
    out["gdn_norm_w"] = g["gdn_norm_w"].reshape(LANE)
    out["pool_scale"] = g["pool_scale"].reshape(POOL_GROUPS * POOL_GROUP_DIM)
    for n in ("ln1_g", "ln1_b", "ln2_g", "ln2_b", "ln3_g", "ln3_b"):
        out[n] = g[n].reshape(D_MODEL)
    return out


def _adamw_math(w, g, m, v):
    m = ADAM_B1 * m + (1.0 - ADAM_B1) * g
    v = ADAM_B2 * v + (1.0 - ADAM_B2) * (g * g)
    m_hat = m / (1.0 - ADAM_B1 ** ADAM_STEP)
    v_hat = v / (1.0 - ADAM_B2 ** ADAM_STEP)
    delta = -ADAM_LR * (m_hat / (jnp.sqrt(v_hat) + ADAM_EPS) + ADAM_WD * w)
    return delta, m, v


def _adamw_shard(parts, w, m, v, *, tr, name):
    s, r, c = parts.shape
    tr = min(tr, r)
    assert r % tr == 0, (name, r, tr)

    def body(p_ref, w_ref, m_ref, v_ref, g_ref, d_ref, nm_ref, nv_ref):
        g = p_ref[0].astype(F32)
        for i in range(1, s):
            g = g + p_ref[i].astype(F32)
        delta, nm, nv = _adamw_math(w_ref[...], g, m_ref[...], v_ref[...])
        g_ref[...] = g
        d_ref[...] = delta
        nm_ref[...] = nm
        nv_ref[...] = nv

    blk = pl.BlockSpec((tr, c), lambda i: (i, 0))
    out = jax.ShapeDtypeStruct((r, c), F32)
    return pl.pallas_call(
        body, grid=(r // tr,), in_specs=[pl.BlockSpec((s, tr, c), lambda i: (0, i, 0)), blk, blk, blk],
        out_specs=[blk, blk, blk, blk], out_shape=[out, out, out, out],
        compiler_params=_params("parallel"), name=name,
    )(parts, w, m, v)


def _place():
    return lax.axis_index("x"), lax.axis_index("y"), lax.axis_index("c")


def _slot(px, py, pc):
    return 4 * px + 2 * py + pc


_HBM = pl.BlockSpec(memory_space=pltpu.HBM)


_SEM = pl.BlockSpec(memory_space=pltpu.SEMAPHORE)
_ANY = pl.BlockSpec(memory_space=pl.ANY)
_EFFECT = pltpu.SideEffectType.DATAFLOW_SIDE_EFFECTING
_N_PEERS = N_DEV - 1


def _peer(k, x, y, c):
    return (1 - x if k & 4 else x, 1 - y if k & 2 else y, 1 - c if k & 1 else c)


def _exchange_copy(gather, src, land, w, k, place, send_sems, recv_sems, receiving):
    peer = _peer(k, *place)
    dst_slot = _slot(*peer) if receiving else _slot(*place)
    return pltpu.make_async_remote_copy(
        src_ref=src[w] if gather else src[w].at[_slot(*peer)], dst_ref=land[w].at[dst_slot],
        send_sem=send_sems.at[w * _N_PEERS + k - 1], recv_sem=recv_sems.at[w * _N_PEERS + k - 1], device_id=peer,
        device_id_type=MESH)


def _exchange_start(srcs, *, gather, name):
    n = len(srcs)
    lands = [lax.empty((N_DEV, *s.shape) if gather else s.shape, s.dtype) for s in srcs]

    def body(*refs):
        src, land = refs[:n], refs[n:2 * n]
        send_sems, recv_sems = refs[2 * n:2 * n + 2]
        token = refs[-1]
        place = _place()
        for w in range(n):
            for k in range(1, N_DEV):
                _exchange_copy(gather, src, land, w, k, place, send_sems, recv_sems, receiving=False).start()
        token[...] = jnp.zeros_like(token)

    sems = pltpu.SemaphoreType.DMA((n * _N_PEERS,))
    res = pl.pallas_call(
        body, name=name, in_specs=[_HBM] * (2 * n),
        out_specs=(_SEM, _SEM, *([_HBM] * (2 * n)), pl.BlockSpec(memory_space=pltpu.VMEM)),
        out_shape=(sems, sems, *[pltpu.HBM(a.shape, a.dtype) for a in list(srcs) + lands],
                   jax.ShapeDtypeStruct((8, LANE), F32)),
        input_output_aliases={i: 2 + i for i in range(2 * n)},
        compiler_params=pltpu.CompilerParams(has_side_effects=_EFFECT),
    )(*[pltpu.with_memory_space_constraint(a, pltpu.HBM) for a in list(srcs) + lands])
    return res[0], res[1], list(res[2:2 + n]), list(res[2 + n:2 + 2 * n]), res[-1]


def _exchange_wait(started, after, *, gather, name):
    send_sems, recv_sems, srcs, lands, _ = started
    n = len(srcs)

    def body(*refs):
        src, land = refs[:n], refs[n:2 * n]
        send_sems, recv_sems = refs[2 * n:2 * n + 2]
        place = _place()
        for w in range(n):
            for k in range(1, N_DEV):
                cp = _exchange_copy(gather, src, land, w, k, place, send_sems, recv_sems, receiving=True)
                cp.wait_send()
                cp.wait_recv()

    res = pl.pallas_call(
        body, name=name, in_specs=[_HBM] * (2 * n) + [_SEM, _SEM, _ANY], out_specs=[_HBM] * (2 * n),
        out_shape=[pltpu.HBM(a.shape, a.dtype) for a in list(srcs) + list(lands)],
        input_output_aliases={i: i for i in range(2 * n)},
        compiler_params=pltpu.CompilerParams(has_side_effects=_EFFECT),
    )(*srcs, *lands, send_sems, recv_sems, after)
    return list(res[:n]), list(res[n:])


def _fill_own_slot(lands, srcs, *, gather, name):
    n = len(srcs)

    def body(*refs):
        src, out = refs[n:2 * n], refs[2 * n:3 * n]
        sems = refs[3 * n]
        me = _slot(*_place())
        copies = [pltpu.make_async_copy(src[w] if gather else src[w].at[me], out[w].at[me], sems.at[w]) for w in range(n)]
        for cp in copies:
            cp.start()
        for cp in copies:
            cp.wait()

    return pl.pallas_call(
        body, name=name, in_specs=[_ANY] * (2 * n), out_specs=[_ANY] * n,
        out_shape=[jax.ShapeDtypeStruct(a.shape, a.dtype) for a in lands],
        input_output_aliases={i: i for i in range(n)},
        scratch_shapes=[pltpu.SemaphoreType.DMA((n,))],
    )(*lands, *srcs)


def _small_allreduce_adamw(gvec, wvec, mvec, vvec):
    rows, length = gvec.shape

    def body(g_ref, w_ref, m_ref, v_ref, gs_ref, d_ref, nm_ref, nv_ref, slots, send_sems, recv_sems):
        x, y, c = _place()
        me = _slot(x, y, c)
        slots[me] = g_ref[...]
        sends = []
        for k in range(1, N_DEV):
            peer = _peer(k, x, y, c)
            sends.append(pltpu.make_async_remote_copy(
                src_ref=g_ref, dst_ref=slots.at[me], send_sem=send_sems.at[k - 1], recv_sem=recv_sems.at[k - 1],
                device_id=peer, device_id_type=MESH))
        for cp in sends:
            cp.start()
        for k in range(1, N_DEV):
            peer = _peer(k, x, y, c)
            pltpu.make_async_remote_copy(
                src_ref=g_ref, dst_ref=slots.at[_slot(*peer)], send_sem=send_sems.at[k - 1], recv_sem=recv_sems.at[k - 1],
                device_id=peer, device_id_type=MESH).wait_recv()
        for cp in sends:
            cp.wait_send()
        g = slots[0]
        for s in range(1, N_DEV):
            g = g + slots[s]
        delta, nm, nv = _adamw_math(w_ref[...], g, m_ref[...], v_ref[...])
        gs_ref[...] = g
        d_ref[...] = delta
        nm_ref[...] = nm
        nv_ref[...] = nv

    vmem = pl.BlockSpec(memory_space=pltpu.VMEM)
    out = jax.ShapeDtypeStruct((rows, length), F32)
    return pl.pallas_call(
        body, in_specs=[vmem] * 4, out_specs=[vmem] * 4, out_shape=[out] * 4,
        scratch_shapes=[pltpu.VMEM((N_DEV, rows, length), F32), pltpu.SemaphoreType.DMA((N_DEV - 1,)),
                        pltpu.SemaphoreType.DMA((N_DEV - 1,))],
        name="small_allreduce_adamw",
    )(gvec, wvec, mvec, vvec)


_SMALL_SEGMENTS = (("a_log", GDN_HEADS), ("dt_bias", GDN_HEADS), ("gdn_norm_w", HEAD_DIM), ("pool_scale", GDN_WIDTH),
                   ("ln1_g", D_MODEL), ("ln1_b", D_MODEL), ("ln2_g", D_MODEL), ("ln2_b", D_MODEL),
                   ("ln3_g", D_MODEL), ("ln3_b", D_MODEL), ("conv_w", CONV_K * QKV_COLS))
_SMALL_ROWS = 8
_SMALL_LEN = -(-sum(sz for _, sz in _SMALL_SEGMENTS) // (_SMALL_ROWS * LANE)) * LANE


def _pack_small(vals):
    parts = [vals[n].reshape(-1).astype(F32) if n in vals else jnp.zeros((sz,), F32) for n, sz in _SMALL_SEGMENTS]
    flat = jnp.concatenate(parts)
    flat = jnp.pad(flat, (0, _SMALL_ROWS * _SMALL_LEN - flat.shape[0]))
    return flat.reshape(_SMALL_ROWS, _SMALL_LEN)


def _unpack_small(vec):
    flat = vec.reshape(-1)
    out, off = {}, 0
    for n, sz in _SMALL_SEGMENTS:
        out[n] = flat[off:off + sz]
        off += sz
    return out


_WEIGHT_ORDER = ("w_in", "conv_w", "a_log", "dt_bias", "gdn_norm_w", "pool_w", "pool_scale", "w_out", "ln1_g", "ln1_b",
                 "xq_w", "xk_w", "xv_w", "xo_w", "ln2_g", "ln2_b", "w_up", "w_down", "ln3_g", "ln3_b")
_ADAM_ROWS = {"w_in": 256, "pool_w": 128, "w_out": 128, "xq_w": 128, "xk_w": 128, "xv_w": 128, "xo_w": 128,
              "w_up": 128, "w_down": 128}


def _shard2d(name, a):
    return a.reshape(-1, a.shape[-1]) if name == "pool_w" else a


def _gathered_to_full(name, gth):
    if name == "w_up":
        return gth
    if name in ("w_in", "conv_w"):
        return jnp.transpose(gth, (1, 0, 2)).reshape(gth.shape[1], N_DEV * gth.shape[2])
    if name == "pool_w":
        g4 = gth.reshape(N_DEV, POOL_GROUPS, POOL_GROUP_DIM // N_DEV, POOL_GROUP_DIM)
        return jnp.transpose(g4, (1, 0, 2, 3)).reshape(POOL_GROUPS, POOL_GROUP_DIM, POOL_GROUP_DIM)
    return gth.reshape(N_DEV * gth.shape[1], gth.shape[2])


def _full_to_chunks(name, full):
    if name == "w_up":
        return full
    if name == "w_in":
        r, cols = full.shape
        return jnp.transpose(full.reshape(r, N_DEV, cols // N_DEV), (1, 0, 2))
    if name == "pool_w":
        g4 = full.reshape(POOL_GROUPS, N_DEV, POOL_GROUP_DIM // N_DEV, POOL_GROUP_DIM)
        return jnp.transpose(g4, (1, 0, 2, 3)).reshape(N_DEV, POOL_GROUPS * POOL_GROUP_DIM // N_DEV, POOL_GROUP_DIM)
    return full.reshape(N_DEV, full.shape[0] // N_DEV, full.shape[1])


_GATHER_GROUPS = (("mixer", ("w_in", "conv_w", "pool_w")), ("attn", ("w_out", "xq_w", "xk_w", "xv_w", "xo_w")),
                  ("mlp", ("w_up", "w_down")))


def _grad_chunks(name, g):
    if name == "w_in":
        g = _unpad_w_in(g)
    return _full_to_chunks(name, g.astype(BF16))


def kernel(x, mem, w_in, conv_w, a_log, dt_bias, gdn_norm_w, pool_w, pool_scale, w_out, ln1_g, ln1_b, xq_w, xk_w, xv_w, xo_w, ln2_g, ln2_b, w_up, w_down, ln3_g, ln3_b, loss_target, m_w_in, m_conv_w, m_a_log, m_dt_bias, m_gdn_norm_w, m_pool_w, m_pool_scale, m_w_out, m_ln1_g, m_ln1_b, m_xq_w, m_xk_w, m_xv_w, m_xo_w, m_ln2_g, m_ln2_b, m_w_up, m_w_down, m_ln3_g, m_ln3_b, v_w_in, v_conv_w, v_a_log, v_dt_bias, v_gdn_norm_w, v_pool_w, v_pool_scale, v_w_out, v_ln1_g, v_ln1_b, v_xq_w, v_xk_w, v_xv_w, v_xo_w, v_ln2_g, v_ln2_b, v_w_up, v_w_down, v_ln3_g, v_ln3_b):
    args = dict(locals())
    wt = {n: args[n][0] for n in _WEIGHT_ORDER}
    mo = {n: args["m_" + n][0] for n in _WEIGHT_ORDER}
    vo = {n: args["v_" + n][0] for n in _WEIGHT_ORDER}

    gathers = {}
    for group, names in _GATHER_GROUPS:
        shards = [_shard2d(n, wt[n]).astype(F32 if n == "conv_w" else BF16) for n in names]
        gathers[group] = (names, _exchange_start(shards, gather=True, name="gather_start_" + group))

    def weights_of(group, after):
        names, started = gathers[group]
        if after is None:
            after = sum(st[4] for g, (_, st) in gathers.items() if g != group)
        shards, lands = _exchange_wait(started, after, gather=True, name="gather_wait_" + group)
        lands = _fill_own_slot(lands, shards, gather=True, name="gather_own_" + group)
        full = {n: _gathered_to_full(n, land) for n, land in zip(names, lands)}
        full.update({n: wt[n] for n in _VECTORS})
        return _group_weights(group, full)

    scatters = {}

    def grads_ready(group, grads):
        names = tuple(grads)
        started = _exchange_start([_grad_chunks(n, grads[n]) for n in names], gather=False, name="scatter_start_" + group)
        scatters[group] = (names, started)
        return started[4][0:1, 0:1]

    sq, grad_x, g = _local_step(x[0], mem[0], loss_target[0], weights_of, grads_ready)
    small = _finish_small_grads(g)

    gs, ds, ms, vs = _small_allreduce_adamw(
        _pack_small(small), _pack_small({n: wt[n] for n in _VECTORS}), _pack_small({n: mo[n] for n in _VECTORS}),
        _pack_small({n: vo[n] for n in _VECTORS}))
    gs, ds, ms, vs = _unpack_small(gs), _unpack_small(ds), _unpack_small(ms), _unpack_small(vs)

    out = {}
    after = grad_x
    for group, (names, started) in scatters.items():
        chunks, lands = _exchange_wait(started, after, gather=False, name="scatter_wait_" + group)
        lands = _fill_own_slot(lands, chunks, gather=False, name="scatter_own_" + group)
        for n, parts in zip(names, lands):
            res = _adamw_shard(parts, _shard2d(n, wt[n]), _shard2d(n, mo[n]), _shard2d(n, vo[n]), tr=_ADAM_ROWS[n],
                               name="adamw_" + n)
            out[n] = [r.reshape(args[n].shape) for r in res]
            after = res[1]
    cols = conv_w.shape[-1]
    me = _slot(*_place())
    conv_full = gs["conv_w"].reshape(CONV_K, QKV_COLS)
    conv_mine = lax.dynamic_slice(conv_full, (0, me * cols), (CONV_K, cols))
    res = _adamw_shard(conv_mine[None], wt["conv_w"], mo["conv_w"], vo["conv_w"], tr=CONV_K, name="adamw_conv_w")
    out["conv_w"] = [r.reshape(conv_w.shape) for r in res]
    for n in _VECTORS:
        out[n] = [t[n].reshape(args[n].shape) for t in (gs, ds, ms, vs)]

    loss = lax.psum(0.5 * sq[0, 0] / D_MODEL, ("x", "y", "c"))
    return (loss, grad_x[None], *[out[n][0] for n in _WEIGHT_ORDER], *[out[n][1] for n in _WEIGHT_ORDER],
            *[out[n][2] for n in _WEIGHT_ORDER], *[out[n][3] for n in _WEIGHT_ORDER])
```

```python
import functools
import math

import jax
import jax.numpy as jnp
from jax import lax
from jax.experimental import pallas as pl
from jax.experimental.pallas import tpu as pltpu

F32 = jnp.float32
BF16 = jnp.bfloat16
MESH = pl.DeviceIdType.MESH

N_DEV = 8
D_MODEL = 2048
GDN_WIDTH = 1024
GDN_HEADS = 8
HEAD_DIM = 128
CONV_K = 4
CHUNK = 64
POOL_GROUPS = 4
POOL_GROUP_DIM = 256
MEM_LEN = 256
XATTN_HEADS = 4
XATTN_HEAD_DIM = 512
D_FF = 8192
IN_COLS = 5136
ALPHA = 2.0 ** 0.25
LN_EPS = 1e-5
NORM_EPS = 1e-6

LANE = 128
QKV_COLS = 3 * GDN_WIDTH
Z_OFF = QKV_COLS
BA_OFF = 4 * GDN_WIDTH
POOL_OFF = BA_OFF + 2 * LANE
PROJ_COLS = POOL_OFF + GDN_WIDTH
Z_BLK = Z_OFF // LANE
BA_BLK = BA_OFF // LANE
POOL_BLK = POOL_OFF // POOL_GROUP_DIM

ADAM_LR = 0.001
ADAM_B1 = 0.9
ADAM_B2 = 0.999
ADAM_EPS = 1e-08
ADAM_WD = 0.01
ADAM_STEP = 10

VMEM_LIMIT_BYTES = 48 * 1024 * 1024


def _params(*sem):
    return pltpu.CompilerParams(dimension_semantics=sem if sem else None, vmem_limit_bytes=VMEM_LIMIT_BYTES)


def _make_dots(cast, precision):
    def dg(a, b, ca, cb):
        if cast is not None:
            a = a.astype(cast)
            b = b.astype(cast)
        return lax.dot_general(a, b, (((ca,), (cb,)), ((), ())), precision=precision, preferred_element_type=F32)

    def nn_(a, b):
        return dg(a, b, 1, 0)

    def nt_(a, b):
        return dg(a, b, 1, 1)

    def tn_(a, b):
        return dg(a, b, 0, 0)

    @jax.custom_vjp
    def nn(a, b):
        return nn_(a, b)

    nn.defvjp(lambda a, b: (nn_(a, b), (a, b)), lambda r, g: (nt_(g, r[1]), tn_(r[0], g)))

    @jax.custom_vjp
    def nt(a, b):
        return nt_(a, b)

    nt.defvjp(lambda a, b: (nt_(a, b), (a, b)), lambda r, g: (nn_(g, r[1]), tn_(g, r[0])))

    @jax.custom_vjp
    def tn(a, b):
        return tn_(a, b)

    tn.defvjp(lambda a, b: (tn_(a, b), (a, b)), lambda r, g: (nt_(r[1], g), nn_(r[0], g)))

    return (nn_, nt_, tn_), (nn, nt, tn)


_BDOT_PLAIN, _BDOT_VJP = _make_dots(BF16, None)
_FDOT_PLAIN, _FDOT_VJP = _make_dots(None, lax.Precision.HIGHEST)


def _mm(a, b, *, ta=False, tb=False, out_dtype=F32, tm=None, tn=512, tk=None, epi=None, extra=None, add_scale=1.0,
        b_chunks=False, o_chunks=False, name):
    m, k = (a.shape[1], a.shape[0]) if ta else a.shape
    if b_chunks:
        n, kb = (b.shape[1], N_DEV * b.shape[2]) if tb else (N_DEV * b.shape[2], b.shape[1])
    else:
        n, kb = b.shape if tb else (b.shape[1], b.shape[0])
    assert kb == k, (name, a.shape, b.shape)
    tm, tn, tk = min(tm or m, m), min(tn, n), min(tk or k, k)
    assert m % tm == 0 and n % tn == 0 and k % tk == 0, (name, m, n, k)
    nk = k // tk
    dims = (((0 if ta else 1,), (1 if tb else 0,)), ((), ()))
    n_extra = 0 if epi in (None, "relu2") else 1
    n_out = 2 if epi == "relu2" else 1
    if epi in ("relu2", "mul2r"):
        out_dtype = BF16

    def body(*refs):
        a_ref, b_ref = refs[:2]
        c_ref = refs[2] if n_extra else None
        o_refs = refs[2 + n_extra:2 + n_extra + n_out]
        scr = refs[2 + n_extra + n_out:]
        r = lax.dot_general(a_ref[...].astype(BF16), b_ref[...].astype(BF16), dims, preferred_element_type=F32)

        def finish(v):
            if epi == "add":
                o_refs[0][...] = (v + add_scale * c_ref[...]).astype(out_dtype)
            elif epi == "relu2":
                p = jnp.maximum(v, 0.0)
                o_refs[0][...] = (p * p).astype(BF16)
                o_refs[1][...] = p.astype(BF16)
            elif epi == "mul2r":
                o_refs[0][...] = (v * (2.0 * c_ref[...].astype(F32))).astype(BF16)
            else:
                o_refs[0][...] = v.astype(out_dtype)

        if nk == 1:
            finish(r)
        else:
            acc = scr[0]
            kk = pl.program_id(2)

            @pl.when(kk == 0)
            def _():
                acc[...] = r

            @pl.when(kk > 0)
            def _():
                acc[...] += r

            @pl.when(kk == nk - 1)
            def _():
                finish(acc[...])

    a_spec = pl.BlockSpec((tk, tm), lambda i, j, kk: (kk, i)) if ta else pl.BlockSpec((tm, tk), lambda i, j, kk: (i, kk))
    if b_chunks and tb:
        kc = k // N_DEV // tk
        b_spec = pl.BlockSpec((None, tn, tk), lambda i, j, kk: (kk // kc, j, kk % kc))
    elif b_chunks:
        nc = n // N_DEV // tn
        b_spec = pl.BlockSpec((None, tk, tn), lambda i, j, kk: (j // nc, kk, j % nc))
    elif tb:
        b_spec = pl.BlockSpec((tn, tk), lambda i, j, kk: (j, kk))
    else:
        b_spec = pl.BlockSpec((tk, tn), lambda i, j, kk: (kk, j))
    mn_spec = pl.BlockSpec((tm, tn), lambda i, j, kk: (i, j))
    if o_chunks:
        oc = n // N_DEV // tn
        o_spec = pl.BlockSpec((None, tm, tn), lambda i, j, kk: (j // oc, i, j % oc))
        o_shape = jax.ShapeDtypeStruct((N_DEV, m, n // N_DEV), out_dtype)
    else:
        o_spec, o_shape = mn_spec, jax.ShapeDtypeStruct((m, n), out_dtype)
    res = pl.pallas_call(
        body, grid=(m // tm, n // tn, nk), in_specs=[a_spec, b_spec] + [mn_spec] * n_extra,
        out_specs=[o_spec] * n_out, out_shape=[o_shape] * n_out,
        scratch_shapes=[pltpu.VMEM((tm, tn), F32)] if nk > 1 else [],
        compiler_params=_params("parallel", "parallel", "arbitrary"), name=name,
    )(a, b, *([extra] if n_extra else []))
    return res if n_out > 1 else res[0]


def _cast_bf16(v, *, name, tm=512):
    t, d = v.shape
    tm = min(tm, t)

    def body(v_ref, o_ref):
        o_ref[...] = v_ref[...].astype(BF16)

    spec = pl.BlockSpec((tm, d), lambda i: (i, 0))
    return pl.pallas_call(body, grid=(t // tm,), in_specs=[spec], out_specs=spec,
                          out_shape=jax.ShapeDtypeStruct((t, d), BF16), compiler_params=_params("parallel"), name=name)(v)


def _shift_down(v, s):
    if s == 0:
        return v
    row = lax.broadcasted_iota(jnp.int32, v.shape, 0)
    return jnp.where(row >= s, pltpu.roll(v, s, axis=0), 0.0)


def _shift_up(v, s):
    if s == 0:
        return v
    t = v.shape[0]
    row = lax.broadcasted_iota(jnp.int32, v.shape, 0)
    return jnp.where(row < t - s, pltpu.roll(v, t - s, axis=0), 0.0)


def _post_col(j):
    return (j % GDN_HEADS) * 3 + j // GDN_HEADS


def _gdn_prep_fwd(proj, conv_w):
    t = proj.shape[0]

    def body(x_ref, w_ref, o_ref):
        j = pl.program_id(0)
        x = x_ref[...]
        y = jnp.zeros_like(x)
        for tap in range(CONV_K):
            y = y + w_ref[tap:tap + 1, :] * _shift_down(x, CONV_K - 1 - tap)
        c = y * jax.nn.sigmoid(y)
        nrm = c * lax.rsqrt(jnp.sum(c * c, axis=1, keepdims=True) + NORM_EPS)
        o_ref[...] = jnp.where(j < 2 * GDN_HEADS, nrm, c)

    return pl.pallas_call(
        body, grid=(QKV_COLS // LANE,),
        in_specs=[pl.BlockSpec((t, LANE), lambda j: (0, j)), pl.BlockSpec((CONV_K, LANE), lambda j: (0, j))],
        out_specs=pl.BlockSpec((t, LANE), lambda j: (0, _post_col(j))),
        out_shape=jax.ShapeDtypeStruct((t, QKV_COLS), F32),
        compiler_params=_params("parallel"), name="gdn_prep_fwd",
    )(proj, conv_w)


def _gdn_prep_bwd(proj, conv_w, dpost, dproj):
    t = proj.shape[0]

    def body(x_ref, w_ref, d_ref, _, dx_ref, dw_ref):
        j = pl.program_id(0)
        x = x_ref[...]
        xs = [_shift_down(x, CONV_K - 1 - tap) for tap in range(CONV_K)]
        y = jnp.zeros_like(x)
        for tap in range(CONV_K):
            y = y + w_ref[tap:tap + 1, :] * xs[tap]
        sig = jax.nn.sigmoid(y)
        c = y * sig
        r = lax.rsqrt(jnp.sum(c * c, axis=1, keepdims=True) + NORM_EPS)
        nrm = c * r
        d = d_ref[...]
        dc_norm = r * (d - nrm * jnp.sum(d * nrm, axis=1, keepdims=True))
        dc = jnp.where(j < 2 * GDN_HEADS, dc_norm, d)
        dy = dc * (sig * (1.0 + y * (1.0 - sig)))
        dx = jnp.zeros_like(x)
        for tap in range(CONV_K):
            dx = dx + _shift_up(w_ref[tap:tap + 1, :] * dy, CONV_K - 1 - tap)
            dw_ref[tap:tap + 1, :] = jnp.sum(dy * xs[tap], axis=0, keepdims=True)
        dx_ref[...] = dx.astype(dx_ref.dtype)

    return pl.pallas_call(
        body, grid=(QKV_COLS // LANE,),
        in_specs=[pl.BlockSpec((t, LANE), lambda j: (0, j)), pl.BlockSpec((CONV_K, LANE), lambda j: (0, j)),
                  pl.BlockSpec((t, LANE), lambda j: (0, _post_col(j))), pl.BlockSpec(memory_space=pl.ANY)],
        out_specs=[pl.BlockSpec((t, LANE), lambda j: (0, j)), pl.BlockSpec((CONV_K, LANE), lambda j: (0, j))],
        out_shape=[jax.ShapeDtypeStruct(dproj.shape, dproj.dtype), jax.ShapeDtypeStruct((CONV_K, QKV_COLS), F32)],
        input_output_aliases={3: 0},
        compiler_params=_params("parallel"), name="gdn_prep_bwd",
    )(proj, conv_w, dpost, dproj)


def _softplus(v):
    return jnp.maximum(v, 0.0) + jnp.log(1.0 + jnp.exp(-jnp.abs(v)))


def _tri_inv(low, nn):
    r = lax.broadcasted_iota(jnp.int32, (CHUNK, CHUNK), 0)
    c = lax.broadcasted_iota(jnp.int32, (CHUNK, CHUNK), 1)
    eye = (r == c).astype(F32)
    same_blk = lax.shift_right_logical(r, 4) == lax.shift_right_logical(c, 4)
    diag = jnp.where(same_blk, low, 0.0)
    off = low - diag
    n1 = -diag
    n2 = nn(n1, n1)
    n4 = nn(n2, n2)
    n8 = nn(n4, n4)
    inv_d = nn(nn(nn(eye + n1, eye + n2), eye + n4), eye + n8)
    m1 = nn(inv_d, off)
    m2 = nn(m1, m1)
    return nn(nn(eye - m1, eye + m2), inv_d)


def _chunk_fn(qkv, ba, alog_row, dtb_row, state, h, bdots, fdots):
    nn, nt, tn = bdots
    fnn = fdots[0]
    q = qkv[:, 0:HEAD_DIM] * (HEAD_DIM ** -0.5)
    k = qkv[:, HEAD_DIM:2 * HEAD_DIM]
    v = qkv[:, 2 * HEAD_DIM:3 * HEAD_DIM]
    lane = lax.broadcasted_iota(jnp.int32, ba.shape, 1)
    bg = jnp.where(lane < GDN_HEADS, jax.nn.sigmoid(ba), -jnp.exp(alog_row) * _softplus(ba + dtb_row))
    beta = jnp.sum(jnp.where(lane == h, bg, 0.0), axis=1, keepdims=True)
    g = jnp.sum(jnp.where(lane == h + GDN_HEADS, bg, 0.0), axis=1, keepdims=True)

    r = lax.broadcasted_iota(jnp.int32, (CHUNK, CHUNK), 0)
    c = lax.broadcasted_iota(jnp.int32, (CHUNK, CHUNK), 1)
    incl = r >= c
    strict = r > c
    eye = r == c

    def to_row(col):
        return jnp.sum(jnp.where(eye, col, 0.0), axis=0, keepdims=True)

    gc = jnp.sum(jnp.where(incl, to_row(g), 0.0), axis=1, keepdims=True)
    diff = gc - to_row(gc)
    decay = jnp.where(incl, jnp.exp(jnp.where(incl, diff, 0.0)), 0.0)
    k_beta = k * beta
    v_beta = v * beta
    low = jnp.where(strict, nt(k_beta, k) * decay, 0.0)
    t_inv = _tri_inv(low, fnn)
    eg = jnp.exp(gc)
    u = fnn(t_inv, v_beta)
    w = fnn(t_inv, k_beta * eg)
    attn = jnp.where(incl, nt(q, k) * decay, 0.0)
    v_new = u - nn(w, state)
    o = nn(q * eg, state) + nn(attn, v_new)
    last = lax.broadcasted_iota(jnp.int32, (CHUNK, 1), 0) == CHUNK - 1
    g_last = jnp.sum(jnp.where(last, gc, 0.0), axis=0, keepdims=True)
    k_dec = k * jnp.exp(g_last - gc)
    new_state = state * jnp.exp(g_last) + tn(k_dec, v_new)
    return o, new_state


def _gdn_scan_fwd(post, proj, alog_row, dtb_row):
    t = post.shape[0]
    n_chunks = t // CHUNK

    def body(qkv_ref, ba_ref, al_ref, dt_ref, o_ref, save_ref, state_ref):
        n = pl.program_id(0)
        h = pl.program_id(1)

        @pl.when(n == 0)
        def _():
            state_ref[h] = jnp.zeros((HEAD_DIM, HEAD_DIM), F32)

        state = state_ref[h]
        save_ref[0, 0] = state
        o, new_state = _chunk_fn(qkv_ref[...], ba_ref[...], al_ref[...], dt_ref[...], state, h, _BDOT_PLAIN, _FDOT_PLAIN)
        o_ref[...] = o
        state_ref[h] = new_state

    return pl.pallas_call(
        body, grid=(n_chunks, GDN_HEADS),
        in_specs=[pl.BlockSpec((CHUNK, 3 * HEAD_DIM), lambda n, h: (n, h)),
                  pl.BlockSpec((CHUNK, LANE), lambda n, h: (n, BA_BLK)),
                  pl.BlockSpec((1, LANE), lambda n, h: (0, 0)), pl.BlockSpec((1, LANE), lambda n, h: (0, 0))],
        out_specs=[pl.BlockSpec((CHUNK, HEAD_DIM), lambda n, h: (n, h)),
                   pl.BlockSpec((1, 1, HEAD_DIM, HEAD_DIM), lambda n, h: (h, n, 0, 0))],
        out_shape=[jax.ShapeDtypeStruct((t, GDN_WIDTH), F32),
                   jax.ShapeDtypeStruct((GDN_HEADS, n_chunks, HEAD_DIM, HEAD_DIM), F32)],
        scratch_shapes=[pltpu.VMEM((GDN_HEADS, HEAD_DIM, HEAD_DIM), F32)],
        compiler_params=_params("arbitrary", "arbitrary"), name="gdn_scan_fwd",
    )(post, proj, alog_row, dtb_row)


def _gdn_scan_bwd(post, proj, alog_row, dtb_row, saved, do, dproj):
    t = post.shape[0]
    n_chunks = t // CHUNK
    last = n_chunks - 1

    def body(qkv_ref, ba_ref, al_ref, dt_ref, save_ref, do_ref, _, dqkv_ref, dba_ref, dal_ref, ddt_ref, dstate_ref, dba_acc):
        n = pl.program_id(0)
        h = pl.program_id(1)

        @pl.when(n == 0)
        def _():
            dstate_ref[h] = jnp.zeros((HEAD_DIM, HEAD_DIM), F32)

        @pl.when((n == 0) & (h == 0))
        def _():
            dal_ref[...] = jnp.zeros_like(dal_ref)
            ddt_ref[...] = jnp.zeros_like(ddt_ref)

        def f(qkv, ba, al, dt, state):
            return _chunk_fn(qkv, ba, al, dt, state, h, _BDOT_VJP, _FDOT_VJP)

        _, vjp = jax.vjp(f, qkv_ref[...], ba_ref[...], al_ref[...], dt_ref[...], save_ref[0, 0])
        dqkv, dba, dal, ddt, dstate = vjp((do_ref[...], dstate_ref[h]))
        dqkv_ref[...] = dqkv

        @pl.when(h == 0)
        def _():
            dba_acc[...] = dba

        @pl.when(h > 0)
        def _():
            dba_acc[...] += dba

        @pl.when(h == GDN_HEADS - 1)
        def _():
            dba_ref[:, 0:LANE] = dba_acc[...].astype(dba_ref.dtype)
            dba_ref[:, LANE:2 * LANE] = jnp.zeros((CHUNK, LANE), dba_ref.dtype)

        dal_ref[...] += dal
        ddt_ref[...] += ddt
        dstate_ref[h] = dstate

    return pl.pallas_call(
        body, grid=(n_chunks, GDN_HEADS),
        in_specs=[pl.BlockSpec((CHUNK, 3 * HEAD_DIM), lambda n, h: (last - n, h)),
                  pl.BlockSpec((CHUNK, LANE), lambda n, h: (last - n, BA_BLK)),
                  pl.BlockSpec((1, LANE), lambda n, h: (0, 0)), pl.BlockSpec((1, LANE), lambda n, h: (0, 0)),
                  pl.BlockSpec((1, 1, HEAD_DIM, HEAD_DIM), lambda n, h: (h, last - n, 0, 0)),
                  pl.BlockSpec((CHUNK, HEAD_DIM), lambda n, h: (last - n, h)),
                  pl.BlockSpec(memory_space=pl.ANY)],
        out_specs=[pl.BlockSpec((CHUNK, 3 * HEAD_DIM), lambda n, h: (last - n, h)),
                   pl.BlockSpec((CHUNK, 2 * LANE), lambda n, h: (last - n, BA_BLK // 2)),
                   pl.BlockSpec((1, LANE), lambda n, h: (0, 0)), pl.BlockSpec((1, LANE), lambda n, h: (0, 0))],
        out_shape=[jax.ShapeDtypeStruct((t, QKV_COLS), F32), jax.ShapeDtypeStruct(dproj.shape, dproj.dtype),
                   jax.ShapeDtypeStruct((1, LANE), F32), jax.ShapeDtypeStruct((1, LANE), F32)],
        input_output_aliases={6: 1},
        scratch_shapes=[pltpu.VMEM((GDN_HEADS, HEAD_DIM, HEAD_DIM), F32), pltpu.VMEM((CHUNK, LANE), F32)],
        compiler_params=_params("arbitrary", "arbitrary"), name="gdn_scan_bwd",
    )(post, proj, alog_row, dtb_row, saved, do, dproj)


def _onorm_fn(o, z, w):
    return o * lax.rsqrt(jnp.mean(o * o, axis=1, keepdims=True) + NORM_EPS) * w * (z * jax.nn.sigmoid(z))


def _onorm_fwd(o_raw, proj, norm_w, mixin, tm=512):
    t = o_raw.shape[0]
    tm = min(tm, t)

    def body(o_ref, z_ref, w_ref, _, out_ref):
        out_ref[...] = _onorm_fn(o_ref[...], z_ref[...], w_ref[...]).astype(out_ref.dtype)

    return pl.pallas_call(
        body, grid=(t // tm, GDN_HEADS),
        in_specs=[pl.BlockSpec((tm, LANE), lambda i, h: (i, h)), pl.BlockSpec((tm, LANE), lambda i, h: (i, Z_BLK + h)),
                  pl.BlockSpec((1, LANE), lambda i, h: (0, 0)), pl.BlockSpec(memory_space=pl.ANY)],
        out_specs=pl.BlockSpec((tm, LANE), lambda i, h: (i, h)),
        out_shape=jax.ShapeDtypeStruct(mixin.shape, mixin.dtype), input_output_aliases={3: 0},
        compiler_params=_params("parallel", "parallel"), name="gdn_onorm_fwd",
    )(o_raw, proj, norm_w, mixin)


def _onorm_bwd(o_raw, proj, norm_w, dmixin, dproj, tm=512):
    t = o_raw.shape[0]
    tm = min(tm, t)

    def body(o_ref, z_ref, w_ref, d_ref, _, do_ref, dz_ref, dw_ref):
        @pl.when((pl.program_id(0) == 0) & (pl.program_id(1) == 0))
        def _():
            dw_ref[...] = jnp.zeros_like(dw_ref)

        _, vjp = jax.vjp(_onorm_fn, o_ref[...], z_ref[...], w_ref[...])
        do, dz, dw = vjp(d_ref[...])
        do_ref[...] = do
        dz_ref[...] = dz.astype(dz_ref.dtype)
        dw_ref[...] += dw

    return pl.pallas_call(
        body, grid=(t // tm, GDN_HEADS),
        in_specs=[pl.BlockSpec((tm, LANE), lambda i, h: (i, h)), pl.BlockSpec((tm, LANE), lambda i, h: (i, Z_BLK + h)),
                  pl.BlockSpec((1, LANE), lambda i, h: (0, 0)), pl.BlockSpec((tm, LANE), lambda i, h: (i, h)),
                  pl.BlockSpec(memory_space=pl.ANY)],
        out_specs=[pl.BlockSpec((tm, LANE), lambda i, h: (i, h)), pl.BlockSpec((tm, LANE), lambda i, h: (i, Z_BLK + h)),
                   pl.BlockSpec((1, LANE), lambda i, h: (0, 0))],
        out_shape=[jax.ShapeDtypeStruct((t, GDN_WIDTH), F32), jax.ShapeDtypeStruct(dproj.shape, dproj.dtype),
                   jax.ShapeDtypeStruct((1, LANE), F32)],
        input_output_aliases={4: 1},
        compiler_params=_params("arbitrary", "arbitrary"), name="gdn_onorm_bwd",
    )(o_raw, proj, norm_w, dmixin, dproj)


def _pool_select(levels, gi):
    out = levels[-1]
    for lvl in range(len(levels) - 2, -1, -1):
        out = jnp.where(gi == lvl, levels[lvl], out)
    return out


def _pool_count(shape, gi):
    pos = lax.broadcasted_iota(jnp.int32, shape, 0)
    win = lax.shift_left(jnp.int32(2), gi)
    return jnp.minimum(pos + 1, win).astype(F32)


def _pooled(p, gi):
    acc = p
    levels = []
    for lvl in range(POOL_GROUPS):
        acc = acc + _shift_down(acc, 1 << lvl)
        levels.append(acc)
    return _pool_select(levels, gi) / _pool_count(p.shape, gi) - p


def _pool_fwd(proj, pool_w, pool_scale):
    t = proj.shape[0]

    def body(p_ref, w_ref, s_ref, out_ref):
        gi = pl.program_id(0)
        pooled = _pooled(p_ref[...], gi)
        out_ref[...] = (_BDOT_PLAIN[0](pooled, w_ref[0]) * s_ref[0]).astype(out_ref.dtype)

    return pl.pallas_call(
        body, grid=(POOL_GROUPS,),
        in_specs=[pl.BlockSpec((t, POOL_GROUP_DIM), lambda g: (0, POOL_BLK + g)),
                  pl.BlockSpec((1, POOL_GROUP_DIM, POOL_GROUP_DIM), lambda g: (g, 0, 0)),
                  pl.BlockSpec((1, 1, POOL_GROUP_DIM), lambda g: (g, 0, 0))],
        out_specs=pl.BlockSpec((t, POOL_GROUP_DIM), lambda g: (0, GDN_WIDTH // POOL_GROUP_DIM + g)),
        out_shape=jax.ShapeDtypeStruct((t, 2 * GDN_WIDTH), BF16),
        compiler_params=_params("parallel"), name="pool_fwd",
    )(proj, pool_w, pool_scale)


def _pool_bwd(proj, pool_w, pool_scale, dmixin):
    t = proj.shape[0]
    nn, nt, tn = _BDOT_PLAIN

    def body(p_ref, w_ref, s_ref, d_ref, dp_ref, dw_ref, ds_ref):
        gi = pl.program_id(0)
        p = p_ref[...]
        pooled = _pooled(p, gi)
        mixed = nn(pooled, w_ref[0])
        d = d_ref[...]
        ds_ref[0] = jnp.sum(d * mixed, axis=0, keepdims=True)
        dmixed = d * s_ref[0]
        dw_ref[0] = tn(pooled, dmixed)
        dpooled = nt(dmixed, w_ref[0])
        acc = dpooled / _pool_count(p.shape, gi)
        levels = []
        for lvl in range(POOL_GROUPS):
            acc = acc + _shift_up(acc, 1 << lvl)
            levels.append(acc)
        dp_ref[...] = (_pool_select(levels, gi) - dpooled).astype(dp_ref.dtype)

    return pl.pallas_call(
        body, grid=(POOL_GROUPS,),
        in_specs=[pl.BlockSpec((t, POOL_GROUP_DIM), lambda g: (0, POOL_BLK + g)),
                  pl.BlockSpec((1, POOL_GROUP_DIM, POOL_GROUP_DIM), lambda g: (g, 0, 0)),
                  pl.BlockSpec((1, 1, POOL_GROUP_DIM), lambda g: (g, 0, 0)),
                  pl.BlockSpec((t, POOL_GROUP_DIM), lambda g: (0, GDN_WIDTH // POOL_GROUP_DIM + g))],
        out_specs=[pl.BlockSpec((t, POOL_GROUP_DIM), lambda g: (0, POOL_BLK + g)),
                   pl.BlockSpec((1, POOL_GROUP_DIM, POOL_GROUP_DIM), lambda g: (g, 0, 0)),
                   pl.BlockSpec((1, 1, POOL_GROUP_DIM), lambda g: (g, 0, 0))],
        out_shape=[jax.ShapeDtypeStruct((t, PROJ_COLS), BF16),
                   jax.ShapeDtypeStruct((POOL_GROUPS, POOL_GROUP_DIM, POOL_GROUP_DIM), F32),
                   jax.ShapeDtypeStruct((POOL_GROUPS, 1, POOL_GROUP_DIM), F32)],
        compiler_params=_params("parallel"), name="pool_bwd",
    )(proj, pool_w, pool_scale, dmixin)


def _ln_stats(s):
    mu = jnp.mean(s, axis=1, keepdims=True)
    xc = s - mu
    var = jnp.mean(xc * xc, axis=1, keepdims=True)
    rstd = lax.rsqrt(var + LN_EPS)
    return xc * rstd, rstd


def _ln_fwd(h_in, y, g, b, *, name, tm=256):
    t, d = h_in.shape
    tm = min(tm, t)

    def body(h_ref, y_ref, g_ref, b_ref, o_ref, o16_ref):
        xhat, _ = _ln_stats(ALPHA * h_ref[...] + y_ref[...])
        out = xhat * g_ref[...] + b_ref[...]
        o_ref[...] = out
        o16_ref[...] = out.astype(BF16)

    row = pl.BlockSpec((tm, d), lambda i: (i, 0))
    vec = pl.BlockSpec((1, d), lambda i: (0, 0))
    return pl.pallas_call(
        body, grid=(t // tm,), in_specs=[row, row, vec, vec], out_specs=[row, row],
        out_shape=[jax.ShapeDtypeStruct((t, d), F32), jax.ShapeDtypeStruct((t, d), BF16)],
        compiler_params=_params("parallel"), name=name,
    )(h_in, y, g, b)


def _ln_loss_fwd(h_in, y, g, b, target, *, name, tm=256):
    t, d = h_in.shape
    tm = min(tm, t)

    def body(h_ref, y_ref, g_ref, b_ref, t_ref, dy_ref, sq_ref):
        @pl.when(pl.program_id(0) == 0)
        def _():
            sq_ref[...] = jnp.zeros_like(sq_ref)

        xhat, _ = _ln_stats(ALPHA * h_ref[...] + y_ref[...])
        err = xhat * g_ref[...] + b_ref[...] - t_ref[...]
        dy_ref[...] = err * (1.0 / d)
        sq_ref[...] += jnp.sum(jnp.sum(err * err, axis=1, keepdims=True), axis=0, keepdims=True)

    row = pl.BlockSpec((tm, d), lambda i: (i, 0))
    vec = pl.BlockSpec((1, d), lambda i: (0, 0))
    return pl.pallas_call(
        body, grid=(t // tm,), in_specs=[row, row, vec, vec, row],
        out_specs=[row, pl.BlockSpec((1, LANE), lambda i: (0, 0))],
        out_shape=[jax.ShapeDtypeStruct((t, d), F32), jax.ShapeDtypeStruct((1, LANE), F32)],
        compiler_params=_params("arbitrary"), name=name,
    )(h_in, y, g, b, target)


def _ln_bwd(h_in, y, g, d_a, d_b, *, name, tm=256):
    t, d = h_in.shape
    tm = min(tm, t)
    has_b = d_b is not None

    def body(*refs):
        if has_b:
            h_ref, y_ref, g_ref, da_ref, db_ref, ds_ref, ds16_ref, dg_ref, dbias_ref = refs
        else:
            h_ref, y_ref, g_ref, da_ref, ds_ref, ds16_ref, dg_ref, dbias_ref = refs

        @pl.when(pl.program_id(0) == 0)
        def _():
            dg_ref[...] = jnp.zeros_like(dg_ref)
            dbias_ref[...] = jnp.zeros_like(dbias_ref)

        xhat, rstd = _ln_stats(ALPHA * h_ref[...] + y_ref[...])
        dout = da_ref[...]
        if has_b:
            dout = dout + ALPHA * db_ref[...]
        dxhat = dout * g_ref[...]
        m1 = jnp.mean(dxhat, axis=1, keepdims=True)
        m2 = jnp.mean(dxhat * xhat, axis=1, keepdims=True)
        ds = rstd * (dxhat - m1 - xhat * m2)
        ds_ref[...] = ds
        ds16_ref[...] = ds.astype(BF16)
        dg_ref[...] += jnp.sum(dout * xhat, axis=0, keepdims=True)
        dbias_ref[...] += jnp.sum(dout, axis=0, keepdims=True)

    row = pl.BlockSpec((tm, d), lambda i: (i, 0))
    vec = pl.BlockSpec((1, d), lambda i: (0, 0))
    args = [h_in, y, g, d_a] + ([d_b] if has_b else [])
    return pl.pallas_call(
        body, grid=(t // tm,), in_specs=[row, row, vec, row] + ([row] if has_b else []),
        out_specs=[row, row, vec, vec],
        out_shape=[jax.ShapeDtypeStruct((t, d), F32), jax.ShapeDtypeStruct((t, d), BF16),
                   jax.ShapeDtypeStruct((1, d), F32), jax.ShapeDtypeStruct((1, d), F32)],
        compiler_params=_params("arbitrary"), name=name,
    )(*args)


def _attn_fn(q, k, v, dots):
    nn, nt, _ = dots
    s = nt(q, k) * (XATTN_HEAD_DIM ** -0.5)
    s = s - lax.stop_gradient(jnp.max(s, axis=1, keepdims=True))
    e = jnp.exp(s)
    p = e / jnp.sum(e, axis=1, keepdims=True)
    return nn(p, v)


def _attn_fwd(q, k, v, tq=512):
    t = q.shape[0]
    tq = min(tq, t)

    def body(q_ref, k_ref, v_ref, o_ref):
        o_ref[...] = _attn_fn(q_ref[...], k_ref[...], v_ref[...], _BDOT_PLAIN).astype(BF16)

    qs = pl.BlockSpec((tq, XATTN_HEAD_DIM), lambda h, i: (i, h))
    ks = pl.BlockSpec((MEM_LEN, XATTN_HEAD_DIM), lambda h, i: (0, h))
    return pl.pallas_call(
        body, grid=(XATTN_HEADS, t // tq), in_specs=[qs, ks, ks], out_specs=qs,
        out_shape=jax.ShapeDtypeStruct(q.shape, BF16), compiler_params=_params("parallel", "parallel"), name="xattn_fwd",
    )(q, k, v)


def _attn_bwd(q, k, v, do, tq=512):
    t = q.shape[0]
    tq = min(tq, t)

    def body(q_ref, k_ref, v_ref, do_ref, dq_ref, dk_ref, dv_ref):
        @pl.when(pl.program_id(1) == 0)
        def _():
            dk_ref[...] = jnp.zeros_like(dk_ref)
            dv_ref[...] = jnp.zeros_like(dv_ref)

        _, vjp = jax.vjp(lambda a, b, c: _attn_fn(a, b, c, _BDOT_VJP), q_ref[...].astype(F32), k_ref[...].astype(F32),
                         v_ref[...].astype(F32))
        dq, dk, dv = vjp(do_ref[...].astype(F32))
        dq_ref[...] = dq.astype(BF16)
        dk_ref[...] += dk
        dv_ref[...] += dv

    qs = pl.BlockSpec((tq, XATTN_HEAD_DIM), lambda h, i: (i, h))
    ks = pl.BlockSpec((MEM_LEN, XATTN_HEAD_DIM), lambda h, i: (0, h))
    return pl.pallas_call(
        body, grid=(XATTN_HEADS, t // tq), in_specs=[qs, ks, ks, qs], out_specs=[qs, ks, ks],
        out_shape=[jax.ShapeDtypeStruct(q.shape, BF16), jax.ShapeDtypeStruct(k.shape, F32), jax.ShapeDtypeStruct(v.shape, F32)],
        compiler_params=_params("parallel", "arbitrary"), name="xattn_bwd",
    )(q, k, v, do)


def _local_step(x, mem, target, weights_of, grads_ready):
    def behind(vec, token):
        return vec if token is None else vec + token

    x16 = _cast_bf16(x, name="cast_x")
    w = dict(weights_of("mixer", None))
    proj = _mm(x16, w["w_in"], tn=768, name="mm_in_proj")
    post = _gdn_prep_fwd(proj, w["conv_w"])
    o_raw, saved = _gdn_scan_fwd(post, proj, w["alog_row"], w["dtb_row"])
    mixin = _pool_fwd(proj, w["pool_w"], w["pool_scale"])
    mixin = _onorm_fwd(o_raw, proj, w["gdn_norm_w"], mixin)
    w.update(weights_of("attn", mixin))
    mix = _mm(mixin, w["w_out"], name="mm_out_proj")
    h1, h1_16 = _ln_fwd(x, mix, w["ln1_g"], w["ln1_b"], name="ln1_fwd")
    xq = _mm(h1_16, w["xq_w"], out_dtype=BF16, name="mm_xq")
    xk = _mm(mem, w["xk_w"], out_dtype=BF16, name="mm_xk")
    xv = _mm(mem, w["xv_w"], out_dtype=BF16, name="mm_xv")
    xo = _attn_fwd(xq, xk, xv)
    xa = _mm(xo, w["xo_w"], name="mm_xo")
    h2, h2_16 = _ln_fwd(h1, xa, w["ln2_g"], w["ln2_b"], name="ln2_fwd")
    w.update(weights_of("mlp", h2_16))
    act, relu = _mm(h2_16, w["w_up"], b_chunks=True, epi="relu2", name="mm_up")
    ff = _mm(act, w["w_down"], tn=1024, tk=512, name="mm_down")
    dy, sq = _ln_loss_fwd(h2, ff, w["ln3_g"], w["ln3_b"], target, name="ln3_loss_fwd")

    g = {}
    ds3, ds3_16, g["ln3_g"], g["ln3_b"] = _ln_bwd(h2, ff, w["ln3_g"], dy, None, name="ln3_bwd")
    gw_down = _mm(act, ds3_16, ta=True, out_dtype=BF16, tm=512, tn=D_MODEL, name="mm_gw_down")
    du = _mm(ds3_16, w["w_down"], tb=True, epi="mul2r", extra=relu, name="mm_du")
    gw_up = _mm(h2_16, du, ta=True, out_dtype=BF16, o_chunks=True, name="mm_gw_up")
    token = grads_ready("mlp", {"w_down": gw_down, "w_up": gw_up})
    dh2 = _mm(du, w["w_up"], tb=True, b_chunks=True, tn=1024, tk=512, name="mm_dh2")
    ds2, ds2_16, g["ln2_g"], g["ln2_b"] = _ln_bwd(h1, xa, behind(w["ln2_g"], token), dh2, ds3, name="ln2_bwd")
    gw_xo = _mm(xo, ds2_16, ta=True, out_dtype=BF16, name="mm_gw_xo")
    dxo = _mm(ds2_16, w["xo_w"], tb=True, out_dtype=BF16, name="mm_dxo")
    dxq, dxk, dxv = _attn_bwd(xq, xk, xv, dxo)
    gw_xq = _mm(h1_16, dxq, ta=True, out_dtype=BF16, name="mm_gw_xq")
    gw_xk = _mm(mem, dxk, ta=True, out_dtype=BF16, name="mm_gw_xk")
    gw_xv = _mm(mem, dxv, ta=True, out_dtype=BF16, name="mm_gw_xv")
    token = grads_ready("attn", {"xo_w": gw_xo, "xq_w": gw_xq, "xk_w": gw_xk, "xv_w": gw_xv})
    dh1 = _mm(dxq, w["xq_w"], tb=True, name="mm_dh1")
    ds1, ds1_16, g["ln1_g"], g["ln1_b"] = _ln_bwd(x, mix, behind(w["ln1_g"], token), dh1, ds2, name="ln1_bwd")
    gw_out = _mm(mixin, ds1_16, ta=True, out_dtype=BF16, name="mm_gw_out")
    dmixin = _mm(ds1_16, w["w_out"], tb=True, name="mm_dmixin")
    dproj, gw_pool, g["pool_scale"] = _pool_bwd(proj, w["pool_w"], w["pool_scale"], dmixin)
    token = grads_ready("mix", {"w_out": gw_out, "pool_w": gw_pool})
    do_raw, dproj, g["gdn_norm_w"] = _onorm_bwd(o_raw, proj, behind(w["gdn_norm_w"], token), dmixin, dproj)
    dpost, dproj, g["alog_row"], g["dtb_row"] = _gdn_scan_bwd(post, proj, w["alog_row"], w["dtb_row"], saved, do_raw, dproj)
    dproj, g["conv_w"] = _gdn_prep_bwd(proj, w["conv_w"], dpost, dproj)
    gw_in = _mm(x16, dproj, ta=True, out_dtype=BF16, tn=768, name="mm_gw_in")
    grads_ready("in", {"w_in": gw_in})
    grad_x = _mm(dproj, w["w_in"], tb=True, tk=768, epi="add", extra=ds1, add_scale=ALPHA, name="mm_dx")
    return sq, grad_x, g


_MATRICES = ("w_in", "pool_w", "w_out", "xq_w", "xk_w", "xv_w", "xo_w", "w_up", "w_down")
_VECTORS = ("a_log", "dt_bias", "gdn_norm_w", "pool_scale", "ln1_g", "ln1_b", "ln2_g", "ln2_b", "ln3_g", "ln3_b")
_BA_SPLIT = BA_OFF + 2 * GDN_HEADS


def _lane_row(v, offset):
    return jnp.zeros((1, LANE), F32).at[0, offset:offset + v.shape[0]].set(v)


_GROUP_VECTORS = {"mixer": (), "attn": ("ln1_g", "ln1_b", "ln2_g", "ln2_b"), "mlp": ("ln3_g", "ln3_b")}


def _group_weights(group, full):
    w = {n: full[n].reshape(1, D_MODEL) for n in _GROUP_VECTORS[group]}
    if group == "mixer":
        w_in = full["w_in"]
        zeros = jnp.zeros((w_in.shape[0], POOL_OFF - _BA_SPLIT), w_in.dtype)
        w.update({
            "w_in": jnp.concatenate([w_in[:, :_BA_SPLIT], zeros, w_in[:, _BA_SPLIT:]], axis=1),
            "conv_w": full["conv_w"],
            "alog_row": _lane_row(full["a_log"], GDN_HEADS),
            "dtb_row": _lane_row(full["dt_bias"], GDN_HEADS),
            "gdn_norm_w": full["gdn_norm_w"].reshape(1, LANE),
            "pool_w": full["pool_w"],
            "pool_scale": full["pool_scale"].reshape(POOL_GROUPS, 1, POOL_GROUP_DIM),
        })
    elif group == "attn":
        w.update({n: full[n] for n in ("w_out", "xq_w", "xk_w", "xv_w", "xo_w")})
    else:
        w.update({n: full[n] for n in ("w_up", "w_down")})
    return w


def _unpad_w_in(g):
    return jnp.concatenate([g[:, :_BA_SPLIT], g[:, POOL_OFF:]], axis=1)


def _finish_small_grads(g):
    out = {"conv_w": g["conv_w"]}
    out["a_log"] = g["alog_row"][0, GDN_HEADS:2 * GDN_HEADS]
    out["dt_bias"] = g["dtb_row"][0, GDN_HEADS:2 * GDN_HEADS]
    out["gdn_norm_w"] = g["gdn_norm_w"].reshape(LANE)
    out["pool_scale"] = g["pool_scale"].reshape(POOL_GROUPS * POOL_GROUP_DIM)
    for n in ("ln1_g", "ln1_b", "ln2_g", "ln2_b", "ln3_g", "ln3_b"):
        out[n] = g[n].reshape(D_MODEL)
    return out


def _adamw_math(w, g, m, v):
    m = ADAM_B1 * m + (1.0 - ADAM_B1) * g
    v = ADAM_B2 * v + (1.0 - ADAM_B2) * (g * g)
    m_hat = m / (1.0 - ADAM_B1 ** ADAM_STEP)
    v_hat = v / (1.0 - ADAM_B2 ** ADAM_STEP)
    delta = -ADAM_LR * (m_hat / (jnp.sqrt(v_hat) + ADAM_EPS) + ADAM_WD * w)
    return delta, m, v


def _adamw_shard(parts, own, me, w, m, v, *, tr, name):
    s, r, c = parts.shape
    tr = min(tr, r)
    assert r % tr == 0, (name, r, tr)

    def body(me_ref, p_ref, own_ref, w_ref, m_ref, v_ref, g_ref, d_ref, nm_ref, nv_ref):
        mine = own_ref[...].astype(F32)
        g = None
        for i in range(s):
            part = jnp.where(me_ref[0] == i, mine, p_ref[i].astype(F32))
            g = part if g is None else g + part
        delta, nm, nv = _adamw_math(w_ref[...], g, m_ref[...], v_ref[...])
        g_ref[...] = g
        d_ref[...] = delta
        nm_ref[...] = nm
        nv_ref[...] = nv

    blk = pl.BlockSpec((tr, c), lambda i, me_ref: (i, 0))
    out = jax.ShapeDtypeStruct((r, c), F32)
    return pl.pallas_call(
        body,
        grid_spec=pltpu.PrefetchScalarGridSpec(
            num_scalar_prefetch=1, grid=(r // tr,),
            in_specs=[pl.BlockSpec((s, tr, c), lambda i, me_ref: (0, i, 0)),
                      pl.BlockSpec((None, tr, c), lambda i, me_ref: (me_ref[0], i, 0)), blk, blk, blk],
            out_specs=[blk, blk, blk, blk]),
        out_shape=[out, out, out, out], compiler_params=_params("parallel"), name=name,
    )(me, parts, own, w, m, v)


def _place():
    return lax.axis_index("x"), lax.axis_index("y"), lax.axis_index("c")


def _slot(px, py, pc):
    return 4 * px + 2 * py + pc


_HBM = pl.BlockSpec(memory_space=pltpu.HBM)


_SEM = pl.BlockSpec(memory_space=pltpu.SEMAPHORE)
_ANY = pl.BlockSpec(memory_space=pl.ANY)
_EFFECT = pltpu.SideEffectType.DATAFLOW_SIDE_EFFECTING
_N_PEERS = N_DEV - 1


def _peer(k, x, y, c):
    return (1 - x if k & 4 else x, 1 - y if k & 2 else y, 1 - c if k & 1 else c)


_EXCHANGE_BITS = {"scatter": (1, 2, 3, 4, 5, 6, 7), "gather_chips": (1, 2, 4, 6), "gather_pass": (2, 4, 6)}


def _exchange_copy(mode, src, land, w, i, place, send_sems, recv_sems, receiving):
    bits = _EXCHANGE_BITS[mode]
    k = bits[i]
    peer = _peer(k, *place)
    me = _slot(*place)
    if mode == "scatter":
        to, src_ref, sent_to, got_at = peer, src[w].at[_slot(*peer)], me, _slot(*peer)
    elif mode == "gather_chips":
        to, src_ref, sent_to, got_at = peer, src[w], me, _slot(*peer)
    else:
        blk = _slot(*peer)
        to, src_ref, sent_to, got_at = _peer(1, *place), land[w].at[blk], blk, _slot(*_peer(k | 1, *place))
    sem = w * len(bits) + i
    return pltpu.make_async_remote_copy(
        src_ref=src_ref, dst_ref=land[w].at[got_at if receiving else sent_to], send_sem=send_sems.at[sem],
        recv_sem=recv_sems.at[sem], device_id=to, device_id_type=MESH)


def _exchange_start(mode, srcs, lands, after, *, name):
    ns, nl = len(srcs), len(lands)
    n_sem = nl * len(_EXCHANGE_BITS[mode])

    def body(*refs):
        src, land = refs[:ns], refs[ns:ns + nl]
        send_sems, recv_sems = refs[ns + nl + 1:ns + nl + 3]
        token = refs[-1]
        place = _place()
        for w in range(nl):
            for i in range(len(_EXCHANGE_BITS[mode])):
                _exchange_copy(mode, src, land, w, i, place, send_sems, recv_sems, receiving=False).start()
        token[...] = jnp.zeros_like(token)

    sems = pltpu.SemaphoreType.DMA((n_sem,))
    arrays = list(srcs) + list(lands)
    res = pl.pallas_call(
        body, name=name, in_specs=[_HBM] * (ns + nl) + [_ANY],
        out_specs=(_SEM, _SEM, *([_HBM] * (ns + nl)), pl.BlockSpec(memory_space=pltpu.VMEM)),
        out_shape=(sems, sems, *[pltpu.HBM(a.shape, a.dtype) for a in arrays], jax.ShapeDtypeStruct((8, LANE), F32)),
        input_output_aliases={i: 2 + i for i in range(ns + nl)},
        compiler_params=pltpu.CompilerParams(has_side_effects=_EFFECT),
    )(*[pltpu.with_memory_space_constraint(a, pltpu.HBM) for a in arrays], after)
    return res[0], res[1], list(res[2:2 + ns]), list(res[2 + ns:2 + ns + nl]), res[-1]


def _exchange_wait(mode, started, after, *, name):
    send_sems, recv_sems, srcs, lands, _ = started
    ns, nl = len(srcs), len(lands)

    def body(*refs):
        src, land = refs[:ns], refs[ns:ns + nl]
        send_sems, recv_sems = refs[ns + nl:ns + nl + 2]
        place = _place()
        for w in range(nl):
            for i in range(len(_EXCHANGE_BITS[mode])):
                cp = _exchange_copy(mode, src, land, w, i, place, send_sems, recv_sems, receiving=True)
                cp.wait_send()
                cp.wait_recv()

    arrays = list(srcs) + list(lands)
    res = pl.pallas_call(
        body, name=name, in_specs=[_HBM] * (ns + nl) + [_SEM, _SEM, _ANY], out_specs=[_HBM] * (ns + nl),
        out_shape=[pltpu.HBM(a.shape, a.dtype) for a in arrays],
        input_output_aliases={i: i for i in range(ns + nl)},
        compiler_params=pltpu.CompilerParams(has_side_effects=_EFFECT),
    )(*arrays, send_sems, recv_sems, after)
    return list(res[:ns]), list(res[ns:])


def _small_allreduce_adamw(gvec, wvec, mvec, vvec):
    rows, length = gvec.shape

    def body(g_ref, w_ref, m_ref, v_ref, gs_ref, d_ref, nm_ref, nv_ref, slots, send_sems, recv_sems):
        x, y, c = _place()
        me = _slot(x, y, c)
        slots[me] = g_ref[...]
        sends = []
        for k in range(1, N_DEV):
            peer = _peer(k, x, y, c)
            sends.append(pltpu.make_async_remote_copy(
                src_ref=g_ref, dst_ref=slots.at[me], send_sem=send_sems.at[k - 1], recv_sem=recv_sems.at[k - 1],
                device_id=peer, device_id_type=MESH))
        for cp in sends:
            cp.start()
        for k in range(1, N_DEV):
            peer = _peer(k, x, y, c)
            pltpu.make_async_remote_copy(
                src_ref=g_ref, dst_ref=slots.at[_slot(*peer)], send_sem=send_sems.at[k - 1], recv_sem=recv_sems.at[k - 1],
                device_id=peer, device_id_type=MESH).wait_recv()
        for cp in sends:
            cp.wait_send()
        g = slots[0]
        for s in range(1, N_DEV):
            g = g + slots[s]
        delta, nm, nv = _adamw_math(w_ref[...], g, m_ref[...], v_ref[...])
        gs_ref[...] = g
        d_ref[...] = delta
        nm_ref[...] = nm
        nv_ref[...] = nv

    vmem = pl.BlockSpec(memory_space=pltpu.VMEM)
    out = jax.ShapeDtypeStruct((rows, length), F32)
    return pl.pallas_call(
        body, in_specs=[vmem] * 4, out_specs=[vmem] * 4, out_shape=[out] * 4,
        scratch_shapes=[pltpu.VMEM((N_DEV, rows, length), F32), pltpu.SemaphoreType.DMA((N_DEV - 1,)),
                        pltpu.SemaphoreType.DMA((N_DEV - 1,))],
        name="small_allreduce_adamw",
    )(gvec, wvec, mvec, vvec)


_SMALL_SEGMENTS = (("a_log", GDN_HEADS), ("dt_bias", GDN_HEADS), ("gdn_norm_w", HEAD_DIM), ("pool_scale", GDN_WIDTH),
                   ("ln1_g", D_MODEL), ("ln1_b", D_MODEL), ("ln2_g", D_MODEL), ("ln2_b", D_MODEL),
                   ("ln3_g", D_MODEL), ("ln3_b", D_MODEL), ("conv_w", CONV_K * QKV_COLS))
_SMALL_ROWS = 8
_SMALL_LEN = -(-sum(sz for _, sz in _SMALL_SEGMENTS) // (_SMALL_ROWS * LANE)) * LANE


def _pack_small(vals):
    parts = [vals[n].reshape(-1).astype(F32) if n in vals else jnp.zeros((sz,), F32) for n, sz in _SMALL_SEGMENTS]
    flat = jnp.concatenate(parts)
    flat = jnp.pad(flat, (0, _SMALL_ROWS * _SMALL_LEN - flat.shape[0]))
    return flat.reshape(_SMALL_ROWS, _SMALL_LEN)


def _unpack_small(vec):
    flat = vec.reshape(-1)
    out, off = {}, 0
    for n, sz in _SMALL_SEGMENTS:
        out[n] = flat[off:off + sz]
        off += sz
    return out


_WEIGHT_ORDER = ("w_in", "conv_w", "a_log", "dt_bias", "gdn_norm_w", "pool_w", "pool_scale", "w_out", "ln1_g", "ln1_b",
                 "xq_w", "xk_w", "xv_w", "xo_w", "ln2_g", "ln2_b", "w_up", "w_down", "ln3_g", "ln3_b")
_ADAM_ROWS = {"w_in": 256, "pool_w": 128, "w_out": 128, "xq_w": 128, "xk_w": 128, "xv_w": 128, "xo_w": 128,
              "w_up": 128, "w_down": 128}


def _shard2d(name, a):
    return a.reshape(-1, a.shape[-1]) if name == "pool_w" else a


def _gathered_to_full(name, gth):
    if name == "w_up":
        return gth
    if name in ("w_in", "conv_w"):
        return jnp.transpose(gth, (1, 0, 2)).reshape(gth.shape[1], N_DEV * gth.shape[2])
    if name == "pool_w":
        g4 = gth.reshape(N_DEV, POOL_GROUPS, POOL_GROUP_DIM // N_DEV, POOL_GROUP_DIM)
        return jnp.transpose(g4, (1, 0, 2, 3)).reshape(POOL_GROUPS, POOL_GROUP_DIM, POOL_GROUP_DIM)
    return gth.reshape(N_DEV * gth.shape[1], gth.shape[2])


def _full_to_chunks(name, full):
    if name == "w_up":
        return full
    if name == "w_in":
        r, cols = full.shape
        return jnp.transpose(full.reshape(r, N_DEV, cols // N_DEV), (1, 0, 2))
    if name == "pool_w":
        g4 = full.reshape(POOL_GROUPS, N_DEV, POOL_GROUP_DIM // N_DEV, POOL_GROUP_DIM)
        return jnp.transpose(g4, (1, 0, 2, 3)).reshape(N_DEV, POOL_GROUPS * POOL_GROUP_DIM // N_DEV, POOL_GROUP_DIM)
    return full.reshape(N_DEV, full.shape[0] // N_DEV, full.shape[1])


_GATHER_GROUPS = (("mixer", ("w_in", "conv_w", "pool_w")), ("attn", ("w_out", "xq_w", "xk_w", "xv_w", "xo_w")),
                  ("mlp", ("w_up", "w_down")))


def _grad_chunks(name, g):
    if name == "w_in":
        g = _unpad_w_in(g)
    return _full_to_chunks(name, g.astype(BF16))


def kernel(x, mem, w_in, conv_w, a_log, dt_bias, gdn_norm_w, pool_w, pool_scale, w_out, ln1_g, ln1_b, xq_w, xk_w, xv_w, xo_w, ln2_g, ln2_b, w_up, w_down, ln3_g, ln3_b, loss_target, m_w_in, m_conv_w, m_a_log, m_dt_bias, m_gdn_norm_w, m_pool_w, m_pool_scale, m_w_out, m_ln1_g, m_ln1_b, m_xq_w, m_xk_w, m_xv_w, m_xo_w, m_ln2_g, m_ln2_b, m_w_up, m_w_down, m_ln3_g, m_ln3_b, v_w_in, v_conv_w, v_a_log, v_dt_bias, v_gdn_norm_w, v_pool_w, v_pool_scale, v_w_out, v_ln1_g, v_ln1_b, v_xq_w, v_xk_w, v_xv_w, v_xo_w, v_ln2_g, v_ln2_b, v_w_up, v_w_down, v_ln3_g, v_ln3_b):
    args = dict(locals())
    wt = {n: args[n][0] for n in _WEIGHT_ORDER}
    mo = {n: args["m_" + n][0] for n in _WEIGHT_ORDER}
    vo = {n: args["v_" + n][0] for n in _WEIGHT_ORDER}

    me = _slot(*_place())
    me_arr = jnp.reshape(me, (1,)).astype(jnp.int32)
    nothing = jnp.zeros((8, LANE), F32)

    def landing_zones(names):
        shards = [_shard2d(n, wt[n]).astype(F32 if n == "conv_w" else BF16) for n in names]
        zones = [lax.dynamic_update_slice(lax.empty((N_DEV, *s.shape), s.dtype), s[None], (me, 0, 0)) for s in shards]
        return shards, zones

    first, attn_names, mlp_names = (names for _, names in _GATHER_GROUPS)
    shards, zones = landing_zones(first)
    chips_first = _exchange_start("gather_chips", shards, zones, nothing, name="gather_chips_mixer")
    _, zones = _exchange_wait("gather_chips", chips_first, chips_first[4], name="gather_chips_mixer_wait")
    pass_first = _exchange_start("gather_pass", [], zones, nothing, name="gather_pass_mixer")
    shards, zones = landing_zones(attn_names + mlp_names)
    chips_rest = _exchange_start("gather_chips", shards, zones, pass_first[4], name="gather_chips_rest")
    pending = {}

    def full_weights(group, names, zones, token=None):
        full = {n: _gathered_to_full(n, z) for n, z in zip(names, zones)}
        full.update({n: wt[n] if token is None else wt[n] + token for n in _VECTORS})
        return _group_weights(group, full)

    def weights_of(group, after):
        if group == "mixer":
            _, zones = _exchange_wait("gather_pass", pass_first, chips_rest[4], name="gather_pass_mixer_wait")
            return full_weights(group, first, zones)
        if group == "attn":
            _, zones = _exchange_wait("gather_chips", chips_rest, after, name="gather_chips_rest_wait")
            pass_attn = _exchange_start("gather_pass", [], zones[:len(attn_names)], nothing, name="gather_pass_attn")
            _, attn_zones = _exchange_wait("gather_pass", pass_attn, pass_attn[4], name="gather_pass_attn_wait")
            pending["mlp"] = _exchange_start("gather_pass", [], zones[len(attn_names):], attn_zones[0], name="gather_pass_mlp")
            return full_weights(group, attn_names, attn_zones, pending["mlp"][4][0, 0])
        _, zones = _exchange_wait("gather_pass", pending["mlp"], after, name="gather_pass_mlp_wait")
        return full_weights(group, mlp_names, zones)

    scatters = {}

    def grads_ready(group, grads):
        names = tuple(grads)
        chunks = [_grad_chunks(n, grads[n]) for n in names]
        started = _exchange_start("scatter", chunks, [lax.empty(c.shape, c.dtype) for c in chunks], nothing,
                                  name="scatter_start_" + group)
        scatters[group] = (names, started)
        return started[4][0:1, 0:1]

    sq, grad_x, g = _local_step(x[0], mem[0], loss_target[0], weights_of, grads_ready)
    small = _finish_small_grads(g)

    gs, ds, ms, vs = _small_allreduce_adamw(
        _pack_small(small), _pack_small({n: wt[n] for n in _VECTORS}), _pack_small({n: mo[n] for n in _VECTORS}),
        _pack_small({n: vo[n] for n in _VECTORS}))
    gs, ds, ms, vs = _unpack_small(gs), _unpack_small(ds), _unpack_small(ms), _unpack_small(vs)

    out = {}
    after = grad_x
    for group, (names, started) in scatters.items():
        chunks, lands = _exchange_wait("scatter", started, after, name="scatter_wait_" + group)
        for n, parts, own in zip(names, lands, chunks):
            res = _adamw_shard(parts, own, me_arr, _shard2d(n, wt[n]), _shard2d(n, mo[n]), _shard2d(n, vo[n]),
                               tr=_ADAM_ROWS[n], name="adamw_" + n)
            out[n] = [r.reshape(args[n].shape) for r in res]
            after = res[1]
    cols = conv_w.shape[-1]
    conv_full = gs["conv_w"].reshape(CONV_K, QKV_COLS)
    conv_mine = lax.dynamic_slice(conv_full, (0, me * cols), (CONV_K, cols))[None]
    res = _adamw_shard(conv_mine, conv_mine, jnp.zeros((1,), jnp.int32), wt["conv_w"], mo["conv_w"], vo["conv_w"],
                       tr=CONV_K, name="adamw_conv_w")
    out["conv_w"] = [r.reshape(conv_w.shape) for r in res]
    for n in _VECTORS:
        out[n] = [t[n].reshape(args[n].shape) for t in (gs, ds, ms, vs)]

    loss = lax.psum(0.5 * sq[0, 0] / D_MODEL, ("x", "y", "c"))
    return (loss, grad_x[None], *[out[n][0] for n in _WEIGHT_ORDER], *[out[n][1] for n in _WEIGHT_ORDER],
            *[out[n][2] for n in _WEIGHT_ORDER], *[out[n][3] for n in _WEIGHT_ORDER])
```

```python
import functools
import math

import jax
import jax.numpy as jnp
from jax import lax
from jax.experimental import pallas as pl
from jax.experimental.pallas import tpu as pltpu

F32 = jnp.float32
BF16 = jnp.bfloat16
MESH = pl.DeviceIdType.MESH

N_DEV = 8
D_MODEL = 2048
GDN_WIDTH = 1024
GDN_HEADS = 8
HEAD_DIM = 128
CONV_K = 4
CHUNK = 64
POOL_GROUPS = 4
POOL_GROUP_DIM = 256
MEM_LEN = 256
XATTN_HEADS = 4
XATTN_HEAD_DIM = 512
D_FF = 8192
IN_COLS = 5136
ALPHA = 2.0 ** 0.25
LN_EPS = 1e-5
NORM_EPS = 1e-6

LANE = 128
QKV_COLS = 3 * GDN_WIDTH
Z_OFF = QKV_COLS
BA_OFF = 4 * GDN_WIDTH
POOL_OFF = BA_OFF + 2 * LANE
PROJ_COLS = POOL_OFF + GDN_WIDTH
Z_BLK = Z_OFF // LANE
BA_BLK = BA_OFF // LANE
POOL_BLK = POOL_OFF // POOL_GROUP_DIM

ADAM_LR = 0.001
ADAM_B1 = 0.9
ADAM_B2 = 0.999
ADAM_EPS = 1e-08
ADAM_WD = 0.01
ADAM_STEP = 10

VMEM_LIMIT_BYTES = 48 * 1024 * 1024


def _params(*sem):
    return pltpu.CompilerParams(dimension_semantics=sem if sem else None, vmem_limit_bytes=VMEM_LIMIT_BYTES)


def _make_dots(cast, precision, batched=False):
    lead = 1 if batched else 0
    batch = ((0,), (0,)) if batched else ((), ())

    def dg(a, b, ca, cb):
        if cast is not None:
            a = a.astype(cast)
            b = b.astype(cast)
        return lax.dot_general(a, b, (((ca + lead,), (cb + lead,)), batch), precision=precision, preferred_element_type=F32)

    def nn_(a, b):
        return dg(a, b, 1, 0)

    def nt_(a, b):
        return dg(a, b, 1, 1)

    def tn_(a, b):
        return dg(a, b, 0, 0)

    @jax.custom_vjp
    def nn(a, b):
        return nn_(a, b)

    nn.defvjp(lambda a, b: (nn_(a, b), (a, b)), lambda r, g: (nt_(g, r[1]), tn_(r[0], g)))

    @jax.custom_vjp
    def nt(a, b):
        return nt_(a, b)

    nt.defvjp(lambda a, b: (nt_(a, b), (a, b)), lambda r, g: (nn_(g, r[1]), tn_(g, r[0])))

    @jax.custom_vjp
    def tn(a, b):
        return tn_(a, b)

    tn.defvjp(lambda a, b: (tn_(a, b), (a, b)), lambda r, g: (nt_(r[1], g), nn_(r[0], g)))

    return (nn_, nt_, tn_), (nn, nt, tn)


_BDOT_PLAIN, _BDOT_VJP = _make_dots(BF16, None)
_BDOT_BATCH_PLAIN, _BDOT_BATCH_VJP = _make_dots(BF16, None, batched=True)
_FDOT_BATCH_PLAIN, _FDOT_BATCH_VJP = _make_dots(None, lax.Precision.HIGH, batched=True)


def _mm(a, b, *, ta=False, tb=False, out_dtype=F32, tm=None, tn=512, tk=None, epi=None, extra=None, add_scale=1.0,
        b_chunks=False, o_chunks=False, name):
    m, k = (a.shape[1], a.shape[0]) if ta else a.shape
    if b_chunks:
        n, kb = (b.shape[1], N_DEV * b.shape[2]) if tb else (N_DEV * b.shape[2], b.shape[1])
    else:
        n, kb = b.shape if tb else (b.shape[1], b.shape[0])
    assert kb == k, (name, a.shape, b.shape)
    tm, tn, tk = min(tm or m, m), min(tn, n), min(tk or k, k)
    assert m % tm == 0 and n % tn == 0 and k % tk == 0, (name, m, n, k)
    nk = k // tk
    dims = (((0 if ta else 1,), (1 if tb else 0,)), ((), ()))
    n_extra = 0 if epi in (None, "relu2") else 1
    n_out = 2 if epi == "relu2" else 1
    if epi in ("relu2", "mul2r"):
        out_dtype = BF16

    def body(*refs):
        a_ref, b_ref = refs[:2]
        c_ref = refs[2] if n_extra else None
        o_refs = refs[2 + n_extra:2 + n_extra + n_out]
        scr = refs[2 + n_extra + n_out:]
        r = lax.dot_general(a_ref[...].astype(BF16), b_ref[...].astype(BF16), dims, preferred_element_type=F32)

        def finish(v):
            if epi == "add":
                o_refs[0][...] = (v + add_scale * c_ref[...]).astype(out_dtype)
            elif epi == "relu2":
                p = jnp.maximum(v, 0.0)
                o_refs[0][...] = (p * p).astype(BF16)
                o_refs[1][...] = p.astype(BF16)
            elif epi == "mul2r":
                o_refs[0][...] = (v * (2.0 * c_ref[...].astype(F32))).astype(BF16)
            else:
                o_refs[0][...] = v.astype(out_dtype)

        if nk == 1:
            finish(r)
        else:
            acc = scr[0]
            kk = pl.program_id(2)

            @pl.when(kk == 0)
            def _():
                acc[...] = r

            @pl.when(kk > 0)
            def _():
                acc[...] += r

            @pl.when(kk == nk - 1)
            def _():
                finish(acc[...])

    a_spec = pl.BlockSpec((tk, tm), lambda i, j, kk: (kk, i)) if ta else pl.BlockSpec((tm, tk), lambda i, j, kk: (i, kk))
    if b_chunks and tb:
        kc = k // N_DEV // tk
        b_spec = pl.BlockSpec((None, tn, tk), lambda i, j, kk: (kk // kc, j, kk % kc))
    elif b_chunks:
        nc = n // N_DEV // tn
        b_spec = pl.BlockSpec((None, tk, tn), lambda i, j, kk: (j // nc, kk, j % nc))
    elif tb:
        b_spec = pl.BlockSpec((tn, tk), lambda i, j, kk: (j, kk))
    else:
        b_spec = pl.BlockSpec((tk, tn), lambda i, j, kk: (kk, j))
    mn_spec = pl.BlockSpec((tm, tn), lambda i, j, kk: (i, j))
    if o_chunks:
        oc = n // N_DEV // tn
        o_spec = pl.BlockSpec((None, tm, tn), lambda i, j, kk: (j // oc, i, j % oc))
        o_shape = jax.ShapeDtypeStruct((N_DEV, m, n // N_DEV), out_dtype)
    else:
        o_spec, o_shape = mn_spec, jax.ShapeDtypeStruct((m, n), out_dtype)
    res = pl.pallas_call(
        body, grid=(m // tm, n // tn, nk), in_specs=[a_spec, b_spec] + [mn_spec] * n_extra,
        out_specs=[o_spec] * n_out, out_shape=[o_shape] * n_out,
        scratch_shapes=[pltpu.VMEM((tm, tn), F32)] if nk > 1 else [],
        compiler_params=_params("parallel", "parallel", "arbitrary"), name=name,
    )(a, b, *([extra] if n_extra else []))
    return res if n_out > 1 else res[0]


def _cast_bf16(v, *, name, tm=512):
    t, d = v.shape
    tm = min(tm, t)

    def body(v_ref, o_ref):
        o_ref[...] = v_ref[...].astype(BF16)

    spec = pl.BlockSpec((tm, d), lambda i: (i, 0))
    return pl.pallas_call(body, grid=(t // tm,), in_specs=[spec], out_specs=spec,
                          out_shape=jax.ShapeDtypeStruct((t, d), BF16), compiler_params=_params("parallel"), name=name)(v)


def _shift_down(v, s):
    if s == 0:
        return v
    row = lax.broadcasted_iota(jnp.int32, v.shape, 0)
    return jnp.where(row >= s, pltpu.roll(v, s, axis=0), 0.0)


def _shift_up(v, s):
    if s == 0:
        return v
    t = v.shape[0]
    row = lax.broadcasted_iota(jnp.int32, v.shape, 0)
    return jnp.where(row < t - s, pltpu.roll(v, t - s, axis=0), 0.0)


def _post_col(j):
    return (j % GDN_HEADS) * 3 + j // GDN_HEADS


def _gdn_prep_fwd(proj, conv_w):
    t = proj.shape[0]

    def body(x_ref, w_ref, o_ref):
        j = pl.program_id(0)
        x = x_ref[...]
        y = jnp.zeros_like(x)
        for tap in range(CONV_K):
            y = y + w_ref[tap:tap + 1, :] * _shift_down(x, CONV_K - 1 - tap)
        c = y * jax.nn.sigmoid(y)
        nrm = c * lax.rsqrt(jnp.sum(c * c, axis=1, keepdims=True) + NORM_EPS)
        o_ref[...] = jnp.where(j < 2 * GDN_HEADS, nrm, c)

    return pl.pallas_call(
        body, grid=(QKV_COLS // LANE,),
        in_specs=[pl.BlockSpec((t, LANE), lambda j: (0, j)), pl.BlockSpec((CONV_K, LANE), lambda j: (0, j))],
        out_specs=pl.BlockSpec((t, LANE), lambda j: (0, _post_col(j))),
        out_shape=jax.ShapeDtypeStruct((t, QKV_COLS), F32),
        compiler_params=_params("parallel"), name="gdn_prep_fwd",
    )(proj, conv_w)


def _gdn_prep_bwd(proj, conv_w, dpost, dproj):
    t = proj.shape[0]

    def body(x_ref, w_ref, d_ref, _, dx_ref, dw_ref):
        j = pl.program_id(0)
        x = x_ref[...]
        xs = [_shift_down(x, CONV_K - 1 - tap) for tap in range(CONV_K)]
        y = jnp.zeros_like(x)
        for tap in range(CONV_K):
            y = y + w_ref[tap:tap + 1, :] * xs[tap]
        sig = jax.nn.sigmoid(y)
        c = y * sig
        r = lax.rsqrt(jnp.sum(c * c, axis=1, keepdims=True) + NORM_EPS)
        nrm = c * r
        d = d_ref[...]
        dc_norm = r * (d - nrm * jnp.sum(d * nrm, axis=1, keepdims=True))
        dc = jnp.where(j < 2 * GDN_HEADS, dc_norm, d)
        dy = dc * (sig * (1.0 + y * (1.0 - sig)))
        dx = jnp.zeros_like(x)
        for tap in range(CONV_K):
            dx = dx + _shift_up(w_ref[tap:tap + 1, :] * dy, CONV_K - 1 - tap)
            dw_ref[tap:tap + 1, :] = jnp.sum(dy * xs[tap], axis=0, keepdims=True)
        dx_ref[...] = dx.astype(dx_ref.dtype)

    return pl.pallas_call(
        body, grid=(QKV_COLS // LANE,),
        in_specs=[pl.BlockSpec((t, LANE), lambda j: (0, j)), pl.BlockSpec((CONV_K, LANE), lambda j: (0, j)),
                  pl.BlockSpec((t, LANE), lambda j: (0, _post_col(j))), pl.BlockSpec(memory_space=pl.ANY)],
        out_specs=[pl.BlockSpec((t, LANE), lambda j: (0, j)), pl.BlockSpec((CONV_K, LANE), lambda j: (0, j))],
        out_shape=[jax.ShapeDtypeStruct(dproj.shape, dproj.dtype), jax.ShapeDtypeStruct((CONV_K, QKV_COLS), F32)],
        input_output_aliases={3: 0},
        compiler_params=_params("parallel"), name="gdn_prep_bwd",
    )(proj, conv_w, dpost, dproj)


def _softplus(v):
    return jnp.maximum(v, 0.0) + jnp.log(1.0 + jnp.exp(-jnp.abs(v)))


def _tri_inv(low, nn):
    r = lax.broadcasted_iota(jnp.int32, (CHUNK, CHUNK), 0)
    c = lax.broadcasted_iota(jnp.int32, (CHUNK, CHUNK), 1)
    eye = (r == c).astype(F32)
    same_blk = lax.shift_right_logical(r, 4) == lax.shift_right_logical(c, 4)
    diag = jnp.where(same_blk, low, 0.0)
    off = low - diag
    n1 = -diag
    n2 = nn(n1, n1)
    n4 = nn(n2, n2)
    n8 = nn(n4, n4)
    inv_d = nn(nn(nn(eye + n1, eye + n2), eye + n4), eye + n8)
    m1 = nn(inv_d, off)
    m2 = nn(m1, m1)
    return nn(nn(eye - m1, eye + m2), inv_d)


LOCAL_HEADS_PER_STEP = 8


def _gdn_local_fn(qkv, ba, alog_row, dtb_row, first_head, bdots, fdots):
    nn, nt, tn = bdots
    fnn = fdots[0]
    n_heads = qkv.shape[1] // (3 * HEAD_DIM)
    part = lambda i, p: qkv[:, (3 * i + p) * HEAD_DIM:(3 * i + p + 1) * HEAD_DIM]
    q = jnp.stack([part(i, 0) for i in range(n_heads)]) * (HEAD_DIM ** -0.5)
    k = jnp.stack([part(i, 1) for i in range(n_heads)])
    v = jnp.stack([part(i, 2) for i in range(n_heads)])
    lane = lax.broadcasted_iota(jnp.int32, ba.shape, 1)
    bg = jnp.where(lane < GDN_HEADS, jax.nn.sigmoid(ba), -jnp.exp(alog_row) * _softplus(ba + dtb_row))
    pick = lambda l: jnp.sum(jnp.where(lane == l, bg, 0.0), axis=1, keepdims=True)
    beta = jnp.stack([pick(first_head + i) for i in range(n_heads)])
    g = jnp.stack([pick(first_head + i + GDN_HEADS) for i in range(n_heads)])

    r = lax.broadcasted_iota(jnp.int32, (CHUNK, CHUNK), 0)
    c = lax.broadcasted_iota(jnp.int32, (CHUNK, CHUNK), 1)
    incl = r >= c
    strict = r > c
    eye = r == c

    def to_row(col):
        return jnp.sum(jnp.where(eye, col, 0.0), axis=1, keepdims=True)

    gc = jnp.sum(jnp.where(incl, to_row(g), 0.0), axis=2, keepdims=True)
    diff = gc - to_row(gc)
    decay = jnp.where(incl, jnp.exp(jnp.where(incl, diff, 0.0)), 0.0)
    k_beta = k * beta
    v_beta = v * beta
    low = jnp.where(strict, nt(k_beta, k) * decay, 0.0)
    t_inv = _tri_inv(low, fnn)
    eg = jnp.exp(gc)
    u = fnn(t_inv, v_beta)
    w = fnn(t_inv, k_beta * eg)
    attn = jnp.where(incl, nt(q, k) * decay, 0.0)
    last = lax.broadcasted_iota(jnp.int32, (CHUNK, 1), 0) == CHUNK - 1
    g_last = jnp.sum(jnp.where(last, gc, 0.0), axis=1, keepdims=True)
    kdec = k * jnp.exp(g_last - gc)
    elast = jnp.broadcast_to(jnp.exp(g_last), (n_heads, 1, LANE))
    return u, w, q * eg, kdec, attn, elast


def _gdn_state_fn(u, w, qg, kdec, attn, elast, state, bdots):
    nn, _, tn = bdots
    v_new = u - nn(w, state)
    o = nn(qg, state) + nn(attn, v_new)
    return o, state * elast + tn(kdec, v_new)


def _gdn_local_fwd(post, proj, alog_row, dtb_row):
    t = post.shape[0]
    n_chunks = t // CHUNK
    hb = LOCAL_HEADS_PER_STEP

    def body(qkv_ref, ba_ref, al_ref, dt_ref, u_ref, w_ref, qg_ref, kd_ref, at_ref, el_ref):
        u, w, qg, kdec, attn, elast = _gdn_local_fn(qkv_ref[...], ba_ref[...], al_ref[...], dt_ref[...],
                                                    pl.program_id(1) * hb, _BDOT_BATCH_PLAIN, _FDOT_BATCH_PLAIN)
        for i in range(hb):
            cols = slice(i * HEAD_DIM, (i + 1) * HEAD_DIM)
            u_ref[:, cols] = u[i]
            w_ref[:, cols] = w[i].astype(BF16)
            qg_ref[:, cols] = qg[i].astype(BF16)
            kd_ref[:, cols] = kdec[i].astype(BF16)
        at_ref[...] = attn.astype(BF16)
        el_ref[:, 0] = elast

    wide = pl.BlockSpec((CHUNK, hb * HEAD_DIM), lambda n, j: (n, j))
    row = pl.BlockSpec((1, LANE), lambda n, j: (0, 0))
    return pl.pallas_call(
        body, grid=(n_chunks, GDN_HEADS // hb),
        in_specs=[pl.BlockSpec((CHUNK, hb * 3 * HEAD_DIM), lambda n, j: (n, j)),
                  pl.BlockSpec((CHUNK, LANE), lambda n, j: (n, BA_BLK)), row, row],
        out_specs=[wide, wide, wide, wide, pl.BlockSpec((hb, CHUNK, CHUNK), lambda n, j: (j, n, 0)),
                   pl.BlockSpec((hb, 1, 1, LANE), lambda n, j: (j, n, 0, 0))],
        out_shape=[jax.ShapeDtypeStruct((t, GDN_WIDTH), F32), jax.ShapeDtypeStruct((t, GDN_WIDTH), BF16),
                   jax.ShapeDtypeStruct((t, GDN_WIDTH), BF16), jax.ShapeDtypeStruct((t, GDN_WIDTH), BF16),
                   jax.ShapeDtypeStruct((GDN_HEADS, t, CHUNK), BF16),
                   jax.ShapeDtypeStruct((GDN_HEADS, n_chunks, 1, LANE), F32)],
        compiler_params=_params("parallel", "parallel"), name="gdn_local_fwd",
    )(post, proj, alog_row, dtb_row)


def _gdn_state_specs(n_of):
    wide = pl.BlockSpec((CHUNK, GDN_WIDTH), lambda n: (n_of(n), 0))
    attn = pl.BlockSpec((GDN_HEADS, CHUNK, CHUNK), lambda n: (0, n_of(n), 0))
    elast = pl.BlockSpec((GDN_HEADS, 1, 1, LANE), lambda n: (0, n_of(n), 0, 0))
    saved = pl.BlockSpec((GDN_HEADS, 1, HEAD_DIM, HEAD_DIM), lambda n: (0, n_of(n), 0, 0))
    return wide, attn, elast, saved


def _gdn_state_fwd(u, w, qg, kdec, attn, elast):
    t = u.shape[0]
    n_chunks = t // CHUNK

    def body(u_ref, w_ref, qg_ref, kd_ref, at_ref, el_ref, o_ref, save_ref, state_ref):
        @pl.when(pl.program_id(0) == 0)
        def _():
            state_ref[...] = jnp.zeros_like(state_ref)

        for h in range(GDN_HEADS):
            cols = slice(h * HEAD_DIM, (h + 1) * HEAD_DIM)
            state = state_ref[h]
            save_ref[h, 0] = state
            o, new_state = _gdn_state_fn(u_ref[:, cols], w_ref[:, cols], qg_ref[:, cols], kd_ref[:, cols], at_ref[h],
                                         el_ref[h, 0], state, _BDOT_PLAIN)
            o_ref[:, cols] = o
            state_ref[h] = new_state

    wide, attn_spec, elast_spec, saved_spec = _gdn_state_specs(lambda n: n)
    return pl.pallas_call(
        body, grid=(n_chunks,), in_specs=[wide, wide, wide, wide, attn_spec, elast_spec],
        out_specs=[wide, saved_spec],
        out_shape=[jax.ShapeDtypeStruct((t, GDN_WIDTH), F32),
                   jax.ShapeDtypeStruct((GDN_HEADS, n_chunks, HEAD_DIM, HEAD_DIM), F32)],
        scratch_shapes=[pltpu.VMEM((GDN_HEADS, HEAD_DIM, HEAD_DIM), F32)],
        compiler_params=_params("arbitrary"), name="gdn_state_fwd",
    )(u, w, qg, kdec, attn, elast)


def _gdn_state_bwd(u, w, qg, kdec, attn, elast, saved, do):
    t = u.shape[0]
    n_chunks = t // CHUNK
    last = n_chunks - 1

    def body(u_ref, w_ref, qg_ref, kd_ref, at_ref, el_ref, save_ref, do_ref,
             du_ref, dw_ref, dqg_ref, dkd_ref, dat_ref, del_ref, dstate_ref):
        @pl.when(pl.program_id(0) == 0)
        def _():
            dstate_ref[...] = jnp.zeros_like(dstate_ref)

        for h in range(GDN_HEADS):
            cols = slice(h * HEAD_DIM, (h + 1) * HEAD_DIM)
            _, vjp = jax.vjp(
                lambda *a: _gdn_state_fn(*a, _BDOT_VJP), u_ref[:, cols], w_ref[:, cols].astype(F32),
                qg_ref[:, cols].astype(F32), kd_ref[:, cols].astype(F32), at_ref[h].astype(F32), el_ref[h, 0], save_ref[h, 0])
            du, dw, dqg, dkd, dat, de, dstate = vjp((do_ref[:, cols], dstate_ref[h]))
            du_ref[:, cols] = du
            dw_ref[:, cols] = dw
            dqg_ref[:, cols] = dqg
            dkd_ref[:, cols] = dkd
            dat_ref[h] = dat
            del_ref[h, 0] = de
            dstate_ref[h] = dstate

    wide, attn_spec, elast_spec, saved_spec = _gdn_state_specs(lambda n: last - n)
    wide_f32 = jax.ShapeDtypeStruct((t, GDN_WIDTH), F32)
    return pl.pallas_call(
        body, grid=(n_chunks,), in_specs=[wide, wide, wide, wide, attn_spec, elast_spec, saved_spec, wide],
        out_specs=[wide, wide, wide, wide, attn_spec, elast_spec],
        out_shape=[wide_f32, wide_f32, wide_f32, wide_f32, jax.ShapeDtypeStruct((GDN_HEADS, t, CHUNK), F32),
                   jax.ShapeDtypeStruct((GDN_HEADS, n_chunks, 1, LANE), F32)],
        scratch_shapes=[pltpu.VMEM((GDN_HEADS, HEAD_DIM, HEAD_DIM), F32)],
        compiler_params=_params("arbitrary"), name="gdn_state_bwd",
    )(u, w, qg, kdec, attn, elast, saved, do)


def _gdn_local_bwd(post, proj, alog_row, dtb_row, cots, dproj):
    t = post.shape[0]
    n_chunks = t // CHUNK
    hb = LOCAL_HEADS_PER_STEP
    n_steps = GDN_HEADS // hb

    def body(qkv_ref, ba_ref, al_ref, dt_ref, du_ref, dw_ref, dqg_ref, dkd_ref, dat_ref, del_ref, _,
             dqkv_ref, dba_ref, dal_ref, ddt_ref, dba_acc):
        n = pl.program_id(0)
        j = pl.program_id(1)

        @pl.when((n == 0) & (j == 0))
        def _():
            dal_ref[...] = jnp.zeros_like(dal_ref)
            ddt_ref[...] = jnp.zeros_like(ddt_ref)

        @pl.when(j == 0)
        def _():
            dba_acc[...] = jnp.zeros_like(dba_acc)

        heads = lambda ref: jnp.stack([ref[:, i * HEAD_DIM:(i + 1) * HEAD_DIM] for i in range(hb)])
        _, vjp = jax.vjp(lambda a, b, c, d: _gdn_local_fn(a, b, c, d, j * hb, _BDOT_BATCH_VJP, _FDOT_BATCH_VJP),
                         qkv_ref[...], ba_ref[...], al_ref[...], dt_ref[...])
        dqkv, dba, dal, ddt = vjp((heads(du_ref), heads(dw_ref), heads(dqg_ref), heads(dkd_ref), dat_ref[...],
                                   del_ref[:, 0]))
        dqkv_ref[...] = dqkv
        dba_acc[...] += dba
        dal_ref[...] += dal
        ddt_ref[...] += ddt

        @pl.when(j == n_steps - 1)
        def _():
            dba_ref[:, 0:LANE] = dba_acc[...].astype(dba_ref.dtype)
            dba_ref[:, LANE:2 * LANE] = jnp.zeros((CHUNK, LANE), dba_ref.dtype)

    wide = pl.BlockSpec((CHUNK, hb * HEAD_DIM), lambda n, j: (n, j))
    qkv_spec = pl.BlockSpec((CHUNK, hb * 3 * HEAD_DIM), lambda n, j: (n, j))
    row = pl.BlockSpec((1, LANE), lambda n, j: (0, 0))
    return pl.pallas_call(
        body, grid=(n_chunks, n_steps),
        in_specs=[qkv_spec, pl.BlockSpec((CHUNK, LANE), lambda n, j: (n, BA_BLK)), row, row, wide, wide, wide, wide,
                  pl.BlockSpec((hb, CHUNK, CHUNK), lambda n, j: (j, n, 0)),
                  pl.BlockSpec((hb, 1, 1, LANE), lambda n, j: (j, n, 0, 0)), pl.BlockSpec(memory_space=pl.ANY)],
        out_specs=[qkv_spec, pl.BlockSpec((CHUNK, 2 * LANE), lambda n, j: (n, BA_BLK // 2)), row, row],
        out_shape=[jax.ShapeDtypeStruct((t, QKV_COLS), F32), jax.ShapeDtypeStruct(dproj.shape, dproj.dtype),
                   jax.ShapeDtypeStruct((1, LANE), F32), jax.ShapeDtypeStruct((1, LANE), F32)],
        input_output_aliases={10: 1},
        scratch_shapes=[pltpu.VMEM((CHUNK, LANE), F32)],
        compiler_params=_params("arbitrary", "arbitrary"), name="gdn_local_bwd",
    )(post, proj, alog_row, dtb_row, *cots, dproj)


def _onorm_fn(o, z, w):
    return o * lax.rsqrt(jnp.mean(o * o, axis=1, keepdims=True) + NORM_EPS) * w * (z * jax.nn.sigmoid(z))


def _onorm_fwd(o_raw, proj, norm_w, mixin, tm=512):
    t = o_raw.shape[0]
    tm = min(tm, t)

    def body(o_ref, z_ref, w_ref, _, out_ref):
        out_ref[...] = _onorm_fn(o_ref[...], z_ref[...], w_ref[...]).astype(out_ref.dtype)

    return pl.pallas_call(
        body, grid=(t // tm, GDN_HEADS),
        in_specs=[pl.BlockSpec((tm, LANE), lambda i, h: (i, h)), pl.BlockSpec((tm, LANE), lambda i, h: (i, Z_BLK + h)),
                  pl.BlockSpec((1, LANE), lambda i, h: (0, 0)), pl.BlockSpec(memory_space=pl.ANY)],
        out_specs=pl.BlockSpec((tm, LANE), lambda i, h: (i, h)),
        out_shape=jax.ShapeDtypeStruct(mixin.shape, mixin.dtype), input_output_aliases={3: 0},
        compiler_params=_params("parallel", "parallel"), name="gdn_onorm_fwd",
    )(o_raw, proj, norm_w, mixin)


def _onorm_bwd(o_raw, proj, norm_w, dmixin, dproj, tm=512):
    t = o_raw.shape[0]
    tm = min(tm, t)

    def body(o_ref, z_ref, w_ref, d_ref, _, do_ref, dz_ref, dw_ref):
        @pl.when((pl.program_id(0) == 0) & (pl.program_id(1) == 0))
        def _():
            dw_ref[...] = jnp.zeros_like(dw_ref)

        _, vjp = jax.vjp(_onorm_fn, o_ref[...], z_ref[...], w_ref[...])
        do, dz, dw = vjp(d_ref[...])
        do_ref[...] = do
        dz_ref[...] = dz.astype(dz_ref.dtype)
        dw_ref[...] += dw

    return pl.pallas_call(
        body, grid=(t // tm, GDN_HEADS),
        in_specs=[pl.BlockSpec((tm, LANE), lambda i, h: (i, h)), pl.BlockSpec((tm, LANE), lambda i, h: (i, Z_BLK + h)),
                  pl.BlockSpec((1, LANE), lambda i, h: (0, 0)), pl.BlockSpec((tm, LANE), lambda i, h: (i, h)),
                  pl.BlockSpec(memory_space=pl.ANY)],
        out_specs=[pl.BlockSpec((tm, LANE), lambda i, h: (i, h)), pl.BlockSpec((tm, LANE), lambda i, h: (i, Z_BLK + h)),
                   pl.BlockSpec((1, LANE), lambda i, h: (0, 0))],
        out_shape=[jax.ShapeDtypeStruct((t, GDN_WIDTH), F32), jax.ShapeDtypeStruct(dproj.shape, dproj.dtype),
                   jax.ShapeDtypeStruct((1, LANE), F32)],
        input_output_aliases={4: 1},
        compiler_params=_params("arbitrary", "arbitrary"), name="gdn_onorm_bwd",
    )(o_raw, proj, norm_w, dmixin, dproj)


def _pool_select(levels, gi):
    out = levels[-1]
    for lvl in range(len(levels) - 2, -1, -1):
        out = jnp.where(gi == lvl, levels[lvl], out)
    return out


def _pool_count(shape, gi):
    pos = lax.broadcasted_iota(jnp.int32, shape, 0)
    win = lax.shift_left(jnp.int32(2), gi)
    return jnp.minimum(pos + 1, win).astype(F32)


def _pooled(p, gi):
    acc = p
    levels = []
    for lvl in range(POOL_GROUPS):
        acc = acc + _shift_down(acc, 1 << lvl)
        levels.append(acc)
    return _pool_select(levels, gi) / _pool_count(p.shape, gi) - p


def _pool_fwd(proj, pool_w, pool_scale):
    t = proj.shape[0]

    def body(p_ref, w_ref, s_ref, out_ref):
        gi = pl.program_id(0)
        pooled = _pooled(p_ref[...], gi)
        out_ref[...] = (_BDOT_PLAIN[0](pooled, w_ref[0]) * s_ref[0]).astype(out_ref.dtype)

    return pl.pallas_call(
        body, grid=(POOL_GROUPS,),
        in_specs=[pl.BlockSpec((t, POOL_GROUP_DIM), lambda g: (0, POOL_BLK + g)),
                  pl.BlockSpec((1, POOL_GROUP_DIM, POOL_GROUP_DIM), lambda g: (g, 0, 0)),
                  pl.BlockSpec((1, 1, POOL_GROUP_DIM), lambda g: (g, 0, 0))],
        out_specs=pl.BlockSpec((t, POOL_GROUP_DIM), lambda g: (0, GDN_WIDTH // POOL_GROUP_DIM + g)),
        out_shape=jax.ShapeDtypeStruct((t, 2 * GDN_WIDTH), BF16),
        compiler_params=_params("parallel"), name="pool_fwd",
    )(proj, pool_w, pool_scale)


def _pool_bwd(proj, pool_w, pool_scale, dmixin):
    t = proj.shape[0]
    nn, nt, tn = _BDOT_PLAIN

    def body(p_ref, w_ref, s_ref, d_ref, dp_ref, dw_ref, ds_ref):
        gi = pl.program_id(0)
        p = p_ref[...]
        pooled = _pooled(p, gi)
        mixed = nn(pooled, w_ref[0])
        d = d_ref[...]
        ds_ref[0] = jnp.sum(d * mixed, axis=0, keepdims=True)
        dmixed = d * s_ref[0]
        dw_ref[0] = tn(pooled, dmixed)
        dpooled = nt(dmixed, w_ref[0])
        acc = dpooled / _pool_count(p.shape, gi)
        levels = []
        for lvl in range(POOL_GROUPS):
            acc = acc + _shift_up(acc, 1 << lvl)
            levels.append(acc)
        dp_ref[...] = (_pool_select(levels, gi) - dpooled).astype(dp_ref.dtype)

    return pl.pallas_call(
        body, grid=(POOL_GROUPS,),
        in_specs=[pl.BlockSpec((t, POOL_GROUP_DIM), lambda g: (0, POOL_BLK + g)),
                  pl.BlockSpec((1, POOL_GROUP_DIM, POOL_GROUP_DIM), lambda g: (g, 0, 0)),
                  pl.BlockSpec((1, 1, POOL_GROUP_DIM), lambda g: (g, 0, 0)),
                  pl.BlockSpec((t, POOL_GROUP_DIM), lambda g: (0, GDN_WIDTH // POOL_GROUP_DIM + g))],
        out_specs=[pl.BlockSpec((t, POOL_GROUP_DIM), lambda g: (0, POOL_BLK + g)),
                   pl.BlockSpec((1, POOL_GROUP_DIM, POOL_GROUP_DIM), lambda g: (g, 0, 0)),
                   pl.BlockSpec((1, 1, POOL_GROUP_DIM), lambda g: (g, 0, 0))],
        out_shape=[jax.ShapeDtypeStruct((t, PROJ_COLS), BF16),
                   jax.ShapeDtypeStruct((POOL_GROUPS, POOL_GROUP_DIM, POOL_GROUP_DIM), F32),
                   jax.ShapeDtypeStruct((POOL_GROUPS, 1, POOL_GROUP_DIM), F32)],
        compiler_params=_params("parallel"), name="pool_bwd",
    )(proj, pool_w, pool_scale, dmixin)


def _ln_stats(s):
    mu = jnp.mean(s, axis=1, keepdims=True)
    xc = s - mu
    var = jnp.mean(xc * xc, axis=1, keepdims=True)
    rstd = lax.rsqrt(var + LN_EPS)
    return xc * rstd, rstd


def _ln_fwd(h_in, y, g, b, *, name, tm=256):
    t, d = h_in.shape
    tm = min(tm, t)

    def body(h_ref, y_ref, g_ref, b_ref, o_ref, o16_ref):
        xhat, _ = _ln_stats(ALPHA * h_ref[...] + y_ref[...])
        out = xhat * g_ref[...] + b_ref[...]
        o_ref[...] = out
        o16_ref[...] = out.astype(BF16)

    row = pl.BlockSpec((tm, d), lambda i: (i, 0))
    vec = pl.BlockSpec((1, d), lambda i: (0, 0))
    return pl.pallas_call(
        body, grid=(t // tm,), in_specs=[row, row, vec, vec], out_specs=[row, row],
        out_shape=[jax.ShapeDtypeStruct((t, d), F32), jax.ShapeDtypeStruct((t, d), BF16)],
        compiler_params=_params("parallel"), name=name,
    )(h_in, y, g, b)


def _ln_loss_fwd(h_in, y, g, b, target, *, name, tm=256):
    t, d = h_in.shape
    tm = min(tm, t)

    def body(h_ref, y_ref, g_ref, b_ref, t_ref, dy_ref, sq_ref):
        @pl.when(pl.program_id(0) == 0)
        def _():
            sq_ref[...] = jnp.zeros_like(sq_ref)

        xhat, _ = _ln_stats(ALPHA * h_ref[...] + y_ref[...])
        err = xhat * g_ref[...] + b_ref[...] - t_ref[...]
        dy_ref[...] = err * (1.0 / d)
        sq_ref[...] += jnp.sum(jnp.sum(err * err, axis=1, keepdims=True), axis=0, keepdims=True)

    row = pl.BlockSpec((tm, d), lambda i: (i, 0))
    vec = pl.BlockSpec((1, d), lambda i: (0, 0))
    return pl.pallas_call(
        body, grid=(t // tm,), in_specs=[row, row, vec, vec, row],
        out_specs=[row, pl.BlockSpec((1, LANE), lambda i: (0, 0))],
        out_shape=[jax.ShapeDtypeStruct((t, d), F32), jax.ShapeDtypeStruct((1, LANE), F32)],
        compiler_params=_params("arbitrary"), name=name,
    )(h_in, y, g, b, target)


def _ln_bwd(h_in, y, g, d_a, d_b, *, name, tm=256):
    t, d = h_in.shape
    tm = min(tm, t)
    has_b = d_b is not None

    def body(*refs):
        if has_b:
            h_ref, y_ref, g_ref, da_ref, db_ref, ds_ref, ds16_ref, dg_ref, dbias_ref = refs
        else:
            h_ref, y_ref, g_ref, da_ref, ds_ref, ds16_ref, dg_ref, dbias_ref = refs

        @pl.when(pl.program_id(0) == 0)
        def _():
            dg_ref[...] = jnp.zeros_like(dg_ref)
            dbias_ref[...] = jnp.zeros_like(dbias_ref)

        xhat, rstd = _ln_stats(ALPHA * h_ref[...] + y_ref[...])
        dout = da_ref[...]
        if has_b:
            dout = dout + ALPHA * db_ref[...]
        dxhat = dout * g_ref[...]
        m1 = jnp.mean(dxhat, axis=1, keepdims=True)
        m2 = jnp.mean(dxhat * xhat, axis=1, keepdims=True)
        ds = rstd * (dxhat - m1 - xhat * m2)
        ds_ref[...] = ds
        ds16_ref[...] = ds.astype(BF16)
        dg_ref[...] += jnp.sum(dout * xhat, axis=0, keepdims=True)
        dbias_ref[...] += jnp.sum(dout, axis=0, keepdims=True)

    row = pl.BlockSpec((tm, d), lambda i: (i, 0))
    vec = pl.BlockSpec((1, d), lambda i: (0, 0))
    args = [h_in, y, g, d_a] + ([d_b] if has_b else [])
    return pl.pallas_call(
        body, grid=(t // tm,), in_specs=[row, row, vec, row] + ([row] if has_b else []),
        out_specs=[row, row, vec, vec],
        out_shape=[jax.ShapeDtypeStruct((t, d), F32), jax.ShapeDtypeStruct((t, d), BF16),
                   jax.ShapeDtypeStruct((1, d), F32), jax.ShapeDtypeStruct((1, d), F32)],
        compiler_params=_params("arbitrary"), name=name,
    )(*args)


def _attn_fn(q, k, v, dots):
    nn, nt, _ = dots
    s = nt(q, k) * (XATTN_HEAD_DIM ** -0.5)
    s = s - lax.stop_gradient(jnp.max(s, axis=1, keepdims=True))
    e = jnp.exp(s)
    p = e / jnp.sum(e, axis=1, keepdims=True)
    return nn(p, v)


def _attn_fwd(q, k, v, tq=512):
    t = q.shape[0]
    tq = min(tq, t)

    def body(q_ref, k_ref, v_ref, o_ref):
        o_ref[...] = _attn_fn(q_ref[...], k_ref[...], v_ref[...], _BDOT_PLAIN).astype(BF16)

    qs = pl.BlockSpec((tq, XATTN_HEAD_DIM), lambda h, i: (i, h))
    ks = pl.BlockSpec((MEM_LEN, XATTN_HEAD_DIM), lambda h, i: (0, h))
    return pl.pallas_call(
        body, grid=(XATTN_HEADS, t // tq), in_specs=[qs, ks, ks], out_specs=qs,
        out_shape=jax.ShapeDtypeStruct(q.shape, BF16), compiler_params=_params("parallel", "parallel"), name="xattn_fwd",
    )(q, k, v)


def _attn_bwd(q, k, v, do, tq=512):
    t = q.shape[0]
    tq = min(tq, t)

    def body(q_ref, k_ref, v_ref, do_ref, dq_ref, dk_ref, dv_ref):
        @pl.when(pl.program_id(1) == 0)
        def _():
            dk_ref[...] = jnp.zeros_like(dk_ref)
            dv_ref[...] = jnp.zeros_like(dv_ref)

        _, vjp = jax.vjp(lambda a, b, c: _attn_fn(a, b, c, _BDOT_VJP), q_ref[...].astype(F32), k_ref[...].astype(F32),
                         v_ref[...].astype(F32))
        dq, dk, dv = vjp(do_ref[...].astype(F32))
        dq_ref[...] = dq.astype(BF16)
        dk_ref[...] += dk
        dv_ref[...] += dv

    qs = pl.BlockSpec((tq, XATTN_HEAD_DIM), lambda h, i: (i, h))
    ks = pl.BlockSpec((MEM_LEN, XATTN_HEAD_DIM), lambda h, i: (0, h))
    return pl.pallas_call(
        body, grid=(XATTN_HEADS, t // tq), in_specs=[qs, ks, ks, qs], out_specs=[qs, ks, ks],
        out_shape=[jax.ShapeDtypeStruct(q.shape, BF16), jax.ShapeDtypeStruct(k.shape, F32), jax.ShapeDtypeStruct(v.shape, F32)],
        compiler_params=_params("parallel", "arbitrary"), name="xattn_bwd",
    )(q, k, v, do)


def _local_step(x, mem, target, weights_of, grads_ready):
    def behind(vec, token):
        return vec if token is None else vec + token

    x16 = _cast_bf16(x, name="cast_x")
    w = dict(weights_of("mixer", None))
    proj = _mm(x16, w["w_in"], tn=768, name="mm_in_proj")
    post = _gdn_prep_fwd(proj, w["conv_w"])
    chunked = _gdn_local_fwd(post, proj, w["alog_row"], w["dtb_row"])
    o_raw, saved = _gdn_state_fwd(*chunked)
    mixin = _pool_fwd(proj, w["pool_w"], w["pool_scale"])
    mixin = _onorm_fwd(o_raw, proj, w["gdn_norm_w"], mixin)
    w.update(weights_of("attn", mixin))
    mix = _mm(mixin, w["w_out"], name="mm_out_proj")
    h1, h1_16 = _ln_fwd(x, mix, w["ln1_g"], w["ln1_b"], name="ln1_fwd")
    xq = _mm(h1_16, w["xq_w"], out_dtype=BF16, name="mm_xq")
    xk = _mm(mem, w["xk_w"], out_dtype=BF16, name="mm_xk")
    xv = _mm(mem, w["xv_w"], out_dtype=BF16, name="mm_xv")
    xo = _attn_fwd(xq, xk, xv)
    xa = _mm(xo, w["xo_w"], name="mm_xo")
    h2, h2_16 = _ln_fwd(h1, xa, w["ln2_g"], w["ln2_b"], name="ln2_fwd")
    w.update(weights_of("mlp", h2_16))
    act, relu = _mm(h2_16, w["w_up"], b_chunks=True, epi="relu2", name="mm_up")
    ff = _mm(act, w["w_down"], tn=1024, tk=512, name="mm_down")
    dy, sq = _ln_loss_fwd(h2, ff, w["ln3_g"], w["ln3_b"], target, name="ln3_loss_fwd")

    g = {}
    ds3, ds3_16, g["ln3_g"], g["ln3_b"] = _ln_bwd(h2, ff, w["ln3_g"], dy, None, name="ln3_bwd")
    gw_down = _mm(act, ds3_16, ta=True, out_dtype=BF16, tm=512, tn=D_MODEL, name="mm_gw_down")
    du = _mm(ds3_16, w["w_down"], tb=True, epi="mul2r", extra=relu, name="mm_du")
    gw_up = _mm(h2_16, du, ta=True, out_dtype=BF16, o_chunks=True, name="mm_gw_up")
    token = grads_ready("mlp", {"w_down": gw_down, "w_up": gw_up})
    dh2 = _mm(du, w["w_up"], tb=True, b_chunks=True, tn=1024, tk=512, name="mm_dh2")
    ds2, ds2_16, g["ln2_g"], g["ln2_b"] = _ln_bwd(h1, xa, behind(w["ln2_g"], token), dh2, ds3, name="ln2_bwd")
    gw_xo = _mm(xo, ds2_16, ta=True, out_dtype=BF16, name="mm_gw_xo")
    dxo = _mm(ds2_16, w["xo_w"], tb=True, out_dtype=BF16, name="mm_dxo")
    dxq, dxk, dxv = _attn_bwd(xq, xk, xv, dxo)
    gw_xq = _mm(h1_16, dxq, ta=True, out_dtype=BF16, name="mm_gw_xq")
    gw_xk = _mm(mem, dxk, ta=True, out_dtype=BF16, name="mm_gw_xk")
    gw_xv = _mm(mem, dxv, ta=True, out_dtype=BF16, name="mm_gw_xv")
    token = grads_ready("attn", {"xo_w": gw_xo, "xq_w": gw_xq, "xk_w": gw_xk, "xv_w": gw_xv})
    dh1 = _mm(dxq, w["xq_w"], tb=True, name="mm_dh1")
    ds1, ds1_16, g["ln1_g"], g["ln1_b"] = _ln_bwd(x, mix, behind(w["ln1_g"], token), dh1, ds2, name="ln1_bwd")
    gw_out = _mm(mixin, ds1_16, ta=True, out_dtype=BF16, name="mm_gw_out")
    dmixin = _mm(ds1_16, w["w_out"], tb=True, name="mm_dmixin")
    dproj, gw_pool, g["pool_scale"] = _pool_bwd(proj, w["pool_w"], w["pool_scale"], dmixin)
    token = grads_ready("mix", {"w_out": gw_out, "pool_w": gw_pool})
    do_raw, dproj, g["gdn_norm_w"] = _onorm_bwd(o_raw, proj, behind(w["gdn_norm_w"], token), dmixin, dproj)
    cots = _gdn_state_bwd(*chunked, saved, do_raw)
    dpost, dproj, g["alog_row"], g["dtb_row"] = _gdn_local_bwd(post, proj, w["alog_row"], w["dtb_row"], cots, dproj)
    dproj, g["conv_w"] = _gdn_prep_bwd(proj, w["conv_w"], dpost, dproj)
    gw_in = _mm(x16, dproj, ta=True, out_dtype=BF16, tn=768, name="mm_gw_in")
    grads_ready("in", {"w_in": gw_in})
    grad_x = _mm(dproj, w["w_in"], tb=True, tk=768, epi="add", extra=ds1, add_scale=ALPHA, name="mm_dx")
    return sq, grad_x, g


_MATRICES = ("w_in", "pool_w", "w_out", "xq_w", "xk_w", "xv_w", "xo_w", "w_up", "w_down")
_VECTORS = ("a_log", "dt_bias", "gdn_norm_w", "pool_scale", "ln1_g", "ln1_b", "ln2_g", "ln2_b", "ln3_g", "ln3_b")
_BA_SPLIT = BA_OFF + 2 * GDN_HEADS


def _lane_row(v, offset):
    return jnp.zeros((1, LANE), F32).at[0, offset:offset + v.shape[0]].set(v)


_GROUP_VECTORS = {"mixer": (), "attn": ("ln1_g", "ln1_b", "ln2_g", "ln2_b"), "mlp": ("ln3_g", "ln3_b")}


def _group_weights(group, full):
    w = {n: full[n].reshape(1, D_MODEL) for n in _GROUP_VECTORS[group]}
    if group == "mixer":
        w_in = full["w_in"]
        zeros = jnp.zeros((w_in.shape[0], POOL_OFF - _BA_SPLIT), w_in.dtype)
        w.update({
            "w_in": jnp.concatenate([w_in[:, :_BA_SPLIT], zeros, w_in[:, _BA_SPLIT:]], axis=1),
            "conv_w": full["conv_w"],
            "alog_row": _lane_row(full["a_log"], GDN_HEADS),
            "dtb_row": _lane_row(full["dt_bias"], GDN_HEADS),
            "gdn_norm_w": full["gdn_norm_w"].reshape(1, LANE),
            "pool_w": full["pool_w"],
            "pool_scale": full["pool_scale"].reshape(POOL_GROUPS, 1, POOL_GROUP_DIM),
        })
    elif group == "attn":
        w.update({n: full[n] for n in ("w_out", "xq_w", "xk_w", "xv_w", "xo_w")})
    else:
        w.update({n: full[n] for n in ("w_up", "w_down")})
    return w


def _unpad_w_in(g):
    return jnp.concatenate([g[:, :_BA_SPLIT], g[:, POOL_OFF:]], axis=1)


def _finish_small_grads(g):
    out = {"conv_w": g["conv_w"]}
    out["a_log"] = g["alog_row"][0, GDN_HEADS:2 * GDN_HEADS]
    out["dt_bias"] = g["dtb_row"][0, GDN_HEADS:2 * GDN_HEADS]
    out["gdn_norm_w"] = g["gdn_norm_w"].reshape(LANE)
    out["pool_scale"] = g["pool_scale"].reshape(POOL_GROUPS * POOL_GROUP_DIM)
    for n in ("ln1_g", "ln1_b", "ln2_g", "ln2_b", "ln3_g", "ln3_b"):
        out[n] = g[n].reshape(D_MODEL)
    return out


def _adamw_math(w, g, m, v):
    m = ADAM_B1 * m + (1.0 - ADAM_B1) * g
    v = ADAM_B2 * v + (1.0 - ADAM_B2) * (g * g)
    m_hat = m / (1.0 - ADAM_B1 ** ADAM_STEP)
    v_hat = v / (1.0 - ADAM_B2 ** ADAM_STEP)
    delta = -ADAM_LR * (m_hat / (jnp.sqrt(v_hat) + ADAM_EPS) + ADAM_WD * w)
    return delta, m, v


def _adamw_shard(parts, own, me, w, m, v, *, tr, name):
    s, r, c = parts.shape
    tr = min(tr, r)
    assert r % tr == 0, (name, r, tr)

    def body(me_ref, p_ref, own_ref, w_ref, m_ref, v_ref, g_ref, d_ref, nm_ref, nv_ref):
        mine = own_ref[...].astype(F32)
        g = None
        for i in range(s):
            part = jnp.where(me_ref[0] == i, mine, p_ref[i].astype(F32))
            g = part if g is None else g + part
        delta, nm, nv = _adamw_math(w_ref[...], g, m_ref[...], v_ref[...])
        g_ref[...] = g
        d_ref[...] = delta
        nm_ref[...] = nm
        nv_ref[...] = nv

    blk = pl.BlockSpec((tr, c), lambda i, me_ref: (i, 0))
    out = jax.ShapeDtypeStruct((r, c), F32)
    return pl.pallas_call(
        body,
        grid_spec=pltpu.PrefetchScalarGridSpec(
            num_scalar_prefetch=1, grid=(r // tr,),
            in_specs=[pl.BlockSpec((s, tr, c), lambda i, me_ref: (0, i, 0)),
                      pl.BlockSpec((None, tr, c), lambda i, me_ref: (me_ref[0], i, 0)), blk, blk, blk],
            out_specs=[blk, blk, blk, blk]),
        out_shape=[out, out, out, out], compiler_params=_params("parallel"), name=name,
    )(me, parts, own, w, m, v)


def _place():
    return lax.axis_index("x"), lax.axis_index("y"), lax.axis_index("c")


def _slot(px, py, pc):
    return 4 * px + 2 * py + pc


_HBM = pl.BlockSpec(memory_space=pltpu.HBM)


_SEM = pl.BlockSpec(memory_space=pltpu.SEMAPHORE)
_ANY = pl.BlockSpec(memory_space=pl.ANY)
_EFFECT = pltpu.SideEffectType.DATAFLOW_SIDE_EFFECTING
_N_PEERS = N_DEV - 1


def _peer(k, x, y, c):
    return (1 - x if k & 4 else x, 1 - y if k & 2 else y, 1 - c if k & 1 else c)


_EXCHANGE_BITS = {"scatter": (1, 2, 3, 4, 5, 6, 7), "gather_chips": (1, 2, 4, 6), "gather_pass": (2, 4, 6)}


def _exchange_copy(mode, src, land, w, i, place, send_sems, recv_sems, receiving):
    bits = _EXCHANGE_BITS[mode]
    k = bits[i]
    peer = _peer(k, *place)
    me = _slot(*place)
    if mode == "scatter":
        to, src_ref, sent_to, got_at = peer, src[w].at[_slot(*peer)], me, _slot(*peer)
    elif mode == "gather_chips":
        to, src_ref, sent_to, got_at = peer, src[w], me, _slot(*peer)
    else:
        blk = _slot(*peer)
        to, src_ref, sent_to, got_at = _peer(1, *place), land[w].at[blk], blk, _slot(*_peer(k | 1, *place))
    sem = w * len(bits) + i
    return pltpu.make_async_remote_copy(
        src_ref=src_ref, dst_ref=land[w].at[got_at if receiving else sent_to], send_sem=send_sems.at[sem],
        recv_sem=recv_sems.at[sem], device_id=to, device_id_type=MESH)


def _exchange_start(mode, srcs, lands, after, *, name):
    ns, nl = len(srcs), len(lands)
    n_sem = nl * len(_EXCHANGE_BITS[mode])

    def body(*refs):
        src, land = refs[:ns], refs[ns:ns + nl]
        send_sems, recv_sems = refs[ns + nl + 1:ns + nl + 3]
        token = refs[-1]
        place = _place()
        for w in range(nl):
            for i in range(len(_EXCHANGE_BITS[mode])):
                _exchange_copy(mode, src, land, w, i, place, send_sems, recv_sems, receiving=False).start()
        token[...] = jnp.zeros_like(token)

    sems = pltpu.SemaphoreType.DMA((n_sem,))
    arrays = list(srcs) + list(lands)
    res = pl.pallas_call(
        body, name=name, in_specs=[_HBM] * (ns + nl) + [_ANY],
        out_specs=(_SEM, _SEM, *([_HBM] * (ns + nl)), pl.BlockSpec(memory_space=pltpu.VMEM)),
        out_shape=(sems, sems, *[pltpu.HBM(a.shape, a.dtype) for a in arrays], jax.ShapeDtypeStruct((8, LANE), F32)),
        input_output_aliases={i: 2 + i for i in range(ns + nl)},
        compiler_params=pltpu.CompilerParams(has_side_effects=_EFFECT),
    )(*[pltpu.with_memory_space_constraint(a, pltpu.HBM) for a in arrays], after)
    return res[0], res[1], list(res[2:2 + ns]), list(res[2 + ns:2 + ns + nl]), res[-1]


def _exchange_wait(mode, started, after, *, name):
    send_sems, recv_sems, srcs, lands, _ = started
    ns, nl = len(srcs), len(lands)

    def body(*refs):
        src, land = refs[:ns], refs[ns:ns + nl]
        send_sems, recv_sems = refs[ns + nl:ns + nl + 2]
        place = _place()
        for w in range(nl):
            for i in range(len(_EXCHANGE_BITS[mode])):
                cp = _exchange_copy(mode, src, land, w, i, place, send_sems, recv_sems, receiving=True)
                cp.wait_send()
                cp.wait_recv()

    arrays = list(srcs) + list(lands)
    res = pl.pallas_call(
        body, name=name, in_specs=[_HBM] * (ns + nl) + [_SEM, _SEM, _ANY], out_specs=[_HBM] * (ns + nl),
        out_shape=[pltpu.HBM(a.shape, a.dtype) for a in arrays],
        input_output_aliases={i: i for i in range(ns + nl)},
        compiler_params=pltpu.CompilerParams(has_side_effects=_EFFECT),
    )(*arrays, send_sems, recv_sems, after)
    return list(res[:ns]), list(res[ns:])


def _small_allreduce_adamw(gvec, wvec, mvec, vvec):
    rows, length = gvec.shape

    def body(g_ref, w_ref, m_ref, v_ref, gs_ref, d_ref, nm_ref, nv_ref, slots, send_sems, recv_sems):
        x, y, c = _place()
        me = _slot(x, y, c)
        slots[me] = g_ref[...]
        sends = []
        for k in range(1, N_DEV):
            peer = _peer(k, x, y, c)
            sends.append(pltpu.make_async_remote_copy(
                src_ref=g_ref, dst_ref=slots.at[me], send_sem=send_sems.at[k - 1], recv_sem=recv_sems.at[k - 1],
                device_id=peer, device_id_type=MESH))
        for cp in sends:
            cp.start()
        for k in range(1, N_DEV):
            peer = _peer(k, x, y, c)
            pltpu.make_async_remote_copy(
                src_ref=g_ref, dst_ref=slots.at[_slot(*peer)], send_sem=send_sems.at[k - 1], recv_sem=recv_sems.at[k - 1],
                device_id=peer, device_id_type=MESH).wait_recv()
        for cp in sends:
            cp.wait_send()
        g = slots[0]
        for s in range(1, N_DEV):
            g = g + slots[s]
        delta, nm, nv = _adamw_math(w_ref[...], g, m_ref[...], v_ref[...])
        gs_ref[...] = g
        d_ref[...] = delta
        nm_ref[...] = nm
        nv_ref[...] = nv

    vmem = pl.BlockSpec(memory_space=pltpu.VMEM)
    out = jax.ShapeDtypeStruct((rows, length), F32)
    return pl.pallas_call(
        body, in_specs=[vmem] * 4, out_specs=[vmem] * 4, out_shape=[out] * 4,
        scratch_shapes=[pltpu.VMEM((N_DEV, rows, length), F32), pltpu.SemaphoreType.DMA((N_DEV - 1,)),
                        pltpu.SemaphoreType.DMA((N_DEV - 1,))],
        name="small_allreduce_adamw",
    )(gvec, wvec, mvec, vvec)


_SMALL_SEGMENTS = (("a_log", GDN_HEADS), ("dt_bias", GDN_HEADS), ("gdn_norm_w", HEAD_DIM), ("pool_scale", GDN_WIDTH),
                   ("ln1_g", D_MODEL), ("ln1_b", D_MODEL), ("ln2_g", D_MODEL), ("ln2_b", D_MODEL),
                   ("ln3_g", D_MODEL), ("ln3_b", D_MODEL), ("conv_w", CONV_K * QKV_COLS))
_SMALL_ROWS = 8
_SMALL_LEN = -(-sum(sz for _, sz in _SMALL_SEGMENTS) // (_SMALL_ROWS * LANE)) * LANE


def _pack_small(vals):
    parts = [vals[n].reshape(-1).astype(F32) if n in vals else jnp.zeros((sz,), F32) for n, sz in _SMALL_SEGMENTS]
    flat = jnp.concatenate(parts)
    flat = jnp.pad(flat, (0, _SMALL_ROWS * _SMALL_LEN - flat.shape[0]))
    return flat.reshape(_SMALL_ROWS, _SMALL_LEN)


def _unpack_small(vec):
    flat = vec.reshape(-1)
    out, off = {}, 0
    for n, sz in _SMALL_SEGMENTS:
        out[n] = flat[off:off + sz]
        off += sz
    return out


_WEIGHT_ORDER = ("w_in", "conv_w", "a_log", "dt_bias", "gdn_norm_w", "pool_w", "pool_scale", "w_out", "ln1_g", "ln1_b",
                 "xq_w", "xk_w", "xv_w", "xo_w", "ln2_g", "ln2_b", "w_up", "w_down", "ln3_g", "ln3_b")
_ADAM_ROWS = {"w_in": 256, "pool_w": 128, "w_out": 128, "xq_w": 128, "xk_w": 128, "xv_w": 128, "xo_w": 128,
              "w_up": 128, "w_down": 128}


def _shard2d(name, a):
    return a.reshape(-1, a.shape[-1]) if name == "pool_w" else a


def _gathered_to_full(name, gth):
    if name == "w_up":
        return gth
    if name in ("w_in", "conv_w"):
        return jnp.transpose(gth, (1, 0, 2)).reshape(gth.shape[1], N_DEV * gth.shape[2])
    if name == "pool_w":
        g4 = gth.reshape(N_DEV, POOL_GROUPS, POOL_GROUP_DIM // N_DEV, POOL_GROUP_DIM)
        return jnp.transpose(g4, (1, 0, 2, 3)).reshape(POOL_GROUPS, POOL_GROUP_DIM, POOL_GROUP_DIM)
    return gth.reshape(N_DEV * gth.shape[1], gth.shape[2])


def _full_to_chunks(name, full):
    if name == "w_up":
        return full
    if name == "w_in":
        r, cols = full.shape
        return jnp.transpose(full.reshape(r, N_DEV, cols // N_DEV), (1, 0, 2))
    if name == "pool_w":
        g4 = full.reshape(POOL_GROUPS, N_DEV, POOL_GROUP_DIM // N_DEV, POOL_GROUP_DIM)
        return jnp.transpose(g4, (1, 0, 2, 3)).reshape(N_DEV, POOL_GROUPS * POOL_GROUP_DIM // N_DEV, POOL_GROUP_DIM)
    return full.reshape(N_DEV, full.shape[0] // N_DEV, full.shape[1])


_GATHER_GROUPS = (("mixer", ("w_in", "conv_w", "pool_w")), ("attn", ("w_out", "xq_w", "xk_w", "xv_w", "xo_w")),
                  ("mlp", ("w_up", "w_down")))


def _grad_chunks(name, g):
    if name == "w_in":
        g = _unpad_w_in(g)
    return _full_to_chunks(name, g.astype(BF16))


def kernel(x, mem, w_in, conv_w, a_log, dt_bias, gdn_norm_w, pool_w, pool_scale, w_out, ln1_g, ln1_b, xq_w, xk_w, xv_w, xo_w, ln2_g, ln2_b, w_up, w_down, ln3_g, ln3_b, loss_target, m_w_in, m_conv_w, m_a_log, m_dt_bias, m_gdn_norm_w, m_pool_w, m_pool_scale, m_w_out, m_ln1_g, m_ln1_b, m_xq_w, m_xk_w, m_xv_w, m_xo_w, m_ln2_g, m_ln2_b, m_w_up, m_w_down, m_ln3_g, m_ln3_b, v_w_in, v_conv_w, v_a_log, v_dt_bias, v_gdn_norm_w, v_pool_w, v_pool_scale, v_w_out, v_ln1_g, v_ln1_b, v_xq_w, v_xk_w, v_xv_w, v_xo_w, v_ln2_g, v_ln2_b, v_w_up, v_w_down, v_ln3_g, v_ln3_b):
    args = dict(locals())
    wt = {n: args[n][0] for n in _WEIGHT_ORDER}
    mo = {n: args["m_" + n][0] for n in _WEIGHT_ORDER}
    vo = {n: args["v_" + n][0] for n in _WEIGHT_ORDER}

    me = _slot(*_place())
    me_arr = jnp.reshape(me, (1,)).astype(jnp.int32)
    nothing = jnp.zeros((8, LANE), F32)

    def landing_zones(names):
        shards = [_shard2d(n, wt[n]).astype(F32 if n == "conv_w" else BF16) for n in names]
        zones = [lax.dynamic_update_slice(lax.empty((N_DEV, *s.shape), s.dtype), s[None], (me, 0, 0)) for s in shards]
        return shards, zones

    first, attn_names, mlp_names = (names for _, names in _GATHER_GROUPS)
    shards, zones = landing_zones(first)
    chips_first = _exchange_start("gather_chips", shards, zones, nothing, name="gather_chips_mixer")
    _, zones = _exchange_wait("gather_chips", chips_first, chips_first[4], name="gather_chips_mixer_wait")
    pass_first = _exchange_start("gather_pass", [], zones, nothing, name="gather_pass_mixer")
    shards, zones = landing_zones(attn_names + mlp_names)
    chips_rest = _exchange_start("gather_chips", shards, zones, pass_first[4], name="gather_chips_rest")
    pending = {}

    def full_weights(group, names, zones, token=None):
        full = {n: _gathered_to_full(n, z) for n, z in zip(names, zones)}
        full.update({n: wt[n] if token is None else wt[n] + token for n in _VECTORS})
        return _group_weights(group, full)

    def weights_of(group, after):
        if group == "mixer":
            _, zones = _exchange_wait("gather_pass", pass_first, chips_rest[4], name="gather_pass_mixer_wait")
            return full_weights(group, first, zones)
        if group == "attn":
            _, zones = _exchange_wait("gather_chips", chips_rest, after, name="gather_chips_rest_wait")
            pass_attn = _exchange_start("gather_pass", [], zones[:len(attn_names)], nothing, name="gather_pass_attn")
            _, attn_zones = _exchange_wait("gather_pass", pass_attn, pass_attn[4], name="gather_pass_attn_wait")
            pending["mlp"] = _exchange_start("gather_pass", [], zones[len(attn_names):], attn_zones[0], name="gather_pass_mlp")
            return full_weights(group, attn_names, attn_zones, pending["mlp"][4][0, 0])
        _, zones = _exchange_wait("gather_pass", pending["mlp"], after, name="gather_pass_mlp_wait")
        return full_weights(group, mlp_names, zones)

    scatters = {}

    def grads_ready(group, grads):
        names = tuple(grads)
        chunks = [_grad_chunks(n, grads[n]) for n in names]
        started = _exchange_start("scatter", chunks, [lax.empty(c.shape, c.dtype) for c in chunks], nothing,
                                  name="scatter_start_" + group)
        scatters[group] = (names, started)
        return started[4][0:1, 0:1]

    sq, grad_x, g = _local_step(x[0], mem[0], loss_target[0], weights_of, grads_ready)
    small = _finish_small_grads(g)

    gs, ds, ms, vs = _small_allreduce_adamw(
        _pack_small(small), _pack_small({n: wt[n] for n in _VECTORS}), _pack_small({n: mo[n] for n in _VECTORS}),
        _pack_small({n: vo[n] for n in _VECTORS}))
    gs, ds, ms, vs = _unpack_small(gs), _unpack_small(ds), _unpack_small(ms), _unpack_small(vs)

    out = {}
    after = grad_x
    for group, (names, started) in scatters.items():
        chunks, lands = _exchange_wait("scatter", started, after, name="scatter_wait_" + group)
        for n, parts, own in zip(names, lands, chunks):
            res = _adamw_shard(parts, own, me_arr, _shard2d(n, wt[n]), _shard2d(n, mo[n]), _shard2d(n, vo[n]),
                               tr=_ADAM_ROWS[n], name="adamw_" + n)
            out[n] = [r.reshape(args[n].shape) for r in res]
            after = res[1]
    cols = conv_w.shape[-1]
    conv_full = gs["conv_w"].reshape(CONV_K, QKV_COLS)
    conv_mine = lax.dynamic_slice(conv_full, (0, me * cols), (CONV_K, cols))[None]
    res = _adamw_shard(conv_mine, conv_mine, jnp.zeros((1,), jnp.int32), wt["conv_w"], mo["conv_w"], vo["conv_w"],
                       tr=CONV_K, name="adamw_conv_w")
    out["conv_w"] = [r.reshape(conv_w.shape) for r in res]
    for n in _VECTORS:
        out[n] = [t[n].reshape(args[n].shape) for t in (gs, ds, ms, vs)]

    loss = lax.psum(0.5 * sq[0, 0] / D_MODEL, ("x", "y", "c"))
    return (loss, grad_x[None], *[out[n][0] for n in _WEIGHT_ORDER], *[out[n][1] for n in _WEIGHT_ORDER],
            *[out[n][2] for n in _WEIGHT_ORDER], *[out[n][3] for n in _WEIGHT_ORDER])
```

```python
import functools
import math

import jax
import jax.numpy as jnp
from jax import lax
from jax.experimental import pallas as pl
from jax.experimental.pallas import tpu as pltpu

F32 = jnp.float32
BF16 = jnp.bfloat16
MESH = pl.DeviceIdType.MESH

N_DEV = 8
D_MODEL = 2048
GDN_WIDTH = 1024
GDN_HEADS = 8
HEAD_DIM = 128
CONV_K = 4
CHUNK = 64
POOL_GROUPS = 4
POOL_GROUP_DIM = 256
MEM_LEN = 256
XATTN_HEADS = 4
XATTN_HEAD_DIM = 512
D_FF = 8192
IN_COLS = 5136
ALPHA = 2.0 ** 0.25
LN_EPS = 1e-5
NORM_EPS = 1e-6

LANE = 128
QKV_COLS = 3 * GDN_WIDTH
Z_OFF = QKV_COLS
BA_OFF = 4 * GDN_WIDTH
POOL_OFF = BA_OFF + 2 * LANE
PROJ_COLS = POOL_OFF + GDN_WIDTH
Z_BLK = Z_OFF // LANE
BA_BLK = BA_OFF // LANE
POOL_BLK = POOL_OFF // POOL_GROUP_DIM

ADAM_LR = 0.001
ADAM_B1 = 0.9
ADAM_B2 = 0.999
ADAM_EPS = 1e-08
ADAM_WD = 0.01
ADAM_STEP = 10

VMEM_LIMIT_BYTES = 48 * 1024 * 1024


def _params(*sem):
    return pltpu.CompilerParams(dimension_semantics=sem if sem else None, vmem_limit_bytes=VMEM_LIMIT_BYTES)


def _make_dots(cast, precision, batched=False):
    lead = 1 if batched else 0
    batch = ((0,), (0,)) if batched else ((), ())

    def dg(a, b, ca, cb):
        if cast is not None:
            a = a.astype(cast)
            b = b.astype(cast)
        return lax.dot_general(a, b, (((ca + lead,), (cb + lead,)), batch), precision=precision, preferred_element_type=F32)

    def nn_(a, b):
        return dg(a, b, 1, 0)

    def nt_(a, b):
        return dg(a, b, 1, 1)

    def tn_(a, b):
        return dg(a, b, 0, 0)

    @jax.custom_vjp
    def nn(a, b):
        return nn_(a, b)

    nn.defvjp(lambda a, b: (nn_(a, b), (a, b)), lambda r, g: (nt_(g, r[1]), tn_(r[0], g)))

    @jax.custom_vjp
    def nt(a, b):
        return nt_(a, b)

    nt.defvjp(lambda a, b: (nt_(a, b), (a, b)), lambda r, g: (nn_(g, r[1]), tn_(g, r[0])))

    @jax.custom_vjp
    def tn(a, b):
        return tn_(a, b)

    tn.defvjp(lambda a, b: (tn_(a, b), (a, b)), lambda r, g: (nt_(r[1], g), nn_(r[0], g)))

    return (nn_, nt_, tn_), (nn, nt, tn)


_BDOT_PLAIN, _BDOT_VJP = _make_dots(BF16, None)
_BDOT_BATCH_PLAIN, _BDOT_BATCH_VJP = _make_dots(BF16, None, batched=True)
_FDOT_BATCH_PLAIN, _FDOT_BATCH_VJP = _make_dots(None, lax.Precision.HIGH, batched=True)


def _mm(a, b, *, ta=False, tb=False, out_dtype=F32, tm=None, tn=512, tk=None, epi=None, extra=None, add_scale=1.0,
        b_chunks=False, o_chunks=False, name):
    m, k = (a.shape[1], a.shape[0]) if ta else a.shape
    if b_chunks:
        n, kb = (b.shape[1], N_DEV * b.shape[2]) if tb else (N_DEV * b.shape[2], b.shape[1])
    else:
        n, kb = b.shape if tb else (b.shape[1], b.shape[0])
    assert kb == k, (name, a.shape, b.shape)
    tm, tn, tk = min(tm or m, m), min(tn, n), min(tk or k, k)
    assert m % tm == 0 and n % tn == 0 and k % tk == 0, (name, m, n, k)
    nk = k // tk
    dims = (((0 if ta else 1,), (1 if tb else 0,)), ((), ()))
    n_extra = 0 if epi in (None, "relu2") else 1
    n_out = 2 if epi == "relu2" else 1
    if epi in ("relu2", "mul2r"):
        out_dtype = BF16

    def body(*refs):
        a_ref, b_ref = refs[:2]
        c_ref = refs[2] if n_extra else None
        o_refs = refs[2 + n_extra:2 + n_extra + n_out]
        scr = refs[2 + n_extra + n_out:]
        r = lax.dot_general(a_ref[...].astype(BF16), b_ref[...].astype(BF16), dims, preferred_element_type=F32)

        def finish(v):
            if epi == "add":
                o_refs[0][...] = (v + add_scale * c_ref[...]).astype(out_dtype)
            elif epi == "relu2":
                p = jnp.maximum(v, 0.0)
                o_refs[0][...] = (p * p).astype(BF16)
                o_refs[1][...] = p.astype(BF16)
            elif epi == "mul2r":
                o_refs[0][...] = (v * (2.0 * c_ref[...].astype(F32))).astype(BF16)
            else:
                o_refs[0][...] = v.astype(out_dtype)

        if nk == 1:
            finish(r)
        else:
            acc = scr[0]
            kk = pl.program_id(2)

            @pl.when(kk == 0)
            def _():
                acc[...] = r

            @pl.when(kk > 0)
            def _():
                acc[...] += r

            @pl.when(kk == nk - 1)
            def _():
                finish(acc[...])

    a_spec = pl.BlockSpec((tk, tm), lambda i, j, kk: (kk, i)) if ta else pl.BlockSpec((tm, tk), lambda i, j, kk: (i, kk))
    if b_chunks and tb:
        kc = k // N_DEV // tk
        b_spec = pl.BlockSpec((None, tn, tk), lambda i, j, kk: (kk // kc, j, kk % kc))
    elif b_chunks:
        nc = n // N_DEV // tn
        b_spec = pl.BlockSpec((None, tk, tn), lambda i, j, kk: (j // nc, kk, j % nc))
    elif tb:
        b_spec = pl.BlockSpec((tn, tk), lambda i, j, kk: (j, kk))
    else:
        b_spec = pl.BlockSpec((tk, tn), lambda i, j, kk: (kk, j))
    mn_spec = pl.BlockSpec((tm, tn), lambda i, j, kk: (i, j))
    if o_chunks:
        oc = n // N_DEV // tn
        o_spec = pl.BlockSpec((None, tm, tn), lambda i, j, kk: (j // oc, i, j % oc))
        o_shape = jax.ShapeDtypeStruct((N_DEV, m, n // N_DEV), out_dtype)
    else:
        o_spec, o_shape = mn_spec, jax.ShapeDtypeStruct((m, n), out_dtype)
    res = pl.pallas_call(
        body, grid=(m // tm, n // tn, nk), in_specs=[a_spec, b_spec] + [mn_spec] * n_extra,
        out_specs=[o_spec] * n_out, out_shape=[o_shape] * n_out,
        scratch_shapes=[pltpu.VMEM((tm, tn), F32)] if nk > 1 else [],
        compiler_params=_params("parallel", "parallel", "arbitrary"), name=name,
    )(a, b, *([extra] if n_extra else []))
    return res if n_out > 1 else res[0]


def _cast_bf16(v, *, name, tm=512):
    t, d = v.shape
    tm = min(tm, t)

    def body(v_ref, o_ref):
        o_ref[...] = v_ref[...].astype(BF16)

    spec = pl.BlockSpec((tm, d), lambda i: (i, 0))
    return pl.pallas_call(body, grid=(t // tm,), in_specs=[spec], out_specs=spec,
                          out_shape=jax.ShapeDtypeStruct((t, d), BF16), compiler_params=_params("parallel"), name=name)(v)


def _shift_down(v, s):
    if s == 0:
        return v
    row = lax.broadcasted_iota(jnp.int32, v.shape, 0)
    return jnp.where(row >= s, pltpu.roll(v, s, axis=0), 0.0)


def _shift_up(v, s):
    if s == 0:
        return v
    t = v.shape[0]
    row = lax.broadcasted_iota(jnp.int32, v.shape, 0)
    return jnp.where(row < t - s, pltpu.roll(v, t - s, axis=0), 0.0)


def _post_col(j):
    return (j % GDN_HEADS) * 3 + j // GDN_HEADS


def _gdn_prep_fwd(proj, conv_w):
    t = proj.shape[0]

    def body(x_ref, w_ref, o_ref):
        j = pl.program_id(0)
        x = x_ref[...]
        y = jnp.zeros_like(x)
        for tap in range(CONV_K):
            y = y + w_ref[tap:tap + 1, :] * _shift_down(x, CONV_K - 1 - tap)
        c = y * jax.nn.sigmoid(y)
        nrm = c * lax.rsqrt(jnp.sum(c * c, axis=1, keepdims=True) + NORM_EPS)
        o_ref[...] = jnp.where(j < 2 * GDN_HEADS, nrm, c)

    return pl.pallas_call(
        body, grid=(QKV_COLS // LANE,),
        in_specs=[pl.BlockSpec((t, LANE), lambda j: (0, j)), pl.BlockSpec((CONV_K, LANE), lambda j: (0, j))],
        out_specs=pl.BlockSpec((t, LANE), lambda j: (0, _post_col(j))),
        out_shape=jax.ShapeDtypeStruct((t, QKV_COLS), F32),
        compiler_params=_params("parallel"), name="gdn_prep_fwd",
    )(proj, conv_w)


def _gdn_prep_bwd(proj, conv_w, dpost, dproj):
    t = proj.shape[0]

    def body(x_ref, w_ref, d_ref, _, dx_ref, dw_ref):
        j = pl.program_id(0)
        x = x_ref[...]
        xs = [_shift_down(x, CONV_K - 1 - tap) for tap in range(CONV_K)]
        y = jnp.zeros_like(x)
        for tap in range(CONV_K):
            y = y + w_ref[tap:tap + 1, :] * xs[tap]
        sig = jax.nn.sigmoid(y)
        c = y * sig
        r = lax.rsqrt(jnp.sum(c * c, axis=1, keepdims=True) + NORM_EPS)
        nrm = c * r
        d = d_ref[...]
        dc_norm = r * (d - nrm * jnp.sum(d * nrm, axis=1, keepdims=True))
        dc = jnp.where(j < 2 * GDN_HEADS, dc_norm, d)
        dy = dc * (sig * (1.0 + y * (1.0 - sig)))
        dx = jnp.zeros_like(x)
        for tap in range(CONV_K):
            dx = dx + _shift_up(w_ref[tap:tap + 1, :] * dy, CONV_K - 1 - tap)
            dw_ref[tap:tap + 1, :] = jnp.sum(dy * xs[tap], axis=0, keepdims=True)
        dx_ref[...] = dx.astype(dx_ref.dtype)

    return pl.pallas_call(
        body, grid=(QKV_COLS // LANE,),
        in_specs=[pl.BlockSpec((t, LANE), lambda j: (0, j)), pl.BlockSpec((CONV_K, LANE), lambda j: (0, j)),
                  pl.BlockSpec((t, LANE), lambda j: (0, _post_col(j))), pl.BlockSpec(memory_space=pl.ANY)],
        out_specs=[pl.BlockSpec((t, LANE), lambda j: (0, j)), pl.BlockSpec((CONV_K, LANE), lambda j: (0, j))],
        out_shape=[jax.ShapeDtypeStruct(dproj.shape, dproj.dtype), jax.ShapeDtypeStruct((CONV_K, QKV_COLS), F32)],
        input_output_aliases={3: 0},
        compiler_params=_params("parallel"), name="gdn_prep_bwd",
    )(proj, conv_w, dpost, dproj)


def _softplus(v):
    return jnp.maximum(v, 0.0) + jnp.log(1.0 + jnp.exp(-jnp.abs(v)))


def _tri_inv(low, nn):
    r = lax.broadcasted_iota(jnp.int32, (CHUNK, CHUNK), 0)
    c = lax.broadcasted_iota(jnp.int32, (CHUNK, CHUNK), 1)
    eye = (r == c).astype(F32)
    same_blk = lax.shift_right_logical(r, 4) == lax.shift_right_logical(c, 4)
    diag = jnp.where(same_blk, low, 0.0)
    off = low - diag
    n1 = -diag
    n2 = nn(n1, n1)
    n4 = nn(n2, n2)
    n8 = nn(n4, n4)
    inv_d = nn(nn(nn(eye + n1, eye + n2), eye + n4), eye + n8)
    m1 = nn(inv_d, off)
    m2 = nn(m1, m1)
    return nn(nn(eye - m1, eye + m2), inv_d)


LOCAL_HEADS_PER_STEP = 8


def _gdn_local_fn(qkv, ba, alog_row, dtb_row, first_head, bdots, fdots):
    nn, nt, tn = bdots
    fnn = fdots[0]
    n_heads = qkv.shape[1] // (3 * HEAD_DIM)
    part = lambda i, p: qkv[:, (3 * i + p) * HEAD_DIM:(3 * i + p + 1) * HEAD_DIM]
    q = jnp.stack([part(i, 0) for i in range(n_heads)]) * (HEAD_DIM ** -0.5)
    k = jnp.stack([part(i, 1) for i in range(n_heads)])
    v = jnp.stack([part(i, 2) for i in range(n_heads)])
    lane = lax.broadcasted_iota(jnp.int32, ba.shape, 1)
    bg = jnp.where(lane < GDN_HEADS, jax.nn.sigmoid(ba), -jnp.exp(alog_row) * _softplus(ba + dtb_row))
    pick = lambda l: jnp.sum(jnp.where(lane == l, bg, 0.0), axis=1, keepdims=True)
    beta = jnp.stack([pick(first_head + i) for i in range(n_heads)])
    g = jnp.stack([pick(first_head + i + GDN_HEADS) for i in range(n_heads)])

    r = lax.broadcasted_iota(jnp.int32, (CHUNK, CHUNK), 0)
    c = lax.broadcasted_iota(jnp.int32, (CHUNK, CHUNK), 1)
    incl = r >= c
    strict = r > c
    eye = r == c

    def to_row(col):
        return jnp.sum(jnp.where(eye, col, 0.0), axis=1, keepdims=True)

    gc = jnp.sum(jnp.where(incl, to_row(g), 0.0), axis=2, keepdims=True)
    diff = gc - to_row(gc)
    decay = jnp.where(incl, jnp.exp(jnp.where(incl, diff, 0.0)), 0.0)
    k_beta = k * beta
    v_beta = v * beta
    low = jnp.where(strict, nt(k_beta, k) * decay, 0.0)
    t_inv = _tri_inv(low, fnn)
    eg = jnp.exp(gc)
    u = fnn(t_inv, v_beta)
    w = fnn(t_inv, k_beta * eg)
    attn = jnp.where(incl, nt(q, k) * decay, 0.0)
    last = lax.broadcasted_iota(jnp.int32, (CHUNK, 1), 0) == CHUNK - 1
    g_last = jnp.sum(jnp.where(last, gc, 0.0), axis=1, keepdims=True)
    kdec = k * jnp.exp(g_last - gc)
    elast = jnp.broadcast_to(jnp.exp(g_last), (n_heads, 1, LANE))
    return u, w, q * eg, kdec, attn, elast


def _gdn_state_fn(u, w, qg, kdec, attn, elast, state, bdots):
    nn, _, tn = bdots
    v_new = u - nn(w, state)
    o = nn(qg, state) + nn(attn, v_new)
    return o, state * elast + tn(kdec, v_new)


def _gdn_local_fwd(post, proj, alog_row, dtb_row):
    t = post.shape[0]
    n_chunks = t // CHUNK
    hb = LOCAL_HEADS_PER_STEP

    def body(qkv_ref, ba_ref, al_ref, dt_ref, u_ref, w_ref, qg_ref, kd_ref, at_ref, el_ref):
        u, w, qg, kdec, attn, elast = _gdn_local_fn(qkv_ref[...], ba_ref[...], al_ref[...], dt_ref[...],
                                                    pl.program_id(1) * hb, _BDOT_BATCH_PLAIN, _FDOT_BATCH_PLAIN)
        for i in range(hb):
            cols = slice(i * HEAD_DIM, (i + 1) * HEAD_DIM)
            u_ref[:, cols] = u[i]
            w_ref[:, cols] = w[i].astype(BF16)
            qg_ref[:, cols] = qg[i].astype(BF16)
            kd_ref[:, cols] = kdec[i].astype(BF16)
        at_ref[...] = attn.astype(BF16)
        el_ref[:, 0] = elast

    wide = pl.BlockSpec((CHUNK, hb * HEAD_DIM), lambda n, j: (n, j))
    row = pl.BlockSpec((1, LANE), lambda n, j: (0, 0))
    return pl.pallas_call(
        body, grid=(n_chunks, GDN_HEADS // hb),
        in_specs=[pl.BlockSpec((CHUNK, hb * 3 * HEAD_DIM), lambda n, j: (n, j)),
                  pl.BlockSpec((CHUNK, LANE), lambda n, j: (n, BA_BLK)), row, row],
        out_specs=[wide, wide, wide, wide, pl.BlockSpec((hb, CHUNK, CHUNK), lambda n, j: (j, n, 0)),
                   pl.BlockSpec((hb, 1, 1, LANE), lambda n, j: (j, n, 0, 0))],
        out_shape=[jax.ShapeDtypeStruct((t, GDN_WIDTH), F32), jax.ShapeDtypeStruct((t, GDN_WIDTH), BF16),
                   jax.ShapeDtypeStruct((t, GDN_WIDTH), BF16), jax.ShapeDtypeStruct((t, GDN_WIDTH), BF16),
                   jax.ShapeDtypeStruct((GDN_HEADS, t, CHUNK), BF16),
                   jax.ShapeDtypeStruct((GDN_HEADS, n_chunks, 1, LANE), F32)],
        compiler_params=_params("parallel", "parallel"), name="gdn_local_fwd",
    )(post, proj, alog_row, dtb_row)


def _gdn_state_specs(n_of):
    wide = pl.BlockSpec((CHUNK, GDN_WIDTH), lambda n: (n_of(n), 0))
    attn = pl.BlockSpec((GDN_HEADS, CHUNK, CHUNK), lambda n: (0, n_of(n), 0))
    elast = pl.BlockSpec((GDN_HEADS, 1, 1, LANE), lambda n: (0, n_of(n), 0, 0))
    saved = pl.BlockSpec((GDN_HEADS, 1, HEAD_DIM, HEAD_DIM), lambda n: (0, n_of(n), 0, 0))
    return wide, attn, elast, saved


def _gdn_state_fwd(u, w, qg, kdec, attn, elast):
    t = u.shape[0]
    n_chunks = t // CHUNK

    def body(u_ref, w_ref, qg_ref, kd_ref, at_ref, el_ref, o_ref, save_ref, state_ref):
        @pl.when(pl.program_id(0) == 0)
        def _():
            state_ref[...] = jnp.zeros_like(state_ref)

        for h in range(GDN_HEADS):
            cols = slice(h * HEAD_DIM, (h + 1) * HEAD_DIM)
            state = state_ref[h]
            save_ref[h, 0] = state
            o, new_state = _gdn_state_fn(u_ref[:, cols], w_ref[:, cols], qg_ref[:, cols], kd_ref[:, cols], at_ref[h],
                                         el_ref[h, 0], state, _BDOT_PLAIN)
            o_ref[:, cols] = o
            state_ref[h] = new_state

    wide, attn_spec, elast_spec, saved_spec = _gdn_state_specs(lambda n: n)
    return pl.pallas_call(
        body, grid=(n_chunks,), in_specs=[wide, wide, wide, wide, attn_spec, elast_spec],
        out_specs=[wide, saved_spec],
        out_shape=[jax.ShapeDtypeStruct((t, GDN_WIDTH), F32),
                   jax.ShapeDtypeStruct((GDN_HEADS, n_chunks, HEAD_DIM, HEAD_DIM), F32)],
        scratch_shapes=[pltpu.VMEM((GDN_HEADS, HEAD_DIM, HEAD_DIM), F32)],
        compiler_params=_params("arbitrary"), name="gdn_state_fwd",
    )(u, w, qg, kdec, attn, elast)


def _gdn_state_bwd(u, w, qg, kdec, attn, elast, saved, do):
    t = u.shape[0]
    n_chunks = t // CHUNK
    last = n_chunks - 1

    def body(u_ref, w_ref, qg_ref, kd_ref, at_ref, el_ref, save_ref, do_ref,
             du_ref, dw_ref, dqg_ref, dkd_ref, dat_ref, del_ref, dstate_ref):
        @pl.when(pl.program_id(0) == 0)
        def _():
            dstate_ref[...] = jnp.zeros_like(dstate_ref)

        for h in range(GDN_HEADS):
            cols = slice(h * HEAD_DIM, (h + 1) * HEAD_DIM)
            _, vjp = jax.vjp(
                lambda *a: _gdn_state_fn(*a, _BDOT_VJP), u_ref[:, cols], w_ref[:, cols].astype(F32),
                qg_ref[:, cols].astype(F32), kd_ref[:, cols].astype(F32), at_ref[h].astype(F32), el_ref[h, 0], save_ref[h, 0])
            du, dw, dqg, dkd, dat, de, dstate = vjp((do_ref[:, cols], dstate_ref[h]))
            du_ref[:, cols] = du
            dw_ref[:, cols] = dw
            dqg_ref[:, cols] = dqg
            dkd_ref[:, cols] = dkd
            dat_ref[h] = dat
            del_ref[h, 0] = de
            dstate_ref[h] = dstate

    wide, attn_spec, elast_spec, saved_spec = _gdn_state_specs(lambda n: last - n)
    wide_f32 = jax.ShapeDtypeStruct((t, GDN_WIDTH), F32)
    return pl.pallas_call(
        body, grid=(n_chunks,), in_specs=[wide, wide, wide, wide, attn_spec, elast_spec, saved_spec, wide],
        out_specs=[wide, wide, wide, wide, attn_spec, elast_spec],
        out_shape=[wide_f32, wide_f32, wide_f32, wide_f32, jax.ShapeDtypeStruct((GDN_HEADS, t, CHUNK), F32),
                   jax.ShapeDtypeStruct((GDN_HEADS, n_chunks, 1, LANE), F32)],
        scratch_shapes=[pltpu.VMEM((GDN_HEADS, HEAD_DIM, HEAD_DIM), F32)],
        compiler_params=_params("arbitrary"), name="gdn_state_bwd",
    )(u, w, qg, kdec, attn, elast, saved, do)


def _gdn_local_bwd(post, proj, alog_row, dtb_row, cots, dproj):
    t = post.shape[0]
    n_chunks = t // CHUNK
    hb = LOCAL_HEADS_PER_STEP
    n_steps = GDN_HEADS // hb

    def body(qkv_ref, ba_ref, al_ref, dt_ref, du_ref, dw_ref, dqg_ref, dkd_ref, dat_ref, del_ref, _,
             dqkv_ref, dba_ref, dal_ref, ddt_ref, dba_acc):
        n = pl.program_id(0)
        j = pl.program_id(1)

        @pl.when((n == 0) & (j == 0))
        def _():
            dal_ref[...] = jnp.zeros_like(dal_ref)
            ddt_ref[...] = jnp.zeros_like(ddt_ref)

        @pl.when(j == 0)
        def _():
            dba_acc[...] = jnp.zeros_like(dba_acc)

        heads = lambda ref: jnp.stack([ref[:, i * HEAD_DIM:(i + 1) * HEAD_DIM] for i in range(hb)])
        _, vjp = jax.vjp(lambda a, b, c, d: _gdn_local_fn(a, b, c, d, j * hb, _BDOT_BATCH_VJP, _FDOT_BATCH_VJP),
                         qkv_ref[...], ba_ref[...], al_ref[...], dt_ref[...])
        dqkv, dba, dal, ddt = vjp((heads(du_ref), heads(dw_ref), heads(dqg_ref), heads(dkd_ref), dat_ref[...],
                                   del_ref[:, 0]))
        dqkv_ref[...] = dqkv
        dba_acc[...] += dba
        dal_ref[...] += dal
        ddt_ref[...] += ddt

        @pl.when(j == n_steps - 1)
        def _():
            dba_ref[:, 0:LANE] = dba_acc[...].astype(dba_ref.dtype)
            dba_ref[:, LANE:2 * LANE] = jnp.zeros((CHUNK, LANE), dba_ref.dtype)

    wide = pl.BlockSpec((CHUNK, hb * HEAD_DIM), lambda n, j: (n, j))
    qkv_spec = pl.BlockSpec((CHUNK, hb * 3 * HEAD_DIM), lambda n, j: (n, j))
    row = pl.BlockSpec((1, LANE), lambda n, j: (0, 0))
    return pl.pallas_call(
        body, grid=(n_chunks, n_steps),
        in_specs=[qkv_spec, pl.BlockSpec((CHUNK, LANE), lambda n, j: (n, BA_BLK)), row, row, wide, wide, wide, wide,
                  pl.BlockSpec((hb, CHUNK, CHUNK), lambda n, j: (j, n, 0)),
                  pl.BlockSpec((hb, 1, 1, LANE), lambda n, j: (j, n, 0, 0)), pl.BlockSpec(memory_space=pl.ANY)],
        out_specs=[qkv_spec, pl.BlockSpec((CHUNK, 2 * LANE), lambda n, j: (n, BA_BLK // 2)), row, row],
        out_shape=[jax.ShapeDtypeStruct((t, QKV_COLS), F32), jax.ShapeDtypeStruct(dproj.shape, dproj.dtype),
                   jax.ShapeDtypeStruct((1, LANE), F32), jax.ShapeDtypeStruct((1, LANE), F32)],
        input_output_aliases={10: 1},
        scratch_shapes=[pltpu.VMEM((CHUNK, LANE), F32)],
        compiler_params=_params("arbitrary", "arbitrary"), name="gdn_local_bwd",
    )(post, proj, alog_row, dtb_row, *cots, dproj)


def _onorm_fn(o, z, w):
    return o * lax.rsqrt(jnp.mean(o * o, axis=1, keepdims=True) + NORM_EPS) * w * (z * jax.nn.sigmoid(z))


def _onorm_fwd(o_raw, proj, norm_w, mixin, tm=512):
    t = o_raw.shape[0]
    tm = min(tm, t)

    def body(o_ref, z_ref, w_ref, _, out_ref):
        out_ref[...] = _onorm_fn(o_ref[...], z_ref[...], w_ref[...]).astype(out_ref.dtype)

    return pl.pallas_call(
        body, grid=(t // tm, GDN_HEADS),
        in_specs=[pl.BlockSpec((tm, LANE), lambda i, h: (i, h)), pl.BlockSpec((tm, LANE), lambda i, h: (i, Z_BLK + h)),
                  pl.BlockSpec((1, LANE), lambda i, h: (0, 0)), pl.BlockSpec(memory_space=pl.ANY)],
        out_specs=pl.BlockSpec((tm, LANE), lambda i, h: (i, h)),
        out_shape=jax.ShapeDtypeStruct(mixin.shape, mixin.dtype), input_output_aliases={3: 0},
        compiler_params=_params("parallel", "parallel"), name="gdn_onorm_fwd",
    )(o_raw, proj, norm_w, mixin)


def _onorm_bwd(o_raw, proj, norm_w, dmixin, dproj, tm=512):
    t = o_raw.shape[0]
    tm = min(tm, t)

    def body(o_ref, z_ref, w_ref, d_ref, _, do_ref, dz_ref, dw_ref):
        @pl.when((pl.program_id(0) == 0) & (pl.program_id(1) == 0))
        def _():
            dw_ref[...] = jnp.zeros_like(dw_ref)

        _, vjp = jax.vjp(_onorm_fn, o_ref[...], z_ref[...], w_ref[...])
        do, dz, dw = vjp(d_ref[...])
        do_ref[...] = do
        dz_ref[...] = dz.astype(dz_ref.dtype)
        dw_ref[...] += dw

    return pl.pallas_call(
        body, grid=(t // tm, GDN_HEADS),
        in_specs=[pl.BlockSpec((tm, LANE), lambda i, h: (i, h)), pl.BlockSpec((tm, LANE), lambda i, h: (i, Z_BLK + h)),
                  pl.BlockSpec((1, LANE), lambda i, h: (0, 0)), pl.BlockSpec((tm, LANE), lambda i, h: (i, h)),
                  pl.BlockSpec(memory_space=pl.ANY)],
        out_specs=[pl.BlockSpec((tm, LANE), lambda i, h: (i, h)), pl.BlockSpec((tm, LANE), lambda i, h: (i, Z_BLK + h)),
                   pl.BlockSpec((1, LANE), lambda i, h: (0, 0))],
        out_shape=[jax.ShapeDtypeStruct((t, GDN_WIDTH), F32), jax.ShapeDtypeStruct(dproj.shape, dproj.dtype),
                   jax.ShapeDtypeStruct((1, LANE), F32)],
        input_output_aliases={4: 1},
        compiler_params=_params("arbitrary", "arbitrary"), name="gdn_onorm_bwd",
    )(o_raw, proj, norm_w, dmixin, dproj)


def _pool_select(levels, gi):
    out = levels[-1]
    for lvl in range(len(levels) - 2, -1, -1):
        out = jnp.where(gi == lvl, levels[lvl], out)
    return out


def _pool_count(shape, gi):
    pos = lax.broadcasted_iota(jnp.int32, shape, 0)
    win = lax.shift_left(jnp.int32(2), gi)
    return jnp.minimum(pos + 1, win).astype(F32)


def _pooled(p, gi):
    acc = p
    levels = []
    for lvl in range(POOL_GROUPS):
        acc = acc + _shift_down(acc, 1 << lvl)
        levels.append(acc)
    return _pool_select(levels, gi) / _pool_count(p.shape, gi) - p


def _pool_fwd(proj, pool_w, pool_scale):
    t = proj.shape[0]

    def body(p_ref, w_ref, s_ref, out_ref):
        gi = pl.program_id(0)
        pooled = _pooled(p_ref[...], gi)
        out_ref[...] = (_BDOT_PLAIN[0](pooled, w_ref[0]) * s_ref[0]).astype(out_ref.dtype)

    return pl.pallas_call(
        body, grid=(POOL_GROUPS,),
        in_specs=[pl.BlockSpec((t, POOL_GROUP_DIM), lambda g: (0, POOL_BLK + g)),
                  pl.BlockSpec((1, POOL_GROUP_DIM, POOL_GROUP_DIM), lambda g: (g, 0, 0)),
                  pl.BlockSpec((1, 1, POOL_GROUP_DIM), lambda g: (g, 0, 0))],
        out_specs=pl.BlockSpec((t, POOL_GROUP_DIM), lambda g: (0, GDN_WIDTH // POOL_GROUP_DIM + g)),
        out_shape=jax.ShapeDtypeStruct((t, 2 * GDN_WIDTH), BF16),
        compiler_params=_params("parallel"), name="pool_fwd",
    )(proj, pool_w, pool_scale)


def _pool_bwd(proj, pool_w, pool_scale, dmixin):
    t = proj.shape[0]
    nn, nt, tn = _BDOT_PLAIN

    def body(p_ref, w_ref, s_ref, d_ref, dp_ref, dw_ref, ds_ref):
        gi = pl.program_id(0)
        p = p_ref[...]
        pooled = _pooled(p, gi)
        mixed = nn(pooled, w_ref[0])
        d = d_ref[...]
        ds_ref[0] = jnp.sum(d * mixed, axis=0, keepdims=True)
        dmixed = d * s_ref[0]
        dw_ref[0] = tn(pooled, dmixed)
        dpooled = nt(dmixed, w_ref[0])
        acc = dpooled / _pool_count(p.shape, gi)
        levels = []
        for lvl in range(POOL_GROUPS):
            acc = acc + _shift_up(acc, 1 << lvl)
            levels.append(acc)
        dp_ref[...] = (_pool_select(levels, gi) - dpooled).astype(dp_ref.dtype)

    return pl.pallas_call(
        body, grid=(POOL_GROUPS,),
        in_specs=[pl.BlockSpec((t, POOL_GROUP_DIM), lambda g: (0, POOL_BLK + g)),
                  pl.BlockSpec((1, POOL_GROUP_DIM, POOL_GROUP_DIM), lambda g: (g, 0, 0)),
                  pl.BlockSpec((1, 1, POOL_GROUP_DIM), lambda g: (g, 0, 0)),
                  pl.BlockSpec((t, POOL_GROUP_DIM), lambda g: (0, GDN_WIDTH // POOL_GROUP_DIM + g))],
        out_specs=[pl.BlockSpec((t, POOL_GROUP_DIM), lambda g: (0, POOL_BLK + g)),
                   pl.BlockSpec((1, POOL_GROUP_DIM, POOL_GROUP_DIM), lambda g: (g, 0, 0)),
                   pl.BlockSpec((1, 1, POOL_GROUP_DIM), lambda g: (g, 0, 0))],
        out_shape=[jax.ShapeDtypeStruct((t, PROJ_COLS), BF16),
                   jax.ShapeDtypeStruct((POOL_GROUPS, POOL_GROUP_DIM, POOL_GROUP_DIM), F32),
                   jax.ShapeDtypeStruct((POOL_GROUPS, 1, POOL_GROUP_DIM), F32)],
        compiler_params=_params("parallel"), name="pool_bwd",
    )(proj, pool_w, pool_scale, dmixin)


def _ln_stats(s):
    mu = jnp.mean(s, axis=1, keepdims=True)
    xc = s - mu
    var = jnp.mean(xc * xc, axis=1, keepdims=True)
    rstd = lax.rsqrt(var + LN_EPS)
    return xc * rstd, rstd


def _ln_fwd(h_in, y, g, b, *, name, tm=256):
    t, d = h_in.shape
    tm = min(tm, t)

    def body(h_ref, y_ref, g_ref, b_ref, o_ref, o16_ref):
        xhat, _ = _ln_stats(ALPHA * h_ref[...] + y_ref[...])
        out = xhat * g_ref[...] + b_ref[...]
        o_ref[...] = out
        o16_ref[...] = out.astype(BF16)

    row = pl.BlockSpec((tm, d), lambda i: (i, 0))
    vec = pl.BlockSpec((1, d), lambda i: (0, 0))
    return pl.pallas_call(
        body, grid=(t // tm,), in_specs=[row, row, vec, vec], out_specs=[row, row],
        out_shape=[jax.ShapeDtypeStruct((t, d), F32), jax.ShapeDtypeStruct((t, d), BF16)],
        compiler_params=_params("parallel"), name=name,
    )(h_in, y, g, b)


def _ln_loss_fwd(h_in, y, g, b, target, *, name, tm=256):
    t, d = h_in.shape
    tm = min(tm, t)

    def body(h_ref, y_ref, g_ref, b_ref, t_ref, dy_ref, sq_ref):
        @pl.when(pl.program_id(0) == 0)
        def _():
            sq_ref[...] = jnp.zeros_like(sq_ref)

        xhat, _ = _ln_stats(ALPHA * h_ref[...] + y_ref[...])
        err = xhat * g_ref[...] + b_ref[...] - t_ref[...]
        dy_ref[...] = err * (1.0 / d)
        sq_ref[...] += jnp.sum(jnp.sum(err * err, axis=1, keepdims=True), axis=0, keepdims=True)

    row = pl.BlockSpec((tm, d), lambda i: (i, 0))
    vec = pl.BlockSpec((1, d), lambda i: (0, 0))
    return pl.pallas_call(
        body, grid=(t // tm,), in_specs=[row, row, vec, vec, row],
        out_specs=[row, pl.BlockSpec((1, LANE), lambda i: (0, 0))],
        out_shape=[jax.ShapeDtypeStruct((t, d), F32), jax.ShapeDtypeStruct((1, LANE), F32)],
        compiler_params=_params("arbitrary"), name=name,
    )(h_in, y, g, b, target)


def _ln_bwd(h_in, y, g, d_a, d_b, *, name, tm=256):
    t, d = h_in.shape
    tm = min(tm, t)
    has_b = d_b is not None

    def body(*refs):
        if has_b:
            h_ref, y_ref, g_ref, da_ref, db_ref, ds_ref, ds16_ref, dg_ref, dbias_ref = refs
        else:
            h_ref, y_ref, g_ref, da_ref, ds_ref, ds16_ref, dg_ref, dbias_ref = refs

        @pl.when(pl.program_id(0) == 0)
        def _():
            dg_ref[...] = jnp.zeros_like(dg_ref)
            dbias_ref[...] = jnp.zeros_like(dbias_ref)

        xhat, rstd = _ln_stats(ALPHA * h_ref[...] + y_ref[...])
        dout = da_ref[...]
        if has_b:
            dout = dout + ALPHA * db_ref[...]
        dxhat = dout * g_ref[...]
        m1 = jnp.mean(dxhat, axis=1, keepdims=True)
        m2 = jnp.mean(dxhat * xhat, axis=1, keepdims=True)
        ds = rstd * (dxhat - m1 - xhat * m2)
        ds_ref[...] = ds
        ds16_ref[...] = ds.astype(BF16)
        dg_ref[...] += jnp.sum(dout * xhat, axis=0, keepdims=True)
        dbias_ref[...] += jnp.sum(dout, axis=0, keepdims=True)

    row = pl.BlockSpec((tm, d), lambda i: (i, 0))
    vec = pl.BlockSpec((1, d), lambda i: (0, 0))
    args = [h_in, y, g, d_a] + ([d_b] if has_b else [])
    return pl.pallas_call(
        body, grid=(t // tm,), in_specs=[row, row, vec, row] + ([row] if has_b else []),
        out_specs=[row, row, vec, vec],
        out_shape=[jax.ShapeDtypeStruct((t, d), F32), jax.ShapeDtypeStruct((t, d), BF16),
                   jax.ShapeDtypeStruct((1, d), F32), jax.ShapeDtypeStruct((1, d), F32)],
        compiler_params=_params("arbitrary"), name=name,
    )(*args)


def _attn_fn(q, k, v, dots):
    nn, nt, _ = dots
    s = nt(q, k) * (XATTN_HEAD_DIM ** -0.5)
    s = s - lax.stop_gradient(jnp.max(s, axis=1, keepdims=True))
    e = jnp.exp(s)
    p = e / jnp.sum(e, axis=1, keepdims=True)
    return nn(p, v)


def _attn_fwd(q, k, v, tq=512):
    t = q.shape[0]
    tq = min(tq, t)

    def body(q_ref, k_ref, v_ref, o_ref):
        o_ref[...] = _attn_fn(q_ref[...], k_ref[...], v_ref[...], _BDOT_PLAIN).astype(BF16)

    qs = pl.BlockSpec((tq, XATTN_HEAD_DIM), lambda h, i: (i, h))
    ks = pl.BlockSpec((MEM_LEN, XATTN_HEAD_DIM), lambda h, i: (0, h))
    return pl.pallas_call(
        body, grid=(XATTN_HEADS, t // tq), in_specs=[qs, ks, ks], out_specs=qs,
        out_shape=jax.ShapeDtypeStruct(q.shape, BF16), compiler_params=_params("parallel", "parallel"), name="xattn_fwd",
    )(q, k, v)


def _attn_bwd(q, k, v, do, tq=512):
    t = q.shape[0]
    tq = min(tq, t)

    def body(q_ref, k_ref, v_ref, do_ref, dq_ref, dk_ref, dv_ref):
        @pl.when(pl.program_id(1) == 0)
        def _():
            dk_ref[...] = jnp.zeros_like(dk_ref)
            dv_ref[...] = jnp.zeros_like(dv_ref)

        _, vjp = jax.vjp(lambda a, b, c: _attn_fn(a, b, c, _BDOT_VJP), q_ref[...].astype(F32), k_ref[...].astype(F32),
                         v_ref[...].astype(F32))
        dq, dk, dv = vjp(do_ref[...].astype(F32))
        dq_ref[...] = dq.astype(BF16)
        dk_ref[...] += dk
        dv_ref[...] += dv

    qs = pl.BlockSpec((tq, XATTN_HEAD_DIM), lambda h, i: (i, h))
    ks = pl.BlockSpec((MEM_LEN, XATTN_HEAD_DIM), lambda h, i: (0, h))
    return pl.pallas_call(
        body, grid=(XATTN_HEADS, t // tq), in_specs=[qs, ks, ks, qs], out_specs=[qs, ks, ks],
        out_shape=[jax.ShapeDtypeStruct(q.shape, BF16), jax.ShapeDtypeStruct(k.shape, F32), jax.ShapeDtypeStruct(v.shape, F32)],
        compiler_params=_params("parallel", "arbitrary"), name="xattn_bwd",
    )(q, k, v, do)


def _local_step(x, mem, target, weights_of, grads_ready):
    def behind(vec, token):
        return vec if token is None else vec + token

    x16 = _cast_bf16(x, name="cast_x")
    w = dict(weights_of("mixer", None))
    proj = _mm(x16, w["w_in"], tn=768, name="mm_in_proj")
    post = _gdn_prep_fwd(proj, w["conv_w"])
    mixin = _pool_fwd(proj, w["pool_w"], w["pool_scale"])
    token = weights_of("ahead", mixin)
    chunked = _gdn_local_fwd(post, proj, behind(w["alog_row"], token), w["dtb_row"])
    o_raw, saved = _gdn_state_fwd(*chunked)
    mixin = _onorm_fwd(o_raw, proj, w["gdn_norm_w"], mixin)
    w.update(weights_of("attn", mixin))
    mix = _mm(mixin, w["w_out"], name="mm_out_proj")
    h1, h1_16 = _ln_fwd(x, mix, w["ln1_g"], w["ln1_b"], name="ln1_fwd")
    xq = _mm(h1_16, w["xq_w"], out_dtype=BF16, name="mm_xq")
    xk = _mm(mem, w["xk_w"], out_dtype=BF16, name="mm_xk")
    xv = _mm(mem, w["xv_w"], out_dtype=BF16, name="mm_xv")
    xo = _attn_fwd(xq, xk, xv)
    xa = _mm(xo, w["xo_w"], name="mm_xo")
    h2, h2_16 = _ln_fwd(h1, xa, w["ln2_g"], w["ln2_b"], name="ln2_fwd")
    w.update(weights_of("mlp", h2_16))
    act, relu = _mm(h2_16, w["w_up"], b_chunks=True, epi="relu2", name="mm_up")
    ff = _mm(act, w["w_down"], tn=1024, tk=512, name="mm_down")
    dy, sq = _ln_loss_fwd(h2, ff, w["ln3_g"], w["ln3_b"], target, name="ln3_loss_fwd")

    g = {}
    ds3, ds3_16, g["ln3_g"], g["ln3_b"] = _ln_bwd(h2, ff, w["ln3_g"], dy, None, name="ln3_bwd")
    gw_down = _mm(act, ds3_16, ta=True, out_dtype=BF16, tm=512, tn=D_MODEL, name="mm_gw_down")
    du = _mm(ds3_16, w["w_down"], tb=True, epi="mul2r", extra=relu, name="mm_du")
    gw_up = _mm(h2_16, du, ta=True, out_dtype=BF16, o_chunks=True, name="mm_gw_up")
    token = grads_ready("mlp", {"w_down": gw_down, "w_up": gw_up})
    dh2 = _mm(du, w["w_up"], tb=True, b_chunks=True, tn=1024, tk=512, name="mm_dh2")
    ds2, ds2_16, g["ln2_g"], g["ln2_b"] = _ln_bwd(h1, xa, behind(w["ln2_g"], token), dh2, ds3, name="ln2_bwd")
    gw_xo = _mm(xo, ds2_16, ta=True, out_dtype=BF16, name="mm_gw_xo")
    dxo = _mm(ds2_16, w["xo_w"], tb=True, out_dtype=BF16, name="mm_dxo")
    dxq, dxk, dxv = _attn_bwd(xq, xk, xv, dxo)
    gw_xq = _mm(h1_16, dxq, ta=True, out_dtype=BF16, name="mm_gw_xq")
    gw_xk = _mm(mem, dxk, ta=True, out_dtype=BF16, name="mm_gw_xk")
    gw_xv = _mm(mem, dxv, ta=True, out_dtype=BF16, name="mm_gw_xv")
    token = grads_ready("attn", {"xo_w": gw_xo, "xq_w": gw_xq, "xk_w": gw_xk, "xv_w": gw_xv})
    dh1 = _mm(dxq, w["xq_w"], tb=True, name="mm_dh1")
    ds1, ds1_16, g["ln1_g"], g["ln1_b"] = _ln_bwd(x, mix, behind(w["ln1_g"], token), dh1, ds2, name="ln1_bwd")
    gw_out = _mm(mixin, ds1_16, ta=True, out_dtype=BF16, name="mm_gw_out")
    dmixin = _mm(ds1_16, w["w_out"], tb=True, name="mm_dmixin")
    dproj, gw_pool, g["pool_scale"] = _pool_bwd(proj, w["pool_w"], w["pool_scale"], dmixin)
    token = grads_ready("mix", {"w_out": gw_out, "pool_w": gw_pool})
    do_raw, dproj, g["gdn_norm_w"] = _onorm_bwd(o_raw, proj, behind(w["gdn_norm_w"], token), dmixin, dproj)
    cots = _gdn_state_bwd(*chunked, saved, do_raw)
    dpost, dproj, g["alog_row"], g["dtb_row"] = _gdn_local_bwd(post, proj, w["alog_row"], w["dtb_row"], cots, dproj)
    dproj, g["conv_w"] = _gdn_prep_bwd(proj, w["conv_w"], dpost, dproj)
    gw_in = _mm(x16, dproj, ta=True, out_dtype=BF16, tn=768, name="mm_gw_in")
    grads_ready("in", {"w_in": gw_in})
    grad_x = _mm(dproj, w["w_in"], tb=True, tk=768, epi="add", extra=ds1, add_scale=ALPHA, name="mm_dx")
    return sq, grad_x, g


_MATRICES = ("w_in", "pool_w", "w_out", "xq_w", "xk_w", "xv_w", "xo_w", "w_up", "w_down")
_VECTORS = ("a_log", "dt_bias", "gdn_norm_w", "pool_scale", "ln1_g", "ln1_b", "ln2_g", "ln2_b", "ln3_g", "ln3_b")
_BA_SPLIT = BA_OFF + 2 * GDN_HEADS


def _lane_row(v, offset):
    return jnp.zeros((1, LANE), F32).at[0, offset:offset + v.shape[0]].set(v)


_GROUP_VECTORS = {"mixer": (), "attn": ("ln1_g", "ln1_b", "ln2_g", "ln2_b"), "mlp": ("ln3_g", "ln3_b")}


def _group_weights(group, full):
    w = {n: full[n].reshape(1, D_MODEL) for n in _GROUP_VECTORS[group]}
    if group == "mixer":
        w_in = full["w_in"]
        zeros = jnp.zeros((w_in.shape[0], POOL_OFF - _BA_SPLIT), w_in.dtype)
        w.update({
            "w_in": jnp.concatenate([w_in[:, :_BA_SPLIT], zeros, w_in[:, _BA_SPLIT:]], axis=1),
            "conv_w": full["conv_w"],
            "alog_row": _lane_row(full["a_log"], GDN_HEADS),
            "dtb_row": _lane_row(full["dt_bias"], GDN_HEADS),
            "gdn_norm_w": full["gdn_norm_w"].reshape(1, LANE),
            "pool_w": full["pool_w"],
            "pool_scale": full["pool_scale"].reshape(POOL_GROUPS, 1, POOL_GROUP_DIM),
        })
    elif group == "attn":
        w.update({n: full[n] for n in ("w_out", "xq_w", "xk_w", "xv_w", "xo_w")})
    else:
        w.update({n: full[n] for n in ("w_up", "w_down")})
    return w


def _unpad_w_in(g):
    return jnp.concatenate([g[:, :_BA_SPLIT], g[:, POOL_OFF:]], axis=1)


def _finish_small_grads(g):
    out = {"conv_w": g["conv_w"]}
    out["a_log"] = g["alog_row"][0, GDN_HEADS:2 * GDN_HEADS]
    out["dt_bias"] = g["dtb_row"][0, GDN_HEADS:2 * GDN_HEADS]
    out["gdn_norm_w"] = g["gdn_norm_w"].reshape(LANE)
    out["pool_scale"] = g["pool_scale"].reshape(POOL_GROUPS * POOL_GROUP_DIM)
    for n in ("ln1_g", "ln1_b", "ln2_g", "ln2_b", "ln3_g", "ln3_b"):
        out[n] = g[n].reshape(D_MODEL)
    return out


def _adamw_math(w, g, m, v):
    m = ADAM_B1 * m + (1.0 - ADAM_B1) * g
    v = ADAM_B2 * v + (1.0 - ADAM_B2) * (g * g)
    m_hat = m / (1.0 - ADAM_B1 ** ADAM_STEP)
    v_hat = v / (1.0 - ADAM_B2 ** ADAM_STEP)
    delta = -ADAM_LR * (m_hat / (jnp.sqrt(v_hat) + ADAM_EPS) + ADAM_WD * w)
    return delta, m, v


def _adamw_shard(parts, own, me, w, m, v, *, tr, name):
    s, r, c = parts.shape
    tr = min(tr, r)
    assert r % tr == 0, (name, r, tr)

    def body(me_ref, p_ref, own_ref, w_ref, m_ref, v_ref, g_ref, d_ref, nm_ref, nv_ref):
        mine = own_ref[...].astype(F32)
        g = None
        for i in range(s):
            part = jnp.where(me_ref[0] == i, mine, p_ref[i].astype(F32))
            g = part if g is None else g + part
        delta, nm, nv = _adamw_math(w_ref[...], g, m_ref[...], v_ref[...])
        g_ref[...] = g
        d_ref[...] = delta
        nm_ref[...] = nm
        nv_ref[...] = nv

    blk = pl.BlockSpec((tr, c), lambda i, me_ref: (i, 0))
    out = jax.ShapeDtypeStruct((r, c), F32)
    return pl.pallas_call(
        body,
        grid_spec=pltpu.PrefetchScalarGridSpec(
            num_scalar_prefetch=1, grid=(r // tr,),
            in_specs=[pl.BlockSpec((s, tr, c), lambda i, me_ref: (0, i, 0)),
                      pl.BlockSpec((None, tr, c), lambda i, me_ref: (me_ref[0], i, 0)), blk, blk, blk],
            out_specs=[blk, blk, blk, blk]),
        out_shape=[out, out, out, out], compiler_params=_params("parallel"), name=name,
    )(me, parts, own, w, m, v)


N_CHIPS = N_DEV // 2


def _chip_sums(chunks, from_sibling, core, *, tr, name):
    _, r, c = chunks.shape
    tr = min(tr, r)
    assert r % tr == 0, (name, r, tr)

    def body(core_ref, mine_ref, other_ref, o_ref):
        o_ref[...] = (mine_ref[...].astype(F32) + other_ref[...].astype(F32)).astype(o_ref.dtype)

    by_chip = pl.BlockSpec((None, tr, c), lambda q, i, core_ref: (q, i, 0))
    return pl.pallas_call(
        body,
        grid_spec=pltpu.PrefetchScalarGridSpec(
            num_scalar_prefetch=1, grid=(N_CHIPS, r // tr),
            in_specs=[pl.BlockSpec((None, tr, c), lambda q, i, core_ref: (2 * q + core_ref[0], i, 0)), by_chip],
            out_specs=by_chip),
        out_shape=jax.ShapeDtypeStruct((N_CHIPS, r, c), chunks.dtype), compiler_params=_params("parallel", "parallel"),
        name=name,
    )(core, chunks, from_sibling)


def _place():
    return lax.axis_index("x"), lax.axis_index("y"), lax.axis_index("c")


def _slot(px, py, pc):
    return 4 * px + 2 * py + pc


_HBM = pl.BlockSpec(memory_space=pltpu.HBM)


_SEM = pl.BlockSpec(memory_space=pltpu.SEMAPHORE)
_ANY = pl.BlockSpec(memory_space=pl.ANY)
_EFFECT = pltpu.SideEffectType.DATAFLOW_SIDE_EFFECTING
_N_PEERS = N_DEV - 1


def _peer(k, x, y, c):
    return (1 - x if k & 4 else x, 1 - y if k & 2 else y, 1 - c if k & 1 else c)


_EXCHANGE_BITS = {"gather_chips": (1, 2, 4, 6), "gather_pass": (2, 4, 6), "scatter_sibling": (1, 1, 1, 1),
                  "scatter_chips": (2, 4, 6)}


def _exchange_copy(mode, src, land, w, i, place, send_sems, recv_sems, receiving):
    bits = _EXCHANGE_BITS[mode]
    k = bits[i]
    peer = _peer(k, *place)
    me = _slot(*place)
    if mode == "gather_chips":
        to, src_ref, sent_to, got_at = peer, src[w], me, _slot(*peer)
    elif mode == "gather_pass":
        blk = _slot(*peer)
        to, src_ref, sent_to, got_at = _peer(1, *place), land[w].at[blk], blk, _slot(*_peer(k | 1, *place))
    elif mode == "scatter_sibling":
        to, src_ref, sent_to, got_at = peer, src[w].at[2 * i + 1 - place[2]], i, i
    else:
        to, src_ref, sent_to, got_at = peer, src[w].at[_slot(*peer) // 2], me // 2, _slot(*peer) // 2
    sem = w * len(bits) + i
    return pltpu.make_async_remote_copy(
        src_ref=src_ref, dst_ref=land[w].at[got_at if receiving else sent_to], send_sem=send_sems.at[sem],
        recv_sem=recv_sems.at[sem], device_id=to, device_id_type=MESH)


def _exchange_start(mode, srcs, lands, after, *, name):
    ns, nl = len(srcs), len(lands)
    n_sem = nl * len(_EXCHANGE_BITS[mode])

    def body(*refs):
        src, land = refs[:ns], refs[ns:ns + nl]
        send_sems, recv_sems = refs[ns + nl + 1:ns + nl + 3]
        token = refs[-1]
        place = _place()
        for w in range(nl):
            for i in range(len(_EXCHANGE_BITS[mode])):
                _exchange_copy(mode, src, land, w, i, place, send_sems, recv_sems, receiving=False).start()
        token[...] = jnp.zeros_like(token)

    sems = pltpu.SemaphoreType.DMA((n_sem,))
    arrays = list(srcs) + list(lands)
    res = pl.pallas_call(
        body, name=name, in_specs=[_HBM] * (ns + nl) + [_ANY],
        out_specs=(_SEM, _SEM, *([_HBM] * (ns + nl)), pl.BlockSpec(memory_space=pltpu.VMEM)),
        out_shape=(sems, sems, *[pltpu.HBM(a.shape, a.dtype) for a in arrays], jax.ShapeDtypeStruct((8, LANE), F32)),
        input_output_aliases={i: 2 + i for i in range(ns + nl)},
        compiler_params=pltpu.CompilerParams(has_side_effects=_EFFECT),
    )(*[pltpu.with_memory_space_constraint(a, pltpu.HBM) for a in arrays], after)
    return res[0], res[1], list(res[2:2 + ns]), list(res[2 + ns:2 + ns + nl]), res[-1]


def _exchange_wait(mode, started, after, *, name):
    send_sems, recv_sems, srcs, lands, _ = started
    ns, nl = len(srcs), len(lands)

    def body(*refs):
        src, land = refs[:ns], refs[ns:ns + nl]
        send_sems, recv_sems = refs[ns + nl:ns + nl + 2]
        place = _place()
        for w in range(nl):
            for i in range(len(_EXCHANGE_BITS[mode])):
                cp = _exchange_copy(mode, src, land, w, i, place, send_sems, recv_sems, receiving=True)
                cp.wait_send()
                cp.wait_recv()

    arrays = list(srcs) + list(lands)
    res = pl.pallas_call(
        body, name=name, in_specs=[_HBM] * (ns + nl) + [_SEM, _SEM, _ANY], out_specs=[_HBM] * (ns + nl),
        out_shape=[pltpu.HBM(a.shape, a.dtype) for a in arrays],
        input_output_aliases={i: i for i in range(ns + nl)},
        compiler_params=pltpu.CompilerParams(has_side_effects=_EFFECT),
    )(*arrays, send_sems, recv_sems, after)
    return list(res[:ns]), list(res[ns:])


def _small_allreduce_adamw(gvec, wvec, mvec, vvec):
    rows, length = gvec.shape

    def body(g_ref, w_ref, m_ref, v_ref, gs_ref, d_ref, nm_ref, nv_ref, slots, send_sems, recv_sems):
        x, y, c = _place()
        me = _slot(x, y, c)
        slots[me] = g_ref[...]
        sends = []
        for k in range(1, N_DEV):
            peer = _peer(k, x, y, c)
            sends.append(pltpu.make_async_remote_copy(
                src_ref=g_ref, dst_ref=slots.at[me], send_sem=send_sems.at[k - 1], recv_sem=recv_sems.at[k - 1],
                device_id=peer, device_id_type=MESH))
        for cp in sends:
            cp.start()
        for k in range(1, N_DEV):
            peer = _peer(k, x, y, c)
            pltpu.make_async_remote_copy(
                src_ref=g_ref, dst_ref=slots.at[_slot(*peer)], send_sem=send_sems.at[k - 1], recv_sem=recv_sems.at[k - 1],
                device_id=peer, device_id_type=MESH).wait_recv()
        for cp in sends:
            cp.wait_send()
        g = slots[0]
        for s in range(1, N_DEV):
            g = g + slots[s]
        delta, nm, nv = _adamw_math(w_ref[...], g, m_ref[...], v_ref[...])
        gs_ref[...] = g
        d_ref[...] = delta
        nm_ref[...] = nm
        nv_ref[...] = nv

    vmem = pl.BlockSpec(memory_space=pltpu.VMEM)
    out = jax.ShapeDtypeStruct((rows, length), F32)
    return pl.pallas_call(
        body, in_specs=[vmem] * 4, out_specs=[vmem] * 4, out_shape=[out] * 4,
        scratch_shapes=[pltpu.VMEM((N_DEV, rows, length), F32), pltpu.SemaphoreType.DMA((N_DEV - 1,)),
                        pltpu.SemaphoreType.DMA((N_DEV - 1,))],
        name="small_allreduce_adamw",
    )(gvec, wvec, mvec, vvec)


_SMALL_SEGMENTS = (("a_log", GDN_HEADS), ("dt_bias", GDN_HEADS), ("gdn_norm_w", HEAD_DIM), ("pool_scale", GDN_WIDTH),
                   ("ln1_g", D_MODEL), ("ln1_b", D_MODEL), ("ln2_g", D_MODEL), ("ln2_b", D_MODEL),
                   ("ln3_g", D_MODEL), ("ln3_b", D_MODEL), ("conv_w", CONV_K * QKV_COLS))
_SMALL_ROWS = 8
_SMALL_LEN = -(-sum(sz for _, sz in _SMALL_SEGMENTS) // (_SMALL_ROWS * LANE)) * LANE


def _pack_small(vals):
    parts = [vals[n].reshape(-1).astype(F32) if n in vals else jnp.zeros((sz,), F32) for n, sz in _SMALL_SEGMENTS]
    flat = jnp.concatenate(parts)
    flat = jnp.pad(flat, (0, _SMALL_ROWS * _SMALL_LEN - flat.shape[0]))
    return flat.reshape(_SMALL_ROWS, _SMALL_LEN)


def _unpack_small(vec):
    flat = vec.reshape(-1)
    out, off = {}, 0
    for n, sz in _SMALL_SEGMENTS:
        out[n] = flat[off:off + sz]
        off += sz
    return out


_WEIGHT_ORDER = ("w_in", "conv_w", "a_log", "dt_bias", "gdn_norm_w", "pool_w", "pool_scale", "w_out", "ln1_g", "ln1_b",
                 "xq_w", "xk_w", "xv_w", "xo_w", "ln2_g", "ln2_b", "w_up", "w_down", "ln3_g", "ln3_b")
_ADAM_ROWS = {"w_in": 256, "pool_w": 128, "w_out": 128, "xq_w": 128, "xk_w": 128, "xv_w": 128, "xo_w": 128,
              "w_up": 128, "w_down": 128}


def _shard2d(name, a):
    return a.reshape(-1, a.shape[-1]) if name == "pool_w" else a


def _gathered_to_full(name, gth):
    if name == "w_up":
        return gth
    if name in ("w_in", "conv_w"):
        return jnp.transpose(gth, (1, 0, 2)).reshape(gth.shape[1], N_DEV * gth.shape[2])
    if name == "pool_w":
        g4 = gth.reshape(N_DEV, POOL_GROUPS, POOL_GROUP_DIM // N_DEV, POOL_GROUP_DIM)
        return jnp.transpose(g4, (1, 0, 2, 3)).reshape(POOL_GROUPS, POOL_GROUP_DIM, POOL_GROUP_DIM)
    return gth.reshape(N_DEV * gth.shape[1], gth.shape[2])


def _full_to_chunks(name, full):
    if name == "w_up":
        return full
    if name == "w_in":
        r, cols = full.shape
        return jnp.transpose(full.reshape(r, N_DEV, cols // N_DEV), (1, 0, 2))
    if name == "pool_w":
        g4 = full.reshape(POOL_GROUPS, N_DEV, POOL_GROUP_DIM // N_DEV, POOL_GROUP_DIM)
        return jnp.transpose(g4, (1, 0, 2, 3)).reshape(N_DEV, POOL_GROUPS * POOL_GROUP_DIM // N_DEV, POOL_GROUP_DIM)
    return full.reshape(N_DEV, full.shape[0] // N_DEV, full.shape[1])


_GATHER_GROUPS = (("mixer", ("w_in", "conv_w", "pool_w")), ("attn", ("w_out", "xq_w", "xk_w", "xv_w", "xo_w")),
                  ("mlp", ("w_up", "w_down")))


def _grad_chunks(name, g):
    if name == "w_in":
        g = _unpad_w_in(g)
    return _full_to_chunks(name, g.astype(BF16))


def kernel(x, mem, w_in, conv_w, a_log, dt_bias, gdn_norm_w, pool_w, pool_scale, w_out, ln1_g, ln1_b, xq_w, xk_w, xv_w, xo_w, ln2_g, ln2_b, w_up, w_down, ln3_g, ln3_b, loss_target, m_w_in, m_conv_w, m_a_log, m_dt_bias, m_gdn_norm_w, m_pool_w, m_pool_scale, m_w_out, m_ln1_g, m_ln1_b, m_xq_w, m_xk_w, m_xv_w, m_xo_w, m_ln2_g, m_ln2_b, m_w_up, m_w_down, m_ln3_g, m_ln3_b, v_w_in, v_conv_w, v_a_log, v_dt_bias, v_gdn_norm_w, v_pool_w, v_pool_scale, v_w_out, v_ln1_g, v_ln1_b, v_xq_w, v_xk_w, v_xv_w, v_xo_w, v_ln2_g, v_ln2_b, v_w_up, v_w_down, v_ln3_g, v_ln3_b):
    args = dict(locals())
    wt = {n: args[n][0] for n in _WEIGHT_ORDER}
    mo = {n: args["m_" + n][0] for n in _WEIGHT_ORDER}
    vo = {n: args["v_" + n][0] for n in _WEIGHT_ORDER}

    me = _slot(*_place())
    me_arr = jnp.reshape(me, (1,)).astype(jnp.int32)
    nothing = jnp.zeros((8, LANE), F32)

    def landing_zones(names):
        shards = [_shard2d(n, wt[n]).astype(F32 if n == "conv_w" else BF16) for n in names]
        zones = [lax.dynamic_update_slice(lax.empty((N_DEV, *s.shape), s.dtype), s[None], (me, 0, 0)) for s in shards]
        return shards, zones

    chip_arr = jnp.reshape(me // 2, (1,)).astype(jnp.int32)
    core_arr = jnp.reshape(lax.axis_index("c"), (1,)).astype(jnp.int32)
    first, attn_names, mlp_names = (names for _, names in _GATHER_GROUPS)
    gathers = {}

    def gather_chips(group, names, after):
        shards, zones = landing_zones(names)
        gathers[group] = _exchange_start("gather_chips", shards, zones, after, name="gather_chips_" + group)
        return gathers[group][4]

    def gather_pass(group, after):
        _, zones = _exchange_wait("gather_chips", gathers[group], after, name=f"gather_chips_{group}_wait")
        gathers[group] = _exchange_start("gather_pass", [], zones, nothing, name="gather_pass_" + group)
        return gathers[group][4]

    def gathered(group, names, after, token=None):
        _, zones = _exchange_wait("gather_pass", gathers[group], after, name=f"gather_pass_{group}_wait")
        full = {n: _gathered_to_full(n, z) for n, z in zip(names, zones)}
        full.update({n: wt[n] if token is None else wt[n] + token for n in _VECTORS})
        return _group_weights(group, full)

    token = gather_chips("mixer", first, nothing)
    token = gather_chips("attn", attn_names, gather_pass("mixer", token))

    def weights_of(group, after):
        if group == "mixer":
            return gathered(group, first, gathers["attn"][4])
        if group == "ahead":
            return gather_chips("mlp", mlp_names, gather_pass("attn", after))[0:1, 0:1]
        if group == "attn":
            return gathered(group, attn_names, after)
        return gathered(group, mlp_names, gather_pass("mlp", after))

    scatters = {}
    in_flight = []

    def chip_stage(after):
        group, names, started = in_flight.pop()
        chunks, from_sibling = _exchange_wait("scatter_sibling", started, after, name=f"scatter_sibling_{group}_wait")
        sums = [_chip_sums(c, f, core_arr, tr=_ADAM_ROWS[n], name=f"chip_sums_{n}") for n, c, f in zip(names, chunks, from_sibling)]
        scatters[group] = (names, _exchange_start("scatter_chips", sums, [lax.empty(s.shape, s.dtype) for s in sums],
                                                  nothing, name="scatter_chips_" + group))

    def grads_ready(group, grads):
        names = tuple(grads)
        chunks = [_grad_chunks(n, grads[n]) for n in names]
        if in_flight:
            chip_stage(chunks[0])
        zones = [lax.empty((N_CHIPS, *c.shape[1:]), c.dtype) for c in chunks]
        started = _exchange_start("scatter_sibling", chunks, zones, nothing, name="scatter_sibling_" + group)
        in_flight.append((group, names, started))
        return started[4][0:1, 0:1]

    sq, grad_x, g = _local_step(x[0], mem[0], loss_target[0], weights_of, grads_ready)
    chip_stage(grad_x)
    small = _finish_small_grads(g)

    out = {}
    after = grad_x
    for group, (names, started) in scatters.items():
        sums, lands = _exchange_wait("scatter_chips", started, after, name=f"scatter_chips_{group}_wait")
        for n, parts, own in zip(names, lands, sums):
            res = _adamw_shard(parts, own, chip_arr, _shard2d(n, wt[n]), _shard2d(n, mo[n]), _shard2d(n, vo[n]),
                               tr=_ADAM_ROWS[n], name="adamw_" + n)
            out[n] = [r.reshape(args[n].shape) for r in res]
            after = res[1]

    packed, _ = lax.optimization_barrier((_pack_small(small), after))
    gs, ds, ms, vs = _small_allreduce_adamw(
        packed, _pack_small({n: wt[n] for n in _VECTORS}), _pack_small({n: mo[n] for n in _VECTORS}),
        _pack_small({n: vo[n] for n in _VECTORS}))
    gs, ds, ms, vs = _unpack_small(gs), _unpack_small(ds), _unpack_small(ms), _unpack_small(vs)
    cols = conv_w.shape[-1]
    conv_full = gs["conv_w"].reshape(CONV_K, QKV_COLS)
    conv_mine = lax.dynamic_slice(conv_full, (0, me * cols), (CONV_K, cols))[None]
    res = _adamw_shard(conv_mine, conv_mine, jnp.zeros((1,), jnp.int32), wt["conv_w"], mo["conv_w"], vo["conv_w"],
                       tr=CONV_K, name="adamw_conv_w")
    out["conv_w"] = [r.reshape(conv_w.shape) for r in res]
    for n in _VECTORS:
        out[n] = [t[n].reshape(args[n].shape) for t in (gs, ds, ms, vs)]

    loss = lax.psum(0.5 * sq[0, 0] / D_MODEL, ("x", "y", "c"))
    return (loss, grad_x[None], *[out[n][0] for n in _WEIGHT_ORDER], *[out[n][1] for n in _WEIGHT_ORDER],
            *[out[n][2] for n in _WEIGHT_ORDER], *[out[n][3] for n in _WEIGHT_ORDER])
```

```python
import functools
import math

import jax
import jax.numpy as jnp
from jax import lax
from jax.experimental import pallas as pl
from jax.experimental.pallas import tpu as pltpu

F32 = jnp.float32
BF16 = jnp.bfloat16
MESH = pl.DeviceIdType.MESH

N_DEV = 8
D_MODEL = 2048
GDN_WIDTH = 1024
GDN_HEADS = 8
HEAD_DIM = 128
CONV_K = 4
CHUNK = 64
POOL_GROUPS = 4
POOL_GROUP_DIM = 256
MEM_LEN = 256
XATTN_HEADS = 4
XATTN_HEAD_DIM = 512
D_FF = 8192
IN_COLS = 5136
ALPHA = 2.0 ** 0.25
LN_EPS = 1e-5
NORM_EPS = 1e-6

LANE = 128
QKV_COLS = 3 * GDN_WIDTH
Z_OFF = QKV_COLS
BA_OFF = 4 * GDN_WIDTH
POOL_OFF = BA_OFF + 2 * LANE
PROJ_COLS = POOL_OFF + GDN_WIDTH
Z_BLK = Z_OFF // LANE
BA_BLK = BA_OFF // LANE
POOL_BLK = POOL_OFF // POOL_GROUP_DIM

ADAM_LR = 0.001
ADAM_B1 = 0.9
ADAM_B2 = 0.999
ADAM_EPS = 1e-08
ADAM_WD = 0.01
ADAM_STEP = 10

VMEM_LIMIT_BYTES = 48 * 1024 * 1024


def _params(*sem):
    return pltpu.CompilerParams(dimension_semantics=sem if sem else None, vmem_limit_bytes=VMEM_LIMIT_BYTES)


def _make_dots(cast, precision, batched=False):
    lead = 1 if batched else 0
    batch = ((0,), (0,)) if batched else ((), ())

    def dg(a, b, ca, cb):
        if cast is not None:
            a = a.astype(cast)
            b = b.astype(cast)
        return lax.dot_general(a, b, (((ca + lead,), (cb + lead,)), batch), precision=precision, preferred_element_type=F32)

    def nn_(a, b):
        return dg(a, b, 1, 0)

    def nt_(a, b):
        return dg(a, b, 1, 1)

    def tn_(a, b):
        return dg(a, b, 0, 0)

    @jax.custom_vjp
    def nn(a, b):
        return nn_(a, b)

    nn.defvjp(lambda a, b: (nn_(a, b), (a, b)), lambda r, g: (nt_(g, r[1]), tn_(r[0], g)))

    @jax.custom_vjp
    def nt(a, b):
        return nt_(a, b)

    nt.defvjp(lambda a, b: (nt_(a, b), (a, b)), lambda r, g: (nn_(g, r[1]), tn_(g, r[0])))

    @jax.custom_vjp
    def tn(a, b):
        return tn_(a, b)

    tn.defvjp(lambda a, b: (tn_(a, b), (a, b)), lambda r, g: (nt_(r[1], g), nn_(r[0], g)))

    return (nn_, nt_, tn_), (nn, nt, tn)


_BDOT_PLAIN, _BDOT_VJP = _make_dots(BF16, None)
_BDOT_BATCH_PLAIN, _BDOT_BATCH_VJP = _make_dots(BF16, None, batched=True)
_FDOT_BATCH_PLAIN, _FDOT_BATCH_VJP = _make_dots(None, lax.Precision.HIGH, batched=True)


def _mm(a, b, *, ta=False, tb=False, out_dtype=F32, tm=None, tn=512, tk=None, epi=None, extra=None, add_scale=1.0,
        b_chunks=False, o_chunks=False, name):
    m, k = (a.shape[1], a.shape[0]) if ta else a.shape
    if b_chunks:
        n, kb = (b.shape[1], N_DEV * b.shape[2]) if tb else (N_DEV * b.shape[2], b.shape[1])
    else:
        n, kb = b.shape if tb else (b.shape[1], b.shape[0])
    assert kb == k, (name, a.shape, b.shape)
    tm, tn, tk = min(tm or m, m), min(tn, n), min(tk or k, k)
    assert m % tm == 0 and n % tn == 0 and k % tk == 0, (name, m, n, k)
    nk = k // tk
    dims = (((0 if ta else 1,), (1 if tb else 0,)), ((), ()))
    n_extra = 0 if epi in (None, "relu2") else 1
    n_out = 2 if epi == "relu2" else 1
    if epi in ("relu2", "mul2r"):
        out_dtype = BF16

    def body(*refs):
        a_ref, b_ref = refs[:2]
        c_ref = refs[2] if n_extra else None
        o_refs = refs[2 + n_extra:2 + n_extra + n_out]
        scr = refs[2 + n_extra + n_out:]
        r = lax.dot_general(a_ref[...].astype(BF16), b_ref[...].astype(BF16), dims, preferred_element_type=F32)

        def finish(v):
            if epi == "add":
                o_refs[0][...] = (v + add_scale * c_ref[...]).astype(out_dtype)
            elif epi == "relu2":
                p = jnp.maximum(v, 0.0)
                o_refs[0][...] = (p * p).astype(BF16)
                o_refs[1][...] = p.astype(BF16)
            elif epi == "mul2r":
                o_refs[0][...] = (v * (2.0 * c_ref[...].astype(F32))).astype(BF16)
            else:
                o_refs[0][...] = v.astype(out_dtype)

        if nk == 1:
            finish(r)
        else:
            acc = scr[0]
            kk = pl.program_id(2)

            @pl.when(kk == 0)
            def _():
                acc[...] = r

            @pl.when(kk > 0)
            def _():
                acc[...] += r

            @pl.when(kk == nk - 1)
            def _():
                finish(acc[...])

    a_spec = pl.BlockSpec((tk, tm), lambda i, j, kk: (kk, i)) if ta else pl.BlockSpec((tm, tk), lambda i, j, kk: (i, kk))
    if b_chunks and tb:
        kc = k // N_DEV // tk
        b_spec = pl.BlockSpec((None, tn, tk), lambda i, j, kk: (kk // kc, j, kk % kc))
    elif b_chunks:
        nc = n // N_DEV // tn
        b_spec = pl.BlockSpec((None, tk, tn), lambda i, j, kk: (j // nc, kk, j % nc))
    elif tb:
        b_spec = pl.BlockSpec((tn, tk), lambda i, j, kk: (j, kk))
    else:
        b_spec = pl.BlockSpec((tk, tn), lambda i, j, kk: (kk, j))
    mn_spec = pl.BlockSpec((tm, tn), lambda i, j, kk: (i, j))
    if o_chunks:
        oc = n // N_DEV // tn
        o_spec = pl.BlockSpec((None, tm, tn), lambda i, j, kk: (j // oc, i, j % oc))
        o_shape = jax.ShapeDtypeStruct((N_DEV, m, n // N_DEV), out_dtype)
    else:
        o_spec, o_shape = mn_spec, jax.ShapeDtypeStruct((m, n), out_dtype)
    res = pl.pallas_call(
        body, grid=(m // tm, n // tn, nk), in_specs=[a_spec, b_spec] + [mn_spec] * n_extra,
        out_specs=[o_spec] * n_out, out_shape=[o_shape] * n_out,
        scratch_shapes=[pltpu.VMEM((tm, tn), F32)] if nk > 1 else [],
        compiler_params=_params("parallel", "parallel", "arbitrary"), name=name,
    )(a, b, *([extra] if n_extra else []))
    return res if n_out > 1 else res[0]


def _cast_bf16(v, *, name, tm=512):
    t, d = v.shape
    tm = min(tm, t)

    def body(v_ref, o_ref):
        o_ref[...] = v_ref[...].astype(BF16)

    spec = pl.BlockSpec((tm, d), lambda i: (i, 0))
    return pl.pallas_call(body, grid=(t // tm,), in_specs=[spec], out_specs=spec,
                          out_shape=jax.ShapeDtypeStruct((t, d), BF16), compiler_params=_params("parallel"), name=name)(v)


def _shift_down(v, s):
    if s == 0:
        return v
    row = lax.broadcasted_iota(jnp.int32, v.shape, 0)
    return jnp.where(row >= s, pltpu.roll(v, s, axis=0), 0.0)


def _shift_up(v, s):
    if s == 0:
        return v
    t = v.shape[0]
    row = lax.broadcasted_iota(jnp.int32, v.shape, 0)
    return jnp.where(row < t - s, pltpu.roll(v, t - s, axis=0), 0.0)


def _post_col(j):
    return (j % GDN_HEADS) * 3 + j // GDN_HEADS


def _gdn_prep_fwd(proj, conv_w):
    t = proj.shape[0]

    def body(x_ref, w_ref, o_ref):
        j = pl.program_id(0)
        x = x_ref[...]
        y = jnp.zeros_like(x)
        for tap in range(CONV_K):
            y = y + w_ref[tap:tap + 1, :] * _shift_down(x, CONV_K - 1 - tap)
        c = y * jax.nn.sigmoid(y)
        nrm = c * lax.rsqrt(jnp.sum(c * c, axis=1, keepdims=True) + NORM_EPS)
        o_ref[...] = jnp.where(j < 2 * GDN_HEADS, nrm, c)

    return pl.pallas_call(
        body, grid=(QKV_COLS // LANE,),
        in_specs=[pl.BlockSpec((t, LANE), lambda j: (0, j)), pl.BlockSpec((CONV_K, LANE), lambda j: (0, j))],
        out_specs=pl.BlockSpec((t, LANE), lambda j: (0, _post_col(j))),
        out_shape=jax.ShapeDtypeStruct((t, QKV_COLS), F32),
        compiler_params=_params("parallel"), name="gdn_prep_fwd",
    )(proj, conv_w)


def _gdn_prep_bwd(proj, conv_w, dpost, dproj):
    t = proj.shape[0]

    def body(x_ref, w_ref, d_ref, _, dx_ref, dw_ref):
        j = pl.program_id(0)
        x = x_ref[...]
        xs = [_shift_down(x, CONV_K - 1 - tap) for tap in range(CONV_K)]
        y = jnp.zeros_like(x)
        for tap in range(CONV_K):
            y = y + w_ref[tap:tap + 1, :] * xs[tap]
        sig = jax.nn.sigmoid(y)
        c = y * sig
        r = lax.rsqrt(jnp.sum(c * c, axis=1, keepdims=True) + NORM_EPS)
        nrm = c * r
        d = d_ref[...]
        dc_norm = r * (d - nrm * jnp.sum(d * nrm, axis=1, keepdims=True))
        dc = jnp.where(j < 2 * GDN_HEADS, dc_norm, d)
        dy = dc * (sig * (1.0 + y * (1.0 - sig)))
        dx = jnp.zeros_like(x)
        for tap in range(CONV_K):
            dx = dx + _shift_up(w_ref[tap:tap + 1, :] * dy, CONV_K - 1 - tap)
            dw_ref[tap:tap + 1, :] = jnp.sum(dy * xs[tap], axis=0, keepdims=True)
        dx_ref[...] = dx.astype(dx_ref.dtype)

    return pl.pallas_call(
        body, grid=(QKV_COLS // LANE,),
        in_specs=[pl.BlockSpec((t, LANE), lambda j: (0, j)), pl.BlockSpec((CONV_K, LANE), lambda j: (0, j)),
                  pl.BlockSpec((t, LANE), lambda j: (0, _post_col(j))), pl.BlockSpec(memory_space=pl.ANY)],
        out_specs=[pl.BlockSpec((t, LANE), lambda j: (0, j)), pl.BlockSpec((CONV_K, LANE), lambda j: (0, j))],
        out_shape=[jax.ShapeDtypeStruct(dproj.shape, dproj.dtype), jax.ShapeDtypeStruct((CONV_K, QKV_COLS), F32)],
        input_output_aliases={3: 0},
        compiler_params=_params("parallel"), name="gdn_prep_bwd",
    )(proj, conv_w, dpost, dproj)


def _softplus(v):
    return jnp.maximum(v, 0.0) + jnp.log(1.0 + jnp.exp(-jnp.abs(v)))


def _tri_inv(low, nn):
    r = lax.broadcasted_iota(jnp.int32, (CHUNK, CHUNK), 0)
    c = lax.broadcasted_iota(jnp.int32, (CHUNK, CHUNK), 1)
    eye = (r == c).astype(F32)
    same_blk = lax.shift_right_logical(r, 4) == lax.shift_right_logical(c, 4)
    diag = jnp.where(same_blk, low, 0.0)
    off = low - diag
    n1 = -diag
    n2 = nn(n1, n1)
    n4 = nn(n2, n2)
    n8 = nn(n4, n4)
    inv_d = nn(nn(nn(eye + n1, eye + n2), eye + n4), eye + n8)
    m1 = nn(inv_d, off)
    m2 = nn(m1, m1)
    return nn(nn(eye - m1, eye + m2), inv_d)


LOCAL_HEADS_PER_STEP = 8


def _gdn_local_fn(qkv, ba, alog_row, dtb_row, first_head, bdots, fdots):
    nn, nt, tn = bdots
    fnn = fdots[0]
    n_heads = qkv.shape[1] // (3 * HEAD_DIM)
    part = lambda i, p: qkv[:, (3 * i + p) * HEAD_DIM:(3 * i + p + 1) * HEAD_DIM]
    q = jnp.stack([part(i, 0) for i in range(n_heads)]) * (HEAD_DIM ** -0.5)
    k = jnp.stack([part(i, 1) for i in range(n_heads)])
    v = jnp.stack([part(i, 2) for i in range(n_heads)])
    lane = lax.broadcasted_iota(jnp.int32, ba.shape, 1)
    bg = jnp.where(lane < GDN_HEADS, jax.nn.sigmoid(ba), -jnp.exp(alog_row) * _softplus(ba + dtb_row))
    pick = lambda l: jnp.sum(jnp.where(lane == l, bg, 0.0), axis=1, keepdims=True)
    beta = jnp.stack([pick(first_head + i) for i in range(n_heads)])
    g = jnp.stack([pick(first_head + i + GDN_HEADS) for i in range(n_heads)])

    r = lax.broadcasted_iota(jnp.int32, (CHUNK, CHUNK), 0)
    c = lax.broadcasted_iota(jnp.int32, (CHUNK, CHUNK), 1)
    incl = r >= c
    strict = r > c
    eye = r == c

    def to_row(col):
        return jnp.sum(jnp.where(eye, col, 0.0), axis=1, keepdims=True)

    gc = jnp.sum(jnp.where(incl, to_row(g), 0.0), axis=2, keepdims=True)
    diff = gc - to_row(gc)
    decay = jnp.where(incl, jnp.exp(jnp.where(incl, diff, 0.0)), 0.0)
    k_beta = k * beta
    v_beta = v * beta
    low = jnp.where(strict, nt(k_beta, k) * decay, 0.0)
    t_inv = _tri_inv(low, fnn)
    eg = jnp.exp(gc)
    u = fnn(t_inv, v_beta)
    w = fnn(t_inv, k_beta * eg)
    attn = jnp.where(incl, nt(q, k) * decay, 0.0)
    last = lax.broadcasted_iota(jnp.int32, (CHUNK, 1), 0) == CHUNK - 1
    g_last = jnp.sum(jnp.where(last, gc, 0.0), axis=1, keepdims=True)
    kdec = k * jnp.exp(g_last - gc)
    elast = jnp.broadcast_to(jnp.exp(g_last), (n_heads, 1, LANE))
    return u, w, q * eg, kdec, attn, elast


def _gdn_state_fn(u, w, qg, kdec, attn, elast, state, bdots):
    nn, _, tn = bdots
    v_new = u - nn(w, state)
    o = nn(qg, state) + nn(attn, v_new)
    return o, state * elast + tn(kdec, v_new)


def _gdn_local_fwd(post, proj, alog_row, dtb_row):
    t = post.shape[0]
    n_chunks = t // CHUNK
    hb = LOCAL_HEADS_PER_STEP

    def body(qkv_ref, ba_ref, al_ref, dt_ref, u_ref, w_ref, qg_ref, kd_ref, at_ref, el_ref):
        u, w, qg, kdec, attn, elast = _gdn_local_fn(qkv_ref[...], ba_ref[...], al_ref[...], dt_ref[...],
                                                    pl.program_id(1) * hb, _BDOT_BATCH_PLAIN, _FDOT_BATCH_PLAIN)
        for i in range(hb):
            cols = slice(i * HEAD_DIM, (i + 1) * HEAD_DIM)
            u_ref[:, cols] = u[i]
            w_ref[:, cols] = w[i].astype(BF16)
            qg_ref[:, cols] = qg[i].astype(BF16)
            kd_ref[:, cols] = kdec[i].astype(BF16)
        at_ref[...] = attn.astype(BF16)
        el_ref[:, 0] = elast

    wide = pl.BlockSpec((CHUNK, hb * HEAD_DIM), lambda n, j: (n, j))
    row = pl.BlockSpec((1, LANE), lambda n, j: (0, 0))
    return pl.pallas_call(
        body, grid=(n_chunks, GDN_HEADS // hb),
        in_specs=[pl.BlockSpec((CHUNK, hb * 3 * HEAD_DIM), lambda n, j: (n, j)),
                  pl.BlockSpec((CHUNK, LANE), lambda n, j: (n, BA_BLK)), row, row],
        out_specs=[wide, wide, wide, wide, pl.BlockSpec((hb, CHUNK, CHUNK), lambda n, j: (j, n, 0)),
                   pl.BlockSpec((hb, 1, 1, LANE), lambda n, j: (j, n, 0, 0))],
        out_shape=[jax.ShapeDtypeStruct((t, GDN_WIDTH), F32), jax.ShapeDtypeStruct((t, GDN_WIDTH), BF16),
                   jax.ShapeDtypeStruct((t, GDN_WIDTH), BF16), jax.ShapeDtypeStruct((t, GDN_WIDTH), BF16),
                   jax.ShapeDtypeStruct((GDN_HEADS, t, CHUNK), BF16),
                   jax.ShapeDtypeStruct((GDN_HEADS, n_chunks, 1, LANE), F32)],
        compiler_params=_params("parallel", "parallel"), name="gdn_local_fwd",
    )(post, proj, alog_row, dtb_row)


def _gdn_state_specs(n_of):
    wide = pl.BlockSpec((CHUNK, GDN_WIDTH), lambda n: (n_of(n), 0))
    attn = pl.BlockSpec((GDN_HEADS, CHUNK, CHUNK), lambda n: (0, n_of(n), 0))
    elast = pl.BlockSpec((GDN_HEADS, 1, 1, LANE), lambda n: (0, n_of(n), 0, 0))
    saved = pl.BlockSpec((GDN_HEADS, 1, HEAD_DIM, HEAD_DIM), lambda n: (0, n_of(n), 0, 0))
    return wide, attn, elast, saved


def _gdn_state_fwd(u, w, qg, kdec, attn, elast):
    t = u.shape[0]
    n_chunks = t // CHUNK

    def body(u_ref, w_ref, qg_ref, kd_ref, at_ref, el_ref, o_ref, save_ref, state_ref):
        @pl.when(pl.program_id(0) == 0)
        def _():
            state_ref[...] = jnp.zeros_like(state_ref)

        for h in range(GDN_HEADS):
            cols = slice(h * HEAD_DIM, (h + 1) * HEAD_DIM)
            state = state_ref[h]
            save_ref[h, 0] = state
            o, new_state = _gdn_state_fn(u_ref[:, cols], w_ref[:, cols], qg_ref[:, cols], kd_ref[:, cols], at_ref[h],
                                         el_ref[h, 0], state, _BDOT_PLAIN)
            o_ref[:, cols] = o
            state_ref[h] = new_state

    wide, attn_spec, elast_spec, saved_spec = _gdn_state_specs(lambda n: n)
    return pl.pallas_call(
        body, grid=(n_chunks,), in_specs=[wide, wide, wide, wide, attn_spec, elast_spec],
        out_specs=[wide, saved_spec],
        out_shape=[jax.ShapeDtypeStruct((t, GDN_WIDTH), F32),
                   jax.ShapeDtypeStruct((GDN_HEADS, n_chunks, HEAD_DIM, HEAD_DIM), F32)],
        scratch_shapes=[pltpu.VMEM((GDN_HEADS, HEAD_DIM, HEAD_DIM), F32)],
        compiler_params=_params("arbitrary"), name="gdn_state_fwd",
    )(u, w, qg, kdec, attn, elast)


def _gdn_state_bwd(u, w, qg, kdec, attn, elast, saved, do):
    t = u.shape[0]
    n_chunks = t // CHUNK
    last = n_chunks - 1

    def body(u_ref, w_ref, qg_ref, kd_ref, at_ref, el_ref, save_ref, do_ref,
             du_ref, dw_ref, dqg_ref, dkd_ref, dat_ref, del_ref, dstate_ref):
        @pl.when(pl.program_id(0) == 0)
        def _():
            dstate_ref[...] = jnp.zeros_like(dstate_ref)

        for h in range(GDN_HEADS):
            cols = slice(h * HEAD_DIM, (h + 1) * HEAD_DIM)
            _, vjp = jax.vjp(
                lambda *a: _gdn_state_fn(*a, _BDOT_VJP), u_ref[:, cols], w_ref[:, cols].astype(F32),
                qg_ref[:, cols].astype(F32), kd_ref[:, cols].astype(F32), at_ref[h].astype(F32), el_ref[h, 0], save_ref[h, 0])
            du, dw, dqg, dkd, dat, de, dstate = vjp((do_ref[:, cols], dstate_ref[h]))
            du_ref[:, cols] = du
            dw_ref[:, cols] = dw
            dqg_ref[:, cols] = dqg
            dkd_ref[:, cols] = dkd
            dat_ref[h] = dat
            del_ref[h, 0] = de
            dstate_ref[h] = dstate

    wide, attn_spec, elast_spec, saved_spec = _gdn_state_specs(lambda n: last - n)
    wide_f32 = jax.ShapeDtypeStruct((t, GDN_WIDTH), F32)
    return pl.pallas_call(
        body, grid=(n_chunks,), in_specs=[wide, wide, wide, wide, attn_spec, elast_spec, saved_spec, wide],
        out_specs=[wide, wide, wide, wide, attn_spec, elast_spec],
        out_shape=[wide_f32, wide_f32, wide_f32, wide_f32, jax.ShapeDtypeStruct((GDN_HEADS, t, CHUNK), F32),
                   jax.ShapeDtypeStruct((GDN_HEADS, n_chunks, 1, LANE), F32)],
        scratch_shapes=[pltpu.VMEM((GDN_HEADS, HEAD_DIM, HEAD_DIM), F32)],
        compiler_params=_params("arbitrary"), name="gdn_state_bwd",
    )(u, w, qg, kdec, attn, elast, saved, do)


def _gdn_local_bwd(post, proj, alog_row, dtb_row, cots, dproj):
    t = post.shape[0]
    n_chunks = t // CHUNK
    hb = LOCAL_HEADS_PER_STEP
    n_steps = GDN_HEADS // hb

    def body(qkv_ref, ba_ref, al_ref, dt_ref, du_ref, dw_ref, dqg_ref, dkd_ref, dat_ref, del_ref, _,
             dqkv_ref, dba_ref, dal_ref, ddt_ref, dba_acc):
        n = pl.program_id(0)
        j = pl.program_id(1)

        @pl.when((n == 0) & (j == 0))
        def _():
            dal_ref[...] = jnp.zeros_like(dal_ref)
            ddt_ref[...] = jnp.zeros_like(ddt_ref)

        @pl.when(j == 0)
        def _():
            dba_acc[...] = jnp.zeros_like(dba_acc)

        heads = lambda ref: jnp.stack([ref[:, i * HEAD_DIM:(i + 1) * HEAD_DIM] for i in range(hb)])
        _, vjp = jax.vjp(lambda a, b, c, d: _gdn_local_fn(a, b, c, d, j * hb, _BDOT_BATCH_VJP, _FDOT_BATCH_VJP),
                         qkv_ref[...], ba_ref[...], al_ref[...], dt_ref[...])
        dqkv, dba, dal, ddt = vjp((heads(du_ref), heads(dw_ref), heads(dqg_ref), heads(dkd_ref), dat_ref[...],
                                   del_ref[:, 0]))
        dqkv_ref[...] = dqkv
        dba_acc[...] += dba
        dal_ref[...] += dal
        ddt_ref[...] += ddt

        @pl.when(j == n_steps - 1)
        def _():
            dba_ref[:, 0:LANE] = dba_acc[...].astype(dba_ref.dtype)
            dba_ref[:, LANE:2 * LANE] = jnp.zeros((CHUNK, LANE), dba_ref.dtype)

    wide = pl.BlockSpec((CHUNK, hb * HEAD_DIM), lambda n, j: (n, j))
    qkv_spec = pl.BlockSpec((CHUNK, hb * 3 * HEAD_DIM), lambda n, j: (n, j))
    row = pl.BlockSpec((1, LANE), lambda n, j: (0, 0))
    return pl.pallas_call(
        body, grid=(n_chunks, n_steps),
        in_specs=[qkv_spec, pl.BlockSpec((CHUNK, LANE), lambda n, j: (n, BA_BLK)), row, row, wide, wide, wide, wide,
                  pl.BlockSpec((hb, CHUNK, CHUNK), lambda n, j: (j, n, 0)),
                  pl.BlockSpec((hb, 1, 1, LANE), lambda n, j: (j, n, 0, 0)), pl.BlockSpec(memory_space=pl.ANY)],
        out_specs=[qkv_spec, pl.BlockSpec((CHUNK, 2 * LANE), lambda n, j: (n, BA_BLK // 2)), row, row],
        out_shape=[jax.ShapeDtypeStruct((t, QKV_COLS), F32), jax.ShapeDtypeStruct(dproj.shape, dproj.dtype),
                   jax.ShapeDtypeStruct((1, LANE), F32), jax.ShapeDtypeStruct((1, LANE), F32)],
        input_output_aliases={10: 1},
        scratch_shapes=[pltpu.VMEM((CHUNK, LANE), F32)],
        compiler_params=_params("arbitrary", "arbitrary"), name="gdn_local_bwd",
    )(post, proj, alog_row, dtb_row, *cots, dproj)


def _onorm_fn(o, z, w):
    return o * lax.rsqrt(jnp.mean(o * o, axis=1, keepdims=True) + NORM_EPS) * w * (z * jax.nn.sigmoid(z))


def _onorm_fwd(o_raw, proj, norm_w, mixin, tm=512):
    t = o_raw.shape[0]
    tm = min(tm, t)

    def body(o_ref, z_ref, w_ref, _, out_ref):
        out_ref[...] = _onorm_fn(o_ref[...], z_ref[...], w_ref[...]).astype(out_ref.dtype)

    return pl.pallas_call(
        body, grid=(t // tm, GDN_HEADS),
        in_specs=[pl.BlockSpec((tm, LANE), lambda i, h: (i, h)), pl.BlockSpec((tm, LANE), lambda i, h: (i, Z_BLK + h)),
                  pl.BlockSpec((1, LANE), lambda i, h: (0, 0)), pl.BlockSpec(memory_space=pl.ANY)],
        out_specs=pl.BlockSpec((tm, LANE), lambda i, h: (i, h)),
        out_shape=jax.ShapeDtypeStruct(mixin.shape, mixin.dtype), input_output_aliases={3: 0},
        compiler_params=_params("parallel", "parallel"), name="gdn_onorm_fwd",
    )(o_raw, proj, norm_w, mixin)


def _onorm_bwd(o_raw, proj, norm_w, dmixin, dproj, tm=512):
    t = o_raw.shape[0]
    tm = min(tm, t)

    def body(o_ref, z_ref, w_ref, d_ref, _, do_ref, dz_ref, dw_ref):
        @pl.when((pl.program_id(0) == 0) & (pl.program_id(1) == 0))
        def _():
            dw_ref[...] = jnp.zeros_like(dw_ref)

        _, vjp = jax.vjp(_onorm_fn, o_ref[...], z_ref[...], w_ref[...])
        do, dz, dw = vjp(d_ref[...])
        do_ref[...] = do
        dz_ref[...] = dz.astype(dz_ref.dtype)
        dw_ref[...] += dw

    return pl.pallas_call(
        body, grid=(t // tm, GDN_HEADS),
        in_specs=[pl.BlockSpec((tm, LANE), lambda i, h: (i, h)), pl.BlockSpec((tm, LANE), lambda i, h: (i, Z_BLK + h)),
                  pl.BlockSpec((1, LANE), lambda i, h: (0, 0)), pl.BlockSpec((tm, LANE), lambda i, h: (i, h)),
                  pl.BlockSpec(memory_space=pl.ANY)],
        out_specs=[pl.BlockSpec((tm, LANE), lambda i, h: (i, h)), pl.BlockSpec((tm, LANE), lambda i, h: (i, Z_BLK + h)),
                   pl.BlockSpec((1, LANE), lambda i, h: (0, 0))],
        out_shape=[jax.ShapeDtypeStruct((t, GDN_WIDTH), F32), jax.ShapeDtypeStruct(dproj.shape, dproj.dtype),
                   jax.ShapeDtypeStruct((1, LANE), F32)],
        input_output_aliases={4: 1},
        compiler_params=_params("arbitrary", "arbitrary"), name="gdn_onorm_bwd",
    )(o_raw, proj, norm_w, dmixin, dproj)


def _pool_select(levels, gi):
    out = levels[-1]
    for lvl in range(len(levels) - 2, -1, -1):
        out = jnp.where(gi == lvl, levels[lvl], out)
    return out


def _pool_count(shape, gi):
    pos = lax.broadcasted_iota(jnp.int32, shape, 0)
    win = lax.shift_left(jnp.int32(2), gi)
    return jnp.minimum(pos + 1, win).astype(F32)


def _pooled(p, gi):
    acc = p
    levels = []
    for lvl in range(POOL_GROUPS):
        acc = acc + _shift_down(acc, 1 << lvl)
        levels.append(acc)
    return _pool_select(levels, gi) / _pool_count(p.shape, gi) - p


def _pool_fwd(proj, pool_w, pool_scale):
    t = proj.shape[0]

    def body(p_ref, w_ref, s_ref, out_ref):
        gi = pl.program_id(0)
        pooled = _pooled(p_ref[...], gi)
        out_ref[...] = (_BDOT_PLAIN[0](pooled, w_ref[0]) * s_ref[0]).astype(out_ref.dtype)

    return pl.pallas_call(
        body, grid=(POOL_GROUPS,),
        in_specs=[pl.BlockSpec((t, POOL_GROUP_DIM), lambda g: (0, POOL_BLK + g)),
                  pl.BlockSpec((1, POOL_GROUP_DIM, POOL_GROUP_DIM), lambda g: (g, 0, 0)),
                  pl.BlockSpec((1, 1, POOL_GROUP_DIM), lambda g: (g, 0, 0))],
        out_specs=pl.BlockSpec((t, POOL_GROUP_DIM), lambda g: (0, GDN_WIDTH // POOL_GROUP_DIM + g)),
        out_shape=jax.ShapeDtypeStruct((t, 2 * GDN_WIDTH), BF16),
        compiler_params=_params("parallel"), name="pool_fwd",
    )(proj, pool_w, pool_scale)


def _pool_bwd(proj, pool_w, pool_scale, dmixin):
    t = proj.shape[0]
    nn, nt, tn = _BDOT_PLAIN

    def body(p_ref, w_ref, s_ref, d_ref, dp_ref, dw_ref, ds_ref):
        gi = pl.program_id(0)
        p = p_ref[...]
        pooled = _pooled(p, gi)
        mixed = nn(pooled, w_ref[0])
        d = d_ref[...]
        ds_ref[0] = jnp.sum(d * mixed, axis=0, keepdims=True)
        dmixed = d * s_ref[0]
        dw_ref[0] = tn(pooled, dmixed)
        dpooled = nt(dmixed, w_ref[0])
        acc = dpooled / _pool_count(p.shape, gi)
        levels = []
        for lvl in range(POOL_GROUPS):
            acc = acc + _shift_up(acc, 1 << lvl)
            levels.append(acc)
        dp_ref[...] = (_pool_select(levels, gi) - dpooled).astype(dp_ref.dtype)

    return pl.pallas_call(
        body, grid=(POOL_GROUPS,),
        in_specs=[pl.BlockSpec((t, POOL_GROUP_DIM), lambda g: (0, POOL_BLK + g)),
                  pl.BlockSpec((1, POOL_GROUP_DIM, POOL_GROUP_DIM), lambda g: (g, 0, 0)),
                  pl.BlockSpec((1, 1, POOL_GROUP_DIM), lambda g: (g, 0, 0)),
                  pl.BlockSpec((t, POOL_GROUP_DIM), lambda g: (0, GDN_WIDTH // POOL_GROUP_DIM + g))],
        out_specs=[pl.BlockSpec((t, POOL_GROUP_DIM), lambda g: (0, POOL_BLK + g)),
                   pl.BlockSpec((1, POOL_GROUP_DIM, POOL_GROUP_DIM), lambda g: (g, 0, 0)),
                   pl.BlockSpec((1, 1, POOL_GROUP_DIM), lambda g: (g, 0, 0))],
        out_shape=[jax.ShapeDtypeStruct((t, PROJ_COLS), BF16),
                   jax.ShapeDtypeStruct((POOL_GROUPS, POOL_GROUP_DIM, POOL_GROUP_DIM), F32),
                   jax.ShapeDtypeStruct((POOL_GROUPS, 1, POOL_GROUP_DIM), F32)],
        compiler_params=_params("parallel"), name="pool_bwd",
    )(proj, pool_w, pool_scale, dmixin)


def _ln_stats(s):
    mu = jnp.mean(s, axis=1, keepdims=True)
    xc = s - mu
    var = jnp.mean(xc * xc, axis=1, keepdims=True)
    rstd = lax.rsqrt(var + LN_EPS)
    return xc * rstd, rstd


def _ln_fwd(h_in, y, g, b, *, name, tm=256):
    t, d = h_in.shape
    tm = min(tm, t)

    def body(h_ref, y_ref, g_ref, b_ref, o_ref, o16_ref):
        xhat, _ = _ln_stats(ALPHA * h_ref[...] + y_ref[...])
        out = xhat * g_ref[...] + b_ref[...]
        o_ref[...] = out
        o16_ref[...] = out.astype(BF16)

    row = pl.BlockSpec((tm, d), lambda i: (i, 0))
    vec = pl.BlockSpec((1, d), lambda i: (0, 0))
    return pl.pallas_call(
        body, grid=(t // tm,), in_specs=[row, row, vec, vec], out_specs=[row, row],
        out_shape=[jax.ShapeDtypeStruct((t, d), F32), jax.ShapeDtypeStruct((t, d), BF16)],
        compiler_params=_params("parallel"), name=name,
    )(h_in, y, g, b)


def _ln_loss_fwd(h_in, y, g, b, target, *, name, tm=256):
    t, d = h_in.shape
    tm = min(tm, t)

    def body(h_ref, y_ref, g_ref, b_ref, t_ref, dy_ref, sq_ref):
        @pl.when(pl.program_id(0) == 0)
        def _():
            sq_ref[...] = jnp.zeros_like(sq_ref)

        xhat, _ = _ln_stats(ALPHA * h_ref[...] + y_ref[...])
        err = xhat * g_ref[...] + b_ref[...] - t_ref[...]
        dy_ref[...] = err * (1.0 / d)
        sq_ref[...] += jnp.sum(jnp.sum(err * err, axis=1, keepdims=True), axis=0, keepdims=True)

    row = pl.BlockSpec((tm, d), lambda i: (i, 0))
    vec = pl.BlockSpec((1, d), lambda i: (0, 0))
    return pl.pallas_call(
        body, grid=(t // tm,), in_specs=[row, row, vec, vec, row],
        out_specs=[row, pl.BlockSpec((1, LANE), lambda i: (0, 0))],
        out_shape=[jax.ShapeDtypeStruct((t, d), F32), jax.ShapeDtypeStruct((1, LANE), F32)],
        compiler_params=_params("arbitrary"), name=name,
    )(h_in, y, g, b, target)


def _ln_bwd(h_in, y, g, d_a, d_b, *, name, tm=256):
    t, d = h_in.shape
    tm = min(tm, t)
    has_b = d_b is not None

    def body(*refs):
        if has_b:
            h_ref, y_ref, g_ref, da_ref, db_ref, ds_ref, ds16_ref, dg_ref, dbias_ref = refs
        else:
            h_ref, y_ref, g_ref, da_ref, ds_ref, ds16_ref, dg_ref, dbias_ref = refs

        @pl.when(pl.program_id(0) == 0)
        def _():
            dg_ref[...] = jnp.zeros_like(dg_ref)
            dbias_ref[...] = jnp.zeros_like(dbias_ref)

        xhat, rstd = _ln_stats(ALPHA * h_ref[...] + y_ref[...])
        dout = da_ref[...]
        if has_b:
            dout = dout + ALPHA * db_ref[...]
        dxhat = dout * g_ref[...]
        m1 = jnp.mean(dxhat, axis=1, keepdims=True)
        m2 = jnp.mean(dxhat * xhat, axis=1, keepdims=True)
        ds = rstd * (dxhat - m1 - xhat * m2)
        ds_ref[...] = ds
        ds16_ref[...] = ds.astype(BF16)
        dg_ref[...] += jnp.sum(dout * xhat, axis=0, keepdims=True)
        dbias_ref[...] += jnp.sum(dout, axis=0, keepdims=True)

    row = pl.BlockSpec((tm, d), lambda i: (i, 0))
    vec = pl.BlockSpec((1, d), lambda i: (0, 0))
    args = [h_in, y, g, d_a] + ([d_b] if has_b else [])
    return pl.pallas_call(
        body, grid=(t // tm,), in_specs=[row, row, vec, row] + ([row] if has_b else []),
        out_specs=[row, row, vec, vec],
        out_shape=[jax.ShapeDtypeStruct((t, d), F32), jax.ShapeDtypeStruct((t, d), BF16),
                   jax.ShapeDtypeStruct((1, d), F32), jax.ShapeDtypeStruct((1, d), F32)],
        compiler_params=_params("arbitrary"), name=name,
    )(*args)


def _attn_fn(q, k, v, dots):
    nn, nt, _ = dots
    s = nt(q, k) * (XATTN_HEAD_DIM ** -0.5)
    s = s - lax.stop_gradient(jnp.max(s, axis=1, keepdims=True))
    e = jnp.exp(s)
    p = e / jnp.sum(e, axis=1, keepdims=True)
    return nn(p, v)


def _attn_fwd(q, k, v, tq=512):
    t = q.shape[0]
    tq = min(tq, t)

    def body(q_ref, k_ref, v_ref, o_ref):
        o_ref[...] = _attn_fn(q_ref[...], k_ref[...], v_ref[...], _BDOT_PLAIN).astype(BF16)

    qs = pl.BlockSpec((tq, XATTN_HEAD_DIM), lambda h, i: (i, h))
    ks = pl.BlockSpec((MEM_LEN, XATTN_HEAD_DIM), lambda h, i: (0, h))
    return pl.pallas_call(
        body, grid=(XATTN_HEADS, t // tq), in_specs=[qs, ks, ks], out_specs=qs,
        out_shape=jax.ShapeDtypeStruct(q.shape, BF16), compiler_params=_params("parallel", "parallel"), name="xattn_fwd",
    )(q, k, v)


def _attn_bwd(q, k, v, do, tq=512):
    t = q.shape[0]
    tq = min(tq, t)

    def body(q_ref, k_ref, v_ref, do_ref, dq_ref, dk_ref, dv_ref):
        @pl.when(pl.program_id(1) == 0)
        def _():
            dk_ref[...] = jnp.zeros_like(dk_ref)
            dv_ref[...] = jnp.zeros_like(dv_ref)

        _, vjp = jax.vjp(lambda a, b, c: _attn_fn(a, b, c, _BDOT_VJP), q_ref[...].astype(F32), k_ref[...].astype(F32),
                         v_ref[...].astype(F32))
        dq, dk, dv = vjp(do_ref[...].astype(F32))
        dq_ref[...] = dq.astype(BF16)
        dk_ref[...] += dk
        dv_ref[...] += dv

    qs = pl.BlockSpec((tq, XATTN_HEAD_DIM), lambda h, i: (i, h))
    ks = pl.BlockSpec((MEM_LEN, XATTN_HEAD_DIM), lambda h, i: (0, h))
    return pl.pallas_call(
        body, grid=(XATTN_HEADS, t // tq), in_specs=[qs, ks, ks, qs], out_specs=[qs, ks, ks],
        out_shape=[jax.ShapeDtypeStruct(q.shape, BF16), jax.ShapeDtypeStruct(k.shape, F32), jax.ShapeDtypeStruct(v.shape, F32)],
        compiler_params=_params("parallel", "arbitrary"), name="xattn_bwd",
    )(q, k, v, do)


def _local_step(x, mem, target, weights_of, grads_ready):
    def behind(vec, token):
        return vec if token is None else vec + token

    x16 = _cast_bf16(x, name="cast_x")
    w = dict(weights_of("mixer", None))
    proj = _mm(x16, w["w_in"], tn=768, name="mm_in_proj")
    post = _gdn_prep_fwd(proj, w["conv_w"])
    mixin = _pool_fwd(proj, w["pool_w"], w["pool_scale"])
    token = weights_of("ahead", mixin)
    chunked = _gdn_local_fwd(post, proj, behind(w["alog_row"], token), w["dtb_row"])
    o_raw, saved = _gdn_state_fwd(*chunked)
    mixin = _onorm_fwd(o_raw, proj, w["gdn_norm_w"], mixin)
    w.update(weights_of("attn", mixin))
    mix = _mm(mixin, w["w_out"], name="mm_out_proj")
    h1, h1_16 = _ln_fwd(x, mix, w["ln1_g"], w["ln1_b"], name="ln1_fwd")
    xq = _mm(h1_16, w["xq_w"], out_dtype=BF16, name="mm_xq")
    xk = _mm(mem, w["xk_w"], out_dtype=BF16, name="mm_xk")
    xv = _mm(mem, w["xv_w"], out_dtype=BF16, name="mm_xv")
    xo = _attn_fwd(xq, xk, xv)
    xa = _mm(xo, w["xo_w"], name="mm_xo")
    h2, h2_16 = _ln_fwd(h1, xa, w["ln2_g"], w["ln2_b"], name="ln2_fwd")
    w.update(weights_of("mlp", h2_16))
    act, relu = _mm(h2_16, w["w_up"], b_chunks=True, epi="relu2", name="mm_up")
    ff = _mm(act, w["w_down"], tn=1024, tk=512, name="mm_down")
    dy, sq = _ln_loss_fwd(h2, ff, w["ln3_g"], w["ln3_b"], target, name="ln3_loss_fwd")

    g = {}
    ds3, ds3_16, g["ln3_g"], g["ln3_b"] = _ln_bwd(h2, ff, w["ln3_g"], dy, None, name="ln3_bwd")
    gw_down = _mm(act, ds3_16, ta=True, out_dtype=BF16, tm=512, tn=D_MODEL, name="mm_gw_down")
    du = _mm(ds3_16, w["w_down"], tb=True, epi="mul2r", extra=relu, name="mm_du")
    gw_up = _mm(h2_16, du, ta=True, out_dtype=BF16, o_chunks=True, name="mm_gw_up")
    token = grads_ready("mlp", {"w_down": gw_down, "w_up": gw_up})
    dh2 = _mm(du, w["w_up"], tb=True, b_chunks=True, tn=1024, tk=512, name="mm_dh2")
    ds2, ds2_16, g["ln2_g"], g["ln2_b"] = _ln_bwd(h1, xa, behind(w["ln2_g"], token), dh2, ds3, name="ln2_bwd")
    gw_xo = _mm(xo, ds2_16, ta=True, out_dtype=BF16, name="mm_gw_xo")
    dxo = _mm(ds2_16, w["xo_w"], tb=True, out_dtype=BF16, name="mm_dxo")
    dxq, dxk, dxv = _attn_bwd(xq, xk, xv, dxo)
    gw_xq = _mm(h1_16, dxq, ta=True, out_dtype=BF16, name="mm_gw_xq")
    gw_xk = _mm(mem, dxk, ta=True, out_dtype=BF16, name="mm_gw_xk")
    gw_xv = _mm(mem, dxv, ta=True, out_dtype=BF16, name="mm_gw_xv")
    token = grads_ready("attn", {"xo_w": gw_xo, "xq_w": gw_xq, "xk_w": gw_xk, "xv_w": gw_xv})
    dh1 = _mm(dxq, w["xq_w"], tb=True, name="mm_dh1")
    ds1, ds1_16, g["ln1_g"], g["ln1_b"] = _ln_bwd(x, mix, behind(w["ln1_g"], token), dh1, ds2, name="ln1_bwd")
    gw_out = _mm(mixin, ds1_16, ta=True, out_dtype=BF16, name="mm_gw_out")
    dmixin = _mm(ds1_16, w["w_out"], tb=True, name="mm_dmixin")
    dproj, gw_pool, g["pool_scale"] = _pool_bwd(proj, w["pool_w"], w["pool_scale"], dmixin)
    token = grads_ready("mix", {"w_out": gw_out, "pool_w": gw_pool})
    do_raw, dproj, g["gdn_norm_w"] = _onorm_bwd(o_raw, proj, behind(w["gdn_norm_w"], token), dmixin, dproj)
    cots = _gdn_state_bwd(*chunked, saved, do_raw)
    dpost, dproj, g["alog_row"], g["dtb_row"] = _gdn_local_bwd(post, proj, w["alog_row"], w["dtb_row"], cots, dproj)
    dproj, g["conv_w"] = _gdn_prep_bwd(proj, w["conv_w"], dpost, dproj)
    gw_in = _mm(x16, dproj, ta=True, out_dtype=BF16, tn=768, name="mm_gw_in")
    grads_ready("in", {"w_in": gw_in})
    grad_x = _mm(dproj, w["w_in"], tb=True, tk=768, epi="add", extra=ds1, add_scale=ALPHA, name="mm_dx")
    return sq, grad_x, g


_MATRICES = ("w_in", "pool_w", "w_out", "xq_w", "xk_w", "xv_w", "xo_w", "w_up", "w_down")
_VECTORS = ("a_log", "dt_bias", "gdn_norm_w", "pool_scale", "ln1_g", "ln1_b", "ln2_g", "ln2_b", "ln3_g", "ln3_b")
_BA_SPLIT = BA_OFF + 2 * GDN_HEADS


def _lane_row(v, offset):
    return jnp.zeros((1, LANE), F32).at[0, offset:offset + v.shape[0]].set(v)


_GROUP_VECTORS = {"mixer": (), "attn": ("ln1_g", "ln1_b", "ln2_g", "ln2_b"), "mlp": ("ln3_g", "ln3_b")}


def _group_weights(group, full):
    w = {n: full[n].reshape(1, D_MODEL) for n in _GROUP_VECTORS[group]}
    if group == "mixer":
        w_in = full["w_in"]
        zeros = jnp.zeros((w_in.shape[0], POOL_OFF - _BA_SPLIT), w_in.dtype)
        w.update({
            "w_in": jnp.concatenate([w_in[:, :_BA_SPLIT], zeros, w_in[:, _BA_SPLIT:]], axis=1),
            "conv_w": full["conv_w"],
            "alog_row": _lane_row(full["a_log"], GDN_HEADS),
            "dtb_row": _lane_row(full["dt_bias"], GDN_HEADS),
            "gdn_norm_w": full["gdn_norm_w"].reshape(1, LANE),
            "pool_w": full["pool_w"],
            "pool_scale": full["pool_scale"].reshape(POOL_GROUPS, 1, POOL_GROUP_DIM),
        })
    elif group == "attn":
        w.update({n: full[n] for n in ("w_out", "xq_w", "xk_w", "xv_w", "xo_w")})
    else:
        w.update({n: full[n] for n in ("w_up", "w_down")})
    return w


def _unpad_w_in(g):
    return jnp.concatenate([g[:, :_BA_SPLIT], g[:, POOL_OFF:]], axis=1)


def _finish_small_grads(g):
    out = {"conv_w": g["conv_w"]}
    out["a_log"] = g["alog_row"][0, GDN_HEADS:2 * GDN_HEADS]
    out["dt_bias"] = g["dtb_row"][0, GDN_HEADS:2 * GDN_HEADS]
    out["gdn_norm_w"] = g["gdn_norm_w"].reshape(LANE)
    out["pool_scale"] = g["pool_scale"].reshape(POOL_GROUPS * POOL_GROUP_DIM)
    for n in ("ln1_g", "ln1_b", "ln2_g", "ln2_b", "ln3_g", "ln3_b"):
        out[n] = g[n].reshape(D_MODEL)
    return out


def _adamw_math(w, g, m, v):
    m = ADAM_B1 * m + (1.0 - ADAM_B1) * g
    v = ADAM_B2 * v + (1.0 - ADAM_B2) * (g * g)
    m_hat = m / (1.0 - ADAM_B1 ** ADAM_STEP)
    v_hat = v / (1.0 - ADAM_B2 ** ADAM_STEP)
    delta = -ADAM_LR * (m_hat / (jnp.sqrt(v_hat) + ADAM_EPS) + ADAM_WD * w)
    return delta, m, v


def _adamw_shard(parts, own, me, w, m, v, *, tr, name):
    s, r, c = parts.shape
    tr = min(tr, r)
    assert r % tr == 0, (name, r, tr)

    def body(me_ref, p_ref, own_ref, w_ref, m_ref, v_ref, g_ref, d_ref, nm_ref, nv_ref):
        mine = own_ref[...].astype(F32)
        g = None
        for i in range(s):
            part = jnp.where(me_ref[0] == i, mine, p_ref[i].astype(F32))
            g = part if g is None else g + part
        delta, nm, nv = _adamw_math(w_ref[...], g, m_ref[...], v_ref[...])
        g_ref[...] = g
        d_ref[...] = delta
        nm_ref[...] = nm
        nv_ref[...] = nv

    blk = pl.BlockSpec((tr, c), lambda i, me_ref: (i, 0))
    out = jax.ShapeDtypeStruct((r, c), F32)
    return pl.pallas_call(
        body,
        grid_spec=pltpu.PrefetchScalarGridSpec(
            num_scalar_prefetch=1, grid=(r // tr,),
            in_specs=[pl.BlockSpec((s, tr, c), lambda i, me_ref: (0, i, 0)),
                      pl.BlockSpec((None, tr, c), lambda i, me_ref: (me_ref[0], i, 0)), blk, blk, blk],
            out_specs=[blk, blk, blk, blk]),
        out_shape=[out, out, out, out], compiler_params=_params("parallel"), name=name,
    )(me, parts, own, w, m, v)


N_CHIPS = N_DEV // 2


def _chip_sums(chunks, from_sibling, core, *, tr, name):
    _, r, c = chunks.shape
    tr = min(tr, r)
    assert r % tr == 0, (name, r, tr)

    def body(core_ref, mine_ref, other_ref, o_ref):
        o_ref[...] = (mine_ref[...].astype(F32) + other_ref[...].astype(F32)).astype(o_ref.dtype)

    by_chip = pl.BlockSpec((None, tr, c), lambda q, i, core_ref: (q, i, 0))
    return pl.pallas_call(
        body,
        grid_spec=pltpu.PrefetchScalarGridSpec(
            num_scalar_prefetch=1, grid=(N_CHIPS, r // tr),
            in_specs=[pl.BlockSpec((None, tr, c), lambda q, i, core_ref: (2 * q + core_ref[0], i, 0)), by_chip],
            out_specs=by_chip),
        out_shape=jax.ShapeDtypeStruct((N_CHIPS, r, c), chunks.dtype), compiler_params=_params("parallel", "parallel"),
        name=name,
    )(core, chunks, from_sibling)


def _place():
    return lax.axis_index("x"), lax.axis_index("y"), lax.axis_index("c")


def _slot(px, py, pc):
    return 4 * px + 2 * py + pc


_HBM = pl.BlockSpec(memory_space=pltpu.HBM)


_SEM = pl.BlockSpec(memory_space=pltpu.SEMAPHORE)
_ANY = pl.BlockSpec(memory_space=pl.ANY)
_EFFECT = pltpu.SideEffectType.DATAFLOW_SIDE_EFFECTING
_N_PEERS = N_DEV - 1


def _peer(k, x, y, c):
    return (1 - x if k & 4 else x, 1 - y if k & 2 else y, 1 - c if k & 1 else c)


_EXCHANGE_BITS = {"gather_chips": (1, 2, 4, 6), "gather_pass": (2, 4, 6), "scatter_sibling": (1, 1, 1, 1),
                  "scatter_chips": (2, 4, 6)}


def _exchange_copy(mode, src, land, w, i, place, send_sems, recv_sems, receiving):
    bits = _EXCHANGE_BITS[mode]
    k = bits[i]
    peer = _peer(k, *place)
    me = _slot(*place)
    if mode == "gather_chips":
        to, src_ref, sent_to, got_at = peer, src[w], me, _slot(*peer)
    elif mode == "gather_pass":
        blk = _slot(*peer)
        to, src_ref, sent_to, got_at = _peer(1, *place), land[w].at[blk], blk, _slot(*_peer(k | 1, *place))
    elif mode == "scatter_sibling":
        to, src_ref, sent_to, got_at = peer, src[w].at[2 * i + 1 - place[2]], i, i
    else:
        to, src_ref, sent_to, got_at = peer, src[w].at[_slot(*peer) // 2], me // 2, _slot(*peer) // 2
    sem = w * len(bits) + i
    return pltpu.make_async_remote_copy(
        src_ref=src_ref, dst_ref=land[w].at[got_at if receiving else sent_to], send_sem=send_sems.at[sem],
        recv_sem=recv_sems.at[sem], device_id=to, device_id_type=MESH)


def _exchange_start(mode, srcs, lands, after, *, name):
    ns, nl = len(srcs), len(lands)
    n_sem = nl * len(_EXCHANGE_BITS[mode])

    def body(*refs):
        src, land = refs[:ns], refs[ns:ns + nl]
        send_sems, recv_sems = refs[ns + nl + 1:ns + nl + 3]
        token = refs[-1]
        place = _place()
        for w in range(nl):
            for i in range(len(_EXCHANGE_BITS[mode])):
                _exchange_copy(mode, src, land, w, i, place, send_sems, recv_sems, receiving=False).start()
        token[...] = jnp.zeros_like(token)

    sems = pltpu.SemaphoreType.DMA((n_sem,))
    arrays = list(srcs) + list(lands)
    res = pl.pallas_call(
        body, name=name, in_specs=[_HBM] * (ns + nl) + [_ANY],
        out_specs=(_SEM, _SEM, *([_HBM] * (ns + nl)), pl.BlockSpec(memory_space=pltpu.VMEM)),
        out_shape=(sems, sems, *[pltpu.HBM(a.shape, a.dtype) for a in arrays], jax.ShapeDtypeStruct((8, LANE), F32)),
        input_output_aliases={i: 2 + i for i in range(ns + nl)},
        compiler_params=pltpu.CompilerParams(has_side_effects=_EFFECT),
    )(*[pltpu.with_memory_space_constraint(a, pltpu.HBM) for a in arrays], after)
    return res[0], res[1], list(res[2:2 + ns]), list(res[2 + ns:2 + ns + nl]), res[-1]


def _exchange_wait(mode, started, after, *, name):
    send_sems, recv_sems, srcs, lands, _ = started
    ns, nl = len(srcs), len(lands)

    def body(*refs):
        src, land = refs[:ns], refs[ns:ns + nl]
        send_sems, recv_sems = refs[ns + nl:ns + nl + 2]
        place = _place()
        for w in range(nl):
            for i in range(len(_EXCHANGE_BITS[mode])):
                cp = _exchange_copy(mode, src, land, w, i, place, send_sems, recv_sems, receiving=True)
                cp.wait_send()
                cp.wait_recv()

    arrays = list(srcs) + list(lands)
    res = pl.pallas_call(
        body, name=name, in_specs=[_HBM] * (ns + nl) + [_SEM, _SEM, _ANY], out_specs=[_HBM] * (ns + nl),
        out_shape=[pltpu.HBM(a.shape, a.dtype) for a in arrays],
        input_output_aliases={i: i for i in range(ns + nl)},
        compiler_params=pltpu.CompilerParams(has_side_effects=_EFFECT),
    )(*arrays, send_sems, recv_sems, after)
    return list(res[:ns]), list(res[ns:])


def _small_allreduce_adamw(gvec, wvec, mvec, vvec):
    rows, length = gvec.shape

    def body(g_ref, w_ref, m_ref, v_ref, gs_ref, d_ref, nm_ref, nv_ref, slots, send_sems, recv_sems):
        x, y, c = _place()
        me = _slot(x, y, c)
        slots[me] = g_ref[...]
        sends = []
        for k in range(1, N_DEV):
            peer = _peer(k, x, y, c)
            sends.append(pltpu.make_async_remote_copy(
                src_ref=g_ref, dst_ref=slots.at[me], send_sem=send_sems.at[k - 1], recv_sem=recv_sems.at[k - 1],
                device_id=peer, device_id_type=MESH))
        for cp in sends:
            cp.start()
        for k in range(1, N_DEV):
            peer = _peer(k, x, y, c)
            pltpu.make_async_remote_copy(
                src_ref=g_ref, dst_ref=slots.at[_slot(*peer)], send_sem=send_sems.at[k - 1], recv_sem=recv_sems.at[k - 1],
                device_id=peer, device_id_type=MESH).wait_recv()
        for cp in sends:
            cp.wait_send()
        g = slots[0]
        for s in range(1, N_DEV):
            g = g + slots[s]
        delta, nm, nv = _adamw_math(w_ref[...], g, m_ref[...], v_ref[...])
        gs_ref[...] = g
        d_ref[...] = delta
        nm_ref[...] = nm
        nv_ref[...] = nv

    vmem = pl.BlockSpec(memory_space=pltpu.VMEM)
    out = jax.ShapeDtypeStruct((rows, length), F32)
    return pl.pallas_call(
        body, in_specs=[vmem] * 4, out_specs=[vmem] * 4, out_shape=[out] * 4,
        scratch_shapes=[pltpu.VMEM((N_DEV, rows, length), F32), pltpu.SemaphoreType.DMA((N_DEV - 1,)),
                        pltpu.SemaphoreType.DMA((N_DEV - 1,))],
        name="small_allreduce_adamw",
    )(gvec, wvec, mvec, vvec)


_SMALL_SEGMENTS = (("a_log", GDN_HEADS), ("dt_bias", GDN_HEADS), ("gdn_norm_w", HEAD_DIM), ("pool_scale", GDN_WIDTH),
                   ("ln1_g", D_MODEL), ("ln1_b", D_MODEL), ("ln2_g", D_MODEL), ("ln2_b", D_MODEL),
                   ("ln3_g", D_MODEL), ("ln3_b", D_MODEL), ("conv_w", CONV_K * QKV_COLS))
_SMALL_ROWS = 8
_SMALL_LEN = -(-sum(sz for _, sz in _SMALL_SEGMENTS) // (_SMALL_ROWS * LANE)) * LANE


def _pack_small(vals):
    parts = [vals[n].reshape(-1).astype(F32) if n in vals else jnp.zeros((sz,), F32) for n, sz in _SMALL_SEGMENTS]
    flat = jnp.concatenate(parts)
    flat = jnp.pad(flat, (0, _SMALL_ROWS * _SMALL_LEN - flat.shape[0]))
    return flat.reshape(_SMALL_ROWS, _SMALL_LEN)


def _unpack_small(vec):
    flat = vec.reshape(-1)
    out, off = {}, 0
    for n, sz in _SMALL_SEGMENTS:
        out[n] = flat[off:off + sz]
        off += sz
    return out


_WEIGHT_ORDER = ("w_in", "conv_w", "a_log", "dt_bias", "gdn_norm_w", "pool_w", "pool_scale", "w_out", "ln1_g", "ln1_b",
                 "xq_w", "xk_w", "xv_w", "xo_w", "ln2_g", "ln2_b", "w_up", "w_down", "ln3_g", "ln3_b")
_ADAM_ROWS = {"w_in": 256, "pool_w": 128, "w_out": 128, "xq_w": 128, "xk_w": 128, "xv_w": 128, "xo_w": 128,
              "w_up": 128, "w_down": 128}


def _shard2d(name, a):
    return a.reshape(-1, a.shape[-1]) if name == "pool_w" else a


def _gathered_to_full(name, gth):
    if name == "w_up":
        return gth
    if name in ("w_in", "conv_w"):
        return jnp.transpose(gth, (1, 0, 2)).reshape(gth.shape[1], N_DEV * gth.shape[2])
    if name == "pool_w":
        g4 = gth.reshape(N_DEV, POOL_GROUPS, POOL_GROUP_DIM // N_DEV, POOL_GROUP_DIM)
        return jnp.transpose(g4, (1, 0, 2, 3)).reshape(POOL_GROUPS, POOL_GROUP_DIM, POOL_GROUP_DIM)
    return gth.reshape(N_DEV * gth.shape[1], gth.shape[2])


def _full_to_chunks(name, full):
    if name == "w_up":
        return full
    if name == "w_in":
        r, cols = full.shape
        return jnp.transpose(full.reshape(r, N_DEV, cols // N_DEV), (1, 0, 2))
    if name == "pool_w":
        g4 = full.reshape(POOL_GROUPS, N_DEV, POOL_GROUP_DIM // N_DEV, POOL_GROUP_DIM)
        return jnp.transpose(g4, (1, 0, 2, 3)).reshape(N_DEV, POOL_GROUPS * POOL_GROUP_DIM // N_DEV, POOL_GROUP_DIM)
    return full.reshape(N_DEV, full.shape[0] // N_DEV, full.shape[1])


_GATHER_GROUPS = (("mixer", ("w_in", "conv_w", "pool_w")), ("attn", ("w_out", "xq_w", "xk_w", "xv_w", "xo_w")),
                  ("mlp", ("w_up", "w_down")))


def _grad_chunks(name, g):
    if name == "w_in":
        g = _unpad_w_in(g)
    return _full_to_chunks(name, g.astype(BF16))


def kernel(x, mem, w_in, conv_w, a_log, dt_bias, gdn_norm_w, pool_w, pool_scale, w_out, ln1_g, ln1_b, xq_w, xk_w, xv_w, xo_w, ln2_g, ln2_b, w_up, w_down, ln3_g, ln3_b, loss_target, m_w_in, m_conv_w, m_a_log, m_dt_bias, m_gdn_norm_w, m_pool_w, m_pool_scale, m_w_out, m_ln1_g, m_ln1_b, m_xq_w, m_xk_w, m_xv_w, m_xo_w, m_ln2_g, m_ln2_b, m_w_up, m_w_down, m_ln3_g, m_ln3_b, v_w_in, v_conv_w, v_a_log, v_dt_bias, v_gdn_norm_w, v_pool_w, v_pool_scale, v_w_out, v_ln1_g, v_ln1_b, v_xq_w, v_xk_w, v_xv_w, v_xo_w, v_ln2_g, v_ln2_b, v_w_up, v_w_down, v_ln3_g, v_ln3_b):
    args = dict(locals())
    wt = {n: args[n][0] for n in _WEIGHT_ORDER}
    mo = {n: args["m_" + n][0] for n in _WEIGHT_ORDER}
    vo = {n: args["v_" + n][0] for n in _WEIGHT_ORDER}

    me = _slot(*_place())
    me_arr = jnp.reshape(me, (1,)).astype(jnp.int32)
    nothing = jnp.zeros((8, LANE), F32)

    def landing_zones(names):
        shards = [_shard2d(n, wt[n]).astype(F32 if n == "conv_w" else BF16) for n in names]
        zones = [lax.dynamic_update_slice(lax.empty((N_DEV, *s.shape), s.dtype), s[None], (me, 0, 0)) for s in shards]
        return shards, zones

    chip_arr = jnp.reshape(me // 2, (1,)).astype(jnp.int32)
    core_arr = jnp.reshape(lax.axis_index("c"), (1,)).astype(jnp.int32)
    first, attn_names, mlp_names = (names for _, names in _GATHER_GROUPS)
    gathers = {}

    def gather_chips(group, names, after):
        shards, zones = landing_zones(names)
        gathers[group] = _exchange_start("gather_chips", shards, zones, after, name="gather_chips_" + group)
        return gathers[group][4]

    def gather_pass(group, after):
        _, zones = _exchange_wait("gather_chips", gathers[group], after, name=f"gather_chips_{group}_wait")
        gathers[group] = _exchange_start("gather_pass", [], zones, nothing, name="gather_pass_" + group)
        return gathers[group][4]

    def gathered(group, names, after, token=None):
        _, zones = _exchange_wait("gather_pass", gathers[group], after, name=f"gather_pass_{group}_wait")
        full = {n: _gathered_to_full(n, z) for n, z in zip(names, zones)}
        full.update({n: wt[n] if token is None else wt[n] + token for n in _VECTORS})
        return _group_weights(group, full)

    token = gather_chips("mixer", first, nothing)
    token = gather_chips("attn", attn_names, gather_pass("mixer", token))

    def weights_of(group, after):
        if group == "mixer":
            return gathered(group, first, gathers["attn"][4])
        if group == "ahead":
            return gather_chips("mlp", mlp_names, gather_pass("attn", after))[0:1, 0:1]
        if group == "attn":
            return gathered(group, attn_names, after)
        return gathered(group, mlp_names, gather_pass("mlp", after))

    scatters = {}
    in_flight = []

    def chip_stage(after):
        group, names, started = in_flight.pop()
        chunks, from_sibling = _exchange_wait("scatter_sibling", started, after, name=f"scatter_sibling_{group}_wait")
        sums = [_chip_sums(c, f, core_arr, tr=_ADAM_ROWS[n], name=f"chip_sums_{n}") for n, c, f in zip(names, chunks, from_sibling)]
        scatters[group] = (names, _exchange_start("scatter_chips", sums, [lax.empty(s.shape, s.dtype) for s in sums],
                                                  nothing, name="scatter_chips_" + group))
        return scatters[group][1][4]

    def grads_ready(group, grads):
        names = tuple(grads)
        chunks = [_grad_chunks(n, grads[n]) for n in names]
        token = chip_stage(chunks[0]) if in_flight else nothing
        zones = [lax.empty((N_CHIPS, *c.shape[1:]), c.dtype) for c in chunks]
        started = _exchange_start("scatter_sibling", chunks, zones, token, name="scatter_sibling_" + group)
        in_flight.append((group, names, started))
        return started[4][0:1, 0:1]

    sq, grad_x, g = _local_step(x[0], mem[0], loss_target[0], weights_of, grads_ready)
    chip_stage(grad_x)
    small = _finish_small_grads(g)

    out = {}
    after = grad_x
    for group, (names, started) in scatters.items():
        sums, lands = _exchange_wait("scatter_chips", started, after, name=f"scatter_chips_{group}_wait")
        for n, parts, own in zip(names, lands, sums):
            res = _adamw_shard(parts, own, chip_arr, _shard2d(n, wt[n]), _shard2d(n, mo[n]), _shard2d(n, vo[n]),
                               tr=_ADAM_ROWS[n], name="adamw_" + n)
            out[n] = [r.reshape(args[n].shape) for r in res]
            after = res[1]

    packed, _ = lax.optimization_barrier((_pack_small(small), after))
    gs, ds, ms, vs = _small_allreduce_adamw(
        packed, _pack_small({n: wt[n] for n in _VECTORS}), _pack_small({n: mo[n] for n in _VECTORS}),
        _pack_small({n: vo[n] for n in _VECTORS}))
    gs, ds, ms, vs = _unpack_small(gs), _unpack_small(ds), _unpack_small(ms), _unpack_small(vs)
    cols = conv_w.shape[-1]
    conv_full = gs["conv_w"].reshape(CONV_K, QKV_COLS)
    conv_mine = lax.dynamic_slice(conv_full, (0, me * cols), (CONV_K, cols))[None]
    res = _adamw_shard(conv_mine, conv_mine, jnp.zeros((1,), jnp.int32), wt["conv_w"], mo["conv_w"], vo["conv_w"],
                       tr=CONV_K, name="adamw_conv_w")
    out["conv_w"] = [r.reshape(conv_w.shape) for r in res]
    for n in _VECTORS:
        out[n] = [t[n].reshape(args[n].shape) for t in (gs, ds, ms, vs)]

    loss = lax.psum(0.5 * sq[0, 0] / D_MODEL, ("x", "y", "c"))
    return (loss, grad_x[None], *[out[n][0] for n in _WEIGHT_ORDER], *[out[n][1] for n in _WEIGHT_ORDER],
            *[out[n][2] for n in _WEIGHT_ORDER], *[out[n][3] for n in _WEIGHT_ORDER])
```

```python
import functools
import math

import jax
import jax.numpy as jnp
from jax import lax
from jax.experimental import pallas as pl
from jax.experimental.pallas import tpu as pltpu

F32 = jnp.float32
BF16 = jnp.bfloat16
MESH = pl.DeviceIdType.MESH

N_DEV = 8
D_MODEL = 2048
GDN_WIDTH = 1024
GDN_HEADS = 8
HEAD_DIM = 128
CONV_K = 4
CHUNK = 64
POOL_GROUPS = 4
POOL_GROUP_DIM = 256
MEM_LEN = 256
XATTN_HEADS = 4
XATTN_HEAD_DIM = 512
D_FF = 8192
IN_COLS = 5136
ALPHA = 2.0 ** 0.25
LN_EPS = 1e-5
NORM_EPS = 1e-6

LANE = 128
QKV_COLS = 3 * GDN_WIDTH
Z_OFF = QKV_COLS
BA_OFF = 4 * GDN_WIDTH
POOL_OFF = BA_OFF + 2 * LANE
PROJ_COLS = POOL_OFF + GDN_WIDTH
Z_BLK = Z_OFF // LANE
BA_BLK = BA_OFF // LANE
POOL_BLK = POOL_OFF // POOL_GROUP_DIM

ADAM_LR = 0.001
ADAM_B1 = 0.9
ADAM_B2 = 0.999
ADAM_EPS = 1e-08
ADAM_WD = 0.01
ADAM_STEP = 10

VMEM_LIMIT_BYTES = 48 * 1024 * 1024


def _params(*sem):
    return pltpu.CompilerParams(dimension_semantics=sem if sem else None, vmem_limit_bytes=VMEM_LIMIT_BYTES)


def _make_dots(cast, precision, batched=False):
    lead = 1 if batched else 0
    batch = ((0,), (0,)) if batched else ((), ())

    def dg(a, b, ca, cb):
        if cast is not None:
            a = a.astype(cast)
            b = b.astype(cast)
        return lax.dot_general(a, b, (((ca + lead,), (cb + lead,)), batch), precision=precision, preferred_element_type=F32)

    def nn_(a, b):
        return dg(a, b, 1, 0)

    def nt_(a, b):
        return dg(a, b, 1, 1)

    def tn_(a, b):
        return dg(a, b, 0, 0)

    @jax.custom_vjp
    def nn(a, b):
        return nn_(a, b)

    nn.defvjp(lambda a, b: (nn_(a, b), (a, b)), lambda r, g: (nt_(g, r[1]), tn_(r[0], g)))

    @jax.custom_vjp
    def nt(a, b):
        return nt_(a, b)

    nt.defvjp(lambda a, b: (nt_(a, b), (a, b)), lambda r, g: (nn_(g, r[1]), tn_(g, r[0])))

    @jax.custom_vjp
    def tn(a, b):
        return tn_(a, b)

    tn.defvjp(lambda a, b: (tn_(a, b), (a, b)), lambda r, g: (nt_(r[1], g), nn_(r[0], g)))

    return (nn_, nt_, tn_), (nn, nt, tn)


_BDOT_PLAIN, _BDOT_VJP = _make_dots(BF16, None)
_BDOT_BATCH_PLAIN, _BDOT_BATCH_VJP = _make_dots(BF16, None, batched=True)
_FDOT_BATCH_PLAIN, _FDOT_BATCH_VJP = _make_dots(None, lax.Precision.HIGH, batched=True)


def _mm(a, b, *, ta=False, tb=False, out_dtype=F32, tm=None, tn=512, tk=None, epi=None, extra=None, add_scale=1.0,
        b_chunks=False, o_chunks=False, name):
    m, k = (a.shape[1], a.shape[0]) if ta else a.shape
    if b_chunks:
        n, kb = (b.shape[1], N_DEV * b.shape[2]) if tb else (N_DEV * b.shape[2], b.shape[1])
    else:
        n, kb = b.shape if tb else (b.shape[1], b.shape[0])
    assert kb == k, (name, a.shape, b.shape)
    tm, tn, tk = min(tm or m, m), min(tn, n), min(tk or k, k)
    assert m % tm == 0 and n % tn == 0 and k % tk == 0, (name, m, n, k)
    nk = k // tk
    dims = (((0 if ta else 1,), (1 if tb else 0,)), ((), ()))
    n_extra = 0 if epi in (None, "relu2") else 1
    n_out = 2 if epi == "relu2" else 1
    if epi in ("relu2", "mul2r"):
        out_dtype = BF16

    def body(*refs):
        a_ref, b_ref = refs[:2]
        c_ref = refs[2] if n_extra else None
        o_refs = refs[2 + n_extra:2 + n_extra + n_out]
        scr = refs[2 + n_extra + n_out:]
        r = lax.dot_general(a_ref[...].astype(BF16), b_ref[...].astype(BF16), dims, preferred_element_type=F32)

        def finish(v):
            if epi == "add":
                o_refs[0][...] = (v + add_scale * c_ref[...]).astype(out_dtype)
            elif epi == "relu2":
                p = jnp.maximum(v, 0.0)
                o_refs[0][...] = (p * p).astype(BF16)
                o_refs[1][...] = p.astype(BF16)
            elif epi == "mul2r":
                o_refs[0][...] = (v * (2.0 * c_ref[...].astype(F32))).astype(BF16)
            else:
                o_refs[0][...] = v.astype(out_dtype)

        if nk == 1:
            finish(r)
        else:
            acc = scr[0]
            kk = pl.program_id(2)

            @pl.when(kk == 0)
            def _():
                acc[...] = r

            @pl.when(kk > 0)
            def _():
                acc[...] += r

            @pl.when(kk == nk - 1)
            def _():
                finish(acc[...])

    a_spec = pl.BlockSpec((tk, tm), lambda i, j, kk: (kk, i)) if ta else pl.BlockSpec((tm, tk), lambda i, j, kk: (i, kk))
    if b_chunks and tb:
        kc = k // N_DEV // tk
        b_spec = pl.BlockSpec((None, tn, tk), lambda i, j, kk: (kk // kc, j, kk % kc))
    elif b_chunks:
        nc = n // N_DEV // tn
        b_spec = pl.BlockSpec((None, tk, tn), lambda i, j, kk: (j // nc, kk, j % nc))
    elif tb:
        b_spec = pl.BlockSpec((tn, tk), lambda i, j, kk: (j, kk))
    else:
        b_spec = pl.BlockSpec((tk, tn), lambda i, j, kk: (kk, j))
    mn_spec = pl.BlockSpec((tm, tn), lambda i, j, kk: (i, j))
    if o_chunks:
        oc = n // N_DEV // tn
        o_spec = pl.BlockSpec((None, tm, tn), lambda i, j, kk: (j // oc, i, j % oc))
        o_shape = jax.ShapeDtypeStruct((N_DEV, m, n // N_DEV), out_dtype)
    else:
        o_spec, o_shape = mn_spec, jax.ShapeDtypeStruct((m, n), out_dtype)
    res = pl.pallas_call(
        body, grid=(m // tm, n // tn, nk), in_specs=[a_spec, b_spec] + [mn_spec] * n_extra,
        out_specs=[o_spec] * n_out, out_shape=[o_shape] * n_out,
        scratch_shapes=[pltpu.VMEM((tm, tn), F32)] if nk > 1 else [],
        compiler_params=_params("parallel", "parallel", "arbitrary"), name=name,
    )(a, b, *([extra] if n_extra else []))
    return res if n_out > 1 else res[0]


def _cast_bf16(v, *, name, tm=512):
    t, d = v.shape
    tm = min(tm, t)

    def body(v_ref, o_ref):
        o_ref[...] = v_ref[...].astype(BF16)

    spec = pl.BlockSpec((tm, d), lambda i: (i, 0))
    return pl.pallas_call(body, grid=(t // tm,), in_specs=[spec], out_specs=spec,
                          out_shape=jax.ShapeDtypeStruct((t, d), BF16), compiler_params=_params("parallel"), name=name)(v)


def _shift_down(v, s):
    if s == 0:
        return v
    row = lax.broadcasted_iota(jnp.int32, v.shape, 0)
    return jnp.where(row >= s, pltpu.roll(v, s, axis=0), 0.0)


def _shift_up(v, s):
    if s == 0:
        return v
    t = v.shape[0]
    row = lax.broadcasted_iota(jnp.int32, v.shape, 0)
    return jnp.where(row < t - s, pltpu.roll(v, t - s, axis=0), 0.0)


def _post_col(j):
    return (j % GDN_HEADS) * 3 + j // GDN_HEADS


def _gdn_prep_fwd(proj, conv_w):
    t = proj.shape[0]

    def body(x_ref, w_ref, o_ref):
        j = pl.program_id(0)
        x = x_ref[...]
        y = jnp.zeros_like(x)
        for tap in range(CONV_K):
            y = y + w_ref[tap:tap + 1, :] * _shift_down(x, CONV_K - 1 - tap)
        c = y * jax.nn.sigmoid(y)
        nrm = c * lax.rsqrt(jnp.sum(c * c, axis=1, keepdims=True) + NORM_EPS)
        o_ref[...] = jnp.where(j < 2 * GDN_HEADS, nrm, c)

    return pl.pallas_call(
        body, grid=(QKV_COLS // LANE,),
        in_specs=[pl.BlockSpec((t, LANE), lambda j: (0, j)), pl.BlockSpec((CONV_K, LANE), lambda j: (0, j))],
        out_specs=pl.BlockSpec((t, LANE), lambda j: (0, _post_col(j))),
        out_shape=jax.ShapeDtypeStruct((t, QKV_COLS), F32),
        compiler_params=_params("parallel"), name="gdn_prep_fwd",
    )(proj, conv_w)


def _gdn_prep_bwd(proj, conv_w, dpost, dproj):
    t = proj.shape[0]

    def body(x_ref, w_ref, d_ref, _, dx_ref, dw_ref):
        j = pl.program_id(0)
        x = x_ref[...]
        xs = [_shift_down(x, CONV_K - 1 - tap) for tap in range(CONV_K)]
        y = jnp.zeros_like(x)
        for tap in range(CONV_K):
            y = y + w_ref[tap:tap + 1, :] * xs[tap]
        sig = jax.nn.sigmoid(y)
        c = y * sig
        r = lax.rsqrt(jnp.sum(c * c, axis=1, keepdims=True) + NORM_EPS)
        nrm = c * r
        d = d_ref[...]
        dc_norm = r * (d - nrm * jnp.sum(d * nrm, axis=1, keepdims=True))
        dc = jnp.where(j < 2 * GDN_HEADS, dc_norm, d)
        dy = dc * (sig * (1.0 + y * (1.0 - sig)))
        dx = jnp.zeros_like(x)
        for tap in range(CONV_K):
            dx = dx + _shift_up(w_ref[tap:tap + 1, :] * dy, CONV_K - 1 - tap)
            dw_ref[tap:tap + 1, :] = jnp.sum(dy * xs[tap], axis=0, keepdims=True)
        dx_ref[...] = dx.astype(dx_ref.dtype)

    return pl.pallas_call(
        body, grid=(QKV_COLS // LANE,),
        in_specs=[pl.BlockSpec((t, LANE), lambda j: (0, j)), pl.BlockSpec((CONV_K, LANE), lambda j: (0, j)),
                  pl.BlockSpec((t, LANE), lambda j: (0, _post_col(j))), pl.BlockSpec(memory_space=pl.ANY)],
        out_specs=[pl.BlockSpec((t, LANE), lambda j: (0, j)), pl.BlockSpec((CONV_K, LANE), lambda j: (0, j))],
        out_shape=[jax.ShapeDtypeStruct(dproj.shape, dproj.dtype), jax.ShapeDtypeStruct((CONV_K, QKV_COLS), F32)],
        input_output_aliases={3: 0},
        compiler_params=_params("parallel"), name="gdn_prep_bwd",
    )(proj, conv_w, dpost, dproj)


def _softplus(v):
    return jnp.maximum(v, 0.0) + jnp.log(1.0 + jnp.exp(-jnp.abs(v)))


def _tri_inv(low, nn):
    r = lax.broadcasted_iota(jnp.int32, (CHUNK, CHUNK), 0)
    c = lax.broadcasted_iota(jnp.int32, (CHUNK, CHUNK), 1)
    eye = (r == c).astype(F32)
    same_blk = lax.shift_right_logical(r, 4) == lax.shift_right_logical(c, 4)
    diag = jnp.where(same_blk, low, 0.0)
    off = low - diag
    n1 = -diag
    n2 = nn(n1, n1)
    n4 = nn(n2, n2)
    n8 = nn(n4, n4)
    inv_d = nn(nn(nn(eye + n1, eye + n2), eye + n4), eye + n8)
    m1 = nn(inv_d, off)
    m2 = nn(m1, m1)
    return nn(nn(eye - m1, eye + m2), inv_d)


LOCAL_HEADS_PER_STEP = 8


def _gdn_local_fn(qkv, ba, alog_row, dtb_row, first_head, bdots, fdots):
    nn, nt, tn = bdots
    fnn = fdots[0]
    n_heads = qkv.shape[1] // (3 * HEAD_DIM)
    part = lambda i, p: qkv[:, (3 * i + p) * HEAD_DIM:(3 * i + p + 1) * HEAD_DIM]
    q = jnp.stack([part(i, 0) for i in range(n_heads)]) * (HEAD_DIM ** -0.5)
    k = jnp.stack([part(i, 1) for i in range(n_heads)])
    v = jnp.stack([part(i, 2) for i in range(n_heads)])
    lane = lax.broadcasted_iota(jnp.int32, ba.shape, 1)
    bg = jnp.where(lane < GDN_HEADS, jax.nn.sigmoid(ba), -jnp.exp(alog_row) * _softplus(ba + dtb_row))
    pick = lambda l: jnp.sum(jnp.where(lane == l, bg, 0.0), axis=1, keepdims=True)
    beta = jnp.stack([pick(first_head + i) for i in range(n_heads)])
    g = jnp.stack([pick(first_head + i + GDN_HEADS) for i in range(n_heads)])

    r = lax.broadcasted_iota(jnp.int32, (CHUNK, CHUNK), 0)
    c = lax.broadcasted_iota(jnp.int32, (CHUNK, CHUNK), 1)
    incl = r >= c
    strict = r > c
    eye = r == c

    def to_row(col):
        return jnp.sum(jnp.where(eye, col, 0.0), axis=1, keepdims=True)

    gc = jnp.sum(jnp.where(incl, to_row(g), 0.0), axis=2, keepdims=True)
    diff = gc - to_row(gc)
    decay = jnp.where(incl, jnp.exp(jnp.where(incl, diff, 0.0)), 0.0)
    k_beta = k * beta
    v_beta = v * beta
    low = jnp.where(strict, nt(k_beta, k) * decay, 0.0)
    t_inv = _tri_inv(low, fnn)
    eg = jnp.exp(gc)
    u = fnn(t_inv, v_beta)
    w = fnn(t_inv, k_beta * eg)
    attn = jnp.where(incl, nt(q, k) * decay, 0.0)
    last = lax.broadcasted_iota(jnp.int32, (CHUNK, 1), 0) == CHUNK - 1
    g_last = jnp.sum(jnp.where(last, gc, 0.0), axis=1, keepdims=True)
    kdec = k * jnp.exp(g_last - gc)
    elast = jnp.broadcast_to(jnp.exp(g_last), (n_heads, 1, LANE))
    return u, w, q * eg, kdec, attn, elast


def _gdn_state_fn(u, w, qg, kdec, attn, elast, state, bdots):
    nn, _, tn = bdots
    v_new = u - nn(w, state)
    o = nn(qg, state) + nn(attn, v_new)
    return o, state * elast + tn(kdec, v_new)


def _gdn_local_fwd(post, proj, alog_row, dtb_row):
    t = post.shape[0]
    n_chunks = t // CHUNK
    hb = LOCAL_HEADS_PER_STEP

    def body(qkv_ref, ba_ref, al_ref, dt_ref, u_ref, w_ref, qg_ref, kd_ref, at_ref, el_ref):
        u, w, qg, kdec, attn, elast = _gdn_local_fn(qkv_ref[...], ba_ref[...], al_ref[...], dt_ref[...],
                                                    pl.program_id(1) * hb, _BDOT_BATCH_PLAIN, _FDOT_BATCH_PLAIN)
        for i in range(hb):
            cols = slice(i * HEAD_DIM, (i + 1) * HEAD_DIM)
            u_ref[:, cols] = u[i]
            w_ref[:, cols] = w[i].astype(BF16)
            qg_ref[:, cols] = qg[i].astype(BF16)
            kd_ref[:, cols] = kdec[i].astype(BF16)
        at_ref[...] = attn.astype(BF16)
        el_ref[:, 0] = elast

    wide = pl.BlockSpec((CHUNK, hb * HEAD_DIM), lambda n, j: (n, j))
    row = pl.BlockSpec((1, LANE), lambda n, j: (0, 0))
    return pl.pallas_call(
        body, grid=(n_chunks, GDN_HEADS // hb),
        in_specs=[pl.BlockSpec((CHUNK, hb * 3 * HEAD_DIM), lambda n, j: (n, j)),
                  pl.BlockSpec((CHUNK, LANE), lambda n, j: (n, BA_BLK)), row, row],
        out_specs=[wide, wide, wide, wide, pl.BlockSpec((hb, CHUNK, CHUNK), lambda n, j: (j, n, 0)),
                   pl.BlockSpec((hb, 1, 1, LANE), lambda n, j: (j, n, 0, 0))],
        out_shape=[jax.ShapeDtypeStruct((t, GDN_WIDTH), F32), jax.ShapeDtypeStruct((t, GDN_WIDTH), BF16),
                   jax.ShapeDtypeStruct((t, GDN_WIDTH), BF16), jax.ShapeDtypeStruct((t, GDN_WIDTH), BF16),
                   jax.ShapeDtypeStruct((GDN_HEADS, t, CHUNK), BF16),
                   jax.ShapeDtypeStruct((GDN_HEADS, n_chunks, 1, LANE), F32)],
        compiler_params=_params("parallel", "parallel"), name="gdn_local_fwd",
    )(post, proj, alog_row, dtb_row)


def _gdn_state_specs(n_of):
    wide = pl.BlockSpec((CHUNK, GDN_WIDTH), lambda n: (n_of(n), 0))
    attn = pl.BlockSpec((GDN_HEADS, CHUNK, CHUNK), lambda n: (0, n_of(n), 0))
    elast = pl.BlockSpec((GDN_HEADS, 1, 1, LANE), lambda n: (0, n_of(n), 0, 0))
    saved = pl.BlockSpec((GDN_HEADS, 1, HEAD_DIM, HEAD_DIM), lambda n: (0, n_of(n), 0, 0))
    return wide, attn, elast, saved


def _gdn_state_fwd(u, w, qg, kdec, attn, elast):
    t = u.shape[0]
    n_chunks = t // CHUNK

    def body(u_ref, w_ref, qg_ref, kd_ref, at_ref, el_ref, o_ref, save_ref, state_ref):
        @pl.when(pl.program_id(0) == 0)
        def _():
            state_ref[...] = jnp.zeros_like(state_ref)

        for h in range(GDN_HEADS):
            cols = slice(h * HEAD_DIM, (h + 1) * HEAD_DIM)
            state = state_ref[h]
            save_ref[h, 0] = state
            o, new_state = _gdn_state_fn(u_ref[:, cols], w_ref[:, cols], qg_ref[:, cols], kd_ref[:, cols], at_ref[h],
                                         el_ref[h, 0], state, _BDOT_PLAIN)
            o_ref[:, cols] = o
            state_ref[h] = new_state

    wide, attn_spec, elast_spec, saved_spec = _gdn_state_specs(lambda n: n)
    return pl.pallas_call(
        body, grid=(n_chunks,), in_specs=[wide, wide, wide, wide, attn_spec, elast_spec],
        out_specs=[wide, saved_spec],
        out_shape=[jax.ShapeDtypeStruct((t, GDN_WIDTH), F32),
                   jax.ShapeDtypeStruct((GDN_HEADS, n_chunks, HEAD_DIM, HEAD_DIM), F32)],
        scratch_shapes=[pltpu.VMEM((GDN_HEADS, HEAD_DIM, HEAD_DIM), F32)],
        compiler_params=_params("arbitrary"), name="gdn_state_fwd",
    )(u, w, qg, kdec, attn, elast)


def _gdn_state_bwd(u, w, qg, kdec, attn, elast, saved, do):
    t = u.shape[0]
    n_chunks = t // CHUNK
    last = n_chunks - 1

    def body(u_ref, w_ref, qg_ref, kd_ref, at_ref, el_ref, save_ref, do_ref,
             du_ref, dw_ref, dqg_ref, dkd_ref, dat_ref, del_ref, dstate_ref):
        @pl.when(pl.program_id(0) == 0)
        def _():
            dstate_ref[...] = jnp.zeros_like(dstate_ref)

        for h in range(GDN_HEADS):
            cols = slice(h * HEAD_DIM, (h + 1) * HEAD_DIM)
            _, vjp = jax.vjp(
                lambda *a: _gdn_state_fn(*a, _BDOT_VJP), u_ref[:, cols], w_ref[:, cols].astype(F32),
                qg_ref[:, cols].astype(F32), kd_ref[:, cols].astype(F32), at_ref[h].astype(F32), el_ref[h, 0], save_ref[h, 0])
            du, dw, dqg, dkd, dat, de, dstate = vjp((do_ref[:, cols], dstate_ref[h]))
            du_ref[:, cols] = du
            dw_ref[:, cols] = dw
            dqg_ref[:, cols] = dqg
            dkd_ref[:, cols] = dkd
            dat_ref[h] = dat
            del_ref[h, 0] = de
            dstate_ref[h] = dstate

    wide, attn_spec, elast_spec, saved_spec = _gdn_state_specs(lambda n: last - n)
    wide_f32 = jax.ShapeDtypeStruct((t, GDN_WIDTH), F32)
    return pl.pallas_call(
        body, grid=(n_chunks,), in_specs=[wide, wide, wide, wide, attn_spec, elast_spec, saved_spec, wide],
        out_specs=[wide, wide, wide, wide, attn_spec, elast_spec],
        out_shape=[wide_f32, wide_f32, wide_f32, wide_f32, jax.ShapeDtypeStruct((GDN_HEADS, t, CHUNK), F32),
                   jax.ShapeDtypeStruct((GDN_HEADS, n_chunks, 1, LANE), F32)],
        scratch_shapes=[pltpu.VMEM((GDN_HEADS, HEAD_DIM, HEAD_DIM), F32)],
        compiler_params=_params("arbitrary"), name="gdn_state_bwd",
    )(u, w, qg, kdec, attn, elast, saved, do)


def _gdn_local_bwd(post, proj, alog_row, dtb_row, cots, dproj):
    t = post.shape[0]
    n_chunks = t // CHUNK
    hb = LOCAL_HEADS_PER_STEP
    n_steps = GDN_HEADS // hb

    def body(qkv_ref, ba_ref, al_ref, dt_ref, du_ref, dw_ref, dqg_ref, dkd_ref, dat_ref, del_ref, _,
             dqkv_ref, dba_ref, dal_ref, ddt_ref, dba_acc):
        n = pl.program_id(0)
        j = pl.program_id(1)

        @pl.when((n == 0) & (j == 0))
        def _():
            dal_ref[...] = jnp.zeros_like(dal_ref)
            ddt_ref[...] = jnp.zeros_like(ddt_ref)

        @pl.when(j == 0)
        def _():
            dba_acc[...] = jnp.zeros_like(dba_acc)

        heads = lambda ref: jnp.stack([ref[:, i * HEAD_DIM:(i + 1) * HEAD_DIM] for i in range(hb)])
        _, vjp = jax.vjp(lambda a, b, c, d: _gdn_local_fn(a, b, c, d, j * hb, _BDOT_BATCH_VJP, _FDOT_BATCH_VJP),
                         qkv_ref[...], ba_ref[...], al_ref[...], dt_ref[...])
        dqkv, dba, dal, ddt = vjp((heads(du_ref), heads(dw_ref), heads(dqg_ref), heads(dkd_ref), dat_ref[...],
                                   del_ref[:, 0]))
        dqkv_ref[...] = dqkv
        dba_acc[...] += dba
        dal_ref[...] += dal
        ddt_ref[...] += ddt

        @pl.when(j == n_steps - 1)
        def _():
            dba_ref[:, 0:LANE] = dba_acc[...].astype(dba_ref.dtype)
            dba_ref[:, LANE:2 * LANE] = jnp.zeros((CHUNK, LANE), dba_ref.dtype)

    wide = pl.BlockSpec((CHUNK, hb * HEAD_DIM), lambda n, j: (n, j))
    qkv_spec = pl.BlockSpec((CHUNK, hb * 3 * HEAD_DIM), lambda n, j: (n, j))
    row = pl.BlockSpec((1, LANE), lambda n, j: (0, 0))
    return pl.pallas_call(
        body, grid=(n_chunks, n_steps),
        in_specs=[qkv_spec, pl.BlockSpec((CHUNK, LANE), lambda n, j: (n, BA_BLK)), row, row, wide, wide, wide, wide,
                  pl.BlockSpec((hb, CHUNK, CHUNK), lambda n, j: (j, n, 0)),
                  pl.BlockSpec((hb, 1, 1, LANE), lambda n, j: (j, n, 0, 0)), pl.BlockSpec(memory_space=pl.ANY)],
        out_specs=[qkv_spec, pl.BlockSpec((CHUNK, 2 * LANE), lambda n, j: (n, BA_BLK // 2)), row, row],
        out_shape=[jax.ShapeDtypeStruct((t, QKV_COLS), F32), jax.ShapeDtypeStruct(dproj.shape, dproj.dtype),
                   jax.ShapeDtypeStruct((1, LANE), F32), jax.ShapeDtypeStruct((1, LANE), F32)],
        input_output_aliases={10: 1},
        scratch_shapes=[pltpu.VMEM((CHUNK, LANE), F32)],
        compiler_params=_params("arbitrary", "arbitrary"), name="gdn_local_bwd",
    )(post, proj, alog_row, dtb_row, *cots, dproj)


def _onorm_fn(o, z, w):
    return o * lax.rsqrt(jnp.mean(o * o, axis=1, keepdims=True) + NORM_EPS) * w * (z * jax.nn.sigmoid(z))


def _onorm_fwd(o_raw, proj, norm_w, mixin, tm=512):
    t = o_raw.shape[0]
    tm = min(tm, t)

    def body(o_ref, z_ref, w_ref, _, out_ref):
        out_ref[...] = _onorm_fn(o_ref[...], z_ref[...], w_ref[...]).astype(out_ref.dtype)

    return pl.pallas_call(
        body, grid=(t // tm, GDN_HEADS),
        in_specs=[pl.BlockSpec((tm, LANE), lambda i, h: (i, h)), pl.BlockSpec((tm, LANE), lambda i, h: (i, Z_BLK + h)),
                  pl.BlockSpec((1, LANE), lambda i, h: (0, 0)), pl.BlockSpec(memory_space=pl.ANY)],
        out_specs=pl.BlockSpec((tm, LANE), lambda i, h: (i, h)),
        out_shape=jax.ShapeDtypeStruct(mixin.shape, mixin.dtype), input_output_aliases={3: 0},
        compiler_params=_params("parallel", "parallel"), name="gdn_onorm_fwd",
    )(o_raw, proj, norm_w, mixin)


def _onorm_bwd(o_raw, proj, norm_w, dmixin, dproj, tm=512):
    t = o_raw.shape[0]
    tm = min(tm, t)

    def body(o_ref, z_ref, w_ref, d_ref, _, do_ref, dz_ref, dw_ref):
        @pl.when((pl.program_id(0) == 0) & (pl.program_id(1) == 0))
        def _():
            dw_ref[...] = jnp.zeros_like(dw_ref)

        _, vjp = jax.vjp(_onorm_fn, o_ref[...], z_ref[...], w_ref[...])
        do, dz, dw = vjp(d_ref[...])
        do_ref[...] = do
        dz_ref[...] = dz.astype(dz_ref.dtype)
        dw_ref[...] += dw

    return pl.pallas_call(
        body, grid=(t // tm, GDN_HEADS),
        in_specs=[pl.BlockSpec((tm, LANE), lambda i, h: (i, h)), pl.BlockSpec((tm, LANE), lambda i, h: (i, Z_BLK + h)),
                  pl.BlockSpec((1, LANE), lambda i, h: (0, 0)), pl.BlockSpec((tm, LANE), lambda i, h: (i, h)),
                  pl.BlockSpec(memory_space=pl.ANY)],
        out_specs=[pl.BlockSpec((tm, LANE), lambda i, h: (i, h)), pl.BlockSpec((tm, LANE), lambda i, h: (i, Z_BLK + h)),
                   pl.BlockSpec((1, LANE), lambda i, h: (0, 0))],
        out_shape=[jax.ShapeDtypeStruct((t, GDN_WIDTH), F32), jax.ShapeDtypeStruct(dproj.shape, dproj.dtype),
                   jax.ShapeDtypeStruct((1, LANE), F32)],
        input_output_aliases={4: 1},
        compiler_params=_params("arbitrary", "arbitrary"), name="gdn_onorm_bwd",
    )(o_raw, proj, norm_w, dmixin, dproj)


def _pool_select(levels, gi):
    out = levels[-1]
    for lvl in range(len(levels) - 2, -1, -1):
        out = jnp.where(gi == lvl, levels[lvl], out)
    return out


def _pool_count(shape, gi):
    pos = lax.broadcasted_iota(jnp.int32, shape, 0)
    win = lax.shift_left(jnp.int32(2), gi)
    return jnp.minimum(pos + 1, win).astype(F32)


def _pooled(p, gi):
    acc = p
    levels = []
    for lvl in range(POOL_GROUPS):
        acc = acc + _shift_down(acc, 1 << lvl)
        levels.append(acc)
    return _pool_select(levels, gi) / _pool_count(p.shape, gi) - p


def _pool_fwd(proj, pool_w, pool_scale):
    t = proj.shape[0]

    def body(p_ref, w_ref, s_ref, out_ref):
        gi = pl.program_id(0)
        pooled = _pooled(p_ref[...], gi)
        out_ref[...] = (_BDOT_PLAIN[0](pooled, w_ref[0]) * s_ref[0]).astype(out_ref.dtype)

    return pl.pallas_call(
        body, grid=(POOL_GROUPS,),
        in_specs=[pl.BlockSpec((t, POOL_GROUP_DIM), lambda g: (0, POOL_BLK + g)),
                  pl.BlockSpec((1, POOL_GROUP_DIM, POOL_GROUP_DIM), lambda g: (g, 0, 0)),
                  pl.BlockSpec((1, 1, POOL_GROUP_DIM), lambda g: (g, 0, 0))],
        out_specs=pl.BlockSpec((t, POOL_GROUP_DIM), lambda g: (0, GDN_WIDTH // POOL_GROUP_DIM + g)),
        out_shape=jax.ShapeDtypeStruct((t, 2 * GDN_WIDTH), BF16),
        compiler_params=_params("parallel"), name="pool_fwd",
    )(proj, pool_w, pool_scale)


def _pool_bwd(proj, pool_w, pool_scale, dmixin):
    t = proj.shape[0]
    nn, nt, tn = _BDOT_PLAIN

    def body(p_ref, w_ref, s_ref, d_ref, dp_ref, dw_ref, ds_ref):
        gi = pl.program_id(0)
        p = p_ref[...]
        pooled = _pooled(p, gi)
        mixed = nn(pooled, w_ref[0])
        d = d_ref[...]
        ds_ref[0] = jnp.sum(d * mixed, axis=0, keepdims=True)
        dmixed = d * s_ref[0]
        dw_ref[0] = tn(pooled, dmixed)
        dpooled = nt(dmixed, w_ref[0])
        acc = dpooled / _pool_count(p.shape, gi)
        levels = []
        for lvl in range(POOL_GROUPS):
            acc = acc + _shift_up(acc, 1 << lvl)
            levels.append(acc)
        dp_ref[...] = (_pool_select(levels, gi) - dpooled).astype(dp_ref.dtype)

    return pl.pallas_call(
        body, grid=(POOL_GROUPS,),
        in_specs=[pl.BlockSpec((t, POOL_GROUP_DIM), lambda g: (0, POOL_BLK + g)),
                  pl.BlockSpec((1, POOL_GROUP_DIM, POOL_GROUP_DIM), lambda g: (g, 0, 0)),
                  pl.BlockSpec((1, 1, POOL_GROUP_DIM), lambda g: (g, 0, 0)),
                  pl.BlockSpec((t, POOL_GROUP_DIM), lambda g: (0, GDN_WIDTH // POOL_GROUP_DIM + g))],
        out_specs=[pl.BlockSpec((t, POOL_GROUP_DIM), lambda g: (0, POOL_BLK + g)),
                   pl.BlockSpec((1, POOL_GROUP_DIM, POOL_GROUP_DIM), lambda g: (g, 0, 0)),
                   pl.BlockSpec((1, 1, POOL_GROUP_DIM), lambda g: (g, 0, 0))],
        out_shape=[jax.ShapeDtypeStruct((t, PROJ_COLS), BF16),
                   jax.ShapeDtypeStruct((POOL_GROUPS, POOL_GROUP_DIM, POOL_GROUP_DIM), F32),
                   jax.ShapeDtypeStruct((POOL_GROUPS, 1, POOL_GROUP_DIM), F32)],
        compiler_params=_params("parallel"), name="pool_bwd",
    )(proj, pool_w, pool_scale, dmixin)


def _ln_stats(s):
    mu = jnp.mean(s, axis=1, keepdims=True)
    xc = s - mu
    var = jnp.mean(xc * xc, axis=1, keepdims=True)
    rstd = lax.rsqrt(var + LN_EPS)
    return xc * rstd, rstd


def _ln_fwd(h_in, y, g, b, *, name, tm=256):
    t, d = h_in.shape
    tm = min(tm, t)

    def body(h_ref, y_ref, g_ref, b_ref, o_ref, o16_ref):
        xhat, _ = _ln_stats(ALPHA * h_ref[...] + y_ref[...])
        out = xhat * g_ref[...] + b_ref[...]
        o_ref[...] = out
        o16_ref[...] = out.astype(BF16)

    row = pl.BlockSpec((tm, d), lambda i: (i, 0))
    vec = pl.BlockSpec((1, d), lambda i: (0, 0))
    return pl.pallas_call(
        body, grid=(t // tm,), in_specs=[row, row, vec, vec], out_specs=[row, row],
        out_shape=[jax.ShapeDtypeStruct((t, d), F32), jax.ShapeDtypeStruct((t, d), BF16)],
        compiler_params=_params("parallel"), name=name,
    )(h_in, y, g, b)


def _ln_loss_fwd(h_in, y, g, b, target, *, name, tm=256):
    t, d = h_in.shape
    tm = min(tm, t)

    def body(h_ref, y_ref, g_ref, b_ref, t_ref, dy_ref, sq_ref):
        @pl.when(pl.program_id(0) == 0)
        def _():
            sq_ref[...] = jnp.zeros_like(sq_ref)

        xhat, _ = _ln_stats(ALPHA * h_ref[...] + y_ref[...])
        err = xhat * g_ref[...] + b_ref[...] - t_ref[...]
        dy_ref[...] = err * (1.0 / d)
        sq_ref[...] += jnp.sum(jnp.sum(err * err, axis=1, keepdims=True), axis=0, keepdims=True)

    row = pl.BlockSpec((tm, d), lambda i: (i, 0))
    vec = pl.BlockSpec((1, d), lambda i: (0, 0))
    return pl.pallas_call(
        body, grid=(t // tm,), in_specs=[row, row, vec, vec, row],
        out_specs=[row, pl.BlockSpec((1, LANE), lambda i: (0, 0))],
        out_shape=[jax.ShapeDtypeStruct((t, d), F32), jax.ShapeDtypeStruct((1, LANE), F32)],
        compiler_params=_params("arbitrary"), name=name,
    )(h_in, y, g, b, target)


def _ln_bwd(h_in, y, g, d_a, d_b, *, name, tm=256):
    t, d = h_in.shape
    tm = min(tm, t)
    has_b = d_b is not None

    def body(*refs):
        if has_b:
            h_ref, y_ref, g_ref, da_ref, db_ref, ds_ref, ds16_ref, dg_ref, dbias_ref = refs
        else:
            h_ref, y_ref, g_ref, da_ref, ds_ref, ds16_ref, dg_ref, dbias_ref = refs

        @pl.when(pl.program_id(0) == 0)
        def _():
            dg_ref[...] = jnp.zeros_like(dg_ref)
            dbias_ref[...] = jnp.zeros_like(dbias_ref)

        xhat, rstd = _ln_stats(ALPHA * h_ref[...] + y_ref[...])
        dout = da_ref[...]
        if has_b:
            dout = dout + ALPHA * db_ref[...]
        dxhat = dout * g_ref[...]
        m1 = jnp.mean(dxhat, axis=1, keepdims=True)
        m2 = jnp.mean(dxhat * xhat, axis=1, keepdims=True)
        ds = rstd * (dxhat - m1 - xhat * m2)
        ds_ref[...] = ds
        ds16_ref[...] = ds.astype(BF16)
        dg_ref[...] += jnp.sum(dout * xhat, axis=0, keepdims=True)
        dbias_ref[...] += jnp.sum(dout, axis=0, keepdims=True)

    row = pl.BlockSpec((tm, d), lambda i: (i, 0))
    vec = pl.BlockSpec((1, d), lambda i: (0, 0))
    args = [h_in, y, g, d_a] + ([d_b] if has_b else [])
    return pl.pallas_call(
        body, grid=(t // tm,), in_specs=[row, row, vec, row] + ([row] if has_b else []),
        out_specs=[row, row, vec, vec],
        out_shape=[jax.ShapeDtypeStruct((t, d), F32), jax.ShapeDtypeStruct((t, d), BF16),
                   jax.ShapeDtypeStruct((1, d), F32), jax.ShapeDtypeStruct((1, d), F32)],
        compiler_params=_params("arbitrary"), name=name,
    )(*args)


def _attn_fn(q, k, v, dots):
    nn, nt, _ = dots
    s = nt(q, k) * (XATTN_HEAD_DIM ** -0.5)
    s = s - lax.stop_gradient(jnp.max(s, axis=1, keepdims=True))
    e = jnp.exp(s)
    p = e / jnp.sum(e, axis=1, keepdims=True)
    return nn(p, v)


def _attn_fwd(q, k, v, tq=512):
    t = q.shape[0]
    tq = min(tq, t)

    def body(q_ref, k_ref, v_ref, o_ref):
        o_ref[...] = _attn_fn(q_ref[...], k_ref[...], v_ref[...], _BDOT_PLAIN).astype(BF16)

    qs = pl.BlockSpec((tq, XATTN_HEAD_DIM), lambda h, i: (i, h))
    ks = pl.BlockSpec((MEM_LEN, XATTN_HEAD_DIM), lambda h, i: (0, h))
    return pl.pallas_call(
        body, grid=(XATTN_HEADS, t // tq), in_specs=[qs, ks, ks], out_specs=qs,
        out_shape=jax.ShapeDtypeStruct(q.shape, BF16), compiler_params=_params("parallel", "parallel"), name="xattn_fwd",
    )(q, k, v)


def _attn_bwd(q, k, v, do, tq=512):
    t = q.shape[0]
    tq = min(tq, t)

    def body(q_ref, k_ref, v_ref, do_ref, dq_ref, dk_ref, dv_ref):
        @pl.when(pl.program_id(1) == 0)
        def _():
            dk_ref[...] = jnp.zeros_like(dk_ref)
            dv_ref[...] = jnp.zeros_like(dv_ref)

        _, vjp = jax.vjp(lambda a, b, c: _attn_fn(a, b, c, _BDOT_VJP), q_ref[...].astype(F32), k_ref[...].astype(F32),
                         v_ref[...].astype(F32))
        dq, dk, dv = vjp(do_ref[...].astype(F32))
        dq_ref[...] = dq.astype(BF16)
        dk_ref[...] += dk
        dv_ref[...] += dv

    qs = pl.BlockSpec((tq, XATTN_HEAD_DIM), lambda h, i: (i, h))
    ks = pl.BlockSpec((MEM_LEN, XATTN_HEAD_DIM), lambda h, i: (0, h))
    return pl.pallas_call(
        body, grid=(XATTN_HEADS, t // tq), in_specs=[qs, ks, ks, qs], out_specs=[qs, ks, ks],
        out_shape=[jax.ShapeDtypeStruct(q.shape, BF16), jax.ShapeDtypeStruct(k.shape, F32), jax.ShapeDtypeStruct(v.shape, F32)],
        compiler_params=_params("parallel", "arbitrary"), name="xattn_bwd",
    )(q, k, v, do)


def _local_step(x, x16, mem, target, weights_of, grads_ready):
    def behind(vec, token):
        return vec if token is None else vec + token

    w = dict(weights_of("mixer", None))
    proj = _mm(x16, w["w_in"], tb=True, tn=768, name="mm_in_proj")
    post = _gdn_prep_fwd(proj, w["conv_w"])
    mixin = _pool_fwd(proj, w["pool_w"], w["pool_scale"])
    token = weights_of("ahead", mixin)
    chunked = _gdn_local_fwd(post, proj, behind(w["alog_row"], token), w["dtb_row"])
    o_raw, saved = _gdn_state_fwd(*chunked)
    mixin = _onorm_fwd(o_raw, proj, w["gdn_norm_w"], mixin)
    w.update(weights_of("attn", mixin))
    mix = _mm(mixin, w["w_out"], name="mm_out_proj")
    h1, h1_16 = _ln_fwd(x, mix, w["ln1_g"], w["ln1_b"], name="ln1_fwd")
    xq = _mm(h1_16, w["xq_w"], out_dtype=BF16, name="mm_xq")
    xk = _mm(mem, w["xk_w"], out_dtype=BF16, name="mm_xk")
    xv = _mm(mem, w["xv_w"], out_dtype=BF16, name="mm_xv")
    xo = _attn_fwd(xq, xk, xv)
    xa = _mm(xo, w["xo_w"], name="mm_xo")
    h2, h2_16 = _ln_fwd(h1, xa, w["ln2_g"], w["ln2_b"], name="ln2_fwd")
    w.update(weights_of("mlp", h2_16))
    act, relu = _mm(h2_16, w["w_up"], b_chunks=True, epi="relu2", name="mm_up")
    ff = _mm(act, w["w_down"], tn=1024, tk=512, name="mm_down")
    dy, sq = _ln_loss_fwd(h2, ff, w["ln3_g"], w["ln3_b"], target, name="ln3_loss_fwd")

    g = {}
    ds3, ds3_16, g["ln3_g"], g["ln3_b"] = _ln_bwd(h2, ff, w["ln3_g"], dy, None, name="ln3_bwd")
    gw_down = _mm(act, ds3_16, ta=True, out_dtype=BF16, tm=512, tn=D_MODEL, name="mm_gw_down")
    du = _mm(ds3_16, w["w_down"], tb=True, epi="mul2r", extra=relu, name="mm_du")
    gw_up = _mm(h2_16, du, ta=True, out_dtype=BF16, o_chunks=True, name="mm_gw_up")
    token = grads_ready("mlp", {"w_down": gw_down, "w_up": gw_up})
    dh2 = _mm(du, w["w_up"], tb=True, b_chunks=True, tn=1024, tk=512, name="mm_dh2")
    ds2, ds2_16, g["ln2_g"], g["ln2_b"] = _ln_bwd(h1, xa, behind(w["ln2_g"], token), dh2, ds3, name="ln2_bwd")
    gw_xo = _mm(xo, ds2_16, ta=True, out_dtype=BF16, name="mm_gw_xo")
    dxo = _mm(ds2_16, w["xo_w"], tb=True, out_dtype=BF16, name="mm_dxo")
    dxq, dxk, dxv = _attn_bwd(xq, xk, xv, dxo)
    gw_xq = _mm(h1_16, dxq, ta=True, out_dtype=BF16, name="mm_gw_xq")
    gw_xk = _mm(mem, dxk, ta=True, out_dtype=BF16, name="mm_gw_xk")
    gw_xv = _mm(mem, dxv, ta=True, out_dtype=BF16, name="mm_gw_xv")
    token = grads_ready("attn", {"xo_w": gw_xo, "xq_w": gw_xq, "xk_w": gw_xk, "xv_w": gw_xv})
    dh1 = _mm(dxq, w["xq_w"], tb=True, name="mm_dh1")
    ds1, ds1_16, g["ln1_g"], g["ln1_b"] = _ln_bwd(x, mix, behind(w["ln1_g"], token), dh1, ds2, name="ln1_bwd")
    gw_out = _mm(mixin, ds1_16, ta=True, out_dtype=BF16, name="mm_gw_out")
    dmixin = _mm(ds1_16, w["w_out"], tb=True, name="mm_dmixin")
    dproj, gw_pool, g["pool_scale"] = _pool_bwd(proj, w["pool_w"], w["pool_scale"], dmixin)
    token = grads_ready("mix", {"w_out": gw_out, "pool_w": gw_pool})
    do_raw, dproj, g["gdn_norm_w"] = _onorm_bwd(o_raw, proj, behind(w["gdn_norm_w"], token), dmixin, dproj)
    cots = _gdn_state_bwd(*chunked, saved, do_raw)
    dpost, dproj, g["alog_row"], g["dtb_row"] = _gdn_local_bwd(post, proj, w["alog_row"], w["dtb_row"], cots, dproj)
    dproj, g["conv_w"] = _gdn_prep_bwd(proj, w["conv_w"], dpost, dproj)
    gw_in = _mm(dproj, x16, ta=True, out_dtype=BF16, tm=768, tn=D_MODEL, name="mm_gw_in")
    grads_ready("in", {"w_in": gw_in})
    grad_x = _mm(dproj, w["w_in"], tk=768, epi="add", extra=ds1, add_scale=ALPHA, name="mm_dx")
    return sq, grad_x, g


_MATRICES = ("w_in", "pool_w", "w_out", "xq_w", "xk_w", "xv_w", "xo_w", "w_up", "w_down")
_VECTORS = ("a_log", "dt_bias", "gdn_norm_w", "pool_scale", "ln1_g", "ln1_b", "ln2_g", "ln2_b", "ln3_g", "ln3_b")
_BA_SPLIT = BA_OFF + 2 * GDN_HEADS


def _lane_row(v, offset):
    return jnp.zeros((1, LANE), F32).at[0, offset:offset + v.shape[0]].set(v)


_GROUP_VECTORS = {"mixer": (), "attn": ("ln1_g", "ln1_b", "ln2_g", "ln2_b"), "mlp": ("ln3_g", "ln3_b")}


def _group_weights(group, full):
    w = {n: full[n].reshape(1, D_MODEL) for n in _GROUP_VECTORS[group]}
    if group == "mixer":
        w_in = full["w_in"]
        zeros = jnp.zeros((POOL_OFF - _BA_SPLIT, w_in.shape[1]), w_in.dtype)
        w.update({
            "w_in": jnp.concatenate([w_in[:_BA_SPLIT], zeros, w_in[_BA_SPLIT:]], axis=0),
            "conv_w": full["conv_w"],
            "alog_row": _lane_row(full["a_log"], GDN_HEADS),
            "dtb_row": _lane_row(full["dt_bias"], GDN_HEADS),
            "gdn_norm_w": full["gdn_norm_w"].reshape(1, LANE),
            "pool_w": full["pool_w"],
            "pool_scale": full["pool_scale"].reshape(POOL_GROUPS, 1, POOL_GROUP_DIM),
        })
    elif group == "attn":
        w.update({n: full[n] for n in ("w_out", "xq_w", "xk_w", "xv_w", "xo_w")})
    else:
        w.update({n: full[n] for n in ("w_up", "w_down")})
    return w


def _unpad_w_in(g):
    return jnp.concatenate([g[:_BA_SPLIT], g[POOL_OFF:]], axis=0)


def _finish_small_grads(g):
    out = {"conv_w": g["conv_w"]}
    out["a_log"] = g["alog_row"][0, GDN_HEADS:2 * GDN_HEADS]
    out["dt_bias"] = g["dtb_row"][0, GDN_HEADS:2 * GDN_HEADS]
    out["gdn_norm_w"] = g["gdn_norm_w"].reshape(LANE)
    out["pool_scale"] = g["pool_scale"].reshape(POOL_GROUPS * POOL_GROUP_DIM)
    for n in ("ln1_g", "ln1_b", "ln2_g", "ln2_b", "ln3_g", "ln3_b"):
        out[n] = g[n].reshape(D_MODEL)
    return out


def _adamw_math(w, g, m, v):
    m = ADAM_B1 * m + (1.0 - ADAM_B1) * g
    v = ADAM_B2 * v + (1.0 - ADAM_B2) * (g * g)
    m_hat = m / (1.0 - ADAM_B1 ** ADAM_STEP)
    v_hat = v / (1.0 - ADAM_B2 ** ADAM_STEP)
    delta = -ADAM_LR * (m_hat / (jnp.sqrt(v_hat) + ADAM_EPS) + ADAM_WD * w)
    return delta, m, v


def _shard_tile(r, c):
    if r % 128 == 0:
        return 128, c
    return r, 256 if c % 256 == 0 else c


def _adamw_shard(parts, own, me, w, m, v, *, name):
    s, r, c = parts.shape
    tr, tc = _shard_tile(r, c)
    assert r % tr == 0 and c % tc == 0, (name, r, c)

    def body(me_ref, p_ref, own_ref, w_ref, m_ref, v_ref, g_ref, d_ref, nm_ref, nv_ref):
        mine = own_ref[...].astype(F32)
        g = None
        for i in range(s):
            part = jnp.where(me_ref[0] == i, mine, p_ref[i].astype(F32))
            g = part if g is None else g + part
        delta, nm, nv = _adamw_math(w_ref[...], g, m_ref[...], v_ref[...])
        g_ref[...] = g
        d_ref[...] = delta
        nm_ref[...] = nm
        nv_ref[...] = nv

    blk = pl.BlockSpec((tr, tc), lambda i, j, me_ref: (i, j))
    out = jax.ShapeDtypeStruct((r, c), F32)
    return pl.pallas_call(
        body,
        grid_spec=pltpu.PrefetchScalarGridSpec(
            num_scalar_prefetch=1, grid=(r // tr, c // tc),
            in_specs=[pl.BlockSpec((s, tr, tc), lambda i, j, me_ref: (0, i, j)),
                      pl.BlockSpec((None, tr, tc), lambda i, j, me_ref: (me_ref[0], i, j)), blk, blk, blk],
            out_specs=[blk, blk, blk, blk]),
        out_shape=[out, out, out, out], compiler_params=_params("parallel", "parallel"), name=name,
    )(me, parts, own, w, m, v)


N_CHIPS = N_DEV // 2


def _chip_sums(chunks, from_sibling, core, *, name):
    _, r, c = chunks.shape
    tr, tc = _shard_tile(r, c)
    assert r % tr == 0 and c % tc == 0, (name, r, c)

    def body(core_ref, mine_ref, other_ref, o_ref):
        o_ref[...] = (mine_ref[...].astype(F32) + other_ref[...].astype(F32)).astype(o_ref.dtype)

    by_chip = pl.BlockSpec((None, tr, tc), lambda q, i, j, core_ref: (q, i, j))
    return pl.pallas_call(
        body,
        grid_spec=pltpu.PrefetchScalarGridSpec(
            num_scalar_prefetch=1, grid=(N_CHIPS, r // tr, c // tc),
            in_specs=[pl.BlockSpec((None, tr, tc), lambda q, i, j, core_ref: (2 * q + core_ref[0], i, j)), by_chip],
            out_specs=by_chip),
        out_shape=jax.ShapeDtypeStruct((N_CHIPS, r, c), chunks.dtype),
        compiler_params=_params("parallel", "parallel", "parallel"), name=name,
    )(core, chunks, from_sibling)


def _place():
    return lax.axis_index("x"), lax.axis_index("y"), lax.axis_index("c")


def _slot(px, py, pc):
    return 4 * px + 2 * py + pc


_HBM = pl.BlockSpec(memory_space=pltpu.HBM)


_SEM = pl.BlockSpec(memory_space=pltpu.SEMAPHORE)
_ANY = pl.BlockSpec(memory_space=pl.ANY)
_EFFECT = pltpu.SideEffectType.DATAFLOW_SIDE_EFFECTING
_N_PEERS = N_DEV - 1


def _peer(k, x, y, c):
    return (1 - x if k & 4 else x, 1 - y if k & 2 else y, 1 - c if k & 1 else c)


_EXCHANGE_BITS = {"gather_chips": (1, 2, 4, 6), "gather_pass": (2, 4, 6), "scatter_sibling": (1, 1, 1, 1),
                  "scatter_chips": (2, 4, 6)}


def _exchange_copy(mode, src, land, w, i, place, send_sems, recv_sems, receiving):
    bits = _EXCHANGE_BITS[mode]
    k = bits[i]
    peer = _peer(k, *place)
    me = _slot(*place)
    if mode == "gather_chips":
        to, src_ref, sent_to, got_at = peer, src[w], me, _slot(*peer)
    elif mode == "gather_pass":
        blk = _slot(*peer)
        to, src_ref, sent_to, got_at = _peer(1, *place), land[w].at[blk], blk, _slot(*_peer(k | 1, *place))
    elif mode == "scatter_sibling":
        to, src_ref, sent_to, got_at = peer, src[w].at[2 * i + 1 - place[2]], i, i
    else:
        to, src_ref, sent_to, got_at = peer, src[w].at[_slot(*peer) // 2], me // 2, _slot(*peer) // 2
    sem = w * len(bits) + i
    return pltpu.make_async_remote_copy(
        src_ref=src_ref, dst_ref=land[w].at[got_at if receiving else sent_to], send_sem=send_sems.at[sem],
        recv_sem=recv_sems.at[sem], device_id=to, device_id_type=MESH)


def _exchange_start(mode, srcs, lands, after, *, name):
    ns, nl = len(srcs), len(lands)
    n_sem = nl * len(_EXCHANGE_BITS[mode])

    def body(*refs):
        src, land = refs[:ns], refs[ns:ns + nl]
        send_sems, recv_sems = refs[ns + nl + 1:ns + nl + 3]
        token = refs[-1]
        place = _place()
        for w in range(nl):
            for i in range(len(_EXCHANGE_BITS[mode])):
                _exchange_copy(mode, src, land, w, i, place, send_sems, recv_sems, receiving=False).start()
        token[...] = jnp.zeros_like(token)

    sems = pltpu.SemaphoreType.DMA((n_sem,))
    arrays = list(srcs) + list(lands)
    res = pl.pallas_call(
        body, name=name, in_specs=[_HBM] * (ns + nl) + [_ANY],
        out_specs=(_SEM, _SEM, *([_HBM] * (ns + nl)), pl.BlockSpec(memory_space=pltpu.VMEM)),
        out_shape=(sems, sems, *[pltpu.HBM(a.shape, a.dtype) for a in arrays], jax.ShapeDtypeStruct((8, LANE), F32)),
        input_output_aliases={i: 2 + i for i in range(ns + nl)},
        compiler_params=pltpu.CompilerParams(has_side_effects=_EFFECT),
    )(*[pltpu.with_memory_space_constraint(a, pltpu.HBM) for a in arrays], after)
    return res[0], res[1], list(res[2:2 + ns]), list(res[2 + ns:2 + ns + nl]), res[-1]


def _exchange_wait(mode, started, after, *, name):
    send_sems, recv_sems, srcs, lands, _ = started
    ns, nl = len(srcs), len(lands)

    def body(*refs):
        src, land = refs[:ns], refs[ns:ns + nl]
        send_sems, recv_sems = refs[ns + nl:ns + nl + 2]
        place = _place()
        for w in range(nl):
            for i in range(len(_EXCHANGE_BITS[mode])):
                cp = _exchange_copy(mode, src, land, w, i, place, send_sems, recv_sems, receiving=True)
                cp.wait_send()
                cp.wait_recv()

    arrays = list(srcs) + list(lands)
    res = pl.pallas_call(
        body, name=name, in_specs=[_HBM] * (ns + nl) + [_SEM, _SEM, _ANY], out_specs=[_HBM] * (ns + nl),
        out_shape=[pltpu.HBM(a.shape, a.dtype) for a in arrays],
        input_output_aliases={i: i for i in range(ns + nl)},
        compiler_params=pltpu.CompilerParams(has_side_effects=_EFFECT),
    )(*arrays, send_sems, recv_sems, after)
    return list(res[:ns]), list(res[ns:])


def _small_allreduce_adamw(gvec, wvec, mvec, vvec):
    rows, length = gvec.shape

    def body(g_ref, w_ref, m_ref, v_ref, gs_ref, d_ref, nm_ref, nv_ref, slots, send_sems, recv_sems):
        x, y, c = _place()
        me = _slot(x, y, c)
        slots[me] = g_ref[...]
        sends = []
        for k in range(1, N_DEV):
            peer = _peer(k, x, y, c)
            sends.append(pltpu.make_async_remote_copy(
                src_ref=g_ref, dst_ref=slots.at[me], send_sem=send_sems.at[k - 1], recv_sem=recv_sems.at[k - 1],
                device_id=peer, device_id_type=MESH))
        for cp in sends:
            cp.start()
        for k in range(1, N_DEV):
            peer = _peer(k, x, y, c)
            pltpu.make_async_remote_copy(
                src_ref=g_ref, dst_ref=slots.at[_slot(*peer)], send_sem=send_sems.at[k - 1], recv_sem=recv_sems.at[k - 1],
                device_id=peer, device_id_type=MESH).wait_recv()
        for cp in sends:
            cp.wait_send()
        g = slots[0]
        for s in range(1, N_DEV):
            g = g + slots[s]
        delta, nm, nv = _adamw_math(w_ref[...], g, m_ref[...], v_ref[...])
        gs_ref[...] = g
        d_ref[...] = delta
        nm_ref[...] = nm
        nv_ref[...] = nv

    vmem = pl.BlockSpec(memory_space=pltpu.VMEM)
    out = jax.ShapeDtypeStruct((rows, length), F32)
    return pl.pallas_call(
        body, in_specs=[vmem] * 4, out_specs=[vmem] * 4, out_shape=[out] * 4,
        scratch_shapes=[pltpu.VMEM((N_DEV, rows, length), F32), pltpu.SemaphoreType.DMA((N_DEV - 1,)),
                        pltpu.SemaphoreType.DMA((N_DEV - 1,))],
        name="small_allreduce_adamw",
    )(gvec, wvec, mvec, vvec)


_SMALL_SEGMENTS = (("a_log", GDN_HEADS), ("dt_bias", GDN_HEADS), ("gdn_norm_w", HEAD_DIM), ("pool_scale", GDN_WIDTH),
                   ("ln1_g", D_MODEL), ("ln1_b", D_MODEL), ("ln2_g", D_MODEL), ("ln2_b", D_MODEL),
                   ("ln3_g", D_MODEL), ("ln3_b", D_MODEL), ("conv_w", CONV_K * QKV_COLS))
_SMALL_ROWS = 8
_SMALL_LEN = -(-sum(sz for _, sz in _SMALL_SEGMENTS) // (_SMALL_ROWS * LANE)) * LANE


def _pack_small(vals):
    parts = [vals[n].reshape(-1).astype(F32) if n in vals else jnp.zeros((sz,), F32) for n, sz in _SMALL_SEGMENTS]
    flat = jnp.concatenate(parts)
    flat = jnp.pad(flat, (0, _SMALL_ROWS * _SMALL_LEN - flat.shape[0]))
    return flat.reshape(_SMALL_ROWS, _SMALL_LEN)


def _unpack_small(vec):
    flat = vec.reshape(-1)
    out, off = {}, 0
    for n, sz in _SMALL_SEGMENTS:
        out[n] = flat[off:off + sz]
        off += sz
    return out


_WEIGHT_ORDER = ("w_in", "conv_w", "a_log", "dt_bias", "gdn_norm_w", "pool_w", "pool_scale", "w_out", "ln1_g", "ln1_b",
                 "xq_w", "xk_w", "xv_w", "xo_w", "ln2_g", "ln2_b", "w_up", "w_down", "ln3_g", "ln3_b")


def _shard2d(name, a):
    if name == "w_in":
        return a.T
    return a.reshape(-1, a.shape[-1]) if name == "pool_w" else a


def _shard_result(name, r, shape):
    return r.T[None] if name == "w_in" else r.reshape(shape)


def _gathered_to_full(name, gth):
    if name == "w_up":
        return gth
    if name == "conv_w":
        return jnp.transpose(gth, (1, 0, 2)).reshape(gth.shape[1], N_DEV * gth.shape[2])
    if name == "pool_w":
        g4 = gth.reshape(N_DEV, POOL_GROUPS, POOL_GROUP_DIM // N_DEV, POOL_GROUP_DIM)
        return jnp.transpose(g4, (1, 0, 2, 3)).reshape(POOL_GROUPS, POOL_GROUP_DIM, POOL_GROUP_DIM)
    return gth.reshape(N_DEV * gth.shape[1], gth.shape[2])


def _full_to_chunks(name, full):
    if name == "w_up":
        return full
    if name == "pool_w":
        g4 = full.reshape(POOL_GROUPS, N_DEV, POOL_GROUP_DIM // N_DEV, POOL_GROUP_DIM)
        return jnp.transpose(g4, (1, 0, 2, 3)).reshape(N_DEV, POOL_GROUPS * POOL_GROUP_DIM // N_DEV, POOL_GROUP_DIM)
    return full.reshape(N_DEV, full.shape[0] // N_DEV, full.shape[1])


_GATHER_GROUPS = (("mixer", ("w_in", "conv_w", "pool_w")), ("attn", ("w_out", "xq_w", "xk_w", "xv_w", "xo_w")),
                  ("mlp", ("w_up", "w_down")))


def _grad_chunks(name, g):
    if name == "w_in":
        g = _unpad_w_in(g)
    return _full_to_chunks(name, g.astype(BF16))


def kernel(x, mem, w_in, conv_w, a_log, dt_bias, gdn_norm_w, pool_w, pool_scale, w_out, ln1_g, ln1_b, xq_w, xk_w, xv_w, xo_w, ln2_g, ln2_b, w_up, w_down, ln3_g, ln3_b, loss_target, m_w_in, m_conv_w, m_a_log, m_dt_bias, m_gdn_norm_w, m_pool_w, m_pool_scale, m_w_out, m_ln1_g, m_ln1_b, m_xq_w, m_xk_w, m_xv_w, m_xo_w, m_ln2_g, m_ln2_b, m_w_up, m_w_down, m_ln3_g, m_ln3_b, v_w_in, v_conv_w, v_a_log, v_dt_bias, v_gdn_norm_w, v_pool_w, v_pool_scale, v_w_out, v_ln1_g, v_ln1_b, v_xq_w, v_xk_w, v_xv_w, v_xo_w, v_ln2_g, v_ln2_b, v_w_up, v_w_down, v_ln3_g, v_ln3_b):
    args = dict(locals())
    wt = {n: args[n][0] for n in _WEIGHT_ORDER}
    mo = {n: args["m_" + n][0] for n in _WEIGHT_ORDER}
    vo = {n: args["v_" + n][0] for n in _WEIGHT_ORDER}

    me = _slot(*_place())
    me_arr = jnp.reshape(me, (1,)).astype(jnp.int32)
    nothing = jnp.zeros((8, LANE), F32)

    def landing_zones(names):
        shards = [_shard2d(n, wt[n]).astype(F32 if n == "conv_w" else BF16) for n in names]
        zones = [lax.dynamic_update_slice(lax.empty((N_DEV, *s.shape), s.dtype), s[None], (me, 0, 0)) for s in shards]
        return shards, zones

    chip_arr = jnp.reshape(me // 2, (1,)).astype(jnp.int32)
    core_arr = jnp.reshape(lax.axis_index("c"), (1,)).astype(jnp.int32)
    first, attn_names, mlp_names = (names for _, names in _GATHER_GROUPS)
    gathers = {}

    prepared = {}

    def gather_chips(group, names, after):
        shards, zones = prepared.pop(group) if group in prepared else landing_zones(names)
        gathers[group] = _exchange_start("gather_chips", shards, zones, after, name="gather_chips_" + group)
        return gathers[group][4]

    def gather_pass(group, after):
        _, zones = _exchange_wait("gather_chips", gathers[group], after, name=f"gather_chips_{group}_wait")
        gathers[group] = _exchange_start("gather_pass", [], zones, nothing, name="gather_pass_" + group)
        return gathers[group][4]

    def gathered(group, names, after, token=None):
        _, zones = _exchange_wait("gather_pass", gathers[group], after, name=f"gather_pass_{group}_wait")
        full = {n: _gathered_to_full(n, z) for n, z in zip(names, zones)}
        full.update({n: wt[n] if token is None else wt[n] + token for n in _VECTORS})
        return _group_weights(group, full)

    token = gather_chips("mixer", first, nothing)
    x16 = _cast_bf16(x[0], name="cast_x")
    later = [landing_zones(attn_names), landing_zones(mlp_names)]
    token, x16, later = lax.optimization_barrier((token, x16, later))
    prepared["attn"], prepared["mlp"] = later
    token = gather_chips("attn", attn_names, gather_pass("mixer", token))

    def weights_of(group, after):
        if group == "mixer":
            return gathered(group, first, gathers["attn"][4])
        if group == "ahead":
            return gather_chips("mlp", mlp_names, gather_pass("attn", after))[0:1, 0:1]
        if group == "attn":
            return gathered(group, attn_names, after)
        return gathered(group, mlp_names, gather_pass("mlp", after))

    scatters = {}
    in_flight = []

    def chip_stage(after):
        group, names, started = in_flight.pop()
        chunks, from_sibling = _exchange_wait("scatter_sibling", started, after, name=f"scatter_sibling_{group}_wait")
        sums = [_chip_sums(c, f, core_arr, name=f"chip_sums_{n}") for n, c, f in zip(names, chunks, from_sibling)]
        scatters[group] = (names, _exchange_start("scatter_chips", sums, [lax.empty(s.shape, s.dtype) for s in sums],
                                                  nothing, name="scatter_chips_" + group))
        return scatters[group][1][4]

    def grads_ready(group, grads):
        names = tuple(grads)
        chunks = [_grad_chunks(n, grads[n]) for n in names]
        token = chip_stage(chunks[0]) if in_flight else nothing
        zones = [lax.empty((N_CHIPS, *c.shape[1:]), c.dtype) for c in chunks]
        started = _exchange_start("scatter_sibling", chunks, zones, token, name="scatter_sibling_" + group)
        in_flight.append((group, names, started))
        return started[4][0:1, 0:1]

    sq, grad_x, g = _local_step(x[0], x16, mem[0], loss_target[0], weights_of, grads_ready)
    chip_stage(grad_x)
    small = _finish_small_grads(g)

    out = {}
    after = grad_x
    for group, (names, started) in scatters.items():
        sums, lands = _exchange_wait("scatter_chips", started, after, name=f"scatter_chips_{group}_wait")
        for n, parts, own in zip(names, lands, sums):
            res = _adamw_shard(parts, own, chip_arr, _shard2d(n, wt[n]), _shard2d(n, mo[n]), _shard2d(n, vo[n]),
                               name="adamw_" + n)
            out[n] = [_shard_result(n, r, args[n].shape) for r in res]
            after = res[1]

    packed, _ = lax.optimization_barrier((_pack_small(small), after))
    gs, ds, ms, vs = _small_allreduce_adamw(
        packed, _pack_small({n: wt[n] for n in _VECTORS}), _pack_small({n: mo[n] for n in _VECTORS}),
        _pack_small({n: vo[n] for n in _VECTORS}))
    gs, ds, ms, vs = _unpack_small(gs), _unpack_small(ds), _unpack_small(ms), _unpack_small(vs)
    cols = conv_w.shape[-1]
    conv_full = gs["conv_w"].reshape(CONV_K, QKV_COLS)
    conv_mine = lax.dynamic_slice(conv_full, (0, me * cols), (CONV_K, cols))[None]
    res = _adamw_shard(conv_mine, conv_mine, jnp.zeros((1,), jnp.int32), wt["conv_w"], mo["conv_w"], vo["conv_w"],
                       name="adamw_conv_w")
    out["conv_w"] = [r.reshape(conv_w.shape) for r in res]
    for n in _VECTORS:
        out[n] = [t[n].reshape(args[n].shape) for t in (gs, ds, ms, vs)]

    loss = lax.psum(0.5 * sq[0, 0] / D_MODEL, ("x", "y", "c"))
    return (loss, grad_x[None], *[out[n][0] for n in _WEIGHT_ORDER], *[out[n][1] for n in _WEIGHT_ORDER],
            *[out[n][2] for n in _WEIGHT_ORDER], *[out[n][3] for n in _WEIGHT_ORDER])
```

```python
import functools
import math

import jax
import jax.numpy as jnp
from jax import lax
from jax.experimental import pallas as pl
from jax.experimental.pallas import tpu as pltpu

F32 = jnp.float32
BF16 = jnp.bfloat16
MESH = pl.DeviceIdType.MESH

N_DEV = 8
D_MODEL = 2048
GDN_WIDTH = 1024
GDN_HEADS = 8
HEAD_DIM = 128
CONV_K = 4
CHUNK = 64
POOL_GROUPS = 4
POOL_GROUP_DIM = 256
MEM_LEN = 256
XATTN_HEADS = 4
XATTN_HEAD_DIM = 512
D_FF = 8192
IN_COLS = 5136
ALPHA = 2.0 ** 0.25
LN_EPS = 1e-5
NORM_EPS = 1e-6

LANE = 128
QKV_COLS = 3 * GDN_WIDTH
Z_OFF = QKV_COLS
BA_OFF = 4 * GDN_WIDTH
POOL_OFF = BA_OFF + 2 * LANE
PROJ_COLS = POOL_OFF + GDN_WIDTH
Z_BLK = Z_OFF // LANE
BA_BLK = BA_OFF // LANE
POOL_BLK = POOL_OFF // POOL_GROUP_DIM

ADAM_LR = 0.001
ADAM_B1 = 0.9
ADAM_B2 = 0.999
ADAM_EPS = 1e-08
ADAM_WD = 0.01
ADAM_STEP = 10

VMEM_LIMIT_BYTES = 48 * 1024 * 1024


def _params(*sem):
    return pltpu.CompilerParams(dimension_semantics=sem if sem else None, vmem_limit_bytes=VMEM_LIMIT_BYTES)


def _make_dots(cast, precision, batched=False):
    lead = 1 if batched else 0
    batch = ((0,), (0,)) if batched else ((), ())

    def dg(a, b, ca, cb):
        if cast is not None:
            a = a.astype(cast)
            b = b.astype(cast)
        return lax.dot_general(a, b, (((ca + lead,), (cb + lead,)), batch), precision=precision, preferred_element_type=F32)

    def nn_(a, b):
        return dg(a, b, 1, 0)

    def nt_(a, b):
        return dg(a, b, 1, 1)

    def tn_(a, b):
        return dg(a, b, 0, 0)

    @jax.custom_vjp
    def nn(a, b):
        return nn_(a, b)

    nn.defvjp(lambda a, b: (nn_(a, b), (a, b)), lambda r, g: (nt_(g, r[1]), tn_(r[0], g)))

    @jax.custom_vjp
    def nt(a, b):
        return nt_(a, b)

    nt.defvjp(lambda a, b: (nt_(a, b), (a, b)), lambda r, g: (nn_(g, r[1]), tn_(g, r[0])))

    @jax.custom_vjp
    def tn(a, b):
        return tn_(a, b)

    tn.defvjp(lambda a, b: (tn_(a, b), (a, b)), lambda r, g: (nt_(r[1], g), nn_(r[0], g)))

    return (nn_, nt_, tn_), (nn, nt, tn)


_BDOT_PLAIN, _BDOT_VJP = _make_dots(BF16, None)
_BDOT_BATCH_PLAIN, _BDOT_BATCH_VJP = _make_dots(BF16, None, batched=True)
_FDOT_BATCH_PLAIN, _FDOT_BATCH_VJP = _make_dots(None, lax.Precision.HIGH, batched=True)


def _mm(a, b, *, ta=False, tb=False, out_dtype=F32, tm=None, tn=512, tk=None, epi=None, extra=None, add_scale=1.0,
        b_chunks=False, o_chunks=False, name):
    m, k = (a.shape[1], a.shape[0]) if ta else a.shape
    if b_chunks:
        n, kb = (b.shape[1], N_DEV * b.shape[2]) if tb else (N_DEV * b.shape[2], b.shape[1])
    else:
        n, kb = b.shape if tb else (b.shape[1], b.shape[0])
    assert kb == k, (name, a.shape, b.shape)
    tm, tn, tk = min(tm or m, m), min(tn, n), min(tk or k, k)
    assert m % tm == 0 and n % tn == 0 and k % tk == 0, (name, m, n, k)
    nk = k // tk
    dims = (((0 if ta else 1,), (1 if tb else 0,)), ((), ()))
    n_extra = 0 if epi in (None, "relu2") else 1
    n_out = 2 if epi == "relu2" else 1
    if epi in ("relu2", "mul2r"):
        out_dtype = BF16

    def body(*refs):
        a_ref, b_ref = refs[:2]
        c_ref = refs[2] if n_extra else None
        o_refs = refs[2 + n_extra:2 + n_extra + n_out]
        scr = refs[2 + n_extra + n_out:]
        r = lax.dot_general(a_ref[...].astype(BF16), b_ref[...].astype(BF16), dims, preferred_element_type=F32)

        def finish(v):
            if epi == "add":
                o_refs[0][...] = (v + add_scale * c_ref[...]).astype(out_dtype)
            elif epi == "relu2":
                p = jnp.maximum(v, 0.0)
                o_refs[0][...] = (p * p).astype(BF16)
                o_refs[1][...] = p.astype(BF16)
            elif epi == "mul2r":
                o_refs[0][...] = (v * (2.0 * c_ref[...].astype(F32))).astype(BF16)
            else:
                o_refs[0][...] = v.astype(out_dtype)

        if nk == 1:
            finish(r)
        else:
            acc = scr[0]
            kk = pl.program_id(2)

            @pl.when(kk == 0)
            def _():
                acc[...] = r

            @pl.when(kk > 0)
            def _():
                acc[...] += r

            @pl.when(kk == nk - 1)
            def _():
                finish(acc[...])

    a_spec = pl.BlockSpec((tk, tm), lambda i, j, kk: (kk, i)) if ta else pl.BlockSpec((tm, tk), lambda i, j, kk: (i, kk))
    if b_chunks and tb:
        kc = k // N_DEV // tk
        b_spec = pl.BlockSpec((None, tn, tk), lambda i, j, kk: (kk // kc, j, kk % kc))
    elif b_chunks:
        nc = n // N_DEV // tn
        b_spec = pl.BlockSpec((None, tk, tn), lambda i, j, kk: (j // nc, kk, j % nc))
    elif tb:
        b_spec = pl.BlockSpec((tn, tk), lambda i, j, kk: (j, kk))
    else:
        b_spec = pl.BlockSpec((tk, tn), lambda i, j, kk: (kk, j))
    mn_spec = pl.BlockSpec((tm, tn), lambda i, j, kk: (i, j))
    if o_chunks:
        oc = n // N_DEV // tn
        o_spec = pl.BlockSpec((None, tm, tn), lambda i, j, kk: (j // oc, i, j % oc))
        o_shape = jax.ShapeDtypeStruct((N_DEV, m, n // N_DEV), out_dtype)
    else:
        o_spec, o_shape = mn_spec, jax.ShapeDtypeStruct((m, n), out_dtype)
    res = pl.pallas_call(
        body, grid=(m // tm, n // tn, nk), in_specs=[a_spec, b_spec] + [mn_spec] * n_extra,
        out_specs=[o_spec] * n_out, out_shape=[o_shape] * n_out,
        scratch_shapes=[pltpu.VMEM((tm, tn), F32)] if nk > 1 else [],
        compiler_params=_params("parallel", "parallel", "arbitrary"), name=name,
    )(a, b, *([extra] if n_extra else []))
    return res if n_out > 1 else res[0]


def _cast_bf16(v, *, name, tm=512):
    t, d = v.shape
    tm = min(tm, t)

    def body(v_ref, o_ref):
        o_ref[...] = v_ref[...].astype(BF16)

    spec = pl.BlockSpec((tm, d), lambda i: (i, 0))
    return pl.pallas_call(body, grid=(t // tm,), in_specs=[spec], out_specs=spec,
                          out_shape=jax.ShapeDtypeStruct((t, d), BF16), compiler_params=_params("parallel"), name=name)(v)


def _shift_down(v, s):
    if s == 0:
        return v
    row = lax.broadcasted_iota(jnp.int32, v.shape, 0)
    return jnp.where(row >= s, pltpu.roll(v, s, axis=0), 0.0)


def _shift_up(v, s):
    if s == 0:
        return v
    t = v.shape[0]
    row = lax.broadcasted_iota(jnp.int32, v.shape, 0)
    return jnp.where(row < t - s, pltpu.roll(v, t - s, axis=0), 0.0)


def _post_col(j):
    return (j % GDN_HEADS) * 3 + j // GDN_HEADS


def _gdn_prep_fwd(proj, conv_w):
    t = proj.shape[0]

    def body(x_ref, w_ref, o_ref):
        j = pl.program_id(0)
        x = x_ref[...]
        y = jnp.zeros_like(x)
        for tap in range(CONV_K):
            y = y + w_ref[tap:tap + 1, :] * _shift_down(x, CONV_K - 1 - tap)
        c = y * jax.nn.sigmoid(y)
        nrm = c * lax.rsqrt(jnp.sum(c * c, axis=1, keepdims=True) + NORM_EPS)
        o_ref[...] = jnp.where(j < 2 * GDN_HEADS, nrm, c)

    return pl.pallas_call(
        body, grid=(QKV_COLS // LANE,),
        in_specs=[pl.BlockSpec((t, LANE), lambda j: (0, j)), pl.BlockSpec((CONV_K, LANE), lambda j: (0, j))],
        out_specs=pl.BlockSpec((t, LANE), lambda j: (0, _post_col(j))),
        out_shape=jax.ShapeDtypeStruct((t, QKV_COLS), F32),
        compiler_params=_params("parallel"), name="gdn_prep_fwd",
    )(proj, conv_w)


def _gdn_prep_bwd(proj, conv_w, dpost, dproj):
    t = proj.shape[0]

    def body(x_ref, w_ref, d_ref, _, dx_ref, dw_ref):
        j = pl.program_id(0)
        x = x_ref[...]
        xs = [_shift_down(x, CONV_K - 1 - tap) for tap in range(CONV_K)]
        y = jnp.zeros_like(x)
        for tap in range(CONV_K):
            y = y + w_ref[tap:tap + 1, :] * xs[tap]
        sig = jax.nn.sigmoid(y)
        c = y * sig
        r = lax.rsqrt(jnp.sum(c * c, axis=1, keepdims=True) + NORM_EPS)
        nrm = c * r
        d = d_ref[...]
        dc_norm = r * (d - nrm * jnp.sum(d * nrm, axis=1, keepdims=True))
        dc = jnp.where(j < 2 * GDN_HEADS, dc_norm, d)
        dy = dc * (sig * (1.0 + y * (1.0 - sig)))
        dx = jnp.zeros_like(x)
        for tap in range(CONV_K):
            dx = dx + _shift_up(w_ref[tap:tap + 1, :] * dy, CONV_K - 1 - tap)
            dw_ref[tap:tap + 1, :] = jnp.sum(dy * xs[tap], axis=0, keepdims=True)
        dx_ref[...] = dx.astype(dx_ref.dtype)

    return pl.pallas_call(
        body, grid=(QKV_COLS // LANE,),
        in_specs=[pl.BlockSpec((t, LANE), lambda j: (0, j)), pl.BlockSpec((CONV_K, LANE), lambda j: (0, j)),
                  pl.BlockSpec((t, LANE), lambda j: (0, _post_col(j))), pl.BlockSpec(memory_space=pl.ANY)],
        out_specs=[pl.BlockSpec((t, LANE), lambda j: (0, j)), pl.BlockSpec((CONV_K, LANE), lambda j: (0, j))],
        out_shape=[jax.ShapeDtypeStruct(dproj.shape, dproj.dtype), jax.ShapeDtypeStruct((CONV_K, QKV_COLS), F32)],
        input_output_aliases={3: 0},
        compiler_params=_params("parallel"), name="gdn_prep_bwd",
    )(proj, conv_w, dpost, dproj)


def _softplus(v):
    return jnp.maximum(v, 0.0) + jnp.log(1.0 + jnp.exp(-jnp.abs(v)))


def _tri_inv(low, nn):
    r = lax.broadcasted_iota(jnp.int32, (CHUNK, CHUNK), 0)
    c = lax.broadcasted_iota(jnp.int32, (CHUNK, CHUNK), 1)
    eye = (r == c).astype(F32)
    same_blk = lax.shift_right_logical(r, 4) == lax.shift_right_logical(c, 4)
    diag = jnp.where(same_blk, low, 0.0)
    off = low - diag
    n1 = -diag
    n2 = nn(n1, n1)
    n4 = nn(n2, n2)
    n8 = nn(n4, n4)
    inv_d = nn(nn(nn(eye + n1, eye + n2), eye + n4), eye + n8)
    m1 = nn(inv_d, off)
    m2 = nn(m1, m1)
    return nn(nn(eye - m1, eye + m2), inv_d)


LOCAL_HEADS_PER_STEP = 8


def _gdn_local_fn(qkv, ba, alog_row, dtb_row, first_head, bdots, fdots):
    nn, nt, tn = bdots
    fnn = fdots[0]
    n_heads = qkv.shape[1] // (3 * HEAD_DIM)
    part = lambda i, p: qkv[:, (3 * i + p) * HEAD_DIM:(3 * i + p + 1) * HEAD_DIM]
    q = jnp.stack([part(i, 0) for i in range(n_heads)]) * (HEAD_DIM ** -0.5)
    k = jnp.stack([part(i, 1) for i in range(n_heads)])
    v = jnp.stack([part(i, 2) for i in range(n_heads)])
    lane = lax.broadcasted_iota(jnp.int32, ba.shape, 1)
    bg = jnp.where(lane < GDN_HEADS, jax.nn.sigmoid(ba), -jnp.exp(alog_row) * _softplus(ba + dtb_row))
    pick = lambda l: jnp.sum(jnp.where(lane == l, bg, 0.0), axis=1, keepdims=True)
    beta = jnp.stack([pick(first_head + i) for i in range(n_heads)])
    g = jnp.stack([pick(first_head + i + GDN_HEADS) for i in range(n_heads)])

    r = lax.broadcasted_iota(jnp.int32, (CHUNK, CHUNK), 0)
    c = lax.broadcasted_iota(jnp.int32, (CHUNK, CHUNK), 1)
    incl = r >= c
    strict = r > c
    eye = r == c

    def to_row(col):
        return jnp.sum(jnp.where(eye, col, 0.0), axis=1, keepdims=True)

    gc = jnp.sum(jnp.where(incl, to_row(g), 0.0), axis=2, keepdims=True)
    diff = gc - to_row(gc)
    decay = jnp.where(incl, jnp.exp(jnp.where(incl, diff, 0.0)), 0.0)
    k_beta = k * beta
    v_beta = v * beta
    low = jnp.where(strict, nt(k_beta, k) * decay, 0.0)
    t_inv = _tri_inv(low, fnn)
    eg = jnp.exp(gc)
    u = fnn(t_inv, v_beta)
    w = fnn(t_inv, k_beta * eg)
    attn = jnp.where(incl, nt(q, k) * decay, 0.0)
    last = lax.broadcasted_iota(jnp.int32, (CHUNK, 1), 0) == CHUNK - 1
    g_last = jnp.sum(jnp.where(last, gc, 0.0), axis=1, keepdims=True)
    kdec = k * jnp.exp(g_last - gc)
    elast = jnp.broadcast_to(jnp.exp(g_last), (n_heads, 1, LANE))
    return u, w, q * eg, kdec, attn, elast


def _gdn_state_fn(u, w, qg, kdec, attn, elast, state, bdots):
    nn, _, tn = bdots
    v_new = u - nn(w, state)
    o = nn(qg, state) + nn(attn, v_new)
    return o, state * elast + tn(kdec, v_new)


def _gdn_local_fwd(post, proj, alog_row, dtb_row):
    t = post.shape[0]
    n_chunks = t // CHUNK
    hb = LOCAL_HEADS_PER_STEP

    def body(qkv_ref, ba_ref, al_ref, dt_ref, u_ref, w_ref, qg_ref, kd_ref, at_ref, el_ref):
        u, w, qg, kdec, attn, elast = _gdn_local_fn(qkv_ref[...], ba_ref[...], al_ref[...], dt_ref[...],
                                                    pl.program_id(1) * hb, _BDOT_BATCH_PLAIN, _FDOT_BATCH_PLAIN)
        for i in range(hb):
            cols = slice(i * HEAD_DIM, (i + 1) * HEAD_DIM)
            u_ref[:, cols] = u[i]
            w_ref[:, cols] = w[i].astype(BF16)
            qg_ref[:, cols] = qg[i].astype(BF16)
            kd_ref[:, cols] = kdec[i].astype(BF16)
        at_ref[...] = attn.astype(BF16)
        el_ref[:, 0] = elast

    wide = pl.BlockSpec((CHUNK, hb * HEAD_DIM), lambda n, j: (n, j))
    row = pl.BlockSpec((1, LANE), lambda n, j: (0, 0))
    return pl.pallas_call(
        body, grid=(n_chunks, GDN_HEADS // hb),
        in_specs=[pl.BlockSpec((CHUNK, hb * 3 * HEAD_DIM), lambda n, j: (n, j)),
                  pl.BlockSpec((CHUNK, LANE), lambda n, j: (n, BA_BLK)), row, row],
        out_specs=[wide, wide, wide, wide, pl.BlockSpec((hb, CHUNK, CHUNK), lambda n, j: (j, n, 0)),
                   pl.BlockSpec((hb, 1, 1, LANE), lambda n, j: (j, n, 0, 0))],
        out_shape=[jax.ShapeDtypeStruct((t, GDN_WIDTH), F32), jax.ShapeDtypeStruct((t, GDN_WIDTH), BF16),
                   jax.ShapeDtypeStruct((t, GDN_WIDTH), BF16), jax.ShapeDtypeStruct((t, GDN_WIDTH), BF16),
                   jax.ShapeDtypeStruct((GDN_HEADS, t, CHUNK), BF16),
                   jax.ShapeDtypeStruct((GDN_HEADS, n_chunks, 1, LANE), F32)],
        compiler_params=_params("parallel", "parallel"), name="gdn_local_fwd",
    )(post, proj, alog_row, dtb_row)


def _gdn_state_specs(n_of):
    wide = pl.BlockSpec((CHUNK, GDN_WIDTH), lambda n: (n_of(n), 0))
    attn = pl.BlockSpec((GDN_HEADS, CHUNK, CHUNK), lambda n: (0, n_of(n), 0))
    elast = pl.BlockSpec((GDN_HEADS, 1, 1, LANE), lambda n: (0, n_of(n), 0, 0))
    saved = pl.BlockSpec((GDN_HEADS, 1, HEAD_DIM, HEAD_DIM), lambda n: (0, n_of(n), 0, 0))
    return wide, attn, elast, saved


def _gdn_state_fwd(u, w, qg, kdec, attn, elast):
    t = u.shape[0]
    n_chunks = t // CHUNK

    def body(u_ref, w_ref, qg_ref, kd_ref, at_ref, el_ref, o_ref, save_ref, state_ref):
        @pl.when(pl.program_id(0) == 0)
        def _():
            state_ref[...] = jnp.zeros_like(state_ref)

        for h in range(GDN_HEADS):
            cols = slice(h * HEAD_DIM, (h + 1) * HEAD_DIM)
            state = state_ref[h]
            save_ref[h, 0] = state
            o, new_state = _gdn_state_fn(u_ref[:, cols], w_ref[:, cols], qg_ref[:, cols], kd_ref[:, cols], at_ref[h],
                                         el_ref[h, 0], state, _BDOT_PLAIN)
            o_ref[:, cols] = o
            state_ref[h] = new_state

    wide, attn_spec, elast_spec, saved_spec = _gdn_state_specs(lambda n: n)
    return pl.pallas_call(
        body, grid=(n_chunks,), in_specs=[wide, wide, wide, wide, attn_spec, elast_spec],
        out_specs=[wide, saved_spec],
        out_shape=[jax.ShapeDtypeStruct((t, GDN_WIDTH), F32),
                   jax.ShapeDtypeStruct((GDN_HEADS, n_chunks, HEAD_DIM, HEAD_DIM), F32)],
        scratch_shapes=[pltpu.VMEM((GDN_HEADS, HEAD_DIM, HEAD_DIM), F32)],
        compiler_params=_params("arbitrary"), name="gdn_state_fwd",
    )(u, w, qg, kdec, attn, elast)


def _gdn_state_bwd(u, w, qg, kdec, attn, elast, saved, do):
    t = u.shape[0]
    n_chunks = t // CHUNK
    last = n_chunks - 1

    def body(u_ref, w_ref, qg_ref, kd_ref, at_ref, el_ref, save_ref, do_ref,
             du_ref, dw_ref, dqg_ref, dkd_ref, dat_ref, del_ref, dstate_ref):
        @pl.when(pl.program_id(0) == 0)
        def _():
            dstate_ref[...] = jnp.zeros_like(dstate_ref)

        for h in range(GDN_HEADS):
            cols = slice(h * HEAD_DIM, (h + 1) * HEAD_DIM)
            _, vjp = jax.vjp(
                lambda *a: _gdn_state_fn(*a, _BDOT_VJP), u_ref[:, cols], w_ref[:, cols].astype(F32),
                qg_ref[:, cols].astype(F32), kd_ref[:, cols].astype(F32), at_ref[h].astype(F32), el_ref[h, 0], save_ref[h, 0])
            du, dw, dqg, dkd, dat, de, dstate = vjp((do_ref[:, cols], dstate_ref[h]))
            du_ref[:, cols] = du
            dw_ref[:, cols] = dw
            dqg_ref[:, cols] = dqg
            dkd_ref[:, cols] = dkd
            dat_ref[h] = dat
            del_ref[h, 0] = de
            dstate_ref[h] = dstate

    wide, attn_spec, elast_spec, saved_spec = _gdn_state_specs(lambda n: last - n)
    wide_f32 = jax.ShapeDtypeStruct((t, GDN_WIDTH), F32)
    return pl.pallas_call(
        body, grid=(n_chunks,), in_specs=[wide, wide, wide, wide, attn_spec, elast_spec, saved_spec, wide],
        out_specs=[wide, wide, wide, wide, attn_spec, elast_spec],
        out_shape=[wide_f32, wide_f32, wide_f32, wide_f32, jax.ShapeDtypeStruct((GDN_HEADS, t, CHUNK), F32),
                   jax.ShapeDtypeStruct((GDN_HEADS, n_chunks, 1, LANE), F32)],
        scratch_shapes=[pltpu.VMEM((GDN_HEADS, HEAD_DIM, HEAD_DIM), F32)],
        compiler_params=_params("arbitrary"), name="gdn_state_bwd",
    )(u, w, qg, kdec, attn, elast, saved, do)


def _gdn_local_bwd(post, proj, alog_row, dtb_row, cots, dproj):
    t = post.shape[0]
    n_chunks = t // CHUNK
    hb = LOCAL_HEADS_PER_STEP
    n_steps = GDN_HEADS // hb

    def body(qkv_ref, ba_ref, al_ref, dt_ref, du_ref, dw_ref, dqg_ref, dkd_ref, dat_ref, del_ref, _,
             dqkv_ref, dba_ref, dal_ref, ddt_ref, dba_acc):
        n = pl.program_id(0)
        j = pl.program_id(1)

        @pl.when((n == 0) & (j == 0))
        def _():
            dal_ref[...] = jnp.zeros_like(dal_ref)
            ddt_ref[...] = jnp.zeros_like(ddt_ref)

        @pl.when(j == 0)
        def _():
            dba_acc[...] = jnp.zeros_like(dba_acc)

        heads = lambda ref: jnp.stack([ref[:, i * HEAD_DIM:(i + 1) * HEAD_DIM] for i in range(hb)])
        _, vjp = jax.vjp(lambda a, b, c, d: _gdn_local_fn(a, b, c, d, j * hb, _BDOT_BATCH_VJP, _FDOT_BATCH_VJP),
                         qkv_ref[...], ba_ref[...], al_ref[...], dt_ref[...])
        dqkv, dba, dal, ddt = vjp((heads(du_ref), heads(dw_ref), heads(dqg_ref), heads(dkd_ref), dat_ref[...],
                                   del_ref[:, 0]))
        dqkv_ref[...] = dqkv
        dba_acc[...] += dba
        dal_ref[...] += dal
        ddt_ref[...] += ddt

        @pl.when(j == n_steps - 1)
        def _():
            dba_ref[:, 0:LANE] = dba_acc[...].astype(dba_ref.dtype)
            dba_ref[:, LANE:2 * LANE] = jnp.zeros((CHUNK, LANE), dba_ref.dtype)

    wide = pl.BlockSpec((CHUNK, hb * HEAD_DIM), lambda n, j: (n, j))
    qkv_spec = pl.BlockSpec((CHUNK, hb * 3 * HEAD_DIM), lambda n, j: (n, j))
    row = pl.BlockSpec((1, LANE), lambda n, j: (0, 0))
    return pl.pallas_call(
        body, grid=(n_chunks, n_steps),
        in_specs=[qkv_spec, pl.BlockSpec((CHUNK, LANE), lambda n, j: (n, BA_BLK)), row, row, wide, wide, wide, wide,
                  pl.BlockSpec((hb, CHUNK, CHUNK), lambda n, j: (j, n, 0)),
                  pl.BlockSpec((hb, 1, 1, LANE), lambda n, j: (j, n, 0, 0)), pl.BlockSpec(memory_space=pl.ANY)],
        out_specs=[qkv_spec, pl.BlockSpec((CHUNK, 2 * LANE), lambda n, j: (n, BA_BLK // 2)), row, row],
        out_shape=[jax.ShapeDtypeStruct((t, QKV_COLS), F32), jax.ShapeDtypeStruct(dproj.shape, dproj.dtype),
                   jax.ShapeDtypeStruct((1, LANE), F32), jax.ShapeDtypeStruct((1, LANE), F32)],
        input_output_aliases={10: 1},
        scratch_shapes=[pltpu.VMEM((CHUNK, LANE), F32)],
        compiler_params=_params("arbitrary", "arbitrary"), name="gdn_local_bwd",
    )(post, proj, alog_row, dtb_row, *cots, dproj)


def _onorm_fn(o, z, w):
    return o * lax.rsqrt(jnp.mean(o * o, axis=1, keepdims=True) + NORM_EPS) * w * (z * jax.nn.sigmoid(z))


def _onorm_fwd(o_raw, proj, norm_w, mixin, tm=512):
    t = o_raw.shape[0]
    tm = min(tm, t)

    def body(o_ref, z_ref, w_ref, _, out_ref):
        out_ref[...] = _onorm_fn(o_ref[...], z_ref[...], w_ref[...]).astype(out_ref.dtype)

    return pl.pallas_call(
        body, grid=(t // tm, GDN_HEADS),
        in_specs=[pl.BlockSpec((tm, LANE), lambda i, h: (i, h)), pl.BlockSpec((tm, LANE), lambda i, h: (i, Z_BLK + h)),
                  pl.BlockSpec((1, LANE), lambda i, h: (0, 0)), pl.BlockSpec(memory_space=pl.ANY)],
        out_specs=pl.BlockSpec((tm, LANE), lambda i, h: (i, h)),
        out_shape=jax.ShapeDtypeStruct(mixin.shape, mixin.dtype), input_output_aliases={3: 0},
        compiler_params=_params("parallel", "parallel"), name="gdn_onorm_fwd",
    )(o_raw, proj, norm_w, mixin)


def _onorm_bwd(o_raw, proj, norm_w, dmixin, dproj, tm=512):
    t = o_raw.shape[0]
    tm = min(tm, t)

    def body(o_ref, z_ref, w_ref, d_ref, _, do_ref, dz_ref, dw_ref):
        @pl.when((pl.program_id(0) == 0) & (pl.program_id(1) == 0))
        def _():
            dw_ref[...] = jnp.zeros_like(dw_ref)

        _, vjp = jax.vjp(_onorm_fn, o_ref[...], z_ref[...], w_ref[...])
        do, dz, dw = vjp(d_ref[...])
        do_ref[...] = do
        dz_ref[...] = dz.astype(dz_ref.dtype)
        dw_ref[...] += dw

    return pl.pallas_call(
        body, grid=(t // tm, GDN_HEADS),
        in_specs=[pl.BlockSpec((tm, LANE), lambda i, h: (i, h)), pl.BlockSpec((tm, LANE), lambda i, h: (i, Z_BLK + h)),
                  pl.BlockSpec((1, LANE), lambda i, h: (0, 0)), pl.BlockSpec((tm, LANE), lambda i, h: (i, h)),
                  pl.BlockSpec(memory_space=pl.ANY)],
        out_specs=[pl.BlockSpec((tm, LANE), lambda i, h: (i, h)), pl.BlockSpec((tm, LANE), lambda i, h: (i, Z_BLK + h)),
                   pl.BlockSpec((1, LANE), lambda i, h: (0, 0))],
        out_shape=[jax.ShapeDtypeStruct((t, GDN_WIDTH), F32), jax.ShapeDtypeStruct(dproj.shape, dproj.dtype),
                   jax.ShapeDtypeStruct((1, LANE), F32)],
        input_output_aliases={4: 1},
        compiler_params=_params("arbitrary", "arbitrary"), name="gdn_onorm_bwd",
    )(o_raw, proj, norm_w, dmixin, dproj)


def _pool_select(levels, gi):
    out = levels[-1]
    for lvl in range(len(levels) - 2, -1, -1):
        out = jnp.where(gi == lvl, levels[lvl], out)
    return out


def _pool_count(shape, gi):
    pos = lax.broadcasted_iota(jnp.int32, shape, 0)
    win = lax.shift_left(jnp.int32(2), gi)
    return jnp.minimum(pos + 1, win).astype(F32)


def _pooled(p, gi):
    acc = p
    levels = []
    for lvl in range(POOL_GROUPS):
        acc = acc + _shift_down(acc, 1 << lvl)
        levels.append(acc)
    return _pool_select(levels, gi) / _pool_count(p.shape, gi) - p


def _pool_fwd(proj, pool_w, pool_scale):
    t = proj.shape[0]

    def body(p_ref, w_ref, s_ref, out_ref):
        gi = pl.program_id(0)
        pooled = _pooled(p_ref[...], gi)
        out_ref[...] = (_BDOT_PLAIN[0](pooled, w_ref[0]) * s_ref[0]).astype(out_ref.dtype)

    return pl.pallas_call(
        body, grid=(POOL_GROUPS,),
        in_specs=[pl.BlockSpec((t, POOL_GROUP_DIM), lambda g: (0, POOL_BLK + g)),
                  pl.BlockSpec((1, POOL_GROUP_DIM, POOL_GROUP_DIM), lambda g: (g, 0, 0)),
                  pl.BlockSpec((1, 1, POOL_GROUP_DIM), lambda g: (g, 0, 0))],
        out_specs=pl.BlockSpec((t, POOL_GROUP_DIM), lambda g: (0, GDN_WIDTH // POOL_GROUP_DIM + g)),
        out_shape=jax.ShapeDtypeStruct((t, 2 * GDN_WIDTH), BF16),
        compiler_params=_params("parallel"), name="pool_fwd",
    )(proj, pool_w, pool_scale)


def _pool_bwd(proj, pool_w, pool_scale, dmixin):
    t = proj.shape[0]
    nn, nt, tn = _BDOT_PLAIN

    def body(p_ref, w_ref, s_ref, d_ref, dp_ref, dw_ref, ds_ref):
        gi = pl.program_id(0)
        p = p_ref[...]
        pooled = _pooled(p, gi)
        mixed = nn(pooled, w_ref[0])
        d = d_ref[...]
        ds_ref[0] = jnp.sum(d * mixed, axis=0, keepdims=True)
        dmixed = d * s_ref[0]
        dw_ref[0] = tn(pooled, dmixed)
        dpooled = nt(dmixed, w_ref[0])
        acc = dpooled / _pool_count(p.shape, gi)
        levels = []
        for lvl in range(POOL_GROUPS):
            acc = acc + _shift_up(acc, 1 << lvl)
            levels.append(acc)
        dp_ref[...] = (_pool_select(levels, gi) - dpooled).astype(dp_ref.dtype)

    return pl.pallas_call(
        body, grid=(POOL_GROUPS,),
        in_specs=[pl.BlockSpec((t, POOL_GROUP_DIM), lambda g: (0, POOL_BLK + g)),
                  pl.BlockSpec((1, POOL_GROUP_DIM, POOL_GROUP_DIM), lambda g: (g, 0, 0)),
                  pl.BlockSpec((1, 1, POOL_GROUP_DIM), lambda g: (g, 0, 0)),
                  pl.BlockSpec((t, POOL_GROUP_DIM), lambda g: (0, GDN_WIDTH // POOL_GROUP_DIM + g))],
        out_specs=[pl.BlockSpec((t, POOL_GROUP_DIM), lambda g: (0, POOL_BLK + g)),
                   pl.BlockSpec((1, POOL_GROUP_DIM, POOL_GROUP_DIM), lambda g: (g, 0, 0)),
                   pl.BlockSpec((1, 1, POOL_GROUP_DIM), lambda g: (g, 0, 0))],
        out_shape=[jax.ShapeDtypeStruct((t, PROJ_COLS), BF16),
                   jax.ShapeDtypeStruct((POOL_GROUPS, POOL_GROUP_DIM, POOL_GROUP_DIM), F32),
                   jax.ShapeDtypeStruct((POOL_GROUPS, 1, POOL_GROUP_DIM), F32)],
        compiler_params=_params("parallel"), name="pool_bwd",
    )(proj, pool_w, pool_scale, dmixin)


def _ln_stats(s):
    mu = jnp.mean(s, axis=1, keepdims=True)
    xc = s - mu
    var = jnp.mean(xc * xc, axis=1, keepdims=True)
    rstd = lax.rsqrt(var + LN_EPS)
    return xc * rstd, rstd


def _ln_fwd(h_in, y, g, b, *, name, tm=256):
    t, d = h_in.shape
    tm = min(tm, t)

    def body(h_ref, y_ref, g_ref, b_ref, o_ref, o16_ref):
        xhat, _ = _ln_stats(ALPHA * h_ref[...] + y_ref[...])
        out = xhat * g_ref[...] + b_ref[...]
        o_ref[...] = out
        o16_ref[...] = out.astype(BF16)

    row = pl.BlockSpec((tm, d), lambda i: (i, 0))
    vec = pl.BlockSpec((1, d), lambda i: (0, 0))
    return pl.pallas_call(
        body, grid=(t // tm,), in_specs=[row, row, vec, vec], out_specs=[row, row],
        out_shape=[jax.ShapeDtypeStruct((t, d), F32), jax.ShapeDtypeStruct((t, d), BF16)],
        compiler_params=_params("parallel"), name=name,
    )(h_in, y, g, b)


def _ln_loss_fwd(h_in, y, g, b, target, *, name, tm=256):
    t, d = h_in.shape
    tm = min(tm, t)

    def body(h_ref, y_ref, g_ref, b_ref, t_ref, dy_ref, sq_ref):
        @pl.when(pl.program_id(0) == 0)
        def _():
            sq_ref[...] = jnp.zeros_like(sq_ref)

        xhat, _ = _ln_stats(ALPHA * h_ref[...] + y_ref[...])
        err = xhat * g_ref[...] + b_ref[...] - t_ref[...]
        dy_ref[...] = err * (1.0 / d)
        sq_ref[...] += jnp.sum(jnp.sum(err * err, axis=1, keepdims=True), axis=0, keepdims=True)

    row = pl.BlockSpec((tm, d), lambda i: (i, 0))
    vec = pl.BlockSpec((1, d), lambda i: (0, 0))
    return pl.pallas_call(
        body, grid=(t // tm,), in_specs=[row, row, vec, vec, row],
        out_specs=[row, pl.BlockSpec((1, LANE), lambda i: (0, 0))],
        out_shape=[jax.ShapeDtypeStruct((t, d), F32), jax.ShapeDtypeStruct((1, LANE), F32)],
        compiler_params=_params("arbitrary"), name=name,
    )(h_in, y, g, b, target)


def _ln_bwd(h_in, y, g, d_a, d_b, *, name, tm=256):
    t, d = h_in.shape
    tm = min(tm, t)
    has_b = d_b is not None

    def body(*refs):
        if has_b:
            h_ref, y_ref, g_ref, da_ref, db_ref, ds_ref, ds16_ref, dg_ref, dbias_ref = refs
        else:
            h_ref, y_ref, g_ref, da_ref, ds_ref, ds16_ref, dg_ref, dbias_ref = refs

        @pl.when(pl.program_id(0) == 0)
        def _():
            dg_ref[...] = jnp.zeros_like(dg_ref)
            dbias_ref[...] = jnp.zeros_like(dbias_ref)

        xhat, rstd = _ln_stats(ALPHA * h_ref[...] + y_ref[...])
        dout = da_ref[...]
        if has_b:
            dout = dout + ALPHA * db_ref[...]
        dxhat = dout * g_ref[...]
        m1 = jnp.mean(dxhat, axis=1, keepdims=True)
        m2 = jnp.mean(dxhat * xhat, axis=1, keepdims=True)
        ds = rstd * (dxhat - m1 - xhat * m2)
        ds_ref[...] = ds
        ds16_ref[...] = ds.astype(BF16)
        dg_ref[...] += jnp.sum(dout * xhat, axis=0, keepdims=True)
        dbias_ref[...] += jnp.sum(dout, axis=0, keepdims=True)

    row = pl.BlockSpec((tm, d), lambda i: (i, 0))
    vec = pl.BlockSpec((1, d), lambda i: (0, 0))
    args = [h_in, y, g, d_a] + ([d_b] if has_b else [])
    return pl.pallas_call(
        body, grid=(t // tm,), in_specs=[row, row, vec, row] + ([row] if has_b else []),
        out_specs=[row, row, vec, vec],
        out_shape=[jax.ShapeDtypeStruct((t, d), F32), jax.ShapeDtypeStruct((t, d), BF16),
                   jax.ShapeDtypeStruct((1, d), F32), jax.ShapeDtypeStruct((1, d), F32)],
        compiler_params=_params("arbitrary"), name=name,
    )(*args)


def _attn_fn(q, k, v, dots):
    nn, nt, _ = dots
    s = nt(q, k) * (XATTN_HEAD_DIM ** -0.5)
    s = s - lax.stop_gradient(jnp.max(s, axis=1, keepdims=True))
    e = jnp.exp(s)
    p = e / jnp.sum(e, axis=1, keepdims=True)
    return nn(p, v)


def _attn_fwd(q, k, v, tq=512):
    t = q.shape[0]
    tq = min(tq, t)

    def body(q_ref, k_ref, v_ref, o_ref):
        o_ref[...] = _attn_fn(q_ref[...], k_ref[...], v_ref[...], _BDOT_PLAIN).astype(BF16)

    qs = pl.BlockSpec((tq, XATTN_HEAD_DIM), lambda h, i: (i, h))
    ks = pl.BlockSpec((MEM_LEN, XATTN_HEAD_DIM), lambda h, i: (0, h))
    return pl.pallas_call(
        body, grid=(XATTN_HEADS, t // tq), in_specs=[qs, ks, ks], out_specs=qs,
        out_shape=jax.ShapeDtypeStruct(q.shape, BF16), compiler_params=_params("parallel", "parallel"), name="xattn_fwd",
    )(q, k, v)


def _attn_bwd(q, k, v, do, tq=512):
    t = q.shape[0]
    tq = min(tq, t)

    def body(q_ref, k_ref, v_ref, do_ref, dq_ref, dk_ref, dv_ref):
        @pl.when(pl.program_id(1) == 0)
        def _():
            dk_ref[...] = jnp.zeros_like(dk_ref)
            dv_ref[...] = jnp.zeros_like(dv_ref)

        _, vjp = jax.vjp(lambda a, b, c: _attn_fn(a, b, c, _BDOT_VJP), q_ref[...].astype(F32), k_ref[...].astype(F32),
                         v_ref[...].astype(F32))
        dq, dk, dv = vjp(do_ref[...].astype(F32))
        dq_ref[...] = dq.astype(BF16)
        dk_ref[...] += dk
        dv_ref[...] += dv

    qs = pl.BlockSpec((tq, XATTN_HEAD_DIM), lambda h, i: (i, h))
    ks = pl.BlockSpec((MEM_LEN, XATTN_HEAD_DIM), lambda h, i: (0, h))
    return pl.pallas_call(
        body, grid=(XATTN_HEADS, t // tq), in_specs=[qs, ks, ks, qs], out_specs=[qs, ks, ks],
        out_shape=[jax.ShapeDtypeStruct(q.shape, BF16), jax.ShapeDtypeStruct(k.shape, F32), jax.ShapeDtypeStruct(v.shape, F32)],
        compiler_params=_params("parallel", "arbitrary"), name="xattn_bwd",
    )(q, k, v, do)


def _local_step(x, x16, mem, target, weights_of, grads_ready):
    def behind(vec, token):
        return vec if token is None else vec + token

    w = dict(weights_of("mixer", None))
    proj = _mm(x16, w["w_in"], tb=True, tn=768, name="mm_in_proj")
    post = _gdn_prep_fwd(proj, w["conv_w"])
    mixin = _pool_fwd(proj, w["pool_w"], w["pool_scale"])
    token = weights_of("ahead", mixin)
    chunked = _gdn_local_fwd(post, proj, behind(w["alog_row"], token), w["dtb_row"])
    o_raw, saved = _gdn_state_fwd(*chunked)
    mixin = _onorm_fwd(o_raw, proj, w["gdn_norm_w"], mixin)
    w.update(weights_of("attn", mixin))
    mix = _mm(mixin, w["w_out"], name="mm_out_proj")
    h1, h1_16 = _ln_fwd(x, mix, w["ln1_g"], w["ln1_b"], name="ln1_fwd")
    xq = _mm(h1_16, w["xq_w"], out_dtype=BF16, name="mm_xq")
    xk = _mm(mem, w["xk_w"], out_dtype=BF16, name="mm_xk")
    xv = _mm(mem, w["xv_w"], out_dtype=BF16, name="mm_xv")
    xo = _attn_fwd(xq, xk, xv)
    xa = _mm(xo, w["xo_w"], name="mm_xo")
    h2, h2_16 = _ln_fwd(h1, xa, w["ln2_g"], w["ln2_b"], name="ln2_fwd")
    w.update(weights_of("mlp", h2_16))
    act, relu = _mm(h2_16, w["w_up"], b_chunks=True, epi="relu2", name="mm_up")
    ff = _mm(act, w["w_down"], tn=1024, tk=512, name="mm_down")
    dy, sq = _ln_loss_fwd(h2, ff, w["ln3_g"], w["ln3_b"], target, name="ln3_loss_fwd")

    g = {}
    ds3, ds3_16, g["ln3_g"], g["ln3_b"] = _ln_bwd(h2, ff, w["ln3_g"], dy, None, name="ln3_bwd")
    gw_down = _mm(act, ds3_16, ta=True, out_dtype=BF16, tm=512, tn=D_MODEL, name="mm_gw_down")
    du = _mm(ds3_16, w["w_down"], tb=True, epi="mul2r", extra=relu, name="mm_du")
    gw_up = _mm(h2_16, du, ta=True, out_dtype=BF16, o_chunks=True, name="mm_gw_up")
    token = grads_ready("mlp", {"w_down": gw_down, "w_up": gw_up})
    dh2 = _mm(du, w["w_up"], tb=True, b_chunks=True, tn=1024, tk=512, name="mm_dh2")
    ds2, ds2_16, g["ln2_g"], g["ln2_b"] = _ln_bwd(h1, xa, behind(w["ln2_g"], token), dh2, ds3, name="ln2_bwd")
    gw_xo = _mm(xo, ds2_16, ta=True, out_dtype=BF16, name="mm_gw_xo")
    dxo = _mm(ds2_16, w["xo_w"], tb=True, out_dtype=BF16, name="mm_dxo")
    dxq, dxk, dxv = _attn_bwd(xq, xk, xv, dxo)
    gw_xq = _mm(h1_16, dxq, ta=True, out_dtype=BF16, name="mm_gw_xq")
    gw_xk = _mm(mem, dxk, ta=True, out_dtype=BF16, name="mm_gw_xk")
    gw_xv = _mm(mem, dxv, ta=True, out_dtype=BF16, name="mm_gw_xv")
    token = grads_ready("attn", {"xo_w": gw_xo, "xq_w": gw_xq, "xk_w": gw_xk, "xv_w": gw_xv})
    dh1 = _mm(dxq, w["xq_w"], tb=True, name="mm_dh1")
    ds1, ds1_16, g["ln1_g"], g["ln1_b"] = _ln_bwd(x, mix, behind(w["ln1_g"], token), dh1, ds2, name="ln1_bwd")
    gw_out = _mm(mixin, ds1_16, ta=True, out_dtype=BF16, name="mm_gw_out")
    dmixin = _mm(ds1_16, w["w_out"], tb=True, name="mm_dmixin")
    dproj, gw_pool, g["pool_scale"] = _pool_bwd(proj, w["pool_w"], w["pool_scale"], dmixin)
    token = grads_ready("mix", {"w_out": gw_out, "pool_w": gw_pool})
    do_raw, dproj, g["gdn_norm_w"] = _onorm_bwd(o_raw, proj, behind(w["gdn_norm_w"], token), dmixin, dproj)
    cots = _gdn_state_bwd(*chunked, saved, do_raw)
    token = grads_ready("tick", {"after": cots[0]})
    dpost, dproj, g["alog_row"], g["dtb_row"] = _gdn_local_bwd(post, proj, behind(w["alog_row"], token), w["dtb_row"],
                                                               cots, dproj)
    dproj, g["conv_w"] = _gdn_prep_bwd(proj, w["conv_w"], dpost, dproj)
    gw_in = _mm(dproj, x16, ta=True, out_dtype=BF16, tm=768, tn=D_MODEL, name="mm_gw_in")
    token = grads_ready("in", {"w_in": gw_in})
    if token is not None:
        ds1, _ = lax.optimization_barrier((ds1, token))
    grad_x = _mm(dproj, w["w_in"], tk=768, epi="add", extra=ds1, add_scale=ALPHA, name="mm_dx")
    return sq, grad_x, g


_MATRICES = ("w_in", "pool_w", "w_out", "xq_w", "xk_w", "xv_w", "xo_w", "w_up", "w_down")
_VECTORS = ("a_log", "dt_bias", "gdn_norm_w", "pool_scale", "ln1_g", "ln1_b", "ln2_g", "ln2_b", "ln3_g", "ln3_b")
_BA_SPLIT = BA_OFF + 2 * GDN_HEADS


def _lane_row(v, offset):
    return jnp.zeros((1, LANE), F32).at[0, offset:offset + v.shape[0]].set(v)


_GROUP_VECTORS = {"mixer": (), "attn": ("ln1_g", "ln1_b", "ln2_g", "ln2_b"), "mlp": ("ln3_g", "ln3_b")}


def _group_weights(group, full):
    w = {n: full[n].reshape(1, D_MODEL) for n in _GROUP_VECTORS[group]}
    if group == "mixer":
        w_in = full["w_in"].reshape(IN_COLS, D_MODEL)
        zeros = jnp.zeros((POOL_OFF - _BA_SPLIT, D_MODEL), w_in.dtype)
        w.update({
            "w_in": jnp.concatenate([w_in[:_BA_SPLIT], zeros, w_in[_BA_SPLIT:]], axis=0),
            "conv_w": full["conv_w"],
            "alog_row": _lane_row(full["a_log"], GDN_HEADS),
            "dtb_row": _lane_row(full["dt_bias"], GDN_HEADS),
            "gdn_norm_w": full["gdn_norm_w"].reshape(1, LANE),
            "pool_w": full["pool_w"],
            "pool_scale": full["pool_scale"].reshape(POOL_GROUPS, 1, POOL_GROUP_DIM),
        })
    elif group == "attn":
        w.update({n: full[n] for n in ("w_out", "xq_w", "xk_w", "xv_w", "xo_w")})
    else:
        w.update({n: full[n] for n in ("w_up", "w_down")})
    return w


def _w_in_chunks(g):
    unpadded = jnp.concatenate([g[:_BA_SPLIT], g[POOL_OFF:]], axis=0)
    return unpadded.reshape(N_DEV, IN_COLS // N_DEV, D_MODEL)


def _finish_small_grads(g):
    out = {"conv_w": g["conv_w"]}
    out["a_log"] = g["alog_row"][0, GDN_HEADS:2 * GDN_HEADS]
    out["dt_bias"] = g["dtb_row"][0, GDN_HEADS:2 * GDN_HEADS]
    out["gdn_norm_w"] = g["gdn_norm_w"].reshape(LANE)
    out["pool_scale"] = g["pool_scale"].reshape(POOL_GROUPS * POOL_GROUP_DIM)
    for n in ("ln1_g", "ln1_b", "ln2_g", "ln2_b", "ln3_g", "ln3_b"):
        out[n] = g[n].reshape(D_MODEL)
    return out


def _adamw_math(w, g, m, v):
    m = ADAM_B1 * m + (1.0 - ADAM_B1) * g
    v = ADAM_B2 * v + (1.0 - ADAM_B2) * (g * g)
    m_hat = m / (1.0 - ADAM_B1 ** ADAM_STEP)
    v_hat = v / (1.0 - ADAM_B2 ** ADAM_STEP)
    delta = -ADAM_LR * (m_hat / (jnp.sqrt(v_hat) + ADAM_EPS) + ADAM_WD * w)
    return delta, m, v


def _shard_tile(r, c):
    if r % 128 == 0:
        return 128, c
    return r, 256 if c % 256 == 0 else c


def _adamw_shard(parts, own, me, w, m, v, *, name):
    s, r, c = parts.shape
    tr, tc = _shard_tile(r, c)
    assert r % tr == 0 and c % tc == 0, (name, r, c)

    def body(me_ref, p_ref, own_ref, w_ref, m_ref, v_ref, g_ref, d_ref, nm_ref, nv_ref):
        mine = own_ref[...].astype(F32)
        g = None
        for i in range(s):
            part = jnp.where(me_ref[0] == i, mine, p_ref[i].astype(F32))
            g = part if g is None else g + part
        delta, nm, nv = _adamw_math(w_ref[...], g, m_ref[...], v_ref[...])
        g_ref[...] = g
        d_ref[...] = delta
        nm_ref[...] = nm
        nv_ref[...] = nv

    blk = pl.BlockSpec((tr, tc), lambda i, j, me_ref: (i, j))
    out = jax.ShapeDtypeStruct((r, c), F32)
    return pl.pallas_call(
        body,
        grid_spec=pltpu.PrefetchScalarGridSpec(
            num_scalar_prefetch=1, grid=(r // tr, c // tc),
            in_specs=[pl.BlockSpec((s, tr, tc), lambda i, j, me_ref: (0, i, j)),
                      pl.BlockSpec((None, tr, tc), lambda i, j, me_ref: (me_ref[0], i, j)), blk, blk, blk],
            out_specs=[blk, blk, blk, blk]),
        out_shape=[out, out, out, out], compiler_params=_params("parallel", "parallel"), name=name,
    )(me, parts, own, w, m, v)


N_CHIPS = N_DEV // 2


def _chip_sums(chunks, from_sibling, core, *, name):
    _, r, c = chunks.shape
    tr, tc = _shard_tile(r, c)
    assert r % tr == 0 and c % tc == 0, (name, r, c)

    def body(core_ref, mine_ref, other_ref, o_ref):
        o_ref[...] = (mine_ref[...].astype(F32) + other_ref[...].astype(F32)).astype(o_ref.dtype)

    by_chip = pl.BlockSpec((None, tr, tc), lambda q, i, j, core_ref: (q, i, j))
    return pl.pallas_call(
        body,
        grid_spec=pltpu.PrefetchScalarGridSpec(
            num_scalar_prefetch=1, grid=(N_CHIPS, r // tr, c // tc),
            in_specs=[pl.BlockSpec((None, tr, tc), lambda q, i, j, core_ref: (2 * q + core_ref[0], i, j)), by_chip],
            out_specs=by_chip),
        out_shape=jax.ShapeDtypeStruct((N_CHIPS, r, c), chunks.dtype),
        compiler_params=_params("parallel", "parallel", "parallel"), name=name,
    )(core, chunks, from_sibling)


def _place():
    return lax.axis_index("x"), lax.axis_index("y"), lax.axis_index("c")


def _slot(px, py, pc):
    return 4 * px + 2 * py + pc


_HBM = pl.BlockSpec(memory_space=pltpu.HBM)


_SEM = pl.BlockSpec(memory_space=pltpu.SEMAPHORE)
_ANY = pl.BlockSpec(memory_space=pl.ANY)
_EFFECT = pltpu.SideEffectType.DATAFLOW_SIDE_EFFECTING
_N_PEERS = N_DEV - 1


def _peer(k, x, y, c):
    return (1 - x if k & 4 else x, 1 - y if k & 2 else y, 1 - c if k & 1 else c)


_EXCHANGE_BITS = {"gather_chips": (1, 2, 4, 6), "gather_pass": (2, 4, 6), "scatter_sibling": (1, 1, 1, 1),
                  "scatter_chips": (2, 4, 6)}


def _exchange_copy(mode, src, land, w, i, place, send_sems, recv_sems, receiving):
    bits = _EXCHANGE_BITS[mode]
    k = bits[i]
    peer = _peer(k, *place)
    me = _slot(*place)
    if mode == "gather_chips":
        to, src_ref, sent_to, got_at = peer, src[w], me, _slot(*peer)
    elif mode == "gather_pass":
        blk = _slot(*peer)
        to, src_ref, sent_to, got_at = _peer(1, *place), land[w].at[blk], blk, _slot(*_peer(k | 1, *place))
    elif mode == "scatter_sibling":
        to, src_ref, sent_to, got_at = peer, src[w].at[2 * i + 1 - place[2]], i, i
    else:
        to, src_ref, sent_to, got_at = peer, src[w].at[_slot(*peer) // 2], me // 2, _slot(*peer) // 2
    sem = w * len(bits) + i
    return pltpu.make_async_remote_copy(
        src_ref=src_ref, dst_ref=land[w].at[got_at if receiving else sent_to], send_sem=send_sems.at[sem],
        recv_sem=recv_sems.at[sem], device_id=to, device_id_type=MESH)


def _exchange_start(mode, srcs, lands, after, *, name):
    ns, nl = len(srcs), len(lands)
    n_sem = nl * len(_EXCHANGE_BITS[mode])

    def body(*refs):
        src, land = refs[:ns], refs[ns:ns + nl]
        send_sems, recv_sems = refs[ns + nl + 1:ns + nl + 3]
        token = refs[-1]
        place = _place()
        for w in range(nl):
            for i in range(len(_EXCHANGE_BITS[mode])):
                _exchange_copy(mode, src, land, w, i, place, send_sems, recv_sems, receiving=False).start()
        token[...] = jnp.zeros_like(token)

    sems = pltpu.SemaphoreType.DMA((n_sem,))
    arrays = list(srcs) + list(lands)
    res = pl.pallas_call(
        body, name=name, in_specs=[_HBM] * (ns + nl) + [_ANY],
        out_specs=(_SEM, _SEM, *([_HBM] * (ns + nl)), pl.BlockSpec(memory_space=pltpu.VMEM)),
        out_shape=(sems, sems, *[pltpu.HBM(a.shape, a.dtype) for a in arrays], jax.ShapeDtypeStruct((8, LANE), F32)),
        input_output_aliases={i: 2 + i for i in range(ns + nl)},
        compiler_params=pltpu.CompilerParams(has_side_effects=_EFFECT),
    )(*[pltpu.with_memory_space_constraint(a, pltpu.HBM) for a in arrays], after)
    return res[0], res[1], list(res[2:2 + ns]), list(res[2 + ns:2 + ns + nl]), res[-1]


def _exchange_wait(mode, started, after, *, name):
    send_sems, recv_sems, srcs, lands, _ = started
    ns, nl = len(srcs), len(lands)

    def body(*refs):
        src, land = refs[:ns], refs[ns:ns + nl]
        send_sems, recv_sems = refs[ns + nl:ns + nl + 2]
        place = _place()
        for w in range(nl):
            for i in range(len(_EXCHANGE_BITS[mode])):
                cp = _exchange_copy(mode, src, land, w, i, place, send_sems, recv_sems, receiving=True)
                cp.wait_send()
                cp.wait_recv()

    arrays = list(srcs) + list(lands)
    res = pl.pallas_call(
        body, name=name, in_specs=[_HBM] * (ns + nl) + [_SEM, _SEM, _ANY], out_specs=[_HBM] * (ns + nl),
        out_shape=[pltpu.HBM(a.shape, a.dtype) for a in arrays],
        input_output_aliases={i: i for i in range(ns + nl)},
        compiler_params=pltpu.CompilerParams(has_side_effects=_EFFECT),
    )(*arrays, send_sems, recv_sems, after)
    return list(res[:ns]), list(res[ns:])


def _small_allreduce_adamw(gvec, wvec, mvec, vvec):
    rows, length = gvec.shape

    def body(g_ref, w_ref, m_ref, v_ref, gs_ref, d_ref, nm_ref, nv_ref, slots, send_sems, recv_sems):
        x, y, c = _place()
        me = _slot(x, y, c)
        slots[me] = g_ref[...]
        sends = []
        for k in range(1, N_DEV):
            peer = _peer(k, x, y, c)
            sends.append(pltpu.make_async_remote_copy(
                src_ref=g_ref, dst_ref=slots.at[me], send_sem=send_sems.at[k - 1], recv_sem=recv_sems.at[k - 1],
                device_id=peer, device_id_type=MESH))
        for cp in sends:
            cp.start()
        for k in range(1, N_DEV):
            peer = _peer(k, x, y, c)
            pltpu.make_async_remote_copy(
                src_ref=g_ref, dst_ref=slots.at[_slot(*peer)], send_sem=send_sems.at[k - 1], recv_sem=recv_sems.at[k - 1],
                device_id=peer, device_id_type=MESH).wait_recv()
        for cp in sends:
            cp.wait_send()
        g = slots[0]
        for s in range(1, N_DEV):
            g = g + slots[s]
        delta, nm, nv = _adamw_math(w_ref[...], g, m_ref[...], v_ref[...])
        gs_ref[...] = g
        d_ref[...] = delta
        nm_ref[...] = nm
        nv_ref[...] = nv

    vmem = pl.BlockSpec(memory_space=pltpu.VMEM)
    out = jax.ShapeDtypeStruct((rows, length), F32)
    return pl.pallas_call(
        body, in_specs=[vmem] * 4, out_specs=[vmem] * 4, out_shape=[out] * 4,
        scratch_shapes=[pltpu.VMEM((N_DEV, rows, length), F32), pltpu.SemaphoreType.DMA((N_DEV - 1,)),
                        pltpu.SemaphoreType.DMA((N_DEV - 1,))],
        name="small_allreduce_adamw",
    )(gvec, wvec, mvec, vvec)


_SMALL_SEGMENTS = (("a_log", GDN_HEADS), ("dt_bias", GDN_HEADS), ("gdn_norm_w", HEAD_DIM), ("pool_scale", GDN_WIDTH),
                   ("ln1_g", D_MODEL), ("ln1_b", D_MODEL), ("ln2_g", D_MODEL), ("ln2_b", D_MODEL),
                   ("ln3_g", D_MODEL), ("ln3_b", D_MODEL), ("conv_w", CONV_K * QKV_COLS))
_SMALL_ROWS = 8
_SMALL_LEN = -(-sum(sz for _, sz in _SMALL_SEGMENTS) // (_SMALL_ROWS * LANE)) * LANE


def _pack_small(vals):
    parts = [vals[n].reshape(-1).astype(F32) if n in vals else jnp.zeros((sz,), F32) for n, sz in _SMALL_SEGMENTS]
    flat = jnp.concatenate(parts)
    flat = jnp.pad(flat, (0, _SMALL_ROWS * _SMALL_LEN - flat.shape[0]))
    return flat.reshape(_SMALL_ROWS, _SMALL_LEN)


def _unpack_small(vec):
    flat = vec.reshape(-1)
    out, off = {}, 0
    for n, sz in _SMALL_SEGMENTS:
        out[n] = flat[off:off + sz]
        off += sz
    return out


_WEIGHT_ORDER = ("w_in", "conv_w", "a_log", "dt_bias", "gdn_norm_w", "pool_w", "pool_scale", "w_out", "ln1_g", "ln1_b",
                 "xq_w", "xk_w", "xv_w", "xo_w", "ln2_g", "ln2_b", "w_up", "w_down", "ln3_g", "ln3_b")


def _shard2d(name, a):
    if name == "w_in":
        return a.T
    return a.reshape(-1, a.shape[-1]) if name == "pool_w" else a


def _shard_result(name, r, shape):
    return r.T[None] if name == "w_in" else r.reshape(shape)


def _gathered_to_full(name, gth):
    if name in ("w_up", "w_in"):
        return gth
    if name == "conv_w":
        return jnp.transpose(gth, (1, 0, 2)).reshape(gth.shape[1], N_DEV * gth.shape[2])
    if name == "pool_w":
        g4 = gth.reshape(N_DEV, POOL_GROUPS, POOL_GROUP_DIM // N_DEV, POOL_GROUP_DIM)
        return jnp.transpose(g4, (1, 0, 2, 3)).reshape(POOL_GROUPS, POOL_GROUP_DIM, POOL_GROUP_DIM)
    return gth.reshape(N_DEV * gth.shape[1], gth.shape[2])


def _full_to_chunks(name, full):
    if name == "w_up":
        return full
    if name == "pool_w":
        g4 = full.reshape(POOL_GROUPS, N_DEV, POOL_GROUP_DIM // N_DEV, POOL_GROUP_DIM)
        return jnp.transpose(g4, (1, 0, 2, 3)).reshape(N_DEV, POOL_GROUPS * POOL_GROUP_DIM // N_DEV, POOL_GROUP_DIM)
    return full.reshape(N_DEV, full.shape[0] // N_DEV, full.shape[1])


_GATHER_GROUPS = (("mixer", ("w_in", "conv_w", "pool_w")), ("attn", ("w_out", "xq_w", "xk_w", "xv_w", "xo_w")),
                  ("mlp", ("w_up", "w_down")))


def _grad_chunks(name, g):
    if name == "w_in":
        return _w_in_chunks(g.astype(BF16))
    return _full_to_chunks(name, g.astype(BF16))


def kernel(x, mem, w_in, conv_w, a_log, dt_bias, gdn_norm_w, pool_w, pool_scale, w_out, ln1_g, ln1_b, xq_w, xk_w, xv_w, xo_w, ln2_g, ln2_b, w_up, w_down, ln3_g, ln3_b, loss_target, m_w_in, m_conv_w, m_a_log, m_dt_bias, m_gdn_norm_w, m_pool_w, m_pool_scale, m_w_out, m_ln1_g, m_ln1_b, m_xq_w, m_xk_w, m_xv_w, m_xo_w, m_ln2_g, m_ln2_b, m_w_up, m_w_down, m_ln3_g, m_ln3_b, v_w_in, v_conv_w, v_a_log, v_dt_bias, v_gdn_norm_w, v_pool_w, v_pool_scale, v_w_out, v_ln1_g, v_ln1_b, v_xq_w, v_xk_w, v_xv_w, v_xo_w, v_ln2_g, v_ln2_b, v_w_up, v_w_down, v_ln3_g, v_ln3_b):
    args = dict(locals())
    wt = {n: args[n][0] for n in _WEIGHT_ORDER}
    mo = {n: args["m_" + n][0] for n in _WEIGHT_ORDER}
    vo = {n: args["v_" + n][0] for n in _WEIGHT_ORDER}

    me = _slot(*_place())
    me_arr = jnp.reshape(me, (1,)).astype(jnp.int32)
    nothing = jnp.zeros((8, LANE), F32)

    def landing_zones(names):
        shards = [_shard2d(n, wt[n]).astype(F32 if n == "conv_w" else BF16) for n in names]
        zones = [lax.dynamic_update_slice(lax.empty((N_DEV, *s.shape), s.dtype), s[None], (me, 0, 0)) for s in shards]
        return shards, zones

    chip_arr = jnp.reshape(me // 2, (1,)).astype(jnp.int32)
    core_arr = jnp.reshape(lax.axis_index("c"), (1,)).astype(jnp.int32)
    first, attn_names, mlp_names = (names for _, names in _GATHER_GROUPS)
    gathers = {}

    prepared = {}

    def gather_chips(group, names, after):
        shards, zones = prepared.pop(group) if group in prepared else landing_zones(names)
        gathers[group] = _exchange_start("gather_chips", shards, zones, after, name="gather_chips_" + group)
        return gathers[group][4]

    def gather_pass(group, after):
        _, zones = _exchange_wait("gather_chips", gathers[group], after, name=f"gather_chips_{group}_wait")
        gathers[group] = _exchange_start("gather_pass", [], zones, nothing, name="gather_pass_" + group)
        return gathers[group][4]

    def gathered(group, names, after, token=None):
        _, zones = _exchange_wait("gather_pass", gathers[group], after, name=f"gather_pass_{group}_wait")
        full = {n: _gathered_to_full(n, z) for n, z in zip(names, zones)}
        full.update({n: wt[n] if token is None else wt[n] + token for n in _VECTORS})
        return _group_weights(group, full)

    token = gather_chips("mixer", first, nothing)
    x16 = _cast_bf16(x[0], name="cast_x")
    later = [landing_zones(attn_names), landing_zones(mlp_names)]
    token, x16, later = lax.optimization_barrier((token, x16, later))
    prepared["attn"], prepared["mlp"] = later
    token = gather_chips("attn", attn_names, gather_pass("mixer", token))

    def weights_of(group, after):
        if group == "mixer":
            return gathered(group, first, gathers["attn"][4])
        if group == "ahead":
            return gather_chips("mlp", mlp_names, gather_pass("attn", after))[0:1, 0:1]
        if group == "attn":
            return gathered(group, attn_names, after)
        return gathered(group, mlp_names, gather_pass("mlp", after))

    scatters = {}
    in_flight = []

    def chip_stage(after):
        group, names, started = in_flight.pop()
        chunks, from_sibling = _exchange_wait("scatter_sibling", started, after, name=f"scatter_sibling_{group}_wait")
        sums = [_chip_sums(c, f, core_arr, name=f"chip_sums_{n}") for n, c, f in zip(names, chunks, from_sibling)]
        scatters[group] = (names, _exchange_start("scatter_chips", sums, [lax.empty(s.shape, s.dtype) for s in sums],
                                                  nothing, name="scatter_chips_" + group))
        return scatters[group][1][4]

    def grads_ready(group, grads):
        if group == "tick":
            return chip_stage(grads["after"])[0:1, 0:1] if in_flight else None
        names = tuple(grads)
        chunks = [_grad_chunks(n, grads[n]) for n in names]
        token = chip_stage(chunks[0]) if in_flight else nothing
        zones = [lax.empty((N_CHIPS, *c.shape[1:]), c.dtype) for c in chunks]
        started = _exchange_start("scatter_sibling", chunks, zones, token, name="scatter_sibling_" + group)
        in_flight.append((group, names, started))
        return started[4][0:1, 0:1]

    sq, grad_x, g = _local_step(x[0], x16, mem[0], loss_target[0], weights_of, grads_ready)
    small = _finish_small_grads(g)

    out = {}
    after = chip_stage(grad_x)
    for group, (names, started) in scatters.items():
        sums, lands = _exchange_wait("scatter_chips", started, after, name=f"scatter_chips_{group}_wait")
        for n, parts, own in zip(names, lands, sums):
            res = _adamw_shard(parts, own, chip_arr, _shard2d(n, wt[n]), _shard2d(n, mo[n]), _shard2d(n, vo[n]),
                               name="adamw_" + n)
            out[n] = [_shard_result(n, r, args[n].shape) for r in res]
            after = res[1]

    packed, _ = lax.optimization_barrier((_pack_small(small), after))
    gs, ds, ms, vs = _small_allreduce_adamw(
        packed, _pack_small({n: wt[n] for n in _VECTORS}), _pack_small({n: mo[n] for n in _VECTORS}),
        _pack_small({n: vo[n] for n in _VECTORS}))
    gs, ds, ms, vs = _unpack_small(gs), _unpack_small(ds), _unpack_small(ms), _unpack_small(vs)
    cols = conv_w.shape[-1]
    conv_full = gs["conv_w"].reshape(CONV_K, QKV_COLS)
    conv_mine = lax.dynamic_slice(conv_full, (0, me * cols), (CONV_K, cols))[None]
    res = _adamw_shard(conv_mine, conv_mine, jnp.zeros((1,), jnp.int32), wt["conv_w"], mo["conv_w"], vo["conv_w"],
                       name="adamw_conv_w")
    out["conv_w"] = [r.reshape(conv_w.shape) for r in res]
    for n in _VECTORS:
        out[n] = [t[n].reshape(args[n].shape) for t in (gs, ds, ms, vs)]

    loss = lax.psum(0.5 * sq[0, 0] / D_MODEL, ("x", "y", "c"))
    return (loss, grad_x[None], *[out[n][0] for n in _WEIGHT_ORDER], *[out[n][1] for n in _WEIGHT_ORDER],
            *[out[n][2] for n in _WEIGHT_ORDER], *[out[n][3] for n in _WEIGHT_ORDER])
```

```python
import functools
import math

import jax
import jax.numpy as jnp
from jax import lax
from jax.experimental import pallas as pl
from jax.experimental.pallas import tpu as pltpu

F32 = jnp.float32
BF16 = jnp.bfloat16
MESH = pl.DeviceIdType.MESH

N_DEV = 8
D_MODEL = 2048
GDN_WIDTH = 1024
GDN_HEADS = 8
HEAD_DIM = 128
CONV_K = 4
CHUNK = 64
POOL_GROUPS = 4
POOL_GROUP_DIM = 256
MEM_LEN = 256
XATTN_HEADS = 4
XATTN_HEAD_DIM = 512
D_FF = 8192
IN_COLS = 5136
ALPHA = 2.0 ** 0.25
LN_EPS = 1e-5
NORM_EPS = 1e-6

LANE = 128
QKV_COLS = 3 * GDN_WIDTH
Z_OFF = QKV_COLS
BA_OFF = 4 * GDN_WIDTH
POOL_OFF = BA_OFF + 2 * LANE
PROJ_COLS = POOL_OFF + GDN_WIDTH
Z_BLK = Z_OFF // LANE
BA_BLK = BA_OFF // LANE
POOL_BLK = POOL_OFF // POOL_GROUP_DIM

ADAM_LR = 0.001
ADAM_B1 = 0.9
ADAM_B2 = 0.999
ADAM_EPS = 1e-08
ADAM_WD = 0.01
ADAM_STEP = 10

VMEM_LIMIT_BYTES = 48 * 1024 * 1024


def _params(*sem):
    return pltpu.CompilerParams(dimension_semantics=sem if sem else None, vmem_limit_bytes=VMEM_LIMIT_BYTES)


def _make_dots(cast, precision, batched=False):
    lead = 1 if batched else 0
    batch = ((0,), (0,)) if batched else ((), ())

    def dg(a, b, ca, cb):
        if cast is not None:
            a = a.astype(cast)
            b = b.astype(cast)
        return lax.dot_general(a, b, (((ca + lead,), (cb + lead,)), batch), precision=precision, preferred_element_type=F32)

    def nn_(a, b):
        return dg(a, b, 1, 0)

    def nt_(a, b):
        return dg(a, b, 1, 1)

    def tn_(a, b):
        return dg(a, b, 0, 0)

    @jax.custom_vjp
    def nn(a, b):
        return nn_(a, b)

    nn.defvjp(lambda a, b: (nn_(a, b), (a, b)), lambda r, g: (nt_(g, r[1]), tn_(r[0], g)))

    @jax.custom_vjp
    def nt(a, b):
        return nt_(a, b)

    nt.defvjp(lambda a, b: (nt_(a, b), (a, b)), lambda r, g: (nn_(g, r[1]), tn_(g, r[0])))

    @jax.custom_vjp
    def tn(a, b):
        return tn_(a, b)

    tn.defvjp(lambda a, b: (tn_(a, b), (a, b)), lambda r, g: (nt_(r[1], g), nn_(r[0], g)))

    return (nn_, nt_, tn_), (nn, nt, tn)


_BDOT_PLAIN, _BDOT_VJP = _make_dots(BF16, None)
_BDOT_BATCH_PLAIN, _BDOT_BATCH_VJP = _make_dots(BF16, None, batched=True)
_FDOT_BATCH_PLAIN, _FDOT_BATCH_VJP = _make_dots(None, lax.Precision.HIGH, batched=True)


def _mm(a, b, *, ta=False, tb=False, out_dtype=F32, tm=None, tn=512, tk=None, epi=None, extra=None, add_scale=1.0,
        b_chunks=False, o_chunks=False, name):
    m, k = (a.shape[1], a.shape[0]) if ta else a.shape
    if b_chunks:
        n, kb = (b.shape[1], N_DEV * b.shape[2]) if tb else (N_DEV * b.shape[2], b.shape[1])
    else:
        n, kb = b.shape if tb else (b.shape[1], b.shape[0])
    assert kb == k, (name, a.shape, b.shape)
    tm, tn, tk = min(tm or m, m), min(tn, n), min(tk or k, k)
    assert m % tm == 0 and n % tn == 0 and k % tk == 0, (name, m, n, k)
    nk = k // tk
    dims = (((0 if ta else 1,), (1 if tb else 0,)), ((), ()))
    n_extra = 0 if epi in (None, "relu2") else 1
    n_out = 2 if epi == "relu2" else 1
    if epi in ("relu2", "mul2r"):
        out_dtype = BF16

    def body(*refs):
        a_ref, b_ref = refs[:2]
        c_ref = refs[2] if n_extra else None
        o_refs = refs[2 + n_extra:2 + n_extra + n_out]
        scr = refs[2 + n_extra + n_out:]
        r = lax.dot_general(a_ref[...].astype(BF16), b_ref[...].astype(BF16), dims, preferred_element_type=F32)

        def finish(v):
            if epi == "add":
                o_refs[0][...] = (v + add_scale * c_ref[...]).astype(out_dtype)
            elif epi == "relu2":
                p = jnp.maximum(v, 0.0)
                o_refs[0][...] = (p * p).astype(BF16)
                o_refs[1][...] = p.astype(BF16)
            elif epi == "mul2r":
                o_refs[0][...] = (v * (2.0 * c_ref[...].astype(F32))).astype(BF16)
            else:
                o_refs[0][...] = v.astype(out_dtype)

        if nk == 1:
            finish(r)
        else:
            acc = scr[0]
            kk = pl.program_id(2)

            @pl.when(kk == 0)
            def _():
                acc[...] = r

            @pl.when(kk > 0)
            def _():
                acc[...] += r

            @pl.when(kk == nk - 1)
            def _():
                finish(acc[...])

    a_spec = pl.BlockSpec((tk, tm), lambda i, j, kk: (kk, i)) if ta else pl.BlockSpec((tm, tk), lambda i, j, kk: (i, kk))
    if b_chunks and tb:
        kc = k // N_DEV // tk
        b_spec = pl.BlockSpec((None, tn, tk), lambda i, j, kk: (kk // kc, j, kk % kc))
    elif b_chunks:
        nc = n // N_DEV // tn
        b_spec = pl.BlockSpec((None, tk, tn), lambda i, j, kk: (j // nc, kk, j % nc))
    elif tb:
        b_spec = pl.BlockSpec((tn, tk), lambda i, j, kk: (j, kk))
    else:
        b_spec = pl.BlockSpec((tk, tn), lambda i, j, kk: (kk, j))
    mn_spec = pl.BlockSpec((tm, tn), lambda i, j, kk: (i, j))
    if o_chunks:
        oc = n // N_DEV // tn
        o_spec = pl.BlockSpec((None, tm, tn), lambda i, j, kk: (j // oc, i, j % oc))
        o_shape = jax.ShapeDtypeStruct((N_DEV, m, n // N_DEV), out_dtype)
    else:
        o_spec, o_shape = mn_spec, jax.ShapeDtypeStruct((m, n), out_dtype)
    res = pl.pallas_call(
        body, grid=(m // tm, n // tn, nk), in_specs=[a_spec, b_spec] + [mn_spec] * n_extra,
        out_specs=[o_spec] * n_out, out_shape=[o_shape] * n_out,
        scratch_shapes=[pltpu.VMEM((tm, tn), F32)] if nk > 1 else [],
        compiler_params=_params("parallel", "parallel", "arbitrary"), name=name,
    )(a, b, *([extra] if n_extra else []))
    return res if n_out > 1 else res[0]


def _cast_bf16(v, *, name, tm=512):
    t, d = v.shape
    tm = min(tm, t)

    def body(v_ref, o_ref):
        o_ref[...] = v_ref[...].astype(BF16)

    spec = pl.BlockSpec((tm, d), lambda i: (i, 0))
    return pl.pallas_call(body, grid=(t // tm,), in_specs=[spec], out_specs=spec,
                          out_shape=jax.ShapeDtypeStruct((t, d), BF16), compiler_params=_params("parallel"), name=name)(v)


def _shift_down(v, s):
    if s == 0:
        return v
    row = lax.broadcasted_iota(jnp.int32, v.shape, 0)
    return jnp.where(row >= s, pltpu.roll(v, s, axis=0), 0.0)


def _shift_up(v, s):
    if s == 0:
        return v
    t = v.shape[0]
    row = lax.broadcasted_iota(jnp.int32, v.shape, 0)
    return jnp.where(row < t - s, pltpu.roll(v, t - s, axis=0), 0.0)


def _post_col(j):
    return (j % GDN_HEADS) * 3 + j // GDN_HEADS


def _gdn_prep_fwd(proj, conv_w):
    t = proj.shape[0]

    def body(x_ref, w_ref, o_ref):
        j = pl.program_id(0)
        x = x_ref[...]
        y = jnp.zeros_like(x)
        for tap in range(CONV_K):
            y = y + w_ref[tap:tap + 1, :] * _shift_down(x, CONV_K - 1 - tap)
        c = y * jax.nn.sigmoid(y)
        nrm = c * lax.rsqrt(jnp.sum(c * c, axis=1, keepdims=True) + NORM_EPS)
        o_ref[...] = jnp.where(j < 2 * GDN_HEADS, nrm, c)

    return pl.pallas_call(
        body, grid=(QKV_COLS // LANE,),
        in_specs=[pl.BlockSpec((t, LANE), lambda j: (0, j)), pl.BlockSpec((CONV_K, LANE), lambda j: (0, j))],
        out_specs=pl.BlockSpec((t, LANE), lambda j: (0, _post_col(j))),
        out_shape=jax.ShapeDtypeStruct((t, QKV_COLS), F32),
        compiler_params=_params("parallel"), name="gdn_prep_fwd",
    )(proj, conv_w)


def _gdn_prep_bwd(proj, conv_w, dpost, dproj):
    t = proj.shape[0]

    def body(x_ref, w_ref, d_ref, _, dx_ref, dw_ref):
        j = pl.program_id(0)
        x = x_ref[...]
        xs = [_shift_down(x, CONV_K - 1 - tap) for tap in range(CONV_K)]
        y = jnp.zeros_like(x)
        for tap in range(CONV_K):
            y = y + w_ref[tap:tap + 1, :] * xs[tap]
        sig = jax.nn.sigmoid(y)
        c = y * sig
        r = lax.rsqrt(jnp.sum(c * c, axis=1, keepdims=True) + NORM_EPS)
        nrm = c * r
        d = d_ref[...]
        dc_norm = r * (d - nrm * jnp.sum(d * nrm, axis=1, keepdims=True))
        dc = jnp.where(j < 2 * GDN_HEADS, dc_norm, d)
        dy = dc * (sig * (1.0 + y * (1.0 - sig)))
        dx = jnp.zeros_like(x)
        for tap in range(CONV_K):
            dx = dx + _shift_up(w_ref[tap:tap + 1, :] * dy, CONV_K - 1 - tap)
            dw_ref[tap:tap + 1, :] = jnp.sum(dy * xs[tap], axis=0, keepdims=True)
        dx_ref[...] = dx.astype(dx_ref.dtype)

    return pl.pallas_call(
        body, grid=(QKV_COLS // LANE,),
        in_specs=[pl.BlockSpec((t, LANE), lambda j: (0, j)), pl.BlockSpec((CONV_K, LANE), lambda j: (0, j)),
                  pl.BlockSpec((t, LANE), lambda j: (0, _post_col(j))), pl.BlockSpec(memory_space=pl.ANY)],
        out_specs=[pl.BlockSpec((t, LANE), lambda j: (0, j)), pl.BlockSpec((CONV_K, LANE), lambda j: (0, j))],
        out_shape=[jax.ShapeDtypeStruct(dproj.shape, dproj.dtype), jax.ShapeDtypeStruct((CONV_K, QKV_COLS), F32)],
        input_output_aliases={3: 0},
        compiler_params=_params("parallel"), name="gdn_prep_bwd",
    )(proj, conv_w, dpost, dproj)


def _softplus(v):
    return jnp.maximum(v, 0.0) + jnp.log(1.0 + jnp.exp(-jnp.abs(v)))


def _tri_inv(low, nn):
    r = lax.broadcasted_iota(jnp.int32, (CHUNK, CHUNK), 0)
    c = lax.broadcasted_iota(jnp.int32, (CHUNK, CHUNK), 1)
    eye = (r == c).astype(F32)
    same_blk = lax.shift_right_logical(r, 4) == lax.shift_right_logical(c, 4)
    diag = jnp.where(same_blk, low, 0.0)
    off = low - diag
    n1 = -diag
    n2 = nn(n1, n1)
    n4 = nn(n2, n2)
    n8 = nn(n4, n4)
    inv_d = nn(nn(nn(eye + n1, eye + n2), eye + n4), eye + n8)
    m1 = nn(inv_d, off)
    m2 = nn(m1, m1)
    return nn(nn(eye - m1, eye + m2), inv_d)


LOCAL_HEADS_PER_STEP = 8


def _gdn_local_fn(qkv, ba, alog_row, dtb_row, first_head, bdots, fdots):
    nn, nt, tn = bdots
    fnn = fdots[0]
    n_heads = qkv.shape[1] // (3 * HEAD_DIM)
    part = lambda i, p: qkv[:, (3 * i + p) * HEAD_DIM:(3 * i + p + 1) * HEAD_DIM]
    q = jnp.stack([part(i, 0) for i in range(n_heads)]) * (HEAD_DIM ** -0.5)
    k = jnp.stack([part(i, 1) for i in range(n_heads)])
    v = jnp.stack([part(i, 2) for i in range(n_heads)])
    lane = lax.broadcasted_iota(jnp.int32, ba.shape, 1)
    bg = jnp.where(lane < GDN_HEADS, jax.nn.sigmoid(ba), -jnp.exp(alog_row) * _softplus(ba + dtb_row))
    pick = lambda l: jnp.sum(jnp.where(lane == l, bg, 0.0), axis=1, keepdims=True)
    beta = jnp.stack([pick(first_head + i) for i in range(n_heads)])
    g = jnp.stack([pick(first_head + i + GDN_HEADS) for i in range(n_heads)])

    r = lax.broadcasted_iota(jnp.int32, (CHUNK, CHUNK), 0)
    c = lax.broadcasted_iota(jnp.int32, (CHUNK, CHUNK), 1)
    incl = r >= c
    strict = r > c
    eye = r == c

    def to_row(col):
        return jnp.sum(jnp.where(eye, col, 0.0), axis=1, keepdims=True)

    gc = jnp.sum(jnp.where(incl, to_row(g), 0.0), axis=2, keepdims=True)
    diff = gc - to_row(gc)
    decay = jnp.where(incl, jnp.exp(jnp.where(incl, diff, 0.0)), 0.0)
    k_beta = k * beta
    v_beta = v * beta
    low = jnp.where(strict, nt(k_beta, k) * decay, 0.0)
    t_inv = _tri_inv(low, fnn)
    eg = jnp.exp(gc)
    u = fnn(t_inv, v_beta)
    w = fnn(t_inv, k_beta * eg)
    attn = jnp.where(incl, nt(q, k) * decay, 0.0)
    last = lax.broadcasted_iota(jnp.int32, (CHUNK, 1), 0) == CHUNK - 1
    g_last = jnp.sum(jnp.where(last, gc, 0.0), axis=1, keepdims=True)
    kdec = k * jnp.exp(g_last - gc)
    elast = jnp.broadcast_to(jnp.exp(g_last), (n_heads, 1, LANE))
    return u, w, q * eg, kdec, attn, elast


def _gdn_state_fn(u, w, qg, kdec, attn, elast, state, bdots):
    nn, _, tn = bdots
    v_new = u - nn(w, state)
    o = nn(qg, state) + nn(attn, v_new)
    return o, state * elast + tn(kdec, v_new)


def _gdn_local_fwd(post, proj, alog_row, dtb_row):
    t = post.shape[0]
    n_chunks = t // CHUNK
    hb = LOCAL_HEADS_PER_STEP

    def body(qkv_ref, ba_ref, al_ref, dt_ref, u_ref, w_ref, qg_ref, kd_ref, at_ref, el_ref):
        u, w, qg, kdec, attn, elast = _gdn_local_fn(qkv_ref[...], ba_ref[...], al_ref[...], dt_ref[...],
                                                    pl.program_id(1) * hb, _BDOT_BATCH_PLAIN, _FDOT_BATCH_PLAIN)
        for i in range(hb):
            cols = slice(i * HEAD_DIM, (i + 1) * HEAD_DIM)
            u_ref[:, cols] = u[i]
            w_ref[:, cols] = w[i].astype(BF16)
            qg_ref[:, cols] = qg[i].astype(BF16)
            kd_ref[:, cols] = kdec[i].astype(BF16)
        at_ref[...] = attn.astype(BF16)
        el_ref[:, 0] = elast

    wide = pl.BlockSpec((CHUNK, hb * HEAD_DIM), lambda n, j: (n, j))
    row = pl.BlockSpec((1, LANE), lambda n, j: (0, 0))
    return pl.pallas_call(
        body, grid=(n_chunks, GDN_HEADS // hb),
        in_specs=[pl.BlockSpec((CHUNK, hb * 3 * HEAD_DIM), lambda n, j: (n, j)),
                  pl.BlockSpec((CHUNK, LANE), lambda n, j: (n, BA_BLK)), row, row],
        out_specs=[wide, wide, wide, wide, pl.BlockSpec((hb, CHUNK, CHUNK), lambda n, j: (j, n, 0)),
                   pl.BlockSpec((hb, 1, 1, LANE), lambda n, j: (j, n, 0, 0))],
        out_shape=[jax.ShapeDtypeStruct((t, GDN_WIDTH), F32), jax.ShapeDtypeStruct((t, GDN_WIDTH), BF16),
                   jax.ShapeDtypeStruct((t, GDN_WIDTH), BF16), jax.ShapeDtypeStruct((t, GDN_WIDTH), BF16),
                   jax.ShapeDtypeStruct((GDN_HEADS, t, CHUNK), BF16),
                   jax.ShapeDtypeStruct((GDN_HEADS, n_chunks, 1, LANE), F32)],
        compiler_params=_params("parallel", "parallel"), name="gdn_local_fwd",
    )(post, proj, alog_row, dtb_row)


def _gdn_state_specs(n_of):
    wide = pl.BlockSpec((CHUNK, GDN_WIDTH), lambda n: (n_of(n), 0))
    attn = pl.BlockSpec((GDN_HEADS, CHUNK, CHUNK), lambda n: (0, n_of(n), 0))
    elast = pl.BlockSpec((GDN_HEADS, 1, 1, LANE), lambda n: (0, n_of(n), 0, 0))
    saved = pl.BlockSpec((GDN_HEADS, 1, HEAD_DIM, HEAD_DIM), lambda n: (0, n_of(n), 0, 0))
    return wide, attn, elast, saved


def _gdn_state_fwd(u, w, qg, kdec, attn, elast):
    t = u.shape[0]
    n_chunks = t // CHUNK

    def body(u_ref, w_ref, qg_ref, kd_ref, at_ref, el_ref, o_ref, save_ref, state_ref):
        @pl.when(pl.program_id(0) == 0)
        def _():
            state_ref[...] = jnp.zeros_like(state_ref)

        for h in range(GDN_HEADS):
            cols = slice(h * HEAD_DIM, (h + 1) * HEAD_DIM)
            state = state_ref[h]
            save_ref[h, 0] = state
            o, new_state = _gdn_state_fn(u_ref[:, cols], w_ref[:, cols], qg_ref[:, cols], kd_ref[:, cols], at_ref[h],
                                         el_ref[h, 0], state, _BDOT_PLAIN)
            o_ref[:, cols] = o
            state_ref[h] = new_state

    wide, attn_spec, elast_spec, saved_spec = _gdn_state_specs(lambda n: n)
    return pl.pallas_call(
        body, grid=(n_chunks,), in_specs=[wide, wide, wide, wide, attn_spec, elast_spec],
        out_specs=[wide, saved_spec],
        out_shape=[jax.ShapeDtypeStruct((t, GDN_WIDTH), F32),
                   jax.ShapeDtypeStruct((GDN_HEADS, n_chunks, HEAD_DIM, HEAD_DIM), F32)],
        scratch_shapes=[pltpu.VMEM((GDN_HEADS, HEAD_DIM, HEAD_DIM), F32)],
        compiler_params=_params("arbitrary"), name="gdn_state_fwd",
    )(u, w, qg, kdec, attn, elast)


def _gdn_state_bwd(u, w, qg, kdec, attn, elast, saved, do):
    t = u.shape[0]
    n_chunks = t // CHUNK
    last = n_chunks - 1

    def body(u_ref, w_ref, qg_ref, kd_ref, at_ref, el_ref, save_ref, do_ref,
             du_ref, dw_ref, dqg_ref, dkd_ref, dat_ref, del_ref, dstate_ref):
        @pl.when(pl.program_id(0) == 0)
        def _():
            dstate_ref[...] = jnp.zeros_like(dstate_ref)

        for h in range(GDN_HEADS):
            cols = slice(h * HEAD_DIM, (h + 1) * HEAD_DIM)
            _, vjp = jax.vjp(
                lambda *a: _gdn_state_fn(*a, _BDOT_VJP), u_ref[:, cols], w_ref[:, cols].astype(F32),
                qg_ref[:, cols].astype(F32), kd_ref[:, cols].astype(F32), at_ref[h].astype(F32), el_ref[h, 0], save_ref[h, 0])
            du, dw, dqg, dkd, dat, de, dstate = vjp((do_ref[:, cols], dstate_ref[h]))
            du_ref[:, cols] = du
            dw_ref[:, cols] = dw
            dqg_ref[:, cols] = dqg
            dkd_ref[:, cols] = dkd
            dat_ref[h] = dat
            del_ref[h, 0] = de
            dstate_ref[h] = dstate

    wide, attn_spec, elast_spec, saved_spec = _gdn_state_specs(lambda n: last - n)
    wide_f32 = jax.ShapeDtypeStruct((t, GDN_WIDTH), F32)
    return pl.pallas_call(
        body, grid=(n_chunks,), in_specs=[wide, wide, wide, wide, attn_spec, elast_spec, saved_spec, wide],
        out_specs=[wide, wide, wide, wide, attn_spec, elast_spec],
        out_shape=[wide_f32, wide_f32, wide_f32, wide_f32, jax.ShapeDtypeStruct((GDN_HEADS, t, CHUNK), F32),
                   jax.ShapeDtypeStruct((GDN_HEADS, n_chunks, 1, LANE), F32)],
        scratch_shapes=[pltpu.VMEM((GDN_HEADS, HEAD_DIM, HEAD_DIM), F32)],
        compiler_params=_params("arbitrary"), name="gdn_state_bwd",
    )(u, w, qg, kdec, attn, elast, saved, do)


def _gdn_local_bwd(post, proj, alog_row, dtb_row, cots, dproj):
    t = post.shape[0]
    n_chunks = t // CHUNK
    hb = LOCAL_HEADS_PER_STEP
    n_steps = GDN_HEADS // hb

    def body(qkv_ref, ba_ref, al_ref, dt_ref, du_ref, dw_ref, dqg_ref, dkd_ref, dat_ref, del_ref, _,
             dqkv_ref, dba_ref, dal_ref, ddt_ref, dba_acc):
        n = pl.program_id(0)
        j = pl.program_id(1)

        @pl.when((n == 0) & (j == 0))
        def _():
            dal_ref[...] = jnp.zeros_like(dal_ref)
            ddt_ref[...] = jnp.zeros_like(ddt_ref)

        @pl.when(j == 0)
        def _():
            dba_acc[...] = jnp.zeros_like(dba_acc)

        heads = lambda ref: jnp.stack([ref[:, i * HEAD_DIM:(i + 1) * HEAD_DIM] for i in range(hb)])
        _, vjp = jax.vjp(lambda a, b, c, d: _gdn_local_fn(a, b, c, d, j * hb, _BDOT_BATCH_VJP, _FDOT_BATCH_VJP),
                         qkv_ref[...], ba_ref[...], al_ref[...], dt_ref[...])
        dqkv, dba, dal, ddt = vjp((heads(du_ref), heads(dw_ref), heads(dqg_ref), heads(dkd_ref), dat_ref[...],
                                   del_ref[:, 0]))
        dqkv_ref[...] = dqkv
        dba_acc[...] += dba
        dal_ref[...] += dal
        ddt_ref[...] += ddt

        @pl.when(j == n_steps - 1)
        def _():
            dba_ref[:, 0:LANE] = dba_acc[...].astype(dba_ref.dtype)
            dba_ref[:, LANE:2 * LANE] = jnp.zeros((CHUNK, LANE), dba_ref.dtype)

    wide = pl.BlockSpec((CHUNK, hb * HEAD_DIM), lambda n, j: (n, j))
    qkv_spec = pl.BlockSpec((CHUNK, hb * 3 * HEAD_DIM), lambda n, j: (n, j))
    row = pl.BlockSpec((1, LANE), lambda n, j: (0, 0))
    return pl.pallas_call(
        body, grid=(n_chunks, n_steps),
        in_specs=[qkv_spec, pl.BlockSpec((CHUNK, LANE), lambda n, j: (n, BA_BLK)), row, row, wide, wide, wide, wide,
                  pl.BlockSpec((hb, CHUNK, CHUNK), lambda n, j: (j, n, 0)),
                  pl.BlockSpec((hb, 1, 1, LANE), lambda n, j: (j, n, 0, 0)), pl.BlockSpec(memory_space=pl.ANY)],
        out_specs=[qkv_spec, pl.BlockSpec((CHUNK, 2 * LANE), lambda n, j: (n, BA_BLK // 2)), row, row],
        out_shape=[jax.ShapeDtypeStruct((t, QKV_COLS), F32), jax.ShapeDtypeStruct(dproj.shape, dproj.dtype),
                   jax.ShapeDtypeStruct((1, LANE), F32), jax.ShapeDtypeStruct((1, LANE), F32)],
        input_output_aliases={10: 1},
        scratch_shapes=[pltpu.VMEM((CHUNK, LANE), F32)],
        compiler_params=_params("arbitrary", "arbitrary"), name="gdn_local_bwd",
    )(post, proj, alog_row, dtb_row, *cots, dproj)


def _onorm_fn(o, z, w):
    return o * lax.rsqrt(jnp.mean(o * o, axis=1, keepdims=True) + NORM_EPS) * w * (z * jax.nn.sigmoid(z))


def _onorm_fwd(o_raw, proj, norm_w, mixin, tm=512):
    t = o_raw.shape[0]
    tm = min(tm, t)

    def body(o_ref, z_ref, w_ref, _, out_ref):
        out_ref[...] = _onorm_fn(o_ref[...], z_ref[...], w_ref[...]).astype(out_ref.dtype)

    return pl.pallas_call(
        body, grid=(t // tm, GDN_HEADS),
        in_specs=[pl.BlockSpec((tm, LANE), lambda i, h: (i, h)), pl.BlockSpec((tm, LANE), lambda i, h: (i, Z_BLK + h)),
                  pl.BlockSpec((1, LANE), lambda i, h: (0, 0)), pl.BlockSpec(memory_space=pl.ANY)],
        out_specs=pl.BlockSpec((tm, LANE), lambda i, h: (i, h)),
        out_shape=jax.ShapeDtypeStruct(mixin.shape, mixin.dtype), input_output_aliases={3: 0},
        compiler_params=_params("parallel", "parallel"), name="gdn_onorm_fwd",
    )(o_raw, proj, norm_w, mixin)


def _onorm_bwd(o_raw, proj, norm_w, dmixin, dproj, tm=512):
    t = o_raw.shape[0]
    tm = min(tm, t)

    def body(o_ref, z_ref, w_ref, d_ref, _, do_ref, dz_ref, dw_ref):
        @pl.when((pl.program_id(0) == 0) & (pl.program_id(1) == 0))
        def _():
            dw_ref[...] = jnp.zeros_like(dw_ref)

        _, vjp = jax.vjp(_onorm_fn, o_ref[...], z_ref[...], w_ref[...])
        do, dz, dw = vjp(d_ref[...])
        do_ref[...] = do
        dz_ref[...] = dz.astype(dz_ref.dtype)
        dw_ref[...] += dw

    return pl.pallas_call(
        body, grid=(t // tm, GDN_HEADS),
        in_specs=[pl.BlockSpec((tm, LANE), lambda i, h: (i, h)), pl.BlockSpec((tm, LANE), lambda i, h: (i, Z_BLK + h)),
                  pl.BlockSpec((1, LANE), lambda i, h: (0, 0)), pl.BlockSpec((tm, LANE), lambda i, h: (i, h)),
                  pl.BlockSpec(memory_space=pl.ANY)],
        out_specs=[pl.BlockSpec((tm, LANE), lambda i, h: (i, h)), pl.BlockSpec((tm, LANE), lambda i, h: (i, Z_BLK + h)),
                   pl.BlockSpec((1, LANE), lambda i, h: (0, 0))],
        out_shape=[jax.ShapeDtypeStruct((t, GDN_WIDTH), F32), jax.ShapeDtypeStruct(dproj.shape, dproj.dtype),
                   jax.ShapeDtypeStruct((1, LANE), F32)],
        input_output_aliases={4: 1},
        compiler_params=_params("arbitrary", "arbitrary"), name="gdn_onorm_bwd",
    )(o_raw, proj, norm_w, dmixin, dproj)


def _pool_select(levels, gi):
    out = levels[-1]
    for lvl in range(len(levels) - 2, -1, -1):
        out = jnp.where(gi == lvl, levels[lvl], out)
    return out


def _pool_count(shape, gi):
    pos = lax.broadcasted_iota(jnp.int32, shape, 0)
    win = lax.shift_left(jnp.int32(2), gi)
    return jnp.minimum(pos + 1, win).astype(F32)


def _pooled(p, gi):
    acc = p
    levels = []
    for lvl in range(POOL_GROUPS):
        acc = acc + _shift_down(acc, 1 << lvl)
        levels.append(acc)
    return _pool_select(levels, gi) / _pool_count(p.shape, gi) - p


def _pool_fwd(proj, pool_w, pool_scale):
    t = proj.shape[0]

    def body(p_ref, w_ref, s_ref, out_ref):
        gi = pl.program_id(0)
        pooled = _pooled(p_ref[...], gi)
        out_ref[...] = (_BDOT_PLAIN[0](pooled, w_ref[0]) * s_ref[0]).astype(out_ref.dtype)

    return pl.pallas_call(
        body, grid=(POOL_GROUPS,),
        in_specs=[pl.BlockSpec((t, POOL_GROUP_DIM), lambda g: (0, POOL_BLK + g)),
                  pl.BlockSpec((1, POOL_GROUP_DIM, POOL_GROUP_DIM), lambda g: (g, 0, 0)),
                  pl.BlockSpec((1, 1, POOL_GROUP_DIM), lambda g: (g, 0, 0))],
        out_specs=pl.BlockSpec((t, POOL_GROUP_DIM), lambda g: (0, GDN_WIDTH // POOL_GROUP_DIM + g)),
        out_shape=jax.ShapeDtypeStruct((t, 2 * GDN_WIDTH), BF16),
        compiler_params=_params("parallel"), name="pool_fwd",
    )(proj, pool_w, pool_scale)


def _pool_bwd(proj, pool_w, pool_scale, dmixin):
    t = proj.shape[0]
    nn, nt, tn = _BDOT_PLAIN

    def body(p_ref, w_ref, s_ref, d_ref, dp_ref, dw_ref, ds_ref):
        gi = pl.program_id(0)
        p = p_ref[...]
        pooled = _pooled(p, gi)
        mixed = nn(pooled, w_ref[0])
        d = d_ref[...]
        ds_ref[0] = jnp.sum(d * mixed, axis=0, keepdims=True)
        dmixed = d * s_ref[0]
        dw_ref[0] = tn(pooled, dmixed)
        dpooled = nt(dmixed, w_ref[0])
        acc = dpooled / _pool_count(p.shape, gi)
        levels = []
        for lvl in range(POOL_GROUPS):
            acc = acc + _shift_up(acc, 1 << lvl)
            levels.append(acc)
        dp_ref[...] = (_pool_select(levels, gi) - dpooled).astype(dp_ref.dtype)

    return pl.pallas_call(
        body, grid=(POOL_GROUPS,),
        in_specs=[pl.BlockSpec((t, POOL_GROUP_DIM), lambda g: (0, POOL_BLK + g)),
                  pl.BlockSpec((1, POOL_GROUP_DIM, POOL_GROUP_DIM), lambda g: (g, 0, 0)),
                  pl.BlockSpec((1, 1, POOL_GROUP_DIM), lambda g: (g, 0, 0)),
                  pl.BlockSpec((t, POOL_GROUP_DIM), lambda g: (0, GDN_WIDTH // POOL_GROUP_DIM + g))],
        out_specs=[pl.BlockSpec((t, POOL_GROUP_DIM), lambda g: (0, POOL_BLK + g)),
                   pl.BlockSpec((1, POOL_GROUP_DIM, POOL_GROUP_DIM), lambda g: (g, 0, 0)),
                   pl.BlockSpec((1, 1, POOL_GROUP_DIM), lambda g: (g, 0, 0))],
        out_shape=[jax.ShapeDtypeStruct((t, PROJ_COLS), BF16),
                   jax.ShapeDtypeStruct((POOL_GROUPS, POOL_GROUP_DIM, POOL_GROUP_DIM), F32),
                   jax.ShapeDtypeStruct((POOL_GROUPS, 1, POOL_GROUP_DIM), F32)],
        compiler_params=_params("parallel"), name="pool_bwd",
    )(proj, pool_w, pool_scale, dmixin)


def _ln_stats(s):
    mu = jnp.mean(s, axis=1, keepdims=True)
    xc = s - mu
    var = jnp.mean(xc * xc, axis=1, keepdims=True)
    rstd = lax.rsqrt(var + LN_EPS)
    return xc * rstd, rstd


def _ln_fwd(h_in, y, g, b, *, name, tm=256):
    t, d = h_in.shape
    tm = min(tm, t)

    def body(h_ref, y_ref, g_ref, b_ref, o_ref, o16_ref):
        xhat, _ = _ln_stats(ALPHA * h_ref[...] + y_ref[...])
        out = xhat * g_ref[...] + b_ref[...]
        o_ref[...] = out
        o16_ref[...] = out.astype(BF16)

    row = pl.BlockSpec((tm, d), lambda i: (i, 0))
    vec = pl.BlockSpec((1, d), lambda i: (0, 0))
    return pl.pallas_call(
        body, grid=(t // tm,), in_specs=[row, row, vec, vec], out_specs=[row, row],
        out_shape=[jax.ShapeDtypeStruct((t, d), F32), jax.ShapeDtypeStruct((t, d), BF16)],
        compiler_params=_params("parallel"), name=name,
    )(h_in, y, g, b)


def _ln_loss_fwd(h_in, y, g, b, target, *, name, tm=256):
    t, d = h_in.shape
    tm = min(tm, t)

    def body(h_ref, y_ref, g_ref, b_ref, t_ref, dy_ref, sq_ref):
        @pl.when(pl.program_id(0) == 0)
        def _():
            sq_ref[...] = jnp.zeros_like(sq_ref)

        xhat, _ = _ln_stats(ALPHA * h_ref[...] + y_ref[...])
        err = xhat * g_ref[...] + b_ref[...] - t_ref[...]
        dy_ref[...] = err * (1.0 / d)
        sq_ref[...] += jnp.sum(jnp.sum(err * err, axis=1, keepdims=True), axis=0, keepdims=True)

    row = pl.BlockSpec((tm, d), lambda i: (i, 0))
    vec = pl.BlockSpec((1, d), lambda i: (0, 0))
    return pl.pallas_call(
        body, grid=(t // tm,), in_specs=[row, row, vec, vec, row],
        out_specs=[row, pl.BlockSpec((1, LANE), lambda i: (0, 0))],
        out_shape=[jax.ShapeDtypeStruct((t, d), F32), jax.ShapeDtypeStruct((1, LANE), F32)],
        compiler_params=_params("arbitrary"), name=name,
    )(h_in, y, g, b, target)


def _ln_bwd(h_in, y, g, d_a, d_b, *, name, tm=256):
    t, d = h_in.shape
    tm = min(tm, t)
    has_b = d_b is not None

    def body(*refs):
        if has_b:
            h_ref, y_ref, g_ref, da_ref, db_ref, ds_ref, ds16_ref, dg_ref, dbias_ref = refs
        else:
            h_ref, y_ref, g_ref, da_ref, ds_ref, ds16_ref, dg_ref, dbias_ref = refs

        @pl.when(pl.program_id(0) == 0)
        def _():
            dg_ref[...] = jnp.zeros_like(dg_ref)
            dbias_ref[...] = jnp.zeros_like(dbias_ref)

        xhat, rstd = _ln_stats(ALPHA * h_ref[...] + y_ref[...])
        dout = da_ref[...]
        if has_b:
            dout = dout + ALPHA * db_ref[...]
        dxhat = dout * g_ref[...]
        m1 = jnp.mean(dxhat, axis=1, keepdims=True)
        m2 = jnp.mean(dxhat * xhat, axis=1, keepdims=True)
        ds = rstd * (dxhat - m1 - xhat * m2)
        ds_ref[...] = ds
        ds16_ref[...] = ds.astype(BF16)
        dg_ref[...] += jnp.sum(dout * xhat, axis=0, keepdims=True)
        dbias_ref[...] += jnp.sum(dout, axis=0, keepdims=True)

    row = pl.BlockSpec((tm, d), lambda i: (i, 0))
    vec = pl.BlockSpec((1, d), lambda i: (0, 0))
    args = [h_in, y, g, d_a] + ([d_b] if has_b else [])
    return pl.pallas_call(
        body, grid=(t // tm,), in_specs=[row, row, vec, row] + ([row] if has_b else []),
        out_specs=[row, row, vec, vec],
        out_shape=[jax.ShapeDtypeStruct((t, d), F32), jax.ShapeDtypeStruct((t, d), BF16),
                   jax.ShapeDtypeStruct((1, d), F32), jax.ShapeDtypeStruct((1, d), F32)],
        compiler_params=_params("arbitrary"), name=name,
    )(*args)


def _attn_fn(q, k, v, dots):
    nn, nt, _ = dots
    s = nt(q, k) * (XATTN_HEAD_DIM ** -0.5)
    s = s - lax.stop_gradient(jnp.max(s, axis=1, keepdims=True))
    e = jnp.exp(s)
    p = e / jnp.sum(e, axis=1, keepdims=True)
    return nn(p, v)


def _attn_fwd(q, k, v, tq=512):
    t = q.shape[0]
    tq = min(tq, t)

    def body(q_ref, k_ref, v_ref, o_ref):
        o_ref[...] = _attn_fn(q_ref[...], k_ref[...], v_ref[...], _BDOT_PLAIN).astype(BF16)

    qs = pl.BlockSpec((tq, XATTN_HEAD_DIM), lambda h, i: (i, h))
    ks = pl.BlockSpec((MEM_LEN, XATTN_HEAD_DIM), lambda h, i: (0, h))
    return pl.pallas_call(
        body, grid=(XATTN_HEADS, t // tq), in_specs=[qs, ks, ks], out_specs=qs,
        out_shape=jax.ShapeDtypeStruct(q.shape, BF16), compiler_params=_params("parallel", "parallel"), name="xattn_fwd",
    )(q, k, v)


def _attn_bwd(q, k, v, do, tq=512):
    t = q.shape[0]
    tq = min(tq, t)

    def body(q_ref, k_ref, v_ref, do_ref, dq_ref, dk_ref, dv_ref):
        @pl.when(pl.program_id(1) == 0)
        def _():
            dk_ref[...] = jnp.zeros_like(dk_ref)
            dv_ref[...] = jnp.zeros_like(dv_ref)

        _, vjp = jax.vjp(lambda a, b, c: _attn_fn(a, b, c, _BDOT_VJP), q_ref[...].astype(F32), k_ref[...].astype(F32),
                         v_ref[...].astype(F32))
        dq, dk, dv = vjp(do_ref[...].astype(F32))
        dq_ref[...] = dq.astype(BF16)
        dk_ref[...] += dk
        dv_ref[...] += dv

    qs = pl.BlockSpec((tq, XATTN_HEAD_DIM), lambda h, i: (i, h))
    ks = pl.BlockSpec((MEM_LEN, XATTN_HEAD_DIM), lambda h, i: (0, h))
    return pl.pallas_call(
        body, grid=(XATTN_HEADS, t // tq), in_specs=[qs, ks, ks, qs], out_specs=[qs, ks, ks],
        out_shape=[jax.ShapeDtypeStruct(q.shape, BF16), jax.ShapeDtypeStruct(k.shape, F32), jax.ShapeDtypeStruct(v.shape, F32)],
        compiler_params=_params("parallel", "arbitrary"), name="xattn_bwd",
    )(q, k, v, do)


def _local_step(x, x16, mem, target, weights_of, grads_ready):
    def behind(vec, token):
        return vec if token is None else vec + token

    w = dict(weights_of("mixer", None))
    proj = _mm(x16, w["w_in"], tb=True, tn=768, name="mm_in_proj")
    post = _gdn_prep_fwd(proj, w["conv_w"])
    mixin = _pool_fwd(proj, w["pool_w"], w["pool_scale"])
    token = weights_of("ahead", mixin)
    chunked = _gdn_local_fwd(post, proj, behind(w["alog_row"], token), w["dtb_row"])
    o_raw, saved = _gdn_state_fwd(*chunked)
    mixin = _onorm_fwd(o_raw, proj, w["gdn_norm_w"], mixin)
    w.update(weights_of("attn", mixin))
    mix = _mm(mixin, w["w_out"], name="mm_out_proj")
    h1, h1_16 = _ln_fwd(x, mix, w["ln1_g"], w["ln1_b"], name="ln1_fwd")
    xq = _mm(h1_16, w["xq_w"], out_dtype=BF16, name="mm_xq")
    xk = _mm(mem, w["xk_w"], out_dtype=BF16, name="mm_xk")
    xv = _mm(mem, w["xv_w"], out_dtype=BF16, name="mm_xv")
    xo = _attn_fwd(xq, xk, xv)
    xa = _mm(xo, w["xo_w"], name="mm_xo")
    h2, h2_16 = _ln_fwd(h1, xa, w["ln2_g"], w["ln2_b"], name="ln2_fwd")
    w.update(weights_of("mlp", h2_16))
    act, relu = _mm(h2_16, w["w_up"], b_chunks=True, epi="relu2", name="mm_up")
    ff = _mm(act, w["w_down"], tn=1024, tk=1024, name="mm_down")
    dy, sq = _ln_loss_fwd(h2, ff, w["ln3_g"], w["ln3_b"], target, name="ln3_loss_fwd")

    g = {}
    ds3, ds3_16, g["ln3_g"], g["ln3_b"] = _ln_bwd(h2, ff, w["ln3_g"], dy, None, name="ln3_bwd")
    gw_down = _mm(act, ds3_16, ta=True, out_dtype=BF16, tm=512, tn=D_MODEL, name="mm_gw_down")
    du = _mm(ds3_16, w["w_down"], tb=True, epi="mul2r", extra=relu, name="mm_du")
    gw_up = _mm(h2_16, du, ta=True, out_dtype=BF16, o_chunks=True, name="mm_gw_up")
    token = grads_ready("mlp", {"w_down": gw_down, "w_up": gw_up})
    dh2 = _mm(du, w["w_up"], tb=True, b_chunks=True, tn=1024, tk=1024, name="mm_dh2")
    ds2, ds2_16, g["ln2_g"], g["ln2_b"] = _ln_bwd(h1, xa, behind(w["ln2_g"], token), dh2, ds3, name="ln2_bwd")
    gw_xo = _mm(xo, ds2_16, ta=True, out_dtype=BF16, name="mm_gw_xo")
    dxo = _mm(ds2_16, w["xo_w"], tb=True, out_dtype=BF16, name="mm_dxo")
    dxq, dxk, dxv = _attn_bwd(xq, xk, xv, dxo)
    gw_xq = _mm(h1_16, dxq, ta=True, out_dtype=BF16, name="mm_gw_xq")
    gw_xk = _mm(mem, dxk, ta=True, out_dtype=BF16, name="mm_gw_xk")
    gw_xv = _mm(mem, dxv, ta=True, out_dtype=BF16, name="mm_gw_xv")
    token = grads_ready("attn", {"xo_w": gw_xo, "xq_w": gw_xq, "xk_w": gw_xk, "xv_w": gw_xv})
    dh1 = _mm(dxq, w["xq_w"], tb=True, name="mm_dh1")
    ds1, ds1_16, g["ln1_g"], g["ln1_b"] = _ln_bwd(x, mix, behind(w["ln1_g"], token), dh1, ds2, name="ln1_bwd")
    gw_out = _mm(mixin, ds1_16, ta=True, out_dtype=BF16, name="mm_gw_out")
    dmixin = _mm(ds1_16, w["w_out"], tb=True, name="mm_dmixin")
    dproj, gw_pool, g["pool_scale"] = _pool_bwd(proj, w["pool_w"], w["pool_scale"], dmixin)
    token = grads_ready("mix", {"w_out": gw_out, "pool_w": gw_pool})
    do_raw, dproj, g["gdn_norm_w"] = _onorm_bwd(o_raw, proj, behind(w["gdn_norm_w"], token), dmixin, dproj)
    cots = _gdn_state_bwd(*chunked, saved, do_raw)
    token = grads_ready("tick", {"after": cots[0]})
    dpost, dproj, g["alog_row"], g["dtb_row"] = _gdn_local_bwd(post, proj, behind(w["alog_row"], token), w["dtb_row"],
                                                               cots, dproj)
    dproj, g["conv_w"] = _gdn_prep_bwd(proj, w["conv_w"], dpost, dproj)
    gw_in = _mm(dproj, x16, ta=True, out_dtype=BF16, tm=768, tn=D_MODEL, name="mm_gw_in")
    token = grads_ready("in", {"w_in": gw_in})
    if token is not None:
        ds1, _ = lax.optimization_barrier((ds1, token))
    grad_x = _mm(dproj, w["w_in"], tk=768, epi="add", extra=ds1, add_scale=ALPHA, name="mm_dx")
    return sq, grad_x, g


_MATRICES = ("w_in", "pool_w", "w_out", "xq_w", "xk_w", "xv_w", "xo_w", "w_up", "w_down")
_VECTORS = ("a_log", "dt_bias", "gdn_norm_w", "pool_scale", "ln1_g", "ln1_b", "ln2_g", "ln2_b", "ln3_g", "ln3_b")
_BA_SPLIT = BA_OFF + 2 * GDN_HEADS


def _lane_row(v, offset):
    return jnp.zeros((1, LANE), F32).at[0, offset:offset + v.shape[0]].set(v)


_GROUP_VECTORS = {"mixer": (), "attn": ("ln1_g", "ln1_b", "ln2_g", "ln2_b"), "mlp": ("ln3_g", "ln3_b")}


def _group_weights(group, full):
    w = {n: full[n].reshape(1, D_MODEL) for n in _GROUP_VECTORS[group]}
    if group == "mixer":
        w_in = full["w_in"].reshape(IN_COLS, D_MODEL)
        zeros = jnp.zeros((POOL_OFF - _BA_SPLIT, D_MODEL), w_in.dtype)
        w.update({
            "w_in": jnp.concatenate([w_in[:_BA_SPLIT], zeros, w_in[_BA_SPLIT:]], axis=0),
            "conv_w": full["conv_w"],
            "alog_row": _lane_row(full["a_log"], GDN_HEADS),
            "dtb_row": _lane_row(full["dt_bias"], GDN_HEADS),
            "gdn_norm_w": full["gdn_norm_w"].reshape(1, LANE),
            "pool_w": full["pool_w"],
            "pool_scale": full["pool_scale"].reshape(POOL_GROUPS, 1, POOL_GROUP_DIM),
        })
    elif group == "attn":
        w.update({n: full[n] for n in ("w_out", "xq_w", "xk_w", "xv_w", "xo_w")})
    else:
        w.update({n: full[n] for n in ("w_up", "w_down")})
    return w


def _w_in_chunks(g):
    unpadded = jnp.concatenate([g[:_BA_SPLIT], g[POOL_OFF:]], axis=0)
    return unpadded.reshape(N_DEV, IN_COLS // N_DEV, D_MODEL)


def _finish_small_grads(g):
    out = {"conv_w": g["conv_w"]}
    out["a_log"] = g["alog_row"][0, GDN_HEADS:2 * GDN_HEADS]
    out["dt_bias"] = g["dtb_row"][0, GDN_HEADS:2 * GDN_HEADS]
    out["gdn_norm_w"] = g["gdn_norm_w"].reshape(LANE)
    out["pool_scale"] = g["pool_scale"].reshape(POOL_GROUPS * POOL_GROUP_DIM)
    for n in ("ln1_g", "ln1_b", "ln2_g", "ln2_b", "ln3_g", "ln3_b"):
        out[n] = g[n].reshape(D_MODEL)
    return out


def _adamw_math(w, g, m, v):
    m = ADAM_B1 * m + (1.0 - ADAM_B1) * g
    v = ADAM_B2 * v + (1.0 - ADAM_B2) * (g * g)
    m_hat = m / (1.0 - ADAM_B1 ** ADAM_STEP)
    v_hat = v / (1.0 - ADAM_B2 ** ADAM_STEP)
    delta = -ADAM_LR * (m_hat / (jnp.sqrt(v_hat) + ADAM_EPS) + ADAM_WD * w)
    return delta, m, v


def _shard_tile(r, c):
    if r % 128 == 0:
        return 128, c
    return r, 256 if c % 256 == 0 else c


def _adamw_shard(parts, own, me, w, m, v, *, name):
    s, r, c = parts.shape
    tr, tc = _shard_tile(r, c)
    assert r % tr == 0 and c % tc == 0, (name, r, c)
    unit_axis = w.ndim == 3
    at = (slice(None), 0, slice(None)) if unit_axis else Ellipsis

    def body(me_ref, p_ref, own_ref, w_ref, m_ref, v_ref, g_ref, d_ref, nm_ref, nv_ref):
        mine = own_ref[...].astype(F32)
        g = None
        for i in range(s):
            part = jnp.where(me_ref[0] == i, mine, p_ref[i].astype(F32))
            g = part if g is None else g + part
        delta, nm, nv = _adamw_math(w_ref[at], g, m_ref[at], v_ref[at])
        g_ref[at] = g
        d_ref[at] = delta
        nm_ref[at] = nm
        nv_ref[at] = nv

    if unit_axis:
        blk = pl.BlockSpec((tr, 1, tc), lambda i, j, me_ref: (i, 0, j))
        out = jax.ShapeDtypeStruct((r, 1, c), F32)
    else:
        blk = pl.BlockSpec((tr, tc), lambda i, j, me_ref: (i, j))
        out = jax.ShapeDtypeStruct((r, c), F32)
    return pl.pallas_call(
        body,
        grid_spec=pltpu.PrefetchScalarGridSpec(
            num_scalar_prefetch=1, grid=(r // tr, c // tc),
            in_specs=[pl.BlockSpec((s, tr, tc), lambda i, j, me_ref: (0, i, j)),
                      pl.BlockSpec((None, tr, tc), lambda i, j, me_ref: (me_ref[0], i, j)), blk, blk, blk],
            out_specs=[blk, blk, blk, blk]),
        out_shape=[out, out, out, out], compiler_params=_params("parallel", "parallel"), name=name,
    )(me, parts, own, w, m, v)


N_CHIPS = N_DEV // 2


def _chip_sums(chunks, from_sibling, core, *, name):
    _, r, c = chunks.shape
    tr, tc = _shard_tile(r, c)
    assert r % tr == 0 and c % tc == 0, (name, r, c)

    def body(core_ref, mine_ref, other_ref, o_ref):
        o_ref[...] = (mine_ref[...].astype(F32) + other_ref[...].astype(F32)).astype(o_ref.dtype)

    by_chip = pl.BlockSpec((None, tr, tc), lambda q, i, j, core_ref: (q, i, j))
    return pl.pallas_call(
        body,
        grid_spec=pltpu.PrefetchScalarGridSpec(
            num_scalar_prefetch=1, grid=(N_CHIPS, r // tr, c // tc),
            in_specs=[pl.BlockSpec((None, tr, tc), lambda q, i, j, core_ref: (2 * q + core_ref[0], i, j)), by_chip],
            out_specs=by_chip),
        out_shape=jax.ShapeDtypeStruct((N_CHIPS, r, c), chunks.dtype),
        compiler_params=_params("parallel", "parallel", "parallel"), name=name,
    )(core, chunks, from_sibling)


def _place():
    return lax.axis_index("x"), lax.axis_index("y"), lax.axis_index("c")


def _slot(px, py, pc):
    return 4 * px + 2 * py + pc


_HBM = pl.BlockSpec(memory_space=pltpu.HBM)


_SEM = pl.BlockSpec(memory_space=pltpu.SEMAPHORE)
_ANY = pl.BlockSpec(memory_space=pl.ANY)
_EFFECT = pltpu.SideEffectType.DATAFLOW_SIDE_EFFECTING
_N_PEERS = N_DEV - 1


def _peer(k, x, y, c):
    return (1 - x if k & 4 else x, 1 - y if k & 2 else y, 1 - c if k & 1 else c)


_EXCHANGE_BITS = {"gather_chips": (1, 2, 4, 6), "gather_pass": (2, 4, 6), "scatter_sibling": (1, 1, 1, 1),
                  "scatter_chips": (2, 4, 6)}


def _exchange_copy(mode, src, land, w, i, place, send_sems, recv_sems, receiving):
    bits = _EXCHANGE_BITS[mode]
    k = bits[i]
    peer = _peer(k, *place)
    me = _slot(*place)
    if mode == "gather_chips":
        to, src_ref, sent_to, got_at = peer, src[w], me, _slot(*peer)
    elif mode == "gather_pass":
        blk = _slot(*peer)
        to, src_ref, sent_to, got_at = _peer(1, *place), land[w].at[blk], blk, _slot(*_peer(k | 1, *place))
    elif mode == "scatter_sibling":
        to, src_ref, sent_to, got_at = peer, src[w].at[2 * i + 1 - place[2]], i, i
    else:
        to, src_ref, sent_to, got_at = peer, src[w].at[_slot(*peer) // 2], me // 2, _slot(*peer) // 2
    sem = w * len(bits) + i
    return pltpu.make_async_remote_copy(
        src_ref=src_ref, dst_ref=land[w].at[got_at if receiving else sent_to], send_sem=send_sems.at[sem],
        recv_sem=recv_sems.at[sem], device_id=to, device_id_type=MESH)


def _exchange_start(mode, srcs, lands, after, *, name):
    ns, nl = len(srcs), len(lands)
    n_sem = nl * len(_EXCHANGE_BITS[mode])

    def body(*refs):
        src, land = refs[:ns], refs[ns:ns + nl]
        send_sems, recv_sems = refs[ns + nl + 1:ns + nl + 3]
        token = refs[-1]
        place = _place()
        for w in range(nl):
            for i in range(len(_EXCHANGE_BITS[mode])):
                _exchange_copy(mode, src, land, w, i, place, send_sems, recv_sems, receiving=False).start()
        token[...] = jnp.zeros_like(token)

    sems = pltpu.SemaphoreType.DMA((n_sem,))
    arrays = list(srcs) + list(lands)
    res = pl.pallas_call(
        body, name=name, in_specs=[_HBM] * (ns + nl) + [_ANY],
        out_specs=(_SEM, _SEM, *([_HBM] * (ns + nl)), pl.BlockSpec(memory_space=pltpu.VMEM)),
        out_shape=(sems, sems, *[pltpu.HBM(a.shape, a.dtype) for a in arrays], jax.ShapeDtypeStruct((8, LANE), F32)),
        input_output_aliases={i: 2 + i for i in range(ns + nl)},
        compiler_params=pltpu.CompilerParams(has_side_effects=_EFFECT),
    )(*[pltpu.with_memory_space_constraint(a, pltpu.HBM) for a in arrays], after)
    return res[0], res[1], list(res[2:2 + ns]), list(res[2 + ns:2 + ns + nl]), res[-1]


def _exchange_wait(mode, started, after, *, name):
    send_sems, recv_sems, srcs, lands, _ = started
    ns, nl = len(srcs), len(lands)

    def body(*refs):
        src, land = refs[:ns], refs[ns:ns + nl]
        send_sems, recv_sems = refs[ns + nl:ns + nl + 2]
        place = _place()
        for w in range(nl):
            for i in range(len(_EXCHANGE_BITS[mode])):
                cp = _exchange_copy(mode, src, land, w, i, place, send_sems, recv_sems, receiving=True)
                cp.wait_send()
                cp.wait_recv()

    arrays = list(srcs) + list(lands)
    res = pl.pallas_call(
        body, name=name, in_specs=[_HBM] * (ns + nl) + [_SEM, _SEM, _ANY], out_specs=[_HBM] * (ns + nl),
        out_shape=[pltpu.HBM(a.shape, a.dtype) for a in arrays],
        input_output_aliases={i: i for i in range(ns + nl)},
        compiler_params=pltpu.CompilerParams(has_side_effects=_EFFECT),
    )(*arrays, send_sems, recv_sems, after)
    return list(res[:ns]), list(res[ns:])


def _small_allreduce_adamw(gvec, wvec, mvec, vvec):
    rows, length = gvec.shape

    def body(g_ref, w_ref, m_ref, v_ref, gs_ref, d_ref, nm_ref, nv_ref, slots, send_sems, recv_sems):
        x, y, c = _place()
        me = _slot(x, y, c)
        slots[me] = g_ref[...]
        sends = []
        for k in range(1, N_DEV):
            peer = _peer(k, x, y, c)
            sends.append(pltpu.make_async_remote_copy(
                src_ref=g_ref, dst_ref=slots.at[me], send_sem=send_sems.at[k - 1], recv_sem=recv_sems.at[k - 1],
                device_id=peer, device_id_type=MESH))
        for cp in sends:
            cp.start()
        for k in range(1, N_DEV):
            peer = _peer(k, x, y, c)
            pltpu.make_async_remote_copy(
                src_ref=g_ref, dst_ref=slots.at[_slot(*peer)], send_sem=send_sems.at[k - 1], recv_sem=recv_sems.at[k - 1],
                device_id=peer, device_id_type=MESH).wait_recv()
        for cp in sends:
            cp.wait_send()
        g = slots[0]
        for s in range(1, N_DEV):
            g = g + slots[s]
        delta, nm, nv = _adamw_math(w_ref[...], g, m_ref[...], v_ref[...])
        gs_ref[...] = g
        d_ref[...] = delta
        nm_ref[...] = nm
        nv_ref[...] = nv

    vmem = pl.BlockSpec(memory_space=pltpu.VMEM)
    out = jax.ShapeDtypeStruct((rows, length), F32)
    return pl.pallas_call(
        body, in_specs=[vmem] * 4, out_specs=[vmem] * 4, out_shape=[out] * 4,
        scratch_shapes=[pltpu.VMEM((N_DEV, rows, length), F32), pltpu.SemaphoreType.DMA((N_DEV - 1,)),
                        pltpu.SemaphoreType.DMA((N_DEV - 1,))],
        name="small_allreduce_adamw",
    )(gvec, wvec, mvec, vvec)


_SMALL_SEGMENTS = (("a_log", GDN_HEADS), ("dt_bias", GDN_HEADS), ("gdn_norm_w", HEAD_DIM), ("pool_scale", GDN_WIDTH),
                   ("ln1_g", D_MODEL), ("ln1_b", D_MODEL), ("ln2_g", D_MODEL), ("ln2_b", D_MODEL),
                   ("ln3_g", D_MODEL), ("ln3_b", D_MODEL), ("conv_w", CONV_K * QKV_COLS))
_SMALL_ROWS = 8
_SMALL_LEN = -(-sum(sz for _, sz in _SMALL_SEGMENTS) // (_SMALL_ROWS * LANE)) * LANE


def _pack_small(vals):
    parts = [vals[n].reshape(-1).astype(F32) if n in vals else jnp.zeros((sz,), F32) for n, sz in _SMALL_SEGMENTS]
    flat = jnp.concatenate(parts)
    flat = jnp.pad(flat, (0, _SMALL_ROWS * _SMALL_LEN - flat.shape[0]))
    return flat.reshape(_SMALL_ROWS, _SMALL_LEN)


def _unpack_small(vec):
    flat = vec.reshape(-1)
    out, off = {}, 0
    for n, sz in _SMALL_SEGMENTS:
        out[n] = flat[off:off + sz]
        off += sz
    return out


_WEIGHT_ORDER = ("w_in", "conv_w", "a_log", "dt_bias", "gdn_norm_w", "pool_w", "pool_scale", "w_out", "ln1_g", "ln1_b",
                 "xq_w", "xk_w", "xv_w", "xo_w", "ln2_g", "ln2_b", "w_up", "w_down", "ln3_g", "ln3_b")


def _shard2d(name, a):
    if name == "w_in":
        return a.T
    return a.reshape(-1, a.shape[-1]) if name == "pool_w" else a


def _update_view(name, a):
    return jnp.transpose(a, (2, 0, 1)) if name == "w_in" else _shard2d(name, a[0])


def _shard_result(name, r, shape):
    return jnp.transpose(r, (1, 2, 0)) if name == "w_in" else r.reshape(shape)


def _gathered_to_full(name, gth):
    if name in ("w_up", "w_in"):
        return gth
    if name == "conv_w":
        return jnp.transpose(gth, (1, 0, 2)).reshape(gth.shape[1], N_DEV * gth.shape[2])
    if name == "pool_w":
        g4 = gth.reshape(N_DEV, POOL_GROUPS, POOL_GROUP_DIM // N_DEV, POOL_GROUP_DIM)
        return jnp.transpose(g4, (1, 0, 2, 3)).reshape(POOL_GROUPS, POOL_GROUP_DIM, POOL_GROUP_DIM)
    return gth.reshape(N_DEV * gth.shape[1], gth.shape[2])


def _full_to_chunks(name, full):
    if name == "w_up":
        return full
    if name == "pool_w":
        g4 = full.reshape(POOL_GROUPS, N_DEV, POOL_GROUP_DIM // N_DEV, POOL_GROUP_DIM)
        return jnp.transpose(g4, (1, 0, 2, 3)).reshape(N_DEV, POOL_GROUPS * POOL_GROUP_DIM // N_DEV, POOL_GROUP_DIM)
    return full.reshape(N_DEV, full.shape[0] // N_DEV, full.shape[1])


_GATHER_GROUPS = (("mixer", ("w_in", "conv_w", "pool_w")), ("attn", ("w_out", "xq_w", "xk_w", "xv_w", "xo_w")),
                  ("mlp", ("w_up", "w_down")))


def _grad_chunks(name, g):
    if name == "w_in":
        return _w_in_chunks(g.astype(BF16))
    return _full_to_chunks(name, g.astype(BF16))


def kernel(x, mem, w_in, conv_w, a_log, dt_bias, gdn_norm_w, pool_w, pool_scale, w_out, ln1_g, ln1_b, xq_w, xk_w, xv_w, xo_w, ln2_g, ln2_b, w_up, w_down, ln3_g, ln3_b, loss_target, m_w_in, m_conv_w, m_a_log, m_dt_bias, m_gdn_norm_w, m_pool_w, m_pool_scale, m_w_out, m_ln1_g, m_ln1_b, m_xq_w, m_xk_w, m_xv_w, m_xo_w, m_ln2_g, m_ln2_b, m_w_up, m_w_down, m_ln3_g, m_ln3_b, v_w_in, v_conv_w, v_a_log, v_dt_bias, v_gdn_norm_w, v_pool_w, v_pool_scale, v_w_out, v_ln1_g, v_ln1_b, v_xq_w, v_xk_w, v_xv_w, v_xo_w, v_ln2_g, v_ln2_b, v_w_up, v_w_down, v_ln3_g, v_ln3_b):
    args = dict(locals())
    wt = {n: args[n][0] for n in _WEIGHT_ORDER}
    mo = {n: args["m_" + n][0] for n in _WEIGHT_ORDER}
    vo = {n: args["v_" + n][0] for n in _WEIGHT_ORDER}

    me = _slot(*_place())
    me_arr = jnp.reshape(me, (1,)).astype(jnp.int32)
    nothing = jnp.zeros((8, LANE), F32)

    def landing_zones(names):
        shards = [_shard2d(n, wt[n]).astype(F32 if n == "conv_w" else BF16) for n in names]
        zones = [lax.dynamic_update_slice(lax.empty((N_DEV, *s.shape), s.dtype), s[None], (me, 0, 0)) for s in shards]
        return shards, zones

    chip_arr = jnp.reshape(me // 2, (1,)).astype(jnp.int32)
    core_arr = jnp.reshape(lax.axis_index("c"), (1,)).astype(jnp.int32)
    first, attn_names, mlp_names = (names for _, names in _GATHER_GROUPS)
    gathers = {}

    prepared = {}

    def gather_chips(group, names, after):
        shards, zones = prepared.pop(group) if group in prepared else landing_zones(names)
        gathers[group] = _exchange_start("gather_chips", shards, zones, after, name="gather_chips_" + group)
        return gathers[group][4]

    def gather_pass(group, after):
        _, zones = _exchange_wait("gather_chips", gathers[group], after, name=f"gather_chips_{group}_wait")
        gathers[group] = _exchange_start("gather_pass", [], zones, nothing, name="gather_pass_" + group)
        return gathers[group][4]

    def gathered(group, names, after, token=None):
        _, zones = _exchange_wait("gather_pass", gathers[group], after, name=f"gather_pass_{group}_wait")
        full = {n: _gathered_to_full(n, z) for n, z in zip(names, zones)}
        full.update({n: wt[n] if token is None else wt[n] + token for n in _VECTORS})
        return _group_weights(group, full)

    token = gather_chips("mixer", first, nothing)
    x16 = _cast_bf16(x[0], name="cast_x")
    later = [landing_zones(attn_names), landing_zones(mlp_names)]
    token, x16, later = lax.optimization_barrier((token, x16, later))
    prepared["attn"], prepared["mlp"] = later
    token = gather_chips("attn", attn_names, gather_pass("mixer", token))

    def weights_of(group, after):
        if group == "mixer":
            return gathered(group, first, gathers["attn"][4])
        if group == "ahead":
            return gather_chips("mlp", mlp_names, gather_pass("attn", after))[0:1, 0:1]
        if group == "attn":
            return gathered(group, attn_names, after)
        return gathered(group, mlp_names, gather_pass("mlp", after))

    scatters = {}
    in_flight = []

    def chip_stage(after):
        group, names, started = in_flight.pop()
        chunks, from_sibling = _exchange_wait("scatter_sibling", started, after, name=f"scatter_sibling_{group}_wait")
        sums = [_chip_sums(c, f, core_arr, name=f"chip_sums_{n}") for n, c, f in zip(names, chunks, from_sibling)]
        scatters[group] = (names, _exchange_start("scatter_chips", sums, [lax.empty(s.shape, s.dtype) for s in sums],
                                                  nothing, name="scatter_chips_" + group))
        return scatters[group][1][4]

    def grads_ready(group, grads):
        if group == "tick":
            return chip_stage(grads["after"])[0:1, 0:1] if in_flight else None
        names = tuple(grads)
        chunks = [_grad_chunks(n, grads[n]) for n in names]
        token = chip_stage(chunks[0]) if in_flight else nothing
        zones = [lax.empty((N_CHIPS, *c.shape[1:]), c.dtype) for c in chunks]
        started = _exchange_start("scatter_sibling", chunks, zones, token, name="scatter_sibling_" + group)
        in_flight.append((group, names, started))
        return started[4][0:1, 0:1]

    sq, grad_x, g = _local_step(x[0], x16, mem[0], loss_target[0], weights_of, grads_ready)
    small = _finish_small_grads(g)

    out = {}
    after = chip_stage(grad_x)
    for group, (names, started) in scatters.items():
        sums, lands = _exchange_wait("scatter_chips", started, after, name=f"scatter_chips_{group}_wait")
        for n, parts, own in zip(names, lands, sums):
            res = _adamw_shard(parts, own, chip_arr, _update_view(n, args[n]), _update_view(n, args["m_" + n]),
                               _update_view(n, args["v_" + n]), name="adamw_" + n)
            out[n] = [_shard_result(n, r, args[n].shape) for r in res]
            after = res[1]

    packed, _ = lax.optimization_barrier((_pack_small(small), after))
    gs, ds, ms, vs = _small_allreduce_adamw(
        packed, _pack_small({n: wt[n] for n in _VECTORS}), _pack_small({n: mo[n] for n in _VECTORS}),
        _pack_small({n: vo[n] for n in _VECTORS}))
    gs, ds, ms, vs = _unpack_small(gs), _unpack_small(ds), _unpack_small(ms), _unpack_small(vs)
    cols = conv_w.shape[-1]
    conv_full = gs["conv_w"].reshape(CONV_K, QKV_COLS)
    conv_mine = lax.dynamic_slice(conv_full, (0, me * cols), (CONV_K, cols))[None]
    res = _adamw_shard(conv_mine, conv_mine, jnp.zeros((1,), jnp.int32), wt["conv_w"], mo["conv_w"], vo["conv_w"],
                       name="adamw_conv_w")
    out["conv_w"] = [r.reshape(conv_w.shape) for r in res]
    for n in _VECTORS:
        out[n] = [t[n].reshape(args[n].shape) for t in (gs, ds, ms, vs)]

    loss = lax.psum(0.5 * sq[0, 0] / D_MODEL, ("x", "y", "c"))
    return (loss, grad_x[None], *[out[n][0] for n in _WEIGHT_ORDER], *[out[n][1] for n in _WEIGHT_ORDER],
            *[out[n][2] for n in _WEIGHT_ORDER], *[out[n][3] for n in _WEIGHT_ORDER])
```

```python
import functools
import math

import jax
import jax.numpy as jnp
from jax import lax
from jax.experimental import pallas as pl
from jax.experimental.pallas import tpu as pltpu

F32 = jnp.float32
BF16 = jnp.bfloat16
MESH = pl.DeviceIdType.MESH

N_DEV = 8
D_MODEL = 2048
GDN_WIDTH = 1024
GDN_HEADS = 8
HEAD_DIM = 128
CONV_K = 4
CHUNK = 64
POOL_GROUPS = 4
POOL_GROUP_DIM = 256
MEM_LEN = 256
XATTN_HEADS = 4
XATTN_HEAD_DIM = 512
D_FF = 8192
IN_COLS = 5136
ALPHA = 2.0 ** 0.25
LN_EPS = 1e-5
NORM_EPS = 1e-6

LANE = 128
QKV_COLS = 3 * GDN_WIDTH
Z_OFF = QKV_COLS
BA_OFF = 4 * GDN_WIDTH
POOL_OFF = BA_OFF + 2 * LANE
PROJ_COLS = POOL_OFF + GDN_WIDTH
Z_BLK = Z_OFF // LANE
BA_BLK = BA_OFF // LANE
POOL_BLK = POOL_OFF // POOL_GROUP_DIM

ADAM_LR = 0.001
ADAM_B1 = 0.9
ADAM_B2 = 0.999
ADAM_EPS = 1e-08
ADAM_WD = 0.01
ADAM_STEP = 10

VMEM_LIMIT_BYTES = 48 * 1024 * 1024


def _params(*sem):
    return pltpu.CompilerParams(dimension_semantics=sem if sem else None, vmem_limit_bytes=VMEM_LIMIT_BYTES)


def _make_dots(cast, precision, batched=False):
    lead = 1 if batched else 0
    batch = ((0,), (0,)) if batched else ((), ())

    def dg(a, b, ca, cb):
        if cast is not None:
            a = a.astype(cast)
            b = b.astype(cast)
        return lax.dot_general(a, b, (((ca + lead,), (cb + lead,)), batch), precision=precision, preferred_element_type=F32)

    def nn_(a, b):
        return dg(a, b, 1, 0)

    def nt_(a, b):
        return dg(a, b, 1, 1)

    def tn_(a, b):
        return dg(a, b, 0, 0)

    @jax.custom_vjp
    def nn(a, b):
        return nn_(a, b)

    nn.defvjp(lambda a, b: (nn_(a, b), (a, b)), lambda r, g: (nt_(g, r[1]), tn_(r[0], g)))

    @jax.custom_vjp
    def nt(a, b):
        return nt_(a, b)

    nt.defvjp(lambda a, b: (nt_(a, b), (a, b)), lambda r, g: (nn_(g, r[1]), tn_(g, r[0])))

    @jax.custom_vjp
    def tn(a, b):
        return tn_(a, b)

    tn.defvjp(lambda a, b: (tn_(a, b), (a, b)), lambda r, g: (nt_(r[1], g), nn_(r[0], g)))

    return (nn_, nt_, tn_), (nn, nt, tn)


_BDOT_PLAIN, _BDOT_VJP = _make_dots(BF16, None)
_BDOT_BATCH_PLAIN, _BDOT_BATCH_VJP = _make_dots(BF16, None, batched=True)
_FDOT_BATCH_PLAIN, _FDOT_BATCH_VJP = _make_dots(None, lax.Precision.HIGH, batched=True)


def _mm(a, b, *, ta=False, tb=False, out_dtype=F32, tm=None, tn=512, tk=None, epi=None, extra=None, add_scale=1.0,
        b_chunks=False, o_chunks=False, name):
    m, k = (a.shape[1], a.shape[0]) if ta else a.shape
    if b_chunks:
        n, kb = (b.shape[1], N_DEV * b.shape[2]) if tb else (N_DEV * b.shape[2], b.shape[1])
    else:
        n, kb = b.shape if tb else (b.shape[1], b.shape[0])
    assert kb == k, (name, a.shape, b.shape)
    tm, tn, tk = min(tm or m, m), min(tn, n), min(tk or k, k)
    assert m % tm == 0 and n % tn == 0 and k % tk == 0, (name, m, n, k)
    nk = k // tk
    dims = (((0 if ta else 1,), (1 if tb else 0,)), ((), ()))
    n_extra = 0 if epi in (None, "relu2") else 1
    n_out = 2 if epi == "relu2" else 1
    if epi in ("relu2", "mul2r"):
        out_dtype = BF16

    def body(*refs):
        a_ref, b_ref = refs[:2]
        c_ref = refs[2] if n_extra else None
        o_refs = refs[2 + n_extra:2 + n_extra + n_out]
        scr = refs[2 + n_extra + n_out:]
        r = lax.dot_general(a_ref[...].astype(BF16), b_ref[...].astype(BF16), dims, preferred_element_type=F32)

        def finish(v):
            if epi == "add":
                o_refs[0][...] = (v + add_scale * c_ref[...]).astype(out_dtype)
            elif epi == "relu2":
                p = jnp.maximum(v, 0.0)
                o_refs[0][...] = (p * p).astype(BF16)
                o_refs[1][...] = p.astype(BF16)
            elif epi == "mul2r":
                o_refs[0][...] = (v * (2.0 * c_ref[...].astype(F32))).astype(BF16)
            else:
                o_refs[0][...] = v.astype(out_dtype)

        if nk == 1:
            finish(r)
        else:
            acc = scr[0]
            kk = pl.program_id(2)

            @pl.when(kk == 0)
            def _():
                acc[...] = r

            @pl.when(kk > 0)
            def _():
                acc[...] += r

            @pl.when(kk == nk - 1)
            def _():
                finish(acc[...])

    a_spec = pl.BlockSpec((tk, tm), lambda i, j, kk: (kk, i)) if ta else pl.BlockSpec((tm, tk), lambda i, j, kk: (i, kk))
    if b_chunks and tb:
        kc = k // N_DEV // tk
        b_spec = pl.BlockSpec((None, tn, tk), lambda i, j, kk: (kk // kc, j, kk % kc))
    elif b_chunks:
        nc = n // N_DEV // tn
        b_spec = pl.BlockSpec((None, tk, tn), lambda i, j, kk: (j // nc, kk, j % nc))
    elif tb:
        b_spec = pl.BlockSpec((tn, tk), lambda i, j, kk: (j, kk))
    else:
        b_spec = pl.BlockSpec((tk, tn), lambda i, j, kk: (kk, j))
    mn_spec = pl.BlockSpec((tm, tn), lambda i, j, kk: (i, j))
    if o_chunks:
        oc = n // N_DEV // tn
        o_spec = pl.BlockSpec((None, tm, tn), lambda i, j, kk: (j // oc, i, j % oc))
        o_shape = jax.ShapeDtypeStruct((N_DEV, m, n // N_DEV), out_dtype)
    else:
        o_spec, o_shape = mn_spec, jax.ShapeDtypeStruct((m, n), out_dtype)
    res = pl.pallas_call(
        body, grid=(m // tm, n // tn, nk), in_specs=[a_spec, b_spec] + [mn_spec] * n_extra,
        out_specs=[o_spec] * n_out, out_shape=[o_shape] * n_out,
        scratch_shapes=[pltpu.VMEM((tm, tn), F32)] if nk > 1 else [],
        compiler_params=_params("parallel", "parallel", "arbitrary"), name=name,
    )(a, b, *([extra] if n_extra else []))
    return res if n_out > 1 else res[0]


def _cast_bf16(v, *, name, tm=512):
    t, d = v.shape
    tm = min(tm, t)

    def body(v_ref, o_ref):
        o_ref[...] = v_ref[...].astype(BF16)

    spec = pl.BlockSpec((tm, d), lambda i: (i, 0))
    return pl.pallas_call(body, grid=(t // tm,), in_specs=[spec], out_specs=spec,
                          out_shape=jax.ShapeDtypeStruct((t, d), BF16), compiler_params=_params("parallel"), name=name)(v)


def _shift_down(v, s):
    if s == 0:
        return v
    row = lax.broadcasted_iota(jnp.int32, v.shape, 0)
    return jnp.where(row >= s, pltpu.roll(v, s, axis=0), 0.0)


def _shift_up(v, s):
    if s == 0:
        return v
    t = v.shape[0]
    row = lax.broadcasted_iota(jnp.int32, v.shape, 0)
    return jnp.where(row < t - s, pltpu.roll(v, t - s, axis=0), 0.0)


def _post_col(j):
    return (j % GDN_HEADS) * 3 + j // GDN_HEADS


def _gdn_prep_fwd(proj, conv_w):
    t = proj.shape[0]

    def body(x_ref, w_ref, o_ref):
        j = pl.program_id(0)
        x = x_ref[...]
        y = jnp.zeros_like(x)
        for tap in range(CONV_K):
            y = y + w_ref[tap:tap + 1, :] * _shift_down(x, CONV_K - 1 - tap)
        c = y * jax.nn.sigmoid(y)
        nrm = c * lax.rsqrt(jnp.sum(c * c, axis=1, keepdims=True) + NORM_EPS)
        o_ref[...] = jnp.where(j < 2 * GDN_HEADS, nrm, c)

    return pl.pallas_call(
        body, grid=(QKV_COLS // LANE,),
        in_specs=[pl.BlockSpec((t, LANE), lambda j: (0, j)), pl.BlockSpec((CONV_K, LANE), lambda j: (0, j))],
        out_specs=pl.BlockSpec((t, LANE), lambda j: (0, _post_col(j))),
        out_shape=jax.ShapeDtypeStruct((t, QKV_COLS), F32),
        compiler_params=_params("parallel"), name="gdn_prep_fwd",
    )(proj, conv_w)


def _gdn_prep_bwd(proj, conv_w, dpost, dproj):
    t = proj.shape[0]

    def body(x_ref, w_ref, d_ref, _, dx_ref, dw_ref):
        j = pl.program_id(0)
        x = x_ref[...]
        xs = [_shift_down(x, CONV_K - 1 - tap) for tap in range(CONV_K)]
        y = jnp.zeros_like(x)
        for tap in range(CONV_K):
            y = y + w_ref[tap:tap + 1, :] * xs[tap]
        sig = jax.nn.sigmoid(y)
        c = y * sig
        r = lax.rsqrt(jnp.sum(c * c, axis=1, keepdims=True) + NORM_EPS)
        nrm = c * r
        d = d_ref[...]
        dc_norm = r * (d - nrm * jnp.sum(d * nrm, axis=1, keepdims=True))
        dc = jnp.where(j < 2 * GDN_HEADS, dc_norm, d)
        dy = dc * (sig * (1.0 + y * (1.0 - sig)))
        dx = jnp.zeros_like(x)
        for tap in range(CONV_K):
            dx = dx + _shift_up(w_ref[tap:tap + 1, :] * dy, CONV_K - 1 - tap)
            dw_ref[tap:tap + 1, :] = jnp.sum(dy * xs[tap], axis=0, keepdims=True)
        dx_ref[...] = dx.astype(dx_ref.dtype)

    return pl.pallas_call(
        body, grid=(QKV_COLS // LANE,),
        in_specs=[pl.BlockSpec((t, LANE), lambda j: (0, j)), pl.BlockSpec((CONV_K, LANE), lambda j: (0, j)),
                  pl.BlockSpec((t, LANE), lambda j: (0, _post_col(j))), pl.BlockSpec(memory_space=pl.ANY)],
        out_specs=[pl.BlockSpec((t, LANE), lambda j: (0, j)), pl.BlockSpec((CONV_K, LANE), lambda j: (0, j))],
        out_shape=[jax.ShapeDtypeStruct(dproj.shape, dproj.dtype), jax.ShapeDtypeStruct((CONV_K, QKV_COLS), F32)],
        input_output_aliases={3: 0},
        compiler_params=_params("parallel"), name="gdn_prep_bwd",
    )(proj, conv_w, dpost, dproj)


def _softplus(v):
    return jnp.maximum(v, 0.0) + jnp.log(1.0 + jnp.exp(-jnp.abs(v)))


def _tri_inv(low, nn):
    r = lax.broadcasted_iota(jnp.int32, (CHUNK, CHUNK), 0)
    c = lax.broadcasted_iota(jnp.int32, (CHUNK, CHUNK), 1)
    eye = (r == c).astype(F32)
    same_blk = lax.shift_right_logical(r, 4) == lax.shift_right_logical(c, 4)
    diag = jnp.where(same_blk, low, 0.0)
    off = low - diag
    n1 = -diag
    n2 = nn(n1, n1)
    n4 = nn(n2, n2)
    n8 = nn(n4, n4)
    inv_d = nn(nn(nn(eye + n1, eye + n2), eye + n4), eye + n8)
    m1 = nn(inv_d, off)
    m2 = nn(m1, m1)
    return nn(nn(eye - m1, eye + m2), inv_d)


@jax.custom_vjp
def _tri_inv_batched(low):
    return _tri_inv(low, _FDOT_BATCH_PLAIN[0])


def _tri_inv_batched_fwd(low):
    t_inv = _tri_inv(low, _FDOT_BATCH_PLAIN[0])
    return t_inv, t_inv


def _tri_inv_batched_bwd(t_inv, g):
    _, nt, tn = _FDOT_BATCH_PLAIN
    return (-nt(tn(t_inv, g), t_inv),)


_tri_inv_batched.defvjp(_tri_inv_batched_fwd, _tri_inv_batched_bwd)


LOCAL_HEADS_PER_STEP = 8


def _gdn_local_fn(qkv, ba, alog_row, dtb_row, first_head, bdots, fdots):
    nn, nt, tn = bdots
    fnn = fdots[0]
    n_heads = qkv.shape[1] // (3 * HEAD_DIM)
    part = lambda i, p: qkv[:, (3 * i + p) * HEAD_DIM:(3 * i + p + 1) * HEAD_DIM]
    q = jnp.stack([part(i, 0) for i in range(n_heads)]) * (HEAD_DIM ** -0.5)
    k = jnp.stack([part(i, 1) for i in range(n_heads)])
    v = jnp.stack([part(i, 2) for i in range(n_heads)])
    lane = lax.broadcasted_iota(jnp.int32, ba.shape, 1)
    bg = jnp.where(lane < GDN_HEADS, jax.nn.sigmoid(ba), -jnp.exp(alog_row) * _softplus(ba + dtb_row))
    pick = lambda l: jnp.sum(jnp.where(lane == l, bg, 0.0), axis=1, keepdims=True)
    beta = jnp.stack([pick(first_head + i) for i in range(n_heads)])
    g = jnp.stack([pick(first_head + i + GDN_HEADS) for i in range(n_heads)])

    r = lax.broadcasted_iota(jnp.int32, (CHUNK, CHUNK), 0)
    c = lax.broadcasted_iota(jnp.int32, (CHUNK, CHUNK), 1)
    incl = r >= c
    strict = r > c
    eye = r == c

    def to_row(col):
        return jnp.sum(jnp.where(eye, col, 0.0), axis=1, keepdims=True)

    gc = jnp.sum(jnp.where(incl, to_row(g), 0.0), axis=2, keepdims=True)
    diff = gc - to_row(gc)
    decay = jnp.where(incl, jnp.exp(jnp.where(incl, diff, 0.0)), 0.0)
    k_beta = k * beta
    v_beta = v * beta
    low = jnp.where(strict, nt(k_beta, k) * decay, 0.0)
    t_inv = _tri_inv_batched(low) if fdots is _FDOT_BATCH_VJP else _tri_inv(low, fnn)
    eg = jnp.exp(gc)
    u = fnn(t_inv, v_beta)
    w = fnn(t_inv, k_beta * eg)
    attn = jnp.where(incl, nt(q, k) * decay, 0.0)
    last = lax.broadcasted_iota(jnp.int32, (CHUNK, 1), 0) == CHUNK - 1
    g_last = jnp.sum(jnp.where(last, gc, 0.0), axis=1, keepdims=True)
    kdec = k * jnp.exp(g_last - gc)
    elast = jnp.broadcast_to(jnp.exp(g_last), (n_heads, 1, LANE))
    return u, w, q * eg, kdec, attn, elast


def _gdn_state_fn(u, w, qg, kdec, attn, elast, state, bdots):
    nn, _, tn = bdots
    v_new = u - nn(w, state)
    o = nn(qg, state) + nn(attn, v_new)
    return o, state * elast + tn(kdec, v_new)


def _gdn_local_fwd(post, proj, alog_row, dtb_row):
    t = post.shape[0]
    n_chunks = t // CHUNK
    hb = LOCAL_HEADS_PER_STEP

    def body(qkv_ref, ba_ref, al_ref, dt_ref, u_ref, w_ref, qg_ref, kd_ref, at_ref, el_ref):
        u, w, qg, kdec, attn, elast = _gdn_local_fn(qkv_ref[...], ba_ref[...], al_ref[...], dt_ref[...],
                                                    pl.program_id(1) * hb, _BDOT_BATCH_PLAIN, _FDOT_BATCH_PLAIN)
        for i in range(hb):
            cols = slice(i * HEAD_DIM, (i + 1) * HEAD_DIM)
            u_ref[:, cols] = u[i]
            w_ref[:, cols] = w[i].astype(BF16)
            qg_ref[:, cols] = qg[i].astype(BF16)
            kd_ref[:, cols] = kdec[i].astype(BF16)
        at_ref[...] = attn.astype(BF16)
        el_ref[:, 0] = elast

    wide = pl.BlockSpec((CHUNK, hb * HEAD_DIM), lambda n, j: (n, j))
    row = pl.BlockSpec((1, LANE), lambda n, j: (0, 0))
    return pl.pallas_call(
        body, grid=(n_chunks, GDN_HEADS // hb),
        in_specs=[pl.BlockSpec((CHUNK, hb * 3 * HEAD_DIM), lambda n, j: (n, j)),
                  pl.BlockSpec((CHUNK, LANE), lambda n, j: (n, BA_BLK)), row, row],
        out_specs=[wide, wide, wide, wide, pl.BlockSpec((hb, CHUNK, CHUNK), lambda n, j: (j, n, 0)),
                   pl.BlockSpec((hb, 1, 1, LANE), lambda n, j: (j, n, 0, 0))],
        out_shape=[jax.ShapeDtypeStruct((t, GDN_WIDTH), F32), jax.ShapeDtypeStruct((t, GDN_WIDTH), BF16),
                   jax.ShapeDtypeStruct((t, GDN_WIDTH), BF16), jax.ShapeDtypeStruct((t, GDN_WIDTH), BF16),
                   jax.ShapeDtypeStruct((GDN_HEADS, t, CHUNK), BF16),
                   jax.ShapeDtypeStruct((GDN_HEADS, n_chunks, 1, LANE), F32)],
        compiler_params=_params("parallel", "parallel"), name="gdn_local_fwd",
    )(post, proj, alog_row, dtb_row)


def _gdn_state_specs(n_of):
    wide = pl.BlockSpec((CHUNK, GDN_WIDTH), lambda n: (n_of(n), 0))
    attn = pl.BlockSpec((GDN_HEADS, CHUNK, CHUNK), lambda n: (0, n_of(n), 0))
    elast = pl.BlockSpec((GDN_HEADS, 1, 1, LANE), lambda n: (0, n_of(n), 0, 0))
    saved = pl.BlockSpec((GDN_HEADS, 1, HEAD_DIM, HEAD_DIM), lambda n: (0, n_of(n), 0, 0))
    return wide, attn, elast, saved


def _gdn_state_fwd(u, w, qg, kdec, attn, elast):
    t = u.shape[0]
    n_chunks = t // CHUNK

    def body(u_ref, w_ref, qg_ref, kd_ref, at_ref, el_ref, o_ref, save_ref, state_ref):
        @pl.when(pl.program_id(0) == 0)
        def _():
            state_ref[...] = jnp.zeros_like(state_ref)

        for h in range(GDN_HEADS):
            cols = slice(h * HEAD_DIM, (h + 1) * HEAD_DIM)
            state = state_ref[h]
            save_ref[h, 0] = state
            o, new_state = _gdn_state_fn(u_ref[:, cols], w_ref[:, cols], qg_ref[:, cols], kd_ref[:, cols], at_ref[h],
                                         el_ref[h, 0], state, _BDOT_PLAIN)
            o_ref[:, cols] = o
            state_ref[h] = new_state

    wide, attn_spec, elast_spec, saved_spec = _gdn_state_specs(lambda n: n)
    return pl.pallas_call(
        body, grid=(n_chunks,), in_specs=[wide, wide, wide, wide, attn_spec, elast_spec],
        out_specs=[wide, saved_spec],
        out_shape=[jax.ShapeDtypeStruct((t, GDN_WIDTH), F32),
                   jax.ShapeDtypeStruct((GDN_HEADS, n_chunks, HEAD_DIM, HEAD_DIM), F32)],
        scratch_shapes=[pltpu.VMEM((GDN_HEADS, HEAD_DIM, HEAD_DIM), F32)],
        compiler_params=_params("arbitrary"), name="gdn_state_fwd",
    )(u, w, qg, kdec, attn, elast)


def _gdn_state_bwd(u, w, qg, kdec, attn, elast, saved, do):
    t = u.shape[0]
    n_chunks = t // CHUNK
    last = n_chunks - 1

    def body(u_ref, w_ref, qg_ref, kd_ref, at_ref, el_ref, save_ref, do_ref,
             du_ref, dw_ref, dqg_ref, dkd_ref, dat_ref, del_ref, dstate_ref):
        @pl.when(pl.program_id(0) == 0)
        def _():
            dstate_ref[...] = jnp.zeros_like(dstate_ref)

        for h in range(GDN_HEADS):
            cols = slice(h * HEAD_DIM, (h + 1) * HEAD_DIM)
            _, vjp = jax.vjp(
                lambda *a: _gdn_state_fn(*a, _BDOT_VJP), u_ref[:, cols], w_ref[:, cols].astype(F32),
                qg_ref[:, cols].astype(F32), kd_ref[:, cols].astype(F32), at_ref[h].astype(F32), el_ref[h, 0], save_ref[h, 0])
            du, dw, dqg, dkd, dat, de, dstate = vjp((do_ref[:, cols], dstate_ref[h]))
            du_ref[:, cols] = du
            dw_ref[:, cols] = dw
            dqg_ref[:, cols] = dqg
            dkd_ref[:, cols] = dkd
            dat_ref[h] = dat
            del_ref[h, 0] = de
            dstate_ref[h] = dstate

    wide, attn_spec, elast_spec, saved_spec = _gdn_state_specs(lambda n: last - n)
    wide_f32 = jax.ShapeDtypeStruct((t, GDN_WIDTH), F32)
    return pl.pallas_call(
        body, grid=(n_chunks,), in_specs=[wide, wide, wide, wide, attn_spec, elast_spec, saved_spec, wide],
        out_specs=[wide, wide, wide, wide, attn_spec, elast_spec],
        out_shape=[wide_f32, wide_f32, wide_f32, wide_f32, jax.ShapeDtypeStruct((GDN_HEADS, t, CHUNK), F32),
                   jax.ShapeDtypeStruct((GDN_HEADS, n_chunks, 1, LANE), F32)],
        scratch_shapes=[pltpu.VMEM((GDN_HEADS, HEAD_DIM, HEAD_DIM), F32)],
        compiler_params=_params("arbitrary"), name="gdn_state_bwd",
    )(u, w, qg, kdec, attn, elast, saved, do)


def _gdn_local_bwd(post, proj, alog_row, dtb_row, cots, dproj):
    t = post.shape[0]
    n_chunks = t // CHUNK
    hb = LOCAL_HEADS_PER_STEP
    n_steps = GDN_HEADS // hb

    def body(qkv_ref, ba_ref, al_ref, dt_ref, du_ref, dw_ref, dqg_ref, dkd_ref, dat_ref, del_ref, _,
             dqkv_ref, dba_ref, dal_ref, ddt_ref, dba_acc):
        n = pl.program_id(0)
        j = pl.program_id(1)

        @pl.when((n == 0) & (j == 0))
        def _():
            dal_ref[...] = jnp.zeros_like(dal_ref)
            ddt_ref[...] = jnp.zeros_like(ddt_ref)

        @pl.when(j == 0)
        def _():
            dba_acc[...] = jnp.zeros_like(dba_acc)

        heads = lambda ref: jnp.stack([ref[:, i * HEAD_DIM:(i + 1) * HEAD_DIM] for i in range(hb)])
        _, vjp = jax.vjp(lambda a, b, c, d: _gdn_local_fn(a, b, c, d, j * hb, _BDOT_BATCH_VJP, _FDOT_BATCH_VJP),
                         qkv_ref[...], ba_ref[...], al_ref[...], dt_ref[...])
        dqkv, dba, dal, ddt = vjp((heads(du_ref), heads(dw_ref), heads(dqg_ref), heads(dkd_ref), dat_ref[...],
                                   del_ref[:, 0]))
        dqkv_ref[...] = dqkv
        dba_acc[...] += dba
        dal_ref[...] += dal
        ddt_ref[...] += ddt

        @pl.when(j == n_steps - 1)
        def _():
            dba_ref[:, 0:LANE] = dba_acc[...].astype(dba_ref.dtype)
            dba_ref[:, LANE:2 * LANE] = jnp.zeros((CHUNK, LANE), dba_ref.dtype)

    wide = pl.BlockSpec((CHUNK, hb * HEAD_DIM), lambda n, j: (n, j))
    qkv_spec = pl.BlockSpec((CHUNK, hb * 3 * HEAD_DIM), lambda n, j: (n, j))
    row = pl.BlockSpec((1, LANE), lambda n, j: (0, 0))
    return pl.pallas_call(
        body, grid=(n_chunks, n_steps),
        in_specs=[qkv_spec, pl.BlockSpec((CHUNK, LANE), lambda n, j: (n, BA_BLK)), row, row, wide, wide, wide, wide,
                  pl.BlockSpec((hb, CHUNK, CHUNK), lambda n, j: (j, n, 0)),
                  pl.BlockSpec((hb, 1, 1, LANE), lambda n, j: (j, n, 0, 0)), pl.BlockSpec(memory_space=pl.ANY)],
        out_specs=[qkv_spec, pl.BlockSpec((CHUNK, 2 * LANE), lambda n, j: (n, BA_BLK // 2)), row, row],
        out_shape=[jax.ShapeDtypeStruct((t, QKV_COLS), F32), jax.ShapeDtypeStruct(dproj.shape, dproj.dtype),
                   jax.ShapeDtypeStruct((1, LANE), F32), jax.ShapeDtypeStruct((1, LANE), F32)],
        input_output_aliases={10: 1},
        scratch_shapes=[pltpu.VMEM((CHUNK, LANE), F32)],
        compiler_params=_params("arbitrary", "arbitrary"), name="gdn_local_bwd",
    )(post, proj, alog_row, dtb_row, *cots, dproj)


def _onorm_fn(o, z, w):
    return o * lax.rsqrt(jnp.mean(o * o, axis=1, keepdims=True) + NORM_EPS) * w * (z * jax.nn.sigmoid(z))


def _onorm_fwd(o_raw, proj, norm_w, mixin, tm=512):
    t = o_raw.shape[0]
    tm = min(tm, t)

    def body(o_ref, z_ref, w_ref, _, out_ref):
        out_ref[...] = _onorm_fn(o_ref[...], z_ref[...], w_ref[...]).astype(out_ref.dtype)

    return pl.pallas_call(
        body, grid=(t // tm, GDN_HEADS),
        in_specs=[pl.BlockSpec((tm, LANE), lambda i, h: (i, h)), pl.BlockSpec((tm, LANE), lambda i, h: (i, Z_BLK + h)),
                  pl.BlockSpec((1, LANE), lambda i, h: (0, 0)), pl.BlockSpec(memory_space=pl.ANY)],
        out_specs=pl.BlockSpec((tm, LANE), lambda i, h: (i, h)),
        out_shape=jax.ShapeDtypeStruct(mixin.shape, mixin.dtype), input_output_aliases={3: 0},
        compiler_params=_params("parallel", "parallel"), name="gdn_onorm_fwd",
    )(o_raw, proj, norm_w, mixin)


def _onorm_bwd(o_raw, proj, norm_w, dmixin, dproj, tm=512):
    t = o_raw.shape[0]
    tm = min(tm, t)

    def body(o_ref, z_ref, w_ref, d_ref, _, do_ref, dz_ref, dw_ref):
        @pl.when((pl.program_id(0) == 0) & (pl.program_id(1) == 0))
        def _():
            dw_ref[...] = jnp.zeros_like(dw_ref)

        _, vjp = jax.vjp(_onorm_fn, o_ref[...], z_ref[...], w_ref[...])
        do, dz, dw = vjp(d_ref[...])
        do_ref[...] = do
        dz_ref[...] = dz.astype(dz_ref.dtype)
        dw_ref[...] += dw

    return pl.pallas_call(
        body, grid=(t // tm, GDN_HEADS),
        in_specs=[pl.BlockSpec((tm, LANE), lambda i, h: (i, h)), pl.BlockSpec((tm, LANE), lambda i, h: (i, Z_BLK + h)),
                  pl.BlockSpec((1, LANE), lambda i, h: (0, 0)), pl.BlockSpec((tm, LANE), lambda i, h: (i, h)),
                  pl.BlockSpec(memory_space=pl.ANY)],
        out_specs=[pl.BlockSpec((tm, LANE), lambda i, h: (i, h)), pl.BlockSpec((tm, LANE), lambda i, h: (i, Z_BLK + h)),
                   pl.BlockSpec((1, LANE), lambda i, h: (0, 0))],
        out_shape=[jax.ShapeDtypeStruct((t, GDN_WIDTH), F32), jax.ShapeDtypeStruct(dproj.shape, dproj.dtype),
                   jax.ShapeDtypeStruct((1, LANE), F32)],
        input_output_aliases={4: 1},
        compiler_params=_params("arbitrary", "arbitrary"), name="gdn_onorm_bwd",
    )(o_raw, proj, norm_w, dmixin, dproj)


def _pool_select(levels, gi):
    out = levels[-1]
    for lvl in range(len(levels) - 2, -1, -1):
        out = jnp.where(gi == lvl, levels[lvl], out)
    return out


def _pool_count(shape, gi):
    pos = lax.broadcasted_iota(jnp.int32, shape, 0)
    win = lax.shift_left(jnp.int32(2), gi)
    return jnp.minimum(pos + 1, win).astype(F32)


def _pooled(p, gi):
    acc = p
    levels = []
    for lvl in range(POOL_GROUPS):
        acc = acc + _shift_down(acc, 1 << lvl)
        levels.append(acc)
    return _pool_select(levels, gi) / _pool_count(p.shape, gi) - p


def _pool_fwd(proj, pool_w, pool_scale):
    t = proj.shape[0]

    def body(p_ref, w_ref, s_ref, out_ref):
        gi = pl.program_id(0)
        pooled = _pooled(p_ref[...], gi)
        out_ref[...] = (_BDOT_PLAIN[0](pooled, w_ref[0]) * s_ref[0]).astype(out_ref.dtype)

    return pl.pallas_call(
        body, grid=(POOL_GROUPS,),
        in_specs=[pl.BlockSpec((t, POOL_GROUP_DIM), lambda g: (0, POOL_BLK + g)),
                  pl.BlockSpec((1, POOL_GROUP_DIM, POOL_GROUP_DIM), lambda g: (g, 0, 0)),
                  pl.BlockSpec((1, 1, POOL_GROUP_DIM), lambda g: (g, 0, 0))],
        out_specs=pl.BlockSpec((t, POOL_GROUP_DIM), lambda g: (0, GDN_WIDTH // POOL_GROUP_DIM + g)),
        out_shape=jax.ShapeDtypeStruct((t, 2 * GDN_WIDTH), BF16),
        compiler_params=_params("parallel"), name="pool_fwd",
    )(proj, pool_w, pool_scale)


def _pool_bwd(proj, pool_w, pool_scale, dmixin):
    t = proj.shape[0]
    nn, nt, tn = _BDOT_PLAIN

    def body(p_ref, w_ref, s_ref, d_ref, dp_ref, dw_ref, ds_ref):
        gi = pl.program_id(0)
        p = p_ref[...]
        pooled = _pooled(p, gi)
        mixed = nn(pooled, w_ref[0])
        d = d_ref[...]
        ds_ref[0] = jnp.sum(d * mixed, axis=0, keepdims=True)
        dmixed = d * s_ref[0]
        dw_ref[0] = tn(pooled, dmixed)
        dpooled = nt(dmixed, w_ref[0])
        acc = dpooled / _pool_count(p.shape, gi)
        levels = []
        for lvl in range(POOL_GROUPS):
            acc = acc + _shift_up(acc, 1 << lvl)
            levels.append(acc)
        dp_ref[...] = (_pool_select(levels, gi) - dpooled).astype(dp_ref.dtype)

    return pl.pallas_call(
        body, grid=(POOL_GROUPS,),
        in_specs=[pl.BlockSpec((t, POOL_GROUP_DIM), lambda g: (0, POOL_BLK + g)),
                  pl.BlockSpec((1, POOL_GROUP_DIM, POOL_GROUP_DIM), lambda g: (g, 0, 0)),
                  pl.BlockSpec((1, 1, POOL_GROUP_DIM), lambda g: (g, 0, 0)),
                  pl.BlockSpec((t, POOL_GROUP_DIM), lambda g: (0, GDN_WIDTH // POOL_GROUP_DIM + g))],
        out_specs=[pl.BlockSpec((t, POOL_GROUP_DIM), lambda g: (0, POOL_BLK + g)),
                   pl.BlockSpec((1, POOL_GROUP_DIM, POOL_GROUP_DIM), lambda g: (g, 0, 0)),
                   pl.BlockSpec((1, 1, POOL_GROUP_DIM), lambda g: (g, 0, 0))],
        out_shape=[jax.ShapeDtypeStruct((t, PROJ_COLS), BF16),
                   jax.ShapeDtypeStruct((POOL_GROUPS, POOL_GROUP_DIM, POOL_GROUP_DIM), F32),
                   jax.ShapeDtypeStruct((POOL_GROUPS, 1, POOL_GROUP_DIM), F32)],
        compiler_params=_params("parallel"), name="pool_bwd",
    )(proj, pool_w, pool_scale, dmixin)


def _ln_stats(s):
    mu = jnp.mean(s, axis=1, keepdims=True)
    xc = s - mu
    var = jnp.mean(xc * xc, axis=1, keepdims=True)
    rstd = lax.rsqrt(var + LN_EPS)
    return xc * rstd, rstd


def _ln_fwd(h_in, y, g, b, *, name, tm=256):
    t, d = h_in.shape
    tm = min(tm, t)

    def body(h_ref, y_ref, g_ref, b_ref, o_ref, o16_ref):
        xhat, _ = _ln_stats(ALPHA * h_ref[...] + y_ref[...])
        out = xhat * g_ref[...] + b_ref[...]
        o_ref[...] = out
        o16_ref[...] = out.astype(BF16)

    row = pl.BlockSpec((tm, d), lambda i: (i, 0))
    vec = pl.BlockSpec((1, d), lambda i: (0, 0))
    return pl.pallas_call(
        body, grid=(t // tm,), in_specs=[row, row, vec, vec], out_specs=[row, row],
        out_shape=[jax.ShapeDtypeStruct((t, d), F32), jax.ShapeDtypeStruct((t, d), BF16)],
        compiler_params=_params("parallel"), name=name,
    )(h_in, y, g, b)


def _ln_loss_fwd(h_in, y, g, b, target, *, name, tm=256):
    t, d = h_in.shape
    tm = min(tm, t)

    def body(h_ref, y_ref, g_ref, b_ref, t_ref, dy_ref, sq_ref):
        @pl.when(pl.program_id(0) == 0)
        def _():
            sq_ref[...] = jnp.zeros_like(sq_ref)

        xhat, _ = _ln_stats(ALPHA * h_ref[...] + y_ref[...])
        err = xhat * g_ref[...] + b_ref[...] - t_ref[...]
        dy_ref[...] = err * (1.0 / d)
        sq_ref[...] += jnp.sum(jnp.sum(err * err, axis=1, keepdims=True), axis=0, keepdims=True)

    row = pl.BlockSpec((tm, d), lambda i: (i, 0))
    vec = pl.BlockSpec((1, d), lambda i: (0, 0))
    return pl.pallas_call(
        body, grid=(t // tm,), in_specs=[row, row, vec, vec, row],
        out_specs=[row, pl.BlockSpec((1, LANE), lambda i: (0, 0))],
        out_shape=[jax.ShapeDtypeStruct((t, d), F32), jax.ShapeDtypeStruct((1, LANE), F32)],
        compiler_params=_params("arbitrary"), name=name,
    )(h_in, y, g, b, target)


def _ln_bwd(h_in, y, g, d_a, d_b, *, name, tm=256):
    t, d = h_in.shape
    tm = min(tm, t)
    has_b = d_b is not None

    def body(*refs):
        if has_b:
            h_ref, y_ref, g_ref, da_ref, db_ref, ds_ref, ds16_ref, dg_ref, dbias_ref = refs
        else:
            h_ref, y_ref, g_ref, da_ref, ds_ref, ds16_ref, dg_ref, dbias_ref = refs

        @pl.when(pl.program_id(0) == 0)
        def _():
            dg_ref[...] = jnp.zeros_like(dg_ref)
            dbias_ref[...] = jnp.zeros_like(dbias_ref)

        xhat, rstd = _ln_stats(ALPHA * h_ref[...] + y_ref[...])
        dout = da_ref[...]
        if has_b:
            dout = dout + ALPHA * db_ref[...]
        dxhat = dout * g_ref[...]
        m1 = jnp.mean(dxhat, axis=1, keepdims=True)
        m2 = jnp.mean(dxhat * xhat, axis=1, keepdims=True)
        ds = rstd * (dxhat - m1 - xhat * m2)
        ds_ref[...] = ds
        ds16_ref[...] = ds.astype(BF16)
        dg_ref[...] += jnp.sum(dout * xhat, axis=0, keepdims=True)
        dbias_ref[...] += jnp.sum(dout, axis=0, keepdims=True)

    row = pl.BlockSpec((tm, d), lambda i: (i, 0))
    vec = pl.BlockSpec((1, d), lambda i: (0, 0))
    args = [h_in, y, g, d_a] + ([d_b] if has_b else [])
    return pl.pallas_call(
        body, grid=(t // tm,), in_specs=[row, row, vec, row] + ([row] if has_b else []),
        out_specs=[row, row, vec, vec],
        out_shape=[jax.ShapeDtypeStruct((t, d), F32), jax.ShapeDtypeStruct((t, d), BF16),
                   jax.ShapeDtypeStruct((1, d), F32), jax.ShapeDtypeStruct((1, d), F32)],
        compiler_params=_params("arbitrary"), name=name,
    )(*args)


def _attn_fn(q, k, v, dots):
    nn, nt, _ = dots
    s = nt(q, k) * (XATTN_HEAD_DIM ** -0.5)
    s = s - lax.stop_gradient(jnp.max(s, axis=1, keepdims=True))
    e = jnp.exp(s)
    p = e / jnp.sum(e, axis=1, keepdims=True)
    return nn(p, v)


def _attn_fwd(q, k, v, tq=512):
    t = q.shape[0]
    tq = min(tq, t)

    def body(q_ref, k_ref, v_ref, o_ref):
        o_ref[...] = _attn_fn(q_ref[...], k_ref[...], v_ref[...], _BDOT_PLAIN).astype(BF16)

    qs = pl.BlockSpec((tq, XATTN_HEAD_DIM), lambda h, i: (i, h))
    ks = pl.BlockSpec((MEM_LEN, XATTN_HEAD_DIM), lambda h, i: (0, h))
    return pl.pallas_call(
        body, grid=(XATTN_HEADS, t // tq), in_specs=[qs, ks, ks], out_specs=qs,
        out_shape=jax.ShapeDtypeStruct(q.shape, BF16), compiler_params=_params("parallel", "parallel"), name="xattn_fwd",
    )(q, k, v)


def _attn_bwd(q, k, v, do, tq=512):
    t = q.shape[0]
    tq = min(tq, t)

    def body(q_ref, k_ref, v_ref, do_ref, dq_ref, dk_ref, dv_ref):
        @pl.when(pl.program_id(1) == 0)
        def _():
            dk_ref[...] = jnp.zeros_like(dk_ref)
            dv_ref[...] = jnp.zeros_like(dv_ref)

        _, vjp = jax.vjp(lambda a, b, c: _attn_fn(a, b, c, _BDOT_VJP), q_ref[...].astype(F32), k_ref[...].astype(F32),
                         v_ref[...].astype(F32))
        dq, dk, dv = vjp(do_ref[...].astype(F32))
        dq_ref[...] = dq.astype(BF16)
        dk_ref[...] += dk
        dv_ref[...] += dv

    qs = pl.BlockSpec((tq, XATTN_HEAD_DIM), lambda h, i: (i, h))
    ks = pl.BlockSpec((MEM_LEN, XATTN_HEAD_DIM), lambda h, i: (0, h))
    return pl.pallas_call(
        body, grid=(XATTN_HEADS, t // tq), in_specs=[qs, ks, ks, qs], out_specs=[qs, ks, ks],
        out_shape=[jax.ShapeDtypeStruct(q.shape, BF16), jax.ShapeDtypeStruct(k.shape, F32), jax.ShapeDtypeStruct(v.shape, F32)],
        compiler_params=_params("parallel", "arbitrary"), name="xattn_bwd",
    )(q, k, v, do)


def _local_step(x, x16, mem, target, weights_of, grads_ready):
    def behind(vec, token):
        return vec if token is None else vec + token

    w = dict(weights_of("mixer", None))
    proj = _mm(x16, w["w_in"], tb=True, tn=768, name="mm_in_proj")
    post = _gdn_prep_fwd(proj, w["conv_w"])
    mixin = _pool_fwd(proj, w["pool_w"], w["pool_scale"])
    token = weights_of("ahead", mixin)
    chunked = _gdn_local_fwd(post, proj, behind(w["alog_row"], token), w["dtb_row"])
    o_raw, saved = _gdn_state_fwd(*chunked)
    mixin = _onorm_fwd(o_raw, proj, w["gdn_norm_w"], mixin)
    w.update(weights_of("attn", mixin))
    mix = _mm(mixin, w["w_out"], name="mm_out_proj")
    h1, h1_16 = _ln_fwd(x, mix, w["ln1_g"], w["ln1_b"], name="ln1_fwd")
    xq = _mm(h1_16, w["xq_w"], out_dtype=BF16, name="mm_xq")
    xk = _mm(mem, w["xk_w"], out_dtype=BF16, name="mm_xk")
    xv = _mm(mem, w["xv_w"], out_dtype=BF16, name="mm_xv")
    xo = _attn_fwd(xq, xk, xv)
    xa = _mm(xo, w["xo_w"], name="mm_xo")
    h2, h2_16 = _ln_fwd(h1, xa, w["ln2_g"], w["ln2_b"], name="ln2_fwd")
    w.update(weights_of("up", h2_16))
    act, relu = _mm(h2_16, w["w_up"], b_chunks=True, epi="relu2", name="mm_up")
    w.update(weights_of("down", act))
    ff = _mm(act, w["w_down"], tn=1024, tk=1024, name="mm_down")
    dy, sq = _ln_loss_fwd(h2, ff, w["ln3_g"], w["ln3_b"], target, name="ln3_loss_fwd")

    g = {}
    ds3, ds3_16, g["ln3_g"], g["ln3_b"] = _ln_bwd(h2, ff, w["ln3_g"], dy, None, name="ln3_bwd")
    gw_down = _mm(act, ds3_16, ta=True, out_dtype=BF16, tm=512, tn=D_MODEL, name="mm_gw_down")
    du = _mm(ds3_16, w["w_down"], tb=True, epi="mul2r", extra=relu, name="mm_du")
    gw_up = _mm(h2_16, du, ta=True, out_dtype=BF16, o_chunks=True, name="mm_gw_up")
    token = grads_ready("mlp", {"w_down": gw_down, "w_up": gw_up})
    dh2 = _mm(du, w["w_up"], tb=True, b_chunks=True, tn=1024, tk=1024, name="mm_dh2")
    ds2, ds2_16, g["ln2_g"], g["ln2_b"] = _ln_bwd(h1, xa, behind(w["ln2_g"], token), dh2, ds3, name="ln2_bwd")
    gw_xo = _mm(xo, ds2_16, ta=True, out_dtype=BF16, name="mm_gw_xo")
    dxo = _mm(ds2_16, w["xo_w"], tb=True, out_dtype=BF16, name="mm_dxo")
    dxq, dxk, dxv = _attn_bwd(xq, xk, xv, dxo)
    gw_xq = _mm(h1_16, dxq, ta=True, out_dtype=BF16, name="mm_gw_xq")
    gw_xk = _mm(mem, dxk, ta=True, out_dtype=BF16, name="mm_gw_xk")
    gw_xv = _mm(mem, dxv, ta=True, out_dtype=BF16, name="mm_gw_xv")
    token = grads_ready("attn", {"xo_w": gw_xo, "xq_w": gw_xq, "xk_w": gw_xk, "xv_w": gw_xv})
    dh1 = _mm(dxq, w["xq_w"], tb=True, name="mm_dh1")
    ds1, ds1_16, g["ln1_g"], g["ln1_b"] = _ln_bwd(x, mix, behind(w["ln1_g"], token), dh1, ds2, name="ln1_bwd")
    gw_out = _mm(mixin, ds1_16, ta=True, out_dtype=BF16, name="mm_gw_out")
    dmixin = _mm(ds1_16, w["w_out"], tb=True, name="mm_dmixin")
    dproj, gw_pool, g["pool_scale"] = _pool_bwd(proj, w["pool_w"], w["pool_scale"], dmixin)
    token = grads_ready("mix", {"w_out": gw_out, "pool_w": gw_pool})
    do_raw, dproj, g["gdn_norm_w"] = _onorm_bwd(o_raw, proj, behind(w["gdn_norm_w"], token), dmixin, dproj)
    cots = _gdn_state_bwd(*chunked, saved, do_raw)
    token = grads_ready("tick", {"after": cots[0]})
    dpost, dproj, g["alog_row"], g["dtb_row"] = _gdn_local_bwd(post, proj, behind(w["alog_row"], token), w["dtb_row"],
                                                               cots, dproj)
    dproj, g["conv_w"] = _gdn_prep_bwd(proj, w["conv_w"], dpost, dproj)
    gw_in = _mm(dproj, x16, ta=True, out_dtype=BF16, tm=768, tn=D_MODEL, name="mm_gw_in")
    token = grads_ready("in", {"w_in": gw_in})
    if token is not None:
        ds1, _ = lax.optimization_barrier((ds1, token))
    grad_x = _mm(dproj, w["w_in"], tk=768, epi="add", extra=ds1, add_scale=ALPHA, name="mm_dx")
    return sq, grad_x, g


_MATRICES = ("w_in", "pool_w", "w_out", "xq_w", "xk_w", "xv_w", "xo_w", "w_up", "w_down")
_VECTORS = ("a_log", "dt_bias", "gdn_norm_w", "pool_scale", "ln1_g", "ln1_b", "ln2_g", "ln2_b", "ln3_g", "ln3_b")
_BA_SPLIT = BA_OFF + 2 * GDN_HEADS


def _lane_row(v, offset):
    return jnp.zeros((1, LANE), F32).at[0, offset:offset + v.shape[0]].set(v)


_GROUP_VECTORS = {"mixer": (), "attn": ("ln1_g", "ln1_b", "ln2_g", "ln2_b"), "up": (), "down": ("ln3_g", "ln3_b")}


def _group_weights(group, full):
    w = {n: full[n].reshape(1, D_MODEL) for n in _GROUP_VECTORS[group]}
    if group == "mixer":
        w_in = full["w_in"].reshape(IN_COLS, D_MODEL)
        zeros = jnp.zeros((POOL_OFF - _BA_SPLIT, D_MODEL), w_in.dtype)
        w.update({
            "w_in": jnp.concatenate([w_in[:_BA_SPLIT], zeros, w_in[_BA_SPLIT:]], axis=0),
            "conv_w": full["conv_w"],
            "alog_row": _lane_row(full["a_log"], GDN_HEADS),
            "dtb_row": _lane_row(full["dt_bias"], GDN_HEADS),
            "gdn_norm_w": full["gdn_norm_w"].reshape(1, LANE),
            "pool_w": full["pool_w"],
            "pool_scale": full["pool_scale"].reshape(POOL_GROUPS, 1, POOL_GROUP_DIM),
        })
    else:
        w.update({n: full[n] for n in dict(_GATHER_GROUPS)[group]})
    return w


def _w_in_chunks(g):
    unpadded = jnp.concatenate([g[:_BA_SPLIT], g[POOL_OFF:]], axis=0)
    return unpadded.reshape(N_DEV, IN_COLS // N_DEV, D_MODEL)


def _finish_small_grads(g):
    out = {"conv_w": g["conv_w"]}
    out["a_log"] = g["alog_row"][0, GDN_HEADS:2 * GDN_HEADS]
    out["dt_bias"] = g["dtb_row"][0, GDN_HEADS:2 * GDN_HEADS]
    out["gdn_norm_w"] = g["gdn_norm_w"].reshape(LANE)
    out["pool_scale"] = g["pool_scale"].reshape(POOL_GROUPS * POOL_GROUP_DIM)
    for n in ("ln1_g", "ln1_b", "ln2_g", "ln2_b", "ln3_g", "ln3_b"):
        out[n] = g[n].reshape(D_MODEL)
    return out


def _adamw_math(w, g, m, v):
    m = ADAM_B1 * m + (1.0 - ADAM_B1) * g
    v = ADAM_B2 * v + (1.0 - ADAM_B2) * (g * g)
    m_hat = m / (1.0 - ADAM_B1 ** ADAM_STEP)
    v_hat = v / (1.0 - ADAM_B2 ** ADAM_STEP)
    delta = -ADAM_LR * (m_hat / (jnp.sqrt(v_hat) + ADAM_EPS) + ADAM_WD * w)
    return delta, m, v


def _shard_tile(r, c):
    if r % 128 == 0:
        return 128, c
    return r, 256 if c % 256 == 0 else c


def _adamw_shard(parts, own, me, w, m, v, *, name):
    s, r, c = parts.shape
    tr, tc = _shard_tile(r, c)
    assert r % tr == 0 and c % tc == 0, (name, r, c)
    unit_axis = w.ndim == 3
    at = (slice(None), 0, slice(None)) if unit_axis else Ellipsis

    def body(me_ref, p_ref, own_ref, w_ref, m_ref, v_ref, g_ref, d_ref, nm_ref, nv_ref):
        mine = own_ref[...].astype(F32)
        g = None
        for i in range(s):
            part = jnp.where(me_ref[0] == i, mine, p_ref[i].astype(F32))
            g = part if g is None else g + part
        delta, nm, nv = _adamw_math(w_ref[at], g, m_ref[at], v_ref[at])
        g_ref[at] = g
        d_ref[at] = delta
        nm_ref[at] = nm
        nv_ref[at] = nv

    if unit_axis:
        blk = pl.BlockSpec((tr, 1, tc), lambda i, j, me_ref: (i, 0, j))
        out = jax.ShapeDtypeStruct((r, 1, c), F32)
    else:
        blk = pl.BlockSpec((tr, tc), lambda i, j, me_ref: (i, j))
        out = jax.ShapeDtypeStruct((r, c), F32)
    return pl.pallas_call(
        body,
        grid_spec=pltpu.PrefetchScalarGridSpec(
            num_scalar_prefetch=1, grid=(r // tr, c // tc),
            in_specs=[pl.BlockSpec((s, tr, tc), lambda i, j, me_ref: (0, i, j)),
                      pl.BlockSpec((None, tr, tc), lambda i, j, me_ref: (me_ref[0], i, j)), blk, blk, blk],
            out_specs=[blk, blk, blk, blk]),
        out_shape=[out, out, out, out], compiler_params=_params("parallel", "parallel"), name=name,
    )(me, parts, own, w, m, v)


N_CHIPS = N_DEV // 2


def _chip_sums(chunks, from_sibling, core, *, name):
    _, r, c = chunks.shape
    tr, tc = _shard_tile(r, c)
    assert r % tr == 0 and c % tc == 0, (name, r, c)

    def body(core_ref, mine_ref, other_ref, o_ref):
        o_ref[...] = (mine_ref[...].astype(F32) + other_ref[...].astype(F32)).astype(o_ref.dtype)

    by_chip = pl.BlockSpec((None, tr, tc), lambda q, i, j, core_ref: (q, i, j))
    return pl.pallas_call(
        body,
        grid_spec=pltpu.PrefetchScalarGridSpec(
            num_scalar_prefetch=1, grid=(N_CHIPS, r // tr, c // tc),
            in_specs=[pl.BlockSpec((None, tr, tc), lambda q, i, j, core_ref: (2 * q + core_ref[0], i, j)), by_chip],
            out_specs=by_chip),
        out_shape=jax.ShapeDtypeStruct((N_CHIPS, r, c), chunks.dtype),
        compiler_params=_params("parallel", "parallel", "parallel"), name=name,
    )(core, chunks, from_sibling)


def _place():
    return lax.axis_index("x"), lax.axis_index("y"), lax.axis_index("c")


def _slot(px, py, pc):
    return 4 * px + 2 * py + pc


_HBM = pl.BlockSpec(memory_space=pltpu.HBM)


_SEM = pl.BlockSpec(memory_space=pltpu.SEMAPHORE)
_ANY = pl.BlockSpec(memory_space=pl.ANY)
_EFFECT = pltpu.SideEffectType.DATAFLOW_SIDE_EFFECTING
_N_PEERS = N_DEV - 1


def _peer(k, x, y, c):
    return (1 - x if k & 4 else x, 1 - y if k & 2 else y, 1 - c if k & 1 else c)


_EXCHANGE_BITS = {"gather_chips": (1, 2, 4, 6), "gather_pass": (2, 4, 6), "scatter_sibling": (1, 1, 1, 1),
                  "scatter_chips": (2, 4, 6)}


def _exchange_copy(mode, src, land, w, i, place, send_sems, recv_sems, receiving):
    bits = _EXCHANGE_BITS[mode]
    k = bits[i]
    peer = _peer(k, *place)
    me = _slot(*place)
    if mode == "gather_chips":
        to, src_ref, sent_to, got_at = peer, src[w], me, _slot(*peer)
    elif mode == "gather_pass":
        blk = _slot(*peer)
        to, src_ref, sent_to, got_at = _peer(1, *place), land[w].at[blk], blk, _slot(*_peer(k | 1, *place))
    elif mode == "scatter_sibling":
        to, src_ref, sent_to, got_at = peer, src[w].at[2 * i + 1 - place[2]], i, i
    else:
        to, src_ref, sent_to, got_at = peer, src[w].at[_slot(*peer) // 2], me // 2, _slot(*peer) // 2
    sem = w * len(bits) + i
    return pltpu.make_async_remote_copy(
        src_ref=src_ref, dst_ref=land[w].at[got_at if receiving else sent_to], send_sem=send_sems.at[sem],
        recv_sem=recv_sems.at[sem], device_id=to, device_id_type=MESH)


def _exchange_start(mode, srcs, lands, after, *, name):
    ns, nl = len(srcs), len(lands)
    n_sem = nl * len(_EXCHANGE_BITS[mode])

    def body(*refs):
        src, land = refs[:ns], refs[ns:ns + nl]
        send_sems, recv_sems = refs[ns + nl + 1:ns + nl + 3]
        token = refs[-1]
        place = _place()
        for w in range(nl):
            for i in range(len(_EXCHANGE_BITS[mode])):
                _exchange_copy(mode, src, land, w, i, place, send_sems, recv_sems, receiving=False).start()
        token[...] = jnp.zeros_like(token)

    sems = pltpu.SemaphoreType.DMA((n_sem,))
    arrays = list(srcs) + list(lands)
    res = pl.pallas_call(
        body, name=name, in_specs=[_HBM] * (ns + nl) + [_ANY],
        out_specs=(_SEM, _SEM, *([_HBM] * (ns + nl)), pl.BlockSpec(memory_space=pltpu.VMEM)),
        out_shape=(sems, sems, *[pltpu.HBM(a.shape, a.dtype) for a in arrays], jax.ShapeDtypeStruct((8, LANE), F32)),
        input_output_aliases={i: 2 + i for i in range(ns + nl)},
        compiler_params=pltpu.CompilerParams(has_side_effects=_EFFECT),
    )(*[pltpu.with_memory_space_constraint(a, pltpu.HBM) for a in arrays], after)
    return res[0], res[1], list(res[2:2 + ns]), list(res[2 + ns:2 + ns + nl]), res[-1]


def _exchange_wait(mode, started, after, *, name):
    send_sems, recv_sems, srcs, lands, _ = started
    ns, nl = len(srcs), len(lands)

    def body(*refs):
        src, land = refs[:ns], refs[ns:ns + nl]
        send_sems, recv_sems = refs[ns + nl:ns + nl + 2]
        place = _place()
        for w in range(nl):
            for i in range(len(_EXCHANGE_BITS[mode])):
                cp = _exchange_copy(mode, src, land, w, i, place, send_sems, recv_sems, receiving=True)
                cp.wait_send()
                cp.wait_recv()

    arrays = list(srcs) + list(lands)
    res = pl.pallas_call(
        body, name=name, in_specs=[_HBM] * (ns + nl) + [_SEM, _SEM, _ANY], out_specs=[_HBM] * (ns + nl),
        out_shape=[pltpu.HBM(a.shape, a.dtype) for a in arrays],
        input_output_aliases={i: i for i in range(ns + nl)},
        compiler_params=pltpu.CompilerParams(has_side_effects=_EFFECT),
    )(*arrays, send_sems, recv_sems, after)
    return list(res[:ns]), list(res[ns:])


def _small_allreduce_adamw(gvec, wvec, mvec, vvec):
    rows, length = gvec.shape

    def body(g_ref, w_ref, m_ref, v_ref, gs_ref, d_ref, nm_ref, nv_ref, slots, send_sems, recv_sems):
        x, y, c = _place()
        me = _slot(x, y, c)
        slots[me] = g_ref[...]
        sends = []
        for k in range(1, N_DEV):
            peer = _peer(k, x, y, c)
            sends.append(pltpu.make_async_remote_copy(
                src_ref=g_ref, dst_ref=slots.at[me], send_sem=send_sems.at[k - 1], recv_sem=recv_sems.at[k - 1],
                device_id=peer, device_id_type=MESH))
        for cp in sends:
            cp.start()
        for k in range(1, N_DEV):
            peer = _peer(k, x, y, c)
            pltpu.make_async_remote_copy(
                src_ref=g_ref, dst_ref=slots.at[_slot(*peer)], send_sem=send_sems.at[k - 1], recv_sem=recv_sems.at[k - 1],
                device_id=peer, device_id_type=MESH).wait_recv()
        for cp in sends:
            cp.wait_send()
        g = slots[0]
        for s in range(1, N_DEV):
            g = g + slots[s]
        delta, nm, nv = _adamw_math(w_ref[...], g, m_ref[...], v_ref[...])
        gs_ref[...] = g
        d_ref[...] = delta
        nm_ref[...] = nm
        nv_ref[...] = nv

    vmem = pl.BlockSpec(memory_space=pltpu.VMEM)
    out = jax.ShapeDtypeStruct((rows, length), F32)
    return pl.pallas_call(
        body, in_specs=[vmem] * 4, out_specs=[vmem] * 4, out_shape=[out] * 4,
        scratch_shapes=[pltpu.VMEM((N_DEV, rows, length), F32), pltpu.SemaphoreType.DMA((N_DEV - 1,)),
                        pltpu.SemaphoreType.DMA((N_DEV - 1,))],
        name="small_allreduce_adamw",
    )(gvec, wvec, mvec, vvec)


_SMALL_SEGMENTS = (("a_log", GDN_HEADS), ("dt_bias", GDN_HEADS), ("gdn_norm_w", HEAD_DIM), ("pool_scale", GDN_WIDTH),
                   ("ln1_g", D_MODEL), ("ln1_b", D_MODEL), ("ln2_g", D_MODEL), ("ln2_b", D_MODEL),
                   ("ln3_g", D_MODEL), ("ln3_b", D_MODEL), ("conv_w", CONV_K * QKV_COLS))
_SMALL_ROWS = 8
_SMALL_LEN = -(-sum(sz for _, sz in _SMALL_SEGMENTS) // (_SMALL_ROWS * LANE)) * LANE


def _pack_small(vals):
    parts = [vals[n].reshape(-1).astype(F32) if n in vals else jnp.zeros((sz,), F32) for n, sz in _SMALL_SEGMENTS]
    flat = jnp.concatenate(parts)
    flat = jnp.pad(flat, (0, _SMALL_ROWS * _SMALL_LEN - flat.shape[0]))
    return flat.reshape(_SMALL_ROWS, _SMALL_LEN)


def _unpack_small(vec):
    flat = vec.reshape(-1)
    out, off = {}, 0
    for n, sz in _SMALL_SEGMENTS:
        out[n] = flat[off:off + sz]
        off += sz
    return out


_WEIGHT_ORDER = ("w_in", "conv_w", "a_log", "dt_bias", "gdn_norm_w", "pool_w", "pool_scale", "w_out", "ln1_g", "ln1_b",
                 "xq_w", "xk_w", "xv_w", "xo_w", "ln2_g", "ln2_b", "w_up", "w_down", "ln3_g", "ln3_b")


def _shard2d(name, a):
    if name == "w_in":
        return a.T
    return a.reshape(-1, a.shape[-1]) if name == "pool_w" else a


def _update_view(name, a):
    return jnp.transpose(a, (2, 0, 1)) if name == "w_in" else _shard2d(name, a[0])


def _shard_result(name, r, shape):
    return jnp.transpose(r, (1, 2, 0)) if name == "w_in" else r.reshape(shape)


def _gathered_to_full(name, gth):
    if name in ("w_up", "w_in"):
        return gth
    if name == "conv_w":
        return jnp.transpose(gth, (1, 0, 2)).reshape(gth.shape[1], N_DEV * gth.shape[2])
    if name == "pool_w":
        g4 = gth.reshape(N_DEV, POOL_GROUPS, POOL_GROUP_DIM // N_DEV, POOL_GROUP_DIM)
        return jnp.transpose(g4, (1, 0, 2, 3)).reshape(POOL_GROUPS, POOL_GROUP_DIM, POOL_GROUP_DIM)
    return gth.reshape(N_DEV * gth.shape[1], gth.shape[2])


def _full_to_chunks(name, full):
    if name == "w_up":
        return full
    if name == "pool_w":
        g4 = full.reshape(POOL_GROUPS, N_DEV, POOL_GROUP_DIM // N_DEV, POOL_GROUP_DIM)
        return jnp.transpose(g4, (1, 0, 2, 3)).reshape(N_DEV, POOL_GROUPS * POOL_GROUP_DIM // N_DEV, POOL_GROUP_DIM)
    return full.reshape(N_DEV, full.shape[0] // N_DEV, full.shape[1])


_GATHER_GROUPS = (("mixer", ("w_in", "conv_w", "pool_w")), ("attn", ("w_out", "xq_w", "xk_w", "xv_w", "xo_w")),
                  ("up", ("w_up",)), ("down", ("w_down",)))


def _grad_chunks(name, g):
    if name == "w_in":
        return _w_in_chunks(g.astype(BF16))
    return _full_to_chunks(name, g.astype(BF16))


def kernel(x, mem, w_in, conv_w, a_log, dt_bias, gdn_norm_w, pool_w, pool_scale, w_out, ln1_g, ln1_b, xq_w, xk_w, xv_w, xo_w, ln2_g, ln2_b, w_up, w_down, ln3_g, ln3_b, loss_target, m_w_in, m_conv_w, m_a_log, m_dt_bias, m_gdn_norm_w, m_pool_w, m_pool_scale, m_w_out, m_ln1_g, m_ln1_b, m_xq_w, m_xk_w, m_xv_w, m_xo_w, m_ln2_g, m_ln2_b, m_w_up, m_w_down, m_ln3_g, m_ln3_b, v_w_in, v_conv_w, v_a_log, v_dt_bias, v_gdn_norm_w, v_pool_w, v_pool_scale, v_w_out, v_ln1_g, v_ln1_b, v_xq_w, v_xk_w, v_xv_w, v_xo_w, v_ln2_g, v_ln2_b, v_w_up, v_w_down, v_ln3_g, v_ln3_b):
    args = dict(locals())
    wt = {n: args[n][0] for n in _WEIGHT_ORDER}
    mo = {n: args["m_" + n][0] for n in _WEIGHT_ORDER}
    vo = {n: args["v_" + n][0] for n in _WEIGHT_ORDER}

    me = _slot(*_place())
    me_arr = jnp.reshape(me, (1,)).astype(jnp.int32)
    nothing = jnp.zeros((8, LANE), F32)

    def landing_zones(names):
        shards = [_shard2d(n, wt[n]).astype(F32 if n == "conv_w" else BF16) for n in names]
        zones = [lax.dynamic_update_slice(lax.empty((N_DEV, *s.shape), s.dtype), s[None], (me, 0, 0)) for s in shards]
        return shards, zones

    chip_arr = jnp.reshape(me // 2, (1,)).astype(jnp.int32)
    core_arr = jnp.reshape(lax.axis_index("c"), (1,)).astype(jnp.int32)
    names_of = dict(_GATHER_GROUPS)
    gathers = {}
    prepared = {}

    def gather_chips(group, after):
        shards, zones = prepared.pop(group) if group in prepared else landing_zones(names_of[group])
        gathers[group] = _exchange_start("gather_chips", shards, zones, after, name="gather_chips_" + group)
        return gathers[group][4]

    def gather_pass(group, after):
        _, zones = _exchange_wait("gather_chips", gathers[group], after, name=f"gather_chips_{group}_wait")
        gathers[group] = _exchange_start("gather_pass", [], zones, nothing, name="gather_pass_" + group)
        return gathers[group][4]

    def gathered(group, after, token=None):
        _, zones = _exchange_wait("gather_pass", gathers[group], after, name=f"gather_pass_{group}_wait")
        full = {n: _gathered_to_full(n, z) for n, z in zip(names_of[group], zones)}
        full.update({n: wt[n] if token is None else wt[n] + token for n in _VECTORS})
        return _group_weights(group, full)

    token = gather_chips("mixer", nothing)
    x16 = _cast_bf16(x[0], name="cast_x")
    later = {group: landing_zones(names_of[group]) for group in ("attn", "up", "down")}
    token, x16, later = lax.optimization_barrier((token, x16, later))
    prepared.update(later)
    token = gather_chips("attn", gather_pass("mixer", token))

    def weights_of(group, after):
        if group == "mixer":
            return gathered(group, gathers["attn"][4])
        if group == "ahead":
            return gather_chips("up", gather_pass("attn", after))[0:1, 0:1]
        if group == "attn":
            weights = gathered(group, after)
            token = gather_chips("down", gather_pass("up", weights["w_out"]))[0, 0]
            return {n: (v + token if n == "ln1_g" else v) for n, v in weights.items()}
        if group == "up":
            weights = gathered(group, after)
            gather_pass("down", weights["w_up"])
            return weights
        return gathered(group, after)

    scatters = {}
    in_flight = []

    def chip_stage(after):
        group, names, started = in_flight.pop()
        chunks, from_sibling = _exchange_wait("scatter_sibling", started, after, name=f"scatter_sibling_{group}_wait")
        sums = [_chip_sums(c, f, core_arr, name=f"chip_sums_{n}") for n, c, f in zip(names, chunks, from_sibling)]
        scatters[group] = (names, _exchange_start("scatter_chips", sums, [lax.empty(s.shape, s.dtype) for s in sums],
                                                  nothing, name="scatter_chips_" + group))
        return scatters[group][1][4]

    def grads_ready(group, grads):
        if group == "tick":
            return chip_stage(grads["after"])[0:1, 0:1] if in_flight else None
        names = tuple(grads)
        chunks = [_grad_chunks(n, grads[n]) for n in names]
        token = chip_stage(chunks[0]) if in_flight else nothing
        zones = [lax.empty((N_CHIPS, *c.shape[1:]), c.dtype) for c in chunks]
        started = _exchange_start("scatter_sibling", chunks, zones, token, name="scatter_sibling_" + group)
        in_flight.append((group, names, started))
        return started[4][0:1, 0:1]

    sq, grad_x, g = _local_step(x[0], x16, mem[0], loss_target[0], weights_of, grads_ready)
    small = _finish_small_grads(g)

    out = {}
    after = chip_stage(grad_x)
    for group, (names, started) in scatters.items():
        sums, lands = _exchange_wait("scatter_chips", started, after, name=f"scatter_chips_{group}_wait")
        for n, parts, own in zip(names, lands, sums):
            res = _adamw_shard(parts, own, chip_arr, _update_view(n, args[n]), _update_view(n, args["m_" + n]),
                               _update_view(n, args["v_" + n]), name="adamw_" + n)
            out[n] = [_shard_result(n, r, args[n].shape) for r in res]
            after = res[1]

    packed, _ = lax.optimization_barrier((_pack_small(small), after))
    gs, ds, ms, vs = _small_allreduce_adamw(
        packed, _pack_small({n: wt[n] for n in _VECTORS}), _pack_small({n: mo[n] for n in _VECTORS}),
        _pack_small({n: vo[n] for n in _VECTORS}))
    gs, ds, ms, vs = _unpack_small(gs), _unpack_small(ds), _unpack_small(ms), _unpack_small(vs)
    cols = conv_w.shape[-1]
    conv_full = gs["conv_w"].reshape(CONV_K, QKV_COLS)
    conv_mine = lax.dynamic_slice(conv_full, (0, me * cols), (CONV_K, cols))[None]
    res = _adamw_shard(conv_mine, conv_mine, jnp.zeros((1,), jnp.int32), wt["conv_w"], mo["conv_w"], vo["conv_w"],
                       name="adamw_conv_w")
    out["conv_w"] = [r.reshape(conv_w.shape) for r in res]
    for n in _VECTORS:
        out[n] = [t[n].reshape(args[n].shape) for t in (gs, ds, ms, vs)]

    loss = lax.psum(0.5 * sq[0, 0] / D_MODEL, ("x", "y", "c"))
    return (loss, grad_x[None], *[out[n][0] for n in _WEIGHT_ORDER], *[out[n][1] for n in _WEIGHT_ORDER],
            *[out[n][2] for n in _WEIGHT_ORDER], *[out[n][3] for n in _WEIGHT_ORDER])
```

```python
import functools
import math

import jax
import jax.numpy as jnp
from jax import lax
from jax.experimental import pallas as pl
from jax.experimental.pallas import tpu as pltpu

F32 = jnp.float32
BF16 = jnp.bfloat16
MESH = pl.DeviceIdType.MESH

N_DEV = 8
D_MODEL = 2048
GDN_WIDTH = 1024
GDN_HEADS = 8
HEAD_DIM = 128
CONV_K = 4
CHUNK = 64
POOL_GROUPS = 4
POOL_GROUP_DIM = 256
MEM_LEN = 256
XATTN_HEADS = 4
XATTN_HEAD_DIM = 512
D_FF = 8192
IN_COLS = 5136
ALPHA = 2.0 ** 0.25
LN_EPS = 1e-5
NORM_EPS = 1e-6

LANE = 128
QKV_COLS = 3 * GDN_WIDTH
Z_OFF = QKV_COLS
BA_OFF = 4 * GDN_WIDTH
POOL_OFF = BA_OFF + 2 * LANE
PROJ_COLS = POOL_OFF + GDN_WIDTH
Z_BLK = Z_OFF // LANE
BA_BLK = BA_OFF // LANE
POOL_BLK = POOL_OFF // POOL_GROUP_DIM

ADAM_LR = 0.001
ADAM_B1 = 0.9
ADAM_B2 = 0.999
ADAM_EPS = 1e-08
ADAM_WD = 0.01
ADAM_STEP = 10

VMEM_LIMIT_BYTES = 48 * 1024 * 1024


def _params(*sem):
    return pltpu.CompilerParams(dimension_semantics=sem if sem else None, vmem_limit_bytes=VMEM_LIMIT_BYTES)


def _make_dots(cast, precision, batched=False):
    lead = 1 if batched else 0
    batch = ((0,), (0,)) if batched else ((), ())

    def dg(a, b, ca, cb):
        if cast is not None:
            a = a.astype(cast)
            b = b.astype(cast)
        return lax.dot_general(a, b, (((ca + lead,), (cb + lead,)), batch), precision=precision, preferred_element_type=F32)

    def nn_(a, b):
        return dg(a, b, 1, 0)

    def nt_(a, b):
        return dg(a, b, 1, 1)

    def tn_(a, b):
        return dg(a, b, 0, 0)

    @jax.custom_vjp
    def nn(a, b):
        return nn_(a, b)

    nn.defvjp(lambda a, b: (nn_(a, b), (a, b)), lambda r, g: (nt_(g, r[1]), tn_(r[0], g)))

    @jax.custom_vjp
    def nt(a, b):
        return nt_(a, b)

    nt.defvjp(lambda a, b: (nt_(a, b), (a, b)), lambda r, g: (nn_(g, r[1]), tn_(g, r[0])))

    @jax.custom_vjp
    def tn(a, b):
        return tn_(a, b)

    tn.defvjp(lambda a, b: (tn_(a, b), (a, b)), lambda r, g: (nt_(r[1], g), nn_(r[0], g)))

    return (nn_, nt_, tn_), (nn, nt, tn)


_BDOT_PLAIN, _BDOT_VJP = _make_dots(BF16, None)
_BDOT_BATCH_PLAIN, _BDOT_BATCH_VJP = _make_dots(BF16, None, batched=True)
_FDOT_BATCH_PLAIN, _FDOT_BATCH_VJP = _make_dots(None, lax.Precision.HIGH, batched=True)


def _mm(a, b, *, ta=False, tb=False, out_dtype=F32, tm=None, tn=512, tk=None, epi=None, extra=None, add_scale=1.0,
        b_chunks=False, o_chunks=False, name):
    m, k = (a.shape[1], a.shape[0]) if ta else a.shape
    if b_chunks:
        n, kb = (b.shape[1], N_DEV * b.shape[2]) if tb else (N_DEV * b.shape[2], b.shape[1])
    else:
        n, kb = b.shape if tb else (b.shape[1], b.shape[0])
    assert kb == k, (name, a.shape, b.shape)
    tm, tn, tk = min(tm or m, m), min(tn, n), min(tk or k, k)
    assert m % tm == 0 and n % tn == 0 and k % tk == 0, (name, m, n, k)
    nk = k // tk
    dims = (((0 if ta else 1,), (1 if tb else 0,)), ((), ()))
    n_extra = 0 if epi in (None, "relu2") else 1
    n_out = 2 if epi == "relu2" else 1
    if epi in ("relu2", "mul2r"):
        out_dtype = BF16

    def body(*refs):
        a_ref, b_ref = refs[:2]
        c_ref = refs[2] if n_extra else None
        o_refs = refs[2 + n_extra:2 + n_extra + n_out]
        scr = refs[2 + n_extra + n_out:]
        r = lax.dot_general(a_ref[...].astype(BF16), b_ref[...].astype(BF16), dims, preferred_element_type=F32)

        def finish(v):
            if epi == "add":
                o_refs[0][...] = (v + add_scale * c_ref[...]).astype(out_dtype)
            elif epi == "relu2":
                p = jnp.maximum(v, 0.0)
                o_refs[0][...] = (p * p).astype(BF16)
                o_refs[1][...] = p.astype(BF16)
            elif epi == "mul2r":
                o_refs[0][...] = (v * (2.0 * c_ref[...].astype(F32))).astype(BF16)
            else:
                o_refs[0][...] = v.astype(out_dtype)

        if nk == 1:
            finish(r)
        else:
            acc = scr[0]
            kk = pl.program_id(2)

            @pl.when(kk == 0)
            def _():
                acc[...] = r

            @pl.when(kk > 0)
            def _():
                acc[...] += r

            @pl.when(kk == nk - 1)
            def _():
                finish(acc[...])

    a_spec = pl.BlockSpec((tk, tm), lambda i, j, kk: (kk, i)) if ta else pl.BlockSpec((tm, tk), lambda i, j, kk: (i, kk))
    if b_chunks and tb:
        kc = k // N_DEV // tk
        b_spec = pl.BlockSpec((None, tn, tk), lambda i, j, kk: (kk // kc, j, kk % kc))
    elif b_chunks:
        nc = n // N_DEV // tn
        b_spec = pl.BlockSpec((None, tk, tn), lambda i, j, kk: (j // nc, kk, j % nc))
    elif tb:
        b_spec = pl.BlockSpec((tn, tk), lambda i, j, kk: (j, kk))
    else:
        b_spec = pl.BlockSpec((tk, tn), lambda i, j, kk: (kk, j))
    mn_spec = pl.BlockSpec((tm, tn), lambda i, j, kk: (i, j))
    if o_chunks:
        oc = n // N_DEV // tn
        o_spec = pl.BlockSpec((None, tm, tn), lambda i, j, kk: (j // oc, i, j % oc))
        o_shape = jax.ShapeDtypeStruct((N_DEV, m, n // N_DEV), out_dtype)
    else:
        o_spec, o_shape = mn_spec, jax.ShapeDtypeStruct((m, n), out_dtype)
    res = pl.pallas_call(
        body, grid=(m // tm, n // tn, nk), in_specs=[a_spec, b_spec] + [mn_spec] * n_extra,
        out_specs=[o_spec] * n_out, out_shape=[o_shape] * n_out,
        scratch_shapes=[pltpu.VMEM((tm, tn), F32)] if nk > 1 else [],
        compiler_params=_params("parallel", "parallel", "arbitrary"), name=name,
    )(a, b, *([extra] if n_extra else []))
    return res if n_out > 1 else res[0]


def _cast_bf16(v, *, name, tm=512):
    t, d = v.shape
    tm = min(tm, t)

    def body(v_ref, o_ref):
        o_ref[...] = v_ref[...].astype(BF16)

    spec = pl.BlockSpec((tm, d), lambda i: (i, 0))
    return pl.pallas_call(body, grid=(t // tm,), in_specs=[spec], out_specs=spec,
                          out_shape=jax.ShapeDtypeStruct((t, d), BF16), compiler_params=_params("parallel"), name=name)(v)


def _shift_down(v, s):
    if s == 0:
        return v
    row = lax.broadcasted_iota(jnp.int32, v.shape, 0)
    return jnp.where(row >= s, pltpu.roll(v, s, axis=0), 0.0)


def _shift_up(v, s):
    if s == 0:
        return v
    t = v.shape[0]
    row = lax.broadcasted_iota(jnp.int32, v.shape, 0)
    return jnp.where(row < t - s, pltpu.roll(v, t - s, axis=0), 0.0)


def _post_col(j):
    return (j % GDN_HEADS) * 3 + j // GDN_HEADS


def _gdn_prep_fwd(proj, conv_w):
    t = proj.shape[0]

    def body(x_ref, w_ref, o_ref):
        j = pl.program_id(0)
        x = x_ref[...]
        y = jnp.zeros_like(x)
        for tap in range(CONV_K):
            y = y + w_ref[tap:tap + 1, :] * _shift_down(x, CONV_K - 1 - tap)
        c = y * jax.nn.sigmoid(y)
        nrm = c * lax.rsqrt(jnp.sum(c * c, axis=1, keepdims=True) + NORM_EPS)
        o_ref[...] = jnp.where(j < 2 * GDN_HEADS, nrm, c)

    return pl.pallas_call(
        body, grid=(QKV_COLS // LANE,),
        in_specs=[pl.BlockSpec((t, LANE), lambda j: (0, j)), pl.BlockSpec((CONV_K, LANE), lambda j: (0, j))],
        out_specs=pl.BlockSpec((t, LANE), lambda j: (0, _post_col(j))),
        out_shape=jax.ShapeDtypeStruct((t, QKV_COLS), F32),
        compiler_params=_params("parallel"), name="gdn_prep_fwd",
    )(proj, conv_w)


def _gdn_prep_bwd(proj, conv_w, dpost, dproj):
    t = proj.shape[0]

    def body(x_ref, w_ref, d_ref, _, dx_ref, dw_ref):
        j = pl.program_id(0)
        x = x_ref[...]
        xs = [_shift_down(x, CONV_K - 1 - tap) for tap in range(CONV_K)]
        y = jnp.zeros_like(x)
        for tap in range(CONV_K):
            y = y + w_ref[tap:tap + 1, :] * xs[tap]
        sig = jax.nn.sigmoid(y)
        c = y * sig
        r = lax.rsqrt(jnp.sum(c * c, axis=1, keepdims=True) + NORM_EPS)
        nrm = c * r
        d = d_ref[...]
        dc_norm = r * (d - nrm * jnp.sum(d * nrm, axis=1, keepdims=True))
        dc = jnp.where(j < 2 * GDN_HEADS, dc_norm, d)
        dy = dc * (sig * (1.0 + y * (1.0 - sig)))
        dx = jnp.zeros_like(x)
        for tap in range(CONV_K):
            dx = dx + _shift_up(w_ref[tap:tap + 1, :] * dy, CONV_K - 1 - tap)
            dw_ref[tap:tap + 1, :] = jnp.sum(dy * xs[tap], axis=0, keepdims=True)
        dx_ref[...] = dx.astype(dx_ref.dtype)

    return pl.pallas_call(
        body, grid=(QKV_COLS // LANE,),
        in_specs=[pl.BlockSpec((t, LANE), lambda j: (0, j)), pl.BlockSpec((CONV_K, LANE), lambda j: (0, j)),
                  pl.BlockSpec((t, LANE), lambda j: (0, _post_col(j))), pl.BlockSpec(memory_space=pl.ANY)],
        out_specs=[pl.BlockSpec((t, LANE), lambda j: (0, j)), pl.BlockSpec((CONV_K, LANE), lambda j: (0, j))],
        out_shape=[jax.ShapeDtypeStruct(dproj.shape, dproj.dtype), jax.ShapeDtypeStruct((CONV_K, QKV_COLS), F32)],
        input_output_aliases={3: 0},
        compiler_params=_params("parallel"), name="gdn_prep_bwd",
    )(proj, conv_w, dpost, dproj)


def _softplus(v):
    return jnp.maximum(v, 0.0) + jnp.log(1.0 + jnp.exp(-jnp.abs(v)))


def _tri_inv(low, nn):
    r = lax.broadcasted_iota(jnp.int32, (CHUNK, CHUNK), 0)
    c = lax.broadcasted_iota(jnp.int32, (CHUNK, CHUNK), 1)
    eye = (r == c).astype(F32)
    same_blk = lax.shift_right_logical(r, 4) == lax.shift_right_logical(c, 4)
    diag = jnp.where(same_blk, low, 0.0)
    off = low - diag
    n1 = -diag
    n2 = nn(n1, n1)
    n4 = nn(n2, n2)
    n8 = nn(n4, n4)
    inv_d = nn(nn(nn(eye + n1, eye + n2), eye + n4), eye + n8)
    m1 = nn(inv_d, off)
    m2 = nn(m1, m1)
    return nn(nn(eye - m1, eye + m2), inv_d)


@jax.custom_vjp
def _tri_inv_batched(low):
    return _tri_inv(low, _FDOT_BATCH_PLAIN[0])


def _tri_inv_batched_fwd(low):
    t_inv = _tri_inv(low, _FDOT_BATCH_PLAIN[0])
    return t_inv, t_inv


def _tri_inv_batched_bwd(t_inv, g):
    _, nt, tn = _FDOT_BATCH_PLAIN
    return (-nt(tn(t_inv, g), t_inv),)


_tri_inv_batched.defvjp(_tri_inv_batched_fwd, _tri_inv_batched_bwd)


LOCAL_HEADS_PER_STEP = 8


def _gdn_local_fn(qkv, ba, alog_row, dtb_row, first_head, bdots, fdots):
    nn, nt, tn = bdots
    fnn = fdots[0]
    n_heads = qkv.shape[1] // (3 * HEAD_DIM)
    part = lambda i, p: qkv[:, (3 * i + p) * HEAD_DIM:(3 * i + p + 1) * HEAD_DIM]
    q = jnp.stack([part(i, 0) for i in range(n_heads)]) * (HEAD_DIM ** -0.5)
    k = jnp.stack([part(i, 1) for i in range(n_heads)])
    v = jnp.stack([part(i, 2) for i in range(n_heads)])
    lane = lax.broadcasted_iota(jnp.int32, ba.shape, 1)
    bg = jnp.where(lane < GDN_HEADS, jax.nn.sigmoid(ba), -jnp.exp(alog_row) * _softplus(ba + dtb_row))
    pick = lambda l: jnp.sum(jnp.where(lane == l, bg, 0.0), axis=1, keepdims=True)
    beta = jnp.stack([pick(first_head + i) for i in range(n_heads)])
    g = jnp.stack([pick(first_head + i + GDN_HEADS) for i in range(n_heads)])

    r = lax.broadcasted_iota(jnp.int32, (CHUNK, CHUNK), 0)
    c = lax.broadcasted_iota(jnp.int32, (CHUNK, CHUNK), 1)
    incl = r >= c
    strict = r > c
    eye = r == c

    def to_row(col):
        return jnp.sum(jnp.where(eye, col, 0.0), axis=1, keepdims=True)

    gc = jnp.sum(jnp.where(incl, to_row(g), 0.0), axis=2, keepdims=True)
    diff = gc - to_row(gc)
    decay = jnp.where(incl, jnp.exp(jnp.where(incl, diff, 0.0)), 0.0)
    k_beta = k * beta
    v_beta = v * beta
    low = jnp.where(strict, nt(k_beta, k) * decay, 0.0)
    t_inv = _tri_inv_batched(low) if fdots is _FDOT_BATCH_VJP else _tri_inv(low, fnn)
    eg = jnp.exp(gc)
    u = fnn(t_inv, v_beta)
    w = fnn(t_inv, k_beta * eg)
    attn = jnp.where(incl, nt(q, k) * decay, 0.0)
    last = lax.broadcasted_iota(jnp.int32, (CHUNK, 1), 0) == CHUNK - 1
    g_last = jnp.sum(jnp.where(last, gc, 0.0), axis=1, keepdims=True)
    kdec = k * jnp.exp(g_last - gc)
    elast = jnp.broadcast_to(jnp.exp(g_last), (n_heads, 1, LANE))
    return u, w, q * eg, kdec, attn, elast


def _gdn_state_fn(u, w, qg, kdec, attn, elast, state, bdots):
    nn, _, tn = bdots
    v_new = u - nn(w, state)
    o = nn(qg, state) + nn(attn, v_new)
    return o, state * elast + tn(kdec, v_new)


def _gdn_local_fwd(post, proj, alog_row, dtb_row):
    t = post.shape[0]
    n_chunks = t // CHUNK
    hb = LOCAL_HEADS_PER_STEP

    def body(qkv_ref, ba_ref, al_ref, dt_ref, u_ref, w_ref, qg_ref, kd_ref, at_ref, el_ref):
        u, w, qg, kdec, attn, elast = _gdn_local_fn(qkv_ref[...], ba_ref[...], al_ref[...], dt_ref[...],
                                                    pl.program_id(1) * hb, _BDOT_BATCH_PLAIN, _FDOT_BATCH_PLAIN)
        for i in range(hb):
            cols = slice(i * HEAD_DIM, (i + 1) * HEAD_DIM)
            u_ref[:, cols] = u[i]
            w_ref[:, cols] = w[i].astype(BF16)
            qg_ref[:, cols] = qg[i].astype(BF16)
            kd_ref[:, cols] = kdec[i].astype(BF16)
        at_ref[...] = attn.astype(BF16)
        el_ref[:, 0] = elast

    wide = pl.BlockSpec((CHUNK, hb * HEAD_DIM), lambda n, j: (n, j))
    row = pl.BlockSpec((1, LANE), lambda n, j: (0, 0))
    return pl.pallas_call(
        body, grid=(n_chunks, GDN_HEADS // hb),
        in_specs=[pl.BlockSpec((CHUNK, hb * 3 * HEAD_DIM), lambda n, j: (n, j)),
                  pl.BlockSpec((CHUNK, LANE), lambda n, j: (n, BA_BLK)), row, row],
        out_specs=[wide, wide, wide, wide, pl.BlockSpec((hb, CHUNK, CHUNK), lambda n, j: (j, n, 0)),
                   pl.BlockSpec((hb, 1, 1, LANE), lambda n, j: (j, n, 0, 0))],
        out_shape=[jax.ShapeDtypeStruct((t, GDN_WIDTH), F32), jax.ShapeDtypeStruct((t, GDN_WIDTH), BF16),
                   jax.ShapeDtypeStruct((t, GDN_WIDTH), BF16), jax.ShapeDtypeStruct((t, GDN_WIDTH), BF16),
                   jax.ShapeDtypeStruct((GDN_HEADS, t, CHUNK), BF16),
                   jax.ShapeDtypeStruct((GDN_HEADS, n_chunks, 1, LANE), F32)],
        compiler_params=_params("parallel", "parallel"), name="gdn_local_fwd",
    )(post, proj, alog_row, dtb_row)


def _gdn_state_specs(n_of):
    wide = pl.BlockSpec((CHUNK, GDN_WIDTH), lambda n: (n_of(n), 0))
    attn = pl.BlockSpec((GDN_HEADS, CHUNK, CHUNK), lambda n: (0, n_of(n), 0))
    elast = pl.BlockSpec((GDN_HEADS, 1, 1, LANE), lambda n: (0, n_of(n), 0, 0))
    saved = pl.BlockSpec((GDN_HEADS, 1, HEAD_DIM, HEAD_DIM), lambda n: (0, n_of(n), 0, 0))
    return wide, attn, elast, saved


def _gdn_state_fwd(u, w, qg, kdec, attn, elast):
    t = u.shape[0]
    n_chunks = t // CHUNK

    def body(u_ref, w_ref, qg_ref, kd_ref, at_ref, el_ref, o_ref, save_ref, state_ref):
        @pl.when(pl.program_id(0) == 0)
        def _():
            state_ref[...] = jnp.zeros_like(state_ref)

        for h in range(GDN_HEADS):
            cols = slice(h * HEAD_DIM, (h + 1) * HEAD_DIM)
            state = state_ref[h]
            save_ref[h, 0] = state
            o, new_state = _gdn_state_fn(u_ref[:, cols], w_ref[:, cols], qg_ref[:, cols], kd_ref[:, cols], at_ref[h],
                                         el_ref[h, 0], state, _BDOT_PLAIN)
            o_ref[:, cols] = o
            state_ref[h] = new_state

    wide, attn_spec, elast_spec, saved_spec = _gdn_state_specs(lambda n: n)
    return pl.pallas_call(
        body, grid=(n_chunks,), in_specs=[wide, wide, wide, wide, attn_spec, elast_spec],
        out_specs=[wide, saved_spec],
        out_shape=[jax.ShapeDtypeStruct((t, GDN_WIDTH), F32),
                   jax.ShapeDtypeStruct((GDN_HEADS, n_chunks, HEAD_DIM, HEAD_DIM), F32)],
        scratch_shapes=[pltpu.VMEM((GDN_HEADS, HEAD_DIM, HEAD_DIM), F32)],
        compiler_params=_params("arbitrary"), name="gdn_state_fwd",
    )(u, w, qg, kdec, attn, elast)


def _gdn_state_bwd(u, w, qg, kdec, attn, elast, saved, do):
    t = u.shape[0]
    n_chunks = t // CHUNK
    last = n_chunks - 1

    def body(u_ref, w_ref, qg_ref, kd_ref, at_ref, el_ref, save_ref, do_ref,
             du_ref, dw_ref, dqg_ref, dkd_ref, dat_ref, del_ref, dstate_ref):
        @pl.when(pl.program_id(0) == 0)
        def _():
            dstate_ref[...] = jnp.zeros_like(dstate_ref)

        for h in range(GDN_HEADS):
            cols = slice(h * HEAD_DIM, (h + 1) * HEAD_DIM)
            _, vjp = jax.vjp(
                lambda *a: _gdn_state_fn(*a, _BDOT_VJP), u_ref[:, cols], w_ref[:, cols].astype(F32),
                qg_ref[:, cols].astype(F32), kd_ref[:, cols].astype(F32), at_ref[h].astype(F32), el_ref[h, 0], save_ref[h, 0])
            du, dw, dqg, dkd, dat, de, dstate = vjp((do_ref[:, cols], dstate_ref[h]))
            du_ref[:, cols] = du
            dw_ref[:, cols] = dw
            dqg_ref[:, cols] = dqg
            dkd_ref[:, cols] = dkd
            dat_ref[h] = dat
            del_ref[h, 0] = de
            dstate_ref[h] = dstate

    wide, attn_spec, elast_spec, saved_spec = _gdn_state_specs(lambda n: last - n)
    wide_f32 = jax.ShapeDtypeStruct((t, GDN_WIDTH), F32)
    return pl.pallas_call(
        body, grid=(n_chunks,), in_specs=[wide, wide, wide, wide, attn_spec, elast_spec, saved_spec, wide],
        out_specs=[wide, wide, wide, wide, attn_spec, elast_spec],
        out_shape=[wide_f32, wide_f32, wide_f32, wide_f32, jax.ShapeDtypeStruct((GDN_HEADS, t, CHUNK), F32),
                   jax.ShapeDtypeStruct((GDN_HEADS, n_chunks, 1, LANE), F32)],
        scratch_shapes=[pltpu.VMEM((GDN_HEADS, HEAD_DIM, HEAD_DIM), F32)],
        compiler_params=_params("arbitrary"), name="gdn_state_bwd",
    )(u, w, qg, kdec, attn, elast, saved, do)


def _gdn_local_bwd(post, proj, alog_row, dtb_row, cots, dproj):
    t = post.shape[0]
    n_chunks = t // CHUNK
    hb = LOCAL_HEADS_PER_STEP
    n_steps = GDN_HEADS // hb

    def body(qkv_ref, ba_ref, al_ref, dt_ref, du_ref, dw_ref, dqg_ref, dkd_ref, dat_ref, del_ref, _,
             dqkv_ref, dba_ref, dal_ref, ddt_ref, dba_acc):
        n = pl.program_id(0)
        j = pl.program_id(1)

        @pl.when((n == 0) & (j == 0))
        def _():
            dal_ref[...] = jnp.zeros_like(dal_ref)
            ddt_ref[...] = jnp.zeros_like(ddt_ref)

        @pl.when(j == 0)
        def _():
            dba_acc[...] = jnp.zeros_like(dba_acc)

        heads = lambda ref: jnp.stack([ref[:, i * HEAD_DIM:(i + 1) * HEAD_DIM] for i in range(hb)])
        _, vjp = jax.vjp(lambda a, b, c, d: _gdn_local_fn(a, b, c, d, j * hb, _BDOT_BATCH_VJP, _FDOT_BATCH_VJP),
                         qkv_ref[...], ba_ref[...], al_ref[...], dt_ref[...])
        dqkv, dba, dal, ddt = vjp((heads(du_ref), heads(dw_ref), heads(dqg_ref), heads(dkd_ref), dat_ref[...],
                                   del_ref[:, 0]))
        dqkv_ref[...] = dqkv
        dba_acc[...] += dba
        dal_ref[...] += dal
        ddt_ref[...] += ddt

        @pl.when(j == n_steps - 1)
        def _():
            dba_ref[:, 0:LANE] = dba_acc[...].astype(dba_ref.dtype)
            dba_ref[:, LANE:2 * LANE] = jnp.zeros((CHUNK, LANE), dba_ref.dtype)

    wide = pl.BlockSpec((CHUNK, hb * HEAD_DIM), lambda n, j: (n, j))
    qkv_spec = pl.BlockSpec((CHUNK, hb * 3 * HEAD_DIM), lambda n, j: (n, j))
    row = pl.BlockSpec((1, LANE), lambda n, j: (0, 0))
    return pl.pallas_call(
        body, grid=(n_chunks, n_steps),
        in_specs=[qkv_spec, pl.BlockSpec((CHUNK, LANE), lambda n, j: (n, BA_BLK)), row, row, wide, wide, wide, wide,
                  pl.BlockSpec((hb, CHUNK, CHUNK), lambda n, j: (j, n, 0)),
                  pl.BlockSpec((hb, 1, 1, LANE), lambda n, j: (j, n, 0, 0)), pl.BlockSpec(memory_space=pl.ANY)],
        out_specs=[qkv_spec, pl.BlockSpec((CHUNK, 2 * LANE), lambda n, j: (n, BA_BLK // 2)), row, row],
        out_shape=[jax.ShapeDtypeStruct((t, QKV_COLS), F32), jax.ShapeDtypeStruct(dproj.shape, dproj.dtype),
                   jax.ShapeDtypeStruct((1, LANE), F32), jax.ShapeDtypeStruct((1, LANE), F32)],
        input_output_aliases={10: 1},
        scratch_shapes=[pltpu.VMEM((CHUNK, LANE), F32)],
        compiler_params=_params("arbitrary", "arbitrary"), name="gdn_local_bwd",
    )(post, proj, alog_row, dtb_row, *cots, dproj)


def _onorm_fn(o, z, w):
    return o * lax.rsqrt(jnp.mean(o * o, axis=1, keepdims=True) + NORM_EPS) * w * (z * jax.nn.sigmoid(z))


def _onorm_fwd(o_raw, proj, norm_w, mixin, tm=512):
    t = o_raw.shape[0]
    tm = min(tm, t)

    def body(o_ref, z_ref, w_ref, _, out_ref):
        out_ref[...] = _onorm_fn(o_ref[...], z_ref[...], w_ref[...]).astype(out_ref.dtype)

    return pl.pallas_call(
        body, grid=(t // tm, GDN_HEADS),
        in_specs=[pl.BlockSpec((tm, LANE), lambda i, h: (i, h)), pl.BlockSpec((tm, LANE), lambda i, h: (i, Z_BLK + h)),
                  pl.BlockSpec((1, LANE), lambda i, h: (0, 0)), pl.BlockSpec(memory_space=pl.ANY)],
        out_specs=pl.BlockSpec((tm, LANE), lambda i, h: (i, h)),
        out_shape=jax.ShapeDtypeStruct(mixin.shape, mixin.dtype), input_output_aliases={3: 0},
        compiler_params=_params("parallel", "parallel"), name="gdn_onorm_fwd",
    )(o_raw, proj, norm_w, mixin)


def _onorm_bwd(o_raw, proj, norm_w, dmixin, dproj, tm=512):
    t = o_raw.shape[0]
    tm = min(tm, t)

    def body(o_ref, z_ref, w_ref, d_ref, _, do_ref, dz_ref, dw_ref):
        @pl.when((pl.program_id(0) == 0) & (pl.program_id(1) == 0))
        def _():
            dw_ref[...] = jnp.zeros_like(dw_ref)

        _, vjp = jax.vjp(_onorm_fn, o_ref[...], z_ref[...], w_ref[...])
        do, dz, dw = vjp(d_ref[...])
        do_ref[...] = do
        dz_ref[...] = dz.astype(dz_ref.dtype)
        dw_ref[...] += dw

    return pl.pallas_call(
        body, grid=(t // tm, GDN_HEADS),
        in_specs=[pl.BlockSpec((tm, LANE), lambda i, h: (i, h)), pl.BlockSpec((tm, LANE), lambda i, h: (i, Z_BLK + h)),
                  pl.BlockSpec((1, LANE), lambda i, h: (0, 0)), pl.BlockSpec((tm, LANE), lambda i, h: (i, h)),
                  pl.BlockSpec(memory_space=pl.ANY)],
        out_specs=[pl.BlockSpec((tm, LANE), lambda i, h: (i, h)), pl.BlockSpec((tm, LANE), lambda i, h: (i, Z_BLK + h)),
                   pl.BlockSpec((1, LANE), lambda i, h: (0, 0))],
        out_shape=[jax.ShapeDtypeStruct((t, GDN_WIDTH), F32), jax.ShapeDtypeStruct(dproj.shape, dproj.dtype),
                   jax.ShapeDtypeStruct((1, LANE), F32)],
        input_output_aliases={4: 1},
        compiler_params=_params("arbitrary", "arbitrary"), name="gdn_onorm_bwd",
    )(o_raw, proj, norm_w, dmixin, dproj)


def _pool_select(levels, gi):
    out = levels[-1]
    for lvl in range(len(levels) - 2, -1, -1):
        out = jnp.where(gi == lvl, levels[lvl], out)
    return out


def _pool_count(shape, gi):
    pos = lax.broadcasted_iota(jnp.int32, shape, 0)
    win = lax.shift_left(jnp.int32(2), gi)
    return jnp.minimum(pos + 1, win).astype(F32)


def _pooled(p, gi):
    acc = p
    levels = []
    for lvl in range(POOL_GROUPS):
        acc = acc + _shift_down(acc, 1 << lvl)
        levels.append(acc)
    return _pool_select(levels, gi) / _pool_count(p.shape, gi) - p


def _pool_fwd(proj, pool_w, pool_scale):
    t = proj.shape[0]

    def body(p_ref, w_ref, s_ref, out_ref):
        gi = pl.program_id(0)
        pooled = _pooled(p_ref[...], gi)
        out_ref[...] = (_BDOT_PLAIN[0](pooled, w_ref[0]) * s_ref[0]).astype(out_ref.dtype)

    return pl.pallas_call(
        body, grid=(POOL_GROUPS,),
        in_specs=[pl.BlockSpec((t, POOL_GROUP_DIM), lambda g: (0, POOL_BLK + g)),
                  pl.BlockSpec((1, POOL_GROUP_DIM, POOL_GROUP_DIM), lambda g: (g, 0, 0)),
                  pl.BlockSpec((1, 1, POOL_GROUP_DIM), lambda g: (g, 0, 0))],
        out_specs=pl.BlockSpec((t, POOL_GROUP_DIM), lambda g: (0, GDN_WIDTH // POOL_GROUP_DIM + g)),
        out_shape=jax.ShapeDtypeStruct((t, 2 * GDN_WIDTH), BF16),
        compiler_params=_params("parallel"), name="pool_fwd",
    )(proj, pool_w, pool_scale)


def _pool_bwd(proj, pool_w, pool_scale, dmixin):
    t = proj.shape[0]
    nn, nt, tn = _BDOT_PLAIN

    def body(p_ref, w_ref, s_ref, d_ref, dp_ref, dw_ref, ds_ref):
        gi = pl.program_id(0)
        p = p_ref[...]
        pooled = _pooled(p, gi)
        mixed = nn(pooled, w_ref[0])
        d = d_ref[...]
        ds_ref[0] = jnp.sum(d * mixed, axis=0, keepdims=True)
        dmixed = d * s_ref[0]
        dw_ref[0] = tn(pooled, dmixed)
        dpooled = nt(dmixed, w_ref[0])
        acc = dpooled / _pool_count(p.shape, gi)
        levels = []
        for lvl in range(POOL_GROUPS):
            acc = acc + _shift_up(acc, 1 << lvl)
            levels.append(acc)
        dp_ref[...] = (_pool_select(levels, gi) - dpooled).astype(dp_ref.dtype)

    return pl.pallas_call(
        body, grid=(POOL_GROUPS,),
        in_specs=[pl.BlockSpec((t, POOL_GROUP_DIM), lambda g: (0, POOL_BLK + g)),
                  pl.BlockSpec((1, POOL_GROUP_DIM, POOL_GROUP_DIM), lambda g: (g, 0, 0)),
                  pl.BlockSpec((1, 1, POOL_GROUP_DIM), lambda g: (g, 0, 0)),
                  pl.BlockSpec((t, POOL_GROUP_DIM), lambda g: (0, GDN_WIDTH // POOL_GROUP_DIM + g))],
        out_specs=[pl.BlockSpec((t, POOL_GROUP_DIM), lambda g: (0, POOL_BLK + g)),
                   pl.BlockSpec((1, POOL_GROUP_DIM, POOL_GROUP_DIM), lambda g: (g, 0, 0)),
                   pl.BlockSpec((1, 1, POOL_GROUP_DIM), lambda g: (g, 0, 0))],
        out_shape=[jax.ShapeDtypeStruct((t, PROJ_COLS), BF16),
                   jax.ShapeDtypeStruct((POOL_GROUPS, POOL_GROUP_DIM, POOL_GROUP_DIM), F32),
                   jax.ShapeDtypeStruct((POOL_GROUPS, 1, POOL_GROUP_DIM), F32)],
        compiler_params=_params("parallel"), name="pool_bwd",
    )(proj, pool_w, pool_scale, dmixin)


def _ln_stats(s):
    mu = jnp.mean(s, axis=1, keepdims=True)
    xc = s - mu
    var = jnp.mean(xc * xc, axis=1, keepdims=True)
    rstd = lax.rsqrt(var + LN_EPS)
    return xc * rstd, rstd


def _ln_fwd(h_in, y, g, b, *, name, tm=256):
    t, d = h_in.shape
    tm = min(tm, t)

    def body(h_ref, y_ref, g_ref, b_ref, o_ref, o16_ref):
        xhat, _ = _ln_stats(ALPHA * h_ref[...] + y_ref[...])
        out = xhat * g_ref[...] + b_ref[...]
        o_ref[...] = out
        o16_ref[...] = out.astype(BF16)

    row = pl.BlockSpec((tm, d), lambda i: (i, 0))
    vec = pl.BlockSpec((1, d), lambda i: (0, 0))
    return pl.pallas_call(
        body, grid=(t // tm,), in_specs=[row, row, vec, vec], out_specs=[row, row],
        out_shape=[jax.ShapeDtypeStruct((t, d), F32), jax.ShapeDtypeStruct((t, d), BF16)],
        compiler_params=_params("parallel"), name=name,
    )(h_in, y, g, b)


def _ln_backward(xhat, rstd, dout, gain):
    dxhat = dout * gain
    m1 = jnp.mean(dxhat, axis=1, keepdims=True)
    m2 = jnp.mean(dxhat * xhat, axis=1, keepdims=True)
    return (rstd * (dxhat - m1 - xhat * m2), jnp.sum(dout * xhat, axis=0, keepdims=True),
            jnp.sum(dout, axis=0, keepdims=True))


def _ln_loss(h_in, y, g, b, target, *, name, tm=256):
    t, d = h_in.shape
    tm = min(tm, t)

    def body(h_ref, y_ref, g_ref, b_ref, t_ref, sq_ref, ds_ref, ds16_ref, dg_ref, dbias_ref):
        @pl.when(pl.program_id(0) == 0)
        def _():
            sq_ref[...] = jnp.zeros_like(sq_ref)
            dg_ref[...] = jnp.zeros_like(dg_ref)
            dbias_ref[...] = jnp.zeros_like(dbias_ref)

        xhat, rstd = _ln_stats(ALPHA * h_ref[...] + y_ref[...])
        err = xhat * g_ref[...] + b_ref[...] - t_ref[...]
        sq_ref[...] += jnp.sum(jnp.sum(err * err, axis=1, keepdims=True), axis=0, keepdims=True)
        ds, dg, dbias = _ln_backward(xhat, rstd, err * (1.0 / d), g_ref[...])
        ds_ref[...] = ds
        ds16_ref[...] = ds.astype(BF16)
        dg_ref[...] += dg
        dbias_ref[...] += dbias

    row = pl.BlockSpec((tm, d), lambda i: (i, 0))
    vec = pl.BlockSpec((1, d), lambda i: (0, 0))
    return pl.pallas_call(
        body, grid=(t // tm,), in_specs=[row, row, vec, vec, row],
        out_specs=[pl.BlockSpec((1, LANE), lambda i: (0, 0)), row, row, vec, vec],
        out_shape=[jax.ShapeDtypeStruct((1, LANE), F32), jax.ShapeDtypeStruct((t, d), F32),
                   jax.ShapeDtypeStruct((t, d), BF16), jax.ShapeDtypeStruct((1, d), F32), jax.ShapeDtypeStruct((1, d), F32)],
        compiler_params=_params("arbitrary"), name=name,
    )(h_in, y, g, b, target)


def _ln_bwd(h_in, y, g, d_a, d_b, *, name, tm=256):
    t, d = h_in.shape
    tm = min(tm, t)
    has_b = d_b is not None

    def body(*refs):
        if has_b:
            h_ref, y_ref, g_ref, da_ref, db_ref, ds_ref, ds16_ref, dg_ref, dbias_ref = refs
        else:
            h_ref, y_ref, g_ref, da_ref, ds_ref, ds16_ref, dg_ref, dbias_ref = refs

        @pl.when(pl.program_id(0) == 0)
        def _():
            dg_ref[...] = jnp.zeros_like(dg_ref)
            dbias_ref[...] = jnp.zeros_like(dbias_ref)

        xhat, rstd = _ln_stats(ALPHA * h_ref[...] + y_ref[...])
        dout = da_ref[...]
        if has_b:
            dout = dout + ALPHA * db_ref[...]
        ds, dg, dbias = _ln_backward(xhat, rstd, dout, g_ref[...])
        ds_ref[...] = ds
        ds16_ref[...] = ds.astype(BF16)
        dg_ref[...] += dg
        dbias_ref[...] += dbias

    row = pl.BlockSpec((tm, d), lambda i: (i, 0))
    vec = pl.BlockSpec((1, d), lambda i: (0, 0))
    args = [h_in, y, g, d_a] + ([d_b] if has_b else [])
    return pl.pallas_call(
        body, grid=(t // tm,), in_specs=[row, row, vec, row] + ([row] if has_b else []),
        out_specs=[row, row, vec, vec],
        out_shape=[jax.ShapeDtypeStruct((t, d), F32), jax.ShapeDtypeStruct((t, d), BF16),
                   jax.ShapeDtypeStruct((1, d), F32), jax.ShapeDtypeStruct((1, d), F32)],
        compiler_params=_params("arbitrary"), name=name,
    )(*args)


def _attn_fn(q, k, v, dots):
    nn, nt, _ = dots
    s = nt(q, k) * (XATTN_HEAD_DIM ** -0.5)
    s = s - lax.stop_gradient(jnp.max(s, axis=1, keepdims=True))
    e = jnp.exp(s)
    p = e / jnp.sum(e, axis=1, keepdims=True)
    return nn(p, v)


def _attn_fwd(q, k, v, tq=512):
    t = q.shape[0]
    tq = min(tq, t)

    def body(q_ref, k_ref, v_ref, o_ref):
        o_ref[...] = _attn_fn(q_ref[...], k_ref[...], v_ref[...], _BDOT_PLAIN).astype(BF16)

    qs = pl.BlockSpec((tq, XATTN_HEAD_DIM), lambda h, i: (i, h))
    ks = pl.BlockSpec((MEM_LEN, XATTN_HEAD_DIM), lambda h, i: (0, h))
    return pl.pallas_call(
        body, grid=(XATTN_HEADS, t // tq), in_specs=[qs, ks, ks], out_specs=qs,
        out_shape=jax.ShapeDtypeStruct(q.shape, BF16), compiler_params=_params("parallel", "parallel"), name="xattn_fwd",
    )(q, k, v)


def _attn_bwd(q, k, v, do, tq=512):
    t = q.shape[0]
    tq = min(tq, t)

    def body(q_ref, k_ref, v_ref, do_ref, dq_ref, dk_ref, dv_ref):
        @pl.when(pl.program_id(1) == 0)
        def _():
            dk_ref[...] = jnp.zeros_like(dk_ref)
            dv_ref[...] = jnp.zeros_like(dv_ref)

        _, vjp = jax.vjp(lambda a, b, c: _attn_fn(a, b, c, _BDOT_VJP), q_ref[...].astype(F32), k_ref[...].astype(F32),
                         v_ref[...].astype(F32))
        dq, dk, dv = vjp(do_ref[...].astype(F32))
        dq_ref[...] = dq.astype(BF16)
        dk_ref[...] += dk
        dv_ref[...] += dv

    qs = pl.BlockSpec((tq, XATTN_HEAD_DIM), lambda h, i: (i, h))
    ks = pl.BlockSpec((MEM_LEN, XATTN_HEAD_DIM), lambda h, i: (0, h))
    return pl.pallas_call(
        body, grid=(XATTN_HEADS, t // tq), in_specs=[qs, ks, ks, qs], out_specs=[qs, ks, ks],
        out_shape=[jax.ShapeDtypeStruct(q.shape, BF16), jax.ShapeDtypeStruct(k.shape, F32), jax.ShapeDtypeStruct(v.shape, F32)],
        compiler_params=_params("parallel", "arbitrary"), name="xattn_bwd",
    )(q, k, v, do)


def _local_step(x, x16, mem, target, weights_of, grads_ready):
    def behind(vec, token):
        return vec if token is None else vec + token

    w = dict(weights_of("mixer", None))
    proj = _mm(x16, w["w_in"], tb=True, tn=768, name="mm_in_proj")
    mixin = _pool_fwd(proj, w["pool_w"], w["pool_scale"])
    post = _gdn_prep_fwd(proj, w["conv_w"])
    token = weights_of("ahead", post)
    chunked = _gdn_local_fwd(post, proj, behind(w["alog_row"], token), w["dtb_row"])
    o_raw, saved = _gdn_state_fwd(*chunked)
    mixin = _onorm_fwd(o_raw, proj, w["gdn_norm_w"], mixin)
    w.update(weights_of("attn", mixin))
    mix = _mm(mixin, w["w_out"], name="mm_out_proj")
    h1, h1_16 = _ln_fwd(x, mix, w["ln1_g"], w["ln1_b"], name="ln1_fwd")
    xq = _mm(h1_16, w["xq_w"], out_dtype=BF16, name="mm_xq")
    xk = _mm(mem, w["xk_w"], out_dtype=BF16, name="mm_xk")
    xv = _mm(mem, w["xv_w"], out_dtype=BF16, name="mm_xv")
    xo = _attn_fwd(xq, xk, xv)
    xa = _mm(xo, w["xo_w"], name="mm_xo")
    h2, h2_16 = _ln_fwd(h1, xa, w["ln2_g"], w["ln2_b"], name="ln2_fwd")
    w.update(weights_of("up", h2_16))
    act, relu = _mm(h2_16, w["w_up"], b_chunks=True, epi="relu2", name="mm_up")
    w.update(weights_of("down", act))
    ff = _mm(act, w["w_down"], tn=1024, tk=1024, name="mm_down")
    g = {}
    sq, ds3, ds3_16, g["ln3_g"], g["ln3_b"] = _ln_loss(h2, ff, w["ln3_g"], w["ln3_b"], target, name="ln3_loss")

    gw_down = _mm(act, ds3_16, ta=True, out_dtype=BF16, tm=512, tn=D_MODEL, name="mm_gw_down")
    du = _mm(ds3_16, w["w_down"], tb=True, epi="mul2r", extra=relu, name="mm_du")
    gw_up = _mm(h2_16, du, ta=True, out_dtype=BF16, o_chunks=True, name="mm_gw_up")
    token = grads_ready("mlp", {"w_down": gw_down, "w_up": gw_up})
    dh2 = _mm(du, w["w_up"], tb=True, b_chunks=True, tn=1024, tk=1024, name="mm_dh2")
    ds2, ds2_16, g["ln2_g"], g["ln2_b"] = _ln_bwd(h1, xa, behind(w["ln2_g"], token), dh2, ds3, name="ln2_bwd")
    gw_xo = _mm(xo, ds2_16, ta=True, out_dtype=BF16, name="mm_gw_xo")
    dxo = _mm(ds2_16, w["xo_w"], tb=True, out_dtype=BF16, name="mm_dxo")
    dxq, dxk, dxv = _attn_bwd(xq, xk, xv, dxo)
    gw_xq = _mm(h1_16, dxq, ta=True, out_dtype=BF16, name="mm_gw_xq")
    gw_xk = _mm(mem, dxk, ta=True, out_dtype=BF16, name="mm_gw_xk")
    gw_xv = _mm(mem, dxv, ta=True, out_dtype=BF16, name="mm_gw_xv")
    token = grads_ready("attn", {"xo_w": gw_xo, "xq_w": gw_xq, "xk_w": gw_xk, "xv_w": gw_xv})
    dh1 = _mm(dxq, w["xq_w"], tb=True, name="mm_dh1")
    ds1, ds1_16, g["ln1_g"], g["ln1_b"] = _ln_bwd(x, mix, behind(w["ln1_g"], token), dh1, ds2, name="ln1_bwd")
    gw_out = _mm(mixin, ds1_16, ta=True, out_dtype=BF16, name="mm_gw_out")
    dmixin = _mm(ds1_16, w["w_out"], tb=True, name="mm_dmixin")
    dproj, gw_pool, g["pool_scale"] = _pool_bwd(proj, w["pool_w"], w["pool_scale"], dmixin)
    token = grads_ready("mix", {"w_out": gw_out, "pool_w": gw_pool})
    do_raw, dproj, g["gdn_norm_w"] = _onorm_bwd(o_raw, proj, behind(w["gdn_norm_w"], token), dmixin, dproj)
    cots = _gdn_state_bwd(*chunked, saved, do_raw)
    token = grads_ready("tick", {"after": cots[0]})
    dpost, dproj, g["alog_row"], g["dtb_row"] = _gdn_local_bwd(post, proj, behind(w["alog_row"], token), w["dtb_row"],
                                                               cots, dproj)
    dproj, g["conv_w"] = _gdn_prep_bwd(proj, w["conv_w"], dpost, dproj)
    gw_in = _mm(dproj, x16, ta=True, out_dtype=BF16, tm=768, tn=D_MODEL, name="mm_gw_in")
    token = grads_ready("in", {"w_in": gw_in})
    if token is not None:
        ds1, _ = lax.optimization_barrier((ds1, token))
    grad_x = _mm(dproj, w["w_in"], tk=768, epi="add", extra=ds1, add_scale=ALPHA, name="mm_dx")
    return sq, grad_x, g


_MATRICES = ("w_in", "pool_w", "w_out", "xq_w", "xk_w", "xv_w", "xo_w", "w_up", "w_down")
_VECTORS = ("a_log", "dt_bias", "gdn_norm_w", "pool_scale", "ln1_g", "ln1_b", "ln2_g", "ln2_b", "ln3_g", "ln3_b")
_BA_SPLIT = BA_OFF + 2 * GDN_HEADS


def _lane_row(v, offset):
    return jnp.zeros((1, LANE), F32).at[0, offset:offset + v.shape[0]].set(v)


_GROUP_VECTORS = {"mixer": (), "attn": ("ln1_g", "ln1_b", "ln2_g", "ln2_b"), "up": (), "down": ("ln3_g", "ln3_b")}


def _group_weights(group, full):
    w = {n: full[n].reshape(1, D_MODEL) for n in _GROUP_VECTORS[group]}
    if group == "mixer":
        w_in = full["w_in"].reshape(IN_COLS, D_MODEL)
        zeros = jnp.zeros((POOL_OFF - _BA_SPLIT, D_MODEL), w_in.dtype)
        w.update({
            "w_in": jnp.concatenate([w_in[:_BA_SPLIT], zeros, w_in[_BA_SPLIT:]], axis=0),
            "conv_w": full["conv_w"],
            "alog_row": _lane_row(full["a_log"], GDN_HEADS),
            "dtb_row": _lane_row(full["dt_bias"], GDN_HEADS),
            "gdn_norm_w": full["gdn_norm_w"].reshape(1, LANE),
            "pool_w": full["pool_w"],
            "pool_scale": full["pool_scale"].reshape(POOL_GROUPS, 1, POOL_GROUP_DIM),
        })
    else:
        w.update({n: full[n] for n in dict(_GATHER_GROUPS)[group]})
    return w


def _w_in_chunks(g):
    unpadded = jnp.concatenate([g[:_BA_SPLIT], g[POOL_OFF:]], axis=0)
    return unpadded.reshape(N_DEV, IN_COLS // N_DEV, D_MODEL)


def _finish_small_grads(g):
    out = {"conv_w": g["conv_w"]}
    out["a_log"] = g["alog_row"][0, GDN_HEADS:2 * GDN_HEADS]
    out["dt_bias"] = g["dtb_row"][0, GDN_HEADS:2 * GDN_HEADS]
    out["gdn_norm_w"] = g["gdn_norm_w"].reshape(LANE)
    out["pool_scale"] = g["pool_scale"].reshape(POOL_GROUPS * POOL_GROUP_DIM)
    for n in ("ln1_g", "ln1_b", "ln2_g", "ln2_b", "ln3_g", "ln3_b"):
        out[n] = g[n].reshape(D_MODEL)
    return out


def _adamw_math(w, g, m, v):
    m = ADAM_B1 * m + (1.0 - ADAM_B1) * g
    v = ADAM_B2 * v + (1.0 - ADAM_B2) * (g * g)
    m_hat = m / (1.0 - ADAM_B1 ** ADAM_STEP)
    v_hat = v / (1.0 - ADAM_B2 ** ADAM_STEP)
    delta = -ADAM_LR * (m_hat / (jnp.sqrt(v_hat) + ADAM_EPS) + ADAM_WD * w)
    return delta, m, v


ADAMW_TILE_ELEMS = 256 * 1024
CHIP_SUM_TILE_ELEMS = 1024 * 1024


def _shard_tile(r, c, elems):
    for rows in (1024, 512, 256, 128):
        if r % rows == 0 and rows * c <= elems:
            return rows, c
    if r % 128 == 0:
        return 128, c
    return r, 256 if c % 256 == 0 else c


def _adamw_shard(parts, own, me, w, m, v, *, name):
    s, r, c = parts.shape
    tr, tc = _shard_tile(r, c, ADAMW_TILE_ELEMS)
    assert r % tr == 0 and c % tc == 0, (name, r, c)
    unit_axis = w.ndim == 3
    at = (slice(None), 0, slice(None)) if unit_axis else Ellipsis

    def body(me_ref, p_ref, own_ref, w_ref, m_ref, v_ref, g_ref, d_ref, nm_ref, nv_ref):
        mine = own_ref[...].astype(F32)
        g = None
        for i in range(s):
            part = jnp.where(me_ref[0] == i, mine, p_ref[i].astype(F32))
            g = part if g is None else g + part
        delta, nm, nv = _adamw_math(w_ref[at], g, m_ref[at], v_ref[at])
        g_ref[at] = g
        d_ref[at] = delta
        nm_ref[at] = nm
        nv_ref[at] = nv

    if unit_axis:
        blk = pl.BlockSpec((tr, 1, tc), lambda i, j, me_ref: (i, 0, j))
        out = jax.ShapeDtypeStruct((r, 1, c), F32)
    else:
        blk = pl.BlockSpec((tr, tc), lambda i, j, me_ref: (i, j))
        out = jax.ShapeDtypeStruct((r, c), F32)
    return pl.pallas_call(
        body,
        grid_spec=pltpu.PrefetchScalarGridSpec(
            num_scalar_prefetch=1, grid=(r // tr, c // tc),
            in_specs=[pl.BlockSpec((s, tr, tc), lambda i, j, me_ref: (0, i, j)),
                      pl.BlockSpec((None, tr, tc), lambda i, j, me_ref: (me_ref[0], i, j)), blk, blk, blk],
            out_specs=[blk, blk, blk, blk]),
        out_shape=[out, out, out, out], compiler_params=_params("parallel", "parallel"), name=name,
    )(me, parts, own, w, m, v)


N_CHIPS = N_DEV // 2


def _chip_sums(chunks, from_sibling, core, *, name):
    _, r, c = chunks.shape
    tr, tc = _shard_tile(r, c, CHIP_SUM_TILE_ELEMS)
    assert r % tr == 0 and c % tc == 0, (name, r, c)

    def body(core_ref, mine_ref, other_ref, o_ref):
        o_ref[...] = (mine_ref[...].astype(F32) + other_ref[...].astype(F32)).astype(o_ref.dtype)

    by_chip = pl.BlockSpec((None, tr, tc), lambda q, i, j, core_ref: (q, i, j))
    return pl.pallas_call(
        body,
        grid_spec=pltpu.PrefetchScalarGridSpec(
            num_scalar_prefetch=1, grid=(N_CHIPS, r // tr, c // tc),
            in_specs=[pl.BlockSpec((None, tr, tc), lambda q, i, j, core_ref: (2 * q + core_ref[0], i, j)), by_chip],
            out_specs=by_chip),
        out_shape=jax.ShapeDtypeStruct((N_CHIPS, r, c), chunks.dtype),
        compiler_params=_params("parallel", "parallel", "parallel"), name=name,
    )(core, chunks, from_sibling)


def _place():
    return lax.axis_index("x"), lax.axis_index("y"), lax.axis_index("c")


def _slot(px, py, pc):
    return 4 * px + 2 * py + pc


_HBM = pl.BlockSpec(memory_space=pltpu.HBM)


_SEM = pl.BlockSpec(memory_space=pltpu.SEMAPHORE)
_ANY = pl.BlockSpec(memory_space=pl.ANY)
_EFFECT = pltpu.SideEffectType.DATAFLOW_SIDE_EFFECTING
_N_PEERS = N_DEV - 1


def _peer(k, x, y, c):
    return (1 - x if k & 4 else x, 1 - y if k & 2 else y, 1 - c if k & 1 else c)


_EXCHANGE_BITS = {"gather_chips": (1, 2, 4, 6), "gather_pass": (2, 4, 6), "scatter_sibling": (1, 1, 1, 1),
                  "scatter_chips": (2, 4, 6)}


def _exchange_copy(mode, src, land, w, i, place, send_sems, recv_sems, receiving):
    bits = _EXCHANGE_BITS[mode]
    k = bits[i]
    peer = _peer(k, *place)
    me = _slot(*place)
    if mode == "gather_chips":
        to, src_ref, sent_to, got_at = peer, src[w], me, _slot(*peer)
    elif mode == "gather_pass":
        blk = _slot(*peer)
        to, src_ref, sent_to, got_at = _peer(1, *place), land[w].at[blk], blk, _slot(*_peer(k | 1, *place))
    elif mode == "scatter_sibling":
        to, src_ref, sent_to, got_at = peer, src[w].at[2 * i + 1 - place[2]], i, i
    else:
        to, src_ref, sent_to, got_at = peer, src[w].at[_slot(*peer) // 2], me // 2, _slot(*peer) // 2
    sem = w * len(bits) + i
    return pltpu.make_async_remote_copy(
        src_ref=src_ref, dst_ref=land[w].at[got_at if receiving else sent_to], send_sem=send_sems.at[sem],
        recv_sem=recv_sems.at[sem], device_id=to, device_id_type=MESH)


def _exchange_start(mode, srcs, lands, after, *, name):
    ns, nl = len(srcs), len(lands)
    n_sem = nl * len(_EXCHANGE_BITS[mode])

    def body(*refs):
        src, land = refs[:ns], refs[ns:ns + nl]
        send_sems, recv_sems = refs[ns + nl + 1:ns + nl + 3]
        token = refs[-1]
        place = _place()
        for w in range(nl):
            for i in range(len(_EXCHANGE_BITS[mode])):
                _exchange_copy(mode, src, land, w, i, place, send_sems, recv_sems, receiving=False).start()
        token[...] = jnp.zeros_like(token)

    sems = pltpu.SemaphoreType.DMA((n_sem,))
    arrays = list(srcs) + list(lands)
    res = pl.pallas_call(
        body, name=name, in_specs=[_HBM] * (ns + nl) + [_ANY],
        out_specs=(_SEM, _SEM, *([_HBM] * (ns + nl)), pl.BlockSpec(memory_space=pltpu.VMEM)),
        out_shape=(sems, sems, *[pltpu.HBM(a.shape, a.dtype) for a in arrays], jax.ShapeDtypeStruct((8, LANE), F32)),
        input_output_aliases={i: 2 + i for i in range(ns + nl)},
        compiler_params=pltpu.CompilerParams(has_side_effects=_EFFECT),
    )(*[pltpu.with_memory_space_constraint(a, pltpu.HBM) for a in arrays], after)
    return res[0], res[1], list(res[2:2 + ns]), list(res[2 + ns:2 + ns + nl]), res[-1]


def _exchange_wait(mode, started, after, *, name):
    send_sems, recv_sems, srcs, lands, _ = started
    ns, nl = len(srcs), len(lands)

    def body(*refs):
        src, land = refs[:ns], refs[ns:ns + nl]
        send_sems, recv_sems = refs[ns + nl:ns + nl + 2]
        place = _place()
        for w in range(nl):
            for i in range(len(_EXCHANGE_BITS[mode])):
                cp = _exchange_copy(mode, src, land, w, i, place, send_sems, recv_sems, receiving=True)
                cp.wait_send()
                cp.wait_recv()

    arrays = list(srcs) + list(lands)
    res = pl.pallas_call(
        body, name=name, in_specs=[_HBM] * (ns + nl) + [_SEM, _SEM, _ANY], out_specs=[_HBM] * (ns + nl),
        out_shape=[pltpu.HBM(a.shape, a.dtype) for a in arrays],
        input_output_aliases={i: i for i in range(ns + nl)},
        compiler_params=pltpu.CompilerParams(has_side_effects=_EFFECT),
    )(*arrays, send_sems, recv_sems, after)
    return list(res[:ns]), list(res[ns:])


def _small_allreduce_adamw(gvec, wvec, mvec, vvec):
    rows, length = gvec.shape

    def body(g_ref, w_ref, m_ref, v_ref, gs_ref, d_ref, nm_ref, nv_ref, slots, send_sems, recv_sems):
        x, y, c = _place()
        me = _slot(x, y, c)
        slots[me] = g_ref[...]
        sends = []
        for k in range(1, N_DEV):
            peer = _peer(k, x, y, c)
            sends.append(pltpu.make_async_remote_copy(
                src_ref=g_ref, dst_ref=slots.at[me], send_sem=send_sems.at[k - 1], recv_sem=recv_sems.at[k - 1],
                device_id=peer, device_id_type=MESH))
        for cp in sends:
            cp.start()
        for k in range(1, N_DEV):
            peer = _peer(k, x, y, c)
            pltpu.make_async_remote_copy(
                src_ref=g_ref, dst_ref=slots.at[_slot(*peer)], send_sem=send_sems.at[k - 1], recv_sem=recv_sems.at[k - 1],
                device_id=peer, device_id_type=MESH).wait_recv()
        for cp in sends:
            cp.wait_send()
        g = slots[0]
        for s in range(1, N_DEV):
            g = g + slots[s]
        delta, nm, nv = _adamw_math(w_ref[...], g, m_ref[...], v_ref[...])
        gs_ref[...] = g
        d_ref[...] = delta
        nm_ref[...] = nm
        nv_ref[...] = nv

    vmem = pl.BlockSpec(memory_space=pltpu.VMEM)
    out = jax.ShapeDtypeStruct((rows, length), F32)
    return pl.pallas_call(
        body, in_specs=[vmem] * 4, out_specs=[vmem] * 4, out_shape=[out] * 4,
        scratch_shapes=[pltpu.VMEM((N_DEV, rows, length), F32), pltpu.SemaphoreType.DMA((N_DEV - 1,)),
                        pltpu.SemaphoreType.DMA((N_DEV - 1,))],
        name="small_allreduce_adamw",
    )(gvec, wvec, mvec, vvec)


_SMALL_SEGMENTS = (("a_log", GDN_HEADS), ("dt_bias", GDN_HEADS), ("gdn_norm_w", HEAD_DIM), ("pool_scale", GDN_WIDTH),
                   ("ln1_g", D_MODEL), ("ln1_b", D_MODEL), ("ln2_g", D_MODEL), ("ln2_b", D_MODEL),
                   ("ln3_g", D_MODEL), ("ln3_b", D_MODEL), ("conv_w", CONV_K * QKV_COLS))
_SMALL_ROWS = 8
_SMALL_LEN = -(-sum(sz for _, sz in _SMALL_SEGMENTS) // (_SMALL_ROWS * LANE)) * LANE


def _pack_small(vals):
    parts = [vals[n].reshape(-1).astype(F32) if n in vals else jnp.zeros((sz,), F32) for n, sz in _SMALL_SEGMENTS]
    flat = jnp.concatenate(parts)
    flat = jnp.pad(flat, (0, _SMALL_ROWS * _SMALL_LEN - flat.shape[0]))
    return flat.reshape(_SMALL_ROWS, _SMALL_LEN)


def _unpack_small(vec):
    flat = vec.reshape(-1)
    out, off = {}, 0
    for n, sz in _SMALL_SEGMENTS:
        out[n] = flat[off:off + sz]
        off += sz
    return out


_WEIGHT_ORDER = ("w_in", "conv_w", "a_log", "dt_bias", "gdn_norm_w", "pool_w", "pool_scale", "w_out", "ln1_g", "ln1_b",
                 "xq_w", "xk_w", "xv_w", "xo_w", "ln2_g", "ln2_b", "w_up", "w_down", "ln3_g", "ln3_b")


def _shard2d(name, a):
    if name == "w_in":
        return a.T
    return a.reshape(-1, a.shape[-1]) if name == "pool_w" else a


def _update_view(name, a):
    return jnp.transpose(a, (2, 0, 1)) if name == "w_in" else _shard2d(name, a[0])


def _shard_result(name, r, shape):
    return jnp.transpose(r, (1, 2, 0)) if name == "w_in" else r.reshape(shape)


def _gathered_to_full(name, gth):
    if name in ("w_up", "w_in"):
        return gth
    if name == "conv_w":
        return jnp.transpose(gth, (1, 0, 2)).reshape(gth.shape[1], N_DEV * gth.shape[2])
    if name == "pool_w":
        g4 = gth.reshape(N_DEV, POOL_GROUPS, POOL_GROUP_DIM // N_DEV, POOL_GROUP_DIM)
        return jnp.transpose(g4, (1, 0, 2, 3)).reshape(POOL_GROUPS, POOL_GROUP_DIM, POOL_GROUP_DIM)
    return gth.reshape(N_DEV * gth.shape[1], gth.shape[2])


def _full_to_chunks(name, full):
    if name == "w_up":
        return full
    if name == "pool_w":
        g4 = full.reshape(POOL_GROUPS, N_DEV, POOL_GROUP_DIM // N_DEV, POOL_GROUP_DIM)
        return jnp.transpose(g4, (1, 0, 2, 3)).reshape(N_DEV, POOL_GROUPS * POOL_GROUP_DIM // N_DEV, POOL_GROUP_DIM)
    return full.reshape(N_DEV, full.shape[0] // N_DEV, full.shape[1])


_GATHER_GROUPS = (("mixer", ("w_in", "conv_w", "pool_w")), ("attn", ("w_out", "xq_w", "xk_w", "xv_w", "xo_w")),
                  ("up", ("w_up",)), ("down", ("w_down",)))


def _grad_chunks(name, g):
    if name == "w_in":
        return _w_in_chunks(g.astype(BF16))
    return _full_to_chunks(name, g.astype(BF16))


def kernel(x, mem, w_in, conv_w, a_log, dt_bias, gdn_norm_w, pool_w, pool_scale, w_out, ln1_g, ln1_b, xq_w, xk_w, xv_w, xo_w, ln2_g, ln2_b, w_up, w_down, ln3_g, ln3_b, loss_target, m_w_in, m_conv_w, m_a_log, m_dt_bias, m_gdn_norm_w, m_pool_w, m_pool_scale, m_w_out, m_ln1_g, m_ln1_b, m_xq_w, m_xk_w, m_xv_w, m_xo_w, m_ln2_g, m_ln2_b, m_w_up, m_w_down, m_ln3_g, m_ln3_b, v_w_in, v_conv_w, v_a_log, v_dt_bias, v_gdn_norm_w, v_pool_w, v_pool_scale, v_w_out, v_ln1_g, v_ln1_b, v_xq_w, v_xk_w, v_xv_w, v_xo_w, v_ln2_g, v_ln2_b, v_w_up, v_w_down, v_ln3_g, v_ln3_b):
    args = dict(locals())
    wt = {n: args[n][0] for n in _WEIGHT_ORDER}
    mo = {n: args["m_" + n][0] for n in _WEIGHT_ORDER}
    vo = {n: args["v_" + n][0] for n in _WEIGHT_ORDER}

    me = _slot(*_place())
    me_arr = jnp.reshape(me, (1,)).astype(jnp.int32)
    nothing = jnp.zeros((8, LANE), F32)

    def landing_zones(names):
        shards = [_shard2d(n, wt[n]).astype(F32 if n == "conv_w" else BF16) for n in names]
        zones = [lax.dynamic_update_slice(lax.empty((N_DEV, *s.shape), s.dtype), s[None], (me, 0, 0)) for s in shards]
        return shards, zones

    chip_arr = jnp.reshape(me // 2, (1,)).astype(jnp.int32)
    core_arr = jnp.reshape(lax.axis_index("c"), (1,)).astype(jnp.int32)
    names_of = dict(_GATHER_GROUPS)
    gathers = {}
    prepared = {}

    def gather_chips(group, after):
        shards, zones = prepared.pop(group) if group in prepared else landing_zones(names_of[group])
        gathers[group] = _exchange_start("gather_chips", shards, zones, after, name="gather_chips_" + group)
        return gathers[group][4]

    def gather_pass(group, after):
        _, zones = _exchange_wait("gather_chips", gathers[group], after, name=f"gather_chips_{group}_wait")
        gathers[group] = _exchange_start("gather_pass", [], zones, nothing, name="gather_pass_" + group)
        return gathers[group][4]

    def gathered(group, after, token=None):
        _, zones = _exchange_wait("gather_pass", gathers[group], after, name=f"gather_pass_{group}_wait")
        full = {n: _gathered_to_full(n, z) for n, z in zip(names_of[group], zones)}
        full.update({n: wt[n] if token is None else wt[n] + token for n in _VECTORS})
        return _group_weights(group, full)

    token = gather_chips("mixer", nothing)
    x16 = _cast_bf16(x[0], name="cast_x")
    later = {group: landing_zones(names_of[group]) for group in ("attn", "up", "down")}
    token, x16, later = lax.optimization_barrier((token, x16, later))
    prepared.update(later)
    token = gather_chips("attn", gather_pass("mixer", token))

    def weights_of(group, after):
        if group == "mixer":
            return gathered(group, gathers["attn"][4])
        if group == "ahead":
            return gather_chips("up", gather_pass("attn", after))[0:1, 0:1]
        if group == "attn":
            weights = gathered(group, after)
            token = gather_chips("down", gather_pass("up", weights["w_out"]))[0, 0]
            return {n: (v + token if n == "ln1_g" else v) for n, v in weights.items()}
        if group == "up":
            weights = gathered(group, after)
            gather_pass("down", weights["w_up"])
            return weights
        return gathered(group, after)

    scatters = {}
    in_flight = []

    def chip_stage(after):
        group, names, started = in_flight.pop()
        chunks, from_sibling = _exchange_wait("scatter_sibling", started, after, name=f"scatter_sibling_{group}_wait")
        sums = [_chip_sums(c, f, core_arr, name=f"chip_sums_{n}") for n, c, f in zip(names, chunks, from_sibling)]
        scatters[group] = (names, _exchange_start("scatter_chips", sums, [lax.empty(s.shape, s.dtype) for s in sums],
                                                  nothing, name="scatter_chips_" + group))
        return scatters[group][1][4]

    def grads_ready(group, grads):
        if group == "tick":
            return chip_stage(grads["after"])[0:1, 0:1] if in_flight else None
        names = tuple(grads)
        chunks = [_grad_chunks(n, grads[n]) for n in names]
        token = chip_stage(chunks[0]) if in_flight else nothing
        zones = [lax.empty((N_CHIPS, *c.shape[1:]), c.dtype) for c in chunks]
        started = _exchange_start("scatter_sibling", chunks, zones, token, name="scatter_sibling_" + group)
        in_flight.append((group, names, started))
        return started[4][0:1, 0:1]

    sq, grad_x, g = _local_step(x[0], x16, mem[0], loss_target[0], weights_of, grads_ready)
    small = _finish_small_grads(g)

    out = {}
    after = chip_stage(grad_x)
    for group, (names, started) in scatters.items():
        sums, lands = _exchange_wait("scatter_chips", started, after, name=f"scatter_chips_{group}_wait")
        for n, parts, own in zip(names, lands, sums):
            res = _adamw_shard(parts, own, chip_arr, _update_view(n, args[n]), _update_view(n, args["m_" + n]),
                               _update_view(n, args["v_" + n]), name="adamw_" + n)
            out[n] = [_shard_result(n, r, args[n].shape) for r in res]
            after = res[1]

    packed, _ = lax.optimization_barrier((_pack_small(small), after))
    gs, ds, ms, vs = _small_allreduce_adamw(
        packed, _pack_small({n: wt[n] for n in _VECTORS}), _pack_small({n: mo[n] for n in _VECTORS}),
        _pack_small({n: vo[n] for n in _VECTORS}))
    gs, ds, ms, vs = _unpack_small(gs), _unpack_small(ds), _unpack_small(ms), _unpack_small(vs)
    cols = conv_w.shape[-1]
    conv_full = gs["conv_w"].reshape(CONV_K, QKV_COLS)
    conv_mine = lax.dynamic_slice(conv_full, (0, me * cols), (CONV_K, cols))[None]
    res = _adamw_shard(conv_mine, conv_mine, jnp.zeros((1,), jnp.int32), wt["conv_w"], mo["conv_w"], vo["conv_w"],
                       name="adamw_conv_w")
    out["conv_w"] = [r.reshape(conv_w.shape) for r in res]
    for n in _VECTORS:
        out[n] = [t[n].reshape(args[n].shape) for t in (gs, ds, ms, vs)]

    loss = lax.psum(0.5 * sq[0, 0] / D_MODEL, ("x", "y", "c"))
    return (loss, grad_x[None], *[out[n][0] for n in _WEIGHT_ORDER], *[out[n][1] for n in _WEIGHT_ORDER],
            *[out[n][2] for n in _WEIGHT_ORDER], *[out[n][3] for n in _WEIGHT_ORDER])
```

```python
import functools
import math

import jax
import jax.numpy as jnp
from jax import lax
from jax.experimental import pallas as pl
from jax.experimental.pallas import tpu as pltpu

F32 = jnp.float32
BF16 = jnp.bfloat16
MESH = pl.DeviceIdType.MESH

N_DEV = 8
D_MODEL = 2048
GDN_WIDTH = 1024
GDN_HEADS = 8
HEAD_DIM = 128
CONV_K = 4
CHUNK = 64
POOL_GROUPS = 4
POOL_GROUP_DIM = 256
MEM_LEN = 256
XATTN_HEADS = 4
XATTN_HEAD_DIM = 512
D_FF = 8192
IN_COLS = 5136
ALPHA = 2.0 ** 0.25
LN_EPS = 1e-5
NORM_EPS = 1e-6

LANE = 128
QKV_COLS = 3 * GDN_WIDTH
Z_OFF = QKV_COLS
BA_OFF = 4 * GDN_WIDTH
POOL_OFF = BA_OFF + 2 * LANE
PROJ_COLS = POOL_OFF + GDN_WIDTH
Z_BLK = Z_OFF // LANE
BA_BLK = BA_OFF // LANE
POOL_BLK = POOL_OFF // POOL_GROUP_DIM

ADAM_LR = 0.001
ADAM_B1 = 0.9
ADAM_B2 = 0.999
ADAM_EPS = 1e-08
ADAM_WD = 0.01
ADAM_STEP = 10

VMEM_LIMIT_BYTES = 48 * 1024 * 1024


def _params(*sem):
    return pltpu.CompilerParams(dimension_semantics=sem if sem else None, vmem_limit_bytes=VMEM_LIMIT_BYTES)


def _make_dots(cast, precision, batched=False):
    lead = 1 if batched else 0
    batch = ((0,), (0,)) if batched else ((), ())

    def dg(a, b, ca, cb):
        if cast is not None:
            a = a.astype(cast)
            b = b.astype(cast)
        return lax.dot_general(a, b, (((ca + lead,), (cb + lead,)), batch), precision=precision, preferred_element_type=F32)

    def nn_(a, b):
        return dg(a, b, 1, 0)

    def nt_(a, b):
        return dg(a, b, 1, 1)

    def tn_(a, b):
        return dg(a, b, 0, 0)

    @jax.custom_vjp
    def nn(a, b):
        return nn_(a, b)

    nn.defvjp(lambda a, b: (nn_(a, b), (a, b)), lambda r, g: (nt_(g, r[1]), tn_(r[0], g)))

    @jax.custom_vjp
    def nt(a, b):
        return nt_(a, b)

    nt.defvjp(lambda a, b: (nt_(a, b), (a, b)), lambda r, g: (nn_(g, r[1]), tn_(g, r[0])))

    @jax.custom_vjp
    def tn(a, b):
        return tn_(a, b)

    tn.defvjp(lambda a, b: (tn_(a, b), (a, b)), lambda r, g: (nt_(r[1], g), nn_(r[0], g)))

    return (nn_, nt_, tn_), (nn, nt, tn)


_BDOT_PLAIN, _BDOT_VJP = _make_dots(BF16, None)
_BDOT_BATCH_PLAIN, _BDOT_BATCH_VJP = _make_dots(BF16, None, batched=True)
_FDOT_BATCH_PLAIN, _FDOT_BATCH_VJP = _make_dots(None, lax.Precision.HIGH, batched=True)


def _mm(a, b, *, ta=False, tb=False, out_dtype=F32, tm=None, tn=512, tk=None, epi=None, extra=None, add_scale=1.0,
        b_chunks=False, o_chunks=False, name):
    m, k = (a.shape[1], a.shape[0]) if ta else a.shape
    if b_chunks:
        n, kb = (b.shape[1], N_DEV * b.shape[2]) if tb else (N_DEV * b.shape[2], b.shape[1])
    else:
        n, kb = b.shape if tb else (b.shape[1], b.shape[0])
    assert kb == k, (name, a.shape, b.shape)
    tm, tn, tk = min(tm or m, m), min(tn, n), min(tk or k, k)
    assert m % tm == 0 and n % tn == 0 and k % tk == 0, (name, m, n, k)
    nk = k // tk
    dims = (((0 if ta else 1,), (1 if tb else 0,)), ((), ()))
    n_extra = 0 if epi in (None, "relu2") else 1
    n_out = 2 if epi == "relu2" else 1
    if epi in ("relu2", "mul2r"):
        out_dtype = BF16

    def body(*refs):
        a_ref, b_ref = refs[:2]
        c_ref = refs[2] if n_extra else None
        o_refs = refs[2 + n_extra:2 + n_extra + n_out]
        scr = refs[2 + n_extra + n_out:]
        r = lax.dot_general(a_ref[...].astype(BF16), b_ref[...].astype(BF16), dims, preferred_element_type=F32)

        def finish(v):
            if epi == "add":
                o_refs[0][...] = (v + add_scale * c_ref[...]).astype(out_dtype)
            elif epi == "relu2":
                p = jnp.maximum(v, 0.0)
                o_refs[0][...] = (p * p).astype(BF16)
                o_refs[1][...] = p.astype(BF16)
            elif epi == "mul2r":
                o_refs[0][...] = (v * (2.0 * c_ref[...].astype(F32))).astype(BF16)
            else:
                o_refs[0][...] = v.astype(out_dtype)

        if nk == 1:
            finish(r)
        else:
            acc = scr[0]
            kk = pl.program_id(2)

            @pl.when(kk == 0)
            def _():
                acc[...] = r

            @pl.when(kk > 0)
            def _():
                acc[...] += r

            @pl.when(kk == nk - 1)
            def _():
                finish(acc[...])

    a_spec = pl.BlockSpec((tk, tm), lambda i, j, kk: (kk, i)) if ta else pl.BlockSpec((tm, tk), lambda i, j, kk: (i, kk))
    if b_chunks and tb:
        kc = k // N_DEV // tk
        b_spec = pl.BlockSpec((None, tn, tk), lambda i, j, kk: (kk // kc, j, kk % kc))
    elif b_chunks:
        nc = n // N_DEV // tn
        b_spec = pl.BlockSpec((None, tk, tn), lambda i, j, kk: (j // nc, kk, j % nc))
    elif tb:
        b_spec = pl.BlockSpec((tn, tk), lambda i, j, kk: (j, kk))
    else:
        b_spec = pl.BlockSpec((tk, tn), lambda i, j, kk: (kk, j))
    mn_spec = pl.BlockSpec((tm, tn), lambda i, j, kk: (i, j))
    if o_chunks:
        oc = n // N_DEV // tn
        o_spec = pl.BlockSpec((None, tm, tn), lambda i, j, kk: (j // oc, i, j % oc))
        o_shape = jax.ShapeDtypeStruct((N_DEV, m, n // N_DEV), out_dtype)
    else:
        o_spec, o_shape = mn_spec, jax.ShapeDtypeStruct((m, n), out_dtype)
    res = pl.pallas_call(
        body, grid=(m // tm, n // tn, nk), in_specs=[a_spec, b_spec] + [mn_spec] * n_extra,
        out_specs=[o_spec] * n_out, out_shape=[o_shape] * n_out,
        scratch_shapes=[pltpu.VMEM((tm, tn), F32)] if nk > 1 else [],
        compiler_params=_params("parallel", "parallel", "arbitrary"), name=name,
    )(a, b, *([extra] if n_extra else []))
    return res if n_out > 1 else res[0]


def _cast_bf16(v, *, name, tm=512):
    t, d = v.shape
    tm = min(tm, t)

    def body(v_ref, o_ref):
        o_ref[...] = v_ref[...].astype(BF16)

    spec = pl.BlockSpec((tm, d), lambda i: (i, 0))
    return pl.pallas_call(body, grid=(t // tm,), in_specs=[spec], out_specs=spec,
                          out_shape=jax.ShapeDtypeStruct((t, d), BF16), compiler_params=_params("parallel"), name=name)(v)


def _shift_down(v, s):
    if s == 0:
        return v
    row = lax.broadcasted_iota(jnp.int32, v.shape, 0)
    return jnp.where(row >= s, pltpu.roll(v, s, axis=0), 0.0)


def _shift_up(v, s):
    if s == 0:
        return v
    t = v.shape[0]
    row = lax.broadcasted_iota(jnp.int32, v.shape, 0)
    return jnp.where(row < t - s, pltpu.roll(v, t - s, axis=0), 0.0)


def _post_col(j):
    return (j % GDN_HEADS) * 3 + j // GDN_HEADS


def _gdn_prep_fwd(proj, conv_w):
    t = proj.shape[0]

    def body(x_ref, w_ref, o_ref):
        j = pl.program_id(0)
        x = x_ref[...]
        y = jnp.zeros_like(x)
        for tap in range(CONV_K):
            y = y + w_ref[tap:tap + 1, :] * _shift_down(x, CONV_K - 1 - tap)
        c = y * jax.nn.sigmoid(y)
        nrm = c * lax.rsqrt(jnp.sum(c * c, axis=1, keepdims=True) + NORM_EPS)
        o_ref[...] = jnp.where(j < 2 * GDN_HEADS, nrm, c)

    return pl.pallas_call(
        body, grid=(QKV_COLS // LANE,),
        in_specs=[pl.BlockSpec((t, LANE), lambda j: (0, j)), pl.BlockSpec((CONV_K, LANE), lambda j: (0, j))],
        out_specs=pl.BlockSpec((t, LANE), lambda j: (0, _post_col(j))),
        out_shape=jax.ShapeDtypeStruct((t, QKV_COLS), F32),
        compiler_params=_params("parallel"), name="gdn_prep_fwd",
    )(proj, conv_w)


def _gdn_prep_bwd(proj, conv_w, dpost, dproj):
    t = proj.shape[0]

    def body(x_ref, w_ref, d_ref, _, dx_ref, dw_ref):
        j = pl.program_id(0)
        x = x_ref[...]
        xs = [_shift_down(x, CONV_K - 1 - tap) for tap in range(CONV_K)]
        y = jnp.zeros_like(x)
        for tap in range(CONV_K):
            y = y + w_ref[tap:tap + 1, :] * xs[tap]
        sig = jax.nn.sigmoid(y)
        c = y * sig
        r = lax.rsqrt(jnp.sum(c * c, axis=1, keepdims=True) + NORM_EPS)
        nrm = c * r
        d = d_ref[...]
        dc_norm = r * (d - nrm * jnp.sum(d * nrm, axis=1, keepdims=True))
        dc = jnp.where(j < 2 * GDN_HEADS, dc_norm, d)
        dy = dc * (sig * (1.0 + y * (1.0 - sig)))
        dx = jnp.zeros_like(x)
        for tap in range(CONV_K):
            dx = dx + _shift_up(w_ref[tap:tap + 1, :] * dy, CONV_K - 1 - tap)
            dw_ref[tap:tap + 1, :] = jnp.sum(dy * xs[tap], axis=0, keepdims=True)
        dx_ref[...] = dx.astype(dx_ref.dtype)

    return pl.pallas_call(
        body, grid=(QKV_COLS // LANE,),
        in_specs=[pl.BlockSpec((t, LANE), lambda j: (0, j)), pl.BlockSpec((CONV_K, LANE), lambda j: (0, j)),
                  pl.BlockSpec((t, LANE), lambda j: (0, _post_col(j))), pl.BlockSpec(memory_space=pl.ANY)],
        out_specs=[pl.BlockSpec((t, LANE), lambda j: (0, j)), pl.BlockSpec((CONV_K, LANE), lambda j: (0, j))],
        out_shape=[jax.ShapeDtypeStruct(dproj.shape, dproj.dtype), jax.ShapeDtypeStruct((CONV_K, QKV_COLS), F32)],
        input_output_aliases={3: 0},
        compiler_params=_params("parallel"), name="gdn_prep_bwd",
    )(proj, conv_w, dpost, dproj)


def _softplus(v):
    return jnp.maximum(v, 0.0) + jnp.log(1.0 + jnp.exp(-jnp.abs(v)))


def _tri_inv(low, nn):
    r = lax.broadcasted_iota(jnp.int32, (CHUNK, CHUNK), 0)
    c = lax.broadcasted_iota(jnp.int32, (CHUNK, CHUNK), 1)
    eye = (r == c).astype(F32)
    same_blk = lax.shift_right_logical(r, 4) == lax.shift_right_logical(c, 4)
    diag = jnp.where(same_blk, low, 0.0)
    off = low - diag
    n1 = -diag
    n2 = nn(n1, n1)
    n4 = nn(n2, n2)
    n8 = nn(n4, n4)
    inv_d = nn(nn(nn(eye + n1, eye + n2), eye + n4), eye + n8)
    m1 = nn(inv_d, off)
    m2 = nn(m1, m1)
    return nn(nn(eye - m1, eye + m2), inv_d)


@jax.custom_vjp
def _tri_inv_known(low, t_inv):
    return t_inv


def _tri_inv_known_fwd(low, t_inv):
    return t_inv, t_inv


def _tri_inv_known_bwd(t_inv, g):
    _, nt, tn = _FDOT_BATCH_PLAIN
    return -nt(tn(t_inv, g), t_inv), jnp.zeros_like(t_inv)


_tri_inv_known.defvjp(_tri_inv_known_fwd, _tri_inv_known_bwd)


LOCAL_HEADS_PER_STEP = 8


def _gdn_local_fn(qkv, ba, alog_row, dtb_row, first_head, bdots, fdots, t_known=None):
    nn, nt, tn = bdots
    fnn = fdots[0]
    n_heads = qkv.shape[1] // (3 * HEAD_DIM)
    part = lambda i, p: qkv[:, (3 * i + p) * HEAD_DIM:(3 * i + p + 1) * HEAD_DIM]
    q = jnp.stack([part(i, 0) for i in range(n_heads)]) * (HEAD_DIM ** -0.5)
    k = jnp.stack([part(i, 1) for i in range(n_heads)])
    v = jnp.stack([part(i, 2) for i in range(n_heads)])
    lane = lax.broadcasted_iota(jnp.int32, ba.shape, 1)
    bg = jnp.where(lane < GDN_HEADS, jax.nn.sigmoid(ba), -jnp.exp(alog_row) * _softplus(ba + dtb_row))
    pick = lambda l: jnp.sum(jnp.where(lane == l, bg, 0.0), axis=1, keepdims=True)
    beta = jnp.stack([pick(first_head + i) for i in range(n_heads)])
    g = jnp.stack([pick(first_head + i + GDN_HEADS) for i in range(n_heads)])

    r = lax.broadcasted_iota(jnp.int32, (CHUNK, CHUNK), 0)
    c = lax.broadcasted_iota(jnp.int32, (CHUNK, CHUNK), 1)
    incl = r >= c
    strict = r > c
    eye = r == c

    def to_row(col):
        return jnp.sum(jnp.where(eye, col, 0.0), axis=1, keepdims=True)

    gc = jnp.sum(jnp.where(incl, to_row(g), 0.0), axis=2, keepdims=True)
    diff = gc - to_row(gc)
    decay = jnp.where(incl, jnp.exp(jnp.where(incl, diff, 0.0)), 0.0)
    k_beta = k * beta
    v_beta = v * beta
    low = jnp.where(strict, nt(k_beta, k) * decay, 0.0)
    t_inv = _tri_inv(low, fnn) if t_known is None else _tri_inv_known(low, t_known)
    eg = jnp.exp(gc)
    u = fnn(t_inv, v_beta)
    w = fnn(t_inv, k_beta * eg)
    attn = jnp.where(incl, nt(q, k) * decay, 0.0)
    last = lax.broadcasted_iota(jnp.int32, (CHUNK, 1), 0) == CHUNK - 1
    g_last = jnp.sum(jnp.where(last, gc, 0.0), axis=1, keepdims=True)
    kdec = k * jnp.exp(g_last - gc)
    elast = jnp.broadcast_to(jnp.exp(g_last), (n_heads, 1, LANE))
    return u, w, q * eg, kdec, attn, elast, t_inv


def _gdn_state_fn(u, w, qg, kdec, attn, elast, state, bdots):
    nn, _, tn = bdots
    v_new = u - nn(w, state)
    o = nn(qg, state) + nn(attn, v_new)
    return o, state * elast + tn(kdec, v_new)


def _gdn_local_fwd(post, proj, alog_row, dtb_row):
    t = post.shape[0]
    n_chunks = t // CHUNK
    hb = LOCAL_HEADS_PER_STEP

    def body(qkv_ref, ba_ref, al_ref, dt_ref, u_ref, w_ref, qg_ref, kd_ref, at_ref, el_ref, ti_ref):
        u, w, qg, kdec, attn, elast, t_inv = _gdn_local_fn(qkv_ref[...], ba_ref[...], al_ref[...], dt_ref[...],
                                                           pl.program_id(1) * hb, _BDOT_BATCH_PLAIN, _FDOT_BATCH_PLAIN)
        for i in range(hb):
            cols = slice(i * HEAD_DIM, (i + 1) * HEAD_DIM)
            u_ref[:, cols] = u[i]
            w_ref[:, cols] = w[i].astype(BF16)
            qg_ref[:, cols] = qg[i].astype(BF16)
            kd_ref[:, cols] = kdec[i].astype(BF16)
        at_ref[...] = attn.astype(BF16)
        el_ref[:, 0] = elast
        ti_ref[...] = t_inv

    wide = pl.BlockSpec((CHUNK, hb * HEAD_DIM), lambda n, j: (n, j))
    square = pl.BlockSpec((hb, CHUNK, CHUNK), lambda n, j: (j, n, 0))
    row = pl.BlockSpec((1, LANE), lambda n, j: (0, 0))
    res = pl.pallas_call(
        body, grid=(n_chunks, GDN_HEADS // hb),
        in_specs=[pl.BlockSpec((CHUNK, hb * 3 * HEAD_DIM), lambda n, j: (n, j)),
                  pl.BlockSpec((CHUNK, LANE), lambda n, j: (n, BA_BLK)), row, row],
        out_specs=[wide, wide, wide, wide, square, pl.BlockSpec((hb, 1, 1, LANE), lambda n, j: (j, n, 0, 0)), square],
        out_shape=[jax.ShapeDtypeStruct((t, GDN_WIDTH), F32), jax.ShapeDtypeStruct((t, GDN_WIDTH), BF16),
                   jax.ShapeDtypeStruct((t, GDN_WIDTH), BF16), jax.ShapeDtypeStruct((t, GDN_WIDTH), BF16),
                   jax.ShapeDtypeStruct((GDN_HEADS, t, CHUNK), BF16),
                   jax.ShapeDtypeStruct((GDN_HEADS, n_chunks, 1, LANE), F32),
                   jax.ShapeDtypeStruct((GDN_HEADS, t, CHUNK), F32)],
        compiler_params=_params("parallel", "parallel"), name="gdn_local_fwd",
    )(post, proj, alog_row, dtb_row)
    return tuple(res[:6]), res[6]


def _by_head(ref):
    return jnp.stack([ref[:, h * HEAD_DIM:(h + 1) * HEAD_DIM] for h in range(ref.shape[1] // HEAD_DIM)])


def _gdn_state_specs(n_of):
    wide = pl.BlockSpec((CHUNK, GDN_WIDTH), lambda n: (n_of(n), 0))
    attn = pl.BlockSpec((GDN_HEADS, CHUNK, CHUNK), lambda n: (0, n_of(n), 0))
    elast = pl.BlockSpec((GDN_HEADS, 1, 1, LANE), lambda n: (0, n_of(n), 0, 0))
    saved = pl.BlockSpec((GDN_HEADS, 1, HEAD_DIM, HEAD_DIM), lambda n: (0, n_of(n), 0, 0))
    return wide, attn, elast, saved


def _gdn_state_fwd(u, w, qg, kdec, attn, elast):
    t = u.shape[0]
    n_chunks = t // CHUNK

    def body(u_ref, w_ref, qg_ref, kd_ref, at_ref, el_ref, o_ref, save_ref, state_ref):
        @pl.when(pl.program_id(0) == 0)
        def _():
            state_ref[...] = jnp.zeros_like(state_ref)

        state = state_ref[...]
        save_ref[:, 0] = state
        o, new_state = _gdn_state_fn(_by_head(u_ref), _by_head(w_ref), _by_head(qg_ref), _by_head(kd_ref), at_ref[...],
                                     el_ref[:, 0], state, _BDOT_BATCH_PLAIN)
        for h in range(GDN_HEADS):
            o_ref[:, h * HEAD_DIM:(h + 1) * HEAD_DIM] = o[h]
        state_ref[...] = new_state

    wide, attn_spec, elast_spec, saved_spec = _gdn_state_specs(lambda n: n)
    return pl.pallas_call(
        body, grid=(n_chunks,), in_specs=[wide, wide, wide, wide, attn_spec, elast_spec],
        out_specs=[wide, saved_spec],
        out_shape=[jax.ShapeDtypeStruct((t, GDN_WIDTH), F32),
                   jax.ShapeDtypeStruct((GDN_HEADS, n_chunks, HEAD_DIM, HEAD_DIM), F32)],
        scratch_shapes=[pltpu.VMEM((GDN_HEADS, HEAD_DIM, HEAD_DIM), F32)],
        compiler_params=_params("arbitrary"), name="gdn_state_fwd",
    )(u, w, qg, kdec, attn, elast)


def _gdn_state_bwd(u, w, qg, kdec, attn, elast, saved, do):
    t = u.shape[0]
    n_chunks = t // CHUNK
    last = n_chunks - 1

    def body(u_ref, w_ref, qg_ref, kd_ref, at_ref, el_ref, save_ref, do_ref,
             du_ref, dw_ref, dqg_ref, dkd_ref, dat_ref, del_ref, dstate_ref):
        @pl.when(pl.program_id(0) == 0)
        def _():
            dstate_ref[...] = jnp.zeros_like(dstate_ref)

        _, vjp = jax.vjp(
            lambda *a: _gdn_state_fn(*a, _BDOT_BATCH_VJP), _by_head(u_ref), _by_head(w_ref).astype(F32),
            _by_head(qg_ref).astype(F32), _by_head(kd_ref).astype(F32), at_ref[...].astype(F32), el_ref[:, 0],
            save_ref[:, 0])
        du, dw, dqg, dkd, dat, de, dstate = vjp((_by_head(do_ref), dstate_ref[...]))
        for h in range(GDN_HEADS):
            cols = slice(h * HEAD_DIM, (h + 1) * HEAD_DIM)
            du_ref[:, cols] = du[h]
            dw_ref[:, cols] = dw[h]
            dqg_ref[:, cols] = dqg[h]
            dkd_ref[:, cols] = dkd[h]
        dat_ref[...] = dat
        del_ref[:, 0] = de
        dstate_ref[...] = dstate

    wide, attn_spec, elast_spec, saved_spec = _gdn_state_specs(lambda n: last - n)
    wide_f32 = jax.ShapeDtypeStruct((t, GDN_WIDTH), F32)
    return pl.pallas_call(
        body, grid=(n_chunks,), in_specs=[wide, wide, wide, wide, attn_spec, elast_spec, saved_spec, wide],
        out_specs=[wide, wide, wide, wide, attn_spec, elast_spec],
        out_shape=[wide_f32, wide_f32, wide_f32, wide_f32, jax.ShapeDtypeStruct((GDN_HEADS, t, CHUNK), F32),
                   jax.ShapeDtypeStruct((GDN_HEADS, n_chunks, 1, LANE), F32)],
        scratch_shapes=[pltpu.VMEM((GDN_HEADS, HEAD_DIM, HEAD_DIM), F32)],
        compiler_params=_params("arbitrary"), name="gdn_state_bwd",
    )(u, w, qg, kdec, attn, elast, saved, do)


def _gdn_local_bwd(post, proj, alog_row, dtb_row, t_inv, cots, dproj):
    t = post.shape[0]
    n_chunks = t // CHUNK
    hb = LOCAL_HEADS_PER_STEP
    n_steps = GDN_HEADS // hb

    def body(qkv_ref, ba_ref, al_ref, dt_ref, ti_ref, du_ref, dw_ref, dqg_ref, dkd_ref, dat_ref, del_ref, _,
             dqkv_ref, dba_ref, dal_ref, ddt_ref, dba_acc):
        n = pl.program_id(0)
        j = pl.program_id(1)

        @pl.when((n == 0) & (j == 0))
        def _():
            dal_ref[...] = jnp.zeros_like(dal_ref)
            ddt_ref[...] = jnp.zeros_like(ddt_ref)

        @pl.when(j == 0)
        def _():
            dba_acc[...] = jnp.zeros_like(dba_acc)

        t_known = ti_ref[...]
        _, vjp = jax.vjp(
            lambda a, b, c, d: _gdn_local_fn(a, b, c, d, j * hb, _BDOT_BATCH_VJP, _FDOT_BATCH_VJP, t_known)[:6],
            qkv_ref[...], ba_ref[...], al_ref[...], dt_ref[...])
        dqkv, dba, dal, ddt = vjp((_by_head(du_ref), _by_head(dw_ref), _by_head(dqg_ref), _by_head(dkd_ref), dat_ref[...],
                                   del_ref[:, 0]))
        dqkv_ref[...] = dqkv
        dba_acc[...] += dba
        dal_ref[...] += dal
        ddt_ref[...] += ddt

        @pl.when(j == n_steps - 1)
        def _():
            dba_ref[:, 0:LANE] = dba_acc[...].astype(dba_ref.dtype)
            dba_ref[:, LANE:2 * LANE] = jnp.zeros((CHUNK, LANE), dba_ref.dtype)

    wide = pl.BlockSpec((CHUNK, hb * HEAD_DIM), lambda n, j: (n, j))
    qkv_spec = pl.BlockSpec((CHUNK, hb * 3 * HEAD_DIM), lambda n, j: (n, j))
    row = pl.BlockSpec((1, LANE), lambda n, j: (0, 0))
    return pl.pallas_call(
        body, grid=(n_chunks, n_steps),
        in_specs=[qkv_spec, pl.BlockSpec((CHUNK, LANE), lambda n, j: (n, BA_BLK)), row, row,
                  pl.BlockSpec((hb, CHUNK, CHUNK), lambda n, j: (j, n, 0)), wide, wide, wide, wide,
                  pl.BlockSpec((hb, CHUNK, CHUNK), lambda n, j: (j, n, 0)),
                  pl.BlockSpec((hb, 1, 1, LANE), lambda n, j: (j, n, 0, 0)), pl.BlockSpec(memory_space=pl.ANY)],
        out_specs=[qkv_spec, pl.BlockSpec((CHUNK, 2 * LANE), lambda n, j: (n, BA_BLK // 2)), row, row],
        out_shape=[jax.ShapeDtypeStruct((t, QKV_COLS), F32), jax.ShapeDtypeStruct(dproj.shape, dproj.dtype),
                   jax.ShapeDtypeStruct((1, LANE), F32), jax.ShapeDtypeStruct((1, LANE), F32)],
        input_output_aliases={11: 1},
        scratch_shapes=[pltpu.VMEM((CHUNK, LANE), F32)],
        compiler_params=_params("arbitrary", "arbitrary"), name="gdn_local_bwd",
    )(post, proj, alog_row, dtb_row, t_inv, *cots, dproj)


def _onorm_fn(o, z, w):
    return o * lax.rsqrt(jnp.mean(o * o, axis=1, keepdims=True) + NORM_EPS) * w * (z * jax.nn.sigmoid(z))


def _onorm_fwd(o_raw, proj, norm_w, mixin, tm=512):
    t = o_raw.shape[0]
    tm = min(tm, t)

    def body(o_ref, z_ref, w_ref, _, out_ref):
        out_ref[...] = _onorm_fn(o_ref[...], z_ref[...], w_ref[...]).astype(out_ref.dtype)

    return pl.pallas_call(
        body, grid=(t // tm, GDN_HEADS),
        in_specs=[pl.BlockSpec((tm, LANE), lambda i, h: (i, h)), pl.BlockSpec((tm, LANE), lambda i, h: (i, Z_BLK + h)),
                  pl.BlockSpec((1, LANE), lambda i, h: (0, 0)), pl.BlockSpec(memory_space=pl.ANY)],
        out_specs=pl.BlockSpec((tm, LANE), lambda i, h: (i, h)),
        out_shape=jax.ShapeDtypeStruct(mixin.shape, mixin.dtype), input_output_aliases={3: 0},
        compiler_params=_params("parallel", "parallel"), name="gdn_onorm_fwd",
    )(o_raw, proj, norm_w, mixin)


def _onorm_bwd(o_raw, proj, norm_w, dmixin, dproj, tm=512):
    t = o_raw.shape[0]
    tm = min(tm, t)

    def body(o_ref, z_ref, w_ref, d_ref, _, do_ref, dz_ref, dw_ref):
        @pl.when((pl.program_id(0) == 0) & (pl.program_id(1) == 0))
        def _():
            dw_ref[...] = jnp.zeros_like(dw_ref)

        _, vjp = jax.vjp(_onorm_fn, o_ref[...], z_ref[...], w_ref[...])
        do, dz, dw = vjp(d_ref[...])
        do_ref[...] = do
        dz_ref[...] = dz.astype(dz_ref.dtype)
        dw_ref[...] += dw

    return pl.pallas_call(
        body, grid=(t // tm, GDN_HEADS),
        in_specs=[pl.BlockSpec((tm, LANE), lambda i, h: (i, h)), pl.BlockSpec((tm, LANE), lambda i, h: (i, Z_BLK + h)),
                  pl.BlockSpec((1, LANE), lambda i, h: (0, 0)), pl.BlockSpec((tm, LANE), lambda i, h: (i, h)),
                  pl.BlockSpec(memory_space=pl.ANY)],
        out_specs=[pl.BlockSpec((tm, LANE), lambda i, h: (i, h)), pl.BlockSpec((tm, LANE), lambda i, h: (i, Z_BLK + h)),
                   pl.BlockSpec((1, LANE), lambda i, h: (0, 0))],
        out_shape=[jax.ShapeDtypeStruct((t, GDN_WIDTH), F32), jax.ShapeDtypeStruct(dproj.shape, dproj.dtype),
                   jax.ShapeDtypeStruct((1, LANE), F32)],
        input_output_aliases={4: 1},
        compiler_params=_params("arbitrary", "arbitrary"), name="gdn_onorm_bwd",
    )(o_raw, proj, norm_w, dmixin, dproj)


def _pool_select(levels, gi):
    out = levels[-1]
    for lvl in range(len(levels) - 2, -1, -1):
        out = jnp.where(gi == lvl, levels[lvl], out)
    return out


def _pool_count(shape, gi):
    pos = lax.broadcasted_iota(jnp.int32, shape, 0)
    win = lax.shift_left(jnp.int32(2), gi)
    return jnp.minimum(pos + 1, win).astype(F32)


def _pooled(p, gi):
    acc = p
    levels = []
    for lvl in range(POOL_GROUPS):
        acc = acc + _shift_down(acc, 1 << lvl)
        levels.append(acc)
    return _pool_select(levels, gi) / _pool_count(p.shape, gi) - p


def _pool_fwd(proj, pool_w, pool_scale):
    t = proj.shape[0]

    def body(p_ref, w_ref, s_ref, out_ref):
        gi = pl.program_id(0)
        pooled = _pooled(p_ref[...], gi)
        out_ref[...] = (_BDOT_PLAIN[0](pooled, w_ref[0]) * s_ref[0]).astype(out_ref.dtype)

    return pl.pallas_call(
        body, grid=(POOL_GROUPS,),
        in_specs=[pl.BlockSpec((t, POOL_GROUP_DIM), lambda g: (0, POOL_BLK + g)),
                  pl.BlockSpec((1, POOL_GROUP_DIM, POOL_GROUP_DIM), lambda g: (g, 0, 0)),
                  pl.BlockSpec((1, 1, POOL_GROUP_DIM), lambda g: (g, 0, 0))],
        out_specs=pl.BlockSpec((t, POOL_GROUP_DIM), lambda g: (0, GDN_WIDTH // POOL_GROUP_DIM + g)),
        out_shape=jax.ShapeDtypeStruct((t, 2 * GDN_WIDTH), BF16),
        compiler_params=_params("parallel"), name="pool_fwd",
    )(proj, pool_w, pool_scale)


def _pool_bwd(proj, pool_w, pool_scale, dmixin):
    t = proj.shape[0]
    nn, nt, tn = _BDOT_PLAIN

    def body(p_ref, w_ref, s_ref, d_ref, dp_ref, dw_ref, ds_ref):
        gi = pl.program_id(0)
        p = p_ref[...]
        pooled = _pooled(p, gi)
        mixed = nn(pooled, w_ref[0])
        d = d_ref[...]
        ds_ref[0] = jnp.sum(d * mixed, axis=0, keepdims=True)
        dmixed = d * s_ref[0]
        dw_ref[0] = tn(pooled, dmixed)
        dpooled = nt(dmixed, w_ref[0])
        acc = dpooled / _pool_count(p.shape, gi)
        levels = []
        for lvl in range(POOL_GROUPS):
            acc = acc + _shift_up(acc, 1 << lvl)
            levels.append(acc)
        dp_ref[...] = (_pool_select(levels, gi) - dpooled).astype(dp_ref.dtype)

    return pl.pallas_call(
        body, grid=(POOL_GROUPS,),
        in_specs=[pl.BlockSpec((t, POOL_GROUP_DIM), lambda g: (0, POOL_BLK + g)),
                  pl.BlockSpec((1, POOL_GROUP_DIM, POOL_GROUP_DIM), lambda g: (g, 0, 0)),
                  pl.BlockSpec((1, 1, POOL_GROUP_DIM), lambda g: (g, 0, 0)),
                  pl.BlockSpec((t, POOL_GROUP_DIM), lambda g: (0, GDN_WIDTH // POOL_GROUP_DIM + g))],
        out_specs=[pl.BlockSpec((t, POOL_GROUP_DIM), lambda g: (0, POOL_BLK + g)),
                   pl.BlockSpec((1, POOL_GROUP_DIM, POOL_GROUP_DIM), lambda g: (g, 0, 0)),
                   pl.BlockSpec((1, 1, POOL_GROUP_DIM), lambda g: (g, 0, 0))],
        out_shape=[jax.ShapeDtypeStruct((t, PROJ_COLS), BF16),
                   jax.ShapeDtypeStruct((POOL_GROUPS, POOL_GROUP_DIM, POOL_GROUP_DIM), F32),
                   jax.ShapeDtypeStruct((POOL_GROUPS, 1, POOL_GROUP_DIM), F32)],
        compiler_params=_params("parallel"), name="pool_bwd",
    )(proj, pool_w, pool_scale, dmixin)


def _ln_stats(s):
    mu = jnp.mean(s, axis=1, keepdims=True)
    xc = s - mu
    var = jnp.mean(xc * xc, axis=1, keepdims=True)
    rstd = lax.rsqrt(var + LN_EPS)
    return xc * rstd, rstd


def _ln_fwd(h_in, y, g, b, *, name, tm=256):
    t, d = h_in.shape
    tm = min(tm, t)

    def body(h_ref, y_ref, g_ref, b_ref, o_ref, o16_ref):
        xhat, _ = _ln_stats(ALPHA * h_ref[...] + y_ref[...])
        out = xhat * g_ref[...] + b_ref[...]
        o_ref[...] = out
        o16_ref[...] = out.astype(BF16)

    row = pl.BlockSpec((tm, d), lambda i: (i, 0))
    vec = pl.BlockSpec((1, d), lambda i: (0, 0))
    return pl.pallas_call(
        body, grid=(t // tm,), in_specs=[row, row, vec, vec], out_specs=[row, row],
        out_shape=[jax.ShapeDtypeStruct((t, d), F32), jax.ShapeDtypeStruct((t, d), BF16)],
        compiler_params=_params("parallel"), name=name,
    )(h_in, y, g, b)


def _ln_backward(xhat, rstd, dout, gain):
    dxhat = dout * gain
    m1 = jnp.mean(dxhat, axis=1, keepdims=True)
    m2 = jnp.mean(dxhat * xhat, axis=1, keepdims=True)
    return (rstd * (dxhat - m1 - xhat * m2), jnp.sum(dout * xhat, axis=0, keepdims=True),
            jnp.sum(dout, axis=0, keepdims=True))


def _ln_loss(h_in, y, g, b, target, *, name, tm=256):
    t, d = h_in.shape
    tm = min(tm, t)

    def body(h_ref, y_ref, g_ref, b_ref, t_ref, sq_ref, ds_ref, ds16_ref, dg_ref, dbias_ref):
        @pl.when(pl.program_id(0) == 0)
        def _():
            sq_ref[...] = jnp.zeros_like(sq_ref)
            dg_ref[...] = jnp.zeros_like(dg_ref)
            dbias_ref[...] = jnp.zeros_like(dbias_ref)

        xhat, rstd = _ln_stats(ALPHA * h_ref[...] + y_ref[...])
        err = xhat * g_ref[...] + b_ref[...] - t_ref[...]
        sq_ref[...] += jnp.sum(jnp.sum(err * err, axis=1, keepdims=True), axis=0, keepdims=True)
        ds, dg, dbias = _ln_backward(xhat, rstd, err * (1.0 / d), g_ref[...])
        ds_ref[...] = ds
        ds16_ref[...] = ds.astype(BF16)
        dg_ref[...] += dg
        dbias_ref[...] += dbias

    row = pl.BlockSpec((tm, d), lambda i: (i, 0))
    vec = pl.BlockSpec((1, d), lambda i: (0, 0))
    return pl.pallas_call(
        body, grid=(t // tm,), in_specs=[row, row, vec, vec, row],
        out_specs=[pl.BlockSpec((1, LANE), lambda i: (0, 0)), row, row, vec, vec],
        out_shape=[jax.ShapeDtypeStruct((1, LANE), F32), jax.ShapeDtypeStruct((t, d), F32),
                   jax.ShapeDtypeStruct((t, d), BF16), jax.ShapeDtypeStruct((1, d), F32), jax.ShapeDtypeStruct((1, d), F32)],
        compiler_params=_params("arbitrary"), name=name,
    )(h_in, y, g, b, target)


def _ln_bwd(h_in, y, g, d_a, d_b, *, name, tm=256):
    t, d = h_in.shape
    tm = min(tm, t)
    has_b = d_b is not None

    def body(*refs):
        if has_b:
            h_ref, y_ref, g_ref, da_ref, db_ref, ds_ref, ds16_ref, dg_ref, dbias_ref = refs
        else:
            h_ref, y_ref, g_ref, da_ref, ds_ref, ds16_ref, dg_ref, dbias_ref = refs

        @pl.when(pl.program_id(0) == 0)
        def _():
            dg_ref[...] = jnp.zeros_like(dg_ref)
            dbias_ref[...] = jnp.zeros_like(dbias_ref)

        xhat, rstd = _ln_stats(ALPHA * h_ref[...] + y_ref[...])
        dout = da_ref[...]
        if has_b:
            dout = dout + ALPHA * db_ref[...]
        ds, dg, dbias = _ln_backward(xhat, rstd, dout, g_ref[...])
        ds_ref[...] = ds
        ds16_ref[...] = ds.astype(BF16)
        dg_ref[...] += dg
        dbias_ref[...] += dbias

    row = pl.BlockSpec((tm, d), lambda i: (i, 0))
    vec = pl.BlockSpec((1, d), lambda i: (0, 0))
    args = [h_in, y, g, d_a] + ([d_b] if has_b else [])
    return pl.pallas_call(
        body, grid=(t // tm,), in_specs=[row, row, vec, row] + ([row] if has_b else []),
        out_specs=[row, row, vec, vec],
        out_shape=[jax.ShapeDtypeStruct((t, d), F32), jax.ShapeDtypeStruct((t, d), BF16),
                   jax.ShapeDtypeStruct((1, d), F32), jax.ShapeDtypeStruct((1, d), F32)],
        compiler_params=_params("arbitrary"), name=name,
    )(*args)


def _attn_fn(q, k, v, dots):
    nn, nt, _ = dots
    s = nt(q, k) * (XATTN_HEAD_DIM ** -0.5)
    s = s - lax.stop_gradient(jnp.max(s, axis=1, keepdims=True))
    e = jnp.exp(s)
    p = e / jnp.sum(e, axis=1, keepdims=True)
    return nn(p, v)


def _attn_fwd(q, k, v, tq=512):
    t = q.shape[0]
    tq = min(tq, t)

    def body(q_ref, k_ref, v_ref, o_ref):
        o_ref[...] = _attn_fn(q_ref[...], k_ref[...], v_ref[...], _BDOT_PLAIN).astype(BF16)

    qs = pl.BlockSpec((tq, XATTN_HEAD_DIM), lambda h, i: (i, h))
    ks = pl.BlockSpec((MEM_LEN, XATTN_HEAD_DIM), lambda h, i: (0, h))
    return pl.pallas_call(
        body, grid=(XATTN_HEADS, t // tq), in_specs=[qs, ks, ks], out_specs=qs,
        out_shape=jax.ShapeDtypeStruct(q.shape, BF16), compiler_params=_params("parallel", "parallel"), name="xattn_fwd",
    )(q, k, v)


def _attn_bwd(q, k, v, do, tq=512):
    t = q.shape[0]
    tq = min(tq, t)

    def body(q_ref, k_ref, v_ref, do_ref, dq_ref, dk_ref, dv_ref):
        @pl.when(pl.program_id(1) == 0)
        def _():
            dk_ref[...] = jnp.zeros_like(dk_ref)
            dv_ref[...] = jnp.zeros_like(dv_ref)

        _, vjp = jax.vjp(lambda a, b, c: _attn_fn(a, b, c, _BDOT_VJP), q_ref[...].astype(F32), k_ref[...].astype(F32),
                         v_ref[...].astype(F32))
        dq, dk, dv = vjp(do_ref[...].astype(F32))
        dq_ref[...] = dq.astype(BF16)
        dk_ref[...] += dk
        dv_ref[...] += dv

    qs = pl.BlockSpec((tq, XATTN_HEAD_DIM), lambda h, i: (i, h))
    ks = pl.BlockSpec((MEM_LEN, XATTN_HEAD_DIM), lambda h, i: (0, h))
    return pl.pallas_call(
        body, grid=(XATTN_HEADS, t // tq), in_specs=[qs, ks, ks, qs], out_specs=[qs, ks, ks],
        out_shape=[jax.ShapeDtypeStruct(q.shape, BF16), jax.ShapeDtypeStruct(k.shape, F32), jax.ShapeDtypeStruct(v.shape, F32)],
        compiler_params=_params("parallel", "arbitrary"), name="xattn_bwd",
    )(q, k, v, do)


def _local_step(x, x16, mem, target, weights_of, grads_ready):
    def behind(vec, token):
        return vec if token is None else vec + token

    w = dict(weights_of("mixer", None))
    proj = _mm(x16, w["w_in"], tb=True, tn=768, name="mm_in_proj")
    mixin = _pool_fwd(proj, w["pool_w"], w["pool_scale"])
    post = _gdn_prep_fwd(proj, w["conv_w"])
    token = weights_of("ahead", post)
    chunked, t_inv = _gdn_local_fwd(post, proj, behind(w["alog_row"], token), w["dtb_row"])
    o_raw, saved = _gdn_state_fwd(*chunked)
    mixin = _onorm_fwd(o_raw, proj, w["gdn_norm_w"], mixin)
    w.update(weights_of("attn", mixin))
    mix = _mm(mixin, w["w_out"], name="mm_out_proj")
    h1, h1_16 = _ln_fwd(x, mix, w["ln1_g"], w["ln1_b"], name="ln1_fwd")
    xq = _mm(h1_16, w["xq_w"], out_dtype=BF16, name="mm_xq")
    xk = _mm(mem, w["xk_w"], out_dtype=BF16, name="mm_xk")
    xv = _mm(mem, w["xv_w"], out_dtype=BF16, name="mm_xv")
    xo = _attn_fwd(xq, xk, xv)
    xa = _mm(xo, w["xo_w"], name="mm_xo")
    h2, h2_16 = _ln_fwd(h1, xa, w["ln2_g"], w["ln2_b"], name="ln2_fwd")
    w.update(weights_of("up", h2_16))
    act, relu = _mm(h2_16, w["w_up"], b_chunks=True, epi="relu2", name="mm_up")
    w.update(weights_of("down", act))
    ff = _mm(act, w["w_down"], tn=512, tk=2048, name="mm_down")
    g = {}
    sq, ds3, ds3_16, g["ln3_g"], g["ln3_b"] = _ln_loss(h2, ff, w["ln3_g"], w["ln3_b"], target, name="ln3_loss")

    gw_down = _mm(act, ds3_16, ta=True, out_dtype=BF16, tm=512, tn=D_MODEL, name="mm_gw_down")
    du = _mm(ds3_16, w["w_down"], tb=True, epi="mul2r", extra=relu, name="mm_du")
    gw_up = _mm(h2_16, du, ta=True, out_dtype=BF16, o_chunks=True, name="mm_gw_up")
    token = grads_ready("mlp", {"w_down": gw_down, "w_up": gw_up})
    dh2 = _mm(du, w["w_up"], tb=True, b_chunks=True, tn=1024, tk=1024, name="mm_dh2")
    ds2, ds2_16, g["ln2_g"], g["ln2_b"] = _ln_bwd(h1, xa, behind(w["ln2_g"], token), dh2, ds3, name="ln2_bwd")
    gw_xo = _mm(xo, ds2_16, ta=True, out_dtype=BF16, name="mm_gw_xo")
    dxo = _mm(ds2_16, w["xo_w"], tb=True, out_dtype=BF16, name="mm_dxo")
    dxq, dxk, dxv = _attn_bwd(xq, xk, xv, dxo)
    gw_xq = _mm(h1_16, dxq, ta=True, out_dtype=BF16, name="mm_gw_xq")
    gw_xk = _mm(mem, dxk, ta=True, out_dtype=BF16, name="mm_gw_xk")
    gw_xv = _mm(mem, dxv, ta=True, out_dtype=BF16, name="mm_gw_xv")
    token = grads_ready("attn", {"xo_w": gw_xo, "xq_w": gw_xq, "xk_w": gw_xk, "xv_w": gw_xv})
    dh1 = _mm(dxq, w["xq_w"], tb=True, name="mm_dh1")
    ds1, ds1_16, g["ln1_g"], g["ln1_b"] = _ln_bwd(x, mix, behind(w["ln1_g"], token), dh1, ds2, name="ln1_bwd")
    gw_out = _mm(mixin, ds1_16, ta=True, out_dtype=BF16, name="mm_gw_out")
    dmixin = _mm(ds1_16, w["w_out"], tb=True, name="mm_dmixin")
    dproj, gw_pool, g["pool_scale"] = _pool_bwd(proj, w["pool_w"], w["pool_scale"], dmixin)
    token = grads_ready("mix", {"w_out": gw_out, "pool_w": gw_pool})
    do_raw, dproj, g["gdn_norm_w"] = _onorm_bwd(o_raw, proj, behind(w["gdn_norm_w"], token), dmixin, dproj)
    cots = _gdn_state_bwd(*chunked, saved, do_raw)
    token = grads_ready("tick", {"after": cots[0]})
    dpost, dproj, g["alog_row"], g["dtb_row"] = _gdn_local_bwd(post, proj, behind(w["alog_row"], token), w["dtb_row"],
                                                               t_inv, cots, dproj)
    dproj, g["conv_w"] = _gdn_prep_bwd(proj, w["conv_w"], dpost, dproj)
    gw_in = _mm(dproj, x16, ta=True, out_dtype=BF16, tm=768, tn=D_MODEL, name="mm_gw_in")
    token = grads_ready("in", {"w_in": gw_in})
    if token is not None:
        ds1, _ = lax.optimization_barrier((ds1, token))
    grad_x = _mm(dproj, w["w_in"], tk=1792, epi="add", extra=ds1, add_scale=ALPHA, name="mm_dx")
    return sq, grad_x, g


_MATRICES = ("w_in", "pool_w", "w_out", "xq_w", "xk_w", "xv_w", "xo_w", "w_up", "w_down")
_VECTORS = ("a_log", "dt_bias", "gdn_norm_w", "pool_scale", "ln1_g", "ln1_b", "ln2_g", "ln2_b", "ln3_g", "ln3_b")
_BA_SPLIT = BA_OFF + 2 * GDN_HEADS


def _lane_row(v, offset):
    return jnp.zeros((1, LANE), F32).at[0, offset:offset + v.shape[0]].set(v)


_GROUP_VECTORS = {"mixer": (), "attn": ("ln1_g", "ln1_b", "ln2_g", "ln2_b"), "up": (), "down": ("ln3_g", "ln3_b")}


def _group_weights(group, full):
    w = {n: full[n].reshape(1, D_MODEL) for n in _GROUP_VECTORS[group]}
    if group == "mixer":
        w_in = full["w_in"].reshape(IN_COLS, D_MODEL)
        zeros = jnp.zeros((POOL_OFF - _BA_SPLIT, D_MODEL), w_in.dtype)
        w.update({
            "w_in": jnp.concatenate([w_in[:_BA_SPLIT], zeros, w_in[_BA_SPLIT:]], axis=0),
            "conv_w": full["conv_w"],
            "alog_row": _lane_row(full["a_log"], GDN_HEADS),
            "dtb_row": _lane_row(full["dt_bias"], GDN_HEADS),
            "gdn_norm_w": full["gdn_norm_w"].reshape(1, LANE),
            "pool_w": full["pool_w"],
            "pool_scale": full["pool_scale"].reshape(POOL_GROUPS, 1, POOL_GROUP_DIM),
        })
    else:
        w.update({n: full[n] for n in dict(_GATHER_GROUPS)[group]})
    return w


def _w_in_chunks(g):
    unpadded = jnp.concatenate([g[:_BA_SPLIT], g[POOL_OFF:]], axis=0)
    return unpadded.reshape(N_DEV, IN_COLS // N_DEV, D_MODEL)


def _finish_small_grads(g):
    out = {"conv_w": g["conv_w"]}
    out["a_log"] = g["alog_row"][0, GDN_HEADS:2 * GDN_HEADS]
    out["dt_bias"] = g["dtb_row"][0, GDN_HEADS:2 * GDN_HEADS]
    out["gdn_norm_w"] = g["gdn_norm_w"].reshape(LANE)
    out["pool_scale"] = g["pool_scale"].reshape(POOL_GROUPS * POOL_GROUP_DIM)
    for n in ("ln1_g", "ln1_b", "ln2_g", "ln2_b", "ln3_g", "ln3_b"):
        out[n] = g[n].reshape(D_MODEL)
    return out


def _adamw_math(w, g, m, v):
    m = ADAM_B1 * m + (1.0 - ADAM_B1) * g
    v = ADAM_B2 * v + (1.0 - ADAM_B2) * (g * g)
    m_hat = m / (1.0 - ADAM_B1 ** ADAM_STEP)
    v_hat = v / (1.0 - ADAM_B2 ** ADAM_STEP)
    delta = -ADAM_LR * (m_hat / (jnp.sqrt(v_hat) + ADAM_EPS) + ADAM_WD * w)
    return delta, m, v


ADAMW_TILE_ELEMS = 256 * 1024
CHIP_SUM_TILE_ELEMS = 1024 * 1024


def _shard_tile(r, c, elems):
    for rows in (1024, 512, 256, 128):
        if r % rows == 0 and rows * c <= elems:
            return rows, c
    if r % 128 == 0:
        return 128, c
    return r, 256 if c % 256 == 0 else c


def _adamw_shard(parts, own, me, w, m, v, *, name):
    s, r, c = parts.shape
    tr, tc = _shard_tile(r, c, ADAMW_TILE_ELEMS)
    assert r % tr == 0 and c % tc == 0, (name, r, c)
    unit_axis = w.ndim == 3
    at = (slice(None), 0, slice(None)) if unit_axis else Ellipsis

    def body(me_ref, p_ref, own_ref, w_ref, m_ref, v_ref, g_ref, d_ref, nm_ref, nv_ref):
        mine = own_ref[...].astype(F32)
        g = None
        for i in range(s):
            part = jnp.where(me_ref[0] == i, mine, p_ref[i].astype(F32))
            g = part if g is None else g + part
        delta, nm, nv = _adamw_math(w_ref[at], g, m_ref[at], v_ref[at])
        g_ref[at] = g
        d_ref[at] = delta
        nm_ref[at] = nm
        nv_ref[at] = nv

    if unit_axis:
        blk = pl.BlockSpec((tr, 1, tc), lambda i, j, me_ref: (i, 0, j))
        out = jax.ShapeDtypeStruct((r, 1, c), F32)
    else:
        blk = pl.BlockSpec((tr, tc), lambda i, j, me_ref: (i, j))
        out = jax.ShapeDtypeStruct((r, c), F32)
    return pl.pallas_call(
        body,
        grid_spec=pltpu.PrefetchScalarGridSpec(
            num_scalar_prefetch=1, grid=(r // tr, c // tc),
            in_specs=[pl.BlockSpec((s, tr, tc), lambda i, j, me_ref: (0, i, j)),
                      pl.BlockSpec((None, tr, tc), lambda i, j, me_ref: (me_ref[0], i, j)), blk, blk, blk],
            out_specs=[blk, blk, blk, blk]),
        out_shape=[out, out, out, out], compiler_params=_params("parallel", "parallel"), name=name,
    )(me, parts, own, w, m, v)


N_CHIPS = N_DEV // 2


def _chip_sums(chunks, from_sibling, core, *, name):
    _, r, c = chunks.shape
    tr, tc = _shard_tile(r, c, CHIP_SUM_TILE_ELEMS)
    assert r % tr == 0 and c % tc == 0, (name, r, c)

    def body(core_ref, mine_ref, other_ref, o_ref):
        o_ref[...] = (mine_ref[...].astype(F32) + other_ref[...].astype(F32)).astype(o_ref.dtype)

    by_chip = pl.BlockSpec((None, tr, tc), lambda q, i, j, core_ref: (q, i, j))
    return pl.pallas_call(
        body,
        grid_spec=pltpu.PrefetchScalarGridSpec(
            num_scalar_prefetch=1, grid=(N_CHIPS, r // tr, c // tc),
            in_specs=[pl.BlockSpec((None, tr, tc), lambda q, i, j, core_ref: (2 * q + core_ref[0], i, j)), by_chip],
            out_specs=by_chip),
        out_shape=jax.ShapeDtypeStruct((N_CHIPS, r, c), chunks.dtype),
        compiler_params=_params("parallel", "parallel", "parallel"), name=name,
    )(core, chunks, from_sibling)


def _place():
    return lax.axis_index("x"), lax.axis_index("y"), lax.axis_index("c")


def _slot(px, py, pc):
    return 4 * px + 2 * py + pc


_HBM = pl.BlockSpec(memory_space=pltpu.HBM)


_SEM = pl.BlockSpec(memory_space=pltpu.SEMAPHORE)
_ANY = pl.BlockSpec(memory_space=pl.ANY)
_EFFECT = pltpu.SideEffectType.DATAFLOW_SIDE_EFFECTING
_N_PEERS = N_DEV - 1


def _peer(k, x, y, c):
    return (1 - x if k & 4 else x, 1 - y if k & 2 else y, 1 - c if k & 1 else c)


_EXCHANGE_BITS = {"gather_chips": (1, 2, 4, 6), "gather_pass": (2, 4, 6), "scatter_sibling": (1, 1, 1, 1),
                  "scatter_chips": (2, 4, 6)}


def _exchange_copy(mode, src, land, w, i, place, send_sems, recv_sems, receiving):
    bits = _EXCHANGE_BITS[mode]
    k = bits[i]
    peer = _peer(k, *place)
    me = _slot(*place)
    if mode == "gather_chips":
        to, src_ref, sent_to, got_at = peer, src[w], me, _slot(*peer)
    elif mode == "gather_pass":
        blk = _slot(*peer)
        to, src_ref, sent_to, got_at = _peer(1, *place), land[w].at[blk], blk, _slot(*_peer(k | 1, *place))
    elif mode == "scatter_sibling":
        to, src_ref, sent_to, got_at = peer, src[w].at[2 * i + 1 - place[2]], i, i
    else:
        to, src_ref, sent_to, got_at = peer, src[w].at[_slot(*peer) // 2], me // 2, _slot(*peer) // 2
    sem = w * len(bits) + i
    return pltpu.make_async_remote_copy(
        src_ref=src_ref, dst_ref=land[w].at[got_at if receiving else sent_to], send_sem=send_sems.at[sem],
        recv_sem=recv_sems.at[sem], device_id=to, device_id_type=MESH)


def _exchange_start(mode, srcs, lands, after, *, name):
    ns, nl = len(srcs), len(lands)
    n_sem = nl * len(_EXCHANGE_BITS[mode])

    def body(*refs):
        src, land = refs[:ns], refs[ns:ns + nl]
        send_sems, recv_sems = refs[ns + nl + 1:ns + nl + 3]
        token = refs[-1]
        place = _place()
        for w in range(nl):
            for i in range(len(_EXCHANGE_BITS[mode])):
                _exchange_copy(mode, src, land, w, i, place, send_sems, recv_sems, receiving=False).start()
        token[...] = jnp.zeros_like(token)

    sems = pltpu.SemaphoreType.DMA((n_sem,))
    arrays = list(srcs) + list(lands)
    res = pl.pallas_call(
        body, name=name, in_specs=[_HBM] * (ns + nl) + [_ANY],
        out_specs=(_SEM, _SEM, *([_HBM] * (ns + nl)), pl.BlockSpec(memory_space=pltpu.VMEM)),
        out_shape=(sems, sems, *[pltpu.HBM(a.shape, a.dtype) for a in arrays], jax.ShapeDtypeStruct((8, LANE), F32)),
        input_output_aliases={i: 2 + i for i in range(ns + nl)},
        compiler_params=pltpu.CompilerParams(has_side_effects=_EFFECT),
    )(*[pltpu.with_memory_space_constraint(a, pltpu.HBM) for a in arrays], after)
    return res[0], res[1], list(res[2:2 + ns]), list(res[2 + ns:2 + ns + nl]), res[-1]


def _exchange_wait(mode, started, after, *, name):
    send_sems, recv_sems, srcs, lands, _ = started
    ns, nl = len(srcs), len(lands)

    def body(*refs):
        src, land = refs[:ns], refs[ns:ns + nl]
        send_sems, recv_sems = refs[ns + nl:ns + nl + 2]
        place = _place()
        for w in range(nl):
            for i in range(len(_EXCHANGE_BITS[mode])):
                cp = _exchange_copy(mode, src, land, w, i, place, send_sems, recv_sems, receiving=True)
                cp.wait_send()
                cp.wait_recv()

    arrays = list(srcs) + list(lands)
    res = pl.pallas_call(
        body, name=name, in_specs=[_HBM] * (ns + nl) + [_SEM, _SEM, _ANY], out_specs=[_HBM] * (ns + nl),
        out_shape=[pltpu.HBM(a.shape, a.dtype) for a in arrays],
        input_output_aliases={i: i for i in range(ns + nl)},
        compiler_params=pltpu.CompilerParams(has_side_effects=_EFFECT),
    )(*arrays, send_sems, recv_sems, after)
    return list(res[:ns]), list(res[ns:])


def _small_allreduce_adamw(gvec, wvec, mvec, vvec):
    rows, length = gvec.shape

    def body(g_ref, w_ref, m_ref, v_ref, gs_ref, d_ref, nm_ref, nv_ref, slots, send_sems, recv_sems):
        x, y, c = _place()
        me = _slot(x, y, c)
        slots[me] = g_ref[...]
        sends = []
        for k in range(1, N_DEV):
            peer = _peer(k, x, y, c)
            sends.append(pltpu.make_async_remote_copy(
                src_ref=g_ref, dst_ref=slots.at[me], send_sem=send_sems.at[k - 1], recv_sem=recv_sems.at[k - 1],
                device_id=peer, device_id_type=MESH))
        for cp in sends:
            cp.start()
        for k in range(1, N_DEV):
            peer = _peer(k, x, y, c)
            pltpu.make_async_remote_copy(
                src_ref=g_ref, dst_ref=slots.at[_slot(*peer)], send_sem=send_sems.at[k - 1], recv_sem=recv_sems.at[k - 1],
                device_id=peer, device_id_type=MESH).wait_recv()
        for cp in sends:
            cp.wait_send()
        g = slots[0]
        for s in range(1, N_DEV):
            g = g + slots[s]
        delta, nm, nv = _adamw_math(w_ref[...], g, m_ref[...], v_ref[...])
        gs_ref[...] = g
        d_ref[...] = delta
        nm_ref[...] = nm
        nv_ref[...] = nv

    vmem = pl.BlockSpec(memory_space=pltpu.VMEM)
    out = jax.ShapeDtypeStruct((rows, length), F32)
    return pl.pallas_call(
        body, in_specs=[vmem] * 4, out_specs=[vmem] * 4, out_shape=[out] * 4,
        scratch_shapes=[pltpu.VMEM((N_DEV, rows, length), F32), pltpu.SemaphoreType.DMA((N_DEV - 1,)),
                        pltpu.SemaphoreType.DMA((N_DEV - 1,))],
        name="small_allreduce_adamw",
    )(gvec, wvec, mvec, vvec)


_SMALL_SEGMENTS = (("a_log", GDN_HEADS), ("dt_bias", GDN_HEADS), ("gdn_norm_w", HEAD_DIM), ("pool_scale", GDN_WIDTH),
                   ("ln1_g", D_MODEL), ("ln1_b", D_MODEL), ("ln2_g", D_MODEL), ("ln2_b", D_MODEL),
                   ("ln3_g", D_MODEL), ("ln3_b", D_MODEL), ("conv_w", CONV_K * QKV_COLS))
_SMALL_ROWS = 8
_SMALL_LEN = -(-sum(sz for _, sz in _SMALL_SEGMENTS) // (_SMALL_ROWS * LANE)) * LANE


def _pack_small(vals):
    parts = [vals[n].reshape(-1).astype(F32) if n in vals else jnp.zeros((sz,), F32) for n, sz in _SMALL_SEGMENTS]
    flat = jnp.concatenate(parts)
    flat = jnp.pad(flat, (0, _SMALL_ROWS * _SMALL_LEN - flat.shape[0]))
    return flat.reshape(_SMALL_ROWS, _SMALL_LEN)


def _unpack_small(vec):
    flat = vec.reshape(-1)
    out, off = {}, 0
    for n, sz in _SMALL_SEGMENTS:
        out[n] = flat[off:off + sz]
        off += sz
    return out


_WEIGHT_ORDER = ("w_in", "conv_w", "a_log", "dt_bias", "gdn_norm_w", "pool_w", "pool_scale", "w_out", "ln1_g", "ln1_b",
                 "xq_w", "xk_w", "xv_w", "xo_w", "ln2_g", "ln2_b", "w_up", "w_down", "ln3_g", "ln3_b")


def _shard2d(name, a):
    if name == "w_in":
        return a.T
    return a.reshape(-1, a.shape[-1]) if name == "pool_w" else a


def _update_view(name, a):
    return jnp.transpose(a, (2, 0, 1)) if name == "w_in" else _shard2d(name, a[0])


def _shard_result(name, r, shape):
    return jnp.transpose(r, (1, 2, 0)) if name == "w_in" else r.reshape(shape)


def _gathered_to_full(name, gth):
    if name in ("w_up", "w_in"):
        return gth
    if name == "conv_w":
        return jnp.transpose(gth, (1, 0, 2)).reshape(gth.shape[1], N_DEV * gth.shape[2])
    if name == "pool_w":
        g4 = gth.reshape(N_DEV, POOL_GROUPS, POOL_GROUP_DIM // N_DEV, POOL_GROUP_DIM)
        return jnp.transpose(g4, (1, 0, 2, 3)).reshape(POOL_GROUPS, POOL_GROUP_DIM, POOL_GROUP_DIM)
    return gth.reshape(N_DEV * gth.shape[1], gth.shape[2])


def _full_to_chunks(name, full):
    if name == "w_up":
        return full
    if name == "pool_w":
        g4 = full.reshape(POOL_GROUPS, N_DEV, POOL_GROUP_DIM // N_DEV, POOL_GROUP_DIM)
        return jnp.transpose(g4, (1, 0, 2, 3)).reshape(N_DEV, POOL_GROUPS * POOL_GROUP_DIM // N_DEV, POOL_GROUP_DIM)
    return full.reshape(N_DEV, full.shape[0] // N_DEV, full.shape[1])


_GATHER_GROUPS = (("mixer", ("w_in", "conv_w", "pool_w")), ("attn", ("w_out", "xq_w", "xk_w", "xv_w", "xo_w")),
                  ("up", ("w_up",)), ("down", ("w_down",)))


def _grad_chunks(name, g):
    if name == "w_in":
        return _w_in_chunks(g.astype(BF16))
    return _full_to_chunks(name, g.astype(BF16))


def kernel(x, mem, w_in, conv_w, a_log, dt_bias, gdn_norm_w, pool_w, pool_scale, w_out, ln1_g, ln1_b, xq_w, xk_w, xv_w, xo_w, ln2_g, ln2_b, w_up, w_down, ln3_g, ln3_b, loss_target, m_w_in, m_conv_w, m_a_log, m_dt_bias, m_gdn_norm_w, m_pool_w, m_pool_scale, m_w_out, m_ln1_g, m_ln1_b, m_xq_w, m_xk_w, m_xv_w, m_xo_w, m_ln2_g, m_ln2_b, m_w_up, m_w_down, m_ln3_g, m_ln3_b, v_w_in, v_conv_w, v_a_log, v_dt_bias, v_gdn_norm_w, v_pool_w, v_pool_scale, v_w_out, v_ln1_g, v_ln1_b, v_xq_w, v_xk_w, v_xv_w, v_xo_w, v_ln2_g, v_ln2_b, v_w_up, v_w_down, v_ln3_g, v_ln3_b):
    args = dict(locals())
    wt = {n: args[n][0] for n in _WEIGHT_ORDER}
    mo = {n: args["m_" + n][0] for n in _WEIGHT_ORDER}
    vo = {n: args["v_" + n][0] for n in _WEIGHT_ORDER}

    me = _slot(*_place())
    me_arr = jnp.reshape(me, (1,)).astype(jnp.int32)
    nothing = jnp.zeros((8, LANE), F32)

    def landing_zones(names):
        shards = [_shard2d(n, wt[n]).astype(F32 if n == "conv_w" else BF16) for n in names]
        zones = [lax.dynamic_update_slice(lax.empty((N_DEV, *s.shape), s.dtype), s[None], (me, 0, 0)) for s in shards]
        return shards, zones

    chip_arr = jnp.reshape(me // 2, (1,)).astype(jnp.int32)
    core_arr = jnp.reshape(lax.axis_index("c"), (1,)).astype(jnp.int32)
    names_of = dict(_GATHER_GROUPS)
    gathers = {}
    prepared = {}

    def gather_chips(group, after):
        shards, zones = prepared.pop(group) if group in prepared else landing_zones(names_of[group])
        gathers[group] = _exchange_start("gather_chips", shards, zones, after, name="gather_chips_" + group)
        return gathers[group][4]

    def gather_pass(group, after):
        _, zones = _exchange_wait("gather_chips", gathers[group], after, name=f"gather_chips_{group}_wait")
        gathers[group] = _exchange_start("gather_pass", [], zones, nothing, name="gather_pass_" + group)
        return gathers[group][4]

    def gathered(group, after, token=None):
        _, zones = _exchange_wait("gather_pass", gathers[group], after, name=f"gather_pass_{group}_wait")
        full = {n: _gathered_to_full(n, z) for n, z in zip(names_of[group], zones)}
        full.update({n: wt[n] if token is None else wt[n] + token for n in _VECTORS})
        return _group_weights(group, full)

    token = gather_chips("mixer", nothing)
    x16 = _cast_bf16(x[0], name="cast_x")
    later = {group: landing_zones(names_of[group]) for group in ("attn", "up", "down")}
    token, x16, later = lax.optimization_barrier((token, x16, later))
    prepared.update(later)
    token = gather_chips("attn", gather_pass("mixer", token))

    def weights_of(group, after):
        if group == "mixer":
            return gathered(group, gathers["attn"][4])
        if group == "ahead":
            return gather_chips("up", gather_pass("attn", after))[0:1, 0:1]
        if group == "attn":
            weights = gathered(group, after)
            token = gather_chips("down", gather_pass("up", weights["w_out"]))[0, 0]
            return {n: (v + token if n == "ln1_g" else v) for n, v in weights.items()}
        if group == "up":
            weights = gathered(group, after)
            gather_pass("down", weights["w_up"])
            return weights
        return gathered(group, after)

    scatters = {}
    in_flight = []

    def chip_stage(after):
        group, names, started = in_flight.pop()
        chunks, from_sibling = _exchange_wait("scatter_sibling", started, after, name=f"scatter_sibling_{group}_wait")
        sums = [_chip_sums(c, f, core_arr, name=f"chip_sums_{n}") for n, c, f in zip(names, chunks, from_sibling)]
        scatters[group] = (names, _exchange_start("scatter_chips", sums, [lax.empty(s.shape, s.dtype) for s in sums],
                                                  nothing, name="scatter_chips_" + group))
        return scatters[group][1][4]

    def grads_ready(group, grads):
        if group == "tick":
            return chip_stage(grads["after"])[0:1, 0:1] if in_flight else None
        names = tuple(grads)
        chunks = [_grad_chunks(n, grads[n]) for n in names]
        token = chip_stage(chunks[0]) if in_flight else nothing
        zones = [lax.empty((N_CHIPS, *c.shape[1:]), c.dtype) for c in chunks]
        started = _exchange_start("scatter_sibling", chunks, zones, token, name="scatter_sibling_" + group)
        in_flight.append((group, names, started))
        return started[4][0:1, 0:1]

    sq, grad_x, g = _local_step(x[0], x16, mem[0], loss_target[0], weights_of, grads_ready)
    small = _finish_small_grads(g)

    out = {}
    after = chip_stage(grad_x)
    for group, (names, started) in scatters.items():
        sums, lands = _exchange_wait("scatter_chips", started, after, name=f"scatter_chips_{group}_wait")
        for n, parts, own in zip(names, lands, sums):
            res = _adamw_shard(parts, own, chip_arr, _update_view(n, args[n]), _update_view(n, args["m_" + n]),
                               _update_view(n, args["v_" + n]), name="adamw_" + n)
            out[n] = [_shard_result(n, r, args[n].shape) for r in res]
            after = res[1]

    packed, _ = lax.optimization_barrier((_pack_small(small), after))
    gs, ds, ms, vs = _small_allreduce_adamw(
        packed, _pack_small({n: wt[n] for n in _VECTORS}), _pack_small({n: mo[n] for n in _VECTORS}),
        _pack_small({n: vo[n] for n in _VECTORS}))
    gs, ds, ms, vs = _unpack_small(gs), _unpack_small(ds), _unpack_small(ms), _unpack_small(vs)
    cols = conv_w.shape[-1]
    conv_full = gs["conv_w"].reshape(CONV_K, QKV_COLS)
    conv_mine = lax.dynamic_slice(conv_full, (0, me * cols), (CONV_K, cols))[None]
    res = _adamw_shard(conv_mine, conv_mine, jnp.zeros((1,), jnp.int32), wt["conv_w"], mo["conv_w"], vo["conv_w"],
                       name="adamw_conv_w")
    out["conv_w"] = [r.reshape(conv_w.shape) for r in res]
    for n in _VECTORS:
        out[n] = [t[n].reshape(args[n].shape) for t in (gs, ds, ms, vs)]

    loss = lax.psum(0.5 * sq[0, 0] / D_MODEL, ("x", "y", "c"))
    return (loss, grad_x[None], *[out[n][0] for n in _WEIGHT_ORDER], *[out[n][1] for n in _WEIGHT_ORDER],
            *[out[n][2] for n in _WEIGHT_ORDER], *[out[n][3] for n in _WEIGHT_ORDER])
```

```python
import functools
import math

import jax
import jax.numpy as jnp
from jax import lax
from jax.experimental import pallas as pl
from jax.experimental.pallas import tpu as pltpu

F32 = jnp.float32
BF16 = jnp.bfloat16
MESH = pl.DeviceIdType.MESH

N_DEV = 8
D_MODEL = 2048
GDN_WIDTH = 1024
GDN_HEADS = 8
HEAD_DIM = 128
CONV_K = 4
CHUNK = 64
POOL_GROUPS = 4
POOL_GROUP_DIM = 256
MEM_LEN = 256
XATTN_HEADS = 4
XATTN_HEAD_DIM = 512
D_FF = 8192
IN_COLS = 5136
ALPHA = 2.0 ** 0.25
LN_EPS = 1e-5
NORM_EPS = 1e-6

LANE = 128
QKV_COLS = 3 * GDN_WIDTH
Z_OFF = QKV_COLS
BA_OFF = 4 * GDN_WIDTH
POOL_OFF = BA_OFF + 2 * LANE
PROJ_COLS = POOL_OFF + GDN_WIDTH
Z_BLK = Z_OFF // LANE
BA_BLK = BA_OFF // LANE
POOL_BLK = POOL_OFF // POOL_GROUP_DIM

ADAM_LR = 0.001
ADAM_B1 = 0.9
ADAM_B2 = 0.999
ADAM_EPS = 1e-08
ADAM_WD = 0.01
ADAM_STEP = 10

VMEM_LIMIT_BYTES = 48 * 1024 * 1024


def _params(*sem):
    return pltpu.CompilerParams(dimension_semantics=sem if sem else None, vmem_limit_bytes=VMEM_LIMIT_BYTES)


def _make_dots(cast, precision, batched=False):
    lead = 1 if batched else 0
    batch = ((0,), (0,)) if batched else ((), ())

    def dg(a, b, ca, cb):
        if cast is not None:
            a = a.astype(cast)
            b = b.astype(cast)
        return lax.dot_general(a, b, (((ca + lead,), (cb + lead,)), batch), precision=precision, preferred_element_type=F32)

    def nn_(a, b):
        return dg(a, b, 1, 0)

    def nt_(a, b):
        return dg(a, b, 1, 1)

    def tn_(a, b):
        return dg(a, b, 0, 0)

    @jax.custom_vjp
    def nn(a, b):
        return nn_(a, b)

    nn.defvjp(lambda a, b: (nn_(a, b), (a, b)), lambda r, g: (nt_(g, r[1]), tn_(r[0], g)))

    @jax.custom_vjp
    def nt(a, b):
        return nt_(a, b)

    nt.defvjp(lambda a, b: (nt_(a, b), (a, b)), lambda r, g: (nn_(g, r[1]), tn_(g, r[0])))

    @jax.custom_vjp
    def tn(a, b):
        return tn_(a, b)

    tn.defvjp(lambda a, b: (tn_(a, b), (a, b)), lambda r, g: (nt_(r[1], g), nn_(r[0], g)))

    return (nn_, nt_, tn_), (nn, nt, tn)


_BDOT_PLAIN, _BDOT_VJP = _make_dots(BF16, None)
_BDOT_BATCH_PLAIN, _BDOT_BATCH_VJP = _make_dots(BF16, None, batched=True)
_FDOT_BATCH_PLAIN, _FDOT_BATCH_VJP = _make_dots(BF16, None, batched=True)


def _mm(a, b, *, ta=False, tb=False, out_dtype=F32, tm=None, tn=512, tk=None, epi=None, extra=None, add_scale=1.0,
        b_chunks=False, o_chunks=False, name):
    m, k = (a.shape[1], a.shape[0]) if ta else a.shape
    if b_chunks:
        n, kb = (b.shape[1], N_DEV * b.shape[2]) if tb else (N_DEV * b.shape[2], b.shape[1])
    else:
        n, kb = b.shape if tb else (b.shape[1], b.shape[0])
    assert kb == k, (name, a.shape, b.shape)
    tm, tn, tk = min(tm or m, m), min(tn, n), min(tk or k, k)
    assert m % tm == 0 and n % tn == 0 and k % tk == 0, (name, m, n, k)
    nk = k // tk
    dims = (((0 if ta else 1,), (1 if tb else 0,)), ((), ()))
    n_extra = 0 if epi in (None, "relu2") else 1
    n_out = 2 if epi == "relu2" else 1
    if epi in ("relu2", "mul2r"):
        out_dtype = BF16

    def body(*refs):
        a_ref, b_ref = refs[:2]
        c_ref = refs[2] if n_extra else None
        o_refs = refs[2 + n_extra:2 + n_extra + n_out]
        scr = refs[2 + n_extra + n_out:]
        r = lax.dot_general(a_ref[...].astype(BF16), b_ref[...].astype(BF16), dims, preferred_element_type=F32)

        def finish(v):
            if epi == "add":
                o_refs[0][...] = (v + add_scale * c_ref[...]).astype(out_dtype)
            elif epi == "relu2":
                p = jnp.maximum(v, 0.0)
                o_refs[0][...] = (p * p).astype(BF16)
                o_refs[1][...] = p.astype(BF16)
            elif epi == "mul2r":
                o_refs[0][...] = (v * (2.0 * c_ref[...].astype(F32))).astype(BF16)
            else:
                o_refs[0][...] = v.astype(out_dtype)

        if nk == 1:
            finish(r)
        else:
            acc = scr[0]
            kk = pl.program_id(2)

            @pl.when(kk == 0)
            def _():
                acc[...] = r

            @pl.when(kk > 0)
            def _():
                acc[...] += r

            @pl.when(kk == nk - 1)
            def _():
                finish(acc[...])

    a_spec = pl.BlockSpec((tk, tm), lambda i, j, kk: (kk, i)) if ta else pl.BlockSpec((tm, tk), lambda i, j, kk: (i, kk))
    if b_chunks and tb:
        kc = k // N_DEV // tk
        b_spec = pl.BlockSpec((None, tn, tk), lambda i, j, kk: (kk // kc, j, kk % kc))
    elif b_chunks:
        nc = n // N_DEV // tn
        b_spec = pl.BlockSpec((None, tk, tn), lambda i, j, kk: (j // nc, kk, j % nc))
    elif tb:
        b_spec = pl.BlockSpec((tn, tk), lambda i, j, kk: (j, kk))
    else:
        b_spec = pl.BlockSpec((tk, tn), lambda i, j, kk: (kk, j))
    mn_spec = pl.BlockSpec((tm, tn), lambda i, j, kk: (i, j))
    if o_chunks:
        oc = n // N_DEV // tn
        o_spec = pl.BlockSpec((None, tm, tn), lambda i, j, kk: (j // oc, i, j % oc))
        o_shape = jax.ShapeDtypeStruct((N_DEV, m, n // N_DEV), out_dtype)
    else:
        o_spec, o_shape = mn_spec, jax.ShapeDtypeStruct((m, n), out_dtype)
    res = pl.pallas_call(
        body, grid=(m // tm, n // tn, nk), in_specs=[a_spec, b_spec] + [mn_spec] * n_extra,
        out_specs=[o_spec] * n_out, out_shape=[o_shape] * n_out,
        scratch_shapes=[pltpu.VMEM((tm, tn), F32)] if nk > 1 else [],
        compiler_params=_params("parallel", "parallel", "arbitrary"), name=name,
    )(a, b, *([extra] if n_extra else []))
    return res if n_out > 1 else res[0]


def _cast_bf16(v, *, name, tm=512):
    t, d = v.shape
    tm = min(tm, t)

    def body(v_ref, o_ref):
        o_ref[...] = v_ref[...].astype(BF16)

    spec = pl.BlockSpec((tm, d), lambda i: (i, 0))
    return pl.pallas_call(body, grid=(t // tm,), in_specs=[spec], out_specs=spec,
                          out_shape=jax.ShapeDtypeStruct((t, d), BF16), compiler_params=_params("parallel"), name=name)(v)


def _shift_down(v, s):
    if s == 0:
        return v
    row = lax.broadcasted_iota(jnp.int32, v.shape, 0)
    return jnp.where(row >= s, pltpu.roll(v, s, axis=0), 0.0)


def _shift_up(v, s):
    if s == 0:
        return v
    t = v.shape[0]
    row = lax.broadcasted_iota(jnp.int32, v.shape, 0)
    return jnp.where(row < t - s, pltpu.roll(v, t - s, axis=0), 0.0)


def _post_col(j):
    return (j % GDN_HEADS) * 3 + j // GDN_HEADS


def _gdn_prep_fwd(proj, conv_w):
    t = proj.shape[0]

    def body(x_ref, w_ref, o_ref):
        j = pl.program_id(0)
        x = x_ref[...]
        y = jnp.zeros_like(x)
        for tap in range(CONV_K):
            y = y + w_ref[tap:tap + 1, :] * _shift_down(x, CONV_K - 1 - tap)
        c = y * jax.nn.sigmoid(y)
        nrm = c * lax.rsqrt(jnp.sum(c * c, axis=1, keepdims=True) + NORM_EPS)
        o_ref[...] = jnp.where(j < 2 * GDN_HEADS, nrm, c)

    return pl.pallas_call(
        body, grid=(QKV_COLS // LANE,),
        in_specs=[pl.BlockSpec((t, LANE), lambda j: (0, j)), pl.BlockSpec((CONV_K, LANE), lambda j: (0, j))],
        out_specs=pl.BlockSpec((t, LANE), lambda j: (0, _post_col(j))),
        out_shape=jax.ShapeDtypeStruct((t, QKV_COLS), F32),
        compiler_params=_params("parallel"), name="gdn_prep_fwd",
    )(proj, conv_w)


def _gdn_prep_bwd(proj, conv_w, dpost, dproj):
    t = proj.shape[0]

    def body(x_ref, w_ref, d_ref, _, dx_ref, dw_ref):
        j = pl.program_id(0)
        x = x_ref[...]
        xs = [_shift_down(x, CONV_K - 1 - tap) for tap in range(CONV_K)]
        y = jnp.zeros_like(x)
        for tap in range(CONV_K):
            y = y + w_ref[tap:tap + 1, :] * xs[tap]
        sig = jax.nn.sigmoid(y)
        c = y * sig
        r = lax.rsqrt(jnp.sum(c * c, axis=1, keepdims=True) + NORM_EPS)
        nrm = c * r
        d = d_ref[...]
        dc_norm = r * (d - nrm * jnp.sum(d * nrm, axis=1, keepdims=True))
        dc = jnp.where(j < 2 * GDN_HEADS, dc_norm, d)
        dy = dc * (sig * (1.0 + y * (1.0 - sig)))
        dx = jnp.zeros_like(x)
        for tap in range(CONV_K):
            dx = dx + _shift_up(w_ref[tap:tap + 1, :] * dy, CONV_K - 1 - tap)
            dw_ref[tap:tap + 1, :] = jnp.sum(dy * xs[tap], axis=0, keepdims=True)
        dx_ref[...] = dx.astype(dx_ref.dtype)

    return pl.pallas_call(
        body, grid=(QKV_COLS // LANE,),
        in_specs=[pl.BlockSpec((t, LANE), lambda j: (0, j)), pl.BlockSpec((CONV_K, LANE), lambda j: (0, j)),
                  pl.BlockSpec((t, LANE), lambda j: (0, _post_col(j))), pl.BlockSpec(memory_space=pl.ANY)],
        out_specs=[pl.BlockSpec((t, LANE), lambda j: (0, j)), pl.BlockSpec((CONV_K, LANE), lambda j: (0, j))],
        out_shape=[jax.ShapeDtypeStruct(dproj.shape, dproj.dtype), jax.ShapeDtypeStruct((CONV_K, QKV_COLS), F32)],
        input_output_aliases={3: 0},
        compiler_params=_params("parallel"), name="gdn_prep_bwd",
    )(proj, conv_w, dpost, dproj)


def _softplus(v):
    return jnp.maximum(v, 0.0) + jnp.log(1.0 + jnp.exp(-jnp.abs(v)))


def _tri_inv(low, nn):
    r = lax.broadcasted_iota(jnp.int32, (CHUNK, CHUNK), 0)
    c = lax.broadcasted_iota(jnp.int32, (CHUNK, CHUNK), 1)
    eye = (r == c).astype(F32)
    same_blk = lax.shift_right_logical(r, 4) == lax.shift_right_logical(c, 4)
    diag = jnp.where(same_blk, low, 0.0)
    off = low - diag
    n1 = -diag
    n2 = nn(n1, n1)
    n4 = nn(n2, n2)
    n8 = nn(n4, n4)
    inv_d = nn(nn(nn(eye + n1, eye + n2), eye + n4), eye + n8)
    m1 = nn(inv_d, off)
    m2 = nn(m1, m1)
    return nn(nn(eye - m1, eye + m2), inv_d)


@jax.custom_vjp
def _tri_inv_known(low, t_inv):
    return t_inv


def _tri_inv_known_fwd(low, t_inv):
    return t_inv, t_inv


def _tri_inv_known_bwd(t_inv, g):
    _, nt, tn = _FDOT_BATCH_PLAIN
    return -nt(tn(t_inv, g), t_inv), jnp.zeros_like(t_inv)


_tri_inv_known.defvjp(_tri_inv_known_fwd, _tri_inv_known_bwd)


LOCAL_HEADS_PER_STEP = 8


def _gdn_local_fn(qkv, ba, alog_row, dtb_row, first_head, bdots, fdots, t_known=None):
    nn, nt, tn = bdots
    fnn = fdots[0]
    n_heads = qkv.shape[1] // (3 * HEAD_DIM)
    part = lambda i, p: qkv[:, (3 * i + p) * HEAD_DIM:(3 * i + p + 1) * HEAD_DIM]
    q = jnp.stack([part(i, 0) for i in range(n_heads)]) * (HEAD_DIM ** -0.5)
    k = jnp.stack([part(i, 1) for i in range(n_heads)])
    v = jnp.stack([part(i, 2) for i in range(n_heads)])
    lane = lax.broadcasted_iota(jnp.int32, ba.shape, 1)
    bg = jnp.where(lane < GDN_HEADS, jax.nn.sigmoid(ba), -jnp.exp(alog_row) * _softplus(ba + dtb_row))
    pick = lambda l: jnp.sum(jnp.where(lane == l, bg, 0.0), axis=1, keepdims=True)
    beta = jnp.stack([pick(first_head + i) for i in range(n_heads)])
    g = jnp.stack([pick(first_head + i + GDN_HEADS) for i in range(n_heads)])

    r = lax.broadcasted_iota(jnp.int32, (CHUNK, CHUNK), 0)
    c = lax.broadcasted_iota(jnp.int32, (CHUNK, CHUNK), 1)
    incl = r >= c
    strict = r > c
    eye = r == c

    def to_row(col):
        return jnp.sum(jnp.where(eye, col, 0.0), axis=1, keepdims=True)

    gc = jnp.sum(jnp.where(incl, to_row(g), 0.0), axis=2, keepdims=True)
    diff = gc - to_row(gc)
    decay = jnp.where(incl, jnp.exp(jnp.where(incl, diff, 0.0)), 0.0)
    k_beta = k * beta
    v_beta = v * beta
    low = jnp.where(strict, nt(k_beta, k) * decay, 0.0)
    t_inv = _tri_inv(low, fnn) if t_known is None else _tri_inv_known(low, t_known)
    eg = jnp.exp(gc)
    u = fnn(t_inv, v_beta)
    w = fnn(t_inv, k_beta * eg)
    attn = jnp.where(incl, nt(q, k) * decay, 0.0)
    last = lax.broadcasted_iota(jnp.int32, (CHUNK, 1), 0) == CHUNK - 1
    g_last = jnp.sum(jnp.where(last, gc, 0.0), axis=1, keepdims=True)
    kdec = k * jnp.exp(g_last - gc)
    elast = jnp.broadcast_to(jnp.exp(g_last), (n_heads, 1, LANE))
    return u, w, q * eg, kdec, attn, elast, t_inv


def _gdn_state_fn(u, w, qg, kdec, attn, elast, state, bdots):
    nn, _, tn = bdots
    v_new = u - nn(w, state)
    o = nn(qg, state) + nn(attn, v_new)
    return o, state * elast + tn(kdec, v_new)


def _gdn_local_fwd(post, proj, alog_row, dtb_row):
    t = post.shape[0]
    n_chunks = t // CHUNK
    hb = LOCAL_HEADS_PER_STEP

    def body(qkv_ref, ba_ref, al_ref, dt_ref, u_ref, w_ref, qg_ref, kd_ref, at_ref, el_ref, ti_ref):
        u, w, qg, kdec, attn, elast, t_inv = _gdn_local_fn(qkv_ref[...], ba_ref[...], al_ref[...], dt_ref[...],
                                                           pl.program_id(1) * hb, _BDOT_BATCH_PLAIN, _FDOT_BATCH_PLAIN)
        for i in range(hb):
            cols = slice(i * HEAD_DIM, (i + 1) * HEAD_DIM)
            u_ref[:, cols] = u[i]
            w_ref[:, cols] = w[i].astype(BF16)
            qg_ref[:, cols] = qg[i].astype(BF16)
            kd_ref[:, cols] = kdec[i].astype(BF16)
        at_ref[...] = attn.astype(BF16)
        el_ref[:, 0] = elast
        ti_ref[...] = t_inv

    wide = pl.BlockSpec((CHUNK, hb * HEAD_DIM), lambda n, j: (n, j))
    square = pl.BlockSpec((hb, CHUNK, CHUNK), lambda n, j: (j, n, 0))
    row = pl.BlockSpec((1, LANE), lambda n, j: (0, 0))
    res = pl.pallas_call(
        body, grid=(n_chunks, GDN_HEADS // hb),
        in_specs=[pl.BlockSpec((CHUNK, hb * 3 * HEAD_DIM), lambda n, j: (n, j)),
                  pl.BlockSpec((CHUNK, LANE), lambda n, j: (n, BA_BLK)), row, row],
        out_specs=[wide, wide, wide, wide, square, pl.BlockSpec((hb, 1, 1, LANE), lambda n, j: (j, n, 0, 0)), square],
        out_shape=[jax.ShapeDtypeStruct((t, GDN_WIDTH), F32), jax.ShapeDtypeStruct((t, GDN_WIDTH), BF16),
                   jax.ShapeDtypeStruct((t, GDN_WIDTH), BF16), jax.ShapeDtypeStruct((t, GDN_WIDTH), BF16),
                   jax.ShapeDtypeStruct((GDN_HEADS, t, CHUNK), BF16),
                   jax.ShapeDtypeStruct((GDN_HEADS, n_chunks, 1, LANE), F32),
                   jax.ShapeDtypeStruct((GDN_HEADS, t, CHUNK), F32)],
        compiler_params=_params("parallel", "parallel"), name="gdn_local_fwd",
    )(post, proj, alog_row, dtb_row)
    return tuple(res[:6]), res[6]


def _by_head(ref):
    return jnp.stack([ref[:, h * HEAD_DIM:(h + 1) * HEAD_DIM] for h in range(ref.shape[1] // HEAD_DIM)])


def _gdn_state_specs(n_of):
    wide = pl.BlockSpec((CHUNK, GDN_WIDTH), lambda n: (n_of(n), 0))
    attn = pl.BlockSpec((GDN_HEADS, CHUNK, CHUNK), lambda n: (0, n_of(n), 0))
    elast = pl.BlockSpec((GDN_HEADS, 1, 1, LANE), lambda n: (0, n_of(n), 0, 0))
    saved = pl.BlockSpec((GDN_HEADS, 1, HEAD_DIM, HEAD_DIM), lambda n: (0, n_of(n), 0, 0))
    return wide, attn, elast, saved


def _gdn_state_fwd(u, w, qg, kdec, attn, elast):
    t = u.shape[0]
    n_chunks = t // CHUNK

    def body(u_ref, w_ref, qg_ref, kd_ref, at_ref, el_ref, o_ref, save_ref, state_ref):
        @pl.when(pl.program_id(0) == 0)
        def _():
            state_ref[...] = jnp.zeros_like(state_ref)

        state = state_ref[...]
        save_ref[:, 0] = state
        o, new_state = _gdn_state_fn(_by_head(u_ref), _by_head(w_ref), _by_head(qg_ref), _by_head(kd_ref), at_ref[...],
                                     el_ref[:, 0], state, _BDOT_BATCH_PLAIN)
        for h in range(GDN_HEADS):
            o_ref[:, h * HEAD_DIM:(h + 1) * HEAD_DIM] = o[h]
        state_ref[...] = new_state

    wide, attn_spec, elast_spec, saved_spec = _gdn_state_specs(lambda n: n)
    return pl.pallas_call(
        body, grid=(n_chunks,), in_specs=[wide, wide, wide, wide, attn_spec, elast_spec],
        out_specs=[wide, saved_spec],
        out_shape=[jax.ShapeDtypeStruct((t, GDN_WIDTH), F32),
                   jax.ShapeDtypeStruct((GDN_HEADS, n_chunks, HEAD_DIM, HEAD_DIM), F32)],
        scratch_shapes=[pltpu.VMEM((GDN_HEADS, HEAD_DIM, HEAD_DIM), F32)],
        compiler_params=_params("arbitrary"), name="gdn_state_fwd",
    )(u, w, qg, kdec, attn, elast)


def _gdn_state_bwd(u, w, qg, kdec, attn, elast, saved, do):
    t = u.shape[0]
    n_chunks = t // CHUNK
    last = n_chunks - 1

    def body(u_ref, w_ref, qg_ref, kd_ref, at_ref, el_ref, save_ref, do_ref,
             du_ref, dw_ref, dqg_ref, dkd_ref, dat_ref, del_ref, dstate_ref):
        @pl.when(pl.program_id(0) == 0)
        def _():
            dstate_ref[...] = jnp.zeros_like(dstate_ref)

        _, vjp = jax.vjp(
            lambda *a: _gdn_state_fn(*a, _BDOT_BATCH_VJP), _by_head(u_ref), _by_head(w_ref).astype(F32),
            _by_head(qg_ref).astype(F32), _by_head(kd_ref).astype(F32), at_ref[...].astype(F32), el_ref[:, 0],
            save_ref[:, 0])
        du, dw, dqg, dkd, dat, de, dstate = vjp((_by_head(do_ref), dstate_ref[...]))
        for h in range(GDN_HEADS):
            cols = slice(h * HEAD_DIM, (h + 1) * HEAD_DIM)
            du_ref[:, cols] = du[h]
            dw_ref[:, cols] = dw[h]
            dqg_ref[:, cols] = dqg[h]
            dkd_ref[:, cols] = dkd[h]
        dat_ref[...] = dat
        del_ref[:, 0] = de
        dstate_ref[...] = dstate

    wide, attn_spec, elast_spec, saved_spec = _gdn_state_specs(lambda n: last - n)
    wide_f32 = jax.ShapeDtypeStruct((t, GDN_WIDTH), F32)
    return pl.pallas_call(
        body, grid=(n_chunks,), in_specs=[wide, wide, wide, wide, attn_spec, elast_spec, saved_spec, wide],
        out_specs=[wide, wide, wide, wide, attn_spec, elast_spec],
        out_shape=[wide_f32, wide_f32, wide_f32, wide_f32, jax.ShapeDtypeStruct((GDN_HEADS, t, CHUNK), F32),
                   jax.ShapeDtypeStruct((GDN_HEADS, n_chunks, 1, LANE), F32)],
        scratch_shapes=[pltpu.VMEM((GDN_HEADS, HEAD_DIM, HEAD_DIM), F32)],
        compiler_params=_params("arbitrary"), name="gdn_state_bwd",
    )(u, w, qg, kdec, attn, elast, saved, do)


def _gdn_local_bwd(post, proj, alog_row, dtb_row, t_inv, cots, dproj):
    t = post.shape[0]
    n_chunks = t // CHUNK
    hb = LOCAL_HEADS_PER_STEP
    n_steps = GDN_HEADS // hb

    def body(qkv_ref, ba_ref, al_ref, dt_ref, ti_ref, du_ref, dw_ref, dqg_ref, dkd_ref, dat_ref, del_ref, _,
             dqkv_ref, dba_ref, dal_ref, ddt_ref, dba_acc):
        n = pl.program_id(0)
        j = pl.program_id(1)

        @pl.when((n == 0) & (j == 0))
        def _():
            dal_ref[...] = jnp.zeros_like(dal_ref)
            ddt_ref[...] = jnp.zeros_like(ddt_ref)

        @pl.when(j == 0)
        def _():
            dba_acc[...] = jnp.zeros_like(dba_acc)

        t_known = ti_ref[...]
        _, vjp = jax.vjp(
            lambda a, b, c, d: _gdn_local_fn(a, b, c, d, j * hb, _BDOT_BATCH_VJP, _FDOT_BATCH_VJP, t_known)[:6],
            qkv_ref[...], ba_ref[...], al_ref[...], dt_ref[...])
        dqkv, dba, dal, ddt = vjp((_by_head(du_ref), _by_head(dw_ref), _by_head(dqg_ref), _by_head(dkd_ref), dat_ref[...],
                                   del_ref[:, 0]))
        dqkv_ref[...] = dqkv
        dba_acc[...] += dba
        dal_ref[...] += dal
        ddt_ref[...] += ddt

        @pl.when(j == n_steps - 1)
        def _():
            dba_ref[:, 0:LANE] = dba_acc[...].astype(dba_ref.dtype)
            dba_ref[:, LANE:2 * LANE] = jnp.zeros((CHUNK, LANE), dba_ref.dtype)

    wide = pl.BlockSpec((CHUNK, hb * HEAD_DIM), lambda n, j: (n, j))
    qkv_spec = pl.BlockSpec((CHUNK, hb * 3 * HEAD_DIM), lambda n, j: (n, j))
    row = pl.BlockSpec((1, LANE), lambda n, j: (0, 0))
    return pl.pallas_call(
        body, grid=(n_chunks, n_steps),
        in_specs=[qkv_spec, pl.BlockSpec((CHUNK, LANE), lambda n, j: (n, BA_BLK)), row, row,
                  pl.BlockSpec((hb, CHUNK, CHUNK), lambda n, j: (j, n, 0)), wide, wide, wide, wide,
                  pl.BlockSpec((hb, CHUNK, CHUNK), lambda n, j: (j, n, 0)),
                  pl.BlockSpec((hb, 1, 1, LANE), lambda n, j: (j, n, 0, 0)), pl.BlockSpec(memory_space=pl.ANY)],
        out_specs=[qkv_spec, pl.BlockSpec((CHUNK, 2 * LANE), lambda n, j: (n, BA_BLK // 2)), row, row],
        out_shape=[jax.ShapeDtypeStruct((t, QKV_COLS), F32), jax.ShapeDtypeStruct(dproj.shape, dproj.dtype),
                   jax.ShapeDtypeStruct((1, LANE), F32), jax.ShapeDtypeStruct((1, LANE), F32)],
        input_output_aliases={11: 1},
        scratch_shapes=[pltpu.VMEM((CHUNK, LANE), F32)],
        compiler_params=_params("arbitrary", "arbitrary"), name="gdn_local_bwd",
    )(post, proj, alog_row, dtb_row, t_inv, *cots, dproj)


def _onorm_fn(o, z, w):
    return o * lax.rsqrt(jnp.mean(o * o, axis=1, keepdims=True) + NORM_EPS) * w * (z * jax.nn.sigmoid(z))


def _onorm_fwd(o_raw, proj, norm_w, mixin, tm=512):
    t = o_raw.shape[0]
    tm = min(tm, t)

    def body(o_ref, z_ref, w_ref, _, out_ref):
        out_ref[...] = _onorm_fn(o_ref[...], z_ref[...], w_ref[...]).astype(out_ref.dtype)

    return pl.pallas_call(
        body, grid=(t // tm, GDN_HEADS),
        in_specs=[pl.BlockSpec((tm, LANE), lambda i, h: (i, h)), pl.BlockSpec((tm, LANE), lambda i, h: (i, Z_BLK + h)),
                  pl.BlockSpec((1, LANE), lambda i, h: (0, 0)), pl.BlockSpec(memory_space=pl.ANY)],
        out_specs=pl.BlockSpec((tm, LANE), lambda i, h: (i, h)),
        out_shape=jax.ShapeDtypeStruct(mixin.shape, mixin.dtype), input_output_aliases={3: 0},
        compiler_params=_params("parallel", "parallel"), name="gdn_onorm_fwd",
    )(o_raw, proj, norm_w, mixin)


def _onorm_bwd(o_raw, proj, norm_w, dmixin, dproj, tm=512):
    t = o_raw.shape[0]
    tm = min(tm, t)

    def body(o_ref, z_ref, w_ref, d_ref, _, do_ref, dz_ref, dw_ref):
        @pl.when((pl.program_id(0) == 0) & (pl.program_id(1) == 0))
        def _():
            dw_ref[...] = jnp.zeros_like(dw_ref)

        _, vjp = jax.vjp(_onorm_fn, o_ref[...], z_ref[...], w_ref[...])
        do, dz, dw = vjp(d_ref[...])
        do_ref[...] = do
        dz_ref[...] = dz.astype(dz_ref.dtype)
        dw_ref[...] += dw

    return pl.pallas_call(
        body, grid=(t // tm, GDN_HEADS),
        in_specs=[pl.BlockSpec((tm, LANE), lambda i, h: (i, h)), pl.BlockSpec((tm, LANE), lambda i, h: (i, Z_BLK + h)),
                  pl.BlockSpec((1, LANE), lambda i, h: (0, 0)), pl.BlockSpec((tm, LANE), lambda i, h: (i, h)),
                  pl.BlockSpec(memory_space=pl.ANY)],
        out_specs=[pl.BlockSpec((tm, LANE), lambda i, h: (i, h)), pl.BlockSpec((tm, LANE), lambda i, h: (i, Z_BLK + h)),
                   pl.BlockSpec((1, LANE), lambda i, h: (0, 0))],
        out_shape=[jax.ShapeDtypeStruct((t, GDN_WIDTH), F32), jax.ShapeDtypeStruct(dproj.shape, dproj.dtype),
                   jax.ShapeDtypeStruct((1, LANE), F32)],
        input_output_aliases={4: 1},
        compiler_params=_params("arbitrary", "arbitrary"), name="gdn_onorm_bwd",
    )(o_raw, proj, norm_w, dmixin, dproj)


def _pool_select(levels, gi):
    out = levels[-1]
    for lvl in range(len(levels) - 2, -1, -1):
        out = jnp.where(gi == lvl, levels[lvl], out)
    return out


def _pool_count(shape, gi):
    pos = lax.broadcasted_iota(jnp.int32, shape, 0)
    win = lax.shift_left(jnp.int32(2), gi)
    return jnp.minimum(pos + 1, win).astype(F32)


def _pooled(p, gi):
    acc = p
    levels = []
    for lvl in range(POOL_GROUPS):
        acc = acc + _shift_down(acc, 1 << lvl)
        levels.append(acc)
    return _pool_select(levels, gi) / _pool_count(p.shape, gi) - p


def _pool_fwd(proj, pool_w, pool_scale):
    t = proj.shape[0]

    def body(p_ref, w_ref, s_ref, out_ref):
        gi = pl.program_id(0)
        pooled = _pooled(p_ref[...], gi)
        out_ref[...] = (_BDOT_PLAIN[0](pooled, w_ref[0]) * s_ref[0]).astype(out_ref.dtype)

    return pl.pallas_call(
        body, grid=(POOL_GROUPS,),
        in_specs=[pl.BlockSpec((t, POOL_GROUP_DIM), lambda g: (0, POOL_BLK + g)),
                  pl.BlockSpec((1, POOL_GROUP_DIM, POOL_GROUP_DIM), lambda g: (g, 0, 0)),
                  pl.BlockSpec((1, 1, POOL_GROUP_DIM), lambda g: (g, 0, 0))],
        out_specs=pl.BlockSpec((t, POOL_GROUP_DIM), lambda g: (0, GDN_WIDTH // POOL_GROUP_DIM + g)),
        out_shape=jax.ShapeDtypeStruct((t, 2 * GDN_WIDTH), BF16),
        compiler_params=_params("parallel"), name="pool_fwd",
    )(proj, pool_w, pool_scale)


def _pool_bwd(proj, pool_w, pool_scale, dmixin):
    t = proj.shape[0]
    nn, nt, tn = _BDOT_PLAIN

    def body(p_ref, w_ref, s_ref, d_ref, dp_ref, dw_ref, ds_ref):
        gi = pl.program_id(0)
        p = p_ref[...]
        pooled = _pooled(p, gi)
        mixed = nn(pooled, w_ref[0])
        d = d_ref[...]
        ds_ref[0] = jnp.sum(d * mixed, axis=0, keepdims=True)
        dmixed = d * s_ref[0]
        dw_ref[0] = tn(pooled, dmixed)
        dpooled = nt(dmixed, w_ref[0])
        acc = dpooled / _pool_count(p.shape, gi)
        levels = []
        for lvl in range(POOL_GROUPS):
            acc = acc + _shift_up(acc, 1 << lvl)
            levels.append(acc)
        dp_ref[...] = (_pool_select(levels, gi) - dpooled).astype(dp_ref.dtype)

    return pl.pallas_call(
        body, grid=(POOL_GROUPS,),
        in_specs=[pl.BlockSpec((t, POOL_GROUP_DIM), lambda g: (0, POOL_BLK + g)),
                  pl.BlockSpec((1, POOL_GROUP_DIM, POOL_GROUP_DIM), lambda g: (g, 0, 0)),
                  pl.BlockSpec((1, 1, POOL_GROUP_DIM), lambda g: (g, 0, 0)),
                  pl.BlockSpec((t, POOL_GROUP_DIM), lambda g: (0, GDN_WIDTH // POOL_GROUP_DIM + g))],
        out_specs=[pl.BlockSpec((t, POOL_GROUP_DIM), lambda g: (0, POOL_BLK + g)),
                   pl.BlockSpec((1, POOL_GROUP_DIM, POOL_GROUP_DIM), lambda g: (g, 0, 0)),
                   pl.BlockSpec((1, 1, POOL_GROUP_DIM), lambda g: (g, 0, 0))],
        out_shape=[jax.ShapeDtypeStruct((t, PROJ_COLS), BF16),
                   jax.ShapeDtypeStruct((POOL_GROUPS, POOL_GROUP_DIM, POOL_GROUP_DIM), F32),
                   jax.ShapeDtypeStruct((POOL_GROUPS, 1, POOL_GROUP_DIM), F32)],
        compiler_params=_params("parallel"), name="pool_bwd",
    )(proj, pool_w, pool_scale, dmixin)


def _ln_stats(s):
    mu = jnp.mean(s, axis=1, keepdims=True)
    xc = s - mu
    var = jnp.mean(xc * xc, axis=1, keepdims=True)
    rstd = lax.rsqrt(var + LN_EPS)
    return xc * rstd, rstd


def _ln_fwd(h_in, y, g, b, *, name, tm=256):
    t, d = h_in.shape
    tm = min(tm, t)

    def body(h_ref, y_ref, g_ref, b_ref, o_ref, o16_ref):
        xhat, _ = _ln_stats(ALPHA * h_ref[...] + y_ref[...])
        out = xhat * g_ref[...] + b_ref[...]
        o_ref[...] = out
        o16_ref[...] = out.astype(BF16)

    row = pl.BlockSpec((tm, d), lambda i: (i, 0))
    vec = pl.BlockSpec((1, d), lambda i: (0, 0))
    return pl.pallas_call(
        body, grid=(t // tm,), in_specs=[row, row, vec, vec], out_specs=[row, row],
        out_shape=[jax.ShapeDtypeStruct((t, d), F32), jax.ShapeDtypeStruct((t, d), BF16)],
        compiler_params=_params("parallel"), name=name,
    )(h_in, y, g, b)


def _ln_backward(xhat, rstd, dout, gain):
    dxhat = dout * gain
    m1 = jnp.mean(dxhat, axis=1, keepdims=True)
    m2 = jnp.mean(dxhat * xhat, axis=1, keepdims=True)
    return (rstd * (dxhat - m1 - xhat * m2), jnp.sum(dout * xhat, axis=0, keepdims=True),
            jnp.sum(dout, axis=0, keepdims=True))


def _ln_loss(h_in, y, g, b, target, *, name, tm=256):
    t, d = h_in.shape
    tm = min(tm, t)

    def body(h_ref, y_ref, g_ref, b_ref, t_ref, sq_ref, ds_ref, ds16_ref, dg_ref, dbias_ref):
        @pl.when(pl.program_id(0) == 0)
        def _():
            sq_ref[...] = jnp.zeros_like(sq_ref)
            dg_ref[...] = jnp.zeros_like(dg_ref)
            dbias_ref[...] = jnp.zeros_like(dbias_ref)

        xhat, rstd = _ln_stats(ALPHA * h_ref[...] + y_ref[...])
        err = xhat * g_ref[...] + b_ref[...] - t_ref[...]
        sq_ref[...] += jnp.sum(jnp.sum(err * err, axis=1, keepdims=True), axis=0, keepdims=True)
        ds, dg, dbias = _ln_backward(xhat, rstd, err * (1.0 / d), g_ref[...])
        ds_ref[...] = ds
        ds16_ref[...] = ds.astype(BF16)
        dg_ref[...] += dg
        dbias_ref[...] += dbias

    row = pl.BlockSpec((tm, d), lambda i: (i, 0))
    vec = pl.BlockSpec((1, d), lambda i: (0, 0))
    return pl.pallas_call(
        body, grid=(t // tm,), in_specs=[row, row, vec, vec, row],
        out_specs=[pl.BlockSpec((1, LANE), lambda i: (0, 0)), row, row, vec, vec],
        out_shape=[jax.ShapeDtypeStruct((1, LANE), F32), jax.ShapeDtypeStruct((t, d), F32),
                   jax.ShapeDtypeStruct((t, d), BF16), jax.ShapeDtypeStruct((1, d), F32), jax.ShapeDtypeStruct((1, d), F32)],
        compiler_params=_params("arbitrary"), name=name,
    )(h_in, y, g, b, target)


def _ln_bwd(h_in, y, g, d_a, d_b, *, name, tm=256):
    t, d = h_in.shape
    tm = min(tm, t)
    has_b = d_b is not None

    def body(*refs):
        if has_b:
            h_ref, y_ref, g_ref, da_ref, db_ref, ds_ref, ds16_ref, dg_ref, dbias_ref = refs
        else:
            h_ref, y_ref, g_ref, da_ref, ds_ref, ds16_ref, dg_ref, dbias_ref = refs

        @pl.when(pl.program_id(0) == 0)
        def _():
            dg_ref[...] = jnp.zeros_like(dg_ref)
            dbias_ref[...] = jnp.zeros_like(dbias_ref)

        xhat, rstd = _ln_stats(ALPHA * h_ref[...] + y_ref[...])
        dout = da_ref[...]
        if has_b:
            dout = dout + ALPHA * db_ref[...]
        ds, dg, dbias = _ln_backward(xhat, rstd, dout, g_ref[...])
        ds_ref[...] = ds
        ds16_ref[...] = ds.astype(BF16)
        dg_ref[...] += dg
        dbias_ref[...] += dbias

    row = pl.BlockSpec((tm, d), lambda i: (i, 0))
    vec = pl.BlockSpec((1, d), lambda i: (0, 0))
    args = [h_in, y, g, d_a] + ([d_b] if has_b else [])
    return pl.pallas_call(
        body, grid=(t // tm,), in_specs=[row, row, vec, row] + ([row] if has_b else []),
        out_specs=[row, row, vec, vec],
        out_shape=[jax.ShapeDtypeStruct((t, d), F32), jax.ShapeDtypeStruct((t, d), BF16),
                   jax.ShapeDtypeStruct((1, d), F32), jax.ShapeDtypeStruct((1, d), F32)],
        compiler_params=_params("arbitrary"), name=name,
    )(*args)


def _attn_fn(q, k, v, dots):
    nn, nt, _ = dots
    s = nt(q, k) * (XATTN_HEAD_DIM ** -0.5)
    s = s - lax.stop_gradient(jnp.max(s, axis=1, keepdims=True))
    e = jnp.exp(s)
    p = e / jnp.sum(e, axis=1, keepdims=True)
    return nn(p, v)


def _attn_fwd(q, k, v, tq=512):
    t = q.shape[0]
    tq = min(tq, t)

    def body(q_ref, k_ref, v_ref, o_ref):
        o_ref[...] = _attn_fn(q_ref[...], k_ref[...], v_ref[...], _BDOT_PLAIN).astype(BF16)

    qs = pl.BlockSpec((tq, XATTN_HEAD_DIM), lambda h, i: (i, h))
    ks = pl.BlockSpec((MEM_LEN, XATTN_HEAD_DIM), lambda h, i: (0, h))
    return pl.pallas_call(
        body, grid=(XATTN_HEADS, t // tq), in_specs=[qs, ks, ks], out_specs=qs,
        out_shape=jax.ShapeDtypeStruct(q.shape, BF16), compiler_params=_params("parallel", "parallel"), name="xattn_fwd",
    )(q, k, v)


def _attn_bwd(q, k, v, do, tq=512):
    t = q.shape[0]
    tq = min(tq, t)

    def body(q_ref, k_ref, v_ref, do_ref, dq_ref, dk_ref, dv_ref):
        @pl.when(pl.program_id(1) == 0)
        def _():
            dk_ref[...] = jnp.zeros_like(dk_ref)
            dv_ref[...] = jnp.zeros_like(dv_ref)

        _, vjp = jax.vjp(lambda a, b, c: _attn_fn(a, b, c, _BDOT_VJP), q_ref[...].astype(F32), k_ref[...].astype(F32),
                         v_ref[...].astype(F32))
        dq, dk, dv = vjp(do_ref[...].astype(F32))
        dq_ref[...] = dq.astype(BF16)
        dk_ref[...] += dk
        dv_ref[...] += dv

    qs = pl.BlockSpec((tq, XATTN_HEAD_DIM), lambda h, i: (i, h))
    ks = pl.BlockSpec((MEM_LEN, XATTN_HEAD_DIM), lambda h, i: (0, h))
    return pl.pallas_call(
        body, grid=(XATTN_HEADS, t // tq), in_specs=[qs, ks, ks, qs], out_specs=[qs, ks, ks],
        out_shape=[jax.ShapeDtypeStruct(q.shape, BF16), jax.ShapeDtypeStruct(k.shape, F32), jax.ShapeDtypeStruct(v.shape, F32)],
        compiler_params=_params("parallel", "arbitrary"), name="xattn_bwd",
    )(q, k, v, do)


def _local_step(x, x16, mem, target, weights_of, grads_ready):
    def behind(vec, token):
        return vec if token is None else vec + token

    w = dict(weights_of("mixer", None))
    proj = _mm(x16, w["w_in"], tb=True, tn=768, name="mm_in_proj")
    mixin = _pool_fwd(proj, w["pool_w"], w["pool_scale"])
    post = _gdn_prep_fwd(proj, w["conv_w"])
    token = weights_of("ahead", post)
    chunked, t_inv = _gdn_local_fwd(post, proj, behind(w["alog_row"], token), w["dtb_row"])
    o_raw, saved = _gdn_state_fwd(*chunked)
    mixin = _onorm_fwd(o_raw, proj, w["gdn_norm_w"], mixin)
    w.update(weights_of("attn", mixin))
    mix = _mm(mixin, w["w_out"], name="mm_out_proj")
    h1, h1_16 = _ln_fwd(x, mix, w["ln1_g"], w["ln1_b"], name="ln1_fwd")
    xq = _mm(h1_16, w["xq_w"], out_dtype=BF16, name="mm_xq")
    xk = _mm(mem, w["xk_w"], out_dtype=BF16, name="mm_xk")
    xv = _mm(mem, w["xv_w"], out_dtype=BF16, name="mm_xv")
    xo = _attn_fwd(xq, xk, xv)
    xa = _mm(xo, w["xo_w"], name="mm_xo")
    h2, h2_16 = _ln_fwd(h1, xa, w["ln2_g"], w["ln2_b"], name="ln2_fwd")
    w.update(weights_of("up", h2_16))
    act, relu = _mm(h2_16, w["w_up"], b_chunks=True, epi="relu2", name="mm_up")
    w.update(weights_of("down", act))
    ff = _mm(act, w["w_down"], tn=512, tk=2048, name="mm_down")
    g = {}
    sq, ds3, ds3_16, g["ln3_g"], g["ln3_b"] = _ln_loss(h2, ff, w["ln3_g"], w["ln3_b"], target, name="ln3_loss")

    gw_down = _mm(act, ds3_16, ta=True, out_dtype=BF16, tm=512, tn=D_MODEL, name="mm_gw_down")
    du = _mm(ds3_16, w["w_down"], tb=True, epi="mul2r", extra=relu, name="mm_du")
    gw_up = _mm(h2_16, du, ta=True, out_dtype=BF16, o_chunks=True, name="mm_gw_up")
    token = grads_ready("mlp", {"w_down": gw_down, "w_up": gw_up})
    dh2 = _mm(du, w["w_up"], tb=True, b_chunks=True, tn=1024, tk=1024, name="mm_dh2")
    ds2, ds2_16, g["ln2_g"], g["ln2_b"] = _ln_bwd(h1, xa, behind(w["ln2_g"], token), dh2, ds3, name="ln2_bwd")
    gw_xo = _mm(xo, ds2_16, ta=True, out_dtype=BF16, name="mm_gw_xo")
    dxo = _mm(ds2_16, w["xo_w"], tb=True, out_dtype=BF16, name="mm_dxo")
    dxq, dxk, dxv = _attn_bwd(xq, xk, xv, dxo)
    gw_xq = _mm(h1_16, dxq, ta=True, out_dtype=BF16, name="mm_gw_xq")
    gw_xk = _mm(mem, dxk, ta=True, out_dtype=BF16, name="mm_gw_xk")
    gw_xv = _mm(mem, dxv, ta=True, out_dtype=BF16, name="mm_gw_xv")
    token = grads_ready("attn", {"xo_w": gw_xo, "xq_w": gw_xq, "xk_w": gw_xk, "xv_w": gw_xv})
    dh1 = _mm(dxq, w["xq_w"], tb=True, name="mm_dh1")
    ds1, ds1_16, g["ln1_g"], g["ln1_b"] = _ln_bwd(x, mix, behind(w["ln1_g"], token), dh1, ds2, name="ln1_bwd")
    gw_out = _mm(mixin, ds1_16, ta=True, out_dtype=BF16, name="mm_gw_out")
    dmixin = _mm(ds1_16, w["w_out"], tb=True, name="mm_dmixin")
    dproj, gw_pool, g["pool_scale"] = _pool_bwd(proj, w["pool_w"], w["pool_scale"], dmixin)
    token = grads_ready("mix", {"w_out": gw_out, "pool_w": gw_pool})
    do_raw, dproj, g["gdn_norm_w"] = _onorm_bwd(o_raw, proj, behind(w["gdn_norm_w"], token), dmixin, dproj)
    cots = _gdn_state_bwd(*chunked, saved, do_raw)
    token = grads_ready("tick", {"after": cots[0]})
    dpost, dproj, g["alog_row"], g["dtb_row"] = _gdn_local_bwd(post, proj, behind(w["alog_row"], token), w["dtb_row"],
                                                               t_inv, cots, dproj)
    dproj, g["conv_w"] = _gdn_prep_bwd(proj, w["conv_w"], dpost, dproj)
    gw_in = _mm(dproj, x16, ta=True, out_dtype=BF16, tm=768, tn=D_MODEL, name="mm_gw_in")
    token = grads_ready("in", {"w_in": gw_in})
    if token is not None:
        ds1, _ = lax.optimization_barrier((ds1, token))
    grad_x = _mm(dproj, w["w_in"], tk=1792, epi="add", extra=ds1, add_scale=ALPHA, name="mm_dx")
    return sq, grad_x, g


_MATRICES = ("w_in", "pool_w", "w_out", "xq_w", "xk_w", "xv_w", "xo_w", "w_up", "w_down")
_VECTORS = ("a_log", "dt_bias", "gdn_norm_w", "pool_scale", "ln1_g", "ln1_b", "ln2_g", "ln2_b", "ln3_g", "ln3_b")
_BA_SPLIT = BA_OFF + 2 * GDN_HEADS


def _lane_row(v, offset):
    return jnp.zeros((1, LANE), F32).at[0, offset:offset + v.shape[0]].set(v)


_GROUP_VECTORS = {"mixer": (), "attn": ("ln1_g", "ln1_b", "ln2_g", "ln2_b"), "up": (), "down": ("ln3_g", "ln3_b")}


def _group_weights(group, full):
    w = {n: full[n].reshape(1, D_MODEL) for n in _GROUP_VECTORS[group]}
    if group == "mixer":
        w.update({
            "w_in": _w_in_padded(full["w_in"]),
            "conv_w": full["conv_w"],
            "alog_row": _lane_row(full["a_log"], GDN_HEADS),
            "dtb_row": _lane_row(full["dt_bias"], GDN_HEADS),
            "gdn_norm_w": full["gdn_norm_w"].reshape(1, LANE),
            "pool_w": full["pool_w"],
            "pool_scale": full["pool_scale"].reshape(POOL_GROUPS, 1, POOL_GROUP_DIM),
        })
    else:
        w.update({n: full[n] for n in dict(_GATHER_GROUPS)[group]})
    return w


def _w_in_row_map():
    per = IN_COLS // N_DEV
    gap = POOL_OFF - _BA_SPLIT
    pieces = []
    for d in range(N_DEV):
        lo, hi = d * per, (d + 1) * per
        if hi <= _BA_SPLIT:
            pieces.append([(0, lo, per)])
        elif lo >= _BA_SPLIT:
            pieces.append([(0, lo + gap, per)])
        else:
            pieces.append([(0, lo, _BA_SPLIT - lo), (_BA_SPLIT - lo, POOL_OFF, hi - _BA_SPLIT)])
    return pieces


_W_IN_LANES = 256


def _w_in_padded(blocks):
    def body(b_ref, o_ref):
        for d, pieces in enumerate(_w_in_row_map()):
            for src, dst, rows in pieces:
                o_ref[dst:dst + rows, :] = b_ref[d, src:src + rows, :]
        o_ref[_BA_SPLIT:POOL_OFF, :] = jnp.zeros((POOL_OFF - _BA_SPLIT, _W_IN_LANES), o_ref.dtype)

    n, per, cols = blocks.shape
    return pl.pallas_call(
        body, grid=(cols // _W_IN_LANES,), in_specs=[pl.BlockSpec((n, per, _W_IN_LANES), lambda j: (0, 0, j))],
        out_specs=pl.BlockSpec((PROJ_COLS, _W_IN_LANES), lambda j: (0, j)),
        out_shape=jax.ShapeDtypeStruct((PROJ_COLS, cols), blocks.dtype), compiler_params=_params("parallel"),
        name="w_in_padded")(blocks)


def _w_in_chunks(g):
    def body(g_ref, o_ref):
        for d, pieces in enumerate(_w_in_row_map()):
            for dst, src, rows in pieces:
                o_ref[d, dst:dst + rows, :] = g_ref[src:src + rows, :]

    cols = g.shape[1]
    per = IN_COLS // N_DEV
    return pl.pallas_call(
        body, grid=(cols // _W_IN_LANES,), in_specs=[pl.BlockSpec((PROJ_COLS, _W_IN_LANES), lambda j: (0, j))],
        out_specs=pl.BlockSpec((N_DEV, per, _W_IN_LANES), lambda j: (0, 0, j)),
        out_shape=jax.ShapeDtypeStruct((N_DEV, per, cols), g.dtype), compiler_params=_params("parallel"),
        name="w_in_chunks")(g)


def _finish_small_grads(g):
    out = {"conv_w": g["conv_w"]}
    out["a_log"] = g["alog_row"][0, GDN_HEADS:2 * GDN_HEADS]
    out["dt_bias"] = g["dtb_row"][0, GDN_HEADS:2 * GDN_HEADS]
    out["gdn_norm_w"] = g["gdn_norm_w"].reshape(LANE)
    out["pool_scale"] = g["pool_scale"].reshape(POOL_GROUPS * POOL_GROUP_DIM)
    for n in ("ln1_g", "ln1_b", "ln2_g", "ln2_b", "ln3_g", "ln3_b"):
        out[n] = g[n].reshape(D_MODEL)
    return out


def _adamw_math(w, g, m, v):
    m = ADAM_B1 * m + (1.0 - ADAM_B1) * g
    v = ADAM_B2 * v + (1.0 - ADAM_B2) * (g * g)
    m_hat = m / (1.0 - ADAM_B1 ** ADAM_STEP)
    v_hat = v / (1.0 - ADAM_B2 ** ADAM_STEP)
    delta = -ADAM_LR * (m_hat / (jnp.sqrt(v_hat) + ADAM_EPS) + ADAM_WD * w)
    return delta, m, v


ADAMW_TILE_ELEMS = 256 * 1024
CHIP_SUM_TILE_ELEMS = 1024 * 1024


def _shard_tile(r, c, elems):
    for rows in (1024, 512, 256, 128):
        if r % rows == 0 and rows * c <= elems:
            return rows, c
    if r % 128 == 0:
        return 128, c
    return r, 256 if c % 256 == 0 else c


def _adamw_shard(parts, own, me, w, m, v, *, name):
    s, r, c = parts.shape
    tr, tc = _shard_tile(r, c, ADAMW_TILE_ELEMS)
    assert r % tr == 0 and c % tc == 0, (name, r, c)
    unit_axis = w.ndim == 3
    at = (slice(None), 0, slice(None)) if unit_axis else Ellipsis

    def body(me_ref, p_ref, own_ref, w_ref, m_ref, v_ref, g_ref, d_ref, nm_ref, nv_ref):
        mine = own_ref[...].astype(F32)
        g = None
        for i in range(s):
            part = jnp.where(me_ref[0] == i, mine, p_ref[i].astype(F32))
            g = part if g is None else g + part
        delta, nm, nv = _adamw_math(w_ref[at], g, m_ref[at], v_ref[at])
        g_ref[at] = g
        d_ref[at] = delta
        nm_ref[at] = nm
        nv_ref[at] = nv

    if unit_axis:
        blk = pl.BlockSpec((tr, 1, tc), lambda i, j, me_ref: (i, 0, j))
        out = jax.ShapeDtypeStruct((r, 1, c), F32)
    else:
        blk = pl.BlockSpec((tr, tc), lambda i, j, me_ref: (i, j))
        out = jax.ShapeDtypeStruct((r, c), F32)
    return pl.pallas_call(
        body,
        grid_spec=pltpu.PrefetchScalarGridSpec(
            num_scalar_prefetch=1, grid=(r // tr, c // tc),
            in_specs=[pl.BlockSpec((s, tr, tc), lambda i, j, me_ref: (0, i, j)),
                      pl.BlockSpec((None, tr, tc), lambda i, j, me_ref: (me_ref[0], i, j)), blk, blk, blk],
            out_specs=[blk, blk, blk, blk]),
        out_shape=[out, out, out, out], compiler_params=_params("parallel", "parallel"), name=name,
    )(me, parts, own, w, m, v)


N_CHIPS = N_DEV // 2


def _chip_sums(chunks, from_sibling, core, *, name):
    _, r, c = chunks.shape
    tr, tc = _shard_tile(r, c, CHIP_SUM_TILE_ELEMS)
    assert r % tr == 0 and c % tc == 0, (name, r, c)

    def body(core_ref, mine_ref, other_ref, o_ref):
        o_ref[...] = (mine_ref[...].astype(F32) + other_ref[...].astype(F32)).astype(o_ref.dtype)

    by_chip = pl.BlockSpec((None, tr, tc), lambda q, i, j, core_ref: (q, i, j))
    return pl.pallas_call(
        body,
        grid_spec=pltpu.PrefetchScalarGridSpec(
            num_scalar_prefetch=1, grid=(N_CHIPS, r // tr, c // tc),
            in_specs=[pl.BlockSpec((None, tr, tc), lambda q, i, j, core_ref: (2 * q + core_ref[0], i, j)), by_chip],
            out_specs=by_chip),
        out_shape=jax.ShapeDtypeStruct((N_CHIPS, r, c), chunks.dtype),
        compiler_params=_params("parallel", "parallel", "parallel"), name=name,
    )(core, chunks, from_sibling)


def _place():
    return lax.axis_index("x"), lax.axis_index("y"), lax.axis_index("c")


def _slot(px, py, pc):
    return 4 * px + 2 * py + pc


_HBM = pl.BlockSpec(memory_space=pltpu.HBM)


_SEM = pl.BlockSpec(memory_space=pltpu.SEMAPHORE)
_ANY = pl.BlockSpec(memory_space=pl.ANY)
_EFFECT = pltpu.SideEffectType.DATAFLOW_SIDE_EFFECTING
_N_PEERS = N_DEV - 1


def _peer(k, x, y, c):
    return (1 - x if k & 4 else x, 1 - y if k & 2 else y, 1 - c if k & 1 else c)


_EXCHANGE_BITS = {"gather_chips": (1, 2, 4, 6), "gather_pass": (2, 4, 6), "scatter_sibling": (1, 1, 1, 1),
                  "scatter_chips": (2, 4, 6)}


def _exchange_copy(mode, src, land, w, i, place, send_sems, recv_sems, receiving):
    bits = _EXCHANGE_BITS[mode]
    k = bits[i]
    peer = _peer(k, *place)
    me = _slot(*place)
    if mode == "gather_chips":
        to, src_ref, sent_to, got_at = peer, src[w], me, _slot(*peer)
    elif mode == "gather_pass":
        blk = _slot(*peer)
        to, src_ref, sent_to, got_at = _peer(1, *place), land[w].at[blk], blk, _slot(*_peer(k | 1, *place))
    elif mode == "scatter_sibling":
        to, src_ref, sent_to, got_at = peer, src[w].at[2 * i + 1 - place[2]], i, i
    else:
        to, src_ref, sent_to, got_at = peer, src[w].at[_slot(*peer) // 2], me // 2, _slot(*peer) // 2
    sem = w * len(bits) + i
    return pltpu.make_async_remote_copy(
        src_ref=src_ref, dst_ref=land[w].at[got_at if receiving else sent_to], send_sem=send_sems.at[sem],
        recv_sem=recv_sems.at[sem], device_id=to, device_id_type=MESH)


def _exchange_start(mode, srcs, lands, after, *, name):
    ns, nl = len(srcs), len(lands)
    n_sem = nl * len(_EXCHANGE_BITS[mode])

    def body(*refs):
        src, land = refs[:ns], refs[ns:ns + nl]
        send_sems, recv_sems = refs[ns + nl + 1:ns + nl + 3]
        token = refs[-1]
        place = _place()
        for w in range(nl):
            for i in range(len(_EXCHANGE_BITS[mode])):
                _exchange_copy(mode, src, land, w, i, place, send_sems, recv_sems, receiving=False).start()
        token[...] = jnp.zeros_like(token)

    sems = pltpu.SemaphoreType.DMA((n_sem,))
    arrays = list(srcs) + list(lands)
    res = pl.pallas_call(
        body, name=name, in_specs=[_HBM] * (ns + nl) + [_ANY],
        out_specs=(_SEM, _SEM, *([_HBM] * (ns + nl)), pl.BlockSpec(memory_space=pltpu.VMEM)),
        out_shape=(sems, sems, *[pltpu.HBM(a.shape, a.dtype) for a in arrays], jax.ShapeDtypeStruct((8, LANE), F32)),
        input_output_aliases={i: 2 + i for i in range(ns + nl)},
        compiler_params=pltpu.CompilerParams(has_side_effects=_EFFECT),
    )(*[pltpu.with_memory_space_constraint(a, pltpu.HBM) for a in arrays], after)
    return res[0], res[1], list(res[2:2 + ns]), list(res[2 + ns:2 + ns + nl]), res[-1]


def _exchange_wait(mode, started, after, *, name):
    send_sems, recv_sems, srcs, lands, _ = started
    ns, nl = len(srcs), len(lands)

    def body(*refs):
        src, land = refs[:ns], refs[ns:ns + nl]
        send_sems, recv_sems = refs[ns + nl:ns + nl + 2]
        place = _place()
        for w in range(nl):
            for i in range(len(_EXCHANGE_BITS[mode])):
                cp = _exchange_copy(mode, src, land, w, i, place, send_sems, recv_sems, receiving=True)
                cp.wait_send()
                cp.wait_recv()

    arrays = list(srcs) + list(lands)
    res = pl.pallas_call(
        body, name=name, in_specs=[_HBM] * (ns + nl) + [_SEM, _SEM, _ANY], out_specs=[_HBM] * (ns + nl),
        out_shape=[pltpu.HBM(a.shape, a.dtype) for a in arrays],
        input_output_aliases={i: i for i in range(ns + nl)},
        compiler_params=pltpu.CompilerParams(has_side_effects=_EFFECT),
    )(*arrays, send_sems, recv_sems, after)
    return list(res[:ns]), list(res[ns:])


def _small_allreduce_adamw(gvec, wvec, mvec, vvec):
    rows, length = gvec.shape

    def body(g_ref, w_ref, m_ref, v_ref, gs_ref, d_ref, nm_ref, nv_ref, slots, send_sems, recv_sems):
        x, y, c = _place()
        me = _slot(x, y, c)
        slots[me] = g_ref[...]
        sends = []
        for k in range(1, N_DEV):
            peer = _peer(k, x, y, c)
            sends.append(pltpu.make_async_remote_copy(
                src_ref=g_ref, dst_ref=slots.at[me], send_sem=send_sems.at[k - 1], recv_sem=recv_sems.at[k - 1],
                device_id=peer, device_id_type=MESH))
        for cp in sends:
            cp.start()
        for k in range(1, N_DEV):
            peer = _peer(k, x, y, c)
            pltpu.make_async_remote_copy(
                src_ref=g_ref, dst_ref=slots.at[_slot(*peer)], send_sem=send_sems.at[k - 1], recv_sem=recv_sems.at[k - 1],
                device_id=peer, device_id_type=MESH).wait_recv()
        for cp in sends:
            cp.wait_send()
        g = slots[0]
        for s in range(1, N_DEV):
            g = g + slots[s]
        delta, nm, nv = _adamw_math(w_ref[...], g, m_ref[...], v_ref[...])
        gs_ref[...] = g
        d_ref[...] = delta
        nm_ref[...] = nm
        nv_ref[...] = nv

    vmem = pl.BlockSpec(memory_space=pltpu.VMEM)
    out = jax.ShapeDtypeStruct((rows, length), F32)
    return pl.pallas_call(
        body, in_specs=[vmem] * 4, out_specs=[vmem] * 4, out_shape=[out] * 4,
        scratch_shapes=[pltpu.VMEM((N_DEV, rows, length), F32), pltpu.SemaphoreType.DMA((N_DEV - 1,)),
                        pltpu.SemaphoreType.DMA((N_DEV - 1,))],
        name="small_allreduce_adamw",
    )(gvec, wvec, mvec, vvec)


_SMALL_SEGMENTS = (("a_log", GDN_HEADS), ("dt_bias", GDN_HEADS), ("gdn_norm_w", HEAD_DIM), ("pool_scale", GDN_WIDTH),
                   ("ln1_g", D_MODEL), ("ln1_b", D_MODEL), ("ln2_g", D_MODEL), ("ln2_b", D_MODEL),
                   ("ln3_g", D_MODEL), ("ln3_b", D_MODEL), ("conv_w", CONV_K * QKV_COLS))
_SMALL_ROWS = 8
_SMALL_LEN = -(-sum(sz for _, sz in _SMALL_SEGMENTS) // (_SMALL_ROWS * LANE)) * LANE


def _pack_small(vals):
    parts = [vals[n].reshape(-1).astype(F32) if n in vals else jnp.zeros((sz,), F32) for n, sz in _SMALL_SEGMENTS]
    flat = jnp.concatenate(parts)
    flat = jnp.pad(flat, (0, _SMALL_ROWS * _SMALL_LEN - flat.shape[0]))
    return flat.reshape(_SMALL_ROWS, _SMALL_LEN)


def _unpack_small(vec):
    flat = vec.reshape(-1)
    out, off = {}, 0
    for n, sz in _SMALL_SEGMENTS:
        out[n] = flat[off:off + sz]
        off += sz
    return out


_WEIGHT_ORDER = ("w_in", "conv_w", "a_log", "dt_bias", "gdn_norm_w", "pool_w", "pool_scale", "w_out", "ln1_g", "ln1_b",
                 "xq_w", "xk_w", "xv_w", "xo_w", "ln2_g", "ln2_b", "w_up", "w_down", "ln3_g", "ln3_b")


def _shard2d(name, a):
    if name == "w_in":
        return a.T
    return a.reshape(-1, a.shape[-1]) if name == "pool_w" else a


def _update_view(name, a):
    return jnp.transpose(a, (2, 0, 1)) if name == "w_in" else _shard2d(name, a[0])


def _shard_result(name, r, shape):
    return jnp.transpose(r, (1, 2, 0)) if name == "w_in" else r.reshape(shape)


def _gathered_to_full(name, gth):
    if name in ("w_up", "w_in"):
        return gth
    if name == "conv_w":
        return jnp.transpose(gth, (1, 0, 2)).reshape(gth.shape[1], N_DEV * gth.shape[2])
    if name == "pool_w":
        g4 = gth.reshape(N_DEV, POOL_GROUPS, POOL_GROUP_DIM // N_DEV, POOL_GROUP_DIM)
        return jnp.transpose(g4, (1, 0, 2, 3)).reshape(POOL_GROUPS, POOL_GROUP_DIM, POOL_GROUP_DIM)
    return gth.reshape(N_DEV * gth.shape[1], gth.shape[2])


def _full_to_chunks(name, full):
    if name == "w_up":
        return full
    if name == "pool_w":
        g4 = full.reshape(POOL_GROUPS, N_DEV, POOL_GROUP_DIM // N_DEV, POOL_GROUP_DIM)
        return jnp.transpose(g4, (1, 0, 2, 3)).reshape(N_DEV, POOL_GROUPS * POOL_GROUP_DIM // N_DEV, POOL_GROUP_DIM)
    return full.reshape(N_DEV, full.shape[0] // N_DEV, full.shape[1])


_GATHER_GROUPS = (("mixer", ("w_in", "conv_w", "pool_w")), ("attn", ("w_out", "xq_w", "xk_w", "xv_w", "xo_w")),
                  ("up", ("w_up",)), ("down", ("w_down",)))


def _grad_chunks(name, g):
    if name == "w_in":
        return _w_in_chunks(g.astype(BF16))
    return _full_to_chunks(name, g.astype(BF16))


def kernel(x, mem, w_in, conv_w, a_log, dt_bias, gdn_norm_w, pool_w, pool_scale, w_out, ln1_g, ln1_b, xq_w, xk_w, xv_w, xo_w, ln2_g, ln2_b, w_up, w_down, ln3_g, ln3_b, loss_target, m_w_in, m_conv_w, m_a_log, m_dt_bias, m_gdn_norm_w, m_pool_w, m_pool_scale, m_w_out, m_ln1_g, m_ln1_b, m_xq_w, m_xk_w, m_xv_w, m_xo_w, m_ln2_g, m_ln2_b, m_w_up, m_w_down, m_ln3_g, m_ln3_b, v_w_in, v_conv_w, v_a_log, v_dt_bias, v_gdn_norm_w, v_pool_w, v_pool_scale, v_w_out, v_ln1_g, v_ln1_b, v_xq_w, v_xk_w, v_xv_w, v_xo_w, v_ln2_g, v_ln2_b, v_w_up, v_w_down, v_ln3_g, v_ln3_b):
    args = dict(locals())
    wt = {n: args[n][0] for n in _WEIGHT_ORDER}
    mo = {n: args["m_" + n][0] for n in _WEIGHT_ORDER}
    vo = {n: args["v_" + n][0] for n in _WEIGHT_ORDER}

    me = _slot(*_place())
    me_arr = jnp.reshape(me, (1,)).astype(jnp.int32)
    nothing = jnp.zeros((8, LANE), F32)

    def landing_zones(names):
        shards = [_shard2d(n, wt[n]).astype(F32 if n == "conv_w" else BF16) for n in names]
        zones = [lax.dynamic_update_slice(lax.empty((N_DEV, *s.shape), s.dtype), s[None], (me, 0, 0)) for s in shards]
        return shards, zones

    chip_arr = jnp.reshape(me // 2, (1,)).astype(jnp.int32)
    core_arr = jnp.reshape(lax.axis_index("c"), (1,)).astype(jnp.int32)
    names_of = dict(_GATHER_GROUPS)
    gathers = {}
    prepared = {}

    def gather_chips(group, after):
        shards, zones = prepared.pop(group) if group in prepared else landing_zones(names_of[group])
        gathers[group] = _exchange_start("gather_chips", shards, zones, after, name="gather_chips_" + group)
        return gathers[group][4]

    def gather_pass(group, after):
        _, zones = _exchange_wait("gather_chips", gathers[group], after, name=f"gather_chips_{group}_wait")
        gathers[group] = _exchange_start("gather_pass", [], zones, nothing, name="gather_pass_" + group)
        return gathers[group][4]

    def gathered(group, after, token=None):
        _, zones = _exchange_wait("gather_pass", gathers[group], after, name=f"gather_pass_{group}_wait")
        full = {n: _gathered_to_full(n, z) for n, z in zip(names_of[group], zones)}
        full.update({n: wt[n] if token is None else wt[n] + token for n in _VECTORS})
        return _group_weights(group, full)

    token = gather_chips("mixer", nothing)
    x16 = _cast_bf16(x[0], name="cast_x")
    later = {group: landing_zones(names_of[group]) for group in ("attn", "up", "down")}
    token, x16, later = lax.optimization_barrier((token, x16, later))
    prepared.update(later)
    token = gather_chips("attn", gather_pass("mixer", token))

    def weights_of(group, after):
        if group == "mixer":
            return gathered(group, gathers["attn"][4])
        if group == "ahead":
            return gather_chips("up", gather_pass("attn", after))[0:1, 0:1]
        if group == "attn":
            weights = gathered(group, after)
            token = gather_chips("down", gather_pass("up", weights["w_out"]))[0, 0]
            return {n: (v + token if n == "ln1_g" else v) for n, v in weights.items()}
        if group == "up":
            weights = gathered(group, after)
            gather_pass("down", weights["w_up"])
            return weights
        return gathered(group, after)

    scatters = {}
    in_flight = []

    def chip_stage(after):
        group, names, started = in_flight.pop()
        chunks, from_sibling = _exchange_wait("scatter_sibling", started, after, name=f"scatter_sibling_{group}_wait")
        sums = [_chip_sums(c, f, core_arr, name=f"chip_sums_{n}") for n, c, f in zip(names, chunks, from_sibling)]
        scatters[group] = (names, _exchange_start("scatter_chips", sums, [lax.empty(s.shape, s.dtype) for s in sums],
                                                  nothing, name="scatter_chips_" + group))
        return scatters[group][1][4]

    def grads_ready(group, grads):
        if group == "tick":
            return chip_stage(grads["after"])[0:1, 0:1] if in_flight else None
        names = tuple(grads)
        chunks = [_grad_chunks(n, grads[n]) for n in names]
        token = chip_stage(chunks[0]) if in_flight else nothing
        zones = [lax.empty((N_CHIPS, *c.shape[1:]), c.dtype) for c in chunks]
        started = _exchange_start("scatter_sibling", chunks, zones, token, name="scatter_sibling_" + group)
        in_flight.append((group, names, started))
        return started[4][0:1, 0:1]

    sq, grad_x, g = _local_step(x[0], x16, mem[0], loss_target[0], weights_of, grads_ready)
    small = _finish_small_grads(g)

    out = {}
    after = chip_stage(grad_x)
    for group, (names, started) in scatters.items():
        sums, lands = _exchange_wait("scatter_chips", started, after, name=f"scatter_chips_{group}_wait")
        for n, parts, own in zip(names, lands, sums):
            res = _adamw_shard(parts, own, chip_arr, _update_view(n, args[n]), _update_view(n, args["m_" + n]),
                               _update_view(n, args["v_" + n]), name="adamw_" + n)
            out[n] = [_shard_result(n, r, args[n].shape) for r in res]
            after = res[1]

    packed, _ = lax.optimization_barrier((_pack_small(small), after))
    gs, ds, ms, vs = _small_allreduce_adamw(
        packed, _pack_small({n: wt[n] for n in _VECTORS}), _pack_small({n: mo[n] for n in _VECTORS}),
        _pack_small({n: vo[n] for n in _VECTORS}))
    gs, ds, ms, vs = _unpack_small(gs), _unpack_small(ds), _unpack_small(ms), _unpack_small(vs)
    cols = conv_w.shape[-1]
    conv_full = gs["conv_w"].reshape(CONV_K, QKV_COLS)
    conv_mine = lax.dynamic_slice(conv_full, (0, me * cols), (CONV_K, cols))[None]
    res = _adamw_shard(conv_mine, conv_mine, jnp.zeros((1,), jnp.int32), wt["conv_w"], mo["conv_w"], vo["conv_w"],
                       name="adamw_conv_w")
    out["conv_w"] = [r.reshape(conv_w.shape) for r in res]
    for n in _VECTORS:
        out[n] = [t[n].reshape(args[n].shape) for t in (gs, ds, ms, vs)]

    loss = lax.psum(0.5 * sq[0, 0] / D_MODEL, ("x", "y", "c"))
    return (loss, grad_x[None], *[out[n][0] for n in _WEIGHT_ORDER], *[out[n][1] for n in _WEIGHT_ORDER],
            *[out[n][2] for n in _WEIGHT_ORDER], *[out[n][3] for n in _WEIGHT_ORDER])
```

```python
import functools
import math

import jax
import jax.numpy as jnp
from jax import lax
from jax.experimental import pallas as pl
from jax.experimental.pallas import tpu as pltpu

F32 = jnp.float32
BF16 = jnp.bfloat16
MESH = pl.DeviceIdType.MESH

N_DEV = 8
D_MODEL = 2048
GDN_WIDTH = 1024
GDN_HEADS = 8
HEAD_DIM = 128
CONV_K = 4
CHUNK = 64
POOL_GROUPS = 4
POOL_GROUP_DIM = 256
MEM_LEN = 256
XATTN_HEADS = 4
XATTN_HEAD_DIM = 512
D_FF = 8192
IN_COLS = 5136
ALPHA = 2.0 ** 0.25
LN_EPS = 1e-5
NORM_EPS = 1e-6

LANE = 128
QKV_COLS = 3 * GDN_WIDTH
Z_OFF = QKV_COLS
BA_OFF = 4 * GDN_WIDTH
POOL_OFF = BA_OFF + 2 * LANE
PROJ_COLS = POOL_OFF + GDN_WIDTH
Z_BLK = Z_OFF // LANE
BA_BLK = BA_OFF // LANE
POOL_BLK = POOL_OFF // POOL_GROUP_DIM

ADAM_LR = 0.001
ADAM_B1 = 0.9
ADAM_B2 = 0.999
ADAM_EPS = 1e-08
ADAM_WD = 0.01
ADAM_STEP = 10

VMEM_LIMIT_BYTES = 48 * 1024 * 1024


def _params(*sem):
    return pltpu.CompilerParams(dimension_semantics=sem if sem else None, vmem_limit_bytes=VMEM_LIMIT_BYTES)


def _make_dots(cast, precision, batched=False):
    lead = 1 if batched else 0
    batch = ((0,), (0,)) if batched else ((), ())

    def dg(a, b, ca, cb):
        if cast is not None:
            a = a.astype(cast)
            b = b.astype(cast)
        return lax.dot_general(a, b, (((ca + lead,), (cb + lead,)), batch), precision=precision, preferred_element_type=F32)

    def nn_(a, b):
        return dg(a, b, 1, 0)

    def nt_(a, b):
        return dg(a, b, 1, 1)

    def tn_(a, b):
        return dg(a, b, 0, 0)

    @jax.custom_vjp
    def nn(a, b):
        return nn_(a, b)

    nn.defvjp(lambda a, b: (nn_(a, b), (a, b)), lambda r, g: (nt_(g, r[1]), tn_(r[0], g)))

    @jax.custom_vjp
    def nt(a, b):
        return nt_(a, b)

    nt.defvjp(lambda a, b: (nt_(a, b), (a, b)), lambda r, g: (nn_(g, r[1]), tn_(g, r[0])))

    @jax.custom_vjp
    def tn(a, b):
        return tn_(a, b)

    tn.defvjp(lambda a, b: (tn_(a, b), (a, b)), lambda r, g: (nt_(r[1], g), nn_(r[0], g)))

    return (nn_, nt_, tn_), (nn, nt, tn)


_BDOT_PLAIN, _BDOT_VJP = _make_dots(BF16, None)
_BDOT_BATCH_PLAIN, _BDOT_BATCH_VJP = _make_dots(BF16, None, batched=True)
_FDOT_BATCH_PLAIN, _FDOT_BATCH_VJP = _make_dots(BF16, None, batched=True)


def _mm(a, b, *, ta=False, tb=False, out_dtype=F32, tm=None, tn=512, tk=None, epi=None, extra=None, add_scale=1.0,
        b_chunks=False, o_chunks=False, name):
    m, k = (a.shape[1], a.shape[0]) if ta else a.shape
    if b_chunks:
        n, kb = (b.shape[1], N_DEV * b.shape[2]) if tb else (N_DEV * b.shape[2], b.shape[1])
    else:
        n, kb = b.shape if tb else (b.shape[1], b.shape[0])
    assert kb == k, (name, a.shape, b.shape)
    tm, tn, tk = min(tm or m, m), min(tn, n), min(tk or k, k)
    assert m % tm == 0 and n % tn == 0 and k % tk == 0, (name, m, n, k)
    nk = k // tk
    dims = (((0 if ta else 1,), (1 if tb else 0,)), ((), ()))
    n_extra = 0 if epi in (None, "relu2") else 1
    n_out = 2 if epi == "relu2" else 1
    if epi in ("relu2", "mul2r"):
        out_dtype = BF16

    def body(*refs):
        a_ref, b_ref = refs[:2]
        c_ref = refs[2] if n_extra else None
        o_refs = refs[2 + n_extra:2 + n_extra + n_out]
        scr = refs[2 + n_extra + n_out:]
        r = lax.dot_general(a_ref[...].astype(BF16), b_ref[...].astype(BF16), dims, preferred_element_type=F32)

        def finish(v):
            if epi == "add":
                o_refs[0][...] = (v + add_scale * c_ref[...]).astype(out_dtype)
            elif epi == "relu2":
                p = jnp.maximum(v, 0.0)
                o_refs[0][...] = (p * p).astype(BF16)
                o_refs[1][...] = p.astype(BF16)
            elif epi == "mul2r":
                o_refs[0][...] = (v * (2.0 * c_ref[...].astype(F32))).astype(BF16)
            else:
                o_refs[0][...] = v.astype(out_dtype)

        if nk == 1:
            finish(r)
        else:
            acc = scr[0]
            kk = pl.program_id(2)

            @pl.when(kk == 0)
            def _():
                acc[...] = r

            @pl.when(kk > 0)
            def _():
                acc[...] += r

            @pl.when(kk == nk - 1)
            def _():
                finish(acc[...])

    a_spec = pl.BlockSpec((tk, tm), lambda i, j, kk: (kk, i)) if ta else pl.BlockSpec((tm, tk), lambda i, j, kk: (i, kk))
    if b_chunks and tb:
        kc = k // N_DEV // tk
        b_spec = pl.BlockSpec((None, tn, tk), lambda i, j, kk: (kk // kc, j, kk % kc))
    elif b_chunks:
        nc = n // N_DEV // tn
        b_spec = pl.BlockSpec((None, tk, tn), lambda i, j, kk: (j // nc, kk, j % nc))
    elif tb:
        b_spec = pl.BlockSpec((tn, tk), lambda i, j, kk: (j, kk))
    else:
        b_spec = pl.BlockSpec((tk, tn), lambda i, j, kk: (kk, j))
    mn_spec = pl.BlockSpec((tm, tn), lambda i, j, kk: (i, j))
    if o_chunks:
        oc = n // N_DEV // tn
        o_spec = pl.BlockSpec((None, tm, tn), lambda i, j, kk: (j // oc, i, j % oc))
        o_shape = jax.ShapeDtypeStruct((N_DEV, m, n // N_DEV), out_dtype)
    else:
        o_spec, o_shape = mn_spec, jax.ShapeDtypeStruct((m, n), out_dtype)
    res = pl.pallas_call(
        body, grid=(m // tm, n // tn, nk), in_specs=[a_spec, b_spec] + [mn_spec] * n_extra,
        out_specs=[o_spec] * n_out, out_shape=[o_shape] * n_out,
        scratch_shapes=[pltpu.VMEM((tm, tn), F32)] if nk > 1 else [],
        compiler_params=_params("parallel", "parallel", "arbitrary"), name=name,
    )(a, b, *([extra] if n_extra else []))
    return res if n_out > 1 else res[0]


def _cast_bf16(v, *, name, tm=512):
    t, d = v.shape
    tm = min(tm, t)

    def body(v_ref, o_ref):
        o_ref[...] = v_ref[...].astype(BF16)

    spec = pl.BlockSpec((tm, d), lambda i: (i, 0))
    return pl.pallas_call(body, grid=(t // tm,), in_specs=[spec], out_specs=spec,
                          out_shape=jax.ShapeDtypeStruct((t, d), BF16), compiler_params=_params("parallel"), name=name)(v)


def _shift_down(v, s):
    if s == 0:
        return v
    row = lax.broadcasted_iota(jnp.int32, v.shape, 0)
    return jnp.where(row >= s, pltpu.roll(v, s, axis=0), 0.0)


def _shift_up(v, s):
    if s == 0:
        return v
    t = v.shape[0]
    row = lax.broadcasted_iota(jnp.int32, v.shape, 0)
    return jnp.where(row < t - s, pltpu.roll(v, t - s, axis=0), 0.0)


def _post_col(j):
    return (j % GDN_HEADS) * 3 + j // GDN_HEADS


def _gdn_prep_fwd(proj, conv_w):
    t = proj.shape[0]

    def body(x_ref, w_ref, o_ref):
        j = pl.program_id(0)
        x = x_ref[...]
        y = jnp.zeros_like(x)
        for tap in range(CONV_K):
            y = y + w_ref[tap:tap + 1, :] * _shift_down(x, CONV_K - 1 - tap)
        c = y * jax.nn.sigmoid(y)
        nrm = c * lax.rsqrt(jnp.sum(c * c, axis=1, keepdims=True) + NORM_EPS)
        o_ref[...] = jnp.where(j < 2 * GDN_HEADS, nrm, c)

    return pl.pallas_call(
        body, grid=(QKV_COLS // LANE,),
        in_specs=[pl.BlockSpec((t, LANE), lambda j: (0, j)), pl.BlockSpec((CONV_K, LANE), lambda j: (0, j))],
        out_specs=pl.BlockSpec((t, LANE), lambda j: (0, _post_col(j))),
        out_shape=jax.ShapeDtypeStruct((t, QKV_COLS), F32),
        compiler_params=_params("parallel"), name="gdn_prep_fwd",
    )(proj, conv_w)


def _gdn_prep_bwd(proj, conv_w, dpost, dproj):
    t = proj.shape[0]

    def body(x_ref, w_ref, d_ref, _, dx_ref, dw_ref):
        j = pl.program_id(0)
        x = x_ref[...]
        xs = [_shift_down(x, CONV_K - 1 - tap) for tap in range(CONV_K)]
        y = jnp.zeros_like(x)
        for tap in range(CONV_K):
            y = y + w_ref[tap:tap + 1, :] * xs[tap]
        sig = jax.nn.sigmoid(y)
        c = y * sig
        r = lax.rsqrt(jnp.sum(c * c, axis=1, keepdims=True) + NORM_EPS)
        nrm = c * r
        d = d_ref[...]
        dc_norm = r * (d - nrm * jnp.sum(d * nrm, axis=1, keepdims=True))
        dc = jnp.where(j < 2 * GDN_HEADS, dc_norm, d)
        dy = dc * (sig * (1.0 + y * (1.0 - sig)))
        dx = jnp.zeros_like(x)
        for tap in range(CONV_K):
            dx = dx + _shift_up(w_ref[tap:tap + 1, :] * dy, CONV_K - 1 - tap)
            dw_ref[tap:tap + 1, :] = jnp.sum(dy * xs[tap], axis=0, keepdims=True)
        dx_ref[...] = dx.astype(dx_ref.dtype)

    return pl.pallas_call(
        body, grid=(QKV_COLS // LANE,),
        in_specs=[pl.BlockSpec((t, LANE), lambda j: (0, j)), pl.BlockSpec((CONV_K, LANE), lambda j: (0, j)),
                  pl.BlockSpec((t, LANE), lambda j: (0, _post_col(j))), pl.BlockSpec(memory_space=pl.ANY)],
        out_specs=[pl.BlockSpec((t, LANE), lambda j: (0, j)), pl.BlockSpec((CONV_K, LANE), lambda j: (0, j))],
        out_shape=[jax.ShapeDtypeStruct(dproj.shape, dproj.dtype), jax.ShapeDtypeStruct((CONV_K, QKV_COLS), F32)],
        input_output_aliases={3: 0},
        compiler_params=_params("parallel"), name="gdn_prep_bwd",
    )(proj, conv_w, dpost, dproj)


def _softplus(v):
    return jnp.maximum(v, 0.0) + jnp.log(1.0 + jnp.exp(-jnp.abs(v)))


def _tri_inv(low, nn):
    r = lax.broadcasted_iota(jnp.int32, (CHUNK, CHUNK), 0)
    c = lax.broadcasted_iota(jnp.int32, (CHUNK, CHUNK), 1)
    eye = (r == c).astype(F32)
    same_blk = lax.shift_right_logical(r, 4) == lax.shift_right_logical(c, 4)
    diag = jnp.where(same_blk, low, 0.0)
    off = low - diag
    n1 = -diag
    n2 = nn(n1, n1)
    n4 = nn(n2, n2)
    n8 = nn(n4, n4)
    inv_d = nn(nn(nn(eye + n1, eye + n2), eye + n4), eye + n8)
    m1 = nn(inv_d, off)
    m2 = nn(m1, m1)
    return nn(nn(eye - m1, eye + m2), inv_d)


@jax.custom_vjp
def _tri_inv_known(low, t_inv):
    return t_inv


def _tri_inv_known_fwd(low, t_inv):
    return t_inv, t_inv


def _tri_inv_known_bwd(t_inv, g):
    _, nt, tn = _FDOT_BATCH_PLAIN
    return -nt(tn(t_inv, g), t_inv), jnp.zeros_like(t_inv)


_tri_inv_known.defvjp(_tri_inv_known_fwd, _tri_inv_known_bwd)


LOCAL_HEADS_PER_STEP = 8


def _gdn_local_fn(qkv, ba, alog_row, dtb_row, first_head, bdots, fdots, t_known=None):
    nn, nt, tn = bdots
    fnn = fdots[0]
    n_heads = qkv.shape[1] // (3 * HEAD_DIM)
    part = lambda i, p: qkv[:, (3 * i + p) * HEAD_DIM:(3 * i + p + 1) * HEAD_DIM]
    q = jnp.stack([part(i, 0) for i in range(n_heads)]) * (HEAD_DIM ** -0.5)
    k = jnp.stack([part(i, 1) for i in range(n_heads)])
    v = jnp.stack([part(i, 2) for i in range(n_heads)])
    lane = lax.broadcasted_iota(jnp.int32, ba.shape, 1)
    bg = jnp.where(lane < GDN_HEADS, jax.nn.sigmoid(ba), -jnp.exp(alog_row) * _softplus(ba + dtb_row))
    pick = lambda l: jnp.sum(jnp.where(lane == l, bg, 0.0), axis=1, keepdims=True)
    beta = jnp.stack([pick(first_head + i) for i in range(n_heads)])
    g = jnp.stack([pick(first_head + i + GDN_HEADS) for i in range(n_heads)])

    r = lax.broadcasted_iota(jnp.int32, (CHUNK, CHUNK), 0)
    c = lax.broadcasted_iota(jnp.int32, (CHUNK, CHUNK), 1)
    incl = r >= c
    strict = r > c
    eye = r == c

    def to_row(col):
        return jnp.sum(jnp.where(eye, col, 0.0), axis=1, keepdims=True)

    gc = jnp.sum(jnp.where(incl, to_row(g), 0.0), axis=2, keepdims=True)
    diff = gc - to_row(gc)
    decay = jnp.where(incl, jnp.exp(jnp.where(incl, diff, 0.0)), 0.0)
    k_beta = k * beta
    v_beta = v * beta
    low = jnp.where(strict, nt(k_beta, k) * decay, 0.0)
    t_inv = _tri_inv(low, fnn) if t_known is None else _tri_inv_known(low, t_known)
    eg = jnp.exp(gc)
    u = fnn(t_inv, v_beta)
    w = fnn(t_inv, k_beta * eg)
    attn = jnp.where(incl, nt(q, k) * decay, 0.0)
    last = lax.broadcasted_iota(jnp.int32, (CHUNK, 1), 0) == CHUNK - 1
    g_last = jnp.sum(jnp.where(last, gc, 0.0), axis=1, keepdims=True)
    kdec = k * jnp.exp(g_last - gc)
    elast = jnp.broadcast_to(jnp.exp(g_last), (n_heads, 1, LANE))
    return u, w, q * eg, kdec, attn, elast, t_inv


def _gdn_state_fn(u, w, qg, kdec, attn, elast, state, bdots):
    nn, _, tn = bdots
    v_new = u - nn(w, state)
    o = nn(qg, state) + nn(attn, v_new)
    return o, state * elast + tn(kdec, v_new)


def _gdn_local_fwd(post, proj, alog_row, dtb_row):
    t = post.shape[0]
    n_chunks = t // CHUNK
    hb = LOCAL_HEADS_PER_STEP

    def body(qkv_ref, ba_ref, al_ref, dt_ref, u_ref, w_ref, qg_ref, kd_ref, at_ref, el_ref, ti_ref):
        u, w, qg, kdec, attn, elast, t_inv = _gdn_local_fn(qkv_ref[...], ba_ref[...], al_ref[...], dt_ref[...],
                                                           pl.program_id(1) * hb, _BDOT_BATCH_PLAIN, _FDOT_BATCH_PLAIN)
        for i in range(hb):
            cols = slice(i * HEAD_DIM, (i + 1) * HEAD_DIM)
            u_ref[:, cols] = u[i]
            w_ref[:, cols] = w[i].astype(BF16)
            qg_ref[:, cols] = qg[i].astype(BF16)
            kd_ref[:, cols] = kdec[i].astype(BF16)
        at_ref[...] = attn.astype(BF16)
        el_ref[:, 0] = elast
        ti_ref[...] = t_inv

    wide = pl.BlockSpec((CHUNK, hb * HEAD_DIM), lambda n, j: (n, j))
    square = pl.BlockSpec((hb, CHUNK, CHUNK), lambda n, j: (j, n, 0))
    row = pl.BlockSpec((1, LANE), lambda n, j: (0, 0))
    res = pl.pallas_call(
        body, grid=(n_chunks, GDN_HEADS // hb),
        in_specs=[pl.BlockSpec((CHUNK, hb * 3 * HEAD_DIM), lambda n, j: (n, j)),
                  pl.BlockSpec((CHUNK, LANE), lambda n, j: (n, BA_BLK)), row, row],
        out_specs=[wide, wide, wide, wide, square, pl.BlockSpec((hb, 1, 1, LANE), lambda n, j: (j, n, 0, 0)), square],
        out_shape=[jax.ShapeDtypeStruct((t, GDN_WIDTH), F32), jax.ShapeDtypeStruct((t, GDN_WIDTH), BF16),
                   jax.ShapeDtypeStruct((t, GDN_WIDTH), BF16), jax.ShapeDtypeStruct((t, GDN_WIDTH), BF16),
                   jax.ShapeDtypeStruct((GDN_HEADS, t, CHUNK), BF16),
                   jax.ShapeDtypeStruct((GDN_HEADS, n_chunks, 1, LANE), F32),
                   jax.ShapeDtypeStruct((GDN_HEADS, t, CHUNK), F32)],
        compiler_params=_params("parallel", "parallel"), name="gdn_local_fwd",
    )(post, proj, alog_row, dtb_row)
    return tuple(res[:6]), res[6]


def _by_head(ref):
    return jnp.stack([ref[:, h * HEAD_DIM:(h + 1) * HEAD_DIM] for h in range(ref.shape[1] // HEAD_DIM)])


def _gdn_state_specs(n_of):
    wide = pl.BlockSpec((CHUNK, GDN_WIDTH), lambda n: (n_of(n), 0))
    attn = pl.BlockSpec((GDN_HEADS, CHUNK, CHUNK), lambda n: (0, n_of(n), 0))
    elast = pl.BlockSpec((GDN_HEADS, 1, 1, LANE), lambda n: (0, n_of(n), 0, 0))
    saved = pl.BlockSpec((GDN_HEADS, 1, HEAD_DIM, HEAD_DIM), lambda n: (0, n_of(n), 0, 0))
    return wide, attn, elast, saved


def _gdn_state_fwd(u, w, qg, kdec, attn, elast):
    t = u.shape[0]
    n_chunks = t // CHUNK

    def body(u_ref, w_ref, qg_ref, kd_ref, at_ref, el_ref, o_ref, save_ref, state_ref):
        @pl.when(pl.program_id(0) == 0)
        def _():
            state_ref[...] = jnp.zeros_like(state_ref)

        state = state_ref[...]
        save_ref[:, 0] = state
        o, new_state = _gdn_state_fn(_by_head(u_ref), _by_head(w_ref), _by_head(qg_ref), _by_head(kd_ref), at_ref[...],
                                     el_ref[:, 0], state, _BDOT_BATCH_PLAIN)
        for h in range(GDN_HEADS):
            o_ref[:, h * HEAD_DIM:(h + 1) * HEAD_DIM] = o[h]
        state_ref[...] = new_state

    wide, attn_spec, elast_spec, saved_spec = _gdn_state_specs(lambda n: n)
    return pl.pallas_call(
        body, grid=(n_chunks,), in_specs=[wide, wide, wide, wide, attn_spec, elast_spec],
        out_specs=[wide, saved_spec],
        out_shape=[jax.ShapeDtypeStruct((t, GDN_WIDTH), F32),
                   jax.ShapeDtypeStruct((GDN_HEADS, n_chunks, HEAD_DIM, HEAD_DIM), F32)],
        scratch_shapes=[pltpu.VMEM((GDN_HEADS, HEAD_DIM, HEAD_DIM), F32)],
        compiler_params=_params("arbitrary"), name="gdn_state_fwd",
    )(u, w, qg, kdec, attn, elast)


def _gdn_state_bwd(u, w, qg, kdec, attn, elast, saved, do):
    t = u.shape[0]
    n_chunks = t // CHUNK
    last = n_chunks - 1

    def body(u_ref, w_ref, qg_ref, kd_ref, at_ref, el_ref, save_ref, do_ref,
             du_ref, dw_ref, dqg_ref, dkd_ref, dat_ref, del_ref, dstate_ref):
        @pl.when(pl.program_id(0) == 0)
        def _():
            dstate_ref[...] = jnp.zeros_like(dstate_ref)

        _, vjp = jax.vjp(
            lambda *a: _gdn_state_fn(*a, _BDOT_BATCH_VJP), _by_head(u_ref), _by_head(w_ref).astype(F32),
            _by_head(qg_ref).astype(F32), _by_head(kd_ref).astype(F32), at_ref[...].astype(F32), el_ref[:, 0],
            save_ref[:, 0])
        du, dw, dqg, dkd, dat, de, dstate = vjp((_by_head(do_ref), dstate_ref[...]))
        for h in range(GDN_HEADS):
            cols = slice(h * HEAD_DIM, (h + 1) * HEAD_DIM)
            du_ref[:, cols] = du[h]
            dw_ref[:, cols] = dw[h]
            dqg_ref[:, cols] = dqg[h]
            dkd_ref[:, cols] = dkd[h]
        dat_ref[...] = dat
        del_ref[:, 0] = de
        dstate_ref[...] = dstate

    wide, attn_spec, elast_spec, saved_spec = _gdn_state_specs(lambda n: last - n)
    wide_f32 = jax.ShapeDtypeStruct((t, GDN_WIDTH), F32)
    return pl.pallas_call(
        body, grid=(n_chunks,), in_specs=[wide, wide, wide, wide, attn_spec, elast_spec, saved_spec, wide],
        out_specs=[wide, wide, wide, wide, attn_spec, elast_spec],
        out_shape=[wide_f32, wide_f32, wide_f32, wide_f32, jax.ShapeDtypeStruct((GDN_HEADS, t, CHUNK), F32),
                   jax.ShapeDtypeStruct((GDN_HEADS, n_chunks, 1, LANE), F32)],
        scratch_shapes=[pltpu.VMEM((GDN_HEADS, HEAD_DIM, HEAD_DIM), F32)],
        compiler_params=_params("arbitrary"), name="gdn_state_bwd",
    )(u, w, qg, kdec, attn, elast, saved, do)


def _gdn_local_bwd(post, proj, alog_row, dtb_row, t_inv, cots, dproj):
    t = post.shape[0]
    n_chunks = t // CHUNK
    hb = LOCAL_HEADS_PER_STEP
    n_steps = GDN_HEADS // hb

    def body(qkv_ref, ba_ref, al_ref, dt_ref, ti_ref, du_ref, dw_ref, dqg_ref, dkd_ref, dat_ref, del_ref, _,
             dqkv_ref, dba_ref, dal_ref, ddt_ref, dba_acc):
        n = pl.program_id(0)
        j = pl.program_id(1)

        @pl.when((n == 0) & (j == 0))
        def _():
            dal_ref[...] = jnp.zeros_like(dal_ref)
            ddt_ref[...] = jnp.zeros_like(ddt_ref)

        @pl.when(j == 0)
        def _():
            dba_acc[...] = jnp.zeros_like(dba_acc)

        t_known = ti_ref[...]
        _, vjp = jax.vjp(
            lambda a, b, c, d: _gdn_local_fn(a, b, c, d, j * hb, _BDOT_BATCH_VJP, _FDOT_BATCH_VJP, t_known)[:6],
            qkv_ref[...], ba_ref[...], al_ref[...], dt_ref[...])
        dqkv, dba, dal, ddt = vjp((_by_head(du_ref), _by_head(dw_ref), _by_head(dqg_ref), _by_head(dkd_ref), dat_ref[...],
                                   del_ref[:, 0]))
        dqkv_ref[...] = dqkv
        dba_acc[...] += dba
        dal_ref[...] += dal
        ddt_ref[...] += ddt

        @pl.when(j == n_steps - 1)
        def _():
            dba_ref[:, 0:LANE] = dba_acc[...].astype(dba_ref.dtype)
            dba_ref[:, LANE:2 * LANE] = jnp.zeros((CHUNK, LANE), dba_ref.dtype)

    wide = pl.BlockSpec((CHUNK, hb * HEAD_DIM), lambda n, j: (n, j))
    qkv_spec = pl.BlockSpec((CHUNK, hb * 3 * HEAD_DIM), lambda n, j: (n, j))
    row = pl.BlockSpec((1, LANE), lambda n, j: (0, 0))
    return pl.pallas_call(
        body, grid=(n_chunks, n_steps),
        in_specs=[qkv_spec, pl.BlockSpec((CHUNK, LANE), lambda n, j: (n, BA_BLK)), row, row,
                  pl.BlockSpec((hb, CHUNK, CHUNK), lambda n, j: (j, n, 0)), wide, wide, wide, wide,
                  pl.BlockSpec((hb, CHUNK, CHUNK), lambda n, j: (j, n, 0)),
                  pl.BlockSpec((hb, 1, 1, LANE), lambda n, j: (j, n, 0, 0)), pl.BlockSpec(memory_space=pl.ANY)],
        out_specs=[qkv_spec, pl.BlockSpec((CHUNK, 2 * LANE), lambda n, j: (n, BA_BLK // 2)), row, row],
        out_shape=[jax.ShapeDtypeStruct((t, QKV_COLS), F32), jax.ShapeDtypeStruct(dproj.shape, dproj.dtype),
                   jax.ShapeDtypeStruct((1, LANE), F32), jax.ShapeDtypeStruct((1, LANE), F32)],
        input_output_aliases={11: 1},
        scratch_shapes=[pltpu.VMEM((CHUNK, LANE), F32)],
        compiler_params=_params("arbitrary", "arbitrary"), name="gdn_local_bwd",
    )(post, proj, alog_row, dtb_row, t_inv, *cots, dproj)


def _onorm_fn(o, z, w):
    return o * lax.rsqrt(jnp.mean(o * o, axis=1, keepdims=True) + NORM_EPS) * w * (z * jax.nn.sigmoid(z))


def _onorm_fwd(o_raw, proj, norm_w, mixin, tm=512):
    t = o_raw.shape[0]
    tm = min(tm, t)

    def body(o_ref, z_ref, w_ref, _, out_ref):
        out_ref[...] = _onorm_fn(o_ref[...], z_ref[...], w_ref[...]).astype(out_ref.dtype)

    return pl.pallas_call(
        body, grid=(t // tm, GDN_HEADS),
        in_specs=[pl.BlockSpec((tm, LANE), lambda i, h: (i, h)), pl.BlockSpec((tm, LANE), lambda i, h: (i, Z_BLK + h)),
                  pl.BlockSpec((1, LANE), lambda i, h: (0, 0)), pl.BlockSpec(memory_space=pl.ANY)],
        out_specs=pl.BlockSpec((tm, LANE), lambda i, h: (i, h)),
        out_shape=jax.ShapeDtypeStruct(mixin.shape, mixin.dtype), input_output_aliases={3: 0},
        compiler_params=_params("parallel", "parallel"), name="gdn_onorm_fwd",
    )(o_raw, proj, norm_w, mixin)


def _onorm_bwd(o_raw, proj, norm_w, dmixin, dproj, tm=512):
    t = o_raw.shape[0]
    tm = min(tm, t)

    def body(o_ref, z_ref, w_ref, d_ref, _, do_ref, dz_ref, dw_ref):
        @pl.when((pl.program_id(0) == 0) & (pl.program_id(1) == 0))
        def _():
            dw_ref[...] = jnp.zeros_like(dw_ref)

        _, vjp = jax.vjp(_onorm_fn, o_ref[...], z_ref[...], w_ref[...])
        do, dz, dw = vjp(d_ref[...])
        do_ref[...] = do
        dz_ref[...] = dz.astype(dz_ref.dtype)
        dw_ref[...] += dw

    return pl.pallas_call(
        body, grid=(t // tm, GDN_HEADS),
        in_specs=[pl.BlockSpec((tm, LANE), lambda i, h: (i, h)), pl.BlockSpec((tm, LANE), lambda i, h: (i, Z_BLK + h)),
                  pl.BlockSpec((1, LANE), lambda i, h: (0, 0)), pl.BlockSpec((tm, LANE), lambda i, h: (i, h)),
                  pl.BlockSpec(memory_space=pl.ANY)],
        out_specs=[pl.BlockSpec((tm, LANE), lambda i, h: (i, h)), pl.BlockSpec((tm, LANE), lambda i, h: (i, Z_BLK + h)),
                   pl.BlockSpec((1, LANE), lambda i, h: (0, 0))],
        out_shape=[jax.ShapeDtypeStruct((t, GDN_WIDTH), F32), jax.ShapeDtypeStruct(dproj.shape, dproj.dtype),
                   jax.ShapeDtypeStruct((1, LANE), F32)],
        input_output_aliases={4: 1},
        compiler_params=_params("arbitrary", "arbitrary"), name="gdn_onorm_bwd",
    )(o_raw, proj, norm_w, dmixin, dproj)


def _pool_select(levels, gi):
    out = levels[-1]
    for lvl in range(len(levels) - 2, -1, -1):
        out = jnp.where(gi == lvl, levels[lvl], out)
    return out


def _pool_count(shape, gi):
    pos = lax.broadcasted_iota(jnp.int32, shape, 0)
    win = lax.shift_left(jnp.int32(2), gi)
    return jnp.minimum(pos + 1, win).astype(F32)


def _pooled(p, gi):
    acc = p
    levels = []
    for lvl in range(POOL_GROUPS):
        acc = acc + _shift_down(acc, 1 << lvl)
        levels.append(acc)
    return _pool_select(levels, gi) / _pool_count(p.shape, gi) - p


def _pool_fwd(proj, pool_w, pool_scale):
    t = proj.shape[0]

    def body(p_ref, w_ref, s_ref, out_ref):
        gi = pl.program_id(0)
        pooled = _pooled(p_ref[...], gi)
        out_ref[...] = (_BDOT_PLAIN[0](pooled, w_ref[0]) * s_ref[0]).astype(out_ref.dtype)

    return pl.pallas_call(
        body, grid=(POOL_GROUPS,),
        in_specs=[pl.BlockSpec((t, POOL_GROUP_DIM), lambda g: (0, POOL_BLK + g)),
                  pl.BlockSpec((1, POOL_GROUP_DIM, POOL_GROUP_DIM), lambda g: (g, 0, 0)),
                  pl.BlockSpec((1, 1, POOL_GROUP_DIM), lambda g: (g, 0, 0))],
        out_specs=pl.BlockSpec((t, POOL_GROUP_DIM), lambda g: (0, GDN_WIDTH // POOL_GROUP_DIM + g)),
        out_shape=jax.ShapeDtypeStruct((t, 2 * GDN_WIDTH), BF16),
        compiler_params=_params("parallel"), name="pool_fwd",
    )(proj, pool_w, pool_scale)


def _pool_bwd(proj, pool_w, pool_scale, dmixin):
    t = proj.shape[0]
    nn, nt, tn = _BDOT_PLAIN

    def body(p_ref, w_ref, s_ref, d_ref, dp_ref, dw_ref, ds_ref):
        gi = pl.program_id(0)
        p = p_ref[...]
        pooled = _pooled(p, gi)
        mixed = nn(pooled, w_ref[0])
        d = d_ref[...]
        ds_ref[0] = jnp.sum(d * mixed, axis=0, keepdims=True)
        dmixed = d * s_ref[0]
        dw_ref[0] = tn(pooled, dmixed)
        dpooled = nt(dmixed, w_ref[0])
        acc = dpooled / _pool_count(p.shape, gi)
        levels = []
        for lvl in range(POOL_GROUPS):
            acc = acc + _shift_up(acc, 1 << lvl)
            levels.append(acc)
        dp_ref[...] = (_pool_select(levels, gi) - dpooled).astype(dp_ref.dtype)

    return pl.pallas_call(
        body, grid=(POOL_GROUPS,),
        in_specs=[pl.BlockSpec((t, POOL_GROUP_DIM), lambda g: (0, POOL_BLK + g)),
                  pl.BlockSpec((1, POOL_GROUP_DIM, POOL_GROUP_DIM), lambda g: (g, 0, 0)),
                  pl.BlockSpec((1, 1, POOL_GROUP_DIM), lambda g: (g, 0, 0)),
                  pl.BlockSpec((t, POOL_GROUP_DIM), lambda g: (0, GDN_WIDTH // POOL_GROUP_DIM + g))],
        out_specs=[pl.BlockSpec((t, POOL_GROUP_DIM), lambda g: (0, POOL_BLK + g)),
                   pl.BlockSpec((1, POOL_GROUP_DIM, POOL_GROUP_DIM), lambda g: (g, 0, 0)),
                   pl.BlockSpec((1, 1, POOL_GROUP_DIM), lambda g: (g, 0, 0))],
        out_shape=[jax.ShapeDtypeStruct((t, PROJ_COLS), BF16),
                   jax.ShapeDtypeStruct((POOL_GROUPS, POOL_GROUP_DIM, POOL_GROUP_DIM), F32),
                   jax.ShapeDtypeStruct((POOL_GROUPS, 1, POOL_GROUP_DIM), F32)],
        compiler_params=_params("parallel"), name="pool_bwd",
    )(proj, pool_w, pool_scale, dmixin)


def _ln_stats(s):
    mu = jnp.mean(s, axis=1, keepdims=True)
    xc = s - mu
    var = jnp.mean(xc * xc, axis=1, keepdims=True)
    rstd = lax.rsqrt(var + LN_EPS)
    return xc * rstd, rstd


def _ln_fwd(h_in, y, g, b, *, name, tm=256):
    t, d = h_in.shape
    tm = min(tm, t)

    def body(h_ref, y_ref, g_ref, b_ref, o_ref, o16_ref):
        xhat, _ = _ln_stats(ALPHA * h_ref[...] + y_ref[...])
        out = xhat * g_ref[...] + b_ref[...]
        o_ref[...] = out
        o16_ref[...] = out.astype(BF16)

    row = pl.BlockSpec((tm, d), lambda i: (i, 0))
    vec = pl.BlockSpec((1, d), lambda i: (0, 0))
    return pl.pallas_call(
        body, grid=(t // tm,), in_specs=[row, row, vec, vec], out_specs=[row, row],
        out_shape=[jax.ShapeDtypeStruct((t, d), F32), jax.ShapeDtypeStruct((t, d), BF16)],
        compiler_params=_params("parallel"), name=name,
    )(h_in, y, g, b)


def _ln_backward(xhat, rstd, dout, gain):
    dxhat = dout * gain
    m1 = jnp.mean(dxhat, axis=1, keepdims=True)
    m2 = jnp.mean(dxhat * xhat, axis=1, keepdims=True)
    return (rstd * (dxhat - m1 - xhat * m2), jnp.sum(dout * xhat, axis=0, keepdims=True),
            jnp.sum(dout, axis=0, keepdims=True))


def _ln_loss(h_in, y, g, b, target, *, name, tm=256):
    t, d = h_in.shape
    tm = min(tm, t)

    def body(h_ref, y_ref, g_ref, b_ref, t_ref, sq_ref, ds_ref, ds16_ref, dg_ref, dbias_ref):
        @pl.when(pl.program_id(0) == 0)
        def _():
            sq_ref[...] = jnp.zeros_like(sq_ref)
            dg_ref[...] = jnp.zeros_like(dg_ref)
            dbias_ref[...] = jnp.zeros_like(dbias_ref)

        xhat, rstd = _ln_stats(ALPHA * h_ref[...] + y_ref[...])
        err = xhat * g_ref[...] + b_ref[...] - t_ref[...]
        sq_ref[...] += jnp.sum(jnp.sum(err * err, axis=1, keepdims=True), axis=0, keepdims=True)
        ds, dg, dbias = _ln_backward(xhat, rstd, err * (1.0 / d), g_ref[...])
        ds_ref[...] = ds
        ds16_ref[...] = ds.astype(BF16)
        dg_ref[...] += dg
        dbias_ref[...] += dbias

    row = pl.BlockSpec((tm, d), lambda i: (i, 0))
    vec = pl.BlockSpec((1, d), lambda i: (0, 0))
    return pl.pallas_call(
        body, grid=(t // tm,), in_specs=[row, row, vec, vec, row],
        out_specs=[pl.BlockSpec((1, LANE), lambda i: (0, 0)), row, row, vec, vec],
        out_shape=[jax.ShapeDtypeStruct((1, LANE), F32), jax.ShapeDtypeStruct((t, d), F32),
                   jax.ShapeDtypeStruct((t, d), BF16), jax.ShapeDtypeStruct((1, d), F32), jax.ShapeDtypeStruct((1, d), F32)],
        compiler_params=_params("arbitrary"), name=name,
    )(h_in, y, g, b, target)


def _ln_bwd(h_in, y, g, d_a, d_b, *, name, tm=256):
    t, d = h_in.shape
    tm = min(tm, t)
    has_b = d_b is not None

    def body(*refs):
        if has_b:
            h_ref, y_ref, g_ref, da_ref, db_ref, ds_ref, ds16_ref, dg_ref, dbias_ref = refs
        else:
            h_ref, y_ref, g_ref, da_ref, ds_ref, ds16_ref, dg_ref, dbias_ref = refs

        @pl.when(pl.program_id(0) == 0)
        def _():
            dg_ref[...] = jnp.zeros_like(dg_ref)
            dbias_ref[...] = jnp.zeros_like(dbias_ref)

        xhat, rstd = _ln_stats(ALPHA * h_ref[...] + y_ref[...])
        dout = da_ref[...]
        if has_b:
            dout = dout + ALPHA * db_ref[...]
        ds, dg, dbias = _ln_backward(xhat, rstd, dout, g_ref[...])
        ds_ref[...] = ds
        ds16_ref[...] = ds.astype(BF16)
        dg_ref[...] += dg
        dbias_ref[...] += dbias

    row = pl.BlockSpec((tm, d), lambda i: (i, 0))
    vec = pl.BlockSpec((1, d), lambda i: (0, 0))
    args = [h_in, y, g, d_a] + ([d_b] if has_b else [])
    return pl.pallas_call(
        body, grid=(t // tm,), in_specs=[row, row, vec, row] + ([row] if has_b else []),
        out_specs=[row, row, vec, vec],
        out_shape=[jax.ShapeDtypeStruct((t, d), F32), jax.ShapeDtypeStruct((t, d), BF16),
                   jax.ShapeDtypeStruct((1, d), F32), jax.ShapeDtypeStruct((1, d), F32)],
        compiler_params=_params("arbitrary"), name=name,
    )(*args)


def _attn_fn(q, k, v, dots):
    nn, nt, _ = dots
    s = nt(q, k) * (XATTN_HEAD_DIM ** -0.5)
    s = s - lax.stop_gradient(jnp.max(s, axis=1, keepdims=True))
    e = jnp.exp(s)
    p = e / jnp.sum(e, axis=1, keepdims=True)
    return nn(p, v)


def _attn_fwd(q, k, v, tq=512):
    t = q.shape[0]
    tq = min(tq, t)

    def body(q_ref, k_ref, v_ref, o_ref):
        o_ref[...] = _attn_fn(q_ref[...], k_ref[...], v_ref[...], _BDOT_PLAIN).astype(BF16)

    qs = pl.BlockSpec((tq, XATTN_HEAD_DIM), lambda h, i: (i, h))
    ks = pl.BlockSpec((MEM_LEN, XATTN_HEAD_DIM), lambda h, i: (0, h))
    return pl.pallas_call(
        body, grid=(XATTN_HEADS, t // tq), in_specs=[qs, ks, ks], out_specs=qs,
        out_shape=jax.ShapeDtypeStruct(q.shape, BF16), compiler_params=_params("parallel", "parallel"), name="xattn_fwd",
    )(q, k, v)


def _attn_bwd(q, k, v, do, tq=512):
    t = q.shape[0]
    tq = min(tq, t)

    def body(q_ref, k_ref, v_ref, do_ref, dq_ref, dk_ref, dv_ref):
        @pl.when(pl.program_id(1) == 0)
        def _():
            dk_ref[...] = jnp.zeros_like(dk_ref)
            dv_ref[...] = jnp.zeros_like(dv_ref)

        _, vjp = jax.vjp(lambda a, b, c: _attn_fn(a, b, c, _BDOT_VJP), q_ref[...].astype(F32), k_ref[...].astype(F32),
                         v_ref[...].astype(F32))
        dq, dk, dv = vjp(do_ref[...].astype(F32))
        dq_ref[...] = dq.astype(BF16)
        dk_ref[...] += dk
        dv_ref[...] += dv

    qs = pl.BlockSpec((tq, XATTN_HEAD_DIM), lambda h, i: (i, h))
    ks = pl.BlockSpec((MEM_LEN, XATTN_HEAD_DIM), lambda h, i: (0, h))
    return pl.pallas_call(
        body, grid=(XATTN_HEADS, t // tq), in_specs=[qs, ks, ks, qs], out_specs=[qs, ks, ks],
        out_shape=[jax.ShapeDtypeStruct(q.shape, BF16), jax.ShapeDtypeStruct(k.shape, F32), jax.ShapeDtypeStruct(v.shape, F32)],
        compiler_params=_params("parallel", "arbitrary"), name="xattn_bwd",
    )(q, k, v, do)


def _local_step(x, x16, mem, target, weights_of, grads_ready):
    def behind(vec, token):
        return vec if token is None else vec + token

    w = dict(weights_of("mixer", None))
    proj = _mm(x16, w["w_in"], tb=True, tn=768, name="mm_in_proj")
    mixin = _pool_fwd(proj, w["pool_w"], w["pool_scale"])
    post = _gdn_prep_fwd(proj, w["conv_w"])
    chunked, t_inv = _gdn_local_fwd(post, proj, w["alog_row"], w["dtb_row"])
    o_raw, saved = _gdn_state_fwd(*chunked)
    token = weights_of("ahead", o_raw)
    mixin = _onorm_fwd(o_raw, proj, behind(w["gdn_norm_w"], token), mixin)
    w.update(weights_of("attn", mixin))
    mix = _mm(mixin, w["w_out"], name="mm_out_proj")
    h1, h1_16 = _ln_fwd(x, mix, w["ln1_g"], w["ln1_b"], name="ln1_fwd")
    xq = _mm(h1_16, w["xq_w"], out_dtype=BF16, name="mm_xq")
    xk = _mm(mem, w["xk_w"], out_dtype=BF16, name="mm_xk")
    xv = _mm(mem, w["xv_w"], out_dtype=BF16, name="mm_xv")
    xo = _attn_fwd(xq, xk, xv)
    xa = _mm(xo, w["xo_w"], name="mm_xo")
    h2, h2_16 = _ln_fwd(h1, xa, w["ln2_g"], w["ln2_b"], name="ln2_fwd")
    w.update(weights_of("up", h2_16))
    act, relu = _mm(h2_16, w["w_up"], b_chunks=True, epi="relu2", name="mm_up")
    w.update(weights_of("down", act))
    ff = _mm(act, w["w_down"], tn=512, tk=2048, name="mm_down")
    g = {}
    sq, ds3, ds3_16, g["ln3_g"], g["ln3_b"] = _ln_loss(h2, ff, w["ln3_g"], w["ln3_b"], target, name="ln3_loss")

    gw_down = _mm(act, ds3_16, ta=True, out_dtype=BF16, tm=512, tn=D_MODEL, name="mm_gw_down")
    du = _mm(ds3_16, w["w_down"], tb=True, epi="mul2r", extra=relu, name="mm_du")
    gw_up = _mm(h2_16, du, ta=True, out_dtype=BF16, o_chunks=True, name="mm_gw_up")
    token = grads_ready("mlp", {"w_down": gw_down, "w_up": gw_up})
    dh2 = _mm(du, w["w_up"], tb=True, b_chunks=True, tn=1024, tk=1024, name="mm_dh2")
    ds2, ds2_16, g["ln2_g"], g["ln2_b"] = _ln_bwd(h1, xa, behind(w["ln2_g"], token), dh2, ds3, name="ln2_bwd")
    gw_xo = _mm(xo, ds2_16, ta=True, out_dtype=BF16, name="mm_gw_xo")
    dxo = _mm(ds2_16, w["xo_w"], tb=True, out_dtype=BF16, name="mm_dxo")
    dxq, dxk, dxv = _attn_bwd(xq, xk, xv, dxo)
    gw_xq = _mm(h1_16, dxq, ta=True, out_dtype=BF16, name="mm_gw_xq")
    gw_xk = _mm(mem, dxk, ta=True, out_dtype=BF16, name="mm_gw_xk")
    gw_xv = _mm(mem, dxv, ta=True, out_dtype=BF16, name="mm_gw_xv")
    token = grads_ready("attn", {"xo_w": gw_xo, "xq_w": gw_xq, "xk_w": gw_xk, "xv_w": gw_xv})
    dh1 = _mm(dxq, w["xq_w"], tb=True, name="mm_dh1")
    ds1, ds1_16, g["ln1_g"], g["ln1_b"] = _ln_bwd(x, mix, behind(w["ln1_g"], token), dh1, ds2, name="ln1_bwd")
    gw_out = _mm(mixin, ds1_16, ta=True, out_dtype=BF16, name="mm_gw_out")
    dmixin = _mm(ds1_16, w["w_out"], tb=True, name="mm_dmixin")
    dproj, gw_pool, g["pool_scale"] = _pool_bwd(proj, w["pool_w"], w["pool_scale"], dmixin)
    token = grads_ready("mix", {"w_out": gw_out, "pool_w": gw_pool})
    do_raw, dproj, g["gdn_norm_w"] = _onorm_bwd(o_raw, proj, behind(w["gdn_norm_w"], token), dmixin, dproj)
    cots = _gdn_state_bwd(*chunked, saved, do_raw)
    token = grads_ready("tick", {"after": cots[0]})
    dpost, dproj, g["alog_row"], g["dtb_row"] = _gdn_local_bwd(post, proj, behind(w["alog_row"], token), w["dtb_row"],
                                                               t_inv, cots, dproj)
    dproj, g["conv_w"] = _gdn_prep_bwd(proj, w["conv_w"], dpost, dproj)
    gw_in = _mm(dproj, x16, ta=True, out_dtype=BF16, tm=768, tn=D_MODEL, name="mm_gw_in")
    token = grads_ready("in", {"w_in": gw_in})
    if token is not None:
        ds1, _ = lax.optimization_barrier((ds1, token))
    grad_x = _mm(dproj, w["w_in"], tk=1792, epi="add", extra=ds1, add_scale=ALPHA, name="mm_dx")
    return sq, grad_x, g


_MATRICES = ("w_in", "pool_w", "w_out", "xq_w", "xk_w", "xv_w", "xo_w", "w_up", "w_down")
_VECTORS = ("a_log", "dt_bias", "gdn_norm_w", "pool_scale", "ln1_g", "ln1_b", "ln2_g", "ln2_b", "ln3_g", "ln3_b")
_BA_SPLIT = BA_OFF + 2 * GDN_HEADS


def _lane_row(v, offset):
    return jnp.zeros((1, LANE), F32).at[0, offset:offset + v.shape[0]].set(v)


_GROUP_VECTORS = {"mixer": (), "attn": ("ln1_g", "ln1_b", "ln2_g", "ln2_b"), "up": (), "down": ("ln3_g", "ln3_b")}


def _group_weights(group, full):
    w = {n: full[n].reshape(1, D_MODEL) for n in _GROUP_VECTORS[group]}
    if group == "mixer":
        w.update({
            "w_in": _w_in_padded(full["w_in"]),
            "conv_w": full["conv_w"],
            "alog_row": _lane_row(full["a_log"], GDN_HEADS),
            "dtb_row": _lane_row(full["dt_bias"], GDN_HEADS),
            "gdn_norm_w": full["gdn_norm_w"].reshape(1, LANE),
            "pool_w": full["pool_w"],
            "pool_scale": full["pool_scale"].reshape(POOL_GROUPS, 1, POOL_GROUP_DIM),
        })
    else:
        w.update({n: full[n] for n in dict(_GATHER_GROUPS)[group]})
    return w


def _w_in_row_map():
    per = IN_COLS // N_DEV
    gap = POOL_OFF - _BA_SPLIT
    pieces = []
    for d in range(N_DEV):
        lo, hi = d * per, (d + 1) * per
        if hi <= _BA_SPLIT:
            pieces.append([(0, lo, per)])
        elif lo >= _BA_SPLIT:
            pieces.append([(0, lo + gap, per)])
        else:
            pieces.append([(0, lo, _BA_SPLIT - lo), (_BA_SPLIT - lo, POOL_OFF, hi - _BA_SPLIT)])
    return pieces


_W_IN_LANES = 256


def _w_in_padded(blocks):
    def body(b_ref, o_ref):
        for d, pieces in enumerate(_w_in_row_map()):
            for src, dst, rows in pieces:
                o_ref[dst:dst + rows, :] = b_ref[d, src:src + rows, :]
        o_ref[_BA_SPLIT:POOL_OFF, :] = jnp.zeros((POOL_OFF - _BA_SPLIT, _W_IN_LANES), o_ref.dtype)

    n, per, cols = blocks.shape
    return pl.pallas_call(
        body, grid=(cols // _W_IN_LANES,), in_specs=[pl.BlockSpec((n, per, _W_IN_LANES), lambda j: (0, 0, j))],
        out_specs=pl.BlockSpec((PROJ_COLS, _W_IN_LANES), lambda j: (0, j)),
        out_shape=jax.ShapeDtypeStruct((PROJ_COLS, cols), blocks.dtype), compiler_params=_params("parallel"),
        name="w_in_padded")(blocks)


def _w_in_chunks(g):
    def body(g_ref, o_ref):
        for d, pieces in enumerate(_w_in_row_map()):
            for dst, src, rows in pieces:
                o_ref[d, dst:dst + rows, :] = g_ref[src:src + rows, :]

    cols = g.shape[1]
    per = IN_COLS // N_DEV
    return pl.pallas_call(
        body, grid=(cols // _W_IN_LANES,), in_specs=[pl.BlockSpec((PROJ_COLS, _W_IN_LANES), lambda j: (0, j))],
        out_specs=pl.BlockSpec((N_DEV, per, _W_IN_LANES), lambda j: (0, 0, j)),
        out_shape=jax.ShapeDtypeStruct((N_DEV, per, cols), g.dtype), compiler_params=_params("parallel"),
        name="w_in_chunks")(g)


def _finish_small_grads(g):
    out = {"conv_w": g["conv_w"]}
    out["a_log"] = g["alog_row"][0, GDN_HEADS:2 * GDN_HEADS]
    out["dt_bias"] = g["dtb_row"][0, GDN_HEADS:2 * GDN_HEADS]
    out["gdn_norm_w"] = g["gdn_norm_w"].reshape(LANE)
    out["pool_scale"] = g["pool_scale"].reshape(POOL_GROUPS * POOL_GROUP_DIM)
    for n in ("ln1_g", "ln1_b", "ln2_g", "ln2_b", "ln3_g", "ln3_b"):
        out[n] = g[n].reshape(D_MODEL)
    return out


def _adamw_math(w, g, m, v):
    m = ADAM_B1 * m + (1.0 - ADAM_B1) * g
    v = ADAM_B2 * v + (1.0 - ADAM_B2) * (g * g)
    m_hat = m / (1.0 - ADAM_B1 ** ADAM_STEP)
    v_hat = v / (1.0 - ADAM_B2 ** ADAM_STEP)
    delta = -ADAM_LR * (m_hat / (jnp.sqrt(v_hat) + ADAM_EPS) + ADAM_WD * w)
    return delta, m, v


ADAMW_TILE_ELEMS = 256 * 1024
CHIP_SUM_TILE_ELEMS = 1024 * 1024


def _shard_tile(r, c, elems):
    for rows in (1024, 512, 256, 128):
        if r % rows == 0 and rows * c <= elems:
            return rows, c
    if r % 128 == 0:
        return 128, c
    return r, 256 if c % 256 == 0 else c


def _adamw_shard(parts, own, me, w, m, v, *, name):
    s, r, c = parts.shape
    tr, tc = _shard_tile(r, c, ADAMW_TILE_ELEMS)
    assert r % tr == 0 and c % tc == 0, (name, r, c)
    unit_axis = w.ndim == 3
    at = (slice(None), 0, slice(None)) if unit_axis else Ellipsis

    def body(me_ref, p_ref, own_ref, w_ref, m_ref, v_ref, g_ref, d_ref, nm_ref, nv_ref):
        mine = own_ref[...].astype(F32)
        g = None
        for i in range(s):
            part = jnp.where(me_ref[0] == i, mine, p_ref[i].astype(F32))
            g = part if g is None else g + part
        delta, nm, nv = _adamw_math(w_ref[at], g, m_ref[at], v_ref[at])
        g_ref[at] = g
        d_ref[at] = delta
        nm_ref[at] = nm
        nv_ref[at] = nv

    if unit_axis:
        blk = pl.BlockSpec((tr, 1, tc), lambda i, j, me_ref: (i, 0, j))
        out = jax.ShapeDtypeStruct((r, 1, c), F32)
    else:
        blk = pl.BlockSpec((tr, tc), lambda i, j, me_ref: (i, j))
        out = jax.ShapeDtypeStruct((r, c), F32)
    return pl.pallas_call(
        body,
        grid_spec=pltpu.PrefetchScalarGridSpec(
            num_scalar_prefetch=1, grid=(r // tr, c // tc),
            in_specs=[pl.BlockSpec((s, tr, tc), lambda i, j, me_ref: (0, i, j)),
                      pl.BlockSpec((None, tr, tc), lambda i, j, me_ref: (me_ref[0], i, j)), blk, blk, blk],
            out_specs=[blk, blk, blk, blk]),
        out_shape=[out, out, out, out], compiler_params=_params("parallel", "parallel"), name=name,
    )(me, parts, own, w, m, v)


N_CHIPS = N_DEV // 2


def _chip_sums(chunks, from_sibling, core, *, name):
    _, r, c = chunks.shape
    tr, tc = _shard_tile(r, c, CHIP_SUM_TILE_ELEMS)
    assert r % tr == 0 and c % tc == 0, (name, r, c)

    def body(core_ref, mine_ref, other_ref, o_ref):
        o_ref[...] = (mine_ref[...].astype(F32) + other_ref[...].astype(F32)).astype(o_ref.dtype)

    by_chip = pl.BlockSpec((None, tr, tc), lambda q, i, j, core_ref: (q, i, j))
    return pl.pallas_call(
        body,
        grid_spec=pltpu.PrefetchScalarGridSpec(
            num_scalar_prefetch=1, grid=(N_CHIPS, r // tr, c // tc),
            in_specs=[pl.BlockSpec((None, tr, tc), lambda q, i, j, core_ref: (2 * q + core_ref[0], i, j)), by_chip],
            out_specs=by_chip),
        out_shape=jax.ShapeDtypeStruct((N_CHIPS, r, c), chunks.dtype),
        compiler_params=_params("parallel", "parallel", "parallel"), name=name,
    )(core, chunks, from_sibling)


def _place():
    return lax.axis_index("x"), lax.axis_index("y"), lax.axis_index("c")


def _slot(px, py, pc):
    return 4 * px + 2 * py + pc


_HBM = pl.BlockSpec(memory_space=pltpu.HBM)


_SEM = pl.BlockSpec(memory_space=pltpu.SEMAPHORE)
_ANY = pl.BlockSpec(memory_space=pl.ANY)
_EFFECT = pltpu.SideEffectType.DATAFLOW_SIDE_EFFECTING
_N_PEERS = N_DEV - 1


def _peer(k, x, y, c):
    return (1 - x if k & 4 else x, 1 - y if k & 2 else y, 1 - c if k & 1 else c)


_EXCHANGE_BITS = {"gather_chips": (1, 2, 4, 6), "gather_pass": (2, 4, 6), "scatter_sibling": (1, 1, 1, 1),
                  "scatter_chips": (2, 4, 6)}


def _exchange_copy(mode, src, land, w, i, place, send_sems, recv_sems, receiving):
    bits = _EXCHANGE_BITS[mode]
    k = bits[i]
    peer = _peer(k, *place)
    me = _slot(*place)
    if mode == "gather_chips":
        to, src_ref, sent_to, got_at = peer, src[w], me, _slot(*peer)
    elif mode == "gather_pass":
        blk = _slot(*peer)
        to, src_ref, sent_to, got_at = _peer(1, *place), land[w].at[blk], blk, _slot(*_peer(k | 1, *place))
    elif mode == "scatter_sibling":
        to, src_ref, sent_to, got_at = peer, src[w].at[2 * i + 1 - place[2]], i, i
    else:
        to, src_ref, sent_to, got_at = peer, src[w].at[_slot(*peer) // 2], me // 2, _slot(*peer) // 2
    sem = w * len(bits) + i
    return pltpu.make_async_remote_copy(
        src_ref=src_ref, dst_ref=land[w].at[got_at if receiving else sent_to], send_sem=send_sems.at[sem],
        recv_sem=recv_sems.at[sem], device_id=to, device_id_type=MESH)


def _exchange_start(mode, srcs, lands, after, *, name):
    ns, nl = len(srcs), len(lands)
    n_sem = nl * len(_EXCHANGE_BITS[mode])

    def body(*refs):
        src, land = refs[:ns], refs[ns:ns + nl]
        send_sems, recv_sems = refs[ns + nl + 1:ns + nl + 3]
        token = refs[-1]
        place = _place()
        for w in range(nl):
            for i in range(len(_EXCHANGE_BITS[mode])):
                _exchange_copy(mode, src, land, w, i, place, send_sems, recv_sems, receiving=False).start()
        token[...] = jnp.zeros_like(token)

    sems = pltpu.SemaphoreType.DMA((n_sem,))
    arrays = list(srcs) + list(lands)
    res = pl.pallas_call(
        body, name=name, in_specs=[_HBM] * (ns + nl) + [_ANY],
        out_specs=(_SEM, _SEM, *([_HBM] * (ns + nl)), pl.BlockSpec(memory_space=pltpu.VMEM)),
        out_shape=(sems, sems, *[pltpu.HBM(a.shape, a.dtype) for a in arrays], jax.ShapeDtypeStruct((8, LANE), F32)),
        input_output_aliases={i: 2 + i for i in range(ns + nl)},
        compiler_params=pltpu.CompilerParams(has_side_effects=_EFFECT),
    )(*[pltpu.with_memory_space_constraint(a, pltpu.HBM) for a in arrays], after)
    return res[0], res[1], list(res[2:2 + ns]), list(res[2 + ns:2 + ns + nl]), res[-1]


def _exchange_wait(mode, started, after, *, name):
    send_sems, recv_sems, srcs, lands, _ = started
    ns, nl = len(srcs), len(lands)

    def body(*refs):
        src, land = refs[:ns], refs[ns:ns + nl]
        send_sems, recv_sems = refs[ns + nl:ns + nl + 2]
        place = _place()
        for w in range(nl):
            for i in range(len(_EXCHANGE_BITS[mode])):
                cp = _exchange_copy(mode, src, land, w, i, place, send_sems, recv_sems, receiving=True)
                cp.wait_send()
                cp.wait_recv()

    arrays = list(srcs) + list(lands)
    res = pl.pallas_call(
        body, name=name, in_specs=[_HBM] * (ns + nl) + [_SEM, _SEM, _ANY], out_specs=[_HBM] * (ns + nl),
        out_shape=[pltpu.HBM(a.shape, a.dtype) for a in arrays],
        input_output_aliases={i: i for i in range(ns + nl)},
        compiler_params=pltpu.CompilerParams(has_side_effects=_EFFECT),
    )(*arrays, send_sems, recv_sems, after)
    return list(res[:ns]), list(res[ns:])


def _small_allreduce_adamw(gvec, wvec, mvec, vvec):
    rows, length = gvec.shape

    def body(g_ref, w_ref, m_ref, v_ref, gs_ref, d_ref, nm_ref, nv_ref, slots, send_sems, recv_sems):
        x, y, c = _place()
        me = _slot(x, y, c)
        slots[me] = g_ref[...]
        sends = []
        for k in range(1, N_DEV):
            peer = _peer(k, x, y, c)
            sends.append(pltpu.make_async_remote_copy(
                src_ref=g_ref, dst_ref=slots.at[me], send_sem=send_sems.at[k - 1], recv_sem=recv_sems.at[k - 1],
                device_id=peer, device_id_type=MESH))
        for cp in sends:
            cp.start()
        for k in range(1, N_DEV):
            peer = _peer(k, x, y, c)
            pltpu.make_async_remote_copy(
                src_ref=g_ref, dst_ref=slots.at[_slot(*peer)], send_sem=send_sems.at[k - 1], recv_sem=recv_sems.at[k - 1],
                device_id=peer, device_id_type=MESH).wait_recv()
        for cp in sends:
            cp.wait_send()
        g = slots[0]
        for s in range(1, N_DEV):
            g = g + slots[s]
        delta, nm, nv = _adamw_math(w_ref[...], g, m_ref[...], v_ref[...])
        gs_ref[...] = g
        d_ref[...] = delta
        nm_ref[...] = nm
        nv_ref[...] = nv

    vmem = pl.BlockSpec(memory_space=pltpu.VMEM)
    out = jax.ShapeDtypeStruct((rows, length), F32)
    return pl.pallas_call(
        body, in_specs=[vmem] * 4, out_specs=[vmem] * 4, out_shape=[out] * 4,
        scratch_shapes=[pltpu.VMEM((N_DEV, rows, length), F32), pltpu.SemaphoreType.DMA((N_DEV - 1,)),
                        pltpu.SemaphoreType.DMA((N_DEV - 1,))],
        name="small_allreduce_adamw",
    )(gvec, wvec, mvec, vvec)


_SMALL_SEGMENTS = (("a_log", GDN_HEADS), ("dt_bias", GDN_HEADS), ("gdn_norm_w", HEAD_DIM), ("pool_scale", GDN_WIDTH),
                   ("ln1_g", D_MODEL), ("ln1_b", D_MODEL), ("ln2_g", D_MODEL), ("ln2_b", D_MODEL),
                   ("ln3_g", D_MODEL), ("ln3_b", D_MODEL), ("conv_w", CONV_K * QKV_COLS))
_SMALL_ROWS = 8
_SMALL_LEN = -(-sum(sz for _, sz in _SMALL_SEGMENTS) // (_SMALL_ROWS * LANE)) * LANE


def _pack_small(vals):
    parts = [vals[n].reshape(-1).astype(F32) if n in vals else jnp.zeros((sz,), F32) for n, sz in _SMALL_SEGMENTS]
    flat = jnp.concatenate(parts)
    flat = jnp.pad(flat, (0, _SMALL_ROWS * _SMALL_LEN - flat.shape[0]))
    return flat.reshape(_SMALL_ROWS, _SMALL_LEN)


def _unpack_small(vec):
    flat = vec.reshape(-1)
    out, off = {}, 0
    for n, sz in _SMALL_SEGMENTS:
        out[n] = flat[off:off + sz]
        off += sz
    return out


_WEIGHT_ORDER = ("w_in", "conv_w", "a_log", "dt_bias", "gdn_norm_w", "pool_w", "pool_scale", "w_out", "ln1_g", "ln1_b",
                 "xq_w", "xk_w", "xv_w", "xo_w", "ln2_g", "ln2_b", "w_up", "w_down", "ln3_g", "ln3_b")


def _shard2d(name, a):
    if name == "w_in":
        return a.T
    return a.reshape(-1, a.shape[-1]) if name == "pool_w" else a


def _update_view(name, a):
    return jnp.transpose(a, (2, 0, 1)) if name == "w_in" else _shard2d(name, a[0])


def _shard_result(name, r, shape):
    return jnp.transpose(r, (1, 2, 0)) if name == "w_in" else r.reshape(shape)


def _gathered_to_full(name, gth):
    if name in ("w_up", "w_in"):
        return gth
    if name == "conv_w":
        return jnp.transpose(gth, (1, 0, 2)).reshape(gth.shape[1], N_DEV * gth.shape[2])
    if name == "pool_w":
        g4 = gth.reshape(N_DEV, POOL_GROUPS, POOL_GROUP_DIM // N_DEV, POOL_GROUP_DIM)
        return jnp.transpose(g4, (1, 0, 2, 3)).reshape(POOL_GROUPS, POOL_GROUP_DIM, POOL_GROUP_DIM)
    return gth.reshape(N_DEV * gth.shape[1], gth.shape[2])


def _full_to_chunks(name, full):
    if name == "w_up":
        return full
    if name == "pool_w":
        g4 = full.reshape(POOL_GROUPS, N_DEV, POOL_GROUP_DIM // N_DEV, POOL_GROUP_DIM)
        return jnp.transpose(g4, (1, 0, 2, 3)).reshape(N_DEV, POOL_GROUPS * POOL_GROUP_DIM // N_DEV, POOL_GROUP_DIM)
    return full.reshape(N_DEV, full.shape[0] // N_DEV, full.shape[1])


_GATHER_GROUPS = (("mixer", ("w_in", "conv_w", "pool_w")), ("attn", ("w_out", "xq_w", "xk_w", "xv_w", "xo_w")),
                  ("up", ("w_up",)), ("down", ("w_down",)))


def _grad_chunks(name, g):
    if name == "w_in":
        return _w_in_chunks(g.astype(BF16))
    return _full_to_chunks(name, g.astype(BF16))


def kernel(x, mem, w_in, conv_w, a_log, dt_bias, gdn_norm_w, pool_w, pool_scale, w_out, ln1_g, ln1_b, xq_w, xk_w, xv_w, xo_w, ln2_g, ln2_b, w_up, w_down, ln3_g, ln3_b, loss_target, m_w_in, m_conv_w, m_a_log, m_dt_bias, m_gdn_norm_w, m_pool_w, m_pool_scale, m_w_out, m_ln1_g, m_ln1_b, m_xq_w, m_xk_w, m_xv_w, m_xo_w, m_ln2_g, m_ln2_b, m_w_up, m_w_down, m_ln3_g, m_ln3_b, v_w_in, v_conv_w, v_a_log, v_dt_bias, v_gdn_norm_w, v_pool_w, v_pool_scale, v_w_out, v_ln1_g, v_ln1_b, v_xq_w, v_xk_w, v_xv_w, v_xo_w, v_ln2_g, v_ln2_b, v_w_up, v_w_down, v_ln3_g, v_ln3_b):
    args = dict(locals())
    wt = {n: args[n][0] for n in _WEIGHT_ORDER}
    mo = {n: args["m_" + n][0] for n in _WEIGHT_ORDER}
    vo = {n: args["v_" + n][0] for n in _WEIGHT_ORDER}

    me = _slot(*_place())
    me_arr = jnp.reshape(me, (1,)).astype(jnp.int32)
    nothing = jnp.zeros((8, LANE), F32)

    def landing_zones(names):
        shards = [_shard2d(n, wt[n]).astype(F32 if n == "conv_w" else BF16) for n in names]
        zones = [lax.dynamic_update_slice(lax.empty((N_DEV, *s.shape), s.dtype), s[None], (me, 0, 0)) for s in shards]
        return shards, zones

    chip_arr = jnp.reshape(me // 2, (1,)).astype(jnp.int32)
    core_arr = jnp.reshape(lax.axis_index("c"), (1,)).astype(jnp.int32)
    names_of = dict(_GATHER_GROUPS)
    gathers = {}
    prepared = {}

    def gather_chips(group, after):
        shards, zones = prepared.pop(group) if group in prepared else landing_zones(names_of[group])
        gathers[group] = _exchange_start("gather_chips", shards, zones, after, name="gather_chips_" + group)
        return gathers[group][4]

    def gather_pass(group, after):
        _, zones = _exchange_wait("gather_chips", gathers[group], after, name=f"gather_chips_{group}_wait")
        gathers[group] = _exchange_start("gather_pass", [], zones, nothing, name="gather_pass_" + group)
        return gathers[group][4]

    def gathered(group, after, token=None):
        _, zones = _exchange_wait("gather_pass", gathers[group], after, name=f"gather_pass_{group}_wait")
        full = {n: _gathered_to_full(n, z) for n, z in zip(names_of[group], zones)}
        full.update({n: wt[n] if token is None else wt[n] + token for n in _VECTORS})
        return _group_weights(group, full)

    token = gather_chips("mixer", nothing)
    x16 = _cast_bf16(x[0], name="cast_x")
    later = {group: landing_zones(names_of[group]) for group in ("attn", "up", "down")}
    token, x16, later = lax.optimization_barrier((token, x16, later))
    prepared.update(later)
    token = gather_chips("attn", gather_pass("mixer", token))

    def weights_of(group, after):
        if group == "mixer":
            return gathered(group, gathers["attn"][4])
        if group == "ahead":
            return gather_chips("up", gather_pass("attn", after))[0:1, 0:1]
        if group == "attn":
            return gathered(group, after)
        if group == "up":
            return gathered(group, gather_chips("down", gather_pass("up", after)))
        return gathered(group, gather_pass("down", after))

    scatters = {}
    in_flight = []

    def chip_stage(after):
        group, names, started = in_flight.pop()
        chunks, from_sibling = _exchange_wait("scatter_sibling", started, after, name=f"scatter_sibling_{group}_wait")
        sums = [_chip_sums(c, f, core_arr, name=f"chip_sums_{n}") for n, c, f in zip(names, chunks, from_sibling)]
        scatters[group] = (names, _exchange_start("scatter_chips", sums, [lax.empty(s.shape, s.dtype) for s in sums],
                                                  nothing, name="scatter_chips_" + group))
        return scatters[group][1][4]

    def grads_ready(group, grads):
        if group == "tick":
            return chip_stage(grads["after"])[0:1, 0:1] if in_flight else None
        names = tuple(grads)
        chunks = [_grad_chunks(n, grads[n]) for n in names]
        token = chip_stage(chunks[0]) if in_flight else nothing
        zones = [lax.empty((N_CHIPS, *c.shape[1:]), c.dtype) for c in chunks]
        started = _exchange_start("scatter_sibling", chunks, zones, token, name="scatter_sibling_" + group)
        in_flight.append((group, names, started))
        return started[4][0:1, 0:1]

    sq, grad_x, g = _local_step(x[0], x16, mem[0], loss_target[0], weights_of, grads_ready)
    small = _finish_small_grads(g)

    out = {}
    after = chip_stage(grad_x)
    for group, (names, started) in scatters.items():
        sums, lands = _exchange_wait("scatter_chips", started, after, name=f"scatter_chips_{group}_wait")
        for n, parts, own in zip(names, lands, sums):
            res = _adamw_shard(parts, own, chip_arr, _update_view(n, args[n]), _update_view(n, args["m_" + n]),
                               _update_view(n, args["v_" + n]), name="adamw_" + n)
            out[n] = [_shard_result(n, r, args[n].shape) for r in res]
            after = res[1]

    packed, _ = lax.optimization_barrier((_pack_small(small), after))
    gs, ds, ms, vs = _small_allreduce_adamw(
        packed, _pack_small({n: wt[n] for n in _VECTORS}), _pack_small({n: mo[n] for n in _VECTORS}),
        _pack_small({n: vo[n] for n in _VECTORS}))
    gs, ds, ms, vs = _unpack_small(gs), _unpack_small(ds), _unpack_small(ms), _unpack_small(vs)
    cols = conv_w.shape[-1]
    conv_full = gs["conv_w"].reshape(CONV_K, QKV_COLS)
    conv_mine = lax.dynamic_slice(conv_full, (0, me * cols), (CONV_K, cols))[None]
    res = _adamw_shard(conv_mine, conv_mine, jnp.zeros((1,), jnp.int32), wt["conv_w"], mo["conv_w"], vo["conv_w"],
                       name="adamw_conv_w")
    out["conv_w"] = [r.reshape(conv_w.shape) for r in res]
    for n in _VECTORS:
        out[n] = [t[n].reshape(args[n].shape) for t in (gs, ds, ms, vs)]

    loss = lax.psum(0.5 * sq[0, 0] / D_MODEL, ("x", "y", "c"))
    return (loss, grad_x[None], *[out[n][0] for n in _WEIGHT_ORDER], *[out[n][1] for n in _WEIGHT_ORDER],
            *[out[n][2] for n in _WEIGHT_ORDER], *[out[n][3] for n in _WEIGHT_ORDER])
```

```python
import functools
import math

import jax
import jax.numpy as jnp
from jax import lax
from jax.experimental import pallas as pl
from jax.experimental.pallas import tpu as pltpu

F32 = jnp.float32
BF16 = jnp.bfloat16
MESH = pl.DeviceIdType.MESH

N_DEV = 8
D_MODEL = 2048
GDN_WIDTH = 1024
GDN_HEADS = 8
HEAD_DIM = 128
CONV_K = 4
CHUNK = 64
POOL_GROUPS = 4
POOL_GROUP_DIM = 256
MEM_LEN = 256
XATTN_HEADS = 4
XATTN_HEAD_DIM = 512
D_FF = 8192
IN_COLS = 5136
ALPHA = 2.0 ** 0.25
LN_EPS = 1e-5
NORM_EPS = 1e-6

LANE = 128
QKV_COLS = 3 * GDN_WIDTH
Z_OFF = QKV_COLS
BA_OFF = 4 * GDN_WIDTH
POOL_OFF = BA_OFF + 2 * LANE
PROJ_COLS = POOL_OFF + GDN_WIDTH
Z_BLK = Z_OFF // LANE
BA_BLK = BA_OFF // LANE
POOL_BLK = POOL_OFF // POOL_GROUP_DIM

ADAM_LR = 0.001
ADAM_B1 = 0.9
ADAM_B2 = 0.999
ADAM_EPS = 1e-08
ADAM_WD = 0.01
ADAM_STEP = 10

VMEM_LIMIT_BYTES = 48 * 1024 * 1024


def _params(*sem):
    return pltpu.CompilerParams(dimension_semantics=sem if sem else None, vmem_limit_bytes=VMEM_LIMIT_BYTES)


def _make_dots(cast, precision, batched=False):
    lead = 1 if batched else 0
    batch = ((0,), (0,)) if batched else ((), ())

    def dg(a, b, ca, cb):
        if cast is not None:
            a = a.astype(cast)
            b = b.astype(cast)
        return lax.dot_general(a, b, (((ca + lead,), (cb + lead,)), batch), precision=precision, preferred_element_type=F32)

    def nn_(a, b):
        return dg(a, b, 1, 0)

    def nt_(a, b):
        return dg(a, b, 1, 1)

    def tn_(a, b):
        return dg(a, b, 0, 0)

    @jax.custom_vjp
    def nn(a, b):
        return nn_(a, b)

    nn.defvjp(lambda a, b: (nn_(a, b), (a, b)), lambda r, g: (nt_(g, r[1]), tn_(r[0], g)))

    @jax.custom_vjp
    def nt(a, b):
        return nt_(a, b)

    nt.defvjp(lambda a, b: (nt_(a, b), (a, b)), lambda r, g: (nn_(g, r[1]), tn_(g, r[0])))

    @jax.custom_vjp
    def tn(a, b):
        return tn_(a, b)

    tn.defvjp(lambda a, b: (tn_(a, b), (a, b)), lambda r, g: (nt_(r[1], g), nn_(r[0], g)))

    return (nn_, nt_, tn_), (nn, nt, tn)


_BDOT_PLAIN, _BDOT_VJP = _make_dots(BF16, None)
_BDOT_BATCH_PLAIN, _BDOT_BATCH_VJP = _make_dots(BF16, None, batched=True)
_FDOT_BATCH_PLAIN, _FDOT_BATCH_VJP = _make_dots(BF16, None, batched=True)


def _mm(a, b, *, ta=False, tb=False, out_dtype=F32, tm=None, tn=512, tk=None, epi=None, extra=None, add_scale=1.0,
        b_chunks=False, o_chunks=False, name):
    m, k = (a.shape[1], a.shape[0]) if ta else a.shape
    if b_chunks:
        n, kb = (b.shape[1], N_DEV * b.shape[2]) if tb else (N_DEV * b.shape[2], b.shape[1])
    else:
        n, kb = b.shape if tb else (b.shape[1], b.shape[0])
    assert kb == k, (name, a.shape, b.shape)
    tm, tn, tk = min(tm or m, m), min(tn, n), min(tk or k, k)
    assert m % tm == 0 and n % tn == 0 and k % tk == 0, (name, m, n, k)
    nk = k // tk
    dims = (((0 if ta else 1,), (1 if tb else 0,)), ((), ()))
    n_extra = 0 if epi in (None, "relu2") else 1
    n_out = 2 if epi == "relu2" else 1
    if epi in ("relu2", "mul2r"):
        out_dtype = BF16

    def body(*refs):
        a_ref, b_ref = refs[:2]
        c_ref = refs[2] if n_extra else None
        o_refs = refs[2 + n_extra:2 + n_extra + n_out]
        scr = refs[2 + n_extra + n_out:]
        r = lax.dot_general(a_ref[...].astype(BF16), b_ref[...].astype(BF16), dims, preferred_element_type=F32)

        def finish(v):
            if epi == "add":
                o_refs[0][...] = (v + add_scale * c_ref[...]).astype(out_dtype)
            elif epi == "relu2":
                p = jnp.maximum(v, 0.0)
                o_refs[0][...] = (p * p).astype(BF16)
                o_refs[1][...] = p.astype(BF16)
            elif epi == "mul2r":
                o_refs[0][...] = (v * (2.0 * c_ref[...].astype(F32))).astype(BF16)
            else:
                o_refs[0][...] = v.astype(out_dtype)

        if nk == 1:
            finish(r)
        else:
            acc = scr[0]
            kk = pl.program_id(2)

            @pl.when(kk == 0)
            def _():
                acc[...] = r

            @pl.when(kk > 0)
            def _():
                acc[...] += r

            @pl.when(kk == nk - 1)
            def _():
                finish(acc[...])

    a_spec = pl.BlockSpec((tk, tm), lambda i, j, kk: (kk, i)) if ta else pl.BlockSpec((tm, tk), lambda i, j, kk: (i, kk))
    if b_chunks and tb:
        kc = k // N_DEV // tk
        b_spec = pl.BlockSpec((None, tn, tk), lambda i, j, kk: (kk // kc, j, kk % kc))
    elif b_chunks:
        nc = n // N_DEV // tn
        b_spec = pl.BlockSpec((None, tk, tn), lambda i, j, kk: (j // nc, kk, j % nc))
    elif tb:
        b_spec = pl.BlockSpec((tn, tk), lambda i, j, kk: (j, kk))
    else:
        b_spec = pl.BlockSpec((tk, tn), lambda i, j, kk: (kk, j))
    mn_spec = pl.BlockSpec((tm, tn), lambda i, j, kk: (i, j))
    if o_chunks:
        oc = n // N_DEV // tn
        o_spec = pl.BlockSpec((None, tm, tn), lambda i, j, kk: (j // oc, i, j % oc))
        o_shape = jax.ShapeDtypeStruct((N_DEV, m, n // N_DEV), out_dtype)
    else:
        o_spec, o_shape = mn_spec, jax.ShapeDtypeStruct((m, n), out_dtype)
    res = pl.pallas_call(
        body, grid=(m // tm, n // tn, nk), in_specs=[a_spec, b_spec] + [mn_spec] * n_extra,
        out_specs=[o_spec] * n_out, out_shape=[o_shape] * n_out,
        scratch_shapes=[pltpu.VMEM((tm, tn), F32)] if nk > 1 else [],
        compiler_params=_params("parallel", "parallel", "arbitrary"), name=name,
    )(a, b, *([extra] if n_extra else []))
    return res if n_out > 1 else res[0]


def _cast_bf16(v, *, name, tm=512):
    t, d = v.shape
    tm = min(tm, t)

    def body(v_ref, o_ref):
        o_ref[...] = v_ref[...].astype(BF16)

    spec = pl.BlockSpec((tm, d), lambda i: (i, 0))
    return pl.pallas_call(body, grid=(t // tm,), in_specs=[spec], out_specs=spec,
                          out_shape=jax.ShapeDtypeStruct((t, d), BF16), compiler_params=_params("parallel"), name=name)(v)


def _mlp_up_chunks(h16, w_up, which, prev, *, tn=512, name):
    t, d = h16.shape
    per = w_up.shape[2]
    n_which = which.shape[0]

    def body(which_ref, h_ref, w_ref, *rest):
        act_ref, relu_ref = rest[-2:]
        p = jnp.maximum(jnp.dot(h_ref[...], w_ref[...], preferred_element_type=F32), 0.0)
        act_ref[...] = (p * p).astype(BF16)
        relu_ref[...] = p.astype(BF16)

    out_spec = pl.BlockSpec((t, tn), lambda i, j, which_ref: (0, which_ref[i] * (per // tn) + j))
    out = jax.ShapeDtypeStruct((t, N_DEV * per), BF16)
    args = [which, h16, w_up] + (list(prev) if prev is not None else [])
    return pl.pallas_call(
        body,
        grid_spec=pltpu.PrefetchScalarGridSpec(
            num_scalar_prefetch=1, grid=(n_which, per // tn),
            in_specs=[pl.BlockSpec((t, d), lambda i, j, which_ref: (0, 0)),
                      pl.BlockSpec((None, d, tn), lambda i, j, which_ref: (which_ref[i], 0, j))]
                     + ([_ANY, _ANY] if prev is not None else []),
            out_specs=[out_spec, out_spec]),
        out_shape=[out, out], input_output_aliases={3: 0, 4: 1} if prev is not None else {},
        compiler_params=_params("arbitrary", "arbitrary"), name=name,
    )(*args)


def _mlp_down_chunks(act, w_down, which, prev, *, tn=512, name):
    t = act.shape[0]
    per, d = w_down.shape[1:]
    n_which = which.shape[0]

    def body(which_ref, a_ref, w_ref, *rest):
        o_ref, acc = rest[-2:]
        i = pl.program_id(1)
        r = jnp.dot(a_ref[...], w_ref[...], preferred_element_type=F32)

        @pl.when(i == 0)
        def _():
            acc[...] = r if prev is None else r + rest[0][...]

        @pl.when(i > 0)
        def _():
            acc[...] += r

        @pl.when(i == n_which - 1)
        def _():
            o_ref[...] = acc[...]

    out_spec = pl.BlockSpec((t, tn), lambda j, i, which_ref: (0, j))
    return pl.pallas_call(
        body,
        grid_spec=pltpu.PrefetchScalarGridSpec(
            num_scalar_prefetch=1, grid=(d // tn, n_which),
            in_specs=[pl.BlockSpec((t, per), lambda j, i, which_ref: (0, which_ref[i])),
                      pl.BlockSpec((None, per, tn), lambda j, i, which_ref: (which_ref[i], 0, j))]
                     + ([out_spec] if prev is not None else []),
            out_specs=out_spec, scratch_shapes=[pltpu.VMEM((t, tn), F32)]),
        out_shape=jax.ShapeDtypeStruct((t, d), F32),
        compiler_params=_params("parallel", "arbitrary"), name=name,
    )(which, act, w_down, *([prev] if prev is not None else []))


def _shift_down(v, s):
    if s == 0:
        return v
    row = lax.broadcasted_iota(jnp.int32, v.shape, 0)
    return jnp.where(row >= s, pltpu.roll(v, s, axis=0), 0.0)


def _shift_up(v, s):
    if s == 0:
        return v
    t = v.shape[0]
    row = lax.broadcasted_iota(jnp.int32, v.shape, 0)
    return jnp.where(row < t - s, pltpu.roll(v, t - s, axis=0), 0.0)


def _post_col(j):
    return (j % GDN_HEADS) * 3 + j // GDN_HEADS


def _gdn_prep_fwd(proj, conv_w):
    t = proj.shape[0]

    def body(x_ref, w_ref, o_ref):
        j = pl.program_id(0)
        x = x_ref[...]
        y = jnp.zeros_like(x)
        for tap in range(CONV_K):
            y = y + w_ref[tap:tap + 1, :] * _shift_down(x, CONV_K - 1 - tap)
        c = y * jax.nn.sigmoid(y)
        nrm = c * lax.rsqrt(jnp.sum(c * c, axis=1, keepdims=True) + NORM_EPS)
        o_ref[...] = jnp.where(j < 2 * GDN_HEADS, nrm, c)

    return pl.pallas_call(
        body, grid=(QKV_COLS // LANE,),
        in_specs=[pl.BlockSpec((t, LANE), lambda j: (0, j)), pl.BlockSpec((CONV_K, LANE), lambda j: (0, j))],
        out_specs=pl.BlockSpec((t, LANE), lambda j: (0, _post_col(j))),
        out_shape=jax.ShapeDtypeStruct((t, QKV_COLS), F32),
        compiler_params=_params("parallel"), name="gdn_prep_fwd",
    )(proj, conv_w)


def _gdn_prep_bwd(proj, conv_w, dpost, dproj):
    t = proj.shape[0]

    def body(x_ref, w_ref, d_ref, _, dx_ref, dw_ref):
        j = pl.program_id(0)
        x = x_ref[...]
        xs = [_shift_down(x, CONV_K - 1 - tap) for tap in range(CONV_K)]
        y = jnp.zeros_like(x)
        for tap in range(CONV_K):
            y = y + w_ref[tap:tap + 1, :] * xs[tap]
        sig = jax.nn.sigmoid(y)
        c = y * sig
        r = lax.rsqrt(jnp.sum(c * c, axis=1, keepdims=True) + NORM_EPS)
        nrm = c * r
        d = d_ref[...]
        dc_norm = r * (d - nrm * jnp.sum(d * nrm, axis=1, keepdims=True))
        dc = jnp.where(j < 2 * GDN_HEADS, dc_norm, d)
        dy = dc * (sig * (1.0 + y * (1.0 - sig)))
        dx = jnp.zeros_like(x)
        for tap in range(CONV_K):
            dx = dx + _shift_up(w_ref[tap:tap + 1, :] * dy, CONV_K - 1 - tap)
            dw_ref[tap:tap + 1, :] = jnp.sum(dy * xs[tap], axis=0, keepdims=True)
        dx_ref[...] = dx.astype(dx_ref.dtype)

    return pl.pallas_call(
        body, grid=(QKV_COLS // LANE,),
        in_specs=[pl.BlockSpec((t, LANE), lambda j: (0, j)), pl.BlockSpec((CONV_K, LANE), lambda j: (0, j)),
                  pl.BlockSpec((t, LANE), lambda j: (0, _post_col(j))), pl.BlockSpec(memory_space=pl.ANY)],
        out_specs=[pl.BlockSpec((t, LANE), lambda j: (0, j)), pl.BlockSpec((CONV_K, LANE), lambda j: (0, j))],
        out_shape=[jax.ShapeDtypeStruct(dproj.shape, dproj.dtype), jax.ShapeDtypeStruct((CONV_K, QKV_COLS), F32)],
        input_output_aliases={3: 0},
        compiler_params=_params("parallel"), name="gdn_prep_bwd",
    )(proj, conv_w, dpost, dproj)


def _softplus(v):
    return jnp.maximum(v, 0.0) + jnp.log(1.0 + jnp.exp(-jnp.abs(v)))


def _tri_inv(low, nn):
    r = lax.broadcasted_iota(jnp.int32, (CHUNK, CHUNK), 0)
    c = lax.broadcasted_iota(jnp.int32, (CHUNK, CHUNK), 1)
    eye = (r == c).astype(F32)
    same_blk = lax.shift_right_logical(r, 4) == lax.shift_right_logical(c, 4)
    diag = jnp.where(same_blk, low, 0.0)
    off = low - diag
    n1 = -diag
    n2 = nn(n1, n1)
    n4 = nn(n2, n2)
    n8 = nn(n4, n4)
    inv_d = nn(nn(nn(eye + n1, eye + n2), eye + n4), eye + n8)
    m1 = nn(inv_d, off)
    m2 = nn(m1, m1)
    return nn(nn(eye - m1, eye + m2), inv_d)


@jax.custom_vjp
def _tri_inv_known(low, t_inv):
    return t_inv


def _tri_inv_known_fwd(low, t_inv):
    return t_inv, t_inv


def _tri_inv_known_bwd(t_inv, g):
    _, nt, tn = _FDOT_BATCH_PLAIN
    return -nt(tn(t_inv, g), t_inv), jnp.zeros_like(t_inv)


_tri_inv_known.defvjp(_tri_inv_known_fwd, _tri_inv_known_bwd)


LOCAL_HEADS_PER_STEP = 8


def _gdn_local_fn(qkv, ba, alog_row, dtb_row, first_head, bdots, fdots, t_known=None):
    nn, nt, tn = bdots
    fnn = fdots[0]
    n_heads = qkv.shape[1] // (3 * HEAD_DIM)
    part = lambda i, p: qkv[:, (3 * i + p) * HEAD_DIM:(3 * i + p + 1) * HEAD_DIM]
    q = jnp.stack([part(i, 0) for i in range(n_heads)]) * (HEAD_DIM ** -0.5)
    k = jnp.stack([part(i, 1) for i in range(n_heads)])
    v = jnp.stack([part(i, 2) for i in range(n_heads)])
    lane = lax.broadcasted_iota(jnp.int32, ba.shape, 1)
    bg = jnp.where(lane < GDN_HEADS, jax.nn.sigmoid(ba), -jnp.exp(alog_row) * _softplus(ba + dtb_row))
    pick = lambda l: jnp.sum(jnp.where(lane == l, bg, 0.0), axis=1, keepdims=True)
    beta = jnp.stack([pick(first_head + i) for i in range(n_heads)])
    g = jnp.stack([pick(first_head + i + GDN_HEADS) for i in range(n_heads)])

    r = lax.broadcasted_iota(jnp.int32, (CHUNK, CHUNK), 0)
    c = lax.broadcasted_iota(jnp.int32, (CHUNK, CHUNK), 1)
    incl = r >= c
    strict = r > c
    eye = r == c

    def to_row(col):
        return jnp.sum(jnp.where(eye, col, 0.0), axis=1, keepdims=True)

    gc = jnp.sum(jnp.where(incl, to_row(g), 0.0), axis=2, keepdims=True)
    diff = gc - to_row(gc)
    decay = jnp.where(incl, jnp.exp(jnp.where(incl, diff, 0.0)), 0.0)
    k_beta = k * beta
    v_beta = v * beta
    low = jnp.where(strict, nt(k_beta, k) * decay, 0.0)
    t_inv = _tri_inv(low, fnn) if t_known is None else _tri_inv_known(low, t_known)
    eg = jnp.exp(gc)
    u = fnn(t_inv, v_beta)
    w = fnn(t_inv, k_beta * eg)
    attn = jnp.where(incl, nt(q, k) * decay, 0.0)
    last = lax.broadcasted_iota(jnp.int32, (CHUNK, 1), 0) == CHUNK - 1
    g_last = jnp.sum(jnp.where(last, gc, 0.0), axis=1, keepdims=True)
    kdec = k * jnp.exp(g_last - gc)
    elast = jnp.broadcast_to(jnp.exp(g_last), (n_heads, 1, LANE))
    return u, w, q * eg, kdec, attn, elast, t_inv


def _gdn_state_fn(u, w, qg, kdec, attn, elast, state, bdots):
    nn, _, tn = bdots
    v_new = u - nn(w, state)
    o = nn(qg, state) + nn(attn, v_new)
    return o, state * elast + tn(kdec, v_new)


def _gdn_local_fwd(post, proj, alog_row, dtb_row):
    t = post.shape[0]
    n_chunks = t // CHUNK
    hb = LOCAL_HEADS_PER_STEP

    def body(qkv_ref, ba_ref, al_ref, dt_ref, u_ref, w_ref, qg_ref, kd_ref, at_ref, el_ref, ti_ref):
        u, w, qg, kdec, attn, elast, t_inv = _gdn_local_fn(qkv_ref[...], ba_ref[...], al_ref[...], dt_ref[...],
                                                           pl.program_id(1) * hb, _BDOT_BATCH_PLAIN, _FDOT_BATCH_PLAIN)
        for i in range(hb):
            cols = slice(i * HEAD_DIM, (i + 1) * HEAD_DIM)
            u_ref[:, cols] = u[i]
            w_ref[:, cols] = w[i].astype(BF16)
            qg_ref[:, cols] = qg[i].astype(BF16)
            kd_ref[:, cols] = kdec[i].astype(BF16)
        at_ref[...] = attn.astype(BF16)
        el_ref[:, 0] = elast
        ti_ref[...] = t_inv

    wide = pl.BlockSpec((CHUNK, hb * HEAD_DIM), lambda n, j: (n, j))
    square = pl.BlockSpec((hb, CHUNK, CHUNK), lambda n, j: (j, n, 0))
    row = pl.BlockSpec((1, LANE), lambda n, j: (0, 0))
    res = pl.pallas_call(
        body, grid=(n_chunks, GDN_HEADS // hb),
        in_specs=[pl.BlockSpec((CHUNK, hb * 3 * HEAD_DIM), lambda n, j: (n, j)),
                  pl.BlockSpec((CHUNK, LANE), lambda n, j: (n, BA_BLK)), row, row],
        out_specs=[wide, wide, wide, wide, square, pl.BlockSpec((hb, 1, 1, LANE), lambda n, j: (j, n, 0, 0)), square],
        out_shape=[jax.ShapeDtypeStruct((t, GDN_WIDTH), F32), jax.ShapeDtypeStruct((t, GDN_WIDTH), BF16),
                   jax.ShapeDtypeStruct((t, GDN_WIDTH), BF16), jax.ShapeDtypeStruct((t, GDN_WIDTH), BF16),
                   jax.ShapeDtypeStruct((GDN_HEADS, t, CHUNK), BF16),
                   jax.ShapeDtypeStruct((GDN_HEADS, n_chunks, 1, LANE), F32),
                   jax.ShapeDtypeStruct((GDN_HEADS, t, CHUNK), F32)],
        compiler_params=_params("parallel", "parallel"), name="gdn_local_fwd",
    )(post, proj, alog_row, dtb_row)
    return tuple(res[:6]), res[6]


def _by_head(ref):
    return jnp.stack([ref[:, h * HEAD_DIM:(h + 1) * HEAD_DIM] for h in range(ref.shape[1] // HEAD_DIM)])


def _gdn_state_specs(n_of):
    wide = pl.BlockSpec((CHUNK, GDN_WIDTH), lambda n: (n_of(n), 0))
    attn = pl.BlockSpec((GDN_HEADS, CHUNK, CHUNK), lambda n: (0, n_of(n), 0))
    elast = pl.BlockSpec((GDN_HEADS, 1, 1, LANE), lambda n: (0, n_of(n), 0, 0))
    saved = pl.BlockSpec((GDN_HEADS, 1, HEAD_DIM, HEAD_DIM), lambda n: (0, n_of(n), 0, 0))
    return wide, attn, elast, saved


def _gdn_state_fwd(u, w, qg, kdec, attn, elast):
    t = u.shape[0]
    n_chunks = t // CHUNK

    def body(u_ref, w_ref, qg_ref, kd_ref, at_ref, el_ref, o_ref, save_ref, state_ref):
        @pl.when(pl.program_id(0) == 0)
        def _():
            state_ref[...] = jnp.zeros_like(state_ref)

        state = state_ref[...]
        save_ref[:, 0] = state
        o, new_state = _gdn_state_fn(_by_head(u_ref), _by_head(w_ref), _by_head(qg_ref), _by_head(kd_ref), at_ref[...],
                                     el_ref[:, 0], state, _BDOT_BATCH_PLAIN)
        for h in range(GDN_HEADS):
            o_ref[:, h * HEAD_DIM:(h + 1) * HEAD_DIM] = o[h]
        state_ref[...] = new_state

    wide, attn_spec, elast_spec, saved_spec = _gdn_state_specs(lambda n: n)
    return pl.pallas_call(
        body, grid=(n_chunks,), in_specs=[wide, wide, wide, wide, attn_spec, elast_spec],
        out_specs=[wide, saved_spec],
        out_shape=[jax.ShapeDtypeStruct((t, GDN_WIDTH), F32),
                   jax.ShapeDtypeStruct((GDN_HEADS, n_chunks, HEAD_DIM, HEAD_DIM), F32)],
        scratch_shapes=[pltpu.VMEM((GDN_HEADS, HEAD_DIM, HEAD_DIM), F32)],
        compiler_params=_params("arbitrary"), name="gdn_state_fwd",
    )(u, w, qg, kdec, attn, elast)


def _gdn_state_bwd(u, w, qg, kdec, attn, elast, saved, do):
    t = u.shape[0]
    n_chunks = t // CHUNK
    last = n_chunks - 1

    def body(u_ref, w_ref, qg_ref, kd_ref, at_ref, el_ref, save_ref, do_ref,
             du_ref, dw_ref, dqg_ref, dkd_ref, dat_ref, del_ref, dstate_ref):
        @pl.when(pl.program_id(0) == 0)
        def _():
            dstate_ref[...] = jnp.zeros_like(dstate_ref)

        _, vjp = jax.vjp(
            lambda *a: _gdn_state_fn(*a, _BDOT_BATCH_VJP), _by_head(u_ref), _by_head(w_ref).astype(F32),
            _by_head(qg_ref).astype(F32), _by_head(kd_ref).astype(F32), at_ref[...].astype(F32), el_ref[:, 0],
            save_ref[:, 0])
        du, dw, dqg, dkd, dat, de, dstate = vjp((_by_head(do_ref), dstate_ref[...]))
        for h in range(GDN_HEADS):
            cols = slice(h * HEAD_DIM, (h + 1) * HEAD_DIM)
            du_ref[:, cols] = du[h]
            dw_ref[:, cols] = dw[h]
            dqg_ref[:, cols] = dqg[h]
            dkd_ref[:, cols] = dkd[h]
        dat_ref[...] = dat
        del_ref[:, 0] = de
        dstate_ref[...] = dstate

    wide, attn_spec, elast_spec, saved_spec = _gdn_state_specs(lambda n: last - n)
    wide_f32 = jax.ShapeDtypeStruct((t, GDN_WIDTH), F32)
    return pl.pallas_call(
        body, grid=(n_chunks,), in_specs=[wide, wide, wide, wide, attn_spec, elast_spec, saved_spec, wide],
        out_specs=[wide, wide, wide, wide, attn_spec, elast_spec],
        out_shape=[wide_f32, wide_f32, wide_f32, wide_f32, jax.ShapeDtypeStruct((GDN_HEADS, t, CHUNK), F32),
                   jax.ShapeDtypeStruct((GDN_HEADS, n_chunks, 1, LANE), F32)],
        scratch_shapes=[pltpu.VMEM((GDN_HEADS, HEAD_DIM, HEAD_DIM), F32)],
        compiler_params=_params("arbitrary"), name="gdn_state_bwd",
    )(u, w, qg, kdec, attn, elast, saved, do)


def _gdn_local_bwd(post, proj, alog_row, dtb_row, t_inv, cots, dproj):
    t = post.shape[0]
    n_chunks = t // CHUNK
    hb = LOCAL_HEADS_PER_STEP
    n_steps = GDN_HEADS // hb

    def body(qkv_ref, ba_ref, al_ref, dt_ref, ti_ref, du_ref, dw_ref, dqg_ref, dkd_ref, dat_ref, del_ref, _,
             dqkv_ref, dba_ref, dal_ref, ddt_ref, dba_acc):
        n = pl.program_id(0)
        j = pl.program_id(1)

        @pl.when((n == 0) & (j == 0))
        def _():
            dal_ref[...] = jnp.zeros_like(dal_ref)
            ddt_ref[...] = jnp.zeros_like(ddt_ref)

        @pl.when(j == 0)
        def _():
            dba_acc[...] = jnp.zeros_like(dba_acc)

        t_known = ti_ref[...]
        _, vjp = jax.vjp(
            lambda a, b, c, d: _gdn_local_fn(a, b, c, d, j * hb, _BDOT_BATCH_VJP, _FDOT_BATCH_VJP, t_known)[:6],
            qkv_ref[...], ba_ref[...], al_ref[...], dt_ref[...])
        dqkv, dba, dal, ddt = vjp((_by_head(du_ref), _by_head(dw_ref), _by_head(dqg_ref), _by_head(dkd_ref), dat_ref[...],
                                   del_ref[:, 0]))
        dqkv_ref[...] = dqkv
        dba_acc[...] += dba
        dal_ref[...] += dal
        ddt_ref[...] += ddt

        @pl.when(j == n_steps - 1)
        def _():
            dba_ref[:, 0:LANE] = dba_acc[...].astype(dba_ref.dtype)
            dba_ref[:, LANE:2 * LANE] = jnp.zeros((CHUNK, LANE), dba_ref.dtype)

    wide = pl.BlockSpec((CHUNK, hb * HEAD_DIM), lambda n, j: (n, j))
    qkv_spec = pl.BlockSpec((CHUNK, hb * 3 * HEAD_DIM), lambda n, j: (n, j))
    row = pl.BlockSpec((1, LANE), lambda n, j: (0, 0))
    return pl.pallas_call(
        body, grid=(n_chunks, n_steps),
        in_specs=[qkv_spec, pl.BlockSpec((CHUNK, LANE), lambda n, j: (n, BA_BLK)), row, row,
                  pl.BlockSpec((hb, CHUNK, CHUNK), lambda n, j: (j, n, 0)), wide, wide, wide, wide,
                  pl.BlockSpec((hb, CHUNK, CHUNK), lambda n, j: (j, n, 0)),
                  pl.BlockSpec((hb, 1, 1, LANE), lambda n, j: (j, n, 0, 0)), pl.BlockSpec(memory_space=pl.ANY)],
        out_specs=[qkv_spec, pl.BlockSpec((CHUNK, 2 * LANE), lambda n, j: (n, BA_BLK // 2)), row, row],
        out_shape=[jax.ShapeDtypeStruct((t, QKV_COLS), F32), jax.ShapeDtypeStruct(dproj.shape, dproj.dtype),
                   jax.ShapeDtypeStruct((1, LANE), F32), jax.ShapeDtypeStruct((1, LANE), F32)],
        input_output_aliases={11: 1},
        scratch_shapes=[pltpu.VMEM((CHUNK, LANE), F32)],
        compiler_params=_params("arbitrary", "arbitrary"), name="gdn_local_bwd",
    )(post, proj, alog_row, dtb_row, t_inv, *cots, dproj)


def _onorm_fn(o, z, w):
    return o * lax.rsqrt(jnp.mean(o * o, axis=1, keepdims=True) + NORM_EPS) * w * (z * jax.nn.sigmoid(z))


def _onorm_fwd(o_raw, proj, norm_w, mixin, tm=512):
    t = o_raw.shape[0]
    tm = min(tm, t)

    def body(o_ref, z_ref, w_ref, _, out_ref):
        out_ref[...] = _onorm_fn(o_ref[...], z_ref[...], w_ref[...]).astype(out_ref.dtype)

    return pl.pallas_call(
        body, grid=(t // tm, GDN_HEADS),
        in_specs=[pl.BlockSpec((tm, LANE), lambda i, h: (i, h)), pl.BlockSpec((tm, LANE), lambda i, h: (i, Z_BLK + h)),
                  pl.BlockSpec((1, LANE), lambda i, h: (0, 0)), pl.BlockSpec(memory_space=pl.ANY)],
        out_specs=pl.BlockSpec((tm, LANE), lambda i, h: (i, h)),
        out_shape=jax.ShapeDtypeStruct(mixin.shape, mixin.dtype), input_output_aliases={3: 0},
        compiler_params=_params("parallel", "parallel"), name="gdn_onorm_fwd",
    )(o_raw, proj, norm_w, mixin)


def _onorm_bwd(o_raw, proj, norm_w, dmixin, dproj, tm=512):
    t = o_raw.shape[0]
    tm = min(tm, t)

    def body(o_ref, z_ref, w_ref, d_ref, _, do_ref, dz_ref, dw_ref):
        @pl.when((pl.program_id(0) == 0) & (pl.program_id(1) == 0))
        def _():
            dw_ref[...] = jnp.zeros_like(dw_ref)

        _, vjp = jax.vjp(_onorm_fn, o_ref[...], z_ref[...], w_ref[...])
        do, dz, dw = vjp(d_ref[...])
        do_ref[...] = do
        dz_ref[...] = dz.astype(dz_ref.dtype)
        dw_ref[...] += dw

    return pl.pallas_call(
        body, grid=(t // tm, GDN_HEADS),
        in_specs=[pl.BlockSpec((tm, LANE), lambda i, h: (i, h)), pl.BlockSpec((tm, LANE), lambda i, h: (i, Z_BLK + h)),
                  pl.BlockSpec((1, LANE), lambda i, h: (0, 0)), pl.BlockSpec((tm, LANE), lambda i, h: (i, h)),
                  pl.BlockSpec(memory_space=pl.ANY)],
        out_specs=[pl.BlockSpec((tm, LANE), lambda i, h: (i, h)), pl.BlockSpec((tm, LANE), lambda i, h: (i, Z_BLK + h)),
                   pl.BlockSpec((1, LANE), lambda i, h: (0, 0))],
        out_shape=[jax.ShapeDtypeStruct((t, GDN_WIDTH), F32), jax.ShapeDtypeStruct(dproj.shape, dproj.dtype),
                   jax.ShapeDtypeStruct((1, LANE), F32)],
        input_output_aliases={4: 1},
        compiler_params=_params("arbitrary", "arbitrary"), name="gdn_onorm_bwd",
    )(o_raw, proj, norm_w, dmixin, dproj)


def _pool_select(levels, gi):
    out = levels[-1]
    for lvl in range(len(levels) - 2, -1, -1):
        out = jnp.where(gi == lvl, levels[lvl], out)
    return out


def _pool_count(shape, gi):
    pos = lax.broadcasted_iota(jnp.int32, shape, 0)
    win = lax.shift_left(jnp.int32(2), gi)
    return jnp.minimum(pos + 1, win).astype(F32)


def _pooled(p, gi):
    acc = p
    levels = []
    for lvl in range(POOL_GROUPS):
        acc = acc + _shift_down(acc, 1 << lvl)
        levels.append(acc)
    return _pool_select(levels, gi) / _pool_count(p.shape, gi) - p


def _pool_fwd(proj, pool_w, pool_scale):
    t = proj.shape[0]

    def body(p_ref, w_ref, s_ref, out_ref):
        gi = pl.program_id(0)
        pooled = _pooled(p_ref[...], gi)
        out_ref[...] = (_BDOT_PLAIN[0](pooled, w_ref[0]) * s_ref[0]).astype(out_ref.dtype)

    return pl.pallas_call(
        body, grid=(POOL_GROUPS,),
        in_specs=[pl.BlockSpec((t, POOL_GROUP_DIM), lambda g: (0, POOL_BLK + g)),
                  pl.BlockSpec((1, POOL_GROUP_DIM, POOL_GROUP_DIM), lambda g: (g, 0, 0)),
                  pl.BlockSpec((1, 1, POOL_GROUP_DIM), lambda g: (g, 0, 0))],
        out_specs=pl.BlockSpec((t, POOL_GROUP_DIM), lambda g: (0, GDN_WIDTH // POOL_GROUP_DIM + g)),
        out_shape=jax.ShapeDtypeStruct((t, 2 * GDN_WIDTH), BF16),
        compiler_params=_params("parallel"), name="pool_fwd",
    )(proj, pool_w, pool_scale)


def _pool_bwd(proj, pool_w, pool_scale, dmixin):
    t = proj.shape[0]
    nn, nt, tn = _BDOT_PLAIN

    def body(p_ref, w_ref, s_ref, d_ref, dp_ref, dw_ref, ds_ref):
        gi = pl.program_id(0)
        p = p_ref[...]
        pooled = _pooled(p, gi)
        mixed = nn(pooled, w_ref[0])
        d = d_ref[...]
        ds_ref[0] = jnp.sum(d * mixed, axis=0, keepdims=True)
        dmixed = d * s_ref[0]
        dw_ref[0] = tn(pooled, dmixed)
        dpooled = nt(dmixed, w_ref[0])
        acc = dpooled / _pool_count(p.shape, gi)
        levels = []
        for lvl in range(POOL_GROUPS):
            acc = acc + _shift_up(acc, 1 << lvl)
            levels.append(acc)
        dp_ref[...] = (_pool_select(levels, gi) - dpooled).astype(dp_ref.dtype)

    return pl.pallas_call(
        body, grid=(POOL_GROUPS,),
        in_specs=[pl.BlockSpec((t, POOL_GROUP_DIM), lambda g: (0, POOL_BLK + g)),
                  pl.BlockSpec((1, POOL_GROUP_DIM, POOL_GROUP_DIM), lambda g: (g, 0, 0)),
                  pl.BlockSpec((1, 1, POOL_GROUP_DIM), lambda g: (g, 0, 0)),
                  pl.BlockSpec((t, POOL_GROUP_DIM), lambda g: (0, GDN_WIDTH // POOL_GROUP_DIM + g))],
        out_specs=[pl.BlockSpec((t, POOL_GROUP_DIM), lambda g: (0, POOL_BLK + g)),
                   pl.BlockSpec((1, POOL_GROUP_DIM, POOL_GROUP_DIM), lambda g: (g, 0, 0)),
                   pl.BlockSpec((1, 1, POOL_GROUP_DIM), lambda g: (g, 0, 0))],
        out_shape=[jax.ShapeDtypeStruct((t, PROJ_COLS), BF16),
                   jax.ShapeDtypeStruct((POOL_GROUPS, POOL_GROUP_DIM, POOL_GROUP_DIM), F32),
                   jax.ShapeDtypeStruct((POOL_GROUPS, 1, POOL_GROUP_DIM), F32)],
        compiler_params=_params("parallel"), name="pool_bwd",
    )(proj, pool_w, pool_scale, dmixin)


def _ln_stats(s):
    mu = jnp.mean(s, axis=1, keepdims=True)
    xc = s - mu
    var = jnp.mean(xc * xc, axis=1, keepdims=True)
    rstd = lax.rsqrt(var + LN_EPS)
    return xc * rstd, rstd


def _ln_fwd(h_in, y, g, b, *, name, tm=256):
    t, d = h_in.shape
    tm = min(tm, t)

    def body(h_ref, y_ref, g_ref, b_ref, o_ref, o16_ref):
        xhat, _ = _ln_stats(ALPHA * h_ref[...] + y_ref[...])
        out = xhat * g_ref[...] + b_ref[...]
        o_ref[...] = out
        o16_ref[...] = out.astype(BF16)

    row = pl.BlockSpec((tm, d), lambda i: (i, 0))
    vec = pl.BlockSpec((1, d), lambda i: (0, 0))
    return pl.pallas_call(
        body, grid=(t // tm,), in_specs=[row, row, vec, vec], out_specs=[row, row],
        out_shape=[jax.ShapeDtypeStruct((t, d), F32), jax.ShapeDtypeStruct((t, d), BF16)],
        compiler_params=_params("parallel"), name=name,
    )(h_in, y, g, b)


def _ln_backward(xhat, rstd, dout, gain):
    dxhat = dout * gain
    m1 = jnp.mean(dxhat, axis=1, keepdims=True)
    m2 = jnp.mean(dxhat * xhat, axis=1, keepdims=True)
    return (rstd * (dxhat - m1 - xhat * m2), jnp.sum(dout * xhat, axis=0, keepdims=True),
            jnp.sum(dout, axis=0, keepdims=True))


def _ln_loss(h_in, y, g, b, target, *, name, tm=256):
    t, d = h_in.shape
    tm = min(tm, t)

    def body(h_ref, y_ref, g_ref, b_ref, t_ref, sq_ref, ds_ref, ds16_ref, dg_ref, dbias_ref):
        @pl.when(pl.program_id(0) == 0)
        def _():
            sq_ref[...] = jnp.zeros_like(sq_ref)
            dg_ref[...] = jnp.zeros_like(dg_ref)
            dbias_ref[...] = jnp.zeros_like(dbias_ref)

        xhat, rstd = _ln_stats(ALPHA * h_ref[...] + y_ref[...])
        err = xhat * g_ref[...] + b_ref[...] - t_ref[...]
        sq_ref[...] += jnp.sum(jnp.sum(err * err, axis=1, keepdims=True), axis=0, keepdims=True)
        ds, dg, dbias = _ln_backward(xhat, rstd, err * (1.0 / d), g_ref[...])
        ds_ref[...] = ds
        ds16_ref[...] = ds.astype(BF16)
        dg_ref[...] += dg
        dbias_ref[...] += dbias

    row = pl.BlockSpec((tm, d), lambda i: (i, 0))
    vec = pl.BlockSpec((1, d), lambda i: (0, 0))
    return pl.pallas_call(
        body, grid=(t // tm,), in_specs=[row, row, vec, vec, row],
        out_specs=[pl.BlockSpec((1, LANE), lambda i: (0, 0)), row, row, vec, vec],
        out_shape=[jax.ShapeDtypeStruct((1, LANE), F32), jax.ShapeDtypeStruct((t, d), F32),
                   jax.ShapeDtypeStruct((t, d), BF16), jax.ShapeDtypeStruct((1, d), F32), jax.ShapeDtypeStruct((1, d), F32)],
        compiler_params=_params("arbitrary"), name=name,
    )(h_in, y, g, b, target)


def _ln_bwd(h_in, y, g, d_a, d_b, *, name, tm=256):
    t, d = h_in.shape
    tm = min(tm, t)
    has_b = d_b is not None

    def body(*refs):
        if has_b:
            h_ref, y_ref, g_ref, da_ref, db_ref, ds_ref, ds16_ref, dg_ref, dbias_ref = refs
        else:
            h_ref, y_ref, g_ref, da_ref, ds_ref, ds16_ref, dg_ref, dbias_ref = refs

        @pl.when(pl.program_id(0) == 0)
        def _():
            dg_ref[...] = jnp.zeros_like(dg_ref)
            dbias_ref[...] = jnp.zeros_like(dbias_ref)

        xhat, rstd = _ln_stats(ALPHA * h_ref[...] + y_ref[...])
        dout = da_ref[...]
        if has_b:
            dout = dout + ALPHA * db_ref[...]
        ds, dg, dbias = _ln_backward(xhat, rstd, dout, g_ref[...])
        ds_ref[...] = ds
        ds16_ref[...] = ds.astype(BF16)
        dg_ref[...] += dg
        dbias_ref[...] += dbias

    row = pl.BlockSpec((tm, d), lambda i: (i, 0))
    vec = pl.BlockSpec((1, d), lambda i: (0, 0))
    args = [h_in, y, g, d_a] + ([d_b] if has_b else [])
    return pl.pallas_call(
        body, grid=(t // tm,), in_specs=[row, row, vec, row] + ([row] if has_b else []),
        out_specs=[row, row, vec, vec],
        out_shape=[jax.ShapeDtypeStruct((t, d), F32), jax.ShapeDtypeStruct((t, d), BF16),
                   jax.ShapeDtypeStruct((1, d), F32), jax.ShapeDtypeStruct((1, d), F32)],
        compiler_params=_params("arbitrary"), name=name,
    )(*args)


def _attn_fn(q, k, v, dots):
    nn, nt, _ = dots
    s = nt(q, k) * (XATTN_HEAD_DIM ** -0.5)
    s = s - lax.stop_gradient(jnp.max(s, axis=1, keepdims=True))
    e = jnp.exp(s)
    p = e / jnp.sum(e, axis=1, keepdims=True)
    return nn(p, v)


def _attn_fwd(q, k, v, tq=512):
    t = q.shape[0]
    tq = min(tq, t)

    def body(q_ref, k_ref, v_ref, o_ref):
        o_ref[...] = _attn_fn(q_ref[...], k_ref[...], v_ref[...], _BDOT_PLAIN).astype(BF16)

    qs = pl.BlockSpec((tq, XATTN_HEAD_DIM), lambda h, i: (i, h))
    ks = pl.BlockSpec((MEM_LEN, XATTN_HEAD_DIM), lambda h, i: (0, h))
    return pl.pallas_call(
        body, grid=(XATTN_HEADS, t // tq), in_specs=[qs, ks, ks], out_specs=qs,
        out_shape=jax.ShapeDtypeStruct(q.shape, BF16), compiler_params=_params("parallel", "parallel"), name="xattn_fwd",
    )(q, k, v)


def _attn_bwd(q, k, v, do, tq=512):
    t = q.shape[0]
    tq = min(tq, t)

    def body(q_ref, k_ref, v_ref, do_ref, dq_ref, dk_ref, dv_ref):
        @pl.when(pl.program_id(1) == 0)
        def _():
            dk_ref[...] = jnp.zeros_like(dk_ref)
            dv_ref[...] = jnp.zeros_like(dv_ref)

        _, vjp = jax.vjp(lambda a, b, c: _attn_fn(a, b, c, _BDOT_VJP), q_ref[...].astype(F32), k_ref[...].astype(F32),
                         v_ref[...].astype(F32))
        dq, dk, dv = vjp(do_ref[...].astype(F32))
        dq_ref[...] = dq.astype(BF16)
        dk_ref[...] += dk
        dv_ref[...] += dv

    qs = pl.BlockSpec((tq, XATTN_HEAD_DIM), lambda h, i: (i, h))
    ks = pl.BlockSpec((MEM_LEN, XATTN_HEAD_DIM), lambda h, i: (0, h))
    return pl.pallas_call(
        body, grid=(XATTN_HEADS, t // tq), in_specs=[qs, ks, ks, qs], out_specs=[qs, ks, ks],
        out_shape=[jax.ShapeDtypeStruct(q.shape, BF16), jax.ShapeDtypeStruct(k.shape, F32), jax.ShapeDtypeStruct(v.shape, F32)],
        compiler_params=_params("parallel", "arbitrary"), name="xattn_bwd",
    )(q, k, v, do)


def _local_step(x, x16, mem, target, weights_of, grads_ready):
    def behind(vec, token):
        return vec if token is None else vec + token

    w = dict(weights_of("mixer", None))
    proj = _mm(x16, w["w_in"], tb=True, tn=768, name="mm_in_proj")
    mixin = _pool_fwd(proj, w["pool_w"], w["pool_scale"])
    post = _gdn_prep_fwd(proj, w["conv_w"])
    chunked, t_inv = _gdn_local_fwd(post, proj, w["alog_row"], w["dtb_row"])
    o_raw, saved = _gdn_state_fwd(*chunked)
    token = weights_of("ahead", o_raw)
    mixin = _onorm_fwd(o_raw, proj, behind(w["gdn_norm_w"], token), mixin)
    w.update(weights_of("attn", mixin))
    mix = _mm(mixin, w["w_out"], name="mm_out_proj")
    h1, h1_16 = _ln_fwd(x, mix, w["ln1_g"], w["ln1_b"], name="ln1_fwd")
    xq = _mm(h1_16, w["xq_w"], out_dtype=BF16, name="mm_xq")
    xk = _mm(mem, w["xk_w"], out_dtype=BF16, name="mm_xk")
    xv = _mm(mem, w["xv_w"], out_dtype=BF16, name="mm_xv")
    xo = _attn_fwd(xq, xk, xv)
    xa = _mm(xo, w["xo_w"], name="mm_xo")
    h2, h2_16 = _ln_fwd(h1, xa, w["ln2_g"], w["ln2_b"], name="ln2_fwd")
    pair, ff = None, None
    for step, arrived in enumerate(weights_of("up", h2_16)):
        which, w["w_up"] = arrived(h2_16 if pair is None else pair[0])
        pair = _mlp_up_chunks(h2_16, w["w_up"], which, pair, name=f"mm_up_{step}")
    act, relu = pair
    for step, arrived in enumerate(weights_of("down", act)):
        which, w["w_down"], norms = arrived(act if ff is None else ff)
        ff = _mlp_down_chunks(act, w["w_down"], which, ff, name=f"mm_down_{step}")
    w.update(norms)
    w["w_down"] = w["w_down"].reshape(-1, D_MODEL)
    g = {}
    sq, ds3, ds3_16, g["ln3_g"], g["ln3_b"] = _ln_loss(h2, ff, w["ln3_g"], w["ln3_b"], target, name="ln3_loss")

    gw_down = _mm(act, ds3_16, ta=True, out_dtype=BF16, tm=512, tn=D_MODEL, name="mm_gw_down")
    du = _mm(ds3_16, w["w_down"], tb=True, epi="mul2r", extra=relu, name="mm_du")
    gw_up = _mm(h2_16, du, ta=True, out_dtype=BF16, o_chunks=True, name="mm_gw_up")
    token = grads_ready("mlp", {"w_down": gw_down, "w_up": gw_up})
    dh2 = _mm(du, w["w_up"], tb=True, b_chunks=True, tn=1024, tk=1024, name="mm_dh2")
    ds2, ds2_16, g["ln2_g"], g["ln2_b"] = _ln_bwd(h1, xa, behind(w["ln2_g"], token), dh2, ds3, name="ln2_bwd")
    gw_xo = _mm(xo, ds2_16, ta=True, out_dtype=BF16, name="mm_gw_xo")
    dxo = _mm(ds2_16, w["xo_w"], tb=True, out_dtype=BF16, name="mm_dxo")
    dxq, dxk, dxv = _attn_bwd(xq, xk, xv, dxo)
    gw_xq = _mm(h1_16, dxq, ta=True, out_dtype=BF16, name="mm_gw_xq")
    gw_xk = _mm(mem, dxk, ta=True, out_dtype=BF16, name="mm_gw_xk")
    gw_xv = _mm(mem, dxv, ta=True, out_dtype=BF16, name="mm_gw_xv")
    token = grads_ready("attn", {"xo_w": gw_xo, "xq_w": gw_xq, "xk_w": gw_xk, "xv_w": gw_xv})
    dh1 = _mm(dxq, w["xq_w"], tb=True, name="mm_dh1")
    ds1, ds1_16, g["ln1_g"], g["ln1_b"] = _ln_bwd(x, mix, behind(w["ln1_g"], token), dh1, ds2, name="ln1_bwd")
    gw_out = _mm(mixin, ds1_16, ta=True, out_dtype=BF16, name="mm_gw_out")
    dmixin = _mm(ds1_16, w["w_out"], tb=True, name="mm_dmixin")
    dproj, gw_pool, g["pool_scale"] = _pool_bwd(proj, w["pool_w"], w["pool_scale"], dmixin)
    token = grads_ready("mix", {"w_out": gw_out, "pool_w": gw_pool})
    do_raw, dproj, g["gdn_norm_w"] = _onorm_bwd(o_raw, proj, behind(w["gdn_norm_w"], token), dmixin, dproj)
    cots = _gdn_state_bwd(*chunked, saved, do_raw)
    token = grads_ready("tick", {"after": cots[0]})
    dpost, dproj, g["alog_row"], g["dtb_row"] = _gdn_local_bwd(post, proj, behind(w["alog_row"], token), w["dtb_row"],
                                                               t_inv, cots, dproj)
    dproj, g["conv_w"] = _gdn_prep_bwd(proj, w["conv_w"], dpost, dproj)
    gw_in = _mm(dproj, x16, ta=True, out_dtype=BF16, tm=768, tn=D_MODEL, name="mm_gw_in")
    token = grads_ready("in", {"w_in": gw_in})
    if token is not None:
        ds1, _ = lax.optimization_barrier((ds1, token))
    grad_x = _mm(dproj, w["w_in"], tk=1792, epi="add", extra=ds1, add_scale=ALPHA, name="mm_dx")
    return sq, grad_x, g


_MATRICES = ("w_in", "pool_w", "w_out", "xq_w", "xk_w", "xv_w", "xo_w", "w_up", "w_down")
_VECTORS = ("a_log", "dt_bias", "gdn_norm_w", "pool_scale", "ln1_g", "ln1_b", "ln2_g", "ln2_b", "ln3_g", "ln3_b")
_BA_SPLIT = BA_OFF + 2 * GDN_HEADS


def _lane_row(v, offset):
    return jnp.zeros((1, LANE), F32).at[0, offset:offset + v.shape[0]].set(v)


_GROUP_VECTORS = {"mixer": (), "attn": ("ln1_g", "ln1_b", "ln2_g", "ln2_b"), "up": (), "down": ("ln3_g", "ln3_b")}


def _group_weights(group, full):
    w = {n: full[n].reshape(1, D_MODEL) for n in _GROUP_VECTORS[group]}
    if group == "mixer":
        w.update({
            "w_in": _w_in_padded(full["w_in"]),
            "conv_w": full["conv_w"],
            "alog_row": _lane_row(full["a_log"], GDN_HEADS),
            "dtb_row": _lane_row(full["dt_bias"], GDN_HEADS),
            "gdn_norm_w": full["gdn_norm_w"].reshape(1, LANE),
            "pool_w": full["pool_w"],
            "pool_scale": full["pool_scale"].reshape(POOL_GROUPS, 1, POOL_GROUP_DIM),
        })
    else:
        w.update({n: full[n] for n in dict(_GATHER_GROUPS)[group]})
    return w


def _w_in_row_map():
    per = IN_COLS // N_DEV
    gap = POOL_OFF - _BA_SPLIT
    pieces = []
    for d in range(N_DEV):
        lo, hi = d * per, (d + 1) * per
        if hi <= _BA_SPLIT:
            pieces.append([(0, lo, per)])
        elif lo >= _BA_SPLIT:
            pieces.append([(0, lo + gap, per)])
        else:
            pieces.append([(0, lo, _BA_SPLIT - lo), (_BA_SPLIT - lo, POOL_OFF, hi - _BA_SPLIT)])
    return pieces


_W_IN_LANES = 256


def _w_in_padded(blocks):
    def body(b_ref, o_ref):
        for d, pieces in enumerate(_w_in_row_map()):
            for src, dst, rows in pieces:
                o_ref[dst:dst + rows, :] = b_ref[d, src:src + rows, :]
        o_ref[_BA_SPLIT:POOL_OFF, :] = jnp.zeros((POOL_OFF - _BA_SPLIT, _W_IN_LANES), o_ref.dtype)

    n, per, cols = blocks.shape
    return pl.pallas_call(
        body, grid=(cols // _W_IN_LANES,), in_specs=[pl.BlockSpec((n, per, _W_IN_LANES), lambda j: (0, 0, j))],
        out_specs=pl.BlockSpec((PROJ_COLS, _W_IN_LANES), lambda j: (0, j)),
        out_shape=jax.ShapeDtypeStruct((PROJ_COLS, cols), blocks.dtype), compiler_params=_params("parallel"),
        name="w_in_padded")(blocks)


def _w_in_chunks(g):
    def body(g_ref, o_ref):
        for d, pieces in enumerate(_w_in_row_map()):
            for dst, src, rows in pieces:
                o_ref[d, dst:dst + rows, :] = g_ref[src:src + rows, :]

    cols = g.shape[1]
    per = IN_COLS // N_DEV
    return pl.pallas_call(
        body, grid=(cols // _W_IN_LANES,), in_specs=[pl.BlockSpec((PROJ_COLS, _W_IN_LANES), lambda j: (0, j))],
        out_specs=pl.BlockSpec((N_DEV, per, _W_IN_LANES), lambda j: (0, 0, j)),
        out_shape=jax.ShapeDtypeStruct((N_DEV, per, cols), g.dtype), compiler_params=_params("parallel"),
        name="w_in_chunks")(g)


def _finish_small_grads(g):
    out = {"conv_w": g["conv_w"]}
    out["a_log"] = g["alog_row"][0, GDN_HEADS:2 * GDN_HEADS]
    out["dt_bias"] = g["dtb_row"][0, GDN_HEADS:2 * GDN_HEADS]
    out["gdn_norm_w"] = g["gdn_norm_w"].reshape(LANE)
    out["pool_scale"] = g["pool_scale"].reshape(POOL_GROUPS * POOL_GROUP_DIM)
    for n in ("ln1_g", "ln1_b", "ln2_g", "ln2_b", "ln3_g", "ln3_b"):
        out[n] = g[n].reshape(D_MODEL)
    return out


def _adamw_math(w, g, m, v):
    m = ADAM_B1 * m + (1.0 - ADAM_B1) * g
    v = ADAM_B2 * v + (1.0 - ADAM_B2) * (g * g)
    m_hat = m / (1.0 - ADAM_B1 ** ADAM_STEP)
    v_hat = v / (1.0 - ADAM_B2 ** ADAM_STEP)
    delta = -ADAM_LR * (m_hat / (jnp.sqrt(v_hat) + ADAM_EPS) + ADAM_WD * w)
    return delta, m, v


ADAMW_TILE_ELEMS = 256 * 1024
CHIP_SUM_TILE_ELEMS = 1024 * 1024


def _shard_tile(r, c, elems):
    for rows in (1024, 512, 256, 128):
        if r % rows == 0 and rows * c <= elems:
            return rows, c
    if r % 128 == 0:
        return 128, c
    return r, 256 if c % 256 == 0 else c


def _adamw_shard(parts, own, me, w, m, v, *, name):
    s, r, c = parts.shape
    tr, tc = _shard_tile(r, c, ADAMW_TILE_ELEMS)
    assert r % tr == 0 and c % tc == 0, (name, r, c)
    unit_axis = w.ndim == 3
    at = (slice(None), 0, slice(None)) if unit_axis else Ellipsis

    def body(me_ref, p_ref, own_ref, w_ref, m_ref, v_ref, g_ref, d_ref, nm_ref, nv_ref):
        mine = own_ref[...].astype(F32)
        g = None
        for i in range(s):
            part = jnp.where(me_ref[0] == i, mine, p_ref[i].astype(F32))
            g = part if g is None else g + part
        delta, nm, nv = _adamw_math(w_ref[at], g, m_ref[at], v_ref[at])
        g_ref[at] = g
        d_ref[at] = delta
        nm_ref[at] = nm
        nv_ref[at] = nv

    if unit_axis:
        blk = pl.BlockSpec((tr, 1, tc), lambda i, j, me_ref: (i, 0, j))
        out = jax.ShapeDtypeStruct((r, 1, c), F32)
    else:
        blk = pl.BlockSpec((tr, tc), lambda i, j, me_ref: (i, j))
        out = jax.ShapeDtypeStruct((r, c), F32)
    return pl.pallas_call(
        body,
        grid_spec=pltpu.PrefetchScalarGridSpec(
            num_scalar_prefetch=1, grid=(r // tr, c // tc),
            in_specs=[pl.BlockSpec((s, tr, tc), lambda i, j, me_ref: (0, i, j)),
                      pl.BlockSpec((None, tr, tc), lambda i, j, me_ref: (me_ref[0], i, j)), blk, blk, blk],
            out_specs=[blk, blk, blk, blk]),
        out_shape=[out, out, out, out], compiler_params=_params("parallel", "parallel"), name=name,
    )(me, parts, own, w, m, v)


N_CHIPS = N_DEV // 2


def _chip_sums(chunks, from_sibling, core, *, name):
    _, r, c = chunks.shape
    tr, tc = _shard_tile(r, c, CHIP_SUM_TILE_ELEMS)
    assert r % tr == 0 and c % tc == 0, (name, r, c)

    def body(core_ref, mine_ref, other_ref, o_ref):
        o_ref[...] = (mine_ref[...].astype(F32) + other_ref[...].astype(F32)).astype(o_ref.dtype)

    by_chip = pl.BlockSpec((None, tr, tc), lambda q, i, j, core_ref: (q, i, j))
    return pl.pallas_call(
        body,
        grid_spec=pltpu.PrefetchScalarGridSpec(
            num_scalar_prefetch=1, grid=(N_CHIPS, r // tr, c // tc),
            in_specs=[pl.BlockSpec((None, tr, tc), lambda q, i, j, core_ref: (2 * q + core_ref[0], i, j)), by_chip],
            out_specs=by_chip),
        out_shape=jax.ShapeDtypeStruct((N_CHIPS, r, c), chunks.dtype),
        compiler_params=_params("parallel", "parallel", "parallel"), name=name,
    )(core, chunks, from_sibling)


def _place():
    return lax.axis_index("x"), lax.axis_index("y"), lax.axis_index("c")


def _slot(px, py, pc):
    return 4 * px + 2 * py + pc


_HBM = pl.BlockSpec(memory_space=pltpu.HBM)


_SEM = pl.BlockSpec(memory_space=pltpu.SEMAPHORE)
_ANY = pl.BlockSpec(memory_space=pl.ANY)
_EFFECT = pltpu.SideEffectType.DATAFLOW_SIDE_EFFECTING
_N_PEERS = N_DEV - 1


def _peer(k, x, y, c):
    return (1 - x if k & 4 else x, 1 - y if k & 2 else y, 1 - c if k & 1 else c)


_EXCHANGE_BITS = {"gather_chips": (1, 2, 4, 6), "gather_pass": (2, 4, 6), "scatter_sibling": (1, 1, 1, 1),
                  "scatter_chips": (2, 4, 6)}


def _exchange_copy(mode, src, land, w, i, place, send_sems, recv_sems, receiving):
    bits = _EXCHANGE_BITS[mode]
    k = bits[i]
    peer = _peer(k, *place)
    me = _slot(*place)
    if mode == "gather_chips":
        to, src_ref, sent_to, got_at = peer, src[w], me, _slot(*peer)
    elif mode == "gather_pass":
        blk = _slot(*peer)
        to, src_ref, sent_to, got_at = _peer(1, *place), land[w].at[blk], blk, _slot(*_peer(k | 1, *place))
    elif mode == "scatter_sibling":
        to, src_ref, sent_to, got_at = peer, src[w].at[2 * i + 1 - place[2]], i, i
    else:
        to, src_ref, sent_to, got_at = peer, src[w].at[_slot(*peer) // 2], me // 2, _slot(*peer) // 2
    sem = w * len(bits) + i
    return pltpu.make_async_remote_copy(
        src_ref=src_ref, dst_ref=land[w].at[got_at if receiving else sent_to], send_sem=send_sems.at[sem],
        recv_sem=recv_sems.at[sem], device_id=to, device_id_type=MESH)


def _exchange_start(mode, srcs, lands, after, *, name):
    ns, nl = len(srcs), len(lands)
    n_sem = nl * len(_EXCHANGE_BITS[mode])

    def body(*refs):
        src, land = refs[:ns], refs[ns:ns + nl]
        send_sems, recv_sems = refs[ns + nl + 1:ns + nl + 3]
        token = refs[-1]
        place = _place()
        for w in range(nl):
            for i in range(len(_EXCHANGE_BITS[mode])):
                _exchange_copy(mode, src, land, w, i, place, send_sems, recv_sems, receiving=False).start()
        token[...] = jnp.zeros_like(token)

    sems = pltpu.SemaphoreType.DMA((n_sem,))
    arrays = list(srcs) + list(lands)
    res = pl.pallas_call(
        body, name=name, in_specs=[_HBM] * (ns + nl) + [_ANY],
        out_specs=(_SEM, _SEM, *([_HBM] * (ns + nl)), pl.BlockSpec(memory_space=pltpu.VMEM)),
        out_shape=(sems, sems, *[pltpu.HBM(a.shape, a.dtype) for a in arrays], jax.ShapeDtypeStruct((8, LANE), F32)),
        input_output_aliases={i: 2 + i for i in range(ns + nl)},
        compiler_params=pltpu.CompilerParams(has_side_effects=_EFFECT),
    )(*[pltpu.with_memory_space_constraint(a, pltpu.HBM) for a in arrays], after)
    return res[0], res[1], list(res[2:2 + ns]), list(res[2 + ns:2 + ns + nl]), res[-1]


def _exchange_wait(mode, started, after, *, only=None, name):
    send_sems, recv_sems, srcs, lands, _ = started
    ns, nl = len(srcs), len(lands)
    copies = range(len(_EXCHANGE_BITS[mode])) if only is None else only

    def body(*refs):
        src, land = refs[:ns], refs[ns:ns + nl]
        send_sems, recv_sems = refs[ns + nl:ns + nl + 2]
        place = _place()
        for w in range(nl):
            for i in copies:
                cp = _exchange_copy(mode, src, land, w, i, place, send_sems, recv_sems, receiving=True)
                cp.wait_send()
                cp.wait_recv()

    arrays = list(srcs) + list(lands)
    res = pl.pallas_call(
        body, name=name, in_specs=[_HBM] * (ns + nl) + [_SEM, _SEM, _ANY], out_specs=[_HBM] * (ns + nl),
        out_shape=[pltpu.HBM(a.shape, a.dtype) for a in arrays],
        input_output_aliases={i: i for i in range(ns + nl)},
        compiler_params=pltpu.CompilerParams(has_side_effects=_EFFECT),
    )(*arrays, send_sems, recv_sems, after)
    return list(res[:ns]), list(res[ns:])


def _small_allreduce_adamw(gvec, wvec, mvec, vvec):
    rows, length = gvec.shape

    def body(g_ref, w_ref, m_ref, v_ref, gs_ref, d_ref, nm_ref, nv_ref, slots, send_sems, recv_sems):
        x, y, c = _place()
        me = _slot(x, y, c)
        slots[me] = g_ref[...]
        sends = []
        for k in range(1, N_DEV):
            peer = _peer(k, x, y, c)
            sends.append(pltpu.make_async_remote_copy(
                src_ref=g_ref, dst_ref=slots.at[me], send_sem=send_sems.at[k - 1], recv_sem=recv_sems.at[k - 1],
                device_id=peer, device_id_type=MESH))
        for cp in sends:
            cp.start()
        for k in range(1, N_DEV):
            peer = _peer(k, x, y, c)
            pltpu.make_async_remote_copy(
                src_ref=g_ref, dst_ref=slots.at[_slot(*peer)], send_sem=send_sems.at[k - 1], recv_sem=recv_sems.at[k - 1],
                device_id=peer, device_id_type=MESH).wait_recv()
        for cp in sends:
            cp.wait_send()
        g = slots[0]
        for s in range(1, N_DEV):
            g = g + slots[s]
        delta, nm, nv = _adamw_math(w_ref[...], g, m_ref[...], v_ref[...])
        gs_ref[...] = g
        d_ref[...] = delta
        nm_ref[...] = nm
        nv_ref[...] = nv

    vmem = pl.BlockSpec(memory_space=pltpu.VMEM)
    out = jax.ShapeDtypeStruct((rows, length), F32)
    return pl.pallas_call(
        body, in_specs=[vmem] * 4, out_specs=[vmem] * 4, out_shape=[out] * 4,
        scratch_shapes=[pltpu.VMEM((N_DEV, rows, length), F32), pltpu.SemaphoreType.DMA((N_DEV - 1,)),
                        pltpu.SemaphoreType.DMA((N_DEV - 1,))],
        name="small_allreduce_adamw",
    )(gvec, wvec, mvec, vvec)


_SMALL_SEGMENTS = (("a_log", GDN_HEADS), ("dt_bias", GDN_HEADS), ("gdn_norm_w", HEAD_DIM), ("pool_scale", GDN_WIDTH),
                   ("ln1_g", D_MODEL), ("ln1_b", D_MODEL), ("ln2_g", D_MODEL), ("ln2_b", D_MODEL),
                   ("ln3_g", D_MODEL), ("ln3_b", D_MODEL), ("conv_w", CONV_K * QKV_COLS))
_SMALL_ROWS = 8
_SMALL_LEN = -(-sum(sz for _, sz in _SMALL_SEGMENTS) // (_SMALL_ROWS * LANE)) * LANE


def _pack_small(vals):
    parts = [vals[n].reshape(-1).astype(F32) if n in vals else jnp.zeros((sz,), F32) for n, sz in _SMALL_SEGMENTS]
    flat = jnp.concatenate(parts)
    flat = jnp.pad(flat, (0, _SMALL_ROWS * _SMALL_LEN - flat.shape[0]))
    return flat.reshape(_SMALL_ROWS, _SMALL_LEN)


def _unpack_small(vec):
    flat = vec.reshape(-1)
    out, off = {}, 0
    for n, sz in _SMALL_SEGMENTS:
        out[n] = flat[off:off + sz]
        off += sz
    return out


_WEIGHT_ORDER = ("w_in", "conv_w", "a_log", "dt_bias", "gdn_norm_w", "pool_w", "pool_scale", "w_out", "ln1_g", "ln1_b",
                 "xq_w", "xk_w", "xv_w", "xo_w", "ln2_g", "ln2_b", "w_up", "w_down", "ln3_g", "ln3_b")


def _shard2d(name, a):
    if name == "w_in":
        return a.T
    return a.reshape(-1, a.shape[-1]) if name == "pool_w" else a


def _update_view(name, a):
    return jnp.transpose(a, (2, 0, 1)) if name == "w_in" else _shard2d(name, a[0])


def _shard_result(name, r, shape):
    return jnp.transpose(r, (1, 2, 0)) if name == "w_in" else r.reshape(shape)


def _gathered_to_full(name, gth):
    if name in ("w_up", "w_in"):
        return gth
    if name == "conv_w":
        return jnp.transpose(gth, (1, 0, 2)).reshape(gth.shape[1], N_DEV * gth.shape[2])
    if name == "pool_w":
        g4 = gth.reshape(N_DEV, POOL_GROUPS, POOL_GROUP_DIM // N_DEV, POOL_GROUP_DIM)
        return jnp.transpose(g4, (1, 0, 2, 3)).reshape(POOL_GROUPS, POOL_GROUP_DIM, POOL_GROUP_DIM)
    return gth.reshape(N_DEV * gth.shape[1], gth.shape[2])


def _full_to_chunks(name, full):
    if name == "w_up":
        return full
    if name == "pool_w":
        g4 = full.reshape(POOL_GROUPS, N_DEV, POOL_GROUP_DIM // N_DEV, POOL_GROUP_DIM)
        return jnp.transpose(g4, (1, 0, 2, 3)).reshape(N_DEV, POOL_GROUPS * POOL_GROUP_DIM // N_DEV, POOL_GROUP_DIM)
    return full.reshape(N_DEV, full.shape[0] // N_DEV, full.shape[1])


_GATHER_GROUPS = (("mixer", ("w_in", "conv_w", "pool_w")), ("attn", ("w_out", "xq_w", "xk_w", "xv_w", "xo_w")),
                  ("up", ("w_up",)), ("down", ("w_down",)))


def _grad_chunks(name, g):
    if name == "w_in":
        return _w_in_chunks(g.astype(BF16))
    return _full_to_chunks(name, g.astype(BF16))


def kernel(x, mem, w_in, conv_w, a_log, dt_bias, gdn_norm_w, pool_w, pool_scale, w_out, ln1_g, ln1_b, xq_w, xk_w, xv_w, xo_w, ln2_g, ln2_b, w_up, w_down, ln3_g, ln3_b, loss_target, m_w_in, m_conv_w, m_a_log, m_dt_bias, m_gdn_norm_w, m_pool_w, m_pool_scale, m_w_out, m_ln1_g, m_ln1_b, m_xq_w, m_xk_w, m_xv_w, m_xo_w, m_ln2_g, m_ln2_b, m_w_up, m_w_down, m_ln3_g, m_ln3_b, v_w_in, v_conv_w, v_a_log, v_dt_bias, v_gdn_norm_w, v_pool_w, v_pool_scale, v_w_out, v_ln1_g, v_ln1_b, v_xq_w, v_xk_w, v_xv_w, v_xo_w, v_ln2_g, v_ln2_b, v_w_up, v_w_down, v_ln3_g, v_ln3_b):
    args = dict(locals())
    wt = {n: args[n][0] for n in _WEIGHT_ORDER}
    mo = {n: args["m_" + n][0] for n in _WEIGHT_ORDER}
    vo = {n: args["v_" + n][0] for n in _WEIGHT_ORDER}

    me = _slot(*_place())
    me_arr = jnp.reshape(me, (1,)).astype(jnp.int32)
    nothing = jnp.zeros((8, LANE), F32)

    def landing_zones(names):
        shards = [_shard2d(n, wt[n]).astype(F32 if n == "conv_w" else BF16) for n in names]
        zones = [lax.dynamic_update_slice(lax.empty((N_DEV, *s.shape), s.dtype), s[None], (me, 0, 0)) for s in shards]
        return shards, zones

    chip_arr = jnp.reshape(me // 2, (1,)).astype(jnp.int32)
    core_arr = jnp.reshape(lax.axis_index("c"), (1,)).astype(jnp.int32)
    names_of = dict(_GATHER_GROUPS)
    gathers = {}
    prepared = {}

    def gather_chips(group, after):
        shards, zones = prepared.pop(group) if group in prepared else landing_zones(names_of[group])
        gathers[group] = _exchange_start("gather_chips", shards, zones, after, name="gather_chips_" + group)
        return gathers[group][4]

    def gather_pass(group, after):
        _, zones = _exchange_wait("gather_chips", gathers[group], after, name=f"gather_chips_{group}_wait")
        gathers[group] = _exchange_start("gather_pass", [], zones, nothing, name="gather_pass_" + group)
        return gathers[group][4]

    def gathered(group, after, token=None):
        _, zones = _exchange_wait("gather_pass", gathers[group], after, name=f"gather_pass_{group}_wait")
        full = {n: _gathered_to_full(n, z) for n, z in zip(names_of[group], zones)}
        full.update({n: wt[n] if token is None else wt[n] + token for n in _VECTORS})
        return _group_weights(group, full)

    token = gather_chips("mixer", nothing)
    x16 = _cast_bf16(x[0], name="cast_x")
    later = {group: landing_zones(names_of[group]) for group in ("attn", "up", "down")}
    token, x16, later = lax.optimization_barrier((token, x16, later))
    prepared.update(later)
    token = gather_chips("attn", gather_pass("mixer", token))

    def weights_of(group, after):
        if group == "mixer":
            return gathered(group, gathers["attn"][4])
        if group == "ahead":
            return gather_chips("up", gather_pass("attn", after))[0:1, 0:1]
        if group == "attn":
            return gathered(group, after)
        slots = lambda *bits: jnp.stack([jnp.bitwise_xor(me, k) for k in bits]).astype(jnp.int32)

        def wait_some(mode, after, only, name):
            send, recv, srcs, zones, token = gathers[group]
            srcs, zones = _exchange_wait(mode, gathers[group], after, only=only, name=name)
            gathers[group] = (send, recv, srcs, zones, token)

        def pass_on():
            gathers[group] = _exchange_start("gather_pass", [], gathers[group][3], nothing, name="gather_pass_" + group)

        def passed(after):
            _, zones = _exchange_wait("gather_pass", gathers[group], after, name=f"gather_pass_{group}_wait")
            return slots(3, 5, 7), zones[0]

        if group == "up":
            def crossed(after):
                wait_some("gather_chips", after, None, "gather_chips_up_wait")
                pass_on()
                gather_chips("down", gathers[group][4])
                return slots(0, 1, 2, 4, 6), gathers[group][3][0]
            return [crossed, passed]

        norms = {n: wt[n].reshape(1, D_MODEL) for n in _GROUP_VECTORS[group]}

        def on_chip(after):
            wait_some("gather_chips", after, [0], "gather_chips_down_wait_sibling")
            return slots(0, 1), gathers[group][3][0], norms

        def crossed(after):
            wait_some("gather_chips", after, [1, 2, 3], "gather_chips_down_wait")
            pass_on()
            return slots(2, 4, 6), gathers[group][3][0], norms

        return [on_chip, crossed, lambda after: (*passed(after), norms)]

    scatters = {}
    in_flight = []

    def chip_stage(after):
        group, names, started = in_flight.pop()
        chunks, from_sibling = _exchange_wait("scatter_sibling", started, after, name=f"scatter_sibling_{group}_wait")
        sums = [_chip_sums(c, f, core_arr, name=f"chip_sums_{n}") for n, c, f in zip(names, chunks, from_sibling)]
        scatters[group] = (names, _exchange_start("scatter_chips", sums, [lax.empty(s.shape, s.dtype) for s in sums],
                                                  nothing, name="scatter_chips_" + group))
        return scatters[group][1][4]

    def grads_ready(group, grads):
        if group == "tick":
            return chip_stage(grads["after"])[0:1, 0:1] if in_flight else None
        names = tuple(grads)
        chunks = [_grad_chunks(n, grads[n]) for n in names]
        token = chip_stage(chunks[0]) if in_flight else nothing
        zones = [lax.empty((N_CHIPS, *c.shape[1:]), c.dtype) for c in chunks]
        started = _exchange_start("scatter_sibling", chunks, zones, token, name="scatter_sibling_" + group)
        in_flight.append((group, names, started))
        return started[4][0:1, 0:1]

    sq, grad_x, g = _local_step(x[0], x16, mem[0], loss_target[0], weights_of, grads_ready)
    small = _finish_small_grads(g)

    out = {}
    after = chip_stage(grad_x)
    for group, (names, started) in scatters.items():
        sums, lands = _exchange_wait("scatter_chips", started, after, name=f"scatter_chips_{group}_wait")
        for n, parts, own in zip(names, lands, sums):
            res = _adamw_shard(parts, own, chip_arr, _update_view(n, args[n]), _update_view(n, args["m_" + n]),
                               _update_view(n, args["v_" + n]), name="adamw_" + n)
            out[n] = [_shard_result(n, r, args[n].shape) for r in res]
            after = res[1]

    packed, _ = lax.optimization_barrier((_pack_small(small), after))
    gs, ds, ms, vs = _small_allreduce_adamw(
        packed, _pack_small({n: wt[n] for n in _VECTORS}), _pack_small({n: mo[n] for n in _VECTORS}),
        _pack_small({n: vo[n] for n in _VECTORS}))
    gs, ds, ms, vs = _unpack_small(gs), _unpack_small(ds), _unpack_small(ms), _unpack_small(vs)
    cols = conv_w.shape[-1]
    conv_full = gs["conv_w"].reshape(CONV_K, QKV_COLS)
    conv_mine = lax.dynamic_slice(conv_full, (0, me * cols), (CONV_K, cols))[None]
    res = _adamw_shard(conv_mine, conv_mine, jnp.zeros((1,), jnp.int32), wt["conv_w"], mo["conv_w"], vo["conv_w"],
                       name="adamw_conv_w")
    out["conv_w"] = [r.reshape(conv_w.shape) for r in res]
    for n in _VECTORS:
        out[n] = [t[n].reshape(args[n].shape) for t in (gs, ds, ms, vs)]

    loss = lax.psum(0.5 * sq[0, 0] / D_MODEL, ("x", "y", "c"))
    return (loss, grad_x[None], *[out[n][0] for n in _WEIGHT_ORDER], *[out[n][1] for n in _WEIGHT_ORDER],
            *[out[n][2] for n in _WEIGHT_ORDER], *[out[n][3] for n in _WEIGHT_ORDER])
```

```python
import functools
import math

import jax
import jax.numpy as jnp
from jax import lax
from jax.experimental import pallas as pl
from jax.experimental.pallas import tpu as pltpu

F32 = jnp.float32
BF16 = jnp.bfloat16
MESH = pl.DeviceIdType.MESH

N_DEV = 8
D_MODEL = 2048
GDN_WIDTH = 1024
GDN_HEADS = 8
HEAD_DIM = 128
CONV_K = 4
CHUNK = 64
POOL_GROUPS = 4
POOL_GROUP_DIM = 256
MEM_LEN = 256
XATTN_HEADS = 4
XATTN_HEAD_DIM = 512
D_FF = 8192
IN_COLS = 5136
ALPHA = 2.0 ** 0.25
LN_EPS = 1e-5
NORM_EPS = 1e-6

LANE = 128
QKV_COLS = 3 * GDN_WIDTH
Z_OFF = QKV_COLS
BA_OFF = 4 * GDN_WIDTH
POOL_OFF = BA_OFF + 2 * LANE
PROJ_COLS = POOL_OFF + GDN_WIDTH
Z_BLK = Z_OFF // LANE
BA_BLK = BA_OFF // LANE
POOL_BLK = POOL_OFF // POOL_GROUP_DIM

ADAM_LR = 0.001
ADAM_B1 = 0.9
ADAM_B2 = 0.999
ADAM_EPS = 1e-08
ADAM_WD = 0.01
ADAM_STEP = 10

VMEM_LIMIT_BYTES = 48 * 1024 * 1024


def _params(*sem):
    return pltpu.CompilerParams(dimension_semantics=sem if sem else None, vmem_limit_bytes=VMEM_LIMIT_BYTES)


def _make_dots(cast, precision, batched=False):
    lead = 1 if batched else 0
    batch = ((0,), (0,)) if batched else ((), ())

    def dg(a, b, ca, cb):
        if cast is not None:
            a = a.astype(cast)
            b = b.astype(cast)
        return lax.dot_general(a, b, (((ca + lead,), (cb + lead,)), batch), precision=precision, preferred_element_type=F32)

    def nn_(a, b):
        return dg(a, b, 1, 0)

    def nt_(a, b):
        return dg(a, b, 1, 1)

    def tn_(a, b):
        return dg(a, b, 0, 0)

    @jax.custom_vjp
    def nn(a, b):
        return nn_(a, b)

    nn.defvjp(lambda a, b: (nn_(a, b), (a, b)), lambda r, g: (nt_(g, r[1]), tn_(r[0], g)))

    @jax.custom_vjp
    def nt(a, b):
        return nt_(a, b)

    nt.defvjp(lambda a, b: (nt_(a, b), (a, b)), lambda r, g: (nn_(g, r[1]), tn_(g, r[0])))

    @jax.custom_vjp
    def tn(a, b):
        return tn_(a, b)

    tn.defvjp(lambda a, b: (tn_(a, b), (a, b)), lambda r, g: (nt_(r[1], g), nn_(r[0], g)))

    return (nn_, nt_, tn_), (nn, nt, tn)


_BDOT_PLAIN, _BDOT_VJP = _make_dots(BF16, None)
_BDOT_BATCH_PLAIN, _BDOT_BATCH_VJP = _make_dots(BF16, None, batched=True)
_FDOT_BATCH_PLAIN, _FDOT_BATCH_VJP = _make_dots(BF16, None, batched=True)


def _mm(a, b, *, ta=False, tb=False, out_dtype=F32, tm=None, tn=512, tk=None, epi=None, extra=None, add_scale=1.0,
        b_chunks=False, o_chunks=False, name):
    m, k = (a.shape[1], a.shape[0]) if ta else a.shape
    if b_chunks:
        n, kb = (b.shape[1], N_DEV * b.shape[2]) if tb else (N_DEV * b.shape[2], b.shape[1])
    else:
        n, kb = b.shape if tb else (b.shape[1], b.shape[0])
    assert kb == k, (name, a.shape, b.shape)
    tm, tn, tk = min(tm or m, m), min(tn, n), min(tk or k, k)
    assert m % tm == 0 and n % tn == 0 and k % tk == 0, (name, m, n, k)
    nk = k // tk
    dims = (((0 if ta else 1,), (1 if tb else 0,)), ((), ()))
    n_extra = 0 if epi in (None, "relu2") else 1
    n_out = 2 if epi == "relu2" else 1
    if epi in ("relu2", "mul2r"):
        out_dtype = BF16

    def body(*refs):
        a_ref, b_ref = refs[:2]
        c_ref = refs[2] if n_extra else None
        o_refs = refs[2 + n_extra:2 + n_extra + n_out]
        scr = refs[2 + n_extra + n_out:]
        r = lax.dot_general(a_ref[...].astype(BF16), b_ref[...].astype(BF16), dims, preferred_element_type=F32)

        def finish(v):
            if epi == "add":
                o_refs[0][...] = (v + add_scale * c_ref[...]).astype(out_dtype)
            elif epi == "relu2":
                p = jnp.maximum(v, 0.0)
                o_refs[0][...] = (p * p).astype(BF16)
                o_refs[1][...] = p.astype(BF16)
            elif epi == "mul2r":
                o_refs[0][...] = (v * (2.0 * c_ref[...].astype(F32))).astype(BF16)
            else:
                o_refs[0][...] = v.astype(out_dtype)

        if nk == 1:
            finish(r)
        else:
            acc = scr[0]
            kk = pl.program_id(2)

            @pl.when(kk == 0)
            def _():
                acc[...] = r

            @pl.when(kk > 0)
            def _():
                acc[...] += r

            @pl.when(kk == nk - 1)
            def _():
                finish(acc[...])

    a_spec = pl.BlockSpec((tk, tm), lambda i, j, kk: (kk, i)) if ta else pl.BlockSpec((tm, tk), lambda i, j, kk: (i, kk))
    if b_chunks and tb:
        kc = k // N_DEV // tk
        b_spec = pl.BlockSpec((None, tn, tk), lambda i, j, kk: (kk // kc, j, kk % kc))
    elif b_chunks:
        nc = n // N_DEV // tn
        b_spec = pl.BlockSpec((None, tk, tn), lambda i, j, kk: (j // nc, kk, j % nc))
    elif tb:
        b_spec = pl.BlockSpec((tn, tk), lambda i, j, kk: (j, kk))
    else:
        b_spec = pl.BlockSpec((tk, tn), lambda i, j, kk: (kk, j))
    mn_spec = pl.BlockSpec((tm, tn), lambda i, j, kk: (i, j))
    if o_chunks:
        oc = n // N_DEV // tn
        o_spec = pl.BlockSpec((None, tm, tn), lambda i, j, kk: (j // oc, i, j % oc))
        o_shape = jax.ShapeDtypeStruct((N_DEV, m, n // N_DEV), out_dtype)
    else:
        o_spec, o_shape = mn_spec, jax.ShapeDtypeStruct((m, n), out_dtype)
    res = pl.pallas_call(
        body, grid=(m // tm, n // tn, nk), in_specs=[a_spec, b_spec] + [mn_spec] * n_extra,
        out_specs=[o_spec] * n_out, out_shape=[o_shape] * n_out,
        scratch_shapes=[pltpu.VMEM((tm, tn), F32)] if nk > 1 else [],
        compiler_params=_params("parallel", "parallel", "arbitrary"), name=name,
    )(a, b, *([extra] if n_extra else []))
    return res if n_out > 1 else res[0]


def _cast_bf16(v, *, name, tm=512):
    t, d = v.shape
    tm = min(tm, t)

    def body(v_ref, o_ref):
        o_ref[...] = v_ref[...].astype(BF16)

    spec = pl.BlockSpec((tm, d), lambda i: (i, 0))
    return pl.pallas_call(body, grid=(t // tm,), in_specs=[spec], out_specs=spec,
                          out_shape=jax.ShapeDtypeStruct((t, d), BF16), compiler_params=_params("parallel"), name=name)(v)


def _shift_down(v, s):
    if s == 0:
        return v
    row = lax.broadcasted_iota(jnp.int32, v.shape, 0)
    return jnp.where(row >= s, pltpu.roll(v, s, axis=0), 0.0)


def _shift_up(v, s):
    if s == 0:
        return v
    t = v.shape[0]
    row = lax.broadcasted_iota(jnp.int32, v.shape, 0)
    return jnp.where(row < t - s, pltpu.roll(v, t - s, axis=0), 0.0)


def _post_col(j):
    return (j % GDN_HEADS) * 3 + j // GDN_HEADS


def _gdn_prep_fwd(proj, conv_w):
    t = proj.shape[0]

    def body(x_ref, w_ref, o_ref):
        j = pl.program_id(0)
        x = x_ref[...]
        y = jnp.zeros_like(x)
        for tap in range(CONV_K):
            y = y + w_ref[tap:tap + 1, :] * _shift_down(x, CONV_K - 1 - tap)
        c = y * jax.nn.sigmoid(y)
        nrm = c * lax.rsqrt(jnp.sum(c * c, axis=1, keepdims=True) + NORM_EPS)
        o_ref[...] = jnp.where(j < 2 * GDN_HEADS, nrm, c)

    return pl.pallas_call(
        body, grid=(QKV_COLS // LANE,),
        in_specs=[pl.BlockSpec((t, LANE), lambda j: (0, j)), pl.BlockSpec((CONV_K, LANE), lambda j: (0, j))],
        out_specs=pl.BlockSpec((t, LANE), lambda j: (0, _post_col(j))),
        out_shape=jax.ShapeDtypeStruct((t, QKV_COLS), F32),
        compiler_params=_params("parallel"), name="gdn_prep_fwd",
    )(proj, conv_w)


def _gdn_prep_bwd(proj, conv_w, dpost, dproj):
    t = proj.shape[0]

    def body(x_ref, w_ref, d_ref, _, dx_ref, dw_ref):
        j = pl.program_id(0)
        x = x_ref[...]
        xs = [_shift_down(x, CONV_K - 1 - tap) for tap in range(CONV_K)]
        y = jnp.zeros_like(x)
        for tap in range(CONV_K):
            y = y + w_ref[tap:tap + 1, :] * xs[tap]
        sig = jax.nn.sigmoid(y)
        c = y * sig
        r = lax.rsqrt(jnp.sum(c * c, axis=1, keepdims=True) + NORM_EPS)
        nrm = c * r
        d = d_ref[...]
        dc_norm = r * (d - nrm * jnp.sum(d * nrm, axis=1, keepdims=True))
        dc = jnp.where(j < 2 * GDN_HEADS, dc_norm, d)
        dy = dc * (sig * (1.0 + y * (1.0 - sig)))
        dx = jnp.zeros_like(x)
        for tap in range(CONV_K):
            dx = dx + _shift_up(w_ref[tap:tap + 1, :] * dy, CONV_K - 1 - tap)
            dw_ref[tap:tap + 1, :] = jnp.sum(dy * xs[tap], axis=0, keepdims=True)
        dx_ref[...] = dx.astype(dx_ref.dtype)

    return pl.pallas_call(
        body, grid=(QKV_COLS // LANE,),
        in_specs=[pl.BlockSpec((t, LANE), lambda j: (0, j)), pl.BlockSpec((CONV_K, LANE), lambda j: (0, j)),
                  pl.BlockSpec((t, LANE), lambda j: (0, _post_col(j))), pl.BlockSpec(memory_space=pl.ANY)],
        out_specs=[pl.BlockSpec((t, LANE), lambda j: (0, j)), pl.BlockSpec((CONV_K, LANE), lambda j: (0, j))],
        out_shape=[jax.ShapeDtypeStruct(dproj.shape, dproj.dtype), jax.ShapeDtypeStruct((CONV_K, QKV_COLS), F32)],
        input_output_aliases={3: 0},
        compiler_params=_params("parallel"), name="gdn_prep_bwd",
    )(proj, conv_w, dpost, dproj)


def _softplus(v):
    return jnp.maximum(v, 0.0) + jnp.log(1.0 + jnp.exp(-jnp.abs(v)))


def _tri_inv(low, nn):
    r = lax.broadcasted_iota(jnp.int32, (CHUNK, CHUNK), 0)
    c = lax.broadcasted_iota(jnp.int32, (CHUNK, CHUNK), 1)
    eye = (r == c).astype(F32)
    same_blk = lax.shift_right_logical(r, 4) == lax.shift_right_logical(c, 4)
    diag = jnp.where(same_blk, low, 0.0)
    off = low - diag
    n1 = -diag
    n2 = nn(n1, n1)
    n4 = nn(n2, n2)
    n8 = nn(n4, n4)
    inv_d = nn(nn(nn(eye + n1, eye + n2), eye + n4), eye + n8)
    m1 = nn(inv_d, off)
    m2 = nn(m1, m1)
    return nn(nn(eye - m1, eye + m2), inv_d)


@jax.custom_vjp
def _tri_inv_known(low, t_inv):
    return t_inv


def _tri_inv_known_fwd(low, t_inv):
    return t_inv, t_inv


def _tri_inv_known_bwd(t_inv, g):
    _, nt, tn = _FDOT_BATCH_PLAIN
    return -nt(tn(t_inv, g), t_inv), jnp.zeros_like(t_inv)


_tri_inv_known.defvjp(_tri_inv_known_fwd, _tri_inv_known_bwd)


LOCAL_HEADS_PER_STEP = 8


def _gdn_local_fn(qkv, ba, alog_row, dtb_row, first_head, bdots, fdots, t_known=None):
    nn, nt, tn = bdots
    fnn = fdots[0]
    n_heads = qkv.shape[1] // (3 * HEAD_DIM)
    part = lambda i, p: qkv[:, (3 * i + p) * HEAD_DIM:(3 * i + p + 1) * HEAD_DIM]
    q = jnp.stack([part(i, 0) for i in range(n_heads)]) * (HEAD_DIM ** -0.5)
    k = jnp.stack([part(i, 1) for i in range(n_heads)])
    v = jnp.stack([part(i, 2) for i in range(n_heads)])
    lane = lax.broadcasted_iota(jnp.int32, ba.shape, 1)
    bg = jnp.where(lane < GDN_HEADS, jax.nn.sigmoid(ba), -jnp.exp(alog_row) * _softplus(ba + dtb_row))
    pick = lambda l: jnp.sum(jnp.where(lane == l, bg, 0.0), axis=1, keepdims=True)
    beta = jnp.stack([pick(first_head + i) for i in range(n_heads)])
    g = jnp.stack([pick(first_head + i + GDN_HEADS) for i in range(n_heads)])

    r = lax.broadcasted_iota(jnp.int32, (CHUNK, CHUNK), 0)
    c = lax.broadcasted_iota(jnp.int32, (CHUNK, CHUNK), 1)
    incl = r >= c
    strict = r > c
    eye = r == c

    def to_row(col):
        return jnp.sum(jnp.where(eye, col, 0.0), axis=1, keepdims=True)

    gc = jnp.sum(jnp.where(incl, to_row(g), 0.0), axis=2, keepdims=True)
    diff = gc - to_row(gc)
    decay = jnp.where(incl, jnp.exp(jnp.where(incl, diff, 0.0)), 0.0)
    k_beta = k * beta
    v_beta = v * beta
    low = jnp.where(strict, nt(k_beta, k) * decay, 0.0)
    t_inv = _tri_inv(low, fnn) if t_known is None else _tri_inv_known(low, t_known)
    eg = jnp.exp(gc)
    u = fnn(t_inv, v_beta)
    w = fnn(t_inv, k_beta * eg)
    attn = jnp.where(incl, nt(q, k) * decay, 0.0)
    last = lax.broadcasted_iota(jnp.int32, (CHUNK, 1), 0) == CHUNK - 1
    g_last = jnp.sum(jnp.where(last, gc, 0.0), axis=1, keepdims=True)
    kdec = k * jnp.exp(g_last - gc)
    elast = jnp.broadcast_to(jnp.exp(g_last), (n_heads, 1, LANE))
    return u, w, q * eg, kdec, attn, elast, t_inv


def _gdn_state_fn(u, w, qg, kdec, attn, elast, state, bdots):
    nn, _, tn = bdots
    v_new = u - nn(w, state)
    o = nn(qg, state) + nn(attn, v_new)
    return o, state * elast + tn(kdec, v_new)


def _gdn_local_fwd(post, proj, alog_row, dtb_row):
    t = post.shape[0]
    n_chunks = t // CHUNK
    hb = LOCAL_HEADS_PER_STEP

    def body(qkv_ref, ba_ref, al_ref, dt_ref, u_ref, w_ref, qg_ref, kd_ref, at_ref, el_ref, ti_ref):
        u, w, qg, kdec, attn, elast, t_inv = _gdn_local_fn(qkv_ref[...], ba_ref[...], al_ref[...], dt_ref[...],
                                                           pl.program_id(1) * hb, _BDOT_BATCH_PLAIN, _FDOT_BATCH_PLAIN)
        for i in range(hb):
            cols = slice(i * HEAD_DIM, (i + 1) * HEAD_DIM)
            u_ref[:, cols] = u[i]
            w_ref[:, cols] = w[i].astype(BF16)
            qg_ref[:, cols] = qg[i].astype(BF16)
            kd_ref[:, cols] = kdec[i].astype(BF16)
        at_ref[...] = attn.astype(BF16)
        el_ref[:, 0] = elast
        ti_ref[...] = t_inv

    wide = pl.BlockSpec((CHUNK, hb * HEAD_DIM), lambda n, j: (n, j))
    square = pl.BlockSpec((hb, CHUNK, CHUNK), lambda n, j: (j, n, 0))
    row = pl.BlockSpec((1, LANE), lambda n, j: (0, 0))
    res = pl.pallas_call(
        body, grid=(n_chunks, GDN_HEADS // hb),
        in_specs=[pl.BlockSpec((CHUNK, hb * 3 * HEAD_DIM), lambda n, j: (n, j)),
                  pl.BlockSpec((CHUNK, LANE), lambda n, j: (n, BA_BLK)), row, row],
        out_specs=[wide, wide, wide, wide, square, pl.BlockSpec((hb, 1, 1, LANE), lambda n, j: (j, n, 0, 0)), square],
        out_shape=[jax.ShapeDtypeStruct((t, GDN_WIDTH), F32), jax.ShapeDtypeStruct((t, GDN_WIDTH), BF16),
                   jax.ShapeDtypeStruct((t, GDN_WIDTH), BF16), jax.ShapeDtypeStruct((t, GDN_WIDTH), BF16),
                   jax.ShapeDtypeStruct((GDN_HEADS, t, CHUNK), BF16),
                   jax.ShapeDtypeStruct((GDN_HEADS, n_chunks, 1, LANE), F32),
                   jax.ShapeDtypeStruct((GDN_HEADS, t, CHUNK), F32)],
        compiler_params=_params("parallel", "parallel"), name="gdn_local_fwd",
    )(post, proj, alog_row, dtb_row)
    return tuple(res[:6]), res[6]


def _by_head(ref):
    return jnp.stack([ref[:, h * HEAD_DIM:(h + 1) * HEAD_DIM] for h in range(ref.shape[1] // HEAD_DIM)])


def _gdn_state_specs(n_of):
    wide = pl.BlockSpec((CHUNK, GDN_WIDTH), lambda n: (n_of(n), 0))
    attn = pl.BlockSpec((GDN_HEADS, CHUNK, CHUNK), lambda n: (0, n_of(n), 0))
    elast = pl.BlockSpec((GDN_HEADS, 1, 1, LANE), lambda n: (0, n_of(n), 0, 0))
    saved = pl.BlockSpec((GDN_HEADS, 1, HEAD_DIM, HEAD_DIM), lambda n: (0, n_of(n), 0, 0))
    return wide, attn, elast, saved


def _gdn_state_fwd(u, w, qg, kdec, attn, elast):
    t = u.shape[0]
    n_chunks = t // CHUNK

    def body(u_ref, w_ref, qg_ref, kd_ref, at_ref, el_ref, o_ref, save_ref, state_ref):
        @pl.when(pl.program_id(0) == 0)
        def _():
            state_ref[...] = jnp.zeros_like(state_ref)

        state = state_ref[...]
        save_ref[:, 0] = state
        o, new_state = _gdn_state_fn(_by_head(u_ref), _by_head(w_ref), _by_head(qg_ref), _by_head(kd_ref), at_ref[...],
                                     el_ref[:, 0], state, _BDOT_BATCH_PLAIN)
        for h in range(GDN_HEADS):
            o_ref[:, h * HEAD_DIM:(h + 1) * HEAD_DIM] = o[h]
        state_ref[...] = new_state

    wide, attn_spec, elast_spec, saved_spec = _gdn_state_specs(lambda n: n)
    return pl.pallas_call(
        body, grid=(n_chunks,), in_specs=[wide, wide, wide, wide, attn_spec, elast_spec],
        out_specs=[wide, saved_spec],
        out_shape=[jax.ShapeDtypeStruct((t, GDN_WIDTH), F32),
                   jax.ShapeDtypeStruct((GDN_HEADS, n_chunks, HEAD_DIM, HEAD_DIM), F32)],
        scratch_shapes=[pltpu.VMEM((GDN_HEADS, HEAD_DIM, HEAD_DIM), F32)],
        compiler_params=_params("arbitrary"), name="gdn_state_fwd",
    )(u, w, qg, kdec, attn, elast)


def _gdn_state_bwd(u, w, qg, kdec, attn, elast, saved, do):
    t = u.shape[0]
    n_chunks = t // CHUNK
    last = n_chunks - 1

    def body(u_ref, w_ref, qg_ref, kd_ref, at_ref, el_ref, save_ref, do_ref,
             du_ref, dw_ref, dqg_ref, dkd_ref, dat_ref, del_ref, dstate_ref):
        @pl.when(pl.program_id(0) == 0)
        def _():
            dstate_ref[...] = jnp.zeros_like(dstate_ref)

        _, vjp = jax.vjp(
            lambda *a: _gdn_state_fn(*a, _BDOT_BATCH_VJP), _by_head(u_ref), _by_head(w_ref).astype(F32),
            _by_head(qg_ref).astype(F32), _by_head(kd_ref).astype(F32), at_ref[...].astype(F32), el_ref[:, 0],
            save_ref[:, 0])
        du, dw, dqg, dkd, dat, de, dstate = vjp((_by_head(do_ref), dstate_ref[...]))
        for h in range(GDN_HEADS):
            cols = slice(h * HEAD_DIM, (h + 1) * HEAD_DIM)
            du_ref[:, cols] = du[h]
            dw_ref[:, cols] = dw[h]
            dqg_ref[:, cols] = dqg[h]
            dkd_ref[:, cols] = dkd[h]
        dat_ref[...] = dat
        del_ref[:, 0] = de
        dstate_ref[...] = dstate

    wide, attn_spec, elast_spec, saved_spec = _gdn_state_specs(lambda n: last - n)
    wide_f32 = jax.ShapeDtypeStruct((t, GDN_WIDTH), F32)
    return pl.pallas_call(
        body, grid=(n_chunks,), in_specs=[wide, wide, wide, wide, attn_spec, elast_spec, saved_spec, wide],
        out_specs=[wide, wide, wide, wide, attn_spec, elast_spec],
        out_shape=[wide_f32, wide_f32, wide_f32, wide_f32, jax.ShapeDtypeStruct((GDN_HEADS, t, CHUNK), F32),
                   jax.ShapeDtypeStruct((GDN_HEADS, n_chunks, 1, LANE), F32)],
        scratch_shapes=[pltpu.VMEM((GDN_HEADS, HEAD_DIM, HEAD_DIM), F32)],
        compiler_params=_params("arbitrary"), name="gdn_state_bwd",
    )(u, w, qg, kdec, attn, elast, saved, do)


def _gdn_local_bwd(post, proj, alog_row, dtb_row, t_inv, cots, dproj):
    t = post.shape[0]
    n_chunks = t // CHUNK
    hb = LOCAL_HEADS_PER_STEP
    n_steps = GDN_HEADS // hb

    def body(qkv_ref, ba_ref, al_ref, dt_ref, ti_ref, du_ref, dw_ref, dqg_ref, dkd_ref, dat_ref, del_ref, _,
             dqkv_ref, dba_ref, dal_ref, ddt_ref, dba_acc):
        n = pl.program_id(0)
        j = pl.program_id(1)

        @pl.when((n == 0) & (j == 0))
        def _():
            dal_ref[...] = jnp.zeros_like(dal_ref)
            ddt_ref[...] = jnp.zeros_like(ddt_ref)

        @pl.when(j == 0)
        def _():
            dba_acc[...] = jnp.zeros_like(dba_acc)

        t_known = ti_ref[...]
        _, vjp = jax.vjp(
            lambda a, b, c, d: _gdn_local_fn(a, b, c, d, j * hb, _BDOT_BATCH_VJP, _FDOT_BATCH_VJP, t_known)[:6],
            qkv_ref[...], ba_ref[...], al_ref[...], dt_ref[...])
        dqkv, dba, dal, ddt = vjp((_by_head(du_ref), _by_head(dw_ref), _by_head(dqg_ref), _by_head(dkd_ref), dat_ref[...],
                                   del_ref[:, 0]))
        dqkv_ref[...] = dqkv
        dba_acc[...] += dba
        dal_ref[...] += dal
        ddt_ref[...] += ddt

        @pl.when(j == n_steps - 1)
        def _():
            dba_ref[:, 0:LANE] = dba_acc[...].astype(dba_ref.dtype)
            dba_ref[:, LANE:2 * LANE] = jnp.zeros((CHUNK, LANE), dba_ref.dtype)

    wide = pl.BlockSpec((CHUNK, hb * HEAD_DIM), lambda n, j: (n, j))
    qkv_spec = pl.BlockSpec((CHUNK, hb * 3 * HEAD_DIM), lambda n, j: (n, j))
    row = pl.BlockSpec((1, LANE), lambda n, j: (0, 0))
    return pl.pallas_call(
        body, grid=(n_chunks, n_steps),
        in_specs=[qkv_spec, pl.BlockSpec((CHUNK, LANE), lambda n, j: (n, BA_BLK)), row, row,
                  pl.BlockSpec((hb, CHUNK, CHUNK), lambda n, j: (j, n, 0)), wide, wide, wide, wide,
                  pl.BlockSpec((hb, CHUNK, CHUNK), lambda n, j: (j, n, 0)),
                  pl.BlockSpec((hb, 1, 1, LANE), lambda n, j: (j, n, 0, 0)), pl.BlockSpec(memory_space=pl.ANY)],
        out_specs=[qkv_spec, pl.BlockSpec((CHUNK, 2 * LANE), lambda n, j: (n, BA_BLK // 2)), row, row],
        out_shape=[jax.ShapeDtypeStruct((t, QKV_COLS), F32), jax.ShapeDtypeStruct(dproj.shape, dproj.dtype),
                   jax.ShapeDtypeStruct((1, LANE), F32), jax.ShapeDtypeStruct((1, LANE), F32)],
        input_output_aliases={11: 1},
        scratch_shapes=[pltpu.VMEM((CHUNK, LANE), F32)],
        compiler_params=_params("arbitrary", "arbitrary"), name="gdn_local_bwd",
    )(post, proj, alog_row, dtb_row, t_inv, *cots, dproj)


def _onorm_fn(o, z, w):
    return o * lax.rsqrt(jnp.mean(o * o, axis=1, keepdims=True) + NORM_EPS) * w * (z * jax.nn.sigmoid(z))


def _onorm_fwd(o_raw, proj, norm_w, mixin, tm=512):
    t = o_raw.shape[0]
    tm = min(tm, t)

    def body(o_ref, z_ref, w_ref, _, out_ref):
        out_ref[...] = _onorm_fn(o_ref[...], z_ref[...], w_ref[...]).astype(out_ref.dtype)

    return pl.pallas_call(
        body, grid=(t // tm, GDN_HEADS),
        in_specs=[pl.BlockSpec((tm, LANE), lambda i, h: (i, h)), pl.BlockSpec((tm, LANE), lambda i, h: (i, Z_BLK + h)),
                  pl.BlockSpec((1, LANE), lambda i, h: (0, 0)), pl.BlockSpec(memory_space=pl.ANY)],
        out_specs=pl.BlockSpec((tm, LANE), lambda i, h: (i, h)),
        out_shape=jax.ShapeDtypeStruct(mixin.shape, mixin.dtype), input_output_aliases={3: 0},
        compiler_params=_params("parallel", "parallel"), name="gdn_onorm_fwd",
    )(o_raw, proj, norm_w, mixin)


def _onorm_bwd(o_raw, proj, norm_w, dmixin, dproj, tm=512):
    t = o_raw.shape[0]
    tm = min(tm, t)

    def body(o_ref, z_ref, w_ref, d_ref, _, do_ref, dz_ref, dw_ref):
        @pl.when((pl.program_id(0) == 0) & (pl.program_id(1) == 0))
        def _():
            dw_ref[...] = jnp.zeros_like(dw_ref)

        _, vjp = jax.vjp(_onorm_fn, o_ref[...], z_ref[...], w_ref[...])
        do, dz, dw = vjp(d_ref[...])
        do_ref[...] = do
        dz_ref[...] = dz.astype(dz_ref.dtype)
        dw_ref[...] += dw

    return pl.pallas_call(
        body, grid=(t // tm, GDN_HEADS),
        in_specs=[pl.BlockSpec((tm, LANE), lambda i, h: (i, h)), pl.BlockSpec((tm, LANE), lambda i, h: (i, Z_BLK + h)),
                  pl.BlockSpec((1, LANE), lambda i, h: (0, 0)), pl.BlockSpec((tm, LANE), lambda i, h: (i, h)),
                  pl.BlockSpec(memory_space=pl.ANY)],
        out_specs=[pl.BlockSpec((tm, LANE), lambda i, h: (i, h)), pl.BlockSpec((tm, LANE), lambda i, h: (i, Z_BLK + h)),
                   pl.BlockSpec((1, LANE), lambda i, h: (0, 0))],
        out_shape=[jax.ShapeDtypeStruct((t, GDN_WIDTH), F32), jax.ShapeDtypeStruct(dproj.shape, dproj.dtype),
                   jax.ShapeDtypeStruct((1, LANE), F32)],
        input_output_aliases={4: 1},
        compiler_params=_params("arbitrary", "arbitrary"), name="gdn_onorm_bwd",
    )(o_raw, proj, norm_w, dmixin, dproj)


def _pool_select(levels, gi):
    out = levels[-1]
    for lvl in range(len(levels) - 2, -1, -1):
        out = jnp.where(gi == lvl, levels[lvl], out)
    return out


def _pool_count(shape, gi):
    pos = lax.broadcasted_iota(jnp.int32, shape, 0)
    win = lax.shift_left(jnp.int32(2), gi)
    return jnp.minimum(pos + 1, win).astype(F32)


def _pooled(p, gi):
    acc = p
    levels = []
    for lvl in range(POOL_GROUPS):
        acc = acc + _shift_down(acc, 1 << lvl)
        levels.append(acc)
    return _pool_select(levels, gi) / _pool_count(p.shape, gi) - p


def _pool_fwd(proj, pool_w, pool_scale):
    t = proj.shape[0]

    def body(p_ref, w_ref, s_ref, out_ref):
        gi = pl.program_id(0)
        pooled = _pooled(p_ref[...], gi)
        out_ref[...] = (_BDOT_PLAIN[0](pooled, w_ref[0]) * s_ref[0]).astype(out_ref.dtype)

    return pl.pallas_call(
        body, grid=(POOL_GROUPS,),
        in_specs=[pl.BlockSpec((t, POOL_GROUP_DIM), lambda g: (0, POOL_BLK + g)),
                  pl.BlockSpec((1, POOL_GROUP_DIM, POOL_GROUP_DIM), lambda g: (g, 0, 0)),
                  pl.BlockSpec((1, 1, POOL_GROUP_DIM), lambda g: (g, 0, 0))],
        out_specs=pl.BlockSpec((t, POOL_GROUP_DIM), lambda g: (0, GDN_WIDTH // POOL_GROUP_DIM + g)),
        out_shape=jax.ShapeDtypeStruct((t, 2 * GDN_WIDTH), BF16),
        compiler_params=_params("parallel"), name="pool_fwd",
    )(proj, pool_w, pool_scale)


def _pool_bwd(proj, pool_w, pool_scale, dmixin):
    t = proj.shape[0]
    nn, nt, tn = _BDOT_PLAIN

    def body(p_ref, w_ref, s_ref, d_ref, dp_ref, dw_ref, ds_ref):
        gi = pl.program_id(0)
        p = p_ref[...]
        pooled = _pooled(p, gi)
        mixed = nn(pooled, w_ref[0])
        d = d_ref[...]
        ds_ref[0] = jnp.sum(d * mixed, axis=0, keepdims=True)
        dmixed = d * s_ref[0]
        dw_ref[0] = tn(pooled, dmixed)
        dpooled = nt(dmixed, w_ref[0])
        acc = dpooled / _pool_count(p.shape, gi)
        levels = []
        for lvl in range(POOL_GROUPS):
            acc = acc + _shift_up(acc, 1 << lvl)
            levels.append(acc)
        dp_ref[...] = (_pool_select(levels, gi) - dpooled).astype(dp_ref.dtype)

    return pl.pallas_call(
        body, grid=(POOL_GROUPS,),
        in_specs=[pl.BlockSpec((t, POOL_GROUP_DIM), lambda g: (0, POOL_BLK + g)),
                  pl.BlockSpec((1, POOL_GROUP_DIM, POOL_GROUP_DIM), lambda g: (g, 0, 0)),
                  pl.BlockSpec((1, 1, POOL_GROUP_DIM), lambda g: (g, 0, 0)),
                  pl.BlockSpec((t, POOL_GROUP_DIM), lambda g: (0, GDN_WIDTH // POOL_GROUP_DIM + g))],
        out_specs=[pl.BlockSpec((t, POOL_GROUP_DIM), lambda g: (0, POOL_BLK + g)),
                   pl.BlockSpec((1, POOL_GROUP_DIM, POOL_GROUP_DIM), lambda g: (g, 0, 0)),
                   pl.BlockSpec((1, 1, POOL_GROUP_DIM), lambda g: (g, 0, 0))],
        out_shape=[jax.ShapeDtypeStruct((t, PROJ_COLS), BF16),
                   jax.ShapeDtypeStruct((POOL_GROUPS, POOL_GROUP_DIM, POOL_GROUP_DIM), F32),
                   jax.ShapeDtypeStruct((POOL_GROUPS, 1, POOL_GROUP_DIM), F32)],
        compiler_params=_params("parallel"), name="pool_bwd",
    )(proj, pool_w, pool_scale, dmixin)


def _ln_stats(s):
    mu = jnp.mean(s, axis=1, keepdims=True)
    xc = s - mu
    var = jnp.mean(xc * xc, axis=1, keepdims=True)
    rstd = lax.rsqrt(var + LN_EPS)
    return xc * rstd, rstd


def _ln_fwd(h_in, y, g, b, *, name, tm=256):
    t, d = h_in.shape
    tm = min(tm, t)

    def body(h_ref, y_ref, g_ref, b_ref, o_ref, o16_ref):
        xhat, _ = _ln_stats(ALPHA * h_ref[...] + y_ref[...])
        out = xhat * g_ref[...] + b_ref[...]
        o_ref[...] = out
        o16_ref[...] = out.astype(BF16)

    row = pl.BlockSpec((tm, d), lambda i: (i, 0))
    vec = pl.BlockSpec((1, d), lambda i: (0, 0))
    return pl.pallas_call(
        body, grid=(t // tm,), in_specs=[row, row, vec, vec], out_specs=[row, row],
        out_shape=[jax.ShapeDtypeStruct((t, d), F32), jax.ShapeDtypeStruct((t, d), BF16)],
        compiler_params=_params("parallel"), name=name,
    )(h_in, y, g, b)


def _ln_backward(xhat, rstd, dout, gain):
    dxhat = dout * gain
    m1 = jnp.mean(dxhat, axis=1, keepdims=True)
    m2 = jnp.mean(dxhat * xhat, axis=1, keepdims=True)
    return (rstd * (dxhat - m1 - xhat * m2), jnp.sum(dout * xhat, axis=0, keepdims=True),
            jnp.sum(dout, axis=0, keepdims=True))


def _ln_loss(h_in, y, g, b, target, *, name, tm=256):
    t, d = h_in.shape
    tm = min(tm, t)

    def body(h_ref, y_ref, g_ref, b_ref, t_ref, sq_ref, ds_ref, ds16_ref, dg_ref, dbias_ref):
        @pl.when(pl.program_id(0) == 0)
        def _():
            sq_ref[...] = jnp.zeros_like(sq_ref)
            dg_ref[...] = jnp.zeros_like(dg_ref)
            dbias_ref[...] = jnp.zeros_like(dbias_ref)

        xhat, rstd = _ln_stats(ALPHA * h_ref[...] + y_ref[...])
        err = xhat * g_ref[...] + b_ref[...] - t_ref[...]
        sq_ref[...] += jnp.sum(jnp.sum(err * err, axis=1, keepdims=True), axis=0, keepdims=True)
        ds, dg, dbias = _ln_backward(xhat, rstd, err * (1.0 / d), g_ref[...])
        ds_ref[...] = ds
        ds16_ref[...] = ds.astype(BF16)
        dg_ref[...] += dg
        dbias_ref[...] += dbias

    row = pl.BlockSpec((tm, d), lambda i: (i, 0))
    vec = pl.BlockSpec((1, d), lambda i: (0, 0))
    return pl.pallas_call(
        body, grid=(t // tm,), in_specs=[row, row, vec, vec, row],
        out_specs=[pl.BlockSpec((1, LANE), lambda i: (0, 0)), row, row, vec, vec],
        out_shape=[jax.ShapeDtypeStruct((1, LANE), F32), jax.ShapeDtypeStruct((t, d), F32),
                   jax.ShapeDtypeStruct((t, d), BF16), jax.ShapeDtypeStruct((1, d), F32), jax.ShapeDtypeStruct((1, d), F32)],
        compiler_params=_params("arbitrary"), name=name,
    )(h_in, y, g, b, target)


def _ln_bwd(h_in, y, g, d_a, d_b, *, name, tm=256):
    t, d = h_in.shape
    tm = min(tm, t)
    has_b = d_b is not None

    def body(*refs):
        if has_b:
            h_ref, y_ref, g_ref, da_ref, db_ref, ds_ref, ds16_ref, dg_ref, dbias_ref = refs
        else:
            h_ref, y_ref, g_ref, da_ref, ds_ref, ds16_ref, dg_ref, dbias_ref = refs

        @pl.when(pl.program_id(0) == 0)
        def _():
            dg_ref[...] = jnp.zeros_like(dg_ref)
            dbias_ref[...] = jnp.zeros_like(dbias_ref)

        xhat, rstd = _ln_stats(ALPHA * h_ref[...] + y_ref[...])
        dout = da_ref[...]
        if has_b:
            dout = dout + ALPHA * db_ref[...]
        ds, dg, dbias = _ln_backward(xhat, rstd, dout, g_ref[...])
        ds_ref[...] = ds
        ds16_ref[...] = ds.astype(BF16)
        dg_ref[...] += dg
        dbias_ref[...] += dbias

    row = pl.BlockSpec((tm, d), lambda i: (i, 0))
    vec = pl.BlockSpec((1, d), lambda i: (0, 0))
    args = [h_in, y, g, d_a] + ([d_b] if has_b else [])
    return pl.pallas_call(
        body, grid=(t // tm,), in_specs=[row, row, vec, row] + ([row] if has_b else []),
        out_specs=[row, row, vec, vec],
        out_shape=[jax.ShapeDtypeStruct((t, d), F32), jax.ShapeDtypeStruct((t, d), BF16),
                   jax.ShapeDtypeStruct((1, d), F32), jax.ShapeDtypeStruct((1, d), F32)],
        compiler_params=_params("arbitrary"), name=name,
    )(*args)


def _attn_fn(q, k, v, dots):
    nn, nt, _ = dots
    s = nt(q, k) * (XATTN_HEAD_DIM ** -0.5)
    s = s - lax.stop_gradient(jnp.max(s, axis=1, keepdims=True))
    e = jnp.exp(s)
    p = e / jnp.sum(e, axis=1, keepdims=True)
    return nn(p, v)


def _attn_fwd(q, k, v, tq=512):
    t = q.shape[0]
    tq = min(tq, t)

    def body(q_ref, k_ref, v_ref, o_ref):
        o_ref[...] = _attn_fn(q_ref[...], k_ref[...], v_ref[...], _BDOT_PLAIN).astype(BF16)

    qs = pl.BlockSpec((tq, XATTN_HEAD_DIM), lambda h, i: (i, h))
    ks = pl.BlockSpec((MEM_LEN, XATTN_HEAD_DIM), lambda h, i: (0, h))
    return pl.pallas_call(
        body, grid=(XATTN_HEADS, t // tq), in_specs=[qs, ks, ks], out_specs=qs,
        out_shape=jax.ShapeDtypeStruct(q.shape, BF16), compiler_params=_params("parallel", "parallel"), name="xattn_fwd",
    )(q, k, v)


def _attn_bwd(q, k, v, do, tq=512):
    t = q.shape[0]
    tq = min(tq, t)

    def body(q_ref, k_ref, v_ref, do_ref, dq_ref, dk_ref, dv_ref):
        @pl.when(pl.program_id(1) == 0)
        def _():
            dk_ref[...] = jnp.zeros_like(dk_ref)
            dv_ref[...] = jnp.zeros_like(dv_ref)

        _, vjp = jax.vjp(lambda a, b, c: _attn_fn(a, b, c, _BDOT_VJP), q_ref[...].astype(F32), k_ref[...].astype(F32),
                         v_ref[...].astype(F32))
        dq, dk, dv = vjp(do_ref[...].astype(F32))
        dq_ref[...] = dq.astype(BF16)
        dk_ref[...] += dk
        dv_ref[...] += dv

    qs = pl.BlockSpec((tq, XATTN_HEAD_DIM), lambda h, i: (i, h))
    ks = pl.BlockSpec((MEM_LEN, XATTN_HEAD_DIM), lambda h, i: (0, h))
    return pl.pallas_call(
        body, grid=(XATTN_HEADS, t // tq), in_specs=[qs, ks, ks, qs], out_specs=[qs, ks, ks],
        out_shape=[jax.ShapeDtypeStruct(q.shape, BF16), jax.ShapeDtypeStruct(k.shape, F32), jax.ShapeDtypeStruct(v.shape, F32)],
        compiler_params=_params("parallel", "arbitrary"), name="xattn_bwd",
    )(q, k, v, do)


def _local_step(x, x16, mem, target, weights_of, grads_ready):
    def behind(vec, token):
        return vec if token is None else vec + token

    w = dict(weights_of("mixer", None))
    proj = _mm(x16, w["w_in"], tb=True, tn=768, name="mm_in_proj")
    mixin = _pool_fwd(proj, w["pool_w"], w["pool_scale"])
    post = _gdn_prep_fwd(proj, w["conv_w"])
    token = weights_of("ahead_conv", post)
    chunked, t_inv = _gdn_local_fwd(post, proj, behind(w["alog_row"], token), w["dtb_row"])
    o_raw, saved = _gdn_state_fwd(*chunked)
    token = weights_of("ahead_scan", o_raw)
    mixin = _onorm_fwd(o_raw, proj, behind(w["gdn_norm_w"], token), mixin)
    w.update(weights_of("attn", mixin))
    mix = _mm(mixin, w["w_out"], name="mm_out_proj")
    h1, h1_16 = _ln_fwd(x, mix, w["ln1_g"], w["ln1_b"], name="ln1_fwd")
    xq = _mm(h1_16, w["xq_w"], out_dtype=BF16, name="mm_xq")
    xk = _mm(mem, w["xk_w"], out_dtype=BF16, name="mm_xk")
    xv = _mm(mem, w["xv_w"], out_dtype=BF16, name="mm_xv")
    xo = _attn_fwd(xq, xk, xv)
    token = weights_of("ahead_attn", xo)
    if token is not None:
        xo, _ = lax.optimization_barrier((xo, token))
    xa = _mm(xo, w["xo_w"], name="mm_xo")
    h2, h2_16 = _ln_fwd(h1, xa, w["ln2_g"], w["ln2_b"], name="ln2_fwd")
    w.update(weights_of("up", h2_16))
    act, relu = _mm(h2_16, w["w_up"], b_chunks=True, epi="relu2", name="mm_up")
    w.update(weights_of("down", act))
    ff = _mm(act, w["w_down"], tn=512, tk=2048, name="mm_down")
    g = {}
    sq, ds3, ds3_16, g["ln3_g"], g["ln3_b"] = _ln_loss(h2, ff, w["ln3_g"], w["ln3_b"], target, name="ln3_loss")

    gw_down = _mm(act, ds3_16, ta=True, out_dtype=BF16, tm=512, tn=D_MODEL, name="mm_gw_down")
    du = _mm(ds3_16, w["w_down"], tb=True, epi="mul2r", extra=relu, name="mm_du")
    gw_up = _mm(h2_16, du, ta=True, out_dtype=BF16, o_chunks=True, name="mm_gw_up")
    token = grads_ready("mlp", {"w_down": gw_down, "w_up": gw_up})
    dh2 = _mm(du, w["w_up"], tb=True, b_chunks=True, tn=1024, tk=1024, name="mm_dh2")
    ds2, ds2_16, g["ln2_g"], g["ln2_b"] = _ln_bwd(h1, xa, behind(w["ln2_g"], token), dh2, ds3, name="ln2_bwd")
    gw_xo = _mm(xo, ds2_16, ta=True, out_dtype=BF16, name="mm_gw_xo")
    dxo = _mm(ds2_16, w["xo_w"], tb=True, out_dtype=BF16, name="mm_dxo")
    dxq, dxk, dxv = _attn_bwd(xq, xk, xv, dxo)
    gw_xq = _mm(h1_16, dxq, ta=True, out_dtype=BF16, name="mm_gw_xq")
    gw_xk = _mm(mem, dxk, ta=True, out_dtype=BF16, name="mm_gw_xk")
    gw_xv = _mm(mem, dxv, ta=True, out_dtype=BF16, name="mm_gw_xv")
    token = grads_ready("attn", {"xo_w": gw_xo, "xq_w": gw_xq, "xk_w": gw_xk, "xv_w": gw_xv})
    dh1 = _mm(dxq, w["xq_w"], tb=True, name="mm_dh1")
    ds1, ds1_16, g["ln1_g"], g["ln1_b"] = _ln_bwd(x, mix, behind(w["ln1_g"], token), dh1, ds2, name="ln1_bwd")
    gw_out = _mm(mixin, ds1_16, ta=True, out_dtype=BF16, name="mm_gw_out")
    dmixin = _mm(ds1_16, w["w_out"], tb=True, name="mm_dmixin")
    dproj, gw_pool, g["pool_scale"] = _pool_bwd(proj, w["pool_w"], w["pool_scale"], dmixin)
    token = grads_ready("mix", {"w_out": gw_out, "pool_w": gw_pool})
    do_raw, dproj, g["gdn_norm_w"] = _onorm_bwd(o_raw, proj, behind(w["gdn_norm_w"], token), dmixin, dproj)
    cots = _gdn_state_bwd(*chunked, saved, do_raw)
    token = grads_ready("tick", {"after": cots[0]})
    dpost, dproj, g["alog_row"], g["dtb_row"] = _gdn_local_bwd(post, proj, behind(w["alog_row"], token), w["dtb_row"],
                                                               t_inv, cots, dproj)
    dproj, g["conv_w"] = _gdn_prep_bwd(proj, w["conv_w"], dpost, dproj)
    gw_in = _mm(dproj, x16, ta=True, out_dtype=BF16, tm=768, tn=D_MODEL, name="mm_gw_in")
    token = grads_ready("in", {"w_in": gw_in})
    if token is not None:
        ds1, _ = lax.optimization_barrier((ds1, token))
    grad_x = _mm(dproj, w["w_in"], tk=1792, epi="add", extra=ds1, add_scale=ALPHA, name="mm_dx")
    return sq, grad_x, g


_MATRICES = ("w_in", "pool_w", "w_out", "xq_w", "xk_w", "xv_w", "xo_w", "w_up", "w_down")
_VECTORS = ("a_log", "dt_bias", "gdn_norm_w", "pool_scale", "ln1_g", "ln1_b", "ln2_g", "ln2_b", "ln3_g", "ln3_b")
_BA_SPLIT = BA_OFF + 2 * GDN_HEADS


def _lane_row(v, offset):
    return jnp.zeros((1, LANE), F32).at[0, offset:offset + v.shape[0]].set(v)


_GROUP_VECTORS = {"mixer": (), "attn": ("ln1_g", "ln1_b", "ln2_g", "ln2_b"), "up": (), "down": ("ln3_g", "ln3_b")}


def _group_weights(group, full):
    w = {n: full[n].reshape(1, D_MODEL) for n in _GROUP_VECTORS[group]}
    if group == "mixer":
        w.update({
            "w_in": _w_in_padded(full["w_in"]),
            "conv_w": full["conv_w"],
            "alog_row": _lane_row(full["a_log"], GDN_HEADS),
            "dtb_row": _lane_row(full["dt_bias"], GDN_HEADS),
            "gdn_norm_w": full["gdn_norm_w"].reshape(1, LANE),
            "pool_w": full["pool_w"],
            "pool_scale": full["pool_scale"].reshape(POOL_GROUPS, 1, POOL_GROUP_DIM),
        })
    else:
        w.update({n: full[n] for n in dict(_GATHER_GROUPS)[group]})
    return w


def _w_in_row_map():
    per = IN_COLS // N_DEV
    gap = POOL_OFF - _BA_SPLIT
    pieces = []
    for d in range(N_DEV):
        lo, hi = d * per, (d + 1) * per
        if hi <= _BA_SPLIT:
            pieces.append([(0, lo, per)])
        elif lo >= _BA_SPLIT:
            pieces.append([(0, lo + gap, per)])
        else:
            pieces.append([(0, lo, _BA_SPLIT - lo), (_BA_SPLIT - lo, POOL_OFF, hi - _BA_SPLIT)])
    return pieces


_W_IN_LANES = 256


def _w_in_padded(blocks):
    def body(b_ref, o_ref):
        for d, pieces in enumerate(_w_in_row_map()):
            for src, dst, rows in pieces:
                o_ref[dst:dst + rows, :] = b_ref[d, src:src + rows, :]
        o_ref[_BA_SPLIT:POOL_OFF, :] = jnp.zeros((POOL_OFF - _BA_SPLIT, _W_IN_LANES), o_ref.dtype)

    n, per, cols = blocks.shape
    return pl.pallas_call(
        body, grid=(cols // _W_IN_LANES,), in_specs=[pl.BlockSpec((n, per, _W_IN_LANES), lambda j: (0, 0, j))],
        out_specs=pl.BlockSpec((PROJ_COLS, _W_IN_LANES), lambda j: (0, j)),
        out_shape=jax.ShapeDtypeStruct((PROJ_COLS, cols), blocks.dtype), compiler_params=_params("parallel"),
        name="w_in_padded")(blocks)


def _w_in_chunks(g):
    def body(g_ref, o_ref):
        for d, pieces in enumerate(_w_in_row_map()):
            for dst, src, rows in pieces:
                o_ref[d, dst:dst + rows, :] = g_ref[src:src + rows, :]

    cols = g.shape[1]
    per = IN_COLS // N_DEV
    return pl.pallas_call(
        body, grid=(cols // _W_IN_LANES,), in_specs=[pl.BlockSpec((PROJ_COLS, _W_IN_LANES), lambda j: (0, j))],
        out_specs=pl.BlockSpec((N_DEV, per, _W_IN_LANES), lambda j: (0, 0, j)),
        out_shape=jax.ShapeDtypeStruct((N_DEV, per, cols), g.dtype), compiler_params=_params("parallel"),
        name="w_in_chunks")(g)


def _finish_small_grads(g):
    out = {"conv_w": g["conv_w"]}
    out["a_log"] = g["alog_row"][0, GDN_HEADS:2 * GDN_HEADS]
    out["dt_bias"] = g["dtb_row"][0, GDN_HEADS:2 * GDN_HEADS]
    out["gdn_norm_w"] = g["gdn_norm_w"].reshape(LANE)
    out["pool_scale"] = g["pool_scale"].reshape(POOL_GROUPS * POOL_GROUP_DIM)
    for n in ("ln1_g", "ln1_b", "ln2_g", "ln2_b", "ln3_g", "ln3_b"):
        out[n] = g[n].reshape(D_MODEL)
    return out


def _adamw_math(w, g, m, v):
    m = ADAM_B1 * m + (1.0 - ADAM_B1) * g
    v = ADAM_B2 * v + (1.0 - ADAM_B2) * (g * g)
    m_hat = m / (1.0 - ADAM_B1 ** ADAM_STEP)
    v_hat = v / (1.0 - ADAM_B2 ** ADAM_STEP)
    delta = -ADAM_LR * (m_hat / (jnp.sqrt(v_hat) + ADAM_EPS) + ADAM_WD * w)
    return delta, m, v


ADAMW_TILE_ELEMS = 256 * 1024
CHIP_SUM_TILE_ELEMS = 1024 * 1024


def _shard_tile(r, c, elems):
    for rows in (1024, 512, 256, 128):
        if r % rows == 0 and rows * c <= elems:
            return rows, c
    if r % 128 == 0:
        return 128, c
    return r, 256 if c % 256 == 0 else c


def _adamw_shard(parts, own, me, w, m, v, *, name):
    s, r, c = parts.shape
    tr, tc = _shard_tile(r, c, ADAMW_TILE_ELEMS)
    assert r % tr == 0 and c % tc == 0, (name, r, c)
    unit_axis = w.ndim == 3
    at = (slice(None), 0, slice(None)) if unit_axis else Ellipsis

    def body(me_ref, p_ref, own_ref, w_ref, m_ref, v_ref, g_ref, d_ref, nm_ref, nv_ref):
        mine = own_ref[...].astype(F32)
        g = None
        for i in range(s):
            part = jnp.where(me_ref[0] == i, mine, p_ref[i].astype(F32))
            g = part if g is None else g + part
        delta, nm, nv = _adamw_math(w_ref[at], g, m_ref[at], v_ref[at])
        g_ref[at] = g
        d_ref[at] = delta
        nm_ref[at] = nm
        nv_ref[at] = nv

    if unit_axis:
        blk = pl.BlockSpec((tr, 1, tc), lambda i, j, me_ref: (i, 0, j))
        out = jax.ShapeDtypeStruct((r, 1, c), F32)
    else:
        blk = pl.BlockSpec((tr, tc), lambda i, j, me_ref: (i, j))
        out = jax.ShapeDtypeStruct((r, c), F32)
    return pl.pallas_call(
        body,
        grid_spec=pltpu.PrefetchScalarGridSpec(
            num_scalar_prefetch=1, grid=(r // tr, c // tc),
            in_specs=[pl.BlockSpec((s, tr, tc), lambda i, j, me_ref: (0, i, j)),
                      pl.BlockSpec((None, tr, tc), lambda i, j, me_ref: (me_ref[0], i, j)), blk, blk, blk],
            out_specs=[blk, blk, blk, blk]),
        out_shape=[out, out, out, out], compiler_params=_params("parallel", "parallel"), name=name,
    )(me, parts, own, w, m, v)


N_CHIPS = N_DEV // 2


def _chip_sums(chunks, from_sibling, core, *, name):
    _, r, c = chunks.shape
    tr, tc = _shard_tile(r, c, CHIP_SUM_TILE_ELEMS)
    assert r % tr == 0 and c % tc == 0, (name, r, c)

    def body(core_ref, mine_ref, other_ref, o_ref):
        o_ref[...] = (mine_ref[...].astype(F32) + other_ref[...].astype(F32)).astype(o_ref.dtype)

    by_chip = pl.BlockSpec((None, tr, tc), lambda q, i, j, core_ref: (q, i, j))
    return pl.pallas_call(
        body,
        grid_spec=pltpu.PrefetchScalarGridSpec(
            num_scalar_prefetch=1, grid=(N_CHIPS, r // tr, c // tc),
            in_specs=[pl.BlockSpec((None, tr, tc), lambda q, i, j, core_ref: (2 * q + core_ref[0], i, j)), by_chip],
            out_specs=by_chip),
        out_shape=jax.ShapeDtypeStruct((N_CHIPS, r, c), chunks.dtype),
        compiler_params=_params("parallel", "parallel", "parallel"), name=name,
    )(core, chunks, from_sibling)


def _place():
    return lax.axis_index("x"), lax.axis_index("y"), lax.axis_index("c")


def _slot(px, py, pc):
    return 4 * px + 2 * py + pc


_HBM = pl.BlockSpec(memory_space=pltpu.HBM)


_SEM = pl.BlockSpec(memory_space=pltpu.SEMAPHORE)
_ANY = pl.BlockSpec(memory_space=pl.ANY)
_EFFECT = pltpu.SideEffectType.DATAFLOW_SIDE_EFFECTING
_N_PEERS = N_DEV - 1


def _peer(k, x, y, c):
    return (1 - x if k & 4 else x, 1 - y if k & 2 else y, 1 - c if k & 1 else c)


_EXCHANGE_BITS = {"gather_near": (1, 2, 4), "gather_relay": (6,), "gather_pass": (2, 4, 6),
                  "scatter_sibling": (1, 1, 1, 1), "scatter_chips": (2, 4, 6)}


def _exchange_copy(mode, src, land, w, i, place, send_sems, recv_sems, receiving):
    bits = _EXCHANGE_BITS[mode]
    k = bits[i]
    peer = _peer(k, *place)
    me = _slot(*place)
    if mode == "gather_near":
        to, src_ref, sent_to, got_at = peer, src[w], me, _slot(*peer)
    elif mode == "gather_relay":
        x, y, c = place
        other = 1 - c
        to = (lax.bitwise_xor(x, c), lax.bitwise_xor(y, other), c)
        blk = _slot(lax.bitwise_xor(x, other), lax.bitwise_xor(y, c), c)
        src_ref, sent_to, got_at = land[w].at[blk], blk, _slot(*peer)
    elif mode == "gather_pass":
        blk = _slot(*peer)
        to, src_ref, sent_to, got_at = _peer(1, *place), land[w].at[blk], blk, _slot(*_peer(k | 1, *place))
    elif mode == "scatter_sibling":
        to, src_ref, sent_to, got_at = peer, src[w].at[2 * i + 1 - place[2]], i, i
    else:
        to, src_ref, sent_to, got_at = peer, src[w].at[_slot(*peer) // 2], me // 2, _slot(*peer) // 2
    sem = w * len(bits) + i
    return pltpu.make_async_remote_copy(
        src_ref=src_ref, dst_ref=land[w].at[got_at if receiving else sent_to], send_sem=send_sems.at[sem],
        recv_sem=recv_sems.at[sem], device_id=to, device_id_type=MESH)


def _exchange_start(mode, srcs, lands, after, *, name):
    ns, nl = len(srcs), len(lands)
    n_sem = nl * len(_EXCHANGE_BITS[mode])

    def body(*refs):
        src, land = refs[:ns], refs[ns:ns + nl]
        send_sems, recv_sems = refs[ns + nl + 1:ns + nl + 3]
        token = refs[-1]
        place = _place()
        for w in range(nl):
            for i in range(len(_EXCHANGE_BITS[mode])):
                _exchange_copy(mode, src, land, w, i, place, send_sems, recv_sems, receiving=False).start()
        token[...] = jnp.zeros_like(token)

    sems = pltpu.SemaphoreType.DMA((n_sem,))
    arrays = list(srcs) + list(lands)
    res = pl.pallas_call(
        body, name=name, in_specs=[_HBM] * (ns + nl) + [_ANY],
        out_specs=(_SEM, _SEM, *([_HBM] * (ns + nl)), pl.BlockSpec(memory_space=pltpu.VMEM)),
        out_shape=(sems, sems, *[pltpu.HBM(a.shape, a.dtype) for a in arrays], jax.ShapeDtypeStruct((8, LANE), F32)),
        input_output_aliases={i: 2 + i for i in range(ns + nl)},
        compiler_params=pltpu.CompilerParams(has_side_effects=_EFFECT),
    )(*[pltpu.with_memory_space_constraint(a, pltpu.HBM) for a in arrays], after)
    return res[0], res[1], list(res[2:2 + ns]), list(res[2 + ns:2 + ns + nl]), res[-1]


def _exchange_wait(mode, started, after, *, name):
    send_sems, recv_sems, srcs, lands, _ = started
    ns, nl = len(srcs), len(lands)

    def body(*refs):
        src, land = refs[:ns], refs[ns:ns + nl]
        send_sems, recv_sems = refs[ns + nl:ns + nl + 2]
        place = _place()
        for w in range(nl):
            for i in range(len(_EXCHANGE_BITS[mode])):
                cp = _exchange_copy(mode, src, land, w, i, place, send_sems, recv_sems, receiving=True)
                cp.wait_send()
                cp.wait_recv()

    arrays = list(srcs) + list(lands)
    res = pl.pallas_call(
        body, name=name, in_specs=[_HBM] * (ns + nl) + [_SEM, _SEM, _ANY], out_specs=[_HBM] * (ns + nl),
        out_shape=[pltpu.HBM(a.shape, a.dtype) for a in arrays],
        input_output_aliases={i: i for i in range(ns + nl)},
        compiler_params=pltpu.CompilerParams(has_side_effects=_EFFECT),
    )(*arrays, send_sems, recv_sems, after)
    return list(res[:ns]), list(res[ns:])


def _small_allreduce_adamw(gvec, wvec, mvec, vvec):
    rows, length = gvec.shape

    def body(g_ref, w_ref, m_ref, v_ref, gs_ref, d_ref, nm_ref, nv_ref, slots, send_sems, recv_sems):
        x, y, c = _place()
        me = _slot(x, y, c)
        slots[me] = g_ref[...]
        sends = []
        for k in range(1, N_DEV):
            peer = _peer(k, x, y, c)
            sends.append(pltpu.make_async_remote_copy(
                src_ref=g_ref, dst_ref=slots.at[me], send_sem=send_sems.at[k - 1], recv_sem=recv_sems.at[k - 1],
                device_id=peer, device_id_type=MESH))
        for cp in sends:
            cp.start()
        for k in range(1, N_DEV):
            peer = _peer(k, x, y, c)
            pltpu.make_async_remote_copy(
                src_ref=g_ref, dst_ref=slots.at[_slot(*peer)], send_sem=send_sems.at[k - 1], recv_sem=recv_sems.at[k - 1],
                device_id=peer, device_id_type=MESH).wait_recv()
        for cp in sends:
            cp.wait_send()
        g = slots[0]
        for s in range(1, N_DEV):
            g = g + slots[s]
        delta, nm, nv = _adamw_math(w_ref[...], g, m_ref[...], v_ref[...])
        gs_ref[...] = g
        d_ref[...] = delta
        nm_ref[...] = nm
        nv_ref[...] = nv

    vmem = pl.BlockSpec(memory_space=pltpu.VMEM)
    out = jax.ShapeDtypeStruct((rows, length), F32)
    return pl.pallas_call(
        body, in_specs=[vmem] * 4, out_specs=[vmem] * 4, out_shape=[out] * 4,
        scratch_shapes=[pltpu.VMEM((N_DEV, rows, length), F32), pltpu.SemaphoreType.DMA((N_DEV - 1,)),
                        pltpu.SemaphoreType.DMA((N_DEV - 1,))],
        name="small_allreduce_adamw",
    )(gvec, wvec, mvec, vvec)


_SMALL_SEGMENTS = (("a_log", GDN_HEADS), ("dt_bias", GDN_HEADS), ("gdn_norm_w", HEAD_DIM), ("pool_scale", GDN_WIDTH),
                   ("ln1_g", D_MODEL), ("ln1_b", D_MODEL), ("ln2_g", D_MODEL), ("ln2_b", D_MODEL),
                   ("ln3_g", D_MODEL), ("ln3_b", D_MODEL), ("conv_w", CONV_K * QKV_COLS))
_SMALL_ROWS = 8
_SMALL_LEN = -(-sum(sz for _, sz in _SMALL_SEGMENTS) // (_SMALL_ROWS * LANE)) * LANE


def _pack_small(vals):
    parts = [vals[n].reshape(-1).astype(F32) if n in vals else jnp.zeros((sz,), F32) for n, sz in _SMALL_SEGMENTS]
    flat = jnp.concatenate(parts)
    flat = jnp.pad(flat, (0, _SMALL_ROWS * _SMALL_LEN - flat.shape[0]))
    return flat.reshape(_SMALL_ROWS, _SMALL_LEN)


def _unpack_small(vec):
    flat = vec.reshape(-1)
    out, off = {}, 0
    for n, sz in _SMALL_SEGMENTS:
        out[n] = flat[off:off + sz]
        off += sz
    return out


_WEIGHT_ORDER = ("w_in", "conv_w", "a_log", "dt_bias", "gdn_norm_w", "pool_w", "pool_scale", "w_out", "ln1_g", "ln1_b",
                 "xq_w", "xk_w", "xv_w", "xo_w", "ln2_g", "ln2_b", "w_up", "w_down", "ln3_g", "ln3_b")


def _shard2d(name, a):
    if name == "w_in":
        return a.T
    return a.reshape(-1, a.shape[-1]) if name == "pool_w" else a


def _update_view(name, a):
    return jnp.transpose(a, (2, 0, 1)) if name == "w_in" else _shard2d(name, a[0])


def _shard_result(name, r, shape):
    return jnp.transpose(r, (1, 2, 0)) if name == "w_in" else r.reshape(shape)


def _gathered_to_full(name, gth):
    if name in ("w_up", "w_in"):
        return gth
    if name == "conv_w":
        return jnp.transpose(gth, (1, 0, 2)).reshape(gth.shape[1], N_DEV * gth.shape[2])
    if name == "pool_w":
        g4 = gth.reshape(N_DEV, POOL_GROUPS, POOL_GROUP_DIM // N_DEV, POOL_GROUP_DIM)
        return jnp.transpose(g4, (1, 0, 2, 3)).reshape(POOL_GROUPS, POOL_GROUP_DIM, POOL_GROUP_DIM)
    return gth.reshape(N_DEV * gth.shape[1], gth.shape[2])


def _full_to_chunks(name, full):
    if name == "w_up":
        return full
    if name == "pool_w":
        g4 = full.reshape(POOL_GROUPS, N_DEV, POOL_GROUP_DIM // N_DEV, POOL_GROUP_DIM)
        return jnp.transpose(g4, (1, 0, 2, 3)).reshape(N_DEV, POOL_GROUPS * POOL_GROUP_DIM // N_DEV, POOL_GROUP_DIM)
    return full.reshape(N_DEV, full.shape[0] // N_DEV, full.shape[1])


_GATHER_GROUPS = (("mixer", ("w_in", "conv_w", "pool_w")), ("attn", ("w_out", "xq_w", "xk_w", "xv_w", "xo_w")),
                  ("up", ("w_up",)), ("down", ("w_down",)))


def _grad_chunks(name, g):
    if name == "w_in":
        return _w_in_chunks(g.astype(BF16))
    return _full_to_chunks(name, g.astype(BF16))


def kernel(x, mem, w_in, conv_w, a_log, dt_bias, gdn_norm_w, pool_w, pool_scale, w_out, ln1_g, ln1_b, xq_w, xk_w, xv_w, xo_w, ln2_g, ln2_b, w_up, w_down, ln3_g, ln3_b, loss_target, m_w_in, m_conv_w, m_a_log, m_dt_bias, m_gdn_norm_w, m_pool_w, m_pool_scale, m_w_out, m_ln1_g, m_ln1_b, m_xq_w, m_xk_w, m_xv_w, m_xo_w, m_ln2_g, m_ln2_b, m_w_up, m_w_down, m_ln3_g, m_ln3_b, v_w_in, v_conv_w, v_a_log, v_dt_bias, v_gdn_norm_w, v_pool_w, v_pool_scale, v_w_out, v_ln1_g, v_ln1_b, v_xq_w, v_xk_w, v_xv_w, v_xo_w, v_ln2_g, v_ln2_b, v_w_up, v_w_down, v_ln3_g, v_ln3_b):
    args = dict(locals())
    wt = {n: args[n][0] for n in _WEIGHT_ORDER}
    mo = {n: args["m_" + n][0] for n in _WEIGHT_ORDER}
    vo = {n: args["v_" + n][0] for n in _WEIGHT_ORDER}

    me = _slot(*_place())
    me_arr = jnp.reshape(me, (1,)).astype(jnp.int32)
    nothing = jnp.zeros((8, LANE), F32)

    def landing_zones(names):
        shards = [_shard2d(n, wt[n]).astype(F32 if n == "conv_w" else BF16) for n in names]
        zones = [lax.dynamic_update_slice(lax.empty((N_DEV, *s.shape), s.dtype), s[None], (me, 0, 0)) for s in shards]
        return shards, zones

    chip_arr = jnp.reshape(me // 2, (1,)).astype(jnp.int32)
    core_arr = jnp.reshape(lax.axis_index("c"), (1,)).astype(jnp.int32)
    names_of = dict(_GATHER_GROUPS)
    gathers = {}
    prepared = {}

    def gather_near(group, after):
        shards, zones = prepared.pop(group) if group in prepared else landing_zones(names_of[group])
        gathers[group] = _exchange_start("gather_near", shards, zones, after, name="gather_near_" + group)
        return gathers[group][4]

    def gather_next(group, was, now, after):
        _, zones = _exchange_wait(was, gathers[group], after, name=f"{was}_{group}_wait")
        gathers[group] = _exchange_start(now, [], zones, nothing, name=f"{now}_{group}")
        return gathers[group][4]

    def gather_relay(group, after):
        return gather_next(group, "gather_near", "gather_relay", after)

    def gather_pass(group, after):
        return gather_next(group, "gather_relay", "gather_pass", after)

    def gathered(group, after):
        _, zones = _exchange_wait("gather_pass", gathers[group], after, name=f"gather_pass_{group}_wait")
        full = {n: _gathered_to_full(n, z) for n, z in zip(names_of[group], zones)}
        full.update({n: wt[n] for n in _VECTORS})
        return _group_weights(group, full)

    token = gather_near("mixer", nothing)
    x16 = _cast_bf16(x[0], name="cast_x")
    later = {group: landing_zones(names_of[group]) for group in ("attn", "up", "down")}
    token, x16, later = lax.optimization_barrier((token, x16, later))
    prepared.update(later)
    token = gather_near("attn", gather_relay("mixer", token))
    token = gather_pass("mixer", token)

    def weights_of(group, after):
        if group == "mixer":
            return gathered(group, token)
        if group == "ahead_conv":
            return gather_near("up", gather_relay("attn", after))[0:1, 0:1]
        if group == "ahead_scan":
            return gather_near("down", gather_relay("up", gather_pass("attn", after)))[0:1, 0:1]
        if group == "ahead_attn":
            return gather_relay("down", gather_pass("up", after))[0:1, 0:1]
        if group == "up":
            return gathered(group, gather_pass("down", after))
        return gathered(group, after)

    scatters = {}
    in_flight = []

    def chip_stage(after):
        group, names, started = in_flight.pop()
        chunks, from_sibling = _exchange_wait("scatter_sibling", started, after, name=f"scatter_sibling_{group}_wait")
        sums = [_chip_sums(c, f, core_arr, name=f"chip_sums_{n}") for n, c, f in zip(names, chunks, from_sibling)]
        scatters[group] = (names, _exchange_start("scatter_chips", sums, [lax.empty(s.shape, s.dtype) for s in sums],
                                                  nothing, name="scatter_chips_" + group))
        return scatters[group][1][4]

    def grads_ready(group, grads):
        if group == "tick":
            return chip_stage(grads["after"])[0:1, 0:1] if in_flight else None
        names = tuple(grads)
        chunks = [_grad_chunks(n, grads[n]) for n in names]
        token = chip_stage(chunks[0]) if in_flight else nothing
        zones = [lax.empty((N_CHIPS, *c.shape[1:]), c.dtype) for c in chunks]
        started = _exchange_start("scatter_sibling", chunks, zones, token, name="scatter_sibling_" + group)
        in_flight.append((group, names, started))
        return started[4][0:1, 0:1]

    sq, grad_x, g = _local_step(x[0], x16, mem[0], loss_target[0], weights_of, grads_ready)
    small = _finish_small_grads(g)

    out = {}
    after = chip_stage(grad_x)
    for group, (names, started) in scatters.items():
        sums, lands = _exchange_wait("scatter_chips", started, after, name=f"scatter_chips_{group}_wait")
        for n, parts, own in zip(names, lands, sums):
            res = _adamw_shard(parts, own, chip_arr, _update_view(n, args[n]), _update_view(n, args["m_" + n]),
                               _update_view(n, args["v_" + n]), name="adamw_" + n)
            out[n] = [_shard_result(n, r, args[n].shape) for r in res]
            after = res[1]

    packed, _ = lax.optimization_barrier((_pack_small(small), after))
    gs, ds, ms, vs = _small_allreduce_adamw(
        packed, _pack_small({n: wt[n] for n in _VECTORS}), _pack_small({n: mo[n] for n in _VECTORS}),
        _pack_small({n: vo[n] for n in _VECTORS}))
    gs, ds, ms, vs = _unpack_small(gs), _unpack_small(ds), _unpack_small(ms), _unpack_small(vs)
    cols = conv_w.shape[-1]
    conv_full = gs["conv_w"].reshape(CONV_K, QKV_COLS)
    conv_mine = lax.dynamic_slice(conv_full, (0, me * cols), (CONV_K, cols))[None]
    res = _adamw_shard(conv_mine, conv_mine, jnp.zeros((1,), jnp.int32), wt["conv_w"], mo["conv_w"], vo["conv_w"],
                       name="adamw_conv_w")
    out["conv_w"] = [r.reshape(conv_w.shape) for r in res]
    for n in _VECTORS:
        out[n] = [t[n].reshape(args[n].shape) for t in (gs, ds, ms, vs)]

    loss = lax.psum(0.5 * sq[0, 0] / D_MODEL, ("x", "y", "c"))
    return (loss, grad_x[None], *[out[n][0] for n in _WEIGHT_ORDER], *[out[n][1] for n in _WEIGHT_ORDER],
            *[out[n][2] for n in _WEIGHT_ORDER], *[out[n][3] for n in _WEIGHT_ORDER])
```

```python
import functools
import math

import jax
import jax.numpy as jnp
from jax import lax
from jax.experimental import pallas as pl
from jax.experimental.pallas import tpu as pltpu

F32 = jnp.float32
BF16 = jnp.bfloat16
MESH = pl.DeviceIdType.MESH

N_DEV = 8
D_MODEL = 2048
GDN_WIDTH = 1024
GDN_HEADS = 8
HEAD_DIM = 128
CONV_K = 4
CHUNK = 64
POOL_GROUPS = 4
POOL_GROUP_DIM = 256
MEM_LEN = 256
XATTN_HEADS = 4
XATTN_HEAD_DIM = 512
D_FF = 8192
IN_COLS = 5136
ALPHA = 2.0 ** 0.25
LN_EPS = 1e-5
NORM_EPS = 1e-6

LANE = 128
QKV_COLS = 3 * GDN_WIDTH
Z_OFF = QKV_COLS
BA_OFF = 4 * GDN_WIDTH
POOL_OFF = BA_OFF + 2 * LANE
PROJ_COLS = POOL_OFF + GDN_WIDTH
Z_BLK = Z_OFF // LANE
BA_BLK = BA_OFF // LANE
POOL_BLK = POOL_OFF // POOL_GROUP_DIM

ADAM_LR = 0.001
ADAM_B1 = 0.9
ADAM_B2 = 0.999
ADAM_EPS = 1e-08
ADAM_WD = 0.01
ADAM_STEP = 10

VMEM_LIMIT_BYTES = 48 * 1024 * 1024


def _params(*sem):
    return pltpu.CompilerParams(dimension_semantics=sem if sem else None, vmem_limit_bytes=VMEM_LIMIT_BYTES)


def _make_dots(cast, precision, batched=False):
    lead = 1 if batched else 0
    batch = ((0,), (0,)) if batched else ((), ())

    def dg(a, b, ca, cb):
        if cast is not None:
            a = a.astype(cast)
            b = b.astype(cast)
        return lax.dot_general(a, b, (((ca + lead,), (cb + lead,)), batch), precision=precision, preferred_element_type=F32)

    def nn_(a, b):
        return dg(a, b, 1, 0)

    def nt_(a, b):
        return dg(a, b, 1, 1)

    def tn_(a, b):
        return dg(a, b, 0, 0)

    @jax.custom_vjp
    def nn(a, b):
        return nn_(a, b)

    nn.defvjp(lambda a, b: (nn_(a, b), (a, b)), lambda r, g: (nt_(g, r[1]), tn_(r[0], g)))

    @jax.custom_vjp
    def nt(a, b):
        return nt_(a, b)

    nt.defvjp(lambda a, b: (nt_(a, b), (a, b)), lambda r, g: (nn_(g, r[1]), tn_(g, r[0])))

    @jax.custom_vjp
    def tn(a, b):
        return tn_(a, b)

    tn.defvjp(lambda a, b: (tn_(a, b), (a, b)), lambda r, g: (nt_(r[1], g), nn_(r[0], g)))

    return (nn_, nt_, tn_), (nn, nt, tn)


_BDOT_PLAIN, _BDOT_VJP = _make_dots(BF16, None)
_BDOT_BATCH_PLAIN, _BDOT_BATCH_VJP = _make_dots(BF16, None, batched=True)
_FDOT_BATCH_PLAIN, _FDOT_BATCH_VJP = _make_dots(BF16, None, batched=True)


def _mm(a, b, *, ta=False, tb=False, out_dtype=F32, tm=None, tn=512, tk=None, epi=None, extra=None, add_scale=1.0,
        b_chunks=False, o_chunks=False, name):
    m, k = (a.shape[1], a.shape[0]) if ta else a.shape
    if b_chunks:
        n, kb = (b.shape[1], N_DEV * b.shape[2]) if tb else (N_DEV * b.shape[2], b.shape[1])
    else:
        n, kb = b.shape if tb else (b.shape[1], b.shape[0])
    assert kb == k, (name, a.shape, b.shape)
    tm, tn, tk = min(tm or m, m), min(tn, n), min(tk or k, k)
    assert m % tm == 0 and n % tn == 0 and k % tk == 0, (name, m, n, k)
    nk = k // tk
    dims = (((0 if ta else 1,), (1 if tb else 0,)), ((), ()))
    n_extra = 0 if epi in (None, "relu2") else 1
    n_out = 2 if epi == "relu2" else 1
    if epi in ("relu2", "mul2r"):
        out_dtype = BF16

    def body(*refs):
        a_ref, b_ref = refs[:2]
        c_ref = refs[2] if n_extra else None
        o_refs = refs[2 + n_extra:2 + n_extra + n_out]
        scr = refs[2 + n_extra + n_out:]
        r = lax.dot_general(a_ref[...].astype(BF16), b_ref[...].astype(BF16), dims, preferred_element_type=F32)

        def finish(v):
            if epi == "add":
                o_refs[0][...] = (v + add_scale * c_ref[...]).astype(out_dtype)
            elif epi == "relu2":
                p = jnp.maximum(v, 0.0)
                o_refs[0][...] = (p * p).astype(BF16)
                o_refs[1][...] = p.astype(BF16)
            elif epi == "mul2r":
                o_refs[0][...] = (v * (2.0 * c_ref[...].astype(F32))).astype(BF16)
            else:
                o_refs[0][...] = v.astype(out_dtype)

        if nk == 1:
            finish(r)
        else:
            acc = scr[0]
            kk = pl.program_id(2)

            @pl.when(kk == 0)
            def _():
                acc[...] = r

            @pl.when(kk > 0)
            def _():
                acc[...] += r

            @pl.when(kk == nk - 1)
            def _():
                finish(acc[...])

    a_spec = pl.BlockSpec((tk, tm), lambda i, j, kk: (kk, i)) if ta else pl.BlockSpec((tm, tk), lambda i, j, kk: (i, kk))
    if b_chunks and tb:
        kc = k // N_DEV // tk
        b_spec = pl.BlockSpec((None, tn, tk), lambda i, j, kk: (kk // kc, j, kk % kc))
    elif b_chunks:
        nc = n // N_DEV // tn
        b_spec = pl.BlockSpec((None, tk, tn), lambda i, j, kk: (j // nc, kk, j % nc))
    elif tb:
        b_spec = pl.BlockSpec((tn, tk), lambda i, j, kk: (j, kk))
    else:
        b_spec = pl.BlockSpec((tk, tn), lambda i, j, kk: (kk, j))
    mn_spec = pl.BlockSpec((tm, tn), lambda i, j, kk: (i, j))
    if o_chunks:
        oc = n // N_DEV // tn
        o_spec = pl.BlockSpec((None, tm, tn), lambda i, j, kk: (j // oc, i, j % oc))
        o_shape = jax.ShapeDtypeStruct((N_DEV, m, n // N_DEV), out_dtype)
    else:
        o_spec, o_shape = mn_spec, jax.ShapeDtypeStruct((m, n), out_dtype)
    res = pl.pallas_call(
        body, grid=(m // tm, n // tn, nk), in_specs=[a_spec, b_spec] + [mn_spec] * n_extra,
        out_specs=[o_spec] * n_out, out_shape=[o_shape] * n_out,
        scratch_shapes=[pltpu.VMEM((tm, tn), F32)] if nk > 1 else [],
        compiler_params=_params("parallel", "parallel", "arbitrary"), name=name,
    )(a, b, *([extra] if n_extra else []))
    return res if n_out > 1 else res[0]


def _cast_bf16(v, *, name, tm=512):
    t, d = v.shape
    tm = min(tm, t)

    def body(v_ref, o_ref):
        o_ref[...] = v_ref[...].astype(BF16)

    spec = pl.BlockSpec((tm, d), lambda i: (i, 0))
    return pl.pallas_call(body, grid=(t // tm,), in_specs=[spec], out_specs=spec,
                          out_shape=jax.ShapeDtypeStruct((t, d), BF16), compiler_params=_params("parallel"), name=name)(v)


def _shift_down(v, s):
    if s == 0:
        return v
    row = lax.broadcasted_iota(jnp.int32, v.shape, 0)
    return jnp.where(row >= s, pltpu.roll(v, s, axis=0), 0.0)


def _shift_up(v, s):
    if s == 0:
        return v
    t = v.shape[0]
    row = lax.broadcasted_iota(jnp.int32, v.shape, 0)
    return jnp.where(row < t - s, pltpu.roll(v, t - s, axis=0), 0.0)


def _post_col(j):
    return (j % GDN_HEADS) * 3 + j // GDN_HEADS


def _gdn_prep_fwd(proj, conv_w):
    t = proj.shape[0]

    def body(x_ref, w_ref, o_ref):
        j = pl.program_id(0)
        x = x_ref[...]
        y = jnp.zeros_like(x)
        for tap in range(CONV_K):
            y = y + w_ref[tap:tap + 1, :] * _shift_down(x, CONV_K - 1 - tap)
        c = y * jax.nn.sigmoid(y)
        nrm = c * lax.rsqrt(jnp.sum(c * c, axis=1, keepdims=True) + NORM_EPS)
        o_ref[...] = jnp.where(j < 2 * GDN_HEADS, nrm, c)

    return pl.pallas_call(
        body, grid=(QKV_COLS // LANE,),
        in_specs=[pl.BlockSpec((t, LANE), lambda j: (0, j)), pl.BlockSpec((CONV_K, LANE), lambda j: (0, j))],
        out_specs=pl.BlockSpec((t, LANE), lambda j: (0, _post_col(j))),
        out_shape=jax.ShapeDtypeStruct((t, QKV_COLS), F32),
        compiler_params=_params("parallel"), name="gdn_prep_fwd",
    )(proj, conv_w)


def _gdn_prep_bwd(proj, conv_w, dpost, dproj):
    t = proj.shape[0]

    def body(x_ref, w_ref, d_ref, _, dx_ref, dw_ref):
        j = pl.program_id(0)
        x = x_ref[...]
        xs = [_shift_down(x, CONV_K - 1 - tap) for tap in range(CONV_K)]
        y = jnp.zeros_like(x)
        for tap in range(CONV_K):
            y = y + w_ref[tap:tap + 1, :] * xs[tap]
        sig = jax.nn.sigmoid(y)
        c = y * sig
        r = lax.rsqrt(jnp.sum(c * c, axis=1, keepdims=True) + NORM_EPS)
        nrm = c * r
        d = d_ref[...]
        dc_norm = r * (d - nrm * jnp.sum(d * nrm, axis=1, keepdims=True))
        dc = jnp.where(j < 2 * GDN_HEADS, dc_norm, d)
        dy = dc * (sig * (1.0 + y * (1.0 - sig)))
        dx = jnp.zeros_like(x)
        for tap in range(CONV_K):
            dx = dx + _shift_up(w_ref[tap:tap + 1, :] * dy, CONV_K - 1 - tap)
            dw_ref[tap:tap + 1, :] = jnp.sum(dy * xs[tap], axis=0, keepdims=True)
        dx_ref[...] = dx.astype(dx_ref.dtype)

    return pl.pallas_call(
        body, grid=(QKV_COLS // LANE,),
        in_specs=[pl.BlockSpec((t, LANE), lambda j: (0, j)), pl.BlockSpec((CONV_K, LANE), lambda j: (0, j)),
                  pl.BlockSpec((t, LANE), lambda j: (0, _post_col(j))), pl.BlockSpec(memory_space=pl.ANY)],
        out_specs=[pl.BlockSpec((t, LANE), lambda j: (0, j)), pl.BlockSpec((CONV_K, LANE), lambda j: (0, j))],
        out_shape=[jax.ShapeDtypeStruct(dproj.shape, dproj.dtype), jax.ShapeDtypeStruct((CONV_K, QKV_COLS), F32)],
        input_output_aliases={3: 0},
        compiler_params=_params("parallel"), name="gdn_prep_bwd",
    )(proj, conv_w, dpost, dproj)


def _softplus(v):
    return jnp.maximum(v, 0.0) + jnp.log(1.0 + jnp.exp(-jnp.abs(v)))


def _tri_inv(low, nn):
    r = lax.broadcasted_iota(jnp.int32, (CHUNK, CHUNK), 0)
    c = lax.broadcasted_iota(jnp.int32, (CHUNK, CHUNK), 1)
    eye = (r == c).astype(F32)
    same_blk = lax.shift_right_logical(r, 4) == lax.shift_right_logical(c, 4)
    diag = jnp.where(same_blk, low, 0.0)
    off = low - diag
    n1 = -diag
    n2 = nn(n1, n1)
    n4 = nn(n2, n2)
    n8 = nn(n4, n4)
    inv_d = nn(nn(nn(eye + n1, eye + n2), eye + n4), eye + n8)
    m1 = nn(inv_d, off)
    m2 = nn(m1, m1)
    return nn(nn(eye - m1, eye + m2), inv_d)


@jax.custom_vjp
def _tri_inv_known(low, t_inv):
    return t_inv


def _tri_inv_known_fwd(low, t_inv):
    return t_inv, t_inv


def _tri_inv_known_bwd(t_inv, g):
    _, nt, tn = _FDOT_BATCH_PLAIN
    return -nt(tn(t_inv, g), t_inv), jnp.zeros_like(t_inv)


_tri_inv_known.defvjp(_tri_inv_known_fwd, _tri_inv_known_bwd)


LOCAL_HEADS_PER_STEP = 8


def _gdn_local_fn(qkv, ba, alog_row, dtb_row, first_head, bdots, fdots, t_known=None):
    nn, nt, tn = bdots
    fnn = fdots[0]
    n_heads = qkv.shape[1] // (3 * HEAD_DIM)
    part = lambda i, p: qkv[:, (3 * i + p) * HEAD_DIM:(3 * i + p + 1) * HEAD_DIM]
    q = jnp.stack([part(i, 0) for i in range(n_heads)]) * (HEAD_DIM ** -0.5)
    k = jnp.stack([part(i, 1) for i in range(n_heads)])
    v = jnp.stack([part(i, 2) for i in range(n_heads)])
    lane = lax.broadcasted_iota(jnp.int32, ba.shape, 1)
    bg = jnp.where(lane < GDN_HEADS, jax.nn.sigmoid(ba), -jnp.exp(alog_row) * _softplus(ba + dtb_row))
    pick = lambda l: jnp.sum(jnp.where(lane == l, bg, 0.0), axis=1, keepdims=True)
    beta = jnp.stack([pick(first_head + i) for i in range(n_heads)])
    g = jnp.stack([pick(first_head + i + GDN_HEADS) for i in range(n_heads)])

    r = lax.broadcasted_iota(jnp.int32, (CHUNK, CHUNK), 0)
    c = lax.broadcasted_iota(jnp.int32, (CHUNK, CHUNK), 1)
    incl = r >= c
    strict = r > c
    eye = r == c

    def to_row(col):
        return jnp.sum(jnp.where(eye, col, 0.0), axis=1, keepdims=True)

    gc = jnp.sum(jnp.where(incl, to_row(g), 0.0), axis=2, keepdims=True)
    diff = gc - to_row(gc)
    decay = jnp.where(incl, jnp.exp(jnp.where(incl, diff, 0.0)), 0.0)
    k_beta = k * beta
    v_beta = v * beta
    low = jnp.where(strict, nt(k_beta, k) * decay, 0.0)
    t_inv = _tri_inv(low, fnn) if t_known is None else _tri_inv_known(low, t_known)
    eg = jnp.exp(gc)
    u = fnn(t_inv, v_beta)
    w = fnn(t_inv, k_beta * eg)
    attn = jnp.where(incl, nt(q, k) * decay, 0.0)
    last = lax.broadcasted_iota(jnp.int32, (CHUNK, 1), 0) == CHUNK - 1
    g_last = jnp.sum(jnp.where(last, gc, 0.0), axis=1, keepdims=True)
    kdec = k * jnp.exp(g_last - gc)
    elast = jnp.broadcast_to(jnp.exp(g_last), (n_heads, 1, LANE))
    return u, w, q * eg, kdec, attn, elast, t_inv


def _gdn_state_fn(u, w, qg, kdec, attn, elast, state, bdots):
    nn, _, tn = bdots
    v_new = u - nn(w, state)
    o = nn(qg, state) + nn(attn, v_new)
    return o, state * elast + tn(kdec, v_new)


def _gdn_local_fwd(post, proj, alog_row, dtb_row):
    t = post.shape[0]
    n_chunks = t // CHUNK
    hb = LOCAL_HEADS_PER_STEP

    def body(qkv_ref, ba_ref, al_ref, dt_ref, u_ref, w_ref, qg_ref, kd_ref, at_ref, el_ref, ti_ref):
        u, w, qg, kdec, attn, elast, t_inv = _gdn_local_fn(qkv_ref[...], ba_ref[...], al_ref[...], dt_ref[...],
                                                           pl.program_id(1) * hb, _BDOT_BATCH_PLAIN, _FDOT_BATCH_PLAIN)
        for i in range(hb):
            cols = slice(i * HEAD_DIM, (i + 1) * HEAD_DIM)
            u_ref[:, cols] = u[i]
            w_ref[:, cols] = w[i].astype(BF16)
            qg_ref[:, cols] = qg[i].astype(BF16)
            kd_ref[:, cols] = kdec[i].astype(BF16)
        at_ref[...] = attn.astype(BF16)
        el_ref[:, 0] = elast
        ti_ref[...] = t_inv

    wide = pl.BlockSpec((CHUNK, hb * HEAD_DIM), lambda n, j: (n, j))
    square = pl.BlockSpec((hb, CHUNK, CHUNK), lambda n, j: (j, n, 0))
    row = pl.BlockSpec((1, LANE), lambda n, j: (0, 0))
    res = pl.pallas_call(
        body, grid=(n_chunks, GDN_HEADS // hb),
        in_specs=[pl.BlockSpec((CHUNK, hb * 3 * HEAD_DIM), lambda n, j: (n, j)),
                  pl.BlockSpec((CHUNK, LANE), lambda n, j: (n, BA_BLK)), row, row],
        out_specs=[wide, wide, wide, wide, square, pl.BlockSpec((hb, 1, 1, LANE), lambda n, j: (j, n, 0, 0)), square],
        out_shape=[jax.ShapeDtypeStruct((t, GDN_WIDTH), F32), jax.ShapeDtypeStruct((t, GDN_WIDTH), BF16),
                   jax.ShapeDtypeStruct((t, GDN_WIDTH), BF16), jax.ShapeDtypeStruct((t, GDN_WIDTH), BF16),
                   jax.ShapeDtypeStruct((GDN_HEADS, t, CHUNK), BF16),
                   jax.ShapeDtypeStruct((GDN_HEADS, n_chunks, 1, LANE), F32),
                   jax.ShapeDtypeStruct((GDN_HEADS, t, CHUNK), F32)],
        compiler_params=_params("parallel", "parallel"), name="gdn_local_fwd",
    )(post, proj, alog_row, dtb_row)
    return tuple(res[:6]), res[6]


def _by_head(ref):
    return jnp.stack([ref[:, h * HEAD_DIM:(h + 1) * HEAD_DIM] for h in range(ref.shape[1] // HEAD_DIM)])


def _gdn_state_specs(n_of):
    wide = pl.BlockSpec((CHUNK, GDN_WIDTH), lambda n: (n_of(n), 0))
    attn = pl.BlockSpec((GDN_HEADS, CHUNK, CHUNK), lambda n: (0, n_of(n), 0))
    elast = pl.BlockSpec((GDN_HEADS, 1, 1, LANE), lambda n: (0, n_of(n), 0, 0))
    saved = pl.BlockSpec((GDN_HEADS, 1, HEAD_DIM, HEAD_DIM), lambda n: (0, n_of(n), 0, 0))
    return wide, attn, elast, saved


def _gdn_state_fwd(u, w, qg, kdec, attn, elast):
    t = u.shape[0]
    n_chunks = t // CHUNK

    def body(u_ref, w_ref, qg_ref, kd_ref, at_ref, el_ref, o_ref, save_ref, state_ref):
        @pl.when(pl.program_id(0) == 0)
        def _():
            state_ref[...] = jnp.zeros_like(state_ref)

        state = state_ref[...]
        save_ref[:, 0] = state
        o, new_state = _gdn_state_fn(_by_head(u_ref), _by_head(w_ref), _by_head(qg_ref), _by_head(kd_ref), at_ref[...],
                                     el_ref[:, 0], state, _BDOT_BATCH_PLAIN)
        for h in range(GDN_HEADS):
            o_ref[:, h * HEAD_DIM:(h + 1) * HEAD_DIM] = o[h]
        state_ref[...] = new_state

    wide, attn_spec, elast_spec, saved_spec = _gdn_state_specs(lambda n: n)
    return pl.pallas_call(
        body, grid=(n_chunks,), in_specs=[wide, wide, wide, wide, attn_spec, elast_spec],
        out_specs=[wide, saved_spec],
        out_shape=[jax.ShapeDtypeStruct((t, GDN_WIDTH), F32),
                   jax.ShapeDtypeStruct((GDN_HEADS, n_chunks, HEAD_DIM, HEAD_DIM), F32)],
        scratch_shapes=[pltpu.VMEM((GDN_HEADS, HEAD_DIM, HEAD_DIM), F32)],
        compiler_params=_params("arbitrary"), name="gdn_state_fwd",
    )(u, w, qg, kdec, attn, elast)


def _gdn_state_bwd(u, w, qg, kdec, attn, elast, saved, do):
    t = u.shape[0]
    n_chunks = t // CHUNK
    last = n_chunks - 1

    def body(u_ref, w_ref, qg_ref, kd_ref, at_ref, el_ref, save_ref, do_ref,
             du_ref, dw_ref, dqg_ref, dkd_ref, dat_ref, del_ref, dstate_ref):
        @pl.when(pl.program_id(0) == 0)
        def _():
            dstate_ref[...] = jnp.zeros_like(dstate_ref)

        _, vjp = jax.vjp(
            lambda *a: _gdn_state_fn(*a, _BDOT_BATCH_VJP), _by_head(u_ref), _by_head(w_ref).astype(F32),
            _by_head(qg_ref).astype(F32), _by_head(kd_ref).astype(F32), at_ref[...].astype(F32), el_ref[:, 0],
            save_ref[:, 0])
        du, dw, dqg, dkd, dat, de, dstate = vjp((_by_head(do_ref), dstate_ref[...]))
        for h in range(GDN_HEADS):
            cols = slice(h * HEAD_DIM, (h + 1) * HEAD_DIM)
            du_ref[:, cols] = du[h]
            dw_ref[:, cols] = dw[h]
            dqg_ref[:, cols] = dqg[h]
            dkd_ref[:, cols] = dkd[h]
        dat_ref[...] = dat
        del_ref[:, 0] = de
        dstate_ref[...] = dstate

    wide, attn_spec, elast_spec, saved_spec = _gdn_state_specs(lambda n: last - n)
    wide_f32 = jax.ShapeDtypeStruct((t, GDN_WIDTH), F32)
    return pl.pallas_call(
        body, grid=(n_chunks,), in_specs=[wide, wide, wide, wide, attn_spec, elast_spec, saved_spec, wide],
        out_specs=[wide, wide, wide, wide, attn_spec, elast_spec],
        out_shape=[wide_f32, wide_f32, wide_f32, wide_f32, jax.ShapeDtypeStruct((GDN_HEADS, t, CHUNK), F32),
                   jax.ShapeDtypeStruct((GDN_HEADS, n_chunks, 1, LANE), F32)],
        scratch_shapes=[pltpu.VMEM((GDN_HEADS, HEAD_DIM, HEAD_DIM), F32)],
        compiler_params=_params("arbitrary"), name="gdn_state_bwd",
    )(u, w, qg, kdec, attn, elast, saved, do)


def _gdn_local_bwd(post, proj, alog_row, dtb_row, t_inv, cots, dproj):
    t = post.shape[0]
    n_chunks = t // CHUNK
    hb = LOCAL_HEADS_PER_STEP
    n_steps = GDN_HEADS // hb

    def body(qkv_ref, ba_ref, al_ref, dt_ref, ti_ref, du_ref, dw_ref, dqg_ref, dkd_ref, dat_ref, del_ref, _,
             dqkv_ref, dba_ref, dal_ref, ddt_ref, dba_acc):
        n = pl.program_id(0)
        j = pl.program_id(1)

        @pl.when((n == 0) & (j == 0))
        def _():
            dal_ref[...] = jnp.zeros_like(dal_ref)
            ddt_ref[...] = jnp.zeros_like(ddt_ref)

        @pl.when(j == 0)
        def _():
            dba_acc[...] = jnp.zeros_like(dba_acc)

        t_known = ti_ref[...]
        _, vjp = jax.vjp(
            lambda a, b, c, d: _gdn_local_fn(a, b, c, d, j * hb, _BDOT_BATCH_VJP, _FDOT_BATCH_VJP, t_known)[:6],
            qkv_ref[...], ba_ref[...], al_ref[...], dt_ref[...])
        dqkv, dba, dal, ddt = vjp((_by_head(du_ref), _by_head(dw_ref), _by_head(dqg_ref), _by_head(dkd_ref), dat_ref[...],
                                   del_ref[:, 0]))
        dqkv_ref[...] = dqkv
        dba_acc[...] += dba
        dal_ref[...] += dal
        ddt_ref[...] += ddt

        @pl.when(j == n_steps - 1)
        def _():
            dba_ref[:, 0:LANE] = dba_acc[...].astype(dba_ref.dtype)
            dba_ref[:, LANE:2 * LANE] = jnp.zeros((CHUNK, LANE), dba_ref.dtype)

    wide = pl.BlockSpec((CHUNK, hb * HEAD_DIM), lambda n, j: (n, j))
    qkv_spec = pl.BlockSpec((CHUNK, hb * 3 * HEAD_DIM), lambda n, j: (n, j))
    row = pl.BlockSpec((1, LANE), lambda n, j: (0, 0))
    return pl.pallas_call(
        body, grid=(n_chunks, n_steps),
        in_specs=[qkv_spec, pl.BlockSpec((CHUNK, LANE), lambda n, j: (n, BA_BLK)), row, row,
                  pl.BlockSpec((hb, CHUNK, CHUNK), lambda n, j: (j, n, 0)), wide, wide, wide, wide,
                  pl.BlockSpec((hb, CHUNK, CHUNK), lambda n, j: (j, n, 0)),
                  pl.BlockSpec((hb, 1, 1, LANE), lambda n, j: (j, n, 0, 0)), pl.BlockSpec(memory_space=pl.ANY)],
        out_specs=[qkv_spec, pl.BlockSpec((CHUNK, 2 * LANE), lambda n, j: (n, BA_BLK // 2)), row, row],
        out_shape=[jax.ShapeDtypeStruct((t, QKV_COLS), F32), jax.ShapeDtypeStruct(dproj.shape, dproj.dtype),
                   jax.ShapeDtypeStruct((1, LANE), F32), jax.ShapeDtypeStruct((1, LANE), F32)],
        input_output_aliases={11: 1},
        scratch_shapes=[pltpu.VMEM((CHUNK, LANE), F32)],
        compiler_params=_params("arbitrary", "arbitrary"), name="gdn_local_bwd",
    )(post, proj, alog_row, dtb_row, t_inv, *cots, dproj)


def _onorm_fn(o, z, w):
    return o * lax.rsqrt(jnp.mean(o * o, axis=1, keepdims=True) + NORM_EPS) * w * (z * jax.nn.sigmoid(z))


def _onorm_fwd(o_raw, proj, norm_w, mixin, tm=512):
    t = o_raw.shape[0]
    tm = min(tm, t)

    def body(o_ref, z_ref, w_ref, _, out_ref):
        out_ref[...] = _onorm_fn(o_ref[...], z_ref[...], w_ref[...]).astype(out_ref.dtype)

    return pl.pallas_call(
        body, grid=(t // tm, GDN_HEADS),
        in_specs=[pl.BlockSpec((tm, LANE), lambda i, h: (i, h)), pl.BlockSpec((tm, LANE), lambda i, h: (i, Z_BLK + h)),
                  pl.BlockSpec((1, LANE), lambda i, h: (0, 0)), pl.BlockSpec(memory_space=pl.ANY)],
        out_specs=pl.BlockSpec((tm, LANE), lambda i, h: (i, h)),
        out_shape=jax.ShapeDtypeStruct(mixin.shape, mixin.dtype), input_output_aliases={3: 0},
        compiler_params=_params("parallel", "parallel"), name="gdn_onorm_fwd",
    )(o_raw, proj, norm_w, mixin)


def _onorm_bwd(o_raw, proj, norm_w, dmixin, dproj, tm=512):
    t = o_raw.shape[0]
    tm = min(tm, t)

    def body(o_ref, z_ref, w_ref, d_ref, _, do_ref, dz_ref, dw_ref):
        @pl.when((pl.program_id(0) == 0) & (pl.program_id(1) == 0))
        def _():
            dw_ref[...] = jnp.zeros_like(dw_ref)

        _, vjp = jax.vjp(_onorm_fn, o_ref[...], z_ref[...], w_ref[...])
        do, dz, dw = vjp(d_ref[...])
        do_ref[...] = do
        dz_ref[...] = dz.astype(dz_ref.dtype)
        dw_ref[...] += dw

    return pl.pallas_call(
        body, grid=(t // tm, GDN_HEADS),
        in_specs=[pl.BlockSpec((tm, LANE), lambda i, h: (i, h)), pl.BlockSpec((tm, LANE), lambda i, h: (i, Z_BLK + h)),
                  pl.BlockSpec((1, LANE), lambda i, h: (0, 0)), pl.BlockSpec((tm, LANE), lambda i, h: (i, h)),
                  pl.BlockSpec(memory_space=pl.ANY)],
        out_specs=[pl.BlockSpec((tm, LANE), lambda i, h: (i, h)), pl.BlockSpec((tm, LANE), lambda i, h: (i, Z_BLK + h)),
                   pl.BlockSpec((1, LANE), lambda i, h: (0, 0))],
        out_shape=[jax.ShapeDtypeStruct((t, GDN_WIDTH), F32), jax.ShapeDtypeStruct(dproj.shape, dproj.dtype),
                   jax.ShapeDtypeStruct((1, LANE), F32)],
        input_output_aliases={4: 1},
        compiler_params=_params("arbitrary", "arbitrary"), name="gdn_onorm_bwd",
    )(o_raw, proj, norm_w, dmixin, dproj)


def _pool_select(levels, gi):
    out = levels[-1]
    for lvl in range(len(levels) - 2, -1, -1):
        out = jnp.where(gi == lvl, levels[lvl], out)
    return out


def _pool_count(shape, gi):
    pos = lax.broadcasted_iota(jnp.int32, shape, 0)
    win = lax.shift_left(jnp.int32(2), gi)
    return jnp.minimum(pos + 1, win).astype(F32)


def _pooled(p, gi):
    acc = p
    levels = []
    for lvl in range(POOL_GROUPS):
        acc = acc + _shift_down(acc, 1 << lvl)
        levels.append(acc)
    return _pool_select(levels, gi) / _pool_count(p.shape, gi) - p


def _pool_fwd(proj, pool_w, pool_scale):
    t = proj.shape[0]

    def body(p_ref, w_ref, s_ref, out_ref):
        gi = pl.program_id(0)
        pooled = _pooled(p_ref[...], gi)
        out_ref[...] = (_BDOT_PLAIN[0](pooled, w_ref[0]) * s_ref[0]).astype(out_ref.dtype)

    return pl.pallas_call(
        body, grid=(POOL_GROUPS,),
        in_specs=[pl.BlockSpec((t, POOL_GROUP_DIM), lambda g: (0, POOL_BLK + g)),
                  pl.BlockSpec((1, POOL_GROUP_DIM, POOL_GROUP_DIM), lambda g: (g, 0, 0)),
                  pl.BlockSpec((1, 1, POOL_GROUP_DIM), lambda g: (g, 0, 0))],
        out_specs=pl.BlockSpec((t, POOL_GROUP_DIM), lambda g: (0, GDN_WIDTH // POOL_GROUP_DIM + g)),
        out_shape=jax.ShapeDtypeStruct((t, 2 * GDN_WIDTH), BF16),
        compiler_params=_params("parallel"), name="pool_fwd",
    )(proj, pool_w, pool_scale)


def _pool_bwd(proj, pool_w, pool_scale, dmixin):
    t = proj.shape[0]
    nn, nt, tn = _BDOT_PLAIN

    def body(p_ref, w_ref, s_ref, d_ref, dp_ref, dw_ref, ds_ref):
        gi = pl.program_id(0)
        p = p_ref[...]
        pooled = _pooled(p, gi)
        mixed = nn(pooled, w_ref[0])
        d = d_ref[...]
        ds_ref[0] = jnp.sum(d * mixed, axis=0, keepdims=True)
        dmixed = d * s_ref[0]
        dw_ref[0] = tn(pooled, dmixed)
        dpooled = nt(dmixed, w_ref[0])
        acc = dpooled / _pool_count(p.shape, gi)
        levels = []
        for lvl in range(POOL_GROUPS):
            acc = acc + _shift_up(acc, 1 << lvl)
            levels.append(acc)
        dp_ref[...] = (_pool_select(levels, gi) - dpooled).astype(dp_ref.dtype)

    return pl.pallas_call(
        body, grid=(POOL_GROUPS,),
        in_specs=[pl.BlockSpec((t, POOL_GROUP_DIM), lambda g: (0, POOL_BLK + g)),
                  pl.BlockSpec((1, POOL_GROUP_DIM, POOL_GROUP_DIM), lambda g: (g, 0, 0)),
                  pl.BlockSpec((1, 1, POOL_GROUP_DIM), lambda g: (g, 0, 0)),
                  pl.BlockSpec((t, POOL_GROUP_DIM), lambda g: (0, GDN_WIDTH // POOL_GROUP_DIM + g))],
        out_specs=[pl.BlockSpec((t, POOL_GROUP_DIM), lambda g: (0, POOL_BLK + g)),
                   pl.BlockSpec((1, POOL_GROUP_DIM, POOL_GROUP_DIM), lambda g: (g, 0, 0)),
                   pl.BlockSpec((1, 1, POOL_GROUP_DIM), lambda g: (g, 0, 0))],
        out_shape=[jax.ShapeDtypeStruct((t, PROJ_COLS), BF16),
                   jax.ShapeDtypeStruct((POOL_GROUPS, POOL_GROUP_DIM, POOL_GROUP_DIM), F32),
                   jax.ShapeDtypeStruct((POOL_GROUPS, 1, POOL_GROUP_DIM), F32)],
        compiler_params=_params("parallel"), name="pool_bwd",
    )(proj, pool_w, pool_scale, dmixin)


def _ln_stats(s):
    mu = jnp.mean(s, axis=1, keepdims=True)
    xc = s - mu
    var = jnp.mean(xc * xc, axis=1, keepdims=True)
    rstd = lax.rsqrt(var + LN_EPS)
    return xc * rstd, rstd


def _ln_fwd(h_in, y, g, b, *, name, tm=256):
    t, d = h_in.shape
    tm = min(tm, t)

    def body(h_ref, y_ref, g_ref, b_ref, o_ref, o16_ref):
        xhat, _ = _ln_stats(ALPHA * h_ref[...] + y_ref[...])
        out = xhat * g_ref[...] + b_ref[...]
        o_ref[...] = out
        o16_ref[...] = out.astype(BF16)

    row = pl.BlockSpec((tm, d), lambda i: (i, 0))
    vec = pl.BlockSpec((1, d), lambda i: (0, 0))
    return pl.pallas_call(
        body, grid=(t // tm,), in_specs=[row, row, vec, vec], out_specs=[row, row],
        out_shape=[jax.ShapeDtypeStruct((t, d), F32), jax.ShapeDtypeStruct((t, d), BF16)],
        compiler_params=_params("parallel"), name=name,
    )(h_in, y, g, b)


def _ln_backward(xhat, rstd, dout, gain):
    dxhat = dout * gain
    m1 = jnp.mean(dxhat, axis=1, keepdims=True)
    m2 = jnp.mean(dxhat * xhat, axis=1, keepdims=True)
    return (rstd * (dxhat - m1 - xhat * m2), jnp.sum(dout * xhat, axis=0, keepdims=True),
            jnp.sum(dout, axis=0, keepdims=True))


def _ln_loss(h_in, y, g, b, target, *, name, tm=256):
    t, d = h_in.shape
    tm = min(tm, t)

    def body(h_ref, y_ref, g_ref, b_ref, t_ref, sq_ref, ds_ref, ds16_ref, dg_ref, dbias_ref):
        @pl.when(pl.program_id(0) == 0)
        def _():
            sq_ref[...] = jnp.zeros_like(sq_ref)
            dg_ref[...] = jnp.zeros_like(dg_ref)
            dbias_ref[...] = jnp.zeros_like(dbias_ref)

        xhat, rstd = _ln_stats(ALPHA * h_ref[...] + y_ref[...])
        err = xhat * g_ref[...] + b_ref[...] - t_ref[...]
        sq_ref[...] += jnp.sum(jnp.sum(err * err, axis=1, keepdims=True), axis=0, keepdims=True)
        ds, dg, dbias = _ln_backward(xhat, rstd, err * (1.0 / d), g_ref[...])
        ds_ref[...] = ds
        ds16_ref[...] = ds.astype(BF16)
        dg_ref[...] += dg
        dbias_ref[...] += dbias

    row = pl.BlockSpec((tm, d), lambda i: (i, 0))
    vec = pl.BlockSpec((1, d), lambda i: (0, 0))
    return pl.pallas_call(
        body, grid=(t // tm,), in_specs=[row, row, vec, vec, row],
        out_specs=[pl.BlockSpec((1, LANE), lambda i: (0, 0)), row, row, vec, vec],
        out_shape=[jax.ShapeDtypeStruct((1, LANE), F32), jax.ShapeDtypeStruct((t, d), F32),
                   jax.ShapeDtypeStruct((t, d), BF16), jax.ShapeDtypeStruct((1, d), F32), jax.ShapeDtypeStruct((1, d), F32)],
        compiler_params=_params("arbitrary"), name=name,
    )(h_in, y, g, b, target)


def _ln_bwd(h_in, y, g, d_a, d_b, *, name, tm=256):
    t, d = h_in.shape
    tm = min(tm, t)
    has_b = d_b is not None

    def body(*refs):
        if has_b:
            h_ref, y_ref, g_ref, da_ref, db_ref, ds_ref, ds16_ref, dg_ref, dbias_ref = refs
        else:
            h_ref, y_ref, g_ref, da_ref, ds_ref, ds16_ref, dg_ref, dbias_ref = refs

        @pl.when(pl.program_id(0) == 0)
        def _():
            dg_ref[...] = jnp.zeros_like(dg_ref)
            dbias_ref[...] = jnp.zeros_like(dbias_ref)

        xhat, rstd = _ln_stats(ALPHA * h_ref[...] + y_ref[...])
        dout = da_ref[...]
        if has_b:
            dout = dout + ALPHA * db_ref[...]
        ds, dg, dbias = _ln_backward(xhat, rstd, dout, g_ref[...])
        ds_ref[...] = ds
        ds16_ref[...] = ds.astype(BF16)
        dg_ref[...] += dg
        dbias_ref[...] += dbias

    row = pl.BlockSpec((tm, d), lambda i: (i, 0))
    vec = pl.BlockSpec((1, d), lambda i: (0, 0))
    args = [h_in, y, g, d_a] + ([d_b] if has_b else [])
    return pl.pallas_call(
        body, grid=(t // tm,), in_specs=[row, row, vec, row] + ([row] if has_b else []),
        out_specs=[row, row, vec, vec],
        out_shape=[jax.ShapeDtypeStruct((t, d), F32), jax.ShapeDtypeStruct((t, d), BF16),
                   jax.ShapeDtypeStruct((1, d), F32), jax.ShapeDtypeStruct((1, d), F32)],
        compiler_params=_params("arbitrary"), name=name,
    )(*args)


def _attn_fn(q, k, v, dots):
    nn, nt, _ = dots
    s = nt(q, k) * (XATTN_HEAD_DIM ** -0.5)
    s = s - lax.stop_gradient(jnp.max(s, axis=1, keepdims=True))
    e = jnp.exp(s)
    p = e / jnp.sum(e, axis=1, keepdims=True)
    return nn(p, v)


def _attn_fwd(q, k, v, tq=512):
    t = q.shape[0]
    tq = min(tq, t)

    def body(q_ref, k_ref, v_ref, o_ref):
        o_ref[...] = _attn_fn(q_ref[...], k_ref[...], v_ref[...], _BDOT_PLAIN).astype(BF16)

    qs = pl.BlockSpec((tq, XATTN_HEAD_DIM), lambda h, i: (i, h))
    ks = pl.BlockSpec((MEM_LEN, XATTN_HEAD_DIM), lambda h, i: (0, h))
    return pl.pallas_call(
        body, grid=(XATTN_HEADS, t // tq), in_specs=[qs, ks, ks], out_specs=qs,
        out_shape=jax.ShapeDtypeStruct(q.shape, BF16), compiler_params=_params("parallel", "parallel"), name="xattn_fwd",
    )(q, k, v)


def _attn_bwd(q, k, v, do, tq=512):
    t = q.shape[0]
    tq = min(tq, t)

    def body(q_ref, k_ref, v_ref, do_ref, dq_ref, dk_ref, dv_ref):
        @pl.when(pl.program_id(1) == 0)
        def _():
            dk_ref[...] = jnp.zeros_like(dk_ref)
            dv_ref[...] = jnp.zeros_like(dv_ref)

        _, vjp = jax.vjp(lambda a, b, c: _attn_fn(a, b, c, _BDOT_VJP), q_ref[...].astype(F32), k_ref[...].astype(F32),
                         v_ref[...].astype(F32))
        dq, dk, dv = vjp(do_ref[...].astype(F32))
        dq_ref[...] = dq.astype(BF16)
        dk_ref[...] += dk
        dv_ref[...] += dv

    qs = pl.BlockSpec((tq, XATTN_HEAD_DIM), lambda h, i: (i, h))
    ks = pl.BlockSpec((MEM_LEN, XATTN_HEAD_DIM), lambda h, i: (0, h))
    return pl.pallas_call(
        body, grid=(XATTN_HEADS, t // tq), in_specs=[qs, ks, ks, qs], out_specs=[qs, ks, ks],
        out_shape=[jax.ShapeDtypeStruct(q.shape, BF16), jax.ShapeDtypeStruct(k.shape, F32), jax.ShapeDtypeStruct(v.shape, F32)],
        compiler_params=_params("parallel", "arbitrary"), name="xattn_bwd",
    )(q, k, v, do)


def _local_step(x, x16, mem, target, weights_of, grads_ready):
    def behind(vec, token):
        return vec if token is None else vec + token

    w = dict(weights_of("mixer", None))
    proj = _mm(x16, w["w_in"], tb=True, tn=768, name="mm_in_proj")
    mixin = _pool_fwd(proj, w["pool_w"], w["pool_scale"])
    post = _gdn_prep_fwd(proj, w["conv_w"])
    token = weights_of("ahead_conv", post)
    chunked, t_inv = _gdn_local_fwd(post, proj, behind(w["alog_row"], token), w["dtb_row"])
    o_raw, saved = _gdn_state_fwd(*chunked)
    token = weights_of("ahead_scan", o_raw)
    mixin = _onorm_fwd(o_raw, proj, behind(w["gdn_norm_w"], token), mixin)
    w.update(weights_of("attn", mixin))
    mix = _mm(mixin, w["w_out"], name="mm_out_proj")
    h1, h1_16 = _ln_fwd(x, mix, w["ln1_g"], w["ln1_b"], name="ln1_fwd")
    xq = _mm(h1_16, w["xq_w"], out_dtype=BF16, name="mm_xq")
    xk = _mm(mem, w["xk_w"], out_dtype=BF16, name="mm_xk")
    xv = _mm(mem, w["xv_w"], out_dtype=BF16, name="mm_xv")
    xo = _attn_fwd(xq, xk, xv)
    token = weights_of("ahead_attn", xo)
    if token is not None:
        xo, _ = lax.optimization_barrier((xo, token))
    xa = _mm(xo, w["xo_w"], name="mm_xo")
    h2, h2_16 = _ln_fwd(h1, xa, w["ln2_g"], w["ln2_b"], name="ln2_fwd")
    w.update(weights_of("up", h2_16))
    act, relu = _mm(h2_16, w["w_up"], b_chunks=True, epi="relu2", name="mm_up")
    w.update(weights_of("down", act))
    ff = _mm(act, w["w_down"], tn=512, tk=2048, name="mm_down")
    g = {}
    sq, ds3, ds3_16, g["ln3_g"], g["ln3_b"] = _ln_loss(h2, ff, w["ln3_g"], w["ln3_b"], target, name="ln3_loss")

    gw_down = _mm(act, ds3_16, ta=True, out_dtype=BF16, tm=512, tn=D_MODEL, name="mm_gw_down")
    du = _mm(ds3_16, w["w_down"], tb=True, epi="mul2r", extra=relu, name="mm_du")
    gw_up = _mm(h2_16, du, ta=True, out_dtype=BF16, o_chunks=True, name="mm_gw_up")
    token = grads_ready("mlp", {"w_down": gw_down, "w_up": gw_up})
    dh2 = _mm(du, w["w_up"], tb=True, b_chunks=True, tn=1024, tk=1024, name="mm_dh2")
    ds2, ds2_16, g["ln2_g"], g["ln2_b"] = _ln_bwd(h1, xa, behind(w["ln2_g"], token), dh2, ds3, name="ln2_bwd")
    gw_xo = _mm(xo, ds2_16, ta=True, out_dtype=BF16, name="mm_gw_xo")
    dxo = _mm(ds2_16, w["xo_w"], tb=True, out_dtype=BF16, name="mm_dxo")
    dxq, dxk, dxv = _attn_bwd(xq, xk, xv, dxo)
    gw_xq = _mm(h1_16, dxq, ta=True, out_dtype=BF16, name="mm_gw_xq")
    gw_xk = _mm(mem, dxk, ta=True, out_dtype=BF16, name="mm_gw_xk")
    gw_xv = _mm(mem, dxv, ta=True, out_dtype=BF16, name="mm_gw_xv")
    token = grads_ready("attn", {"xo_w": gw_xo, "xq_w": gw_xq, "xk_w": gw_xk, "xv_w": gw_xv})
    dh1 = _mm(dxq, w["xq_w"], tb=True, name="mm_dh1")
    ds1, ds1_16, g["ln1_g"], g["ln1_b"] = _ln_bwd(x, mix, behind(w["ln1_g"], token), dh1, ds2, name="ln1_bwd")
    gw_out = _mm(mixin, ds1_16, ta=True, out_dtype=BF16, name="mm_gw_out")
    dmixin = _mm(ds1_16, w["w_out"], tb=True, name="mm_dmixin")
    dproj, gw_pool, g["pool_scale"] = _pool_bwd(proj, w["pool_w"], w["pool_scale"], dmixin)
    token = grads_ready("mix", {"w_out": gw_out, "pool_w": gw_pool})
    do_raw, dproj, g["gdn_norm_w"] = _onorm_bwd(o_raw, proj, behind(w["gdn_norm_w"], token), dmixin, dproj)
    cots = _gdn_state_bwd(*chunked, saved, do_raw)
    token = grads_ready("tick", {"after": cots[0]})
    dpost, dproj, g["alog_row"], g["dtb_row"] = _gdn_local_bwd(post, proj, behind(w["alog_row"], token), w["dtb_row"],
                                                               t_inv, cots, dproj)
    dproj, g["conv_w"] = _gdn_prep_bwd(proj, w["conv_w"], dpost, dproj)
    gw_in = _mm(dproj, x16, ta=True, out_dtype=BF16, tm=768, tn=D_MODEL, name="mm_gw_in")
    token = grads_ready("in", {"w_in": gw_in})
    if token is not None:
        ds1, _ = lax.optimization_barrier((ds1, token))
    grad_x = _mm(dproj, w["w_in"], tk=1792, epi="add", extra=ds1, add_scale=ALPHA, name="mm_dx")
    return sq, grad_x, g


_MATRICES = ("w_in", "pool_w", "w_out", "xq_w", "xk_w", "xv_w", "xo_w", "w_up", "w_down")
_VECTORS = ("a_log", "dt_bias", "gdn_norm_w", "pool_scale", "ln1_g", "ln1_b", "ln2_g", "ln2_b", "ln3_g", "ln3_b")
_BA_SPLIT = BA_OFF + 2 * GDN_HEADS


def _lane_row(v, offset):
    return jnp.zeros((1, LANE), F32).at[0, offset:offset + v.shape[0]].set(v)


_GROUP_VECTORS = {"mixer": (), "attn": ("ln1_g", "ln1_b", "ln2_g", "ln2_b"), "up": (), "down": ("ln3_g", "ln3_b")}


def _group_weights(group, full):
    w = {n: full[n].reshape(1, D_MODEL) for n in _GROUP_VECTORS[group]}
    if group == "mixer":
        w.update({
            "w_in": _w_in_padded(full["w_in"]),
            "conv_w": full["conv_w"],
            "alog_row": _lane_row(full["a_log"], GDN_HEADS),
            "dtb_row": _lane_row(full["dt_bias"], GDN_HEADS),
            "gdn_norm_w": full["gdn_norm_w"].reshape(1, LANE),
            "pool_w": full["pool_w"],
            "pool_scale": full["pool_scale"].reshape(POOL_GROUPS, 1, POOL_GROUP_DIM),
        })
    else:
        w.update({n: full[n] for n in dict(_GATHER_GROUPS)[group]})
    return w


def _w_in_row_map():
    per = IN_COLS // N_DEV
    gap = POOL_OFF - _BA_SPLIT
    pieces = []
    for d in range(N_DEV):
        lo, hi = d * per, (d + 1) * per
        if hi <= _BA_SPLIT:
            pieces.append([(0, lo, per)])
        elif lo >= _BA_SPLIT:
            pieces.append([(0, lo + gap, per)])
        else:
            pieces.append([(0, lo, _BA_SPLIT - lo), (_BA_SPLIT - lo, POOL_OFF, hi - _BA_SPLIT)])
    return pieces


_W_IN_LANES = 256


def _w_in_padded(blocks):
    def body(b_ref, o_ref):
        for d, pieces in enumerate(_w_in_row_map()):
            for src, dst, rows in pieces:
                o_ref[dst:dst + rows, :] = b_ref[d, src:src + rows, :]
        o_ref[_BA_SPLIT:POOL_OFF, :] = jnp.zeros((POOL_OFF - _BA_SPLIT, _W_IN_LANES), o_ref.dtype)

    n, per, cols = blocks.shape
    return pl.pallas_call(
        body, grid=(cols // _W_IN_LANES,), in_specs=[pl.BlockSpec((n, per, _W_IN_LANES), lambda j: (0, 0, j))],
        out_specs=pl.BlockSpec((PROJ_COLS, _W_IN_LANES), lambda j: (0, j)),
        out_shape=jax.ShapeDtypeStruct((PROJ_COLS, cols), blocks.dtype), compiler_params=_params("parallel"),
        name="w_in_padded")(blocks)


def _w_in_chunks(g):
    def body(g_ref, o_ref):
        for d, pieces in enumerate(_w_in_row_map()):
            for dst, src, rows in pieces:
                o_ref[d, dst:dst + rows, :] = g_ref[src:src + rows, :]

    cols = g.shape[1]
    per = IN_COLS // N_DEV
    return pl.pallas_call(
        body, grid=(cols // _W_IN_LANES,), in_specs=[pl.BlockSpec((PROJ_COLS, _W_IN_LANES), lambda j: (0, j))],
        out_specs=pl.BlockSpec((N_DEV, per, _W_IN_LANES), lambda j: (0, 0, j)),
        out_shape=jax.ShapeDtypeStruct((N_DEV, per, cols), g.dtype), compiler_params=_params("parallel"),
        name="w_in_chunks")(g)


def _finish_small_grads(g):
    out = {"conv_w": g["conv_w"]}
    out["a_log"] = g["alog_row"][0, GDN_HEADS:2 * GDN_HEADS]
    out["dt_bias"] = g["dtb_row"][0, GDN_HEADS:2 * GDN_HEADS]
    out["gdn_norm_w"] = g["gdn_norm_w"].reshape(LANE)
    out["pool_scale"] = g["pool_scale"].reshape(POOL_GROUPS * POOL_GROUP_DIM)
    for n in ("ln1_g", "ln1_b", "ln2_g", "ln2_b", "ln3_g", "ln3_b"):
        out[n] = g[n].reshape(D_MODEL)
    return out


def _adamw_math(w, g, m, v):
    m = ADAM_B1 * m + (1.0 - ADAM_B1) * g
    v = ADAM_B2 * v + (1.0 - ADAM_B2) * (g * g)
    m_hat = m / (1.0 - ADAM_B1 ** ADAM_STEP)
    v_hat = v / (1.0 - ADAM_B2 ** ADAM_STEP)
    delta = -ADAM_LR * (m_hat / (jnp.sqrt(v_hat) + ADAM_EPS) + ADAM_WD * w)
    return delta, m, v


ADAMW_TILE_ELEMS = 256 * 1024
CHIP_SUM_TILE_ELEMS = 1024 * 1024


def _shard_tile(r, c, elems):
    for rows in (1024, 512, 256, 128):
        if r % rows == 0 and rows * c <= elems:
            return rows, c
    if r % 128 == 0:
        return 128, c
    return r, 256 if c % 256 == 0 else c


def _adamw_shard(parts, own, me, w, m, v, *, name):
    s, r, c = parts.shape
    tr, tc = _shard_tile(r, c, ADAMW_TILE_ELEMS)
    assert r % tr == 0 and c % tc == 0, (name, r, c)
    unit_axis = w.ndim == 3
    at = (slice(None), 0, slice(None)) if unit_axis else Ellipsis

    def body(me_ref, p_ref, own_ref, w_ref, m_ref, v_ref, g_ref, d_ref, nm_ref, nv_ref):
        mine = own_ref[...].astype(F32)
        g = None
        for i in range(s):
            part = jnp.where(me_ref[0] == i, mine, p_ref[i].astype(F32))
            g = part if g is None else g + part
        delta, nm, nv = _adamw_math(w_ref[at], g, m_ref[at], v_ref[at])
        g_ref[at] = g
        d_ref[at] = delta
        nm_ref[at] = nm
        nv_ref[at] = nv

    if unit_axis:
        blk = pl.BlockSpec((tr, 1, tc), lambda i, j, me_ref: (i, 0, j))
        out = jax.ShapeDtypeStruct((r, 1, c), F32)
    else:
        blk = pl.BlockSpec((tr, tc), lambda i, j, me_ref: (i, j))
        out = jax.ShapeDtypeStruct((r, c), F32)
    return pl.pallas_call(
        body,
        grid_spec=pltpu.PrefetchScalarGridSpec(
            num_scalar_prefetch=1, grid=(r // tr, c // tc),
            in_specs=[pl.BlockSpec((s, tr, tc), lambda i, j, me_ref: (0, i, j)),
                      pl.BlockSpec((None, tr, tc), lambda i, j, me_ref: (me_ref[0], i, j)), blk, blk, blk],
            out_specs=[blk, blk, blk, blk]),
        out_shape=[out, out, out, out], compiler_params=_params("parallel", "parallel"), name=name,
    )(me, parts, own, w, m, v)


N_CHIPS = N_DEV // 2


def _chip_sums(chunks, from_sibling, core, *, name):
    _, r, c = chunks.shape
    tr, tc = _shard_tile(r, c, CHIP_SUM_TILE_ELEMS)
    assert r % tr == 0 and c % tc == 0, (name, r, c)

    def body(core_ref, mine_ref, other_ref, o_ref):
        o_ref[...] = (mine_ref[...].astype(F32) + other_ref[...].astype(F32)).astype(o_ref.dtype)

    by_chip = pl.BlockSpec((None, tr, tc), lambda q, i, j, core_ref: (q, i, j))
    return pl.pallas_call(
        body,
        grid_spec=pltpu.PrefetchScalarGridSpec(
            num_scalar_prefetch=1, grid=(N_CHIPS, r // tr, c // tc),
            in_specs=[pl.BlockSpec((None, tr, tc), lambda q, i, j, core_ref: (2 * q + core_ref[0], i, j)), by_chip],
            out_specs=by_chip),
        out_shape=jax.ShapeDtypeStruct((N_CHIPS, r, c), chunks.dtype),
        compiler_params=_params("parallel", "parallel", "parallel"), name=name,
    )(core, chunks, from_sibling)


def _place():
    return lax.axis_index("x"), lax.axis_index("y"), lax.axis_index("c")


def _slot(px, py, pc):
    return 4 * px + 2 * py + pc


_HBM = pl.BlockSpec(memory_space=pltpu.HBM)


_SEM = pl.BlockSpec(memory_space=pltpu.SEMAPHORE)
_ANY = pl.BlockSpec(memory_space=pl.ANY)
_EFFECT = pltpu.SideEffectType.DATAFLOW_SIDE_EFFECTING
_N_PEERS = N_DEV - 1


def _peer(k, x, y, c):
    return (1 - x if k & 4 else x, 1 - y if k & 2 else y, 1 - c if k & 1 else c)


_EXCHANGE_BITS = {"gather_near": (1, 2, 4), "gather_relay": (6,), "gather_pass": (2, 4, 6),
                  "scatter_sibling": (1, 1, 1, 1), "scatter_chips": (2, 4, 6)}


def _exchange_copy(mode, src, land, w, i, place, send_sems, recv_sems, receiving):
    bits = _EXCHANGE_BITS[mode]
    k = bits[i]
    peer = _peer(k, *place)
    me = _slot(*place)
    if mode == "gather_near":
        to, src_ref, sent_to, got_at = peer, src[w], me, _slot(*peer)
    elif mode == "gather_relay":
        x, y, c = place
        other = 1 - c
        to = (lax.bitwise_xor(x, c), lax.bitwise_xor(y, other), c)
        blk = _slot(lax.bitwise_xor(x, other), lax.bitwise_xor(y, c), c)
        src_ref, sent_to, got_at = land[w].at[blk], blk, _slot(*peer)
    elif mode == "gather_pass":
        blk = _slot(*peer)
        to, src_ref, sent_to, got_at = _peer(1, *place), land[w].at[blk], blk, _slot(*_peer(k | 1, *place))
    elif mode == "scatter_sibling":
        to, src_ref, sent_to, got_at = peer, src[w].at[2 * i + 1 - place[2]], i, i
    else:
        to, src_ref, sent_to, got_at = peer, src[w].at[_slot(*peer) // 2], me // 2, _slot(*peer) // 2
    sem = w * len(bits) + i
    return pltpu.make_async_remote_copy(
        src_ref=src_ref, dst_ref=land[w].at[got_at if receiving else sent_to], send_sem=send_sems.at[sem],
        recv_sem=recv_sems.at[sem], device_id=to, device_id_type=MESH)


def _exchange_start(mode, srcs, lands, after, *, name):
    ns, nl = len(srcs), len(lands)
    n_sem = nl * len(_EXCHANGE_BITS[mode])

    def body(*refs):
        src, land = refs[:ns], refs[ns:ns + nl]
        send_sems, recv_sems = refs[ns + nl + 1:ns + nl + 3]
        token = refs[-1]
        place = _place()
        for w in range(nl):
            for i in range(len(_EXCHANGE_BITS[mode])):
                _exchange_copy(mode, src, land, w, i, place, send_sems, recv_sems, receiving=False).start()
        token[...] = jnp.zeros_like(token)

    sems = pltpu.SemaphoreType.DMA((n_sem,))
    arrays = list(srcs) + list(lands)
    res = pl.pallas_call(
        body, name=name, in_specs=[_HBM] * (ns + nl) + [_ANY],
        out_specs=(_SEM, _SEM, *([_HBM] * (ns + nl)), pl.BlockSpec(memory_space=pltpu.VMEM)),
        out_shape=(sems, sems, *[pltpu.HBM(a.shape, a.dtype) for a in arrays], jax.ShapeDtypeStruct((8, LANE), F32)),
        input_output_aliases={i: 2 + i for i in range(ns + nl)},
        compiler_params=pltpu.CompilerParams(has_side_effects=_EFFECT),
    )(*[pltpu.with_memory_space_constraint(a, pltpu.HBM) for a in arrays], after)
    return res[0], res[1], list(res[2:2 + ns]), list(res[2 + ns:2 + ns + nl]), res[-1]


def _exchange_wait(mode, started, after, *, name):
    send_sems, recv_sems, srcs, lands, _ = started
    ns, nl = len(srcs), len(lands)

    def body(*refs):
        src, land = refs[:ns], refs[ns:ns + nl]
        send_sems, recv_sems = refs[ns + nl:ns + nl + 2]
        place = _place()
        for w in range(nl):
            for i in range(len(_EXCHANGE_BITS[mode])):
                cp = _exchange_copy(mode, src, land, w, i, place, send_sems, recv_sems, receiving=True)
                cp.wait_send()
                cp.wait_recv()

    arrays = list(srcs) + list(lands)
    res = pl.pallas_call(
        body, name=name, in_specs=[_HBM] * (ns + nl) + [_SEM, _SEM, _ANY], out_specs=[_HBM] * (ns + nl),
        out_shape=[pltpu.HBM(a.shape, a.dtype) for a in arrays],
        input_output_aliases={i: i for i in range(ns + nl)},
        compiler_params=pltpu.CompilerParams(has_side_effects=_EFFECT),
    )(*arrays, send_sems, recv_sems, after)
    return list(res[:ns]), list(res[ns:])


def _small_allreduce_adamw(gvec, wvec, mvec, vvec):
    rows, length = gvec.shape

    def body(g_ref, w_ref, m_ref, v_ref, gs_ref, d_ref, nm_ref, nv_ref, slots, send_sems, recv_sems):
        x, y, c = _place()
        me = _slot(x, y, c)
        slots[me] = g_ref[...]
        sends = []
        for k in range(1, N_DEV):
            peer = _peer(k, x, y, c)
            sends.append(pltpu.make_async_remote_copy(
                src_ref=g_ref, dst_ref=slots.at[me], send_sem=send_sems.at[k - 1], recv_sem=recv_sems.at[k - 1],
                device_id=peer, device_id_type=MESH))
        for cp in sends:
            cp.start()
        for k in range(1, N_DEV):
            peer = _peer(k, x, y, c)
            pltpu.make_async_remote_copy(
                src_ref=g_ref, dst_ref=slots.at[_slot(*peer)], send_sem=send_sems.at[k - 1], recv_sem=recv_sems.at[k - 1],
                device_id=peer, device_id_type=MESH).wait_recv()
        for cp in sends:
            cp.wait_send()
        g = slots[0]
        for s in range(1, N_DEV):
            g = g + slots[s]
        delta, nm, nv = _adamw_math(w_ref[...], g, m_ref[...], v_ref[...])
        gs_ref[...] = g
        d_ref[...] = delta
        nm_ref[...] = nm
        nv_ref[...] = nv

    vmem = pl.BlockSpec(memory_space=pltpu.VMEM)
    out = jax.ShapeDtypeStruct((rows, length), F32)
    return pl.pallas_call(
        body, in_specs=[vmem] * 4, out_specs=[vmem] * 4, out_shape=[out] * 4,
        scratch_shapes=[pltpu.VMEM((N_DEV, rows, length), F32), pltpu.SemaphoreType.DMA((N_DEV - 1,)),
                        pltpu.SemaphoreType.DMA((N_DEV - 1,))],
        name="small_allreduce_adamw",
    )(gvec, wvec, mvec, vvec)


_SMALL_SEGMENTS = (("a_log", GDN_HEADS), ("dt_bias", GDN_HEADS), ("gdn_norm_w", HEAD_DIM), ("pool_scale", GDN_WIDTH),
                   ("ln1_g", D_MODEL), ("ln1_b", D_MODEL), ("ln2_g", D_MODEL), ("ln2_b", D_MODEL),
                   ("ln3_g", D_MODEL), ("ln3_b", D_MODEL), ("conv_w", CONV_K * QKV_COLS))
_SMALL_ROWS = 8
_SMALL_LEN = -(-sum(sz for _, sz in _SMALL_SEGMENTS) // (_SMALL_ROWS * LANE)) * LANE


def _pack_small(vals):
    parts = [vals[n].reshape(-1).astype(F32) if n in vals else jnp.zeros((sz,), F32) for n, sz in _SMALL_SEGMENTS]
    flat = jnp.concatenate(parts)
    flat = jnp.pad(flat, (0, _SMALL_ROWS * _SMALL_LEN - flat.shape[0]))
    return flat.reshape(_SMALL_ROWS, _SMALL_LEN)


def _unpack_small(vec):
    flat = vec.reshape(-1)
    out, off = {}, 0
    for n, sz in _SMALL_SEGMENTS:
        out[n] = flat[off:off + sz]
        off += sz
    return out


_WEIGHT_ORDER = ("w_in", "conv_w", "a_log", "dt_bias", "gdn_norm_w", "pool_w", "pool_scale", "w_out", "ln1_g", "ln1_b",
                 "xq_w", "xk_w", "xv_w", "xo_w", "ln2_g", "ln2_b", "w_up", "w_down", "ln3_g", "ln3_b")


def _shard2d(name, a):
    if name == "w_in":
        return a.T
    return a.reshape(-1, a.shape[-1]) if name == "pool_w" else a


def _update_view(name, a):
    return jnp.transpose(a, (2, 0, 1)) if name == "w_in" else _shard2d(name, a[0])


def _shard_result(name, r, shape):
    return jnp.transpose(r, (1, 2, 0)) if name == "w_in" else r.reshape(shape)


def _gathered_to_full(name, gth):
    if name in ("w_up", "w_in"):
        return gth
    if name == "conv_w":
        return jnp.transpose(gth, (1, 0, 2)).reshape(gth.shape[1], N_DEV * gth.shape[2])
    if name == "pool_w":
        g4 = gth.reshape(N_DEV, POOL_GROUPS, POOL_GROUP_DIM // N_DEV, POOL_GROUP_DIM)
        return jnp.transpose(g4, (1, 0, 2, 3)).reshape(POOL_GROUPS, POOL_GROUP_DIM, POOL_GROUP_DIM)
    return gth.reshape(N_DEV * gth.shape[1], gth.shape[2])


def _full_to_chunks(name, full):
    if name == "w_up":
        return full
    if name == "pool_w":
        g4 = full.reshape(POOL_GROUPS, N_DEV, POOL_GROUP_DIM // N_DEV, POOL_GROUP_DIM)
        return jnp.transpose(g4, (1, 0, 2, 3)).reshape(N_DEV, POOL_GROUPS * POOL_GROUP_DIM // N_DEV, POOL_GROUP_DIM)
    return full.reshape(N_DEV, full.shape[0] // N_DEV, full.shape[1])


_GATHER_GROUPS = (("mixer", ("w_in", "conv_w", "pool_w")), ("attn", ("w_out", "xq_w", "xk_w", "xv_w", "xo_w")),
                  ("up", ("w_up",)), ("down", ("w_down",)))


def _grad_chunks(name, g):
    if name == "w_in":
        return _w_in_chunks(g.astype(BF16))
    return _full_to_chunks(name, g.astype(BF16))


def kernel(x, mem, w_in, conv_w, a_log, dt_bias, gdn_norm_w, pool_w, pool_scale, w_out, ln1_g, ln1_b, xq_w, xk_w, xv_w, xo_w, ln2_g, ln2_b, w_up, w_down, ln3_g, ln3_b, loss_target, m_w_in, m_conv_w, m_a_log, m_dt_bias, m_gdn_norm_w, m_pool_w, m_pool_scale, m_w_out, m_ln1_g, m_ln1_b, m_xq_w, m_xk_w, m_xv_w, m_xo_w, m_ln2_g, m_ln2_b, m_w_up, m_w_down, m_ln3_g, m_ln3_b, v_w_in, v_conv_w, v_a_log, v_dt_bias, v_gdn_norm_w, v_pool_w, v_pool_scale, v_w_out, v_ln1_g, v_ln1_b, v_xq_w, v_xk_w, v_xv_w, v_xo_w, v_ln2_g, v_ln2_b, v_w_up, v_w_down, v_ln3_g, v_ln3_b):
    args = dict(locals())
    wt = {n: args[n][0] for n in _WEIGHT_ORDER}
    mo = {n: args["m_" + n][0] for n in _WEIGHT_ORDER}
    vo = {n: args["v_" + n][0] for n in _WEIGHT_ORDER}

    me = _slot(*_place())
    me_arr = jnp.reshape(me, (1,)).astype(jnp.int32)
    nothing = jnp.zeros((8, LANE), F32)

    def landing_zones(names):
        shards = [_shard2d(n, wt[n]).astype(F32 if n == "conv_w" else BF16) for n in names]
        zones = [lax.dynamic_update_slice(lax.empty((N_DEV, *s.shape), s.dtype), s[None], (me, 0, 0)) for s in shards]
        return shards, zones

    chip_arr = jnp.reshape(me // 2, (1,)).astype(jnp.int32)
    core_arr = jnp.reshape(lax.axis_index("c"), (1,)).astype(jnp.int32)
    names_of = dict(_GATHER_GROUPS)
    gathers = {}
    prepared = {}

    def gather_near(group, after):
        shards, zones = prepared.pop(group) if group in prepared else landing_zones(names_of[group])
        gathers[group] = _exchange_start("gather_near", shards, zones, after, name="gather_near_" + group)
        return gathers[group][4]

    def gather_next(group, was, now, after):
        _, zones = _exchange_wait(was, gathers[group], after, name=f"{was}_{group}_wait")
        gathers[group] = _exchange_start(now, [], zones, nothing, name=f"{now}_{group}")
        return gathers[group][4]

    def gather_relay(group, after):
        return gather_next(group, "gather_near", "gather_relay", after)

    def gather_pass(group, after):
        return gather_next(group, "gather_relay", "gather_pass", after)

    def gathered(group, after):
        _, zones = _exchange_wait("gather_pass", gathers[group], after, name=f"gather_pass_{group}_wait")
        full = {n: _gathered_to_full(n, z) for n, z in zip(names_of[group], zones)}
        full.update({n: wt[n] for n in _VECTORS})
        return _group_weights(group, full)

    token = gather_near("mixer", nothing)
    x16 = _cast_bf16(x[0], name="cast_x")
    later = {group: landing_zones(names_of[group]) for group in ("attn", "up", "down")}
    token, x16, later = lax.optimization_barrier((token, x16, later))
    prepared.update(later)
    token = gather_pass("mixer", gather_relay("mixer", token))
    token = gather_near("attn", token)

    def weights_of(group, after):
        if group == "mixer":
            return gathered(group, token)
        if group == "ahead_conv":
            return gather_near("up", gather_relay("attn", after))[0:1, 0:1]
        if group == "ahead_scan":
            return gather_near("down", gather_relay("up", gather_pass("attn", after)))[0:1, 0:1]
        if group == "ahead_attn":
            return gather_relay("down", gather_pass("up", after))[0:1, 0:1]
        if group == "up":
            return gathered(group, gather_pass("down", after))
        return gathered(group, after)

    scatters = {}
    in_flight = []

    def chip_stage(after):
        group, names, started = in_flight.pop()
        chunks, from_sibling = _exchange_wait("scatter_sibling", started, after, name=f"scatter_sibling_{group}_wait")
        sums = [_chip_sums(c, f, core_arr, name=f"chip_sums_{n}") for n, c, f in zip(names, chunks, from_sibling)]
        scatters[group] = (names, _exchange_start("scatter_chips", sums, [lax.empty(s.shape, s.dtype) for s in sums],
                                                  nothing, name="scatter_chips_" + group))
        return scatters[group][1][4]

    def grads_ready(group, grads):
        if group == "tick":
            return chip_stage(grads["after"])[0:1, 0:1] if in_flight else None
        names = tuple(grads)
        chunks = [_grad_chunks(n, grads[n]) for n in names]
        token = chip_stage(chunks[0]) if in_flight else nothing
        zones = [lax.empty((N_CHIPS, *c.shape[1:]), c.dtype) for c in chunks]
        started = _exchange_start("scatter_sibling", chunks, zones, token, name="scatter_sibling_" + group)
        in_flight.append((group, names, started))
        return started[4][0:1, 0:1]

    sq, grad_x, g = _local_step(x[0], x16, mem[0], loss_target[0], weights_of, grads_ready)
    small = _finish_small_grads(g)

    out = {}
    after = chip_stage(grad_x)
    for group, (names, started) in scatters.items():
        sums, lands = _exchange_wait("scatter_chips", started, after, name=f"scatter_chips_{group}_wait")
        for n, parts, own in zip(names, lands, sums):
            res = _adamw_shard(parts, own, chip_arr, _update_view(n, args[n]), _update_view(n, args["m_" + n]),
                               _update_view(n, args["v_" + n]), name="adamw_" + n)
            out[n] = [_shard_result(n, r, args[n].shape) for r in res]
            after = res[1]

    packed, _ = lax.optimization_barrier((_pack_small(small), after))
    gs, ds, ms, vs = _small_allreduce_adamw(
        packed, _pack_small({n: wt[n] for n in _VECTORS}), _pack_small({n: mo[n] for n in _VECTORS}),
        _pack_small({n: vo[n] for n in _VECTORS}))
    gs, ds, ms, vs = _unpack_small(gs), _unpack_small(ds), _unpack_small(ms), _unpack_small(vs)
    cols = conv_w.shape[-1]
    conv_full = gs["conv_w"].reshape(CONV_K, QKV_COLS)
    conv_mine = lax.dynamic_slice(conv_full, (0, me * cols), (CONV_K, cols))[None]
    res = _adamw_shard(conv_mine, conv_mine, jnp.zeros((1,), jnp.int32), wt["conv_w"], mo["conv_w"], vo["conv_w"],
                       name="adamw_conv_w")
    out["conv_w"] = [r.reshape(conv_w.shape) for r in res]
    for n in _VECTORS:
        out[n] = [t[n].reshape(args[n].shape) for t in (gs, ds, ms, vs)]

    loss = lax.psum(0.5 * sq[0, 0] / D_MODEL, ("x", "y", "c"))
    return (loss, grad_x[None], *[out[n][0] for n in _WEIGHT_ORDER], *[out[n][1] for n in _WEIGHT_ORDER],
            *[out[n][2] for n in _WEIGHT_ORDER], *[out[n][3] for n in _WEIGHT_ORDER])
```

```python
import functools
import math

import jax
import jax.numpy as jnp
from jax import lax
from jax.experimental import pallas as pl
from jax.experimental.pallas import tpu as pltpu

F32 = jnp.float32
BF16 = jnp.bfloat16
MESH = pl.DeviceIdType.MESH

N_DEV = 8
D_MODEL = 2048
GDN_WIDTH = 1024
GDN_HEADS = 8
HEAD_DIM = 128
CONV_K = 4
CHUNK = 64
POOL_GROUPS = 4
POOL_GROUP_DIM = 256
MEM_LEN = 256
XATTN_HEADS = 4
XATTN_HEAD_DIM = 512
D_FF = 8192
IN_COLS = 5136
ALPHA = 2.0 ** 0.25
LN_EPS = 1e-5
NORM_EPS = 1e-6

LANE = 128
QKV_COLS = 3 * GDN_WIDTH
Z_OFF = QKV_COLS
BA_OFF = 4 * GDN_WIDTH
POOL_OFF = BA_OFF + 2 * LANE
PROJ_COLS = POOL_OFF + GDN_WIDTH
Z_BLK = Z_OFF // LANE
BA_BLK = BA_OFF // LANE
POOL_BLK = POOL_OFF // POOL_GROUP_DIM

ADAM_LR = 0.001
ADAM_B1 = 0.9
ADAM_B2 = 0.999
ADAM_EPS = 1e-08
ADAM_WD = 0.01
ADAM_STEP = 10

VMEM_LIMIT_BYTES = 48 * 1024 * 1024


def _params(*sem):
    return pltpu.CompilerParams(dimension_semantics=sem if sem else None, vmem_limit_bytes=VMEM_LIMIT_BYTES)


def _make_dots(cast, precision, batched=False):
    lead = 1 if batched else 0
    batch = ((0,), (0,)) if batched else ((), ())

    def dg(a, b, ca, cb):
        if cast is not None:
            a = a.astype(cast)
            b = b.astype(cast)
        return lax.dot_general(a, b, (((ca + lead,), (cb + lead,)), batch), precision=precision, preferred_element_type=F32)

    def nn_(a, b):
        return dg(a, b, 1, 0)

    def nt_(a, b):
        return dg(a, b, 1, 1)

    def tn_(a, b):
        return dg(a, b, 0, 0)

    @jax.custom_vjp
    def nn(a, b):
        return nn_(a, b)

    nn.defvjp(lambda a, b: (nn_(a, b), (a, b)), lambda r, g: (nt_(g, r[1]), tn_(r[0], g)))

    @jax.custom_vjp
    def nt(a, b):
        return nt_(a, b)

    nt.defvjp(lambda a, b: (nt_(a, b), (a, b)), lambda r, g: (nn_(g, r[1]), tn_(g, r[0])))

    @jax.custom_vjp
    def tn(a, b):
        return tn_(a, b)

    tn.defvjp(lambda a, b: (tn_(a, b), (a, b)), lambda r, g: (nt_(r[1], g), nn_(r[0], g)))

    return (nn_, nt_, tn_), (nn, nt, tn)


_BDOT_PLAIN, _BDOT_VJP = _make_dots(BF16, None)
_BDOT_BATCH_PLAIN, _BDOT_BATCH_VJP = _make_dots(BF16, None, batched=True)
_FDOT_BATCH_PLAIN, _FDOT_BATCH_VJP = _make_dots(BF16, None, batched=True)


def _mm(a, b, *, ta=False, tb=False, out_dtype=F32, tm=None, tn=512, tk=None, epi=None, extra=None, add_scale=1.0,
        b_chunks=False, o_chunks=False, name):
    m, k = (a.shape[1], a.shape[0]) if ta else a.shape
    if b_chunks:
        n, kb = (b.shape[1], N_DEV * b.shape[2]) if tb else (N_DEV * b.shape[2], b.shape[1])
    else:
        n, kb = b.shape if tb else (b.shape[1], b.shape[0])
    assert kb == k, (name, a.shape, b.shape)
    tm, tn, tk = min(tm or m, m), min(tn, n), min(tk or k, k)
    assert m % tm == 0 and n % tn == 0 and k % tk == 0, (name, m, n, k)
    nk = k // tk
    dims = (((0 if ta else 1,), (1 if tb else 0,)), ((), ()))
    n_extra = 0 if epi in (None, "relu2") else 1
    n_out = 2 if epi == "relu2" else 1
    if epi in ("relu2", "mul2r"):
        out_dtype = BF16

    def body(*refs):
        a_ref, b_ref = refs[:2]
        c_ref = refs[2] if n_extra else None
        o_refs = refs[2 + n_extra:2 + n_extra + n_out]
        scr = refs[2 + n_extra + n_out:]
        r = lax.dot_general(a_ref[...].astype(BF16), b_ref[...].astype(BF16), dims, preferred_element_type=F32)

        def finish(v):
            if epi == "add":
                o_refs[0][...] = (v + add_scale * c_ref[...]).astype(out_dtype)
            elif epi == "relu2":
                p = jnp.maximum(v, 0.0)
                o_refs[0][...] = (p * p).astype(BF16)
                o_refs[1][...] = p.astype(BF16)
            elif epi == "mul2r":
                o_refs[0][...] = (v * (2.0 * c_ref[...].astype(F32))).astype(BF16)
            else:
                o_refs[0][...] = v.astype(out_dtype)

        if nk == 1:
            finish(r)
        else:
            acc = scr[0]
            kk = pl.program_id(2)

            @pl.when(kk == 0)
            def _():
                acc[...] = r

            @pl.when(kk > 0)
            def _():
                acc[...] += r

            @pl.when(kk == nk - 1)
            def _():
                finish(acc[...])

    a_spec = pl.BlockSpec((tk, tm), lambda i, j, kk: (kk, i)) if ta else pl.BlockSpec((tm, tk), lambda i, j, kk: (i, kk))
    if b_chunks and tb:
        kc = k // N_DEV // tk
        b_spec = pl.BlockSpec((None, tn, tk), lambda i, j, kk: (kk // kc, j, kk % kc))
    elif b_chunks:
        nc = n // N_DEV // tn
        b_spec = pl.BlockSpec((None, tk, tn), lambda i, j, kk: (j // nc, kk, j % nc))
    elif tb:
        b_spec = pl.BlockSpec((tn, tk), lambda i, j, kk: (j, kk))
    else:
        b_spec = pl.BlockSpec((tk, tn), lambda i, j, kk: (kk, j))
    mn_spec = pl.BlockSpec((tm, tn), lambda i, j, kk: (i, j))
    if o_chunks:
        oc = n // N_DEV // tn
        o_spec = pl.BlockSpec((None, tm, tn), lambda i, j, kk: (j // oc, i, j % oc))
        o_shape = jax.ShapeDtypeStruct((N_DEV, m, n // N_DEV), out_dtype)
    else:
        o_spec, o_shape = mn_spec, jax.ShapeDtypeStruct((m, n), out_dtype)
    res = pl.pallas_call(
        body, grid=(m // tm, n // tn, nk), in_specs=[a_spec, b_spec] + [mn_spec] * n_extra,
        out_specs=[o_spec] * n_out, out_shape=[o_shape] * n_out,
        scratch_shapes=[pltpu.VMEM((tm, tn), F32)] if nk > 1 else [],
        compiler_params=_params("parallel", "parallel", "arbitrary"), name=name,
    )(a, b, *([extra] if n_extra else []))
    return res if n_out > 1 else res[0]


def _cast_bf16(v, *, name, tm=512):
    t, d = v.shape
    tm = min(tm, t)

    def body(v_ref, o_ref):
        o_ref[...] = v_ref[...].astype(BF16)

    spec = pl.BlockSpec((tm, d), lambda i: (i, 0))
    return pl.pallas_call(body, grid=(t // tm,), in_specs=[spec], out_specs=spec,
                          out_shape=jax.ShapeDtypeStruct((t, d), BF16), compiler_params=_params("parallel"), name=name)(v)


def _shift_down(v, s):
    if s == 0:
        return v
    row = lax.broadcasted_iota(jnp.int32, v.shape, 0)
    return jnp.where(row >= s, pltpu.roll(v, s, axis=0), 0.0)


def _shift_up(v, s):
    if s == 0:
        return v
    t = v.shape[0]
    row = lax.broadcasted_iota(jnp.int32, v.shape, 0)
    return jnp.where(row < t - s, pltpu.roll(v, t - s, axis=0), 0.0)


def _post_col(j):
    return (j % GDN_HEADS) * 3 + j // GDN_HEADS


def _gdn_prep_fwd(proj, conv_w):
    t = proj.shape[0]

    def body(x_ref, w_ref, o_ref):
        j = pl.program_id(0)
        x = x_ref[...]
        y = jnp.zeros_like(x)
        for tap in range(CONV_K):
            y = y + w_ref[tap:tap + 1, :] * _shift_down(x, CONV_K - 1 - tap)
        c = y * jax.nn.sigmoid(y)
        nrm = c * lax.rsqrt(jnp.sum(c * c, axis=1, keepdims=True) + NORM_EPS)
        o_ref[...] = jnp.where(j < 2 * GDN_HEADS, nrm, c)

    return pl.pallas_call(
        body, grid=(QKV_COLS // LANE,),
        in_specs=[pl.BlockSpec((t, LANE), lambda j: (0, j)), pl.BlockSpec((CONV_K, LANE), lambda j: (0, j))],
        out_specs=pl.BlockSpec((t, LANE), lambda j: (0, _post_col(j))),
        out_shape=jax.ShapeDtypeStruct((t, QKV_COLS), F32),
        compiler_params=_params("parallel"), name="gdn_prep_fwd",
    )(proj, conv_w)


def _gdn_prep_bwd(proj, conv_w, dpost, dproj):
    t = proj.shape[0]

    def body(x_ref, w_ref, d_ref, _, dx_ref, dw_ref):
        j = pl.program_id(0)
        x = x_ref[...]
        xs = [_shift_down(x, CONV_K - 1 - tap) for tap in range(CONV_K)]
        y = jnp.zeros_like(x)
        for tap in range(CONV_K):
            y = y + w_ref[tap:tap + 1, :] * xs[tap]
        sig = jax.nn.sigmoid(y)
        c = y * sig
        r = lax.rsqrt(jnp.sum(c * c, axis=1, keepdims=True) + NORM_EPS)
        nrm = c * r
        d = d_ref[...]
        dc_norm = r * (d - nrm * jnp.sum(d * nrm, axis=1, keepdims=True))
        dc = jnp.where(j < 2 * GDN_HEADS, dc_norm, d)
        dy = dc * (sig * (1.0 + y * (1.0 - sig)))
        dx = jnp.zeros_like(x)
        for tap in range(CONV_K):
            dx = dx + _shift_up(w_ref[tap:tap + 1, :] * dy, CONV_K - 1 - tap)
            dw_ref[tap:tap + 1, :] = jnp.sum(dy * xs[tap], axis=0, keepdims=True)
        dx_ref[...] = dx.astype(dx_ref.dtype)

    return pl.pallas_call(
        body, grid=(QKV_COLS // LANE,),
        in_specs=[pl.BlockSpec((t, LANE), lambda j: (0, j)), pl.BlockSpec((CONV_K, LANE), lambda j: (0, j)),
                  pl.BlockSpec((t, LANE), lambda j: (0, _post_col(j))), pl.BlockSpec(memory_space=pl.ANY)],
        out_specs=[pl.BlockSpec((t, LANE), lambda j: (0, j)), pl.BlockSpec((CONV_K, LANE), lambda j: (0, j))],
        out_shape=[jax.ShapeDtypeStruct(dproj.shape, dproj.dtype), jax.ShapeDtypeStruct((CONV_K, QKV_COLS), F32)],
        input_output_aliases={3: 0},
        compiler_params=_params("parallel"), name="gdn_prep_bwd",
    )(proj, conv_w, dpost, dproj)


def _softplus(v):
    return jnp.maximum(v, 0.0) + jnp.log(1.0 + jnp.exp(-jnp.abs(v)))


def _tri_inv(low, nn):
    r = lax.broadcasted_iota(jnp.int32, (CHUNK, CHUNK), 0)
    c = lax.broadcasted_iota(jnp.int32, (CHUNK, CHUNK), 1)
    eye = (r == c).astype(F32)
    same_blk = lax.shift_right_logical(r, 4) == lax.shift_right_logical(c, 4)
    diag = jnp.where(same_blk, low, 0.0)
    off = low - diag
    n1 = -diag
    n2 = nn(n1, n1)
    n4 = nn(n2, n2)
    n8 = nn(n4, n4)
    inv_d = nn(nn(nn(eye + n1, eye + n2), eye + n4), eye + n8)
    m1 = nn(inv_d, off)
    m2 = nn(m1, m1)
    return nn(nn(eye - m1, eye + m2), inv_d)


@jax.custom_vjp
def _tri_inv_known(low, t_inv):
    return t_inv


def _tri_inv_known_fwd(low, t_inv):
    return t_inv, t_inv


def _tri_inv_known_bwd(t_inv, g):
    _, nt, tn = _FDOT_BATCH_PLAIN
    return -nt(tn(t_inv, g), t_inv), jnp.zeros_like(t_inv)


_tri_inv_known.defvjp(_tri_inv_known_fwd, _tri_inv_known_bwd)


LOCAL_HEADS_PER_STEP = 8


def _gdn_local_fn(qkv, ba, alog_row, dtb_row, first_head, bdots, fdots, t_known=None):
    nn, nt, tn = bdots
    fnn = fdots[0]
    n_heads = qkv.shape[1] // (3 * HEAD_DIM)
    part = lambda i, p: qkv[:, (3 * i + p) * HEAD_DIM:(3 * i + p + 1) * HEAD_DIM]
    q = jnp.stack([part(i, 0) for i in range(n_heads)]) * (HEAD_DIM ** -0.5)
    k = jnp.stack([part(i, 1) for i in range(n_heads)])
    v = jnp.stack([part(i, 2) for i in range(n_heads)])
    lane = lax.broadcasted_iota(jnp.int32, ba.shape, 1)
    bg = jnp.where(lane < GDN_HEADS, jax.nn.sigmoid(ba), -jnp.exp(alog_row) * _softplus(ba + dtb_row))
    pick = lambda l: jnp.sum(jnp.where(lane == l, bg, 0.0), axis=1, keepdims=True)
    beta = jnp.stack([pick(first_head + i) for i in range(n_heads)])
    g = jnp.stack([pick(first_head + i + GDN_HEADS) for i in range(n_heads)])

    r = lax.broadcasted_iota(jnp.int32, (CHUNK, CHUNK), 0)
    c = lax.broadcasted_iota(jnp.int32, (CHUNK, CHUNK), 1)
    incl = r >= c
    strict = r > c
    eye = r == c

    def to_row(col):
        return jnp.sum(jnp.where(eye, col, 0.0), axis=1, keepdims=True)

    gc = jnp.sum(jnp.where(incl, to_row(g), 0.0), axis=2, keepdims=True)
    diff = gc - to_row(gc)
    decay = jnp.where(incl, jnp.exp(jnp.where(incl, diff, 0.0)), 0.0)
    k_beta = k * beta
    v_beta = v * beta
    low = jnp.where(strict, nt(k_beta, k) * decay, 0.0)
    t_inv = _tri_inv(low, fnn) if t_known is None else _tri_inv_known(low, t_known)
    eg = jnp.exp(gc)
    u = fnn(t_inv, v_beta)
    w = fnn(t_inv, k_beta * eg)
    attn = jnp.where(incl, nt(q, k) * decay, 0.0)
    last = lax.broadcasted_iota(jnp.int32, (CHUNK, 1), 0) == CHUNK - 1
    g_last = jnp.sum(jnp.where(last, gc, 0.0), axis=1, keepdims=True)
    kdec = k * jnp.exp(g_last - gc)
    elast = jnp.broadcast_to(jnp.exp(g_last), (n_heads, 1, LANE))
    return u, w, q * eg, kdec, attn, elast, t_inv


def _gdn_state_fn(u, w, qg, kdec, attn, elast, state, bdots):
    nn, _, tn = bdots
    v_new = u - nn(w, state)
    o = nn(qg, state) + nn(attn, v_new)
    return o, state * elast + tn(kdec, v_new)


def _gdn_local_fwd(post, proj, alog_row, dtb_row):
    t = post.shape[0]
    n_chunks = t // CHUNK
    hb = LOCAL_HEADS_PER_STEP

    def body(qkv_ref, ba_ref, al_ref, dt_ref, u_ref, w_ref, qg_ref, kd_ref, at_ref, el_ref, ti_ref):
        u, w, qg, kdec, attn, elast, t_inv = _gdn_local_fn(qkv_ref[...], ba_ref[...], al_ref[...], dt_ref[...],
                                                           pl.program_id(1) * hb, _BDOT_BATCH_PLAIN, _FDOT_BATCH_PLAIN)
        for i in range(hb):
            cols = slice(i * HEAD_DIM, (i + 1) * HEAD_DIM)
            u_ref[:, cols] = u[i]
            w_ref[:, cols] = w[i].astype(BF16)
            qg_ref[:, cols] = qg[i].astype(BF16)
            kd_ref[:, cols] = kdec[i].astype(BF16)
        at_ref[...] = attn.astype(BF16)
        el_ref[:, 0] = elast
        ti_ref[...] = t_inv

    wide = pl.BlockSpec((CHUNK, hb * HEAD_DIM), lambda n, j: (n, j))
    square = pl.BlockSpec((hb, CHUNK, CHUNK), lambda n, j: (j, n, 0))
    row = pl.BlockSpec((1, LANE), lambda n, j: (0, 0))
    res = pl.pallas_call(
        body, grid=(n_chunks, GDN_HEADS // hb),
        in_specs=[pl.BlockSpec((CHUNK, hb * 3 * HEAD_DIM), lambda n, j: (n, j)),
                  pl.BlockSpec((CHUNK, LANE), lambda n, j: (n, BA_BLK)), row, row],
        out_specs=[wide, wide, wide, wide, square, pl.BlockSpec((hb, 1, 1, LANE), lambda n, j: (j, n, 0, 0)), square],
        out_shape=[jax.ShapeDtypeStruct((t, GDN_WIDTH), F32), jax.ShapeDtypeStruct((t, GDN_WIDTH), BF16),
                   jax.ShapeDtypeStruct((t, GDN_WIDTH), BF16), jax.ShapeDtypeStruct((t, GDN_WIDTH), BF16),
                   jax.ShapeDtypeStruct((GDN_HEADS, t, CHUNK), BF16),
                   jax.ShapeDtypeStruct((GDN_HEADS, n_chunks, 1, LANE), F32),
                   jax.ShapeDtypeStruct((GDN_HEADS, t, CHUNK), F32)],
        compiler_params=_params("parallel", "parallel"), name="gdn_local_fwd",
    )(post, proj, alog_row, dtb_row)
    return tuple(res[:6]), res[6]


def _by_head(ref):
    return jnp.stack([ref[:, h * HEAD_DIM:(h + 1) * HEAD_DIM] for h in range(ref.shape[1] // HEAD_DIM)])


def _gdn_state_specs(n_of):
    wide = pl.BlockSpec((CHUNK, GDN_WIDTH), lambda n: (n_of(n), 0))
    attn = pl.BlockSpec((GDN_HEADS, CHUNK, CHUNK), lambda n: (0, n_of(n), 0))
    elast = pl.BlockSpec((GDN_HEADS, 1, 1, LANE), lambda n: (0, n_of(n), 0, 0))
    saved = pl.BlockSpec((GDN_HEADS, 1, HEAD_DIM, HEAD_DIM), lambda n: (0, n_of(n), 0, 0))
    return wide, attn, elast, saved


def _gdn_state_fwd(u, w, qg, kdec, attn, elast):
    t = u.shape[0]
    n_chunks = t // CHUNK

    def body(u_ref, w_ref, qg_ref, kd_ref, at_ref, el_ref, o_ref, save_ref, state_ref):
        @pl.when(pl.program_id(0) == 0)
        def _():
            state_ref[...] = jnp.zeros_like(state_ref)

        state = state_ref[...]
        save_ref[:, 0] = state
        o, new_state = _gdn_state_fn(_by_head(u_ref), _by_head(w_ref), _by_head(qg_ref), _by_head(kd_ref), at_ref[...],
                                     el_ref[:, 0], state, _BDOT_BATCH_PLAIN)
        for h in range(GDN_HEADS):
            o_ref[:, h * HEAD_DIM:(h + 1) * HEAD_DIM] = o[h]
        state_ref[...] = new_state

    wide, attn_spec, elast_spec, saved_spec = _gdn_state_specs(lambda n: n)
    return pl.pallas_call(
        body, grid=(n_chunks,), in_specs=[wide, wide, wide, wide, attn_spec, elast_spec],
        out_specs=[wide, saved_spec],
        out_shape=[jax.ShapeDtypeStruct((t, GDN_WIDTH), F32),
                   jax.ShapeDtypeStruct((GDN_HEADS, n_chunks, HEAD_DIM, HEAD_DIM), F32)],
        scratch_shapes=[pltpu.VMEM((GDN_HEADS, HEAD_DIM, HEAD_DIM), F32)],
        compiler_params=_params("arbitrary"), name="gdn_state_fwd",
    )(u, w, qg, kdec, attn, elast)


def _gdn_state_bwd(u, w, qg, kdec, attn, elast, saved, do):
    t = u.shape[0]
    n_chunks = t // CHUNK
    last = n_chunks - 1

    def body(u_ref, w_ref, qg_ref, kd_ref, at_ref, el_ref, save_ref, do_ref,
             du_ref, dw_ref, dqg_ref, dkd_ref, dat_ref, del_ref, dstate_ref):
        @pl.when(pl.program_id(0) == 0)
        def _():
            dstate_ref[...] = jnp.zeros_like(dstate_ref)

        _, vjp = jax.vjp(
            lambda *a: _gdn_state_fn(*a, _BDOT_BATCH_VJP), _by_head(u_ref), _by_head(w_ref).astype(F32),
            _by_head(qg_ref).astype(F32), _by_head(kd_ref).astype(F32), at_ref[...].astype(F32), el_ref[:, 0],
            save_ref[:, 0])
        du, dw, dqg, dkd, dat, de, dstate = vjp((_by_head(do_ref), dstate_ref[...]))
        for h in range(GDN_HEADS):
            cols = slice(h * HEAD_DIM, (h + 1) * HEAD_DIM)
            du_ref[:, cols] = du[h]
            dw_ref[:, cols] = dw[h]
            dqg_ref[:, cols] = dqg[h]
            dkd_ref[:, cols] = dkd[h]
        dat_ref[...] = dat
        del_ref[:, 0] = de
        dstate_ref[...] = dstate

    wide, attn_spec, elast_spec, saved_spec = _gdn_state_specs(lambda n: last - n)
    wide_f32 = jax.ShapeDtypeStruct((t, GDN_WIDTH), F32)
    return pl.pallas_call(
        body, grid=(n_chunks,), in_specs=[wide, wide, wide, wide, attn_spec, elast_spec, saved_spec, wide],
        out_specs=[wide, wide, wide, wide, attn_spec, elast_spec],
        out_shape=[wide_f32, wide_f32, wide_f32, wide_f32, jax.ShapeDtypeStruct((GDN_HEADS, t, CHUNK), F32),
                   jax.ShapeDtypeStruct((GDN_HEADS, n_chunks, 1, LANE), F32)],
        scratch_shapes=[pltpu.VMEM((GDN_HEADS, HEAD_DIM, HEAD_DIM), F32)],
        compiler_params=_params("arbitrary"), name="gdn_state_bwd",
    )(u, w, qg, kdec, attn, elast, saved, do)


def _gdn_local_bwd(post, proj, alog_row, dtb_row, t_inv, cots, dproj):
    t = post.shape[0]
    n_chunks = t // CHUNK
    hb = LOCAL_HEADS_PER_STEP
    n_steps = GDN_HEADS // hb

    def body(qkv_ref, ba_ref, al_ref, dt_ref, ti_ref, du_ref, dw_ref, dqg_ref, dkd_ref, dat_ref, del_ref, _,
             dqkv_ref, dba_ref, dal_ref, ddt_ref, dba_acc):
        n = pl.program_id(0)
        j = pl.program_id(1)

        @pl.when((n == 0) & (j == 0))
        def _():
            dal_ref[...] = jnp.zeros_like(dal_ref)
            ddt_ref[...] = jnp.zeros_like(ddt_ref)

        @pl.when(j == 0)
        def _():
            dba_acc[...] = jnp.zeros_like(dba_acc)

        t_known = ti_ref[...]
        _, vjp = jax.vjp(
            lambda a, b, c, d: _gdn_local_fn(a, b, c, d, j * hb, _BDOT_BATCH_VJP, _FDOT_BATCH_VJP, t_known)[:6],
            qkv_ref[...], ba_ref[...], al_ref[...], dt_ref[...])
        dqkv, dba, dal, ddt = vjp((_by_head(du_ref), _by_head(dw_ref), _by_head(dqg_ref), _by_head(dkd_ref), dat_ref[...],
                                   del_ref[:, 0]))
        dqkv_ref[...] = dqkv
        dba_acc[...] += dba
        dal_ref[...] += dal
        ddt_ref[...] += ddt

        @pl.when(j == n_steps - 1)
        def _():
            dba_ref[:, 0:LANE] = dba_acc[...].astype(dba_ref.dtype)
            dba_ref[:, LANE:2 * LANE] = jnp.zeros((CHUNK, LANE), dba_ref.dtype)

    wide = pl.BlockSpec((CHUNK, hb * HEAD_DIM), lambda n, j: (n, j))
    qkv_spec = pl.BlockSpec((CHUNK, hb * 3 * HEAD_DIM), lambda n, j: (n, j))
    row = pl.BlockSpec((1, LANE), lambda n, j: (0, 0))
    return pl.pallas_call(
        body, grid=(n_chunks, n_steps),
        in_specs=[qkv_spec, pl.BlockSpec((CHUNK, LANE), lambda n, j: (n, BA_BLK)), row, row,
                  pl.BlockSpec((hb, CHUNK, CHUNK), lambda n, j: (j, n, 0)), wide, wide, wide, wide,
                  pl.BlockSpec((hb, CHUNK, CHUNK), lambda n, j: (j, n, 0)),
                  pl.BlockSpec((hb, 1, 1, LANE), lambda n, j: (j, n, 0, 0)), pl.BlockSpec(memory_space=pl.ANY)],
        out_specs=[qkv_spec, pl.BlockSpec((CHUNK, 2 * LANE), lambda n, j: (n, BA_BLK // 2)), row, row],
        out_shape=[jax.ShapeDtypeStruct((t, QKV_COLS), F32), jax.ShapeDtypeStruct(dproj.shape, dproj.dtype),
                   jax.ShapeDtypeStruct((1, LANE), F32), jax.ShapeDtypeStruct((1, LANE), F32)],
        input_output_aliases={11: 1},
        scratch_shapes=[pltpu.VMEM((CHUNK, LANE), F32)],
        compiler_params=_params("arbitrary", "arbitrary"), name="gdn_local_bwd",
    )(post, proj, alog_row, dtb_row, t_inv, *cots, dproj)


def _onorm_fn(o, z, w):
    return o * lax.rsqrt(jnp.mean(o * o, axis=1, keepdims=True) + NORM_EPS) * w * (z * jax.nn.sigmoid(z))


def _onorm_fwd(o_raw, proj, norm_w, mixin, tm=512):
    t = o_raw.shape[0]
    tm = min(tm, t)

    def body(o_ref, z_ref, w_ref, _, out_ref):
        out_ref[...] = _onorm_fn(o_ref[...], z_ref[...], w_ref[...]).astype(out_ref.dtype)

    return pl.pallas_call(
        body, grid=(t // tm, GDN_HEADS),
        in_specs=[pl.BlockSpec((tm, LANE), lambda i, h: (i, h)), pl.BlockSpec((tm, LANE), lambda i, h: (i, Z_BLK + h)),
                  pl.BlockSpec((1, LANE), lambda i, h: (0, 0)), pl.BlockSpec(memory_space=pl.ANY)],
        out_specs=pl.BlockSpec((tm, LANE), lambda i, h: (i, h)),
        out_shape=jax.ShapeDtypeStruct(mixin.shape, mixin.dtype), input_output_aliases={3: 0},
        compiler_params=_params("parallel", "parallel"), name="gdn_onorm_fwd",
    )(o_raw, proj, norm_w, mixin)


def _onorm_bwd(o_raw, proj, norm_w, dmixin, dproj, tm=512):
    t = o_raw.shape[0]
    tm = min(tm, t)

    def body(o_ref, z_ref, w_ref, d_ref, _, do_ref, dz_ref, dw_ref):
        @pl.when((pl.program_id(0) == 0) & (pl.program_id(1) == 0))
        def _():
            dw_ref[...] = jnp.zeros_like(dw_ref)

        _, vjp = jax.vjp(_onorm_fn, o_ref[...], z_ref[...], w_ref[...])
        do, dz, dw = vjp(d_ref[...])
        do_ref[...] = do
        dz_ref[...] = dz.astype(dz_ref.dtype)
        dw_ref[...] += dw

    return pl.pallas_call(
        body, grid=(t // tm, GDN_HEADS),
        in_specs=[pl.BlockSpec((tm, LANE), lambda i, h: (i, h)), pl.BlockSpec((tm, LANE), lambda i, h: (i, Z_BLK + h)),
                  pl.BlockSpec((1, LANE), lambda i, h: (0, 0)), pl.BlockSpec((tm, LANE), lambda i, h: (i, h)),
                  pl.BlockSpec(memory_space=pl.ANY)],
        out_specs=[pl.BlockSpec((tm, LANE), lambda i, h: (i, h)), pl.BlockSpec((tm, LANE), lambda i, h: (i, Z_BLK + h)),
                   pl.BlockSpec((1, LANE), lambda i, h: (0, 0))],
        out_shape=[jax.ShapeDtypeStruct((t, GDN_WIDTH), F32), jax.ShapeDtypeStruct(dproj.shape, dproj.dtype),
                   jax.ShapeDtypeStruct((1, LANE), F32)],
        input_output_aliases={4: 1},
        compiler_params=_params("arbitrary", "arbitrary"), name="gdn_onorm_bwd",
    )(o_raw, proj, norm_w, dmixin, dproj)


def _pool_select(levels, gi):
    out = levels[-1]
    for lvl in range(len(levels) - 2, -1, -1):
        out = jnp.where(gi == lvl, levels[lvl], out)
    return out


def _pool_count(shape, gi):
    pos = lax.broadcasted_iota(jnp.int32, shape, 0)
    win = lax.shift_left(jnp.int32(2), gi)
    return jnp.minimum(pos + 1, win).astype(F32)


def _pooled(p, gi):
    acc = p
    levels = []
    for lvl in range(POOL_GROUPS):
        acc = acc + _shift_down(acc, 1 << lvl)
        levels.append(acc)
    return _pool_select(levels, gi) / _pool_count(p.shape, gi) - p


def _pool_fwd(proj, pool_w, pool_scale):
    t = proj.shape[0]

    def body(p_ref, w_ref, s_ref, out_ref):
        gi = pl.program_id(0)
        pooled = _pooled(p_ref[...], gi)
        out_ref[...] = (_BDOT_PLAIN[0](pooled, w_ref[0]) * s_ref[0]).astype(out_ref.dtype)

    return pl.pallas_call(
        body, grid=(POOL_GROUPS,),
        in_specs=[pl.BlockSpec((t, POOL_GROUP_DIM), lambda g: (0, POOL_BLK + g)),
                  pl.BlockSpec((1, POOL_GROUP_DIM, POOL_GROUP_DIM), lambda g: (g, 0, 0)),
                  pl.BlockSpec((1, 1, POOL_GROUP_DIM), lambda g: (g, 0, 0))],
        out_specs=pl.BlockSpec((t, POOL_GROUP_DIM), lambda g: (0, GDN_WIDTH // POOL_GROUP_DIM + g)),
        out_shape=jax.ShapeDtypeStruct((t, 2 * GDN_WIDTH), BF16),
        compiler_params=_params("parallel"), name="pool_fwd",
    )(proj, pool_w, pool_scale)


def _pool_bwd(proj, pool_w, pool_scale, dmixin):
    t = proj.shape[0]
    nn, nt, tn = _BDOT_PLAIN

    def body(p_ref, w_ref, s_ref, d_ref, dp_ref, dw_ref, ds_ref):
        gi = pl.program_id(0)
        p = p_ref[...]
        pooled = _pooled(p, gi)
        mixed = nn(pooled, w_ref[0])
        d = d_ref[...]
        ds_ref[0] = jnp.sum(d * mixed, axis=0, keepdims=True)
        dmixed = d * s_ref[0]
        dw_ref[0] = tn(pooled, dmixed)
        dpooled = nt(dmixed, w_ref[0])
        acc = dpooled / _pool_count(p.shape, gi)
        levels = []
        for lvl in range(POOL_GROUPS):
            acc = acc + _shift_up(acc, 1 << lvl)
            levels.append(acc)
        dp_ref[...] = (_pool_select(levels, gi) - dpooled).astype(dp_ref.dtype)

    return pl.pallas_call(
        body, grid=(POOL_GROUPS,),
        in_specs=[pl.BlockSpec((t, POOL_GROUP_DIM), lambda g: (0, POOL_BLK + g)),
                  pl.BlockSpec((1, POOL_GROUP_DIM, POOL_GROUP_DIM), lambda g: (g, 0, 0)),
                  pl.BlockSpec((1, 1, POOL_GROUP_DIM), lambda g: (g, 0, 0)),
                  pl.BlockSpec((t, POOL_GROUP_DIM), lambda g: (0, GDN_WIDTH // POOL_GROUP_DIM + g))],
        out_specs=[pl.BlockSpec((t, POOL_GROUP_DIM), lambda g: (0, POOL_BLK + g)),
                   pl.BlockSpec((1, POOL_GROUP_DIM, POOL_GROUP_DIM), lambda g: (g, 0, 0)),
                   pl.BlockSpec((1, 1, POOL_GROUP_DIM), lambda g: (g, 0, 0))],
        out_shape=[jax.ShapeDtypeStruct((t, PROJ_COLS), BF16),
                   jax.ShapeDtypeStruct((POOL_GROUPS, POOL_GROUP_DIM, POOL_GROUP_DIM), F32),
                   jax.ShapeDtypeStruct((POOL_GROUPS, 1, POOL_GROUP_DIM), F32)],
        compiler_params=_params("parallel"), name="pool_bwd",
    )(proj, pool_w, pool_scale, dmixin)


def _ln_stats(s):
    mu = jnp.mean(s, axis=1, keepdims=True)
    xc = s - mu
    var = jnp.mean(xc * xc, axis=1, keepdims=True)
    rstd = lax.rsqrt(var + LN_EPS)
    return xc * rstd, rstd


def _ln_fwd(h_in, y, g, b, *, name, tm=256):
    t, d = h_in.shape
    tm = min(tm, t)

    def body(h_ref, y_ref, g_ref, b_ref, o_ref, o16_ref):
        xhat, _ = _ln_stats(ALPHA * h_ref[...] + y_ref[...])
        out = xhat * g_ref[...] + b_ref[...]
        o_ref[...] = out
        o16_ref[...] = out.astype(BF16)

    row = pl.BlockSpec((tm, d), lambda i: (i, 0))
    vec = pl.BlockSpec((1, d), lambda i: (0, 0))
    return pl.pallas_call(
        body, grid=(t // tm,), in_specs=[row, row, vec, vec], out_specs=[row, row],
        out_shape=[jax.ShapeDtypeStruct((t, d), F32), jax.ShapeDtypeStruct((t, d), BF16)],
        compiler_params=_params("parallel"), name=name,
    )(h_in, y, g, b)


def _ln_backward(xhat, rstd, dout, gain):
    dxhat = dout * gain
    m1 = jnp.mean(dxhat, axis=1, keepdims=True)
    m2 = jnp.mean(dxhat * xhat, axis=1, keepdims=True)
    return (rstd * (dxhat - m1 - xhat * m2), jnp.sum(dout * xhat, axis=0, keepdims=True),
            jnp.sum(dout, axis=0, keepdims=True))


def _ln_loss(h_in, y, g, b, target, *, name, tm=256):
    t, d = h_in.shape
    tm = min(tm, t)

    def body(h_ref, y_ref, g_ref, b_ref, t_ref, sq_ref, ds_ref, ds16_ref, dg_ref, dbias_ref):
        @pl.when(pl.program_id(0) == 0)
        def _():
            sq_ref[...] = jnp.zeros_like(sq_ref)
            dg_ref[...] = jnp.zeros_like(dg_ref)
            dbias_ref[...] = jnp.zeros_like(dbias_ref)

        xhat, rstd = _ln_stats(ALPHA * h_ref[...] + y_ref[...])
        err = xhat * g_ref[...] + b_ref[...] - t_ref[...]
        sq_ref[...] += jnp.sum(jnp.sum(err * err, axis=1, keepdims=True), axis=0, keepdims=True)
        ds, dg, dbias = _ln_backward(xhat, rstd, err * (1.0 / d), g_ref[...])
        ds_ref[...] = ds
        ds16_ref[...] = ds.astype(BF16)
        dg_ref[...] += dg
        dbias_ref[...] += dbias

    row = pl.BlockSpec((tm, d), lambda i: (i, 0))
    vec = pl.BlockSpec((1, d), lambda i: (0, 0))
    return pl.pallas_call(
        body, grid=(t // tm,), in_specs=[row, row, vec, vec, row],
        out_specs=[pl.BlockSpec((1, LANE), lambda i: (0, 0)), row, row, vec, vec],
        out_shape=[jax.ShapeDtypeStruct((1, LANE), F32), jax.ShapeDtypeStruct((t, d), F32),
                   jax.ShapeDtypeStruct((t, d), BF16), jax.ShapeDtypeStruct((1, d), F32), jax.ShapeDtypeStruct((1, d), F32)],
        compiler_params=_params("arbitrary"), name=name,
    )(h_in, y, g, b, target)


def _ln_bwd(h_in, y, g, d_a, d_b, *, name, tm=256):
    t, d = h_in.shape
    tm = min(tm, t)
    has_b = d_b is not None

    def body(*refs):
        if has_b:
            h_ref, y_ref, g_ref, da_ref, db_ref, ds_ref, ds16_ref, dg_ref, dbias_ref = refs
        else:
            h_ref, y_ref, g_ref, da_ref, ds_ref, ds16_ref, dg_ref, dbias_ref = refs

        @pl.when(pl.program_id(0) == 0)
        def _():
            dg_ref[...] = jnp.zeros_like(dg_ref)
            dbias_ref[...] = jnp.zeros_like(dbias_ref)

        xhat, rstd = _ln_stats(ALPHA * h_ref[...] + y_ref[...])
        dout = da_ref[...]
        if has_b:
            dout = dout + ALPHA * db_ref[...]
        ds, dg, dbias = _ln_backward(xhat, rstd, dout, g_ref[...])
        ds_ref[...] = ds
        ds16_ref[...] = ds.astype(BF16)
        dg_ref[...] += dg
        dbias_ref[...] += dbias

    row = pl.BlockSpec((tm, d), lambda i: (i, 0))
    vec = pl.BlockSpec((1, d), lambda i: (0, 0))
    args = [h_in, y, g, d_a] + ([d_b] if has_b else [])
    return pl.pallas_call(
        body, grid=(t // tm,), in_specs=[row, row, vec, row] + ([row] if has_b else []),
        out_specs=[row, row, vec, vec],
        out_shape=[jax.ShapeDtypeStruct((t, d), F32), jax.ShapeDtypeStruct((t, d), BF16),
                   jax.ShapeDtypeStruct((1, d), F32), jax.ShapeDtypeStruct((1, d), F32)],
        compiler_params=_params("arbitrary"), name=name,
    )(*args)


def _attn_fn(q, k, v, dots):
    nn, nt, _ = dots
    s = nt(q, k) * (XATTN_HEAD_DIM ** -0.5)
    s = s - lax.stop_gradient(jnp.max(s, axis=1, keepdims=True))
    e = jnp.exp(s)
    p = e / jnp.sum(e, axis=1, keepdims=True)
    return nn(p, v)


def _attn_fwd(q, k, v, tq=512):
    t = q.shape[0]
    tq = min(tq, t)

    def body(q_ref, k_ref, v_ref, o_ref):
        o_ref[...] = _attn_fn(q_ref[...], k_ref[...], v_ref[...], _BDOT_PLAIN).astype(BF16)

    qs = pl.BlockSpec((tq, XATTN_HEAD_DIM), lambda h, i: (i, h))
    ks = pl.BlockSpec((MEM_LEN, XATTN_HEAD_DIM), lambda h, i: (0, h))
    return pl.pallas_call(
        body, grid=(XATTN_HEADS, t // tq), in_specs=[qs, ks, ks], out_specs=qs,
        out_shape=jax.ShapeDtypeStruct(q.shape, BF16), compiler_params=_params("parallel", "parallel"), name="xattn_fwd",
    )(q, k, v)


def _attn_bwd(q, k, v, do, tq=512):
    t = q.shape[0]
    tq = min(tq, t)

    def body(q_ref, k_ref, v_ref, do_ref, dq_ref, dk_ref, dv_ref):
        @pl.when(pl.program_id(1) == 0)
        def _():
            dk_ref[...] = jnp.zeros_like(dk_ref)
            dv_ref[...] = jnp.zeros_like(dv_ref)

        _, vjp = jax.vjp(lambda a, b, c: _attn_fn(a, b, c, _BDOT_VJP), q_ref[...].astype(F32), k_ref[...].astype(F32),
                         v_ref[...].astype(F32))
        dq, dk, dv = vjp(do_ref[...].astype(F32))
        dq_ref[...] = dq.astype(BF16)
        dk_ref[...] += dk
        dv_ref[...] += dv

    qs = pl.BlockSpec((tq, XATTN_HEAD_DIM), lambda h, i: (i, h))
    ks = pl.BlockSpec((MEM_LEN, XATTN_HEAD_DIM), lambda h, i: (0, h))
    return pl.pallas_call(
        body, grid=(XATTN_HEADS, t // tq), in_specs=[qs, ks, ks, qs], out_specs=[qs, ks, ks],
        out_shape=[jax.ShapeDtypeStruct(q.shape, BF16), jax.ShapeDtypeStruct(k.shape, F32), jax.ShapeDtypeStruct(v.shape, F32)],
        compiler_params=_params("parallel", "arbitrary"), name="xattn_bwd",
    )(q, k, v, do)


def _local_step(x, x16, mem, target, weights_of, grads_ready):
    def behind(vec, token):
        return vec if token is None else vec + token

    w = dict(weights_of("mixer", None))
    proj = _mm(x16, w["w_in"], tb=True, tn=768, name="mm_in_proj")
    mixin = _pool_fwd(proj, w["pool_w"], w["pool_scale"])
    post = _gdn_prep_fwd(proj, w["conv_w"])
    token = weights_of("ahead_conv", post)
    chunked, t_inv = _gdn_local_fwd(post, proj, behind(w["alog_row"], token), w["dtb_row"])
    o_raw, saved = _gdn_state_fwd(*chunked)
    token = weights_of("ahead_scan", o_raw)
    mixin = _onorm_fwd(o_raw, proj, behind(w["gdn_norm_w"], token), mixin)
    w.update(weights_of("attn", mixin))
    mix = _mm(mixin, w["w_out"], name="mm_out_proj")
    h1, h1_16 = _ln_fwd(x, mix, w["ln1_g"], w["ln1_b"], name="ln1_fwd")
    xq = _mm(h1_16, w["xq_w"], out_dtype=BF16, name="mm_xq")
    xk = _mm(mem, w["xk_w"], out_dtype=BF16, name="mm_xk")
    xv = _mm(mem, w["xv_w"], out_dtype=BF16, name="mm_xv")
    xo = _attn_fwd(xq, xk, xv)
    token = weights_of("ahead_attn", xo)
    if token is not None:
        xo, _ = lax.optimization_barrier((xo, token))
    xa = _mm(xo, w["xo_w"], name="mm_xo")
    h2, h2_16 = _ln_fwd(h1, xa, w["ln2_g"], w["ln2_b"], name="ln2_fwd")
    w.update(weights_of("up", h2_16))
    act, relu = _mm(h2_16, w["w_up"], b_chunks=True, epi="relu2", name="mm_up")
    w.update(weights_of("down", act))
    ff = _mm(act, w["w_down"], tn=512, tk=2048, name="mm_down")
    g = {}
    sq, ds3, ds3_16, g["ln3_g"], g["ln3_b"] = _ln_loss(h2, ff, w["ln3_g"], w["ln3_b"], target, name="ln3_loss")

    gw_down = _mm(act, ds3_16, ta=True, out_dtype=BF16, tm=512, tn=D_MODEL, name="mm_gw_down")
    du = _mm(ds3_16, w["w_down"], tb=True, epi="mul2r", extra=relu, name="mm_du")
    gw_up = _mm(h2_16, du, ta=True, out_dtype=BF16, o_chunks=True, name="mm_gw_up")
    token = grads_ready("mlp", {"w_down": gw_down, "w_up": gw_up})
    dh2 = _mm(du, w["w_up"], tb=True, b_chunks=True, tn=1024, tk=1024, name="mm_dh2")
    ds2, ds2_16, g["ln2_g"], g["ln2_b"] = _ln_bwd(h1, xa, behind(w["ln2_g"], token), dh2, ds3, name="ln2_bwd")
    gw_xo = _mm(xo, ds2_16, ta=True, out_dtype=BF16, name="mm_gw_xo")
    dxo = _mm(ds2_16, w["xo_w"], tb=True, out_dtype=BF16, name="mm_dxo")
    dxq, dxk, dxv = _attn_bwd(xq, xk, xv, dxo)
    gw_xq = _mm(h1_16, dxq, ta=True, out_dtype=BF16, name="mm_gw_xq")
    gw_xk = _mm(mem, dxk, ta=True, out_dtype=BF16, name="mm_gw_xk")
    gw_xv = _mm(mem, dxv, ta=True, out_dtype=BF16, name="mm_gw_xv")
    token = grads_ready("attn", {"xo_w": gw_xo, "xq_w": gw_xq, "xk_w": gw_xk, "xv_w": gw_xv})
    dh1 = _mm(dxq, w["xq_w"], tb=True, name="mm_dh1")
    ds1, ds1_16, g["ln1_g"], g["ln1_b"] = _ln_bwd(x, mix, behind(w["ln1_g"], token), dh1, ds2, name="ln1_bwd")
    gw_out = _mm(mixin, ds1_16, ta=True, out_dtype=BF16, name="mm_gw_out")
    dmixin = _mm(ds1_16, w["w_out"], tb=True, name="mm_dmixin")
    dproj, gw_pool, g["pool_scale"] = _pool_bwd(proj, w["pool_w"], w["pool_scale"], dmixin)
    token = grads_ready("mix", {"w_out": gw_out, "pool_w": gw_pool})
    do_raw, dproj, g["gdn_norm_w"] = _onorm_bwd(o_raw, proj, behind(w["gdn_norm_w"], token), dmixin, dproj)
    cots = _gdn_state_bwd(*chunked, saved, do_raw)
    token = grads_ready("tick", {"after": cots[0]})
    dpost, dproj, g["alog_row"], g["dtb_row"] = _gdn_local_bwd(post, proj, behind(w["alog_row"], token), w["dtb_row"],
                                                               t_inv, cots, dproj)
    dproj, g["conv_w"] = _gdn_prep_bwd(proj, w["conv_w"], dpost, dproj)
    token = grads_ready("small", {**g, "sq": sq})
    if token is not None:
        dproj, _ = lax.optimization_barrier((dproj, token))
    gw_in = _mm(dproj, x16, ta=True, out_dtype=BF16, tm=768, tn=D_MODEL, name="mm_gw_in")
    token = grads_ready("in", {"w_in": gw_in})
    if token is not None:
        ds1, _ = lax.optimization_barrier((ds1, token))
    grad_x = _mm(dproj, w["w_in"], tk=1792, epi="add", extra=ds1, add_scale=ALPHA, name="mm_dx")
    return sq, grad_x, g


_MATRICES = ("w_in", "pool_w", "w_out", "xq_w", "xk_w", "xv_w", "xo_w", "w_up", "w_down")
_VECTORS = ("a_log", "dt_bias", "gdn_norm_w", "pool_scale", "ln1_g", "ln1_b", "ln2_g", "ln2_b", "ln3_g", "ln3_b")
_BA_SPLIT = BA_OFF + 2 * GDN_HEADS


def _lane_row(v, offset):
    return jnp.zeros((1, LANE), F32).at[0, offset:offset + v.shape[0]].set(v)


_GROUP_VECTORS = {"mixer": (), "attn": ("ln1_g", "ln1_b", "ln2_g", "ln2_b"), "up": (), "down": ("ln3_g", "ln3_b")}


def _group_weights(group, full):
    w = {n: full[n].reshape(1, D_MODEL) for n in _GROUP_VECTORS[group]}
    if group == "mixer":
        w.update({
            "w_in": _w_in_padded(full["w_in"]),
            "conv_w": full["conv_w"],
            "alog_row": _lane_row(full["a_log"], GDN_HEADS),
            "dtb_row": _lane_row(full["dt_bias"], GDN_HEADS),
            "gdn_norm_w": full["gdn_norm_w"].reshape(1, LANE),
            "pool_w": full["pool_w"],
            "pool_scale": full["pool_scale"].reshape(POOL_GROUPS, 1, POOL_GROUP_DIM),
        })
    else:
        w.update({n: full[n] for n in dict(_GATHER_GROUPS)[group]})
    return w


def _w_in_row_map():
    per = IN_COLS // N_DEV
    gap = POOL_OFF - _BA_SPLIT
    pieces = []
    for d in range(N_DEV):
        lo, hi = d * per, (d + 1) * per
        if hi <= _BA_SPLIT:
            pieces.append([(0, lo, per)])
        elif lo >= _BA_SPLIT:
            pieces.append([(0, lo + gap, per)])
        else:
            pieces.append([(0, lo, _BA_SPLIT - lo), (_BA_SPLIT - lo, POOL_OFF, hi - _BA_SPLIT)])
    return pieces


_W_IN_LANES = 256


def _w_in_padded(blocks):
    def body(b_ref, o_ref):
        for d, pieces in enumerate(_w_in_row_map()):
            for src, dst, rows in pieces:
                o_ref[dst:dst + rows, :] = b_ref[d, src:src + rows, :]
        o_ref[_BA_SPLIT:POOL_OFF, :] = jnp.zeros((POOL_OFF - _BA_SPLIT, _W_IN_LANES), o_ref.dtype)

    n, per, cols = blocks.shape
    return pl.pallas_call(
        body, grid=(cols // _W_IN_LANES,), in_specs=[pl.BlockSpec((n, per, _W_IN_LANES), lambda j: (0, 0, j))],
        out_specs=pl.BlockSpec((PROJ_COLS, _W_IN_LANES), lambda j: (0, j)),
        out_shape=jax.ShapeDtypeStruct((PROJ_COLS, cols), blocks.dtype), compiler_params=_params("parallel"),
        name="w_in_padded")(blocks)


def _w_in_chunks(g):
    def body(g_ref, o_ref):
        for d, pieces in enumerate(_w_in_row_map()):
            for dst, src, rows in pieces:
                o_ref[d, dst:dst + rows, :] = g_ref[src:src + rows, :]

    cols = g.shape[1]
    per = IN_COLS // N_DEV
    return pl.pallas_call(
        body, grid=(cols // _W_IN_LANES,), in_specs=[pl.BlockSpec((PROJ_COLS, _W_IN_LANES), lambda j: (0, j))],
        out_specs=pl.BlockSpec((N_DEV, per, _W_IN_LANES), lambda j: (0, 0, j)),
        out_shape=jax.ShapeDtypeStruct((N_DEV, per, cols), g.dtype), compiler_params=_params("parallel"),
        name="w_in_chunks")(g)


def _finish_small_grads(g):
    out = {"conv_w": g["conv_w"]}
    out["a_log"] = g["alog_row"][0, GDN_HEADS:2 * GDN_HEADS]
    out["dt_bias"] = g["dtb_row"][0, GDN_HEADS:2 * GDN_HEADS]
    out["gdn_norm_w"] = g["gdn_norm_w"].reshape(LANE)
    out["pool_scale"] = g["pool_scale"].reshape(POOL_GROUPS * POOL_GROUP_DIM)
    for n in ("ln1_g", "ln1_b", "ln2_g", "ln2_b", "ln3_g", "ln3_b"):
        out[n] = g[n].reshape(D_MODEL)
    return out


def _adamw_math(w, g, m, v):
    m = ADAM_B1 * m + (1.0 - ADAM_B1) * g
    v = ADAM_B2 * v + (1.0 - ADAM_B2) * (g * g)
    m_hat = m / (1.0 - ADAM_B1 ** ADAM_STEP)
    v_hat = v / (1.0 - ADAM_B2 ** ADAM_STEP)
    delta = -ADAM_LR * (m_hat / (jnp.sqrt(v_hat) + ADAM_EPS) + ADAM_WD * w)
    return delta, m, v


ADAMW_TILE_ELEMS = 256 * 1024
CHIP_SUM_TILE_ELEMS = 1024 * 1024


def _shard_tile(r, c, elems):
    for rows in (1024, 512, 256, 128):
        if r % rows == 0 and rows * c <= elems:
            return rows, c
    if r % 128 == 0:
        return 128, c
    return r, 256 if c % 256 == 0 else c


def _adamw_shard(parts, own, me, w, m, v, *, name):
    s, r, c = parts.shape
    tr, tc = _shard_tile(r, c, ADAMW_TILE_ELEMS)
    assert r % tr == 0 and c % tc == 0, (name, r, c)
    unit_axis = w.ndim == 3
    at = (slice(None), 0, slice(None)) if unit_axis else Ellipsis

    def body(me_ref, p_ref, own_ref, w_ref, m_ref, v_ref, g_ref, d_ref, nm_ref, nv_ref):
        mine = own_ref[...].astype(F32)
        g = None
        for i in range(s):
            part = jnp.where(me_ref[0] == i, mine, p_ref[i].astype(F32))
            g = part if g is None else g + part
        delta, nm, nv = _adamw_math(w_ref[at], g, m_ref[at], v_ref[at])
        g_ref[at] = g
        d_ref[at] = delta
        nm_ref[at] = nm
        nv_ref[at] = nv

    if unit_axis:
        blk = pl.BlockSpec((tr, 1, tc), lambda i, j, me_ref: (i, 0, j))
        out = jax.ShapeDtypeStruct((r, 1, c), F32)
    else:
        blk = pl.BlockSpec((tr, tc), lambda i, j, me_ref: (i, j))
        out = jax.ShapeDtypeStruct((r, c), F32)
    return pl.pallas_call(
        body,
        grid_spec=pltpu.PrefetchScalarGridSpec(
            num_scalar_prefetch=1, grid=(r // tr, c // tc),
            in_specs=[pl.BlockSpec((s, tr, tc), lambda i, j, me_ref: (0, i, j)),
                      pl.BlockSpec((None, tr, tc), lambda i, j, me_ref: (me_ref[0], i, j)), blk, blk, blk],
            out_specs=[blk, blk, blk, blk]),
        out_shape=[out, out, out, out], compiler_params=_params("parallel", "parallel"), name=name,
    )(me, parts, own, w, m, v)


N_CHIPS = N_DEV // 2


def _chip_sums(chunks, from_sibling, core, *, name):
    _, r, c = chunks.shape
    tr, tc = _shard_tile(r, c, CHIP_SUM_TILE_ELEMS)
    assert r % tr == 0 and c % tc == 0, (name, r, c)

    def body(core_ref, mine_ref, other_ref, o_ref):
        o_ref[...] = (mine_ref[...].astype(F32) + other_ref[...].astype(F32)).astype(o_ref.dtype)

    by_chip = pl.BlockSpec((None, tr, tc), lambda q, i, j, core_ref: (q, i, j))
    return pl.pallas_call(
        body,
        grid_spec=pltpu.PrefetchScalarGridSpec(
            num_scalar_prefetch=1, grid=(N_CHIPS, r // tr, c // tc),
            in_specs=[pl.BlockSpec((None, tr, tc), lambda q, i, j, core_ref: (2 * q + core_ref[0], i, j)), by_chip],
            out_specs=by_chip),
        out_shape=jax.ShapeDtypeStruct((N_CHIPS, r, c), chunks.dtype),
        compiler_params=_params("parallel", "parallel", "parallel"), name=name,
    )(core, chunks, from_sibling)


def _place():
    return lax.axis_index("x"), lax.axis_index("y"), lax.axis_index("c")


def _slot(px, py, pc):
    return 4 * px + 2 * py + pc


_HBM = pl.BlockSpec(memory_space=pltpu.HBM)


_SEM = pl.BlockSpec(memory_space=pltpu.SEMAPHORE)
_ANY = pl.BlockSpec(memory_space=pl.ANY)
_EFFECT = pltpu.SideEffectType.DATAFLOW_SIDE_EFFECTING
_N_PEERS = N_DEV - 1


def _peer(k, x, y, c):
    return (1 - x if k & 4 else x, 1 - y if k & 2 else y, 1 - c if k & 1 else c)


_EXCHANGE_BITS = {"gather_near": (1, 2, 4), "gather_relay": (6,), "gather_pass": (2, 4, 6),
                  "scatter_sibling": (1, 1, 1, 1), "scatter_chips": (2, 4, 6), "all_small": (1, 2, 3, 4, 5, 6, 7)}


def _exchange_copy(mode, src, land, w, i, place, send_sems, recv_sems, receiving):
    bits = _EXCHANGE_BITS[mode]
    k = bits[i]
    peer = _peer(k, *place)
    me = _slot(*place)
    if mode in ("gather_near", "all_small"):
        to, src_ref, sent_to, got_at = peer, src[w], me, _slot(*peer)
    elif mode == "gather_relay":
        x, y, c = place
        other = 1 - c
        to = (lax.bitwise_xor(x, c), lax.bitwise_xor(y, other), c)
        blk = _slot(lax.bitwise_xor(x, other), lax.bitwise_xor(y, c), c)
        src_ref, sent_to, got_at = land[w].at[blk], blk, _slot(*peer)
    elif mode == "gather_pass":
        blk = _slot(*peer)
        to, src_ref, sent_to, got_at = _peer(1, *place), land[w].at[blk], blk, _slot(*_peer(k | 1, *place))
    elif mode == "scatter_sibling":
        to, src_ref, sent_to, got_at = peer, src[w].at[2 * i + 1 - place[2]], i, i
    else:
        to, src_ref, sent_to, got_at = peer, src[w].at[_slot(*peer) // 2], me // 2, _slot(*peer) // 2
    sem = w * len(bits) + i
    return pltpu.make_async_remote_copy(
        src_ref=src_ref, dst_ref=land[w].at[got_at if receiving else sent_to], send_sem=send_sems.at[sem],
        recv_sem=recv_sems.at[sem], device_id=to, device_id_type=MESH)


def _exchange_start(mode, srcs, lands, after, *, name):
    ns, nl = len(srcs), len(lands)
    n_sem = nl * len(_EXCHANGE_BITS[mode])

    def body(*refs):
        src, land = refs[:ns], refs[ns:ns + nl]
        send_sems, recv_sems = refs[ns + nl + 1:ns + nl + 3]
        token = refs[-1]
        place = _place()
        for w in range(nl):
            for i in range(len(_EXCHANGE_BITS[mode])):
                _exchange_copy(mode, src, land, w, i, place, send_sems, recv_sems, receiving=False).start()
        token[...] = jnp.zeros_like(token)

    sems = pltpu.SemaphoreType.DMA((n_sem,))
    arrays = list(srcs) + list(lands)
    res = pl.pallas_call(
        body, name=name, in_specs=[_HBM] * (ns + nl) + [_ANY],
        out_specs=(_SEM, _SEM, *([_HBM] * (ns + nl)), pl.BlockSpec(memory_space=pltpu.VMEM)),
        out_shape=(sems, sems, *[pltpu.HBM(a.shape, a.dtype) for a in arrays], jax.ShapeDtypeStruct((8, LANE), F32)),
        input_output_aliases={i: 2 + i for i in range(ns + nl)},
        compiler_params=pltpu.CompilerParams(has_side_effects=_EFFECT),
    )(*[pltpu.with_memory_space_constraint(a, pltpu.HBM) for a in arrays], after)
    return res[0], res[1], list(res[2:2 + ns]), list(res[2 + ns:2 + ns + nl]), res[-1]


def _exchange_wait(mode, started, after, *, name):
    send_sems, recv_sems, srcs, lands, _ = started
    ns, nl = len(srcs), len(lands)

    def body(*refs):
        src, land = refs[:ns], refs[ns:ns + nl]
        send_sems, recv_sems = refs[ns + nl:ns + nl + 2]
        place = _place()
        for w in range(nl):
            for i in range(len(_EXCHANGE_BITS[mode])):
                cp = _exchange_copy(mode, src, land, w, i, place, send_sems, recv_sems, receiving=True)
                cp.wait_send()
                cp.wait_recv()

    arrays = list(srcs) + list(lands)
    res = pl.pallas_call(
        body, name=name, in_specs=[_HBM] * (ns + nl) + [_SEM, _SEM, _ANY], out_specs=[_HBM] * (ns + nl),
        out_shape=[pltpu.HBM(a.shape, a.dtype) for a in arrays],
        input_output_aliases={i: i for i in range(ns + nl)},
        compiler_params=pltpu.CompilerParams(has_side_effects=_EFFECT),
    )(*arrays, send_sems, recv_sems, after)
    return list(res[:ns]), list(res[ns:])


_SMALL_SEGMENTS = (("a_log", GDN_HEADS), ("dt_bias", GDN_HEADS), ("gdn_norm_w", HEAD_DIM), ("pool_scale", GDN_WIDTH),
                   ("ln1_g", D_MODEL), ("ln1_b", D_MODEL), ("ln2_g", D_MODEL), ("ln2_b", D_MODEL),
                   ("ln3_g", D_MODEL), ("ln3_b", D_MODEL), ("conv_w", CONV_K * QKV_COLS), ("loss", 1))
_SMALL_ROWS = 8
_SMALL_LEN = -(-sum(sz for _, sz in _SMALL_SEGMENTS) // (_SMALL_ROWS * LANE)) * LANE


def _pack_small(vals):
    parts = [vals[n].reshape(-1).astype(F32) if n in vals else jnp.zeros((sz,), F32) for n, sz in _SMALL_SEGMENTS]
    flat = jnp.concatenate(parts)
    flat = jnp.pad(flat, (0, _SMALL_ROWS * _SMALL_LEN - flat.shape[0]))
    return flat.reshape(_SMALL_ROWS, _SMALL_LEN)


def _unpack_small(vec):
    flat = vec.reshape(-1)
    out, off = {}, 0
    for n, sz in _SMALL_SEGMENTS:
        out[n] = flat[off:off + sz]
        off += sz
    return out


_WEIGHT_ORDER = ("w_in", "conv_w", "a_log", "dt_bias", "gdn_norm_w", "pool_w", "pool_scale", "w_out", "ln1_g", "ln1_b",
                 "xq_w", "xk_w", "xv_w", "xo_w", "ln2_g", "ln2_b", "w_up", "w_down", "ln3_g", "ln3_b")


def _shard2d(name, a):
    if name == "w_in":
        return a.T
    return a.reshape(-1, a.shape[-1]) if name == "pool_w" else a


def _update_view(name, a):
    return jnp.transpose(a, (2, 0, 1)) if name == "w_in" else _shard2d(name, a[0])


def _shard_result(name, r, shape):
    return jnp.transpose(r, (1, 2, 0)) if name == "w_in" else r.reshape(shape)


def _gathered_to_full(name, gth):
    if name in ("w_up", "w_in"):
        return gth
    if name == "conv_w":
        return jnp.transpose(gth, (1, 0, 2)).reshape(gth.shape[1], N_DEV * gth.shape[2])
    if name == "pool_w":
        g4 = gth.reshape(N_DEV, POOL_GROUPS, POOL_GROUP_DIM // N_DEV, POOL_GROUP_DIM)
        return jnp.transpose(g4, (1, 0, 2, 3)).reshape(POOL_GROUPS, POOL_GROUP_DIM, POOL_GROUP_DIM)
    return gth.reshape(N_DEV * gth.shape[1], gth.shape[2])


def _full_to_chunks(name, full):
    if name == "w_up":
        return full
    if name == "pool_w":
        g4 = full.reshape(POOL_GROUPS, N_DEV, POOL_GROUP_DIM // N_DEV, POOL_GROUP_DIM)
        return jnp.transpose(g4, (1, 0, 2, 3)).reshape(N_DEV, POOL_GROUPS * POOL_GROUP_DIM // N_DEV, POOL_GROUP_DIM)
    return full.reshape(N_DEV, full.shape[0] // N_DEV, full.shape[1])


_GATHER_GROUPS = (("mixer", ("w_in", "conv_w", "pool_w")), ("attn", ("w_out", "xq_w", "xk_w", "xv_w", "xo_w")),
                  ("up", ("w_up",)), ("down", ("w_down",)))


def _grad_chunks(name, g):
    if name == "w_in":
        return _w_in_chunks(g.astype(BF16))
    return _full_to_chunks(name, g.astype(BF16))


def kernel(x, mem, w_in, conv_w, a_log, dt_bias, gdn_norm_w, pool_w, pool_scale, w_out, ln1_g, ln1_b, xq_w, xk_w, xv_w, xo_w, ln2_g, ln2_b, w_up, w_down, ln3_g, ln3_b, loss_target, m_w_in, m_conv_w, m_a_log, m_dt_bias, m_gdn_norm_w, m_pool_w, m_pool_scale, m_w_out, m_ln1_g, m_ln1_b, m_xq_w, m_xk_w, m_xv_w, m_xo_w, m_ln2_g, m_ln2_b, m_w_up, m_w_down, m_ln3_g, m_ln3_b, v_w_in, v_conv_w, v_a_log, v_dt_bias, v_gdn_norm_w, v_pool_w, v_pool_scale, v_w_out, v_ln1_g, v_ln1_b, v_xq_w, v_xk_w, v_xv_w, v_xo_w, v_ln2_g, v_ln2_b, v_w_up, v_w_down, v_ln3_g, v_ln3_b):
    args = dict(locals())
    wt = {n: args[n][0] for n in _WEIGHT_ORDER}
    mo = {n: args["m_" + n][0] for n in _WEIGHT_ORDER}
    vo = {n: args["v_" + n][0] for n in _WEIGHT_ORDER}

    me = _slot(*_place())
    me_arr = jnp.reshape(me, (1,)).astype(jnp.int32)
    nothing = jnp.zeros((8, LANE), F32)

    def landing_zones(names):
        shards = [_shard2d(n, wt[n]).astype(F32 if n == "conv_w" else BF16) for n in names]
        zones = [lax.dynamic_update_slice(lax.empty((N_DEV, *s.shape), s.dtype), s[None], (me, 0, 0)) for s in shards]
        return shards, zones

    chip_arr = jnp.reshape(me // 2, (1,)).astype(jnp.int32)
    core_arr = jnp.reshape(lax.axis_index("c"), (1,)).astype(jnp.int32)
    names_of = dict(_GATHER_GROUPS)
    gathers = {}
    prepared = {}

    def gather_near(group, after):
        shards, zones = prepared.pop(group) if group in prepared else landing_zones(names_of[group])
        gathers[group] = _exchange_start("gather_near", shards, zones, after, name="gather_near_" + group)
        return gathers[group][4]

    def gather_next(group, was, now, after):
        _, zones = _exchange_wait(was, gathers[group], after, name=f"{was}_{group}_wait")
        gathers[group] = _exchange_start(now, [], zones, nothing, name=f"{now}_{group}")
        return gathers[group][4]

    def gather_relay(group, after):
        return gather_next(group, "gather_near", "gather_relay", after)

    def gather_pass(group, after):
        return gather_next(group, "gather_relay", "gather_pass", after)

    def gathered(group, after):
        _, zones = _exchange_wait("gather_pass", gathers[group], after, name=f"gather_pass_{group}_wait")
        full = {n: _gathered_to_full(n, z) for n, z in zip(names_of[group], zones)}
        full.update({n: wt[n] for n in _VECTORS})
        return _group_weights(group, full)

    token = gather_near("mixer", nothing)
    x16 = _cast_bf16(x[0], name="cast_x")
    later = {group: landing_zones(names_of[group]) for group in ("attn", "up", "down")}
    token, x16, later = lax.optimization_barrier((token, x16, later))
    prepared.update(later)
    token = gather_pass("mixer", gather_relay("mixer", token))
    token = gather_near("attn", token)

    def weights_of(group, after):
        if group == "mixer":
            return gathered(group, token)
        if group == "ahead_conv":
            return gather_near("up", gather_relay("attn", after))[0:1, 0:1]
        if group == "ahead_scan":
            return gather_pass("attn", after)[0:1, 0:1]
        if group == "attn":
            return gathered(group, gather_near("down", gather_relay("up", after)))
        if group == "ahead_attn":
            return gather_relay("down", gather_pass("up", after))[0:1, 0:1]
        if group == "up":
            return gathered(group, gather_pass("down", after))
        return gathered(group, after)

    scatters = {}
    in_flight = []

    def chip_stage(after):
        group, names, started = in_flight.pop()
        chunks, from_sibling = _exchange_wait("scatter_sibling", started, after, name=f"scatter_sibling_{group}_wait")
        sums = [_chip_sums(c, f, core_arr, name=f"chip_sums_{n}") for n, c, f in zip(names, chunks, from_sibling)]
        scatters[group] = (names, _exchange_start("scatter_chips", sums, [lax.empty(s.shape, s.dtype) for s in sums],
                                                  nothing, name="scatter_chips_" + group))
        return scatters[group][1][4]

    small_sent = []

    def grads_ready(group, grads):
        if group == "tick":
            return chip_stage(grads["after"])[0:1, 0:1] if in_flight else None
        if group == "small":
            small = _finish_small_grads(grads)
            small["loss"] = 0.5 * grads["sq"][0:1, 0] / D_MODEL
            packed = _pack_small(small)
            zone = lax.empty((N_DEV, *packed.shape), F32)
            small_sent.append(_exchange_start("all_small", [packed], [zone], nothing, name="small_grads_start"))
            return small_sent[0][4][0:1, 0:1]
        names = tuple(grads)
        chunks = [_grad_chunks(n, grads[n]) for n in names]
        token = chip_stage(chunks[0]) if in_flight else nothing
        zones = [lax.empty((N_CHIPS, *c.shape[1:]), c.dtype) for c in chunks]
        started = _exchange_start("scatter_sibling", chunks, zones, token, name="scatter_sibling_" + group)
        in_flight.append((group, names, started))
        return started[4][0:1, 0:1]

    sq, grad_x, g = _local_step(x[0], x16, mem[0], loss_target[0], weights_of, grads_ready)

    out = {}
    after = chip_stage(grad_x)
    for group, (names, started) in scatters.items():
        sums, lands = _exchange_wait("scatter_chips", started, after, name=f"scatter_chips_{group}_wait")
        for n, parts, own in zip(names, lands, sums):
            res = _adamw_shard(parts, own, chip_arr, _update_view(n, args[n]), _update_view(n, args["m_" + n]),
                               _update_view(n, args["v_" + n]), name="adamw_" + n)
            out[n] = [_shard_result(n, r, args[n].shape) for r in res]
            after = res[1]

    (packed,), (zone,) = _exchange_wait("all_small", small_sent[0], after, name="small_grads_wait")
    gs, ds, ms, vs = _adamw_shard(
        zone, jnp.broadcast_to(packed, zone.shape), me_arr, _pack_small({n: wt[n] for n in _VECTORS}),
        _pack_small({n: mo[n] for n in _VECTORS}), _pack_small({n: vo[n] for n in _VECTORS}), name="adamw_small")
    gs, ds, ms, vs = _unpack_small(gs), _unpack_small(ds), _unpack_small(ms), _unpack_small(vs)
    cols = conv_w.shape[-1]
    conv_full = gs["conv_w"].reshape(CONV_K, QKV_COLS)
    conv_mine = lax.dynamic_slice(conv_full, (0, me * cols), (CONV_K, cols))[None]
    res = _adamw_shard(conv_mine, conv_mine, jnp.zeros((1,), jnp.int32), wt["conv_w"], mo["conv_w"], vo["conv_w"],
                       name="adamw_conv_w")
    out["conv_w"] = [r.reshape(conv_w.shape) for r in res]
    for n in _VECTORS:
        out[n] = [t[n].reshape(args[n].shape) for t in (gs, ds, ms, vs)]

    return (gs["loss"][0], grad_x[None], *[out[n][0] for n in _WEIGHT_ORDER], *[out[n][1] for n in _WEIGHT_ORDER],
            *[out[n][2] for n in _WEIGHT_ORDER], *[out[n][3] for n in _WEIGHT_ORDER])
```

```python
import functools
import math

import jax
import jax.numpy as jnp
from jax import lax
from jax.experimental import pallas as pl
from jax.experimental.pallas import tpu as pltpu

F32 = jnp.float32
BF16 = jnp.bfloat16
MESH = pl.DeviceIdType.MESH

N_DEV = 8
D_MODEL = 2048
GDN_WIDTH = 1024
GDN_HEADS = 8
HEAD_DIM = 128
CONV_K = 4
CHUNK = 64
POOL_GROUPS = 4
POOL_GROUP_DIM = 256
MEM_LEN = 256
XATTN_HEADS = 4
XATTN_HEAD_DIM = 512
D_FF = 8192
IN_COLS = 5136
ALPHA = 2.0 ** 0.25
LN_EPS = 1e-5
NORM_EPS = 1e-6

LANE = 128
QKV_COLS = 3 * GDN_WIDTH
Z_OFF = QKV_COLS
BA_OFF = 4 * GDN_WIDTH
POOL_OFF = BA_OFF + 2 * LANE
PROJ_COLS = POOL_OFF + GDN_WIDTH
Z_BLK = Z_OFF // LANE
BA_BLK = BA_OFF // LANE
POOL_BLK = POOL_OFF // POOL_GROUP_DIM

ADAM_LR = 0.001
ADAM_B1 = 0.9
ADAM_B2 = 0.999
ADAM_EPS = 1e-08
ADAM_WD = 0.01
ADAM_STEP = 10

VMEM_LIMIT_BYTES = 48 * 1024 * 1024


def _params(*sem):
    return pltpu.CompilerParams(dimension_semantics=sem if sem else None, vmem_limit_bytes=VMEM_LIMIT_BYTES)


def _make_dots(cast, precision, batched=False):
    lead = 1 if batched else 0
    batch = ((0,), (0,)) if batched else ((), ())

    def dg(a, b, ca, cb):
        if cast is not None:
            a = a.astype(cast)
            b = b.astype(cast)
        return lax.dot_general(a, b, (((ca + lead,), (cb + lead,)), batch), precision=precision, preferred_element_type=F32)

    def nn_(a, b):
        return dg(a, b, 1, 0)

    def nt_(a, b):
        return dg(a, b, 1, 1)

    def tn_(a, b):
        return dg(a, b, 0, 0)

    @jax.custom_vjp
    def nn(a, b):
        return nn_(a, b)

    nn.defvjp(lambda a, b: (nn_(a, b), (a, b)), lambda r, g: (nt_(g, r[1]), tn_(r[0], g)))

    @jax.custom_vjp
    def nt(a, b):
        return nt_(a, b)

    nt.defvjp(lambda a, b: (nt_(a, b), (a, b)), lambda r, g: (nn_(g, r[1]), tn_(g, r[0])))

    @jax.custom_vjp
    def tn(a, b):
        return tn_(a, b)

    tn.defvjp(lambda a, b: (tn_(a, b), (a, b)), lambda r, g: (nt_(r[1], g), nn_(r[0], g)))

    return (nn_, nt_, tn_), (nn, nt, tn)


_BDOT_PLAIN, _BDOT_VJP = _make_dots(BF16, None)
_BDOT_BATCH_PLAIN, _BDOT_BATCH_VJP = _make_dots(BF16, None, batched=True)
_FDOT_BATCH_PLAIN, _FDOT_BATCH_VJP = _make_dots(BF16, None, batched=True)


def _mm(a, b, *, ta=False, tb=False, out_dtype=F32, tm=None, tn=512, tk=None, epi=None, extra=None, add_scale=1.0,
        b_chunks=False, o_chunks=False, after=None, name):
    m, k = (a.shape[1], a.shape[0]) if ta else a.shape
    if b_chunks:
        n, kb = (b.shape[1], N_DEV * b.shape[2]) if tb else (N_DEV * b.shape[2], b.shape[1])
    else:
        n, kb = b.shape if tb else (b.shape[1], b.shape[0])
    assert kb == k, (name, a.shape, b.shape)
    tm, tn, tk = min(tm or m, m), min(tn, n), min(tk or k, k)
    assert m % tm == 0 and n % tn == 0 and k % tk == 0, (name, m, n, k)
    nk = k // tk
    dims = (((0 if ta else 1,), (1 if tb else 0,)), ((), ()))
    n_extra = 0 if epi in (None, "relu2") else 1
    n_out = 2 if epi == "relu2" else 1
    if epi in ("relu2", "mul2r"):
        out_dtype = BF16
    n_after = 0 if after is None else 1

    def body(*refs):
        a_ref, b_ref = refs[:2]
        c_ref = refs[2] if n_extra else None
        o_refs = refs[2 + n_extra + n_after:2 + n_extra + n_after + n_out]
        scr = refs[2 + n_extra + n_after + n_out:]
        r = lax.dot_general(a_ref[...].astype(BF16), b_ref[...].astype(BF16), dims, preferred_element_type=F32)

        def finish(v):
            if epi == "add":
                o_refs[0][...] = (v + add_scale * c_ref[...]).astype(out_dtype)
            elif epi == "relu2":
                p = jnp.maximum(v, 0.0)
                o_refs[0][...] = (p * p).astype(BF16)
                o_refs[1][...] = p.astype(BF16)
            elif epi == "mul2r":
                o_refs[0][...] = (v * (2.0 * c_ref[...].astype(F32))).astype(BF16)
            else:
                o_refs[0][...] = v.astype(out_dtype)

        if nk == 1:
            finish(r)
        else:
            acc = scr[0]
            kk = pl.program_id(2)

            @pl.when(kk == 0)
            def _():
                acc[...] = r

            @pl.when(kk > 0)
            def _():
                acc[...] += r

            @pl.when(kk == nk - 1)
            def _():
                finish(acc[...])

    a_spec = pl.BlockSpec((tk, tm), lambda i, j, kk: (kk, i)) if ta else pl.BlockSpec((tm, tk), lambda i, j, kk: (i, kk))
    if b_chunks and tb:
        kc = k // N_DEV // tk
        b_spec = pl.BlockSpec((None, tn, tk), lambda i, j, kk: (kk // kc, j, kk % kc))
    elif b_chunks:
        nc = n // N_DEV // tn
        b_spec = pl.BlockSpec((None, tk, tn), lambda i, j, kk: (j // nc, kk, j % nc))
    elif tb:
        b_spec = pl.BlockSpec((tn, tk), lambda i, j, kk: (j, kk))
    else:
        b_spec = pl.BlockSpec((tk, tn), lambda i, j, kk: (kk, j))
    mn_spec = pl.BlockSpec((tm, tn), lambda i, j, kk: (i, j))
    if o_chunks:
        oc = n // N_DEV // tn
        o_spec = pl.BlockSpec((None, tm, tn), lambda i, j, kk: (j // oc, i, j % oc))
        o_shape = jax.ShapeDtypeStruct((N_DEV, m, n // N_DEV), out_dtype)
    else:
        o_spec, o_shape = mn_spec, jax.ShapeDtypeStruct((m, n), out_dtype)
    res = pl.pallas_call(
        body, grid=(m // tm, n // tn, nk),
        in_specs=[a_spec, b_spec] + [mn_spec] * n_extra + [pl.BlockSpec(memory_space=pl.ANY)] * n_after,
        out_specs=[o_spec] * n_out, out_shape=[o_shape] * n_out,
        scratch_shapes=[pltpu.VMEM((tm, tn), F32)] if nk > 1 else [],
        compiler_params=_params("parallel", "parallel", "arbitrary"), name=name,
    )(a, b, *([extra] if n_extra else []), *([after] if n_after else []))
    return res if n_out > 1 else res[0]


def _cast_bf16(v, *, name, tm=512):
    t, d = v.shape
    tm = min(tm, t)

    def body(v_ref, o_ref):
        o_ref[...] = v_ref[...].astype(BF16)

    spec = pl.BlockSpec((tm, d), lambda i: (i, 0))
    return pl.pallas_call(body, grid=(t // tm,), in_specs=[spec], out_specs=spec,
                          out_shape=jax.ShapeDtypeStruct((t, d), BF16), compiler_params=_params("parallel"), name=name)(v)


def _shift_down(v, s):
    if s == 0:
        return v
    row = lax.broadcasted_iota(jnp.int32, v.shape, 0)
    return jnp.where(row >= s, pltpu.roll(v, s, axis=0), 0.0)


def _shift_up(v, s):
    if s == 0:
        return v
    t = v.shape[0]
    row = lax.broadcasted_iota(jnp.int32, v.shape, 0)
    return jnp.where(row < t - s, pltpu.roll(v, t - s, axis=0), 0.0)


def _post_col(j):
    return (j % GDN_HEADS) * 3 + j // GDN_HEADS


def _gdn_prep_fwd(proj, conv_w):
    t = proj.shape[0]

    def body(x_ref, w_ref, o_ref):
        j = pl.program_id(0)
        x = x_ref[...]
        y = jnp.zeros_like(x)
        for tap in range(CONV_K):
            y = y + w_ref[tap:tap + 1, :] * _shift_down(x, CONV_K - 1 - tap)
        c = y * jax.nn.sigmoid(y)
        nrm = c * lax.rsqrt(jnp.sum(c * c, axis=1, keepdims=True) + NORM_EPS)
        o_ref[...] = jnp.where(j < 2 * GDN_HEADS, nrm, c)

    return pl.pallas_call(
        body, grid=(QKV_COLS // LANE,),
        in_specs=[pl.BlockSpec((t, LANE), lambda j: (0, j)), pl.BlockSpec((CONV_K, LANE), lambda j: (0, j))],
        out_specs=pl.BlockSpec((t, LANE), lambda j: (0, _post_col(j))),
        out_shape=jax.ShapeDtypeStruct((t, QKV_COLS), F32),
        compiler_params=_params("parallel"), name="gdn_prep_fwd",
    )(proj, conv_w)


def _gdn_prep_bwd(proj, conv_w, dpost, dproj):
    t = proj.shape[0]

    def body(x_ref, w_ref, d_ref, _, dx_ref, dw_ref):
        j = pl.program_id(0)
        x = x_ref[...]
        xs = [_shift_down(x, CONV_K - 1 - tap) for tap in range(CONV_K)]
        y = jnp.zeros_like(x)
        for tap in range(CONV_K):
            y = y + w_ref[tap:tap + 1, :] * xs[tap]
        sig = jax.nn.sigmoid(y)
        c = y * sig
        r = lax.rsqrt(jnp.sum(c * c, axis=1, keepdims=True) + NORM_EPS)
        nrm = c * r
        d = d_ref[...]
        dc_norm = r * (d - nrm * jnp.sum(d * nrm, axis=1, keepdims=True))
        dc = jnp.where(j < 2 * GDN_HEADS, dc_norm, d)
        dy = dc * (sig * (1.0 + y * (1.0 - sig)))
        dx = jnp.zeros_like(x)
        for tap in range(CONV_K):
            dx = dx + _shift_up(w_ref[tap:tap + 1, :] * dy, CONV_K - 1 - tap)
            dw_ref[tap:tap + 1, :] = jnp.sum(dy * xs[tap], axis=0, keepdims=True)
        dx_ref[...] = dx.astype(dx_ref.dtype)

    return pl.pallas_call(
        body, grid=(QKV_COLS // LANE,),
        in_specs=[pl.BlockSpec((t, LANE), lambda j: (0, j)), pl.BlockSpec((CONV_K, LANE), lambda j: (0, j)),
                  pl.BlockSpec((t, LANE), lambda j: (0, _post_col(j))), pl.BlockSpec(memory_space=pl.ANY)],
        out_specs=[pl.BlockSpec((t, LANE), lambda j: (0, j)), pl.BlockSpec((CONV_K, LANE), lambda j: (0, j))],
        out_shape=[jax.ShapeDtypeStruct(dproj.shape, dproj.dtype), jax.ShapeDtypeStruct((CONV_K, QKV_COLS), F32)],
        input_output_aliases={3: 0},
        compiler_params=_params("parallel"), name="gdn_prep_bwd",
    )(proj, conv_w, dpost, dproj)


def _softplus(v):
    return jnp.maximum(v, 0.0) + jnp.log(1.0 + jnp.exp(-jnp.abs(v)))


def _tri_inv(low, nn):
    r = lax.broadcasted_iota(jnp.int32, (CHUNK, CHUNK), 0)
    c = lax.broadcasted_iota(jnp.int32, (CHUNK, CHUNK), 1)
    eye = (r == c).astype(F32)
    same_blk = lax.shift_right_logical(r, 4) == lax.shift_right_logical(c, 4)
    diag = jnp.where(same_blk, low, 0.0)
    off = low - diag
    n1 = -diag
    n2 = nn(n1, n1)
    n4 = nn(n2, n2)
    n8 = nn(n4, n4)
    inv_d = nn(nn(nn(eye + n1, eye + n2), eye + n4), eye + n8)
    m1 = nn(inv_d, off)
    m2 = nn(m1, m1)
    return nn(nn(eye - m1, eye + m2), inv_d)


@jax.custom_vjp
def _tri_inv_known(low, t_inv):
    return t_inv


def _tri_inv_known_fwd(low, t_inv):
    return t_inv, t_inv


def _tri_inv_known_bwd(t_inv, g):
    _, nt, tn = _FDOT_BATCH_PLAIN
    return -nt(tn(t_inv, g), t_inv), jnp.zeros_like(t_inv)


_tri_inv_known.defvjp(_tri_inv_known_fwd, _tri_inv_known_bwd)


LOCAL_HEADS_PER_STEP = 8


def _gdn_local_fn(qkv, ba, alog_row, dtb_row, first_head, bdots, fdots, t_known=None):
    nn, nt, tn = bdots
    fnn = fdots[0]
    n_heads = qkv.shape[1] // (3 * HEAD_DIM)
    part = lambda i, p: qkv[:, (3 * i + p) * HEAD_DIM:(3 * i + p + 1) * HEAD_DIM]
    q = jnp.stack([part(i, 0) for i in range(n_heads)]) * (HEAD_DIM ** -0.5)
    k = jnp.stack([part(i, 1) for i in range(n_heads)])
    v = jnp.stack([part(i, 2) for i in range(n_heads)])
    lane = lax.broadcasted_iota(jnp.int32, ba.shape, 1)
    bg = jnp.where(lane < GDN_HEADS, jax.nn.sigmoid(ba), -jnp.exp(alog_row) * _softplus(ba + dtb_row))
    pick = lambda l: jnp.sum(jnp.where(lane == l, bg, 0.0), axis=1, keepdims=True)
    beta = jnp.stack([pick(first_head + i) for i in range(n_heads)])
    g = jnp.stack([pick(first_head + i + GDN_HEADS) for i in range(n_heads)])

    r = lax.broadcasted_iota(jnp.int32, (CHUNK, CHUNK), 0)
    c = lax.broadcasted_iota(jnp.int32, (CHUNK, CHUNK), 1)
    incl = r >= c
    strict = r > c
    eye = r == c

    def to_row(col):
        return jnp.sum(jnp.where(eye, col, 0.0), axis=1, keepdims=True)

    gc = jnp.sum(jnp.where(incl, to_row(g), 0.0), axis=2, keepdims=True)
    diff = gc - to_row(gc)
    decay = jnp.where(incl, jnp.exp(jnp.where(incl, diff, 0.0)), 0.0)
    k_beta = k * beta
    v_beta = v * beta
    low = jnp.where(strict, nt(k_beta, k) * decay, 0.0)
    t_inv = _tri_inv(low, fnn) if t_known is None else _tri_inv_known(low, t_known)
    eg = jnp.exp(gc)
    u = fnn(t_inv, v_beta)
    w = fnn(t_inv, k_beta * eg)
    attn = jnp.where(incl, nt(q, k) * decay, 0.0)
    last = lax.broadcasted_iota(jnp.int32, (CHUNK, 1), 0) == CHUNK - 1
    g_last = jnp.sum(jnp.where(last, gc, 0.0), axis=1, keepdims=True)
    kdec = k * jnp.exp(g_last - gc)
    elast = jnp.broadcast_to(jnp.exp(g_last), (n_heads, 1, LANE))
    return u, w, q * eg, kdec, attn, elast, t_inv


def _gdn_state_fn(u, w, qg, kdec, attn, elast, state, bdots):
    nn, _, tn = bdots
    v_new = u - nn(w, state)
    o = nn(qg, state) + nn(attn, v_new)
    return o, state * elast + tn(kdec, v_new)


def _gdn_local_fwd(post, proj, alog_row, dtb_row):
    t = post.shape[0]
    n_chunks = t // CHUNK
    hb = LOCAL_HEADS_PER_STEP

    def body(qkv_ref, ba_ref, al_ref, dt_ref, u_ref, w_ref, qg_ref, kd_ref, at_ref, el_ref, ti_ref):
        u, w, qg, kdec, attn, elast, t_inv = _gdn_local_fn(qkv_ref[...], ba_ref[...], al_ref[...], dt_ref[...],
                                                           pl.program_id(1) * hb, _BDOT_BATCH_PLAIN, _FDOT_BATCH_PLAIN)
        for i in range(hb):
            cols = slice(i * HEAD_DIM, (i + 1) * HEAD_DIM)
            u_ref[:, cols] = u[i]
            w_ref[:, cols] = w[i].astype(BF16)
            qg_ref[:, cols] = qg[i].astype(BF16)
            kd_ref[:, cols] = kdec[i].astype(BF16)
        at_ref[...] = attn.astype(BF16)
        el_ref[:, 0] = elast
        ti_ref[...] = t_inv

    wide = pl.BlockSpec((CHUNK, hb * HEAD_DIM), lambda n, j: (n, j))
    square = pl.BlockSpec((hb, CHUNK, CHUNK), lambda n, j: (j, n, 0))
    row = pl.BlockSpec((1, LANE), lambda n, j: (0, 0))
    res = pl.pallas_call(
        body, grid=(n_chunks, GDN_HEADS // hb),
        in_specs=[pl.BlockSpec((CHUNK, hb * 3 * HEAD_DIM), lambda n, j: (n, j)),
                  pl.BlockSpec((CHUNK, LANE), lambda n, j: (n, BA_BLK)), row, row],
        out_specs=[wide, wide, wide, wide, square, pl.BlockSpec((hb, 1, 1, LANE), lambda n, j: (j, n, 0, 0)), square],
        out_shape=[jax.ShapeDtypeStruct((t, GDN_WIDTH), F32), jax.ShapeDtypeStruct((t, GDN_WIDTH), BF16),
                   jax.ShapeDtypeStruct((t, GDN_WIDTH), BF16), jax.ShapeDtypeStruct((t, GDN_WIDTH), BF16),
                   jax.ShapeDtypeStruct((GDN_HEADS, t, CHUNK), BF16),
                   jax.ShapeDtypeStruct((GDN_HEADS, n_chunks, 1, LANE), F32),
                   jax.ShapeDtypeStruct((GDN_HEADS, t, CHUNK), F32)],
        compiler_params=_params("parallel", "parallel"), name="gdn_local_fwd",
    )(post, proj, alog_row, dtb_row)
    return tuple(res[:6]), res[6]


def _by_head(ref):
    return jnp.stack([ref[:, h * HEAD_DIM:(h + 1) * HEAD_DIM] for h in range(ref.shape[1] // HEAD_DIM)])


def _gdn_state_specs(n_of):
    wide = pl.BlockSpec((CHUNK, GDN_WIDTH), lambda n: (n_of(n), 0))
    attn = pl.BlockSpec((GDN_HEADS, CHUNK, CHUNK), lambda n: (0, n_of(n), 0))
    elast = pl.BlockSpec((GDN_HEADS, 1, 1, LANE), lambda n: (0, n_of(n), 0, 0))
    saved = pl.BlockSpec((GDN_HEADS, 1, HEAD_DIM, HEAD_DIM), lambda n: (0, n_of(n), 0, 0))
    return wide, attn, elast, saved


def _gdn_state_fwd(u, w, qg, kdec, attn, elast):
    t = u.shape[0]
    n_chunks = t // CHUNK

    def body(u_ref, w_ref, qg_ref, kd_ref, at_ref, el_ref, o_ref, save_ref, state_ref):
        @pl.when(pl.program_id(0) == 0)
        def _():
            state_ref[...] = jnp.zeros_like(state_ref)

        state = state_ref[...]
        save_ref[:, 0] = state
        o, new_state = _gdn_state_fn(_by_head(u_ref), _by_head(w_ref), _by_head(qg_ref), _by_head(kd_ref), at_ref[...],
                                     el_ref[:, 0], state, _BDOT_BATCH_PLAIN)
        for h in range(GDN_HEADS):
            o_ref[:, h * HEAD_DIM:(h + 1) * HEAD_DIM] = o[h]
        state_ref[...] = new_state

    wide, attn_spec, elast_spec, saved_spec = _gdn_state_specs(lambda n: n)
    return pl.pallas_call(
        body, grid=(n_chunks,), in_specs=[wide, wide, wide, wide, attn_spec, elast_spec],
        out_specs=[wide, saved_spec],
        out_shape=[jax.ShapeDtypeStruct((t, GDN_WIDTH), F32),
                   jax.ShapeDtypeStruct((GDN_HEADS, n_chunks, HEAD_DIM, HEAD_DIM), F32)],
        scratch_shapes=[pltpu.VMEM((GDN_HEADS, HEAD_DIM, HEAD_DIM), F32)],
        compiler_params=_params("arbitrary"), name="gdn_state_fwd",
    )(u, w, qg, kdec, attn, elast)


def _gdn_state_bwd(u, w, qg, kdec, attn, elast, saved, do):
    t = u.shape[0]
    n_chunks = t // CHUNK
    last = n_chunks - 1

    def body(u_ref, w_ref, qg_ref, kd_ref, at_ref, el_ref, save_ref, do_ref,
             du_ref, dw_ref, dqg_ref, dkd_ref, dat_ref, del_ref, dstate_ref):
        @pl.when(pl.program_id(0) == 0)
        def _():
            dstate_ref[...] = jnp.zeros_like(dstate_ref)

        _, vjp = jax.vjp(
            lambda *a: _gdn_state_fn(*a, _BDOT_BATCH_VJP), _by_head(u_ref), _by_head(w_ref).astype(F32),
            _by_head(qg_ref).astype(F32), _by_head(kd_ref).astype(F32), at_ref[...].astype(F32), el_ref[:, 0],
            save_ref[:, 0])
        du, dw, dqg, dkd, dat, de, dstate = vjp((_by_head(do_ref), dstate_ref[...]))
        for h in range(GDN_HEADS):
            cols = slice(h * HEAD_DIM, (h + 1) * HEAD_DIM)
            du_ref[:, cols] = du[h]
            dw_ref[:, cols] = dw[h]
            dqg_ref[:, cols] = dqg[h]
            dkd_ref[:, cols] = dkd[h]
        dat_ref[...] = dat
        del_ref[:, 0] = de
        dstate_ref[...] = dstate

    wide, attn_spec, elast_spec, saved_spec = _gdn_state_specs(lambda n: last - n)
    wide_f32 = jax.ShapeDtypeStruct((t, GDN_WIDTH), F32)
    return pl.pallas_call(
        body, grid=(n_chunks,), in_specs=[wide, wide, wide, wide, attn_spec, elast_spec, saved_spec, wide],
        out_specs=[wide, wide, wide, wide, attn_spec, elast_spec],
        out_shape=[wide_f32, wide_f32, wide_f32, wide_f32, jax.ShapeDtypeStruct((GDN_HEADS, t, CHUNK), F32),
                   jax.ShapeDtypeStruct((GDN_HEADS, n_chunks, 1, LANE), F32)],
        scratch_shapes=[pltpu.VMEM((GDN_HEADS, HEAD_DIM, HEAD_DIM), F32)],
        compiler_params=_params("arbitrary"), name="gdn_state_bwd",
    )(u, w, qg, kdec, attn, elast, saved, do)


def _gdn_local_bwd(post, proj, alog_row, dtb_row, t_inv, cots, dproj):
    t = post.shape[0]
    n_chunks = t // CHUNK
    hb = LOCAL_HEADS_PER_STEP
    n_steps = GDN_HEADS // hb

    def body(qkv_ref, ba_ref, al_ref, dt_ref, ti_ref, du_ref, dw_ref, dqg_ref, dkd_ref, dat_ref, del_ref, _,
             dqkv_ref, dba_ref, dal_ref, ddt_ref, dba_acc):
        n = pl.program_id(0)
        j = pl.program_id(1)

        @pl.when((n == 0) & (j == 0))
        def _():
            dal_ref[...] = jnp.zeros_like(dal_ref)
            ddt_ref[...] = jnp.zeros_like(ddt_ref)

        @pl.when(j == 0)
        def _():
            dba_acc[...] = jnp.zeros_like(dba_acc)

        t_known = ti_ref[...]
        _, vjp = jax.vjp(
            lambda a, b, c, d: _gdn_local_fn(a, b, c, d, j * hb, _BDOT_BATCH_VJP, _FDOT_BATCH_VJP, t_known)[:6],
            qkv_ref[...], ba_ref[...], al_ref[...], dt_ref[...])
        dqkv, dba, dal, ddt = vjp((_by_head(du_ref), _by_head(dw_ref), _by_head(dqg_ref), _by_head(dkd_ref), dat_ref[...],
                                   del_ref[:, 0]))
        dqkv_ref[...] = dqkv
        dba_acc[...] += dba
        dal_ref[...] += dal
        ddt_ref[...] += ddt

        @pl.when(j == n_steps - 1)
        def _():
            dba_ref[:, 0:LANE] = dba_acc[...].astype(dba_ref.dtype)
            dba_ref[:, LANE:2 * LANE] = jnp.zeros((CHUNK, LANE), dba_ref.dtype)

    wide = pl.BlockSpec((CHUNK, hb * HEAD_DIM), lambda n, j: (n, j))
    qkv_spec = pl.BlockSpec((CHUNK, hb * 3 * HEAD_DIM), lambda n, j: (n, j))
    row = pl.BlockSpec((1, LANE), lambda n, j: (0, 0))
    return pl.pallas_call(
        body, grid=(n_chunks, n_steps),
        in_specs=[qkv_spec, pl.BlockSpec((CHUNK, LANE), lambda n, j: (n, BA_BLK)), row, row,
                  pl.BlockSpec((hb, CHUNK, CHUNK), lambda n, j: (j, n, 0)), wide, wide, wide, wide,
                  pl.BlockSpec((hb, CHUNK, CHUNK), lambda n, j: (j, n, 0)),
                  pl.BlockSpec((hb, 1, 1, LANE), lambda n, j: (j, n, 0, 0)), pl.BlockSpec(memory_space=pl.ANY)],
        out_specs=[qkv_spec, pl.BlockSpec((CHUNK, 2 * LANE), lambda n, j: (n, BA_BLK // 2)), row, row],
        out_shape=[jax.ShapeDtypeStruct((t, QKV_COLS), F32), jax.ShapeDtypeStruct(dproj.shape, dproj.dtype),
                   jax.ShapeDtypeStruct((1, LANE), F32), jax.ShapeDtypeStruct((1, LANE), F32)],
        input_output_aliases={11: 1},
        scratch_shapes=[pltpu.VMEM((CHUNK, LANE), F32)],
        compiler_params=_params("arbitrary", "arbitrary"), name="gdn_local_bwd",
    )(post, proj, alog_row, dtb_row, t_inv, *cots, dproj)


def _onorm_fn(o, z, w):
    return o * lax.rsqrt(jnp.mean(o * o, axis=1, keepdims=True) + NORM_EPS) * w * (z * jax.nn.sigmoid(z))


def _onorm_fwd(o_raw, proj, norm_w, mixin, tm=512):
    t = o_raw.shape[0]
    tm = min(tm, t)

    def body(o_ref, z_ref, w_ref, _, out_ref):
        out_ref[...] = _onorm_fn(o_ref[...], z_ref[...], w_ref[...]).astype(out_ref.dtype)

    return pl.pallas_call(
        body, grid=(t // tm, GDN_HEADS),
        in_specs=[pl.BlockSpec((tm, LANE), lambda i, h: (i, h)), pl.BlockSpec((tm, LANE), lambda i, h: (i, Z_BLK + h)),
                  pl.BlockSpec((1, LANE), lambda i, h: (0, 0)), pl.BlockSpec(memory_space=pl.ANY)],
        out_specs=pl.BlockSpec((tm, LANE), lambda i, h: (i, h)),
        out_shape=jax.ShapeDtypeStruct(mixin.shape, mixin.dtype), input_output_aliases={3: 0},
        compiler_params=_params("parallel", "parallel"), name="gdn_onorm_fwd",
    )(o_raw, proj, norm_w, mixin)


def _onorm_bwd(o_raw, proj, norm_w, dmixin, dproj, tm=512):
    t = o_raw.shape[0]
    tm = min(tm, t)

    def body(o_ref, z_ref, w_ref, d_ref, _, do_ref, dz_ref, dw_ref):
        @pl.when((pl.program_id(0) == 0) & (pl.program_id(1) == 0))
        def _():
            dw_ref[...] = jnp.zeros_like(dw_ref)

        _, vjp = jax.vjp(_onorm_fn, o_ref[...], z_ref[...], w_ref[...])
        do, dz, dw = vjp(d_ref[...])
        do_ref[...] = do
        dz_ref[...] = dz.astype(dz_ref.dtype)
        dw_ref[...] += dw

    return pl.pallas_call(
        body, grid=(t // tm, GDN_HEADS),
        in_specs=[pl.BlockSpec((tm, LANE), lambda i, h: (i, h)), pl.BlockSpec((tm, LANE), lambda i, h: (i, Z_BLK + h)),
                  pl.BlockSpec((1, LANE), lambda i, h: (0, 0)), pl.BlockSpec((tm, LANE), lambda i, h: (i, h)),
                  pl.BlockSpec(memory_space=pl.ANY)],
        out_specs=[pl.BlockSpec((tm, LANE), lambda i, h: (i, h)), pl.BlockSpec((tm, LANE), lambda i, h: (i, Z_BLK + h)),
                   pl.BlockSpec((1, LANE), lambda i, h: (0, 0))],
        out_shape=[jax.ShapeDtypeStruct((t, GDN_WIDTH), F32), jax.ShapeDtypeStruct(dproj.shape, dproj.dtype),
                   jax.ShapeDtypeStruct((1, LANE), F32)],
        input_output_aliases={4: 1},
        compiler_params=_params("arbitrary", "arbitrary"), name="gdn_onorm_bwd",
    )(o_raw, proj, norm_w, dmixin, dproj)


def _pool_select(levels, gi):
    out = levels[-1]
    for lvl in range(len(levels) - 2, -1, -1):
        out = jnp.where(gi == lvl, levels[lvl], out)
    return out


def _pool_count(shape, gi):
    pos = lax.broadcasted_iota(jnp.int32, shape, 0)
    win = lax.shift_left(jnp.int32(2), gi)
    return jnp.minimum(pos + 1, win).astype(F32)


def _pooled(p, gi):
    acc = p
    levels = []
    for lvl in range(POOL_GROUPS):
        acc = acc + _shift_down(acc, 1 << lvl)
        levels.append(acc)
    return _pool_select(levels, gi) / _pool_count(p.shape, gi) - p


def _pool_fwd(proj, pool_w, pool_scale):
    t = proj.shape[0]

    def body(p_ref, w_ref, s_ref, out_ref):
        gi = pl.program_id(0)
        pooled = _pooled(p_ref[...], gi)
        out_ref[...] = (_BDOT_PLAIN[0](pooled, w_ref[0]) * s_ref[0]).astype(out_ref.dtype)

    return pl.pallas_call(
        body, grid=(POOL_GROUPS,),
        in_specs=[pl.BlockSpec((t, POOL_GROUP_DIM), lambda g: (0, POOL_BLK + g)),
                  pl.BlockSpec((1, POOL_GROUP_DIM, POOL_GROUP_DIM), lambda g: (g, 0, 0)),
                  pl.BlockSpec((1, 1, POOL_GROUP_DIM), lambda g: (g, 0, 0))],
        out_specs=pl.BlockSpec((t, POOL_GROUP_DIM), lambda g: (0, GDN_WIDTH // POOL_GROUP_DIM + g)),
        out_shape=jax.ShapeDtypeStruct((t, 2 * GDN_WIDTH), BF16),
        compiler_params=_params("parallel"), name="pool_fwd",
    )(proj, pool_w, pool_scale)


def _pool_bwd(proj, pool_w, pool_scale, dmixin):
    t = proj.shape[0]
    nn, nt, tn = _BDOT_PLAIN

    def body(p_ref, w_ref, s_ref, d_ref, dp_ref, dw_ref, ds_ref):
        gi = pl.program_id(0)
        p = p_ref[...]
        pooled = _pooled(p, gi)
        mixed = nn(pooled, w_ref[0])
        d = d_ref[...]
        ds_ref[0] = jnp.sum(d * mixed, axis=0, keepdims=True)
        dmixed = d * s_ref[0]
        dw_ref[0] = tn(pooled, dmixed)
        dpooled = nt(dmixed, w_ref[0])
        acc = dpooled / _pool_count(p.shape, gi)
        levels = []
        for lvl in range(POOL_GROUPS):
            acc = acc + _shift_up(acc, 1 << lvl)
            levels.append(acc)
        dp_ref[...] = (_pool_select(levels, gi) - dpooled).astype(dp_ref.dtype)

    return pl.pallas_call(
        body, grid=(POOL_GROUPS,),
        in_specs=[pl.BlockSpec((t, POOL_GROUP_DIM), lambda g: (0, POOL_BLK + g)),
                  pl.BlockSpec((1, POOL_GROUP_DIM, POOL_GROUP_DIM), lambda g: (g, 0, 0)),
                  pl.BlockSpec((1, 1, POOL_GROUP_DIM), lambda g: (g, 0, 0)),
                  pl.BlockSpec((t, POOL_GROUP_DIM), lambda g: (0, GDN_WIDTH // POOL_GROUP_DIM + g))],
        out_specs=[pl.BlockSpec((t, POOL_GROUP_DIM), lambda g: (0, POOL_BLK + g)),
                   pl.BlockSpec((1, POOL_GROUP_DIM, POOL_GROUP_DIM), lambda g: (g, 0, 0)),
                   pl.BlockSpec((1, 1, POOL_GROUP_DIM), lambda g: (g, 0, 0))],
        out_shape=[jax.ShapeDtypeStruct((t, PROJ_COLS), BF16),
                   jax.ShapeDtypeStruct((POOL_GROUPS, POOL_GROUP_DIM, POOL_GROUP_DIM), F32),
                   jax.ShapeDtypeStruct((POOL_GROUPS, 1, POOL_GROUP_DIM), F32)],
        compiler_params=_params("parallel"), name="pool_bwd",
    )(proj, pool_w, pool_scale, dmixin)


def _ln_stats(s):
    mu = jnp.mean(s, axis=1, keepdims=True)
    xc = s - mu
    var = jnp.mean(xc * xc, axis=1, keepdims=True)
    rstd = lax.rsqrt(var + LN_EPS)
    return xc * rstd, rstd


def _ln_fwd(h_in, y, g, b, *, name, tm=256):
    t, d = h_in.shape
    tm = min(tm, t)

    def body(h_ref, y_ref, g_ref, b_ref, o_ref, o16_ref):
        xhat, _ = _ln_stats(ALPHA * h_ref[...] + y_ref[...])
        out = xhat * g_ref[...] + b_ref[...]
        o_ref[...] = out
        o16_ref[...] = out.astype(BF16)

    row = pl.BlockSpec((tm, d), lambda i: (i, 0))
    vec = pl.BlockSpec((1, d), lambda i: (0, 0))
    return pl.pallas_call(
        body, grid=(t // tm,), in_specs=[row, row, vec, vec], out_specs=[row, row],
        out_shape=[jax.ShapeDtypeStruct((t, d), F32), jax.ShapeDtypeStruct((t, d), BF16)],
        compiler_params=_params("parallel"), name=name,
    )(h_in, y, g, b)


def _ln_backward(xhat, rstd, dout, gain):
    dxhat = dout * gain
    m1 = jnp.mean(dxhat, axis=1, keepdims=True)
    m2 = jnp.mean(dxhat * xhat, axis=1, keepdims=True)
    return (rstd * (dxhat - m1 - xhat * m2), jnp.sum(dout * xhat, axis=0, keepdims=True),
            jnp.sum(dout, axis=0, keepdims=True))


def _ln_loss(h_in, y, g, b, target, *, name, tm=256):
    t, d = h_in.shape
    tm = min(tm, t)

    def body(h_ref, y_ref, g_ref, b_ref, t_ref, sq_ref, ds_ref, ds16_ref, dg_ref, dbias_ref):
        @pl.when(pl.program_id(0) == 0)
        def _():
            sq_ref[...] = jnp.zeros_like(sq_ref)
            dg_ref[...] = jnp.zeros_like(dg_ref)
            dbias_ref[...] = jnp.zeros_like(dbias_ref)

        xhat, rstd = _ln_stats(ALPHA * h_ref[...] + y_ref[...])
        err = xhat * g_ref[...] + b_ref[...] - t_ref[...]
        sq_ref[...] += jnp.sum(jnp.sum(err * err, axis=1, keepdims=True), axis=0, keepdims=True)
        ds, dg, dbias = _ln_backward(xhat, rstd, err * (1.0 / d), g_ref[...])
        ds_ref[...] = ds
        ds16_ref[...] = ds.astype(BF16)
        dg_ref[...] += dg
        dbias_ref[...] += dbias

    row = pl.BlockSpec((tm, d), lambda i: (i, 0))
    vec = pl.BlockSpec((1, d), lambda i: (0, 0))
    return pl.pallas_call(
        body, grid=(t // tm,), in_specs=[row, row, vec, vec, row],
        out_specs=[pl.BlockSpec((1, LANE), lambda i: (0, 0)), row, row, vec, vec],
        out_shape=[jax.ShapeDtypeStruct((1, LANE), F32), jax.ShapeDtypeStruct((t, d), F32),
                   jax.ShapeDtypeStruct((t, d), BF16), jax.ShapeDtypeStruct((1, d), F32), jax.ShapeDtypeStruct((1, d), F32)],
        compiler_params=_params("arbitrary"), name=name,
    )(h_in, y, g, b, target)


def _ln_bwd(h_in, y, g, d_a, d_b, *, name, tm=256):
    t, d = h_in.shape
    tm = min(tm, t)
    has_b = d_b is not None

    def body(*refs):
        if has_b:
            h_ref, y_ref, g_ref, da_ref, db_ref, ds_ref, ds16_ref, dg_ref, dbias_ref = refs
        else:
            h_ref, y_ref, g_ref, da_ref, ds_ref, ds16_ref, dg_ref, dbias_ref = refs

        @pl.when(pl.program_id(0) == 0)
        def _():
            dg_ref[...] = jnp.zeros_like(dg_ref)
            dbias_ref[...] = jnp.zeros_like(dbias_ref)

        xhat, rstd = _ln_stats(ALPHA * h_ref[...] + y_ref[...])
        dout = da_ref[...]
        if has_b:
            dout = dout + ALPHA * db_ref[...]
        ds, dg, dbias = _ln_backward(xhat, rstd, dout, g_ref[...])
        ds_ref[...] = ds
        ds16_ref[...] = ds.astype(BF16)
        dg_ref[...] += dg
        dbias_ref[...] += dbias

    row = pl.BlockSpec((tm, d), lambda i: (i, 0))
    vec = pl.BlockSpec((1, d), lambda i: (0, 0))
    args = [h_in, y, g, d_a] + ([d_b] if has_b else [])
    return pl.pallas_call(
        body, grid=(t // tm,), in_specs=[row, row, vec, row] + ([row] if has_b else []),
        out_specs=[row, row, vec, vec],
        out_shape=[jax.ShapeDtypeStruct((t, d), F32), jax.ShapeDtypeStruct((t, d), BF16),
                   jax.ShapeDtypeStruct((1, d), F32), jax.ShapeDtypeStruct((1, d), F32)],
        compiler_params=_params("arbitrary"), name=name,
    )(*args)


def _attn_fn(q, k, v, dots):
    nn, nt, _ = dots
    s = nt(q, k) * (XATTN_HEAD_DIM ** -0.5)
    s = s - lax.stop_gradient(jnp.max(s, axis=1, keepdims=True))
    e = jnp.exp(s)
    p = e / jnp.sum(e, axis=1, keepdims=True)
    return nn(p, v)


def _attn_fwd(q, k, v, tq=512):
    t = q.shape[0]
    tq = min(tq, t)

    def body(q_ref, k_ref, v_ref, o_ref):
        o_ref[...] = _attn_fn(q_ref[...], k_ref[...], v_ref[...], _BDOT_PLAIN).astype(BF16)

    qs = pl.BlockSpec((tq, XATTN_HEAD_DIM), lambda h, i: (i, h))
    ks = pl.BlockSpec((MEM_LEN, XATTN_HEAD_DIM), lambda h, i: (0, h))
    return pl.pallas_call(
        body, grid=(XATTN_HEADS, t // tq), in_specs=[qs, ks, ks], out_specs=qs,
        out_shape=jax.ShapeDtypeStruct(q.shape, BF16), compiler_params=_params("parallel", "parallel"), name="xattn_fwd",
    )(q, k, v)


def _attn_bwd(q, k, v, do, tq=512):
    t = q.shape[0]
    tq = min(tq, t)

    def body(q_ref, k_ref, v_ref, do_ref, dq_ref, dk_ref, dv_ref):
        @pl.when(pl.program_id(1) == 0)
        def _():
            dk_ref[...] = jnp.zeros_like(dk_ref)
            dv_ref[...] = jnp.zeros_like(dv_ref)

        _, vjp = jax.vjp(lambda a, b, c: _attn_fn(a, b, c, _BDOT_VJP), q_ref[...].astype(F32), k_ref[...].astype(F32),
                         v_ref[...].astype(F32))
        dq, dk, dv = vjp(do_ref[...].astype(F32))
        dq_ref[...] = dq.astype(BF16)
        dk_ref[...] += dk
        dv_ref[...] += dv

    qs = pl.BlockSpec((tq, XATTN_HEAD_DIM), lambda h, i: (i, h))
    ks = pl.BlockSpec((MEM_LEN, XATTN_HEAD_DIM), lambda h, i: (0, h))
    return pl.pallas_call(
        body, grid=(XATTN_HEADS, t // tq), in_specs=[qs, ks, ks, qs], out_specs=[qs, ks, ks],
        out_shape=[jax.ShapeDtypeStruct(q.shape, BF16), jax.ShapeDtypeStruct(k.shape, F32), jax.ShapeDtypeStruct(v.shape, F32)],
        compiler_params=_params("parallel", "arbitrary"), name="xattn_bwd",
    )(q, k, v, do)


def _local_step(x, x16, mem, target, weights_of, grads_ready):
    def behind(vec, token):
        return vec if token is None else vec + token

    w = dict(weights_of("mixer", None))
    proj = _mm(x16, w["w_in"], tb=True, tn=768, name="mm_in_proj")
    mixin = _pool_fwd(proj, w["pool_w"], w["pool_scale"])
    post = _gdn_prep_fwd(proj, w["conv_w"])
    token = weights_of("ahead_conv", post)
    chunked, t_inv = _gdn_local_fwd(post, proj, behind(w["alog_row"], token), w["dtb_row"])
    o_raw, saved = _gdn_state_fwd(*chunked)
    token = weights_of("ahead_scan", o_raw)
    mixin = _onorm_fwd(o_raw, proj, behind(w["gdn_norm_w"], token), mixin)
    w.update(weights_of("attn", mixin))
    mix = _mm(mixin, w["w_out"], name="mm_out_proj")
    h1, h1_16 = _ln_fwd(x, mix, w["ln1_g"], w["ln1_b"], name="ln1_fwd")
    xq = _mm(h1_16, w["xq_w"], out_dtype=BF16, name="mm_xq")
    xk = _mm(mem, w["xk_w"], out_dtype=BF16, name="mm_xk")
    xv = _mm(mem, w["xv_w"], out_dtype=BF16, name="mm_xv")
    xo = _attn_fwd(xq, xk, xv)
    token = weights_of("ahead_attn", xo)
    if token is not None:
        xo, _ = lax.optimization_barrier((xo, token))
    xa = _mm(xo, w["xo_w"], name="mm_xo")
    h2, h2_16 = _ln_fwd(h1, xa, w["ln2_g"], w["ln2_b"], name="ln2_fwd")
    w.update(weights_of("up", h2_16))
    act, relu = _mm(h2_16, w["w_up"], b_chunks=True, epi="relu2", name="mm_up")
    w.update(weights_of("down", act))
    ff = _mm(act, w["w_down"], tn=512, tk=2048, name="mm_down")
    g = {}
    sq, ds3, ds3_16, g["ln3_g"], g["ln3_b"] = _ln_loss(h2, ff, w["ln3_g"], w["ln3_b"], target, name="ln3_loss")

    gw_down = _mm(act, ds3_16, ta=True, out_dtype=BF16, tm=512, tn=D_MODEL, name="mm_gw_down")
    du = _mm(ds3_16, w["w_down"], tb=True, epi="mul2r", extra=relu, name="mm_du")
    gw_up = _mm(h2_16, du, ta=True, out_dtype=BF16, o_chunks=True, name="mm_gw_up")
    token = grads_ready("mlp", {"w_down": gw_down, "w_up": gw_up})
    dh2 = _mm(du, w["w_up"], tb=True, b_chunks=True, tn=1024, tk=1024, name="mm_dh2")
    ds2, ds2_16, g["ln2_g"], g["ln2_b"] = _ln_bwd(h1, xa, behind(w["ln2_g"], token), dh2, ds3, name="ln2_bwd")
    gw_xo = _mm(xo, ds2_16, ta=True, out_dtype=BF16, name="mm_gw_xo")
    dxo = _mm(ds2_16, w["xo_w"], tb=True, out_dtype=BF16, name="mm_dxo")
    dxq, dxk, dxv = _attn_bwd(xq, xk, xv, dxo)
    gw_xq = _mm(h1_16, dxq, ta=True, out_dtype=BF16, name="mm_gw_xq")
    gw_xk = _mm(mem, dxk, ta=True, out_dtype=BF16, name="mm_gw_xk")
    gw_xv = _mm(mem, dxv, ta=True, out_dtype=BF16, name="mm_gw_xv")
    token = grads_ready("attn", {"xo_w": gw_xo, "xq_w": gw_xq, "xk_w": gw_xk, "xv_w": gw_xv})
    dh1 = _mm(dxq, w["xq_w"], tb=True, name="mm_dh1")
    ds1, ds1_16, g["ln1_g"], g["ln1_b"] = _ln_bwd(x, mix, behind(w["ln1_g"], token), dh1, ds2, name="ln1_bwd")
    gw_out = _mm(mixin, ds1_16, ta=True, out_dtype=BF16, name="mm_gw_out")
    dmixin = _mm(ds1_16, w["w_out"], tb=True, name="mm_dmixin")
    dproj, gw_pool, g["pool_scale"] = _pool_bwd(proj, w["pool_w"], w["pool_scale"], dmixin)
    token = grads_ready("mix", {"w_out": gw_out, "pool_w": gw_pool})
    do_raw, dproj, g["gdn_norm_w"] = _onorm_bwd(o_raw, proj, behind(w["gdn_norm_w"], token), dmixin, dproj)
    cots = _gdn_state_bwd(*chunked, saved, do_raw)
    token = grads_ready("tick", {"after": cots[0]})
    dpost, dproj, g["alog_row"], g["dtb_row"] = _gdn_local_bwd(post, proj, behind(w["alog_row"], token), w["dtb_row"],
                                                               t_inv, cots, dproj)
    dproj, g["conv_w"] = _gdn_prep_bwd(proj, w["conv_w"], dpost, dproj)
    token = grads_ready("small", {**g, "sq": sq})
    gw_in = _mm(dproj, x16, ta=True, out_dtype=BF16, tm=768, tn=D_MODEL, after=token, name="mm_gw_in")
    token = grads_ready("in", {"w_in": gw_in})
    grad_x = _mm(dproj, w["w_in"], tk=1792, epi="add", extra=ds1, add_scale=ALPHA, after=token, name="mm_dx")
    return sq, grad_x, g


_MATRICES = ("w_in", "pool_w", "w_out", "xq_w", "xk_w", "xv_w", "xo_w", "w_up", "w_down")
_VECTORS = ("a_log", "dt_bias", "gdn_norm_w", "pool_scale", "ln1_g", "ln1_b", "ln2_g", "ln2_b", "ln3_g", "ln3_b")
_BA_SPLIT = BA_OFF + 2 * GDN_HEADS


def _lane_row(v, offset):
    return jnp.zeros((1, LANE), F32).at[0, offset:offset + v.shape[0]].set(v)


_GROUP_VECTORS = {"mixer": (), "attn": ("ln1_g", "ln1_b", "ln2_g", "ln2_b"), "up": (), "down": ("ln3_g", "ln3_b")}


def _group_weights(group, full):
    w = {n: full[n].reshape(1, D_MODEL) for n in _GROUP_VECTORS[group]}
    if group == "mixer":
        w.update({
            "w_in": _w_in_padded(full["w_in"]),
            "conv_w": full["conv_w"],
            "alog_row": _lane_row(full["a_log"], GDN_HEADS),
            "dtb_row": _lane_row(full["dt_bias"], GDN_HEADS),
            "gdn_norm_w": full["gdn_norm_w"].reshape(1, LANE),
            "pool_w": full["pool_w"],
            "pool_scale": full["pool_scale"].reshape(POOL_GROUPS, 1, POOL_GROUP_DIM),
        })
    else:
        w.update({n: full[n] for n in dict(_GATHER_GROUPS)[group]})
    return w


def _w_in_row_map():
    per = IN_COLS // N_DEV
    gap = POOL_OFF - _BA_SPLIT
    pieces = []
    for d in range(N_DEV):
        lo, hi = d * per, (d + 1) * per
        if hi <= _BA_SPLIT:
            pieces.append([(0, lo, per)])
        elif lo >= _BA_SPLIT:
            pieces.append([(0, lo + gap, per)])
        else:
            pieces.append([(0, lo, _BA_SPLIT - lo), (_BA_SPLIT - lo, POOL_OFF, hi - _BA_SPLIT)])
    return pieces


_W_IN_LANES = 256


def _w_in_padded(blocks):
    def body(b_ref, o_ref):
        for d, pieces in enumerate(_w_in_row_map()):
            for src, dst, rows in pieces:
                o_ref[dst:dst + rows, :] = b_ref[d, src:src + rows, :]
        o_ref[_BA_SPLIT:POOL_OFF, :] = jnp.zeros((POOL_OFF - _BA_SPLIT, _W_IN_LANES), o_ref.dtype)

    n, per, cols = blocks.shape
    return pl.pallas_call(
        body, grid=(cols // _W_IN_LANES,), in_specs=[pl.BlockSpec((n, per, _W_IN_LANES), lambda j: (0, 0, j))],
        out_specs=pl.BlockSpec((PROJ_COLS, _W_IN_LANES), lambda j: (0, j)),
        out_shape=jax.ShapeDtypeStruct((PROJ_COLS, cols), blocks.dtype), compiler_params=_params("parallel"),
        name="w_in_padded")(blocks)


def _w_in_chunks(g):
    def body(g_ref, o_ref):
        for d, pieces in enumerate(_w_in_row_map()):
            for dst, src, rows in pieces:
                o_ref[d, dst:dst + rows, :] = g_ref[src:src + rows, :]

    cols = g.shape[1]
    per = IN_COLS // N_DEV
    return pl.pallas_call(
        body, grid=(cols // _W_IN_LANES,), in_specs=[pl.BlockSpec((PROJ_COLS, _W_IN_LANES), lambda j: (0, j))],
        out_specs=pl.BlockSpec((N_DEV, per, _W_IN_LANES), lambda j: (0, 0, j)),
        out_shape=jax.ShapeDtypeStruct((N_DEV, per, cols), g.dtype), compiler_params=_params("parallel"),
        name="w_in_chunks")(g)


def _finish_small_grads(g):
    out = {"conv_w": g["conv_w"]}
    out["a_log"] = g["alog_row"][0, GDN_HEADS:2 * GDN_HEADS]
    out["dt_bias"] = g["dtb_row"][0, GDN_HEADS:2 * GDN_HEADS]
    out["gdn_norm_w"] = g["gdn_norm_w"].reshape(LANE)
    out["pool_scale"] = g["pool_scale"].reshape(POOL_GROUPS * POOL_GROUP_DIM)
    for n in ("ln1_g", "ln1_b", "ln2_g", "ln2_b", "ln3_g", "ln3_b"):
        out[n] = g[n].reshape(D_MODEL)
    return out


def _adamw_math(w, g, m, v):
    m = ADAM_B1 * m + (1.0 - ADAM_B1) * g
    v = ADAM_B2 * v + (1.0 - ADAM_B2) * (g * g)
    m_hat = m / (1.0 - ADAM_B1 ** ADAM_STEP)
    v_hat = v / (1.0 - ADAM_B2 ** ADAM_STEP)
    delta = -ADAM_LR * (m_hat / (jnp.sqrt(v_hat) + ADAM_EPS) + ADAM_WD * w)
    return delta, m, v


ADAMW_TILE_ELEMS = 256 * 1024
CHIP_SUM_TILE_ELEMS = 1024 * 1024


def _shard_tile(r, c, elems):
    for rows in (1024, 512, 256, 128):
        if r % rows == 0 and rows * c <= elems:
            return rows, c
    if r % 128 == 0:
        return 128, c
    return r, 256 if c % 256 == 0 else c


def _adamw_shard(parts, own, me, w, m, v, *, name):
    s, r, c = parts.shape
    tr, tc = _shard_tile(r, c, ADAMW_TILE_ELEMS)
    assert r % tr == 0 and c % tc == 0, (name, r, c)
    unit_axis = w.ndim == 3
    at = (slice(None), 0, slice(None)) if unit_axis else Ellipsis

    def body(me_ref, p_ref, own_ref, w_ref, m_ref, v_ref, g_ref, d_ref, nm_ref, nv_ref):
        mine = own_ref[...].astype(F32)
        g = None
        for i in range(s):
            part = jnp.where(me_ref[0] == i, mine, p_ref[i].astype(F32))
            g = part if g is None else g + part
        delta, nm, nv = _adamw_math(w_ref[at], g, m_ref[at], v_ref[at])
        g_ref[at] = g
        d_ref[at] = delta
        nm_ref[at] = nm
        nv_ref[at] = nv

    if unit_axis:
        blk = pl.BlockSpec((tr, 1, tc), lambda i, j, me_ref: (i, 0, j))
        out = jax.ShapeDtypeStruct((r, 1, c), F32)
    else:
        blk = pl.BlockSpec((tr, tc), lambda i, j, me_ref: (i, j))
        out = jax.ShapeDtypeStruct((r, c), F32)
    return pl.pallas_call(
        body,
        grid_spec=pltpu.PrefetchScalarGridSpec(
            num_scalar_prefetch=1, grid=(r // tr, c // tc),
            in_specs=[pl.BlockSpec((s, tr, tc), lambda i, j, me_ref: (0, i, j)),
                      pl.BlockSpec((None, tr, tc), lambda i, j, me_ref: (me_ref[0], i, j)), blk, blk, blk],
            out_specs=[blk, blk, blk, blk]),
        out_shape=[out, out, out, out], compiler_params=_params("parallel", "parallel"), name=name,
    )(me, parts, own, w, m, v)


N_CHIPS = N_DEV // 2


def _chip_sums(chunks, from_sibling, core, *, name):
    _, r, c = chunks.shape
    tr, tc = _shard_tile(r, c, CHIP_SUM_TILE_ELEMS)
    assert r % tr == 0 and c % tc == 0, (name, r, c)

    def body(core_ref, mine_ref, other_ref, o_ref):
        o_ref[...] = (mine_ref[...].astype(F32) + other_ref[...].astype(F32)).astype(o_ref.dtype)

    by_chip = pl.BlockSpec((None, tr, tc), lambda q, i, j, core_ref: (q, i, j))
    return pl.pallas_call(
        body,
        grid_spec=pltpu.PrefetchScalarGridSpec(
            num_scalar_prefetch=1, grid=(N_CHIPS, r // tr, c // tc),
            in_specs=[pl.BlockSpec((None, tr, tc), lambda q, i, j, core_ref: (2 * q + core_ref[0], i, j)), by_chip],
            out_specs=by_chip),
        out_shape=jax.ShapeDtypeStruct((N_CHIPS, r, c), chunks.dtype),
        compiler_params=_params("parallel", "parallel", "parallel"), name=name,
    )(core, chunks, from_sibling)


def _place():
    return lax.axis_index("x"), lax.axis_index("y"), lax.axis_index("c")


def _slot(px, py, pc):
    return 4 * px + 2 * py + pc


_HBM = pl.BlockSpec(memory_space=pltpu.HBM)


_SEM = pl.BlockSpec(memory_space=pltpu.SEMAPHORE)
_ANY = pl.BlockSpec(memory_space=pl.ANY)
_EFFECT = pltpu.SideEffectType.DATAFLOW_SIDE_EFFECTING
_N_PEERS = N_DEV - 1


def _peer(k, x, y, c):
    return (1 - x if k & 4 else x, 1 - y if k & 2 else y, 1 - c if k & 1 else c)


_EXCHANGE_BITS = {"gather_near": (1, 2, 4), "gather_relay": (6,), "gather_pass": (2, 4, 6),
                  "scatter_sibling": (1, 1, 1, 1), "scatter_chips": (2, 4, 6), "all_small": (1, 2, 3, 4, 5, 6, 7)}


def _exchange_copy(mode, src, land, w, i, place, send_sems, recv_sems, receiving):
    bits = _EXCHANGE_BITS[mode]
    k = bits[i]
    peer = _peer(k, *place)
    me = _slot(*place)
    if mode in ("gather_near", "all_small"):
        to, src_ref, sent_to, got_at = peer, src[w], me, _slot(*peer)
    elif mode == "gather_relay":
        x, y, c = place
        other = 1 - c
        to = (lax.bitwise_xor(x, c), lax.bitwise_xor(y, other), c)
        blk = _slot(lax.bitwise_xor(x, other), lax.bitwise_xor(y, c), c)
        src_ref, sent_to, got_at = land[w].at[blk], blk, _slot(*peer)
    elif mode == "gather_pass":
        blk = _slot(*peer)
        to, src_ref, sent_to, got_at = _peer(1, *place), land[w].at[blk], blk, _slot(*_peer(k | 1, *place))
    elif mode == "scatter_sibling":
        to, src_ref, sent_to, got_at = peer, src[w].at[2 * i + 1 - place[2]], i, i
    else:
        to, src_ref, sent_to, got_at = peer, src[w].at[_slot(*peer) // 2], me // 2, _slot(*peer) // 2
    sem = w * len(bits) + i
    return pltpu.make_async_remote_copy(
        src_ref=src_ref, dst_ref=land[w].at[got_at if receiving else sent_to], send_sem=send_sems.at[sem],
        recv_sem=recv_sems.at[sem], device_id=to, device_id_type=MESH)


def _exchange_start(mode, srcs, lands, after, *, name):
    ns, nl = len(srcs), len(lands)
    n_sem = nl * len(_EXCHANGE_BITS[mode])

    def body(*refs):
        src, land = refs[:ns], refs[ns:ns + nl]
        send_sems, recv_sems = refs[ns + nl + 1:ns + nl + 3]
        token = refs[-1]
        place = _place()
        for w in range(nl):
            for i in range(len(_EXCHANGE_BITS[mode])):
                _exchange_copy(mode, src, land, w, i, place, send_sems, recv_sems, receiving=False).start()
        token[...] = jnp.zeros_like(token)

    sems = pltpu.SemaphoreType.DMA((n_sem,))
    arrays = list(srcs) + list(lands)
    res = pl.pallas_call(
        body, name=name, in_specs=[_HBM] * (ns + nl) + [_ANY],
        out_specs=(_SEM, _SEM, *([_HBM] * (ns + nl)), pl.BlockSpec(memory_space=pltpu.VMEM)),
        out_shape=(sems, sems, *[pltpu.HBM(a.shape, a.dtype) for a in arrays], jax.ShapeDtypeStruct((8, LANE), F32)),
        input_output_aliases={i: 2 + i for i in range(ns + nl)},
        compiler_params=pltpu.CompilerParams(has_side_effects=_EFFECT),
    )(*[pltpu.with_memory_space_constraint(a, pltpu.HBM) for a in arrays], after)
    return res[0], res[1], list(res[2:2 + ns]), list(res[2 + ns:2 + ns + nl]), res[-1]


def _exchange_wait(mode, started, after, *, name):
    send_sems, recv_sems, srcs, lands, _ = started
    ns, nl = len(srcs), len(lands)

    def body(*refs):
        src, land = refs[:ns], refs[ns:ns + nl]
        send_sems, recv_sems = refs[ns + nl:ns + nl + 2]
        place = _place()
        for w in range(nl):
            for i in range(len(_EXCHANGE_BITS[mode])):
                cp = _exchange_copy(mode, src, land, w, i, place, send_sems, recv_sems, receiving=True)
                cp.wait_send()
                cp.wait_recv()

    arrays = list(srcs) + list(lands)
    res = pl.pallas_call(
        body, name=name, in_specs=[_HBM] * (ns + nl) + [_SEM, _SEM, _ANY], out_specs=[_HBM] * (ns + nl),
        out_shape=[pltpu.HBM(a.shape, a.dtype) for a in arrays],
        input_output_aliases={i: i for i in range(ns + nl)},
        compiler_params=pltpu.CompilerParams(has_side_effects=_EFFECT),
    )(*arrays, send_sems, recv_sems, after)
    return list(res[:ns]), list(res[ns:])


_SMALL_SEGMENTS = (("a_log", GDN_HEADS), ("dt_bias", GDN_HEADS), ("gdn_norm_w", HEAD_DIM), ("pool_scale", GDN_WIDTH),
                   ("ln1_g", D_MODEL), ("ln1_b", D_MODEL), ("ln2_g", D_MODEL), ("ln2_b", D_MODEL),
                   ("ln3_g", D_MODEL), ("ln3_b", D_MODEL), ("conv_w", CONV_K * QKV_COLS), ("loss", 1))
_SMALL_ROWS = 8
_SMALL_LEN = -(-sum(sz for _, sz in _SMALL_SEGMENTS) // (_SMALL_ROWS * LANE)) * LANE


def _pack_small(vals):
    parts = [vals[n].reshape(-1).astype(F32) if n in vals else jnp.zeros((sz,), F32) for n, sz in _SMALL_SEGMENTS]
    flat = jnp.concatenate(parts)
    flat = jnp.pad(flat, (0, _SMALL_ROWS * _SMALL_LEN - flat.shape[0]))
    return flat.reshape(_SMALL_ROWS, _SMALL_LEN)


def _unpack_small(vec):
    flat = vec.reshape(-1)
    out, off = {}, 0
    for n, sz in _SMALL_SEGMENTS:
        out[n] = flat[off:off + sz]
        off += sz
    return out


_WEIGHT_ORDER = ("w_in", "conv_w", "a_log", "dt_bias", "gdn_norm_w", "pool_w", "pool_scale", "w_out", "ln1_g", "ln1_b",
                 "xq_w", "xk_w", "xv_w", "xo_w", "ln2_g", "ln2_b", "w_up", "w_down", "ln3_g", "ln3_b")


def _shard2d(name, a):
    if name == "w_in":
        return a.T
    return a.reshape(-1, a.shape[-1]) if name == "pool_w" else a


def _update_view(name, a):
    return jnp.transpose(a, (2, 0, 1)) if name == "w_in" else _shard2d(name, a[0])


def _shard_result(name, r, shape):
    return jnp.transpose(r, (1, 2, 0)) if name == "w_in" else r.reshape(shape)


def _gathered_to_full(name, gth):
    if name in ("w_up", "w_in"):
        return gth
    if name == "conv_w":
        return jnp.transpose(gth, (1, 0, 2)).reshape(gth.shape[1], N_DEV * gth.shape[2])
    if name == "pool_w":
        g4 = gth.reshape(N_DEV, POOL_GROUPS, POOL_GROUP_DIM // N_DEV, POOL_GROUP_DIM)
        return jnp.transpose(g4, (1, 0, 2, 3)).reshape(POOL_GROUPS, POOL_GROUP_DIM, POOL_GROUP_DIM)
    return gth.reshape(N_DEV * gth.shape[1], gth.shape[2])


def _full_to_chunks(name, full):
    if name == "w_up":
        return full
    if name == "pool_w":
        g4 = full.reshape(POOL_GROUPS, N_DEV, POOL_GROUP_DIM // N_DEV, POOL_GROUP_DIM)
        return jnp.transpose(g4, (1, 0, 2, 3)).reshape(N_DEV, POOL_GROUPS * POOL_GROUP_DIM // N_DEV, POOL_GROUP_DIM)
    return full.reshape(N_DEV, full.shape[0] // N_DEV, full.shape[1])


_GATHER_GROUPS = (("mixer", ("w_in", "conv_w", "pool_w")), ("attn", ("w_out", "xq_w", "xk_w", "xv_w", "xo_w")),
                  ("up", ("w_up",)), ("down", ("w_down",)))


def _grad_chunks(name, g):
    if name == "w_in":
        return _w_in_chunks(g.astype(BF16))
    return _full_to_chunks(name, g.astype(BF16))


def kernel(x, mem, w_in, conv_w, a_log, dt_bias, gdn_norm_w, pool_w, pool_scale, w_out, ln1_g, ln1_b, xq_w, xk_w, xv_w, xo_w, ln2_g, ln2_b, w_up, w_down, ln3_g, ln3_b, loss_target, m_w_in, m_conv_w, m_a_log, m_dt_bias, m_gdn_norm_w, m_pool_w, m_pool_scale, m_w_out, m_ln1_g, m_ln1_b, m_xq_w, m_xk_w, m_xv_w, m_xo_w, m_ln2_g, m_ln2_b, m_w_up, m_w_down, m_ln3_g, m_ln3_b, v_w_in, v_conv_w, v_a_log, v_dt_bias, v_gdn_norm_w, v_pool_w, v_pool_scale, v_w_out, v_ln1_g, v_ln1_b, v_xq_w, v_xk_w, v_xv_w, v_xo_w, v_ln2_g, v_ln2_b, v_w_up, v_w_down, v_ln3_g, v_ln3_b):
    args = dict(locals())
    wt = {n: args[n][0] for n in _WEIGHT_ORDER}
    mo = {n: args["m_" + n][0] for n in _WEIGHT_ORDER}
    vo = {n: args["v_" + n][0] for n in _WEIGHT_ORDER}

    me = _slot(*_place())
    me_arr = jnp.reshape(me, (1,)).astype(jnp.int32)
    nothing = jnp.zeros((8, LANE), F32)

    def landing_zones(names):
        shards = [_shard2d(n, wt[n]).astype(F32 if n == "conv_w" else BF16) for n in names]
        zones = [lax.dynamic_update_slice(lax.empty((N_DEV, *s.shape), s.dtype), s[None], (me, 0, 0)) for s in shards]
        return shards, zones

    chip_arr = jnp.reshape(me // 2, (1,)).astype(jnp.int32)
    core_arr = jnp.reshape(lax.axis_index("c"), (1,)).astype(jnp.int32)
    names_of = dict(_GATHER_GROUPS)
    gathers = {}
    prepared = {}

    def gather_near(group, after):
        shards, zones = prepared.pop(group) if group in prepared else landing_zones(names_of[group])
        gathers[group] = _exchange_start("gather_near", shards, zones, after, name="gather_near_" + group)
        return gathers[group][4]

    def gather_next(group, was, now, after):
        _, zones = _exchange_wait(was, gathers[group], after, name=f"{was}_{group}_wait")
        gathers[group] = _exchange_start(now, [], zones, nothing, name=f"{now}_{group}")
        return gathers[group][4]

    def gather_relay(group, after):
        return gather_next(group, "gather_near", "gather_relay", after)

    def gather_pass(group, after):
        return gather_next(group, "gather_relay", "gather_pass", after)

    def gathered(group, after):
        _, zones = _exchange_wait("gather_pass", gathers[group], after, name=f"gather_pass_{group}_wait")
        full = {n: _gathered_to_full(n, z) for n, z in zip(names_of[group], zones)}
        full.update({n: wt[n] for n in _VECTORS})
        return _group_weights(group, full)

    token = gather_near("mixer", nothing)
    x16 = _cast_bf16(x[0], name="cast_x")
    later = {group: landing_zones(names_of[group]) for group in ("attn", "up", "down")}
    token, x16, later = lax.optimization_barrier((token, x16, later))
    prepared.update(later)
    token = gather_pass("mixer", gather_relay("mixer", token))
    token = gather_near("attn", token)

    def weights_of(group, after):
        if group == "mixer":
            return gathered(group, token)
        if group == "ahead_conv":
            return gather_near("up", gather_relay("attn", after))[0:1, 0:1]
        if group == "ahead_scan":
            return gather_pass("attn", after)[0:1, 0:1]
        if group == "attn":
            return gathered(group, gather_near("down", gather_relay("up", after)))
        if group == "ahead_attn":
            return gather_relay("down", gather_pass("up", after))[0:1, 0:1]
        if group == "up":
            return gathered(group, gather_pass("down", after))
        return gathered(group, after)

    scatters = {}
    in_flight = []

    def chip_stage(after):
        group, names, started = in_flight.pop()
        chunks, from_sibling = _exchange_wait("scatter_sibling", started, after, name=f"scatter_sibling_{group}_wait")
        sums = [_chip_sums(c, f, core_arr, name=f"chip_sums_{n}") for n, c, f in zip(names, chunks, from_sibling)]
        scatters[group] = (names, _exchange_start("scatter_chips", sums, [lax.empty(s.shape, s.dtype) for s in sums],
                                                  nothing, name="scatter_chips_" + group))
        return scatters[group][1][4]

    small_sent = []

    def grads_ready(group, grads):
        if group == "tick":
            return chip_stage(grads["after"])[0:1, 0:1] if in_flight else None
        if group == "small":
            small = _finish_small_grads(grads)
            small["loss"] = 0.5 * grads["sq"][0:1, 0] / D_MODEL
            packed = _pack_small(small)
            zone = lax.empty((N_DEV, *packed.shape), F32)
            small_sent.append(_exchange_start("all_small", [packed], [zone], nothing, name="small_grads_start"))
            return small_sent[0][4][0:1, 0:1]
        names = tuple(grads)
        chunks = [_grad_chunks(n, grads[n]) for n in names]
        token = chip_stage(chunks[0]) if in_flight else nothing
        zones = [lax.empty((N_CHIPS, *c.shape[1:]), c.dtype) for c in chunks]
        started = _exchange_start("scatter_sibling", chunks, zones, token, name="scatter_sibling_" + group)
        in_flight.append((group, names, started))
        return started[4][0:1, 0:1]

    sq, grad_x, g = _local_step(x[0], x16, mem[0], loss_target[0], weights_of, grads_ready)

    out = {}
    after = chip_stage(grad_x)
    for group, (names, started) in scatters.items():
        sums, lands = _exchange_wait("scatter_chips", started, after, name=f"scatter_chips_{group}_wait")
        for n, parts, own in zip(names, lands, sums):
            res = _adamw_shard(parts, own, chip_arr, _update_view(n, args[n]), _update_view(n, args["m_" + n]),
                               _update_view(n, args["v_" + n]), name="adamw_" + n)
            out[n] = [_shard_result(n, r, args[n].shape) for r in res]
            after = res[1]

    (packed,), (zone,) = _exchange_wait("all_small", small_sent[0], after, name="small_grads_wait")
    gs, ds, ms, vs = _adamw_shard(
        zone, jnp.broadcast_to(packed, zone.shape), me_arr, _pack_small({n: wt[n] for n in _VECTORS}),
        _pack_small({n: mo[n] for n in _VECTORS}), _pack_small({n: vo[n] for n in _VECTORS}), name="adamw_small")
    gs, ds, ms, vs = _unpack_small(gs), _unpack_small(ds), _unpack_small(ms), _unpack_small(vs)
    cols = conv_w.shape[-1]
    conv_full = gs["conv_w"].reshape(CONV_K, QKV_COLS)
    conv_mine = lax.dynamic_slice(conv_full, (0, me * cols), (CONV_K, cols))[None]
    res = _adamw_shard(conv_mine, conv_mine, jnp.zeros((1,), jnp.int32), wt["conv_w"], mo["conv_w"], vo["conv_w"],
                       name="adamw_conv_w")
    out["conv_w"] = [r.reshape(conv_w.shape) for r in res]
    for n in _VECTORS:
        out[n] = [t[n].reshape(args[n].shape) for t in (gs, ds, ms, vs)]

    return (gs["loss"][0], grad_x[None], *[out[n][0] for n in _WEIGHT_ORDER], *[out[n][1] for n in _WEIGHT_ORDER],
            *[out[n][2] for n in _WEIGHT_ORDER], *[out[n][3] for n in _WEIGHT_ORDER])
```

```python
import jax
import jax.numpy as jnp
from jax import lax
from jax.experimental import pallas as pl
from jax.experimental.pallas import tpu as pltpu

F32 = jnp.float32
BF16 = jnp.bfloat16
MESH = pl.DeviceIdType.MESH

N_DEV = 8
D_MODEL = 2048
GDN_WIDTH = 1024
GDN_HEADS = 8
HEAD_DIM = 128
CONV_K = 4
CHUNK = 64
POOL_GROUPS = 4
POOL_GROUP_DIM = 256
MEM_LEN = 256
XATTN_HEADS = 4
XATTN_HEAD_DIM = 512
D_FF = 8192
IN_COLS = 5136
ALPHA = 2.0 ** 0.25
LN_EPS = 1e-5
NORM_EPS = 1e-6

LANE = 128
QKV_COLS = 3 * GDN_WIDTH
Z_OFF = QKV_COLS
BA_OFF = 4 * GDN_WIDTH
POOL_OFF = BA_OFF + 2 * LANE
PROJ_COLS = POOL_OFF + GDN_WIDTH
BA_BLK = BA_OFF // LANE
POOL_BLK = POOL_OFF // POOL_GROUP_DIM

ADAM_LR = 0.001
ADAM_B1 = 0.9
ADAM_B2 = 0.999
ADAM_EPS = 1e-08
ADAM_WD = 0.01
ADAM_STEP = 10

VMEM_LIMIT_BYTES = 48 * 1024 * 1024


def _params(*sem):
    return pltpu.CompilerParams(dimension_semantics=sem if sem else None, vmem_limit_bytes=VMEM_LIMIT_BYTES)


def _make_dots(cast, precision, batched=False):
    lead = 1 if batched else 0
    batch = ((0,), (0,)) if batched else ((), ())

    def dg(a, b, ca, cb):
        if cast is not None:
            a = a.astype(cast)
            b = b.astype(cast)
        return lax.dot_general(a, b, (((ca + lead,), (cb + lead,)), batch), precision=precision, preferred_element_type=F32)

    def nn_(a, b):
        return dg(a, b, 1, 0)

    def nt_(a, b):
        return dg(a, b, 1, 1)

    def tn_(a, b):
        return dg(a, b, 0, 0)

    @jax.custom_vjp
    def nn(a, b):
        return nn_(a, b)

    nn.defvjp(lambda a, b: (nn_(a, b), (a, b)), lambda r, g: (nt_(g, r[1]), tn_(r[0], g)))

    @jax.custom_vjp
    def nt(a, b):
        return nt_(a, b)

    nt.defvjp(lambda a, b: (nt_(a, b), (a, b)), lambda r, g: (nn_(g, r[1]), tn_(g, r[0])))

    @jax.custom_vjp
    def tn(a, b):
        return tn_(a, b)

    tn.defvjp(lambda a, b: (tn_(a, b), (a, b)), lambda r, g: (nt_(r[1], g), nn_(r[0], g)))

    return (nn_, nt_, tn_), (nn, nt, tn)


_BDOT_PLAIN, _BDOT_VJP = _make_dots(BF16, None)
_BDOT_BATCH_PLAIN, _BDOT_BATCH_VJP = _make_dots(BF16, None, batched=True)
_FDOT_BATCH_PLAIN, _FDOT_BATCH_VJP = _make_dots(BF16, None, batched=True)


def _mm(a, b, *, ta=False, tb=False, out_dtype=F32, tm=None, tn=512, tk=None, epi=None, extra=None, add_scale=1.0,
        b_chunks=False, o_chunks=False, after=None, name):
    m, k = (a.shape[1], a.shape[0]) if ta else a.shape
    if b_chunks:
        n, kb = (b.shape[1], N_DEV * b.shape[2]) if tb else (N_DEV * b.shape[2], b.shape[1])
    else:
        n, kb = b.shape if tb else (b.shape[1], b.shape[0])
    assert kb == k, (name, a.shape, b.shape)
    tm, tn, tk = min(tm or m, m), min(tn, n), min(tk or k, k)
    assert m % tm == 0 and n % tn == 0 and k % tk == 0, (name, m, n, k)
    nk = k // tk
    dims = (((0 if ta else 1,), (1 if tb else 0,)), ((), ()))
    n_extra = 0 if epi in (None, "relu2") else 1
    n_out = 2 if epi == "relu2" else 1
    if epi in ("relu2", "mul2r"):
        out_dtype = BF16
    n_after = 0 if after is None else 1

    def body(*refs):
        a_ref, b_ref = refs[:2]
        c_ref = refs[2] if n_extra else None
        o_refs = refs[2 + n_extra + n_after:2 + n_extra + n_after + n_out]
        scr = refs[2 + n_extra + n_after + n_out:]
        r = lax.dot_general(a_ref[...].astype(BF16), b_ref[...].astype(BF16), dims, preferred_element_type=F32)

        def finish(v):
            if epi == "add":
                o_refs[0][...] = (v + add_scale * c_ref[...]).astype(out_dtype)
            elif epi == "relu2":
                p = jnp.maximum(v, 0.0)
                o_refs[0][...] = (p * p).astype(BF16)
                o_refs[1][...] = p.astype(BF16)
            elif epi == "mul2r":
                o_refs[0][...] = (v * (2.0 * c_ref[...].astype(F32))).astype(BF16)
            else:
                o_refs[0][...] = v.astype(out_dtype)

        if nk == 1:
            finish(r)
        else:
            acc = scr[0]
            kk = pl.program_id(2)

            @pl.when(kk == 0)
            def _():
                acc[...] = r

            @pl.when(kk > 0)
            def _():
                acc[...] += r

            @pl.when(kk == nk - 1)
            def _():
                finish(acc[...])

    a_spec = pl.BlockSpec((tk, tm), lambda i, j, kk: (kk, i)) if ta else pl.BlockSpec((tm, tk), lambda i, j, kk: (i, kk))
    if b_chunks and tb:
        kc = k // N_DEV // tk
        b_spec = pl.BlockSpec((None, tn, tk), lambda i, j, kk: (kk // kc, j, kk % kc))
    elif b_chunks:
        nc = n // N_DEV // tn
        b_spec = pl.BlockSpec((None, tk, tn), lambda i, j, kk: (j // nc, kk, j % nc))
    elif tb:
        b_spec = pl.BlockSpec((tn, tk), lambda i, j, kk: (j, kk))
    else:
        b_spec = pl.BlockSpec((tk, tn), lambda i, j, kk: (kk, j))
    mn_spec = pl.BlockSpec((tm, tn), lambda i, j, kk: (i, j))
    if o_chunks:
        oc = n // N_DEV // tn
        o_spec = pl.BlockSpec((None, tm, tn), lambda i, j, kk: (j // oc, i, j % oc))
        o_shape = jax.ShapeDtypeStruct((N_DEV, m, n // N_DEV), out_dtype)
    else:
        o_spec, o_shape = mn_spec, jax.ShapeDtypeStruct((m, n), out_dtype)
    res = pl.pallas_call(
        body, grid=(m // tm, n // tn, nk),
        in_specs=[a_spec, b_spec] + [mn_spec] * n_extra + [pl.BlockSpec(memory_space=pl.ANY)] * n_after,
        out_specs=[o_spec] * n_out, out_shape=[o_shape] * n_out,
        scratch_shapes=[pltpu.VMEM((tm, tn), F32)] if nk > 1 else [],
        compiler_params=_params("parallel", "parallel", "arbitrary"), name=name,
    )(a, b, *([extra] if n_extra else []), *([after] if n_after else []))
    return res if n_out > 1 else res[0]


def _cast_bf16(v, *, name, tm=512):
    t, d = v.shape
    tm = min(tm, t)

    def body(v_ref, o_ref):
        o_ref[...] = v_ref[...].astype(BF16)

    spec = pl.BlockSpec((tm, d), lambda i: (i, 0))
    return pl.pallas_call(body, grid=(t // tm,), in_specs=[spec], out_specs=spec,
                          out_shape=jax.ShapeDtypeStruct((t, d), BF16), compiler_params=_params("parallel"), name=name)(v)


def _shift_down(v, s):
    if s == 0:
        return v
    row = lax.broadcasted_iota(jnp.int32, v.shape, 0)
    return jnp.where(row >= s, pltpu.roll(v, s, axis=0), 0.0)


def _shift_up(v, s):
    if s == 0:
        return v
    t = v.shape[0]
    row = lax.broadcasted_iota(jnp.int32, v.shape, 0)
    return jnp.where(row < t - s, pltpu.roll(v, t - s, axis=0), 0.0)


def _post_col(j):
    return (j % GDN_HEADS) * 3 + j // GDN_HEADS


def _gdn_prep_fwd(proj, conv_w):
    t = proj.shape[0]

    def body(x_ref, w_ref, o_ref):
        j = pl.program_id(0)
        x = x_ref[...]
        y = jnp.zeros_like(x)
        for tap in range(CONV_K):
            y = y + w_ref[tap:tap + 1, :] * _shift_down(x, CONV_K - 1 - tap)
        c = y * jax.nn.sigmoid(y)
        nrm = c * lax.rsqrt(jnp.sum(c * c, axis=1, keepdims=True) + NORM_EPS)
        o_ref[...] = jnp.where(j < 2 * GDN_HEADS, nrm, c)

    return pl.pallas_call(
        body, grid=(QKV_COLS // LANE,),
        in_specs=[pl.BlockSpec((t, LANE), lambda j: (0, j)), pl.BlockSpec((CONV_K, LANE), lambda j: (0, j))],
        out_specs=pl.BlockSpec((t, LANE), lambda j: (0, _post_col(j))),
        out_shape=jax.ShapeDtypeStruct((t, QKV_COLS), F32),
        compiler_params=_params("parallel"), name="gdn_prep_fwd",
    )(proj, conv_w)


def _gdn_prep_bwd(proj, conv_w, dpost, dproj):
    t = proj.shape[0]

    def body(x_ref, w_ref, d_ref, _, dx_ref, dw_ref):
        j = pl.program_id(0)
        x = x_ref[...]
        xs = [_shift_down(x, CONV_K - 1 - tap) for tap in range(CONV_K)]
        y = jnp.zeros_like(x)
        for tap in range(CONV_K):
            y = y + w_ref[tap:tap + 1, :] * xs[tap]
        sig = jax.nn.sigmoid(y)
        c = y * sig
        r = lax.rsqrt(jnp.sum(c * c, axis=1, keepdims=True) + NORM_EPS)
        nrm = c * r
        d = d_ref[...]
        dc_norm = r * (d - nrm * jnp.sum(d * nrm, axis=1, keepdims=True))
        dc = jnp.where(j < 2 * GDN_HEADS, dc_norm, d)
        dy = dc * (sig * (1.0 + y * (1.0 - sig)))
        dx = jnp.zeros_like(x)
        for tap in range(CONV_K):
            dx = dx + _shift_up(w_ref[tap:tap + 1, :] * dy, CONV_K - 1 - tap)
            dw_ref[tap:tap + 1, :] = jnp.sum(dy * xs[tap], axis=0, keepdims=True)
        dx_ref[...] = dx.astype(dx_ref.dtype)

    return pl.pallas_call(
        body, grid=(QKV_COLS // LANE,),
        in_specs=[pl.BlockSpec((t, LANE), lambda j: (0, j)), pl.BlockSpec((CONV_K, LANE), lambda j: (0, j)),
                  pl.BlockSpec((t, LANE), lambda j: (0, _post_col(j))), pl.BlockSpec(memory_space=pl.ANY)],
        out_specs=[pl.BlockSpec((t, LANE), lambda j: (0, j)), pl.BlockSpec((CONV_K, LANE), lambda j: (0, j))],
        out_shape=[jax.ShapeDtypeStruct(dproj.shape, dproj.dtype), jax.ShapeDtypeStruct((CONV_K, QKV_COLS), F32)],
        input_output_aliases={3: 0},
        compiler_params=_params("parallel"), name="gdn_prep_bwd",
    )(proj, conv_w, dpost, dproj)


def _softplus(v):
    return jnp.maximum(v, 0.0) + jnp.log(1.0 + jnp.exp(-jnp.abs(v)))


def _tri_inv(low, nn):
    r = lax.broadcasted_iota(jnp.int32, (CHUNK, CHUNK), 0)
    c = lax.broadcasted_iota(jnp.int32, (CHUNK, CHUNK), 1)
    eye = (r == c).astype(F32)
    same_blk = lax.shift_right_logical(r, 4) == lax.shift_right_logical(c, 4)
    diag = jnp.where(same_blk, low, 0.0)
    off = low - diag
    n1 = -diag
    n2 = nn(n1, n1)
    n4 = nn(n2, n2)
    n8 = nn(n4, n4)
    inv_d = nn(nn(nn(eye + n1, eye + n2), eye + n4), eye + n8)
    m1 = nn(inv_d, off)
    m2 = nn(m1, m1)
    return nn(nn(eye - m1, eye + m2), inv_d)


@jax.custom_vjp
def _tri_inv_known(low, t_inv):
    return t_inv


def _tri_inv_known_fwd(low, t_inv):
    return t_inv, t_inv


def _tri_inv_known_bwd(t_inv, g):
    _, nt, tn = _FDOT_BATCH_PLAIN
    return -nt(tn(t_inv, g), t_inv), jnp.zeros_like(t_inv)


_tri_inv_known.defvjp(_tri_inv_known_fwd, _tri_inv_known_bwd)


LOCAL_HEADS_PER_STEP = 8


def _gdn_local_fn(qkv, ba, alog_row, dtb_row, first_head, bdots, fdots, t_known=None):
    nn, nt, tn = bdots
    fnn = fdots[0]
    n_heads = qkv.shape[1] // (3 * HEAD_DIM)
    part = lambda i, p: qkv[:, (3 * i + p) * HEAD_DIM:(3 * i + p + 1) * HEAD_DIM]
    q = jnp.stack([part(i, 0) for i in range(n_heads)]) * (HEAD_DIM ** -0.5)
    k = jnp.stack([part(i, 1) for i in range(n_heads)])
    v = jnp.stack([part(i, 2) for i in range(n_heads)])
    lane = lax.broadcasted_iota(jnp.int32, ba.shape, 1)
    bg = jnp.where(lane < GDN_HEADS, jax.nn.sigmoid(ba), -jnp.exp(alog_row) * _softplus(ba + dtb_row))
    pick = lambda l: jnp.sum(jnp.where(lane == l, bg, 0.0), axis=1, keepdims=True)
    beta = jnp.stack([pick(first_head + i) for i in range(n_heads)])
    g = jnp.stack([pick(first_head + i + GDN_HEADS) for i in range(n_heads)])

    r = lax.broadcasted_iota(jnp.int32, (CHUNK, CHUNK), 0)
    c = lax.broadcasted_iota(jnp.int32, (CHUNK, CHUNK), 1)
    incl = r >= c
    strict = r > c
    eye = r == c

    def to_row(col):
        return jnp.sum(jnp.where(eye, col, 0.0), axis=1, keepdims=True)

    gc = jnp.sum(jnp.where(incl, to_row(g), 0.0), axis=2, keepdims=True)
    diff = gc - to_row(gc)
    decay = jnp.where(incl, jnp.exp(jnp.where(incl, diff, 0.0)), 0.0)
    k_beta = k * beta
    v_beta = v * beta
    low = jnp.where(strict, nt(k_beta, k) * decay, 0.0)
    t_inv = _tri_inv(low, fnn) if t_known is None else _tri_inv_known(low, t_known)
    eg = jnp.exp(gc)
    u = fnn(t_inv, v_beta)
    w = fnn(t_inv, k_beta * eg)
    attn = jnp.where(incl, nt(q, k) * decay, 0.0)
    last = lax.broadcasted_iota(jnp.int32, (CHUNK, 1), 0) == CHUNK - 1
    g_last = jnp.sum(jnp.where(last, gc, 0.0), axis=1, keepdims=True)
    kdec = k * jnp.exp(g_last - gc)
    elast = jnp.broadcast_to(jnp.exp(g_last), (n_heads, 1, LANE))
    return u, w, q * eg, kdec, attn, elast, t_inv


def _gdn_state_fn(u, w, qg, kdec, attn, elast, state, bdots):
    nn, _, tn = bdots
    v_new = u - nn(w, state)
    o = nn(qg, state) + nn(attn, v_new)
    return o, state * elast + tn(kdec, v_new)


def _gdn_local_fwd(post, proj, alog_row, dtb_row):
    t = post.shape[0]
    n_chunks = t // CHUNK
    hb = LOCAL_HEADS_PER_STEP

    def body(qkv_ref, ba_ref, al_ref, dt_ref, u_ref, w_ref, qg_ref, kd_ref, at_ref, el_ref, ti_ref):
        u, w, qg, kdec, attn, elast, t_inv = _gdn_local_fn(qkv_ref[...], ba_ref[...], al_ref[...], dt_ref[...],
                                                           pl.program_id(1) * hb, _BDOT_BATCH_PLAIN, _FDOT_BATCH_PLAIN)
        for i in range(hb):
            cols = slice(i * HEAD_DIM, (i + 1) * HEAD_DIM)
            u_ref[:, cols] = u[i]
            w_ref[:, cols] = w[i].astype(BF16)
            qg_ref[:, cols] = qg[i].astype(BF16)
            kd_ref[:, cols] = kdec[i].astype(BF16)
        at_ref[...] = attn.astype(BF16)
        el_ref[:, 0] = elast
        ti_ref[...] = t_inv

    wide = pl.BlockSpec((CHUNK, hb * HEAD_DIM), lambda n, j: (n, j))
    square = pl.BlockSpec((hb, CHUNK, CHUNK), lambda n, j: (j, n, 0))
    row = pl.BlockSpec((1, LANE), lambda n, j: (0, 0))
    res = pl.pallas_call(
        body, grid=(n_chunks, GDN_HEADS // hb),
        in_specs=[pl.BlockSpec((CHUNK, hb * 3 * HEAD_DIM), lambda n, j: (n, j)),
                  pl.BlockSpec((CHUNK, LANE), lambda n, j: (n, BA_BLK)), row, row],
        out_specs=[wide, wide, wide, wide, square, pl.BlockSpec((hb, 1, 1, LANE), lambda n, j: (j, n, 0, 0)), square],
        out_shape=[jax.ShapeDtypeStruct((t, GDN_WIDTH), F32), jax.ShapeDtypeStruct((t, GDN_WIDTH), BF16),
                   jax.ShapeDtypeStruct((t, GDN_WIDTH), BF16), jax.ShapeDtypeStruct((t, GDN_WIDTH), BF16),
                   jax.ShapeDtypeStruct((GDN_HEADS, t, CHUNK), BF16),
                   jax.ShapeDtypeStruct((GDN_HEADS, n_chunks, 1, LANE), F32),
                   jax.ShapeDtypeStruct((GDN_HEADS, t, CHUNK), F32)],
        compiler_params=_params("parallel", "parallel"), name="gdn_local_fwd",
    )(post, proj, alog_row, dtb_row)
    return tuple(res[:6]), res[6]


def _by_head(ref):
    return jnp.stack([ref[:, h * HEAD_DIM:(h + 1) * HEAD_DIM] for h in range(ref.shape[1] // HEAD_DIM)])


def _gdn_state_specs(n_of):
    wide = pl.BlockSpec((CHUNK, GDN_WIDTH), lambda n: (n_of(n), 0))
    attn = pl.BlockSpec((GDN_HEADS, CHUNK, CHUNK), lambda n: (0, n_of(n), 0))
    elast = pl.BlockSpec((GDN_HEADS, 1, 1, LANE), lambda n: (0, n_of(n), 0, 0))
    saved = pl.BlockSpec((GDN_HEADS, 1, HEAD_DIM, HEAD_DIM), lambda n: (0, n_of(n), 0, 0))
    return wide, attn, elast, saved


def _gdn_state_fwd(u, w, qg, kdec, attn, elast):
    t = u.shape[0]
    n_chunks = t // CHUNK

    def body(u_ref, w_ref, qg_ref, kd_ref, at_ref, el_ref, o_ref, save_ref, state_ref):
        @pl.when(pl.program_id(0) == 0)
        def _():
            state_ref[...] = jnp.zeros_like(state_ref)

        state = state_ref[...]
        save_ref[:, 0] = state
        o, new_state = _gdn_state_fn(_by_head(u_ref), _by_head(w_ref), _by_head(qg_ref), _by_head(kd_ref), at_ref[...],
                                     el_ref[:, 0], state, _BDOT_BATCH_PLAIN)
        for h in range(GDN_HEADS):
            o_ref[:, h * HEAD_DIM:(h + 1) * HEAD_DIM] = o[h]
        state_ref[...] = new_state

    wide, attn_spec, elast_spec, saved_spec = _gdn_state_specs(lambda n: n)
    return pl.pallas_call(
        body, grid=(n_chunks,), in_specs=[wide, wide, wide, wide, attn_spec, elast_spec],
        out_specs=[wide, saved_spec],
        out_shape=[jax.ShapeDtypeStruct((t, GDN_WIDTH), F32),
                   jax.ShapeDtypeStruct((GDN_HEADS, n_chunks, HEAD_DIM, HEAD_DIM), F32)],
        scratch_shapes=[pltpu.VMEM((GDN_HEADS, HEAD_DIM, HEAD_DIM), F32)],
        compiler_params=_params("arbitrary"), name="gdn_state_fwd",
    )(u, w, qg, kdec, attn, elast)


def _gdn_state_bwd(u, w, qg, kdec, attn, elast, saved, do):
    t = u.shape[0]
    n_chunks = t // CHUNK
    last = n_chunks - 1

    def body(u_ref, w_ref, qg_ref, kd_ref, at_ref, el_ref, save_ref, do_ref,
             du_ref, dw_ref, dqg_ref, dkd_ref, dat_ref, del_ref, dstate_ref):
        @pl.when(pl.program_id(0) == 0)
        def _():
            dstate_ref[...] = jnp.zeros_like(dstate_ref)

        _, vjp = jax.vjp(
            lambda *a: _gdn_state_fn(*a, _BDOT_BATCH_VJP), _by_head(u_ref), _by_head(w_ref).astype(F32),
            _by_head(qg_ref).astype(F32), _by_head(kd_ref).astype(F32), at_ref[...].astype(F32), el_ref[:, 0],
            save_ref[:, 0])
        du, dw, dqg, dkd, dat, de, dstate = vjp((_by_head(do_ref), dstate_ref[...]))
        for h in range(GDN_HEADS):
            cols = slice(h * HEAD_DIM, (h + 1) * HEAD_DIM)
            du_ref[:, cols] = du[h]
            dw_ref[:, cols] = dw[h]
            dqg_ref[:, cols] = dqg[h]
            dkd_ref[:, cols] = dkd[h]
        dat_ref[...] = dat
        del_ref[:, 0] = de
        dstate_ref[...] = dstate

    wide, attn_spec, elast_spec, saved_spec = _gdn_state_specs(lambda n: last - n)
    wide_f32 = jax.ShapeDtypeStruct((t, GDN_WIDTH), F32)
    return pl.pallas_call(
        body, grid=(n_chunks,), in_specs=[wide, wide, wide, wide, attn_spec, elast_spec, saved_spec, wide],
        out_specs=[wide, wide, wide, wide, attn_spec, elast_spec],
        out_shape=[wide_f32, wide_f32, wide_f32, wide_f32, jax.ShapeDtypeStruct((GDN_HEADS, t, CHUNK), F32),
                   jax.ShapeDtypeStruct((GDN_HEADS, n_chunks, 1, LANE), F32)],
        scratch_shapes=[pltpu.VMEM((GDN_HEADS, HEAD_DIM, HEAD_DIM), F32)],
        compiler_params=_params("arbitrary"), name="gdn_state_bwd",
    )(u, w, qg, kdec, attn, elast, saved, do)


def _gdn_local_bwd(post, proj, alog_row, dtb_row, t_inv, cots, dproj):
    t = post.shape[0]
    n_chunks = t // CHUNK
    hb = LOCAL_HEADS_PER_STEP
    n_steps = GDN_HEADS // hb

    def body(qkv_ref, ba_ref, al_ref, dt_ref, ti_ref, du_ref, dw_ref, dqg_ref, dkd_ref, dat_ref, del_ref, _,
             dqkv_ref, dba_ref, dal_ref, ddt_ref, dba_acc):
        n = pl.program_id(0)
        j = pl.program_id(1)

        @pl.when((n == 0) & (j == 0))
        def _():
            dal_ref[...] = jnp.zeros_like(dal_ref)
            ddt_ref[...] = jnp.zeros_like(ddt_ref)

        @pl.when(j == 0)
        def _():
            dba_acc[...] = jnp.zeros_like(dba_acc)

        t_known = ti_ref[...]
        _, vjp = jax.vjp(
            lambda a, b, c, d: _gdn_local_fn(a, b, c, d, j * hb, _BDOT_BATCH_VJP, _FDOT_BATCH_VJP, t_known)[:6],
            qkv_ref[...], ba_ref[...], al_ref[...], dt_ref[...])
        dqkv, dba, dal, ddt = vjp((_by_head(du_ref), _by_head(dw_ref), _by_head(dqg_ref), _by_head(dkd_ref), dat_ref[...],
                                   del_ref[:, 0]))
        dqkv_ref[...] = dqkv
        dba_acc[...] += dba
        dal_ref[...] += dal
        ddt_ref[...] += ddt

        @pl.when(j == n_steps - 1)
        def _():
            dba_ref[:, 0:LANE] = dba_acc[...].astype(dba_ref.dtype)
            dba_ref[:, LANE:2 * LANE] = jnp.zeros((CHUNK, LANE), dba_ref.dtype)

    wide = pl.BlockSpec((CHUNK, hb * HEAD_DIM), lambda n, j: (n, j))
    qkv_spec = pl.BlockSpec((CHUNK, hb * 3 * HEAD_DIM), lambda n, j: (n, j))
    row = pl.BlockSpec((1, LANE), lambda n, j: (0, 0))
    return pl.pallas_call(
        body, grid=(n_chunks, n_steps),
        in_specs=[qkv_spec, pl.BlockSpec((CHUNK, LANE), lambda n, j: (n, BA_BLK)), row, row,
                  pl.BlockSpec((hb, CHUNK, CHUNK), lambda n, j: (j, n, 0)), wide, wide, wide, wide,
                  pl.BlockSpec((hb, CHUNK, CHUNK), lambda n, j: (j, n, 0)),
                  pl.BlockSpec((hb, 1, 1, LANE), lambda n, j: (j, n, 0, 0)), pl.BlockSpec(memory_space=pl.ANY)],
        out_specs=[qkv_spec, pl.BlockSpec((CHUNK, 2 * LANE), lambda n, j: (n, BA_BLK // 2)), row, row],
        out_shape=[jax.ShapeDtypeStruct((t, QKV_COLS), F32), jax.ShapeDtypeStruct(dproj.shape, dproj.dtype),
                   jax.ShapeDtypeStruct((1, LANE), F32), jax.ShapeDtypeStruct((1, LANE), F32)],
        input_output_aliases={11: 1},
        scratch_shapes=[pltpu.VMEM((CHUNK, LANE), F32)],
        compiler_params=_params("arbitrary", "arbitrary"), name="gdn_local_bwd",
    )(post, proj, alog_row, dtb_row, t_inv, *cots, dproj)


def _onorm_fn(o, z, w):
    return o * lax.rsqrt(jnp.mean(o * o, axis=1, keepdims=True) + NORM_EPS) * w * (z * jax.nn.sigmoid(z))


_Z_WIDE_BLK = Z_OFF // GDN_WIDTH


def _onorm_fwd(o_raw, proj, norm_w, mixin, tm=256):
    t = o_raw.shape[0]
    tm = min(tm, t)

    def body(o_ref, z_ref, w_ref, _, out_ref):
        for h in range(GDN_HEADS):
            cols = slice(h * HEAD_DIM, (h + 1) * HEAD_DIM)
            out_ref[:, cols] = _onorm_fn(o_ref[:, cols], z_ref[:, cols], w_ref[...]).astype(out_ref.dtype)

    wide = pl.BlockSpec((tm, GDN_WIDTH), lambda i: (i, 0))
    return pl.pallas_call(
        body, grid=(t // tm,),
        in_specs=[wide, pl.BlockSpec((tm, GDN_WIDTH), lambda i: (i, _Z_WIDE_BLK)), pl.BlockSpec((1, LANE), lambda i: (0, 0)),
                  pl.BlockSpec(memory_space=pl.ANY)],
        out_specs=wide, out_shape=jax.ShapeDtypeStruct(mixin.shape, mixin.dtype), input_output_aliases={3: 0},
        compiler_params=_params("parallel"), name="gdn_onorm_fwd",
    )(o_raw, proj, norm_w, mixin)


def _onorm_bwd(o_raw, proj, norm_w, dmixin, dproj, tm=256):
    t = o_raw.shape[0]
    tm = min(tm, t)

    def body(o_ref, z_ref, w_ref, d_ref, _, do_ref, dz_ref, dw_ref):
        @pl.when(pl.program_id(0) == 0)
        def _():
            dw_ref[...] = jnp.zeros_like(dw_ref)

        for h in range(GDN_HEADS):
            cols = slice(h * HEAD_DIM, (h + 1) * HEAD_DIM)
            _, vjp = jax.vjp(_onorm_fn, o_ref[:, cols], z_ref[:, cols], w_ref[...])
            do, dz, dw = vjp(d_ref[:, cols])
            do_ref[:, cols] = do
            dz_ref[:, cols] = dz.astype(dz_ref.dtype)
            dw_ref[...] += dw

    wide = pl.BlockSpec((tm, GDN_WIDTH), lambda i: (i, 0))
    gate = pl.BlockSpec((tm, GDN_WIDTH), lambda i: (i, _Z_WIDE_BLK))
    row = pl.BlockSpec((1, LANE), lambda i: (0, 0))
    return pl.pallas_call(
        body, grid=(t // tm,), in_specs=[wide, gate, row, wide, pl.BlockSpec(memory_space=pl.ANY)],
        out_specs=[wide, gate, row],
        out_shape=[jax.ShapeDtypeStruct((t, GDN_WIDTH), F32), jax.ShapeDtypeStruct(dproj.shape, dproj.dtype),
                   jax.ShapeDtypeStruct((1, LANE), F32)],
        input_output_aliases={4: 1},
        compiler_params=_params("arbitrary"), name="gdn_onorm_bwd",
    )(o_raw, proj, norm_w, dmixin, dproj)


def _pool_select(levels, gi):
    out = levels[-1]
    for lvl in range(len(levels) - 2, -1, -1):
        out = jnp.where(gi == lvl, levels[lvl], out)
    return out


def _pool_count(shape, gi):
    pos = lax.broadcasted_iota(jnp.int32, shape, 0)
    win = lax.shift_left(jnp.int32(2), gi)
    return jnp.minimum(pos + 1, win).astype(F32)


def _pooled(p, gi):
    acc = p
    levels = []
    for lvl in range(POOL_GROUPS):
        acc = acc + _shift_down(acc, 1 << lvl)
        levels.append(acc)
    return _pool_select(levels, gi) / _pool_count(p.shape, gi) - p


def _pool_fwd(proj, pool_w, pool_scale):
    t = proj.shape[0]

    def body(p_ref, w_ref, s_ref, out_ref):
        gi = pl.program_id(0)
        pooled = _pooled(p_ref[...], gi)
        out_ref[...] = (_BDOT_PLAIN[0](pooled, w_ref[0]) * s_ref[0]).astype(out_ref.dtype)

    return pl.pallas_call(
        body, grid=(POOL_GROUPS,),
        in_specs=[pl.BlockSpec((t, POOL_GROUP_DIM), lambda g: (0, POOL_BLK + g)),
                  pl.BlockSpec((1, POOL_GROUP_DIM, POOL_GROUP_DIM), lambda g: (g, 0, 0)),
                  pl.BlockSpec((1, 1, POOL_GROUP_DIM), lambda g: (g, 0, 0))],
        out_specs=pl.BlockSpec((t, POOL_GROUP_DIM), lambda g: (0, GDN_WIDTH // POOL_GROUP_DIM + g)),
        out_shape=jax.ShapeDtypeStruct((t, 2 * GDN_WIDTH), BF16),
        compiler_params=_params("parallel"), name="pool_fwd",
    )(proj, pool_w, pool_scale)


def _pool_bwd(proj, pool_w, pool_scale, dmixin):
    t = proj.shape[0]
    nn, nt, tn = _BDOT_PLAIN

    def body(p_ref, w_ref, s_ref, d_ref, dp_ref, dw_ref, ds_ref):
        gi = pl.program_id(0)
        p = p_ref[...]
        pooled = _pooled(p, gi)
        mixed = nn(pooled, w_ref[0])
        d = d_ref[...]
        ds_ref[0] = jnp.sum(d * mixed, axis=0, keepdims=True)
        dmixed = d * s_ref[0]
        dw_ref[0] = tn(pooled, dmixed)
        dpooled = nt(dmixed, w_ref[0])
        acc = dpooled / _pool_count(p.shape, gi)
        levels = []
        for lvl in range(POOL_GROUPS):
            acc = acc + _shift_up(acc, 1 << lvl)
            levels.append(acc)
        dp_ref[...] = (_pool_select(levels, gi) - dpooled).astype(dp_ref.dtype)

    return pl.pallas_call(
        body, grid=(POOL_GROUPS,),
        in_specs=[pl.BlockSpec((t, POOL_GROUP_DIM), lambda g: (0, POOL_BLK + g)),
                  pl.BlockSpec((1, POOL_GROUP_DIM, POOL_GROUP_DIM), lambda g: (g, 0, 0)),
                  pl.BlockSpec((1, 1, POOL_GROUP_DIM), lambda g: (g, 0, 0)),
                  pl.BlockSpec((t, POOL_GROUP_DIM), lambda g: (0, GDN_WIDTH // POOL_GROUP_DIM + g))],
        out_specs=[pl.BlockSpec((t, POOL_GROUP_DIM), lambda g: (0, POOL_BLK + g)),
                   pl.BlockSpec((1, POOL_GROUP_DIM, POOL_GROUP_DIM), lambda g: (g, 0, 0)),
                   pl.BlockSpec((1, 1, POOL_GROUP_DIM), lambda g: (g, 0, 0))],
        out_shape=[jax.ShapeDtypeStruct((t, PROJ_COLS), BF16),
                   jax.ShapeDtypeStruct((POOL_GROUPS, POOL_GROUP_DIM, POOL_GROUP_DIM), F32),
                   jax.ShapeDtypeStruct((POOL_GROUPS, 1, POOL_GROUP_DIM), F32)],
        compiler_params=_params("parallel"), name="pool_bwd",
    )(proj, pool_w, pool_scale, dmixin)


def _ln_stats(s):
    mu = jnp.mean(s, axis=1, keepdims=True)
    xc = s - mu
    var = jnp.mean(xc * xc, axis=1, keepdims=True)
    rstd = lax.rsqrt(var + LN_EPS)
    return xc * rstd, rstd


def _ln_fwd(h_in, y, g, b, *, name, tm=256):
    t, d = h_in.shape
    tm = min(tm, t)

    def body(h_ref, y_ref, g_ref, b_ref, o_ref, o16_ref):
        xhat, _ = _ln_stats(ALPHA * h_ref[...] + y_ref[...])
        out = xhat * g_ref[...] + b_ref[...]
        o_ref[...] = out
        o16_ref[...] = out.astype(BF16)

    row = pl.BlockSpec((tm, d), lambda i: (i, 0))
    vec = pl.BlockSpec((1, d), lambda i: (0, 0))
    return pl.pallas_call(
        body, grid=(t // tm,), in_specs=[row, row, vec, vec], out_specs=[row, row],
        out_shape=[jax.ShapeDtypeStruct((t, d), F32), jax.ShapeDtypeStruct((t, d), BF16)],
        compiler_params=_params("parallel"), name=name,
    )(h_in, y, g, b)


def _ln_backward(xhat, rstd, dout, gain):
    dxhat = dout * gain
    m1 = jnp.mean(dxhat, axis=1, keepdims=True)
    m2 = jnp.mean(dxhat * xhat, axis=1, keepdims=True)
    return (rstd * (dxhat - m1 - xhat * m2), jnp.sum(dout * xhat, axis=0, keepdims=True),
            jnp.sum(dout, axis=0, keepdims=True))


def _ln_loss(h_in, y, g, b, target, *, name, tm=256):
    t, d = h_in.shape
    tm = min(tm, t)

    def body(h_ref, y_ref, g_ref, b_ref, t_ref, sq_ref, ds_ref, ds16_ref, dg_ref, dbias_ref):
        @pl.when(pl.program_id(0) == 0)
        def _():
            sq_ref[...] = jnp.zeros_like(sq_ref)
            dg_ref[...] = jnp.zeros_like(dg_ref)
            dbias_ref[...] = jnp.zeros_like(dbias_ref)

        xhat, rstd = _ln_stats(ALPHA * h_ref[...] + y_ref[...])
        err = xhat * g_ref[...] + b_ref[...] - t_ref[...]
        sq_ref[...] += jnp.sum(jnp.sum(err * err, axis=1, keepdims=True), axis=0, keepdims=True)
        ds, dg, dbias = _ln_backward(xhat, rstd, err * (1.0 / d), g_ref[...])
        ds_ref[...] = ds
        ds16_ref[...] = ds.astype(BF16)
        dg_ref[...] += dg
        dbias_ref[...] += dbias

    row = pl.BlockSpec((tm, d), lambda i: (i, 0))
    vec = pl.BlockSpec((1, d), lambda i: (0, 0))
    return pl.pallas_call(
        body, grid=(t // tm,), in_specs=[row, row, vec, vec, row],
        out_specs=[pl.BlockSpec((1, LANE), lambda i: (0, 0)), row, row, vec, vec],
        out_shape=[jax.ShapeDtypeStruct((1, LANE), F32), jax.ShapeDtypeStruct((t, d), F32),
                   jax.ShapeDtypeStruct((t, d), BF16), jax.ShapeDtypeStruct((1, d), F32), jax.ShapeDtypeStruct((1, d), F32)],
        compiler_params=_params("arbitrary"), name=name,
    )(h_in, y, g, b, target)


def _ln_bwd(h_in, y, g, d_a, d_b, *, name, tm=256):
    t, d = h_in.shape
    tm = min(tm, t)
    has_b = d_b is not None

    def body(*refs):
        if has_b:
            h_ref, y_ref, g_ref, da_ref, db_ref, ds_ref, ds16_ref, dg_ref, dbias_ref = refs
        else:
            h_ref, y_ref, g_ref, da_ref, ds_ref, ds16_ref, dg_ref, dbias_ref = refs

        @pl.when(pl.program_id(0) == 0)
        def _():
            dg_ref[...] = jnp.zeros_like(dg_ref)
            dbias_ref[...] = jnp.zeros_like(dbias_ref)

        xhat, rstd = _ln_stats(ALPHA * h_ref[...] + y_ref[...])
        dout = da_ref[...]
        if has_b:
            dout = dout + ALPHA * db_ref[...]
        ds, dg, dbias = _ln_backward(xhat, rstd, dout, g_ref[...])
        ds_ref[...] = ds
        ds16_ref[...] = ds.astype(BF16)
        dg_ref[...] += dg
        dbias_ref[...] += dbias

    row = pl.BlockSpec((tm, d), lambda i: (i, 0))
    vec = pl.BlockSpec((1, d), lambda i: (0, 0))
    args = [h_in, y, g, d_a] + ([d_b] if has_b else [])
    return pl.pallas_call(
        body, grid=(t // tm,), in_specs=[row, row, vec, row] + ([row] if has_b else []),
        out_specs=[row, row, vec, vec],
        out_shape=[jax.ShapeDtypeStruct((t, d), F32), jax.ShapeDtypeStruct((t, d), BF16),
                   jax.ShapeDtypeStruct((1, d), F32), jax.ShapeDtypeStruct((1, d), F32)],
        compiler_params=_params("arbitrary"), name=name,
    )(*args)


def _attn_fn(q, k, v, dots):
    nn, nt, _ = dots
    s = nt(q, k) * (XATTN_HEAD_DIM ** -0.5)
    s = s - lax.stop_gradient(jnp.max(s, axis=1, keepdims=True))
    e = jnp.exp(s)
    p = e / jnp.sum(e, axis=1, keepdims=True)
    return nn(p, v)


def _attn_fwd(q, k, v, tq=512):
    t = q.shape[0]
    tq = min(tq, t)

    def body(q_ref, k_ref, v_ref, o_ref):
        o_ref[...] = _attn_fn(q_ref[...], k_ref[...], v_ref[...], _BDOT_PLAIN).astype(BF16)

    qs = pl.BlockSpec((tq, XATTN_HEAD_DIM), lambda h, i: (i, h))
    ks = pl.BlockSpec((MEM_LEN, XATTN_HEAD_DIM), lambda h, i: (0, h))
    return pl.pallas_call(
        body, grid=(XATTN_HEADS, t // tq), in_specs=[qs, ks, ks], out_specs=qs,
        out_shape=jax.ShapeDtypeStruct(q.shape, BF16), compiler_params=_params("parallel", "parallel"), name="xattn_fwd",
    )(q, k, v)


def _attn_bwd(q, k, v, do, tq=512):
    t = q.shape[0]
    tq = min(tq, t)

    def body(q_ref, k_ref, v_ref, do_ref, dq_ref, dk_ref, dv_ref):
        @pl.when(pl.program_id(1) == 0)
        def _():
            dk_ref[...] = jnp.zeros_like(dk_ref)
            dv_ref[...] = jnp.zeros_like(dv_ref)

        _, vjp = jax.vjp(lambda a, b, c: _attn_fn(a, b, c, _BDOT_VJP), q_ref[...].astype(F32), k_ref[...].astype(F32),
                         v_ref[...].astype(F32))
        dq, dk, dv = vjp(do_ref[...].astype(F32))
        dq_ref[...] = dq.astype(BF16)
        dk_ref[...] += dk
        dv_ref[...] += dv

    qs = pl.BlockSpec((tq, XATTN_HEAD_DIM), lambda h, i: (i, h))
    ks = pl.BlockSpec((MEM_LEN, XATTN_HEAD_DIM), lambda h, i: (0, h))
    return pl.pallas_call(
        body, grid=(XATTN_HEADS, t // tq), in_specs=[qs, ks, ks, qs], out_specs=[qs, ks, ks],
        out_shape=[jax.ShapeDtypeStruct(q.shape, BF16), jax.ShapeDtypeStruct(k.shape, F32), jax.ShapeDtypeStruct(v.shape, F32)],
        compiler_params=_params("parallel", "arbitrary"), name="xattn_bwd",
    )(q, k, v, do)


def _local_step(x, x16, mem, target, weights_of, grads_ready):
    def behind(vec, token):
        return vec if token is None else vec + token

    w = dict(weights_of("mixer", None))
    proj = _mm(x16, w["w_in"], tb=True, tn=768, name="mm_in_proj")
    mixin = _pool_fwd(proj, w["pool_w"], w["pool_scale"])
    post = _gdn_prep_fwd(proj, w["conv_w"])
    token = weights_of("ahead_conv", post)
    chunked, t_inv = _gdn_local_fwd(post, proj, behind(w["alog_row"], token), w["dtb_row"])
    o_raw, saved = _gdn_state_fwd(*chunked)
    token = weights_of("ahead_scan", o_raw)
    mixin = _onorm_fwd(o_raw, proj, behind(w["gdn_norm_w"], token), mixin)
    w.update(weights_of("attn", mixin))
    mix = _mm(mixin, w["w_out"], name="mm_out_proj")
    h1, h1_16 = _ln_fwd(x, mix, w["ln1_g"], w["ln1_b"], name="ln1_fwd")
    xq = _mm(h1_16, w["xq_w"], out_dtype=BF16, name="mm_xq")
    xk = _mm(mem, w["xk_w"], out_dtype=BF16, name="mm_xk")
    xv = _mm(mem, w["xv_w"], out_dtype=BF16, name="mm_xv")
    xo = _attn_fwd(xq, xk, xv)
    token = weights_of("ahead_attn", xo)
    if token is not None:
        xo, _ = lax.optimization_barrier((xo, token))
    xa = _mm(xo, w["xo_w"], name="mm_xo")
    h2, h2_16 = _ln_fwd(h1, xa, w["ln2_g"], w["ln2_b"], name="ln2_fwd")
    w.update(weights_of("up", h2_16))
    act, relu = _mm(h2_16, w["w_up"], b_chunks=True, epi="relu2", name="mm_up")
    w.update(weights_of("down", act))
    ff = _mm(act, w["w_down"], tn=512, tk=2048, name="mm_down")
    g = {}
    sq, ds3, ds3_16, g["ln3_g"], g["ln3_b"] = _ln_loss(h2, ff, w["ln3_g"], w["ln3_b"], target, name="ln3_loss")

    gw_down = _mm(act, ds3_16, ta=True, out_dtype=BF16, tm=512, tn=D_MODEL, name="mm_gw_down")
    du = _mm(ds3_16, w["w_down"], tb=True, epi="mul2r", extra=relu, name="mm_du")
    gw_up = _mm(h2_16, du, ta=True, out_dtype=BF16, o_chunks=True, name="mm_gw_up")
    token = grads_ready("mlp", {"w_down": gw_down, "w_up": gw_up})
    dh2 = _mm(du, w["w_up"], tb=True, b_chunks=True, tn=1024, tk=1024, name="mm_dh2")
    ds2, ds2_16, g["ln2_g"], g["ln2_b"] = _ln_bwd(h1, xa, behind(w["ln2_g"], token), dh2, ds3, name="ln2_bwd")
    gw_xo = _mm(xo, ds2_16, ta=True, out_dtype=BF16, name="mm_gw_xo")
    dxo = _mm(ds2_16, w["xo_w"], tb=True, out_dtype=BF16, name="mm_dxo")
    dxq, dxk, dxv = _attn_bwd(xq, xk, xv, dxo)
    gw_xq = _mm(h1_16, dxq, ta=True, out_dtype=BF16, name="mm_gw_xq")
    gw_xk = _mm(mem, dxk, ta=True, out_dtype=BF16, name="mm_gw_xk")
    gw_xv = _mm(mem, dxv, ta=True, out_dtype=BF16, name="mm_gw_xv")
    token = grads_ready("attn", {"xo_w": gw_xo, "xq_w": gw_xq, "xk_w": gw_xk, "xv_w": gw_xv})
    dh1 = _mm(dxq, w["xq_w"], tb=True, name="mm_dh1")
    ds1, ds1_16, g["ln1_g"], g["ln1_b"] = _ln_bwd(x, mix, behind(w["ln1_g"], token), dh1, ds2, name="ln1_bwd")
    gw_out = _mm(mixin, ds1_16, ta=True, out_dtype=BF16, name="mm_gw_out")
    dmixin = _mm(ds1_16, w["w_out"], tb=True, name="mm_dmixin")
    dproj, gw_pool, g["pool_scale"] = _pool_bwd(proj, w["pool_w"], w["pool_scale"], dmixin)
    token = grads_ready("mix", {"w_out": gw_out, "pool_w": gw_pool})
    do_raw, dproj, g["gdn_norm_w"] = _onorm_bwd(o_raw, proj, behind(w["gdn_norm_w"], token), dmixin, dproj)
    cots = _gdn_state_bwd(*chunked, saved, do_raw)
    token = grads_ready("tick", {"after": cots[0]})
    dpost, dproj, g["alog_row"], g["dtb_row"] = _gdn_local_bwd(post, proj, behind(w["alog_row"], token), w["dtb_row"],
                                                               t_inv, cots, dproj)
    dproj, g["conv_w"] = _gdn_prep_bwd(proj, w["conv_w"], dpost, dproj)
    token = grads_ready("small", {**g, "sq": sq})
    gw_in = _mm(dproj, x16, ta=True, out_dtype=BF16, tm=768, tn=D_MODEL, after=token, name="mm_gw_in")
    token = grads_ready("in", {"w_in": gw_in})
    grad_x = _mm(dproj, w["w_in"], tk=1792, epi="add", extra=ds1, add_scale=ALPHA, after=token, name="mm_dx")
    return sq, grad_x, g


_VECTORS = ("a_log", "dt_bias", "gdn_norm_w", "pool_scale", "ln1_g", "ln1_b", "ln2_g", "ln2_b", "ln3_g", "ln3_b")
_BA_SPLIT = BA_OFF + 2 * GDN_HEADS


def _lane_row(v, offset):
    return jnp.zeros((1, LANE), F32).at[0, offset:offset + v.shape[0]].set(v)


_GROUP_VECTORS = {"mixer": (), "attn": ("ln1_g", "ln1_b", "ln2_g", "ln2_b"), "up": (), "down": ("ln3_g", "ln3_b")}


def _group_weights(group, full):
    w = {n: full[n].reshape(1, D_MODEL) for n in _GROUP_VECTORS[group]}
    if group == "mixer":
        w.update({
            "w_in": _w_in_padded(full["w_in"]),
            "conv_w": full["conv_w"],
            "alog_row": _lane_row(full["a_log"], GDN_HEADS),
            "dtb_row": _lane_row(full["dt_bias"], GDN_HEADS),
            "gdn_norm_w": full["gdn_norm_w"].reshape(1, LANE),
            "pool_w": full["pool_w"],
            "pool_scale": full["pool_scale"].reshape(POOL_GROUPS, 1, POOL_GROUP_DIM),
        })
    else:
        w.update({n: full[n] for n in dict(_GATHER_GROUPS)[group]})
    return w


def _w_in_row_map():
    per = IN_COLS // N_DEV
    gap = POOL_OFF - _BA_SPLIT
    pieces = []
    for d in range(N_DEV):
        lo, hi = d * per, (d + 1) * per
        if hi <= _BA_SPLIT:
            pieces.append([(0, lo, per)])
        elif lo >= _BA_SPLIT:
            pieces.append([(0, lo + gap, per)])
        else:
            pieces.append([(0, lo, _BA_SPLIT - lo), (_BA_SPLIT - lo, POOL_OFF, hi - _BA_SPLIT)])
    return pieces


_W_IN_LANES = 256


def _w_in_padded(blocks):
    def body(b_ref, o_ref):
        for d, pieces in enumerate(_w_in_row_map()):
            for src, dst, rows in pieces:
                o_ref[dst:dst + rows, :] = b_ref[d, src:src + rows, :]
        o_ref[_BA_SPLIT:POOL_OFF, :] = jnp.zeros((POOL_OFF - _BA_SPLIT, _W_IN_LANES), o_ref.dtype)

    n, per, cols = blocks.shape
    return pl.pallas_call(
        body, grid=(cols // _W_IN_LANES,), in_specs=[pl.BlockSpec((n, per, _W_IN_LANES), lambda j: (0, 0, j))],
        out_specs=pl.BlockSpec((PROJ_COLS, _W_IN_LANES), lambda j: (0, j)),
        out_shape=jax.ShapeDtypeStruct((PROJ_COLS, cols), blocks.dtype), compiler_params=_params("parallel"),
        name="w_in_padded")(blocks)


def _w_in_chunks(g):
    def body(g_ref, o_ref):
        for d, pieces in enumerate(_w_in_row_map()):
            for dst, src, rows in pieces:
                o_ref[d, dst:dst + rows, :] = g_ref[src:src + rows, :]

    cols = g.shape[1]
    per = IN_COLS // N_DEV
    return pl.pallas_call(
        body, grid=(cols // _W_IN_LANES,), in_specs=[pl.BlockSpec((PROJ_COLS, _W_IN_LANES), lambda j: (0, j))],
        out_specs=pl.BlockSpec((N_DEV, per, _W_IN_LANES), lambda j: (0, 0, j)),
        out_shape=jax.ShapeDtypeStruct((N_DEV, per, cols), g.dtype), compiler_params=_params("parallel"),
        name="w_in_chunks")(g)


def _finish_small_grads(g):
    out = {"conv_w": g["conv_w"]}
    out["a_log"] = g["alog_row"][0, GDN_HEADS:2 * GDN_HEADS]
    out["dt_bias"] = g["dtb_row"][0, GDN_HEADS:2 * GDN_HEADS]
    out["gdn_norm_w"] = g["gdn_norm_w"].reshape(LANE)
    out["pool_scale"] = g["pool_scale"].reshape(POOL_GROUPS * POOL_GROUP_DIM)
    for n in ("ln1_g", "ln1_b", "ln2_g", "ln2_b", "ln3_g", "ln3_b"):
        out[n] = g[n].reshape(D_MODEL)
    return out


def _adamw_math(w, g, m, v):
    m = ADAM_B1 * m + (1.0 - ADAM_B1) * g
    v = ADAM_B2 * v + (1.0 - ADAM_B2) * (g * g)
    m_hat = m / (1.0 - ADAM_B1 ** ADAM_STEP)
    v_hat = v / (1.0 - ADAM_B2 ** ADAM_STEP)
    delta = -ADAM_LR * (m_hat / (jnp.sqrt(v_hat) + ADAM_EPS) + ADAM_WD * w)
    return delta, m, v


ADAMW_TILE_ELEMS = 256 * 1024
CHIP_SUM_TILE_ELEMS = 1024 * 1024


def _shard_tile(r, c, elems):
    for rows in (1024, 512, 256, 128):
        if r % rows == 0 and rows * c <= elems:
            return rows, c
    if r % 128 == 0:
        return 128, c
    return r, 256 if c % 256 == 0 else c


def _adamw_shard(parts, own, me, w, m, v, *, name):
    s, r, c = parts.shape
    tr, tc = _shard_tile(r, c, ADAMW_TILE_ELEMS)
    assert r % tr == 0 and c % tc == 0, (name, r, c)
    unit_axis = w.ndim == 3
    at = (slice(None), 0, slice(None)) if unit_axis else Ellipsis

    def body(me_ref, p_ref, own_ref, w_ref, m_ref, v_ref, g_ref, d_ref, nm_ref, nv_ref):
        mine = own_ref[...].astype(F32)
        g = None
        for i in range(s):
            part = jnp.where(me_ref[0] == i, mine, p_ref[i].astype(F32))
            g = part if g is None else g + part
        delta, nm, nv = _adamw_math(w_ref[at], g, m_ref[at], v_ref[at])
        g_ref[at] = g
        d_ref[at] = delta
        nm_ref[at] = nm
        nv_ref[at] = nv

    if unit_axis:
        blk = pl.BlockSpec((tr, 1, tc), lambda i, j, me_ref: (i, 0, j))
        out = jax.ShapeDtypeStruct((r, 1, c), F32)
    else:
        blk = pl.BlockSpec((tr, tc), lambda i, j, me_ref: (i, j))
        out = jax.ShapeDtypeStruct((r, c), F32)
    return pl.pallas_call(
        body,
        grid_spec=pltpu.PrefetchScalarGridSpec(
            num_scalar_prefetch=1, grid=(r // tr, c // tc),
            in_specs=[pl.BlockSpec((s, tr, tc), lambda i, j, me_ref: (0, i, j)),
                      pl.BlockSpec((None, tr, tc), lambda i, j, me_ref: (me_ref[0], i, j)), blk, blk, blk],
            out_specs=[blk, blk, blk, blk]),
        out_shape=[out, out, out, out], compiler_params=_params("parallel", "parallel"), name=name,
    )(me, parts, own, w, m, v)


N_CHIPS = N_DEV // 2


def _chip_sums(chunks, from_sibling, core, *, name):
    _, r, c = chunks.shape
    tr, tc = _shard_tile(r, c, CHIP_SUM_TILE_ELEMS)
    assert r % tr == 0 and c % tc == 0, (name, r, c)

    def body(core_ref, mine_ref, other_ref, o_ref):
        o_ref[...] = (mine_ref[...].astype(F32) + other_ref[...].astype(F32)).astype(o_ref.dtype)

    by_chip = pl.BlockSpec((None, tr, tc), lambda q, i, j, core_ref: (q, i, j))
    return pl.pallas_call(
        body,
        grid_spec=pltpu.PrefetchScalarGridSpec(
            num_scalar_prefetch=1, grid=(N_CHIPS, r // tr, c // tc),
            in_specs=[pl.BlockSpec((None, tr, tc), lambda q, i, j, core_ref: (2 * q + core_ref[0], i, j)), by_chip],
            out_specs=by_chip),
        out_shape=jax.ShapeDtypeStruct((N_CHIPS, r, c), chunks.dtype),
        compiler_params=_params("parallel", "parallel", "parallel"), name=name,
    )(core, chunks, from_sibling)


def _place():
    return lax.axis_index("x"), lax.axis_index("y"), lax.axis_index("c")


def _slot(px, py, pc):
    return 4 * px + 2 * py + pc


_HBM = pl.BlockSpec(memory_space=pltpu.HBM)


_SEM = pl.BlockSpec(memory_space=pltpu.SEMAPHORE)
_ANY = pl.BlockSpec(memory_space=pl.ANY)
_EFFECT = pltpu.SideEffectType.DATAFLOW_SIDE_EFFECTING


def _peer(k, x, y, c):
    return (1 - x if k & 4 else x, 1 - y if k & 2 else y, 1 - c if k & 1 else c)


_EXCHANGE_BITS = {"gather_near": (1, 2, 4), "gather_relay": (6,), "gather_pass": (2, 4, 6),
                  "scatter_sibling": (1, 1, 1, 1), "scatter_chips": (2, 4, 6), "all_small": (1, 2, 3, 4, 5, 6, 7)}


def _exchange_copy(mode, src, land, w, i, place, send_sems, recv_sems, receiving):
    bits = _EXCHANGE_BITS[mode]
    k = bits[i]
    peer = _peer(k, *place)
    me = _slot(*place)
    if mode in ("gather_near", "all_small"):
        to, src_ref, sent_to, got_at = peer, src[w], me, _slot(*peer)
    elif mode == "gather_relay":
        x, y, c = place
        other = 1 - c
        to = (lax.bitwise_xor(x, c), lax.bitwise_xor(y, other), c)
        blk = _slot(lax.bitwise_xor(x, other), lax.bitwise_xor(y, c), c)
        src_ref, sent_to, got_at = land[w].at[blk], blk, _slot(*peer)
    elif mode == "gather_pass":
        blk = _slot(*peer)
        to, src_ref, sent_to, got_at = _peer(1, *place), land[w].at[blk], blk, _slot(*_peer(k | 1, *place))
    elif mode == "scatter_sibling":
        to, src_ref, sent_to, got_at = peer, src[w].at[2 * i + 1 - place[2]], i, i
    else:
        to, src_ref, sent_to, got_at = peer, src[w].at[_slot(*peer) // 2], me // 2, _slot(*peer) // 2
    sem = w * len(bits) + i
    return pltpu.make_async_remote_copy(
        src_ref=src_ref, dst_ref=land[w].at[got_at if receiving else sent_to], send_sem=send_sems.at[sem],
        recv_sem=recv_sems.at[sem], device_id=to, device_id_type=MESH)


def _exchange_start(mode, srcs, lands, after, *, name):
    ns, nl = len(srcs), len(lands)
    n_sem = nl * len(_EXCHANGE_BITS[mode])

    def body(*refs):
        src, land = refs[:ns], refs[ns:ns + nl]
        send_sems, recv_sems = refs[ns + nl + 1:ns + nl + 3]
        token = refs[-1]
        place = _place()
        for w in range(nl):
            for i in range(len(_EXCHANGE_BITS[mode])):
                _exchange_copy(mode, src, land, w, i, place, send_sems, recv_sems, receiving=False).start()
        token[...] = jnp.zeros_like(token)

    sems = pltpu.SemaphoreType.DMA((n_sem,))
    arrays = list(srcs) + list(lands)
    res = pl.pallas_call(
        body, name=name, in_specs=[_HBM] * (ns + nl) + [_ANY],
        out_specs=(_SEM, _SEM, *([_HBM] * (ns + nl)), pl.BlockSpec(memory_space=pltpu.VMEM)),
        out_shape=(sems, sems, *[pltpu.HBM(a.shape, a.dtype) for a in arrays], jax.ShapeDtypeStruct((8, LANE), F32)),
        input_output_aliases={i: 2 + i for i in range(ns + nl)},
        compiler_params=pltpu.CompilerParams(has_side_effects=_EFFECT),
    )(*[pltpu.with_memory_space_constraint(a, pltpu.HBM) for a in arrays], after)
    return res[0], res[1], list(res[2:2 + ns]), list(res[2 + ns:2 + ns + nl]), res[-1]


def _exchange_wait(mode, started, after, *, name):
    send_sems, recv_sems, srcs, lands, _ = started
    ns, nl = len(srcs), len(lands)

    def body(*refs):
        src, land = refs[:ns], refs[ns:ns + nl]
        send_sems, recv_sems = refs[ns + nl:ns + nl + 2]
        place = _place()
        for w in range(nl):
            for i in range(len(_EXCHANGE_BITS[mode])):
                cp = _exchange_copy(mode, src, land, w, i, place, send_sems, recv_sems, receiving=True)
                cp.wait_send()
                cp.wait_recv()

    arrays = list(srcs) + list(lands)
    res = pl.pallas_call(
        body, name=name, in_specs=[_HBM] * (ns + nl) + [_SEM, _SEM, _ANY], out_specs=[_HBM] * (ns + nl),
        out_shape=[pltpu.HBM(a.shape, a.dtype) for a in arrays],
        input_output_aliases={i: i for i in range(ns + nl)},
        compiler_params=pltpu.CompilerParams(has_side_effects=_EFFECT),
    )(*arrays, send_sems, recv_sems, after)
    return list(res[:ns]), list(res[ns:])


_SMALL_SEGMENTS = (("a_log", GDN_HEADS), ("dt_bias", GDN_HEADS), ("gdn_norm_w", HEAD_DIM), ("pool_scale", GDN_WIDTH),
                   ("ln1_g", D_MODEL), ("ln1_b", D_MODEL), ("ln2_g", D_MODEL), ("ln2_b", D_MODEL),
                   ("ln3_g", D_MODEL), ("ln3_b", D_MODEL), ("conv_w", CONV_K * QKV_COLS), ("loss", 1))
_SMALL_ROWS = 8
_SMALL_LEN = -(-sum(sz for _, sz in _SMALL_SEGMENTS) // (_SMALL_ROWS * LANE)) * LANE


def _pack_small(vals):
    parts = [vals[n].reshape(-1).astype(F32) if n in vals else jnp.zeros((sz,), F32) for n, sz in _SMALL_SEGMENTS]
    flat = jnp.concatenate(parts)
    flat = jnp.pad(flat, (0, _SMALL_ROWS * _SMALL_LEN - flat.shape[0]))
    return flat.reshape(_SMALL_ROWS, _SMALL_LEN)


def _unpack_small(vec):
    flat = vec.reshape(-1)
    out, off = {}, 0
    for n, sz in _SMALL_SEGMENTS:
        out[n] = flat[off:off + sz]
        off += sz
    return out


_WEIGHT_ORDER = ("w_in", "conv_w", "a_log", "dt_bias", "gdn_norm_w", "pool_w", "pool_scale", "w_out", "ln1_g", "ln1_b",
                 "xq_w", "xk_w", "xv_w", "xo_w", "ln2_g", "ln2_b", "w_up", "w_down", "ln3_g", "ln3_b")


def _shard2d(name, a):
    if name == "w_in":
        return a.T
    return a.reshape(-1, a.shape[-1]) if name == "pool_w" else a


def _update_view(name, a):
    return jnp.transpose(a, (2, 0, 1)) if name == "w_in" else _shard2d(name, a[0])


def _shard_result(name, r, shape):
    return jnp.transpose(r, (1, 2, 0)) if name == "w_in" else r.reshape(shape)


def _gathered_to_full(name, gth):
    if name in ("w_up", "w_in"):
        return gth
    if name == "conv_w":
        return jnp.transpose(gth, (1, 0, 2)).reshape(gth.shape[1], N_DEV * gth.shape[2])
    if name == "pool_w":
        g4 = gth.reshape(N_DEV, POOL_GROUPS, POOL_GROUP_DIM // N_DEV, POOL_GROUP_DIM)
        return jnp.transpose(g4, (1, 0, 2, 3)).reshape(POOL_GROUPS, POOL_GROUP_DIM, POOL_GROUP_DIM)
    return gth.reshape(N_DEV * gth.shape[1], gth.shape[2])


def _full_to_chunks(name, full):
    if name == "w_up":
        return full
    if name == "pool_w":
        g4 = full.reshape(POOL_GROUPS, N_DEV, POOL_GROUP_DIM // N_DEV, POOL_GROUP_DIM)
        return jnp.transpose(g4, (1, 0, 2, 3)).reshape(N_DEV, POOL_GROUPS * POOL_GROUP_DIM // N_DEV, POOL_GROUP_DIM)
    return full.reshape(N_DEV, full.shape[0] // N_DEV, full.shape[1])


_GATHER_GROUPS = (("mixer", ("w_in", "conv_w", "pool_w")), ("attn", ("w_out", "xq_w", "xk_w", "xv_w", "xo_w")),
                  ("up", ("w_up",)), ("down", ("w_down",)))


def _grad_chunks(name, g):
    if name == "w_in":
        return _w_in_chunks(g.astype(BF16))
    return _full_to_chunks(name, g.astype(BF16))


def kernel(x, mem, w_in, conv_w, a_log, dt_bias, gdn_norm_w, pool_w, pool_scale, w_out, ln1_g, ln1_b, xq_w, xk_w, xv_w, xo_w, ln2_g, ln2_b, w_up, w_down, ln3_g, ln3_b, loss_target, m_w_in, m_conv_w, m_a_log, m_dt_bias, m_gdn_norm_w, m_pool_w, m_pool_scale, m_w_out, m_ln1_g, m_ln1_b, m_xq_w, m_xk_w, m_xv_w, m_xo_w, m_ln2_g, m_ln2_b, m_w_up, m_w_down, m_ln3_g, m_ln3_b, v_w_in, v_conv_w, v_a_log, v_dt_bias, v_gdn_norm_w, v_pool_w, v_pool_scale, v_w_out, v_ln1_g, v_ln1_b, v_xq_w, v_xk_w, v_xv_w, v_xo_w, v_ln2_g, v_ln2_b, v_w_up, v_w_down, v_ln3_g, v_ln3_b):
    args = dict(locals())
    wt = {n: args[n][0] for n in _WEIGHT_ORDER}
    mo = {n: args["m_" + n][0] for n in _WEIGHT_ORDER}
    vo = {n: args["v_" + n][0] for n in _WEIGHT_ORDER}

    me = _slot(*_place())
    me_arr = jnp.reshape(me, (1,)).astype(jnp.int32)
    nothing = jnp.zeros((8, LANE), F32)

    def landing_zones(names):
        shards = [_shard2d(n, wt[n]).astype(F32 if n == "conv_w" else BF16) for n in names]
        zones = [lax.dynamic_update_slice(lax.empty((N_DEV, *s.shape), s.dtype), s[None], (me, 0, 0)) for s in shards]
        return shards, zones

    chip_arr = jnp.reshape(me // 2, (1,)).astype(jnp.int32)
    core_arr = jnp.reshape(lax.axis_index("c"), (1,)).astype(jnp.int32)
    names_of = dict(_GATHER_GROUPS)
    gathers = {}
    prepared = {}

    def gather_near(group, after):
        shards, zones = prepared.pop(group) if group in prepared else landing_zones(names_of[group])
        gathers[group] = _exchange_start("gather_near", shards, zones, after, name="gather_near_" + group)
        return gathers[group][4]

    def gather_next(group, was, now, after):
        _, zones = _exchange_wait(was, gathers[group], after, name=f"{was}_{group}_wait")
        gathers[group] = _exchange_start(now, [], zones, nothing, name=f"{now}_{group}")
        return gathers[group][4]

    def gather_relay(group, after):
        return gather_next(group, "gather_near", "gather_relay", after)

    def gather_pass(group, after):
        return gather_next(group, "gather_relay", "gather_pass", after)

    def gathered(group, after):
        _, zones = _exchange_wait("gather_pass", gathers[group], after, name=f"gather_pass_{group}_wait")
        full = {n: _gathered_to_full(n, z) for n, z in zip(names_of[group], zones)}
        full.update({n: wt[n] for n in _VECTORS})
        return _group_weights(group, full)

    token = gather_near("mixer", nothing)
    x16 = _cast_bf16(x[0], name="cast_x")
    later = {group: landing_zones(names_of[group]) for group in ("attn", "up", "down")}
    token, x16, later = lax.optimization_barrier((token, x16, later))
    prepared.update(later)
    token = gather_pass("mixer", gather_relay("mixer", token))
    token = gather_near("attn", token)

    def weights_of(group, after):
        if group == "mixer":
            return gathered(group, token)
        if group == "ahead_conv":
            return gather_near("up", gather_relay("attn", after))[0:1, 0:1]
        if group == "ahead_scan":
            return gather_pass("attn", after)[0:1, 0:1]
        if group == "attn":
            return gathered(group, gather_near("down", gather_relay("up", after)))
        if group == "ahead_attn":
            return gather_relay("down", gather_pass("up", after))[0:1, 0:1]
        if group == "up":
            return gathered(group, gather_pass("down", after))
        return gathered(group, after)

    scatters = {}
    in_flight = []

    def chip_stage(after):
        group, names, started = in_flight.pop()
        chunks, from_sibling = _exchange_wait("scatter_sibling", started, after, name=f"scatter_sibling_{group}_wait")
        sums = [_chip_sums(c, f, core_arr, name=f"chip_sums_{n}") for n, c, f in zip(names, chunks, from_sibling)]
        scatters[group] = (names, _exchange_start("scatter_chips", sums, [lax.empty(s.shape, s.dtype) for s in sums],
                                                  nothing, name="scatter_chips_" + group))
        return scatters[group][1][4]

    small_sent = []

    def grads_ready(group, grads):
        if group == "tick":
            return chip_stage(grads["after"])[0:1, 0:1] if in_flight else None
        if group == "small":
            small = _finish_small_grads(grads)
            small["loss"] = 0.5 * grads["sq"][0:1, 0] / D_MODEL
            packed = _pack_small(small)
            zone = lax.empty((N_DEV, *packed.shape), F32)
            small_sent.append(_exchange_start("all_small", [packed], [zone], nothing, name="small_grads_start"))
            return small_sent[0][4][0:1, 0:1]
        names = tuple(grads)
        chunks = [_grad_chunks(n, grads[n]) for n in names]
        token = chip_stage(chunks[0]) if in_flight else nothing
        zones = [lax.empty((N_CHIPS, *c.shape[1:]), c.dtype) for c in chunks]
        started = _exchange_start("scatter_sibling", chunks, zones, token, name="scatter_sibling_" + group)
        in_flight.append((group, names, started))
        return started[4][0:1, 0:1]

    sq, grad_x, g = _local_step(x[0], x16, mem[0], loss_target[0], weights_of, grads_ready)

    out = {}
    after = chip_stage(grad_x)
    for group, (names, started) in scatters.items():
        sums, lands = _exchange_wait("scatter_chips", started, after, name=f"scatter_chips_{group}_wait")
        for n, parts, own in zip(names, lands, sums):
            res = _adamw_shard(parts, own, chip_arr, _update_view(n, args[n]), _update_view(n, args["m_" + n]),
                               _update_view(n, args["v_" + n]), name="adamw_" + n)
            out[n] = [_shard_result(n, r, args[n].shape) for r in res]
            after = res[1]

    (packed,), (zone,) = _exchange_wait("all_small", small_sent[0], after, name="small_grads_wait")
    gs, ds, ms, vs = _adamw_shard(
        zone, jnp.broadcast_to(packed, zone.shape), me_arr, _pack_small({n: wt[n] for n in _VECTORS}),
        _pack_small({n: mo[n] for n in _VECTORS}), _pack_small({n: vo[n] for n in _VECTORS}), name="adamw_small")
    gs, ds, ms, vs = _unpack_small(gs), _unpack_small(ds), _unpack_small(ms), _unpack_small(vs)
    cols = conv_w.shape[-1]
    conv_full = gs["conv_w"].reshape(CONV_K, QKV_COLS)
    conv_mine = lax.dynamic_slice(conv_full, (0, me * cols), (CONV_K, cols))[None]
    res = _adamw_shard(conv_mine, conv_mine, jnp.zeros((1,), jnp.int32), wt["conv_w"], mo["conv_w"], vo["conv_w"],
                       name="adamw_conv_w")
    out["conv_w"] = [r.reshape(conv_w.shape) for r in res]
    for n in _VECTORS:
        out[n] = [t[n].reshape(args[n].shape) for t in (gs, ds, ms, vs)]

    return (gs["loss"][0], grad_x[None], *[out[n][0] for n in _WEIGHT_ORDER], *[out[n][1] for n in _WEIGHT_ORDER],
            *[out[n][2] for n in _WEIGHT_ORDER], *[out[n][3] for n in _WEIGHT_ORDER])
```

```python
import jax
import jax.numpy as jnp
from jax import lax
from jax.experimental import pallas as pl
from jax.experimental.pallas import tpu as pltpu

F32 = jnp.float32
BF16 = jnp.bfloat16
MESH = pl.DeviceIdType.MESH

N_DEV = 8
D_MODEL = 2048
GDN_WIDTH = 1024
GDN_HEADS = 8
HEAD_DIM = 128
CONV_K = 4
CHUNK = 64
POOL_GROUPS = 4
POOL_GROUP_DIM = 256
MEM_LEN = 256
XATTN_HEADS = 4
XATTN_HEAD_DIM = 512
D_FF = 8192
IN_COLS = 5136
ALPHA = 2.0 ** 0.25
LN_EPS = 1e-5
NORM_EPS = 1e-6

LANE = 128
QKV_COLS = 3 * GDN_WIDTH
Z_OFF = QKV_COLS
BA_OFF = 4 * GDN_WIDTH
POOL_OFF = BA_OFF + 2 * LANE
PROJ_COLS = POOL_OFF + GDN_WIDTH
BA_BLK = BA_OFF // LANE
POOL_BLK = POOL_OFF // POOL_GROUP_DIM

ADAM_LR = 0.001
ADAM_B1 = 0.9
ADAM_B2 = 0.999
ADAM_EPS = 1e-08
ADAM_WD = 0.01
ADAM_STEP = 10

VMEM_LIMIT_BYTES = 48 * 1024 * 1024


def _params(*sem):
    return pltpu.CompilerParams(dimension_semantics=sem if sem else None, vmem_limit_bytes=VMEM_LIMIT_BYTES)


def _make_dots(cast, precision, batched=False):
    lead = 1 if batched else 0
    batch = ((0,), (0,)) if batched else ((), ())

    def dg(a, b, ca, cb):
        if cast is not None:
            a = a.astype(cast)
            b = b.astype(cast)
        return lax.dot_general(a, b, (((ca + lead,), (cb + lead,)), batch), precision=precision, preferred_element_type=F32)

    def nn_(a, b):
        return dg(a, b, 1, 0)

    def nt_(a, b):
        return dg(a, b, 1, 1)

    def tn_(a, b):
        return dg(a, b, 0, 0)

    @jax.custom_vjp
    def nn(a, b):
        return nn_(a, b)

    nn.defvjp(lambda a, b: (nn_(a, b), (a, b)), lambda r, g: (nt_(g, r[1]), tn_(r[0], g)))

    @jax.custom_vjp
    def nt(a, b):
        return nt_(a, b)

    nt.defvjp(lambda a, b: (nt_(a, b), (a, b)), lambda r, g: (nn_(g, r[1]), tn_(g, r[0])))

    @jax.custom_vjp
    def tn(a, b):
        return tn_(a, b)

    tn.defvjp(lambda a, b: (tn_(a, b), (a, b)), lambda r, g: (nt_(r[1], g), nn_(r[0], g)))

    return (nn_, nt_, tn_), (nn, nt, tn)


_BDOT_PLAIN, _BDOT_VJP = _make_dots(BF16, None)
_BDOT_BATCH_PLAIN, _BDOT_BATCH_VJP = _make_dots(BF16, None, batched=True)
_FDOT_BATCH_PLAIN, _FDOT_BATCH_VJP = _make_dots(BF16, None, batched=True)


def _mm(a, b, *, ta=False, tb=False, out_dtype=F32, tm=None, tn=512, tk=None, epi=None, extra=None, add_scale=1.0,
        b_chunks=False, o_chunks=False, after=None, name):
    m, k = (a.shape[1], a.shape[0]) if ta else a.shape
    if b_chunks:
        n, kb = (b.shape[1], N_DEV * b.shape[2]) if tb else (N_DEV * b.shape[2], b.shape[1])
    else:
        n, kb = b.shape if tb else (b.shape[1], b.shape[0])
    assert kb == k, (name, a.shape, b.shape)
    tm, tn, tk = min(tm or m, m), min(tn, n), min(tk or k, k)
    assert m % tm == 0 and n % tn == 0 and k % tk == 0, (name, m, n, k)
    nk = k // tk
    dims = (((0 if ta else 1,), (1 if tb else 0,)), ((), ()))
    n_extra = 0 if epi in (None, "relu2") else 1
    n_out = 2 if epi == "relu2" else 1
    if epi in ("relu2", "mul2r"):
        out_dtype = BF16
    n_after = 0 if after is None else 1

    def body(*refs):
        a_ref, b_ref = refs[:2]
        c_ref = refs[2] if n_extra else None
        o_refs = refs[2 + n_extra + n_after:2 + n_extra + n_after + n_out]
        scr = refs[2 + n_extra + n_after + n_out:]
        r = lax.dot_general(a_ref[...].astype(BF16), b_ref[...].astype(BF16), dims, preferred_element_type=F32)

        def finish(v):
            if epi == "add":
                o_refs[0][...] = (v + add_scale * c_ref[...]).astype(out_dtype)
            elif epi == "relu2":
                p = jnp.maximum(v, 0.0)
                o_refs[0][...] = (p * p).astype(BF16)
                o_refs[1][...] = p.astype(BF16)
            elif epi == "mul2r":
                o_refs[0][...] = (v * (2.0 * c_ref[...].astype(F32))).astype(BF16)
            else:
                o_refs[0][...] = v.astype(out_dtype)

        if nk == 1:
            finish(r)
        else:
            acc = scr[0]
            kk = pl.program_id(2)

            @pl.when(kk == 0)
            def _():
                acc[...] = r

            @pl.when(kk > 0)
            def _():
                acc[...] += r

            @pl.when(kk == nk - 1)
            def _():
                finish(acc[...])

    a_spec = pl.BlockSpec((tk, tm), lambda i, j, kk: (kk, i)) if ta else pl.BlockSpec((tm, tk), lambda i, j, kk: (i, kk))
    if b_chunks and tb:
        kc = k // N_DEV // tk
        b_spec = pl.BlockSpec((None, tn, tk), lambda i, j, kk: (kk // kc, j, kk % kc))
    elif b_chunks:
        nc = n // N_DEV // tn
        b_spec = pl.BlockSpec((None, tk, tn), lambda i, j, kk: (j // nc, kk, j % nc))
    elif tb:
        b_spec = pl.BlockSpec((tn, tk), lambda i, j, kk: (j, kk))
    else:
        b_spec = pl.BlockSpec((tk, tn), lambda i, j, kk: (kk, j))
    mn_spec = pl.BlockSpec((tm, tn), lambda i, j, kk: (i, j))
    if o_chunks:
        oc = n // N_DEV // tn
        o_spec = pl.BlockSpec((None, tm, tn), lambda i, j, kk: (j // oc, i, j % oc))
        o_shape = jax.ShapeDtypeStruct((N_DEV, m, n // N_DEV), out_dtype)
    else:
        o_spec, o_shape = mn_spec, jax.ShapeDtypeStruct((m, n), out_dtype)
    res = pl.pallas_call(
        body, grid=(m // tm, n // tn, nk),
        in_specs=[a_spec, b_spec] + [mn_spec] * n_extra + [pl.BlockSpec(memory_space=pl.ANY)] * n_after,
        out_specs=[o_spec] * n_out, out_shape=[o_shape] * n_out,
        scratch_shapes=[pltpu.VMEM((tm, tn), F32)] if nk > 1 else [],
        compiler_params=_params("parallel", "parallel", "arbitrary"), name=name,
    )(a, b, *([extra] if n_extra else []), *([after] if n_after else []))
    return res if n_out > 1 else res[0]


def _cast_bf16(v, *, name, tm=512):
    t, d = v.shape
    tm = min(tm, t)

    def body(v_ref, o_ref):
        o_ref[...] = v_ref[...].astype(BF16)

    spec = pl.BlockSpec((tm, d), lambda i: (i, 0))
    return pl.pallas_call(body, grid=(t // tm,), in_specs=[spec], out_specs=spec,
                          out_shape=jax.ShapeDtypeStruct((t, d), BF16), compiler_params=_params("parallel"), name=name)(v)


def _shift_down(v, s):
    if s == 0:
        return v
    row = lax.broadcasted_iota(jnp.int32, v.shape, 0)
    return jnp.where(row >= s, pltpu.roll(v, s, axis=0), 0.0)


def _shift_up(v, s):
    if s == 0:
        return v
    t = v.shape[0]
    row = lax.broadcasted_iota(jnp.int32, v.shape, 0)
    return jnp.where(row < t - s, pltpu.roll(v, t - s, axis=0), 0.0)


def _post_col(j):
    return (j % GDN_HEADS) * 3 + j // GDN_HEADS


def _gdn_prep_fwd(proj, conv_w):
    t = proj.shape[0]

    def body(x_ref, w_ref, o_ref):
        j = pl.program_id(0)
        x = x_ref[...]
        y = jnp.zeros_like(x)
        for tap in range(CONV_K):
            y = y + w_ref[tap:tap + 1, :] * _shift_down(x, CONV_K - 1 - tap)
        c = y * jax.nn.sigmoid(y)
        nrm = c * lax.rsqrt(jnp.sum(c * c, axis=1, keepdims=True) + NORM_EPS)
        o_ref[...] = jnp.where(j < 2 * GDN_HEADS, nrm, c)

    return pl.pallas_call(
        body, grid=(QKV_COLS // LANE,),
        in_specs=[pl.BlockSpec((t, LANE), lambda j: (0, j)), pl.BlockSpec((CONV_K, LANE), lambda j: (0, j))],
        out_specs=pl.BlockSpec((t, LANE), lambda j: (0, _post_col(j))),
        out_shape=jax.ShapeDtypeStruct((t, QKV_COLS), F32),
        compiler_params=_params("parallel"), name="gdn_prep_fwd",
    )(proj, conv_w)


def _gdn_prep_bwd(proj, conv_w, dpost, dproj):
    t = proj.shape[0]

    def body(x_ref, w_ref, d_ref, _, dx_ref, dw_ref):
        j = pl.program_id(0)
        x = x_ref[...]
        xs = [_shift_down(x, CONV_K - 1 - tap) for tap in range(CONV_K)]
        y = jnp.zeros_like(x)
        for tap in range(CONV_K):
            y = y + w_ref[tap:tap + 1, :] * xs[tap]
        sig = jax.nn.sigmoid(y)
        c = y * sig
        r = lax.rsqrt(jnp.sum(c * c, axis=1, keepdims=True) + NORM_EPS)
        nrm = c * r
        d = d_ref[...]
        dc_norm = r * (d - nrm * jnp.sum(d * nrm, axis=1, keepdims=True))
        dc = jnp.where(j < 2 * GDN_HEADS, dc_norm, d)
        dy = dc * (sig * (1.0 + y * (1.0 - sig)))
        dx = jnp.zeros_like(x)
        for tap in range(CONV_K):
            dx = dx + _shift_up(w_ref[tap:tap + 1, :] * dy, CONV_K - 1 - tap)
            dw_ref[tap:tap + 1, :] = jnp.sum(dy * xs[tap], axis=0, keepdims=True)
        dx_ref[...] = dx.astype(dx_ref.dtype)

    return pl.pallas_call(
        body, grid=(QKV_COLS // LANE,),
        in_specs=[pl.BlockSpec((t, LANE), lambda j: (0, j)), pl.BlockSpec((CONV_K, LANE), lambda j: (0, j)),
                  pl.BlockSpec((t, LANE), lambda j: (0, _post_col(j))), pl.BlockSpec(memory_space=pl.ANY)],
        out_specs=[pl.BlockSpec((t, LANE), lambda j: (0, j)), pl.BlockSpec((CONV_K, LANE), lambda j: (0, j))],
        out_shape=[jax.ShapeDtypeStruct(dproj.shape, dproj.dtype), jax.ShapeDtypeStruct((CONV_K, QKV_COLS), F32)],
        input_output_aliases={3: 0},
        compiler_params=_params("parallel"), name="gdn_prep_bwd",
    )(proj, conv_w, dpost, dproj)


def _softplus(v):
    return jnp.maximum(v, 0.0) + jnp.log(1.0 + jnp.exp(-jnp.abs(v)))


def _tri_inv(low, nn):
    r = lax.broadcasted_iota(jnp.int32, (CHUNK, CHUNK), 0)
    c = lax.broadcasted_iota(jnp.int32, (CHUNK, CHUNK), 1)
    eye = (r == c).astype(F32)
    same_blk = lax.shift_right_logical(r, 4) == lax.shift_right_logical(c, 4)
    diag = jnp.where(same_blk, low, 0.0)
    off = low - diag
    n1 = -diag
    n2 = nn(n1, n1)
    n4 = nn(n2, n2)
    n8 = nn(n4, n4)
    inv_d = nn(nn(nn(eye + n1, eye + n2), eye + n4), eye + n8)
    m1 = nn(inv_d, off)
    m2 = nn(m1, m1)
    return nn(nn(eye - m1, eye + m2), inv_d)


@jax.custom_vjp
def _tri_inv_known(low, t_inv):
    return t_inv


def _tri_inv_known_fwd(low, t_inv):
    return t_inv, t_inv


def _tri_inv_known_bwd(t_inv, g):
    _, nt, tn = _FDOT_BATCH_PLAIN
    return -nt(tn(t_inv, g), t_inv), jnp.zeros_like(t_inv)


_tri_inv_known.defvjp(_tri_inv_known_fwd, _tri_inv_known_bwd)


LOCAL_HEADS_PER_STEP = 8


def _gdn_local_fn(qkv, ba, alog_row, dtb_row, first_head, bdots, fdots, t_known=None):
    nn, nt, tn = bdots
    fnn = fdots[0]
    n_heads = qkv.shape[1] // (3 * HEAD_DIM)
    part = lambda i, p: qkv[:, (3 * i + p) * HEAD_DIM:(3 * i + p + 1) * HEAD_DIM]
    q = jnp.stack([part(i, 0) for i in range(n_heads)]) * (HEAD_DIM ** -0.5)
    k = jnp.stack([part(i, 1) for i in range(n_heads)])
    v = jnp.stack([part(i, 2) for i in range(n_heads)])
    lane = lax.broadcasted_iota(jnp.int32, ba.shape, 1)
    bg = jnp.where(lane < GDN_HEADS, jax.nn.sigmoid(ba), -jnp.exp(alog_row) * _softplus(ba + dtb_row))
    pick = lambda l: jnp.sum(jnp.where(lane == l, bg, 0.0), axis=1, keepdims=True)
    beta = jnp.stack([pick(first_head + i) for i in range(n_heads)])
    g = jnp.stack([pick(first_head + i + GDN_HEADS) for i in range(n_heads)])

    r = lax.broadcasted_iota(jnp.int32, (CHUNK, CHUNK), 0)
    c = lax.broadcasted_iota(jnp.int32, (CHUNK, CHUNK), 1)
    incl = r >= c
    strict = r > c
    eye = r == c

    def to_row(col):
        return jnp.sum(jnp.where(eye, col, 0.0), axis=1, keepdims=True)

    gc = jnp.sum(jnp.where(incl, to_row(g), 0.0), axis=2, keepdims=True)
    diff = gc - to_row(gc)
    decay = jnp.where(incl, jnp.exp(jnp.where(incl, diff, 0.0)), 0.0)
    k_beta = k * beta
    v_beta = v * beta
    low = jnp.where(strict, nt(k_beta, k) * decay, 0.0)
    t_inv = _tri_inv(low, fnn) if t_known is None else _tri_inv_known(low, t_known)
    eg = jnp.exp(gc)
    u = fnn(t_inv, v_beta)
    w = fnn(t_inv, k_beta * eg)
    attn = jnp.where(incl, nt(q, k) * decay, 0.0)
    last = lax.broadcasted_iota(jnp.int32, (CHUNK, 1), 0) == CHUNK - 1
    g_last = jnp.sum(jnp.where(last, gc, 0.0), axis=1, keepdims=True)
    kdec = k * jnp.exp(g_last - gc)
    elast = jnp.broadcast_to(jnp.exp(g_last), (n_heads, 1, LANE))
    return u, w, q * eg, kdec, attn, elast, t_inv


def _gdn_state_fn(u, w, qg, kdec, attn, elast, state, bdots):
    nn, _, tn = bdots
    v_new = u - nn(w, state)
    o = nn(qg, state) + nn(attn, v_new)
    return o, state * elast + tn(kdec, v_new)


def _gdn_local_fwd(post, proj, alog_row, dtb_row):
    t = post.shape[0]
    n_chunks = t // CHUNK
    hb = LOCAL_HEADS_PER_STEP

    def body(qkv_ref, ba_ref, al_ref, dt_ref, u_ref, w_ref, qg_ref, kd_ref, at_ref, el_ref, ti_ref):
        u, w, qg, kdec, attn, elast, t_inv = _gdn_local_fn(qkv_ref[...], ba_ref[...], al_ref[...], dt_ref[...],
                                                           pl.program_id(1) * hb, _BDOT_BATCH_PLAIN, _FDOT_BATCH_PLAIN)
        for i in range(hb):
            cols = slice(i * HEAD_DIM, (i + 1) * HEAD_DIM)
            u_ref[:, cols] = u[i]
            w_ref[:, cols] = w[i].astype(BF16)
            qg_ref[:, cols] = qg[i].astype(BF16)
            kd_ref[:, cols] = kdec[i].astype(BF16)
        at_ref[...] = attn.astype(BF16)
        el_ref[:, 0] = elast
        ti_ref[...] = t_inv

    wide = pl.BlockSpec((CHUNK, hb * HEAD_DIM), lambda n, j: (n, j))
    square = pl.BlockSpec((hb, CHUNK, CHUNK), lambda n, j: (j, n, 0))
    row = pl.BlockSpec((1, LANE), lambda n, j: (0, 0))
    res = pl.pallas_call(
        body, grid=(n_chunks, GDN_HEADS // hb),
        in_specs=[pl.BlockSpec((CHUNK, hb * 3 * HEAD_DIM), lambda n, j: (n, j)),
                  pl.BlockSpec((CHUNK, LANE), lambda n, j: (n, BA_BLK)), row, row],
        out_specs=[wide, wide, wide, wide, square, pl.BlockSpec((hb, 1, 1, LANE), lambda n, j: (j, n, 0, 0)), square],
        out_shape=[jax.ShapeDtypeStruct((t, GDN_WIDTH), F32), jax.ShapeDtypeStruct((t, GDN_WIDTH), BF16),
                   jax.ShapeDtypeStruct((t, GDN_WIDTH), BF16), jax.ShapeDtypeStruct((t, GDN_WIDTH), BF16),
                   jax.ShapeDtypeStruct((GDN_HEADS, t, CHUNK), BF16),
                   jax.ShapeDtypeStruct((GDN_HEADS, n_chunks, 1, LANE), F32),
                   jax.ShapeDtypeStruct((GDN_HEADS, t, CHUNK), F32)],
        compiler_params=_params("parallel", "parallel"), name="gdn_local_fwd",
    )(post, proj, alog_row, dtb_row)
    return tuple(res[:6]), res[6]


def _by_head(ref):
    return jnp.stack([ref[:, h * HEAD_DIM:(h + 1) * HEAD_DIM] for h in range(ref.shape[1] // HEAD_DIM)])


def _gdn_state_specs(n_of):
    wide = pl.BlockSpec((CHUNK, GDN_WIDTH), lambda n: (n_of(n), 0))
    attn = pl.BlockSpec((GDN_HEADS, CHUNK, CHUNK), lambda n: (0, n_of(n), 0))
    elast = pl.BlockSpec((GDN_HEADS, 1, 1, LANE), lambda n: (0, n_of(n), 0, 0))
    saved = pl.BlockSpec((GDN_HEADS, 1, HEAD_DIM, HEAD_DIM), lambda n: (0, n_of(n), 0, 0))
    return wide, attn, elast, saved


def _gdn_state_fwd(u, w, qg, kdec, attn, elast):
    t = u.shape[0]
    n_chunks = t // CHUNK

    def body(u_ref, w_ref, qg_ref, kd_ref, at_ref, el_ref, o_ref, save_ref, state_ref):
        @pl.when(pl.program_id(0) == 0)
        def _():
            state_ref[...] = jnp.zeros_like(state_ref)

        state = state_ref[...]
        save_ref[:, 0] = state
        o, new_state = _gdn_state_fn(_by_head(u_ref), _by_head(w_ref), _by_head(qg_ref), _by_head(kd_ref), at_ref[...],
                                     el_ref[:, 0], state, _BDOT_BATCH_PLAIN)
        for h in range(GDN_HEADS):
            o_ref[:, h * HEAD_DIM:(h + 1) * HEAD_DIM] = o[h]
        state_ref[...] = new_state

    wide, attn_spec, elast_spec, saved_spec = _gdn_state_specs(lambda n: n)
    return pl.pallas_call(
        body, grid=(n_chunks,), in_specs=[wide, wide, wide, wide, attn_spec, elast_spec],
        out_specs=[wide, saved_spec],
        out_shape=[jax.ShapeDtypeStruct((t, GDN_WIDTH), F32),
                   jax.ShapeDtypeStruct((GDN_HEADS, n_chunks, HEAD_DIM, HEAD_DIM), F32)],
        scratch_shapes=[pltpu.VMEM((GDN_HEADS, HEAD_DIM, HEAD_DIM), F32)],
        compiler_params=_params("arbitrary"), name="gdn_state_fwd",
    )(u, w, qg, kdec, attn, elast)


def _gdn_state_bwd(u, w, qg, kdec, attn, elast, saved, do):
    t = u.shape[0]
    n_chunks = t // CHUNK
    last = n_chunks - 1

    def body(u_ref, w_ref, qg_ref, kd_ref, at_ref, el_ref, save_ref, do_ref,
             du_ref, dw_ref, dqg_ref, dkd_ref, dat_ref, del_ref, dstate_ref):
        @pl.when(pl.program_id(0) == 0)
        def _():
            dstate_ref[...] = jnp.zeros_like(dstate_ref)

        _, vjp = jax.vjp(
            lambda *a: _gdn_state_fn(*a, _BDOT_BATCH_VJP), _by_head(u_ref), _by_head(w_ref).astype(F32),
            _by_head(qg_ref).astype(F32), _by_head(kd_ref).astype(F32), at_ref[...].astype(F32), el_ref[:, 0],
            save_ref[:, 0])
        du, dw, dqg, dkd, dat, de, dstate = vjp((_by_head(do_ref), dstate_ref[...]))
        for h in range(GDN_HEADS):
            cols = slice(h * HEAD_DIM, (h + 1) * HEAD_DIM)
            du_ref[:, cols] = du[h]
            dw_ref[:, cols] = dw[h]
            dqg_ref[:, cols] = dqg[h]
            dkd_ref[:, cols] = dkd[h]
        dat_ref[...] = dat
        del_ref[:, 0] = de
        dstate_ref[...] = dstate

    wide, attn_spec, elast_spec, saved_spec = _gdn_state_specs(lambda n: last - n)
    wide_f32 = jax.ShapeDtypeStruct((t, GDN_WIDTH), F32)
    return pl.pallas_call(
        body, grid=(n_chunks,), in_specs=[wide, wide, wide, wide, attn_spec, elast_spec, saved_spec, wide],
        out_specs=[wide, wide, wide, wide, attn_spec, elast_spec],
        out_shape=[wide_f32, wide_f32, wide_f32, wide_f32, jax.ShapeDtypeStruct((GDN_HEADS, t, CHUNK), F32),
                   jax.ShapeDtypeStruct((GDN_HEADS, n_chunks, 1, LANE), F32)],
        scratch_shapes=[pltpu.VMEM((GDN_HEADS, HEAD_DIM, HEAD_DIM), F32)],
        compiler_params=_params("arbitrary"), name="gdn_state_bwd",
    )(u, w, qg, kdec, attn, elast, saved, do)


def _gdn_local_bwd(post, proj, alog_row, dtb_row, t_inv, cots, dproj):
    t = post.shape[0]
    n_chunks = t // CHUNK
    hb = LOCAL_HEADS_PER_STEP
    n_steps = GDN_HEADS // hb

    def body(qkv_ref, ba_ref, al_ref, dt_ref, ti_ref, du_ref, dw_ref, dqg_ref, dkd_ref, dat_ref, del_ref, _,
             dqkv_ref, dba_ref, dal_ref, ddt_ref, dba_acc):
        n = pl.program_id(0)
        j = pl.program_id(1)

        @pl.when((n == 0) & (j == 0))
        def _():
            dal_ref[...] = jnp.zeros_like(dal_ref)
            ddt_ref[...] = jnp.zeros_like(ddt_ref)

        @pl.when(j == 0)
        def _():
            dba_acc[...] = jnp.zeros_like(dba_acc)

        t_known = ti_ref[...]
        _, vjp = jax.vjp(
            lambda a, b, c, d: _gdn_local_fn(a, b, c, d, j * hb, _BDOT_BATCH_VJP, _FDOT_BATCH_VJP, t_known)[:6],
            qkv_ref[...], ba_ref[...], al_ref[...], dt_ref[...])
        dqkv, dba, dal, ddt = vjp((_by_head(du_ref), _by_head(dw_ref), _by_head(dqg_ref), _by_head(dkd_ref), dat_ref[...],
                                   del_ref[:, 0]))
        dqkv_ref[...] = dqkv
        dba_acc[...] += dba
        dal_ref[...] += dal
        ddt_ref[...] += ddt

        @pl.when(j == n_steps - 1)
        def _():
            dba_ref[:, 0:LANE] = dba_acc[...].astype(dba_ref.dtype)
            dba_ref[:, LANE:2 * LANE] = jnp.zeros((CHUNK, LANE), dba_ref.dtype)

    wide = pl.BlockSpec((CHUNK, hb * HEAD_DIM), lambda n, j: (n, j))
    qkv_spec = pl.BlockSpec((CHUNK, hb * 3 * HEAD_DIM), lambda n, j: (n, j))
    row = pl.BlockSpec((1, LANE), lambda n, j: (0, 0))
    return pl.pallas_call(
        body, grid=(n_chunks, n_steps),
        in_specs=[qkv_spec, pl.BlockSpec((CHUNK, LANE), lambda n, j: (n, BA_BLK)), row, row,
                  pl.BlockSpec((hb, CHUNK, CHUNK), lambda n, j: (j, n, 0)), wide, wide, wide, wide,
                  pl.BlockSpec((hb, CHUNK, CHUNK), lambda n, j: (j, n, 0)),
                  pl.BlockSpec((hb, 1, 1, LANE), lambda n, j: (j, n, 0, 0)), pl.BlockSpec(memory_space=pl.ANY)],
        out_specs=[qkv_spec, pl.BlockSpec((CHUNK, 2 * LANE), lambda n, j: (n, BA_BLK // 2)), row, row],
        out_shape=[jax.ShapeDtypeStruct((t, QKV_COLS), F32), jax.ShapeDtypeStruct(dproj.shape, dproj.dtype),
                   jax.ShapeDtypeStruct((1, LANE), F32), jax.ShapeDtypeStruct((1, LANE), F32)],
        input_output_aliases={11: 1},
        scratch_shapes=[pltpu.VMEM((CHUNK, LANE), F32)],
        compiler_params=_params("arbitrary", "arbitrary"), name="gdn_local_bwd",
    )(post, proj, alog_row, dtb_row, t_inv, *cots, dproj)


def _onorm_fn(o, z, w):
    return o * lax.rsqrt(jnp.mean(o * o, axis=1, keepdims=True) + NORM_EPS) * w * (z * jax.nn.sigmoid(z))


_Z_WIDE_BLK = Z_OFF // GDN_WIDTH


def _onorm_fwd(o_raw, proj, norm_w, mixin, tm=256):
    t = o_raw.shape[0]
    tm = min(tm, t)

    def body(o_ref, z_ref, w_ref, _, out_ref):
        for h in range(GDN_HEADS):
            cols = slice(h * HEAD_DIM, (h + 1) * HEAD_DIM)
            out_ref[:, cols] = _onorm_fn(o_ref[:, cols], z_ref[:, cols], w_ref[...]).astype(out_ref.dtype)

    wide = pl.BlockSpec((tm, GDN_WIDTH), lambda i: (i, 0))
    return pl.pallas_call(
        body, grid=(t // tm,),
        in_specs=[wide, pl.BlockSpec((tm, GDN_WIDTH), lambda i: (i, _Z_WIDE_BLK)), pl.BlockSpec((1, LANE), lambda i: (0, 0)),
                  pl.BlockSpec(memory_space=pl.ANY)],
        out_specs=wide, out_shape=jax.ShapeDtypeStruct(mixin.shape, mixin.dtype), input_output_aliases={3: 0},
        compiler_params=_params("parallel"), name="gdn_onorm_fwd",
    )(o_raw, proj, norm_w, mixin)


def _onorm_bwd(o_raw, proj, norm_w, dmixin, dproj, tm=256):
    t = o_raw.shape[0]
    tm = min(tm, t)

    def body(o_ref, z_ref, w_ref, d_ref, _, do_ref, dz_ref, dw_ref):
        @pl.when(pl.program_id(0) == 0)
        def _():
            dw_ref[...] = jnp.zeros_like(dw_ref)

        for h in range(GDN_HEADS):
            cols = slice(h * HEAD_DIM, (h + 1) * HEAD_DIM)
            _, vjp = jax.vjp(_onorm_fn, o_ref[:, cols], z_ref[:, cols], w_ref[...])
            do, dz, dw = vjp(d_ref[:, cols])
            do_ref[:, cols] = do
            dz_ref[:, cols] = dz.astype(dz_ref.dtype)
            dw_ref[...] += dw

    wide = pl.BlockSpec((tm, GDN_WIDTH), lambda i: (i, 0))
    gate = pl.BlockSpec((tm, GDN_WIDTH), lambda i: (i, _Z_WIDE_BLK))
    row = pl.BlockSpec((1, LANE), lambda i: (0, 0))
    return pl.pallas_call(
        body, grid=(t // tm,), in_specs=[wide, gate, row, wide, pl.BlockSpec(memory_space=pl.ANY)],
        out_specs=[wide, gate, row],
        out_shape=[jax.ShapeDtypeStruct((t, GDN_WIDTH), F32), jax.ShapeDtypeStruct(dproj.shape, dproj.dtype),
                   jax.ShapeDtypeStruct((1, LANE), F32)],
        input_output_aliases={4: 1},
        compiler_params=_params("arbitrary"), name="gdn_onorm_bwd",
    )(o_raw, proj, norm_w, dmixin, dproj)


def _pool_select(levels, gi):
    out = levels[-1]
    for lvl in range(len(levels) - 2, -1, -1):
        out = jnp.where(gi == lvl, levels[lvl], out)
    return out


def _pool_count(shape, gi):
    pos = lax.broadcasted_iota(jnp.int32, shape, 0)
    win = lax.shift_left(jnp.int32(2), gi)
    return jnp.minimum(pos + 1, win).astype(F32)


def _pooled(p, gi):
    acc = p
    levels = []
    for lvl in range(POOL_GROUPS):
        acc = acc + _shift_down(acc, 1 << lvl)
        levels.append(acc)
    return _pool_select(levels, gi) / _pool_count(p.shape, gi) - p


def _pool_fwd(proj, pool_w, pool_scale):
    t = proj.shape[0]

    def body(p_ref, w_ref, s_ref, out_ref):
        gi = pl.program_id(0)
        pooled = _pooled(p_ref[...], gi)
        out_ref[...] = (_BDOT_PLAIN[0](pooled, w_ref[0]) * s_ref[0]).astype(out_ref.dtype)

    return pl.pallas_call(
        body, grid=(POOL_GROUPS,),
        in_specs=[pl.BlockSpec((t, POOL_GROUP_DIM), lambda g: (0, POOL_BLK + g)),
                  pl.BlockSpec((1, POOL_GROUP_DIM, POOL_GROUP_DIM), lambda g: (g, 0, 0)),
                  pl.BlockSpec((1, 1, POOL_GROUP_DIM), lambda g: (g, 0, 0))],
        out_specs=pl.BlockSpec((t, POOL_GROUP_DIM), lambda g: (0, GDN_WIDTH // POOL_GROUP_DIM + g)),
        out_shape=jax.ShapeDtypeStruct((t, 2 * GDN_WIDTH), BF16),
        compiler_params=_params("parallel"), name="pool_fwd",
    )(proj, pool_w, pool_scale)


def _pool_bwd(proj, pool_w, pool_scale, dmixin):
    t = proj.shape[0]
    nn, nt, tn = _BDOT_PLAIN

    def body(p_ref, w_ref, s_ref, d_ref, dp_ref, dw_ref, ds_ref):
        gi = pl.program_id(0)
        p = p_ref[...]
        pooled = _pooled(p, gi)
        mixed = nn(pooled, w_ref[0])
        d = d_ref[...]
        ds_ref[0] = jnp.sum(d * mixed, axis=0, keepdims=True)
        dmixed = d * s_ref[0]
        dw_ref[0] = tn(pooled, dmixed)
        dpooled = nt(dmixed, w_ref[0])
        acc = dpooled / _pool_count(p.shape, gi)
        levels = []
        for lvl in range(POOL_GROUPS):
            acc = acc + _shift_up(acc, 1 << lvl)
            levels.append(acc)
        dp_ref[...] = (_pool_select(levels, gi) - dpooled).astype(dp_ref.dtype)

    return pl.pallas_call(
        body, grid=(POOL_GROUPS,),
        in_specs=[pl.BlockSpec((t, POOL_GROUP_DIM), lambda g: (0, POOL_BLK + g)),
                  pl.BlockSpec((1, POOL_GROUP_DIM, POOL_GROUP_DIM), lambda g: (g, 0, 0)),
                  pl.BlockSpec((1, 1, POOL_GROUP_DIM), lambda g: (g, 0, 0)),
                  pl.BlockSpec((t, POOL_GROUP_DIM), lambda g: (0, GDN_WIDTH // POOL_GROUP_DIM + g))],
        out_specs=[pl.BlockSpec((t, POOL_GROUP_DIM), lambda g: (0, POOL_BLK + g)),
                   pl.BlockSpec((1, POOL_GROUP_DIM, POOL_GROUP_DIM), lambda g: (g, 0, 0)),
                   pl.BlockSpec((1, 1, POOL_GROUP_DIM), lambda g: (g, 0, 0))],
        out_shape=[jax.ShapeDtypeStruct((t, PROJ_COLS), BF16),
                   jax.ShapeDtypeStruct((POOL_GROUPS, POOL_GROUP_DIM, POOL_GROUP_DIM), F32),
                   jax.ShapeDtypeStruct((POOL_GROUPS, 1, POOL_GROUP_DIM), F32)],
        compiler_params=_params("parallel"), name="pool_bwd",
    )(proj, pool_w, pool_scale, dmixin)


def _ln_stats(s):
    mu = jnp.mean(s, axis=1, keepdims=True)
    xc = s - mu
    var = jnp.mean(xc * xc, axis=1, keepdims=True)
    rstd = lax.rsqrt(var + LN_EPS)
    return xc * rstd, rstd


def _ln_fwd(h_in, y, g, b, *, name, tm=512):
    t, d = h_in.shape
    tm = min(tm, t)

    def body(h_ref, y_ref, g_ref, b_ref, o_ref, o16_ref):
        xhat, _ = _ln_stats(ALPHA * h_ref[...] + y_ref[...])
        out = xhat * g_ref[...] + b_ref[...]
        o_ref[...] = out
        o16_ref[...] = out.astype(BF16)

    row = pl.BlockSpec((tm, d), lambda i: (i, 0))
    vec = pl.BlockSpec((1, d), lambda i: (0, 0))
    return pl.pallas_call(
        body, grid=(t // tm,), in_specs=[row, row, vec, vec], out_specs=[row, row],
        out_shape=[jax.ShapeDtypeStruct((t, d), F32), jax.ShapeDtypeStruct((t, d), BF16)],
        compiler_params=_params("parallel"), name=name,
    )(h_in, y, g, b)


def _ln_backward(xhat, rstd, dout, gain):
    dxhat = dout * gain
    m1 = jnp.mean(dxhat, axis=1, keepdims=True)
    m2 = jnp.mean(dxhat * xhat, axis=1, keepdims=True)
    return (rstd * (dxhat - m1 - xhat * m2), jnp.sum(dout * xhat, axis=0, keepdims=True),
            jnp.sum(dout, axis=0, keepdims=True))


def _ln_loss(h_in, y, g, b, target, *, name, tm=256):
    t, d = h_in.shape
    tm = min(tm, t)

    def body(h_ref, y_ref, g_ref, b_ref, t_ref, sq_ref, ds_ref, ds16_ref, dg_ref, dbias_ref):
        @pl.when(pl.program_id(0) == 0)
        def _():
            sq_ref[...] = jnp.zeros_like(sq_ref)
            dg_ref[...] = jnp.zeros_like(dg_ref)
            dbias_ref[...] = jnp.zeros_like(dbias_ref)

        xhat, rstd = _ln_stats(ALPHA * h_ref[...] + y_ref[...])
        err = xhat * g_ref[...] + b_ref[...] - t_ref[...]
        sq_ref[...] += jnp.sum(jnp.sum(err * err, axis=1, keepdims=True), axis=0, keepdims=True)
        ds, dg, dbias = _ln_backward(xhat, rstd, err * (1.0 / d), g_ref[...])
        ds_ref[...] = ds
        ds16_ref[...] = ds.astype(BF16)
        dg_ref[...] += dg
        dbias_ref[...] += dbias

    row = pl.BlockSpec((tm, d), lambda i: (i, 0))
    vec = pl.BlockSpec((1, d), lambda i: (0, 0))
    return pl.pallas_call(
        body, grid=(t // tm,), in_specs=[row, row, vec, vec, row],
        out_specs=[pl.BlockSpec((1, LANE), lambda i: (0, 0)), row, row, vec, vec],
        out_shape=[jax.ShapeDtypeStruct((1, LANE), F32), jax.ShapeDtypeStruct((t, d), F32),
                   jax.ShapeDtypeStruct((t, d), BF16), jax.ShapeDtypeStruct((1, d), F32), jax.ShapeDtypeStruct((1, d), F32)],
        compiler_params=_params("arbitrary"), name=name,
    )(h_in, y, g, b, target)


def _ln_bwd(h_in, y, g, d_a, d_b, *, name, tm=256):
    t, d = h_in.shape
    tm = min(tm, t)
    has_b = d_b is not None

    def body(*refs):
        if has_b:
            h_ref, y_ref, g_ref, da_ref, db_ref, ds_ref, ds16_ref, dg_ref, dbias_ref = refs
        else:
            h_ref, y_ref, g_ref, da_ref, ds_ref, ds16_ref, dg_ref, dbias_ref = refs

        @pl.when(pl.program_id(0) == 0)
        def _():
            dg_ref[...] = jnp.zeros_like(dg_ref)
            dbias_ref[...] = jnp.zeros_like(dbias_ref)

        xhat, rstd = _ln_stats(ALPHA * h_ref[...] + y_ref[...])
        dout = da_ref[...]
        if has_b:
            dout = dout + ALPHA * db_ref[...]
        ds, dg, dbias = _ln_backward(xhat, rstd, dout, g_ref[...])
        ds_ref[...] = ds
        ds16_ref[...] = ds.astype(BF16)
        dg_ref[...] += dg
        dbias_ref[...] += dbias

    row = pl.BlockSpec((tm, d), lambda i: (i, 0))
    vec = pl.BlockSpec((1, d), lambda i: (0, 0))
    args = [h_in, y, g, d_a] + ([d_b] if has_b else [])
    return pl.pallas_call(
        body, grid=(t // tm,), in_specs=[row, row, vec, row] + ([row] if has_b else []),
        out_specs=[row, row, vec, vec],
        out_shape=[jax.ShapeDtypeStruct((t, d), F32), jax.ShapeDtypeStruct((t, d), BF16),
                   jax.ShapeDtypeStruct((1, d), F32), jax.ShapeDtypeStruct((1, d), F32)],
        compiler_params=_params("arbitrary"), name=name,
    )(*args)


def _attn_fn(q, k, v, dots):
    nn, nt, _ = dots
    s = nt(q, k) * (XATTN_HEAD_DIM ** -0.5)
    s = s - lax.stop_gradient(jnp.max(s, axis=1, keepdims=True))
    e = jnp.exp(s)
    p = e / jnp.sum(e, axis=1, keepdims=True)
    return nn(p, v)


def _attn_fwd(q, k, v, tq=2048):
    t = q.shape[0]
    tq = min(tq, t)

    def body(q_ref, k_ref, v_ref, o_ref):
        o_ref[...] = _attn_fn(q_ref[...], k_ref[...], v_ref[...], _BDOT_PLAIN).astype(BF16)

    qs = pl.BlockSpec((tq, XATTN_HEAD_DIM), lambda h, i: (i, h))
    ks = pl.BlockSpec((MEM_LEN, XATTN_HEAD_DIM), lambda h, i: (0, h))
    return pl.pallas_call(
        body, grid=(XATTN_HEADS, t // tq), in_specs=[qs, ks, ks], out_specs=qs,
        out_shape=jax.ShapeDtypeStruct(q.shape, BF16), compiler_params=_params("parallel", "parallel"), name="xattn_fwd",
    )(q, k, v)


def _attn_bwd(q, k, v, do, tq=1024):
    t = q.shape[0]
    tq = min(tq, t)

    def body(q_ref, k_ref, v_ref, do_ref, dq_ref, dk_ref, dv_ref):
        @pl.when(pl.program_id(1) == 0)
        def _():
            dk_ref[...] = jnp.zeros_like(dk_ref)
            dv_ref[...] = jnp.zeros_like(dv_ref)

        _, vjp = jax.vjp(lambda a, b, c: _attn_fn(a, b, c, _BDOT_VJP), q_ref[...].astype(F32), k_ref[...].astype(F32),
                         v_ref[...].astype(F32))
        dq, dk, dv = vjp(do_ref[...].astype(F32))
        dq_ref[...] = dq.astype(BF16)
        dk_ref[...] += dk
        dv_ref[...] += dv

    qs = pl.BlockSpec((tq, XATTN_HEAD_DIM), lambda h, i: (i, h))
    ks = pl.BlockSpec((MEM_LEN, XATTN_HEAD_DIM), lambda h, i: (0, h))
    return pl.pallas_call(
        body, grid=(XATTN_HEADS, t // tq), in_specs=[qs, ks, ks, qs], out_specs=[qs, ks, ks],
        out_shape=[jax.ShapeDtypeStruct(q.shape, BF16), jax.ShapeDtypeStruct(k.shape, F32), jax.ShapeDtypeStruct(v.shape, F32)],
        compiler_params=_params("parallel", "arbitrary"), name="xattn_bwd",
    )(q, k, v, do)


def _local_step(x, x16, mem, target, weights_of, grads_ready):
    def behind(vec, token):
        return vec if token is None else vec + token

    w = dict(weights_of("mixer", None))
    proj = _mm(x16, w["w_in"], tb=True, tn=768, name="mm_in_proj")
    mixin = _pool_fwd(proj, w["pool_w"], w["pool_scale"])
    post = _gdn_prep_fwd(proj, w["conv_w"])
    token = weights_of("ahead_conv", post)
    chunked, t_inv = _gdn_local_fwd(post, proj, behind(w["alog_row"], token), w["dtb_row"])
    o_raw, saved = _gdn_state_fwd(*chunked)
    token = weights_of("ahead_scan", o_raw)
    mixin = _onorm_fwd(o_raw, proj, behind(w["gdn_norm_w"], token), mixin)
    w.update(weights_of("attn", mixin))
    mix = _mm(mixin, w["w_out"], name="mm_out_proj")
    h1, h1_16 = _ln_fwd(x, mix, w["ln1_g"], w["ln1_b"], name="ln1_fwd")
    xq = _mm(h1_16, w["xq_w"], out_dtype=BF16, name="mm_xq")
    xk = _mm(mem, w["xk_w"], out_dtype=BF16, name="mm_xk")
    xv = _mm(mem, w["xv_w"], out_dtype=BF16, name="mm_xv")
    xo = _attn_fwd(xq, xk, xv)
    token = weights_of("ahead_attn", xo)
    if token is not None:
        xo, _ = lax.optimization_barrier((xo, token))
    xa = _mm(xo, w["xo_w"], name="mm_xo")
    h2, h2_16 = _ln_fwd(h1, xa, w["ln2_g"], w["ln2_b"], name="ln2_fwd")
    w.update(weights_of("up", h2_16))
    act, relu = _mm(h2_16, w["w_up"], b_chunks=True, epi="relu2", name="mm_up")
    w.update(weights_of("down", act))
    ff = _mm(act, w["w_down"], tn=512, tk=2048, name="mm_down")
    g = {}
    sq, ds3, ds3_16, g["ln3_g"], g["ln3_b"] = _ln_loss(h2, ff, w["ln3_g"], w["ln3_b"], target, name="ln3_loss")

    gw_down = _mm(act, ds3_16, ta=True, out_dtype=BF16, tm=512, tn=D_MODEL, name="mm_gw_down")
    du = _mm(ds3_16, w["w_down"], tb=True, epi="mul2r", extra=relu, name="mm_du")
    gw_up = _mm(h2_16, du, ta=True, out_dtype=BF16, o_chunks=True, name="mm_gw_up")
    token = grads_ready("mlp", {"w_down": gw_down, "w_up": gw_up})
    dh2 = _mm(du, w["w_up"], tb=True, b_chunks=True, tn=1024, tk=1024, name="mm_dh2")
    ds2, ds2_16, g["ln2_g"], g["ln2_b"] = _ln_bwd(h1, xa, behind(w["ln2_g"], token), dh2, ds3, name="ln2_bwd")
    gw_xo = _mm(xo, ds2_16, ta=True, out_dtype=BF16, name="mm_gw_xo")
    dxo = _mm(ds2_16, w["xo_w"], tb=True, out_dtype=BF16, name="mm_dxo")
    dxq, dxk, dxv = _attn_bwd(xq, xk, xv, dxo)
    gw_xq = _mm(h1_16, dxq, ta=True, out_dtype=BF16, name="mm_gw_xq")
    gw_xk = _mm(mem, dxk, ta=True, out_dtype=BF16, name="mm_gw_xk")
    gw_xv = _mm(mem, dxv, ta=True, out_dtype=BF16, name="mm_gw_xv")
    token = grads_ready("attn", {"xo_w": gw_xo, "xq_w": gw_xq, "xk_w": gw_xk, "xv_w": gw_xv})
    dh1 = _mm(dxq, w["xq_w"], tb=True, name="mm_dh1")
    ds1, ds1_16, g["ln1_g"], g["ln1_b"] = _ln_bwd(x, mix, behind(w["ln1_g"], token), dh1, ds2, name="ln1_bwd")
    gw_out = _mm(mixin, ds1_16, ta=True, out_dtype=BF16, name="mm_gw_out")
    dmixin = _mm(ds1_16, w["w_out"], tb=True, name="mm_dmixin")
    dproj, gw_pool, g["pool_scale"] = _pool_bwd(proj, w["pool_w"], w["pool_scale"], dmixin)
    token = grads_ready("mix", {"w_out": gw_out, "pool_w": gw_pool})
    do_raw, dproj, g["gdn_norm_w"] = _onorm_bwd(o_raw, proj, behind(w["gdn_norm_w"], token), dmixin, dproj)
    cots = _gdn_state_bwd(*chunked, saved, do_raw)
    token = grads_ready("tick", {"after": cots[0]})
    dpost, dproj, g["alog_row"], g["dtb_row"] = _gdn_local_bwd(post, proj, behind(w["alog_row"], token), w["dtb_row"],
                                                               t_inv, cots, dproj)
    dproj, g["conv_w"] = _gdn_prep_bwd(proj, w["conv_w"], dpost, dproj)
    token = grads_ready("small", {**g, "sq": sq})
    gw_in = _mm(dproj, x16, ta=True, out_dtype=BF16, tm=768, tn=D_MODEL, after=token, name="mm_gw_in")
    token = grads_ready("in", {"w_in": gw_in})
    grad_x = _mm(dproj, w["w_in"], tk=1792, epi="add", extra=ds1, add_scale=ALPHA, after=token, name="mm_dx")
    return sq, grad_x, g


_VECTORS = ("a_log", "dt_bias", "gdn_norm_w", "pool_scale", "ln1_g", "ln1_b", "ln2_g", "ln2_b", "ln3_g", "ln3_b")
_BA_SPLIT = BA_OFF + 2 * GDN_HEADS


def _lane_row(v, offset):
    return jnp.zeros((1, LANE), F32).at[0, offset:offset + v.shape[0]].set(v)


_GROUP_VECTORS = {"mixer": (), "attn": ("ln1_g", "ln1_b", "ln2_g", "ln2_b"), "up": (), "down": ("ln3_g", "ln3_b")}


def _group_weights(group, full):
    w = {n: full[n].reshape(1, D_MODEL) for n in _GROUP_VECTORS[group]}
    if group == "mixer":
        w.update({
            "w_in": _w_in_padded(full["w_in"]),
            "conv_w": full["conv_w"],
            "alog_row": _lane_row(full["a_log"], GDN_HEADS),
            "dtb_row": _lane_row(full["dt_bias"], GDN_HEADS),
            "gdn_norm_w": full["gdn_norm_w"].reshape(1, LANE),
            "pool_w": full["pool_w"],
            "pool_scale": full["pool_scale"].reshape(POOL_GROUPS, 1, POOL_GROUP_DIM),
        })
    else:
        w.update({n: full[n] for n in dict(_GATHER_GROUPS)[group]})
    return w


def _w_in_row_map():
    per = IN_COLS // N_DEV
    gap = POOL_OFF - _BA_SPLIT
    pieces = []
    for d in range(N_DEV):
        lo, hi = d * per, (d + 1) * per
        if hi <= _BA_SPLIT:
            pieces.append([(0, lo, per)])
        elif lo >= _BA_SPLIT:
            pieces.append([(0, lo + gap, per)])
        else:
            pieces.append([(0, lo, _BA_SPLIT - lo), (_BA_SPLIT - lo, POOL_OFF, hi - _BA_SPLIT)])
    return pieces


_W_IN_LANES = 256


def _w_in_padded(blocks):
    def body(b_ref, o_ref):
        for d, pieces in enumerate(_w_in_row_map()):
            for src, dst, rows in pieces:
                o_ref[dst:dst + rows, :] = b_ref[d, src:src + rows, :]
        o_ref[_BA_SPLIT:POOL_OFF, :] = jnp.zeros((POOL_OFF - _BA_SPLIT, _W_IN_LANES), o_ref.dtype)

    n, per, cols = blocks.shape
    return pl.pallas_call(
        body, grid=(cols // _W_IN_LANES,), in_specs=[pl.BlockSpec((n, per, _W_IN_LANES), lambda j: (0, 0, j))],
        out_specs=pl.BlockSpec((PROJ_COLS, _W_IN_LANES), lambda j: (0, j)),
        out_shape=jax.ShapeDtypeStruct((PROJ_COLS, cols), blocks.dtype), compiler_params=_params("parallel"),
        name="w_in_padded")(blocks)


def _w_in_chunks(g):
    def body(g_ref, o_ref):
        for d, pieces in enumerate(_w_in_row_map()):
            for dst, src, rows in pieces:
                o_ref[d, dst:dst + rows, :] = g_ref[src:src + rows, :]

    cols = g.shape[1]
    per = IN_COLS // N_DEV
    return pl.pallas_call(
        body, grid=(cols // _W_IN_LANES,), in_specs=[pl.BlockSpec((PROJ_COLS, _W_IN_LANES), lambda j: (0, j))],
        out_specs=pl.BlockSpec((N_DEV, per, _W_IN_LANES), lambda j: (0, 0, j)),
        out_shape=jax.ShapeDtypeStruct((N_DEV, per, cols), g.dtype), compiler_params=_params("parallel"),
        name="w_in_chunks")(g)


def _finish_small_grads(g):
    out = {"conv_w": g["conv_w"]}
    out["a_log"] = g["alog_row"][0, GDN_HEADS:2 * GDN_HEADS]
    out["dt_bias"] = g["dtb_row"][0, GDN_HEADS:2 * GDN_HEADS]
    out["gdn_norm_w"] = g["gdn_norm_w"].reshape(LANE)
    out["pool_scale"] = g["pool_scale"].reshape(POOL_GROUPS * POOL_GROUP_DIM)
    for n in ("ln1_g", "ln1_b", "ln2_g", "ln2_b", "ln3_g", "ln3_b"):
        out[n] = g[n].reshape(D_MODEL)
    return out


def _adamw_math(w, g, m, v):
    m = ADAM_B1 * m + (1.0 - ADAM_B1) * g
    v = ADAM_B2 * v + (1.0 - ADAM_B2) * (g * g)
    m_hat = m / (1.0 - ADAM_B1 ** ADAM_STEP)
    v_hat = v / (1.0 - ADAM_B2 ** ADAM_STEP)
    delta = -ADAM_LR * (m_hat / (jnp.sqrt(v_hat) + ADAM_EPS) + ADAM_WD * w)
    return delta, m, v


ADAMW_TILE_ELEMS = 256 * 1024
CHIP_SUM_TILE_ELEMS = 1024 * 1024


def _shard_tile(r, c, elems):
    for rows in (1024, 512, 256, 128):
        if r % rows == 0 and rows * c <= elems:
            return rows, c
    if r % 128 == 0:
        return 128, c
    return r, 256 if c % 256 == 0 else c


def _adamw_shard(parts, own, me, w, m, v, *, name):
    s, r, c = parts.shape
    tr, tc = _shard_tile(r, c, ADAMW_TILE_ELEMS)
    assert r % tr == 0 and c % tc == 0, (name, r, c)
    unit_axis = w.ndim == 3
    at = (slice(None), 0, slice(None)) if unit_axis else Ellipsis

    def body(me_ref, p_ref, own_ref, w_ref, m_ref, v_ref, g_ref, d_ref, nm_ref, nv_ref):
        mine = own_ref[...].astype(F32)
        g = None
        for i in range(s):
            part = jnp.where(me_ref[0] == i, mine, p_ref[i].astype(F32))
            g = part if g is None else g + part
        delta, nm, nv = _adamw_math(w_ref[at], g, m_ref[at], v_ref[at])
        g_ref[at] = g
        d_ref[at] = delta
        nm_ref[at] = nm
        nv_ref[at] = nv

    if unit_axis:
        blk = pl.BlockSpec((tr, 1, tc), lambda i, j, me_ref: (i, 0, j))
        out = jax.ShapeDtypeStruct((r, 1, c), F32)
    else:
        blk = pl.BlockSpec((tr, tc), lambda i, j, me_ref: (i, j))
        out = jax.ShapeDtypeStruct((r, c), F32)
    return pl.pallas_call(
        body,
        grid_spec=pltpu.PrefetchScalarGridSpec(
            num_scalar_prefetch=1, grid=(r // tr, c // tc),
            in_specs=[pl.BlockSpec((s, tr, tc), lambda i, j, me_ref: (0, i, j)),
                      pl.BlockSpec((None, tr, tc), lambda i, j, me_ref: (me_ref[0], i, j)), blk, blk, blk],
            out_specs=[blk, blk, blk, blk]),
        out_shape=[out, out, out, out], compiler_params=_params("parallel", "parallel"), name=name,
    )(me, parts, own, w, m, v)


N_CHIPS = N_DEV // 2


def _chip_sums(chunks, from_sibling, core, *, name):
    _, r, c = chunks.shape
    tr, tc = _shard_tile(r, c, CHIP_SUM_TILE_ELEMS)
    assert r % tr == 0 and c % tc == 0, (name, r, c)

    def body(core_ref, mine_ref, other_ref, o_ref):
        o_ref[...] = (mine_ref[...].astype(F32) + other_ref[...].astype(F32)).astype(o_ref.dtype)

    by_chip = pl.BlockSpec((None, tr, tc), lambda q, i, j, core_ref: (q, i, j))
    return pl.pallas_call(
        body,
        grid_spec=pltpu.PrefetchScalarGridSpec(
            num_scalar_prefetch=1, grid=(N_CHIPS, r // tr, c // tc),
            in_specs=[pl.BlockSpec((None, tr, tc), lambda q, i, j, core_ref: (2 * q + core_ref[0], i, j)), by_chip],
            out_specs=by_chip),
        out_shape=jax.ShapeDtypeStruct((N_CHIPS, r, c), chunks.dtype),
        compiler_params=_params("parallel", "parallel", "parallel"), name=name,
    )(core, chunks, from_sibling)


def _place():
    return lax.axis_index("x"), lax.axis_index("y"), lax.axis_index("c")


def _slot(px, py, pc):
    return 4 * px + 2 * py + pc


_HBM = pl.BlockSpec(memory_space=pltpu.HBM)


_SEM = pl.BlockSpec(memory_space=pltpu.SEMAPHORE)
_ANY = pl.BlockSpec(memory_space=pl.ANY)
_EFFECT = pltpu.SideEffectType.DATAFLOW_SIDE_EFFECTING


def _peer(k, x, y, c):
    return (1 - x if k & 4 else x, 1 - y if k & 2 else y, 1 - c if k & 1 else c)


_EXCHANGE_BITS = {"gather_near": (1, 2, 4), "gather_relay": (6,), "gather_pass": (2, 4, 6),
                  "scatter_sibling": (1, 1, 1, 1), "scatter_chips": (2, 4, 6), "all_small": (1, 2, 3, 4, 5, 6, 7)}


def _exchange_copy(mode, src, land, w, i, place, send_sems, recv_sems, receiving):
    bits = _EXCHANGE_BITS[mode]
    k = bits[i]
    peer = _peer(k, *place)
    me = _slot(*place)
    if mode in ("gather_near", "all_small"):
        to, src_ref, sent_to, got_at = peer, src[w], me, _slot(*peer)
    elif mode == "gather_relay":
        x, y, c = place
        other = 1 - c
        to = (lax.bitwise_xor(x, c), lax.bitwise_xor(y, other), c)
        blk = _slot(lax.bitwise_xor(x, other), lax.bitwise_xor(y, c), c)
        src_ref, sent_to, got_at = land[w].at[blk], blk, _slot(*peer)
    elif mode == "gather_pass":
        blk = _slot(*peer)
        to, src_ref, sent_to, got_at = _peer(1, *place), land[w].at[blk], blk, _slot(*_peer(k | 1, *place))
    elif mode == "scatter_sibling":
        to, src_ref, sent_to, got_at = peer, src[w].at[2 * i + 1 - place[2]], i, i
    else:
        to, src_ref, sent_to, got_at = peer, src[w].at[_slot(*peer) // 2], me // 2, _slot(*peer) // 2
    sem = w * len(bits) + i
    return pltpu.make_async_remote_copy(
        src_ref=src_ref, dst_ref=land[w].at[got_at if receiving else sent_to], send_sem=send_sems.at[sem],
        recv_sem=recv_sems.at[sem], device_id=to, device_id_type=MESH)


def _exchange_start(mode, srcs, lands, after, *, name):
    ns, nl = len(srcs), len(lands)
    n_sem = nl * len(_EXCHANGE_BITS[mode])

    def body(*refs):
        src, land = refs[:ns], refs[ns:ns + nl]
        send_sems, recv_sems = refs[ns + nl + 1:ns + nl + 3]
        token = refs[-1]
        place = _place()
        for w in range(nl):
            for i in range(len(_EXCHANGE_BITS[mode])):
                _exchange_copy(mode, src, land, w, i, place, send_sems, recv_sems, receiving=False).start()
        token[...] = jnp.zeros_like(token)

    sems = pltpu.SemaphoreType.DMA((n_sem,))
    arrays = list(srcs) + list(lands)
    res = pl.pallas_call(
        body, name=name, in_specs=[_HBM] * (ns + nl) + [_ANY],
        out_specs=(_SEM, _SEM, *([_HBM] * (ns + nl)), pl.BlockSpec(memory_space=pltpu.VMEM)),
        out_shape=(sems, sems, *[pltpu.HBM(a.shape, a.dtype) for a in arrays], jax.ShapeDtypeStruct((8, LANE), F32)),
        input_output_aliases={i: 2 + i for i in range(ns + nl)},
        compiler_params=pltpu.CompilerParams(has_side_effects=_EFFECT),
    )(*[pltpu.with_memory_space_constraint(a, pltpu.HBM) for a in arrays], after)
    return res[0], res[1], list(res[2:2 + ns]), list(res[2 + ns:2 + ns + nl]), res[-1]


def _exchange_wait(mode, started, after, *, name):
    send_sems, recv_sems, srcs, lands, _ = started
    ns, nl = len(srcs), len(lands)

    def body(*refs):
        src, land = refs[:ns], refs[ns:ns + nl]
        send_sems, recv_sems = refs[ns + nl:ns + nl + 2]
        place = _place()
        for w in range(nl):
            for i in range(len(_EXCHANGE_BITS[mode])):
                cp = _exchange_copy(mode, src, land, w, i, place, send_sems, recv_sems, receiving=True)
                cp.wait_send()
                cp.wait_recv()

    arrays = list(srcs) + list(lands)
    res = pl.pallas_call(
        body, name=name, in_specs=[_HBM] * (ns + nl) + [_SEM, _SEM, _ANY], out_specs=[_HBM] * (ns + nl),
        out_shape=[pltpu.HBM(a.shape, a.dtype) for a in arrays],
        input_output_aliases={i: i for i in range(ns + nl)},
        compiler_params=pltpu.CompilerParams(has_side_effects=_EFFECT),
    )(*arrays, send_sems, recv_sems, after)
    return list(res[:ns]), list(res[ns:])


_SMALL_SEGMENTS = (("a_log", GDN_HEADS), ("dt_bias", GDN_HEADS), ("gdn_norm_w", HEAD_DIM), ("pool_scale", GDN_WIDTH),
                   ("ln1_g", D_MODEL), ("ln1_b", D_MODEL), ("ln2_g", D_MODEL), ("ln2_b", D_MODEL),
                   ("ln3_g", D_MODEL), ("ln3_b", D_MODEL), ("conv_w", CONV_K * QKV_COLS), ("loss", 1))
_SMALL_ROWS = 8
_SMALL_LEN = -(-sum(sz for _, sz in _SMALL_SEGMENTS) // (_SMALL_ROWS * LANE)) * LANE


def _pack_small(vals):
    parts = [vals[n].reshape(-1).astype(F32) if n in vals else jnp.zeros((sz,), F32) for n, sz in _SMALL_SEGMENTS]
    flat = jnp.concatenate(parts)
    flat = jnp.pad(flat, (0, _SMALL_ROWS * _SMALL_LEN - flat.shape[0]))
    return flat.reshape(_SMALL_ROWS, _SMALL_LEN)


def _unpack_small(vec):
    flat = vec.reshape(-1)
    out, off = {}, 0
    for n, sz in _SMALL_SEGMENTS:
        out[n] = flat[off:off + sz]
        off += sz
    return out


_WEIGHT_ORDER = ("w_in", "conv_w", "a_log", "dt_bias", "gdn_norm_w", "pool_w", "pool_scale", "w_out", "ln1_g", "ln1_b",
                 "xq_w", "xk_w", "xv_w", "xo_w", "ln2_g", "ln2_b", "w_up", "w_down", "ln3_g", "ln3_b")


def _shard2d(name, a):
    if name == "w_in":
        return a.T
    return a.reshape(-1, a.shape[-1]) if name == "pool_w" else a


def _update_view(name, a):
    return jnp.transpose(a, (2, 0, 1)) if name == "w_in" else _shard2d(name, a[0])


def _shard_result(name, r, shape):
    return jnp.transpose(r, (1, 2, 0)) if name == "w_in" else r.reshape(shape)


def _gathered_to_full(name, gth):
    if name in ("w_up", "w_in"):
        return gth
    if name == "conv_w":
        return jnp.transpose(gth, (1, 0, 2)).reshape(gth.shape[1], N_DEV * gth.shape[2])
    if name == "pool_w":
        g4 = gth.reshape(N_DEV, POOL_GROUPS, POOL_GROUP_DIM // N_DEV, POOL_GROUP_DIM)
        return jnp.transpose(g4, (1, 0, 2, 3)).reshape(POOL_GROUPS, POOL_GROUP_DIM, POOL_GROUP_DIM)
    return gth.reshape(N_DEV * gth.shape[1], gth.shape[2])


def _full_to_chunks(name, full):
    if name == "w_up":
        return full
    if name == "pool_w":
        g4 = full.reshape(POOL_GROUPS, N_DEV, POOL_GROUP_DIM // N_DEV, POOL_GROUP_DIM)
        return jnp.transpose(g4, (1, 0, 2, 3)).reshape(N_DEV, POOL_GROUPS * POOL_GROUP_DIM // N_DEV, POOL_GROUP_DIM)
    return full.reshape(N_DEV, full.shape[0] // N_DEV, full.shape[1])


_GATHER_GROUPS = (("mixer", ("w_in", "conv_w", "pool_w")), ("attn", ("w_out", "xq_w", "xk_w", "xv_w", "xo_w")),
                  ("up", ("w_up",)), ("down", ("w_down",)))


def _grad_chunks(name, g):
    if name == "w_in":
        return _w_in_chunks(g.astype(BF16))
    return _full_to_chunks(name, g.astype(BF16))


def kernel(x, mem, w_in, conv_w, a_log, dt_bias, gdn_norm_w, pool_w, pool_scale, w_out, ln1_g, ln1_b, xq_w, xk_w, xv_w, xo_w, ln2_g, ln2_b, w_up, w_down, ln3_g, ln3_b, loss_target, m_w_in, m_conv_w, m_a_log, m_dt_bias, m_gdn_norm_w, m_pool_w, m_pool_scale, m_w_out, m_ln1_g, m_ln1_b, m_xq_w, m_xk_w, m_xv_w, m_xo_w, m_ln2_g, m_ln2_b, m_w_up, m_w_down, m_ln3_g, m_ln3_b, v_w_in, v_conv_w, v_a_log, v_dt_bias, v_gdn_norm_w, v_pool_w, v_pool_scale, v_w_out, v_ln1_g, v_ln1_b, v_xq_w, v_xk_w, v_xv_w, v_xo_w, v_ln2_g, v_ln2_b, v_w_up, v_w_down, v_ln3_g, v_ln3_b):
    args = dict(locals())
    wt = {n: args[n][0] for n in _WEIGHT_ORDER}
    mo = {n: args["m_" + n][0] for n in _WEIGHT_ORDER}
    vo = {n: args["v_" + n][0] for n in _WEIGHT_ORDER}

    me = _slot(*_place())
    me_arr = jnp.reshape(me, (1,)).astype(jnp.int32)
    nothing = jnp.zeros((8, LANE), F32)

    def landing_zones(names):
        shards = [_shard2d(n, wt[n]).astype(F32 if n == "conv_w" else BF16) for n in names]
        zones = [lax.dynamic_update_slice(lax.empty((N_DEV, *s.shape), s.dtype), s[None], (me, 0, 0)) for s in shards]
        return shards, zones

    chip_arr = jnp.reshape(me // 2, (1,)).astype(jnp.int32)
    core_arr = jnp.reshape(lax.axis_index("c"), (1,)).astype(jnp.int32)
    names_of = dict(_GATHER_GROUPS)
    gathers = {}
    prepared = {}

    def gather_near(group, after):
        shards, zones = prepared.pop(group) if group in prepared else landing_zones(names_of[group])
        gathers[group] = _exchange_start("gather_near", shards, zones, after, name="gather_near_" + group)
        return gathers[group][4]

    def gather_next(group, was, now, after):
        _, zones = _exchange_wait(was, gathers[group], after, name=f"{was}_{group}_wait")
        gathers[group] = _exchange_start(now, [], zones, nothing, name=f"{now}_{group}")
        return gathers[group][4]

    def gather_relay(group, after):
        return gather_next(group, "gather_near", "gather_relay", after)

    def gather_pass(group, after):
        return gather_next(group, "gather_relay", "gather_pass", after)

    def gathered(group, after):
        _, zones = _exchange_wait("gather_pass", gathers[group], after, name=f"gather_pass_{group}_wait")
        full = {n: _gathered_to_full(n, z) for n, z in zip(names_of[group], zones)}
        full.update({n: wt[n] for n in _VECTORS})
        return _group_weights(group, full)

    token = gather_near("mixer", nothing)
    x16 = _cast_bf16(x[0], name="cast_x")
    later = {group: landing_zones(names_of[group]) for group in ("attn", "up", "down")}
    token, x16, later = lax.optimization_barrier((token, x16, later))
    prepared.update(later)
    token = gather_pass("mixer", gather_relay("mixer", token))
    token = gather_near("attn", token)

    def weights_of(group, after):
        if group == "mixer":
            return gathered(group, token)
        if group == "ahead_conv":
            return gather_near("up", gather_relay("attn", after))[0:1, 0:1]
        if group == "ahead_scan":
            return gather_pass("attn", after)[0:1, 0:1]
        if group == "attn":
            return gathered(group, gather_near("down", gather_relay("up", after)))
        if group == "ahead_attn":
            return gather_relay("down", gather_pass("up", after))[0:1, 0:1]
        if group == "up":
            return gathered(group, gather_pass("down", after))
        return gathered(group, after)

    scatters = {}
    in_flight = []

    def chip_stage(after):
        group, names, started = in_flight.pop()
        chunks, from_sibling = _exchange_wait("scatter_sibling", started, after, name=f"scatter_sibling_{group}_wait")
        sums = [_chip_sums(c, f, core_arr, name=f"chip_sums_{n}") for n, c, f in zip(names, chunks, from_sibling)]
        scatters[group] = (names, _exchange_start("scatter_chips", sums, [lax.empty(s.shape, s.dtype) for s in sums],
                                                  nothing, name="scatter_chips_" + group))
        return scatters[group][1][4]

    small_sent = []

    def grads_ready(group, grads):
        if group == "tick":
            return chip_stage(grads["after"])[0:1, 0:1] if in_flight else None
        if group == "small":
            small = _finish_small_grads(grads)
            small["loss"] = 0.5 * grads["sq"][0:1, 0] / D_MODEL
            packed = _pack_small(small)
            zone = lax.empty((N_DEV, *packed.shape), F32)
            small_sent.append(_exchange_start("all_small", [packed], [zone], nothing, name="small_grads_start"))
            return small_sent[0][4][0:1, 0:1]
        names = tuple(grads)
        chunks = [_grad_chunks(n, grads[n]) for n in names]
        token = chip_stage(chunks[0]) if in_flight else nothing
        zones = [lax.empty((N_CHIPS, *c.shape[1:]), c.dtype) for c in chunks]
        started = _exchange_start("scatter_sibling", chunks, zones, token, name="scatter_sibling_" + group)
        in_flight.append((group, names, started))
        return started[4][0:1, 0:1]

    sq, grad_x, g = _local_step(x[0], x16, mem[0], loss_target[0], weights_of, grads_ready)

    out = {}
    after = chip_stage(grad_x)
    for group, (names, started) in scatters.items():
        sums, lands = _exchange_wait("scatter_chips", started, after, name=f"scatter_chips_{group}_wait")
        for n, parts, own in zip(names, lands, sums):
            res = _adamw_shard(parts, own, chip_arr, _update_view(n, args[n]), _update_view(n, args["m_" + n]),
                               _update_view(n, args["v_" + n]), name="adamw_" + n)
            out[n] = [_shard_result(n, r, args[n].shape) for r in res]
            after = res[1]

    (packed,), (zone,) = _exchange_wait("all_small", small_sent[0], after, name="small_grads_wait")
    gs, ds, ms, vs = _adamw_shard(
        zone, jnp.broadcast_to(packed, zone.shape), me_arr, _pack_small({n: wt[n] for n in _VECTORS}),
        _pack_small({n: mo[n] for n in _VECTORS}), _pack_small({n: vo[n] for n in _VECTORS}), name="adamw_small")
    gs, ds, ms, vs = _unpack_small(gs), _unpack_small(ds), _unpack_small(ms), _unpack_small(vs)
    cols = conv_w.shape[-1]
    conv_full = gs["conv_w"].reshape(CONV_K, QKV_COLS)
    conv_mine = lax.dynamic_slice(conv_full, (0, me * cols), (CONV_K, cols))[None]
    res = _adamw_shard(conv_mine, conv_mine, jnp.zeros((1,), jnp.int32), wt["conv_w"], mo["conv_w"], vo["conv_w"],
                       name="adamw_conv_w")
    out["conv_w"] = [r.reshape(conv_w.shape) for r in res]
    for n in _VECTORS:
        out[n] = [t[n].reshape(args[n].shape) for t in (gs, ds, ms, vs)]

    return (gs["loss"][0], grad_x[None], *[out[n][0] for n in _WEIGHT_ORDER], *[out[n][1] for n in _WEIGHT_ORDER],
            *[out[n][2] for n in _WEIGHT_ORDER], *[out[n][3] for n in _WEIGHT_ORDER])
```

```python
import jax
import jax.numpy as jnp
from jax import lax
from jax.experimental import pallas as pl
from jax.experimental.pallas import tpu as pltpu

F32 = jnp.float32
BF16 = jnp.bfloat16
MESH = pl.DeviceIdType.MESH

N_DEV = 8
D_MODEL = 2048
GDN_WIDTH = 1024
GDN_HEADS = 8
HEAD_DIM = 128
CONV_K = 4
CHUNK = 64
POOL_GROUPS = 4
POOL_GROUP_DIM = 256
MEM_LEN = 256
XATTN_HEADS = 4
XATTN_HEAD_DIM = 512
D_FF = 8192
IN_COLS = 5136
ALPHA = 2.0 ** 0.25
LN_EPS = 1e-5
NORM_EPS = 1e-6

LANE = 128
QKV_COLS = 3 * GDN_WIDTH
Z_OFF = QKV_COLS
BA_OFF = 4 * GDN_WIDTH
POOL_OFF = BA_OFF + 2 * LANE
PROJ_COLS = POOL_OFF + GDN_WIDTH
BA_BLK = BA_OFF // LANE
POOL_BLK = POOL_OFF // POOL_GROUP_DIM

ADAM_LR = 0.001
ADAM_B1 = 0.9
ADAM_B2 = 0.999
ADAM_EPS = 1e-08
ADAM_WD = 0.01
ADAM_STEP = 10

VMEM_LIMIT_BYTES = 48 * 1024 * 1024


def _params(*sem):
    return pltpu.CompilerParams(dimension_semantics=sem if sem else None, vmem_limit_bytes=VMEM_LIMIT_BYTES)


def _make_dots(cast, precision, batched=False):
    lead = 1 if batched else 0
    batch = ((0,), (0,)) if batched else ((), ())

    def dg(a, b, ca, cb):
        if cast is not None:
            a = a.astype(cast)
            b = b.astype(cast)
        return lax.dot_general(a, b, (((ca + lead,), (cb + lead,)), batch), precision=precision, preferred_element_type=F32)

    def nn_(a, b):
        return dg(a, b, 1, 0)

    def nt_(a, b):
        return dg(a, b, 1, 1)

    def tn_(a, b):
        return dg(a, b, 0, 0)

    @jax.custom_vjp
    def nn(a, b):
        return nn_(a, b)

    nn.defvjp(lambda a, b: (nn_(a, b), (a, b)), lambda r, g: (nt_(g, r[1]), tn_(r[0], g)))

    @jax.custom_vjp
    def nt(a, b):
        return nt_(a, b)

    nt.defvjp(lambda a, b: (nt_(a, b), (a, b)), lambda r, g: (nn_(g, r[1]), tn_(g, r[0])))

    @jax.custom_vjp
    def tn(a, b):
        return tn_(a, b)

    tn.defvjp(lambda a, b: (tn_(a, b), (a, b)), lambda r, g: (nt_(r[1], g), nn_(r[0], g)))

    return (nn_, nt_, tn_), (nn, nt, tn)


_BDOT_PLAIN, _BDOT_VJP = _make_dots(BF16, None)
_BDOT_BATCH_PLAIN, _BDOT_BATCH_VJP = _make_dots(BF16, None, batched=True)
_FDOT_BATCH_PLAIN, _FDOT_BATCH_VJP = _make_dots(BF16, None, batched=True)


def _mm(a, b, *, ta=False, tb=False, out_dtype=F32, tm=None, tn=512, tk=None, epi=None, extra=None, add_scale=1.0,
        b_chunks=False, o_chunks=False, after=None, name):
    m, k = (a.shape[1], a.shape[0]) if ta else a.shape
    if b_chunks:
        n, kb = (b.shape[1], N_DEV * b.shape[2]) if tb else (N_DEV * b.shape[2], b.shape[1])
    else:
        n, kb = b.shape if tb else (b.shape[1], b.shape[0])
    assert kb == k, (name, a.shape, b.shape)
    tm, tn, tk = min(tm or m, m), min(tn, n), min(tk or k, k)
    assert m % tm == 0 and n % tn == 0 and k % tk == 0, (name, m, n, k)
    nk = k // tk
    dims = (((0 if ta else 1,), (1 if tb else 0,)), ((), ()))
    n_extra = 0 if epi in (None, "relu2") else 1
    n_out = 2 if epi == "relu2" else 1
    if epi in ("relu2", "mul2r"):
        out_dtype = BF16
    n_after = 0 if after is None else 1

    def body(*refs):
        a_ref, b_ref = refs[:2]
        c_ref = refs[2] if n_extra else None
        o_refs = refs[2 + n_extra + n_after:2 + n_extra + n_after + n_out]
        scr = refs[2 + n_extra + n_after + n_out:]
        r = lax.dot_general(a_ref[...].astype(BF16), b_ref[...].astype(BF16), dims, preferred_element_type=F32)

        def finish(v):
            if epi == "add":
                o_refs[0][...] = (v + add_scale * c_ref[...]).astype(out_dtype)
            elif epi == "relu2":
                p = jnp.maximum(v, 0.0)
                o_refs[0][...] = (p * p).astype(BF16)
                o_refs[1][...] = p.astype(BF16)
            elif epi == "mul2r":
                o_refs[0][...] = (v * (2.0 * c_ref[...].astype(F32))).astype(BF16)
            else:
                o_refs[0][...] = v.astype(out_dtype)

        if nk == 1:
            finish(r)
        else:
            acc = scr[0]
            kk = pl.program_id(2)

            @pl.when(kk == 0)
            def _():
                acc[...] = r

            @pl.when(kk > 0)
            def _():
                acc[...] += r

            @pl.when(kk == nk - 1)
            def _():
                finish(acc[...])

    a_spec = pl.BlockSpec((tk, tm), lambda i, j, kk: (kk, i)) if ta else pl.BlockSpec((tm, tk), lambda i, j, kk: (i, kk))
    if b_chunks and tb:
        kc = k // N_DEV // tk
        b_spec = pl.BlockSpec((None, tn, tk), lambda i, j, kk: (kk // kc, j, kk % kc))
    elif b_chunks:
        nc = n // N_DEV // tn
        b_spec = pl.BlockSpec((None, tk, tn), lambda i, j, kk: (j // nc, kk, j % nc))
    elif tb:
        b_spec = pl.BlockSpec((tn, tk), lambda i, j, kk: (j, kk))
    else:
        b_spec = pl.BlockSpec((tk, tn), lambda i, j, kk: (kk, j))
    mn_spec = pl.BlockSpec((tm, tn), lambda i, j, kk: (i, j))
    if o_chunks:
        oc = n // N_DEV // tn
        o_spec = pl.BlockSpec((None, tm, tn), lambda i, j, kk: (j // oc, i, j % oc))
        o_shape = jax.ShapeDtypeStruct((N_DEV, m, n // N_DEV), out_dtype)
    else:
        o_spec, o_shape = mn_spec, jax.ShapeDtypeStruct((m, n), out_dtype)
    res = pl.pallas_call(
        body, grid=(m // tm, n // tn, nk),
        in_specs=[a_spec, b_spec] + [mn_spec] * n_extra + [pl.BlockSpec(memory_space=pl.ANY)] * n_after,
        out_specs=[o_spec] * n_out, out_shape=[o_shape] * n_out,
        scratch_shapes=[pltpu.VMEM((tm, tn), F32)] if nk > 1 else [],
        compiler_params=_params("parallel", "parallel", "arbitrary"), name=name,
    )(a, b, *([extra] if n_extra else []), *([after] if n_after else []))
    return res if n_out > 1 else res[0]


def _cast_bf16(v, *, name, tm=512):
    t, d = v.shape
    tm = min(tm, t)

    def body(v_ref, o_ref):
        o_ref[...] = v_ref[...].astype(BF16)

    spec = pl.BlockSpec((tm, d), lambda i: (i, 0))
    return pl.pallas_call(body, grid=(t // tm,), in_specs=[spec], out_specs=spec,
                          out_shape=jax.ShapeDtypeStruct((t, d), BF16), compiler_params=_params("parallel"), name=name)(v)


def _shift_down(v, s):
    if s == 0:
        return v
    row = lax.broadcasted_iota(jnp.int32, v.shape, 0)
    return jnp.where(row >= s, pltpu.roll(v, s, axis=0), 0.0)


def _shift_up(v, s):
    if s == 0:
        return v
    t = v.shape[0]
    row = lax.broadcasted_iota(jnp.int32, v.shape, 0)
    return jnp.where(row < t - s, pltpu.roll(v, t - s, axis=0), 0.0)


def _post_col(j):
    return (j % GDN_HEADS) * 3 + j // GDN_HEADS


def _gdn_prep_fwd(proj, conv_w):
    t = proj.shape[0]

    def body(x_ref, w_ref, o_ref):
        j = pl.program_id(0)
        x = x_ref[...]
        y = jnp.zeros_like(x)
        for tap in range(CONV_K):
            y = y + w_ref[tap:tap + 1, :] * _shift_down(x, CONV_K - 1 - tap)
        c = y * jax.nn.sigmoid(y)
        nrm = c * lax.rsqrt(jnp.sum(c * c, axis=1, keepdims=True) + NORM_EPS)
        o_ref[...] = jnp.where(j < 2 * GDN_HEADS, nrm, c)

    return pl.pallas_call(
        body, grid=(QKV_COLS // LANE,),
        in_specs=[pl.BlockSpec((t, LANE), lambda j: (0, j)), pl.BlockSpec((CONV_K, LANE), lambda j: (0, j))],
        out_specs=pl.BlockSpec((t, LANE), lambda j: (0, _post_col(j))),
        out_shape=jax.ShapeDtypeStruct((t, QKV_COLS), F32),
        compiler_params=_params("parallel"), name="gdn_prep_fwd",
    )(proj, conv_w)


def _gdn_prep_bwd(proj, conv_w, dpost, dproj):
    t = proj.shape[0]

    def body(x_ref, w_ref, d_ref, _, dx_ref, dw_ref):
        j = pl.program_id(0)
        x = x_ref[...]
        xs = [_shift_down(x, CONV_K - 1 - tap) for tap in range(CONV_K)]
        y = jnp.zeros_like(x)
        for tap in range(CONV_K):
            y = y + w_ref[tap:tap + 1, :] * xs[tap]
        sig = jax.nn.sigmoid(y)
        c = y * sig
        r = lax.rsqrt(jnp.sum(c * c, axis=1, keepdims=True) + NORM_EPS)
        nrm = c * r
        d = d_ref[...]
        dc_norm = r * (d - nrm * jnp.sum(d * nrm, axis=1, keepdims=True))
        dc = jnp.where(j < 2 * GDN_HEADS, dc_norm, d)
        dy = dc * (sig * (1.0 + y * (1.0 - sig)))
        dx = jnp.zeros_like(x)
        for tap in range(CONV_K):
            dx = dx + _shift_up(w_ref[tap:tap + 1, :] * dy, CONV_K - 1 - tap)
            dw_ref[tap:tap + 1, :] = jnp.sum(dy * xs[tap], axis=0, keepdims=True)
        dx_ref[...] = dx.astype(dx_ref.dtype)

    return pl.pallas_call(
        body, grid=(QKV_COLS // LANE,),
        in_specs=[pl.BlockSpec((t, LANE), lambda j: (0, j)), pl.BlockSpec((CONV_K, LANE), lambda j: (0, j)),
                  pl.BlockSpec((t, LANE), lambda j: (0, _post_col(j))), pl.BlockSpec(memory_space=pl.ANY)],
        out_specs=[pl.BlockSpec((t, LANE), lambda j: (0, j)), pl.BlockSpec((CONV_K, LANE), lambda j: (0, j))],
        out_shape=[jax.ShapeDtypeStruct(dproj.shape, dproj.dtype), jax.ShapeDtypeStruct((CONV_K, QKV_COLS), F32)],
        input_output_aliases={3: 0},
        compiler_params=_params("parallel"), name="gdn_prep_bwd",
    )(proj, conv_w, dpost, dproj)


def _softplus(v):
    return jnp.maximum(v, 0.0) + jnp.log(1.0 + jnp.exp(-jnp.abs(v)))


def _tri_inv(low, nn):
    r = lax.broadcasted_iota(jnp.int32, (CHUNK, CHUNK), 0)
    c = lax.broadcasted_iota(jnp.int32, (CHUNK, CHUNK), 1)
    eye = (r == c).astype(F32)
    same_blk = lax.shift_right_logical(r, 4) == lax.shift_right_logical(c, 4)
    diag = jnp.where(same_blk, low, 0.0)
    off = low - diag
    n1 = -diag
    n2 = nn(n1, n1)
    n4 = nn(n2, n2)
    n8 = nn(n4, n4)
    inv_d = nn(nn(nn(eye + n1, eye + n2), eye + n4), eye + n8)
    m1 = nn(inv_d, off)
    m2 = nn(m1, m1)
    return nn(nn(eye - m1, eye + m2), inv_d)


@jax.custom_vjp
def _tri_inv_known(low, t_inv):
    return t_inv


def _tri_inv_known_fwd(low, t_inv):
    return t_inv, t_inv


def _tri_inv_known_bwd(t_inv, g):
    _, nt, tn = _FDOT_BATCH_PLAIN
    return -nt(tn(t_inv, g), t_inv), jnp.zeros_like(t_inv)


_tri_inv_known.defvjp(_tri_inv_known_fwd, _tri_inv_known_bwd)


LOCAL_HEADS_PER_STEP = 8


def _gdn_local_fn(qkv, ba, alog_row, dtb_row, first_head, bdots, fdots, t_known=None):
    nn, nt, tn = bdots
    fnn = fdots[0]
    n_heads = qkv.shape[1] // (3 * HEAD_DIM)
    part = lambda i, p: qkv[:, (3 * i + p) * HEAD_DIM:(3 * i + p + 1) * HEAD_DIM]
    q = jnp.stack([part(i, 0) for i in range(n_heads)]) * (HEAD_DIM ** -0.5)
    k = jnp.stack([part(i, 1) for i in range(n_heads)])
    v = jnp.stack([part(i, 2) for i in range(n_heads)])
    lane = lax.broadcasted_iota(jnp.int32, ba.shape, 1)
    bg = jnp.where(lane < GDN_HEADS, jax.nn.sigmoid(ba), -jnp.exp(alog_row) * _softplus(ba + dtb_row))
    pick = lambda l: jnp.sum(jnp.where(lane == l, bg, 0.0), axis=1, keepdims=True)
    beta = jnp.stack([pick(first_head + i) for i in range(n_heads)])
    g = jnp.stack([pick(first_head + i + GDN_HEADS) for i in range(n_heads)])

    r = lax.broadcasted_iota(jnp.int32, (CHUNK, CHUNK), 0)
    c = lax.broadcasted_iota(jnp.int32, (CHUNK, CHUNK), 1)
    incl = r >= c
    strict = r > c
    eye = r == c

    def to_row(col):
        return jnp.sum(jnp.where(eye, col, 0.0), axis=1, keepdims=True)

    gc = jnp.sum(jnp.where(incl, to_row(g), 0.0), axis=2, keepdims=True)
    diff = gc - to_row(gc)
    decay = jnp.where(incl, jnp.exp(jnp.where(incl, diff, 0.0)), 0.0)
    k_beta = k * beta
    v_beta = v * beta
    low = jnp.where(strict, nt(k_beta, k) * decay, 0.0)
    t_inv = _tri_inv(low, fnn) if t_known is None else _tri_inv_known(low, t_known)
    eg = jnp.exp(gc)
    u = fnn(t_inv, v_beta)
    w = fnn(t_inv, k_beta * eg)
    attn = jnp.where(incl, nt(q, k) * decay, 0.0)
    last = lax.broadcasted_iota(jnp.int32, (CHUNK, 1), 0) == CHUNK - 1
    g_last = jnp.sum(jnp.where(last, gc, 0.0), axis=1, keepdims=True)
    kdec = k * jnp.exp(g_last - gc)
    elast = jnp.broadcast_to(jnp.exp(g_last), (n_heads, 1, LANE))
    return u, w, q * eg, kdec, attn, elast, t_inv


def _gdn_state_fn(u, w, qg, kdec, attn, elast, state, bdots):
    nn, _, tn = bdots
    v_new = u - nn(w, state)
    o = nn(qg, state) + nn(attn, v_new)
    return o, state * elast + tn(kdec, v_new)


def _gdn_local_fwd(post, proj, alog_row, dtb_row):
    t = post.shape[0]
    n_chunks = t // CHUNK
    hb = LOCAL_HEADS_PER_STEP

    def body(qkv_ref, ba_ref, al_ref, dt_ref, u_ref, w_ref, qg_ref, kd_ref, at_ref, el_ref, ti_ref):
        u, w, qg, kdec, attn, elast, t_inv = _gdn_local_fn(qkv_ref[...], ba_ref[...], al_ref[...], dt_ref[...],
                                                           pl.program_id(1) * hb, _BDOT_BATCH_PLAIN, _FDOT_BATCH_PLAIN)
        for i in range(hb):
            cols = slice(i * HEAD_DIM, (i + 1) * HEAD_DIM)
            u_ref[:, cols] = u[i]
            w_ref[:, cols] = w[i].astype(BF16)
            qg_ref[:, cols] = qg[i].astype(BF16)
            kd_ref[:, cols] = kdec[i].astype(BF16)
        at_ref[...] = attn.astype(BF16)
        el_ref[:, 0] = elast
        ti_ref[...] = t_inv

    wide = pl.BlockSpec((CHUNK, hb * HEAD_DIM), lambda n, j: (n, j))
    square = pl.BlockSpec((hb, CHUNK, CHUNK), lambda n, j: (j, n, 0))
    row = pl.BlockSpec((1, LANE), lambda n, j: (0, 0))
    res = pl.pallas_call(
        body, grid=(n_chunks, GDN_HEADS // hb),
        in_specs=[pl.BlockSpec((CHUNK, hb * 3 * HEAD_DIM), lambda n, j: (n, j)),
                  pl.BlockSpec((CHUNK, LANE), lambda n, j: (n, BA_BLK)), row, row],
        out_specs=[wide, wide, wide, wide, square, pl.BlockSpec((hb, 1, 1, LANE), lambda n, j: (j, n, 0, 0)), square],
        out_shape=[jax.ShapeDtypeStruct((t, GDN_WIDTH), F32), jax.ShapeDtypeStruct((t, GDN_WIDTH), BF16),
                   jax.ShapeDtypeStruct((t, GDN_WIDTH), BF16), jax.ShapeDtypeStruct((t, GDN_WIDTH), BF16),
                   jax.ShapeDtypeStruct((GDN_HEADS, t, CHUNK), BF16),
                   jax.ShapeDtypeStruct((GDN_HEADS, n_chunks, 1, LANE), F32),
                   jax.ShapeDtypeStruct((GDN_HEADS, t, CHUNK), F32)],
        compiler_params=_params("parallel", "parallel"), name="gdn_local_fwd",
    )(post, proj, alog_row, dtb_row)
    return tuple(res[:6]), res[6]


def _by_head(ref):
    return jnp.stack([ref[:, h * HEAD_DIM:(h + 1) * HEAD_DIM] for h in range(ref.shape[1] // HEAD_DIM)])


def _gdn_state_specs(n_of):
    wide = pl.BlockSpec((CHUNK, GDN_WIDTH), lambda n: (n_of(n), 0))
    attn = pl.BlockSpec((GDN_HEADS, CHUNK, CHUNK), lambda n: (0, n_of(n), 0))
    elast = pl.BlockSpec((GDN_HEADS, 1, 1, LANE), lambda n: (0, n_of(n), 0, 0))
    saved = pl.BlockSpec((GDN_HEADS, 1, HEAD_DIM, HEAD_DIM), lambda n: (0, n_of(n), 0, 0))
    return wide, attn, elast, saved


def _gdn_state_fwd(u, w, qg, kdec, attn, elast):
    t = u.shape[0]
    n_chunks = t // CHUNK

    def body(u_ref, w_ref, qg_ref, kd_ref, at_ref, el_ref, o_ref, save_ref, state_ref):
        @pl.when(pl.program_id(0) == 0)
        def _():
            state_ref[...] = jnp.zeros_like(state_ref)

        state = state_ref[...]
        save_ref[:, 0] = state
        o, new_state = _gdn_state_fn(_by_head(u_ref), _by_head(w_ref), _by_head(qg_ref), _by_head(kd_ref), at_ref[...],
                                     el_ref[:, 0], state, _BDOT_BATCH_PLAIN)
        for h in range(GDN_HEADS):
            o_ref[:, h * HEAD_DIM:(h + 1) * HEAD_DIM] = o[h]
        state_ref[...] = new_state

    wide, attn_spec, elast_spec, saved_spec = _gdn_state_specs(lambda n: n)
    return pl.pallas_call(
        body, grid=(n_chunks,), in_specs=[wide, wide, wide, wide, attn_spec, elast_spec],
        out_specs=[wide, saved_spec],
        out_shape=[jax.ShapeDtypeStruct((t, GDN_WIDTH), F32),
                   jax.ShapeDtypeStruct((GDN_HEADS, n_chunks, HEAD_DIM, HEAD_DIM), F32)],
        scratch_shapes=[pltpu.VMEM((GDN_HEADS, HEAD_DIM, HEAD_DIM), F32)],
        compiler_params=_params("arbitrary"), name="gdn_state_fwd",
    )(u, w, qg, kdec, attn, elast)


def _gdn_state_bwd(u, w, qg, kdec, attn, elast, saved, do):
    t = u.shape[0]
    n_chunks = t // CHUNK
    last = n_chunks - 1

    def body(u_ref, w_ref, qg_ref, kd_ref, at_ref, el_ref, save_ref, do_ref,
             du_ref, dw_ref, dqg_ref, dkd_ref, dat_ref, del_ref, dstate_ref):
        @pl.when(pl.program_id(0) == 0)
        def _():
            dstate_ref[...] = jnp.zeros_like(dstate_ref)

        _, vjp = jax.vjp(
            lambda *a: _gdn_state_fn(*a, _BDOT_BATCH_VJP), _by_head(u_ref), _by_head(w_ref).astype(F32),
            _by_head(qg_ref).astype(F32), _by_head(kd_ref).astype(F32), at_ref[...].astype(F32), el_ref[:, 0],
            save_ref[:, 0])
        du, dw, dqg, dkd, dat, de, dstate = vjp((_by_head(do_ref), dstate_ref[...]))
        for h in range(GDN_HEADS):
            cols = slice(h * HEAD_DIM, (h + 1) * HEAD_DIM)
            du_ref[:, cols] = du[h]
            dw_ref[:, cols] = dw[h]
            dqg_ref[:, cols] = dqg[h]
            dkd_ref[:, cols] = dkd[h]
        dat_ref[...] = dat
        del_ref[:, 0] = de
        dstate_ref[...] = dstate

    wide, attn_spec, elast_spec, saved_spec = _gdn_state_specs(lambda n: last - n)
    wide_f32 = jax.ShapeDtypeStruct((t, GDN_WIDTH), F32)
    return pl.pallas_call(
        body, grid=(n_chunks,), in_specs=[wide, wide, wide, wide, attn_spec, elast_spec, saved_spec, wide],
        out_specs=[wide, wide, wide, wide, attn_spec, elast_spec],
        out_shape=[wide_f32, wide_f32, wide_f32, wide_f32, jax.ShapeDtypeStruct((GDN_HEADS, t, CHUNK), F32),
                   jax.ShapeDtypeStruct((GDN_HEADS, n_chunks, 1, LANE), F32)],
        scratch_shapes=[pltpu.VMEM((GDN_HEADS, HEAD_DIM, HEAD_DIM), F32)],
        compiler_params=_params("arbitrary"), name="gdn_state_bwd",
    )(u, w, qg, kdec, attn, elast, saved, do)


def _gdn_local_bwd(post, proj, alog_row, dtb_row, t_inv, cots, dproj):
    t = post.shape[0]
    n_chunks = t // CHUNK
    hb = LOCAL_HEADS_PER_STEP
    n_steps = GDN_HEADS // hb

    def body(qkv_ref, ba_ref, al_ref, dt_ref, ti_ref, du_ref, dw_ref, dqg_ref, dkd_ref, dat_ref, del_ref, _,
             dqkv_ref, dba_ref, dal_ref, ddt_ref, dba_acc):
        n = pl.program_id(0)
        j = pl.program_id(1)

        @pl.when((n == 0) & (j == 0))
        def _():
            dal_ref[...] = jnp.zeros_like(dal_ref)
            ddt_ref[...] = jnp.zeros_like(ddt_ref)

        @pl.when(j == 0)
        def _():
            dba_acc[...] = jnp.zeros_like(dba_acc)

        t_known = ti_ref[...]
        _, vjp = jax.vjp(
            lambda a, b, c, d: _gdn_local_fn(a, b, c, d, j * hb, _BDOT_BATCH_VJP, _FDOT_BATCH_VJP, t_known)[:6],
            qkv_ref[...], ba_ref[...], al_ref[...], dt_ref[...])
        dqkv, dba, dal, ddt = vjp((_by_head(du_ref), _by_head(dw_ref), _by_head(dqg_ref), _by_head(dkd_ref), dat_ref[...],
                                   del_ref[:, 0]))
        dqkv_ref[...] = dqkv
        dba_acc[...] += dba
        dal_ref[...] += dal
        ddt_ref[...] += ddt

        @pl.when(j == n_steps - 1)
        def _():
            dba_ref[:, 0:LANE] = dba_acc[...].astype(dba_ref.dtype)
            dba_ref[:, LANE:2 * LANE] = jnp.zeros((CHUNK, LANE), dba_ref.dtype)

    wide = pl.BlockSpec((CHUNK, hb * HEAD_DIM), lambda n, j: (n, j))
    qkv_spec = pl.BlockSpec((CHUNK, hb * 3 * HEAD_DIM), lambda n, j: (n, j))
    row = pl.BlockSpec((1, LANE), lambda n, j: (0, 0))
    return pl.pallas_call(
        body, grid=(n_chunks, n_steps),
        in_specs=[qkv_spec, pl.BlockSpec((CHUNK, LANE), lambda n, j: (n, BA_BLK)), row, row,
                  pl.BlockSpec((hb, CHUNK, CHUNK), lambda n, j: (j, n, 0)), wide, wide, wide, wide,
                  pl.BlockSpec((hb, CHUNK, CHUNK), lambda n, j: (j, n, 0)),
                  pl.BlockSpec((hb, 1, 1, LANE), lambda n, j: (j, n, 0, 0)), pl.BlockSpec(memory_space=pl.ANY)],
        out_specs=[qkv_spec, pl.BlockSpec((CHUNK, 2 * LANE), lambda n, j: (n, BA_BLK // 2)), row, row],
        out_shape=[jax.ShapeDtypeStruct((t, QKV_COLS), F32), jax.ShapeDtypeStruct(dproj.shape, dproj.dtype),
                   jax.ShapeDtypeStruct((1, LANE), F32), jax.ShapeDtypeStruct((1, LANE), F32)],
        input_output_aliases={11: 1},
        scratch_shapes=[pltpu.VMEM((CHUNK, LANE), F32)],
        compiler_params=_params("arbitrary", "arbitrary"), name="gdn_local_bwd",
    )(post, proj, alog_row, dtb_row, t_inv, *cots, dproj)


def _onorm_fn(o, z, w):
    return o * lax.rsqrt(jnp.mean(o * o, axis=1, keepdims=True) + NORM_EPS) * w * (z * jax.nn.sigmoid(z))


_Z_WIDE_BLK = Z_OFF // GDN_WIDTH


def _onorm_fwd(o_raw, proj, norm_w, mixin, tm=256):
    t = o_raw.shape[0]
    tm = min(tm, t)

    def body(o_ref, z_ref, w_ref, _, out_ref):
        for h in range(GDN_HEADS):
            cols = slice(h * HEAD_DIM, (h + 1) * HEAD_DIM)
            out_ref[:, cols] = _onorm_fn(o_ref[:, cols], z_ref[:, cols], w_ref[...]).astype(out_ref.dtype)

    wide = pl.BlockSpec((tm, GDN_WIDTH), lambda i: (i, 0))
    return pl.pallas_call(
        body, grid=(t // tm,),
        in_specs=[wide, pl.BlockSpec((tm, GDN_WIDTH), lambda i: (i, _Z_WIDE_BLK)), pl.BlockSpec((1, LANE), lambda i: (0, 0)),
                  pl.BlockSpec(memory_space=pl.ANY)],
        out_specs=wide, out_shape=jax.ShapeDtypeStruct(mixin.shape, mixin.dtype), input_output_aliases={3: 0},
        compiler_params=_params("parallel"), name="gdn_onorm_fwd",
    )(o_raw, proj, norm_w, mixin)


def _onorm_bwd(o_raw, proj, norm_w, dmixin, dproj, tm=256):
    t = o_raw.shape[0]
    tm = min(tm, t)

    def body(o_ref, z_ref, w_ref, d_ref, _, do_ref, dz_ref, dw_ref):
        @pl.when(pl.program_id(0) == 0)
        def _():
            dw_ref[...] = jnp.zeros_like(dw_ref)

        for h in range(GDN_HEADS):
            cols = slice(h * HEAD_DIM, (h + 1) * HEAD_DIM)
            _, vjp = jax.vjp(_onorm_fn, o_ref[:, cols], z_ref[:, cols], w_ref[...])
            do, dz, dw = vjp(d_ref[:, cols])
            do_ref[:, cols] = do
            dz_ref[:, cols] = dz.astype(dz_ref.dtype)
            dw_ref[...] += dw

    wide = pl.BlockSpec((tm, GDN_WIDTH), lambda i: (i, 0))
    gate = pl.BlockSpec((tm, GDN_WIDTH), lambda i: (i, _Z_WIDE_BLK))
    row = pl.BlockSpec((1, LANE), lambda i: (0, 0))
    return pl.pallas_call(
        body, grid=(t // tm,), in_specs=[wide, gate, row, wide, pl.BlockSpec(memory_space=pl.ANY)],
        out_specs=[wide, gate, row],
        out_shape=[jax.ShapeDtypeStruct((t, GDN_WIDTH), F32), jax.ShapeDtypeStruct(dproj.shape, dproj.dtype),
                   jax.ShapeDtypeStruct((1, LANE), F32)],
        input_output_aliases={4: 1},
        compiler_params=_params("arbitrary"), name="gdn_onorm_bwd",
    )(o_raw, proj, norm_w, dmixin, dproj)


def _pool_select(levels, gi):
    out = levels[-1]
    for lvl in range(len(levels) - 2, -1, -1):
        out = jnp.where(gi == lvl, levels[lvl], out)
    return out


def _pool_count(shape, gi):
    pos = lax.broadcasted_iota(jnp.int32, shape, 0)
    win = lax.shift_left(jnp.int32(2), gi)
    return jnp.minimum(pos + 1, win).astype(F32)


def _pooled(p, gi):
    acc = p
    levels = []
    for lvl in range(POOL_GROUPS):
        acc = acc + _shift_down(acc, 1 << lvl)
        levels.append(acc)
    return _pool_select(levels, gi) / _pool_count(p.shape, gi) - p


def _pool_fwd(proj, pool_w, pool_scale):
    t = proj.shape[0]

    def body(p_ref, w_ref, s_ref, out_ref):
        gi = pl.program_id(0)
        pooled = _pooled(p_ref[...], gi)
        out_ref[...] = (_BDOT_PLAIN[0](pooled, w_ref[0]) * s_ref[0]).astype(out_ref.dtype)

    return pl.pallas_call(
        body, grid=(POOL_GROUPS,),
        in_specs=[pl.BlockSpec((t, POOL_GROUP_DIM), lambda g: (0, POOL_BLK + g)),
                  pl.BlockSpec((1, POOL_GROUP_DIM, POOL_GROUP_DIM), lambda g: (g, 0, 0)),
                  pl.BlockSpec((1, 1, POOL_GROUP_DIM), lambda g: (g, 0, 0))],
        out_specs=pl.BlockSpec((t, POOL_GROUP_DIM), lambda g: (0, GDN_WIDTH // POOL_GROUP_DIM + g)),
        out_shape=jax.ShapeDtypeStruct((t, 2 * GDN_WIDTH), BF16),
        compiler_params=_params("parallel"), name="pool_fwd",
    )(proj, pool_w, pool_scale)


def _pool_bwd(proj, pool_w, pool_scale, dmixin):
    t = proj.shape[0]
    nn, nt, tn = _BDOT_PLAIN

    def body(p_ref, w_ref, s_ref, d_ref, dp_ref, dw_ref, ds_ref):
        gi = pl.program_id(0)
        p = p_ref[...]
        pooled = _pooled(p, gi)
        mixed = nn(pooled, w_ref[0])
        d = d_ref[...]
        ds_ref[0] = jnp.sum(d * mixed, axis=0, keepdims=True)
        dmixed = d * s_ref[0]
        dw_ref[0] = tn(pooled, dmixed)
        dpooled = nt(dmixed, w_ref[0])
        acc = dpooled / _pool_count(p.shape, gi)
        levels = []
        for lvl in range(POOL_GROUPS):
            acc = acc + _shift_up(acc, 1 << lvl)
            levels.append(acc)
        dp_ref[...] = (_pool_select(levels, gi) - dpooled).astype(dp_ref.dtype)

    return pl.pallas_call(
        body, grid=(POOL_GROUPS,),
        in_specs=[pl.BlockSpec((t, POOL_GROUP_DIM), lambda g: (0, POOL_BLK + g)),
                  pl.BlockSpec((1, POOL_GROUP_DIM, POOL_GROUP_DIM), lambda g: (g, 0, 0)),
                  pl.BlockSpec((1, 1, POOL_GROUP_DIM), lambda g: (g, 0, 0)),
                  pl.BlockSpec((t, POOL_GROUP_DIM), lambda g: (0, GDN_WIDTH // POOL_GROUP_DIM + g))],
        out_specs=[pl.BlockSpec((t, POOL_GROUP_DIM), lambda g: (0, POOL_BLK + g)),
                   pl.BlockSpec((1, POOL_GROUP_DIM, POOL_GROUP_DIM), lambda g: (g, 0, 0)),
                   pl.BlockSpec((1, 1, POOL_GROUP_DIM), lambda g: (g, 0, 0))],
        out_shape=[jax.ShapeDtypeStruct((t, PROJ_COLS), BF16),
                   jax.ShapeDtypeStruct((POOL_GROUPS, POOL_GROUP_DIM, POOL_GROUP_DIM), F32),
                   jax.ShapeDtypeStruct((POOL_GROUPS, 1, POOL_GROUP_DIM), F32)],
        compiler_params=_params("parallel"), name="pool_bwd",
    )(proj, pool_w, pool_scale, dmixin)


def _ln_stats(s):
    mu = jnp.mean(s, axis=1, keepdims=True)
    xc = s - mu
    var = jnp.mean(xc * xc, axis=1, keepdims=True)
    rstd = lax.rsqrt(var + LN_EPS)
    return xc * rstd, rstd


def _ln_fwd(h_in, y, g, b, *, name, tm=512):
    t, d = h_in.shape
    tm = min(tm, t)

    def body(h_ref, y_ref, g_ref, b_ref, o_ref, o16_ref):
        xhat, _ = _ln_stats(ALPHA * h_ref[...] + y_ref[...])
        out = xhat * g_ref[...] + b_ref[...]
        o_ref[...] = out
        o16_ref[...] = out.astype(BF16)

    row = pl.BlockSpec((tm, d), lambda i: (i, 0))
    vec = pl.BlockSpec((1, d), lambda i: (0, 0))
    return pl.pallas_call(
        body, grid=(t // tm,), in_specs=[row, row, vec, vec], out_specs=[row, row],
        out_shape=[jax.ShapeDtypeStruct((t, d), F32), jax.ShapeDtypeStruct((t, d), BF16)],
        compiler_params=_params("parallel"), name=name,
    )(h_in, y, g, b)


def _ln_backward(xhat, rstd, dout, gain):
    dxhat = dout * gain
    m1 = jnp.mean(dxhat, axis=1, keepdims=True)
    m2 = jnp.mean(dxhat * xhat, axis=1, keepdims=True)
    return (rstd * (dxhat - m1 - xhat * m2), jnp.sum(dout * xhat, axis=0, keepdims=True),
            jnp.sum(dout, axis=0, keepdims=True))


def _ln_loss(h_in, y, g, b, target, *, name, tm=256):
    t, d = h_in.shape
    tm = min(tm, t)

    def body(h_ref, y_ref, g_ref, b_ref, t_ref, sq_ref, ds_ref, ds16_ref, dg_ref, dbias_ref):
        @pl.when(pl.program_id(0) == 0)
        def _():
            sq_ref[...] = jnp.zeros_like(sq_ref)
            dg_ref[...] = jnp.zeros_like(dg_ref)
            dbias_ref[...] = jnp.zeros_like(dbias_ref)

        xhat, rstd = _ln_stats(ALPHA * h_ref[...] + y_ref[...])
        err = xhat * g_ref[...] + b_ref[...] - t_ref[...]
        sq_ref[...] += jnp.sum(jnp.sum(err * err, axis=1, keepdims=True), axis=0, keepdims=True)
        ds, dg, dbias = _ln_backward(xhat, rstd, err * (1.0 / d), g_ref[...])
        ds_ref[...] = ds
        ds16_ref[...] = ds.astype(BF16)
        dg_ref[...] += dg
        dbias_ref[...] += dbias

    row = pl.BlockSpec((tm, d), lambda i: (i, 0))
    vec = pl.BlockSpec((1, d), lambda i: (0, 0))
    return pl.pallas_call(
        body, grid=(t // tm,), in_specs=[row, row, vec, vec, row],
        out_specs=[pl.BlockSpec((1, LANE), lambda i: (0, 0)), row, row, vec, vec],
        out_shape=[jax.ShapeDtypeStruct((1, LANE), F32), jax.ShapeDtypeStruct((t, d), F32),
                   jax.ShapeDtypeStruct((t, d), BF16), jax.ShapeDtypeStruct((1, d), F32), jax.ShapeDtypeStruct((1, d), F32)],
        compiler_params=_params("arbitrary"), name=name,
    )(h_in, y, g, b, target)


def _ln_bwd(h_in, y, g, d_a, d_b, *, name, tm=256):
    t, d = h_in.shape
    tm = min(tm, t)
    has_b = d_b is not None

    def body(*refs):
        if has_b:
            h_ref, y_ref, g_ref, da_ref, db_ref, ds_ref, ds16_ref, dg_ref, dbias_ref = refs
        else:
            h_ref, y_ref, g_ref, da_ref, ds_ref, ds16_ref, dg_ref, dbias_ref = refs

        @pl.when(pl.program_id(0) == 0)
        def _():
            dg_ref[...] = jnp.zeros_like(dg_ref)
            dbias_ref[...] = jnp.zeros_like(dbias_ref)

        xhat, rstd = _ln_stats(ALPHA * h_ref[...] + y_ref[...])
        dout = da_ref[...]
        if has_b:
            dout = dout + ALPHA * db_ref[...]
        ds, dg, dbias = _ln_backward(xhat, rstd, dout, g_ref[...])
        ds_ref[...] = ds
        ds16_ref[...] = ds.astype(BF16)
        dg_ref[...] += dg
        dbias_ref[...] += dbias

    row = pl.BlockSpec((tm, d), lambda i: (i, 0))
    vec = pl.BlockSpec((1, d), lambda i: (0, 0))
    args = [h_in, y, g, d_a] + ([d_b] if has_b else [])
    return pl.pallas_call(
        body, grid=(t // tm,), in_specs=[row, row, vec, row] + ([row] if has_b else []),
        out_specs=[row, row, vec, vec],
        out_shape=[jax.ShapeDtypeStruct((t, d), F32), jax.ShapeDtypeStruct((t, d), BF16),
                   jax.ShapeDtypeStruct((1, d), F32), jax.ShapeDtypeStruct((1, d), F32)],
        compiler_params=_params("arbitrary"), name=name,
    )(*args)


def _attn_fn(q, k, v, dots):
    nn, nt, _ = dots
    s = nt(q, k) * (XATTN_HEAD_DIM ** -0.5)
    s = s - lax.stop_gradient(jnp.max(s, axis=1, keepdims=True))
    e = jnp.exp(s)
    p = e / jnp.sum(e, axis=1, keepdims=True)
    return nn(p, v)


def _attn_fwd(q, k, v, tq=2048):
    t = q.shape[0]
    tq = min(tq, t)

    def body(q_ref, k_ref, v_ref, o_ref):
        o_ref[...] = _attn_fn(q_ref[...], k_ref[...], v_ref[...], _BDOT_PLAIN).astype(BF16)

    qs = pl.BlockSpec((tq, XATTN_HEAD_DIM), lambda h, i: (i, h))
    ks = pl.BlockSpec((MEM_LEN, XATTN_HEAD_DIM), lambda h, i: (0, h))
    return pl.pallas_call(
        body, grid=(XATTN_HEADS, t // tq), in_specs=[qs, ks, ks], out_specs=qs,
        out_shape=jax.ShapeDtypeStruct(q.shape, BF16), compiler_params=_params("parallel", "parallel"), name="xattn_fwd",
    )(q, k, v)


def _attn_bwd(q, k, v, do, tq=1024):
    t = q.shape[0]
    tq = min(tq, t)

    def body(q_ref, k_ref, v_ref, do_ref, dq_ref, dk_ref, dv_ref):
        @pl.when(pl.program_id(1) == 0)
        def _():
            dk_ref[...] = jnp.zeros_like(dk_ref)
            dv_ref[...] = jnp.zeros_like(dv_ref)

        _, vjp = jax.vjp(lambda a, b, c: _attn_fn(a, b, c, _BDOT_VJP), q_ref[...].astype(F32), k_ref[...].astype(F32),
                         v_ref[...].astype(F32))
        dq, dk, dv = vjp(do_ref[...].astype(F32))
        dq_ref[...] = dq.astype(BF16)
        dk_ref[...] += dk
        dv_ref[...] += dv

    qs = pl.BlockSpec((tq, XATTN_HEAD_DIM), lambda h, i: (i, h))
    ks = pl.BlockSpec((MEM_LEN, XATTN_HEAD_DIM), lambda h, i: (0, h))
    return pl.pallas_call(
        body, grid=(XATTN_HEADS, t // tq), in_specs=[qs, ks, ks, qs], out_specs=[qs, ks, ks],
        out_shape=[jax.ShapeDtypeStruct(q.shape, BF16), jax.ShapeDtypeStruct(k.shape, F32), jax.ShapeDtypeStruct(v.shape, F32)],
        compiler_params=_params("parallel", "arbitrary"), name="xattn_bwd",
    )(q, k, v, do)


def _local_step(x, x16, mem, target, weights_of, grads_ready):
    def behind(vec, token):
        return vec if token is None else vec + token

    w = dict(weights_of("mixer", None))
    proj = _mm(x16, w["w_in"], tb=True, tn=768, name="mm_in_proj")
    mixin = _pool_fwd(proj, w["pool_w"], w["pool_scale"])
    post = _gdn_prep_fwd(proj, w["conv_w"])
    token = weights_of("ahead_conv", post)
    chunked, t_inv = _gdn_local_fwd(post, proj, behind(w["alog_row"], token), w["dtb_row"])
    o_raw, saved = _gdn_state_fwd(*chunked)
    token = weights_of("ahead_scan", o_raw)
    mixin = _onorm_fwd(o_raw, proj, behind(w["gdn_norm_w"], token), mixin)
    w.update(weights_of("attn", mixin))
    mix = _mm(mixin, w["w_out"], name="mm_out_proj")
    h1, h1_16 = _ln_fwd(x, mix, w["ln1_g"], w["ln1_b"], name="ln1_fwd")
    xq = _mm(h1_16, w["xq_w"], out_dtype=BF16, name="mm_xq")
    xk = _mm(mem, w["xk_w"], out_dtype=BF16, name="mm_xk")
    xv = _mm(mem, w["xv_w"], out_dtype=BF16, name="mm_xv")
    xo = _attn_fwd(xq, xk, xv)
    token = weights_of("ahead_attn", xo)
    if token is not None:
        xo, _ = lax.optimization_barrier((xo, token))
    xa = _mm(xo, w["xo_w"], name="mm_xo")
    h2, h2_16 = _ln_fwd(h1, xa, w["ln2_g"], w["ln2_b"], name="ln2_fwd")
    w.update(weights_of("up", h2_16))
    act, relu = _mm(h2_16, w["w_up"], b_chunks=True, epi="relu2", name="mm_up")
    w.update(weights_of("down", act))
    ff = _mm(act, w["w_down"], tn=512, tk=2048, name="mm_down")
    g = {}
    sq, ds3, ds3_16, g["ln3_g"], g["ln3_b"] = _ln_loss(h2, ff, w["ln3_g"], w["ln3_b"], target, name="ln3_loss")

    gw_down = _mm(act, ds3_16, ta=True, out_dtype=BF16, tm=512, tn=D_MODEL, name="mm_gw_down")
    du = _mm(ds3_16, w["w_down"], tb=True, epi="mul2r", extra=relu, name="mm_du")
    gw_up = _mm(h2_16, du, ta=True, out_dtype=BF16, o_chunks=True, name="mm_gw_up")
    token = grads_ready("mlp", {"w_down": gw_down, "w_up": gw_up})
    dh2 = _mm(du, w["w_up"], tb=True, b_chunks=True, tn=1024, tk=1024, name="mm_dh2")
    ds2, ds2_16, g["ln2_g"], g["ln2_b"] = _ln_bwd(h1, xa, behind(w["ln2_g"], token), dh2, ds3, name="ln2_bwd")
    gw_xo = _mm(xo, ds2_16, ta=True, out_dtype=BF16, name="mm_gw_xo")
    dxo = _mm(ds2_16, w["xo_w"], tb=True, out_dtype=BF16, name="mm_dxo")
    dxq, dxk, dxv = _attn_bwd(xq, xk, xv, dxo)
    gw_xq = _mm(h1_16, dxq, ta=True, out_dtype=BF16, name="mm_gw_xq")
    gw_xk = _mm(mem, dxk, ta=True, out_dtype=BF16, name="mm_gw_xk")
    gw_xv = _mm(mem, dxv, ta=True, out_dtype=BF16, name="mm_gw_xv")
    token = grads_ready("attn", {"xo_w": gw_xo, "xq_w": gw_xq, "xk_w": gw_xk, "xv_w": gw_xv})
    dh1 = _mm(dxq, w["xq_w"], tb=True, name="mm_dh1")
    ds1, ds1_16, g["ln1_g"], g["ln1_b"] = _ln_bwd(x, mix, behind(w["ln1_g"], token), dh1, ds2, name="ln1_bwd")
    gw_out = _mm(mixin, ds1_16, ta=True, out_dtype=BF16, name="mm_gw_out")
    dmixin = _mm(ds1_16, w["w_out"], tb=True, name="mm_dmixin")
    dproj, gw_pool, g["pool_scale"] = _pool_bwd(proj, w["pool_w"], w["pool_scale"], dmixin)
    token = grads_ready("mix", {"w_out": gw_out, "pool_w": gw_pool})
    do_raw, dproj, g["gdn_norm_w"] = _onorm_bwd(o_raw, proj, behind(w["gdn_norm_w"], token), dmixin, dproj)
    cots = _gdn_state_bwd(*chunked, saved, do_raw)
    token = grads_ready("tick", {"after": cots[0]})
    dpost, dproj, g["alog_row"], g["dtb_row"] = _gdn_local_bwd(post, proj, behind(w["alog_row"], token), w["dtb_row"],
                                                               t_inv, cots, dproj)
    dproj, g["conv_w"] = _gdn_prep_bwd(proj, w["conv_w"], dpost, dproj)
    token = grads_ready("small", {**g, "sq": sq})
    gw_in = _mm(dproj, x16, ta=True, out_dtype=BF16, tm=768, tn=D_MODEL, after=token, name="mm_gw_in")
    token = grads_ready("in", {"w_in": gw_in})
    grad_x = _mm(dproj, w["w_in"], tk=1792, epi="add", extra=ds1, add_scale=ALPHA, after=token, name="mm_dx")
    return sq, grad_x, g


_VECTORS = ("a_log", "dt_bias", "gdn_norm_w", "pool_scale", "ln1_g", "ln1_b", "ln2_g", "ln2_b", "ln3_g", "ln3_b")
_BA_SPLIT = BA_OFF + 2 * GDN_HEADS


def _lane_row(v, offset):
    return jnp.zeros((1, LANE), F32).at[0, offset:offset + v.shape[0]].set(v)


_GROUP_VECTORS = {"mixer": (), "attn": ("ln1_g", "ln1_b", "ln2_g", "ln2_b"), "up": (), "down": ("ln3_g", "ln3_b")}


def _group_weights(group, full):
    w = {n: full[n].reshape(1, D_MODEL) for n in _GROUP_VECTORS[group]}
    if group == "mixer":
        w.update({
            "w_in": _w_in_padded(full["w_in"]),
            "conv_w": full["conv_w"],
            "alog_row": _lane_row(full["a_log"], GDN_HEADS),
            "dtb_row": _lane_row(full["dt_bias"], GDN_HEADS),
            "gdn_norm_w": full["gdn_norm_w"].reshape(1, LANE),
            "pool_w": full["pool_w"],
            "pool_scale": full["pool_scale"].reshape(POOL_GROUPS, 1, POOL_GROUP_DIM),
        })
    else:
        w.update({n: full[n] for n in dict(_GATHER_GROUPS)[group]})
    return w


def _w_in_row_map():
    per = IN_COLS // N_DEV
    gap = POOL_OFF - _BA_SPLIT
    pieces = []
    for d in range(N_DEV):
        lo, hi = d * per, (d + 1) * per
        if hi <= _BA_SPLIT:
            pieces.append([(0, lo, per)])
        elif lo >= _BA_SPLIT:
            pieces.append([(0, lo + gap, per)])
        else:
            pieces.append([(0, lo, _BA_SPLIT - lo), (_BA_SPLIT - lo, POOL_OFF, hi - _BA_SPLIT)])
    return pieces


_W_IN_LANES = 256


def _w_in_padded(blocks):
    def body(b_ref, o_ref):
        for d, pieces in enumerate(_w_in_row_map()):
            for src, dst, rows in pieces:
                o_ref[dst:dst + rows, :] = b_ref[d, src:src + rows, :]
        o_ref[_BA_SPLIT:POOL_OFF, :] = jnp.zeros((POOL_OFF - _BA_SPLIT, _W_IN_LANES), o_ref.dtype)

    n, per, cols = blocks.shape
    return pl.pallas_call(
        body, grid=(cols // _W_IN_LANES,), in_specs=[pl.BlockSpec((n, per, _W_IN_LANES), lambda j: (0, 0, j))],
        out_specs=pl.BlockSpec((PROJ_COLS, _W_IN_LANES), lambda j: (0, j)),
        out_shape=jax.ShapeDtypeStruct((PROJ_COLS, cols), blocks.dtype), compiler_params=_params("parallel"),
        name="w_in_padded")(blocks)


def _w_in_chunks(g):
    def body(g_ref, o_ref):
        for d, pieces in enumerate(_w_in_row_map()):
            for dst, src, rows in pieces:
                o_ref[d, dst:dst + rows, :] = g_ref[src:src + rows, :]

    cols = g.shape[1]
    per = IN_COLS // N_DEV
    return pl.pallas_call(
        body, grid=(cols // _W_IN_LANES,), in_specs=[pl.BlockSpec((PROJ_COLS, _W_IN_LANES), lambda j: (0, j))],
        out_specs=pl.BlockSpec((N_DEV, per, _W_IN_LANES), lambda j: (0, 0, j)),
        out_shape=jax.ShapeDtypeStruct((N_DEV, per, cols), g.dtype), compiler_params=_params("parallel"),
        name="w_in_chunks")(g)


def _finish_small_grads(g):
    out = {"conv_w": g["conv_w"]}
    out["a_log"] = g["alog_row"][0, GDN_HEADS:2 * GDN_HEADS]
    out["dt_bias"] = g["dtb_row"][0, GDN_HEADS:2 * GDN_HEADS]
    out["gdn_norm_w"] = g["gdn_norm_w"].reshape(LANE)
    out["pool_scale"] = g["pool_scale"].reshape(POOL_GROUPS * POOL_GROUP_DIM)
    for n in ("ln1_g", "ln1_b", "ln2_g", "ln2_b", "ln3_g", "ln3_b"):
        out[n] = g[n].reshape(D_MODEL)
    return out


def _adamw_math(w, g, m, v):
    m = ADAM_B1 * m + (1.0 - ADAM_B1) * g
    v = ADAM_B2 * v + (1.0 - ADAM_B2) * (g * g)
    m_hat = m / (1.0 - ADAM_B1 ** ADAM_STEP)
    v_hat = v / (1.0 - ADAM_B2 ** ADAM_STEP)
    delta = -ADAM_LR * (m_hat / (jnp.sqrt(v_hat) + ADAM_EPS) + ADAM_WD * w)
    return delta, m, v


ADAMW_TILE_ELEMS = 256 * 1024
CHIP_SUM_TILE_ELEMS = 1024 * 1024


def _shard_tile(r, c, elems):
    for rows in (1024, 512, 256, 128):
        if r % rows == 0 and rows * c <= elems:
            return rows, c
    if r % 128 == 0:
        return 128, c
    return r, 256 if c % 256 == 0 else c


def _adamw_shard(parts, own, me, w, m, v, *, name):
    s, r, c = parts.shape
    tr, tc = _shard_tile(r, c, ADAMW_TILE_ELEMS)
    assert r % tr == 0 and c % tc == 0, (name, r, c)
    unit_axis = w.ndim == 3
    at = (slice(None), 0, slice(None)) if unit_axis else Ellipsis

    def body(me_ref, p_ref, own_ref, w_ref, m_ref, v_ref, g_ref, d_ref, nm_ref, nv_ref):
        mine = own_ref[...].astype(F32)
        g = None
        for i in range(s):
            part = jnp.where(me_ref[0] == i, mine, p_ref[i].astype(F32))
            g = part if g is None else g + part
        delta, nm, nv = _adamw_math(w_ref[at], g, m_ref[at], v_ref[at])
        g_ref[at] = g
        d_ref[at] = delta
        nm_ref[at] = nm
        nv_ref[at] = nv

    if unit_axis:
        blk = pl.BlockSpec((tr, 1, tc), lambda i, j, me_ref: (i, 0, j))
        out = jax.ShapeDtypeStruct((r, 1, c), F32)
    else:
        blk = pl.BlockSpec((tr, tc), lambda i, j, me_ref: (i, j))
        out = jax.ShapeDtypeStruct((r, c), F32)
    return pl.pallas_call(
        body,
        grid_spec=pltpu.PrefetchScalarGridSpec(
            num_scalar_prefetch=1, grid=(r // tr, c // tc),
            in_specs=[pl.BlockSpec((s, tr, tc), lambda i, j, me_ref: (0, i, j)),
                      pl.BlockSpec((None, tr, tc), lambda i, j, me_ref: (me_ref[0], i, j)), blk, blk, blk],
            out_specs=[blk, blk, blk, blk]),
        out_shape=[out, out, out, out], compiler_params=_params("parallel", "parallel"), name=name,
    )(me, parts, own, w, m, v)


N_CHIPS = N_DEV // 2


def _chip_sums(chunks, from_sibling, core, *, name):
    _, r, c = chunks.shape
    tr, tc = _shard_tile(r, c, CHIP_SUM_TILE_ELEMS)
    assert r % tr == 0 and c % tc == 0, (name, r, c)

    def body(core_ref, mine_ref, other_ref, o_ref):
        o_ref[...] = (mine_ref[...].astype(F32) + other_ref[...].astype(F32)).astype(o_ref.dtype)

    by_chip = pl.BlockSpec((None, tr, tc), lambda q, i, j, core_ref: (q, i, j))
    return pl.pallas_call(
        body,
        grid_spec=pltpu.PrefetchScalarGridSpec(
            num_scalar_prefetch=1, grid=(N_CHIPS, r // tr, c // tc),
            in_specs=[pl.BlockSpec((None, tr, tc), lambda q, i, j, core_ref: (2 * q + core_ref[0], i, j)), by_chip],
            out_specs=by_chip),
        out_shape=jax.ShapeDtypeStruct((N_CHIPS, r, c), chunks.dtype),
        compiler_params=_params("parallel", "parallel", "parallel"), name=name,
    )(core, chunks, from_sibling)


def _place():
    return lax.axis_index("x"), lax.axis_index("y"), lax.axis_index("c")


def _slot(px, py, pc):
    return 4 * px + 2 * py + pc


_HBM = pl.BlockSpec(memory_space=pltpu.HBM)


_SEM = pl.BlockSpec(memory_space=pltpu.SEMAPHORE)
_ANY = pl.BlockSpec(memory_space=pl.ANY)
_EFFECT = pltpu.SideEffectType.DATAFLOW_SIDE_EFFECTING


def _peer(k, x, y, c):
    return (1 - x if k & 4 else x, 1 - y if k & 2 else y, 1 - c if k & 1 else c)


_EXCHANGE_BITS = {"gather_near": (1, 2, 4), "gather_relay": (6,), "gather_pass": (2, 4, 6),
                  "scatter_sibling": (1, 1, 1, 1), "scatter_chips": (2, 4, 6), "all_small": (1, 2, 3, 4, 5, 6, 7)}


def _exchange_copy(mode, src, land, w, i, place, send_sems, recv_sems, receiving):
    bits = _EXCHANGE_BITS[mode]
    k = bits[i]
    peer = _peer(k, *place)
    me = _slot(*place)
    if mode in ("gather_near", "all_small"):
        to, src_ref, sent_to, got_at = peer, src[w], me, _slot(*peer)
    elif mode == "gather_relay":
        x, y, c = place
        other = 1 - c
        to = (lax.bitwise_xor(x, c), lax.bitwise_xor(y, other), c)
        blk = _slot(lax.bitwise_xor(x, other), lax.bitwise_xor(y, c), c)
        src_ref, sent_to, got_at = land[w].at[blk], blk, _slot(*peer)
    elif mode == "gather_pass":
        blk = _slot(*peer)
        to, src_ref, sent_to, got_at = _peer(1, *place), land[w].at[blk], blk, _slot(*_peer(k | 1, *place))
    elif mode == "scatter_sibling":
        to, src_ref, sent_to, got_at = peer, src[w].at[2 * i + 1 - place[2]], i, i
    else:
        to, src_ref, sent_to, got_at = peer, src[w].at[_slot(*peer) // 2], me // 2, _slot(*peer) // 2
    sem = w * len(bits) + i
    return pltpu.make_async_remote_copy(
        src_ref=src_ref, dst_ref=land[w].at[got_at if receiving else sent_to], send_sem=send_sems.at[sem],
        recv_sem=recv_sems.at[sem], device_id=to, device_id_type=MESH)


def _exchange_start(mode, srcs, lands, after, *, name):
    ns, nl = len(srcs), len(lands)
    n_sem = nl * len(_EXCHANGE_BITS[mode])

    def body(*refs):
        src, land = refs[:ns], refs[ns:ns + nl]
        send_sems, recv_sems = refs[ns + nl + 1:ns + nl + 3]
        token = refs[-1]
        place = _place()
        for w in range(nl):
            for i in range(len(_EXCHANGE_BITS[mode])):
                _exchange_copy(mode, src, land, w, i, place, send_sems, recv_sems, receiving=False).start()
        token[...] = jnp.zeros_like(token)

    sems = pltpu.SemaphoreType.DMA((n_sem,))
    arrays = list(srcs) + list(lands)
    res = pl.pallas_call(
        body, name=name, in_specs=[_HBM] * (ns + nl) + [_ANY],
        out_specs=(_SEM, _SEM, *([_HBM] * (ns + nl)), pl.BlockSpec(memory_space=pltpu.VMEM)),
        out_shape=(sems, sems, *[pltpu.HBM(a.shape, a.dtype) for a in arrays], jax.ShapeDtypeStruct((8, LANE), F32)),
        input_output_aliases={i: 2 + i for i in range(ns + nl)},
        compiler_params=pltpu.CompilerParams(has_side_effects=_EFFECT),
    )(*[pltpu.with_memory_space_constraint(a, pltpu.HBM) for a in arrays], after)
    return res[0], res[1], list(res[2:2 + ns]), list(res[2 + ns:2 + ns + nl]), res[-1]


def _exchange_wait(mode, started, after, *, name):
    send_sems, recv_sems, srcs, lands, _ = started
    ns, nl = len(srcs), len(lands)

    def body(*refs):
        src, land = refs[:ns], refs[ns:ns + nl]
        send_sems, recv_sems = refs[ns + nl:ns + nl + 2]
        place = _place()
        for w in range(nl):
            for i in range(len(_EXCHANGE_BITS[mode])):
                cp = _exchange_copy(mode, src, land, w, i, place, send_sems, recv_sems, receiving=True)
                cp.wait_send()
                cp.wait_recv()

    arrays = list(srcs) + list(lands)
    res = pl.pallas_call(
        body, name=name, in_specs=[_HBM] * (ns + nl) + [_SEM, _SEM, _ANY], out_specs=[_HBM] * (ns + nl),
        out_shape=[pltpu.HBM(a.shape, a.dtype) for a in arrays],
        input_output_aliases={i: i for i in range(ns + nl)},
        compiler_params=pltpu.CompilerParams(has_side_effects=_EFFECT),
    )(*arrays, send_sems, recv_sems, after)
    return list(res[:ns]), list(res[ns:])


_LN_ROWS = ("ln1_g", "ln1_b", "ln2_g", "ln2_b", "ln3_g", "ln3_b")
_MISC_ROW = len(_LN_ROWS)
_MISC = (("pool_scale", 0, GDN_WIDTH), ("gdn_norm_w", GDN_WIDTH, HEAD_DIM), ("a_log", GDN_WIDTH + LANE, GDN_HEADS),
         ("dt_bias", GDN_WIDTH + 2 * LANE, GDN_HEADS), ("loss", GDN_WIDTH + 3 * LANE, 1))
_CONV_ROW = _MISC_ROW + 1
_CONV_ROWS = CONV_K * QKV_COLS // D_MODEL
_SMALL_ROWS = 16


def _pack_small(vals):
    pieces, at = [], 0
    for n, off, size in _MISC:
        pieces.append(jnp.zeros((off - at,), F32))
        pieces.append(vals[n].reshape(size).astype(F32) if n in vals else jnp.zeros((size,), F32))
        at = off + size
    pieces.append(jnp.zeros((D_MODEL - at,), F32))
    conv = vals["conv_w"].reshape(-1) if "conv_w" in vals else jnp.zeros((_CONV_ROWS * D_MODEL,), F32)
    tail = jnp.zeros(((_SMALL_ROWS - _CONV_ROW - _CONV_ROWS) * D_MODEL,), F32)
    flat = jnp.concatenate([vals[n].reshape(D_MODEL) for n in _LN_ROWS] + pieces + [conv, tail])
    return flat.reshape(_SMALL_ROWS, D_MODEL)


def _adamw_small(zone, mine, me, w, m, v):
    short = [(n, off, size) for n, off, size in _MISC if n != "loss"]

    def body(me_ref, z_ref, mine_ref, w_ref, m_ref, v_ref, *rest):
        outs, (g_s, d_s, nm_s, nv_s) = rest[:-4], rest[-4:]
        g = None
        for s in range(N_DEV):
            part = jnp.where(me_ref[0] == s, mine_ref[...], z_ref[s])
            g = part if g is None else g + part
        g_s[...] = g
        d_s[...], nm_s[...], nv_s[...] = _adamw_math(w_ref[...], g, m_ref[...], v_ref[...])
        k = 0
        for src in (g_s, d_s, nm_s, nv_s):
            for r in range(len(_LN_ROWS)):
                outs[k][...] = src[r:r + 1, :]
                k += 1
            for _, off, size in short:
                outs[k][...] = src[_MISC_ROW:_MISC_ROW + 1, off:off + size]
                k += 1
        outs[k][...] = g_s[_CONV_ROW:_CONV_ROW + _CONV_ROWS, :]
        outs[k + 1][...] = g_s[_MISC_ROW:_MISC_ROW + 1, :]

    rows, d = mine.shape
    per_quantity = [jax.ShapeDtypeStruct((1, D_MODEL), F32)] * len(_LN_ROWS) + [
        jax.ShapeDtypeStruct((1, size), F32) for _, _, size in short]
    out_shape = per_quantity * 4 + [jax.ShapeDtypeStruct((_CONV_ROWS, d), F32), jax.ShapeDtypeStruct((1, d), F32)]
    whole = lambda a: pl.BlockSpec(a.shape, lambda i, me_ref: (0,) * len(a.shape))
    res = pl.pallas_call(
        body,
        grid_spec=pltpu.PrefetchScalarGridSpec(
            num_scalar_prefetch=1, grid=(1,), in_specs=[whole(a) for a in (zone, mine, w, m, v)],
            out_specs=[whole(s) for s in out_shape], scratch_shapes=[pltpu.VMEM((rows, d), F32)] * 4),
        out_shape=out_shape, compiler_params=_params("arbitrary"), name="adamw_small",
    )(me, zone, mine, w, m, v)
    names = list(_LN_ROWS) + [n for n, _, _ in short]
    n_each = len(names)
    quantities = [dict(zip(names, res[q * n_each:(q + 1) * n_each])) for q in range(4)]
    return quantities, res[-2], res[-1]


_WEIGHT_ORDER = ("w_in", "conv_w", "a_log", "dt_bias", "gdn_norm_w", "pool_w", "pool_scale", "w_out", "ln1_g", "ln1_b",
                 "xq_w", "xk_w", "xv_w", "xo_w", "ln2_g", "ln2_b", "w_up", "w_down", "ln3_g", "ln3_b")


def _shard2d(name, a):
    if name == "w_in":
        return a.T
    return a.reshape(-1, a.shape[-1]) if name == "pool_w" else a


def _update_view(name, a):
    return jnp.transpose(a, (2, 0, 1)) if name == "w_in" else _shard2d(name, a[0])


def _shard_result(name, r, shape):
    return jnp.transpose(r, (1, 2, 0)) if name == "w_in" else r.reshape(shape)


def _gathered_to_full(name, gth):
    if name in ("w_up", "w_in"):
        return gth
    if name == "conv_w":
        return jnp.transpose(gth, (1, 0, 2)).reshape(gth.shape[1], N_DEV * gth.shape[2])
    if name == "pool_w":
        g4 = gth.reshape(N_DEV, POOL_GROUPS, POOL_GROUP_DIM // N_DEV, POOL_GROUP_DIM)
        return jnp.transpose(g4, (1, 0, 2, 3)).reshape(POOL_GROUPS, POOL_GROUP_DIM, POOL_GROUP_DIM)
    return gth.reshape(N_DEV * gth.shape[1], gth.shape[2])


def _full_to_chunks(name, full):
    if name == "w_up":
        return full
    if name == "pool_w":
        g4 = full.reshape(POOL_GROUPS, N_DEV, POOL_GROUP_DIM // N_DEV, POOL_GROUP_DIM)
        return jnp.transpose(g4, (1, 0, 2, 3)).reshape(N_DEV, POOL_GROUPS * POOL_GROUP_DIM // N_DEV, POOL_GROUP_DIM)
    return full.reshape(N_DEV, full.shape[0] // N_DEV, full.shape[1])


_GATHER_GROUPS = (("mixer", ("w_in", "conv_w", "pool_w")), ("attn", ("w_out", "xq_w", "xk_w", "xv_w", "xo_w")),
                  ("up", ("w_up",)), ("down", ("w_down",)))


def _grad_chunks(name, g):
    if name == "w_in":
        return _w_in_chunks(g.astype(BF16))
    return _full_to_chunks(name, g.astype(BF16))


def kernel(x, mem, w_in, conv_w, a_log, dt_bias, gdn_norm_w, pool_w, pool_scale, w_out, ln1_g, ln1_b, xq_w, xk_w, xv_w, xo_w, ln2_g, ln2_b, w_up, w_down, ln3_g, ln3_b, loss_target, m_w_in, m_conv_w, m_a_log, m_dt_bias, m_gdn_norm_w, m_pool_w, m_pool_scale, m_w_out, m_ln1_g, m_ln1_b, m_xq_w, m_xk_w, m_xv_w, m_xo_w, m_ln2_g, m_ln2_b, m_w_up, m_w_down, m_ln3_g, m_ln3_b, v_w_in, v_conv_w, v_a_log, v_dt_bias, v_gdn_norm_w, v_pool_w, v_pool_scale, v_w_out, v_ln1_g, v_ln1_b, v_xq_w, v_xk_w, v_xv_w, v_xo_w, v_ln2_g, v_ln2_b, v_w_up, v_w_down, v_ln3_g, v_ln3_b):
    args = dict(locals())
    wt = {n: args[n][0] for n in _WEIGHT_ORDER}
    mo = {n: args["m_" + n][0] for n in _WEIGHT_ORDER}
    vo = {n: args["v_" + n][0] for n in _WEIGHT_ORDER}

    me = _slot(*_place())
    me_arr = jnp.reshape(me, (1,)).astype(jnp.int32)
    nothing = jnp.zeros((8, LANE), F32)

    def landing_zones(names):
        shards = [_shard2d(n, wt[n]).astype(F32 if n == "conv_w" else BF16) for n in names]
        zones = [lax.dynamic_update_slice(lax.empty((N_DEV, *s.shape), s.dtype), s[None], (me, 0, 0)) for s in shards]
        return shards, zones

    chip_arr = jnp.reshape(me // 2, (1,)).astype(jnp.int32)
    core_arr = jnp.reshape(lax.axis_index("c"), (1,)).astype(jnp.int32)
    names_of = dict(_GATHER_GROUPS)
    gathers = {}
    prepared = {}

    def gather_near(group, after):
        shards, zones = prepared.pop(group) if group in prepared else landing_zones(names_of[group])
        gathers[group] = _exchange_start("gather_near", shards, zones, after, name="gather_near_" + group)
        return gathers[group][4]

    def gather_next(group, was, now, after):
        _, zones = _exchange_wait(was, gathers[group], after, name=f"{was}_{group}_wait")
        gathers[group] = _exchange_start(now, [], zones, nothing, name=f"{now}_{group}")
        return gathers[group][4]

    def gather_relay(group, after):
        return gather_next(group, "gather_near", "gather_relay", after)

    def gather_pass(group, after):
        return gather_next(group, "gather_relay", "gather_pass", after)

    def gathered(group, after):
        _, zones = _exchange_wait("gather_pass", gathers[group], after, name=f"gather_pass_{group}_wait")
        full = {n: _gathered_to_full(n, z) for n, z in zip(names_of[group], zones)}
        full.update({n: wt[n] for n in _VECTORS})
        return _group_weights(group, full)

    token = gather_near("mixer", nothing)
    x16 = _cast_bf16(x[0], name="cast_x")
    later = {group: landing_zones(names_of[group]) for group in ("attn", "up", "down")}
    token, x16, later = lax.optimization_barrier((token, x16, later))
    prepared.update(later)
    token = gather_pass("mixer", gather_relay("mixer", token))
    token = gather_near("attn", token)

    def weights_of(group, after):
        if group == "mixer":
            return gathered(group, token)
        if group == "ahead_conv":
            return gather_near("up", gather_relay("attn", after))[0:1, 0:1]
        if group == "ahead_scan":
            return gather_pass("attn", after)[0:1, 0:1]
        if group == "attn":
            return gathered(group, gather_near("down", gather_relay("up", after)))
        if group == "ahead_attn":
            return gather_relay("down", gather_pass("up", after))[0:1, 0:1]
        if group == "up":
            return gathered(group, gather_pass("down", after))
        return gathered(group, after)

    scatters = {}
    in_flight = []

    def chip_stage(after):
        group, names, started = in_flight.pop()
        chunks, from_sibling = _exchange_wait("scatter_sibling", started, after, name=f"scatter_sibling_{group}_wait")
        sums = [_chip_sums(c, f, core_arr, name=f"chip_sums_{n}") for n, c, f in zip(names, chunks, from_sibling)]
        scatters[group] = (names, _exchange_start("scatter_chips", sums, [lax.empty(s.shape, s.dtype) for s in sums],
                                                  nothing, name="scatter_chips_" + group))
        return scatters[group][1][4]

    small_sent = []

    def grads_ready(group, grads):
        if group == "tick":
            return chip_stage(grads["after"])[0:1, 0:1] if in_flight else None
        if group == "small":
            small = _finish_small_grads(grads)
            small["loss"] = 0.5 * grads["sq"][0:1, 0] / D_MODEL
            packed = _pack_small(small)
            zone = lax.empty((N_DEV, *packed.shape), F32)
            small_sent.append(_exchange_start("all_small", [packed], [zone], nothing, name="small_grads_start"))
            return small_sent[0][4][0:1, 0:1]
        names = tuple(grads)
        chunks = [_grad_chunks(n, grads[n]) for n in names]
        token = chip_stage(chunks[0]) if in_flight else nothing
        zones = [lax.empty((N_CHIPS, *c.shape[1:]), c.dtype) for c in chunks]
        started = _exchange_start("scatter_sibling", chunks, zones, token, name="scatter_sibling_" + group)
        in_flight.append((group, names, started))
        return started[4][0:1, 0:1]

    sq, grad_x, g = _local_step(x[0], x16, mem[0], loss_target[0], weights_of, grads_ready)

    out = {}
    after = chip_stage(grad_x)
    for group, (names, started) in scatters.items():
        sums, lands = _exchange_wait("scatter_chips", started, after, name=f"scatter_chips_{group}_wait")
        for n, parts, own in zip(names, lands, sums):
            res = _adamw_shard(parts, own, chip_arr, _update_view(n, args[n]), _update_view(n, args["m_" + n]),
                               _update_view(n, args["v_" + n]), name="adamw_" + n)
            out[n] = [_shard_result(n, r, args[n].shape) for r in res]
            after = res[1]

    (packed,), (zone,) = _exchange_wait("all_small", small_sent[0], after, name="small_grads_wait")
    quantities, conv_rows, misc_row = _adamw_small(
        zone, packed, me_arr, _pack_small({n: wt[n] for n in _VECTORS}), _pack_small({n: mo[n] for n in _VECTORS}),
        _pack_small({n: vo[n] for n in _VECTORS}))
    cols = conv_w.shape[-1]
    conv_mine = lax.dynamic_slice(conv_rows.reshape(CONV_K, QKV_COLS), (0, me * cols), (CONV_K, cols))[None]
    res = _adamw_shard(conv_mine, conv_mine, jnp.zeros((1,), jnp.int32), wt["conv_w"], mo["conv_w"], vo["conv_w"],
                       name="adamw_conv_w")
    out["conv_w"] = [r.reshape(conv_w.shape) for r in res]
    for n in _VECTORS:
        out[n] = [q[n] for q in quantities]
    loss_at = dict((n, off) for n, off, _ in _MISC)["loss"]

    return (misc_row[0, loss_at], grad_x[None], *[out[n][0] for n in _WEIGHT_ORDER], *[out[n][1] for n in _WEIGHT_ORDER],
            *[out[n][2] for n in _WEIGHT_ORDER], *[out[n][3] for n in _WEIGHT_ORDER])
```

```python
import jax
import jax.numpy as jnp
from jax import lax
from jax.experimental import pallas as pl
from jax.experimental.pallas import tpu as pltpu

F32 = jnp.float32
BF16 = jnp.bfloat16
MESH = pl.DeviceIdType.MESH

N_DEV = 8
D_MODEL = 2048
GDN_WIDTH = 1024
GDN_HEADS = 8
HEAD_DIM = 128
CONV_K = 4
CHUNK = 64
POOL_GROUPS = 4
POOL_GROUP_DIM = 256
MEM_LEN = 256
XATTN_HEADS = 4
XATTN_HEAD_DIM = 512
D_FF = 8192
IN_COLS = 5136
ALPHA = 2.0 ** 0.25
LN_EPS = 1e-5
NORM_EPS = 1e-6

LANE = 128
QKV_COLS = 3 * GDN_WIDTH
Z_OFF = QKV_COLS
BA_OFF = 4 * GDN_WIDTH
POOL_OFF = BA_OFF + 2 * LANE
PROJ_COLS = POOL_OFF + GDN_WIDTH
BA_BLK = BA_OFF // LANE
POOL_BLK = POOL_OFF // POOL_GROUP_DIM

ADAM_LR = 0.001
ADAM_B1 = 0.9
ADAM_B2 = 0.999
ADAM_EPS = 1e-08
ADAM_WD = 0.01
ADAM_STEP = 10

VMEM_LIMIT_BYTES = 48 * 1024 * 1024


def _params(*sem):
    return pltpu.CompilerParams(dimension_semantics=sem if sem else None, vmem_limit_bytes=VMEM_LIMIT_BYTES)


def _make_dots(cast, precision, batched=False):
    lead = 1 if batched else 0
    batch = ((0,), (0,)) if batched else ((), ())

    def dg(a, b, ca, cb):
        if cast is not None:
            a = a.astype(cast)
            b = b.astype(cast)
        return lax.dot_general(a, b, (((ca + lead,), (cb + lead,)), batch), precision=precision, preferred_element_type=F32)

    def nn_(a, b):
        return dg(a, b, 1, 0)

    def nt_(a, b):
        return dg(a, b, 1, 1)

    def tn_(a, b):
        return dg(a, b, 0, 0)

    @jax.custom_vjp
    def nn(a, b):
        return nn_(a, b)

    nn.defvjp(lambda a, b: (nn_(a, b), (a, b)), lambda r, g: (nt_(g, r[1]), tn_(r[0], g)))

    @jax.custom_vjp
    def nt(a, b):
        return nt_(a, b)

    nt.defvjp(lambda a, b: (nt_(a, b), (a, b)), lambda r, g: (nn_(g, r[1]), tn_(g, r[0])))

    @jax.custom_vjp
    def tn(a, b):
        return tn_(a, b)

    tn.defvjp(lambda a, b: (tn_(a, b), (a, b)), lambda r, g: (nt_(r[1], g), nn_(r[0], g)))

    return (nn_, nt_, tn_), (nn, nt, tn)


_BDOT_PLAIN, _BDOT_VJP = _make_dots(BF16, None)
_BDOT_BATCH_PLAIN, _BDOT_BATCH_VJP = _make_dots(BF16, None, batched=True)
_FDOT_BATCH_PLAIN, _FDOT_BATCH_VJP = _make_dots(BF16, None, batched=True)


def _mm(a, b, *, ta=False, tb=False, out_dtype=F32, tm=None, tn=512, tk=None, epi=None, extra=None, add_scale=1.0,
        b_chunks=False, o_chunks=False, after=None, name):
    m, k = (a.shape[1], a.shape[0]) if ta else a.shape
    if b_chunks:
        n, kb = (b.shape[1], N_DEV * b.shape[2]) if tb else (N_DEV * b.shape[2], b.shape[1])
    else:
        n, kb = b.shape if tb else (b.shape[1], b.shape[0])
    assert kb == k, (name, a.shape, b.shape)
    tm, tn, tk = min(tm or m, m), min(tn, n), min(tk or k, k)
    assert m % tm == 0 and n % tn == 0 and k % tk == 0, (name, m, n, k)
    nk = k // tk
    dims = (((0 if ta else 1,), (1 if tb else 0,)), ((), ()))
    n_extra = 0 if epi in (None, "relu2") else 1
    n_out = 2 if epi == "relu2" else 1
    if epi in ("relu2", "mul2r"):
        out_dtype = BF16
    n_after = 0 if after is None else 1

    def body(*refs):
        a_ref, b_ref = refs[:2]
        c_ref = refs[2] if n_extra else None
        o_refs = refs[2 + n_extra + n_after:2 + n_extra + n_after + n_out]
        scr = refs[2 + n_extra + n_after + n_out:]
        r = lax.dot_general(a_ref[...].astype(BF16), b_ref[...].astype(BF16), dims, preferred_element_type=F32)

        def finish(v):
            if epi == "add":
                o_refs[0][...] = (v + add_scale * c_ref[...]).astype(out_dtype)
            elif epi == "relu2":
                p = jnp.maximum(v, 0.0)
                o_refs[0][...] = (p * p).astype(BF16)
                o_refs[1][...] = p.astype(BF16)
            elif epi == "mul2r":
                o_refs[0][...] = (v * (2.0 * c_ref[...].astype(F32))).astype(BF16)
            else:
                o_refs[0][...] = v.astype(out_dtype)

        if nk == 1:
            finish(r)
        else:
            acc = scr[0]
            kk = pl.program_id(2)

            @pl.when(kk == 0)
            def _():
                acc[...] = r

            @pl.when(kk > 0)
            def _():
                acc[...] += r

            @pl.when(kk == nk - 1)
            def _():
                finish(acc[...])

    a_spec = pl.BlockSpec((tk, tm), lambda i, j, kk: (kk, i)) if ta else pl.BlockSpec((tm, tk), lambda i, j, kk: (i, kk))
    if b_chunks and tb:
        kc = k // N_DEV // tk
        b_spec = pl.BlockSpec((None, tn, tk), lambda i, j, kk: (kk // kc, j, kk % kc))
    elif b_chunks:
        nc = n // N_DEV // tn
        b_spec = pl.BlockSpec((None, tk, tn), lambda i, j, kk: (j // nc, kk, j % nc))
    elif tb:
        b_spec = pl.BlockSpec((tn, tk), lambda i, j, kk: (j, kk))
    else:
        b_spec = pl.BlockSpec((tk, tn), lambda i, j, kk: (kk, j))
    mn_spec = pl.BlockSpec((tm, tn), lambda i, j, kk: (i, j))
    if o_chunks:
        oc = n // N_DEV // tn
        o_spec = pl.BlockSpec((None, tm, tn), lambda i, j, kk: (j // oc, i, j % oc))
        o_shape = jax.ShapeDtypeStruct((N_DEV, m, n // N_DEV), out_dtype)
    else:
        o_spec, o_shape = mn_spec, jax.ShapeDtypeStruct((m, n), out_dtype)
    res = pl.pallas_call(
        body, grid=(m // tm, n // tn, nk),
        in_specs=[a_spec, b_spec] + [mn_spec] * n_extra + [pl.BlockSpec(memory_space=pl.ANY)] * n_after,
        out_specs=[o_spec] * n_out, out_shape=[o_shape] * n_out,
        scratch_shapes=[pltpu.VMEM((tm, tn), F32)] if nk > 1 else [],
        compiler_params=_params("parallel", "parallel", "arbitrary"), name=name,
    )(a, b, *([extra] if n_extra else []), *([after] if n_after else []))
    return res if n_out > 1 else res[0]


def _cast_bf16(v, *, name, tm=512):
    t, d = v.shape
    tm = min(tm, t)

    def body(v_ref, o_ref):
        o_ref[...] = v_ref[...].astype(BF16)

    spec = pl.BlockSpec((tm, d), lambda i: (i, 0))
    return pl.pallas_call(body, grid=(t // tm,), in_specs=[spec], out_specs=spec,
                          out_shape=jax.ShapeDtypeStruct((t, d), BF16), compiler_params=_params("parallel"), name=name)(v)


def _shift_down(v, s):
    if s == 0:
        return v
    row = lax.broadcasted_iota(jnp.int32, v.shape, 0)
    return jnp.where(row >= s, pltpu.roll(v, s, axis=0), 0.0)


def _shift_up(v, s):
    if s == 0:
        return v
    t = v.shape[0]
    row = lax.broadcasted_iota(jnp.int32, v.shape, 0)
    return jnp.where(row < t - s, pltpu.roll(v, t - s, axis=0), 0.0)


def _post_col(j):
    return (j % GDN_HEADS) * 3 + j // GDN_HEADS


def _gdn_prep_fwd(proj, conv_w):
    t = proj.shape[0]

    def body(x_ref, w_ref, o_ref):
        j = pl.program_id(0)
        x = x_ref[...]
        y = jnp.zeros_like(x)
        for tap in range(CONV_K):
            y = y + w_ref[tap:tap + 1, :] * _shift_down(x, CONV_K - 1 - tap)
        c = y * jax.nn.sigmoid(y)
        nrm = c * lax.rsqrt(jnp.sum(c * c, axis=1, keepdims=True) + NORM_EPS)
        o_ref[...] = jnp.where(j < 2 * GDN_HEADS, nrm, c)

    return pl.pallas_call(
        body, grid=(QKV_COLS // LANE,),
        in_specs=[pl.BlockSpec((t, LANE), lambda j: (0, j)), pl.BlockSpec((CONV_K, LANE), lambda j: (0, j))],
        out_specs=pl.BlockSpec((t, LANE), lambda j: (0, _post_col(j))),
        out_shape=jax.ShapeDtypeStruct((t, QKV_COLS), F32),
        compiler_params=_params("parallel"), name="gdn_prep_fwd",
    )(proj, conv_w)


def _gdn_prep_bwd(proj, conv_w, dpost, dproj):
    t = proj.shape[0]

    def body(x_ref, w_ref, d_ref, _, dx_ref, dw_ref):
        j = pl.program_id(0)
        x = x_ref[...]
        xs = [_shift_down(x, CONV_K - 1 - tap) for tap in range(CONV_K)]
        y = jnp.zeros_like(x)
        for tap in range(CONV_K):
            y = y + w_ref[tap:tap + 1, :] * xs[tap]
        sig = jax.nn.sigmoid(y)
        c = y * sig
        r = lax.rsqrt(jnp.sum(c * c, axis=1, keepdims=True) + NORM_EPS)
        nrm = c * r
        d = d_ref[...]
        dc_norm = r * (d - nrm * jnp.sum(d * nrm, axis=1, keepdims=True))
        dc = jnp.where(j < 2 * GDN_HEADS, dc_norm, d)
        dy = dc * (sig * (1.0 + y * (1.0 - sig)))
        dx = jnp.zeros_like(x)
        for tap in range(CONV_K):
            dx = dx + _shift_up(w_ref[tap:tap + 1, :] * dy, CONV_K - 1 - tap)
            dw_ref[tap:tap + 1, :] = jnp.sum(dy * xs[tap], axis=0, keepdims=True)
        dx_ref[...] = dx.astype(dx_ref.dtype)

    return pl.pallas_call(
        body, grid=(QKV_COLS // LANE,),
        in_specs=[pl.BlockSpec((t, LANE), lambda j: (0, j)), pl.BlockSpec((CONV_K, LANE), lambda j: (0, j)),
                  pl.BlockSpec((t, LANE), lambda j: (0, _post_col(j))), pl.BlockSpec(memory_space=pl.ANY)],
        out_specs=[pl.BlockSpec((t, LANE), lambda j: (0, j)), pl.BlockSpec((CONV_K, LANE), lambda j: (0, j))],
        out_shape=[jax.ShapeDtypeStruct(dproj.shape, dproj.dtype), jax.ShapeDtypeStruct((CONV_K, QKV_COLS), F32)],
        input_output_aliases={3: 0},
        compiler_params=_params("parallel"), name="gdn_prep_bwd",
    )(proj, conv_w, dpost, dproj)


def _softplus(v):
    return jnp.maximum(v, 0.0) + jnp.log(1.0 + jnp.exp(-jnp.abs(v)))


def _tri_inv(low, nn):
    r = lax.broadcasted_iota(jnp.int32, (CHUNK, CHUNK), 0)
    c = lax.broadcasted_iota(jnp.int32, (CHUNK, CHUNK), 1)
    eye = (r == c).astype(F32)
    same_blk = lax.shift_right_logical(r, 4) == lax.shift_right_logical(c, 4)
    diag = jnp.where(same_blk, low, 0.0)
    off = low - diag
    n1 = -diag
    n2 = nn(n1, n1)
    n4 = nn(n2, n2)
    n8 = nn(n4, n4)
    inv_d = nn(nn(nn(eye + n1, eye + n2), eye + n4), eye + n8)
    m1 = nn(inv_d, off)
    m2 = nn(m1, m1)
    return nn(nn(eye - m1, eye + m2), inv_d)


@jax.custom_vjp
def _tri_inv_known(low, t_inv):
    return t_inv


def _tri_inv_known_fwd(low, t_inv):
    return t_inv, t_inv


def _tri_inv_known_bwd(t_inv, g):
    _, nt, tn = _FDOT_BATCH_PLAIN
    return -nt(tn(t_inv, g), t_inv), jnp.zeros_like(t_inv)


_tri_inv_known.defvjp(_tri_inv_known_fwd, _tri_inv_known_bwd)


LOCAL_HEADS_PER_STEP = 8


def _gdn_local_fn(qkv, ba, alog_row, dtb_row, first_head, bdots, fdots, t_known=None):
    nn, nt, tn = bdots
    fnn = fdots[0]
    n_heads = qkv.shape[1] // (3 * HEAD_DIM)
    part = lambda i, p: qkv[:, (3 * i + p) * HEAD_DIM:(3 * i + p + 1) * HEAD_DIM]
    q = jnp.stack([part(i, 0) for i in range(n_heads)]) * (HEAD_DIM ** -0.5)
    k = jnp.stack([part(i, 1) for i in range(n_heads)])
    v = jnp.stack([part(i, 2) for i in range(n_heads)])
    lane = lax.broadcasted_iota(jnp.int32, ba.shape, 1)
    bg = jnp.where(lane < GDN_HEADS, jax.nn.sigmoid(ba), -jnp.exp(alog_row) * _softplus(ba + dtb_row))
    pick = lambda l: jnp.sum(jnp.where(lane == l, bg, 0.0), axis=1, keepdims=True)
    beta = jnp.stack([pick(first_head + i) for i in range(n_heads)])
    g = jnp.stack([pick(first_head + i + GDN_HEADS) for i in range(n_heads)])

    r = lax.broadcasted_iota(jnp.int32, (CHUNK, CHUNK), 0)
    c = lax.broadcasted_iota(jnp.int32, (CHUNK, CHUNK), 1)
    incl = r >= c
    strict = r > c
    eye = r == c

    def to_row(col):
        return jnp.sum(jnp.where(eye, col, 0.0), axis=1, keepdims=True)

    gc = jnp.sum(jnp.where(incl, to_row(g), 0.0), axis=2, keepdims=True)
    diff = gc - to_row(gc)
    decay = jnp.where(incl, jnp.exp(jnp.where(incl, diff, 0.0)), 0.0)
    k_beta = k * beta
    v_beta = v * beta
    low = jnp.where(strict, nt(k_beta, k) * decay, 0.0)
    t_inv = _tri_inv(low, fnn) if t_known is None else _tri_inv_known(low, t_known)
    eg = jnp.exp(gc)
    u = fnn(t_inv, v_beta)
    w = fnn(t_inv, k_beta * eg)
    attn = jnp.where(incl, nt(q, k) * decay, 0.0)
    last = lax.broadcasted_iota(jnp.int32, (CHUNK, 1), 0) == CHUNK - 1
    g_last = jnp.sum(jnp.where(last, gc, 0.0), axis=1, keepdims=True)
    kdec = k * jnp.exp(g_last - gc)
    elast = jnp.broadcast_to(jnp.exp(g_last), (n_heads, 1, LANE))
    return u, w, q * eg, kdec, attn, elast, t_inv


def _gdn_state_fn(u, w, qg, kdec, attn, elast, state, bdots):
    nn, _, tn = bdots
    v_new = u - nn(w, state)
    o = nn(qg, state) + nn(attn, v_new)
    return o, state * elast + tn(kdec, v_new)


def _gdn_local_fwd(post, proj, alog_row, dtb_row):
    t = post.shape[0]
    n_chunks = t // CHUNK
    hb = LOCAL_HEADS_PER_STEP

    def body(qkv_ref, ba_ref, al_ref, dt_ref, u_ref, w_ref, qg_ref, kd_ref, at_ref, el_ref, ti_ref):
        u, w, qg, kdec, attn, elast, t_inv = _gdn_local_fn(qkv_ref[...], ba_ref[...], al_ref[...], dt_ref[...],
                                                           pl.program_id(1) * hb, _BDOT_BATCH_PLAIN, _FDOT_BATCH_PLAIN)
        for i in range(hb):
            cols = slice(i * HEAD_DIM, (i + 1) * HEAD_DIM)
            u_ref[:, cols] = u[i]
            w_ref[:, cols] = w[i].astype(BF16)
            qg_ref[:, cols] = qg[i].astype(BF16)
            kd_ref[:, cols] = kdec[i].astype(BF16)
        at_ref[...] = attn.astype(BF16)
        el_ref[:, 0] = elast
        ti_ref[...] = t_inv

    wide = pl.BlockSpec((CHUNK, hb * HEAD_DIM), lambda n, j: (n, j))
    square = pl.BlockSpec((hb, CHUNK, CHUNK), lambda n, j: (j, n, 0))
    row = pl.BlockSpec((1, LANE), lambda n, j: (0, 0))
    res = pl.pallas_call(
        body, grid=(n_chunks, GDN_HEADS // hb),
        in_specs=[pl.BlockSpec((CHUNK, hb * 3 * HEAD_DIM), lambda n, j: (n, j)),
                  pl.BlockSpec((CHUNK, LANE), lambda n, j: (n, BA_BLK)), row, row],
        out_specs=[wide, wide, wide, wide, square, pl.BlockSpec((hb, 1, 1, LANE), lambda n, j: (j, n, 0, 0)), square],
        out_shape=[jax.ShapeDtypeStruct((t, GDN_WIDTH), F32), jax.ShapeDtypeStruct((t, GDN_WIDTH), BF16),
                   jax.ShapeDtypeStruct((t, GDN_WIDTH), BF16), jax.ShapeDtypeStruct((t, GDN_WIDTH), BF16),
                   jax.ShapeDtypeStruct((GDN_HEADS, t, CHUNK), BF16),
                   jax.ShapeDtypeStruct((GDN_HEADS, n_chunks, 1, LANE), F32),
                   jax.ShapeDtypeStruct((GDN_HEADS, t, CHUNK), F32)],
        compiler_params=_params("parallel", "parallel"), name="gdn_local_fwd",
    )(post, proj, alog_row, dtb_row)
    return tuple(res[:6]), res[6]


def _by_head(ref):
    return jnp.stack([ref[:, h * HEAD_DIM:(h + 1) * HEAD_DIM] for h in range(ref.shape[1] // HEAD_DIM)])


def _gdn_state_specs(n_of):
    wide = pl.BlockSpec((CHUNK, GDN_WIDTH), lambda n: (n_of(n), 0))
    attn = pl.BlockSpec((GDN_HEADS, CHUNK, CHUNK), lambda n: (0, n_of(n), 0))
    elast = pl.BlockSpec((GDN_HEADS, 1, 1, LANE), lambda n: (0, n_of(n), 0, 0))
    saved = pl.BlockSpec((GDN_HEADS, 1, HEAD_DIM, HEAD_DIM), lambda n: (0, n_of(n), 0, 0))
    return wide, attn, elast, saved


def _gdn_state_fwd(u, w, qg, kdec, attn, elast):
    t = u.shape[0]
    n_chunks = t // CHUNK

    def body(u_ref, w_ref, qg_ref, kd_ref, at_ref, el_ref, o_ref, save_ref, state_ref):
        @pl.when(pl.program_id(0) == 0)
        def _():
            state_ref[...] = jnp.zeros_like(state_ref)

        state = state_ref[...]
        save_ref[:, 0] = state
        o, new_state = _gdn_state_fn(_by_head(u_ref), _by_head(w_ref), _by_head(qg_ref), _by_head(kd_ref), at_ref[...],
                                     el_ref[:, 0], state, _BDOT_BATCH_PLAIN)
        for h in range(GDN_HEADS):
            o_ref[:, h * HEAD_DIM:(h + 1) * HEAD_DIM] = o[h]
        state_ref[...] = new_state

    wide, attn_spec, elast_spec, saved_spec = _gdn_state_specs(lambda n: n)
    return pl.pallas_call(
        body, grid=(n_chunks,), in_specs=[wide, wide, wide, wide, attn_spec, elast_spec],
        out_specs=[wide, saved_spec],
        out_shape=[jax.ShapeDtypeStruct((t, GDN_WIDTH), F32),
                   jax.ShapeDtypeStruct((GDN_HEADS, n_chunks, HEAD_DIM, HEAD_DIM), F32)],
        scratch_shapes=[pltpu.VMEM((GDN_HEADS, HEAD_DIM, HEAD_DIM), F32)],
        compiler_params=_params("arbitrary"), name="gdn_state_fwd",
    )(u, w, qg, kdec, attn, elast)


def _gdn_state_bwd(u, w, qg, kdec, attn, elast, saved, do):
    t = u.shape[0]
    n_chunks = t // CHUNK
    last = n_chunks - 1

    def body(u_ref, w_ref, qg_ref, kd_ref, at_ref, el_ref, save_ref, do_ref,
             du_ref, dw_ref, dqg_ref, dkd_ref, dat_ref, del_ref, dstate_ref):
        @pl.when(pl.program_id(0) == 0)
        def _():
            dstate_ref[...] = jnp.zeros_like(dstate_ref)

        _, vjp = jax.vjp(
            lambda *a: _gdn_state_fn(*a, _BDOT_BATCH_VJP), _by_head(u_ref), _by_head(w_ref).astype(F32),
            _by_head(qg_ref).astype(F32), _by_head(kd_ref).astype(F32), at_ref[...].astype(F32), el_ref[:, 0],
            save_ref[:, 0])
        du, dw, dqg, dkd, dat, de, dstate = vjp((_by_head(do_ref), dstate_ref[...]))
        for h in range(GDN_HEADS):
            cols = slice(h * HEAD_DIM, (h + 1) * HEAD_DIM)
            du_ref[:, cols] = du[h]
            dw_ref[:, cols] = dw[h]
            dqg_ref[:, cols] = dqg[h]
            dkd_ref[:, cols] = dkd[h]
        dat_ref[...] = dat
        del_ref[:, 0] = de
        dstate_ref[...] = dstate

    wide, attn_spec, elast_spec, saved_spec = _gdn_state_specs(lambda n: last - n)
    wide_f32 = jax.ShapeDtypeStruct((t, GDN_WIDTH), F32)
    return pl.pallas_call(
        body, grid=(n_chunks,), in_specs=[wide, wide, wide, wide, attn_spec, elast_spec, saved_spec, wide],
        out_specs=[wide, wide, wide, wide, attn_spec, elast_spec],
        out_shape=[wide_f32, wide_f32, wide_f32, wide_f32, jax.ShapeDtypeStruct((GDN_HEADS, t, CHUNK), F32),
                   jax.ShapeDtypeStruct((GDN_HEADS, n_chunks, 1, LANE), F32)],
        scratch_shapes=[pltpu.VMEM((GDN_HEADS, HEAD_DIM, HEAD_DIM), F32)],
        compiler_params=_params("arbitrary"), name="gdn_state_bwd",
    )(u, w, qg, kdec, attn, elast, saved, do)


def _gdn_local_bwd(post, proj, alog_row, dtb_row, t_inv, cots, dproj):
    t = post.shape[0]
    n_chunks = t // CHUNK
    hb = LOCAL_HEADS_PER_STEP
    n_steps = GDN_HEADS // hb

    def body(qkv_ref, ba_ref, al_ref, dt_ref, ti_ref, du_ref, dw_ref, dqg_ref, dkd_ref, dat_ref, del_ref, _,
             dqkv_ref, dba_ref, dal_ref, ddt_ref, dba_acc):
        n = pl.program_id(0)
        j = pl.program_id(1)

        @pl.when((n == 0) & (j == 0))
        def _():
            dal_ref[...] = jnp.zeros_like(dal_ref)
            ddt_ref[...] = jnp.zeros_like(ddt_ref)

        @pl.when(j == 0)
        def _():
            dba_acc[...] = jnp.zeros_like(dba_acc)

        t_known = ti_ref[...]
        _, vjp = jax.vjp(
            lambda a, b, c, d: _gdn_local_fn(a, b, c, d, j * hb, _BDOT_BATCH_VJP, _FDOT_BATCH_VJP, t_known)[:6],
            qkv_ref[...], ba_ref[...], al_ref[...], dt_ref[...])
        dqkv, dba, dal, ddt = vjp((_by_head(du_ref), _by_head(dw_ref), _by_head(dqg_ref), _by_head(dkd_ref), dat_ref[...],
                                   del_ref[:, 0]))
        dqkv_ref[...] = dqkv
        dba_acc[...] += dba
        dal_ref[...] += dal
        ddt_ref[...] += ddt

        @pl.when(j == n_steps - 1)
        def _():
            dba_ref[:, 0:LANE] = dba_acc[...].astype(dba_ref.dtype)
            dba_ref[:, LANE:2 * LANE] = jnp.zeros((CHUNK, LANE), dba_ref.dtype)

    wide = pl.BlockSpec((CHUNK, hb * HEAD_DIM), lambda n, j: (n, j))
    qkv_spec = pl.BlockSpec((CHUNK, hb * 3 * HEAD_DIM), lambda n, j: (n, j))
    row = pl.BlockSpec((1, LANE), lambda n, j: (0, 0))
    return pl.pallas_call(
        body, grid=(n_chunks, n_steps),
        in_specs=[qkv_spec, pl.BlockSpec((CHUNK, LANE), lambda n, j: (n, BA_BLK)), row, row,
                  pl.BlockSpec((hb, CHUNK, CHUNK), lambda n, j: (j, n, 0)), wide, wide, wide, wide,
                  pl.BlockSpec((hb, CHUNK, CHUNK), lambda n, j: (j, n, 0)),
                  pl.BlockSpec((hb, 1, 1, LANE), lambda n, j: (j, n, 0, 0)), pl.BlockSpec(memory_space=pl.ANY)],
        out_specs=[qkv_spec, pl.BlockSpec((CHUNK, 2 * LANE), lambda n, j: (n, BA_BLK // 2)), row, row],
        out_shape=[jax.ShapeDtypeStruct((t, QKV_COLS), F32), jax.ShapeDtypeStruct(dproj.shape, dproj.dtype),
                   jax.ShapeDtypeStruct((1, LANE), F32), jax.ShapeDtypeStruct((1, LANE), F32)],
        input_output_aliases={11: 1},
        scratch_shapes=[pltpu.VMEM((CHUNK, LANE), F32)],
        compiler_params=_params("arbitrary", "arbitrary"), name="gdn_local_bwd",
    )(post, proj, alog_row, dtb_row, t_inv, *cots, dproj)


def _onorm_fn(o, z, w):
    return o * lax.rsqrt(jnp.mean(o * o, axis=1, keepdims=True) + NORM_EPS) * w * (z * jax.nn.sigmoid(z))


_Z_WIDE_BLK = Z_OFF // GDN_WIDTH


def _onorm_fwd(o_raw, proj, norm_w, mixin, tm=256):
    t = o_raw.shape[0]
    tm = min(tm, t)

    def body(o_ref, z_ref, w_ref, _, out_ref):
        for h in range(GDN_HEADS):
            cols = slice(h * HEAD_DIM, (h + 1) * HEAD_DIM)
            out_ref[:, cols] = _onorm_fn(o_ref[:, cols], z_ref[:, cols], w_ref[...]).astype(out_ref.dtype)

    wide = pl.BlockSpec((tm, GDN_WIDTH), lambda i: (i, 0))
    return pl.pallas_call(
        body, grid=(t // tm,),
        in_specs=[wide, pl.BlockSpec((tm, GDN_WIDTH), lambda i: (i, _Z_WIDE_BLK)), pl.BlockSpec((1, LANE), lambda i: (0, 0)),
                  pl.BlockSpec(memory_space=pl.ANY)],
        out_specs=wide, out_shape=jax.ShapeDtypeStruct(mixin.shape, mixin.dtype), input_output_aliases={3: 0},
        compiler_params=_params("parallel"), name="gdn_onorm_fwd",
    )(o_raw, proj, norm_w, mixin)


def _onorm_bwd(o_raw, proj, norm_w, dmixin, dproj, tm=256):
    t = o_raw.shape[0]
    tm = min(tm, t)

    def body(o_ref, z_ref, w_ref, d_ref, _, do_ref, dz_ref, dw_ref):
        @pl.when(pl.program_id(0) == 0)
        def _():
            dw_ref[...] = jnp.zeros_like(dw_ref)

        for h in range(GDN_HEADS):
            cols = slice(h * HEAD_DIM, (h + 1) * HEAD_DIM)
            _, vjp = jax.vjp(_onorm_fn, o_ref[:, cols], z_ref[:, cols], w_ref[...])
            do, dz, dw = vjp(d_ref[:, cols])
            do_ref[:, cols] = do
            dz_ref[:, cols] = dz.astype(dz_ref.dtype)
            dw_ref[...] += dw

    wide = pl.BlockSpec((tm, GDN_WIDTH), lambda i: (i, 0))
    gate = pl.BlockSpec((tm, GDN_WIDTH), lambda i: (i, _Z_WIDE_BLK))
    row = pl.BlockSpec((1, LANE), lambda i: (0, 0))
    return pl.pallas_call(
        body, grid=(t // tm,), in_specs=[wide, gate, row, wide, pl.BlockSpec(memory_space=pl.ANY)],
        out_specs=[wide, gate, row],
        out_shape=[jax.ShapeDtypeStruct((t, GDN_WIDTH), F32), jax.ShapeDtypeStruct(dproj.shape, dproj.dtype),
                   jax.ShapeDtypeStruct((1, LANE), F32)],
        input_output_aliases={4: 1},
        compiler_params=_params("arbitrary"), name="gdn_onorm_bwd",
    )(o_raw, proj, norm_w, dmixin, dproj)


def _pool_select(levels, gi):
    out = levels[-1]
    for lvl in range(len(levels) - 2, -1, -1):
        out = jnp.where(gi == lvl, levels[lvl], out)
    return out


def _pool_count(shape, gi):
    pos = lax.broadcasted_iota(jnp.int32, shape, 0)
    win = lax.shift_left(jnp.int32(2), gi)
    return jnp.minimum(pos + 1, win).astype(F32)


def _pooled(p, gi):
    acc = p
    levels = []
    for lvl in range(POOL_GROUPS):
        acc = acc + _shift_down(acc, 1 << lvl)
        levels.append(acc)
    return _pool_select(levels, gi) / _pool_count(p.shape, gi) - p


def _pool_fwd(proj, pool_w, pool_scale):
    t = proj.shape[0]

    def body(p_ref, w_ref, s_ref, out_ref):
        gi = pl.program_id(0)
        pooled = _pooled(p_ref[...], gi)
        out_ref[...] = (_BDOT_PLAIN[0](pooled, w_ref[0]) * s_ref[0]).astype(out_ref.dtype)

    return pl.pallas_call(
        body, grid=(POOL_GROUPS,),
        in_specs=[pl.BlockSpec((t, POOL_GROUP_DIM), lambda g: (0, POOL_BLK + g)),
                  pl.BlockSpec((1, POOL_GROUP_DIM, POOL_GROUP_DIM), lambda g: (g, 0, 0)),
                  pl.BlockSpec((1, 1, POOL_GROUP_DIM), lambda g: (g, 0, 0))],
        out_specs=pl.BlockSpec((t, POOL_GROUP_DIM), lambda g: (0, GDN_WIDTH // POOL_GROUP_DIM + g)),
        out_shape=jax.ShapeDtypeStruct((t, 2 * GDN_WIDTH), BF16),
        compiler_params=_params("parallel"), name="pool_fwd",
    )(proj, pool_w, pool_scale)


def _pool_bwd(proj, pool_w, pool_scale, dmixin):
    t = proj.shape[0]
    nn, nt, tn = _BDOT_PLAIN

    def body(p_ref, w_ref, s_ref, d_ref, dp_ref, dw_ref, ds_ref):
        gi = pl.program_id(0)
        p = p_ref[...]
        pooled = _pooled(p, gi)
        mixed = nn(pooled, w_ref[0])
        d = d_ref[...]
        ds_ref[0] = jnp.sum(d * mixed, axis=0, keepdims=True)
        dmixed = d * s_ref[0]
        dw_ref[0] = tn(pooled, dmixed)
        dpooled = nt(dmixed, w_ref[0])
        acc = dpooled / _pool_count(p.shape, gi)
        levels = []
        for lvl in range(POOL_GROUPS):
            acc = acc + _shift_up(acc, 1 << lvl)
            levels.append(acc)
        dp_ref[...] = (_pool_select(levels, gi) - dpooled).astype(dp_ref.dtype)

    return pl.pallas_call(
        body, grid=(POOL_GROUPS,),
        in_specs=[pl.BlockSpec((t, POOL_GROUP_DIM), lambda g: (0, POOL_BLK + g)),
                  pl.BlockSpec((1, POOL_GROUP_DIM, POOL_GROUP_DIM), lambda g: (g, 0, 0)),
                  pl.BlockSpec((1, 1, POOL_GROUP_DIM), lambda g: (g, 0, 0)),
                  pl.BlockSpec((t, POOL_GROUP_DIM), lambda g: (0, GDN_WIDTH // POOL_GROUP_DIM + g))],
        out_specs=[pl.BlockSpec((t, POOL_GROUP_DIM), lambda g: (0, POOL_BLK + g)),
                   pl.BlockSpec((1, POOL_GROUP_DIM, POOL_GROUP_DIM), lambda g: (g, 0, 0)),
                   pl.BlockSpec((1, 1, POOL_GROUP_DIM), lambda g: (g, 0, 0))],
        out_shape=[jax.ShapeDtypeStruct((t, PROJ_COLS), BF16),
                   jax.ShapeDtypeStruct((POOL_GROUPS, POOL_GROUP_DIM, POOL_GROUP_DIM), F32),
                   jax.ShapeDtypeStruct((POOL_GROUPS, 1, POOL_GROUP_DIM), F32)],
        compiler_params=_params("parallel"), name="pool_bwd",
    )(proj, pool_w, pool_scale, dmixin)


def _ln_stats(s):
    mu = jnp.mean(s, axis=1, keepdims=True)
    xc = s - mu
    var = jnp.mean(xc * xc, axis=1, keepdims=True)
    rstd = lax.rsqrt(var + LN_EPS)
    return xc * rstd, rstd


def _ln_fwd(h_in, y, g, b, *, name, tm=512):
    t, d = h_in.shape
    tm = min(tm, t)

    def body(h_ref, y_ref, g_ref, b_ref, o_ref, o16_ref):
        xhat, _ = _ln_stats(ALPHA * h_ref[...] + y_ref[...])
        out = xhat * g_ref[...] + b_ref[...]
        o_ref[...] = out
        o16_ref[...] = out.astype(BF16)

    row = pl.BlockSpec((tm, d), lambda i: (i, 0))
    vec = pl.BlockSpec((1, d), lambda i: (0, 0))
    return pl.pallas_call(
        body, grid=(t // tm,), in_specs=[row, row, vec, vec], out_specs=[row, row],
        out_shape=[jax.ShapeDtypeStruct((t, d), F32), jax.ShapeDtypeStruct((t, d), BF16)],
        compiler_params=_params("parallel"), name=name,
    )(h_in, y, g, b)


def _ln_backward(xhat, rstd, dout, gain):
    dxhat = dout * gain
    m1 = jnp.mean(dxhat, axis=1, keepdims=True)
    m2 = jnp.mean(dxhat * xhat, axis=1, keepdims=True)
    return (rstd * (dxhat - m1 - xhat * m2), jnp.sum(dout * xhat, axis=0, keepdims=True),
            jnp.sum(dout, axis=0, keepdims=True))


def _ln_loss(h_in, y, g, b, target, *, name, tm=256):
    t, d = h_in.shape
    tm = min(tm, t)

    def body(h_ref, y_ref, g_ref, b_ref, t_ref, sq_ref, ds_ref, ds16_ref, dg_ref, dbias_ref):
        @pl.when(pl.program_id(0) == 0)
        def _():
            sq_ref[...] = jnp.zeros_like(sq_ref)
            dg_ref[...] = jnp.zeros_like(dg_ref)
            dbias_ref[...] = jnp.zeros_like(dbias_ref)

        xhat, rstd = _ln_stats(ALPHA * h_ref[...] + y_ref[...])
        err = xhat * g_ref[...] + b_ref[...] - t_ref[...]
        sq_ref[...] += jnp.sum(jnp.sum(err * err, axis=1, keepdims=True), axis=0, keepdims=True)
        ds, dg, dbias = _ln_backward(xhat, rstd, err * (1.0 / d), g_ref[...])
        ds_ref[...] = ds
        ds16_ref[...] = ds.astype(BF16)
        dg_ref[...] += dg
        dbias_ref[...] += dbias

    row = pl.BlockSpec((tm, d), lambda i: (i, 0))
    vec = pl.BlockSpec((1, d), lambda i: (0, 0))
    return pl.pallas_call(
        body, grid=(t // tm,), in_specs=[row, row, vec, vec, row],
        out_specs=[pl.BlockSpec((1, LANE), lambda i: (0, 0)), row, row, vec, vec],
        out_shape=[jax.ShapeDtypeStruct((1, LANE), F32), jax.ShapeDtypeStruct((t, d), F32),
                   jax.ShapeDtypeStruct((t, d), BF16), jax.ShapeDtypeStruct((1, d), F32), jax.ShapeDtypeStruct((1, d), F32)],
        compiler_params=_params("arbitrary"), name=name,
    )(h_in, y, g, b, target)


def _ln_bwd(h_in, y, g, d_a, d_b, *, name, tm=256):
    t, d = h_in.shape
    tm = min(tm, t)
    has_b = d_b is not None

    def body(*refs):
        if has_b:
            h_ref, y_ref, g_ref, da_ref, db_ref, ds_ref, ds16_ref, dg_ref, dbias_ref = refs
        else:
            h_ref, y_ref, g_ref, da_ref, ds_ref, ds16_ref, dg_ref, dbias_ref = refs

        @pl.when(pl.program_id(0) == 0)
        def _():
            dg_ref[...] = jnp.zeros_like(dg_ref)
            dbias_ref[...] = jnp.zeros_like(dbias_ref)

        xhat, rstd = _ln_stats(ALPHA * h_ref[...] + y_ref[...])
        dout = da_ref[...]
        if has_b:
            dout = dout + ALPHA * db_ref[...]
        ds, dg, dbias = _ln_backward(xhat, rstd, dout, g_ref[...])
        ds_ref[...] = ds
        ds16_ref[...] = ds.astype(BF16)
        dg_ref[...] += dg
        dbias_ref[...] += dbias

    row = pl.BlockSpec((tm, d), lambda i: (i, 0))
    vec = pl.BlockSpec((1, d), lambda i: (0, 0))
    args = [h_in, y, g, d_a] + ([d_b] if has_b else [])
    return pl.pallas_call(
        body, grid=(t // tm,), in_specs=[row, row, vec, row] + ([row] if has_b else []),
        out_specs=[row, row, vec, vec],
        out_shape=[jax.ShapeDtypeStruct((t, d), F32), jax.ShapeDtypeStruct((t, d), BF16),
                   jax.ShapeDtypeStruct((1, d), F32), jax.ShapeDtypeStruct((1, d), F32)],
        compiler_params=_params("arbitrary"), name=name,
    )(*args)


def _attn_fn(q, k, v, dots):
    nn, nt, _ = dots
    s = nt(q, k) * (XATTN_HEAD_DIM ** -0.5)
    s = s - lax.stop_gradient(jnp.max(s, axis=1, keepdims=True))
    e = jnp.exp(s)
    p = e / jnp.sum(e, axis=1, keepdims=True)
    return nn(p, v)


def _attn_fwd(q, k, v, tq=2048):
    t = q.shape[0]
    tq = min(tq, t)

    def body(q_ref, k_ref, v_ref, o_ref):
        o_ref[...] = _attn_fn(q_ref[...], k_ref[...], v_ref[...], _BDOT_PLAIN).astype(BF16)

    qs = pl.BlockSpec((tq, XATTN_HEAD_DIM), lambda h, i: (i, h))
    ks = pl.BlockSpec((MEM_LEN, XATTN_HEAD_DIM), lambda h, i: (0, h))
    return pl.pallas_call(
        body, grid=(XATTN_HEADS, t // tq), in_specs=[qs, ks, ks], out_specs=qs,
        out_shape=jax.ShapeDtypeStruct(q.shape, BF16), compiler_params=_params("parallel", "parallel"), name="xattn_fwd",
    )(q, k, v)


def _attn_bwd(q, k, v, do, tq=1024):
    t = q.shape[0]
    tq = min(tq, t)

    def body(q_ref, k_ref, v_ref, do_ref, dq_ref, dk_ref, dv_ref):
        @pl.when(pl.program_id(1) == 0)
        def _():
            dk_ref[...] = jnp.zeros_like(dk_ref)
            dv_ref[...] = jnp.zeros_like(dv_ref)

        _, vjp = jax.vjp(lambda a, b, c: _attn_fn(a, b, c, _BDOT_VJP), q_ref[...].astype(F32), k_ref[...].astype(F32),
                         v_ref[...].astype(F32))
        dq, dk, dv = vjp(do_ref[...].astype(F32))
        dq_ref[...] = dq.astype(BF16)
        dk_ref[...] += dk
        dv_ref[...] += dv

    qs = pl.BlockSpec((tq, XATTN_HEAD_DIM), lambda h, i: (i, h))
    ks = pl.BlockSpec((MEM_LEN, XATTN_HEAD_DIM), lambda h, i: (0, h))
    return pl.pallas_call(
        body, grid=(XATTN_HEADS, t // tq), in_specs=[qs, ks, ks, qs], out_specs=[qs, ks, ks],
        out_shape=[jax.ShapeDtypeStruct(q.shape, BF16), jax.ShapeDtypeStruct(k.shape, F32), jax.ShapeDtypeStruct(v.shape, F32)],
        compiler_params=_params("parallel", "arbitrary"), name="xattn_bwd",
    )(q, k, v, do)


def _local_step(x, x16, mem, target, weights_of, grads_ready):
    def behind(vec, token):
        return vec if token is None else vec + token

    w = dict(weights_of("mixer", None))
    proj = _mm(x16, w["w_in"], tb=True, tn=768, name="mm_in_proj")
    mixin = _pool_fwd(proj, w["pool_w"], w["pool_scale"])
    post = _gdn_prep_fwd(proj, w["conv_w"])
    token = weights_of("ahead_conv", post)
    chunked, t_inv = _gdn_local_fwd(post, proj, behind(w["alog_row"], token), w["dtb_row"])
    o_raw, saved = _gdn_state_fwd(*chunked)
    token = weights_of("ahead_scan", o_raw)
    mixin = _onorm_fwd(o_raw, proj, behind(w["gdn_norm_w"], token), mixin)
    w.update(weights_of("attn", mixin))
    mix = _mm(mixin, w["w_out"], name="mm_out_proj")
    h1, h1_16 = _ln_fwd(x, mix, w["ln1_g"], w["ln1_b"], name="ln1_fwd")
    xq = _mm(h1_16, w["xq_w"], out_dtype=BF16, name="mm_xq")
    xk = _mm(mem, w["xk_w"], out_dtype=BF16, name="mm_xk")
    xv = _mm(mem, w["xv_w"], out_dtype=BF16, name="mm_xv")
    xo = _attn_fwd(xq, xk, xv)
    token = weights_of("ahead_attn", xo)
    if token is not None:
        xo, _ = lax.optimization_barrier((xo, token))
    xa = _mm(xo, w["xo_w"], name="mm_xo")
    h2, h2_16 = _ln_fwd(h1, xa, w["ln2_g"], w["ln2_b"], name="ln2_fwd")
    w.update(weights_of("up", h2_16))
    act, relu = _mm(h2_16, w["w_up"], b_chunks=True, epi="relu2", name="mm_up")
    w.update(weights_of("down", act))
    ff = _mm(act, w["w_down"], tn=512, tk=2048, name="mm_down")
    g = {}
    sq, ds3, ds3_16, g["ln3_g"], g["ln3_b"] = _ln_loss(h2, ff, w["ln3_g"], w["ln3_b"], target, name="ln3_loss")

    gw_down = _mm(act, ds3_16, ta=True, out_dtype=BF16, tm=512, tn=D_MODEL, name="mm_gw_down")
    du = _mm(ds3_16, w["w_down"], tb=True, epi="mul2r", extra=relu, name="mm_du")
    gw_up = _mm(h2_16, du, ta=True, out_dtype=BF16, o_chunks=True, name="mm_gw_up")
    token = grads_ready("mlp", {"w_down": gw_down, "w_up": gw_up})
    dh2 = _mm(du, w["w_up"], tb=True, b_chunks=True, tn=1024, tk=1024, name="mm_dh2")
    ds2, ds2_16, g["ln2_g"], g["ln2_b"] = _ln_bwd(h1, xa, behind(w["ln2_g"], token), dh2, ds3, name="ln2_bwd")
    gw_xo = _mm(xo, ds2_16, ta=True, out_dtype=BF16, name="mm_gw_xo")
    dxo = _mm(ds2_16, w["xo_w"], tb=True, out_dtype=BF16, name="mm_dxo")
    dxq, dxk, dxv = _attn_bwd(xq, xk, xv, dxo)
    gw_xq = _mm(h1_16, dxq, ta=True, out_dtype=BF16, name="mm_gw_xq")
    gw_xk = _mm(mem, dxk, ta=True, out_dtype=BF16, name="mm_gw_xk")
    gw_xv = _mm(mem, dxv, ta=True, out_dtype=BF16, name="mm_gw_xv")
    token = grads_ready("attn", {"xo_w": gw_xo, "xq_w": gw_xq, "xk_w": gw_xk, "xv_w": gw_xv})
    dh1 = _mm(dxq, w["xq_w"], tb=True, name="mm_dh1")
    ds1, ds1_16, g["ln1_g"], g["ln1_b"] = _ln_bwd(x, mix, behind(w["ln1_g"], token), dh1, ds2, name="ln1_bwd")
    gw_out = _mm(mixin, ds1_16, ta=True, out_dtype=BF16, name="mm_gw_out")
    dmixin = _mm(ds1_16, w["w_out"], tb=True, name="mm_dmixin")
    dproj, gw_pool, g["pool_scale"] = _pool_bwd(proj, w["pool_w"], w["pool_scale"], dmixin)
    token = grads_ready("mix", {"w_out": gw_out, "pool_w": gw_pool})
    do_raw, dproj, g["gdn_norm_w"] = _onorm_bwd(o_raw, proj, behind(w["gdn_norm_w"], token), dmixin, dproj)
    cots = _gdn_state_bwd(*chunked, saved, do_raw)
    token = grads_ready("tick", {"after": cots[0]})
    dpost, dproj, g["alog_row"], g["dtb_row"] = _gdn_local_bwd(post, proj, behind(w["alog_row"], token), w["dtb_row"],
                                                               t_inv, cots, dproj)
    dproj, g["conv_w"] = _gdn_prep_bwd(proj, w["conv_w"], dpost, dproj)
    token = grads_ready("small", {**g, "sq": sq})
    gw_in = _mm(dproj, x16, ta=True, out_dtype=BF16, tm=768, tn=D_MODEL, after=token, name="mm_gw_in")
    token = grads_ready("in", {"w_in": gw_in})
    grad_x = _mm(dproj, w["w_in"], tk=1792, epi="add", extra=ds1, add_scale=ALPHA, after=token, name="mm_dx")
    return sq, grad_x, g


_VECTORS = ("a_log", "dt_bias", "gdn_norm_w", "pool_scale", "ln1_g", "ln1_b", "ln2_g", "ln2_b", "ln3_g", "ln3_b")
_BA_SPLIT = BA_OFF + 2 * GDN_HEADS


def _lane_row(v, offset):
    return jnp.zeros((1, LANE), F32).at[0, offset:offset + v.shape[0]].set(v)


_GROUP_VECTORS = {"mixer": (), "attn": ("ln1_g", "ln1_b", "ln2_g", "ln2_b"), "up": (), "down": ("ln3_g", "ln3_b")}


def _group_weights(group, full):
    w = {n: full[n].reshape(1, D_MODEL) for n in _GROUP_VECTORS[group]}
    if group == "mixer":
        w.update({
            "w_in": _w_in_padded(full["w_in"]),
            "conv_w": full["conv_w"],
            "alog_row": _lane_row(full["a_log"], GDN_HEADS),
            "dtb_row": _lane_row(full["dt_bias"], GDN_HEADS),
            "gdn_norm_w": full["gdn_norm_w"].reshape(1, LANE),
            "pool_w": full["pool_w"],
            "pool_scale": full["pool_scale"].reshape(POOL_GROUPS, 1, POOL_GROUP_DIM),
        })
    else:
        w.update({n: full[n] for n in dict(_GATHER_GROUPS)[group]})
    return w


def _w_in_row_map():
    per = IN_COLS // N_DEV
    gap = POOL_OFF - _BA_SPLIT
    pieces = []
    for d in range(N_DEV):
        lo, hi = d * per, (d + 1) * per
        if hi <= _BA_SPLIT:
            pieces.append([(0, lo, per)])
        elif lo >= _BA_SPLIT:
            pieces.append([(0, lo + gap, per)])
        else:
            pieces.append([(0, lo, _BA_SPLIT - lo), (_BA_SPLIT - lo, POOL_OFF, hi - _BA_SPLIT)])
    return pieces


_W_IN_LANES = 256


def _w_in_padded(blocks):
    def body(b_ref, o_ref):
        for d, pieces in enumerate(_w_in_row_map()):
            for src, dst, rows in pieces:
                o_ref[dst:dst + rows, :] = b_ref[d, src:src + rows, :]
        o_ref[_BA_SPLIT:POOL_OFF, :] = jnp.zeros((POOL_OFF - _BA_SPLIT, _W_IN_LANES), o_ref.dtype)

    n, per, cols = blocks.shape
    return pl.pallas_call(
        body, grid=(cols // _W_IN_LANES,), in_specs=[pl.BlockSpec((n, per, _W_IN_LANES), lambda j: (0, 0, j))],
        out_specs=pl.BlockSpec((PROJ_COLS, _W_IN_LANES), lambda j: (0, j)),
        out_shape=jax.ShapeDtypeStruct((PROJ_COLS, cols), blocks.dtype), compiler_params=_params("parallel"),
        name="w_in_padded")(blocks)


def _w_in_chunks(g):
    def body(g_ref, o_ref):
        for d, pieces in enumerate(_w_in_row_map()):
            for dst, src, rows in pieces:
                o_ref[d, dst:dst + rows, :] = g_ref[src:src + rows, :]

    cols = g.shape[1]
    per = IN_COLS // N_DEV
    return pl.pallas_call(
        body, grid=(cols // _W_IN_LANES,), in_specs=[pl.BlockSpec((PROJ_COLS, _W_IN_LANES), lambda j: (0, j))],
        out_specs=pl.BlockSpec((N_DEV, per, _W_IN_LANES), lambda j: (0, 0, j)),
        out_shape=jax.ShapeDtypeStruct((N_DEV, per, cols), g.dtype), compiler_params=_params("parallel"),
        name="w_in_chunks")(g)


def _finish_small_grads(g):
    out = {"conv_w": g["conv_w"]}
    out["a_log"] = g["alog_row"][0, GDN_HEADS:2 * GDN_HEADS]
    out["dt_bias"] = g["dtb_row"][0, GDN_HEADS:2 * GDN_HEADS]
    out["gdn_norm_w"] = g["gdn_norm_w"].reshape(LANE)
    out["pool_scale"] = g["pool_scale"].reshape(POOL_GROUPS * POOL_GROUP_DIM)
    for n in ("ln1_g", "ln1_b", "ln2_g", "ln2_b", "ln3_g", "ln3_b"):
        out[n] = g[n].reshape(D_MODEL)
    return out


def _adamw_math(w, g, m, v):
    m = ADAM_B1 * m + (1.0 - ADAM_B1) * g
    v = ADAM_B2 * v + (1.0 - ADAM_B2) * (g * g)
    m_hat = m / (1.0 - ADAM_B1 ** ADAM_STEP)
    v_hat = v / (1.0 - ADAM_B2 ** ADAM_STEP)
    delta = -ADAM_LR * (m_hat / (jnp.sqrt(v_hat) + ADAM_EPS) + ADAM_WD * w)
    return delta, m, v


ADAMW_TILE_ELEMS = 256 * 1024
CHIP_SUM_TILE_ELEMS = 1024 * 1024


def _shard_tile(r, c, elems):
    for rows in (1024, 512, 256, 128):
        if r % rows == 0 and rows * c <= elems:
            return rows, c
    if r % 128 == 0:
        return 128, c
    return r, 256 if c % 256 == 0 else c


def _adamw_shard(parts, own, me, w, m, v, *, name):
    s, r, c = parts.shape
    tr, tc = _shard_tile(r, c, ADAMW_TILE_ELEMS)
    assert r % tr == 0 and c % tc == 0, (name, r, c)
    unit_axis = w.ndim == 3
    at = (slice(None), 0, slice(None)) if unit_axis else Ellipsis

    def body(me_ref, p_ref, own_ref, w_ref, m_ref, v_ref, g_ref, d_ref, nm_ref, nv_ref):
        mine = own_ref[...].astype(F32)
        g = None
        for i in range(s):
            part = jnp.where(me_ref[0] == i, mine, p_ref[i].astype(F32))
            g = part if g is None else g + part
        delta, nm, nv = _adamw_math(w_ref[at], g, m_ref[at], v_ref[at])
        g_ref[at] = g
        d_ref[at] = delta
        nm_ref[at] = nm
        nv_ref[at] = nv

    if unit_axis:
        blk = pl.BlockSpec((tr, 1, tc), lambda i, j, me_ref: (i, 0, j))
        out = jax.ShapeDtypeStruct((r, 1, c), F32)
    else:
        blk = pl.BlockSpec((tr, tc), lambda i, j, me_ref: (i, j))
        out = jax.ShapeDtypeStruct((r, c), F32)
    return pl.pallas_call(
        body,
        grid_spec=pltpu.PrefetchScalarGridSpec(
            num_scalar_prefetch=1, grid=(r // tr, c // tc),
            in_specs=[pl.BlockSpec((s, tr, tc), lambda i, j, me_ref: (0, i, j)),
                      pl.BlockSpec((None, tr, tc), lambda i, j, me_ref: (me_ref[0], i, j)), blk, blk, blk],
            out_specs=[blk, blk, blk, blk]),
        out_shape=[out, out, out, out], compiler_params=_params("parallel", "parallel"), name=name,
    )(me, parts, own, w, m, v)


N_CHIPS = N_DEV // 2


def _chip_sums(chunks, from_sibling, core, *, name):
    _, r, c = chunks.shape
    tr, tc = _shard_tile(r, c, CHIP_SUM_TILE_ELEMS)
    assert r % tr == 0 and c % tc == 0, (name, r, c)

    def body(core_ref, mine_ref, other_ref, o_ref):
        o_ref[...] = (mine_ref[...].astype(F32) + other_ref[...].astype(F32)).astype(o_ref.dtype)

    by_chip = pl.BlockSpec((None, tr, tc), lambda q, i, j, core_ref: (q, i, j))
    return pl.pallas_call(
        body,
        grid_spec=pltpu.PrefetchScalarGridSpec(
            num_scalar_prefetch=1, grid=(N_CHIPS, r // tr, c // tc),
            in_specs=[pl.BlockSpec((None, tr, tc), lambda q, i, j, core_ref: (2 * q + core_ref[0], i, j)), by_chip],
            out_specs=by_chip),
        out_shape=jax.ShapeDtypeStruct((N_CHIPS, r, c), chunks.dtype),
        compiler_params=_params("parallel", "parallel", "parallel"), name=name,
    )(core, chunks, from_sibling)


def _place():
    return lax.axis_index("x"), lax.axis_index("y"), lax.axis_index("c")


def _slot(px, py, pc):
    return 4 * px + 2 * py + pc


_HBM = pl.BlockSpec(memory_space=pltpu.HBM)


_SEM = pl.BlockSpec(memory_space=pltpu.SEMAPHORE)
_ANY = pl.BlockSpec(memory_space=pl.ANY)
_EFFECT = pltpu.SideEffectType.DATAFLOW_SIDE_EFFECTING


def _peer(k, x, y, c):
    return (1 - x if k & 4 else x, 1 - y if k & 2 else y, 1 - c if k & 1 else c)


_EXCHANGE_BITS = {"gather_near": (1, 2, 4), "gather_relay": (6,), "gather_pass": (2, 4, 6),
                  "scatter_sibling": (1, 1, 1, 1), "scatter_chips": (2, 4, 6), "all_small": (1, 2, 3, 4, 5, 6, 7)}


def _exchange_copy(mode, src, land, w, i, place, send_sems, recv_sems, receiving):
    bits = _EXCHANGE_BITS[mode]
    k = bits[i]
    peer = _peer(k, *place)
    me = _slot(*place)
    if mode in ("gather_near", "all_small"):
        to, src_ref, sent_to, got_at = peer, src[w], me, _slot(*peer)
    elif mode == "gather_relay":
        x, y, c = place
        other = 1 - c
        to = (lax.bitwise_xor(x, c), lax.bitwise_xor(y, other), c)
        blk = _slot(lax.bitwise_xor(x, other), lax.bitwise_xor(y, c), c)
        src_ref, sent_to, got_at = land[w].at[blk], blk, _slot(*peer)
    elif mode == "gather_pass":
        blk = _slot(*peer)
        to, src_ref, sent_to, got_at = _peer(1, *place), land[w].at[blk], blk, _slot(*_peer(k | 1, *place))
    elif mode == "scatter_sibling":
        to, src_ref, sent_to, got_at = peer, src[w].at[2 * i + 1 - place[2]], i, i
    else:
        to, src_ref, sent_to, got_at = peer, src[w].at[_slot(*peer) // 2], me // 2, _slot(*peer) // 2
    sem = w * len(bits) + i
    return pltpu.make_async_remote_copy(
        src_ref=src_ref, dst_ref=land[w].at[got_at if receiving else sent_to], send_sem=send_sems.at[sem],
        recv_sem=recv_sems.at[sem], device_id=to, device_id_type=MESH)


def _exchange_start(mode, srcs, lands, after, *, name):
    ns, nl = len(srcs), len(lands)
    n_sem = nl * len(_EXCHANGE_BITS[mode])

    def body(*refs):
        src, land = refs[:ns], refs[ns:ns + nl]
        send_sems, recv_sems = refs[ns + nl + 1:ns + nl + 3]
        token = refs[-1]
        place = _place()
        for w in range(nl):
            for i in range(len(_EXCHANGE_BITS[mode])):
                _exchange_copy(mode, src, land, w, i, place, send_sems, recv_sems, receiving=False).start()
        token[...] = jnp.zeros_like(token)

    sems = pltpu.SemaphoreType.DMA((n_sem,))
    arrays = list(srcs) + list(lands)
    res = pl.pallas_call(
        body, name=name, in_specs=[_HBM] * (ns + nl) + [_ANY],
        out_specs=(_SEM, _SEM, *([_HBM] * (ns + nl)), pl.BlockSpec(memory_space=pltpu.VMEM)),
        out_shape=(sems, sems, *[pltpu.HBM(a.shape, a.dtype) for a in arrays], jax.ShapeDtypeStruct((8, LANE), F32)),
        input_output_aliases={i: 2 + i for i in range(ns + nl)},
        compiler_params=pltpu.CompilerParams(has_side_effects=_EFFECT),
    )(*[pltpu.with_memory_space_constraint(a, pltpu.HBM) for a in arrays], after)
    return res[0], res[1], list(res[2:2 + ns]), list(res[2 + ns:2 + ns + nl]), res[-1]


def _exchange_wait(mode, started, after, *, name):
    send_sems, recv_sems, srcs, lands, _ = started
    ns, nl = len(srcs), len(lands)

    def body(*refs):
        src, land = refs[:ns], refs[ns:ns + nl]
        send_sems, recv_sems = refs[ns + nl:ns + nl + 2]
        place = _place()
        for w in range(nl):
            for i in range(len(_EXCHANGE_BITS[mode])):
                cp = _exchange_copy(mode, src, land, w, i, place, send_sems, recv_sems, receiving=True)
                cp.wait_send()
                cp.wait_recv()

    arrays = list(srcs) + list(lands)
    res = pl.pallas_call(
        body, name=name, in_specs=[_HBM] * (ns + nl) + [_SEM, _SEM, _ANY], out_specs=[_HBM] * (ns + nl),
        out_shape=[pltpu.HBM(a.shape, a.dtype) for a in arrays],
        input_output_aliases={i: i for i in range(ns + nl)},
        compiler_params=pltpu.CompilerParams(has_side_effects=_EFFECT),
    )(*arrays, send_sems, recv_sems, after)
    return list(res[:ns]), list(res[ns:])


_LN_ROWS = ("ln1_g", "ln1_b", "ln2_g", "ln2_b", "ln3_g", "ln3_b")
_MISC_ROW = len(_LN_ROWS)
_MISC = (("pool_scale", 0, GDN_WIDTH), ("gdn_norm_w", GDN_WIDTH, HEAD_DIM), ("a_log", GDN_WIDTH + LANE, GDN_HEADS),
         ("dt_bias", GDN_WIDTH + 2 * LANE, GDN_HEADS), ("loss", GDN_WIDTH + 3 * LANE, 1))
_CONV_ROW = _MISC_ROW + 1
_CONV_ROWS = CONV_K * QKV_COLS // D_MODEL
_SMALL_ROWS = 16


def _pack_small(vals):
    pieces, at = [], 0
    for n, off, size in _MISC:
        pieces.append(jnp.zeros((off - at,), F32))
        pieces.append(vals[n].reshape(size).astype(F32) if n in vals else jnp.zeros((size,), F32))
        at = off + size
    pieces.append(jnp.zeros((D_MODEL - at,), F32))
    conv = vals["conv_w"].reshape(-1) if "conv_w" in vals else jnp.zeros((_CONV_ROWS * D_MODEL,), F32)
    tail = jnp.zeros(((_SMALL_ROWS - _CONV_ROW - _CONV_ROWS) * D_MODEL,), F32)
    flat = jnp.concatenate([vals[n].reshape(D_MODEL) for n in _LN_ROWS] + pieces + [conv, tail])
    return flat.reshape(_SMALL_ROWS, D_MODEL)


def _adamw_small(zone, mine, me, w, m, v):
    short = [(n, off, size) for n, off, size in _MISC if n != "loss"]

    def body(me_ref, z_ref, mine_ref, w_ref, m_ref, v_ref, *rest):
        outs, (g_s, d_s, nm_s, nv_s) = rest[:-4], rest[-4:]
        g = None
        for s in range(N_DEV):
            part = jnp.where(me_ref[0] == s, mine_ref[...], z_ref[s])
            g = part if g is None else g + part
        g_s[...] = g
        d_s[...], nm_s[...], nv_s[...] = _adamw_math(w_ref[...], g, m_ref[...], v_ref[...])
        k = 0
        for src in (g_s, d_s, nm_s, nv_s):
            for r in range(len(_LN_ROWS)):
                outs[k][...] = src[r:r + 1, :]
                k += 1
            for _, off, size in short:
                outs[k][...] = src[_MISC_ROW:_MISC_ROW + 1, off:off + size]
                k += 1
        outs[k][...] = g_s[_CONV_ROW:_CONV_ROW + _CONV_ROWS, :]
        outs[k + 1][...] = g_s[_MISC_ROW:_MISC_ROW + 1, :]

    rows, d = mine.shape
    per_quantity = [jax.ShapeDtypeStruct((1, D_MODEL), F32)] * len(_LN_ROWS) + [
        jax.ShapeDtypeStruct((1, size), F32) for _, _, size in short]
    out_shape = per_quantity * 4 + [jax.ShapeDtypeStruct((_CONV_ROWS, d), F32), jax.ShapeDtypeStruct((1, d), F32)]
    whole = lambda a: pl.BlockSpec(a.shape, lambda i, me_ref: (0,) * len(a.shape))
    res = pl.pallas_call(
        body,
        grid_spec=pltpu.PrefetchScalarGridSpec(
            num_scalar_prefetch=1, grid=(1,), in_specs=[whole(a) for a in (zone, mine, w, m, v)],
            out_specs=[whole(s) for s in out_shape], scratch_shapes=[pltpu.VMEM((rows, d), F32)] * 4),
        out_shape=out_shape, compiler_params=_params("arbitrary"), name="adamw_small",
    )(me, zone, mine, w, m, v)
    names = list(_LN_ROWS) + [n for n, _, _ in short]
    n_each = len(names)
    quantities = [dict(zip(names, res[q * n_each:(q + 1) * n_each])) for q in range(4)]
    return quantities, res[-2], res[-1]


_WEIGHT_ORDER = ("w_in", "conv_w", "a_log", "dt_bias", "gdn_norm_w", "pool_w", "pool_scale", "w_out", "ln1_g", "ln1_b",
                 "xq_w", "xk_w", "xv_w", "xo_w", "ln2_g", "ln2_b", "w_up", "w_down", "ln3_g", "ln3_b")


def _shard2d(name, a):
    if name == "w_in":
        return a.T
    return a.reshape(-1, a.shape[-1]) if name == "pool_w" else a


def _update_view(name, a):
    return jnp.transpose(a, (2, 0, 1)) if name == "w_in" else _shard2d(name, a[0])


def _shard_result(name, r, shape):
    return jnp.transpose(r, (1, 2, 0)) if name == "w_in" else r.reshape(shape)


def _gathered_to_full(name, gth):
    if name in ("w_up", "w_in"):
        return gth
    if name == "conv_w":
        return jnp.transpose(gth, (1, 0, 2)).reshape(gth.shape[1], N_DEV * gth.shape[2])
    if name == "pool_w":
        g4 = gth.reshape(N_DEV, POOL_GROUPS, POOL_GROUP_DIM // N_DEV, POOL_GROUP_DIM)
        return jnp.transpose(g4, (1, 0, 2, 3)).reshape(POOL_GROUPS, POOL_GROUP_DIM, POOL_GROUP_DIM)
    return gth.reshape(N_DEV * gth.shape[1], gth.shape[2])


def _full_to_chunks(name, full):
    if name == "w_up":
        return full
    if name == "pool_w":
        g4 = full.reshape(POOL_GROUPS, N_DEV, POOL_GROUP_DIM // N_DEV, POOL_GROUP_DIM)
        return jnp.transpose(g4, (1, 0, 2, 3)).reshape(N_DEV, POOL_GROUPS * POOL_GROUP_DIM // N_DEV, POOL_GROUP_DIM)
    return full.reshape(N_DEV, full.shape[0] // N_DEV, full.shape[1])


_GATHER_GROUPS = (("mixer", ("w_in", "conv_w", "pool_w")), ("attn", ("w_out", "xq_w", "xk_w", "xv_w", "xo_w")),
                  ("up", ("w_up",)), ("down", ("w_down",)))


def _grad_chunks(name, g):
    if name == "w_in":
        return _w_in_chunks(g.astype(BF16))
    return _full_to_chunks(name, g.astype(BF16))


def kernel(x, mem, w_in, conv_w, a_log, dt_bias, gdn_norm_w, pool_w, pool_scale, w_out, ln1_g, ln1_b, xq_w, xk_w, xv_w, xo_w, ln2_g, ln2_b, w_up, w_down, ln3_g, ln3_b, loss_target, m_w_in, m_conv_w, m_a_log, m_dt_bias, m_gdn_norm_w, m_pool_w, m_pool_scale, m_w_out, m_ln1_g, m_ln1_b, m_xq_w, m_xk_w, m_xv_w, m_xo_w, m_ln2_g, m_ln2_b, m_w_up, m_w_down, m_ln3_g, m_ln3_b, v_w_in, v_conv_w, v_a_log, v_dt_bias, v_gdn_norm_w, v_pool_w, v_pool_scale, v_w_out, v_ln1_g, v_ln1_b, v_xq_w, v_xk_w, v_xv_w, v_xo_w, v_ln2_g, v_ln2_b, v_w_up, v_w_down, v_ln3_g, v_ln3_b):
    args = dict(locals())
    wt = {n: args[n][0] for n in _WEIGHT_ORDER}
    mo = {n: args["m_" + n][0] for n in _WEIGHT_ORDER}
    vo = {n: args["v_" + n][0] for n in _WEIGHT_ORDER}

    me = _slot(*_place())
    me_arr = jnp.reshape(me, (1,)).astype(jnp.int32)
    nothing = jnp.zeros((8, LANE), F32)

    def landing_zones(names):
        shards = [_shard2d(n, wt[n]).astype(F32 if n == "conv_w" else BF16) for n in names]
        zones = [lax.dynamic_update_slice(lax.empty((N_DEV, *s.shape), s.dtype), s[None], (me, 0, 0)) for s in shards]
        return shards, zones

    chip_arr = jnp.reshape(me // 2, (1,)).astype(jnp.int32)
    core_arr = jnp.reshape(lax.axis_index("c"), (1,)).astype(jnp.int32)
    names_of = dict(_GATHER_GROUPS)
    gathers = {}
    prepared = {}

    def gather_near(group, after):
        shards, zones = prepared.pop(group) if group in prepared else landing_zones(names_of[group])
        gathers[group] = _exchange_start("gather_near", shards, zones, after, name="gather_near_" + group)
        return gathers[group][4]

    def gather_next(group, was, now, after):
        _, zones = _exchange_wait(was, gathers[group], after, name=f"{was}_{group}_wait")
        gathers[group] = _exchange_start(now, [], zones, nothing, name=f"{now}_{group}")
        return gathers[group][4]

    def gather_relay(group, after):
        return gather_next(group, "gather_near", "gather_relay", after)

    def gather_pass(group, after):
        return gather_next(group, "gather_relay", "gather_pass", after)

    def gathered(group, after):
        _, zones = _exchange_wait("gather_pass", gathers[group], after, name=f"gather_pass_{group}_wait")
        full = {n: _gathered_to_full(n, z) for n, z in zip(names_of[group], zones)}
        full.update({n: wt[n] for n in _VECTORS})
        return _group_weights(group, full)

    token = gather_near("mixer", nothing)
    x16 = _cast_bf16(x[0], name="cast_x")
    later = {group: landing_zones(names_of[group]) for group in ("attn", "up", "down")}
    token, x16, later = lax.optimization_barrier((token, x16, later))
    prepared.update(later)
    token = gather_pass("mixer", gather_relay("mixer", token))
    token = gather_near("attn", token)

    def weights_of(group, after):
        if group == "mixer":
            return gathered(group, token)
        if group == "ahead_conv":
            return gather_near("up", gather_relay("attn", after))[0:1, 0:1]
        if group == "ahead_scan":
            return gather_pass("attn", after)[0:1, 0:1]
        if group == "attn":
            return gathered(group, gather_near("down", gather_relay("up", after)))
        if group == "ahead_attn":
            return gather_relay("down", gather_pass("up", after))[0:1, 0:1]
        if group == "up":
            return gathered(group, gather_pass("down", after))
        return gathered(group, after)

    scatters = {}
    in_flight = []

    def chip_stage(after):
        group, names, started = in_flight.pop()
        chunks, from_sibling = _exchange_wait("scatter_sibling", started, after, name=f"scatter_sibling_{group}_wait")
        sums = [_chip_sums(c, f, core_arr, name=f"chip_sums_{n}") for n, c, f in zip(names, chunks, from_sibling)]
        scatters[group] = (names, _exchange_start("scatter_chips", sums, [lax.empty(s.shape, s.dtype) for s in sums],
                                                  nothing, name="scatter_chips_" + group))
        return scatters[group][1][4]

    small_sent = []

    def grads_ready(group, grads):
        if group == "tick":
            return chip_stage(grads["after"])[0:1, 0:1] if in_flight else None
        if group == "small":
            small = _finish_small_grads(grads)
            small["loss"] = 0.5 * grads["sq"][0:1, 0] / D_MODEL
            packed = _pack_small(small)
            zone = lax.empty((N_DEV, *packed.shape), F32)
            small_sent.append(_exchange_start("all_small", [packed], [zone], nothing, name="small_grads_start"))
            return small_sent[0][4][0:1, 0:1]
        names = tuple(grads)
        chunks = [_grad_chunks(n, grads[n]) for n in names]
        token = chip_stage(chunks[0]) if in_flight else nothing
        zones = [lax.empty((N_CHIPS, *c.shape[1:]), c.dtype) for c in chunks]
        started = _exchange_start("scatter_sibling", chunks, zones, token, name="scatter_sibling_" + group)
        in_flight.append((group, names, started))
        if group != "in":
            return started[4][0:1, 0:1]
        return chip_stage(update_group("mlp", started[4]))[0:1, 0:1]

    out = {}

    def update_group(group, after):
        names, started = scatters.pop(group)
        sums, lands = _exchange_wait("scatter_chips", started, after, name=f"scatter_chips_{group}_wait")
        for n, parts, own in zip(names, lands, sums):
            res = _adamw_shard(parts, own, chip_arr, _update_view(n, args[n]), _update_view(n, args["m_" + n]),
                               _update_view(n, args["v_" + n]), name="adamw_" + n)
            out[n] = [_shard_result(n, r, args[n].shape) for r in res]
            after = res[1]
        return after

    sq, grad_x, g = _local_step(x[0], x16, mem[0], loss_target[0], weights_of, grads_ready)

    after = grad_x
    for group in list(scatters):
        after = update_group(group, after)

    (packed,), (zone,) = _exchange_wait("all_small", small_sent[0], after, name="small_grads_wait")
    quantities, conv_rows, misc_row = _adamw_small(
        zone, packed, me_arr, _pack_small({n: wt[n] for n in _VECTORS}), _pack_small({n: mo[n] for n in _VECTORS}),
        _pack_small({n: vo[n] for n in _VECTORS}))
    cols = conv_w.shape[-1]
    conv_mine = lax.dynamic_slice(conv_rows.reshape(CONV_K, QKV_COLS), (0, me * cols), (CONV_K, cols))[None]
    res = _adamw_shard(conv_mine, conv_mine, jnp.zeros((1,), jnp.int32), wt["conv_w"], mo["conv_w"], vo["conv_w"],
                       name="adamw_conv_w")
    out["conv_w"] = [r.reshape(conv_w.shape) for r in res]
    for n in _VECTORS:
        out[n] = [q[n] for q in quantities]
    loss_at = dict((n, off) for n, off, _ in _MISC)["loss"]

    return (misc_row[0, loss_at], grad_x[None], *[out[n][0] for n in _WEIGHT_ORDER], *[out[n][1] for n in _WEIGHT_ORDER],
            *[out[n][2] for n in _WEIGHT_ORDER], *[out[n][3] for n in _WEIGHT_ORDER])
```

```python
import jax
import jax.numpy as jnp
from jax import lax
from jax.experimental import pallas as pl
from jax.experimental.pallas import tpu as pltpu

F32 = jnp.float32
BF16 = jnp.bfloat16
MESH = pl.DeviceIdType.MESH

N_DEV = 8
D_MODEL = 2048
GDN_WIDTH = 1024
GDN_HEADS = 8
HEAD_DIM = 128
CONV_K = 4
CHUNK = 64
POOL_GROUPS = 4
POOL_GROUP_DIM = 256
MEM_LEN = 256
XATTN_HEADS = 4
XATTN_HEAD_DIM = 512
D_FF = 8192
IN_COLS = 5136
ALPHA = 2.0 ** 0.25
LN_EPS = 1e-5
NORM_EPS = 1e-6

LANE = 128
QKV_COLS = 3 * GDN_WIDTH
Z_OFF = QKV_COLS
BA_OFF = 4 * GDN_WIDTH
POOL_OFF = BA_OFF + 2 * LANE
PROJ_COLS = POOL_OFF + GDN_WIDTH
BA_BLK = BA_OFF // LANE
POOL_BLK = POOL_OFF // POOL_GROUP_DIM

ADAM_LR = 0.001
ADAM_B1 = 0.9
ADAM_B2 = 0.999
ADAM_EPS = 1e-08
ADAM_WD = 0.01
ADAM_STEP = 10

VMEM_LIMIT_BYTES = 48 * 1024 * 1024


def _params(*sem):
    return pltpu.CompilerParams(dimension_semantics=sem if sem else None, vmem_limit_bytes=VMEM_LIMIT_BYTES)


def _make_dots(cast, precision, batched=False):
    lead = 1 if batched else 0
    batch = ((0,), (0,)) if batched else ((), ())

    def dg(a, b, ca, cb):
        if cast is not None:
            a = a.astype(cast)
            b = b.astype(cast)
        return lax.dot_general(a, b, (((ca + lead,), (cb + lead,)), batch), precision=precision, preferred_element_type=F32)

    def nn_(a, b):
        return dg(a, b, 1, 0)

    def nt_(a, b):
        return dg(a, b, 1, 1)

    def tn_(a, b):
        return dg(a, b, 0, 0)

    @jax.custom_vjp
    def nn(a, b):
        return nn_(a, b)

    nn.defvjp(lambda a, b: (nn_(a, b), (a, b)), lambda r, g: (nt_(g, r[1]), tn_(r[0], g)))

    @jax.custom_vjp
    def nt(a, b):
        return nt_(a, b)

    nt.defvjp(lambda a, b: (nt_(a, b), (a, b)), lambda r, g: (nn_(g, r[1]), tn_(g, r[0])))

    @jax.custom_vjp
    def tn(a, b):
        return tn_(a, b)

    tn.defvjp(lambda a, b: (tn_(a, b), (a, b)), lambda r, g: (nt_(r[1], g), nn_(r[0], g)))

    return (nn_, nt_, tn_), (nn, nt, tn)


_BDOT_PLAIN, _BDOT_VJP = _make_dots(BF16, None)
_BDOT_BATCH_PLAIN, _BDOT_BATCH_VJP = _make_dots(BF16, None, batched=True)
_FDOT_BATCH_PLAIN, _FDOT_BATCH_VJP = _make_dots(BF16, None, batched=True)


def _mm(a, b, *, ta=False, tb=False, out_dtype=F32, tm=None, tn=512, tk=None, epi=None, extra=None, add_scale=1.0,
        b_chunks=False, o_chunks=False, after=None, name):
    m, k = (a.shape[1], a.shape[0]) if ta else a.shape
    if b_chunks:
        n, kb = (b.shape[1], N_DEV * b.shape[2]) if tb else (N_DEV * b.shape[2], b.shape[1])
    else:
        n, kb = b.shape if tb else (b.shape[1], b.shape[0])
    assert kb == k, (name, a.shape, b.shape)
    tm, tn, tk = min(tm or m, m), min(tn, n), min(tk or k, k)
    assert m % tm == 0 and n % tn == 0 and k % tk == 0, (name, m, n, k)
    nk = k // tk
    dims = (((0 if ta else 1,), (1 if tb else 0,)), ((), ()))
    n_extra = 0 if epi in (None, "relu2") else 1
    n_out = 2 if epi == "relu2" else 1
    if epi in ("relu2", "mul2r"):
        out_dtype = BF16
    n_after = 0 if after is None else 1

    def body(*refs):
        a_ref, b_ref = refs[:2]
        c_ref = refs[2] if n_extra else None
        o_refs = refs[2 + n_extra + n_after:2 + n_extra + n_after + n_out]
        scr = refs[2 + n_extra + n_after + n_out:]
        r = lax.dot_general(a_ref[...].astype(BF16), b_ref[...].astype(BF16), dims, preferred_element_type=F32)

        def finish(v):
            if epi == "add":
                o_refs[0][...] = (v + add_scale * c_ref[...]).astype(out_dtype)
            elif epi == "relu2":
                p = jnp.maximum(v, 0.0)
                o_refs[0][...] = (p * p).astype(BF16)
                o_refs[1][...] = p.astype(BF16)
            elif epi == "mul2r":
                o_refs[0][...] = (v * (2.0 * c_ref[...].astype(F32))).astype(BF16)
            else:
                o_refs[0][...] = v.astype(out_dtype)

        if nk == 1:
            finish(r)
        else:
            acc = scr[0]
            kk = pl.program_id(2)

            @pl.when(kk == 0)
            def _():
                acc[...] = r

            @pl.when(kk > 0)
            def _():
                acc[...] += r

            @pl.when(kk == nk - 1)
            def _():
                finish(acc[...])

    a_spec = pl.BlockSpec((tk, tm), lambda i, j, kk: (kk, i)) if ta else pl.BlockSpec((tm, tk), lambda i, j, kk: (i, kk))
    if b_chunks and tb:
        kc = k // N_DEV // tk
        b_spec = pl.BlockSpec((None, tn, tk), lambda i, j, kk: (kk // kc, j, kk % kc))
    elif b_chunks:
        nc = n // N_DEV // tn
        b_spec = pl.BlockSpec((None, tk, tn), lambda i, j, kk: (j // nc, kk, j % nc))
    elif tb:
        b_spec = pl.BlockSpec((tn, tk), lambda i, j, kk: (j, kk))
    else:
        b_spec = pl.BlockSpec((tk, tn), lambda i, j, kk: (kk, j))
    mn_spec = pl.BlockSpec((tm, tn), lambda i, j, kk: (i, j))
    if o_chunks:
        oc = n // N_DEV // tn
        o_spec = pl.BlockSpec((None, tm, tn), lambda i, j, kk: (j // oc, i, j % oc))
        o_shape = jax.ShapeDtypeStruct((N_DEV, m, n // N_DEV), out_dtype)
    else:
        o_spec, o_shape = mn_spec, jax.ShapeDtypeStruct((m, n), out_dtype)
    res = pl.pallas_call(
        body, grid=(m // tm, n // tn, nk),
        in_specs=[a_spec, b_spec] + [mn_spec] * n_extra + [pl.BlockSpec(memory_space=pl.ANY)] * n_after,
        out_specs=[o_spec] * n_out, out_shape=[o_shape] * n_out,
        scratch_shapes=[pltpu.VMEM((tm, tn), F32)] if nk > 1 else [],
        compiler_params=_params("parallel", "parallel", "arbitrary"), name=name,
    )(a, b, *([extra] if n_extra else []), *([after] if n_after else []))
    return res if n_out > 1 else res[0]


def _cast_bf16(v, *, name, tm=512):
    t, d = v.shape
    tm = min(tm, t)

    def body(v_ref, o_ref):
        o_ref[...] = v_ref[...].astype(BF16)

    spec = pl.BlockSpec((tm, d), lambda i: (i, 0))
    return pl.pallas_call(body, grid=(t // tm,), in_specs=[spec], out_specs=spec,
                          out_shape=jax.ShapeDtypeStruct((t, d), BF16), compiler_params=_params("parallel"), name=name)(v)


def _shift_down(v, s):
    if s == 0:
        return v
    row = lax.broadcasted_iota(jnp.int32, v.shape, 0)
    return jnp.where(row >= s, pltpu.roll(v, s, axis=0), 0.0)


def _shift_up(v, s):
    if s == 0:
        return v
    t = v.shape[0]
    row = lax.broadcasted_iota(jnp.int32, v.shape, 0)
    return jnp.where(row < t - s, pltpu.roll(v, t - s, axis=0), 0.0)


def _post_col(j):
    return (j % GDN_HEADS) * 3 + j // GDN_HEADS


def _gdn_prep_fwd(proj, conv_w):
    t = proj.shape[0]

    def body(x_ref, w_ref, o_ref):
        j = pl.program_id(0)
        x = x_ref[...]
        y = jnp.zeros_like(x)
        for tap in range(CONV_K):
            y = y + w_ref[tap:tap + 1, :] * _shift_down(x, CONV_K - 1 - tap)
        c = y * jax.nn.sigmoid(y)
        nrm = c * lax.rsqrt(jnp.sum(c * c, axis=1, keepdims=True) + NORM_EPS)
        o_ref[...] = jnp.where(j < 2 * GDN_HEADS, nrm, c)

    return pl.pallas_call(
        body, grid=(QKV_COLS // LANE,),
        in_specs=[pl.BlockSpec((t, LANE), lambda j: (0, j)), pl.BlockSpec((CONV_K, LANE), lambda j: (0, j))],
        out_specs=pl.BlockSpec((t, LANE), lambda j: (0, _post_col(j))),
        out_shape=jax.ShapeDtypeStruct((t, QKV_COLS), F32),
        compiler_params=_params("parallel"), name="gdn_prep_fwd",
    )(proj, conv_w)


def _gdn_prep_bwd(proj, conv_w, dpost, dproj):
    t = proj.shape[0]

    def body(x_ref, w_ref, d_ref, _, dx_ref, dw_ref):
        j = pl.program_id(0)
        x = x_ref[...]
        xs = [_shift_down(x, CONV_K - 1 - tap) for tap in range(CONV_K)]
        y = jnp.zeros_like(x)
        for tap in range(CONV_K):
            y = y + w_ref[tap:tap + 1, :] * xs[tap]
        sig = jax.nn.sigmoid(y)
        c = y * sig
        r = lax.rsqrt(jnp.sum(c * c, axis=1, keepdims=True) + NORM_EPS)
        nrm = c * r
        d = d_ref[...]
        dc_norm = r * (d - nrm * jnp.sum(d * nrm, axis=1, keepdims=True))
        dc = jnp.where(j < 2 * GDN_HEADS, dc_norm, d)
        dy = dc * (sig * (1.0 + y * (1.0 - sig)))
        dx = jnp.zeros_like(x)
        for tap in range(CONV_K):
            dx = dx + _shift_up(w_ref[tap:tap + 1, :] * dy, CONV_K - 1 - tap)
            dw_ref[tap:tap + 1, :] = jnp.sum(dy * xs[tap], axis=0, keepdims=True)
        dx_ref[...] = dx.astype(dx_ref.dtype)

    return pl.pallas_call(
        body, grid=(QKV_COLS // LANE,),
        in_specs=[pl.BlockSpec((t, LANE), lambda j: (0, j)), pl.BlockSpec((CONV_K, LANE), lambda j: (0, j)),
                  pl.BlockSpec((t, LANE), lambda j: (0, _post_col(j))), pl.BlockSpec(memory_space=pl.ANY)],
        out_specs=[pl.BlockSpec((t, LANE), lambda j: (0, j)), pl.BlockSpec((CONV_K, LANE), lambda j: (0, j))],
        out_shape=[jax.ShapeDtypeStruct(dproj.shape, dproj.dtype), jax.ShapeDtypeStruct((CONV_K, QKV_COLS), F32)],
        input_output_aliases={3: 0},
        compiler_params=_params("parallel"), name="gdn_prep_bwd",
    )(proj, conv_w, dpost, dproj)


def _softplus(v):
    return jnp.maximum(v, 0.0) + jnp.log(1.0 + jnp.exp(-jnp.abs(v)))


def _tri_inv(low, nn):
    r = lax.broadcasted_iota(jnp.int32, (CHUNK, CHUNK), 0)
    c = lax.broadcasted_iota(jnp.int32, (CHUNK, CHUNK), 1)
    eye = (r == c).astype(F32)
    same_blk = lax.shift_right_logical(r, 4) == lax.shift_right_logical(c, 4)
    diag = jnp.where(same_blk, low, 0.0)
    off = low - diag
    n1 = -diag
    n2 = nn(n1, n1)
    n4 = nn(n2, n2)
    n8 = nn(n4, n4)
    inv_d = nn(nn(nn(eye + n1, eye + n2), eye + n4), eye + n8)
    m1 = nn(inv_d, off)
    m2 = nn(m1, m1)
    return nn(nn(eye - m1, eye + m2), inv_d)


@jax.custom_vjp
def _tri_inv_known(low, t_inv):
    return t_inv


def _tri_inv_known_fwd(low, t_inv):
    return t_inv, t_inv


def _tri_inv_known_bwd(t_inv, g):
    _, nt, tn = _FDOT_BATCH_PLAIN
    return -nt(tn(t_inv, g), t_inv), jnp.zeros_like(t_inv)


_tri_inv_known.defvjp(_tri_inv_known_fwd, _tri_inv_known_bwd)


LOCAL_HEADS_PER_STEP = 8


def _gdn_local_fn(qkv, ba, alog_row, dtb_row, first_head, bdots, fdots, t_known=None):
    nn, nt, tn = bdots
    fnn = fdots[0]
    n_heads = qkv.shape[1] // (3 * HEAD_DIM)
    part = lambda i, p: qkv[:, (3 * i + p) * HEAD_DIM:(3 * i + p + 1) * HEAD_DIM]
    q = jnp.stack([part(i, 0) for i in range(n_heads)]) * (HEAD_DIM ** -0.5)
    k = jnp.stack([part(i, 1) for i in range(n_heads)])
    v = jnp.stack([part(i, 2) for i in range(n_heads)])
    lane = lax.broadcasted_iota(jnp.int32, ba.shape, 1)
    bg = jnp.where(lane < GDN_HEADS, jax.nn.sigmoid(ba), -jnp.exp(alog_row) * _softplus(ba + dtb_row))
    pick = lambda l: jnp.sum(jnp.where(lane == l, bg, 0.0), axis=1, keepdims=True)
    beta = jnp.stack([pick(first_head + i) for i in range(n_heads)])
    g = jnp.stack([pick(first_head + i + GDN_HEADS) for i in range(n_heads)])

    r = lax.broadcasted_iota(jnp.int32, (CHUNK, CHUNK), 0)
    c = lax.broadcasted_iota(jnp.int32, (CHUNK, CHUNK), 1)
    incl = r >= c
    strict = r > c
    eye = r == c

    def to_row(col):
        return jnp.sum(jnp.where(eye, col, 0.0), axis=1, keepdims=True)

    gc = jnp.sum(jnp.where(incl, to_row(g), 0.0), axis=2, keepdims=True)
    diff = gc - to_row(gc)
    decay = jnp.where(incl, jnp.exp(jnp.where(incl, diff, 0.0)), 0.0)
    k_beta = k * beta
    v_beta = v * beta
    low = jnp.where(strict, nt(k_beta, k) * decay, 0.0)
    t_inv = _tri_inv(low, fnn) if t_known is None else _tri_inv_known(low, t_known)
    eg = jnp.exp(gc)
    u = fnn(t_inv, v_beta)
    w = fnn(t_inv, k_beta * eg)
    attn = jnp.where(incl, nt(q, k) * decay, 0.0)
    last = lax.broadcasted_iota(jnp.int32, (CHUNK, 1), 0) == CHUNK - 1
    g_last = jnp.sum(jnp.where(last, gc, 0.0), axis=1, keepdims=True)
    kdec = k * jnp.exp(g_last - gc)
    elast = jnp.broadcast_to(jnp.exp(g_last), (n_heads, 1, LANE))
    return u, w, q * eg, kdec, attn, elast, t_inv


def _gdn_state_fn(u, w, qg, kdec, attn, elast, state, bdots):
    nn, _, tn = bdots
    v_new = u - nn(w, state)
    o = nn(qg, state) + nn(attn, v_new)
    return o, state * elast + tn(kdec, v_new)


def _gdn_local_fwd(post, proj, alog_row, dtb_row):
    t = post.shape[0]
    n_chunks = t // CHUNK
    hb = LOCAL_HEADS_PER_STEP

    def body(qkv_ref, ba_ref, al_ref, dt_ref, u_ref, w_ref, qg_ref, kd_ref, at_ref, el_ref, ti_ref):
        u, w, qg, kdec, attn, elast, t_inv = _gdn_local_fn(qkv_ref[...], ba_ref[...], al_ref[...], dt_ref[...],
                                                           pl.program_id(1) * hb, _BDOT_BATCH_PLAIN, _FDOT_BATCH_PLAIN)
        for i in range(hb):
            cols = slice(i * HEAD_DIM, (i + 1) * HEAD_DIM)
            u_ref[:, cols] = u[i]
            w_ref[:, cols] = w[i].astype(BF16)
            qg_ref[:, cols] = qg[i].astype(BF16)
            kd_ref[:, cols] = kdec[i].astype(BF16)
        at_ref[...] = attn.astype(BF16)
        el_ref[:, 0] = elast
        ti_ref[...] = t_inv

    wide = pl.BlockSpec((CHUNK, hb * HEAD_DIM), lambda n, j: (n, j))
    square = pl.BlockSpec((hb, CHUNK, CHUNK), lambda n, j: (j, n, 0))
    row = pl.BlockSpec((1, LANE), lambda n, j: (0, 0))
    res = pl.pallas_call(
        body, grid=(n_chunks, GDN_HEADS // hb),
        in_specs=[pl.BlockSpec((CHUNK, hb * 3 * HEAD_DIM), lambda n, j: (n, j)),
                  pl.BlockSpec((CHUNK, LANE), lambda n, j: (n, BA_BLK)), row, row],
        out_specs=[wide, wide, wide, wide, square, pl.BlockSpec((hb, 1, 1, LANE), lambda n, j: (j, n, 0, 0)), square],
        out_shape=[jax.ShapeDtypeStruct((t, GDN_WIDTH), F32), jax.ShapeDtypeStruct((t, GDN_WIDTH), BF16),
                   jax.ShapeDtypeStruct((t, GDN_WIDTH), BF16), jax.ShapeDtypeStruct((t, GDN_WIDTH), BF16),
                   jax.ShapeDtypeStruct((GDN_HEADS, t, CHUNK), BF16),
                   jax.ShapeDtypeStruct((GDN_HEADS, n_chunks, 1, LANE), F32),
                   jax.ShapeDtypeStruct((GDN_HEADS, t, CHUNK), F32)],
        compiler_params=_params("parallel", "parallel"), name="gdn_local_fwd",
    )(post, proj, alog_row, dtb_row)
    return tuple(res[:6]), res[6]


def _by_head(ref):
    return jnp.stack([ref[:, h * HEAD_DIM:(h + 1) * HEAD_DIM] for h in range(ref.shape[1] // HEAD_DIM)])


def _gdn_state_specs(n_of):
    wide = pl.BlockSpec((CHUNK, GDN_WIDTH), lambda n: (n_of(n), 0))
    attn = pl.BlockSpec((GDN_HEADS, CHUNK, CHUNK), lambda n: (0, n_of(n), 0))
    elast = pl.BlockSpec((GDN_HEADS, 1, 1, LANE), lambda n: (0, n_of(n), 0, 0))
    saved = pl.BlockSpec((GDN_HEADS, 1, HEAD_DIM, HEAD_DIM), lambda n: (0, n_of(n), 0, 0))
    return wide, attn, elast, saved


def _gdn_state_fwd(u, w, qg, kdec, attn, elast):
    t = u.shape[0]
    n_chunks = t // CHUNK

    def body(u_ref, w_ref, qg_ref, kd_ref, at_ref, el_ref, o_ref, save_ref, state_ref):
        @pl.when(pl.program_id(0) == 0)
        def _():
            state_ref[...] = jnp.zeros_like(state_ref)

        state = state_ref[...]
        save_ref[:, 0] = state
        o, new_state = _gdn_state_fn(_by_head(u_ref), _by_head(w_ref), _by_head(qg_ref), _by_head(kd_ref), at_ref[...],
                                     el_ref[:, 0], state, _BDOT_BATCH_PLAIN)
        for h in range(GDN_HEADS):
            o_ref[:, h * HEAD_DIM:(h + 1) * HEAD_DIM] = o[h]
        state_ref[...] = new_state

    wide, attn_spec, elast_spec, saved_spec = _gdn_state_specs(lambda n: n)
    return pl.pallas_call(
        body, grid=(n_chunks,), in_specs=[wide, wide, wide, wide, attn_spec, elast_spec],
        out_specs=[wide, saved_spec],
        out_shape=[jax.ShapeDtypeStruct((t, GDN_WIDTH), F32),
                   jax.ShapeDtypeStruct((GDN_HEADS, n_chunks, HEAD_DIM, HEAD_DIM), F32)],
        scratch_shapes=[pltpu.VMEM((GDN_HEADS, HEAD_DIM, HEAD_DIM), F32)],
        compiler_params=_params("arbitrary"), name="gdn_state_fwd",
    )(u, w, qg, kdec, attn, elast)


def _gdn_state_bwd(u, w, qg, kdec, attn, elast, saved, do):
    t = u.shape[0]
    n_chunks = t // CHUNK
    last = n_chunks - 1

    def body(u_ref, w_ref, qg_ref, kd_ref, at_ref, el_ref, save_ref, do_ref,
             du_ref, dw_ref, dqg_ref, dkd_ref, dat_ref, del_ref, dstate_ref):
        @pl.when(pl.program_id(0) == 0)
        def _():
            dstate_ref[...] = jnp.zeros_like(dstate_ref)

        _, vjp = jax.vjp(
            lambda *a: _gdn_state_fn(*a, _BDOT_BATCH_VJP), _by_head(u_ref), _by_head(w_ref).astype(F32),
            _by_head(qg_ref).astype(F32), _by_head(kd_ref).astype(F32), at_ref[...].astype(F32), el_ref[:, 0],
            save_ref[:, 0])
        du, dw, dqg, dkd, dat, de, dstate = vjp((_by_head(do_ref), dstate_ref[...]))
        for h in range(GDN_HEADS):
            cols = slice(h * HEAD_DIM, (h + 1) * HEAD_DIM)
            du_ref[:, cols] = du[h]
            dw_ref[:, cols] = dw[h]
            dqg_ref[:, cols] = dqg[h]
            dkd_ref[:, cols] = dkd[h]
        dat_ref[...] = dat
        del_ref[:, 0] = de
        dstate_ref[...] = dstate

    wide, attn_spec, elast_spec, saved_spec = _gdn_state_specs(lambda n: last - n)
    wide_f32 = jax.ShapeDtypeStruct((t, GDN_WIDTH), F32)
    return pl.pallas_call(
        body, grid=(n_chunks,), in_specs=[wide, wide, wide, wide, attn_spec, elast_spec, saved_spec, wide],
        out_specs=[wide, wide, wide, wide, attn_spec, elast_spec],
        out_shape=[wide_f32, wide_f32, wide_f32, wide_f32, jax.ShapeDtypeStruct((GDN_HEADS, t, CHUNK), F32),
                   jax.ShapeDtypeStruct((GDN_HEADS, n_chunks, 1, LANE), F32)],
        scratch_shapes=[pltpu.VMEM((GDN_HEADS, HEAD_DIM, HEAD_DIM), F32)],
        compiler_params=_params("arbitrary"), name="gdn_state_bwd",
    )(u, w, qg, kdec, attn, elast, saved, do)


def _gdn_local_bwd(post, proj, alog_row, dtb_row, t_inv, cots, dproj):
    t = post.shape[0]
    n_chunks = t // CHUNK
    hb = LOCAL_HEADS_PER_STEP
    n_steps = GDN_HEADS // hb

    def body(qkv_ref, ba_ref, al_ref, dt_ref, ti_ref, du_ref, dw_ref, dqg_ref, dkd_ref, dat_ref, del_ref, _,
             dqkv_ref, dba_ref, dal_ref, ddt_ref, dba_acc):
        n = pl.program_id(0)
        j = pl.program_id(1)

        @pl.when((n == 0) & (j == 0))
        def _():
            dal_ref[...] = jnp.zeros_like(dal_ref)
            ddt_ref[...] = jnp.zeros_like(ddt_ref)

        @pl.when(j == 0)
        def _():
            dba_acc[...] = jnp.zeros_like(dba_acc)

        t_known = ti_ref[...]
        _, vjp = jax.vjp(
            lambda a, b, c, d: _gdn_local_fn(a, b, c, d, j * hb, _BDOT_BATCH_VJP, _FDOT_BATCH_VJP, t_known)[:6],
            qkv_ref[...], ba_ref[...], al_ref[...], dt_ref[...])
        dqkv, dba, dal, ddt = vjp((_by_head(du_ref), _by_head(dw_ref), _by_head(dqg_ref), _by_head(dkd_ref), dat_ref[...],
                                   del_ref[:, 0]))
        dqkv_ref[...] = dqkv
        dba_acc[...] += dba
        dal_ref[...] += dal
        ddt_ref[...] += ddt

        @pl.when(j == n_steps - 1)
        def _():
            dba_ref[:, 0:LANE] = dba_acc[...].astype(dba_ref.dtype)
            dba_ref[:, LANE:2 * LANE] = jnp.zeros((CHUNK, LANE), dba_ref.dtype)

    wide = pl.BlockSpec((CHUNK, hb * HEAD_DIM), lambda n, j: (n, j))
    qkv_spec = pl.BlockSpec((CHUNK, hb * 3 * HEAD_DIM), lambda n, j: (n, j))
    row = pl.BlockSpec((1, LANE), lambda n, j: (0, 0))
    return pl.pallas_call(
        body, grid=(n_chunks, n_steps),
        in_specs=[qkv_spec, pl.BlockSpec((CHUNK, LANE), lambda n, j: (n, BA_BLK)), row, row,
                  pl.BlockSpec((hb, CHUNK, CHUNK), lambda n, j: (j, n, 0)), wide, wide, wide, wide,
                  pl.BlockSpec((hb, CHUNK, CHUNK), lambda n, j: (j, n, 0)),
                  pl.BlockSpec((hb, 1, 1, LANE), lambda n, j: (j, n, 0, 0)), pl.BlockSpec(memory_space=pl.ANY)],
        out_specs=[qkv_spec, pl.BlockSpec((CHUNK, 2 * LANE), lambda n, j: (n, BA_BLK // 2)), row, row],
        out_shape=[jax.ShapeDtypeStruct((t, QKV_COLS), F32), jax.ShapeDtypeStruct(dproj.shape, dproj.dtype),
                   jax.ShapeDtypeStruct((1, LANE), F32), jax.ShapeDtypeStruct((1, LANE), F32)],
        input_output_aliases={11: 1},
        scratch_shapes=[pltpu.VMEM((CHUNK, LANE), F32)],
        compiler_params=_params("arbitrary", "arbitrary"), name="gdn_local_bwd",
    )(post, proj, alog_row, dtb_row, t_inv, *cots, dproj)


def _onorm_fn(o, z, w):
    return o * lax.rsqrt(jnp.mean(o * o, axis=1, keepdims=True) + NORM_EPS) * w * (z * jax.nn.sigmoid(z))


_Z_WIDE_BLK = Z_OFF // GDN_WIDTH


def _onorm_fwd(o_raw, proj, norm_w, mixin, tm=256):
    t = o_raw.shape[0]
    tm = min(tm, t)

    def body(o_ref, z_ref, w_ref, _, out_ref):
        for h in range(GDN_HEADS):
            cols = slice(h * HEAD_DIM, (h + 1) * HEAD_DIM)
            out_ref[:, cols] = _onorm_fn(o_ref[:, cols], z_ref[:, cols], w_ref[...]).astype(out_ref.dtype)

    wide = pl.BlockSpec((tm, GDN_WIDTH), lambda i: (i, 0))
    return pl.pallas_call(
        body, grid=(t // tm,),
        in_specs=[wide, pl.BlockSpec((tm, GDN_WIDTH), lambda i: (i, _Z_WIDE_BLK)), pl.BlockSpec((1, LANE), lambda i: (0, 0)),
                  pl.BlockSpec(memory_space=pl.ANY)],
        out_specs=wide, out_shape=jax.ShapeDtypeStruct(mixin.shape, mixin.dtype), input_output_aliases={3: 0},
        compiler_params=_params("parallel"), name="gdn_onorm_fwd",
    )(o_raw, proj, norm_w, mixin)


def _onorm_bwd(o_raw, proj, norm_w, dmixin, dproj, tm=256):
    t = o_raw.shape[0]
    tm = min(tm, t)

    def body(o_ref, z_ref, w_ref, d_ref, _, do_ref, dz_ref, dw_ref):
        @pl.when(pl.program_id(0) == 0)
        def _():
            dw_ref[...] = jnp.zeros_like(dw_ref)

        for h in range(GDN_HEADS):
            cols = slice(h * HEAD_DIM, (h + 1) * HEAD_DIM)
            _, vjp = jax.vjp(_onorm_fn, o_ref[:, cols], z_ref[:, cols], w_ref[...])
            do, dz, dw = vjp(d_ref[:, cols])
            do_ref[:, cols] = do
            dz_ref[:, cols] = dz.astype(dz_ref.dtype)
            dw_ref[...] += dw

    wide = pl.BlockSpec((tm, GDN_WIDTH), lambda i: (i, 0))
    gate = pl.BlockSpec((tm, GDN_WIDTH), lambda i: (i, _Z_WIDE_BLK))
    row = pl.BlockSpec((1, LANE), lambda i: (0, 0))
    return pl.pallas_call(
        body, grid=(t // tm,), in_specs=[wide, gate, row, wide, pl.BlockSpec(memory_space=pl.ANY)],
        out_specs=[wide, gate, row],
        out_shape=[jax.ShapeDtypeStruct((t, GDN_WIDTH), F32), jax.ShapeDtypeStruct(dproj.shape, dproj.dtype),
                   jax.ShapeDtypeStruct((1, LANE), F32)],
        input_output_aliases={4: 1},
        compiler_params=_params("arbitrary"), name="gdn_onorm_bwd",
    )(o_raw, proj, norm_w, dmixin, dproj)


def _pool_select(levels, gi):
    out = levels[-1]
    for lvl in range(len(levels) - 2, -1, -1):
        out = jnp.where(gi == lvl, levels[lvl], out)
    return out


def _pool_count(shape, gi):
    pos = lax.broadcasted_iota(jnp.int32, shape, 0)
    win = lax.shift_left(jnp.int32(2), gi)
    return jnp.minimum(pos + 1, win).astype(F32)


def _pooled(p, gi):
    acc = p
    levels = []
    for lvl in range(POOL_GROUPS):
        acc = acc + _shift_down(acc, 1 << lvl)
        levels.append(acc)
    return _pool_select(levels, gi) / _pool_count(p.shape, gi) - p


def _pool_fwd(proj, pool_w, pool_scale):
    t = proj.shape[0]

    def body(p_ref, w_ref, s_ref, out_ref):
        gi = pl.program_id(0)
        pooled = _pooled(p_ref[...], gi)
        out_ref[...] = (_BDOT_PLAIN[0](pooled, w_ref[0]) * s_ref[0]).astype(out_ref.dtype)

    return pl.pallas_call(
        body, grid=(POOL_GROUPS,),
        in_specs=[pl.BlockSpec((t, POOL_GROUP_DIM), lambda g: (0, POOL_BLK + g)),
                  pl.BlockSpec((1, POOL_GROUP_DIM, POOL_GROUP_DIM), lambda g: (g, 0, 0)),
                  pl.BlockSpec((1, 1, POOL_GROUP_DIM), lambda g: (g, 0, 0))],
        out_specs=pl.BlockSpec((t, POOL_GROUP_DIM), lambda g: (0, GDN_WIDTH // POOL_GROUP_DIM + g)),
        out_shape=jax.ShapeDtypeStruct((t, 2 * GDN_WIDTH), BF16),
        compiler_params=_params("parallel"), name="pool_fwd",
    )(proj, pool_w, pool_scale)


def _pool_bwd(proj, pool_w, pool_scale, dmixin):
    t = proj.shape[0]
    nn, nt, tn = _BDOT_PLAIN

    def body(p_ref, w_ref, s_ref, d_ref, dp_ref, dw_ref, ds_ref):
        gi = pl.program_id(0)
        p = p_ref[...]
        pooled = _pooled(p, gi)
        mixed = nn(pooled, w_ref[0])
        d = d_ref[...]
        ds_ref[0] = jnp.sum(d * mixed, axis=0, keepdims=True)
        dmixed = d * s_ref[0]
        dw_ref[0] = tn(pooled, dmixed)
        dpooled = nt(dmixed, w_ref[0])
        acc = dpooled / _pool_count(p.shape, gi)
        levels = []
        for lvl in range(POOL_GROUPS):
            acc = acc + _shift_up(acc, 1 << lvl)
            levels.append(acc)
        dp_ref[...] = (_pool_select(levels, gi) - dpooled).astype(dp_ref.dtype)

    return pl.pallas_call(
        body, grid=(POOL_GROUPS,),
        in_specs=[pl.BlockSpec((t, POOL_GROUP_DIM), lambda g: (0, POOL_BLK + g)),
                  pl.BlockSpec((1, POOL_GROUP_DIM, POOL_GROUP_DIM), lambda g: (g, 0, 0)),
                  pl.BlockSpec((1, 1, POOL_GROUP_DIM), lambda g: (g, 0, 0)),
                  pl.BlockSpec((t, POOL_GROUP_DIM), lambda g: (0, GDN_WIDTH // POOL_GROUP_DIM + g))],
        out_specs=[pl.BlockSpec((t, POOL_GROUP_DIM), lambda g: (0, POOL_BLK + g)),
                   pl.BlockSpec((1, POOL_GROUP_DIM, POOL_GROUP_DIM), lambda g: (g, 0, 0)),
                   pl.BlockSpec((1, 1, POOL_GROUP_DIM), lambda g: (g, 0, 0))],
        out_shape=[jax.ShapeDtypeStruct((t, PROJ_COLS), BF16),
                   jax.ShapeDtypeStruct((POOL_GROUPS, POOL_GROUP_DIM, POOL_GROUP_DIM), F32),
                   jax.ShapeDtypeStruct((POOL_GROUPS, 1, POOL_GROUP_DIM), F32)],
        compiler_params=_params("parallel"), name="pool_bwd",
    )(proj, pool_w, pool_scale, dmixin)


def _ln_stats(s):
    mu = jnp.mean(s, axis=1, keepdims=True)
    xc = s - mu
    var = jnp.mean(xc * xc, axis=1, keepdims=True)
    rstd = lax.rsqrt(var + LN_EPS)
    return xc * rstd, rstd


def _mm_ln(a, b, h_in, g, bias, *, name, tm=256):
    t, k = a.shape
    d = b.shape[1]
    tm = min(tm, t)

    def body(a_ref, b_ref, h_ref, g_ref, bias_ref, y_ref, o_ref, o16_ref):
        y = jnp.dot(a_ref[...].astype(BF16), b_ref[...], preferred_element_type=F32)
        y_ref[...] = y
        xhat, _ = _ln_stats(ALPHA * h_ref[...] + y)
        out = xhat * g_ref[...] + bias_ref[...]
        o_ref[...] = out
        o16_ref[...] = out.astype(BF16)

    row = pl.BlockSpec((tm, d), lambda i: (i, 0))
    vec = pl.BlockSpec((1, d), lambda i: (0, 0))
    return pl.pallas_call(
        body, grid=(t // tm,),
        in_specs=[pl.BlockSpec((tm, k), lambda i: (i, 0)), pl.BlockSpec((k, d), lambda i: (0, 0)), row, vec, vec],
        out_specs=[row, row, row],
        out_shape=[jax.ShapeDtypeStruct((t, d), F32), jax.ShapeDtypeStruct((t, d), F32), jax.ShapeDtypeStruct((t, d), BF16)],
        compiler_params=_params("parallel"), name=name,
    )(a, b, h_in, g, bias)


def _ln_backward(xhat, rstd, dout, gain):
    dxhat = dout * gain
    m1 = jnp.mean(dxhat, axis=1, keepdims=True)
    m2 = jnp.mean(dxhat * xhat, axis=1, keepdims=True)
    return (rstd * (dxhat - m1 - xhat * m2), jnp.sum(dout * xhat, axis=0, keepdims=True),
            jnp.sum(dout, axis=0, keepdims=True))


def _ln_loss(h_in, y, g, b, target, *, name, tm=256):
    t, d = h_in.shape
    tm = min(tm, t)

    def body(h_ref, y_ref, g_ref, b_ref, t_ref, sq_ref, ds_ref, ds16_ref, dg_ref, dbias_ref):
        @pl.when(pl.program_id(0) == 0)
        def _():
            sq_ref[...] = jnp.zeros_like(sq_ref)
            dg_ref[...] = jnp.zeros_like(dg_ref)
            dbias_ref[...] = jnp.zeros_like(dbias_ref)

        xhat, rstd = _ln_stats(ALPHA * h_ref[...] + y_ref[...])
        err = xhat * g_ref[...] + b_ref[...] - t_ref[...]
        sq_ref[...] += jnp.sum(jnp.sum(err * err, axis=1, keepdims=True), axis=0, keepdims=True)
        ds, dg, dbias = _ln_backward(xhat, rstd, err * (1.0 / d), g_ref[...])
        ds_ref[...] = ds
        ds16_ref[...] = ds.astype(BF16)
        dg_ref[...] += dg
        dbias_ref[...] += dbias

    row = pl.BlockSpec((tm, d), lambda i: (i, 0))
    vec = pl.BlockSpec((1, d), lambda i: (0, 0))
    return pl.pallas_call(
        body, grid=(t // tm,), in_specs=[row, row, vec, vec, row],
        out_specs=[pl.BlockSpec((1, LANE), lambda i: (0, 0)), row, row, vec, vec],
        out_shape=[jax.ShapeDtypeStruct((1, LANE), F32), jax.ShapeDtypeStruct((t, d), F32),
                   jax.ShapeDtypeStruct((t, d), BF16), jax.ShapeDtypeStruct((1, d), F32), jax.ShapeDtypeStruct((1, d), F32)],
        compiler_params=_params("arbitrary"), name=name,
    )(h_in, y, g, b, target)


def _ln_bwd(h_in, y, g, d_a, d_b, *, name, tm=256):
    t, d = h_in.shape
    tm = min(tm, t)
    has_b = d_b is not None

    def body(*refs):
        if has_b:
            h_ref, y_ref, g_ref, da_ref, db_ref, ds_ref, ds16_ref, dg_ref, dbias_ref = refs
        else:
            h_ref, y_ref, g_ref, da_ref, ds_ref, ds16_ref, dg_ref, dbias_ref = refs

        @pl.when(pl.program_id(0) == 0)
        def _():
            dg_ref[...] = jnp.zeros_like(dg_ref)
            dbias_ref[...] = jnp.zeros_like(dbias_ref)

        xhat, rstd = _ln_stats(ALPHA * h_ref[...] + y_ref[...])
        dout = da_ref[...]
        if has_b:
            dout = dout + ALPHA * db_ref[...]
        ds, dg, dbias = _ln_backward(xhat, rstd, dout, g_ref[...])
        ds_ref[...] = ds
        ds16_ref[...] = ds.astype(BF16)
        dg_ref[...] += dg
        dbias_ref[...] += dbias

    row = pl.BlockSpec((tm, d), lambda i: (i, 0))
    vec = pl.BlockSpec((1, d), lambda i: (0, 0))
    args = [h_in, y, g, d_a] + ([d_b] if has_b else [])
    return pl.pallas_call(
        body, grid=(t // tm,), in_specs=[row, row, vec, row] + ([row] if has_b else []),
        out_specs=[row, row, vec, vec],
        out_shape=[jax.ShapeDtypeStruct((t, d), F32), jax.ShapeDtypeStruct((t, d), BF16),
                   jax.ShapeDtypeStruct((1, d), F32), jax.ShapeDtypeStruct((1, d), F32)],
        compiler_params=_params("arbitrary"), name=name,
    )(*args)


def _attn_fn(q, k, v, dots):
    nn, nt, _ = dots
    s = nt(q, k) * (XATTN_HEAD_DIM ** -0.5)
    s = s - lax.stop_gradient(jnp.max(s, axis=1, keepdims=True))
    e = jnp.exp(s)
    p = e / jnp.sum(e, axis=1, keepdims=True)
    return nn(p, v)


def _attn_fwd(q, k, v, tq=2048):
    t = q.shape[0]
    tq = min(tq, t)

    def body(q_ref, k_ref, v_ref, o_ref):
        o_ref[...] = _attn_fn(q_ref[...], k_ref[...], v_ref[...], _BDOT_PLAIN).astype(BF16)

    qs = pl.BlockSpec((tq, XATTN_HEAD_DIM), lambda h, i: (i, h))
    ks = pl.BlockSpec((MEM_LEN, XATTN_HEAD_DIM), lambda h, i: (0, h))
    return pl.pallas_call(
        body, grid=(XATTN_HEADS, t // tq), in_specs=[qs, ks, ks], out_specs=qs,
        out_shape=jax.ShapeDtypeStruct(q.shape, BF16), compiler_params=_params("parallel", "parallel"), name="xattn_fwd",
    )(q, k, v)


def _attn_bwd(q, k, v, do, tq=1024):
    t = q.shape[0]
    tq = min(tq, t)

    def body(q_ref, k_ref, v_ref, do_ref, dq_ref, dk_ref, dv_ref):
        @pl.when(pl.program_id(1) == 0)
        def _():
            dk_ref[...] = jnp.zeros_like(dk_ref)
            dv_ref[...] = jnp.zeros_like(dv_ref)

        _, vjp = jax.vjp(lambda a, b, c: _attn_fn(a, b, c, _BDOT_VJP), q_ref[...].astype(F32), k_ref[...].astype(F32),
                         v_ref[...].astype(F32))
        dq, dk, dv = vjp(do_ref[...].astype(F32))
        dq_ref[...] = dq.astype(BF16)
        dk_ref[...] += dk
        dv_ref[...] += dv

    qs = pl.BlockSpec((tq, XATTN_HEAD_DIM), lambda h, i: (i, h))
    ks = pl.BlockSpec((MEM_LEN, XATTN_HEAD_DIM), lambda h, i: (0, h))
    return pl.pallas_call(
        body, grid=(XATTN_HEADS, t // tq), in_specs=[qs, ks, ks, qs], out_specs=[qs, ks, ks],
        out_shape=[jax.ShapeDtypeStruct(q.shape, BF16), jax.ShapeDtypeStruct(k.shape, F32), jax.ShapeDtypeStruct(v.shape, F32)],
        compiler_params=_params("parallel", "arbitrary"), name="xattn_bwd",
    )(q, k, v, do)


def _local_step(x, x16, mem, target, weights_of, grads_ready):
    def behind(vec, token):
        return vec if token is None else vec + token

    w = dict(weights_of("mixer", None))
    proj = _mm(x16, w["w_in"], tb=True, tn=768, name="mm_in_proj")
    mixin = _pool_fwd(proj, w["pool_w"], w["pool_scale"])
    post = _gdn_prep_fwd(proj, w["conv_w"])
    token = weights_of("ahead_conv", post)
    chunked, t_inv = _gdn_local_fwd(post, proj, behind(w["alog_row"], token), w["dtb_row"])
    o_raw, saved = _gdn_state_fwd(*chunked)
    token = weights_of("ahead_scan", o_raw)
    mixin = _onorm_fwd(o_raw, proj, behind(w["gdn_norm_w"], token), mixin)
    w.update(weights_of("attn", mixin))
    mix, h1, h1_16 = _mm_ln(mixin, w["w_out"], x, w["ln1_g"], w["ln1_b"], name="mm_out_proj_ln1")
    xq = _mm(h1_16, w["xq_w"], out_dtype=BF16, name="mm_xq")
    xk = _mm(mem, w["xk_w"], out_dtype=BF16, name="mm_xk")
    xv = _mm(mem, w["xv_w"], out_dtype=BF16, name="mm_xv")
    xo = _attn_fwd(xq, xk, xv)
    token = weights_of("ahead_attn", xo)
    if token is not None:
        xo, _ = lax.optimization_barrier((xo, token))
    xa, h2, h2_16 = _mm_ln(xo, w["xo_w"], h1, w["ln2_g"], w["ln2_b"], name="mm_xo_ln2")
    w.update(weights_of("up", h2_16))
    act, relu = _mm(h2_16, w["w_up"], b_chunks=True, epi="relu2", name="mm_up")
    w.update(weights_of("down", act))
    ff = _mm(act, w["w_down"], tn=512, tk=2048, name="mm_down")
    g = {}
    sq, ds3, ds3_16, g["ln3_g"], g["ln3_b"] = _ln_loss(h2, ff, w["ln3_g"], w["ln3_b"], target, name="ln3_loss")

    gw_down = _mm(act, ds3_16, ta=True, out_dtype=BF16, tm=512, tn=D_MODEL, name="mm_gw_down")
    du = _mm(ds3_16, w["w_down"], tb=True, epi="mul2r", extra=relu, name="mm_du")
    gw_up = _mm(h2_16, du, ta=True, out_dtype=BF16, o_chunks=True, name="mm_gw_up")
    token = grads_ready("mlp", {"w_down": gw_down, "w_up": gw_up})
    dh2 = _mm(du, w["w_up"], tb=True, b_chunks=True, tn=1024, tk=1024, name="mm_dh2")
    ds2, ds2_16, g["ln2_g"], g["ln2_b"] = _ln_bwd(h1, xa, behind(w["ln2_g"], token), dh2, ds3, name="ln2_bwd")
    gw_xo = _mm(xo, ds2_16, ta=True, out_dtype=BF16, name="mm_gw_xo")
    dxo = _mm(ds2_16, w["xo_w"], tb=True, out_dtype=BF16, name="mm_dxo")
    dxq, dxk, dxv = _attn_bwd(xq, xk, xv, dxo)
    gw_xq = _mm(h1_16, dxq, ta=True, out_dtype=BF16, name="mm_gw_xq")
    gw_xk = _mm(mem, dxk, ta=True, out_dtype=BF16, name="mm_gw_xk")
    gw_xv = _mm(mem, dxv, ta=True, out_dtype=BF16, name="mm_gw_xv")
    token = grads_ready("attn", {"xo_w": gw_xo, "xq_w": gw_xq, "xk_w": gw_xk, "xv_w": gw_xv})
    dh1 = _mm(dxq, w["xq_w"], tb=True, name="mm_dh1")
    ds1, ds1_16, g["ln1_g"], g["ln1_b"] = _ln_bwd(x, mix, behind(w["ln1_g"], token), dh1, ds2, name="ln1_bwd")
    gw_out = _mm(mixin, ds1_16, ta=True, out_dtype=BF16, name="mm_gw_out")
    dmixin = _mm(ds1_16, w["w_out"], tb=True, name="mm_dmixin")
    dproj, gw_pool, g["pool_scale"] = _pool_bwd(proj, w["pool_w"], w["pool_scale"], dmixin)
    token = grads_ready("mix", {"w_out": gw_out, "pool_w": gw_pool})
    do_raw, dproj, g["gdn_norm_w"] = _onorm_bwd(o_raw, proj, behind(w["gdn_norm_w"], token), dmixin, dproj)
    cots = _gdn_state_bwd(*chunked, saved, do_raw)
    token = grads_ready("tick", {"after": cots[0]})
    dpost, dproj, g["alog_row"], g["dtb_row"] = _gdn_local_bwd(post, proj, behind(w["alog_row"], token), w["dtb_row"],
                                                               t_inv, cots, dproj)
    dproj, g["conv_w"] = _gdn_prep_bwd(proj, w["conv_w"], dpost, dproj)
    token = grads_ready("small", {**g, "sq": sq})
    gw_in = _mm(dproj, x16, ta=True, out_dtype=BF16, tm=768, tn=D_MODEL, after=token, name="mm_gw_in")
    token = grads_ready("in", {"w_in": gw_in})
    grad_x = _mm(dproj, w["w_in"], tk=1792, epi="add", extra=ds1, add_scale=ALPHA, after=token, name="mm_dx")
    return sq, grad_x, g


_VECTORS = ("a_log", "dt_bias", "gdn_norm_w", "pool_scale", "ln1_g", "ln1_b", "ln2_g", "ln2_b", "ln3_g", "ln3_b")
_BA_SPLIT = BA_OFF + 2 * GDN_HEADS


def _lane_row(v, offset):
    return jnp.zeros((1, LANE), F32).at[0, offset:offset + v.shape[0]].set(v)


_GROUP_VECTORS = {"mixer": (), "attn": ("ln1_g", "ln1_b", "ln2_g", "ln2_b"), "up": (), "down": ("ln3_g", "ln3_b")}


def _group_weights(group, full):
    w = {n: full[n].reshape(1, D_MODEL) for n in _GROUP_VECTORS[group]}
    if group == "mixer":
        w.update({
            "w_in": _w_in_padded(full["w_in"]),
            "conv_w": full["conv_w"],
            "alog_row": _lane_row(full["a_log"], GDN_HEADS),
            "dtb_row": _lane_row(full["dt_bias"], GDN_HEADS),
            "gdn_norm_w": full["gdn_norm_w"].reshape(1, LANE),
            "pool_w": full["pool_w"],
            "pool_scale": full["pool_scale"].reshape(POOL_GROUPS, 1, POOL_GROUP_DIM),
        })
    else:
        w.update({n: full[n] for n in dict(_GATHER_GROUPS)[group]})
    return w


def _w_in_row_map():
    per = IN_COLS // N_DEV
    gap = POOL_OFF - _BA_SPLIT
    pieces = []
    for d in range(N_DEV):
        lo, hi = d * per, (d + 1) * per
        if hi <= _BA_SPLIT:
            pieces.append([(0, lo, per)])
        elif lo >= _BA_SPLIT:
            pieces.append([(0, lo + gap, per)])
        else:
            pieces.append([(0, lo, _BA_SPLIT - lo), (_BA_SPLIT - lo, POOL_OFF, hi - _BA_SPLIT)])
    return pieces


_W_IN_LANES = 256


def _w_in_padded(blocks):
    def body(b_ref, o_ref):
        for d, pieces in enumerate(_w_in_row_map()):
            for src, dst, rows in pieces:
                o_ref[dst:dst + rows, :] = b_ref[d, src:src + rows, :]
        o_ref[_BA_SPLIT:POOL_OFF, :] = jnp.zeros((POOL_OFF - _BA_SPLIT, _W_IN_LANES), o_ref.dtype)

    n, per, cols = blocks.shape
    return pl.pallas_call(
        body, grid=(cols // _W_IN_LANES,), in_specs=[pl.BlockSpec((n, per, _W_IN_LANES), lambda j: (0, 0, j))],
        out_specs=pl.BlockSpec((PROJ_COLS, _W_IN_LANES), lambda j: (0, j)),
        out_shape=jax.ShapeDtypeStruct((PROJ_COLS, cols), blocks.dtype), compiler_params=_params("parallel"),
        name="w_in_padded")(blocks)


def _w_in_chunks(g):
    def body(g_ref, o_ref):
        for d, pieces in enumerate(_w_in_row_map()):
            for dst, src, rows in pieces:
                o_ref[d, dst:dst + rows, :] = g_ref[src:src + rows, :]

    cols = g.shape[1]
    per = IN_COLS // N_DEV
    return pl.pallas_call(
        body, grid=(cols // _W_IN_LANES,), in_specs=[pl.BlockSpec((PROJ_COLS, _W_IN_LANES), lambda j: (0, j))],
        out_specs=pl.BlockSpec((N_DEV, per, _W_IN_LANES), lambda j: (0, 0, j)),
        out_shape=jax.ShapeDtypeStruct((N_DEV, per, cols), g.dtype), compiler_params=_params("parallel"),
        name="w_in_chunks")(g)


def _finish_small_grads(g):
    out = {"conv_w": g["conv_w"]}
    out["a_log"] = g["alog_row"][0, GDN_HEADS:2 * GDN_HEADS]
    out["dt_bias"] = g["dtb_row"][0, GDN_HEADS:2 * GDN_HEADS]
    out["gdn_norm_w"] = g["gdn_norm_w"].reshape(LANE)
    out["pool_scale"] = g["pool_scale"].reshape(POOL_GROUPS * POOL_GROUP_DIM)
    for n in ("ln1_g", "ln1_b", "ln2_g", "ln2_b", "ln3_g", "ln3_b"):
        out[n] = g[n].reshape(D_MODEL)
    return out


def _adamw_math(w, g, m, v):
    m = ADAM_B1 * m + (1.0 - ADAM_B1) * g
    v = ADAM_B2 * v + (1.0 - ADAM_B2) * (g * g)
    m_hat = m / (1.0 - ADAM_B1 ** ADAM_STEP)
    v_hat = v / (1.0 - ADAM_B2 ** ADAM_STEP)
    delta = -ADAM_LR * (m_hat / (jnp.sqrt(v_hat) + ADAM_EPS) + ADAM_WD * w)
    return delta, m, v


ADAMW_TILE_ELEMS = 256 * 1024
CHIP_SUM_TILE_ELEMS = 1024 * 1024


def _shard_tile(r, c, elems):
    for rows in (1024, 512, 256, 128):
        if r % rows == 0 and rows * c <= elems:
            return rows, c
    if r % 128 == 0:
        return 128, c
    return r, 256 if c % 256 == 0 else c


def _adamw_shard(parts, own, me, w, m, v, *, name):
    s, r, c = parts.shape
    tr, tc = _shard_tile(r, c, ADAMW_TILE_ELEMS)
    assert r % tr == 0 and c % tc == 0, (name, r, c)
    unit_axis = w.ndim == 3
    at = (slice(None), 0, slice(None)) if unit_axis else Ellipsis

    def body(me_ref, p_ref, own_ref, w_ref, m_ref, v_ref, g_ref, d_ref, nm_ref, nv_ref):
        mine = own_ref[...].astype(F32)
        g = None
        for i in range(s):
            part = jnp.where(me_ref[0] == i, mine, p_ref[i].astype(F32))
            g = part if g is None else g + part
        delta, nm, nv = _adamw_math(w_ref[at], g, m_ref[at], v_ref[at])
        g_ref[at] = g
        d_ref[at] = delta
        nm_ref[at] = nm
        nv_ref[at] = nv

    if unit_axis:
        blk = pl.BlockSpec((tr, 1, tc), lambda i, j, me_ref: (i, 0, j))
        out = jax.ShapeDtypeStruct((r, 1, c), F32)
    else:
        blk = pl.BlockSpec((tr, tc), lambda i, j, me_ref: (i, j))
        out = jax.ShapeDtypeStruct((r, c), F32)
    return pl.pallas_call(
        body,
        grid_spec=pltpu.PrefetchScalarGridSpec(
            num_scalar_prefetch=1, grid=(r // tr, c // tc),
            in_specs=[pl.BlockSpec((s, tr, tc), lambda i, j, me_ref: (0, i, j)),
                      pl.BlockSpec((None, tr, tc), lambda i, j, me_ref: (me_ref[0], i, j)), blk, blk, blk],
            out_specs=[blk, blk, blk, blk]),
        out_shape=[out, out, out, out], compiler_params=_params("parallel", "parallel"), name=name,
    )(me, parts, own, w, m, v)


N_CHIPS = N_DEV // 2


def _chip_sums(chunks, from_sibling, core, *, name):
    _, r, c = chunks.shape
    tr, tc = _shard_tile(r, c, CHIP_SUM_TILE_ELEMS)
    assert r % tr == 0 and c % tc == 0, (name, r, c)

    def body(core_ref, mine_ref, other_ref, o_ref):
        o_ref[...] = (mine_ref[...].astype(F32) + other_ref[...].astype(F32)).astype(o_ref.dtype)

    by_chip = pl.BlockSpec((None, tr, tc), lambda q, i, j, core_ref: (q, i, j))
    return pl.pallas_call(
        body,
        grid_spec=pltpu.PrefetchScalarGridSpec(
            num_scalar_prefetch=1, grid=(N_CHIPS, r // tr, c // tc),
            in_specs=[pl.BlockSpec((None, tr, tc), lambda q, i, j, core_ref: (2 * q + core_ref[0], i, j)), by_chip],
            out_specs=by_chip),
        out_shape=jax.ShapeDtypeStruct((N_CHIPS, r, c), chunks.dtype),
        compiler_params=_params("parallel", "parallel", "parallel"), name=name,
    )(core, chunks, from_sibling)


def _place():
    return lax.axis_index("x"), lax.axis_index("y"), lax.axis_index("c")


def _slot(px, py, pc):
    return 4 * px + 2 * py + pc


_HBM = pl.BlockSpec(memory_space=pltpu.HBM)


_SEM = pl.BlockSpec(memory_space=pltpu.SEMAPHORE)
_ANY = pl.BlockSpec(memory_space=pl.ANY)
_EFFECT = pltpu.SideEffectType.DATAFLOW_SIDE_EFFECTING


def _peer(k, x, y, c):
    return (1 - x if k & 4 else x, 1 - y if k & 2 else y, 1 - c if k & 1 else c)


_EXCHANGE_BITS = {"gather_near": (1, 2, 4), "gather_relay": (6,), "gather_pass": (2, 4, 6),
                  "scatter_sibling": (1, 1, 1, 1), "scatter_chips": (2, 4, 6), "all_small": (1, 2, 3, 4, 5, 6, 7)}


def _exchange_copy(mode, src, land, w, i, place, send_sems, recv_sems, receiving):
    bits = _EXCHANGE_BITS[mode]
    k = bits[i]
    peer = _peer(k, *place)
    me = _slot(*place)
    if mode in ("gather_near", "all_small"):
        to, src_ref, sent_to, got_at = peer, src[w], me, _slot(*peer)
    elif mode == "gather_relay":
        x, y, c = place
        other = 1 - c
        to = (lax.bitwise_xor(x, c), lax.bitwise_xor(y, other), c)
        blk = _slot(lax.bitwise_xor(x, other), lax.bitwise_xor(y, c), c)
        src_ref, sent_to, got_at = land[w].at[blk], blk, _slot(*peer)
    elif mode == "gather_pass":
        blk = _slot(*peer)
        to, src_ref, sent_to, got_at = _peer(1, *place), land[w].at[blk], blk, _slot(*_peer(k | 1, *place))
    elif mode == "scatter_sibling":
        to, src_ref, sent_to, got_at = peer, src[w].at[2 * i + 1 - place[2]], i, i
    else:
        to, src_ref, sent_to, got_at = peer, src[w].at[_slot(*peer) // 2], me // 2, _slot(*peer) // 2
    sem = w * len(bits) + i
    return pltpu.make_async_remote_copy(
        src_ref=src_ref, dst_ref=land[w].at[got_at if receiving else sent_to], send_sem=send_sems.at[sem],
        recv_sem=recv_sems.at[sem], device_id=to, device_id_type=MESH)


def _exchange_start(mode, srcs, lands, after, *, name):
    ns, nl = len(srcs), len(lands)
    n_sem = nl * len(_EXCHANGE_BITS[mode])

    def body(*refs):
        src, land = refs[:ns], refs[ns:ns + nl]
        send_sems, recv_sems = refs[ns + nl + 1:ns + nl + 3]
        token = refs[-1]
        place = _place()
        for w in range(nl):
            for i in range(len(_EXCHANGE_BITS[mode])):
                _exchange_copy(mode, src, land, w, i, place, send_sems, recv_sems, receiving=False).start()
        token[...] = jnp.zeros_like(token)

    sems = pltpu.SemaphoreType.DMA((n_sem,))
    arrays = list(srcs) + list(lands)
    res = pl.pallas_call(
        body, name=name, in_specs=[_HBM] * (ns + nl) + [_ANY],
        out_specs=(_SEM, _SEM, *([_HBM] * (ns + nl)), pl.BlockSpec(memory_space=pltpu.VMEM)),
        out_shape=(sems, sems, *[pltpu.HBM(a.shape, a.dtype) for a in arrays], jax.ShapeDtypeStruct((8, LANE), F32)),
        input_output_aliases={i: 2 + i for i in range(ns + nl)},
        compiler_params=pltpu.CompilerParams(has_side_effects=_EFFECT),
    )(*[pltpu.with_memory_space_constraint(a, pltpu.HBM) for a in arrays], after)
    return res[0], res[1], list(res[2:2 + ns]), list(res[2 + ns:2 + ns + nl]), res[-1]


def _exchange_wait(mode, started, after, *, name):
    send_sems, recv_sems, srcs, lands, _ = started
    ns, nl = len(srcs), len(lands)

    def body(*refs):
        src, land = refs[:ns], refs[ns:ns + nl]
        send_sems, recv_sems = refs[ns + nl:ns + nl + 2]
        place = _place()
        for w in range(nl):
            for i in range(len(_EXCHANGE_BITS[mode])):
                cp = _exchange_copy(mode, src, land, w, i, place, send_sems, recv_sems, receiving=True)
                cp.wait_send()
                cp.wait_recv()

    arrays = list(srcs) + list(lands)
    res = pl.pallas_call(
        body, name=name, in_specs=[_HBM] * (ns + nl) + [_SEM, _SEM, _ANY], out_specs=[_HBM] * (ns + nl),
        out_shape=[pltpu.HBM(a.shape, a.dtype) for a in arrays],
        input_output_aliases={i: i for i in range(ns + nl)},
        compiler_params=pltpu.CompilerParams(has_side_effects=_EFFECT),
    )(*arrays, send_sems, recv_sems, after)
    return list(res[:ns]), list(res[ns:])


_LN_ROWS = ("ln1_g", "ln1_b", "ln2_g", "ln2_b", "ln3_g", "ln3_b")
_MISC_ROW = len(_LN_ROWS)
_MISC = (("pool_scale", 0, GDN_WIDTH), ("gdn_norm_w", GDN_WIDTH, HEAD_DIM), ("a_log", GDN_WIDTH + LANE, GDN_HEADS),
         ("dt_bias", GDN_WIDTH + 2 * LANE, GDN_HEADS), ("loss", GDN_WIDTH + 3 * LANE, 1))
_CONV_ROW = _MISC_ROW + 1
_CONV_ROWS = CONV_K * QKV_COLS // D_MODEL
_SMALL_ROWS = 16


def _pack_small(vals):
    pieces, at = [], 0
    for n, off, size in _MISC:
        pieces.append(jnp.zeros((off - at,), F32))
        pieces.append(vals[n].reshape(size).astype(F32) if n in vals else jnp.zeros((size,), F32))
        at = off + size
    pieces.append(jnp.zeros((D_MODEL - at,), F32))
    conv = vals["conv_w"].reshape(-1) if "conv_w" in vals else jnp.zeros((_CONV_ROWS * D_MODEL,), F32)
    tail = jnp.zeros(((_SMALL_ROWS - _CONV_ROW - _CONV_ROWS) * D_MODEL,), F32)
    flat = jnp.concatenate([vals[n].reshape(D_MODEL) for n in _LN_ROWS] + pieces + [conv, tail])
    return flat.reshape(_SMALL_ROWS, D_MODEL)


def _adamw_small(zone, mine, me, w, m, v):
    short = [(n, off, size) for n, off, size in _MISC if n != "loss"]

    def body(me_ref, z_ref, mine_ref, w_ref, m_ref, v_ref, *rest):
        outs, (g_s, d_s, nm_s, nv_s) = rest[:-4], rest[-4:]
        g = None
        for s in range(N_DEV):
            part = jnp.where(me_ref[0] == s, mine_ref[...], z_ref[s])
            g = part if g is None else g + part
        g_s[...] = g
        d_s[...], nm_s[...], nv_s[...] = _adamw_math(w_ref[...], g, m_ref[...], v_ref[...])
        k = 0
        for src in (g_s, d_s, nm_s, nv_s):
            for r in range(len(_LN_ROWS)):
                outs[k][...] = src[r:r + 1, :]
                k += 1
            for _, off, size in short:
                outs[k][...] = src[_MISC_ROW:_MISC_ROW + 1, off:off + size]
                k += 1
        outs[k][...] = g_s[_CONV_ROW:_CONV_ROW + _CONV_ROWS, :]
        outs[k + 1][...] = g_s[_MISC_ROW:_MISC_ROW + 1, :]

    rows, d = mine.shape
    per_quantity = [jax.ShapeDtypeStruct((1, D_MODEL), F32)] * len(_LN_ROWS) + [
        jax.ShapeDtypeStruct((1, size), F32) for _, _, size in short]
    out_shape = per_quantity * 4 + [jax.ShapeDtypeStruct((_CONV_ROWS, d), F32), jax.ShapeDtypeStruct((1, d), F32)]
    whole = lambda a: pl.BlockSpec(a.shape, lambda i, me_ref: (0,) * len(a.shape))
    res = pl.pallas_call(
        body,
        grid_spec=pltpu.PrefetchScalarGridSpec(
            num_scalar_prefetch=1, grid=(1,), in_specs=[whole(a) for a in (zone, mine, w, m, v)],
            out_specs=[whole(s) for s in out_shape], scratch_shapes=[pltpu.VMEM((rows, d), F32)] * 4),
        out_shape=out_shape, compiler_params=_params("arbitrary"), name="adamw_small",
    )(me, zone, mine, w, m, v)
    names = list(_LN_ROWS) + [n for n, _, _ in short]
    n_each = len(names)
    quantities = [dict(zip(names, res[q * n_each:(q + 1) * n_each])) for q in range(4)]
    return quantities, res[-2], res[-1]


_WEIGHT_ORDER = ("w_in", "conv_w", "a_log", "dt_bias", "gdn_norm_w", "pool_w", "pool_scale", "w_out", "ln1_g", "ln1_b",
                 "xq_w", "xk_w", "xv_w", "xo_w", "ln2_g", "ln2_b", "w_up", "w_down", "ln3_g", "ln3_b")


def _shard2d(name, a):
    if name == "w_in":
        return a.T
    return a.reshape(-1, a.shape[-1]) if name == "pool_w" else a


def _update_view(name, a):
    return jnp.transpose(a, (2, 0, 1)) if name == "w_in" else _shard2d(name, a[0])


def _shard_result(name, r, shape):
    return jnp.transpose(r, (1, 2, 0)) if name == "w_in" else r.reshape(shape)


def _gathered_to_full(name, gth):
    if name in ("w_up", "w_in"):
        return gth
    if name == "conv_w":
        return jnp.transpose(gth, (1, 0, 2)).reshape(gth.shape[1], N_DEV * gth.shape[2])
    if name == "pool_w":
        g4 = gth.reshape(N_DEV, POOL_GROUPS, POOL_GROUP_DIM // N_DEV, POOL_GROUP_DIM)
        return jnp.transpose(g4, (1, 0, 2, 3)).reshape(POOL_GROUPS, POOL_GROUP_DIM, POOL_GROUP_DIM)
    return gth.reshape(N_DEV * gth.shape[1], gth.shape[2])


def _full_to_chunks(name, full):
    if name == "w_up":
        return full
    if name == "pool_w":
        g4 = full.reshape(POOL_GROUPS, N_DEV, POOL_GROUP_DIM // N_DEV, POOL_GROUP_DIM)
        return jnp.transpose(g4, (1, 0, 2, 3)).reshape(N_DEV, POOL_GROUPS * POOL_GROUP_DIM // N_DEV, POOL_GROUP_DIM)
    return full.reshape(N_DEV, full.shape[0] // N_DEV, full.shape[1])


_GATHER_GROUPS = (("mixer", ("w_in", "conv_w", "pool_w")), ("attn", ("w_out", "xq_w", "xk_w", "xv_w", "xo_w")),
                  ("up", ("w_up",)), ("down", ("w_down",)))


def _grad_chunks(name, g):
    if name == "w_in":
        return _w_in_chunks(g.astype(BF16))
    return _full_to_chunks(name, g.astype(BF16))


def kernel(x, mem, w_in, conv_w, a_log, dt_bias, gdn_norm_w, pool_w, pool_scale, w_out, ln1_g, ln1_b, xq_w, xk_w, xv_w, xo_w, ln2_g, ln2_b, w_up, w_down, ln3_g, ln3_b, loss_target, m_w_in, m_conv_w, m_a_log, m_dt_bias, m_gdn_norm_w, m_pool_w, m_pool_scale, m_w_out, m_ln1_g, m_ln1_b, m_xq_w, m_xk_w, m_xv_w, m_xo_w, m_ln2_g, m_ln2_b, m_w_up, m_w_down, m_ln3_g, m_ln3_b, v_w_in, v_conv_w, v_a_log, v_dt_bias, v_gdn_norm_w, v_pool_w, v_pool_scale, v_w_out, v_ln1_g, v_ln1_b, v_xq_w, v_xk_w, v_xv_w, v_xo_w, v_ln2_g, v_ln2_b, v_w_up, v_w_down, v_ln3_g, v_ln3_b):
    args = dict(locals())
    wt = {n: args[n][0] for n in _WEIGHT_ORDER}
    mo = {n: args["m_" + n][0] for n in _WEIGHT_ORDER}
    vo = {n: args["v_" + n][0] for n in _WEIGHT_ORDER}

    me = _slot(*_place())
    me_arr = jnp.reshape(me, (1,)).astype(jnp.int32)
    nothing = jnp.zeros((8, LANE), F32)

    def landing_zones(names):
        shards = [_shard2d(n, wt[n]).astype(F32 if n == "conv_w" else BF16) for n in names]
        zones = [lax.dynamic_update_slice(lax.empty((N_DEV, *s.shape), s.dtype), s[None], (me, 0, 0)) for s in shards]
        return shards, zones

    chip_arr = jnp.reshape(me // 2, (1,)).astype(jnp.int32)
    core_arr = jnp.reshape(lax.axis_index("c"), (1,)).astype(jnp.int32)
    names_of = dict(_GATHER_GROUPS)
    gathers = {}
    prepared = {}

    def gather_near(group, after):
        shards, zones = prepared.pop(group) if group in prepared else landing_zones(names_of[group])
        gathers[group] = _exchange_start("gather_near", shards, zones, after, name="gather_near_" + group)
        return gathers[group][4]

    def gather_next(group, was, now, after):
        _, zones = _exchange_wait(was, gathers[group], after, name=f"{was}_{group}_wait")
        gathers[group] = _exchange_start(now, [], zones, nothing, name=f"{now}_{group}")
        return gathers[group][4]

    def gather_relay(group, after):
        return gather_next(group, "gather_near", "gather_relay", after)

    def gather_pass(group, after):
        return gather_next(group, "gather_relay", "gather_pass", after)

    def gathered(group, after):
        _, zones = _exchange_wait("gather_pass", gathers[group], after, name=f"gather_pass_{group}_wait")
        full = {n: _gathered_to_full(n, z) for n, z in zip(names_of[group], zones)}
        full.update({n: wt[n] for n in _VECTORS})
        return _group_weights(group, full)

    token = gather_near("mixer", nothing)
    x16 = _cast_bf16(x[0], name="cast_x")
    later = {group: landing_zones(names_of[group]) for group in ("attn", "up", "down")}
    token, x16, later = lax.optimization_barrier((token, x16, later))
    prepared.update(later)
    token = gather_pass("mixer", gather_relay("mixer", token))
    token = gather_near("attn", token)

    def weights_of(group, after):
        if group == "mixer":
            return gathered(group, token)
        if group == "ahead_conv":
            return gather_near("up", gather_relay("attn", after))[0:1, 0:1]
        if group == "ahead_scan":
            return gather_pass("attn", after)[0:1, 0:1]
        if group == "attn":
            return gathered(group, gather_near("down", gather_relay("up", after)))
        if group == "ahead_attn":
            return gather_relay("down", gather_pass("up", after))[0:1, 0:1]
        if group == "up":
            return gathered(group, gather_pass("down", after))
        return gathered(group, after)

    scatters = {}
    in_flight = []

    def chip_stage(after):
        group, names, started = in_flight.pop()
        chunks, from_sibling = _exchange_wait("scatter_sibling", started, after, name=f"scatter_sibling_{group}_wait")
        sums = [_chip_sums(c, f, core_arr, name=f"chip_sums_{n}") for n, c, f in zip(names, chunks, from_sibling)]
        scatters[group] = (names, _exchange_start("scatter_chips", sums, [lax.empty(s.shape, s.dtype) for s in sums],
                                                  nothing, name="scatter_chips_" + group))
        return scatters[group][1][4]

    small_sent = []

    def grads_ready(group, grads):
        if group == "tick":
            return chip_stage(grads["after"])[0:1, 0:1] if in_flight else None
        if group == "small":
            small = _finish_small_grads(grads)
            small["loss"] = 0.5 * grads["sq"][0:1, 0] / D_MODEL
            packed = _pack_small(small)
            zone = lax.empty((N_DEV, *packed.shape), F32)
            small_sent.append(_exchange_start("all_small", [packed], [zone], nothing, name="small_grads_start"))
            return small_sent[0][4][0:1, 0:1]
        names = tuple(grads)
        chunks = [_grad_chunks(n, grads[n]) for n in names]
        token = chip_stage(chunks[0]) if in_flight else nothing
        zones = [lax.empty((N_CHIPS, *c.shape[1:]), c.dtype) for c in chunks]
        started = _exchange_start("scatter_sibling", chunks, zones, token, name="scatter_sibling_" + group)
        in_flight.append((group, names, started))
        if group != "in":
            return started[4][0:1, 0:1]
        return chip_stage(update_group("mlp", started[4]))[0:1, 0:1]

    out = {}

    def update_group(group, after):
        names, started = scatters.pop(group)
        sums, lands = _exchange_wait("scatter_chips", started, after, name=f"scatter_chips_{group}_wait")
        for n, parts, own in zip(names, lands, sums):
            res = _adamw_shard(parts, own, chip_arr, _update_view(n, args[n]), _update_view(n, args["m_" + n]),
                               _update_view(n, args["v_" + n]), name="adamw_" + n)
            out[n] = [_shard_result(n, r, args[n].shape) for r in res]
            after = res[1]
        return after

    sq, grad_x, g = _local_step(x[0], x16, mem[0], loss_target[0], weights_of, grads_ready)

    after = grad_x
    for group in list(scatters):
        after = update_group(group, after)

    (packed,), (zone,) = _exchange_wait("all_small", small_sent[0], after, name="small_grads_wait")
    quantities, conv_rows, misc_row = _adamw_small(
        zone, packed, me_arr, _pack_small({n: wt[n] for n in _VECTORS}), _pack_small({n: mo[n] for n in _VECTORS}),
        _pack_small({n: vo[n] for n in _VECTORS}))
    cols = conv_w.shape[-1]
    conv_mine = lax.dynamic_slice(conv_rows.reshape(CONV_K, QKV_COLS), (0, me * cols), (CONV_K, cols))[None]
    res = _adamw_shard(conv_mine, conv_mine, jnp.zeros((1,), jnp.int32), wt["conv_w"], mo["conv_w"], vo["conv_w"],
                       name="adamw_conv_w")
    out["conv_w"] = [r.reshape(conv_w.shape) for r in res]
    for n in _VECTORS:
        out[n] = [q[n] for q in quantities]
    loss_at = dict((n, off) for n, off, _ in _MISC)["loss"]

    return (misc_row[0, loss_at], grad_x[None], *[out[n][0] for n in _WEIGHT_ORDER], *[out[n][1] for n in _WEIGHT_ORDER],
            *[out[n][2] for n in _WEIGHT_ORDER], *[out[n][3] for n in _WEIGHT_ORDER])
```

```python
import jax
import jax.numpy as jnp
from jax import lax
from jax.experimental import pallas as pl
from jax.experimental.pallas import tpu as pltpu

F32 = jnp.float32
BF16 = jnp.bfloat16
MESH = pl.DeviceIdType.MESH

N_DEV = 8
D_MODEL = 2048
GDN_WIDTH = 1024
GDN_HEADS = 8
HEAD_DIM = 128
CONV_K = 4
CHUNK = 64
POOL_GROUPS = 4
POOL_GROUP_DIM = 256
MEM_LEN = 256
XATTN_HEADS = 4
XATTN_HEAD_DIM = 512
D_FF = 8192
IN_COLS = 5136
ALPHA = 2.0 ** 0.25
LN_EPS = 1e-5
NORM_EPS = 1e-6

LANE = 128
QKV_COLS = 3 * GDN_WIDTH
Z_OFF = QKV_COLS
BA_OFF = 4 * GDN_WIDTH
POOL_OFF = BA_OFF + 2 * LANE
PROJ_COLS = POOL_OFF + GDN_WIDTH
BA_BLK = BA_OFF // LANE
POOL_BLK = POOL_OFF // POOL_GROUP_DIM

ADAM_LR = 0.001
ADAM_B1 = 0.9
ADAM_B2 = 0.999
ADAM_EPS = 1e-08
ADAM_WD = 0.01
ADAM_STEP = 10

VMEM_LIMIT_BYTES = 48 * 1024 * 1024


def _params(*sem):
    return pltpu.CompilerParams(dimension_semantics=sem if sem else None, vmem_limit_bytes=VMEM_LIMIT_BYTES)


def _make_dots(cast, precision, batched=False):
    lead = 1 if batched else 0
    batch = ((0,), (0,)) if batched else ((), ())

    def dg(a, b, ca, cb):
        if cast is not None:
            a = a.astype(cast)
            b = b.astype(cast)
        return lax.dot_general(a, b, (((ca + lead,), (cb + lead,)), batch), precision=precision, preferred_element_type=F32)

    def nn_(a, b):
        return dg(a, b, 1, 0)

    def nt_(a, b):
        return dg(a, b, 1, 1)

    def tn_(a, b):
        return dg(a, b, 0, 0)

    @jax.custom_vjp
    def nn(a, b):
        return nn_(a, b)

    nn.defvjp(lambda a, b: (nn_(a, b), (a, b)), lambda r, g: (nt_(g, r[1]), tn_(r[0], g)))

    @jax.custom_vjp
    def nt(a, b):
        return nt_(a, b)

    nt.defvjp(lambda a, b: (nt_(a, b), (a, b)), lambda r, g: (nn_(g, r[1]), tn_(g, r[0])))

    @jax.custom_vjp
    def tn(a, b):
        return tn_(a, b)

    tn.defvjp(lambda a, b: (tn_(a, b), (a, b)), lambda r, g: (nt_(r[1], g), nn_(r[0], g)))

    return (nn_, nt_, tn_), (nn, nt, tn)


_BDOT_PLAIN, _BDOT_VJP = _make_dots(BF16, None)
_BDOT_BATCH_PLAIN, _BDOT_BATCH_VJP = _make_dots(BF16, None, batched=True)
_FDOT_BATCH_PLAIN, _FDOT_BATCH_VJP = _make_dots(BF16, None, batched=True)


def _mm(a, b, *, ta=False, tb=False, out_dtype=F32, tm=None, tn=512, tk=None, epi=None, extra=None, add_scale=1.0,
        b_chunks=False, o_chunks=False, after=None, name):
    m, k = (a.shape[1], a.shape[0]) if ta else a.shape
    if b_chunks:
        n, kb = (b.shape[1], N_DEV * b.shape[2]) if tb else (N_DEV * b.shape[2], b.shape[1])
    else:
        n, kb = b.shape if tb else (b.shape[1], b.shape[0])
    assert kb == k, (name, a.shape, b.shape)
    tm, tn, tk = min(tm or m, m), min(tn, n), min(tk or k, k)
    assert m % tm == 0 and n % tn == 0 and k % tk == 0, (name, m, n, k)
    nk = k // tk
    dims = (((0 if ta else 1,), (1 if tb else 0,)), ((), ()))
    n_extra = 0 if epi in (None, "relu2") else 1
    n_out = 2 if epi == "relu2" else 1
    if epi in ("relu2", "mul2r"):
        out_dtype = BF16
    n_after = 0 if after is None else 1

    def body(*refs):
        a_ref, b_ref = refs[:2]
        c_ref = refs[2] if n_extra else None
        o_refs = refs[2 + n_extra + n_after:2 + n_extra + n_after + n_out]
        scr = refs[2 + n_extra + n_after + n_out:]
        r = lax.dot_general(a_ref[...].astype(BF16), b_ref[...].astype(BF16), dims, preferred_element_type=F32)

        def finish(v):
            if epi == "add":
                o_refs[0][...] = (v + add_scale * c_ref[...]).astype(out_dtype)
            elif epi == "relu2":
                p = jnp.maximum(v, 0.0)
                o_refs[0][...] = (p * p).astype(BF16)
                o_refs[1][...] = p.astype(BF16)
            elif epi == "mul2r":
                o_refs[0][...] = (v * (2.0 * c_ref[...].astype(F32))).astype(BF16)
            else:
                o_refs[0][...] = v.astype(out_dtype)

        if nk == 1:
            finish(r)
        else:
            acc = scr[0]
            kk = pl.program_id(2)

            @pl.when(kk == 0)
            def _():
                acc[...] = r

            @pl.when(kk > 0)
            def _():
                acc[...] += r

            @pl.when(kk == nk - 1)
            def _():
                finish(acc[...])

    a_spec = pl.BlockSpec((tk, tm), lambda i, j, kk: (kk, i)) if ta else pl.BlockSpec((tm, tk), lambda i, j, kk: (i, kk))
    if b_chunks and tb:
        kc = k // N_DEV // tk
        b_spec = pl.BlockSpec((None, tn, tk), lambda i, j, kk: (kk // kc, j, kk % kc))
    elif b_chunks:
        nc = n // N_DEV // tn
        b_spec = pl.BlockSpec((None, tk, tn), lambda i, j, kk: (j // nc, kk, j % nc))
    elif tb:
        b_spec = pl.BlockSpec((tn, tk), lambda i, j, kk: (j, kk))
    else:
        b_spec = pl.BlockSpec((tk, tn), lambda i, j, kk: (kk, j))
    mn_spec = pl.BlockSpec((tm, tn), lambda i, j, kk: (i, j))
    if o_chunks:
        oc = n // N_DEV // tn
        o_spec = pl.BlockSpec((None, tm, tn), lambda i, j, kk: (j // oc, i, j % oc))
        o_shape = jax.ShapeDtypeStruct((N_DEV, m, n // N_DEV), out_dtype)
    else:
        o_spec, o_shape = mn_spec, jax.ShapeDtypeStruct((m, n), out_dtype)
    res = pl.pallas_call(
        body, grid=(m // tm, n // tn, nk),
        in_specs=[a_spec, b_spec] + [mn_spec] * n_extra + [pl.BlockSpec(memory_space=pl.ANY)] * n_after,
        out_specs=[o_spec] * n_out, out_shape=[o_shape] * n_out,
        scratch_shapes=[pltpu.VMEM((tm, tn), F32)] if nk > 1 else [],
        compiler_params=_params("parallel", "parallel", "arbitrary"), name=name,
    )(a, b, *([extra] if n_extra else []), *([after] if n_after else []))
    return res if n_out > 1 else res[0]


def _cast_bf16(v, *, name, tm=512):
    t, d = v.shape
    tm = min(tm, t)

    def body(v_ref, o_ref):
        o_ref[...] = v_ref[...].astype(BF16)

    spec = pl.BlockSpec((tm, d), lambda i: (i, 0))
    return pl.pallas_call(body, grid=(t // tm,), in_specs=[spec], out_specs=spec,
                          out_shape=jax.ShapeDtypeStruct((t, d), BF16), compiler_params=_params("parallel"), name=name)(v)


def _shift_down(v, s):
    if s == 0:
        return v
    row = lax.broadcasted_iota(jnp.int32, v.shape, 0)
    return jnp.where(row >= s, pltpu.roll(v, s, axis=0), 0.0)


def _shift_up(v, s):
    if s == 0:
        return v
    t = v.shape[0]
    row = lax.broadcasted_iota(jnp.int32, v.shape, 0)
    return jnp.where(row < t - s, pltpu.roll(v, t - s, axis=0), 0.0)


def _post_col(j):
    return (j % GDN_HEADS) * 3 + j // GDN_HEADS


def _gdn_prep_fwd(proj, conv_w):
    t = proj.shape[0]

    def body(x_ref, w_ref, o_ref):
        j = pl.program_id(0)
        x = x_ref[...]
        y = jnp.zeros_like(x)
        for tap in range(CONV_K):
            y = y + w_ref[tap:tap + 1, :] * _shift_down(x, CONV_K - 1 - tap)
        c = y * jax.nn.sigmoid(y)
        nrm = c * lax.rsqrt(jnp.sum(c * c, axis=1, keepdims=True) + NORM_EPS)
        o_ref[...] = jnp.where(j < 2 * GDN_HEADS, nrm, c)

    return pl.pallas_call(
        body, grid=(QKV_COLS // LANE,),
        in_specs=[pl.BlockSpec((t, LANE), lambda j: (0, j)), pl.BlockSpec((CONV_K, LANE), lambda j: (0, j))],
        out_specs=pl.BlockSpec((t, LANE), lambda j: (0, _post_col(j))),
        out_shape=jax.ShapeDtypeStruct((t, QKV_COLS), F32),
        compiler_params=_params("parallel"), name="gdn_prep_fwd",
    )(proj, conv_w)


def _gdn_prep_bwd(proj, conv_w, dpost, dproj):
    t = proj.shape[0]

    def body(x_ref, w_ref, d_ref, _, dx_ref, dw_ref):
        j = pl.program_id(0)
        x = x_ref[...]
        xs = [_shift_down(x, CONV_K - 1 - tap) for tap in range(CONV_K)]
        y = jnp.zeros_like(x)
        for tap in range(CONV_K):
            y = y + w_ref[tap:tap + 1, :] * xs[tap]
        sig = jax.nn.sigmoid(y)
        c = y * sig
        r = lax.rsqrt(jnp.sum(c * c, axis=1, keepdims=True) + NORM_EPS)
        nrm = c * r
        d = d_ref[...]
        dc_norm = r * (d - nrm * jnp.sum(d * nrm, axis=1, keepdims=True))
        dc = jnp.where(j < 2 * GDN_HEADS, dc_norm, d)
        dy = dc * (sig * (1.0 + y * (1.0 - sig)))
        dx = jnp.zeros_like(x)
        for tap in range(CONV_K):
            dx = dx + _shift_up(w_ref[tap:tap + 1, :] * dy, CONV_K - 1 - tap)
            dw_ref[tap:tap + 1, :] = jnp.sum(dy * xs[tap], axis=0, keepdims=True)
        dx_ref[...] = dx.astype(dx_ref.dtype)

    return pl.pallas_call(
        body, grid=(QKV_COLS // LANE,),
        in_specs=[pl.BlockSpec((t, LANE), lambda j: (0, j)), pl.BlockSpec((CONV_K, LANE), lambda j: (0, j)),
                  pl.BlockSpec((t, LANE), lambda j: (0, _post_col(j))), pl.BlockSpec(memory_space=pl.ANY)],
        out_specs=[pl.BlockSpec((t, LANE), lambda j: (0, j)), pl.BlockSpec((CONV_K, LANE), lambda j: (0, j))],
        out_shape=[jax.ShapeDtypeStruct(dproj.shape, dproj.dtype), jax.ShapeDtypeStruct((CONV_K, QKV_COLS), F32)],
        input_output_aliases={3: 0},
        compiler_params=_params("parallel"), name="gdn_prep_bwd",
    )(proj, conv_w, dpost, dproj)


def _softplus(v):
    return jnp.maximum(v, 0.0) + jnp.log(1.0 + jnp.exp(-jnp.abs(v)))


def _tri_inv(low, nn):
    r = lax.broadcasted_iota(jnp.int32, (CHUNK, CHUNK), 0)
    c = lax.broadcasted_iota(jnp.int32, (CHUNK, CHUNK), 1)
    eye = (r == c).astype(F32)
    same_blk = lax.shift_right_logical(r, 4) == lax.shift_right_logical(c, 4)
    diag = jnp.where(same_blk, low, 0.0)
    off = low - diag
    n1 = -diag
    n2 = nn(n1, n1)
    n4 = nn(n2, n2)
    n8 = nn(n4, n4)
    inv_d = nn(nn(nn(eye + n1, eye + n2), eye + n4), eye + n8)
    m1 = nn(inv_d, off)
    m2 = nn(m1, m1)
    return nn(nn(eye - m1, eye + m2), inv_d)


@jax.custom_vjp
def _tri_inv_known(low, t_inv):
    return t_inv


def _tri_inv_known_fwd(low, t_inv):
    return t_inv, t_inv


def _tri_inv_known_bwd(t_inv, g):
    _, nt, tn = _FDOT_BATCH_PLAIN
    return -nt(tn(t_inv, g), t_inv), jnp.zeros_like(t_inv)


_tri_inv_known.defvjp(_tri_inv_known_fwd, _tri_inv_known_bwd)


LOCAL_HEADS_PER_STEP = 8


def _gdn_local_fn(qkv, ba, alog_row, dtb_row, first_head, bdots, fdots, t_known=None):
    nn, nt, tn = bdots
    fnn = fdots[0]
    n_heads = qkv.shape[1] // (3 * HEAD_DIM)
    part = lambda i, p: qkv[:, (3 * i + p) * HEAD_DIM:(3 * i + p + 1) * HEAD_DIM]
    q = jnp.stack([part(i, 0) for i in range(n_heads)]) * (HEAD_DIM ** -0.5)
    k = jnp.stack([part(i, 1) for i in range(n_heads)])
    v = jnp.stack([part(i, 2) for i in range(n_heads)])
    lane = lax.broadcasted_iota(jnp.int32, ba.shape, 1)
    bg = jnp.where(lane < GDN_HEADS, jax.nn.sigmoid(ba), -jnp.exp(alog_row) * _softplus(ba + dtb_row))
    pick = lambda l: jnp.sum(jnp.where(lane == l, bg, 0.0), axis=1, keepdims=True)
    beta = jnp.stack([pick(first_head + i) for i in range(n_heads)])
    g = jnp.stack([pick(first_head + i + GDN_HEADS) for i in range(n_heads)])

    r = lax.broadcasted_iota(jnp.int32, (CHUNK, CHUNK), 0)
    c = lax.broadcasted_iota(jnp.int32, (CHUNK, CHUNK), 1)
    incl = r >= c
    strict = r > c
    eye = r == c

    def to_row(col):
        return jnp.sum(jnp.where(eye, col, 0.0), axis=1, keepdims=True)

    gc = jnp.sum(jnp.where(incl, to_row(g), 0.0), axis=2, keepdims=True)
    diff = gc - to_row(gc)
    decay = jnp.where(incl, jnp.exp(jnp.where(incl, diff, 0.0)), 0.0)
    k_beta = k * beta
    v_beta = v * beta
    low = jnp.where(strict, nt(k_beta, k) * decay, 0.0)
    t_inv = _tri_inv(low, fnn) if t_known is None else _tri_inv_known(low, t_known)
    eg = jnp.exp(gc)
    u = fnn(t_inv, v_beta)
    w = fnn(t_inv, k_beta * eg)
    attn = jnp.where(incl, nt(q, k) * decay, 0.0)
    last = lax.broadcasted_iota(jnp.int32, (CHUNK, 1), 0) == CHUNK - 1
    g_last = jnp.sum(jnp.where(last, gc, 0.0), axis=1, keepdims=True)
    kdec = k * jnp.exp(g_last - gc)
    elast = jnp.broadcast_to(jnp.exp(g_last), (n_heads, 1, LANE))
    return u, w, q * eg, kdec, attn, elast, t_inv


def _gdn_state_fn(u, w, qg, kdec, attn, elast, state, bdots):
    nn, _, tn = bdots
    v_new = u - nn(w, state)
    o = nn(qg, state) + nn(attn, v_new)
    return o, state * elast + tn(kdec, v_new)


def _gdn_local_fwd(post, proj, alog_row, dtb_row):
    t = post.shape[0]
    n_chunks = t // CHUNK
    hb = LOCAL_HEADS_PER_STEP

    def body(qkv_ref, ba_ref, al_ref, dt_ref, u_ref, w_ref, qg_ref, kd_ref, at_ref, el_ref, ti_ref):
        u, w, qg, kdec, attn, elast, t_inv = _gdn_local_fn(qkv_ref[...], ba_ref[...], al_ref[...], dt_ref[...],
                                                           pl.program_id(1) * hb, _BDOT_BATCH_PLAIN, _FDOT_BATCH_PLAIN)
        for i in range(hb):
            cols = slice(i * HEAD_DIM, (i + 1) * HEAD_DIM)
            u_ref[:, cols] = u[i]
            w_ref[:, cols] = w[i].astype(BF16)
            qg_ref[:, cols] = qg[i].astype(BF16)
            kd_ref[:, cols] = kdec[i].astype(BF16)
        at_ref[...] = attn.astype(BF16)
        el_ref[:, 0] = elast
        ti_ref[...] = t_inv

    wide = pl.BlockSpec((CHUNK, hb * HEAD_DIM), lambda n, j: (n, j))
    square = pl.BlockSpec((hb, CHUNK, CHUNK), lambda n, j: (j, n, 0))
    row = pl.BlockSpec((1, LANE), lambda n, j: (0, 0))
    res = pl.pallas_call(
        body, grid=(n_chunks, GDN_HEADS // hb),
        in_specs=[pl.BlockSpec((CHUNK, hb * 3 * HEAD_DIM), lambda n, j: (n, j)),
                  pl.BlockSpec((CHUNK, LANE), lambda n, j: (n, BA_BLK)), row, row],
        out_specs=[wide, wide, wide, wide, square, pl.BlockSpec((hb, 1, 1, LANE), lambda n, j: (j, n, 0, 0)), square],
        out_shape=[jax.ShapeDtypeStruct((t, GDN_WIDTH), F32), jax.ShapeDtypeStruct((t, GDN_WIDTH), BF16),
                   jax.ShapeDtypeStruct((t, GDN_WIDTH), BF16), jax.ShapeDtypeStruct((t, GDN_WIDTH), BF16),
                   jax.ShapeDtypeStruct((GDN_HEADS, t, CHUNK), BF16),
                   jax.ShapeDtypeStruct((GDN_HEADS, n_chunks, 1, LANE), F32),
                   jax.ShapeDtypeStruct((GDN_HEADS, t, CHUNK), F32)],
        compiler_params=_params("parallel", "parallel"), name="gdn_local_fwd",
    )(post, proj, alog_row, dtb_row)
    return tuple(res[:6]), res[6]


def _by_head(ref):
    return jnp.stack([ref[:, h * HEAD_DIM:(h + 1) * HEAD_DIM] for h in range(ref.shape[1] // HEAD_DIM)])


def _gdn_state_specs(n_of):
    wide = pl.BlockSpec((CHUNK, GDN_WIDTH), lambda n: (n_of(n), 0))
    attn = pl.BlockSpec((GDN_HEADS, CHUNK, CHUNK), lambda n: (0, n_of(n), 0))
    elast = pl.BlockSpec((GDN_HEADS, 1, 1, LANE), lambda n: (0, n_of(n), 0, 0))
    saved = pl.BlockSpec((GDN_HEADS, 1, HEAD_DIM, HEAD_DIM), lambda n: (0, n_of(n), 0, 0))
    return wide, attn, elast, saved


def _gdn_state_fwd(u, w, qg, kdec, attn, elast):
    t = u.shape[0]
    n_chunks = t // CHUNK

    def body(u_ref, w_ref, qg_ref, kd_ref, at_ref, el_ref, o_ref, save_ref, state_ref):
        @pl.when(pl.program_id(0) == 0)
        def _():
            state_ref[...] = jnp.zeros_like(state_ref)

        state = state_ref[...]
        save_ref[:, 0] = state
        o, new_state = _gdn_state_fn(_by_head(u_ref), _by_head(w_ref), _by_head(qg_ref), _by_head(kd_ref), at_ref[...],
                                     el_ref[:, 0], state, _BDOT_BATCH_PLAIN)
        for h in range(GDN_HEADS):
            o_ref[:, h * HEAD_DIM:(h + 1) * HEAD_DIM] = o[h]
        state_ref[...] = new_state

    wide, attn_spec, elast_spec, saved_spec = _gdn_state_specs(lambda n: n)
    return pl.pallas_call(
        body, grid=(n_chunks,), in_specs=[wide, wide, wide, wide, attn_spec, elast_spec],
        out_specs=[wide, saved_spec],
        out_shape=[jax.ShapeDtypeStruct((t, GDN_WIDTH), F32),
                   jax.ShapeDtypeStruct((GDN_HEADS, n_chunks, HEAD_DIM, HEAD_DIM), F32)],
        scratch_shapes=[pltpu.VMEM((GDN_HEADS, HEAD_DIM, HEAD_DIM), F32)],
        compiler_params=_params("arbitrary"), name="gdn_state_fwd",
    )(u, w, qg, kdec, attn, elast)


def _gdn_state_bwd(u, w, qg, kdec, attn, elast, saved, do):
    t = u.shape[0]
    n_chunks = t // CHUNK
    last = n_chunks - 1

    def body(u_ref, w_ref, qg_ref, kd_ref, at_ref, el_ref, save_ref, do_ref,
             du_ref, dw_ref, dqg_ref, dkd_ref, dat_ref, del_ref, dstate_ref):
        @pl.when(pl.program_id(0) == 0)
        def _():
            dstate_ref[...] = jnp.zeros_like(dstate_ref)

        _, vjp = jax.vjp(
            lambda *a: _gdn_state_fn(*a, _BDOT_BATCH_VJP), _by_head(u_ref), _by_head(w_ref).astype(F32),
            _by_head(qg_ref).astype(F32), _by_head(kd_ref).astype(F32), at_ref[...].astype(F32), el_ref[:, 0],
            save_ref[:, 0])
        du, dw, dqg, dkd, dat, de, dstate = vjp((_by_head(do_ref), dstate_ref[...]))
        for h in range(GDN_HEADS):
            cols = slice(h * HEAD_DIM, (h + 1) * HEAD_DIM)
            du_ref[:, cols] = du[h]
            dw_ref[:, cols] = dw[h]
            dqg_ref[:, cols] = dqg[h]
            dkd_ref[:, cols] = dkd[h]
        dat_ref[...] = dat
        del_ref[:, 0] = de
        dstate_ref[...] = dstate

    wide, attn_spec, elast_spec, saved_spec = _gdn_state_specs(lambda n: last - n)
    wide_f32 = jax.ShapeDtypeStruct((t, GDN_WIDTH), F32)
    return pl.pallas_call(
        body, grid=(n_chunks,), in_specs=[wide, wide, wide, wide, attn_spec, elast_spec, saved_spec, wide],
        out_specs=[wide, wide, wide, wide, attn_spec, elast_spec],
        out_shape=[wide_f32, wide_f32, wide_f32, wide_f32, jax.ShapeDtypeStruct((GDN_HEADS, t, CHUNK), F32),
                   jax.ShapeDtypeStruct((GDN_HEADS, n_chunks, 1, LANE), F32)],
        scratch_shapes=[pltpu.VMEM((GDN_HEADS, HEAD_DIM, HEAD_DIM), F32)],
        compiler_params=_params("arbitrary"), name="gdn_state_bwd",
    )(u, w, qg, kdec, attn, elast, saved, do)


def _gdn_local_bwd(post, proj, alog_row, dtb_row, t_inv, cots, dproj):
    t = post.shape[0]
    n_chunks = t // CHUNK
    hb = LOCAL_HEADS_PER_STEP
    n_steps = GDN_HEADS // hb

    def body(qkv_ref, ba_ref, al_ref, dt_ref, ti_ref, du_ref, dw_ref, dqg_ref, dkd_ref, dat_ref, del_ref, _,
             dqkv_ref, dba_ref, dal_ref, ddt_ref, dba_acc):
        n = pl.program_id(0)
        j = pl.program_id(1)

        @pl.when((n == 0) & (j == 0))
        def _():
            dal_ref[...] = jnp.zeros_like(dal_ref)
            ddt_ref[...] = jnp.zeros_like(ddt_ref)

        @pl.when(j == 0)
        def _():
            dba_acc[...] = jnp.zeros_like(dba_acc)

        t_known = ti_ref[...]
        _, vjp = jax.vjp(
            lambda a, b, c, d: _gdn_local_fn(a, b, c, d, j * hb, _BDOT_BATCH_VJP, _FDOT_BATCH_VJP, t_known)[:6],
            qkv_ref[...], ba_ref[...], al_ref[...], dt_ref[...])
        dqkv, dba, dal, ddt = vjp((_by_head(du_ref), _by_head(dw_ref), _by_head(dqg_ref), _by_head(dkd_ref), dat_ref[...],
                                   del_ref[:, 0]))
        dqkv_ref[...] = dqkv
        dba_acc[...] += dba
        dal_ref[...] += dal
        ddt_ref[...] += ddt

        @pl.when(j == n_steps - 1)
        def _():
            dba_ref[:, 0:LANE] = dba_acc[...].astype(dba_ref.dtype)
            dba_ref[:, LANE:2 * LANE] = jnp.zeros((CHUNK, LANE), dba_ref.dtype)

    wide = pl.BlockSpec((CHUNK, hb * HEAD_DIM), lambda n, j: (n, j))
    qkv_spec = pl.BlockSpec((CHUNK, hb * 3 * HEAD_DIM), lambda n, j: (n, j))
    row = pl.BlockSpec((1, LANE), lambda n, j: (0, 0))
    return pl.pallas_call(
        body, grid=(n_chunks, n_steps),
        in_specs=[qkv_spec, pl.BlockSpec((CHUNK, LANE), lambda n, j: (n, BA_BLK)), row, row,
                  pl.BlockSpec((hb, CHUNK, CHUNK), lambda n, j: (j, n, 0)), wide, wide, wide, wide,
                  pl.BlockSpec((hb, CHUNK, CHUNK), lambda n, j: (j, n, 0)),
                  pl.BlockSpec((hb, 1, 1, LANE), lambda n, j: (j, n, 0, 0)), pl.BlockSpec(memory_space=pl.ANY)],
        out_specs=[qkv_spec, pl.BlockSpec((CHUNK, 2 * LANE), lambda n, j: (n, BA_BLK // 2)), row, row],
        out_shape=[jax.ShapeDtypeStruct((t, QKV_COLS), F32), jax.ShapeDtypeStruct(dproj.shape, dproj.dtype),
                   jax.ShapeDtypeStruct((1, LANE), F32), jax.ShapeDtypeStruct((1, LANE), F32)],
        input_output_aliases={11: 1},
        scratch_shapes=[pltpu.VMEM((CHUNK, LANE), F32)],
        compiler_params=_params("arbitrary", "arbitrary"), name="gdn_local_bwd",
    )(post, proj, alog_row, dtb_row, t_inv, *cots, dproj)


def _onorm_fn(o, z, w):
    return o * lax.rsqrt(jnp.mean(o * o, axis=1, keepdims=True) + NORM_EPS) * w * (z * jax.nn.sigmoid(z))


_Z_WIDE_BLK = Z_OFF // GDN_WIDTH


def _onorm_fwd(o_raw, proj, norm_w, mixin, tm=256):
    t = o_raw.shape[0]
    tm = min(tm, t)

    def body(o_ref, z_ref, w_ref, _, out_ref):
        for h in range(GDN_HEADS):
            cols = slice(h * HEAD_DIM, (h + 1) * HEAD_DIM)
            out_ref[:, cols] = _onorm_fn(o_ref[:, cols], z_ref[:, cols], w_ref[...]).astype(out_ref.dtype)

    wide = pl.BlockSpec((tm, GDN_WIDTH), lambda i: (i, 0))
    return pl.pallas_call(
        body, grid=(t // tm,),
        in_specs=[wide, pl.BlockSpec((tm, GDN_WIDTH), lambda i: (i, _Z_WIDE_BLK)), pl.BlockSpec((1, LANE), lambda i: (0, 0)),
                  pl.BlockSpec(memory_space=pl.ANY)],
        out_specs=wide, out_shape=jax.ShapeDtypeStruct(mixin.shape, mixin.dtype), input_output_aliases={3: 0},
        compiler_params=_params("parallel"), name="gdn_onorm_fwd",
    )(o_raw, proj, norm_w, mixin)


def _onorm_bwd(o_raw, proj, norm_w, dmixin, dproj, tm=256):
    t = o_raw.shape[0]
    tm = min(tm, t)

    def body(o_ref, z_ref, w_ref, d_ref, _, do_ref, dz_ref, dw_ref):
        @pl.when(pl.program_id(0) == 0)
        def _():
            dw_ref[...] = jnp.zeros_like(dw_ref)

        for h in range(GDN_HEADS):
            cols = slice(h * HEAD_DIM, (h + 1) * HEAD_DIM)
            _, vjp = jax.vjp(_onorm_fn, o_ref[:, cols], z_ref[:, cols], w_ref[...])
            do, dz, dw = vjp(d_ref[:, cols])
            do_ref[:, cols] = do
            dz_ref[:, cols] = dz.astype(dz_ref.dtype)
            dw_ref[...] += dw

    wide = pl.BlockSpec((tm, GDN_WIDTH), lambda i: (i, 0))
    gate = pl.BlockSpec((tm, GDN_WIDTH), lambda i: (i, _Z_WIDE_BLK))
    row = pl.BlockSpec((1, LANE), lambda i: (0, 0))
    return pl.pallas_call(
        body, grid=(t // tm,), in_specs=[wide, gate, row, wide, pl.BlockSpec(memory_space=pl.ANY)],
        out_specs=[wide, gate, row],
        out_shape=[jax.ShapeDtypeStruct((t, GDN_WIDTH), F32), jax.ShapeDtypeStruct(dproj.shape, dproj.dtype),
                   jax.ShapeDtypeStruct((1, LANE), F32)],
        input_output_aliases={4: 1},
        compiler_params=_params("arbitrary"), name="gdn_onorm_bwd",
    )(o_raw, proj, norm_w, dmixin, dproj)


def _pool_select(levels, gi):
    out = levels[-1]
    for lvl in range(len(levels) - 2, -1, -1):
        out = jnp.where(gi == lvl, levels[lvl], out)
    return out


def _pool_count(shape, gi):
    pos = lax.broadcasted_iota(jnp.int32, shape, 0)
    win = lax.shift_left(jnp.int32(2), gi)
    return jnp.minimum(pos + 1, win).astype(F32)


def _pooled(p, gi):
    acc = p
    levels = []
    for lvl in range(POOL_GROUPS):
        acc = acc + _shift_down(acc, 1 << lvl)
        levels.append(acc)
    return _pool_select(levels, gi) / _pool_count(p.shape, gi) - p


def _pool_fwd(proj, pool_w, pool_scale):
    t = proj.shape[0]

    def body(p_ref, w_ref, s_ref, out_ref):
        gi = pl.program_id(0)
        pooled = _pooled(p_ref[...], gi)
        out_ref[...] = (_BDOT_PLAIN[0](pooled, w_ref[0]) * s_ref[0]).astype(out_ref.dtype)

    return pl.pallas_call(
        body, grid=(POOL_GROUPS,),
        in_specs=[pl.BlockSpec((t, POOL_GROUP_DIM), lambda g: (0, POOL_BLK + g)),
                  pl.BlockSpec((1, POOL_GROUP_DIM, POOL_GROUP_DIM), lambda g: (g, 0, 0)),
                  pl.BlockSpec((1, 1, POOL_GROUP_DIM), lambda g: (g, 0, 0))],
        out_specs=pl.BlockSpec((t, POOL_GROUP_DIM), lambda g: (0, GDN_WIDTH // POOL_GROUP_DIM + g)),
        out_shape=jax.ShapeDtypeStruct((t, 2 * GDN_WIDTH), BF16),
        compiler_params=_params("parallel"), name="pool_fwd",
    )(proj, pool_w, pool_scale)


def _pool_bwd(proj, pool_w, pool_scale, dmixin):
    t = proj.shape[0]
    nn, nt, tn = _BDOT_PLAIN

    def body(p_ref, w_ref, s_ref, d_ref, dp_ref, dw_ref, ds_ref):
        gi = pl.program_id(0)
        p = p_ref[...]
        pooled = _pooled(p, gi)
        mixed = nn(pooled, w_ref[0])
        d = d_ref[...]
        ds_ref[0] = jnp.sum(d * mixed, axis=0, keepdims=True)
        dmixed = d * s_ref[0]
        dw_ref[0] = tn(pooled, dmixed)
        dpooled = nt(dmixed, w_ref[0])
        acc = dpooled / _pool_count(p.shape, gi)
        levels = []
        for lvl in range(POOL_GROUPS):
            acc = acc + _shift_up(acc, 1 << lvl)
            levels.append(acc)
        dp_ref[...] = (_pool_select(levels, gi) - dpooled).astype(dp_ref.dtype)

    return pl.pallas_call(
        body, grid=(POOL_GROUPS,),
        in_specs=[pl.BlockSpec((t, POOL_GROUP_DIM), lambda g: (0, POOL_BLK + g)),
                  pl.BlockSpec((1, POOL_GROUP_DIM, POOL_GROUP_DIM), lambda g: (g, 0, 0)),
                  pl.BlockSpec((1, 1, POOL_GROUP_DIM), lambda g: (g, 0, 0)),
                  pl.BlockSpec((t, POOL_GROUP_DIM), lambda g: (0, GDN_WIDTH // POOL_GROUP_DIM + g))],
        out_specs=[pl.BlockSpec((t, POOL_GROUP_DIM), lambda g: (0, POOL_BLK + g)),
                   pl.BlockSpec((1, POOL_GROUP_DIM, POOL_GROUP_DIM), lambda g: (g, 0, 0)),
                   pl.BlockSpec((1, 1, POOL_GROUP_DIM), lambda g: (g, 0, 0))],
        out_shape=[jax.ShapeDtypeStruct((t, PROJ_COLS), BF16),
                   jax.ShapeDtypeStruct((POOL_GROUPS, POOL_GROUP_DIM, POOL_GROUP_DIM), F32),
                   jax.ShapeDtypeStruct((POOL_GROUPS, 1, POOL_GROUP_DIM), F32)],
        compiler_params=_params("parallel"), name="pool_bwd",
    )(proj, pool_w, pool_scale, dmixin)


def _ln_stats(s):
    mu = jnp.mean(s, axis=1, keepdims=True)
    xc = s - mu
    var = jnp.mean(xc * xc, axis=1, keepdims=True)
    rstd = lax.rsqrt(var + LN_EPS)
    return xc * rstd, rstd


def _mm_ln(a, b, h_in, g, bias, *, name, tm=256):
    t, k = a.shape
    d = b.shape[1]
    tm = min(tm, t)

    def body(a_ref, b_ref, h_ref, g_ref, bias_ref, y_ref, o_ref, o16_ref):
        y = jnp.dot(a_ref[...].astype(BF16), b_ref[...], preferred_element_type=F32)
        y_ref[...] = y
        xhat, _ = _ln_stats(ALPHA * h_ref[...] + y)
        out = xhat * g_ref[...] + bias_ref[...]
        o_ref[...] = out
        o16_ref[...] = out.astype(BF16)

    row = pl.BlockSpec((tm, d), lambda i: (i, 0))
    vec = pl.BlockSpec((1, d), lambda i: (0, 0))
    return pl.pallas_call(
        body, grid=(t // tm,),
        in_specs=[pl.BlockSpec((tm, k), lambda i: (i, 0)), pl.BlockSpec((k, d), lambda i: (0, 0)), row, vec, vec],
        out_specs=[row, row, row],
        out_shape=[jax.ShapeDtypeStruct((t, d), F32), jax.ShapeDtypeStruct((t, d), F32), jax.ShapeDtypeStruct((t, d), BF16)],
        compiler_params=_params("parallel"), name=name,
    )(a, b, h_in, g, bias)


def _ln_backward(xhat, rstd, dout, gain):
    dxhat = dout * gain
    m1 = jnp.mean(dxhat, axis=1, keepdims=True)
    m2 = jnp.mean(dxhat * xhat, axis=1, keepdims=True)
    return (rstd * (dxhat - m1 - xhat * m2), jnp.sum(dout * xhat, axis=0, keepdims=True),
            jnp.sum(dout, axis=0, keepdims=True))


def _ln_loss(h_in, y, g, b, target, *, name, tm=256):
    t, d = h_in.shape
    tm = min(tm, t)

    def body(h_ref, y_ref, g_ref, b_ref, t_ref, sq_ref, ds_ref, ds16_ref, dg_ref, dbias_ref):
        @pl.when(pl.program_id(0) == 0)
        def _():
            sq_ref[...] = jnp.zeros_like(sq_ref)
            dg_ref[...] = jnp.zeros_like(dg_ref)
            dbias_ref[...] = jnp.zeros_like(dbias_ref)

        xhat, rstd = _ln_stats(ALPHA * h_ref[...] + y_ref[...])
        err = xhat * g_ref[...] + b_ref[...] - t_ref[...]
        sq_ref[...] += jnp.sum(jnp.sum(err * err, axis=1, keepdims=True), axis=0, keepdims=True)
        ds, dg, dbias = _ln_backward(xhat, rstd, err * (1.0 / d), g_ref[...])
        ds_ref[...] = ds
        ds16_ref[...] = ds.astype(BF16)
        dg_ref[...] += dg
        dbias_ref[...] += dbias

    row = pl.BlockSpec((tm, d), lambda i: (i, 0))
    vec = pl.BlockSpec((1, d), lambda i: (0, 0))
    return pl.pallas_call(
        body, grid=(t // tm,), in_specs=[row, row, vec, vec, row],
        out_specs=[pl.BlockSpec((1, LANE), lambda i: (0, 0)), row, row, vec, vec],
        out_shape=[jax.ShapeDtypeStruct((1, LANE), F32), jax.ShapeDtypeStruct((t, d), F32),
                   jax.ShapeDtypeStruct((t, d), BF16), jax.ShapeDtypeStruct((1, d), F32), jax.ShapeDtypeStruct((1, d), F32)],
        compiler_params=_params("arbitrary"), name=name,
    )(h_in, y, g, b, target)


def _ln_bwd(h_in, y, g, d_a, d_b, *, name, tm=256):
    t, d = h_in.shape
    tm = min(tm, t)
    has_b = d_b is not None

    def body(*refs):
        if has_b:
            h_ref, y_ref, g_ref, da_ref, db_ref, ds_ref, ds16_ref, dg_ref, dbias_ref = refs
        else:
            h_ref, y_ref, g_ref, da_ref, ds_ref, ds16_ref, dg_ref, dbias_ref = refs

        @pl.when(pl.program_id(0) == 0)
        def _():
            dg_ref[...] = jnp.zeros_like(dg_ref)
            dbias_ref[...] = jnp.zeros_like(dbias_ref)

        xhat, rstd = _ln_stats(ALPHA * h_ref[...] + y_ref[...])
        dout = da_ref[...]
        if has_b:
            dout = dout + ALPHA * db_ref[...]
        ds, dg, dbias = _ln_backward(xhat, rstd, dout, g_ref[...])
        ds_ref[...] = ds
        ds16_ref[...] = ds.astype(BF16)
        dg_ref[...] += dg
        dbias_ref[...] += dbias

    row = pl.BlockSpec((tm, d), lambda i: (i, 0))
    vec = pl.BlockSpec((1, d), lambda i: (0, 0))
    args = [h_in, y, g, d_a] + ([d_b] if has_b else [])
    return pl.pallas_call(
        body, grid=(t // tm,), in_specs=[row, row, vec, row] + ([row] if has_b else []),
        out_specs=[row, row, vec, vec],
        out_shape=[jax.ShapeDtypeStruct((t, d), F32), jax.ShapeDtypeStruct((t, d), BF16),
                   jax.ShapeDtypeStruct((1, d), F32), jax.ShapeDtypeStruct((1, d), F32)],
        compiler_params=_params("arbitrary"), name=name,
    )(*args)


def _attn_fn(q, k, v, dots):
    nn, nt, _ = dots
    s = nt(q, k) * (XATTN_HEAD_DIM ** -0.5)
    s = s - lax.stop_gradient(jnp.max(s, axis=1, keepdims=True))
    e = jnp.exp(s)
    p = e / jnp.sum(e, axis=1, keepdims=True)
    return nn(p, v)


def _attn_fwd(q, k, v, tq=2048):
    t = q.shape[0]
    tq = min(tq, t)

    def body(q_ref, k_ref, v_ref, o_ref):
        o_ref[...] = _attn_fn(q_ref[...], k_ref[...], v_ref[...], _BDOT_PLAIN).astype(BF16)

    qs = pl.BlockSpec((tq, XATTN_HEAD_DIM), lambda h, i: (i, h))
    ks = pl.BlockSpec((MEM_LEN, XATTN_HEAD_DIM), lambda h, i: (0, h))
    return pl.pallas_call(
        body, grid=(XATTN_HEADS, t // tq), in_specs=[qs, ks, ks], out_specs=qs,
        out_shape=jax.ShapeDtypeStruct(q.shape, BF16), compiler_params=_params("parallel", "parallel"), name="xattn_fwd",
    )(q, k, v)


def _attn_bwd(q, k, v, do, tq=1024):
    t = q.shape[0]
    tq = min(tq, t)

    def body(q_ref, k_ref, v_ref, do_ref, dq_ref, dk_ref, dv_ref):
        @pl.when(pl.program_id(1) == 0)
        def _():
            dk_ref[...] = jnp.zeros_like(dk_ref)
            dv_ref[...] = jnp.zeros_like(dv_ref)

        _, vjp = jax.vjp(lambda a, b, c: _attn_fn(a, b, c, _BDOT_VJP), q_ref[...].astype(F32), k_ref[...].astype(F32),
                         v_ref[...].astype(F32))
        dq, dk, dv = vjp(do_ref[...].astype(F32))
        dq_ref[...] = dq.astype(BF16)
        dk_ref[...] += dk
        dv_ref[...] += dv

    qs = pl.BlockSpec((tq, XATTN_HEAD_DIM), lambda h, i: (i, h))
    ks = pl.BlockSpec((MEM_LEN, XATTN_HEAD_DIM), lambda h, i: (0, h))
    return pl.pallas_call(
        body, grid=(XATTN_HEADS, t // tq), in_specs=[qs, ks, ks, qs], out_specs=[qs, ks, ks],
        out_shape=[jax.ShapeDtypeStruct(q.shape, BF16), jax.ShapeDtypeStruct(k.shape, F32), jax.ShapeDtypeStruct(v.shape, F32)],
        compiler_params=_params("parallel", "arbitrary"), name="xattn_bwd",
    )(q, k, v, do)


def _local_step(x, x16, mem, target, weights_of, grads_ready):
    def behind(vec, token):
        return vec if token is None else vec + token

    w = dict(weights_of("mixer", None))
    proj = _mm(x16, w["w_in"], tb=True, tn=768, name="mm_in_proj")
    mixin = _pool_fwd(proj, w["pool_w"], w["pool_scale"])
    post = _gdn_prep_fwd(proj, w["conv_w"])
    token = weights_of("ahead_conv", post)
    chunked, t_inv = _gdn_local_fwd(post, proj, behind(w["alog_row"], token), w["dtb_row"])
    o_raw, saved = _gdn_state_fwd(*chunked)
    token = weights_of("ahead_scan", o_raw)
    mixin = _onorm_fwd(o_raw, proj, behind(w["gdn_norm_w"], token), mixin)
    w.update(weights_of("attn", mixin))
    mix, h1, h1_16 = _mm_ln(mixin, w["w_out"], x, w["ln1_g"], w["ln1_b"], name="mm_out_proj_ln1")
    xq = _mm(h1_16, w["xq_w"], out_dtype=BF16, name="mm_xq")
    xk = _mm(mem, w["xk_w"], out_dtype=BF16, name="mm_xk")
    xv = _mm(mem, w["xv_w"], out_dtype=BF16, name="mm_xv")
    xo = _attn_fwd(xq, xk, xv)
    token = weights_of("ahead_attn", xo)
    if token is not None:
        xo, _ = lax.optimization_barrier((xo, token))
    xa, h2, h2_16 = _mm_ln(xo, w["xo_w"], h1, w["ln2_g"], w["ln2_b"], name="mm_xo_ln2")
    w.update(weights_of("up", h2_16))
    act, relu = _mm(h2_16, w["w_up"], b_chunks=True, epi="relu2", name="mm_up")
    w.update(weights_of("down", act))
    ff = _mm(act, w["w_down"], tn=512, tk=2048, name="mm_down")
    g = {}
    sq, ds3, ds3_16, g["ln3_g"], g["ln3_b"] = _ln_loss(h2, ff, w["ln3_g"], w["ln3_b"], target, name="ln3_loss")

    gw_down = _mm(act, ds3_16, ta=True, out_dtype=BF16, tm=512, tn=D_MODEL, name="mm_gw_down")
    du = _mm(ds3_16, w["w_down"], tb=True, epi="mul2r", extra=relu, name="mm_du")
    gw_up = _mm(h2_16, du, ta=True, out_dtype=BF16, o_chunks=True, name="mm_gw_up")
    token = grads_ready("mlp", {"w_down": gw_down, "w_up": gw_up})
    dh2 = _mm(du, w["w_up"], tb=True, b_chunks=True, tn=1024, tk=1024, name="mm_dh2")
    ds2, ds2_16, g["ln2_g"], g["ln2_b"] = _ln_bwd(h1, xa, behind(w["ln2_g"], token), dh2, ds3, name="ln2_bwd")
    gw_xo = _mm(xo, ds2_16, ta=True, out_dtype=BF16, name="mm_gw_xo")
    dxo = _mm(ds2_16, w["xo_w"], tb=True, out_dtype=BF16, name="mm_dxo")
    dxq, dxk, dxv = _attn_bwd(xq, xk, xv, dxo)
    gw_xq = _mm(h1_16, dxq, ta=True, out_dtype=BF16, name="mm_gw_xq")
    gw_xk = _mm(mem, dxk, ta=True, out_dtype=BF16, name="mm_gw_xk")
    gw_xv = _mm(mem, dxv, ta=True, out_dtype=BF16, name="mm_gw_xv")
    token = grads_ready("attn", {"xo_w": gw_xo, "xq_w": gw_xq, "xk_w": gw_xk, "xv_w": gw_xv})
    dh1 = _mm(dxq, w["xq_w"], tb=True, name="mm_dh1")
    ds1, ds1_16, g["ln1_g"], g["ln1_b"] = _ln_bwd(x, mix, behind(w["ln1_g"], token), dh1, ds2, name="ln1_bwd")
    gw_out = _mm(mixin, ds1_16, ta=True, out_dtype=BF16, name="mm_gw_out")
    dmixin = _mm(ds1_16, w["w_out"], tb=True, name="mm_dmixin")
    dproj, gw_pool, g["pool_scale"] = _pool_bwd(proj, w["pool_w"], w["pool_scale"], dmixin)
    token = grads_ready("mix", {"w_out": gw_out, "pool_w": gw_pool})
    do_raw, dproj, g["gdn_norm_w"] = _onorm_bwd(o_raw, proj, behind(w["gdn_norm_w"], token), dmixin, dproj)
    cots = _gdn_state_bwd(*chunked, saved, do_raw)
    token = grads_ready("tick", {"after": cots[0]})
    dpost, dproj, g["alog_row"], g["dtb_row"] = _gdn_local_bwd(post, proj, behind(w["alog_row"], token), w["dtb_row"],
                                                               t_inv, cots, dproj)
    dproj, g["conv_w"] = _gdn_prep_bwd(proj, w["conv_w"], dpost, dproj)
    token = grads_ready("small", {**g, "sq": sq})
    gw_in = _mm(dproj, x16, ta=True, out_dtype=BF16, tm=768, tn=D_MODEL, after=token, name="mm_gw_in")
    token = grads_ready("in", {"w_in": gw_in})
    grad_x = _mm(dproj, w["w_in"], tk=1792, epi="add", extra=ds1, add_scale=ALPHA, after=token, name="mm_dx")
    return sq, grad_x, g


_VECTORS = ("a_log", "dt_bias", "gdn_norm_w", "pool_scale", "ln1_g", "ln1_b", "ln2_g", "ln2_b", "ln3_g", "ln3_b")
_BA_SPLIT = BA_OFF + 2 * GDN_HEADS


def _lane_row(v, offset):
    return jnp.zeros((1, LANE), F32).at[0, offset:offset + v.shape[0]].set(v)


_GROUP_VECTORS = {"mixer": (), "attn": ("ln1_g", "ln1_b", "ln2_g", "ln2_b"), "up": (), "down": ("ln3_g", "ln3_b")}


def _group_weights(group, full):
    w = {n: full[n].reshape(1, D_MODEL) for n in _GROUP_VECTORS[group]}
    if group == "mixer":
        w.update({
            "w_in": _w_in_padded(full["w_in"]),
            "conv_w": full["conv_w"],
            "alog_row": _lane_row(full["a_log"], GDN_HEADS),
            "dtb_row": _lane_row(full["dt_bias"], GDN_HEADS),
            "gdn_norm_w": full["gdn_norm_w"].reshape(1, LANE),
            "pool_w": full["pool_w"],
            "pool_scale": full["pool_scale"].reshape(POOL_GROUPS, 1, POOL_GROUP_DIM),
        })
    else:
        w.update({n: full[n] for n in dict(_GATHER_GROUPS)[group]})
    return w


def _w_in_row_map():
    per = IN_COLS // N_DEV
    gap = POOL_OFF - _BA_SPLIT
    pieces = []
    for d in range(N_DEV):
        lo, hi = d * per, (d + 1) * per
        if hi <= _BA_SPLIT:
            pieces.append([(0, lo, per)])
        elif lo >= _BA_SPLIT:
            pieces.append([(0, lo + gap, per)])
        else:
            pieces.append([(0, lo, _BA_SPLIT - lo), (_BA_SPLIT - lo, POOL_OFF, hi - _BA_SPLIT)])
    return pieces


_W_IN_LANES = 256


def _w_in_padded(blocks):
    def body(b_ref, o_ref):
        for d, pieces in enumerate(_w_in_row_map()):
            for src, dst, rows in pieces:
                o_ref[dst:dst + rows, :] = b_ref[d, src:src + rows, :]
        o_ref[_BA_SPLIT:POOL_OFF, :] = jnp.zeros((POOL_OFF - _BA_SPLIT, _W_IN_LANES), o_ref.dtype)

    n, per, cols = blocks.shape
    return pl.pallas_call(
        body, grid=(cols // _W_IN_LANES,), in_specs=[pl.BlockSpec((n, per, _W_IN_LANES), lambda j: (0, 0, j))],
        out_specs=pl.BlockSpec((PROJ_COLS, _W_IN_LANES), lambda j: (0, j)),
        out_shape=jax.ShapeDtypeStruct((PROJ_COLS, cols), blocks.dtype), compiler_params=_params("parallel"),
        name="w_in_padded")(blocks)


def _w_in_chunks(g):
    def body(g_ref, o_ref):
        for d, pieces in enumerate(_w_in_row_map()):
            for dst, src, rows in pieces:
                o_ref[d, dst:dst + rows, :] = g_ref[src:src + rows, :]

    cols = g.shape[1]
    per = IN_COLS // N_DEV
    return pl.pallas_call(
        body, grid=(cols // _W_IN_LANES,), in_specs=[pl.BlockSpec((PROJ_COLS, _W_IN_LANES), lambda j: (0, j))],
        out_specs=pl.BlockSpec((N_DEV, per, _W_IN_LANES), lambda j: (0, 0, j)),
        out_shape=jax.ShapeDtypeStruct((N_DEV, per, cols), g.dtype), compiler_params=_params("parallel"),
        name="w_in_chunks")(g)


def _finish_small_grads(g):
    out = {"conv_w": g["conv_w"]}
    out["a_log"] = g["alog_row"][0, GDN_HEADS:2 * GDN_HEADS]
    out["dt_bias"] = g["dtb_row"][0, GDN_HEADS:2 * GDN_HEADS]
    out["gdn_norm_w"] = g["gdn_norm_w"].reshape(LANE)
    out["pool_scale"] = g["pool_scale"].reshape(POOL_GROUPS * POOL_GROUP_DIM)
    for n in ("ln1_g", "ln1_b", "ln2_g", "ln2_b", "ln3_g", "ln3_b"):
        out[n] = g[n].reshape(D_MODEL)
    return out


def _adamw_math(w, g, m, v):
    m = ADAM_B1 * m + (1.0 - ADAM_B1) * g
    v = ADAM_B2 * v + (1.0 - ADAM_B2) * (g * g)
    m_hat = m / (1.0 - ADAM_B1 ** ADAM_STEP)
    v_hat = v / (1.0 - ADAM_B2 ** ADAM_STEP)
    delta = -ADAM_LR * (m_hat / (jnp.sqrt(v_hat) + ADAM_EPS) + ADAM_WD * w)
    return delta, m, v


ADAMW_TILE_ELEMS = 256 * 1024
CHIP_SUM_TILE_ELEMS = 1024 * 1024


def _shard_tile(r, c, elems):
    for rows in (1024, 512, 256, 128):
        if r % rows == 0 and rows * c <= elems:
            return rows, c
    if r % 128 == 0:
        return 128, c
    return r, 256 if c % 256 == 0 else c


def _adamw_shard(parts, own, me, w, m, v, *, name):
    s, r, c = parts.shape
    tr, tc = _shard_tile(r, c, ADAMW_TILE_ELEMS)
    assert r % tr == 0 and c % tc == 0, (name, r, c)
    unit_axis = w.ndim == 3
    at = (slice(None), 0, slice(None)) if unit_axis else Ellipsis

    def body(me_ref, p_ref, own_ref, w_ref, m_ref, v_ref, g_ref, d_ref, nm_ref, nv_ref):
        mine = own_ref[...].astype(F32)
        g = None
        for i in range(s):
            part = jnp.where(me_ref[0] == i, mine, p_ref[i].astype(F32))
            g = part if g is None else g + part
        delta, nm, nv = _adamw_math(w_ref[at], g, m_ref[at], v_ref[at])
        g_ref[at] = g
        d_ref[at] = delta
        nm_ref[at] = nm
        nv_ref[at] = nv

    if unit_axis:
        blk = pl.BlockSpec((tr, 1, tc), lambda i, j, me_ref: (i, 0, j))
        out = jax.ShapeDtypeStruct((r, 1, c), F32)
    else:
        blk = pl.BlockSpec((tr, tc), lambda i, j, me_ref: (i, j))
        out = jax.ShapeDtypeStruct((r, c), F32)
    return pl.pallas_call(
        body,
        grid_spec=pltpu.PrefetchScalarGridSpec(
            num_scalar_prefetch=1, grid=(r // tr, c // tc),
            in_specs=[pl.BlockSpec((s, tr, tc), lambda i, j, me_ref: (0, i, j)),
                      pl.BlockSpec((None, tr, tc), lambda i, j, me_ref: (me_ref[0], i, j)), blk, blk, blk],
            out_specs=[blk, blk, blk, blk]),
        out_shape=[out, out, out, out], compiler_params=_params("parallel", "parallel"), name=name,
    )(me, parts, own, w, m, v)


N_CHIPS = N_DEV // 2


def _chip_sums(chunks, from_sibling, core, *, name):
    n = len(chunks)
    _, r, c = chunks[0].shape
    assert all(a.shape == chunks[0].shape for a in chunks), name
    tr, tc = _shard_tile(r, c, CHIP_SUM_TILE_ELEMS // n)
    assert r % tr == 0 and c % tc == 0, (name, r, c)

    def body(core_ref, *refs):
        for a in range(n):
            refs[2 * n + a][...] = (refs[a][...].astype(F32) + refs[n + a][...].astype(F32)).astype(BF16)

    by_chip = pl.BlockSpec((None, tr, tc), lambda q, i, j, core_ref: (q, i, j))
    mine = pl.BlockSpec((None, tr, tc), lambda q, i, j, core_ref: (2 * q + core_ref[0], i, j))
    return pl.pallas_call(
        body,
        grid_spec=pltpu.PrefetchScalarGridSpec(
            num_scalar_prefetch=1, grid=(N_CHIPS, r // tr, c // tc),
            in_specs=[mine] * n + [by_chip] * n, out_specs=[by_chip] * n),
        out_shape=[jax.ShapeDtypeStruct((N_CHIPS, r, c), chunks[0].dtype)] * n,
        compiler_params=_params("parallel", "parallel", "parallel"), name=name,
    )(core, *chunks, *from_sibling)


def _place():
    return lax.axis_index("x"), lax.axis_index("y"), lax.axis_index("c")


def _slot(px, py, pc):
    return 4 * px + 2 * py + pc


_HBM = pl.BlockSpec(memory_space=pltpu.HBM)


_SEM = pl.BlockSpec(memory_space=pltpu.SEMAPHORE)
_ANY = pl.BlockSpec(memory_space=pl.ANY)
_EFFECT = pltpu.SideEffectType.DATAFLOW_SIDE_EFFECTING


def _peer(k, x, y, c):
    return (1 - x if k & 4 else x, 1 - y if k & 2 else y, 1 - c if k & 1 else c)


_EXCHANGE_BITS = {"gather_near": (1, 2, 4), "gather_relay": (6,), "gather_pass": (2, 4, 6),
                  "scatter_sibling": (1, 1, 1, 1), "scatter_chips": (2, 4, 6), "all_small": (1, 2, 3, 4, 5, 6, 7)}


def _exchange_copy(mode, src, land, w, i, place, send_sems, recv_sems, receiving):
    bits = _EXCHANGE_BITS[mode]
    k = bits[i]
    peer = _peer(k, *place)
    me = _slot(*place)
    if mode in ("gather_near", "all_small"):
        to, src_ref, sent_to, got_at = peer, src[w], me, _slot(*peer)
    elif mode == "gather_relay":
        x, y, c = place
        other = 1 - c
        to = (lax.bitwise_xor(x, c), lax.bitwise_xor(y, other), c)
        blk = _slot(lax.bitwise_xor(x, other), lax.bitwise_xor(y, c), c)
        src_ref, sent_to, got_at = land[w].at[blk], blk, _slot(*peer)
    elif mode == "gather_pass":
        blk = _slot(*peer)
        to, src_ref, sent_to, got_at = _peer(1, *place), land[w].at[blk], blk, _slot(*_peer(k | 1, *place))
    elif mode == "scatter_sibling":
        to, src_ref, sent_to, got_at = peer, src[w].at[2 * i + 1 - place[2]], i, i
    else:
        to, src_ref, sent_to, got_at = peer, src[w].at[_slot(*peer) // 2], me // 2, _slot(*peer) // 2
    sem = w * len(bits) + i
    return pltpu.make_async_remote_copy(
        src_ref=src_ref, dst_ref=land[w].at[got_at if receiving else sent_to], send_sem=send_sems.at[sem],
        recv_sem=recv_sems.at[sem], device_id=to, device_id_type=MESH)


def _exchange_start(mode, srcs, lands, after, *, name):
    ns, nl = len(srcs), len(lands)
    n_sem = nl * len(_EXCHANGE_BITS[mode])

    def body(*refs):
        src, land = refs[:ns], refs[ns:ns + nl]
        send_sems, recv_sems = refs[ns + nl + 1:ns + nl + 3]
        token = refs[-1]
        place = _place()
        for w in range(nl):
            for i in range(len(_EXCHANGE_BITS[mode])):
                _exchange_copy(mode, src, land, w, i, place, send_sems, recv_sems, receiving=False).start()
        token[...] = jnp.zeros_like(token)

    sems = pltpu.SemaphoreType.DMA((n_sem,))
    arrays = list(srcs) + list(lands)
    res = pl.pallas_call(
        body, name=name, in_specs=[_HBM] * (ns + nl) + [_ANY],
        out_specs=(_SEM, _SEM, *([_HBM] * (ns + nl)), pl.BlockSpec(memory_space=pltpu.VMEM)),
        out_shape=(sems, sems, *[pltpu.HBM(a.shape, a.dtype) for a in arrays], jax.ShapeDtypeStruct((8, LANE), F32)),
        input_output_aliases={i: 2 + i for i in range(ns + nl)},
        compiler_params=pltpu.CompilerParams(has_side_effects=_EFFECT),
    )(*[pltpu.with_memory_space_constraint(a, pltpu.HBM) for a in arrays], after)
    return res[0], res[1], list(res[2:2 + ns]), list(res[2 + ns:2 + ns + nl]), res[-1]


def _exchange_wait(mode, started, after, *, name):
    send_sems, recv_sems, srcs, lands, _ = started
    ns, nl = len(srcs), len(lands)

    def body(*refs):
        src, land = refs[:ns], refs[ns:ns + nl]
        send_sems, recv_sems = refs[ns + nl:ns + nl + 2]
        place = _place()
        for w in range(nl):
            for i in range(len(_EXCHANGE_BITS[mode])):
                cp = _exchange_copy(mode, src, land, w, i, place, send_sems, recv_sems, receiving=True)
                cp.wait_send()
                cp.wait_recv()

    arrays = list(srcs) + list(lands)
    res = pl.pallas_call(
        body, name=name, in_specs=[_HBM] * (ns + nl) + [_SEM, _SEM, _ANY], out_specs=[_HBM] * (ns + nl),
        out_shape=[pltpu.HBM(a.shape, a.dtype) for a in arrays],
        input_output_aliases={i: i for i in range(ns + nl)},
        compiler_params=pltpu.CompilerParams(has_side_effects=_EFFECT),
    )(*arrays, send_sems, recv_sems, after)
    return list(res[:ns]), list(res[ns:])


_LN_ROWS = ("ln1_g", "ln1_b", "ln2_g", "ln2_b", "ln3_g", "ln3_b")
_MISC_ROW = len(_LN_ROWS)
_MISC = (("pool_scale", 0, GDN_WIDTH), ("gdn_norm_w", GDN_WIDTH, HEAD_DIM), ("a_log", GDN_WIDTH + LANE, GDN_HEADS),
         ("dt_bias", GDN_WIDTH + 2 * LANE, GDN_HEADS), ("loss", GDN_WIDTH + 3 * LANE, 1))
_CONV_ROW = _MISC_ROW + 1
_CONV_ROWS = CONV_K * QKV_COLS // D_MODEL
_SMALL_ROWS = 16


def _pack_small(vals):
    pieces, at = [], 0
    for n, off, size in _MISC:
        pieces.append(jnp.zeros((off - at,), F32))
        pieces.append(vals[n].reshape(size).astype(F32) if n in vals else jnp.zeros((size,), F32))
        at = off + size
    pieces.append(jnp.zeros((D_MODEL - at,), F32))
    conv = vals["conv_w"].reshape(-1) if "conv_w" in vals else jnp.zeros((_CONV_ROWS * D_MODEL,), F32)
    tail = jnp.zeros(((_SMALL_ROWS - _CONV_ROW - _CONV_ROWS) * D_MODEL,), F32)
    flat = jnp.concatenate([vals[n].reshape(D_MODEL) for n in _LN_ROWS] + pieces + [conv, tail])
    return flat.reshape(_SMALL_ROWS, D_MODEL)


def _adamw_small(zone, mine, me, w, m, v):
    short = [(n, off, size) for n, off, size in _MISC if n != "loss"]

    def body(me_ref, z_ref, mine_ref, w_ref, m_ref, v_ref, *rest):
        outs, (g_s, d_s, nm_s, nv_s) = rest[:-4], rest[-4:]
        g = None
        for s in range(N_DEV):
            part = jnp.where(me_ref[0] == s, mine_ref[...], z_ref[s])
            g = part if g is None else g + part
        g_s[...] = g
        d_s[...], nm_s[...], nv_s[...] = _adamw_math(w_ref[...], g, m_ref[...], v_ref[...])
        k = 0
        for src in (g_s, d_s, nm_s, nv_s):
            for r in range(len(_LN_ROWS)):
                outs[k][...] = src[r:r + 1, :]
                k += 1
            for _, off, size in short:
                outs[k][...] = src[_MISC_ROW:_MISC_ROW + 1, off:off + size]
                k += 1
        outs[k][...] = g_s[_CONV_ROW:_CONV_ROW + _CONV_ROWS, :]
        outs[k + 1][...] = g_s[_MISC_ROW:_MISC_ROW + 1, :]

    rows, d = mine.shape
    per_quantity = [jax.ShapeDtypeStruct((1, D_MODEL), F32)] * len(_LN_ROWS) + [
        jax.ShapeDtypeStruct((1, size), F32) for _, _, size in short]
    out_shape = per_quantity * 4 + [jax.ShapeDtypeStruct((_CONV_ROWS, d), F32), jax.ShapeDtypeStruct((1, d), F32)]
    whole = lambda a: pl.BlockSpec(a.shape, lambda i, me_ref: (0,) * len(a.shape))
    res = pl.pallas_call(
        body,
        grid_spec=pltpu.PrefetchScalarGridSpec(
            num_scalar_prefetch=1, grid=(1,), in_specs=[whole(a) for a in (zone, mine, w, m, v)],
            out_specs=[whole(s) for s in out_shape], scratch_shapes=[pltpu.VMEM((rows, d), F32)] * 4),
        out_shape=out_shape, compiler_params=_params("arbitrary"), name="adamw_small",
    )(me, zone, mine, w, m, v)
    names = list(_LN_ROWS) + [n for n, _, _ in short]
    n_each = len(names)
    quantities = [dict(zip(names, res[q * n_each:(q + 1) * n_each])) for q in range(4)]
    return quantities, res[-2], res[-1]


_WEIGHT_ORDER = ("w_in", "conv_w", "a_log", "dt_bias", "gdn_norm_w", "pool_w", "pool_scale", "w_out", "ln1_g", "ln1_b",
                 "xq_w", "xk_w", "xv_w", "xo_w", "ln2_g", "ln2_b", "w_up", "w_down", "ln3_g", "ln3_b")


def _shard2d(name, a):
    if name == "w_in":
        return a.T
    return a.reshape(-1, a.shape[-1]) if name == "pool_w" else a


def _update_view(name, a):
    return jnp.transpose(a, (2, 0, 1)) if name == "w_in" else _shard2d(name, a[0])


def _shard_result(name, r, shape):
    return jnp.transpose(r, (1, 2, 0)) if name == "w_in" else r.reshape(shape)


def _gathered_to_full(name, gth):
    if name in ("w_up", "w_in"):
        return gth
    if name == "conv_w":
        return jnp.transpose(gth, (1, 0, 2)).reshape(gth.shape[1], N_DEV * gth.shape[2])
    if name == "pool_w":
        g4 = gth.reshape(N_DEV, POOL_GROUPS, POOL_GROUP_DIM // N_DEV, POOL_GROUP_DIM)
        return jnp.transpose(g4, (1, 0, 2, 3)).reshape(POOL_GROUPS, POOL_GROUP_DIM, POOL_GROUP_DIM)
    return gth.reshape(N_DEV * gth.shape[1], gth.shape[2])


def _full_to_chunks(name, full):
    if name == "w_up":
        return full
    if name == "pool_w":
        g4 = full.reshape(POOL_GROUPS, N_DEV, POOL_GROUP_DIM // N_DEV, POOL_GROUP_DIM)
        return jnp.transpose(g4, (1, 0, 2, 3)).reshape(N_DEV, POOL_GROUPS * POOL_GROUP_DIM // N_DEV, POOL_GROUP_DIM)
    return full.reshape(N_DEV, full.shape[0] // N_DEV, full.shape[1])


_GATHER_GROUPS = (("mixer", ("w_in", "conv_w", "pool_w")), ("attn", ("w_out", "xq_w", "xk_w", "xv_w", "xo_w")),
                  ("up", ("w_up",)), ("down", ("w_down",)))


def _grad_chunks(name, g):
    if name == "w_in":
        return _w_in_chunks(g.astype(BF16))
    return _full_to_chunks(name, g.astype(BF16))


def kernel(x, mem, w_in, conv_w, a_log, dt_bias, gdn_norm_w, pool_w, pool_scale, w_out, ln1_g, ln1_b, xq_w, xk_w, xv_w, xo_w, ln2_g, ln2_b, w_up, w_down, ln3_g, ln3_b, loss_target, m_w_in, m_conv_w, m_a_log, m_dt_bias, m_gdn_norm_w, m_pool_w, m_pool_scale, m_w_out, m_ln1_g, m_ln1_b, m_xq_w, m_xk_w, m_xv_w, m_xo_w, m_ln2_g, m_ln2_b, m_w_up, m_w_down, m_ln3_g, m_ln3_b, v_w_in, v_conv_w, v_a_log, v_dt_bias, v_gdn_norm_w, v_pool_w, v_pool_scale, v_w_out, v_ln1_g, v_ln1_b, v_xq_w, v_xk_w, v_xv_w, v_xo_w, v_ln2_g, v_ln2_b, v_w_up, v_w_down, v_ln3_g, v_ln3_b):
    args = dict(locals())
    wt = {n: args[n][0] for n in _WEIGHT_ORDER}
    mo = {n: args["m_" + n][0] for n in _WEIGHT_ORDER}
    vo = {n: args["v_" + n][0] for n in _WEIGHT_ORDER}

    me = _slot(*_place())
    me_arr = jnp.reshape(me, (1,)).astype(jnp.int32)
    nothing = jnp.zeros((8, LANE), F32)

    def landing_zones(names):
        shards = [_shard2d(n, wt[n]).astype(F32 if n == "conv_w" else BF16) for n in names]
        zones = [lax.dynamic_update_slice(lax.empty((N_DEV, *s.shape), s.dtype), s[None], (me, 0, 0)) for s in shards]
        return shards, zones

    chip_arr = jnp.reshape(me // 2, (1,)).astype(jnp.int32)
    core_arr = jnp.reshape(lax.axis_index("c"), (1,)).astype(jnp.int32)
    names_of = dict(_GATHER_GROUPS)
    gathers = {}
    prepared = {}

    def gather_near(group, after):
        shards, zones = prepared.pop(group) if group in prepared else landing_zones(names_of[group])
        gathers[group] = _exchange_start("gather_near", shards, zones, after, name="gather_near_" + group)
        return gathers[group][4]

    def gather_next(group, was, now, after):
        _, zones = _exchange_wait(was, gathers[group], after, name=f"{was}_{group}_wait")
        gathers[group] = _exchange_start(now, [], zones, nothing, name=f"{now}_{group}")
        return gathers[group][4]

    def gather_relay(group, after):
        return gather_next(group, "gather_near", "gather_relay", after)

    def gather_pass(group, after):
        return gather_next(group, "gather_relay", "gather_pass", after)

    def gathered(group, after):
        _, zones = _exchange_wait("gather_pass", gathers[group], after, name=f"gather_pass_{group}_wait")
        full = {n: _gathered_to_full(n, z) for n, z in zip(names_of[group], zones)}
        full.update({n: wt[n] for n in _VECTORS})
        return _group_weights(group, full)

    token = gather_near("mixer", nothing)
    x16 = _cast_bf16(x[0], name="cast_x")
    later = {group: landing_zones(names_of[group]) for group in ("attn", "up", "down")}
    token, x16, later = lax.optimization_barrier((token, x16, later))
    prepared.update(later)
    token = gather_pass("mixer", gather_relay("mixer", token))
    token = gather_near("attn", token)

    def weights_of(group, after):
        if group == "mixer":
            return gathered(group, token)
        if group == "ahead_conv":
            return gather_near("up", gather_relay("attn", after))[0:1, 0:1]
        if group == "ahead_scan":
            return gather_pass("attn", after)[0:1, 0:1]
        if group == "attn":
            return gathered(group, gather_near("down", gather_relay("up", after)))
        if group == "ahead_attn":
            return gather_relay("down", gather_pass("up", after))[0:1, 0:1]
        if group == "up":
            return gathered(group, gather_pass("down", after))
        return gathered(group, after)

    scatters = {}
    in_flight = []

    def chip_stage(after):
        group, names, started = in_flight.pop()
        chunks, from_sibling = _exchange_wait("scatter_sibling", started, after, name=f"scatter_sibling_{group}_wait")
        sums, alike = [None] * len(names), {}
        for i, chunk in enumerate(chunks):
            alike.setdefault(chunk.shape, []).append(i)
        for same in alike.values():
            res = _chip_sums([chunks[i] for i in same], [from_sibling[i] for i in same], core_arr,
                             name="chip_sums_" + names[same[0]])
            for i, r in zip(same, res):
                sums[i] = r
        scatters[group] = (names, _exchange_start("scatter_chips", sums, [lax.empty(s.shape, s.dtype) for s in sums],
                                                  nothing, name="scatter_chips_" + group))
        return scatters[group][1][4]

    small_sent = []

    def grads_ready(group, grads):
        if group == "tick":
            return chip_stage(grads["after"])[0:1, 0:1] if in_flight else None
        if group == "small":
            small = _finish_small_grads(grads)
            small["loss"] = 0.5 * grads["sq"][0:1, 0] / D_MODEL
            packed = _pack_small(small)
            zone = lax.empty((N_DEV, *packed.shape), F32)
            small_sent.append(_exchange_start("all_small", [packed], [zone], nothing, name="small_grads_start"))
            return small_sent[0][4][0:1, 0:1]
        names = tuple(grads)
        chunks = [_grad_chunks(n, grads[n]) for n in names]
        token = chip_stage(chunks[0]) if in_flight else nothing
        zones = [lax.empty((N_CHIPS, *c.shape[1:]), c.dtype) for c in chunks]
        started = _exchange_start("scatter_sibling", chunks, zones, token, name="scatter_sibling_" + group)
        in_flight.append((group, names, started))
        if group != "in":
            return started[4][0:1, 0:1]
        return chip_stage(update_group("mlp", started[4]))[0:1, 0:1]

    out = {}

    def update_group(group, after):
        names, started = scatters.pop(group)
        sums, lands = _exchange_wait("scatter_chips", started, after, name=f"scatter_chips_{group}_wait")
        for n, parts, own in zip(names, lands, sums):
            res = _adamw_shard(parts, own, chip_arr, _update_view(n, args[n]), _update_view(n, args["m_" + n]),
                               _update_view(n, args["v_" + n]), name="adamw_" + n)
            out[n] = [_shard_result(n, r, args[n].shape) for r in res]
            after = res[1]
        return after

    sq, grad_x, g = _local_step(x[0], x16, mem[0], loss_target[0], weights_of, grads_ready)

    after = grad_x
    for group in list(scatters):
        after = update_group(group, after)

    (packed,), (zone,) = _exchange_wait("all_small", small_sent[0], after, name="small_grads_wait")
    quantities, conv_rows, misc_row = _adamw_small(
        zone, packed, me_arr, _pack_small({n: wt[n] for n in _VECTORS}), _pack_small({n: mo[n] for n in _VECTORS}),
        _pack_small({n: vo[n] for n in _VECTORS}))
    cols = conv_w.shape[-1]
    conv_mine = lax.dynamic_slice(conv_rows.reshape(CONV_K, QKV_COLS), (0, me * cols), (CONV_K, cols))[None]
    res = _adamw_shard(conv_mine, conv_mine, jnp.zeros((1,), jnp.int32), wt["conv_w"], mo["conv_w"], vo["conv_w"],
                       name="adamw_conv_w")
    out["conv_w"] = [r.reshape(conv_w.shape) for r in res]
    for n in _VECTORS:
        out[n] = [q[n] for q in quantities]
    loss_at = dict((n, off) for n, off, _ in _MISC)["loss"]

    return (misc_row[0, loss_at], grad_x[None], *[out[n][0] for n in _WEIGHT_ORDER], *[out[n][1] for n in _WEIGHT_ORDER],
            *[out[n][2] for n in _WEIGHT_ORDER], *[out[n][3] for n in _WEIGHT_ORDER])
```

```python
import jax
import jax.numpy as jnp
from jax import lax
from jax.experimental import pallas as pl
from jax.experimental.pallas import tpu as pltpu

F32 = jnp.float32
BF16 = jnp.bfloat16
MESH = pl.DeviceIdType.MESH

N_DEV = 8
D_MODEL = 2048
GDN_WIDTH = 1024
GDN_HEADS = 8
HEAD_DIM = 128
CONV_K = 4
CHUNK = 64
POOL_GROUPS = 4
POOL_GROUP_DIM = 256
MEM_LEN = 256
XATTN_HEADS = 4
XATTN_HEAD_DIM = 512
D_FF = 8192
IN_COLS = 5136
ALPHA = 2.0 ** 0.25
LN_EPS = 1e-5
NORM_EPS = 1e-6

LANE = 128
QKV_COLS = 3 * GDN_WIDTH
Z_OFF = QKV_COLS
BA_OFF = 4 * GDN_WIDTH
POOL_OFF = BA_OFF + 2 * LANE
PROJ_COLS = POOL_OFF + GDN_WIDTH
BA_BLK = BA_OFF // LANE
POOL_BLK = POOL_OFF // POOL_GROUP_DIM

ADAM_LR = 0.001
ADAM_B1 = 0.9
ADAM_B2 = 0.999
ADAM_EPS = 1e-08
ADAM_WD = 0.01
ADAM_STEP = 10

VMEM_LIMIT_BYTES = 48 * 1024 * 1024


def _params(*sem):
    return pltpu.CompilerParams(dimension_semantics=sem if sem else None, vmem_limit_bytes=VMEM_LIMIT_BYTES)


def _make_dots(cast, precision, batched=False):
    lead = 1 if batched else 0
    batch = ((0,), (0,)) if batched else ((), ())

    def dg(a, b, ca, cb):
        if cast is not None:
            a = a.astype(cast)
            b = b.astype(cast)
        return lax.dot_general(a, b, (((ca + lead,), (cb + lead,)), batch), precision=precision, preferred_element_type=F32)

    def nn_(a, b):
        return dg(a, b, 1, 0)

    def nt_(a, b):
        return dg(a, b, 1, 1)

    def tn_(a, b):
        return dg(a, b, 0, 0)

    @jax.custom_vjp
    def nn(a, b):
        return nn_(a, b)

    nn.defvjp(lambda a, b: (nn_(a, b), (a, b)), lambda r, g: (nt_(g, r[1]), tn_(r[0], g)))

    @jax.custom_vjp
    def nt(a, b):
        return nt_(a, b)

    nt.defvjp(lambda a, b: (nt_(a, b), (a, b)), lambda r, g: (nn_(g, r[1]), tn_(g, r[0])))

    @jax.custom_vjp
    def tn(a, b):
        return tn_(a, b)

    tn.defvjp(lambda a, b: (tn_(a, b), (a, b)), lambda r, g: (nt_(r[1], g), nn_(r[0], g)))

    return (nn_, nt_, tn_), (nn, nt, tn)


_BDOT_PLAIN, _BDOT_VJP = _make_dots(BF16, None)
_BDOT_BATCH_PLAIN, _BDOT_BATCH_VJP = _make_dots(BF16, None, batched=True)
_FDOT_BATCH_PLAIN, _FDOT_BATCH_VJP = _make_dots(BF16, None, batched=True)


def _mm(a, b, *, ta=False, tb=False, out_dtype=F32, tm=None, tn=512, tk=None, epi=None, extra=None, add_scale=1.0,
        b_chunks=False, o_chunks=False, after=None, name):
    m, k = (a.shape[1], a.shape[0]) if ta else a.shape
    if b_chunks:
        n, kb = (b.shape[1], N_DEV * b.shape[2]) if tb else (N_DEV * b.shape[2], b.shape[1])
    else:
        n, kb = b.shape if tb else (b.shape[1], b.shape[0])
    assert kb == k, (name, a.shape, b.shape)
    tm, tn, tk = min(tm or m, m), min(tn, n), min(tk or k, k)
    assert m % tm == 0 and n % tn == 0 and k % tk == 0, (name, m, n, k)
    nk = k // tk
    dims = (((0 if ta else 1,), (1 if tb else 0,)), ((), ()))
    n_extra = 0 if epi in (None, "relu2") else 1
    n_out = 2 if epi == "relu2" else 1
    if epi in ("relu2", "mul2r"):
        out_dtype = BF16
    n_after = 0 if after is None else 1

    def body(*refs):
        a_ref, b_ref = refs[:2]
        c_ref = refs[2] if n_extra else None
        o_refs = refs[2 + n_extra + n_after:2 + n_extra + n_after + n_out]
        scr = refs[2 + n_extra + n_after + n_out:]
        r = lax.dot_general(a_ref[...].astype(BF16), b_ref[...].astype(BF16), dims, preferred_element_type=F32)

        def finish(v):
            if epi == "add":
                o_refs[0][...] = (v + add_scale * c_ref[...]).astype(out_dtype)
            elif epi == "relu2":
                p = jnp.maximum(v, 0.0)
                o_refs[0][...] = (p * p).astype(BF16)
                o_refs[1][...] = p.astype(BF16)
            elif epi == "mul2r":
                o_refs[0][...] = (v * (2.0 * c_ref[...].astype(F32))).astype(BF16)
            else:
                o_refs[0][...] = v.astype(out_dtype)

        if nk == 1:
            finish(r)
        else:
            acc = scr[0]
            kk = pl.program_id(2)

            @pl.when(kk == 0)
            def _():
                acc[...] = r

            @pl.when(kk > 0)
            def _():
                acc[...] += r

            @pl.when(kk == nk - 1)
            def _():
                finish(acc[...])

    a_spec = pl.BlockSpec((tk, tm), lambda i, j, kk: (kk, i)) if ta else pl.BlockSpec((tm, tk), lambda i, j, kk: (i, kk))
    if b_chunks and tb:
        kc = k // N_DEV // tk
        b_spec = pl.BlockSpec((None, tn, tk), lambda i, j, kk: (kk // kc, j, kk % kc))
    elif b_chunks:
        nc = n // N_DEV // tn
        b_spec = pl.BlockSpec((None, tk, tn), lambda i, j, kk: (j // nc, kk, j % nc))
    elif tb:
        b_spec = pl.BlockSpec((tn, tk), lambda i, j, kk: (j, kk))
    else:
        b_spec = pl.BlockSpec((tk, tn), lambda i, j, kk: (kk, j))
    mn_spec = pl.BlockSpec((tm, tn), lambda i, j, kk: (i, j))
    if o_chunks:
        oc = n // N_DEV // tn
        o_spec = pl.BlockSpec((None, tm, tn), lambda i, j, kk: (j // oc, i, j % oc))
        o_shape = jax.ShapeDtypeStruct((N_DEV, m, n // N_DEV), out_dtype)
    else:
        o_spec, o_shape = mn_spec, jax.ShapeDtypeStruct((m, n), out_dtype)
    res = pl.pallas_call(
        body, grid=(m // tm, n // tn, nk),
        in_specs=[a_spec, b_spec] + [mn_spec] * n_extra + [pl.BlockSpec(memory_space=pl.ANY)] * n_after,
        out_specs=[o_spec] * n_out, out_shape=[o_shape] * n_out,
        scratch_shapes=[pltpu.VMEM((tm, tn), F32)] if nk > 1 else [],
        compiler_params=_params("parallel", "parallel", "arbitrary"), name=name,
    )(a, b, *([extra] if n_extra else []), *([after] if n_after else []))
    return res if n_out > 1 else res[0]


def _cast_bf16(v, *, name, tm=512):
    t, d = v.shape
    tm = min(tm, t)

    def body(v_ref, o_ref):
        o_ref[...] = v_ref[...].astype(BF16)

    spec = pl.BlockSpec((tm, d), lambda i: (i, 0))
    return pl.pallas_call(body, grid=(t // tm,), in_specs=[spec], out_specs=spec,
                          out_shape=jax.ShapeDtypeStruct((t, d), BF16), compiler_params=_params("parallel"), name=name)(v)


def _shift_down(v, s):
    if s == 0:
        return v
    row = lax.broadcasted_iota(jnp.int32, v.shape, 0)
    return jnp.where(row >= s, pltpu.roll(v, s, axis=0), 0.0)


def _shift_up(v, s):
    if s == 0:
        return v
    t = v.shape[0]
    row = lax.broadcasted_iota(jnp.int32, v.shape, 0)
    return jnp.where(row < t - s, pltpu.roll(v, t - s, axis=0), 0.0)


def _post_col(j):
    return (j % GDN_HEADS) * 3 + j // GDN_HEADS


def _gdn_prep_fwd(proj, conv_w):
    t = proj.shape[0]

    def body(x_ref, w_ref, o_ref):
        j = pl.program_id(0)
        x = x_ref[...]
        y = jnp.zeros_like(x)
        for tap in range(CONV_K):
            y = y + w_ref[tap:tap + 1, :] * _shift_down(x, CONV_K - 1 - tap)
        c = y * jax.nn.sigmoid(y)
        nrm = c * lax.rsqrt(jnp.sum(c * c, axis=1, keepdims=True) + NORM_EPS)
        o_ref[...] = jnp.where(j < 2 * GDN_HEADS, nrm, c)

    return pl.pallas_call(
        body, grid=(QKV_COLS // LANE,),
        in_specs=[pl.BlockSpec((t, LANE), lambda j: (0, j)), pl.BlockSpec((CONV_K, LANE), lambda j: (0, j))],
        out_specs=pl.BlockSpec((t, LANE), lambda j: (0, _post_col(j))),
        out_shape=jax.ShapeDtypeStruct((t, QKV_COLS), F32),
        compiler_params=_params("parallel"), name="gdn_prep_fwd",
    )(proj, conv_w)


def _gdn_prep_bwd(proj, conv_w, dpost, dproj):
    t = proj.shape[0]

    def body(x_ref, w_ref, d_ref, _, dx_ref, dw_ref):
        j = pl.program_id(0)
        x = x_ref[...]
        xs = [_shift_down(x, CONV_K - 1 - tap) for tap in range(CONV_K)]
        y = jnp.zeros_like(x)
        for tap in range(CONV_K):
            y = y + w_ref[tap:tap + 1, :] * xs[tap]
        sig = jax.nn.sigmoid(y)
        c = y * sig
        r = lax.rsqrt(jnp.sum(c * c, axis=1, keepdims=True) + NORM_EPS)
        nrm = c * r
        d = d_ref[...]
        dc_norm = r * (d - nrm * jnp.sum(d * nrm, axis=1, keepdims=True))
        dc = jnp.where(j < 2 * GDN_HEADS, dc_norm, d)
        dy = dc * (sig * (1.0 + y * (1.0 - sig)))
        dx = jnp.zeros_like(x)
        for tap in range(CONV_K):
            dx = dx + _shift_up(w_ref[tap:tap + 1, :] * dy, CONV_K - 1 - tap)
            dw_ref[tap:tap + 1, :] = jnp.sum(dy * xs[tap], axis=0, keepdims=True)
        dx_ref[...] = dx.astype(dx_ref.dtype)

    return pl.pallas_call(
        body, grid=(QKV_COLS // LANE,),
        in_specs=[pl.BlockSpec((t, LANE), lambda j: (0, j)), pl.BlockSpec((CONV_K, LANE), lambda j: (0, j)),
                  pl.BlockSpec((t, LANE), lambda j: (0, _post_col(j))), pl.BlockSpec(memory_space=pl.ANY)],
        out_specs=[pl.BlockSpec((t, LANE), lambda j: (0, j)), pl.BlockSpec((CONV_K, LANE), lambda j: (0, j))],
        out_shape=[jax.ShapeDtypeStruct(dproj.shape, dproj.dtype), jax.ShapeDtypeStruct((CONV_K, QKV_COLS), F32)],
        input_output_aliases={3: 0},
        compiler_params=_params("parallel"), name="gdn_prep_bwd",
    )(proj, conv_w, dpost, dproj)


def _softplus(v):
    return jnp.maximum(v, 0.0) + jnp.log(1.0 + jnp.exp(-jnp.abs(v)))


def _tri_inv(low, nn):
    r = lax.broadcasted_iota(jnp.int32, (CHUNK, CHUNK), 0)
    c = lax.broadcasted_iota(jnp.int32, (CHUNK, CHUNK), 1)
    eye = (r == c).astype(F32)
    same_blk = lax.shift_right_logical(r, 4) == lax.shift_right_logical(c, 4)
    diag = jnp.where(same_blk, low, 0.0)
    off = low - diag
    n1 = -diag
    n2 = nn(n1, n1)
    n4 = nn(n2, n2)
    n8 = nn(n4, n4)
    inv_d = nn(nn(nn(eye + n1, eye + n2), eye + n4), eye + n8)
    m1 = nn(inv_d, off)
    m2 = nn(m1, m1)
    return nn(nn(eye - m1, eye + m2), inv_d)


@jax.custom_vjp
def _tri_inv_known(low, t_inv):
    return t_inv


def _tri_inv_known_fwd(low, t_inv):
    return t_inv, t_inv


def _tri_inv_known_bwd(t_inv, g):
    _, nt, tn = _FDOT_BATCH_PLAIN
    return -nt(tn(t_inv, g), t_inv), jnp.zeros_like(t_inv)


_tri_inv_known.defvjp(_tri_inv_known_fwd, _tri_inv_known_bwd)


LOCAL_HEADS_PER_STEP = 8


def _gdn_local_fn(qkv, ba, alog_row, dtb_row, first_head, bdots, fdots, t_known=None):
    nn, nt, tn = bdots
    fnn = fdots[0]
    n_heads = qkv.shape[1] // (3 * HEAD_DIM)
    part = lambda i, p: qkv[:, (3 * i + p) * HEAD_DIM:(3 * i + p + 1) * HEAD_DIM]
    q = jnp.stack([part(i, 0) for i in range(n_heads)]) * (HEAD_DIM ** -0.5)
    k = jnp.stack([part(i, 1) for i in range(n_heads)])
    v = jnp.stack([part(i, 2) for i in range(n_heads)])
    lane = lax.broadcasted_iota(jnp.int32, ba.shape, 1)
    bg = jnp.where(lane < GDN_HEADS, jax.nn.sigmoid(ba), -jnp.exp(alog_row) * _softplus(ba + dtb_row))
    pick = lambda l: jnp.sum(jnp.where(lane == l, bg, 0.0), axis=1, keepdims=True)
    beta = jnp.stack([pick(first_head + i) for i in range(n_heads)])
    g = jnp.stack([pick(first_head + i + GDN_HEADS) for i in range(n_heads)])

    r = lax.broadcasted_iota(jnp.int32, (CHUNK, CHUNK), 0)
    c = lax.broadcasted_iota(jnp.int32, (CHUNK, CHUNK), 1)
    incl = r >= c
    strict = r > c
    eye = r == c

    def to_row(col):
        return jnp.sum(jnp.where(eye, col, 0.0), axis=1, keepdims=True)

    gc = jnp.sum(jnp.where(incl, to_row(g), 0.0), axis=2, keepdims=True)
    diff = gc - to_row(gc)
    decay = jnp.where(incl, jnp.exp(jnp.where(incl, diff, 0.0)), 0.0)
    k_beta = k * beta
    v_beta = v * beta
    low = jnp.where(strict, nt(k_beta, k) * decay, 0.0)
    t_inv = _tri_inv(low, fnn) if t_known is None else _tri_inv_known(low, t_known)
    eg = jnp.exp(gc)
    u = fnn(t_inv, v_beta)
    w = fnn(t_inv, k_beta * eg)
    attn = jnp.where(incl, nt(q, k) * decay, 0.0)
    last = lax.broadcasted_iota(jnp.int32, (CHUNK, 1), 0) == CHUNK - 1
    g_last = jnp.sum(jnp.where(last, gc, 0.0), axis=1, keepdims=True)
    kdec = k * jnp.exp(g_last - gc)
    elast = jnp.broadcast_to(jnp.exp(g_last), (n_heads, 1, LANE))
    return u, w, q * eg, kdec, attn, elast, t_inv


def _gdn_state_fn(u, w, qg, kdec, attn, elast, state, bdots):
    nn, _, tn = bdots
    v_new = u - nn(w, state)
    o = nn(qg, state) + nn(attn, v_new)
    return o, state * elast + tn(kdec, v_new)


def _gdn_local_fwd(post, proj, alog_row, dtb_row):
    t = post.shape[0]
    n_chunks = t // CHUNK
    hb = LOCAL_HEADS_PER_STEP

    def body(qkv_ref, ba_ref, al_ref, dt_ref, u_ref, w_ref, qg_ref, kd_ref, at_ref, el_ref, ti_ref):
        u, w, qg, kdec, attn, elast, t_inv = _gdn_local_fn(qkv_ref[...], ba_ref[...], al_ref[...], dt_ref[...],
                                                           pl.program_id(1) * hb, _BDOT_BATCH_PLAIN, _FDOT_BATCH_PLAIN)
        for i in range(hb):
            cols = slice(i * HEAD_DIM, (i + 1) * HEAD_DIM)
            u_ref[:, cols] = u[i]
            w_ref[:, cols] = w[i].astype(BF16)
            qg_ref[:, cols] = qg[i].astype(BF16)
            kd_ref[:, cols] = kdec[i].astype(BF16)
        at_ref[...] = attn.astype(BF16)
        el_ref[:, 0] = elast
        ti_ref[...] = t_inv

    wide = pl.BlockSpec((CHUNK, hb * HEAD_DIM), lambda n, j: (n, j))
    square = pl.BlockSpec((hb, CHUNK, CHUNK), lambda n, j: (j, n, 0))
    row = pl.BlockSpec((1, LANE), lambda n, j: (0, 0))
    res = pl.pallas_call(
        body, grid=(n_chunks, GDN_HEADS // hb),
        in_specs=[pl.BlockSpec((CHUNK, hb * 3 * HEAD_DIM), lambda n, j: (n, j)),
                  pl.BlockSpec((CHUNK, LANE), lambda n, j: (n, BA_BLK)), row, row],
        out_specs=[wide, wide, wide, wide, square, pl.BlockSpec((hb, 1, 1, LANE), lambda n, j: (j, n, 0, 0)), square],
        out_shape=[jax.ShapeDtypeStruct((t, GDN_WIDTH), F32), jax.ShapeDtypeStruct((t, GDN_WIDTH), BF16),
                   jax.ShapeDtypeStruct((t, GDN_WIDTH), BF16), jax.ShapeDtypeStruct((t, GDN_WIDTH), BF16),
                   jax.ShapeDtypeStruct((GDN_HEADS, t, CHUNK), BF16),
                   jax.ShapeDtypeStruct((GDN_HEADS, n_chunks, 1, LANE), F32),
                   jax.ShapeDtypeStruct((GDN_HEADS, t, CHUNK), F32)],
        compiler_params=_params("parallel", "parallel"), name="gdn_local_fwd",
    )(post, proj, alog_row, dtb_row)
    return tuple(res[:6]), res[6]


def _by_head(ref):
    return jnp.stack([ref[:, h * HEAD_DIM:(h + 1) * HEAD_DIM] for h in range(ref.shape[1] // HEAD_DIM)])


def _gdn_state_specs(n_of):
    wide = pl.BlockSpec((CHUNK, GDN_WIDTH), lambda n: (n_of(n), 0))
    attn = pl.BlockSpec((GDN_HEADS, CHUNK, CHUNK), lambda n: (0, n_of(n), 0))
    elast = pl.BlockSpec((GDN_HEADS, 1, 1, LANE), lambda n: (0, n_of(n), 0, 0))
    saved = pl.BlockSpec((GDN_HEADS, 1, HEAD_DIM, HEAD_DIM), lambda n: (0, n_of(n), 0, 0))
    return wide, attn, elast, saved


def _gdn_state_fwd(u, w, qg, kdec, attn, elast):
    t = u.shape[0]
    n_chunks = t // CHUNK

    def body(u_ref, w_ref, qg_ref, kd_ref, at_ref, el_ref, o_ref, save_ref, state_ref):
        @pl.when(pl.program_id(0) == 0)
        def _():
            state_ref[...] = jnp.zeros_like(state_ref)

        state = state_ref[...]
        save_ref[:, 0] = state
        o, new_state = _gdn_state_fn(_by_head(u_ref), _by_head(w_ref), _by_head(qg_ref), _by_head(kd_ref), at_ref[...],
                                     el_ref[:, 0], state, _BDOT_BATCH_PLAIN)
        for h in range(GDN_HEADS):
            o_ref[:, h * HEAD_DIM:(h + 1) * HEAD_DIM] = o[h]
        state_ref[...] = new_state

    wide, attn_spec, elast_spec, saved_spec = _gdn_state_specs(lambda n: n)
    return pl.pallas_call(
        body, grid=(n_chunks,), in_specs=[wide, wide, wide, wide, attn_spec, elast_spec],
        out_specs=[wide, saved_spec],
        out_shape=[jax.ShapeDtypeStruct((t, GDN_WIDTH), F32),
                   jax.ShapeDtypeStruct((GDN_HEADS, n_chunks, HEAD_DIM, HEAD_DIM), F32)],
        scratch_shapes=[pltpu.VMEM((GDN_HEADS, HEAD_DIM, HEAD_DIM), F32)],
        compiler_params=_params("arbitrary"), name="gdn_state_fwd",
    )(u, w, qg, kdec, attn, elast)


def _gdn_state_bwd(u, w, qg, kdec, attn, elast, saved, do):
    t = u.shape[0]
    n_chunks = t // CHUNK
    last = n_chunks - 1

    def body(u_ref, w_ref, qg_ref, kd_ref, at_ref, el_ref, save_ref, do_ref,
             du_ref, dw_ref, dqg_ref, dkd_ref, dat_ref, del_ref, dstate_ref):
        @pl.when(pl.program_id(0) == 0)
        def _():
            dstate_ref[...] = jnp.zeros_like(dstate_ref)

        _, vjp = jax.vjp(
            lambda *a: _gdn_state_fn(*a, _BDOT_BATCH_VJP), _by_head(u_ref), _by_head(w_ref).astype(F32),
            _by_head(qg_ref).astype(F32), _by_head(kd_ref).astype(F32), at_ref[...].astype(F32), el_ref[:, 0],
            save_ref[:, 0])
        du, dw, dqg, dkd, dat, de, dstate = vjp((_by_head(do_ref), dstate_ref[...]))
        for h in range(GDN_HEADS):
            cols = slice(h * HEAD_DIM, (h + 1) * HEAD_DIM)
            du_ref[:, cols] = du[h]
            dw_ref[:, cols] = dw[h]
            dqg_ref[:, cols] = dqg[h]
            dkd_ref[:, cols] = dkd[h]
        dat_ref[...] = dat
        del_ref[:, 0] = de
        dstate_ref[...] = dstate

    wide, attn_spec, elast_spec, saved_spec = _gdn_state_specs(lambda n: last - n)
    wide_f32 = jax.ShapeDtypeStruct((t, GDN_WIDTH), F32)
    return pl.pallas_call(
        body, grid=(n_chunks,), in_specs=[wide, wide, wide, wide, attn_spec, elast_spec, saved_spec, wide],
        out_specs=[wide, wide, wide, wide, attn_spec, elast_spec],
        out_shape=[wide_f32, wide_f32, wide_f32, wide_f32, jax.ShapeDtypeStruct((GDN_HEADS, t, CHUNK), F32),
                   jax.ShapeDtypeStruct((GDN_HEADS, n_chunks, 1, LANE), F32)],
        scratch_shapes=[pltpu.VMEM((GDN_HEADS, HEAD_DIM, HEAD_DIM), F32)],
        compiler_params=_params("arbitrary"), name="gdn_state_bwd",
    )(u, w, qg, kdec, attn, elast, saved, do)


def _gdn_local_bwd(post, proj, alog_row, dtb_row, t_inv, cots, dproj):
    t = post.shape[0]
    n_chunks = t // CHUNK
    hb = LOCAL_HEADS_PER_STEP
    n_steps = GDN_HEADS // hb

    def body(qkv_ref, ba_ref, al_ref, dt_ref, ti_ref, du_ref, dw_ref, dqg_ref, dkd_ref, dat_ref, del_ref, _,
             dqkv_ref, dba_ref, dal_ref, ddt_ref, dba_acc):
        n = pl.program_id(0)
        j = pl.program_id(1)

        @pl.when((n == 0) & (j == 0))
        def _():
            dal_ref[...] = jnp.zeros_like(dal_ref)
            ddt_ref[...] = jnp.zeros_like(ddt_ref)

        @pl.when(j == 0)
        def _():
            dba_acc[...] = jnp.zeros_like(dba_acc)

        t_known = ti_ref[...]
        _, vjp = jax.vjp(
            lambda a, b, c, d: _gdn_local_fn(a, b, c, d, j * hb, _BDOT_BATCH_VJP, _FDOT_BATCH_VJP, t_known)[:6],
            qkv_ref[...], ba_ref[...], al_ref[...], dt_ref[...])
        dqkv, dba, dal, ddt = vjp((_by_head(du_ref), _by_head(dw_ref), _by_head(dqg_ref), _by_head(dkd_ref), dat_ref[...],
                                   del_ref[:, 0]))
        dqkv_ref[...] = dqkv
        dba_acc[...] += dba
        dal_ref[...] += dal
        ddt_ref[...] += ddt

        @pl.when(j == n_steps - 1)
        def _():
            dba_ref[:, 0:LANE] = dba_acc[...].astype(dba_ref.dtype)
            dba_ref[:, LANE:2 * LANE] = jnp.zeros((CHUNK, LANE), dba_ref.dtype)

    wide = pl.BlockSpec((CHUNK, hb * HEAD_DIM), lambda n, j: (n, j))
    qkv_spec = pl.BlockSpec((CHUNK, hb * 3 * HEAD_DIM), lambda n, j: (n, j))
    row = pl.BlockSpec((1, LANE), lambda n, j: (0, 0))
    return pl.pallas_call(
        body, grid=(n_chunks, n_steps),
        in_specs=[qkv_spec, pl.BlockSpec((CHUNK, LANE), lambda n, j: (n, BA_BLK)), row, row,
                  pl.BlockSpec((hb, CHUNK, CHUNK), lambda n, j: (j, n, 0)), wide, wide, wide, wide,
                  pl.BlockSpec((hb, CHUNK, CHUNK), lambda n, j: (j, n, 0)),
                  pl.BlockSpec((hb, 1, 1, LANE), lambda n, j: (j, n, 0, 0)), pl.BlockSpec(memory_space=pl.ANY)],
        out_specs=[qkv_spec, pl.BlockSpec((CHUNK, 2 * LANE), lambda n, j: (n, BA_BLK // 2)), row, row],
        out_shape=[jax.ShapeDtypeStruct((t, QKV_COLS), F32), jax.ShapeDtypeStruct(dproj.shape, dproj.dtype),
                   jax.ShapeDtypeStruct((1, LANE), F32), jax.ShapeDtypeStruct((1, LANE), F32)],
        input_output_aliases={11: 1},
        scratch_shapes=[pltpu.VMEM((CHUNK, LANE), F32)],
        compiler_params=_params("arbitrary", "arbitrary"), name="gdn_local_bwd",
    )(post, proj, alog_row, dtb_row, t_inv, *cots, dproj)


def _onorm_fn(o, z, w):
    return o * lax.rsqrt(jnp.mean(o * o, axis=1, keepdims=True) + NORM_EPS) * w * (z * jax.nn.sigmoid(z))


_Z_WIDE_BLK = Z_OFF // GDN_WIDTH


def _onorm_fwd(o_raw, proj, norm_w, mixin, tm=256):
    t = o_raw.shape[0]
    tm = min(tm, t)

    def body(o_ref, z_ref, w_ref, _, out_ref):
        for h in range(GDN_HEADS):
            cols = slice(h * HEAD_DIM, (h + 1) * HEAD_DIM)
            out_ref[:, cols] = _onorm_fn(o_ref[:, cols], z_ref[:, cols], w_ref[...]).astype(out_ref.dtype)

    wide = pl.BlockSpec((tm, GDN_WIDTH), lambda i: (i, 0))
    return pl.pallas_call(
        body, grid=(t // tm,),
        in_specs=[wide, pl.BlockSpec((tm, GDN_WIDTH), lambda i: (i, _Z_WIDE_BLK)), pl.BlockSpec((1, LANE), lambda i: (0, 0)),
                  pl.BlockSpec(memory_space=pl.ANY)],
        out_specs=wide, out_shape=jax.ShapeDtypeStruct(mixin.shape, mixin.dtype), input_output_aliases={3: 0},
        compiler_params=_params("parallel"), name="gdn_onorm_fwd",
    )(o_raw, proj, norm_w, mixin)


def _onorm_bwd(o_raw, proj, norm_w, dmixin, dproj, tm=256):
    t = o_raw.shape[0]
    tm = min(tm, t)

    def body(o_ref, z_ref, w_ref, d_ref, _, do_ref, dz_ref, dw_ref):
        @pl.when(pl.program_id(0) == 0)
        def _():
            dw_ref[...] = jnp.zeros_like(dw_ref)

        for h in range(GDN_HEADS):
            cols = slice(h * HEAD_DIM, (h + 1) * HEAD_DIM)
            _, vjp = jax.vjp(_onorm_fn, o_ref[:, cols], z_ref[:, cols], w_ref[...])
            do, dz, dw = vjp(d_ref[:, cols])
            do_ref[:, cols] = do
            dz_ref[:, cols] = dz.astype(dz_ref.dtype)
            dw_ref[...] += dw

    wide = pl.BlockSpec((tm, GDN_WIDTH), lambda i: (i, 0))
    gate = pl.BlockSpec((tm, GDN_WIDTH), lambda i: (i, _Z_WIDE_BLK))
    row = pl.BlockSpec((1, LANE), lambda i: (0, 0))
    return pl.pallas_call(
        body, grid=(t // tm,), in_specs=[wide, gate, row, wide, pl.BlockSpec(memory_space=pl.ANY)],
        out_specs=[wide, gate, row],
        out_shape=[jax.ShapeDtypeStruct((t, GDN_WIDTH), F32), jax.ShapeDtypeStruct(dproj.shape, dproj.dtype),
                   jax.ShapeDtypeStruct((1, LANE), F32)],
        input_output_aliases={4: 1},
        compiler_params=_params("arbitrary"), name="gdn_onorm_bwd",
    )(o_raw, proj, norm_w, dmixin, dproj)


def _pool_select(levels, gi):
    out = levels[-1]
    for lvl in range(len(levels) - 2, -1, -1):
        out = jnp.where(gi == lvl, levels[lvl], out)
    return out


def _pool_count(shape, gi):
    pos = lax.broadcasted_iota(jnp.int32, shape, 0)
    win = lax.shift_left(jnp.int32(2), gi)
    return jnp.minimum(pos + 1, win).astype(F32)


def _pooled(p, gi):
    acc = p
    levels = []
    for lvl in range(POOL_GROUPS):
        acc = acc + _shift_down(acc, 1 << lvl)
        levels.append(acc)
    return _pool_select(levels, gi) / _pool_count(p.shape, gi) - p


def _pool_fwd(proj, pool_w, pool_scale):
    t = proj.shape[0]

    def body(p_ref, w_ref, s_ref, out_ref):
        gi = pl.program_id(0)
        pooled = _pooled(p_ref[...], gi)
        out_ref[...] = (_BDOT_PLAIN[0](pooled, w_ref[0]) * s_ref[0]).astype(out_ref.dtype)

    return pl.pallas_call(
        body, grid=(POOL_GROUPS,),
        in_specs=[pl.BlockSpec((t, POOL_GROUP_DIM), lambda g: (0, POOL_BLK + g)),
                  pl.BlockSpec((1, POOL_GROUP_DIM, POOL_GROUP_DIM), lambda g: (g, 0, 0)),
                  pl.BlockSpec((1, 1, POOL_GROUP_DIM), lambda g: (g, 0, 0))],
        out_specs=pl.BlockSpec((t, POOL_GROUP_DIM), lambda g: (0, GDN_WIDTH // POOL_GROUP_DIM + g)),
        out_shape=jax.ShapeDtypeStruct((t, 2 * GDN_WIDTH), BF16),
        compiler_params=_params("parallel"), name="pool_fwd",
    )(proj, pool_w, pool_scale)


def _pool_bwd(proj, pool_w, pool_scale, dmixin):
    t = proj.shape[0]
    nn, nt, tn = _BDOT_PLAIN

    def body(p_ref, w_ref, s_ref, d_ref, dp_ref, dw_ref, ds_ref):
        gi = pl.program_id(0)
        p = p_ref[...]
        pooled = _pooled(p, gi)
        mixed = nn(pooled, w_ref[0])
        d = d_ref[...]
        ds_ref[0] = jnp.sum(d * mixed, axis=0, keepdims=True)
        dmixed = d * s_ref[0]
        dw_ref[0] = tn(pooled, dmixed)
        dpooled = nt(dmixed, w_ref[0])
        acc = dpooled / _pool_count(p.shape, gi)
        levels = []
        for lvl in range(POOL_GROUPS):
            acc = acc + _shift_up(acc, 1 << lvl)
            levels.append(acc)
        dp_ref[...] = (_pool_select(levels, gi) - dpooled).astype(dp_ref.dtype)

    return pl.pallas_call(
        body, grid=(POOL_GROUPS,),
        in_specs=[pl.BlockSpec((t, POOL_GROUP_DIM), lambda g: (0, POOL_BLK + g)),
                  pl.BlockSpec((1, POOL_GROUP_DIM, POOL_GROUP_DIM), lambda g: (g, 0, 0)),
                  pl.BlockSpec((1, 1, POOL_GROUP_DIM), lambda g: (g, 0, 0)),
                  pl.BlockSpec((t, POOL_GROUP_DIM), lambda g: (0, GDN_WIDTH // POOL_GROUP_DIM + g))],
        out_specs=[pl.BlockSpec((t, POOL_GROUP_DIM), lambda g: (0, POOL_BLK + g)),
                   pl.BlockSpec((1, POOL_GROUP_DIM, POOL_GROUP_DIM), lambda g: (g, 0, 0)),
                   pl.BlockSpec((1, 1, POOL_GROUP_DIM), lambda g: (g, 0, 0))],
        out_shape=[jax.ShapeDtypeStruct((t, PROJ_COLS), BF16),
                   jax.ShapeDtypeStruct((POOL_GROUPS, POOL_GROUP_DIM, POOL_GROUP_DIM), F32),
                   jax.ShapeDtypeStruct((POOL_GROUPS, 1, POOL_GROUP_DIM), F32)],
        compiler_params=_params("parallel"), name="pool_bwd",
    )(proj, pool_w, pool_scale, dmixin)


def _ln_stats(s):
    mu = jnp.mean(s, axis=1, keepdims=True)
    xc = s - mu
    var = jnp.mean(xc * xc, axis=1, keepdims=True)
    rstd = lax.rsqrt(var + LN_EPS)
    return xc * rstd, rstd


def _mm_ln(a, b, h_in, g, bias, *, name, tm=256):
    t, k = a.shape
    d = b.shape[1]
    tm = min(tm, t)

    def body(a_ref, b_ref, h_ref, g_ref, bias_ref, y_ref, o_ref, o16_ref):
        y = jnp.dot(a_ref[...].astype(BF16), b_ref[...], preferred_element_type=F32)
        y_ref[...] = y
        xhat, _ = _ln_stats(ALPHA * h_ref[...] + y)
        out = xhat * g_ref[...] + bias_ref[...]
        o_ref[...] = out
        o16_ref[...] = out.astype(BF16)

    row = pl.BlockSpec((tm, d), lambda i: (i, 0))
    vec = pl.BlockSpec((1, d), lambda i: (0, 0))
    return pl.pallas_call(
        body, grid=(t // tm,),
        in_specs=[pl.BlockSpec((tm, k), lambda i: (i, 0)), pl.BlockSpec((k, d), lambda i: (0, 0)), row, vec, vec],
        out_specs=[row, row, row],
        out_shape=[jax.ShapeDtypeStruct((t, d), F32), jax.ShapeDtypeStruct((t, d), F32), jax.ShapeDtypeStruct((t, d), BF16)],
        compiler_params=_params("parallel"), name=name,
    )(a, b, h_in, g, bias)


def _ln_backward(xhat, rstd, dout, gain):
    dxhat = dout * gain
    m1 = jnp.mean(dxhat, axis=1, keepdims=True)
    m2 = jnp.mean(dxhat * xhat, axis=1, keepdims=True)
    return (rstd * (dxhat - m1 - xhat * m2), jnp.sum(dout * xhat, axis=0, keepdims=True),
            jnp.sum(dout, axis=0, keepdims=True))


def _ln_loss(h_in, y, g, b, target, *, name, tm=256):
    t, d = h_in.shape
    tm = min(tm, t)

    def body(h_ref, y_ref, g_ref, b_ref, t_ref, sq_ref, ds_ref, ds16_ref, dg_ref, dbias_ref):
        @pl.when(pl.program_id(0) == 0)
        def _():
            sq_ref[...] = jnp.zeros_like(sq_ref)
            dg_ref[...] = jnp.zeros_like(dg_ref)
            dbias_ref[...] = jnp.zeros_like(dbias_ref)

        xhat, rstd = _ln_stats(ALPHA * h_ref[...] + y_ref[...])
        err = xhat * g_ref[...] + b_ref[...] - t_ref[...]
        sq_ref[...] += jnp.sum(jnp.sum(err * err, axis=1, keepdims=True), axis=0, keepdims=True)
        ds, dg, dbias = _ln_backward(xhat, rstd, err * (1.0 / d), g_ref[...])
        ds_ref[...] = ds
        ds16_ref[...] = ds.astype(BF16)
        dg_ref[...] += dg
        dbias_ref[...] += dbias

    row = pl.BlockSpec((tm, d), lambda i: (i, 0))
    vec = pl.BlockSpec((1, d), lambda i: (0, 0))
    return pl.pallas_call(
        body, grid=(t // tm,), in_specs=[row, row, vec, vec, row],
        out_specs=[pl.BlockSpec((1, LANE), lambda i: (0, 0)), row, row, vec, vec],
        out_shape=[jax.ShapeDtypeStruct((1, LANE), F32), jax.ShapeDtypeStruct((t, d), F32),
                   jax.ShapeDtypeStruct((t, d), BF16), jax.ShapeDtypeStruct((1, d), F32), jax.ShapeDtypeStruct((1, d), F32)],
        compiler_params=_params("arbitrary"), name=name,
    )(h_in, y, g, b, target)


def _ln_bwd(h_in, y, g, d_a, d_b, *, name, tm=256):
    t, d = h_in.shape
    tm = min(tm, t)
    has_b = d_b is not None

    def body(*refs):
        if has_b:
            h_ref, y_ref, g_ref, da_ref, db_ref, ds_ref, ds16_ref, dg_ref, dbias_ref = refs
        else:
            h_ref, y_ref, g_ref, da_ref, ds_ref, ds16_ref, dg_ref, dbias_ref = refs

        @pl.when(pl.program_id(0) == 0)
        def _():
            dg_ref[...] = jnp.zeros_like(dg_ref)
            dbias_ref[...] = jnp.zeros_like(dbias_ref)

        xhat, rstd = _ln_stats(ALPHA * h_ref[...] + y_ref[...])
        dout = da_ref[...]
        if has_b:
            dout = dout + ALPHA * db_ref[...]
        ds, dg, dbias = _ln_backward(xhat, rstd, dout, g_ref[...])
        ds_ref[...] = ds
        ds16_ref[...] = ds.astype(BF16)
        dg_ref[...] += dg
        dbias_ref[...] += dbias

    row = pl.BlockSpec((tm, d), lambda i: (i, 0))
    vec = pl.BlockSpec((1, d), lambda i: (0, 0))
    args = [h_in, y, g, d_a] + ([d_b] if has_b else [])
    return pl.pallas_call(
        body, grid=(t // tm,), in_specs=[row, row, vec, row] + ([row] if has_b else []),
        out_specs=[row, row, vec, vec],
        out_shape=[jax.ShapeDtypeStruct((t, d), F32), jax.ShapeDtypeStruct((t, d), BF16),
                   jax.ShapeDtypeStruct((1, d), F32), jax.ShapeDtypeStruct((1, d), F32)],
        compiler_params=_params("arbitrary"), name=name,
    )(*args)


def _attn_fn(q, k, v, dots):
    nn, nt, _ = dots
    s = nt(q, k) * (XATTN_HEAD_DIM ** -0.5)
    s = s - lax.stop_gradient(jnp.max(s, axis=1, keepdims=True))
    e = jnp.exp(s)
    p = e / jnp.sum(e, axis=1, keepdims=True)
    return nn(p, v)


def _attn_fwd(q, k, v, tq=2048):
    t = q.shape[0]
    tq = min(tq, t)

    def body(q_ref, k_ref, v_ref, o_ref):
        o_ref[...] = _attn_fn(q_ref[...], k_ref[...], v_ref[...], _BDOT_PLAIN).astype(BF16)

    qs = pl.BlockSpec((tq, XATTN_HEAD_DIM), lambda h, i: (i, h))
    ks = pl.BlockSpec((MEM_LEN, XATTN_HEAD_DIM), lambda h, i: (0, h))
    return pl.pallas_call(
        body, grid=(XATTN_HEADS, t // tq), in_specs=[qs, ks, ks], out_specs=qs,
        out_shape=jax.ShapeDtypeStruct(q.shape, BF16), compiler_params=_params("parallel", "parallel"), name="xattn_fwd",
    )(q, k, v)


def _attn_bwd(q, k, v, do, tq=1024):
    t = q.shape[0]
    tq = min(tq, t)

    def body(q_ref, k_ref, v_ref, do_ref, dq_ref, dk_ref, dv_ref):
        @pl.when(pl.program_id(1) == 0)
        def _():
            dk_ref[...] = jnp.zeros_like(dk_ref)
            dv_ref[...] = jnp.zeros_like(dv_ref)

        _, vjp = jax.vjp(lambda a, b, c: _attn_fn(a, b, c, _BDOT_VJP), q_ref[...].astype(F32), k_ref[...].astype(F32),
                         v_ref[...].astype(F32))
        dq, dk, dv = vjp(do_ref[...].astype(F32))
        dq_ref[...] = dq.astype(BF16)
        dk_ref[...] += dk
        dv_ref[...] += dv

    qs = pl.BlockSpec((tq, XATTN_HEAD_DIM), lambda h, i: (i, h))
    ks = pl.BlockSpec((MEM_LEN, XATTN_HEAD_DIM), lambda h, i: (0, h))
    return pl.pallas_call(
        body, grid=(XATTN_HEADS, t // tq), in_specs=[qs, ks, ks, qs], out_specs=[qs, ks, ks],
        out_shape=[jax.ShapeDtypeStruct(q.shape, BF16), jax.ShapeDtypeStruct(k.shape, F32), jax.ShapeDtypeStruct(v.shape, F32)],
        compiler_params=_params("parallel", "arbitrary"), name="xattn_bwd",
    )(q, k, v, do)


def _local_step(x, x16, mem, target, weights_of, grads_ready):
    def behind(vec, token):
        return vec if token is None else vec + token

    w = dict(weights_of("mixer", None))
    proj = _mm(x16, w["w_in"], tb=True, tn=768, name="mm_in_proj")
    mixin = _pool_fwd(proj, w["pool_w"], w["pool_scale"])
    post = _gdn_prep_fwd(proj, w["conv_w"])
    token = weights_of("ahead_conv", post)
    chunked, t_inv = _gdn_local_fwd(post, proj, behind(w["alog_row"], token), w["dtb_row"])
    o_raw, saved = _gdn_state_fwd(*chunked)
    token = weights_of("ahead_scan", o_raw)
    mixin = _onorm_fwd(o_raw, proj, behind(w["gdn_norm_w"], token), mixin)
    w.update(weights_of("attn", mixin))
    mix, h1, h1_16 = _mm_ln(mixin, w["w_out"], x, w["ln1_g"], w["ln1_b"], name="mm_out_proj_ln1")
    xq = _mm(h1_16, w["xq_w"], out_dtype=BF16, name="mm_xq")
    xk = _mm(mem, w["xk_w"], out_dtype=BF16, name="mm_xk")
    xv = _mm(mem, w["xv_w"], out_dtype=BF16, name="mm_xv")
    xo = _attn_fwd(xq, xk, xv)
    token = weights_of("ahead_attn", xo)
    if token is not None:
        xo, _ = lax.optimization_barrier((xo, token))
    xa, h2, h2_16 = _mm_ln(xo, w["xo_w"], h1, w["ln2_g"], w["ln2_b"], name="mm_xo_ln2")
    w.update(weights_of("up", h2_16))
    act, relu = _mm(h2_16, w["w_up"], b_chunks=True, epi="relu2", name="mm_up")
    w.update(weights_of("down", act))
    ff = _mm(act, w["w_down"], tn=512, tk=2048, name="mm_down")
    g = {}
    sq, ds3, ds3_16, g["ln3_g"], g["ln3_b"] = _ln_loss(h2, ff, w["ln3_g"], w["ln3_b"], target, name="ln3_loss")

    gw_down = _mm(act, ds3_16, ta=True, out_dtype=BF16, tm=512, tn=D_MODEL, name="mm_gw_down")
    du = _mm(ds3_16, w["w_down"], tb=True, epi="mul2r", extra=relu, name="mm_du")
    gw_up = _mm(h2_16, du, ta=True, out_dtype=BF16, o_chunks=True, name="mm_gw_up")
    token = grads_ready("mlp", {"w_down": gw_down, "w_up": gw_up})
    dh2 = _mm(du, w["w_up"], tb=True, b_chunks=True, tn=1024, tk=1024, name="mm_dh2")
    ds2, ds2_16, g["ln2_g"], g["ln2_b"] = _ln_bwd(h1, xa, behind(w["ln2_g"], token), dh2, ds3, name="ln2_bwd")
    gw_xo = _mm(xo, ds2_16, ta=True, out_dtype=BF16, name="mm_gw_xo")
    dxo = _mm(ds2_16, w["xo_w"], tb=True, out_dtype=BF16, name="mm_dxo")
    dxq, dxk, dxv = _attn_bwd(xq, xk, xv, dxo)
    gw_xq = _mm(h1_16, dxq, ta=True, out_dtype=BF16, name="mm_gw_xq")
    gw_xk = _mm(mem, dxk, ta=True, out_dtype=BF16, name="mm_gw_xk")
    gw_xv = _mm(mem, dxv, ta=True, out_dtype=BF16, name="mm_gw_xv")
    token = grads_ready("attn", {"xo_w": gw_xo, "xq_w": gw_xq, "xk_w": gw_xk, "xv_w": gw_xv})
    dh1 = _mm(dxq, w["xq_w"], tb=True, name="mm_dh1")
    ds1, ds1_16, g["ln1_g"], g["ln1_b"] = _ln_bwd(x, mix, behind(w["ln1_g"], token), dh1, ds2, name="ln1_bwd")
    gw_out = _mm(mixin, ds1_16, ta=True, out_dtype=BF16, name="mm_gw_out")
    dmixin = _mm(ds1_16, w["w_out"], tb=True, name="mm_dmixin")
    dproj, gw_pool, g["pool_scale"] = _pool_bwd(proj, w["pool_w"], w["pool_scale"], dmixin)
    token = grads_ready("mix", {"w_out": gw_out, "pool_w": gw_pool})
    do_raw, dproj, g["gdn_norm_w"] = _onorm_bwd(o_raw, proj, behind(w["gdn_norm_w"], token), dmixin, dproj)
    cots = _gdn_state_bwd(*chunked, saved, do_raw)
    token = grads_ready("tick", {"after": cots[0]})
    dpost, dproj, g["alog_row"], g["dtb_row"] = _gdn_local_bwd(post, proj, behind(w["alog_row"], token), w["dtb_row"],
                                                               t_inv, cots, dproj)
    dproj, g["conv_w"] = _gdn_prep_bwd(proj, w["conv_w"], dpost, dproj)
    token = grads_ready("small", {**g, "sq": sq})
    gw_in = _mm(dproj, x16, ta=True, out_dtype=BF16, tm=768, tn=D_MODEL, after=token, name="mm_gw_in")
    token = grads_ready("in", {"w_in": gw_in})
    grad_x = _mm(dproj, w["w_in"], tk=1792, epi="add", extra=ds1, add_scale=ALPHA, after=token, name="mm_dx")
    return sq, grad_x, g


_VECTORS = ("a_log", "dt_bias", "gdn_norm_w", "pool_scale", "ln1_g", "ln1_b", "ln2_g", "ln2_b", "ln3_g", "ln3_b")
_BA_SPLIT = BA_OFF + 2 * GDN_HEADS


def _lane_row(v, offset):
    return jnp.zeros((1, LANE), F32).at[0, offset:offset + v.shape[0]].set(v)


_GROUP_VECTORS = {"mixer": (), "attn": ("ln1_g", "ln1_b", "ln2_g", "ln2_b"), "up": (), "down": ("ln3_g", "ln3_b")}


def _group_weights(group, full):
    w = {n: full[n].reshape(1, D_MODEL) for n in _GROUP_VECTORS[group]}
    if group == "mixer":
        w.update({
            "w_in": _w_in_padded(full["w_in"]),
            "conv_w": full["conv_w"],
            "alog_row": _lane_row(full["a_log"], GDN_HEADS),
            "dtb_row": _lane_row(full["dt_bias"], GDN_HEADS),
            "gdn_norm_w": full["gdn_norm_w"].reshape(1, LANE),
            "pool_w": full["pool_w"],
            "pool_scale": full["pool_scale"].reshape(POOL_GROUPS, 1, POOL_GROUP_DIM),
        })
    else:
        w.update({n: full[n] for n in dict(_GATHER_GROUPS)[group]})
    return w


def _w_in_row_map():
    per = IN_COLS // N_DEV
    gap = POOL_OFF - _BA_SPLIT
    pieces = []
    for d in range(N_DEV):
        lo, hi = d * per, (d + 1) * per
        if hi <= _BA_SPLIT:
            pieces.append([(0, lo, per)])
        elif lo >= _BA_SPLIT:
            pieces.append([(0, lo + gap, per)])
        else:
            pieces.append([(0, lo, _BA_SPLIT - lo), (_BA_SPLIT - lo, POOL_OFF, hi - _BA_SPLIT)])
    return pieces


_W_IN_LANES = 256


def _w_in_padded(blocks):
    def body(b_ref, o_ref):
        for d, pieces in enumerate(_w_in_row_map()):
            for src, dst, rows in pieces:
                o_ref[dst:dst + rows, :] = b_ref[d, src:src + rows, :]
        o_ref[_BA_SPLIT:POOL_OFF, :] = jnp.zeros((POOL_OFF - _BA_SPLIT, _W_IN_LANES), o_ref.dtype)

    n, per, cols = blocks.shape
    return pl.pallas_call(
        body, grid=(cols // _W_IN_LANES,), in_specs=[pl.BlockSpec((n, per, _W_IN_LANES), lambda j: (0, 0, j))],
        out_specs=pl.BlockSpec((PROJ_COLS, _W_IN_LANES), lambda j: (0, j)),
        out_shape=jax.ShapeDtypeStruct((PROJ_COLS, cols), blocks.dtype), compiler_params=_params("parallel"),
        name="w_in_padded")(blocks)


def _w_in_chunks(g):
    def body(g_ref, o_ref):
        for d, pieces in enumerate(_w_in_row_map()):
            for dst, src, rows in pieces:
                o_ref[d, dst:dst + rows, :] = g_ref[src:src + rows, :]

    cols = g.shape[1]
    per = IN_COLS // N_DEV
    return pl.pallas_call(
        body, grid=(cols // _W_IN_LANES,), in_specs=[pl.BlockSpec((PROJ_COLS, _W_IN_LANES), lambda j: (0, j))],
        out_specs=pl.BlockSpec((N_DEV, per, _W_IN_LANES), lambda j: (0, 0, j)),
        out_shape=jax.ShapeDtypeStruct((N_DEV, per, cols), g.dtype), compiler_params=_params("parallel"),
        name="w_in_chunks")(g)


def _finish_small_grads(g):
    out = {"conv_w": g["conv_w"]}
    out["a_log"] = g["alog_row"][0, GDN_HEADS:2 * GDN_HEADS]
    out["dt_bias"] = g["dtb_row"][0, GDN_HEADS:2 * GDN_HEADS]
    out["gdn_norm_w"] = g["gdn_norm_w"].reshape(LANE)
    out["pool_scale"] = g["pool_scale"].reshape(POOL_GROUPS * POOL_GROUP_DIM)
    for n in ("ln1_g", "ln1_b", "ln2_g", "ln2_b", "ln3_g", "ln3_b"):
        out[n] = g[n].reshape(D_MODEL)
    return out


def _adamw_math(w, g, m, v):
    m = ADAM_B1 * m + (1.0 - ADAM_B1) * g
    v = ADAM_B2 * v + (1.0 - ADAM_B2) * (g * g)
    m_hat = m / (1.0 - ADAM_B1 ** ADAM_STEP)
    v_hat = v / (1.0 - ADAM_B2 ** ADAM_STEP)
    delta = -ADAM_LR * (m_hat / (jnp.sqrt(v_hat) + ADAM_EPS) + ADAM_WD * w)
    return delta, m, v


ADAMW_TILE_ELEMS = 256 * 1024
CHIP_SUM_TILE_ELEMS = 1024 * 1024


def _shard_tile(r, c, elems):
    for rows in (1024, 512, 256, 128):
        if r % rows == 0 and rows * c <= elems:
            return rows, c
    if r % 128 == 0:
        return 128, c
    for cols in (2048, 1024, 512):
        if c % cols == 0 and r * cols <= elems:
            return r, cols
    return r, 256 if c % 256 == 0 else c


def _adamw_shard(parts, own, me, w, m, v, *, name):
    s, r, c = parts.shape
    tr, tc = _shard_tile(r, c, ADAMW_TILE_ELEMS)
    assert r % tr == 0 and c % tc == 0, (name, r, c)
    unit_axis = w.ndim == 3
    at = (slice(None), 0, slice(None)) if unit_axis else Ellipsis

    def body(me_ref, p_ref, own_ref, w_ref, m_ref, v_ref, g_ref, d_ref, nm_ref, nv_ref):
        mine = own_ref[...].astype(F32)
        g = None
        for i in range(s):
            part = jnp.where(me_ref[0] == i, mine, p_ref[i].astype(F32))
            g = part if g is None else g + part
        delta, nm, nv = _adamw_math(w_ref[at], g, m_ref[at], v_ref[at])
        g_ref[at] = g
        d_ref[at] = delta
        nm_ref[at] = nm
        nv_ref[at] = nv

    if unit_axis:
        blk = pl.BlockSpec((tr, 1, tc), lambda i, j, me_ref: (i, 0, j))
        out = jax.ShapeDtypeStruct((r, 1, c), F32)
    else:
        blk = pl.BlockSpec((tr, tc), lambda i, j, me_ref: (i, j))
        out = jax.ShapeDtypeStruct((r, c), F32)
    return pl.pallas_call(
        body,
        grid_spec=pltpu.PrefetchScalarGridSpec(
            num_scalar_prefetch=1, grid=(r // tr, c // tc),
            in_specs=[pl.BlockSpec((s, tr, tc), lambda i, j, me_ref: (0, i, j)),
                      pl.BlockSpec((None, tr, tc), lambda i, j, me_ref: (me_ref[0], i, j)), blk, blk, blk],
            out_specs=[blk, blk, blk, blk]),
        out_shape=[out, out, out, out], compiler_params=_params("parallel", "parallel"), name=name,
    )(me, parts, own, w, m, v)


N_CHIPS = N_DEV // 2


def _chip_sums(chunks, from_sibling, core, *, name):
    n = len(chunks)
    _, r, c = chunks[0].shape
    assert all(a.shape == chunks[0].shape for a in chunks), name
    tr, tc = _shard_tile(r, c, CHIP_SUM_TILE_ELEMS // n)
    assert r % tr == 0 and c % tc == 0, (name, r, c)

    def body(core_ref, *refs):
        for a in range(n):
            refs[2 * n + a][...] = (refs[a][...].astype(F32) + refs[n + a][...].astype(F32)).astype(BF16)

    by_chip = pl.BlockSpec((None, tr, tc), lambda q, i, j, core_ref: (q, i, j))
    mine = pl.BlockSpec((None, tr, tc), lambda q, i, j, core_ref: (2 * q + core_ref[0], i, j))
    return pl.pallas_call(
        body,
        grid_spec=pltpu.PrefetchScalarGridSpec(
            num_scalar_prefetch=1, grid=(N_CHIPS, r // tr, c // tc),
            in_specs=[mine] * n + [by_chip] * n, out_specs=[by_chip] * n),
        out_shape=[jax.ShapeDtypeStruct((N_CHIPS, r, c), chunks[0].dtype)] * n,
        compiler_params=_params("parallel", "parallel", "parallel"), name=name,
    )(core, *chunks, *from_sibling)


def _place():
    return lax.axis_index("x"), lax.axis_index("y"), lax.axis_index("c")


def _slot(px, py, pc):
    return 4 * px + 2 * py + pc


_HBM = pl.BlockSpec(memory_space=pltpu.HBM)


_SEM = pl.BlockSpec(memory_space=pltpu.SEMAPHORE)
_ANY = pl.BlockSpec(memory_space=pl.ANY)
_EFFECT = pltpu.SideEffectType.DATAFLOW_SIDE_EFFECTING


def _peer(k, x, y, c):
    return (1 - x if k & 4 else x, 1 - y if k & 2 else y, 1 - c if k & 1 else c)


_EXCHANGE_BITS = {"gather_near": (1, 2, 4), "gather_relay": (6,), "gather_pass": (2, 4, 6),
                  "scatter_sibling": (1, 1, 1, 1), "scatter_chips": (2, 4, 6), "all_small": (1, 2, 3, 4, 5, 6, 7)}


def _exchange_copy(mode, src, land, w, i, place, send_sems, recv_sems, receiving):
    bits = _EXCHANGE_BITS[mode]
    k = bits[i]
    peer = _peer(k, *place)
    me = _slot(*place)
    if mode in ("gather_near", "all_small"):
        to, src_ref, sent_to, got_at = peer, src[w], me, _slot(*peer)
    elif mode == "gather_relay":
        x, y, c = place
        other = 1 - c
        to = (lax.bitwise_xor(x, c), lax.bitwise_xor(y, other), c)
        blk = _slot(lax.bitwise_xor(x, other), lax.bitwise_xor(y, c), c)
        src_ref, sent_to, got_at = land[w].at[blk], blk, _slot(*peer)
    elif mode == "gather_pass":
        blk = _slot(*peer)
        to, src_ref, sent_to, got_at = _peer(1, *place), land[w].at[blk], blk, _slot(*_peer(k | 1, *place))
    elif mode == "scatter_sibling":
        to, src_ref, sent_to, got_at = peer, src[w].at[2 * i + 1 - place[2]], i, i
    else:
        to, src_ref, sent_to, got_at = peer, src[w].at[_slot(*peer) // 2], me // 2, _slot(*peer) // 2
    sem = w * len(bits) + i
    return pltpu.make_async_remote_copy(
        src_ref=src_ref, dst_ref=land[w].at[got_at if receiving else sent_to], send_sem=send_sems.at[sem],
        recv_sem=recv_sems.at[sem], device_id=to, device_id_type=MESH)


def _exchange_start(mode, srcs, lands, after, *, name):
    ns, nl = len(srcs), len(lands)
    n_sem = nl * len(_EXCHANGE_BITS[mode])

    def body(*refs):
        src, land = refs[:ns], refs[ns:ns + nl]
        send_sems, recv_sems = refs[ns + nl + 1:ns + nl + 3]
        token = refs[-1]
        place = _place()
        for w in range(nl):
            for i in range(len(_EXCHANGE_BITS[mode])):
                _exchange_copy(mode, src, land, w, i, place, send_sems, recv_sems, receiving=False).start()
        token[...] = jnp.zeros_like(token)

    sems = pltpu.SemaphoreType.DMA((n_sem,))
    arrays = list(srcs) + list(lands)
    res = pl.pallas_call(
        body, name=name, in_specs=[_HBM] * (ns + nl) + [_ANY],
        out_specs=(_SEM, _SEM, *([_HBM] * (ns + nl)), pl.BlockSpec(memory_space=pltpu.VMEM)),
        out_shape=(sems, sems, *[pltpu.HBM(a.shape, a.dtype) for a in arrays], jax.ShapeDtypeStruct((8, LANE), F32)),
        input_output_aliases={i: 2 + i for i in range(ns + nl)},
        compiler_params=pltpu.CompilerParams(has_side_effects=_EFFECT),
    )(*[pltpu.with_memory_space_constraint(a, pltpu.HBM) for a in arrays], after)
    return res[0], res[1], list(res[2:2 + ns]), list(res[2 + ns:2 + ns + nl]), res[-1]


def _exchange_wait(mode, started, after, *, name):
    send_sems, recv_sems, srcs, lands, _ = started
    ns, nl = len(srcs), len(lands)

    def body(*refs):
        src, land = refs[:ns], refs[ns:ns + nl]
        send_sems, recv_sems = refs[ns + nl:ns + nl + 2]
        place = _place()
        for w in range(nl):
            for i in range(len(_EXCHANGE_BITS[mode])):
                cp = _exchange_copy(mode, src, land, w, i, place, send_sems, recv_sems, receiving=True)
                cp.wait_send()
                cp.wait_recv()

    arrays = list(srcs) + list(lands)
    res = pl.pallas_call(
        body, name=name, in_specs=[_HBM] * (ns + nl) + [_SEM, _SEM, _ANY], out_specs=[_HBM] * (ns + nl),
        out_shape=[pltpu.HBM(a.shape, a.dtype) for a in arrays],
        input_output_aliases={i: i for i in range(ns + nl)},
        compiler_params=pltpu.CompilerParams(has_side_effects=_EFFECT),
    )(*arrays, send_sems, recv_sems, after)
    return list(res[:ns]), list(res[ns:])


_LN_ROWS = ("ln1_g", "ln1_b", "ln2_g", "ln2_b", "ln3_g", "ln3_b")
_MISC_ROW = len(_LN_ROWS)
_MISC = (("pool_scale", 0, GDN_WIDTH), ("gdn_norm_w", GDN_WIDTH, HEAD_DIM), ("a_log", GDN_WIDTH + LANE, GDN_HEADS),
         ("dt_bias", GDN_WIDTH + 2 * LANE, GDN_HEADS), ("loss", GDN_WIDTH + 3 * LANE, 1))
_CONV_ROW = _MISC_ROW + 1
_CONV_ROWS = CONV_K * QKV_COLS // D_MODEL
_SMALL_ROWS = 16


def _pack_small(vals):
    pieces, at = [], 0
    for n, off, size in _MISC:
        pieces.append(jnp.zeros((off - at,), F32))
        pieces.append(vals[n].reshape(size).astype(F32) if n in vals else jnp.zeros((size,), F32))
        at = off + size
    pieces.append(jnp.zeros((D_MODEL - at,), F32))
    conv = vals["conv_w"].reshape(-1) if "conv_w" in vals else jnp.zeros((_CONV_ROWS * D_MODEL,), F32)
    tail = jnp.zeros(((_SMALL_ROWS - _CONV_ROW - _CONV_ROWS) * D_MODEL,), F32)
    flat = jnp.concatenate([vals[n].reshape(D_MODEL) for n in _LN_ROWS] + pieces + [conv, tail])
    return flat.reshape(_SMALL_ROWS, D_MODEL)


def _adamw_small(zone, mine, me, w, m, v):
    short = [(n, off, size) for n, off, size in _MISC if n != "loss"]

    def body(me_ref, z_ref, mine_ref, w_ref, m_ref, v_ref, *rest):
        outs, (g_s, d_s, nm_s, nv_s) = rest[:-4], rest[-4:]
        g = None
        for s in range(N_DEV):
            part = jnp.where(me_ref[0] == s, mine_ref[...], z_ref[s])
            g = part if g is None else g + part
        g_s[...] = g
        d_s[...], nm_s[...], nv_s[...] = _adamw_math(w_ref[...], g, m_ref[...], v_ref[...])
        k = 0
        for src in (g_s, d_s, nm_s, nv_s):
            for r in range(len(_LN_ROWS)):
                outs[k][...] = src[r:r + 1, :]
                k += 1
            for _, off, size in short:
                outs[k][...] = src[_MISC_ROW:_MISC_ROW + 1, off:off + size]
                k += 1
        outs[k][...] = g_s[_CONV_ROW:_CONV_ROW + _CONV_ROWS, :]
        outs[k + 1][...] = g_s[_MISC_ROW:_MISC_ROW + 1, :]

    rows, d = mine.shape
    per_quantity = [jax.ShapeDtypeStruct((1, D_MODEL), F32)] * len(_LN_ROWS) + [
        jax.ShapeDtypeStruct((1, size), F32) for _, _, size in short]
    out_shape = per_quantity * 4 + [jax.ShapeDtypeStruct((_CONV_ROWS, d), F32), jax.ShapeDtypeStruct((1, d), F32)]
    whole = lambda a: pl.BlockSpec(a.shape, lambda i, me_ref: (0,) * len(a.shape))
    res = pl.pallas_call(
        body,
        grid_spec=pltpu.PrefetchScalarGridSpec(
            num_scalar_prefetch=1, grid=(1,), in_specs=[whole(a) for a in (zone, mine, w, m, v)],
            out_specs=[whole(s) for s in out_shape], scratch_shapes=[pltpu.VMEM((rows, d), F32)] * 4),
        out_shape=out_shape, compiler_params=_params("arbitrary"), name="adamw_small",
    )(me, zone, mine, w, m, v)
    names = list(_LN_ROWS) + [n for n, _, _ in short]
    n_each = len(names)
    quantities = [dict(zip(names, res[q * n_each:(q + 1) * n_each])) for q in range(4)]
    return quantities, res[-2], res[-1]


_WEIGHT_ORDER = ("w_in", "conv_w", "a_log", "dt_bias", "gdn_norm_w", "pool_w", "pool_scale", "w_out", "ln1_g", "ln1_b",
                 "xq_w", "xk_w", "xv_w", "xo_w", "ln2_g", "ln2_b", "w_up", "w_down", "ln3_g", "ln3_b")


def _shard2d(name, a):
    if name == "w_in":
        return a.T
    return a.reshape(-1, a.shape[-1]) if name == "pool_w" else a


def _update_view(name, a):
    return jnp.transpose(a, (2, 0, 1)) if name == "w_in" else _shard2d(name, a[0])


def _shard_result(name, r, shape):
    return jnp.transpose(r, (1, 2, 0)) if name == "w_in" else r.reshape(shape)


def _gathered_to_full(name, gth):
    if name in ("w_up", "w_in"):
        return gth
    if name == "conv_w":
        return jnp.transpose(gth, (1, 0, 2)).reshape(gth.shape[1], N_DEV * gth.shape[2])
    if name == "pool_w":
        g4 = gth.reshape(N_DEV, POOL_GROUPS, POOL_GROUP_DIM // N_DEV, POOL_GROUP_DIM)
        return jnp.transpose(g4, (1, 0, 2, 3)).reshape(POOL_GROUPS, POOL_GROUP_DIM, POOL_GROUP_DIM)
    return gth.reshape(N_DEV * gth.shape[1], gth.shape[2])


def _full_to_chunks(name, full):
    if name == "w_up":
        return full
    if name == "pool_w":
        g4 = full.reshape(POOL_GROUPS, N_DEV, POOL_GROUP_DIM // N_DEV, POOL_GROUP_DIM)
        return jnp.transpose(g4, (1, 0, 2, 3)).reshape(N_DEV, POOL_GROUPS * POOL_GROUP_DIM // N_DEV, POOL_GROUP_DIM)
    return full.reshape(N_DEV, full.shape[0] // N_DEV, full.shape[1])


_GATHER_GROUPS = (("mixer", ("w_in", "conv_w", "pool_w")), ("attn", ("w_out", "xq_w", "xk_w", "xv_w", "xo_w")),
                  ("up", ("w_up",)), ("down", ("w_down",)))


def _grad_chunks(name, g):
    if name == "w_in":
        return _w_in_chunks(g.astype(BF16))
    return _full_to_chunks(name, g.astype(BF16))


def kernel(x, mem, w_in, conv_w, a_log, dt_bias, gdn_norm_w, pool_w, pool_scale, w_out, ln1_g, ln1_b, xq_w, xk_w, xv_w, xo_w, ln2_g, ln2_b, w_up, w_down, ln3_g, ln3_b, loss_target, m_w_in, m_conv_w, m_a_log, m_dt_bias, m_gdn_norm_w, m_pool_w, m_pool_scale, m_w_out, m_ln1_g, m_ln1_b, m_xq_w, m_xk_w, m_xv_w, m_xo_w, m_ln2_g, m_ln2_b, m_w_up, m_w_down, m_ln3_g, m_ln3_b, v_w_in, v_conv_w, v_a_log, v_dt_bias, v_gdn_norm_w, v_pool_w, v_pool_scale, v_w_out, v_ln1_g, v_ln1_b, v_xq_w, v_xk_w, v_xv_w, v_xo_w, v_ln2_g, v_ln2_b, v_w_up, v_w_down, v_ln3_g, v_ln3_b):
    args = dict(locals())
    wt = {n: args[n][0] for n in _WEIGHT_ORDER}
    mo = {n: args["m_" + n][0] for n in _WEIGHT_ORDER}
    vo = {n: args["v_" + n][0] for n in _WEIGHT_ORDER}

    me = _slot(*_place())
    me_arr = jnp.reshape(me, (1,)).astype(jnp.int32)
    nothing = jnp.zeros((8, LANE), F32)

    def landing_zones(names):
        shards = [_shard2d(n, wt[n]).astype(F32 if n == "conv_w" else BF16) for n in names]
        zones = [lax.dynamic_update_slice(lax.empty((N_DEV, *s.shape), s.dtype), s[None], (me, 0, 0)) for s in shards]
        return shards, zones

    chip_arr = jnp.reshape(me // 2, (1,)).astype(jnp.int32)
    core_arr = jnp.reshape(lax.axis_index("c"), (1,)).astype(jnp.int32)
    names_of = dict(_GATHER_GROUPS)
    gathers = {}
    prepared = {}

    def gather_near(group, after):
        shards, zones = prepared.pop(group) if group in prepared else landing_zones(names_of[group])
        gathers[group] = _exchange_start("gather_near", shards, zones, after, name="gather_near_" + group)
        return gathers[group][4]

    def gather_next(group, was, now, after):
        _, zones = _exchange_wait(was, gathers[group], after, name=f"{was}_{group}_wait")
        gathers[group] = _exchange_start(now, [], zones, nothing, name=f"{now}_{group}")
        return gathers[group][4]

    def gather_relay(group, after):
        return gather_next(group, "gather_near", "gather_relay", after)

    def gather_pass(group, after):
        return gather_next(group, "gather_relay", "gather_pass", after)

    def gathered(group, after):
        _, zones = _exchange_wait("gather_pass", gathers[group], after, name=f"gather_pass_{group}_wait")
        full = {n: _gathered_to_full(n, z) for n, z in zip(names_of[group], zones)}
        full.update({n: wt[n] for n in _VECTORS})
        return _group_weights(group, full)

    token = gather_near("mixer", nothing)
    x16 = _cast_bf16(x[0], name="cast_x")
    later = {group: landing_zones(names_of[group]) for group in ("attn", "up", "down")}
    token, x16, later = lax.optimization_barrier((token, x16, later))
    prepared.update(later)
    token = gather_pass("mixer", gather_relay("mixer", token))
    token = gather_near("attn", token)

    def weights_of(group, after):
        if group == "mixer":
            return gathered(group, token)
        if group == "ahead_conv":
            return gather_near("up", gather_relay("attn", after))[0:1, 0:1]
        if group == "ahead_scan":
            return gather_pass("attn", after)[0:1, 0:1]
        if group == "attn":
            return gathered(group, gather_near("down", gather_relay("up", after)))
        if group == "ahead_attn":
            return gather_relay("down", gather_pass("up", after))[0:1, 0:1]
        if group == "up":
            return gathered(group, gather_pass("down", after))
        return gathered(group, after)

    scatters = {}
    in_flight = []

    def chip_stage(after):
        group, names, started = in_flight.pop()
        chunks, from_sibling = _exchange_wait("scatter_sibling", started, after, name=f"scatter_sibling_{group}_wait")
        sums, alike = [None] * len(names), {}
        for i, chunk in enumerate(chunks):
            alike.setdefault(chunk.shape, []).append(i)
        for same in alike.values():
            res = _chip_sums([chunks[i] for i in same], [from_sibling[i] for i in same], core_arr,
                             name="chip_sums_" + names[same[0]])
            for i, r in zip(same, res):
                sums[i] = r
        scatters[group] = (names, _exchange_start("scatter_chips", sums, [lax.empty(s.shape, s.dtype) for s in sums],
                                                  nothing, name="scatter_chips_" + group))
        return scatters[group][1][4]

    small_sent = []

    def grads_ready(group, grads):
        if group == "tick":
            return chip_stage(grads["after"])[0:1, 0:1] if in_flight else None
        if group == "small":
            small = _finish_small_grads(grads)
            small["loss"] = 0.5 * grads["sq"][0:1, 0] / D_MODEL
            packed = _pack_small(small)
            zone = lax.empty((N_DEV, *packed.shape), F32)
            small_sent.append(_exchange_start("all_small", [packed], [zone], nothing, name="small_grads_start"))
            return small_sent[0][4][0:1, 0:1]
        names = tuple(grads)
        chunks = [_grad_chunks(n, grads[n]) for n in names]
        token = chip_stage(chunks[0]) if in_flight else nothing
        zones = [lax.empty((N_CHIPS, *c.shape[1:]), c.dtype) for c in chunks]
        started = _exchange_start("scatter_sibling", chunks, zones, token, name="scatter_sibling_" + group)
        in_flight.append((group, names, started))
        if group != "in":
            return started[4][0:1, 0:1]
        return chip_stage(update_group("mlp", started[4]))[0:1, 0:1]

    out = {}

    def update_group(group, after):
        names, started = scatters.pop(group)
        sums, lands = _exchange_wait("scatter_chips", started, after, name=f"scatter_chips_{group}_wait")
        for n, parts, own in zip(names, lands, sums):
            res = _adamw_shard(parts, own, chip_arr, _update_view(n, args[n]), _update_view(n, args["m_" + n]),
                               _update_view(n, args["v_" + n]), name="adamw_" + n)
            out[n] = [_shard_result(n, r, args[n].shape) for r in res]
            after = res[1]
        return after

    sq, grad_x, g = _local_step(x[0], x16, mem[0], loss_target[0], weights_of, grads_ready)

    after = grad_x
    for group in list(scatters):
        after = update_group(group, after)

    (packed,), (zone,) = _exchange_wait("all_small", small_sent[0], after, name="small_grads_wait")
    quantities, conv_rows, misc_row = _adamw_small(
        zone, packed, me_arr, _pack_small({n: wt[n] for n in _VECTORS}), _pack_small({n: mo[n] for n in _VECTORS}),
        _pack_small({n: vo[n] for n in _VECTORS}))
    cols = conv_w.shape[-1]
    conv_mine = lax.dynamic_slice(conv_rows.reshape(CONV_K, QKV_COLS), (0, me * cols), (CONV_K, cols))[None]
    res = _adamw_shard(conv_mine, conv_mine, jnp.zeros((1,), jnp.int32), wt["conv_w"], mo["conv_w"], vo["conv_w"],
                       name="adamw_conv_w")
    out["conv_w"] = [r.reshape(conv_w.shape) for r in res]
    for n in _VECTORS:
        out[n] = [q[n] for q in quantities]
    loss_at = dict((n, off) for n, off, _ in _MISC)["loss"]

    return (misc_row[0, loss_at], grad_x[None], *[out[n][0] for n in _WEIGHT_ORDER], *[out[n][1] for n in _WEIGHT_ORDER],
            *[out[n][2] for n in _WEIGHT_ORDER], *[out[n][3] for n in _WEIGHT_ORDER])
```

```python
import jax
import jax.numpy as jnp
from jax import lax
from jax.experimental import pallas as pl
from jax.experimental.pallas import tpu as pltpu

F32 = jnp.float32
BF16 = jnp.bfloat16
MESH = pl.DeviceIdType.MESH

N_DEV = 8
D_MODEL = 2048
GDN_WIDTH = 1024
GDN_HEADS = 8
HEAD_DIM = 128
CONV_K = 4
CHUNK = 64
POOL_GROUPS = 4
POOL_GROUP_DIM = 256
MEM_LEN = 256
XATTN_HEADS = 4
XATTN_HEAD_DIM = 512
D_FF = 8192
IN_COLS = 5136
ALPHA = 2.0 ** 0.25
LN_EPS = 1e-5
NORM_EPS = 1e-6

LANE = 128
QKV_COLS = 3 * GDN_WIDTH
Z_OFF = QKV_COLS
BA_OFF = 4 * GDN_WIDTH
POOL_OFF = BA_OFF + 2 * LANE
PROJ_COLS = POOL_OFF + GDN_WIDTH
BA_BLK = BA_OFF // LANE
POOL_BLK = POOL_OFF // POOL_GROUP_DIM

ADAM_LR = 0.001
ADAM_B1 = 0.9
ADAM_B2 = 0.999
ADAM_EPS = 1e-08
ADAM_WD = 0.01
ADAM_STEP = 10

VMEM_LIMIT_BYTES = 48 * 1024 * 1024


def _params(*sem):
    return pltpu.CompilerParams(dimension_semantics=sem if sem else None, vmem_limit_bytes=VMEM_LIMIT_BYTES)


def _make_dots(cast, precision, batched=False):
    lead = 1 if batched else 0
    batch = ((0,), (0,)) if batched else ((), ())

    def dg(a, b, ca, cb):
        if cast is not None:
            a = a.astype(cast)
            b = b.astype(cast)
        return lax.dot_general(a, b, (((ca + lead,), (cb + lead,)), batch), precision=precision, preferred_element_type=F32)

    def nn_(a, b):
        return dg(a, b, 1, 0)

    def nt_(a, b):
        return dg(a, b, 1, 1)

    def tn_(a, b):
        return dg(a, b, 0, 0)

    @jax.custom_vjp
    def nn(a, b):
        return nn_(a, b)

    nn.defvjp(lambda a, b: (nn_(a, b), (a, b)), lambda r, g: (nt_(g, r[1]), tn_(r[0], g)))

    @jax.custom_vjp
    def nt(a, b):
        return nt_(a, b)

    nt.defvjp(lambda a, b: (nt_(a, b), (a, b)), lambda r, g: (nn_(g, r[1]), tn_(g, r[0])))

    @jax.custom_vjp
    def tn(a, b):
        return tn_(a, b)

    tn.defvjp(lambda a, b: (tn_(a, b), (a, b)), lambda r, g: (nt_(r[1], g), nn_(r[0], g)))

    return (nn_, nt_, tn_), (nn, nt, tn)


_BDOT_PLAIN, _BDOT_VJP = _make_dots(BF16, None)
_BDOT_BATCH_PLAIN, _BDOT_BATCH_VJP = _make_dots(BF16, None, batched=True)
_FDOT_BATCH_PLAIN, _FDOT_BATCH_VJP = _make_dots(BF16, None, batched=True)


def _mm(a, b, *, ta=False, tb=False, out_dtype=F32, tm=None, tn=512, tk=None, epi=None, extra=None, add_scale=1.0,
        b_chunks=False, o_chunks=False, after=None, name):
    m, k = (a.shape[1], a.shape[0]) if ta else a.shape
    if b_chunks:
        n, kb = (b.shape[1], N_DEV * b.shape[2]) if tb else (N_DEV * b.shape[2], b.shape[1])
    else:
        n, kb = b.shape if tb else (b.shape[1], b.shape[0])
    assert kb == k, (name, a.shape, b.shape)
    tm, tn, tk = min(tm or m, m), min(tn, n), min(tk or k, k)
    assert m % tm == 0 and n % tn == 0 and k % tk == 0, (name, m, n, k)
    nk = k // tk
    dims = (((0 if ta else 1,), (1 if tb else 0,)), ((), ()))
    n_extra = 0 if epi in (None, "relu2") else 1
    n_out = 2 if epi == "relu2" else 1
    if epi in ("relu2", "mul2r"):
        out_dtype = BF16
    n_after = 0 if after is None else 1

    def body(*refs):
        a_ref, b_ref = refs[:2]
        c_ref = refs[2] if n_extra else None
        o_refs = refs[2 + n_extra + n_after:2 + n_extra + n_after + n_out]
        scr = refs[2 + n_extra + n_after + n_out:]
        r = lax.dot_general(a_ref[...].astype(BF16), b_ref[...].astype(BF16), dims, preferred_element_type=F32)

        def finish(v):
            if epi == "add":
                o_refs[0][...] = (v + add_scale * c_ref[...]).astype(out_dtype)
            elif epi == "relu2":
                p = jnp.maximum(v, 0.0)
                o_refs[0][...] = (p * p).astype(BF16)
                o_refs[1][...] = p.astype(BF16)
            elif epi == "mul2r":
                o_refs[0][...] = (v * (2.0 * c_ref[...].astype(F32))).astype(BF16)
            else:
                o_refs[0][...] = v.astype(out_dtype)

        if nk == 1:
            finish(r)
        else:
            acc = scr[0]
            kk = pl.program_id(2)

            @pl.when(kk == 0)
            def _():
                acc[...] = r

            @pl.when(kk > 0)
            def _():
                acc[...] += r

            @pl.when(kk == nk - 1)
            def _():
                finish(acc[...])

    a_spec = pl.BlockSpec((tk, tm), lambda i, j, kk: (kk, i)) if ta else pl.BlockSpec((tm, tk), lambda i, j, kk: (i, kk))
    if b_chunks and tb:
        kc = k // N_DEV // tk
        b_spec = pl.BlockSpec((None, tn, tk), lambda i, j, kk: (kk // kc, j, kk % kc))
    elif b_chunks:
        nc = n // N_DEV // tn
        b_spec = pl.BlockSpec((None, tk, tn), lambda i, j, kk: (j // nc, kk, j % nc))
    elif tb:
        b_spec = pl.BlockSpec((tn, tk), lambda i, j, kk: (j, kk))
    else:
        b_spec = pl.BlockSpec((tk, tn), lambda i, j, kk: (kk, j))
    mn_spec = pl.BlockSpec((tm, tn), lambda i, j, kk: (i, j))
    if o_chunks:
        oc = n // N_DEV // tn
        o_spec = pl.BlockSpec((None, tm, tn), lambda i, j, kk: (j // oc, i, j % oc))
        o_shape = jax.ShapeDtypeStruct((N_DEV, m, n // N_DEV), out_dtype)
    else:
        o_spec, o_shape = mn_spec, jax.ShapeDtypeStruct((m, n), out_dtype)
    res = pl.pallas_call(
        body, grid=(m // tm, n // tn, nk),
        in_specs=[a_spec, b_spec] + [mn_spec] * n_extra + [pl.BlockSpec(memory_space=pl.ANY)] * n_after,
        out_specs=[o_spec] * n_out, out_shape=[o_shape] * n_out,
        scratch_shapes=[pltpu.VMEM((tm, tn), F32)] if nk > 1 else [],
        compiler_params=_params("parallel", "parallel", "arbitrary"), name=name,
    )(a, b, *([extra] if n_extra else []), *([after] if n_after else []))
    return res if n_out > 1 else res[0]


def _cast_bf16(v, *, name, tm=512):
    t, d = v.shape
    tm = min(tm, t)

    def body(v_ref, o_ref):
        o_ref[...] = v_ref[...].astype(BF16)

    spec = pl.BlockSpec((tm, d), lambda i: (i, 0))
    return pl.pallas_call(body, grid=(t // tm,), in_specs=[spec], out_specs=spec,
                          out_shape=jax.ShapeDtypeStruct((t, d), BF16), compiler_params=_params("parallel"), name=name)(v)


def _shift_down(v, s):
    if s == 0:
        return v
    row = lax.broadcasted_iota(jnp.int32, v.shape, 0)
    return jnp.where(row >= s, pltpu.roll(v, s, axis=0), 0.0)


def _shift_up(v, s):
    if s == 0:
        return v
    t = v.shape[0]
    row = lax.broadcasted_iota(jnp.int32, v.shape, 0)
    return jnp.where(row < t - s, pltpu.roll(v, t - s, axis=0), 0.0)


def _post_col(j):
    return (j % GDN_HEADS) * 3 + j // GDN_HEADS


def _gdn_prep_fwd(proj, conv_w):
    t = proj.shape[0]

    def body(x_ref, w_ref, o_ref):
        j = pl.program_id(0)
        x = x_ref[...]
        y = jnp.zeros_like(x)
        for tap in range(CONV_K):
            y = y + w_ref[tap:tap + 1, :] * _shift_down(x, CONV_K - 1 - tap)
        c = y * jax.nn.sigmoid(y)
        nrm = c * lax.rsqrt(jnp.sum(c * c, axis=1, keepdims=True) + NORM_EPS)
        o_ref[...] = jnp.where(j < 2 * GDN_HEADS, nrm, c)

    return pl.pallas_call(
        body, grid=(QKV_COLS // LANE,),
        in_specs=[pl.BlockSpec((t, LANE), lambda j: (0, j)), pl.BlockSpec((CONV_K, LANE), lambda j: (0, j))],
        out_specs=pl.BlockSpec((t, LANE), lambda j: (0, _post_col(j))),
        out_shape=jax.ShapeDtypeStruct((t, QKV_COLS), F32),
        compiler_params=_params("parallel"), name="gdn_prep_fwd",
    )(proj, conv_w)


def _gdn_prep_bwd(proj, conv_w, dpost, dproj):
    t = proj.shape[0]

    def body(x_ref, w_ref, d_ref, _, dx_ref, dw_ref):
        j = pl.program_id(0)
        x = x_ref[...]
        xs = [_shift_down(x, CONV_K - 1 - tap) for tap in range(CONV_K)]
        y = jnp.zeros_like(x)
        for tap in range(CONV_K):
            y = y + w_ref[tap:tap + 1, :] * xs[tap]
        sig = jax.nn.sigmoid(y)
        c = y * sig
        r = lax.rsqrt(jnp.sum(c * c, axis=1, keepdims=True) + NORM_EPS)
        nrm = c * r
        d = d_ref[...]
        dc_norm = r * (d - nrm * jnp.sum(d * nrm, axis=1, keepdims=True))
        dc = jnp.where(j < 2 * GDN_HEADS, dc_norm, d)
        dy = dc * (sig * (1.0 + y * (1.0 - sig)))
        dx = jnp.zeros_like(x)
        for tap in range(CONV_K):
            dx = dx + _shift_up(w_ref[tap:tap + 1, :] * dy, CONV_K - 1 - tap)
            dw_ref[tap:tap + 1, :] = jnp.sum(dy * xs[tap], axis=0, keepdims=True)
        dx_ref[...] = dx.astype(dx_ref.dtype)

    return pl.pallas_call(
        body, grid=(QKV_COLS // LANE,),
        in_specs=[pl.BlockSpec((t, LANE), lambda j: (0, j)), pl.BlockSpec((CONV_K, LANE), lambda j: (0, j)),
                  pl.BlockSpec((t, LANE), lambda j: (0, _post_col(j))), pl.BlockSpec(memory_space=pl.ANY)],
        out_specs=[pl.BlockSpec((t, LANE), lambda j: (0, j)), pl.BlockSpec((CONV_K, LANE), lambda j: (0, j))],
        out_shape=[jax.ShapeDtypeStruct(dproj.shape, dproj.dtype), jax.ShapeDtypeStruct((CONV_K, QKV_COLS), F32)],
        input_output_aliases={3: 0},
        compiler_params=_params("parallel"), name="gdn_prep_bwd",
    )(proj, conv_w, dpost, dproj)


def _softplus(v):
    return jnp.maximum(v, 0.0) + jnp.log(1.0 + jnp.exp(-jnp.abs(v)))


def _tri_inv(low, nn):
    r = lax.broadcasted_iota(jnp.int32, (CHUNK, CHUNK), 0)
    c = lax.broadcasted_iota(jnp.int32, (CHUNK, CHUNK), 1)
    eye = (r == c).astype(F32)
    same_blk = lax.shift_right_logical(r, 4) == lax.shift_right_logical(c, 4)
    diag = jnp.where(same_blk, low, 0.0)
    off = low - diag
    n1 = -diag
    n2 = nn(n1, n1)
    n4 = nn(n2, n2)
    n8 = nn(n4, n4)
    inv_d = nn(nn(nn(eye + n1, eye + n2), eye + n4), eye + n8)
    m1 = nn(inv_d, off)
    m2 = nn(m1, m1)
    return nn(nn(eye - m1, eye + m2), inv_d)


@jax.custom_vjp
def _tri_inv_known(low, t_inv):
    return t_inv


def _tri_inv_known_fwd(low, t_inv):
    return t_inv, t_inv


def _tri_inv_known_bwd(t_inv, g):
    _, nt, tn = _FDOT_BATCH_PLAIN
    return -nt(tn(t_inv, g), t_inv), jnp.zeros_like(t_inv)


_tri_inv_known.defvjp(_tri_inv_known_fwd, _tri_inv_known_bwd)


LOCAL_HEADS_PER_STEP = 8


def _gdn_local_fn(qkv, ba, alog_row, dtb_row, first_head, bdots, fdots, t_known=None):
    nn, nt, tn = bdots
    fnn = fdots[0]
    n_heads = qkv.shape[1] // (3 * HEAD_DIM)
    part = lambda i, p: qkv[:, (3 * i + p) * HEAD_DIM:(3 * i + p + 1) * HEAD_DIM]
    q = jnp.stack([part(i, 0) for i in range(n_heads)]) * (HEAD_DIM ** -0.5)
    k = jnp.stack([part(i, 1) for i in range(n_heads)])
    v = jnp.stack([part(i, 2) for i in range(n_heads)])
    lane = lax.broadcasted_iota(jnp.int32, ba.shape, 1)
    bg = jnp.where(lane < GDN_HEADS, jax.nn.sigmoid(ba), -jnp.exp(alog_row) * _softplus(ba + dtb_row))
    pick = lambda l: jnp.sum(jnp.where(lane == l, bg, 0.0), axis=1, keepdims=True)
    beta = jnp.stack([pick(first_head + i) for i in range(n_heads)])
    g = jnp.stack([pick(first_head + i + GDN_HEADS) for i in range(n_heads)])

    r = lax.broadcasted_iota(jnp.int32, (CHUNK, CHUNK), 0)
    c = lax.broadcasted_iota(jnp.int32, (CHUNK, CHUNK), 1)
    incl = r >= c
    strict = r > c
    eye = r == c

    def to_row(col):
        return jnp.sum(jnp.where(eye, col, 0.0), axis=1, keepdims=True)

    gc = jnp.sum(jnp.where(incl, to_row(g), 0.0), axis=2, keepdims=True)
    diff = gc - to_row(gc)
    decay = jnp.where(incl, jnp.exp(jnp.where(incl, diff, 0.0)), 0.0)
    k_beta = k * beta
    v_beta = v * beta
    low = jnp.where(strict, nt(k_beta, k) * decay, 0.0)
    t_inv = _tri_inv(low, fnn) if t_known is None else _tri_inv_known(low, t_known)
    eg = jnp.exp(gc)
    u = fnn(t_inv, v_beta)
    w = fnn(t_inv, k_beta * eg)
    attn = jnp.where(incl, nt(q, k) * decay, 0.0)
    last = lax.broadcasted_iota(jnp.int32, (CHUNK, 1), 0) == CHUNK - 1
    g_last = jnp.sum(jnp.where(last, gc, 0.0), axis=1, keepdims=True)
    kdec = k * jnp.exp(g_last - gc)
    elast = jnp.broadcast_to(jnp.exp(g_last), (n_heads, 1, LANE))
    return u, w, q * eg, kdec, attn, elast, t_inv


def _gdn_state_fn(u, w, qg, kdec, attn, elast, state, bdots):
    nn, _, tn = bdots
    v_new = u - nn(w, state)
    o = nn(qg, state) + nn(attn, v_new)
    return o, state * elast + tn(kdec, v_new)


def _gdn_local_fwd(post, proj, alog_row, dtb_row):
    t = post.shape[0]
    n_chunks = t // CHUNK
    hb = LOCAL_HEADS_PER_STEP

    def body(qkv_ref, ba_ref, al_ref, dt_ref, u_ref, w_ref, qg_ref, kd_ref, at_ref, el_ref, ti_ref):
        u, w, qg, kdec, attn, elast, t_inv = _gdn_local_fn(qkv_ref[...], ba_ref[...], al_ref[...], dt_ref[...],
                                                           pl.program_id(1) * hb, _BDOT_BATCH_PLAIN, _FDOT_BATCH_PLAIN)
        for i in range(hb):
            cols = slice(i * HEAD_DIM, (i + 1) * HEAD_DIM)
            u_ref[:, cols] = u[i]
            w_ref[:, cols] = w[i].astype(BF16)
            qg_ref[:, cols] = qg[i].astype(BF16)
            kd_ref[:, cols] = kdec[i].astype(BF16)
        at_ref[...] = attn.astype(BF16)
        el_ref[:, 0] = elast
        ti_ref[...] = t_inv

    wide = pl.BlockSpec((CHUNK, hb * HEAD_DIM), lambda n, j: (n, j))
    square = pl.BlockSpec((hb, CHUNK, CHUNK), lambda n, j: (j, n, 0))
    row = pl.BlockSpec((1, LANE), lambda n, j: (0, 0))
    res = pl.pallas_call(
        body, grid=(n_chunks, GDN_HEADS // hb),
        in_specs=[pl.BlockSpec((CHUNK, hb * 3 * HEAD_DIM), lambda n, j: (n, j)),
                  pl.BlockSpec((CHUNK, LANE), lambda n, j: (n, BA_BLK)), row, row],
        out_specs=[wide, wide, wide, wide, square, pl.BlockSpec((hb, 1, 1, LANE), lambda n, j: (j, n, 0, 0)), square],
        out_shape=[jax.ShapeDtypeStruct((t, GDN_WIDTH), F32), jax.ShapeDtypeStruct((t, GDN_WIDTH), BF16),
                   jax.ShapeDtypeStruct((t, GDN_WIDTH), BF16), jax.ShapeDtypeStruct((t, GDN_WIDTH), BF16),
                   jax.ShapeDtypeStruct((GDN_HEADS, t, CHUNK), BF16),
                   jax.ShapeDtypeStruct((GDN_HEADS, n_chunks, 1, LANE), F32),
                   jax.ShapeDtypeStruct((GDN_HEADS, t, CHUNK), F32)],
        compiler_params=_params("parallel", "parallel"), name="gdn_local_fwd",
    )(post, proj, alog_row, dtb_row)
    return tuple(res[:6]), res[6]


def _by_head(ref):
    return jnp.stack([ref[:, h * HEAD_DIM:(h + 1) * HEAD_DIM] for h in range(ref.shape[1] // HEAD_DIM)])


def _gdn_state_specs(n_of):
    wide = pl.BlockSpec((CHUNK, GDN_WIDTH), lambda n: (n_of(n), 0))
    attn = pl.BlockSpec((GDN_HEADS, CHUNK, CHUNK), lambda n: (0, n_of(n), 0))
    elast = pl.BlockSpec((GDN_HEADS, 1, 1, LANE), lambda n: (0, n_of(n), 0, 0))
    saved = pl.BlockSpec((GDN_HEADS, 1, HEAD_DIM, HEAD_DIM), lambda n: (0, n_of(n), 0, 0))
    return wide, attn, elast, saved


def _gdn_state_fwd(u, w, qg, kdec, attn, elast):
    t = u.shape[0]
    n_chunks = t // CHUNK

    def body(u_ref, w_ref, qg_ref, kd_ref, at_ref, el_ref, o_ref, save_ref, state_ref):
        @pl.when(pl.program_id(0) == 0)
        def _():
            state_ref[...] = jnp.zeros_like(state_ref)

        state = state_ref[...]
        save_ref[:, 0] = state
        o, new_state = _gdn_state_fn(_by_head(u_ref), _by_head(w_ref), _by_head(qg_ref), _by_head(kd_ref), at_ref[...],
                                     el_ref[:, 0], state, _BDOT_BATCH_PLAIN)
        for h in range(GDN_HEADS):
            o_ref[:, h * HEAD_DIM:(h + 1) * HEAD_DIM] = o[h]
        state_ref[...] = new_state

    wide, attn_spec, elast_spec, saved_spec = _gdn_state_specs(lambda n: n)
    return pl.pallas_call(
        body, grid=(n_chunks,), in_specs=[wide, wide, wide, wide, attn_spec, elast_spec],
        out_specs=[wide, saved_spec],
        out_shape=[jax.ShapeDtypeStruct((t, GDN_WIDTH), F32),
                   jax.ShapeDtypeStruct((GDN_HEADS, n_chunks, HEAD_DIM, HEAD_DIM), F32)],
        scratch_shapes=[pltpu.VMEM((GDN_HEADS, HEAD_DIM, HEAD_DIM), F32)],
        compiler_params=_params("arbitrary"), name="gdn_state_fwd",
    )(u, w, qg, kdec, attn, elast)


def _gdn_state_bwd(u, w, qg, kdec, attn, elast, saved, do):
    t = u.shape[0]
    n_chunks = t // CHUNK
    last = n_chunks - 1

    def body(u_ref, w_ref, qg_ref, kd_ref, at_ref, el_ref, save_ref, do_ref,
             du_ref, dw_ref, dqg_ref, dkd_ref, dat_ref, del_ref, dstate_ref):
        @pl.when(pl.program_id(0) == 0)
        def _():
            dstate_ref[...] = jnp.zeros_like(dstate_ref)

        _, vjp = jax.vjp(
            lambda *a: _gdn_state_fn(*a, _BDOT_BATCH_VJP), _by_head(u_ref), _by_head(w_ref).astype(F32),
            _by_head(qg_ref).astype(F32), _by_head(kd_ref).astype(F32), at_ref[...].astype(F32), el_ref[:, 0],
            save_ref[:, 0])
        du, dw, dqg, dkd, dat, de, dstate = vjp((_by_head(do_ref), dstate_ref[...]))
        for h in range(GDN_HEADS):
            cols = slice(h * HEAD_DIM, (h + 1) * HEAD_DIM)
            du_ref[:, cols] = du[h]
            dw_ref[:, cols] = dw[h]
            dqg_ref[:, cols] = dqg[h]
            dkd_ref[:, cols] = dkd[h]
        dat_ref[...] = dat
        del_ref[:, 0] = de
        dstate_ref[...] = dstate

    wide, attn_spec, elast_spec, saved_spec = _gdn_state_specs(lambda n: last - n)
    wide_f32 = jax.ShapeDtypeStruct((t, GDN_WIDTH), F32)
    return pl.pallas_call(
        body, grid=(n_chunks,), in_specs=[wide, wide, wide, wide, attn_spec, elast_spec, saved_spec, wide],
        out_specs=[wide, wide, wide, wide, attn_spec, elast_spec],
        out_shape=[wide_f32, wide_f32, wide_f32, wide_f32, jax.ShapeDtypeStruct((GDN_HEADS, t, CHUNK), F32),
                   jax.ShapeDtypeStruct((GDN_HEADS, n_chunks, 1, LANE), F32)],
        scratch_shapes=[pltpu.VMEM((GDN_HEADS, HEAD_DIM, HEAD_DIM), F32)],
        compiler_params=_params("arbitrary"), name="gdn_state_bwd",
    )(u, w, qg, kdec, attn, elast, saved, do)


def _gdn_local_bwd(post, proj, alog_row, dtb_row, t_inv, cots, dproj):
    t = post.shape[0]
    n_chunks = t // CHUNK
    hb = LOCAL_HEADS_PER_STEP
    n_steps = GDN_HEADS // hb

    def body(qkv_ref, ba_ref, al_ref, dt_ref, ti_ref, du_ref, dw_ref, dqg_ref, dkd_ref, dat_ref, del_ref, _,
             dqkv_ref, dba_ref, dal_ref, ddt_ref, dba_acc):
        n = pl.program_id(0)
        j = pl.program_id(1)

        @pl.when((n == 0) & (j == 0))
        def _():
            dal_ref[...] = jnp.zeros_like(dal_ref)
            ddt_ref[...] = jnp.zeros_like(ddt_ref)

        @pl.when(j == 0)
        def _():
            dba_acc[...] = jnp.zeros_like(dba_acc)

        t_known = ti_ref[...]
        _, vjp = jax.vjp(
            lambda a, b, c, d: _gdn_local_fn(a, b, c, d, j * hb, _BDOT_BATCH_VJP, _FDOT_BATCH_VJP, t_known)[:6],
            qkv_ref[...], ba_ref[...], al_ref[...], dt_ref[...])
        dqkv, dba, dal, ddt = vjp((_by_head(du_ref), _by_head(dw_ref), _by_head(dqg_ref), _by_head(dkd_ref), dat_ref[...],
                                   del_ref[:, 0]))
        dqkv_ref[...] = dqkv
        dba_acc[...] += dba
        dal_ref[...] += dal
        ddt_ref[...] += ddt

        @pl.when(j == n_steps - 1)
        def _():
            dba_ref[:, 0:LANE] = dba_acc[...].astype(dba_ref.dtype)
            dba_ref[:, LANE:2 * LANE] = jnp.zeros((CHUNK, LANE), dba_ref.dtype)

    wide = pl.BlockSpec((CHUNK, hb * HEAD_DIM), lambda n, j: (n, j))
    qkv_spec = pl.BlockSpec((CHUNK, hb * 3 * HEAD_DIM), lambda n, j: (n, j))
    row = pl.BlockSpec((1, LANE), lambda n, j: (0, 0))
    return pl.pallas_call(
        body, grid=(n_chunks, n_steps),
        in_specs=[qkv_spec, pl.BlockSpec((CHUNK, LANE), lambda n, j: (n, BA_BLK)), row, row,
                  pl.BlockSpec((hb, CHUNK, CHUNK), lambda n, j: (j, n, 0)), wide, wide, wide, wide,
                  pl.BlockSpec((hb, CHUNK, CHUNK), lambda n, j: (j, n, 0)),
                  pl.BlockSpec((hb, 1, 1, LANE), lambda n, j: (j, n, 0, 0)), pl.BlockSpec(memory_space=pl.ANY)],
        out_specs=[qkv_spec, pl.BlockSpec((CHUNK, 2 * LANE), lambda n, j: (n, BA_BLK // 2)), row, row],
        out_shape=[jax.ShapeDtypeStruct((t, QKV_COLS), F32), jax.ShapeDtypeStruct(dproj.shape, dproj.dtype),
                   jax.ShapeDtypeStruct((1, LANE), F32), jax.ShapeDtypeStruct((1, LANE), F32)],
        input_output_aliases={11: 1},
        scratch_shapes=[pltpu.VMEM((CHUNK, LANE), F32)],
        compiler_params=_params("arbitrary", "arbitrary"), name="gdn_local_bwd",
    )(post, proj, alog_row, dtb_row, t_inv, *cots, dproj)


def _onorm_fn(o, z, w):
    return o * lax.rsqrt(jnp.mean(o * o, axis=1, keepdims=True) + NORM_EPS) * w * (z * jax.nn.sigmoid(z))


_Z_WIDE_BLK = Z_OFF // GDN_WIDTH


def _onorm_fwd(o_raw, proj, norm_w, mixin, tm=256):
    t = o_raw.shape[0]
    tm = min(tm, t)

    def body(o_ref, z_ref, w_ref, _, out_ref):
        for h in range(GDN_HEADS):
            cols = slice(h * HEAD_DIM, (h + 1) * HEAD_DIM)
            out_ref[:, cols] = _onorm_fn(o_ref[:, cols], z_ref[:, cols], w_ref[...]).astype(out_ref.dtype)

    wide = pl.BlockSpec((tm, GDN_WIDTH), lambda i: (i, 0))
    return pl.pallas_call(
        body, grid=(t // tm,),
        in_specs=[wide, pl.BlockSpec((tm, GDN_WIDTH), lambda i: (i, _Z_WIDE_BLK)), pl.BlockSpec((1, LANE), lambda i: (0, 0)),
                  pl.BlockSpec(memory_space=pl.ANY)],
        out_specs=wide, out_shape=jax.ShapeDtypeStruct(mixin.shape, mixin.dtype), input_output_aliases={3: 0},
        compiler_params=_params("parallel"), name="gdn_onorm_fwd",
    )(o_raw, proj, norm_w, mixin)


def _onorm_bwd(o_raw, proj, norm_w, dmixin, dproj, tm=256):
    t = o_raw.shape[0]
    tm = min(tm, t)

    def body(o_ref, z_ref, w_ref, d_ref, _, do_ref, dz_ref, dw_ref):
        @pl.when(pl.program_id(0) == 0)
        def _():
            dw_ref[...] = jnp.zeros_like(dw_ref)

        for h in range(GDN_HEADS):
            cols = slice(h * HEAD_DIM, (h + 1) * HEAD_DIM)
            _, vjp = jax.vjp(_onorm_fn, o_ref[:, cols], z_ref[:, cols], w_ref[...])
            do, dz, dw = vjp(d_ref[:, cols])
            do_ref[:, cols] = do
            dz_ref[:, cols] = dz.astype(dz_ref.dtype)
            dw_ref[...] += dw

    wide = pl.BlockSpec((tm, GDN_WIDTH), lambda i: (i, 0))
    gate = pl.BlockSpec((tm, GDN_WIDTH), lambda i: (i, _Z_WIDE_BLK))
    row = pl.BlockSpec((1, LANE), lambda i: (0, 0))
    return pl.pallas_call(
        body, grid=(t // tm,), in_specs=[wide, gate, row, wide, pl.BlockSpec(memory_space=pl.ANY)],
        out_specs=[wide, gate, row],
        out_shape=[jax.ShapeDtypeStruct((t, GDN_WIDTH), F32), jax.ShapeDtypeStruct(dproj.shape, dproj.dtype),
                   jax.ShapeDtypeStruct((1, LANE), F32)],
        input_output_aliases={4: 1},
        compiler_params=_params("arbitrary"), name="gdn_onorm_bwd",
    )(o_raw, proj, norm_w, dmixin, dproj)


def _pool_select(levels, gi):
    out = levels[-1]
    for lvl in range(len(levels) - 2, -1, -1):
        out = jnp.where(gi == lvl, levels[lvl], out)
    return out


def _pool_count(shape, gi):
    pos = lax.broadcasted_iota(jnp.int32, shape, 0)
    win = lax.shift_left(jnp.int32(2), gi)
    return jnp.minimum(pos + 1, win).astype(F32)


def _pooled(p, gi):
    acc = p
    levels = []
    for lvl in range(POOL_GROUPS):
        acc = acc + _shift_down(acc, 1 << lvl)
        levels.append(acc)
    return _pool_select(levels, gi) / _pool_count(p.shape, gi) - p


def _pool_fwd(proj, pool_w, pool_scale):
    t = proj.shape[0]

    def body(p_ref, w_ref, s_ref, out_ref):
        gi = pl.program_id(0)
        pooled = _pooled(p_ref[...], gi)
        out_ref[...] = (_BDOT_PLAIN[0](pooled, w_ref[0]) * s_ref[0]).astype(out_ref.dtype)

    return pl.pallas_call(
        body, grid=(POOL_GROUPS,),
        in_specs=[pl.BlockSpec((t, POOL_GROUP_DIM), lambda g: (0, POOL_BLK + g)),
                  pl.BlockSpec((1, POOL_GROUP_DIM, POOL_GROUP_DIM), lambda g: (g, 0, 0)),
                  pl.BlockSpec((1, 1, POOL_GROUP_DIM), lambda g: (g, 0, 0))],
        out_specs=pl.BlockSpec((t, POOL_GROUP_DIM), lambda g: (0, GDN_WIDTH // POOL_GROUP_DIM + g)),
        out_shape=jax.ShapeDtypeStruct((t, 2 * GDN_WIDTH), BF16),
        compiler_params=_params("parallel"), name="pool_fwd",
    )(proj, pool_w, pool_scale)


def _pool_bwd(proj, pool_w, pool_scale, dmixin):
    t = proj.shape[0]
    nn, nt, tn = _BDOT_PLAIN

    def body(p_ref, w_ref, s_ref, d_ref, dp_ref, dw_ref, ds_ref):
        gi = pl.program_id(0)
        p = p_ref[...]
        pooled = _pooled(p, gi)
        mixed = nn(pooled, w_ref[0])
        d = d_ref[...]
        ds_ref[0] = jnp.sum(d * mixed, axis=0, keepdims=True)
        dmixed = d * s_ref[0]
        dw_ref[0] = tn(pooled, dmixed)
        dpooled = nt(dmixed, w_ref[0])
        acc = dpooled / _pool_count(p.shape, gi)
        levels = []
        for lvl in range(POOL_GROUPS):
            acc = acc + _shift_up(acc, 1 << lvl)
            levels.append(acc)
        dp_ref[...] = (_pool_select(levels, gi) - dpooled).astype(dp_ref.dtype)

    return pl.pallas_call(
        body, grid=(POOL_GROUPS,),
        in_specs=[pl.BlockSpec((t, POOL_GROUP_DIM), lambda g: (0, POOL_BLK + g)),
                  pl.BlockSpec((1, POOL_GROUP_DIM, POOL_GROUP_DIM), lambda g: (g, 0, 0)),
                  pl.BlockSpec((1, 1, POOL_GROUP_DIM), lambda g: (g, 0, 0)),
                  pl.BlockSpec((t, POOL_GROUP_DIM), lambda g: (0, GDN_WIDTH // POOL_GROUP_DIM + g))],
        out_specs=[pl.BlockSpec((t, POOL_GROUP_DIM), lambda g: (0, POOL_BLK + g)),
                   pl.BlockSpec((1, POOL_GROUP_DIM, POOL_GROUP_DIM), lambda g: (g, 0, 0)),
                   pl.BlockSpec((1, 1, POOL_GROUP_DIM), lambda g: (g, 0, 0))],
        out_shape=[jax.ShapeDtypeStruct((t, PROJ_COLS), BF16),
                   jax.ShapeDtypeStruct((POOL_GROUPS, POOL_GROUP_DIM, POOL_GROUP_DIM), F32),
                   jax.ShapeDtypeStruct((POOL_GROUPS, 1, POOL_GROUP_DIM), F32)],
        compiler_params=_params("parallel"), name="pool_bwd",
    )(proj, pool_w, pool_scale, dmixin)


def _ln_stats(s):
    mu = jnp.mean(s, axis=1, keepdims=True)
    xc = s - mu
    var = jnp.mean(xc * xc, axis=1, keepdims=True)
    rstd = lax.rsqrt(var + LN_EPS)
    return xc * rstd, rstd


def _mm_ln(a, b, h_in, g, bias, *, name, tm=256):
    t, k = a.shape
    d = b.shape[1]
    tm = min(tm, t)

    def body(a_ref, b_ref, h_ref, g_ref, bias_ref, y_ref, o_ref, o16_ref):
        y = jnp.dot(a_ref[...].astype(BF16), b_ref[...], preferred_element_type=F32)
        y_ref[...] = y
        xhat, _ = _ln_stats(ALPHA * h_ref[...] + y)
        out = xhat * g_ref[...] + bias_ref[...]
        o_ref[...] = out
        o16_ref[...] = out.astype(BF16)

    row = pl.BlockSpec((tm, d), lambda i: (i, 0))
    vec = pl.BlockSpec((1, d), lambda i: (0, 0))
    return pl.pallas_call(
        body, grid=(t // tm,),
        in_specs=[pl.BlockSpec((tm, k), lambda i: (i, 0)), pl.BlockSpec((k, d), lambda i: (0, 0)), row, vec, vec],
        out_specs=[row, row, row],
        out_shape=[jax.ShapeDtypeStruct((t, d), F32), jax.ShapeDtypeStruct((t, d), F32), jax.ShapeDtypeStruct((t, d), BF16)],
        compiler_params=_params("parallel"), name=name,
    )(a, b, h_in, g, bias)


def _ln_backward(xhat, rstd, dout, gain):
    dxhat = dout * gain
    m1 = jnp.mean(dxhat, axis=1, keepdims=True)
    m2 = jnp.mean(dxhat * xhat, axis=1, keepdims=True)
    return (rstd * (dxhat - m1 - xhat * m2), jnp.sum(dout * xhat, axis=0, keepdims=True),
            jnp.sum(dout, axis=0, keepdims=True))


def _ln_loss(h_in, y, g, b, target, *, name, tm=256):
    t, d = h_in.shape
    tm = min(tm, t)

    def body(h_ref, y_ref, g_ref, b_ref, t_ref, sq_ref, ds_ref, ds16_ref, dg_ref, dbias_ref):
        @pl.when(pl.program_id(0) == 0)
        def _():
            sq_ref[...] = jnp.zeros_like(sq_ref)
            dg_ref[...] = jnp.zeros_like(dg_ref)
            dbias_ref[...] = jnp.zeros_like(dbias_ref)

        xhat, rstd = _ln_stats(ALPHA * h_ref[...] + y_ref[...])
        err = xhat * g_ref[...] + b_ref[...] - t_ref[...]
        sq_ref[...] += jnp.sum(jnp.sum(err * err, axis=1, keepdims=True), axis=0, keepdims=True)
        ds, dg, dbias = _ln_backward(xhat, rstd, err * (1.0 / d), g_ref[...])
        ds_ref[...] = ds
        ds16_ref[...] = ds.astype(BF16)
        dg_ref[...] += dg
        dbias_ref[...] += dbias

    row = pl.BlockSpec((tm, d), lambda i: (i, 0))
    vec = pl.BlockSpec((1, d), lambda i: (0, 0))
    return pl.pallas_call(
        body, grid=(t // tm,), in_specs=[row, row, vec, vec, row],
        out_specs=[pl.BlockSpec((1, LANE), lambda i: (0, 0)), row, row, vec, vec],
        out_shape=[jax.ShapeDtypeStruct((1, LANE), F32), jax.ShapeDtypeStruct((t, d), F32),
                   jax.ShapeDtypeStruct((t, d), BF16), jax.ShapeDtypeStruct((1, d), F32), jax.ShapeDtypeStruct((1, d), F32)],
        compiler_params=_params("arbitrary"), name=name,
    )(h_in, y, g, b, target)


def _ln_bwd(h_in, y, g, d_a, d_b, *, name, tm=256):
    t, d = h_in.shape
    tm = min(tm, t)
    has_b = d_b is not None

    def body(*refs):
        if has_b:
            h_ref, y_ref, g_ref, da_ref, db_ref, ds_ref, ds16_ref, dg_ref, dbias_ref = refs
        else:
            h_ref, y_ref, g_ref, da_ref, ds_ref, ds16_ref, dg_ref, dbias_ref = refs

        @pl.when(pl.program_id(0) == 0)
        def _():
            dg_ref[...] = jnp.zeros_like(dg_ref)
            dbias_ref[...] = jnp.zeros_like(dbias_ref)

        xhat, rstd = _ln_stats(ALPHA * h_ref[...] + y_ref[...])
        dout = da_ref[...]
        if has_b:
            dout = dout + ALPHA * db_ref[...]
        ds, dg, dbias = _ln_backward(xhat, rstd, dout, g_ref[...])
        ds_ref[...] = ds
        ds16_ref[...] = ds.astype(BF16)
        dg_ref[...] += dg
        dbias_ref[...] += dbias

    row = pl.BlockSpec((tm, d), lambda i: (i, 0))
    vec = pl.BlockSpec((1, d), lambda i: (0, 0))
    args = [h_in, y, g, d_a] + ([d_b] if has_b else [])
    return pl.pallas_call(
        body, grid=(t // tm,), in_specs=[row, row, vec, row] + ([row] if has_b else []),
        out_specs=[row, row, vec, vec],
        out_shape=[jax.ShapeDtypeStruct((t, d), F32), jax.ShapeDtypeStruct((t, d), BF16),
                   jax.ShapeDtypeStruct((1, d), F32), jax.ShapeDtypeStruct((1, d), F32)],
        compiler_params=_params("arbitrary"), name=name,
    )(*args)


def _attn_fn(q, k, v, dots):
    nn, nt, _ = dots
    s = nt(q, k) * (XATTN_HEAD_DIM ** -0.5)
    s = s - lax.stop_gradient(jnp.max(s, axis=1, keepdims=True))
    e = jnp.exp(s)
    p = e / jnp.sum(e, axis=1, keepdims=True)
    return nn(p, v)


def _attn_fwd(q, k, v, tq=2048):
    t = q.shape[0]
    tq = min(tq, t)

    def body(q_ref, k_ref, v_ref, o_ref):
        o_ref[...] = _attn_fn(q_ref[...], k_ref[...], v_ref[...], _BDOT_PLAIN).astype(BF16)

    qs = pl.BlockSpec((tq, XATTN_HEAD_DIM), lambda h, i: (i, h))
    ks = pl.BlockSpec((MEM_LEN, XATTN_HEAD_DIM), lambda h, i: (0, h))
    return pl.pallas_call(
        body, grid=(XATTN_HEADS, t // tq), in_specs=[qs, ks, ks], out_specs=qs,
        out_shape=jax.ShapeDtypeStruct(q.shape, BF16), compiler_params=_params("parallel", "parallel"), name="xattn_fwd",
    )(q, k, v)


def _attn_bwd(q, k, v, do, tq=1024):
    t = q.shape[0]
    tq = min(tq, t)

    def body(q_ref, k_ref, v_ref, do_ref, dq_ref, dk_ref, dv_ref):
        @pl.when(pl.program_id(1) == 0)
        def _():
            dk_ref[...] = jnp.zeros_like(dk_ref)
            dv_ref[...] = jnp.zeros_like(dv_ref)

        _, vjp = jax.vjp(lambda a, b, c: _attn_fn(a, b, c, _BDOT_VJP), q_ref[...].astype(F32), k_ref[...].astype(F32),
                         v_ref[...].astype(F32))
        dq, dk, dv = vjp(do_ref[...].astype(F32))
        dq_ref[...] = dq.astype(BF16)
        dk_ref[...] += dk
        dv_ref[...] += dv

    qs = pl.BlockSpec((tq, XATTN_HEAD_DIM), lambda h, i: (i, h))
    ks = pl.BlockSpec((MEM_LEN, XATTN_HEAD_DIM), lambda h, i: (0, h))
    return pl.pallas_call(
        body, grid=(XATTN_HEADS, t // tq), in_specs=[qs, ks, ks, qs], out_specs=[qs, ks, ks],
        out_shape=[jax.ShapeDtypeStruct(q.shape, BF16), jax.ShapeDtypeStruct(k.shape, F32), jax.ShapeDtypeStruct(v.shape, F32)],
        compiler_params=_params("parallel", "arbitrary"), name="xattn_bwd",
    )(q, k, v, do)


def _local_step(x, x16, mem, target, weights_of, grads_ready):
    def behind(vec, token):
        return vec if token is None else vec + token

    w = dict(weights_of("mixer", None))
    proj = _mm(x16, w["w_in"], tb=True, tn=768, name="mm_in_proj")
    mixin = _pool_fwd(proj, w["pool_w"], w["pool_scale"])
    post = _gdn_prep_fwd(proj, w["conv_w"])
    token = weights_of("ahead_conv", post)
    chunked, t_inv = _gdn_local_fwd(post, proj, behind(w["alog_row"], token), w["dtb_row"])
    o_raw, saved = _gdn_state_fwd(*chunked)
    token = weights_of("ahead_scan", o_raw)
    mixin = _onorm_fwd(o_raw, proj, behind(w["gdn_norm_w"], token), mixin)
    w.update(weights_of("attn", mixin))
    mix, h1, h1_16 = _mm_ln(mixin, w["w_out"], x, w["ln1_g"], w["ln1_b"], name="mm_out_proj_ln1")
    xq = _mm(h1_16, w["xq_w"], out_dtype=BF16, name="mm_xq")
    xk = _mm(mem, w["xk_w"], out_dtype=BF16, name="mm_xk")
    xv = _mm(mem, w["xv_w"], out_dtype=BF16, name="mm_xv")
    xo = _attn_fwd(xq, xk, xv)
    token = weights_of("ahead_attn", xo)
    if token is not None:
        xo, _ = lax.optimization_barrier((xo, token))
    xa, h2, h2_16 = _mm_ln(xo, w["xo_w"], h1, w["ln2_g"], w["ln2_b"], name="mm_xo_ln2")
    w.update(weights_of("up", h2_16))
    act, relu = _mm(h2_16, w["w_up"], b_chunks=True, epi="relu2", name="mm_up")
    w.update(weights_of("down", act))
    ff = _mm(act, w["w_down"], tn=512, tk=2048, name="mm_down")
    g = {}
    sq, ds3, ds3_16, g["ln3_g"], g["ln3_b"] = _ln_loss(h2, ff, w["ln3_g"], w["ln3_b"], target, name="ln3_loss")

    gw_down = _mm(act, ds3_16, ta=True, out_dtype=BF16, tm=512, tn=D_MODEL, name="mm_gw_down")
    du = _mm(ds3_16, w["w_down"], tb=True, epi="mul2r", extra=relu, name="mm_du")
    gw_up = _mm(h2_16, du, ta=True, out_dtype=BF16, o_chunks=True, name="mm_gw_up")
    token = grads_ready("mlp", {"w_down": gw_down, "w_up": gw_up})
    dh2 = _mm(du, w["w_up"], tb=True, b_chunks=True, tn=1024, tk=1024, name="mm_dh2")
    ds2, ds2_16, g["ln2_g"], g["ln2_b"] = _ln_bwd(h1, xa, behind(w["ln2_g"], token), dh2, ds3, name="ln2_bwd")
    gw_xo = _mm(xo, ds2_16, ta=True, out_dtype=BF16, name="mm_gw_xo")
    dxo = _mm(ds2_16, w["xo_w"], tb=True, out_dtype=BF16, name="mm_dxo")
    dxq, dxk, dxv = _attn_bwd(xq, xk, xv, dxo)
    gw_xq = _mm(h1_16, dxq, ta=True, out_dtype=BF16, name="mm_gw_xq")
    gw_xk = _mm(mem, dxk, ta=True, out_dtype=BF16, name="mm_gw_xk")
    gw_xv = _mm(mem, dxv, ta=True, out_dtype=BF16, name="mm_gw_xv")
    token = grads_ready("attn", {"xo_w": gw_xo, "xq_w": gw_xq, "xk_w": gw_xk, "xv_w": gw_xv})
    dh1 = _mm(dxq, w["xq_w"], tb=True, name="mm_dh1")
    ds1, ds1_16, g["ln1_g"], g["ln1_b"] = _ln_bwd(x, mix, behind(w["ln1_g"], token), dh1, ds2, name="ln1_bwd")
    gw_out = _mm(mixin, ds1_16, ta=True, out_dtype=BF16, name="mm_gw_out")
    dmixin = _mm(ds1_16, w["w_out"], tb=True, name="mm_dmixin")
    dproj, gw_pool, g["pool_scale"] = _pool_bwd(proj, w["pool_w"], w["pool_scale"], dmixin)
    token = grads_ready("mix", {"w_out": gw_out, "pool_w": gw_pool})
    do_raw, dproj, g["gdn_norm_w"] = _onorm_bwd(o_raw, proj, behind(w["gdn_norm_w"], token), dmixin, dproj)
    cots = _gdn_state_bwd(*chunked, saved, do_raw)
    token = grads_ready("tick", {"after": cots[0]})
    dpost, dproj, g["alog_row"], g["dtb_row"] = _gdn_local_bwd(post, proj, behind(w["alog_row"], token), w["dtb_row"],
                                                               t_inv, cots, dproj)
    dproj, g["conv_w"] = _gdn_prep_bwd(proj, w["conv_w"], dpost, dproj)
    token = grads_ready("small", {**g, "sq": sq})
    gw_in = _mm(dproj, x16, ta=True, out_dtype=BF16, tm=768, tn=D_MODEL, after=token, name="mm_gw_in")
    token = grads_ready("in", {"w_in": gw_in})
    grad_x = _mm(dproj, w["w_in"], tk=1792, epi="add", extra=ds1, add_scale=ALPHA, after=token, name="mm_dx")
    return sq, grad_x, g


_VECTORS = ("a_log", "dt_bias", "gdn_norm_w", "pool_scale", "ln1_g", "ln1_b", "ln2_g", "ln2_b", "ln3_g", "ln3_b")
_BA_SPLIT = BA_OFF + 2 * GDN_HEADS


def _lane_row(v, offset):
    return jnp.zeros((1, LANE), F32).at[0, offset:offset + v.shape[0]].set(v)


_GROUP_VECTORS = {"mixer": (), "attn": ("ln1_g", "ln1_b", "ln2_g", "ln2_b"), "up": (), "down": ("ln3_g", "ln3_b")}


def _group_weights(group, full):
    w = {n: full[n].reshape(1, D_MODEL) for n in _GROUP_VECTORS[group]}
    if group == "mixer":
        w.update({
            "w_in": _w_in_padded(full["w_in"]),
            "conv_w": full["conv_w"],
            "alog_row": _lane_row(full["a_log"], GDN_HEADS),
            "dtb_row": _lane_row(full["dt_bias"], GDN_HEADS),
            "gdn_norm_w": full["gdn_norm_w"].reshape(1, LANE),
            "pool_w": full["pool_w"],
            "pool_scale": full["pool_scale"].reshape(POOL_GROUPS, 1, POOL_GROUP_DIM),
        })
    else:
        w.update({n: full[n] for n in dict(_GATHER_GROUPS)[group]})
    return w


def _w_in_row_map():
    per = IN_COLS // N_DEV
    gap = POOL_OFF - _BA_SPLIT
    pieces = []
    for d in range(N_DEV):
        lo, hi = d * per, (d + 1) * per
        if hi <= _BA_SPLIT:
            pieces.append([(0, lo, per)])
        elif lo >= _BA_SPLIT:
            pieces.append([(0, lo + gap, per)])
        else:
            pieces.append([(0, lo, _BA_SPLIT - lo), (_BA_SPLIT - lo, POOL_OFF, hi - _BA_SPLIT)])
    return pieces


_W_IN_LANES = 256


def _w_in_padded(blocks):
    def body(b_ref, o_ref):
        for d, pieces in enumerate(_w_in_row_map()):
            for src, dst, rows in pieces:
                o_ref[dst:dst + rows, :] = b_ref[d, src:src + rows, :]
        o_ref[_BA_SPLIT:POOL_OFF, :] = jnp.zeros((POOL_OFF - _BA_SPLIT, _W_IN_LANES), o_ref.dtype)

    n, per, cols = blocks.shape
    return pl.pallas_call(
        body, grid=(cols // _W_IN_LANES,), in_specs=[pl.BlockSpec((n, per, _W_IN_LANES), lambda j: (0, 0, j))],
        out_specs=pl.BlockSpec((PROJ_COLS, _W_IN_LANES), lambda j: (0, j)),
        out_shape=jax.ShapeDtypeStruct((PROJ_COLS, cols), blocks.dtype), compiler_params=_params("parallel"),
        name="w_in_padded")(blocks)


def _w_in_chunks(g):
    def body(g_ref, o_ref):
        for d, pieces in enumerate(_w_in_row_map()):
            for dst, src, rows in pieces:
                o_ref[d, dst:dst + rows, :] = g_ref[src:src + rows, :]

    cols = g.shape[1]
    per = IN_COLS // N_DEV
    return pl.pallas_call(
        body, grid=(cols // _W_IN_LANES,), in_specs=[pl.BlockSpec((PROJ_COLS, _W_IN_LANES), lambda j: (0, j))],
        out_specs=pl.BlockSpec((N_DEV, per, _W_IN_LANES), lambda j: (0, 0, j)),
        out_shape=jax.ShapeDtypeStruct((N_DEV, per, cols), g.dtype), compiler_params=_params("parallel"),
        name="w_in_chunks")(g)


def _finish_small_grads(g):
    out = {"conv_w": g["conv_w"]}
    out["a_log"] = g["alog_row"][0, GDN_HEADS:2 * GDN_HEADS]
    out["dt_bias"] = g["dtb_row"][0, GDN_HEADS:2 * GDN_HEADS]
    out["gdn_norm_w"] = g["gdn_norm_w"].reshape(LANE)
    out["pool_scale"] = g["pool_scale"].reshape(POOL_GROUPS * POOL_GROUP_DIM)
    for n in ("ln1_g", "ln1_b", "ln2_g", "ln2_b", "ln3_g", "ln3_b"):
        out[n] = g[n].reshape(D_MODEL)
    return out


def _adamw_math(w, g, m, v):
    m = ADAM_B1 * m + (1.0 - ADAM_B1) * g
    v = ADAM_B2 * v + (1.0 - ADAM_B2) * (g * g)
    m_hat = m / (1.0 - ADAM_B1 ** ADAM_STEP)
    v_hat = v / (1.0 - ADAM_B2 ** ADAM_STEP)
    delta = -ADAM_LR * (m_hat / (jnp.sqrt(v_hat) + ADAM_EPS) + ADAM_WD * w)
    return delta, m, v


ADAMW_TILE_ELEMS = 256 * 1024
CHIP_SUM_TILE_ELEMS = 1024 * 1024


def _shard_tile(r, c, elems):
    for rows in (1024, 512, 256, 128):
        if r % rows == 0 and rows * c <= elems:
            return rows, c
    if r % 128 == 0:
        return 128, c
    for cols in (2048, 1024, 512):
        if c % cols == 0 and r * cols <= elems:
            return r, cols
    return r, 256 if c % 256 == 0 else c


def _adamw_shard(parts, own, me, w, m, v, *, name):
    s, r, c = parts.shape
    tr, tc = _shard_tile(r, c, ADAMW_TILE_ELEMS)
    assert r % tr == 0 and c % tc == 0, (name, r, c)
    unit_axis = w.ndim == 3
    at = (slice(None), 0, slice(None)) if unit_axis else Ellipsis

    def body(me_ref, p_ref, own_ref, w_ref, m_ref, v_ref, g_ref, d_ref, nm_ref, nv_ref):
        mine = own_ref[...].astype(F32)
        g = None
        for i in range(s):
            part = jnp.where(me_ref[0] == i, mine, p_ref[i].astype(F32))
            g = part if g is None else g + part
        delta, nm, nv = _adamw_math(w_ref[at], g, m_ref[at], v_ref[at])
        g_ref[at] = g
        d_ref[at] = delta
        nm_ref[at] = nm
        nv_ref[at] = nv

    if unit_axis:
        blk = pl.BlockSpec((tr, 1, tc), lambda i, j, me_ref: (i, 0, j))
        out = jax.ShapeDtypeStruct((r, 1, c), F32)
    else:
        blk = pl.BlockSpec((tr, tc), lambda i, j, me_ref: (i, j))
        out = jax.ShapeDtypeStruct((r, c), F32)
    return pl.pallas_call(
        body,
        grid_spec=pltpu.PrefetchScalarGridSpec(
            num_scalar_prefetch=1, grid=(r // tr, c // tc),
            in_specs=[pl.BlockSpec((s, tr, tc), lambda i, j, me_ref: (0, i, j)),
                      pl.BlockSpec((None, tr, tc), lambda i, j, me_ref: (me_ref[0], i, j)), blk, blk, blk],
            out_specs=[blk, blk, blk, blk]),
        out_shape=[out, out, out, out], compiler_params=_params("parallel", "parallel"), name=name,
    )(me, parts, own, w, m, v)


N_CHIPS = N_DEV // 2


def _chip_sums(chunks, from_sibling, core, *, name):
    n = len(chunks)
    _, r, c = chunks[0].shape
    assert all(a.shape == chunks[0].shape for a in chunks), name
    tr, tc = _shard_tile(r, c, CHIP_SUM_TILE_ELEMS // n)
    assert r % tr == 0 and c % tc == 0, (name, r, c)

    def body(core_ref, *refs):
        for a in range(n):
            refs[2 * n + a][...] = (refs[a][...].astype(F32) + refs[n + a][...].astype(F32)).astype(BF16)

    by_chip = pl.BlockSpec((None, tr, tc), lambda q, i, j, core_ref: (q, i, j))
    mine = pl.BlockSpec((None, tr, tc), lambda q, i, j, core_ref: (2 * q + core_ref[0], i, j))
    return pl.pallas_call(
        body,
        grid_spec=pltpu.PrefetchScalarGridSpec(
            num_scalar_prefetch=1, grid=(N_CHIPS, r // tr, c // tc),
            in_specs=[mine] * n + [by_chip] * n, out_specs=[by_chip] * n),
        out_shape=[jax.ShapeDtypeStruct((N_CHIPS, r, c), chunks[0].dtype)] * n,
        compiler_params=_params("parallel", "parallel", "parallel"), name=name,
    )(core, *chunks, *from_sibling)


def _place():
    return lax.axis_index("x"), lax.axis_index("y"), lax.axis_index("c")


def _slot(px, py, pc):
    return 4 * px + 2 * py + pc


_HBM = pl.BlockSpec(memory_space=pltpu.HBM)


_SEM = pl.BlockSpec(memory_space=pltpu.SEMAPHORE)
_ANY = pl.BlockSpec(memory_space=pl.ANY)
_EFFECT = pltpu.SideEffectType.DATAFLOW_SIDE_EFFECTING


def _peer(k, x, y, c):
    return (1 - x if k & 4 else x, 1 - y if k & 2 else y, 1 - c if k & 1 else c)


_EXCHANGE_BITS = {"gather_near": (1, 2, 4), "gather_relay": (6,), "gather_pass": (2, 4, 6),
                  "scatter_sibling": (1, 1, 1, 1), "scatter_chips": (2, 4, 6), "all_small": (1, 2, 3, 4, 5, 6, 7)}


def _exchange_copy(mode, src, land, w, i, place, send_sems, recv_sems, receiving):
    bits = _EXCHANGE_BITS[mode]
    k = bits[i]
    peer = _peer(k, *place)
    me = _slot(*place)
    if mode in ("gather_near", "all_small"):
        to, src_ref, sent_to, got_at = peer, src[w], me, _slot(*peer)
    elif mode == "gather_relay":
        x, y, c = place
        other = 1 - c
        to = (lax.bitwise_xor(x, c), lax.bitwise_xor(y, other), c)
        blk = _slot(lax.bitwise_xor(x, other), lax.bitwise_xor(y, c), c)
        src_ref, sent_to, got_at = land[w].at[blk], blk, _slot(*peer)
    elif mode == "gather_pass":
        blk = _slot(*peer)
        to, src_ref, sent_to, got_at = _peer(1, *place), land[w].at[blk], blk, _slot(*_peer(k | 1, *place))
    elif mode == "scatter_sibling":
        to, src_ref, sent_to, got_at = peer, src[w].at[2 * i + 1 - place[2]], i, i
    else:
        to, src_ref, sent_to, got_at = peer, src[w].at[_slot(*peer) // 2], me // 2, _slot(*peer) // 2
    sem = w * len(bits) + i
    return pltpu.make_async_remote_copy(
        src_ref=src_ref, dst_ref=land[w].at[got_at if receiving else sent_to], send_sem=send_sems.at[sem],
        recv_sem=recv_sems.at[sem], device_id=to, device_id_type=MESH)


def _exchange_start(mode, srcs, lands, after, *, name):
    ns, nl = len(srcs), len(lands)
    n_sem = nl * len(_EXCHANGE_BITS[mode])

    def body(*refs):
        src, land = refs[:ns], refs[ns:ns + nl]
        send_sems, recv_sems = refs[ns + nl + 1:ns + nl + 3]
        token = refs[-1]
        place = _place()
        for w in range(nl):
            for i in range(len(_EXCHANGE_BITS[mode])):
                _exchange_copy(mode, src, land, w, i, place, send_sems, recv_sems, receiving=False).start()
        token[...] = jnp.zeros_like(token)

    sems = pltpu.SemaphoreType.DMA((n_sem,))
    arrays = list(srcs) + list(lands)
    res = pl.pallas_call(
        body, name=name, in_specs=[_HBM] * (ns + nl) + [_ANY],
        out_specs=(_SEM, _SEM, *([_HBM] * (ns + nl)), pl.BlockSpec(memory_space=pltpu.VMEM)),
        out_shape=(sems, sems, *[pltpu.HBM(a.shape, a.dtype) for a in arrays], jax.ShapeDtypeStruct((8, LANE), F32)),
        input_output_aliases={i: 2 + i for i in range(ns + nl)},
        compiler_params=pltpu.CompilerParams(has_side_effects=_EFFECT),
    )(*[pltpu.with_memory_space_constraint(a, pltpu.HBM) for a in arrays], after)
    return res[0], res[1], list(res[2:2 + ns]), list(res[2 + ns:2 + ns + nl]), res[-1]


def _exchange_wait(mode, started, after, *, name):
    send_sems, recv_sems, srcs, lands, _ = started
    ns, nl = len(srcs), len(lands)

    def body(*refs):
        src, land = refs[:ns], refs[ns:ns + nl]
        send_sems, recv_sems = refs[ns + nl:ns + nl + 2]
        place = _place()
        for w in range(nl):
            for i in range(len(_EXCHANGE_BITS[mode])):
                cp = _exchange_copy(mode, src, land, w, i, place, send_sems, recv_sems, receiving=True)
                cp.wait_send()
                cp.wait_recv()

    arrays = list(srcs) + list(lands)
    res = pl.pallas_call(
        body, name=name, in_specs=[_HBM] * (ns + nl) + [_SEM, _SEM, _ANY], out_specs=[_HBM] * (ns + nl),
        out_shape=[pltpu.HBM(a.shape, a.dtype) for a in arrays],
        input_output_aliases={i: i for i in range(ns + nl)},
        compiler_params=pltpu.CompilerParams(has_side_effects=_EFFECT),
    )(*arrays, send_sems, recv_sems, after)
    return list(res[:ns]), list(res[ns:])


_LN_ROWS = ("ln1_g", "ln1_b", "ln2_g", "ln2_b", "ln3_g", "ln3_b")
_MISC_ROW = len(_LN_ROWS)
_MISC = (("pool_scale", 0, GDN_WIDTH), ("gdn_norm_w", GDN_WIDTH, HEAD_DIM), ("a_log", GDN_WIDTH + LANE, GDN_HEADS),
         ("dt_bias", GDN_WIDTH + 2 * LANE, GDN_HEADS), ("loss", GDN_WIDTH + 3 * LANE, 1))
_CONV_ROW = _MISC_ROW + 1
_CONV_ROWS = CONV_K * QKV_COLS // D_MODEL
_SMALL_ROWS = 16


def _pack_small(vals):
    pieces, at = [], 0
    for n, off, size in _MISC:
        pieces.append(jnp.zeros((off - at,), F32))
        pieces.append(vals[n].reshape(size).astype(F32) if n in vals else jnp.zeros((size,), F32))
        at = off + size
    pieces.append(jnp.zeros((D_MODEL - at,), F32))
    conv = vals["conv_w"].reshape(-1) if "conv_w" in vals else jnp.zeros((_CONV_ROWS * D_MODEL,), F32)
    tail = jnp.zeros(((_SMALL_ROWS - _CONV_ROW - _CONV_ROWS) * D_MODEL,), F32)
    flat = jnp.concatenate([vals[n].reshape(D_MODEL) for n in _LN_ROWS] + pieces + [conv, tail])
    return flat.reshape(_SMALL_ROWS, D_MODEL)


def _adamw_small(zone, mine, me, w, m, v):
    short = [(n, off, size) for n, off, size in _MISC if n != "loss"]

    def body(me_ref, z_ref, mine_ref, w_ref, m_ref, v_ref, *rest):
        outs, (g_s, d_s, nm_s, nv_s) = rest[:-4], rest[-4:]
        g = None
        for s in range(N_DEV):
            part = jnp.where(me_ref[0] == s, mine_ref[...], z_ref[s])
            g = part if g is None else g + part
        g_s[...] = g
        d_s[...], nm_s[...], nv_s[...] = _adamw_math(w_ref[...], g, m_ref[...], v_ref[...])
        k = 0
        for src in (g_s, d_s, nm_s, nv_s):
            for r in range(len(_LN_ROWS)):
                outs[k][...] = src[r:r + 1, :]
                k += 1
            for _, off, size in short:
                outs[k][...] = src[_MISC_ROW:_MISC_ROW + 1, off:off + size]
                k += 1
        outs[k][...] = g_s[_CONV_ROW:_CONV_ROW + _CONV_ROWS, :]
        outs[k + 1][...] = g_s[_MISC_ROW:_MISC_ROW + 1, :]

    rows, d = mine.shape
    per_quantity = [jax.ShapeDtypeStruct((1, D_MODEL), F32)] * len(_LN_ROWS) + [
        jax.ShapeDtypeStruct((1, size), F32) for _, _, size in short]
    out_shape = per_quantity * 4 + [jax.ShapeDtypeStruct((_CONV_ROWS, d), F32), jax.ShapeDtypeStruct((1, d), F32)]
    whole = lambda a: pl.BlockSpec(a.shape, lambda i, me_ref: (0,) * len(a.shape))
    res = pl.pallas_call(
        body,
        grid_spec=pltpu.PrefetchScalarGridSpec(
            num_scalar_prefetch=1, grid=(1,), in_specs=[whole(a) for a in (zone, mine, w, m, v)],
            out_specs=[whole(s) for s in out_shape], scratch_shapes=[pltpu.VMEM((rows, d), F32)] * 4),
        out_shape=out_shape, compiler_params=_params("arbitrary"), name="adamw_small",
    )(me, zone, mine, w, m, v)
    names = list(_LN_ROWS) + [n for n, _, _ in short]
    n_each = len(names)
    quantities = [dict(zip(names, res[q * n_each:(q + 1) * n_each])) for q in range(4)]
    return quantities, res[-2], res[-1]


_WEIGHT_ORDER = ("w_in", "conv_w", "a_log", "dt_bias", "gdn_norm_w", "pool_w", "pool_scale", "w_out", "ln1_g", "ln1_b",
                 "xq_w", "xk_w", "xv_w", "xo_w", "ln2_g", "ln2_b", "w_up", "w_down", "ln3_g", "ln3_b")


def _shard2d(name, a):
    if name == "w_in":
        return a.T
    return a.reshape(-1, a.shape[-1]) if name == "pool_w" else a


def _update_view(name, a):
    return jnp.transpose(a, (2, 0, 1)) if name == "w_in" else _shard2d(name, a[0])


def _shard_result(name, r, shape):
    return jnp.transpose(r, (1, 2, 0)) if name == "w_in" else r.reshape(shape)


def _gathered_to_full(name, gth):
    if name in ("w_up", "w_in"):
        return gth
    if name == "conv_w":
        return jnp.transpose(gth, (1, 0, 2)).reshape(gth.shape[1], N_DEV * gth.shape[2])
    if name == "pool_w":
        g4 = gth.reshape(N_DEV, POOL_GROUPS, POOL_GROUP_DIM // N_DEV, POOL_GROUP_DIM)
        return jnp.transpose(g4, (1, 0, 2, 3)).reshape(POOL_GROUPS, POOL_GROUP_DIM, POOL_GROUP_DIM)
    return gth.reshape(N_DEV * gth.shape[1], gth.shape[2])


def _full_to_chunks(name, full):
    if name == "w_up":
        return full
    if name == "pool_w":
        g4 = full.reshape(POOL_GROUPS, N_DEV, POOL_GROUP_DIM // N_DEV, POOL_GROUP_DIM)
        return jnp.transpose(g4, (1, 0, 2, 3)).reshape(N_DEV, POOL_GROUPS * POOL_GROUP_DIM // N_DEV, POOL_GROUP_DIM)
    return full.reshape(N_DEV, full.shape[0] // N_DEV, full.shape[1])


_GATHER_GROUPS = (("mixer", ("w_in", "conv_w", "pool_w")), ("attn", ("w_out", "xq_w", "xk_w", "xv_w", "xo_w")),
                  ("up", ("w_up",)), ("down", ("w_down",)))


def _grad_chunks(name, g):
    if name == "w_in":
        return _w_in_chunks(g.astype(BF16))
    return _full_to_chunks(name, g.astype(BF16))


def kernel(x, mem, w_in, conv_w, a_log, dt_bias, gdn_norm_w, pool_w, pool_scale, w_out, ln1_g, ln1_b, xq_w, xk_w, xv_w, xo_w, ln2_g, ln2_b, w_up, w_down, ln3_g, ln3_b, loss_target, m_w_in, m_conv_w, m_a_log, m_dt_bias, m_gdn_norm_w, m_pool_w, m_pool_scale, m_w_out, m_ln1_g, m_ln1_b, m_xq_w, m_xk_w, m_xv_w, m_xo_w, m_ln2_g, m_ln2_b, m_w_up, m_w_down, m_ln3_g, m_ln3_b, v_w_in, v_conv_w, v_a_log, v_dt_bias, v_gdn_norm_w, v_pool_w, v_pool_scale, v_w_out, v_ln1_g, v_ln1_b, v_xq_w, v_xk_w, v_xv_w, v_xo_w, v_ln2_g, v_ln2_b, v_w_up, v_w_down, v_ln3_g, v_ln3_b):
    args = dict(locals())
    wt = {n: args[n][0] for n in _WEIGHT_ORDER}
    mo = {n: args["m_" + n][0] for n in _WEIGHT_ORDER}
    vo = {n: args["v_" + n][0] for n in _WEIGHT_ORDER}

    me = _slot(*_place())
    me_arr = jnp.reshape(me, (1,)).astype(jnp.int32)
    nothing = jnp.zeros((8, LANE), F32)

    def landing_zones(names):
        shards = [_shard2d(n, wt[n]).astype(F32 if n == "conv_w" else BF16) for n in names]
        zones = [lax.dynamic_update_slice(lax.empty((N_DEV, *s.shape), s.dtype), s[None], (me, 0, 0)) for s in shards]
        return shards, zones

    chip_arr = jnp.reshape(me // 2, (1,)).astype(jnp.int32)
    core_arr = jnp.reshape(lax.axis_index("c"), (1,)).astype(jnp.int32)
    names_of = dict(_GATHER_GROUPS)
    gathers = {}
    prepared = {}

    def gather_near(group, after):
        shards, zones = prepared.pop(group) if group in prepared else landing_zones(names_of[group])
        gathers[group] = _exchange_start("gather_near", shards, zones, after, name="gather_near_" + group)
        return gathers[group][4]

    def gather_next(group, was, now, after):
        _, zones = _exchange_wait(was, gathers[group], after, name=f"{was}_{group}_wait")
        gathers[group] = _exchange_start(now, [], zones, nothing, name=f"{now}_{group}")
        return gathers[group][4]

    def gather_relay(group, after):
        return gather_next(group, "gather_near", "gather_relay", after)

    def gather_pass(group, after):
        return gather_next(group, "gather_relay", "gather_pass", after)

    def gathered(group, after):
        _, zones = _exchange_wait("gather_pass", gathers[group], after, name=f"gather_pass_{group}_wait")
        full = {n: _gathered_to_full(n, z) for n, z in zip(names_of[group], zones)}
        full.update({n: wt[n] for n in _VECTORS})
        return _group_weights(group, full)

    token = gather_near("mixer", nothing)
    x16 = _cast_bf16(x[0], name="cast_x")
    later = {group: landing_zones(names_of[group]) for group in ("attn", "up", "down")}
    token, x16, later = lax.optimization_barrier((token, x16, later))
    prepared.update(later)
    token = gather_pass("mixer", gather_relay("mixer", token))
    token = gather_near("attn", token)

    def weights_of(group, after):
        if group == "mixer":
            return gathered(group, token)
        if group == "ahead_conv":
            return gather_near("up", gather_relay("attn", after))[0:1, 0:1]
        if group == "ahead_scan":
            return gather_pass("attn", after)[0:1, 0:1]
        if group == "attn":
            return gathered(group, gather_near("down", gather_relay("up", after)))
        if group == "ahead_attn":
            return gather_relay("down", gather_pass("up", after))[0:1, 0:1]
        if group == "up":
            return gathered(group, gather_pass("down", after))
        return gathered(group, after)

    scatters = {}
    in_flight = []

    def chip_stage(after):
        group, names, started = in_flight.pop()
        chunks, from_sibling = _exchange_wait("scatter_sibling", started, after, name=f"scatter_sibling_{group}_wait")
        sums, alike = [None] * len(names), {}
        for i, chunk in enumerate(chunks):
            alike.setdefault(chunk.shape, []).append(i)
        for same in alike.values():
            res = _chip_sums([chunks[i] for i in same], [from_sibling[i] for i in same], core_arr,
                             name="chip_sums_" + names[same[0]])
            for i, r in zip(same, res):
                sums[i] = r
        scatters[group] = (names, _exchange_start("scatter_chips", sums, [lax.empty(s.shape, s.dtype) for s in sums],
                                                  nothing, name="scatter_chips_" + group))
        return scatters[group][1][4]

    small_sent = []

    def grads_ready(group, grads):
        if group == "tick":
            return chip_stage(grads["after"])[0:1, 0:1] if in_flight else None
        if group == "small":
            small = _finish_small_grads(grads)
            small["loss"] = 0.5 * grads["sq"][0:1, 0] / D_MODEL
            packed = _pack_small(small)
            zone = lax.empty((N_DEV, *packed.shape), F32)
            small_sent.append(_exchange_start("all_small", [packed], [zone], nothing, name="small_grads_start"))
            return small_sent[0][4][0:1, 0:1]
        names = tuple(grads)
        chunks = [_grad_chunks(n, grads[n]) for n in names]
        token = chip_stage(chunks[0]) if in_flight else nothing
        zones = [lax.empty((N_CHIPS, *c.shape[1:]), c.dtype) for c in chunks]
        started = _exchange_start("scatter_sibling", chunks, zones, token, name="scatter_sibling_" + group)
        in_flight.append((group, names, started))
        if group != "in":
            return started[4][0:1, 0:1]
        return chip_stage(update_group("attn", started[4]))[0:1, 0:1]

    out = {}

    def update_group(group, after):
        names, started = scatters.pop(group)
        sums, lands = _exchange_wait("scatter_chips", started, after, name=f"scatter_chips_{group}_wait")
        for n, parts, own in zip(names, lands, sums):
            res = _adamw_shard(parts, own, chip_arr, _update_view(n, args[n]), _update_view(n, args["m_" + n]),
                               _update_view(n, args["v_" + n]), name="adamw_" + n)
            out[n] = [_shard_result(n, r, args[n].shape) for r in res]
            after = res[1]
        return after

    sq, grad_x, g = _local_step(x[0], x16, mem[0], loss_target[0], weights_of, grads_ready)

    after = grad_x
    for group in list(scatters):
        after = update_group(group, after)

    (packed,), (zone,) = _exchange_wait("all_small", small_sent[0], after, name="small_grads_wait")
    quantities, conv_rows, misc_row = _adamw_small(
        zone, packed, me_arr, _pack_small({n: wt[n] for n in _VECTORS}), _pack_small({n: mo[n] for n in _VECTORS}),
        _pack_small({n: vo[n] for n in _VECTORS}))
    cols = conv_w.shape[-1]
    conv_mine = lax.dynamic_slice(conv_rows.reshape(CONV_K, QKV_COLS), (0, me * cols), (CONV_K, cols))[None]
    res = _adamw_shard(conv_mine, conv_mine, jnp.zeros((1,), jnp.int32), wt["conv_w"], mo["conv_w"], vo["conv_w"],
                       name="adamw_conv_w")
    out["conv_w"] = [r.reshape(conv_w.shape) for r in res]
    for n in _VECTORS:
        out[n] = [q[n] for q in quantities]
    loss_at = dict((n, off) for n, off, _ in _MISC)["loss"]

    return (misc_row[0, loss_at], grad_x[None], *[out[n][0] for n in _WEIGHT_ORDER], *[out[n][1] for n in _WEIGHT_ORDER],
            *[out[n][2] for n in _WEIGHT_ORDER], *[out[n][3] for n in _WEIGHT_ORDER])
```

```python
import jax
import jax.numpy as jnp
from jax import lax
from jax.experimental import pallas as pl
from jax.experimental.pallas import tpu as pltpu

F32 = jnp.float32
BF16 = jnp.bfloat16
MESH = pl.DeviceIdType.MESH

N_DEV = 8
D_MODEL = 2048
GDN_WIDTH = 1024
GDN_HEADS = 8
HEAD_DIM = 128
CONV_K = 4
CHUNK = 64
POOL_GROUPS = 4
POOL_GROUP_DIM = 256
MEM_LEN = 256
XATTN_HEADS = 4
XATTN_HEAD_DIM = 512
D_FF = 8192
IN_COLS = 5136
ALPHA = 2.0 ** 0.25
LN_EPS = 1e-5
NORM_EPS = 1e-6

LANE = 128
QKV_COLS = 3 * GDN_WIDTH
Z_OFF = QKV_COLS
BA_OFF = 4 * GDN_WIDTH
POOL_OFF = BA_OFF + 2 * LANE
PROJ_COLS = POOL_OFF + GDN_WIDTH
BA_BLK = BA_OFF // LANE
POOL_BLK = POOL_OFF // POOL_GROUP_DIM

ADAM_LR = 0.001
ADAM_B1 = 0.9
ADAM_B2 = 0.999
ADAM_EPS = 1e-08
ADAM_WD = 0.01
ADAM_STEP = 10

VMEM_LIMIT_BYTES = 48 * 1024 * 1024


def _params(*sem):
    return pltpu.CompilerParams(dimension_semantics=sem if sem else None, vmem_limit_bytes=VMEM_LIMIT_BYTES)


def _make_dots(cast, precision, batched=False):
    lead = 1 if batched else 0
    batch = ((0,), (0,)) if batched else ((), ())

    def dg(a, b, ca, cb):
        if cast is not None:
            a = a.astype(cast)
            b = b.astype(cast)
        return lax.dot_general(a, b, (((ca + lead,), (cb + lead,)), batch), precision=precision, preferred_element_type=F32)

    def nn_(a, b):
        return dg(a, b, 1, 0)

    def nt_(a, b):
        return dg(a, b, 1, 1)

    def tn_(a, b):
        return dg(a, b, 0, 0)

    @jax.custom_vjp
    def nn(a, b):
        return nn_(a, b)

    nn.defvjp(lambda a, b: (nn_(a, b), (a, b)), lambda r, g: (nt_(g, r[1]), tn_(r[0], g)))

    @jax.custom_vjp
    def nt(a, b):
        return nt_(a, b)

    nt.defvjp(lambda a, b: (nt_(a, b), (a, b)), lambda r, g: (nn_(g, r[1]), tn_(g, r[0])))

    @jax.custom_vjp
    def tn(a, b):
        return tn_(a, b)

    tn.defvjp(lambda a, b: (tn_(a, b), (a, b)), lambda r, g: (nt_(r[1], g), nn_(r[0], g)))

    return (nn_, nt_, tn_), (nn, nt, tn)


_BDOT_PLAIN, _BDOT_VJP = _make_dots(BF16, None)
_BDOT_BATCH_PLAIN, _BDOT_BATCH_VJP = _make_dots(BF16, None, batched=True)
_FDOT_BATCH_PLAIN, _FDOT_BATCH_VJP = _make_dots(BF16, None, batched=True)


def _mm(a, b, *, ta=False, tb=False, out_dtype=F32, tm=None, tn=512, tk=None, epi=None, extra=None, add_scale=1.0,
        b_chunks=False, o_chunks=False, after=None, name):
    m, k = (a.shape[1], a.shape[0]) if ta else a.shape
    if b_chunks:
        n, kb = (b.shape[1], N_DEV * b.shape[2]) if tb else (N_DEV * b.shape[2], b.shape[1])
    else:
        n, kb = b.shape if tb else (b.shape[1], b.shape[0])
    assert kb == k, (name, a.shape, b.shape)
    tm, tn, tk = min(tm or m, m), min(tn, n), min(tk or k, k)
    assert m % tm == 0 and n % tn == 0 and k % tk == 0, (name, m, n, k)
    nk = k // tk
    dims = (((0 if ta else 1,), (1 if tb else 0,)), ((), ()))
    n_extra = 0 if epi in (None, "relu2") else 1
    n_out = 2 if epi == "relu2" else 1
    if epi in ("relu2", "mul2r"):
        out_dtype = BF16
    n_after = 0 if after is None else 1

    def body(*refs):
        a_ref, b_ref = refs[:2]
        c_ref = refs[2] if n_extra else None
        o_refs = refs[2 + n_extra + n_after:2 + n_extra + n_after + n_out]
        scr = refs[2 + n_extra + n_after + n_out:]
        r = lax.dot_general(a_ref[...].astype(BF16), b_ref[...].astype(BF16), dims, preferred_element_type=F32)

        def finish(v):
            if epi == "add":
                o_refs[0][...] = (v + add_scale * c_ref[...]).astype(out_dtype)
            elif epi == "relu2":
                p = jnp.maximum(v, 0.0)
                o_refs[0][...] = (p * p).astype(BF16)
                o_refs[1][...] = p.astype(BF16)
            elif epi == "mul2r":
                o_refs[0][...] = (v * (2.0 * c_ref[...].astype(F32))).astype(BF16)
            else:
                o_refs[0][...] = v.astype(out_dtype)

        if nk == 1:
            finish(r)
        else:
            acc = scr[0]
            kk = pl.program_id(2)

            @pl.when(kk == 0)
            def _():
                acc[...] = r

            @pl.when(kk > 0)
            def _():
                acc[...] += r

            @pl.when(kk == nk - 1)
            def _():
                finish(acc[...])

    a_spec = pl.BlockSpec((tk, tm), lambda i, j, kk: (kk, i)) if ta else pl.BlockSpec((tm, tk), lambda i, j, kk: (i, kk))
    if b_chunks and tb:
        kc = k // N_DEV // tk
        b_spec = pl.BlockSpec((None, tn, tk), lambda i, j, kk: (kk // kc, j, kk % kc))
    elif b_chunks:
        nc = n // N_DEV // tn
        b_spec = pl.BlockSpec((None, tk, tn), lambda i, j, kk: (j // nc, kk, j % nc))
    elif tb:
        b_spec = pl.BlockSpec((tn, tk), lambda i, j, kk: (j, kk))
    else:
        b_spec = pl.BlockSpec((tk, tn), lambda i, j, kk: (kk, j))
    mn_spec = pl.BlockSpec((tm, tn), lambda i, j, kk: (i, j))
    if o_chunks:
        oc = n // N_DEV // tn
        o_spec = pl.BlockSpec((None, tm, tn), lambda i, j, kk: (j // oc, i, j % oc))
        o_shape = jax.ShapeDtypeStruct((N_DEV, m, n // N_DEV), out_dtype)
    else:
        o_spec, o_shape = mn_spec, jax.ShapeDtypeStruct((m, n), out_dtype)
    res = pl.pallas_call(
        body, grid=(m // tm, n // tn, nk),
        in_specs=[a_spec, b_spec] + [mn_spec] * n_extra + [pl.BlockSpec(memory_space=pl.ANY)] * n_after,
        out_specs=[o_spec] * n_out, out_shape=[o_shape] * n_out,
        scratch_shapes=[pltpu.VMEM((tm, tn), F32)] if nk > 1 else [],
        compiler_params=_params("parallel", "parallel", "arbitrary"), name=name,
    )(a, b, *([extra] if n_extra else []), *([after] if n_after else []))
    return res if n_out > 1 else res[0]


def _cast_bf16(v, *, name, tm=512):
    t, d = v.shape
    tm = min(tm, t)

    def body(v_ref, o_ref):
        o_ref[...] = v_ref[...].astype(BF16)

    spec = pl.BlockSpec((tm, d), lambda i: (i, 0))
    return pl.pallas_call(body, grid=(t // tm,), in_specs=[spec], out_specs=spec,
                          out_shape=jax.ShapeDtypeStruct((t, d), BF16), compiler_params=_params("parallel"), name=name)(v)


def _shift_down(v, s):
    if s == 0:
        return v
    row = lax.broadcasted_iota(jnp.int32, v.shape, 0)
    return jnp.where(row >= s, pltpu.roll(v, s, axis=0), 0.0)


def _shift_up(v, s):
    if s == 0:
        return v
    t = v.shape[0]
    row = lax.broadcasted_iota(jnp.int32, v.shape, 0)
    return jnp.where(row < t - s, pltpu.roll(v, t - s, axis=0), 0.0)


def _post_col(j):
    return (j % GDN_HEADS) * 3 + j // GDN_HEADS


def _gdn_prep_fwd(proj, conv_w):
    t = proj.shape[0]

    def body(x_ref, w_ref, o_ref):
        j = pl.program_id(0)
        x = x_ref[...]
        y = jnp.zeros_like(x)
        for tap in range(CONV_K):
            y = y + w_ref[tap:tap + 1, :] * _shift_down(x, CONV_K - 1 - tap)
        c = y * jax.nn.sigmoid(y)
        nrm = c * lax.rsqrt(jnp.sum(c * c, axis=1, keepdims=True) + NORM_EPS)
        o_ref[...] = jnp.where(j < 2 * GDN_HEADS, nrm, c)

    return pl.pallas_call(
        body, grid=(QKV_COLS // LANE,),
        in_specs=[pl.BlockSpec((t, LANE), lambda j: (0, j)), pl.BlockSpec((CONV_K, LANE), lambda j: (0, j))],
        out_specs=pl.BlockSpec((t, LANE), lambda j: (0, _post_col(j))),
        out_shape=jax.ShapeDtypeStruct((t, QKV_COLS), F32),
        compiler_params=_params("parallel"), name="gdn_prep_fwd",
    )(proj, conv_w)


def _gdn_prep_bwd(proj, conv_w, dpost, dproj):
    t = proj.shape[0]

    def body(x_ref, w_ref, d_ref, _, dx_ref, dw_ref):
        j = pl.program_id(0)
        x = x_ref[...]
        xs = [_shift_down(x, CONV_K - 1 - tap) for tap in range(CONV_K)]
        y = jnp.zeros_like(x)
        for tap in range(CONV_K):
            y = y + w_ref[tap:tap + 1, :] * xs[tap]
        sig = jax.nn.sigmoid(y)
        c = y * sig
        r = lax.rsqrt(jnp.sum(c * c, axis=1, keepdims=True) + NORM_EPS)
        nrm = c * r
        d = d_ref[...]
        dc_norm = r * (d - nrm * jnp.sum(d * nrm, axis=1, keepdims=True))
        dc = jnp.where(j < 2 * GDN_HEADS, dc_norm, d)
        dy = dc * (sig * (1.0 + y * (1.0 - sig)))
        dx = jnp.zeros_like(x)
        for tap in range(CONV_K):
            dx = dx + _shift_up(w_ref[tap:tap + 1, :] * dy, CONV_K - 1 - tap)
            dw_ref[tap:tap + 1, :] = jnp.sum(dy * xs[tap], axis=0, keepdims=True)
        dx_ref[...] = dx.astype(dx_ref.dtype)

    return pl.pallas_call(
        body, grid=(QKV_COLS // LANE,),
        in_specs=[pl.BlockSpec((t, LANE), lambda j: (0, j)), pl.BlockSpec((CONV_K, LANE), lambda j: (0, j)),
                  pl.BlockSpec((t, LANE), lambda j: (0, _post_col(j))), pl.BlockSpec(memory_space=pl.ANY)],
        out_specs=[pl.BlockSpec((t, LANE), lambda j: (0, j)), pl.BlockSpec((CONV_K, LANE), lambda j: (0, j))],
        out_shape=[jax.ShapeDtypeStruct(dproj.shape, dproj.dtype), jax.ShapeDtypeStruct((CONV_K, QKV_COLS), F32)],
        input_output_aliases={3: 0},
        compiler_params=_params("parallel"), name="gdn_prep_bwd",
    )(proj, conv_w, dpost, dproj)


def _softplus(v):
    return jnp.maximum(v, 0.0) + jnp.log(1.0 + jnp.exp(-jnp.abs(v)))


def _tri_inv(low, nn):
    r = lax.broadcasted_iota(jnp.int32, (CHUNK, CHUNK), 0)
    c = lax.broadcasted_iota(jnp.int32, (CHUNK, CHUNK), 1)
    eye = (r == c).astype(F32)
    same_blk = lax.shift_right_logical(r, 4) == lax.shift_right_logical(c, 4)
    diag = jnp.where(same_blk, low, 0.0)
    off = low - diag
    n1 = -diag
    n2 = nn(n1, n1)
    n4 = nn(n2, n2)
    n8 = nn(n4, n4)
    inv_d = nn(nn(nn(eye + n1, eye + n2), eye + n4), eye + n8)
    m1 = nn(inv_d, off)
    m2 = nn(m1, m1)
    return nn(nn(eye - m1, eye + m2), inv_d)


@jax.custom_vjp
def _tri_inv_known(low, t_inv):
    return t_inv


def _tri_inv_known_fwd(low, t_inv):
    return t_inv, t_inv


def _tri_inv_known_bwd(t_inv, g):
    _, nt, tn = _FDOT_BATCH_PLAIN
    return -nt(tn(t_inv, g), t_inv), jnp.zeros_like(t_inv)


_tri_inv_known.defvjp(_tri_inv_known_fwd, _tri_inv_known_bwd)


LOCAL_HEADS_PER_STEP = 8


def _gdn_local_fn(qkv, ba, alog_row, dtb_row, first_head, bdots, fdots, t_known=None):
    nn, nt, tn = bdots
    fnn = fdots[0]
    n_heads = qkv.shape[1] // (3 * HEAD_DIM)
    part = lambda i, p: qkv[:, (3 * i + p) * HEAD_DIM:(3 * i + p + 1) * HEAD_DIM]
    q = jnp.stack([part(i, 0) for i in range(n_heads)]) * (HEAD_DIM ** -0.5)
    k = jnp.stack([part(i, 1) for i in range(n_heads)])
    v = jnp.stack([part(i, 2) for i in range(n_heads)])
    lane = lax.broadcasted_iota(jnp.int32, ba.shape, 1)
    bg = jnp.where(lane < GDN_HEADS, jax.nn.sigmoid(ba), -jnp.exp(alog_row) * _softplus(ba + dtb_row))
    pick = lambda l: jnp.sum(jnp.where(lane == l, bg, 0.0), axis=1, keepdims=True)
    beta = jnp.stack([pick(first_head + i) for i in range(n_heads)])
    g = jnp.stack([pick(first_head + i + GDN_HEADS) for i in range(n_heads)])

    r = lax.broadcasted_iota(jnp.int32, (CHUNK, CHUNK), 0)
    c = lax.broadcasted_iota(jnp.int32, (CHUNK, CHUNK), 1)
    incl = r >= c
    strict = r > c
    eye = r == c

    def to_row(col):
        return jnp.sum(jnp.where(eye, col, 0.0), axis=1, keepdims=True)

    gc = jnp.sum(jnp.where(incl, to_row(g), 0.0), axis=2, keepdims=True)
    diff = gc - to_row(gc)
    decay = jnp.where(incl, jnp.exp(jnp.where(incl, diff, 0.0)), 0.0)
    k_beta = k * beta
    v_beta = v * beta
    low = jnp.where(strict, nt(k_beta, k) * decay, 0.0)
    t_inv = _tri_inv(low, fnn) if t_known is None else _tri_inv_known(low, t_known)
    eg = jnp.exp(gc)
    u = fnn(t_inv, v_beta)
    w = fnn(t_inv, k_beta * eg)
    attn = jnp.where(incl, nt(q, k) * decay, 0.0)
    last = lax.broadcasted_iota(jnp.int32, (CHUNK, 1), 0) == CHUNK - 1
    g_last = jnp.sum(jnp.where(last, gc, 0.0), axis=1, keepdims=True)
    kdec = k * jnp.exp(g_last - gc)
    elast = jnp.broadcast_to(jnp.exp(g_last), (n_heads, 1, LANE))
    return u, w, q * eg, kdec, attn, elast, t_inv


def _gdn_state_fn(u, w, qg, kdec, attn, elast, state, bdots):
    nn, _, tn = bdots
    v_new = u - nn(w, state)
    o = nn(qg, state) + nn(attn, v_new)
    return o, state * elast + tn(kdec, v_new)


def _gdn_local_fwd(post, proj, alog_row, dtb_row):
    t = post.shape[0]
    n_chunks = t // CHUNK
    hb = LOCAL_HEADS_PER_STEP

    def body(qkv_ref, ba_ref, al_ref, dt_ref, u_ref, w_ref, qg_ref, kd_ref, at_ref, el_ref, ti_ref):
        u, w, qg, kdec, attn, elast, t_inv = _gdn_local_fn(qkv_ref[...], ba_ref[...], al_ref[...], dt_ref[...],
                                                           pl.program_id(1) * hb, _BDOT_BATCH_PLAIN, _FDOT_BATCH_PLAIN)
        for i in range(hb):
            cols = slice(i * HEAD_DIM, (i + 1) * HEAD_DIM)
            u_ref[:, cols] = u[i]
            w_ref[:, cols] = w[i].astype(BF16)
            qg_ref[:, cols] = qg[i].astype(BF16)
            kd_ref[:, cols] = kdec[i].astype(BF16)
        at_ref[...] = attn.astype(BF16)
        el_ref[:, 0] = elast
        ti_ref[...] = t_inv

    wide = pl.BlockSpec((CHUNK, hb * HEAD_DIM), lambda n, j: (n, j))
    square = pl.BlockSpec((hb, CHUNK, CHUNK), lambda n, j: (j, n, 0))
    row = pl.BlockSpec((1, LANE), lambda n, j: (0, 0))
    res = pl.pallas_call(
        body, grid=(n_chunks, GDN_HEADS // hb),
        in_specs=[pl.BlockSpec((CHUNK, hb * 3 * HEAD_DIM), lambda n, j: (n, j)),
                  pl.BlockSpec((CHUNK, LANE), lambda n, j: (n, BA_BLK)), row, row],
        out_specs=[wide, wide, wide, wide, square, pl.BlockSpec((hb, 1, 1, LANE), lambda n, j: (j, n, 0, 0)), square],
        out_shape=[jax.ShapeDtypeStruct((t, GDN_WIDTH), F32), jax.ShapeDtypeStruct((t, GDN_WIDTH), BF16),
                   jax.ShapeDtypeStruct((t, GDN_WIDTH), BF16), jax.ShapeDtypeStruct((t, GDN_WIDTH), BF16),
                   jax.ShapeDtypeStruct((GDN_HEADS, t, CHUNK), BF16),
                   jax.ShapeDtypeStruct((GDN_HEADS, n_chunks, 1, LANE), F32),
                   jax.ShapeDtypeStruct((GDN_HEADS, t, CHUNK), F32)],
        compiler_params=_params("parallel", "parallel"), name="gdn_local_fwd",
    )(post, proj, alog_row, dtb_row)
    return tuple(res[:6]), res[6]


def _by_head(ref):
    return jnp.stack([ref[:, h * HEAD_DIM:(h + 1) * HEAD_DIM] for h in range(ref.shape[1] // HEAD_DIM)])


def _gdn_state_specs(n_of):
    wide = pl.BlockSpec((CHUNK, GDN_WIDTH), lambda n: (n_of(n), 0))
    attn = pl.BlockSpec((GDN_HEADS, CHUNK, CHUNK), lambda n: (0, n_of(n), 0))
    elast = pl.BlockSpec((GDN_HEADS, 1, 1, LANE), lambda n: (0, n_of(n), 0, 0))
    saved = pl.BlockSpec((GDN_HEADS, 1, HEAD_DIM, HEAD_DIM), lambda n: (0, n_of(n), 0, 0))
    return wide, attn, elast, saved


def _gdn_state_fwd(u, w, qg, kdec, attn, elast):
    t = u.shape[0]
    n_chunks = t // CHUNK

    def body(u_ref, w_ref, qg_ref, kd_ref, at_ref, el_ref, o_ref, save_ref, state_ref):
        @pl.when(pl.program_id(0) == 0)
        def _():
            state_ref[...] = jnp.zeros_like(state_ref)

        state = state_ref[...]
        save_ref[:, 0] = state
        o, new_state = _gdn_state_fn(_by_head(u_ref), _by_head(w_ref), _by_head(qg_ref), _by_head(kd_ref), at_ref[...],
                                     el_ref[:, 0], state, _BDOT_BATCH_PLAIN)
        for h in range(GDN_HEADS):
            o_ref[:, h * HEAD_DIM:(h + 1) * HEAD_DIM] = o[h]
        state_ref[...] = new_state

    wide, attn_spec, elast_spec, saved_spec = _gdn_state_specs(lambda n: n)
    return pl.pallas_call(
        body, grid=(n_chunks,), in_specs=[wide, wide, wide, wide, attn_spec, elast_spec],
        out_specs=[wide, saved_spec],
        out_shape=[jax.ShapeDtypeStruct((t, GDN_WIDTH), F32),
                   jax.ShapeDtypeStruct((GDN_HEADS, n_chunks, HEAD_DIM, HEAD_DIM), F32)],
        scratch_shapes=[pltpu.VMEM((GDN_HEADS, HEAD_DIM, HEAD_DIM), F32)],
        compiler_params=_params("arbitrary"), name="gdn_state_fwd",
    )(u, w, qg, kdec, attn, elast)


def _gdn_state_bwd(u, w, qg, kdec, attn, elast, saved, do):
    t = u.shape[0]
    n_chunks = t // CHUNK
    last = n_chunks - 1

    def body(u_ref, w_ref, qg_ref, kd_ref, at_ref, el_ref, save_ref, do_ref,
             du_ref, dw_ref, dqg_ref, dkd_ref, dat_ref, del_ref, dstate_ref):
        @pl.when(pl.program_id(0) == 0)
        def _():
            dstate_ref[...] = jnp.zeros_like(dstate_ref)

        _, vjp = jax.vjp(
            lambda *a: _gdn_state_fn(*a, _BDOT_BATCH_VJP), _by_head(u_ref), _by_head(w_ref).astype(F32),
            _by_head(qg_ref).astype(F32), _by_head(kd_ref).astype(F32), at_ref[...].astype(F32), el_ref[:, 0],
            save_ref[:, 0])
        du, dw, dqg, dkd, dat, de, dstate = vjp((_by_head(do_ref), dstate_ref[...]))
        for h in range(GDN_HEADS):
            cols = slice(h * HEAD_DIM, (h + 1) * HEAD_DIM)
            du_ref[:, cols] = du[h]
            dw_ref[:, cols] = dw[h]
            dqg_ref[:, cols] = dqg[h]
            dkd_ref[:, cols] = dkd[h]
        dat_ref[...] = dat
        del_ref[:, 0] = de
        dstate_ref[...] = dstate

    wide, attn_spec, elast_spec, saved_spec = _gdn_state_specs(lambda n: last - n)
    wide_f32 = jax.ShapeDtypeStruct((t, GDN_WIDTH), F32)
    return pl.pallas_call(
        body, grid=(n_chunks,), in_specs=[wide, wide, wide, wide, attn_spec, elast_spec, saved_spec, wide],
        out_specs=[wide, wide, wide, wide, attn_spec, elast_spec],
        out_shape=[wide_f32, wide_f32, wide_f32, wide_f32, jax.ShapeDtypeStruct((GDN_HEADS, t, CHUNK), F32),
                   jax.ShapeDtypeStruct((GDN_HEADS, n_chunks, 1, LANE), F32)],
        scratch_shapes=[pltpu.VMEM((GDN_HEADS, HEAD_DIM, HEAD_DIM), F32)],
        compiler_params=_params("arbitrary"), name="gdn_state_bwd",
    )(u, w, qg, kdec, attn, elast, saved, do)


def _gdn_local_bwd(post, proj, alog_row, dtb_row, t_inv, cots, dproj):
    t = post.shape[0]
    n_chunks = t // CHUNK
    hb = LOCAL_HEADS_PER_STEP
    n_steps = GDN_HEADS // hb

    def body(qkv_ref, ba_ref, al_ref, dt_ref, ti_ref, du_ref, dw_ref, dqg_ref, dkd_ref, dat_ref, del_ref, _,
             dqkv_ref, dba_ref, dal_ref, ddt_ref, dba_acc):
        n = pl.program_id(0)
        j = pl.program_id(1)

        @pl.when((n == 0) & (j == 0))
        def _():
            dal_ref[...] = jnp.zeros_like(dal_ref)
            ddt_ref[...] = jnp.zeros_like(ddt_ref)

        @pl.when(j == 0)
        def _():
            dba_acc[...] = jnp.zeros_like(dba_acc)

        t_known = ti_ref[...]
        _, vjp = jax.vjp(
            lambda a, b, c, d: _gdn_local_fn(a, b, c, d, j * hb, _BDOT_BATCH_VJP, _FDOT_BATCH_VJP, t_known)[:6],
            qkv_ref[...], ba_ref[...], al_ref[...], dt_ref[...])
        dqkv, dba, dal, ddt = vjp((_by_head(du_ref), _by_head(dw_ref), _by_head(dqg_ref), _by_head(dkd_ref), dat_ref[...],
                                   del_ref[:, 0]))
        dqkv_ref[...] = dqkv
        dba_acc[...] += dba
        dal_ref[...] += dal
        ddt_ref[...] += ddt

        @pl.when(j == n_steps - 1)
        def _():
            dba_ref[:, 0:LANE] = dba_acc[...].astype(dba_ref.dtype)
            dba_ref[:, LANE:2 * LANE] = jnp.zeros((CHUNK, LANE), dba_ref.dtype)

    wide = pl.BlockSpec((CHUNK, hb * HEAD_DIM), lambda n, j: (n, j))
    qkv_spec = pl.BlockSpec((CHUNK, hb * 3 * HEAD_DIM), lambda n, j: (n, j))
    row = pl.BlockSpec((1, LANE), lambda n, j: (0, 0))
    return pl.pallas_call(
        body, grid=(n_chunks, n_steps),
        in_specs=[qkv_spec, pl.BlockSpec((CHUNK, LANE), lambda n, j: (n, BA_BLK)), row, row,
                  pl.BlockSpec((hb, CHUNK, CHUNK), lambda n, j: (j, n, 0)), wide, wide, wide, wide,
                  pl.BlockSpec((hb, CHUNK, CHUNK), lambda n, j: (j, n, 0)),
                  pl.BlockSpec((hb, 1, 1, LANE), lambda n, j: (j, n, 0, 0)), pl.BlockSpec(memory_space=pl.ANY)],
        out_specs=[qkv_spec, pl.BlockSpec((CHUNK, 2 * LANE), lambda n, j: (n, BA_BLK // 2)), row, row],
        out_shape=[jax.ShapeDtypeStruct((t, QKV_COLS), F32), jax.ShapeDtypeStruct(dproj.shape, dproj.dtype),
                   jax.ShapeDtypeStruct((1, LANE), F32), jax.ShapeDtypeStruct((1, LANE), F32)],
        input_output_aliases={11: 1},
        scratch_shapes=[pltpu.VMEM((CHUNK, LANE), F32)],
        compiler_params=_params("arbitrary", "arbitrary"), name="gdn_local_bwd",
    )(post, proj, alog_row, dtb_row, t_inv, *cots, dproj)


def _onorm_fn(o, z, w):
    return o * lax.rsqrt(jnp.mean(o * o, axis=1, keepdims=True) + NORM_EPS) * w * (z * jax.nn.sigmoid(z))


_Z_WIDE_BLK = Z_OFF // GDN_WIDTH


def _onorm_fwd(o_raw, proj, norm_w, mixin, tm=256):
    t = o_raw.shape[0]
    tm = min(tm, t)

    def body(o_ref, z_ref, w_ref, _, out_ref):
        for h in range(GDN_HEADS):
            cols = slice(h * HEAD_DIM, (h + 1) * HEAD_DIM)
            out_ref[:, cols] = _onorm_fn(o_ref[:, cols], z_ref[:, cols], w_ref[...]).astype(out_ref.dtype)

    wide = pl.BlockSpec((tm, GDN_WIDTH), lambda i: (i, 0))
    return pl.pallas_call(
        body, grid=(t // tm,),
        in_specs=[wide, pl.BlockSpec((tm, GDN_WIDTH), lambda i: (i, _Z_WIDE_BLK)), pl.BlockSpec((1, LANE), lambda i: (0, 0)),
                  pl.BlockSpec(memory_space=pl.ANY)],
        out_specs=wide, out_shape=jax.ShapeDtypeStruct(mixin.shape, mixin.dtype), input_output_aliases={3: 0},
        compiler_params=_params("parallel"), name="gdn_onorm_fwd",
    )(o_raw, proj, norm_w, mixin)


def _onorm_bwd(o_raw, proj, norm_w, dmixin, dproj, tm=256):
    t = o_raw.shape[0]
    tm = min(tm, t)

    def body(o_ref, z_ref, w_ref, d_ref, _, do_ref, dz_ref, dw_ref):
        @pl.when(pl.program_id(0) == 0)
        def _():
            dw_ref[...] = jnp.zeros_like(dw_ref)

        for h in range(GDN_HEADS):
            cols = slice(h * HEAD_DIM, (h + 1) * HEAD_DIM)
            _, vjp = jax.vjp(_onorm_fn, o_ref[:, cols], z_ref[:, cols], w_ref[...])
            do, dz, dw = vjp(d_ref[:, cols])
            do_ref[:, cols] = do
            dz_ref[:, cols] = dz.astype(dz_ref.dtype)
            dw_ref[...] += dw

    wide = pl.BlockSpec((tm, GDN_WIDTH), lambda i: (i, 0))
    gate = pl.BlockSpec((tm, GDN_WIDTH), lambda i: (i, _Z_WIDE_BLK))
    row = pl.BlockSpec((1, LANE), lambda i: (0, 0))
    return pl.pallas_call(
        body, grid=(t // tm,), in_specs=[wide, gate, row, wide, pl.BlockSpec(memory_space=pl.ANY)],
        out_specs=[wide, gate, row],
        out_shape=[jax.ShapeDtypeStruct((t, GDN_WIDTH), F32), jax.ShapeDtypeStruct(dproj.shape, dproj.dtype),
                   jax.ShapeDtypeStruct((1, LANE), F32)],
        input_output_aliases={4: 1},
        compiler_params=_params("arbitrary"), name="gdn_onorm_bwd",
    )(o_raw, proj, norm_w, dmixin, dproj)


def _pool_select(levels, gi):
    out = levels[-1]
    for lvl in range(len(levels) - 2, -1, -1):
        out = jnp.where(gi == lvl, levels[lvl], out)
    return out


def _pool_count(shape, gi):
    pos = lax.broadcasted_iota(jnp.int32, shape, 0)
    win = lax.shift_left(jnp.int32(2), gi)
    return jnp.minimum(pos + 1, win).astype(F32)


def _pooled(p, gi):
    acc = p
    levels = []
    for lvl in range(POOL_GROUPS):
        acc = acc + _shift_down(acc, 1 << lvl)
        levels.append(acc)
    return _pool_select(levels, gi) / _pool_count(p.shape, gi) - p


def _pool_fwd(proj, pool_w, pool_scale):
    t = proj.shape[0]

    def body(p_ref, w_ref, s_ref, out_ref):
        gi = pl.program_id(0)
        pooled = _pooled(p_ref[...], gi)
        out_ref[...] = (_BDOT_PLAIN[0](pooled, w_ref[0]) * s_ref[0]).astype(out_ref.dtype)

    return pl.pallas_call(
        body, grid=(POOL_GROUPS,),
        in_specs=[pl.BlockSpec((t, POOL_GROUP_DIM), lambda g: (0, POOL_BLK + g)),
                  pl.BlockSpec((1, POOL_GROUP_DIM, POOL_GROUP_DIM), lambda g: (g, 0, 0)),
                  pl.BlockSpec((1, 1, POOL_GROUP_DIM), lambda g: (g, 0, 0))],
        out_specs=pl.BlockSpec((t, POOL_GROUP_DIM), lambda g: (0, GDN_WIDTH // POOL_GROUP_DIM + g)),
        out_shape=jax.ShapeDtypeStruct((t, 2 * GDN_WIDTH), BF16),
        compiler_params=_params("parallel"), name="pool_fwd",
    )(proj, pool_w, pool_scale)


def _pool_bwd(proj, pool_w, pool_scale, dmixin):
    t = proj.shape[0]
    nn, nt, tn = _BDOT_PLAIN

    def body(p_ref, w_ref, s_ref, d_ref, dp_ref, dw_ref, ds_ref):
        gi = pl.program_id(0)
        p = p_ref[...]
        pooled = _pooled(p, gi)
        mixed = nn(pooled, w_ref[0])
        d = d_ref[...]
        ds_ref[0] = jnp.sum(d * mixed, axis=0, keepdims=True)
        dmixed = d * s_ref[0]
        dw_ref[0] = tn(pooled, dmixed)
        dpooled = nt(dmixed, w_ref[0])
        acc = dpooled / _pool_count(p.shape, gi)
        levels = []
        for lvl in range(POOL_GROUPS):
            acc = acc + _shift_up(acc, 1 << lvl)
            levels.append(acc)
        dp_ref[...] = (_pool_select(levels, gi) - dpooled).astype(dp_ref.dtype)

    return pl.pallas_call(
        body, grid=(POOL_GROUPS,),
        in_specs=[pl.BlockSpec((t, POOL_GROUP_DIM), lambda g: (0, POOL_BLK + g)),
                  pl.BlockSpec((1, POOL_GROUP_DIM, POOL_GROUP_DIM), lambda g: (g, 0, 0)),
                  pl.BlockSpec((1, 1, POOL_GROUP_DIM), lambda g: (g, 0, 0)),
                  pl.BlockSpec((t, POOL_GROUP_DIM), lambda g: (0, GDN_WIDTH // POOL_GROUP_DIM + g))],
        out_specs=[pl.BlockSpec((t, POOL_GROUP_DIM), lambda g: (0, POOL_BLK + g)),
                   pl.BlockSpec((1, POOL_GROUP_DIM, POOL_GROUP_DIM), lambda g: (g, 0, 0)),
                   pl.BlockSpec((1, 1, POOL_GROUP_DIM), lambda g: (g, 0, 0))],
        out_shape=[jax.ShapeDtypeStruct((t, PROJ_COLS), BF16),
                   jax.ShapeDtypeStruct((POOL_GROUPS, POOL_GROUP_DIM, POOL_GROUP_DIM), F32),
                   jax.ShapeDtypeStruct((POOL_GROUPS, 1, POOL_GROUP_DIM), F32)],
        compiler_params=_params("parallel"), name="pool_bwd",
    )(proj, pool_w, pool_scale, dmixin)


def _ln_stats(s):
    mu = jnp.mean(s, axis=1, keepdims=True)
    xc = s - mu
    var = jnp.mean(xc * xc, axis=1, keepdims=True)
    rstd = lax.rsqrt(var + LN_EPS)
    return xc * rstd, rstd


def _mm_ln(a, b, h_in, g, bias, *, name, tm=256):
    t, k = a.shape
    d = b.shape[1]
    tm = min(tm, t)

    def body(a_ref, b_ref, h_ref, g_ref, bias_ref, y_ref, o_ref, o16_ref):
        y = jnp.dot(a_ref[...].astype(BF16), b_ref[...], preferred_element_type=F32)
        y_ref[...] = y
        xhat, _ = _ln_stats(ALPHA * h_ref[...] + y)
        out = xhat * g_ref[...] + bias_ref[...]
        o_ref[...] = out
        o16_ref[...] = out.astype(BF16)

    row = pl.BlockSpec((tm, d), lambda i: (i, 0))
    vec = pl.BlockSpec((1, d), lambda i: (0, 0))
    return pl.pallas_call(
        body, grid=(t // tm,),
        in_specs=[pl.BlockSpec((tm, k), lambda i: (i, 0)), pl.BlockSpec((k, d), lambda i: (0, 0)), row, vec, vec],
        out_specs=[row, row, row],
        out_shape=[jax.ShapeDtypeStruct((t, d), F32), jax.ShapeDtypeStruct((t, d), F32), jax.ShapeDtypeStruct((t, d), BF16)],
        compiler_params=_params("parallel"), name=name,
    )(a, b, h_in, g, bias)


def _ln_backward(xhat, rstd, dout, gain):
    dxhat = dout * gain
    m1 = jnp.mean(dxhat, axis=1, keepdims=True)
    m2 = jnp.mean(dxhat * xhat, axis=1, keepdims=True)
    return (rstd * (dxhat - m1 - xhat * m2), jnp.sum(dout * xhat, axis=0, keepdims=True),
            jnp.sum(dout, axis=0, keepdims=True))


def _ln_loss(h_in, y, g, b, target, *, name, tm=256):
    t, d = h_in.shape
    tm = min(tm, t)

    def body(h_ref, y_ref, g_ref, b_ref, t_ref, sq_ref, ds_ref, ds16_ref, dg_ref, dbias_ref):
        @pl.when(pl.program_id(0) == 0)
        def _():
            sq_ref[...] = jnp.zeros_like(sq_ref)
            dg_ref[...] = jnp.zeros_like(dg_ref)
            dbias_ref[...] = jnp.zeros_like(dbias_ref)

        xhat, rstd = _ln_stats(ALPHA * h_ref[...] + y_ref[...])
        err = xhat * g_ref[...] + b_ref[...] - t_ref[...]
        sq_ref[...] += jnp.sum(jnp.sum(err * err, axis=1, keepdims=True), axis=0, keepdims=True)
        ds, dg, dbias = _ln_backward(xhat, rstd, err * (1.0 / d), g_ref[...])
        ds_ref[...] = ds
        ds16_ref[...] = ds.astype(BF16)
        dg_ref[...] += dg
        dbias_ref[...] += dbias

    row = pl.BlockSpec((tm, d), lambda i: (i, 0))
    vec = pl.BlockSpec((1, d), lambda i: (0, 0))
    return pl.pallas_call(
        body, grid=(t // tm,), in_specs=[row, row, vec, vec, row],
        out_specs=[pl.BlockSpec((1, LANE), lambda i: (0, 0)), row, row, vec, vec],
        out_shape=[jax.ShapeDtypeStruct((1, LANE), F32), jax.ShapeDtypeStruct((t, d), F32),
                   jax.ShapeDtypeStruct((t, d), BF16), jax.ShapeDtypeStruct((1, d), F32), jax.ShapeDtypeStruct((1, d), F32)],
        compiler_params=_params("arbitrary"), name=name,
    )(h_in, y, g, b, target)


def _ln_bwd(h_in, y, g, d_a, d_b, *, name, tm=256):
    t, d = h_in.shape
    tm = min(tm, t)
    has_b = d_b is not None

    def body(*refs):
        if has_b:
            h_ref, y_ref, g_ref, da_ref, db_ref, ds_ref, ds16_ref, dg_ref, dbias_ref = refs
        else:
            h_ref, y_ref, g_ref, da_ref, ds_ref, ds16_ref, dg_ref, dbias_ref = refs

        @pl.when(pl.program_id(0) == 0)
        def _():
            dg_ref[...] = jnp.zeros_like(dg_ref)
            dbias_ref[...] = jnp.zeros_like(dbias_ref)

        xhat, rstd = _ln_stats(ALPHA * h_ref[...] + y_ref[...])
        dout = da_ref[...]
        if has_b:
            dout = dout + ALPHA * db_ref[...]
        ds, dg, dbias = _ln_backward(xhat, rstd, dout, g_ref[...])
        ds_ref[...] = ds
        ds16_ref[...] = ds.astype(BF16)
        dg_ref[...] += dg
        dbias_ref[...] += dbias

    row = pl.BlockSpec((tm, d), lambda i: (i, 0))
    vec = pl.BlockSpec((1, d), lambda i: (0, 0))
    args = [h_in, y, g, d_a] + ([d_b] if has_b else [])
    return pl.pallas_call(
        body, grid=(t // tm,), in_specs=[row, row, vec, row] + ([row] if has_b else []),
        out_specs=[row, row, vec, vec],
        out_shape=[jax.ShapeDtypeStruct((t, d), F32), jax.ShapeDtypeStruct((t, d), BF16),
                   jax.ShapeDtypeStruct((1, d), F32), jax.ShapeDtypeStruct((1, d), F32)],
        compiler_params=_params("arbitrary"), name=name,
    )(*args)


def _attn_fn(q, k, v, dots):
    nn, nt, _ = dots
    s = nt(q, k) * (XATTN_HEAD_DIM ** -0.5)
    s = s - lax.stop_gradient(jnp.max(s, axis=1, keepdims=True))
    e = jnp.exp(s)
    p = e / jnp.sum(e, axis=1, keepdims=True)
    return nn(p, v)


def _attn_fwd(q, k, v, tq=2048):
    t = q.shape[0]
    tq = min(tq, t)

    def body(q_ref, k_ref, v_ref, o_ref):
        o_ref[...] = _attn_fn(q_ref[...], k_ref[...], v_ref[...], _BDOT_PLAIN).astype(BF16)

    qs = pl.BlockSpec((tq, XATTN_HEAD_DIM), lambda h, i: (i, h))
    ks = pl.BlockSpec((MEM_LEN, XATTN_HEAD_DIM), lambda h, i: (0, h))
    return pl.pallas_call(
        body, grid=(XATTN_HEADS, t // tq), in_specs=[qs, ks, ks], out_specs=qs,
        out_shape=jax.ShapeDtypeStruct(q.shape, BF16), compiler_params=_params("parallel", "parallel"), name="xattn_fwd",
    )(q, k, v)


def _attn_bwd(q, k, v, do, tq=1024):
    t = q.shape[0]
    tq = min(tq, t)

    def body(q_ref, k_ref, v_ref, do_ref, dq_ref, dk_ref, dv_ref):
        @pl.when(pl.program_id(1) == 0)
        def _():
            dk_ref[...] = jnp.zeros_like(dk_ref)
            dv_ref[...] = jnp.zeros_like(dv_ref)

        _, vjp = jax.vjp(lambda a, b, c: _attn_fn(a, b, c, _BDOT_VJP), q_ref[...].astype(F32), k_ref[...].astype(F32),
                         v_ref[...].astype(F32))
        dq, dk, dv = vjp(do_ref[...].astype(F32))
        dq_ref[...] = dq.astype(BF16)
        dk_ref[...] += dk
        dv_ref[...] += dv

    qs = pl.BlockSpec((tq, XATTN_HEAD_DIM), lambda h, i: (i, h))
    ks = pl.BlockSpec((MEM_LEN, XATTN_HEAD_DIM), lambda h, i: (0, h))
    return pl.pallas_call(
        body, grid=(XATTN_HEADS, t // tq), in_specs=[qs, ks, ks, qs], out_specs=[qs, ks, ks],
        out_shape=[jax.ShapeDtypeStruct(q.shape, BF16), jax.ShapeDtypeStruct(k.shape, F32), jax.ShapeDtypeStruct(v.shape, F32)],
        compiler_params=_params("parallel", "arbitrary"), name="xattn_bwd",
    )(q, k, v, do)


def _local_step(x, x16, mem, target, weights_of, grads_ready):
    def behind(vec, token):
        return vec if token is None else vec + token

    w = dict(weights_of("mixer", None))
    proj = _mm(x16, w["w_in"], tb=True, tn=768, name="mm_in_proj")
    mixin = _pool_fwd(proj, w["pool_w"], w["pool_scale"])
    post = _gdn_prep_fwd(proj, w["conv_w"])
    token = weights_of("ahead_conv", post)
    chunked, t_inv = _gdn_local_fwd(post, proj, behind(w["alog_row"], token), w["dtb_row"])
    o_raw, saved = _gdn_state_fwd(*chunked)
    token = weights_of("ahead_scan", o_raw)
    mixin = _onorm_fwd(o_raw, proj, behind(w["gdn_norm_w"], token), mixin)
    w.update(weights_of("attn", mixin))
    mix, h1, h1_16 = _mm_ln(mixin, w["w_out"], x, w["ln1_g"], w["ln1_b"], name="mm_out_proj_ln1")
    xq = _mm(h1_16, w["xq_w"], out_dtype=BF16, name="mm_xq")
    xk = _mm(mem, w["xk_w"], out_dtype=BF16, name="mm_xk")
    xv = _mm(mem, w["xv_w"], out_dtype=BF16, name="mm_xv")
    xo = _attn_fwd(xq, xk, xv)
    token = weights_of("ahead_attn", xo)
    if token is not None:
        xo, _ = lax.optimization_barrier((xo, token))
    xa, h2, h2_16 = _mm_ln(xo, w["xo_w"], h1, w["ln2_g"], w["ln2_b"], name="mm_xo_ln2")
    w.update(weights_of("up", h2_16))
    act, relu = _mm(h2_16, w["w_up"], b_chunks=True, epi="relu2", name="mm_up")
    w.update(weights_of("down", act))
    ff = _mm(act, w["w_down"], tn=512, tk=2048, name="mm_down")
    g = {}
    sq, ds3, ds3_16, g["ln3_g"], g["ln3_b"] = _ln_loss(h2, ff, w["ln3_g"], w["ln3_b"], target, name="ln3_loss")

    gw_down = _mm(act, ds3_16, ta=True, out_dtype=BF16, tm=512, tn=D_MODEL, name="mm_gw_down")
    du = _mm(ds3_16, w["w_down"], tb=True, epi="mul2r", extra=relu, name="mm_du")
    gw_up = _mm(h2_16, du, ta=True, out_dtype=BF16, o_chunks=True, name="mm_gw_up")
    token = grads_ready("mlp", {"w_down": gw_down, "w_up": gw_up})
    dh2 = _mm(du, w["w_up"], tb=True, b_chunks=True, tn=1024, tk=1024, name="mm_dh2")
    ds2, ds2_16, g["ln2_g"], g["ln2_b"] = _ln_bwd(h1, xa, behind(w["ln2_g"], token), dh2, ds3, name="ln2_bwd")
    gw_xo = _mm(xo, ds2_16, ta=True, out_dtype=BF16, name="mm_gw_xo")
    dxo = _mm(ds2_16, w["xo_w"], tb=True, out_dtype=BF16, name="mm_dxo")
    dxq, dxk, dxv = _attn_bwd(xq, xk, xv, dxo)
    gw_xq = _mm(h1_16, dxq, ta=True, out_dtype=BF16, name="mm_gw_xq")
    gw_xk = _mm(mem, dxk, ta=True, out_dtype=BF16, name="mm_gw_xk")
    gw_xv = _mm(mem, dxv, ta=True, out_dtype=BF16, name="mm_gw_xv")
    token = grads_ready("attn", {"xo_w": gw_xo, "xq_w": gw_xq, "xk_w": gw_xk, "xv_w": gw_xv})
    dh1 = _mm(dxq, w["xq_w"], tb=True, name="mm_dh1")
    ds1, ds1_16, g["ln1_g"], g["ln1_b"] = _ln_bwd(x, mix, behind(w["ln1_g"], token), dh1, ds2, name="ln1_bwd")
    gw_out = _mm(mixin, ds1_16, ta=True, out_dtype=BF16, name="mm_gw_out")
    dmixin = _mm(ds1_16, w["w_out"], tb=True, name="mm_dmixin")
    dproj, gw_pool, g["pool_scale"] = _pool_bwd(proj, w["pool_w"], w["pool_scale"], dmixin)
    token = grads_ready("mix", {"w_out": gw_out, "pool_w": gw_pool})
    do_raw, dproj, g["gdn_norm_w"] = _onorm_bwd(o_raw, proj, behind(w["gdn_norm_w"], token), dmixin, dproj)
    cots = _gdn_state_bwd(*chunked, saved, do_raw)
    token = grads_ready("tick", {"after": cots[0]})
    dpost, dproj, g["alog_row"], g["dtb_row"] = _gdn_local_bwd(post, proj, behind(w["alog_row"], token), w["dtb_row"],
                                                               t_inv, cots, dproj)
    dproj, g["conv_w"] = _gdn_prep_bwd(proj, w["conv_w"], dpost, dproj)
    token = grads_ready("small", {**g, "sq": sq})
    gw_in = _mm(dproj, x16, ta=True, out_dtype=BF16, tm=768, tn=D_MODEL, after=token, name="mm_gw_in")
    token = grads_ready("in", {"w_in": gw_in})
    grad_x = _mm(dproj, w["w_in"], tk=1792, epi="add", extra=ds1, add_scale=ALPHA, after=token, name="mm_dx")
    return sq, grad_x, g


_VECTORS = ("a_log", "dt_bias", "gdn_norm_w", "pool_scale", "ln1_g", "ln1_b", "ln2_g", "ln2_b", "ln3_g", "ln3_b")
_BA_SPLIT = BA_OFF + 2 * GDN_HEADS


def _lane_row(v, offset):
    return jnp.zeros((1, LANE), F32).at[0, offset:offset + v.shape[0]].set(v)


_GROUP_VECTORS = {"mixer": (), "attn": ("ln1_g", "ln1_b", "ln2_g", "ln2_b"), "up": (), "down": ("ln3_g", "ln3_b")}


def _group_weights(group, full):
    w = {n: full[n].reshape(1, D_MODEL) for n in _GROUP_VECTORS[group]}
    if group == "mixer":
        w.update({
            "w_in": _w_in_padded(full["w_in"]),
            "conv_w": full["conv_w"],
            "alog_row": _lane_row(full["a_log"], GDN_HEADS),
            "dtb_row": _lane_row(full["dt_bias"], GDN_HEADS),
            "gdn_norm_w": full["gdn_norm_w"].reshape(1, LANE),
            "pool_w": full["pool_w"],
            "pool_scale": full["pool_scale"].reshape(POOL_GROUPS, 1, POOL_GROUP_DIM),
        })
    else:
        w.update({n: full[n] for n in dict(_GATHER_GROUPS)[group]})
    return w


def _w_in_row_map():
    per = IN_COLS // N_DEV
    gap = POOL_OFF - _BA_SPLIT
    pieces = []
    for d in range(N_DEV):
        lo, hi = d * per, (d + 1) * per
        if hi <= _BA_SPLIT:
            pieces.append([(0, lo, per)])
        elif lo >= _BA_SPLIT:
            pieces.append([(0, lo + gap, per)])
        else:
            pieces.append([(0, lo, _BA_SPLIT - lo), (_BA_SPLIT - lo, POOL_OFF, hi - _BA_SPLIT)])
    return pieces


_W_IN_LANES = 256


def _w_in_padded(blocks):
    def body(b_ref, o_ref):
        for d, pieces in enumerate(_w_in_row_map()):
            for src, dst, rows in pieces:
                o_ref[dst:dst + rows, :] = b_ref[d, src:src + rows, :]
        o_ref[_BA_SPLIT:POOL_OFF, :] = jnp.zeros((POOL_OFF - _BA_SPLIT, _W_IN_LANES), o_ref.dtype)

    n, per, cols = blocks.shape
    return pl.pallas_call(
        body, grid=(cols // _W_IN_LANES,), in_specs=[pl.BlockSpec((n, per, _W_IN_LANES), lambda j: (0, 0, j))],
        out_specs=pl.BlockSpec((PROJ_COLS, _W_IN_LANES), lambda j: (0, j)),
        out_shape=jax.ShapeDtypeStruct((PROJ_COLS, cols), blocks.dtype), compiler_params=_params("parallel"),
        name="w_in_padded")(blocks)


def _w_in_chunks(g):
    def body(g_ref, o_ref):
        for d, pieces in enumerate(_w_in_row_map()):
            for dst, src, rows in pieces:
                o_ref[d, dst:dst + rows, :] = g_ref[src:src + rows, :]

    cols = g.shape[1]
    per = IN_COLS // N_DEV
    return pl.pallas_call(
        body, grid=(cols // _W_IN_LANES,), in_specs=[pl.BlockSpec((PROJ_COLS, _W_IN_LANES), lambda j: (0, j))],
        out_specs=pl.BlockSpec((N_DEV, per, _W_IN_LANES), lambda j: (0, 0, j)),
        out_shape=jax.ShapeDtypeStruct((N_DEV, per, cols), g.dtype), compiler_params=_params("parallel"),
        name="w_in_chunks")(g)


def _finish_small_grads(g):
    out = {"conv_w": g["conv_w"]}
    out["a_log"] = g["alog_row"][0, GDN_HEADS:2 * GDN_HEADS]
    out["dt_bias"] = g["dtb_row"][0, GDN_HEADS:2 * GDN_HEADS]
    out["gdn_norm_w"] = g["gdn_norm_w"].reshape(LANE)
    out["pool_scale"] = g["pool_scale"].reshape(POOL_GROUPS * POOL_GROUP_DIM)
    for n in ("ln1_g", "ln1_b", "ln2_g", "ln2_b", "ln3_g", "ln3_b"):
        out[n] = g[n].reshape(D_MODEL)
    return out


def _adamw_math(w, g, m, v):
    m = ADAM_B1 * m + (1.0 - ADAM_B1) * g
    v = ADAM_B2 * v + (1.0 - ADAM_B2) * (g * g)
    m_hat = m / (1.0 - ADAM_B1 ** ADAM_STEP)
    v_hat = v / (1.0 - ADAM_B2 ** ADAM_STEP)
    delta = -ADAM_LR * (m_hat / (jnp.sqrt(v_hat) + ADAM_EPS) + ADAM_WD * w)
    return delta, m, v


ADAMW_TILE_ELEMS = 256 * 1024
CHIP_SUM_TILE_ELEMS = 1024 * 1024


def _shard_tile(r, c, elems):
    for rows in (1024, 512, 256, 128):
        if r % rows == 0 and rows * c <= elems:
            return rows, c
    if r % 128 == 0:
        return 128, c
    for cols in (2048, 1024, 512):
        if c % cols == 0 and r * cols <= elems:
            return r, cols
    return r, 256 if c % 256 == 0 else c


def _adamw_shard(parts, own, me, w, m, v, *, name):
    s, r, c = parts.shape
    tr, tc = _shard_tile(r, c, ADAMW_TILE_ELEMS)
    assert r % tr == 0 and c % tc == 0, (name, r, c)
    unit_axis = w.ndim == 3
    at = (slice(None), 0, slice(None)) if unit_axis else Ellipsis

    def body(me_ref, p_ref, own_ref, w_ref, m_ref, v_ref, g_ref, d_ref, nm_ref, nv_ref):
        mine = own_ref[...].astype(F32)
        g = None
        for i in range(s):
            part = jnp.where(me_ref[0] == i, mine, p_ref[i].astype(F32))
            g = part if g is None else g + part
        delta, nm, nv = _adamw_math(w_ref[at], g, m_ref[at], v_ref[at])
        g_ref[at] = g
        d_ref[at] = delta
        nm_ref[at] = nm
        nv_ref[at] = nv

    if unit_axis:
        blk = pl.BlockSpec((tr, 1, tc), lambda i, j, me_ref: (i, 0, j))
        out = jax.ShapeDtypeStruct((r, 1, c), F32)
    else:
        blk = pl.BlockSpec((tr, tc), lambda i, j, me_ref: (i, j))
        out = jax.ShapeDtypeStruct((r, c), F32)
    return pl.pallas_call(
        body,
        grid_spec=pltpu.PrefetchScalarGridSpec(
            num_scalar_prefetch=1, grid=(r // tr, c // tc),
            in_specs=[pl.BlockSpec((s, tr, tc), lambda i, j, me_ref: (0, i, j)),
                      pl.BlockSpec((None, tr, tc), lambda i, j, me_ref: (me_ref[0], i, j)), blk, blk, blk],
            out_specs=[blk, blk, blk, blk]),
        out_shape=[out, out, out, out], compiler_params=_params("parallel", "parallel"), name=name,
    )(me, parts, own, w, m, v)


N_CHIPS = N_DEV // 2


def _chip_sums(chunks, from_sibling, core, *, name):
    n = len(chunks)
    _, r, c = chunks[0].shape
    assert all(a.shape == chunks[0].shape for a in chunks), name
    tr, tc = _shard_tile(r, c, CHIP_SUM_TILE_ELEMS // n)
    assert r % tr == 0 and c % tc == 0, (name, r, c)

    def body(core_ref, *refs):
        for a in range(n):
            refs[2 * n + a][...] = (refs[a][...].astype(F32) + refs[n + a][...].astype(F32)).astype(BF16)

    by_chip = pl.BlockSpec((None, tr, tc), lambda q, i, j, core_ref: (q, i, j))
    mine = pl.BlockSpec((None, tr, tc), lambda q, i, j, core_ref: (2 * q + core_ref[0], i, j))
    return pl.pallas_call(
        body,
        grid_spec=pltpu.PrefetchScalarGridSpec(
            num_scalar_prefetch=1, grid=(N_CHIPS, r // tr, c // tc),
            in_specs=[mine] * n + [by_chip] * n, out_specs=[by_chip] * n),
        out_shape=[jax.ShapeDtypeStruct((N_CHIPS, r, c), chunks[0].dtype)] * n,
        compiler_params=_params("parallel", "parallel", "parallel"), name=name,
    )(core, *chunks, *from_sibling)


def _place():
    return lax.axis_index("x"), lax.axis_index("y"), lax.axis_index("c")


def _slot(px, py, pc):
    return 4 * px + 2 * py + pc


_HBM = pl.BlockSpec(memory_space=pltpu.HBM)


_SEM = pl.BlockSpec(memory_space=pltpu.SEMAPHORE)
_ANY = pl.BlockSpec(memory_space=pl.ANY)
_EFFECT = pltpu.SideEffectType.DATAFLOW_SIDE_EFFECTING


def _peer(k, x, y, c):
    return (1 - x if k & 4 else x, 1 - y if k & 2 else y, 1 - c if k & 1 else c)


_EXCHANGE_BITS = {"gather_near": (1, 2, 4), "gather_relay": (6,), "gather_pass": (2, 4, 6),
                  "scatter_sibling": (1, 1, 1, 1), "scatter_chips": (2, 4, 6), "all_small": (1, 2, 3, 4, 5, 6, 7)}


def _exchange_copy(mode, src, land, w, i, place, send_sems, recv_sems, receiving):
    bits = _EXCHANGE_BITS[mode]
    k = bits[i]
    peer = _peer(k, *place)
    me = _slot(*place)
    if mode in ("gather_near", "all_small"):
        to, src_ref, sent_to, got_at = peer, src[w], me, _slot(*peer)
    elif mode == "gather_relay":
        x, y, c = place
        other = 1 - c
        to = (lax.bitwise_xor(x, c), lax.bitwise_xor(y, other), c)
        blk = _slot(lax.bitwise_xor(x, other), lax.bitwise_xor(y, c), c)
        src_ref, sent_to, got_at = land[w].at[blk], blk, _slot(*peer)
    elif mode == "gather_pass":
        blk = _slot(*peer)
        to, src_ref, sent_to, got_at = _peer(1, *place), land[w].at[blk], blk, _slot(*_peer(k | 1, *place))
    elif mode == "scatter_sibling":
        to, src_ref, sent_to, got_at = peer, src[w].at[2 * i + 1 - place[2]], i, i
    else:
        to, src_ref, sent_to, got_at = peer, src[w].at[_slot(*peer) // 2], me // 2, _slot(*peer) // 2
    sem = w * len(bits) + i
    return pltpu.make_async_remote_copy(
        src_ref=src_ref, dst_ref=land[w].at[got_at if receiving else sent_to], send_sem=send_sems.at[sem],
        recv_sem=recv_sems.at[sem], device_id=to, device_id_type=MESH)


def _exchange_start(mode, srcs, lands, after, *, name):
    ns, nl = len(srcs), len(lands)
    n_sem = nl * len(_EXCHANGE_BITS[mode])

    def body(*refs):
        src, land = refs[:ns], refs[ns:ns + nl]
        send_sems, recv_sems = refs[ns + nl + 1:ns + nl + 3]
        token = refs[-1]
        place = _place()
        for w in range(nl):
            for i in range(len(_EXCHANGE_BITS[mode])):
                _exchange_copy(mode, src, land, w, i, place, send_sems, recv_sems, receiving=False).start()
        token[...] = jnp.zeros_like(token)

    sems = pltpu.SemaphoreType.DMA((n_sem,))
    arrays = list(srcs) + list(lands)
    res = pl.pallas_call(
        body, name=name, in_specs=[_HBM] * (ns + nl) + [_ANY],
        out_specs=(_SEM, _SEM, *([_HBM] * (ns + nl)), pl.BlockSpec(memory_space=pltpu.VMEM)),
        out_shape=(sems, sems, *[pltpu.HBM(a.shape, a.dtype) for a in arrays], jax.ShapeDtypeStruct((8, LANE), F32)),
        input_output_aliases={i: 2 + i for i in range(ns + nl)},
        compiler_params=pltpu.CompilerParams(has_side_effects=_EFFECT),
    )(*[pltpu.with_memory_space_constraint(a, pltpu.HBM) for a in arrays], after)
    return res[0], res[1], list(res[2:2 + ns]), list(res[2 + ns:2 + ns + nl]), res[-1]


def _exchange_wait(mode, started, after, *, name):
    send_sems, recv_sems, srcs, lands, _ = started
    ns, nl = len(srcs), len(lands)

    def body(*refs):
        src, land = refs[:ns], refs[ns:ns + nl]
        send_sems, recv_sems = refs[ns + nl:ns + nl + 2]
        place = _place()
        for w in range(nl):
            for i in range(len(_EXCHANGE_BITS[mode])):
                cp = _exchange_copy(mode, src, land, w, i, place, send_sems, recv_sems, receiving=True)
                cp.wait_send()
                cp.wait_recv()

    arrays = list(srcs) + list(lands)
    res = pl.pallas_call(
        body, name=name, in_specs=[_HBM] * (ns + nl) + [_SEM, _SEM, _ANY], out_specs=[_HBM] * (ns + nl),
        out_shape=[pltpu.HBM(a.shape, a.dtype) for a in arrays],
        input_output_aliases={i: i for i in range(ns + nl)},
        compiler_params=pltpu.CompilerParams(has_side_effects=_EFFECT),
    )(*arrays, send_sems, recv_sems, after)
    return list(res[:ns]), list(res[ns:])


_LN_ROWS = ("ln1_g", "ln1_b", "ln2_g", "ln2_b", "ln3_g", "ln3_b")
_MISC_ROW = len(_LN_ROWS)
_MISC = (("pool_scale", 0, GDN_WIDTH), ("gdn_norm_w", GDN_WIDTH, HEAD_DIM), ("a_log", GDN_WIDTH + LANE, GDN_HEADS),
         ("dt_bias", GDN_WIDTH + 2 * LANE, GDN_HEADS), ("loss", GDN_WIDTH + 3 * LANE, 1))
_CONV_ROW = _MISC_ROW + 1
_CONV_ROWS = CONV_K * QKV_COLS // D_MODEL
_SMALL_ROWS = 16


def _pack_small(vals):
    pieces, at = [], 0
    for n, off, size in _MISC:
        pieces.append(jnp.zeros((off - at,), F32))
        pieces.append(vals[n].reshape(size).astype(F32) if n in vals else jnp.zeros((size,), F32))
        at = off + size
    pieces.append(jnp.zeros((D_MODEL - at,), F32))
    conv = vals["conv_w"].reshape(-1) if "conv_w" in vals else jnp.zeros((_CONV_ROWS * D_MODEL,), F32)
    tail = jnp.zeros(((_SMALL_ROWS - _CONV_ROW - _CONV_ROWS) * D_MODEL,), F32)
    flat = jnp.concatenate([vals[n].reshape(D_MODEL) for n in _LN_ROWS] + pieces + [conv, tail])
    return flat.reshape(_SMALL_ROWS, D_MODEL)


def _adamw_small(zone, mine, me, w, m, v):
    short = [(n, off, size) for n, off, size in _MISC if n != "loss"]

    def body(me_ref, z_ref, mine_ref, w_ref, m_ref, v_ref, *rest):
        outs, (g_s, d_s, nm_s, nv_s) = rest[:-4], rest[-4:]
        g = None
        for s in range(N_DEV):
            part = jnp.where(me_ref[0] == s, mine_ref[...], z_ref[s])
            g = part if g is None else g + part
        g_s[...] = g
        d_s[...], nm_s[...], nv_s[...] = _adamw_math(w_ref[...], g, m_ref[...], v_ref[...])
        k = 0
        for src in (g_s, d_s, nm_s, nv_s):
            for r in range(len(_LN_ROWS)):
                outs[k][...] = src[r:r + 1, :]
                k += 1
            for _, off, size in short:
                outs[k][...] = src[_MISC_ROW:_MISC_ROW + 1, off:off + size]
                k += 1
        outs[k][...] = g_s[_CONV_ROW:_CONV_ROW + _CONV_ROWS, :]
        outs[k + 1][...] = g_s[_MISC_ROW:_MISC_ROW + 1, :]

    rows, d = mine.shape
    per_quantity = [jax.ShapeDtypeStruct((1, D_MODEL), F32)] * len(_LN_ROWS) + [
        jax.ShapeDtypeStruct((1, size), F32) for _, _, size in short]
    out_shape = per_quantity * 4 + [jax.ShapeDtypeStruct((_CONV_ROWS, d), F32), jax.ShapeDtypeStruct((1, d), F32)]
    whole = lambda a: pl.BlockSpec(a.shape, lambda i, me_ref: (0,) * len(a.shape))
    res = pl.pallas_call(
        body,
        grid_spec=pltpu.PrefetchScalarGridSpec(
            num_scalar_prefetch=1, grid=(1,), in_specs=[whole(a) for a in (zone, mine, w, m, v)],
            out_specs=[whole(s) for s in out_shape], scratch_shapes=[pltpu.VMEM((rows, d), F32)] * 4),
        out_shape=out_shape, compiler_params=_params("arbitrary"), name="adamw_small",
    )(me, zone, mine, w, m, v)
    names = list(_LN_ROWS) + [n for n, _, _ in short]
    n_each = len(names)
    quantities = [dict(zip(names, res[q * n_each:(q + 1) * n_each])) for q in range(4)]
    return quantities, res[-2], res[-1]


_WEIGHT_ORDER = ("w_in", "conv_w", "a_log", "dt_bias", "gdn_norm_w", "pool_w", "pool_scale", "w_out", "ln1_g", "ln1_b",
                 "xq_w", "xk_w", "xv_w", "xo_w", "ln2_g", "ln2_b", "w_up", "w_down", "ln3_g", "ln3_b")


def _shard2d(name, a):
    if name == "w_in":
        return a.T
    return a.reshape(-1, a.shape[-1]) if name == "pool_w" else a


def _update_view(name, a):
    return jnp.transpose(a, (2, 0, 1)) if name == "w_in" else _shard2d(name, a[0])


def _shard_result(name, r, shape):
    return jnp.transpose(r, (1, 2, 0)) if name == "w_in" else r.reshape(shape)


def _gathered_to_full(name, gth):
    if name in ("w_up", "w_in"):
        return gth
    if name == "conv_w":
        return jnp.transpose(gth, (1, 0, 2)).reshape(gth.shape[1], N_DEV * gth.shape[2])
    if name == "pool_w":
        g4 = gth.reshape(N_DEV, POOL_GROUPS, POOL_GROUP_DIM // N_DEV, POOL_GROUP_DIM)
        return jnp.transpose(g4, (1, 0, 2, 3)).reshape(POOL_GROUPS, POOL_GROUP_DIM, POOL_GROUP_DIM)
    return gth.reshape(N_DEV * gth.shape[1], gth.shape[2])


def _full_to_chunks(name, full):
    if name == "w_up":
        return full
    if name == "pool_w":
        g4 = full.reshape(POOL_GROUPS, N_DEV, POOL_GROUP_DIM // N_DEV, POOL_GROUP_DIM)
        return jnp.transpose(g4, (1, 0, 2, 3)).reshape(N_DEV, POOL_GROUPS * POOL_GROUP_DIM // N_DEV, POOL_GROUP_DIM)
    return full.reshape(N_DEV, full.shape[0] // N_DEV, full.shape[1])


_GATHER_GROUPS = (("mixer", ("w_in", "conv_w", "pool_w")), ("attn", ("w_out", "xq_w", "xk_w", "xv_w", "xo_w")),
                  ("up", ("w_up",)), ("down", ("w_down",)))


def _grad_chunks(name, g):
    if name == "w_in":
        return _w_in_chunks(g.astype(BF16))
    return _full_to_chunks(name, g.astype(BF16))


def kernel(x, mem, w_in, conv_w, a_log, dt_bias, gdn_norm_w, pool_w, pool_scale, w_out, ln1_g, ln1_b, xq_w, xk_w, xv_w, xo_w, ln2_g, ln2_b, w_up, w_down, ln3_g, ln3_b, loss_target, m_w_in, m_conv_w, m_a_log, m_dt_bias, m_gdn_norm_w, m_pool_w, m_pool_scale, m_w_out, m_ln1_g, m_ln1_b, m_xq_w, m_xk_w, m_xv_w, m_xo_w, m_ln2_g, m_ln2_b, m_w_up, m_w_down, m_ln3_g, m_ln3_b, v_w_in, v_conv_w, v_a_log, v_dt_bias, v_gdn_norm_w, v_pool_w, v_pool_scale, v_w_out, v_ln1_g, v_ln1_b, v_xq_w, v_xk_w, v_xv_w, v_xo_w, v_ln2_g, v_ln2_b, v_w_up, v_w_down, v_ln3_g, v_ln3_b):
    args = dict(locals())
    wt = {n: args[n][0] for n in _WEIGHT_ORDER}
    mo = {n: args["m_" + n][0] for n in _WEIGHT_ORDER}
    vo = {n: args["v_" + n][0] for n in _WEIGHT_ORDER}

    me = _slot(*_place())
    me_arr = jnp.reshape(me, (1,)).astype(jnp.int32)
    nothing = jnp.zeros((8, LANE), F32)

    def landing_zones(names):
        shards = [_shard2d(n, wt[n]).astype(F32 if n == "conv_w" else BF16) for n in names]
        zones = [lax.dynamic_update_slice(lax.empty((N_DEV, *s.shape), s.dtype), s[None], (me, 0, 0)) for s in shards]
        return shards, zones

    chip_arr = jnp.reshape(me // 2, (1,)).astype(jnp.int32)
    core_arr = jnp.reshape(lax.axis_index("c"), (1,)).astype(jnp.int32)
    names_of = dict(_GATHER_GROUPS)
    gathers = {}
    prepared = {}

    def gather_near(group, after):
        shards, zones = prepared.pop(group) if group in prepared else landing_zones(names_of[group])
        gathers[group] = _exchange_start("gather_near", shards, zones, after, name="gather_near_" + group)
        return gathers[group][4]

    def gather_next(group, was, now, after):
        _, zones = _exchange_wait(was, gathers[group], after, name=f"{was}_{group}_wait")
        gathers[group] = _exchange_start(now, [], zones, nothing, name=f"{now}_{group}")
        return gathers[group][4]

    def gather_relay(group, after):
        return gather_next(group, "gather_near", "gather_relay", after)

    def gather_pass(group, after):
        return gather_next(group, "gather_relay", "gather_pass", after)

    def gathered(group, after):
        _, zones = _exchange_wait("gather_pass", gathers[group], after, name=f"gather_pass_{group}_wait")
        full = {n: _gathered_to_full(n, z) for n, z in zip(names_of[group], zones)}
        full.update({n: wt[n] for n in _VECTORS})
        return _group_weights(group, full)

    token = gather_near("mixer", nothing)
    x16 = _cast_bf16(x[0], name="cast_x")
    later = {group: landing_zones(names_of[group]) for group in ("attn", "up", "down")}
    token, x16, later = lax.optimization_barrier((token, x16, later))
    prepared.update(later)
    token = gather_pass("mixer", gather_relay("mixer", token))
    token = gather_near("attn", token)

    def weights_of(group, after):
        if group == "mixer":
            return gathered(group, token)
        if group == "ahead_conv":
            return gather_near("up", gather_relay("attn", after))[0:1, 0:1]
        if group == "ahead_scan":
            return gather_pass("attn", after)[0:1, 0:1]
        if group == "attn":
            return gathered(group, gather_near("down", gather_relay("up", after)))
        if group == "ahead_attn":
            return gather_relay("down", gather_pass("up", after))[0:1, 0:1]
        if group == "up":
            return gathered(group, gather_pass("down", after))
        return gathered(group, after)

    scatters = {}
    in_flight = []

    def chip_stage(after):
        group, names, started = in_flight.pop()
        chunks, from_sibling = _exchange_wait("scatter_sibling", started, after, name=f"scatter_sibling_{group}_wait")
        sums, alike = [None] * len(names), {}
        for i, chunk in enumerate(chunks):
            alike.setdefault(chunk.shape, []).append(i)
        for same in alike.values():
            res = _chip_sums([chunks[i] for i in same], [from_sibling[i] for i in same], core_arr,
                             name="chip_sums_" + names[same[0]])
            for i, r in zip(same, res):
                sums[i] = r
        scatters[group] = (names, _exchange_start("scatter_chips", sums, [lax.empty(s.shape, s.dtype) for s in sums],
                                                  nothing, name="scatter_chips_" + group))
        return scatters[group][1][4]

    small_sent = []

    def grads_ready(group, grads):
        if group == "tick":
            return chip_stage(grads["after"])[0:1, 0:1] if in_flight else None
        if group == "small":
            small = _finish_small_grads(grads)
            small["loss"] = 0.5 * grads["sq"][0:1, 0] / D_MODEL
            packed = _pack_small(small)
            zone = lax.empty((N_DEV, *packed.shape), F32)
            small_sent.append(_exchange_start("all_small", [packed], [zone], nothing, name="small_grads_start"))
            return small_sent[0][4][0:1, 0:1]
        names = tuple(grads)
        chunks = [_grad_chunks(n, grads[n]) for n in names]
        token = chip_stage(chunks[0]) if in_flight else nothing
        zones = [lax.empty((N_CHIPS, *c.shape[1:]), c.dtype) for c in chunks]
        started = _exchange_start("scatter_sibling", chunks, zones, token, name="scatter_sibling_" + group)
        in_flight.append((group, names, started))
        if group != "in":
            return started[4][0:1, 0:1]
        return chip_stage(update_group("mlp", started[4], count=1))[0:1, 0:1]

    out = {}
    arrived = {}

    def update_group(group, after, count=None):
        if group not in arrived:
            names, started = scatters.pop(group)
            sums, lands = _exchange_wait("scatter_chips", started, after, name=f"scatter_chips_{group}_wait")
            arrived[group] = list(zip(names, lands, sums))
        todo = arrived[group][:count]
        arrived[group] = arrived[group][len(todo):]
        for n, parts, own in todo:
            res = _adamw_shard(parts, own, chip_arr, _update_view(n, args[n]), _update_view(n, args["m_" + n]),
                               _update_view(n, args["v_" + n]), name="adamw_" + n)
            out[n] = [_shard_result(n, r, args[n].shape) for r in res]
            after = res[1]
        return after

    sq, grad_x, g = _local_step(x[0], x16, mem[0], loss_target[0], weights_of, grads_ready)

    after = grad_x
    for group in list(arrived) + list(scatters):
        after = update_group(group, after)

    (packed,), (zone,) = _exchange_wait("all_small", small_sent[0], after, name="small_grads_wait")
    quantities, conv_rows, misc_row = _adamw_small(
        zone, packed, me_arr, _pack_small({n: wt[n] for n in _VECTORS}), _pack_small({n: mo[n] for n in _VECTORS}),
        _pack_small({n: vo[n] for n in _VECTORS}))
    cols = conv_w.shape[-1]
    conv_mine = lax.dynamic_slice(conv_rows.reshape(CONV_K, QKV_COLS), (0, me * cols), (CONV_K, cols))[None]
    res = _adamw_shard(conv_mine, conv_mine, jnp.zeros((1,), jnp.int32), wt["conv_w"], mo["conv_w"], vo["conv_w"],
                       name="adamw_conv_w")
    out["conv_w"] = [r.reshape(conv_w.shape) for r in res]
    for n in _VECTORS:
        out[n] = [q[n] for q in quantities]
    loss_at = dict((n, off) for n, off, _ in _MISC)["loss"]

    return (misc_row[0, loss_at], grad_x[None], *[out[n][0] for n in _WEIGHT_ORDER], *[out[n][1] for n in _WEIGHT_ORDER],
            *[out[n][2] for n in _WEIGHT_ORDER], *[out[n][3] for n in _WEIGHT_ORDER])
```

```python
import jax
import jax.numpy as jnp
from jax import lax
from jax.experimental import pallas as pl
from jax.experimental.pallas import tpu as pltpu

F32 = jnp.float32
BF16 = jnp.bfloat16
MESH = pl.DeviceIdType.MESH

N_DEV = 8
D_MODEL = 2048
GDN_WIDTH = 1024
GDN_HEADS = 8
HEAD_DIM = 128
CONV_K = 4
CHUNK = 64
POOL_GROUPS = 4
POOL_GROUP_DIM = 256
MEM_LEN = 256
XATTN_HEADS = 4
XATTN_HEAD_DIM = 512
D_FF = 8192
IN_COLS = 5136
ALPHA = 2.0 ** 0.25
LN_EPS = 1e-5
NORM_EPS = 1e-6

LANE = 128
QKV_COLS = 3 * GDN_WIDTH
Z_OFF = QKV_COLS
BA_OFF = 4 * GDN_WIDTH
POOL_OFF = BA_OFF + 2 * LANE
PROJ_COLS = POOL_OFF + GDN_WIDTH
BA_BLK = BA_OFF // LANE
POOL_BLK = POOL_OFF // POOL_GROUP_DIM

ADAM_LR = 0.001
ADAM_B1 = 0.9
ADAM_B2 = 0.999
ADAM_EPS = 1e-08
ADAM_WD = 0.01
ADAM_STEP = 10

VMEM_LIMIT_BYTES = 48 * 1024 * 1024


def _params(*sem):
    return pltpu.CompilerParams(dimension_semantics=sem if sem else None, vmem_limit_bytes=VMEM_LIMIT_BYTES)


def _make_dots(cast, precision, batched=False):
    lead = 1 if batched else 0
    batch = ((0,), (0,)) if batched else ((), ())

    def dg(a, b, ca, cb):
        if cast is not None:
            a = a.astype(cast)
            b = b.astype(cast)
        return lax.dot_general(a, b, (((ca + lead,), (cb + lead,)), batch), precision=precision, preferred_element_type=F32)

    def nn_(a, b):
        return dg(a, b, 1, 0)

    def nt_(a, b):
        return dg(a, b, 1, 1)

    def tn_(a, b):
        return dg(a, b, 0, 0)

    @jax.custom_vjp
    def nn(a, b):
        return nn_(a, b)

    nn.defvjp(lambda a, b: (nn_(a, b), (a, b)), lambda r, g: (nt_(g, r[1]), tn_(r[0], g)))

    @jax.custom_vjp
    def nt(a, b):
        return nt_(a, b)

    nt.defvjp(lambda a, b: (nt_(a, b), (a, b)), lambda r, g: (nn_(g, r[1]), tn_(g, r[0])))

    @jax.custom_vjp
    def tn(a, b):
        return tn_(a, b)

    tn.defvjp(lambda a, b: (tn_(a, b), (a, b)), lambda r, g: (nt_(r[1], g), nn_(r[0], g)))

    return (nn_, nt_, tn_), (nn, nt, tn)


_BDOT_PLAIN, _BDOT_VJP = _make_dots(BF16, None)
_BDOT_BATCH_PLAIN, _BDOT_BATCH_VJP = _make_dots(BF16, None, batched=True)
_FDOT_BATCH_PLAIN, _FDOT_BATCH_VJP = _make_dots(BF16, None, batched=True)


def _mm(a, b, *, ta=False, tb=False, out_dtype=F32, tm=None, tn=512, tk=None, epi=None, extra=None, add_scale=1.0,
        b_chunks=False, o_chunks=False, after=None, name):
    m, k = (a.shape[1], a.shape[0]) if ta else a.shape
    if b_chunks:
        n, kb = (b.shape[1], N_DEV * b.shape[2]) if tb else (N_DEV * b.shape[2], b.shape[1])
    else:
        n, kb = b.shape if tb else (b.shape[1], b.shape[0])
    assert kb == k, (name, a.shape, b.shape)
    tm, tn, tk = min(tm or m, m), min(tn, n), min(tk or k, k)
    assert m % tm == 0 and n % tn == 0 and k % tk == 0, (name, m, n, k)
    nk = k // tk
    dims = (((0 if ta else 1,), (1 if tb else 0,)), ((), ()))
    n_extra = 0 if epi in (None, "relu2") else 1
    n_out = 2 if epi == "relu2" else 1
    if epi in ("relu2", "mul2r"):
        out_dtype = BF16
    n_after = 0 if after is None else 1

    def body(*refs):
        a_ref, b_ref = refs[:2]
        c_ref = refs[2] if n_extra else None
        o_refs = refs[2 + n_extra + n_after:2 + n_extra + n_after + n_out]
        scr = refs[2 + n_extra + n_after + n_out:]
        r = lax.dot_general(a_ref[...].astype(BF16), b_ref[...].astype(BF16), dims, preferred_element_type=F32)

        def finish(v):
            if epi == "add":
                o_refs[0][...] = (v + add_scale * c_ref[...]).astype(out_dtype)
            elif epi == "relu2":
                p = jnp.maximum(v, 0.0)
                o_refs[0][...] = (p * p).astype(BF16)
                o_refs[1][...] = p.astype(BF16)
            elif epi == "mul2r":
                o_refs[0][...] = (v * (2.0 * c_ref[...].astype(F32))).astype(BF16)
            else:
                o_refs[0][...] = v.astype(out_dtype)

        if nk == 1:
            finish(r)
        else:
            acc = scr[0]
            kk = pl.program_id(2)

            @pl.when(kk == 0)
            def _():
                acc[...] = r

            @pl.when(kk > 0)
            def _():
                acc[...] += r

            @pl.when(kk == nk - 1)
            def _():
                finish(acc[...])

    a_spec = pl.BlockSpec((tk, tm), lambda i, j, kk: (kk, i)) if ta else pl.BlockSpec((tm, tk), lambda i, j, kk: (i, kk))
    if b_chunks and tb:
        kc = k // N_DEV // tk
        b_spec = pl.BlockSpec((None, tn, tk), lambda i, j, kk: (kk // kc, j, kk % kc))
    elif b_chunks:
        nc = n // N_DEV // tn
        b_spec = pl.BlockSpec((None, tk, tn), lambda i, j, kk: (j // nc, kk, j % nc))
    elif tb:
        b_spec = pl.BlockSpec((tn, tk), lambda i, j, kk: (j, kk))
    else:
        b_spec = pl.BlockSpec((tk, tn), lambda i, j, kk: (kk, j))
    mn_spec = pl.BlockSpec((tm, tn), lambda i, j, kk: (i, j))
    if o_chunks:
        oc = n // N_DEV // tn
        o_spec = pl.BlockSpec((None, tm, tn), lambda i, j, kk: (j // oc, i, j % oc))
        o_shape = jax.ShapeDtypeStruct((N_DEV, m, n // N_DEV), out_dtype)
    else:
        o_spec, o_shape = mn_spec, jax.ShapeDtypeStruct((m, n), out_dtype)
    res = pl.pallas_call(
        body, grid=(m // tm, n // tn, nk),
        in_specs=[a_spec, b_spec] + [mn_spec] * n_extra + [pl.BlockSpec(memory_space=pl.ANY)] * n_after,
        out_specs=[o_spec] * n_out, out_shape=[o_shape] * n_out,
        scratch_shapes=[pltpu.VMEM((tm, tn), F32)] if nk > 1 else [],
        compiler_params=_params("parallel", "parallel", "arbitrary"), name=name,
    )(a, b, *([extra] if n_extra else []), *([after] if n_after else []))
    return res if n_out > 1 else res[0]


def _cast_bf16(v, *, name, tm=512):
    t, d = v.shape
    tm = min(tm, t)

    def body(v_ref, o_ref):
        o_ref[...] = v_ref[...].astype(BF16)

    spec = pl.BlockSpec((tm, d), lambda i: (i, 0))
    return pl.pallas_call(body, grid=(t // tm,), in_specs=[spec], out_specs=spec,
                          out_shape=jax.ShapeDtypeStruct((t, d), BF16), compiler_params=_params("parallel"), name=name)(v)


def _shift_down(v, s):
    if s == 0:
        return v
    row = lax.broadcasted_iota(jnp.int32, v.shape, 0)
    return jnp.where(row >= s, pltpu.roll(v, s, axis=0), 0.0)


def _shift_up(v, s):
    if s == 0:
        return v
    t = v.shape[0]
    row = lax.broadcasted_iota(jnp.int32, v.shape, 0)
    return jnp.where(row < t - s, pltpu.roll(v, t - s, axis=0), 0.0)


def _post_col(j):
    return (j % GDN_HEADS) * 3 + j // GDN_HEADS


def _gdn_prep_fwd(proj, conv_w):
    t = proj.shape[0]

    def body(x_ref, w_ref, o_ref):
        j = pl.program_id(0)
        x = x_ref[...]
        y = jnp.zeros_like(x)
        for tap in range(CONV_K):
            y = y + w_ref[tap:tap + 1, :] * _shift_down(x, CONV_K - 1 - tap)
        c = y * jax.nn.sigmoid(y)
        nrm = c * lax.rsqrt(jnp.sum(c * c, axis=1, keepdims=True) + NORM_EPS)
        o_ref[...] = jnp.where(j < 2 * GDN_HEADS, nrm, c)

    return pl.pallas_call(
        body, grid=(QKV_COLS // LANE,),
        in_specs=[pl.BlockSpec((t, LANE), lambda j: (0, j)), pl.BlockSpec((CONV_K, LANE), lambda j: (0, j))],
        out_specs=pl.BlockSpec((t, LANE), lambda j: (0, _post_col(j))),
        out_shape=jax.ShapeDtypeStruct((t, QKV_COLS), F32),
        compiler_params=_params("parallel"), name="gdn_prep_fwd",
    )(proj, conv_w)


def _gdn_prep_bwd(proj, conv_w, dpost, dproj):
    t = proj.shape[0]

    def body(x_ref, w_ref, d_ref, _, dx_ref, dw_ref):
        j = pl.program_id(0)
        x = x_ref[...]
        xs = [_shift_down(x, CONV_K - 1 - tap) for tap in range(CONV_K)]
        y = jnp.zeros_like(x)
        for tap in range(CONV_K):
            y = y + w_ref[tap:tap + 1, :] * xs[tap]
        sig = jax.nn.sigmoid(y)
        c = y * sig
        r = lax.rsqrt(jnp.sum(c * c, axis=1, keepdims=True) + NORM_EPS)
        nrm = c * r
        d = d_ref[...]
        dc_norm = r * (d - nrm * jnp.sum(d * nrm, axis=1, keepdims=True))
        dc = jnp.where(j < 2 * GDN_HEADS, dc_norm, d)
        dy = dc * (sig * (1.0 + y * (1.0 - sig)))
        dx = jnp.zeros_like(x)
        for tap in range(CONV_K):
            dx = dx + _shift_up(w_ref[tap:tap + 1, :] * dy, CONV_K - 1 - tap)
            dw_ref[tap:tap + 1, :] = jnp.sum(dy * xs[tap], axis=0, keepdims=True)
        dx_ref[...] = dx.astype(dx_ref.dtype)

    return pl.pallas_call(
        body, grid=(QKV_COLS // LANE,),
        in_specs=[pl.BlockSpec((t, LANE), lambda j: (0, j)), pl.BlockSpec((CONV_K, LANE), lambda j: (0, j)),
                  pl.BlockSpec((t, LANE), lambda j: (0, _post_col(j))), pl.BlockSpec(memory_space=pl.ANY)],
        out_specs=[pl.BlockSpec((t, LANE), lambda j: (0, j)), pl.BlockSpec((CONV_K, LANE), lambda j: (0, j))],
        out_shape=[jax.ShapeDtypeStruct(dproj.shape, dproj.dtype), jax.ShapeDtypeStruct((CONV_K, QKV_COLS), F32)],
        input_output_aliases={3: 0},
        compiler_params=_params("parallel"), name="gdn_prep_bwd",
    )(proj, conv_w, dpost, dproj)


def _softplus(v):
    return jnp.maximum(v, 0.0) + jnp.log(1.0 + jnp.exp(-jnp.abs(v)))


def _tri_inv(low, nn):
    r = lax.broadcasted_iota(jnp.int32, (CHUNK, CHUNK), 0)
    c = lax.broadcasted_iota(jnp.int32, (CHUNK, CHUNK), 1)
    eye = (r == c).astype(F32)
    same_blk = lax.shift_right_logical(r, 4) == lax.shift_right_logical(c, 4)
    diag = jnp.where(same_blk, low, 0.0)
    off = low - diag
    n1 = -diag
    n2 = nn(n1, n1)
    n4 = nn(n2, n2)
    n8 = nn(n4, n4)
    inv_d = nn(nn(nn(eye + n1, eye + n2), eye + n4), eye + n8)
    m1 = nn(inv_d, off)
    m2 = nn(m1, m1)
    return nn(nn(eye - m1, eye + m2), inv_d)


@jax.custom_vjp
def _tri_inv_known(low, t_inv):
    return t_inv


def _tri_inv_known_fwd(low, t_inv):
    return t_inv, t_inv


def _tri_inv_known_bwd(t_inv, g):
    _, nt, tn = _FDOT_BATCH_PLAIN
    return -nt(tn(t_inv, g), t_inv), jnp.zeros_like(t_inv)


_tri_inv_known.defvjp(_tri_inv_known_fwd, _tri_inv_known_bwd)


LOCAL_HEADS_PER_STEP = 8


def _gdn_local_fn(qkv, ba, alog_row, dtb_row, first_head, bdots, fdots, t_known=None):
    nn, nt, tn = bdots
    fnn = fdots[0]
    n_heads = qkv.shape[1] // (3 * HEAD_DIM)
    part = lambda i, p: qkv[:, (3 * i + p) * HEAD_DIM:(3 * i + p + 1) * HEAD_DIM]
    q = jnp.stack([part(i, 0) for i in range(n_heads)]) * (HEAD_DIM ** -0.5)
    k = jnp.stack([part(i, 1) for i in range(n_heads)])
    v = jnp.stack([part(i, 2) for i in range(n_heads)])
    lane = lax.broadcasted_iota(jnp.int32, ba.shape, 1)
    bg = jnp.where(lane < GDN_HEADS, jax.nn.sigmoid(ba), -jnp.exp(alog_row) * _softplus(ba + dtb_row))
    pick = lambda l: jnp.sum(jnp.where(lane == l, bg, 0.0), axis=1, keepdims=True)
    beta = jnp.stack([pick(first_head + i) for i in range(n_heads)])
    g = jnp.stack([pick(first_head + i + GDN_HEADS) for i in range(n_heads)])

    r = lax.broadcasted_iota(jnp.int32, (CHUNK, CHUNK), 0)
    c = lax.broadcasted_iota(jnp.int32, (CHUNK, CHUNK), 1)
    incl = r >= c
    strict = r > c
    eye = r == c

    def to_row(col):
        return jnp.sum(jnp.where(eye, col, 0.0), axis=1, keepdims=True)

    gc = jnp.sum(jnp.where(incl, to_row(g), 0.0), axis=2, keepdims=True)
    diff = gc - to_row(gc)
    decay = jnp.where(incl, jnp.exp(jnp.where(incl, diff, 0.0)), 0.0)
    k_beta = k * beta
    v_beta = v * beta
    low = jnp.where(strict, nt(k_beta, k) * decay, 0.0)
    t_inv = _tri_inv(low, fnn) if t_known is None else _tri_inv_known(low, t_known)
    eg = jnp.exp(gc)
    u = fnn(t_inv, v_beta)
    w = fnn(t_inv, k_beta * eg)
    attn = jnp.where(incl, nt(q, k) * decay, 0.0)
    last = lax.broadcasted_iota(jnp.int32, (CHUNK, 1), 0) == CHUNK - 1
    g_last = jnp.sum(jnp.where(last, gc, 0.0), axis=1, keepdims=True)
    kdec = k * jnp.exp(g_last - gc)
    elast = jnp.broadcast_to(jnp.exp(g_last), (n_heads, 1, LANE))
    return u, w, q * eg, kdec, attn, elast, t_inv


def _gdn_state_fn(u, w, qg, kdec, attn, elast, state, bdots):
    nn, _, tn = bdots
    v_new = u - nn(w, state)
    o = nn(qg, state) + nn(attn, v_new)
    return o, state * elast + tn(kdec, v_new)


def _gdn_local_fwd(post, proj, alog_row, dtb_row):
    t = post.shape[0]
    n_chunks = t // CHUNK
    hb = LOCAL_HEADS_PER_STEP

    def body(qkv_ref, ba_ref, al_ref, dt_ref, u_ref, w_ref, qg_ref, kd_ref, at_ref, el_ref, ti_ref):
        u, w, qg, kdec, attn, elast, t_inv = _gdn_local_fn(qkv_ref[...], ba_ref[...], al_ref[...], dt_ref[...],
                                                           pl.program_id(1) * hb, _BDOT_BATCH_PLAIN, _FDOT_BATCH_PLAIN)
        for i in range(hb):
            cols = slice(i * HEAD_DIM, (i + 1) * HEAD_DIM)
            u_ref[:, cols] = u[i]
            w_ref[:, cols] = w[i].astype(BF16)
            qg_ref[:, cols] = qg[i].astype(BF16)
            kd_ref[:, cols] = kdec[i].astype(BF16)
        at_ref[...] = attn.astype(BF16)
        el_ref[:, 0] = elast
        ti_ref[...] = t_inv

    wide = pl.BlockSpec((CHUNK, hb * HEAD_DIM), lambda n, j: (n, j))
    square = pl.BlockSpec((hb, CHUNK, CHUNK), lambda n, j: (j, n, 0))
    row = pl.BlockSpec((1, LANE), lambda n, j: (0, 0))
    res = pl.pallas_call(
        body, grid=(n_chunks, GDN_HEADS // hb),
        in_specs=[pl.BlockSpec((CHUNK, hb * 3 * HEAD_DIM), lambda n, j: (n, j)),
                  pl.BlockSpec((CHUNK, LANE), lambda n, j: (n, BA_BLK)), row, row],
        out_specs=[wide, wide, wide, wide, square, pl.BlockSpec((hb, 1, 1, LANE), lambda n, j: (j, n, 0, 0)), square],
        out_shape=[jax.ShapeDtypeStruct((t, GDN_WIDTH), F32), jax.ShapeDtypeStruct((t, GDN_WIDTH), BF16),
                   jax.ShapeDtypeStruct((t, GDN_WIDTH), BF16), jax.ShapeDtypeStruct((t, GDN_WIDTH), BF16),
                   jax.ShapeDtypeStruct((GDN_HEADS, t, CHUNK), BF16),
                   jax.ShapeDtypeStruct((GDN_HEADS, n_chunks, 1, LANE), F32),
                   jax.ShapeDtypeStruct((GDN_HEADS, t, CHUNK), F32)],
        compiler_params=_params("parallel", "parallel"), name="gdn_local_fwd",
    )(post, proj, alog_row, dtb_row)
    return tuple(res[:6]), res[6]


def _by_head(ref):
    return jnp.stack([ref[:, h * HEAD_DIM:(h + 1) * HEAD_DIM] for h in range(ref.shape[1] // HEAD_DIM)])


def _gdn_state_specs(n_of):
    wide = pl.BlockSpec((CHUNK, GDN_WIDTH), lambda n: (n_of(n), 0))
    attn = pl.BlockSpec((GDN_HEADS, CHUNK, CHUNK), lambda n: (0, n_of(n), 0))
    elast = pl.BlockSpec((GDN_HEADS, 1, 1, LANE), lambda n: (0, n_of(n), 0, 0))
    saved = pl.BlockSpec((GDN_HEADS, 1, HEAD_DIM, HEAD_DIM), lambda n: (0, n_of(n), 0, 0))
    return wide, attn, elast, saved


def _gdn_state_fwd(u, w, qg, kdec, attn, elast):
    t = u.shape[0]
    n_chunks = t // CHUNK

    def body(u_ref, w_ref, qg_ref, kd_ref, at_ref, el_ref, o_ref, save_ref, state_ref):
        @pl.when(pl.program_id(0) == 0)
        def _():
            state_ref[...] = jnp.zeros_like(state_ref)

        state = state_ref[...]
        save_ref[:, 0] = state
        o, new_state = _gdn_state_fn(_by_head(u_ref), _by_head(w_ref), _by_head(qg_ref), _by_head(kd_ref), at_ref[...],
                                     el_ref[:, 0], state, _BDOT_BATCH_PLAIN)
        for h in range(GDN_HEADS):
            o_ref[:, h * HEAD_DIM:(h + 1) * HEAD_DIM] = o[h]
        state_ref[...] = new_state

    wide, attn_spec, elast_spec, saved_spec = _gdn_state_specs(lambda n: n)
    return pl.pallas_call(
        body, grid=(n_chunks,), in_specs=[wide, wide, wide, wide, attn_spec, elast_spec],
        out_specs=[wide, saved_spec],
        out_shape=[jax.ShapeDtypeStruct((t, GDN_WIDTH), F32),
                   jax.ShapeDtypeStruct((GDN_HEADS, n_chunks, HEAD_DIM, HEAD_DIM), F32)],
        scratch_shapes=[pltpu.VMEM((GDN_HEADS, HEAD_DIM, HEAD_DIM), F32)],
        compiler_params=_params("arbitrary"), name="gdn_state_fwd",
    )(u, w, qg, kdec, attn, elast)


def _gdn_state_bwd(u, w, qg, kdec, attn, elast, saved, do):
    t = u.shape[0]
    n_chunks = t // CHUNK
    last = n_chunks - 1

    def body(u_ref, w_ref, qg_ref, kd_ref, at_ref, el_ref, save_ref, do_ref,
             du_ref, dw_ref, dqg_ref, dkd_ref, dat_ref, del_ref, dstate_ref):
        @pl.when(pl.program_id(0) == 0)
        def _():
            dstate_ref[...] = jnp.zeros_like(dstate_ref)

        _, vjp = jax.vjp(
            lambda *a: _gdn_state_fn(*a, _BDOT_BATCH_VJP), _by_head(u_ref), _by_head(w_ref).astype(F32),
            _by_head(qg_ref).astype(F32), _by_head(kd_ref).astype(F32), at_ref[...].astype(F32), el_ref[:, 0],
            save_ref[:, 0])
        du, dw, dqg, dkd, dat, de, dstate = vjp((_by_head(do_ref), dstate_ref[...]))
        for h in range(GDN_HEADS):
            cols = slice(h * HEAD_DIM, (h + 1) * HEAD_DIM)
            du_ref[:, cols] = du[h]
            dw_ref[:, cols] = dw[h]
            dqg_ref[:, cols] = dqg[h]
            dkd_ref[:, cols] = dkd[h]
        dat_ref[...] = dat
        del_ref[:, 0] = de
        dstate_ref[...] = dstate

    wide, attn_spec, elast_spec, saved_spec = _gdn_state_specs(lambda n: last - n)
    wide_f32 = jax.ShapeDtypeStruct((t, GDN_WIDTH), F32)
    return pl.pallas_call(
        body, grid=(n_chunks,), in_specs=[wide, wide, wide, wide, attn_spec, elast_spec, saved_spec, wide],
        out_specs=[wide, wide, wide, wide, attn_spec, elast_spec],
        out_shape=[wide_f32, wide_f32, wide_f32, wide_f32, jax.ShapeDtypeStruct((GDN_HEADS, t, CHUNK), F32),
                   jax.ShapeDtypeStruct((GDN_HEADS, n_chunks, 1, LANE), F32)],
        scratch_shapes=[pltpu.VMEM((GDN_HEADS, HEAD_DIM, HEAD_DIM), F32)],
        compiler_params=_params("arbitrary"), name="gdn_state_bwd",
    )(u, w, qg, kdec, attn, elast, saved, do)


def _gdn_local_bwd(post, proj, alog_row, dtb_row, t_inv, cots, dproj):
    t = post.shape[0]
    n_chunks = t // CHUNK
    hb = LOCAL_HEADS_PER_STEP
    n_steps = GDN_HEADS // hb

    def body(qkv_ref, ba_ref, al_ref, dt_ref, ti_ref, du_ref, dw_ref, dqg_ref, dkd_ref, dat_ref, del_ref, _,
             dqkv_ref, dba_ref, dal_ref, ddt_ref, dba_acc):
        n = pl.program_id(0)
        j = pl.program_id(1)

        @pl.when((n == 0) & (j == 0))
        def _():
            dal_ref[...] = jnp.zeros_like(dal_ref)
            ddt_ref[...] = jnp.zeros_like(ddt_ref)

        @pl.when(j == 0)
        def _():
            dba_acc[...] = jnp.zeros_like(dba_acc)

        t_known = ti_ref[...]
        _, vjp = jax.vjp(
            lambda a, b, c, d: _gdn_local_fn(a, b, c, d, j * hb, _BDOT_BATCH_VJP, _FDOT_BATCH_VJP, t_known)[:6],
            qkv_ref[...], ba_ref[...], al_ref[...], dt_ref[...])
        dqkv, dba, dal, ddt = vjp((_by_head(du_ref), _by_head(dw_ref), _by_head(dqg_ref), _by_head(dkd_ref), dat_ref[...],
                                   del_ref[:, 0]))
        dqkv_ref[...] = dqkv
        dba_acc[...] += dba
        dal_ref[...] += dal
        ddt_ref[...] += ddt

        @pl.when(j == n_steps - 1)
        def _():
            dba_ref[:, 0:LANE] = dba_acc[...].astype(dba_ref.dtype)
            dba_ref[:, LANE:2 * LANE] = jnp.zeros((CHUNK, LANE), dba_ref.dtype)

    wide = pl.BlockSpec((CHUNK, hb * HEAD_DIM), lambda n, j: (n, j))
    qkv_spec = pl.BlockSpec((CHUNK, hb * 3 * HEAD_DIM), lambda n, j: (n, j))
    row = pl.BlockSpec((1, LANE), lambda n, j: (0, 0))
    return pl.pallas_call(
        body, grid=(n_chunks, n_steps),
        in_specs=[qkv_spec, pl.BlockSpec((CHUNK, LANE), lambda n, j: (n, BA_BLK)), row, row,
                  pl.BlockSpec((hb, CHUNK, CHUNK), lambda n, j: (j, n, 0)), wide, wide, wide, wide,
                  pl.BlockSpec((hb, CHUNK, CHUNK), lambda n, j: (j, n, 0)),
                  pl.BlockSpec((hb, 1, 1, LANE), lambda n, j: (j, n, 0, 0)), pl.BlockSpec(memory_space=pl.ANY)],
        out_specs=[qkv_spec, pl.BlockSpec((CHUNK, 2 * LANE), lambda n, j: (n, BA_BLK // 2)), row, row],
        out_shape=[jax.ShapeDtypeStruct((t, QKV_COLS), F32), jax.ShapeDtypeStruct(dproj.shape, dproj.dtype),
                   jax.ShapeDtypeStruct((1, LANE), F32), jax.ShapeDtypeStruct((1, LANE), F32)],
        input_output_aliases={11: 1},
        scratch_shapes=[pltpu.VMEM((CHUNK, LANE), F32)],
        compiler_params=_params("arbitrary", "arbitrary"), name="gdn_local_bwd",
    )(post, proj, alog_row, dtb_row, t_inv, *cots, dproj)


def _onorm_fn(o, z, w):
    return o * lax.rsqrt(jnp.mean(o * o, axis=1, keepdims=True) + NORM_EPS) * w * (z * jax.nn.sigmoid(z))


_Z_WIDE_BLK = Z_OFF // GDN_WIDTH


def _onorm_fwd(o_raw, proj, norm_w, mixin, tm=256):
    t = o_raw.shape[0]
    tm = min(tm, t)

    def body(o_ref, z_ref, w_ref, _, out_ref):
        for h in range(GDN_HEADS):
            cols = slice(h * HEAD_DIM, (h + 1) * HEAD_DIM)
            out_ref[:, cols] = _onorm_fn(o_ref[:, cols], z_ref[:, cols], w_ref[...]).astype(out_ref.dtype)

    wide = pl.BlockSpec((tm, GDN_WIDTH), lambda i: (i, 0))
    return pl.pallas_call(
        body, grid=(t // tm,),
        in_specs=[wide, pl.BlockSpec((tm, GDN_WIDTH), lambda i: (i, _Z_WIDE_BLK)), pl.BlockSpec((1, LANE), lambda i: (0, 0)),
                  pl.BlockSpec(memory_space=pl.ANY)],
        out_specs=wide, out_shape=jax.ShapeDtypeStruct(mixin.shape, mixin.dtype), input_output_aliases={3: 0},
        compiler_params=_params("parallel"), name="gdn_onorm_fwd",
    )(o_raw, proj, norm_w, mixin)


def _onorm_bwd(o_raw, proj, norm_w, dmixin, dproj, tm=256):
    t = o_raw.shape[0]
    tm = min(tm, t)

    def body(o_ref, z_ref, w_ref, d_ref, _, do_ref, dz_ref, dw_ref):
        @pl.when(pl.program_id(0) == 0)
        def _():
            dw_ref[...] = jnp.zeros_like(dw_ref)

        for h in range(GDN_HEADS):
            cols = slice(h * HEAD_DIM, (h + 1) * HEAD_DIM)
            _, vjp = jax.vjp(_onorm_fn, o_ref[:, cols], z_ref[:, cols], w_ref[...])
            do, dz, dw = vjp(d_ref[:, cols])
            do_ref[:, cols] = do
            dz_ref[:, cols] = dz.astype(dz_ref.dtype)
            dw_ref[...] += dw

    wide = pl.BlockSpec((tm, GDN_WIDTH), lambda i: (i, 0))
    gate = pl.BlockSpec((tm, GDN_WIDTH), lambda i: (i, _Z_WIDE_BLK))
    row = pl.BlockSpec((1, LANE), lambda i: (0, 0))
    return pl.pallas_call(
        body, grid=(t // tm,), in_specs=[wide, gate, row, wide, pl.BlockSpec(memory_space=pl.ANY)],
        out_specs=[wide, gate, row],
        out_shape=[jax.ShapeDtypeStruct((t, GDN_WIDTH), F32), jax.ShapeDtypeStruct(dproj.shape, dproj.dtype),
                   jax.ShapeDtypeStruct((1, LANE), F32)],
        input_output_aliases={4: 1},
        compiler_params=_params("arbitrary"), name="gdn_onorm_bwd",
    )(o_raw, proj, norm_w, dmixin, dproj)


def _pool_select(levels, gi):
    out = levels[-1]
    for lvl in range(len(levels) - 2, -1, -1):
        out = jnp.where(gi == lvl, levels[lvl], out)
    return out


def _pool_count(shape, gi):
    pos = lax.broadcasted_iota(jnp.int32, shape, 0)
    win = lax.shift_left(jnp.int32(2), gi)
    return jnp.minimum(pos + 1, win).astype(F32)


def _pooled(p, gi):
    acc = p
    levels = []
    for lvl in range(POOL_GROUPS):
        acc = acc + _shift_down(acc, 1 << lvl)
        levels.append(acc)
    return _pool_select(levels, gi) / _pool_count(p.shape, gi) - p


def _pool_fwd(proj, pool_w, pool_scale):
    t = proj.shape[0]

    def body(p_ref, w_ref, s_ref, out_ref):
        gi = pl.program_id(0)
        pooled = _pooled(p_ref[...], gi)
        out_ref[...] = (_BDOT_PLAIN[0](pooled, w_ref[0]) * s_ref[0]).astype(out_ref.dtype)

    return pl.pallas_call(
        body, grid=(POOL_GROUPS,),
        in_specs=[pl.BlockSpec((t, POOL_GROUP_DIM), lambda g: (0, POOL_BLK + g)),
                  pl.BlockSpec((1, POOL_GROUP_DIM, POOL_GROUP_DIM), lambda g: (g, 0, 0)),
                  pl.BlockSpec((1, 1, POOL_GROUP_DIM), lambda g: (g, 0, 0))],
        out_specs=pl.BlockSpec((t, POOL_GROUP_DIM), lambda g: (0, GDN_WIDTH // POOL_GROUP_DIM + g)),
        out_shape=jax.ShapeDtypeStruct((t, 2 * GDN_WIDTH), BF16),
        compiler_params=_params("parallel"), name="pool_fwd",
    )(proj, pool_w, pool_scale)


def _pool_bwd(proj, pool_w, pool_scale, dmixin):
    t = proj.shape[0]
    nn, nt, tn = _BDOT_PLAIN

    def body(p_ref, w_ref, s_ref, d_ref, dp_ref, dw_ref, ds_ref):
        gi = pl.program_id(0)
        p = p_ref[...]
        pooled = _pooled(p, gi)
        mixed = nn(pooled, w_ref[0])
        d = d_ref[...]
        ds_ref[0] = jnp.sum(d * mixed, axis=0, keepdims=True)
        dmixed = d * s_ref[0]
        dw_ref[0] = tn(pooled, dmixed)
        dpooled = nt(dmixed, w_ref[0])
        acc = dpooled / _pool_count(p.shape, gi)
        levels = []
        for lvl in range(POOL_GROUPS):
            acc = acc + _shift_up(acc, 1 << lvl)
            levels.append(acc)
        dp_ref[...] = (_pool_select(levels, gi) - dpooled).astype(dp_ref.dtype)

    return pl.pallas_call(
        body, grid=(POOL_GROUPS,),
        in_specs=[pl.BlockSpec((t, POOL_GROUP_DIM), lambda g: (0, POOL_BLK + g)),
                  pl.BlockSpec((1, POOL_GROUP_DIM, POOL_GROUP_DIM), lambda g: (g, 0, 0)),
                  pl.BlockSpec((1, 1, POOL_GROUP_DIM), lambda g: (g, 0, 0)),
                  pl.BlockSpec((t, POOL_GROUP_DIM), lambda g: (0, GDN_WIDTH // POOL_GROUP_DIM + g))],
        out_specs=[pl.BlockSpec((t, POOL_GROUP_DIM), lambda g: (0, POOL_BLK + g)),
                   pl.BlockSpec((1, POOL_GROUP_DIM, POOL_GROUP_DIM), lambda g: (g, 0, 0)),
                   pl.BlockSpec((1, 1, POOL_GROUP_DIM), lambda g: (g, 0, 0))],
        out_shape=[jax.ShapeDtypeStruct((t, PROJ_COLS), BF16),
                   jax.ShapeDtypeStruct((POOL_GROUPS, POOL_GROUP_DIM, POOL_GROUP_DIM), F32),
                   jax.ShapeDtypeStruct((POOL_GROUPS, 1, POOL_GROUP_DIM), F32)],
        compiler_params=_params("parallel"), name="pool_bwd",
    )(proj, pool_w, pool_scale, dmixin)


def _ln_stats(s):
    mu = jnp.mean(s, axis=1, keepdims=True)
    xc = s - mu
    var = jnp.mean(xc * xc, axis=1, keepdims=True)
    rstd = lax.rsqrt(var + LN_EPS)
    return xc * rstd, rstd


def _mm_ln(a, b, h_in, g, bias, *, name, tm=256):
    t, k = a.shape
    d = b.shape[1]
    tm = min(tm, t)

    def body(a_ref, b_ref, h_ref, g_ref, bias_ref, y_ref, o_ref, o16_ref):
        y = jnp.dot(a_ref[...].astype(BF16), b_ref[...], preferred_element_type=F32)
        y_ref[...] = y
        xhat, _ = _ln_stats(ALPHA * h_ref[...] + y)
        out = xhat * g_ref[...] + bias_ref[...]
        o_ref[...] = out
        o16_ref[...] = out.astype(BF16)

    row = pl.BlockSpec((tm, d), lambda i: (i, 0))
    vec = pl.BlockSpec((1, d), lambda i: (0, 0))
    return pl.pallas_call(
        body, grid=(t // tm,),
        in_specs=[pl.BlockSpec((tm, k), lambda i: (i, 0)), pl.BlockSpec((k, d), lambda i: (0, 0)), row, vec, vec],
        out_specs=[row, row, row],
        out_shape=[jax.ShapeDtypeStruct((t, d), F32), jax.ShapeDtypeStruct((t, d), F32), jax.ShapeDtypeStruct((t, d), BF16)],
        compiler_params=_params("parallel"), name=name,
    )(a, b, h_in, g, bias)


def _ln_backward(xhat, rstd, dout, gain):
    dxhat = dout * gain
    m1 = jnp.mean(dxhat, axis=1, keepdims=True)
    m2 = jnp.mean(dxhat * xhat, axis=1, keepdims=True)
    return (rstd * (dxhat - m1 - xhat * m2), jnp.sum(dout * xhat, axis=0, keepdims=True),
            jnp.sum(dout, axis=0, keepdims=True))


def _ln_loss(h_in, y, g, b, target, *, name, tm=256):
    t, d = h_in.shape
    tm = min(tm, t)

    def body(h_ref, y_ref, g_ref, b_ref, t_ref, sq_ref, ds_ref, ds16_ref, dg_ref, dbias_ref):
        @pl.when(pl.program_id(0) == 0)
        def _():
            sq_ref[...] = jnp.zeros_like(sq_ref)
            dg_ref[...] = jnp.zeros_like(dg_ref)
            dbias_ref[...] = jnp.zeros_like(dbias_ref)

        xhat, rstd = _ln_stats(ALPHA * h_ref[...] + y_ref[...])
        err = xhat * g_ref[...] + b_ref[...] - t_ref[...]
        sq_ref[...] += jnp.sum(jnp.sum(err * err, axis=1, keepdims=True), axis=0, keepdims=True)
        ds, dg, dbias = _ln_backward(xhat, rstd, err * (1.0 / d), g_ref[...])
        ds_ref[...] = ds
        ds16_ref[...] = ds.astype(BF16)
        dg_ref[...] += dg
        dbias_ref[...] += dbias

    row = pl.BlockSpec((tm, d), lambda i: (i, 0))
    vec = pl.BlockSpec((1, d), lambda i: (0, 0))
    return pl.pallas_call(
        body, grid=(t // tm,), in_specs=[row, row, vec, vec, row],
        out_specs=[pl.BlockSpec((1, LANE), lambda i: (0, 0)), row, row, vec, vec],
        out_shape=[jax.ShapeDtypeStruct((1, LANE), F32), jax.ShapeDtypeStruct((t, d), F32),
                   jax.ShapeDtypeStruct((t, d), BF16), jax.ShapeDtypeStruct((1, d), F32), jax.ShapeDtypeStruct((1, d), F32)],
        compiler_params=_params("arbitrary"), name=name,
    )(h_in, y, g, b, target)


def _ln_bwd(h_in, y, g, d_a, d_b, *, name, tm=256):
    t, d = h_in.shape
    tm = min(tm, t)
    has_b = d_b is not None

    def body(*refs):
        if has_b:
            h_ref, y_ref, g_ref, da_ref, db_ref, ds_ref, ds16_ref, dg_ref, dbias_ref = refs
        else:
            h_ref, y_ref, g_ref, da_ref, ds_ref, ds16_ref, dg_ref, dbias_ref = refs

        @pl.when(pl.program_id(0) == 0)
        def _():
            dg_ref[...] = jnp.zeros_like(dg_ref)
            dbias_ref[...] = jnp.zeros_like(dbias_ref)

        xhat, rstd = _ln_stats(ALPHA * h_ref[...] + y_ref[...])
        dout = da_ref[...]
        if has_b:
            dout = dout + ALPHA * db_ref[...]
        ds, dg, dbias = _ln_backward(xhat, rstd, dout, g_ref[...])
        ds_ref[...] = ds
        ds16_ref[...] = ds.astype(BF16)
        dg_ref[...] += dg
        dbias_ref[...] += dbias

    row = pl.BlockSpec((tm, d), lambda i: (i, 0))
    vec = pl.BlockSpec((1, d), lambda i: (0, 0))
    args = [h_in, y, g, d_a] + ([d_b] if has_b else [])
    return pl.pallas_call(
        body, grid=(t // tm,), in_specs=[row, row, vec, row] + ([row] if has_b else []),
        out_specs=[row, row, vec, vec],
        out_shape=[jax.ShapeDtypeStruct((t, d), F32), jax.ShapeDtypeStruct((t, d), BF16),
                   jax.ShapeDtypeStruct((1, d), F32), jax.ShapeDtypeStruct((1, d), F32)],
        compiler_params=_params("arbitrary"), name=name,
    )(*args)


def _attn_fn(q, k, v, dots):
    nn, nt, _ = dots
    s = nt(q, k) * (XATTN_HEAD_DIM ** -0.5)
    s = s - lax.stop_gradient(jnp.max(s, axis=1, keepdims=True))
    e = jnp.exp(s)
    p = e / jnp.sum(e, axis=1, keepdims=True)
    return nn(p, v)


def _attn_fwd(q, k, v, tq=2048):
    t = q.shape[0]
    tq = min(tq, t)

    def body(q_ref, k_ref, v_ref, o_ref):
        o_ref[...] = _attn_fn(q_ref[...], k_ref[...], v_ref[...], _BDOT_PLAIN).astype(BF16)

    qs = pl.BlockSpec((tq, XATTN_HEAD_DIM), lambda h, i: (i, h))
    ks = pl.BlockSpec((MEM_LEN, XATTN_HEAD_DIM), lambda h, i: (0, h))
    return pl.pallas_call(
        body, grid=(XATTN_HEADS, t // tq), in_specs=[qs, ks, ks], out_specs=qs,
        out_shape=jax.ShapeDtypeStruct(q.shape, BF16), compiler_params=_params("parallel", "parallel"), name="xattn_fwd",
    )(q, k, v)


def _attn_bwd(q, k, v, do, tq=1024):
    t = q.shape[0]
    tq = min(tq, t)

    def body(q_ref, k_ref, v_ref, do_ref, dq_ref, dk_ref, dv_ref):
        @pl.when(pl.program_id(1) == 0)
        def _():
            dk_ref[...] = jnp.zeros_like(dk_ref)
            dv_ref[...] = jnp.zeros_like(dv_ref)

        _, vjp = jax.vjp(lambda a, b, c: _attn_fn(a, b, c, _BDOT_VJP), q_ref[...].astype(F32), k_ref[...].astype(F32),
                         v_ref[...].astype(F32))
        dq, dk, dv = vjp(do_ref[...].astype(F32))
        dq_ref[...] = dq.astype(BF16)
        dk_ref[...] += dk
        dv_ref[...] += dv

    qs = pl.BlockSpec((tq, XATTN_HEAD_DIM), lambda h, i: (i, h))
    ks = pl.BlockSpec((MEM_LEN, XATTN_HEAD_DIM), lambda h, i: (0, h))
    return pl.pallas_call(
        body, grid=(XATTN_HEADS, t // tq), in_specs=[qs, ks, ks, qs], out_specs=[qs, ks, ks],
        out_shape=[jax.ShapeDtypeStruct(q.shape, BF16), jax.ShapeDtypeStruct(k.shape, F32), jax.ShapeDtypeStruct(v.shape, F32)],
        compiler_params=_params("parallel", "arbitrary"), name="xattn_bwd",
    )(q, k, v, do)


def _local_step(x, x16, mem, target, weights_of, grads_ready):
    def behind(vec, token):
        return vec if token is None else vec + token

    w = dict(weights_of("mixer", None))
    proj = _mm(x16, w["w_in"], tb=True, tn=768, name="mm_in_proj")
    mixin = _pool_fwd(proj, w["pool_w"], w["pool_scale"])
    post = _gdn_prep_fwd(proj, w["conv_w"])
    token = weights_of("ahead_conv", post)
    chunked, t_inv = _gdn_local_fwd(post, proj, behind(w["alog_row"], token), w["dtb_row"])
    o_raw, saved = _gdn_state_fwd(*chunked)
    token = weights_of("ahead_scan", o_raw)
    mixin = _onorm_fwd(o_raw, proj, behind(w["gdn_norm_w"], token), mixin)
    w.update(weights_of("attn", mixin))
    mix, h1, h1_16 = _mm_ln(mixin, w["w_out"], x, w["ln1_g"], w["ln1_b"], name="mm_out_proj_ln1")
    xq = _mm(h1_16, w["xq_w"], out_dtype=BF16, name="mm_xq")
    xk = _mm(mem, w["xk_w"], out_dtype=BF16, name="mm_xk")
    xv = _mm(mem, w["xv_w"], out_dtype=BF16, name="mm_xv")
    xo = _attn_fwd(xq, xk, xv)
    token = weights_of("ahead_attn", xo)
    if token is not None:
        xo, _ = lax.optimization_barrier((xo, token))
    xa, h2, h2_16 = _mm_ln(xo, w["xo_w"], h1, w["ln2_g"], w["ln2_b"], name="mm_xo_ln2")
    w.update(weights_of("up", h2_16))
    act, relu = _mm(h2_16, w["w_up"], b_chunks=True, epi="relu2", name="mm_up")
    w.update(weights_of("down", act))
    ff = _mm(act, w["w_down"], tn=512, tk=2048, name="mm_down")
    g = {}
    sq, ds3, ds3_16, g["ln3_g"], g["ln3_b"] = _ln_loss(h2, ff, w["ln3_g"], w["ln3_b"], target, name="ln3_loss")

    gw_down = _mm(act, ds3_16, ta=True, out_dtype=BF16, tm=512, tn=D_MODEL, name="mm_gw_down")
    du = _mm(ds3_16, w["w_down"], tb=True, epi="mul2r", extra=relu, name="mm_du")
    gw_up = _mm(h2_16, du, ta=True, out_dtype=BF16, o_chunks=True, name="mm_gw_up")
    token = grads_ready("mlp", {"w_down": gw_down, "w_up": gw_up})
    dh2 = _mm(du, w["w_up"], tb=True, b_chunks=True, tn=1024, tk=1024, name="mm_dh2")
    ds2, ds2_16, g["ln2_g"], g["ln2_b"] = _ln_bwd(h1, xa, behind(w["ln2_g"], token), dh2, ds3, name="ln2_bwd")
    gw_xo = _mm(xo, ds2_16, ta=True, out_dtype=BF16, name="mm_gw_xo")
    dxo = _mm(ds2_16, w["xo_w"], tb=True, out_dtype=BF16, name="mm_dxo")
    dxq, dxk, dxv = _attn_bwd(xq, xk, xv, dxo)
    gw_xq = _mm(h1_16, dxq, ta=True, out_dtype=BF16, name="mm_gw_xq")
    gw_xk = _mm(mem, dxk, ta=True, out_dtype=BF16, name="mm_gw_xk")
    gw_xv = _mm(mem, dxv, ta=True, out_dtype=BF16, name="mm_gw_xv")
    token = grads_ready("attn", {"xo_w": gw_xo, "xq_w": gw_xq, "xk_w": gw_xk, "xv_w": gw_xv})
    dh1 = _mm(dxq, w["xq_w"], tb=True, name="mm_dh1")
    ds1, ds1_16, g["ln1_g"], g["ln1_b"] = _ln_bwd(x, mix, behind(w["ln1_g"], token), dh1, ds2, name="ln1_bwd")
    gw_out = _mm(mixin, ds1_16, ta=True, out_dtype=BF16, name="mm_gw_out")
    dmixin = _mm(ds1_16, w["w_out"], tb=True, name="mm_dmixin")
    dproj, gw_pool, g["pool_scale"] = _pool_bwd(proj, w["pool_w"], w["pool_scale"], dmixin)
    token = grads_ready("mix", {"w_out": gw_out, "pool_w": gw_pool})
    do_raw, dproj, g["gdn_norm_w"] = _onorm_bwd(o_raw, proj, behind(w["gdn_norm_w"], token), dmixin, dproj)
    cots = _gdn_state_bwd(*chunked, saved, do_raw)
    token = grads_ready("tick", {"after": cots[0]})
    dpost, dproj, g["alog_row"], g["dtb_row"] = _gdn_local_bwd(post, proj, behind(w["alog_row"], token), w["dtb_row"],
                                                               t_inv, cots, dproj)
    dproj, g["conv_w"] = _gdn_prep_bwd(proj, w["conv_w"], dpost, dproj)
    token = grads_ready("small", {**g, "sq": sq})
    gw_in = _mm(dproj, x16, ta=True, out_dtype=BF16, tm=768, tn=D_MODEL, after=token, name="mm_gw_in")
    token = grads_ready("in", {"w_in": gw_in})
    grad_x = _mm(dproj, w["w_in"], tk=1792, epi="add", extra=ds1, add_scale=ALPHA, after=token, name="mm_dx")
    return sq, grad_x, g


_VECTORS = ("a_log", "dt_bias", "gdn_norm_w", "pool_scale", "ln1_g", "ln1_b", "ln2_g", "ln2_b", "ln3_g", "ln3_b")
_BA_SPLIT = BA_OFF + 2 * GDN_HEADS


def _lane_row(v, offset):
    return jnp.zeros((1, LANE), F32).at[0, offset:offset + v.shape[0]].set(v)


_GROUP_VECTORS = {"mixer": (), "attn": ("ln1_g", "ln1_b", "ln2_g", "ln2_b"), "up": (), "down": ("ln3_g", "ln3_b")}


def _group_weights(group, full):
    w = {n: full[n].reshape(1, D_MODEL) for n in _GROUP_VECTORS[group]}
    if group == "mixer":
        w.update({
            "w_in": _w_in_padded(full["w_in"]),
            "conv_w": full["conv_w"],
            "alog_row": _lane_row(full["a_log"], GDN_HEADS),
            "dtb_row": _lane_row(full["dt_bias"], GDN_HEADS),
            "gdn_norm_w": full["gdn_norm_w"].reshape(1, LANE),
            "pool_w": full["pool_w"],
            "pool_scale": full["pool_scale"].reshape(POOL_GROUPS, 1, POOL_GROUP_DIM),
        })
    else:
        w.update({n: full[n] for n in dict(_GATHER_GROUPS)[group]})
    return w


def _w_in_row_map():
    per = IN_COLS // N_DEV
    gap = POOL_OFF - _BA_SPLIT
    pieces = []
    for d in range(N_DEV):
        lo, hi = d * per, (d + 1) * per
        if hi <= _BA_SPLIT:
            pieces.append([(0, lo, per)])
        elif lo >= _BA_SPLIT:
            pieces.append([(0, lo + gap, per)])
        else:
            pieces.append([(0, lo, _BA_SPLIT - lo), (_BA_SPLIT - lo, POOL_OFF, hi - _BA_SPLIT)])
    return pieces


_W_IN_LANES = 256


def _w_in_padded(blocks):
    def body(b_ref, o_ref):
        for d, pieces in enumerate(_w_in_row_map()):
            for src, dst, rows in pieces:
                o_ref[dst:dst + rows, :] = b_ref[d, src:src + rows, :]
        o_ref[_BA_SPLIT:POOL_OFF, :] = jnp.zeros((POOL_OFF - _BA_SPLIT, _W_IN_LANES), o_ref.dtype)

    n, per, cols = blocks.shape
    return pl.pallas_call(
        body, grid=(cols // _W_IN_LANES,), in_specs=[pl.BlockSpec((n, per, _W_IN_LANES), lambda j: (0, 0, j))],
        out_specs=pl.BlockSpec((PROJ_COLS, _W_IN_LANES), lambda j: (0, j)),
        out_shape=jax.ShapeDtypeStruct((PROJ_COLS, cols), blocks.dtype), compiler_params=_params("parallel"),
        name="w_in_padded")(blocks)


def _w_in_chunks(g):
    def body(g_ref, o_ref):
        for d, pieces in enumerate(_w_in_row_map()):
            for dst, src, rows in pieces:
                o_ref[d, dst:dst + rows, :] = g_ref[src:src + rows, :]

    cols = g.shape[1]
    per = IN_COLS // N_DEV
    return pl.pallas_call(
        body, grid=(cols // _W_IN_LANES,), in_specs=[pl.BlockSpec((PROJ_COLS, _W_IN_LANES), lambda j: (0, j))],
        out_specs=pl.BlockSpec((N_DEV, per, _W_IN_LANES), lambda j: (0, 0, j)),
        out_shape=jax.ShapeDtypeStruct((N_DEV, per, cols), g.dtype), compiler_params=_params("parallel"),
        name="w_in_chunks")(g)


def _finish_small_grads(g):
    out = {"conv_w": g["conv_w"]}
    out["a_log"] = g["alog_row"][0, GDN_HEADS:2 * GDN_HEADS]
    out["dt_bias"] = g["dtb_row"][0, GDN_HEADS:2 * GDN_HEADS]
    out["gdn_norm_w"] = g["gdn_norm_w"].reshape(LANE)
    out["pool_scale"] = g["pool_scale"].reshape(POOL_GROUPS * POOL_GROUP_DIM)
    for n in ("ln1_g", "ln1_b", "ln2_g", "ln2_b", "ln3_g", "ln3_b"):
        out[n] = g[n].reshape(D_MODEL)
    return out


def _adamw_math(w, g, m, v):
    m = ADAM_B1 * m + (1.0 - ADAM_B1) * g
    v = ADAM_B2 * v + (1.0 - ADAM_B2) * (g * g)
    m_hat = m / (1.0 - ADAM_B1 ** ADAM_STEP)
    v_hat = v / (1.0 - ADAM_B2 ** ADAM_STEP)
    delta = -ADAM_LR * (m_hat / (jnp.sqrt(v_hat) + ADAM_EPS) + ADAM_WD * w)
    return delta, m, v


ADAMW_TILE_ELEMS = 256 * 1024
CHIP_SUM_TILE_ELEMS = 1024 * 1024


def _shard_tile(r, c, elems):
    for rows in (1024, 512, 256, 128):
        if r % rows == 0 and rows * c <= elems:
            return rows, c
    if r % 128 == 0:
        return 128, c
    return r, 256 if c % 256 == 0 else c


def _adamw_shard(parts, own, me, w, m, v, *, after=None, name):
    s, r, c = parts.shape
    tr, tc = _shard_tile(r, c, ADAMW_TILE_ELEMS)
    assert r % tr == 0 and c % tc == 0, (name, r, c)
    unit_axis = w.ndim == 3
    at = (slice(None), 0, slice(None)) if unit_axis else Ellipsis

    def body(me_ref, p_ref, own_ref, w_ref, m_ref, v_ref, *rest):
        g_ref, d_ref, nm_ref, nv_ref = rest[-4:]
        mine = own_ref[...].astype(F32)
        g = None
        for i in range(s):
            part = jnp.where(me_ref[0] == i, mine, p_ref[i].astype(F32))
            g = part if g is None else g + part
        delta, nm, nv = _adamw_math(w_ref[at], g, m_ref[at], v_ref[at])
        g_ref[at] = g
        d_ref[at] = delta
        nm_ref[at] = nm
        nv_ref[at] = nv

    if unit_axis:
        blk = pl.BlockSpec((tr, 1, tc), lambda i, j, me_ref: (i, 0, j))
        out = jax.ShapeDtypeStruct((r, 1, c), F32)
    else:
        blk = pl.BlockSpec((tr, tc), lambda i, j, me_ref: (i, j))
        out = jax.ShapeDtypeStruct((r, c), F32)
    return pl.pallas_call(
        body,
        grid_spec=pltpu.PrefetchScalarGridSpec(
            num_scalar_prefetch=1, grid=(r // tr, c // tc),
            in_specs=[pl.BlockSpec((s, tr, tc), lambda i, j, me_ref: (0, i, j)),
                      pl.BlockSpec((None, tr, tc), lambda i, j, me_ref: (me_ref[0], i, j)), blk, blk, blk]
                     + ([] if after is None else [pl.BlockSpec(memory_space=pl.ANY)]),
            out_specs=[blk, blk, blk, blk]),
        out_shape=[out, out, out, out], compiler_params=_params("parallel", "parallel"), name=name,
    )(me, parts, own, w, m, v, *([] if after is None else [after]))


N_CHIPS = N_DEV // 2


def _chip_sums(chunks, from_sibling, core, *, name):
    n = len(chunks)
    _, r, c = chunks[0].shape
    assert all(a.shape == chunks[0].shape for a in chunks), name
    tr, tc = _shard_tile(r, c, CHIP_SUM_TILE_ELEMS // n)
    assert r % tr == 0 and c % tc == 0, (name, r, c)

    def body(core_ref, *refs):
        for a in range(n):
            refs[2 * n + a][...] = (refs[a][...].astype(F32) + refs[n + a][...].astype(F32)).astype(BF16)

    by_chip = pl.BlockSpec((None, tr, tc), lambda q, i, j, core_ref: (q, i, j))
    mine = pl.BlockSpec((None, tr, tc), lambda q, i, j, core_ref: (2 * q + core_ref[0], i, j))
    return pl.pallas_call(
        body,
        grid_spec=pltpu.PrefetchScalarGridSpec(
            num_scalar_prefetch=1, grid=(N_CHIPS, r // tr, c // tc),
            in_specs=[mine] * n + [by_chip] * n, out_specs=[by_chip] * n),
        out_shape=[jax.ShapeDtypeStruct((N_CHIPS, r, c), chunks[0].dtype)] * n,
        compiler_params=_params("parallel", "parallel", "parallel"), name=name,
    )(core, *chunks, *from_sibling)


def _place():
    return lax.axis_index("x"), lax.axis_index("y"), lax.axis_index("c")


def _slot(px, py, pc):
    return 4 * px + 2 * py + pc


_HBM = pl.BlockSpec(memory_space=pltpu.HBM)


_SEM = pl.BlockSpec(memory_space=pltpu.SEMAPHORE)
_ANY = pl.BlockSpec(memory_space=pl.ANY)
_EFFECT = pltpu.SideEffectType.DATAFLOW_SIDE_EFFECTING


def _peer(k, x, y, c):
    return (1 - x if k & 4 else x, 1 - y if k & 2 else y, 1 - c if k & 1 else c)


_EXCHANGE_BITS = {"gather_near": (1, 2, 4), "gather_relay": (6,), "gather_pass": (2, 4, 6),
                  "scatter_sibling": (1, 1, 1, 1), "scatter_chips": (2, 4, 6), "all_small": (1, 2, 3, 4, 5, 6, 7)}


def _exchange_copy(mode, src, land, w, i, place, send_sems, recv_sems, receiving):
    bits = _EXCHANGE_BITS[mode]
    k = bits[i]
    peer = _peer(k, *place)
    me = _slot(*place)
    if mode in ("gather_near", "all_small"):
        to, src_ref, sent_to, got_at = peer, src[w], me, _slot(*peer)
    elif mode == "gather_relay":
        x, y, c = place
        other = 1 - c
        to = (lax.bitwise_xor(x, c), lax.bitwise_xor(y, other), c)
        blk = _slot(lax.bitwise_xor(x, other), lax.bitwise_xor(y, c), c)
        src_ref, sent_to, got_at = land[w].at[blk], blk, _slot(*peer)
    elif mode == "gather_pass":
        blk = _slot(*peer)
        to, src_ref, sent_to, got_at = _peer(1, *place), land[w].at[blk], blk, _slot(*_peer(k | 1, *place))
    elif mode == "scatter_sibling":
        to, src_ref, sent_to, got_at = peer, src[w].at[2 * i + 1 - place[2]], i, i
    else:
        to, src_ref, sent_to, got_at = peer, src[w].at[_slot(*peer) // 2], me // 2, _slot(*peer) // 2
    sem = w * len(bits) + i
    return pltpu.make_async_remote_copy(
        src_ref=src_ref, dst_ref=land[w].at[got_at if receiving else sent_to], send_sem=send_sems.at[sem],
        recv_sem=recv_sems.at[sem], device_id=to, device_id_type=MESH)


def _exchange_start(mode, srcs, lands, after, *, name):
    ns, nl = len(srcs), len(lands)
    n_sem = nl * len(_EXCHANGE_BITS[mode])

    def body(*refs):
        src, land = refs[:ns], refs[ns:ns + nl]
        send_sems, recv_sems = refs[ns + nl + 1:ns + nl + 3]
        token = refs[-1]
        place = _place()
        for w in range(nl):
            for i in range(len(_EXCHANGE_BITS[mode])):
                _exchange_copy(mode, src, land, w, i, place, send_sems, recv_sems, receiving=False).start()
        token[...] = jnp.zeros_like(token)

    sems = pltpu.SemaphoreType.DMA((n_sem,))
    arrays = list(srcs) + list(lands)
    res = pl.pallas_call(
        body, name=name, in_specs=[_HBM] * (ns + nl) + [_ANY],
        out_specs=(_SEM, _SEM, *([_HBM] * (ns + nl)), pl.BlockSpec(memory_space=pltpu.VMEM)),
        out_shape=(sems, sems, *[pltpu.HBM(a.shape, a.dtype) for a in arrays], jax.ShapeDtypeStruct((8, LANE), F32)),
        input_output_aliases={i: 2 + i for i in range(ns + nl)},
        compiler_params=pltpu.CompilerParams(has_side_effects=_EFFECT),
    )(*[pltpu.with_memory_space_constraint(a, pltpu.HBM) for a in arrays], after)
    return res[0], res[1], list(res[2:2 + ns]), list(res[2 + ns:2 + ns + nl]), res[-1]


def _exchange_wait(mode, started, after, *, name):
    send_sems, recv_sems, srcs, lands, _ = started
    ns, nl = len(srcs), len(lands)

    def body(*refs):
        src, land = refs[:ns], refs[ns:ns + nl]
        send_sems, recv_sems = refs[ns + nl:ns + nl + 2]
        place = _place()
        for w in range(nl):
            for i in range(len(_EXCHANGE_BITS[mode])):
                cp = _exchange_copy(mode, src, land, w, i, place, send_sems, recv_sems, receiving=True)
                cp.wait_send()
                cp.wait_recv()

    arrays = list(srcs) + list(lands)
    res = pl.pallas_call(
        body, name=name, in_specs=[_HBM] * (ns + nl) + [_SEM, _SEM, _ANY], out_specs=[_HBM] * (ns + nl),
        out_shape=[pltpu.HBM(a.shape, a.dtype) for a in arrays],
        input_output_aliases={i: i for i in range(ns + nl)},
        compiler_params=pltpu.CompilerParams(has_side_effects=_EFFECT),
    )(*arrays, send_sems, recv_sems, after)
    return list(res[:ns]), list(res[ns:])


_LN_ROWS = ("ln1_g", "ln1_b", "ln2_g", "ln2_b", "ln3_g", "ln3_b")
_MISC_ROW = len(_LN_ROWS)
_MISC = (("pool_scale", 0, GDN_WIDTH), ("gdn_norm_w", GDN_WIDTH, HEAD_DIM), ("a_log", GDN_WIDTH + LANE, GDN_HEADS),
         ("dt_bias", GDN_WIDTH + 2 * LANE, GDN_HEADS), ("loss", GDN_WIDTH + 3 * LANE, 1))
_CONV_ROW = _MISC_ROW + 1
_CONV_ROWS = CONV_K * QKV_COLS // D_MODEL
_SMALL_ROWS = 16


def _pack_small(vals):
    pieces, at = [], 0
    for n, off, size in _MISC:
        pieces.append(jnp.zeros((off - at,), F32))
        pieces.append(vals[n].reshape(size).astype(F32) if n in vals else jnp.zeros((size,), F32))
        at = off + size
    pieces.append(jnp.zeros((D_MODEL - at,), F32))
    conv = vals["conv_w"].reshape(-1) if "conv_w" in vals else jnp.zeros((_CONV_ROWS * D_MODEL,), F32)
    tail = jnp.zeros(((_SMALL_ROWS - _CONV_ROW - _CONV_ROWS) * D_MODEL,), F32)
    flat = jnp.concatenate([vals[n].reshape(D_MODEL) for n in _LN_ROWS] + pieces + [conv, tail])
    return flat.reshape(_SMALL_ROWS, D_MODEL)


def _adamw_small(zone, mine, me, w, m, v):
    short = [(n, off, size) for n, off, size in _MISC if n != "loss"]

    def body(me_ref, z_ref, mine_ref, w_ref, m_ref, v_ref, *rest):
        outs, (g_s, d_s, nm_s, nv_s) = rest[:-4], rest[-4:]
        g = None
        for s in range(N_DEV):
            part = jnp.where(me_ref[0] == s, mine_ref[...], z_ref[s])
            g = part if g is None else g + part
        g_s[...] = g
        d_s[...], nm_s[...], nv_s[...] = _adamw_math(w_ref[...], g, m_ref[...], v_ref[...])
        k = 0
        for src in (g_s, d_s, nm_s, nv_s):
            for r in range(len(_LN_ROWS)):
                outs[k][...] = src[r:r + 1, :]
                k += 1
            for _, off, size in short:
                outs[k][...] = src[_MISC_ROW:_MISC_ROW + 1, off:off + size]
                k += 1
        outs[k][...] = g_s[_CONV_ROW:_CONV_ROW + _CONV_ROWS, :]
        outs[k + 1][...] = g_s[_MISC_ROW:_MISC_ROW + 1, :]

    rows, d = mine.shape
    per_quantity = [jax.ShapeDtypeStruct((1, D_MODEL), F32)] * len(_LN_ROWS) + [
        jax.ShapeDtypeStruct((1, size), F32) for _, _, size in short]
    out_shape = per_quantity * 4 + [jax.ShapeDtypeStruct((_CONV_ROWS, d), F32), jax.ShapeDtypeStruct((1, d), F32)]
    whole = lambda a: pl.BlockSpec(a.shape, lambda i, me_ref: (0,) * len(a.shape))
    res = pl.pallas_call(
        body,
        grid_spec=pltpu.PrefetchScalarGridSpec(
            num_scalar_prefetch=1, grid=(1,), in_specs=[whole(a) for a in (zone, mine, w, m, v)],
            out_specs=[whole(s) for s in out_shape], scratch_shapes=[pltpu.VMEM((rows, d), F32)] * 4),
        out_shape=out_shape, compiler_params=_params("arbitrary"), name="adamw_small",
    )(me, zone, mine, w, m, v)
    names = list(_LN_ROWS) + [n for n, _, _ in short]
    n_each = len(names)
    quantities = [dict(zip(names, res[q * n_each:(q + 1) * n_each])) for q in range(4)]
    return quantities, res[-2], res[-1]


_WEIGHT_ORDER = ("w_in", "conv_w", "a_log", "dt_bias", "gdn_norm_w", "pool_w", "pool_scale", "w_out", "ln1_g", "ln1_b",
                 "xq_w", "xk_w", "xv_w", "xo_w", "ln2_g", "ln2_b", "w_up", "w_down", "ln3_g", "ln3_b")


def _shard2d(name, a):
    if name == "w_in":
        return a.T
    return a.reshape(-1, a.shape[-1]) if name == "pool_w" else a


def _update_view(name, a):
    return jnp.transpose(a, (2, 0, 1)) if name == "w_in" else _shard2d(name, a[0])


def _shard_result(name, r, shape):
    return jnp.transpose(r, (1, 2, 0)) if name == "w_in" else r.reshape(shape)


def _gathered_to_full(name, gth):
    if name in ("w_up", "w_in"):
        return gth
    if name == "conv_w":
        return jnp.transpose(gth, (1, 0, 2)).reshape(gth.shape[1], N_DEV * gth.shape[2])
    if name == "pool_w":
        g4 = gth.reshape(N_DEV, POOL_GROUPS, POOL_GROUP_DIM // N_DEV, POOL_GROUP_DIM)
        return jnp.transpose(g4, (1, 0, 2, 3)).reshape(POOL_GROUPS, POOL_GROUP_DIM, POOL_GROUP_DIM)
    return gth.reshape(N_DEV * gth.shape[1], gth.shape[2])


def _full_to_chunks(name, full):
    if name == "w_up":
        return full
    if name == "pool_w":
        g4 = full.reshape(POOL_GROUPS, N_DEV, POOL_GROUP_DIM // N_DEV, POOL_GROUP_DIM)
        return jnp.transpose(g4, (1, 0, 2, 3)).reshape(N_DEV, POOL_GROUPS * POOL_GROUP_DIM // N_DEV, POOL_GROUP_DIM)
    return full.reshape(N_DEV, full.shape[0] // N_DEV, full.shape[1])


_GATHER_GROUPS = (("mixer", ("w_in", "conv_w", "pool_w")), ("attn", ("w_out", "xq_w", "xk_w", "xv_w", "xo_w")),
                  ("up", ("w_up",)), ("down", ("w_down",)))


def _grad_chunks(name, g):
    if name == "w_in":
        return _w_in_chunks(g.astype(BF16))
    return _full_to_chunks(name, g.astype(BF16))


def kernel(x, mem, w_in, conv_w, a_log, dt_bias, gdn_norm_w, pool_w, pool_scale, w_out, ln1_g, ln1_b, xq_w, xk_w, xv_w, xo_w, ln2_g, ln2_b, w_up, w_down, ln3_g, ln3_b, loss_target, m_w_in, m_conv_w, m_a_log, m_dt_bias, m_gdn_norm_w, m_pool_w, m_pool_scale, m_w_out, m_ln1_g, m_ln1_b, m_xq_w, m_xk_w, m_xv_w, m_xo_w, m_ln2_g, m_ln2_b, m_w_up, m_w_down, m_ln3_g, m_ln3_b, v_w_in, v_conv_w, v_a_log, v_dt_bias, v_gdn_norm_w, v_pool_w, v_pool_scale, v_w_out, v_ln1_g, v_ln1_b, v_xq_w, v_xk_w, v_xv_w, v_xo_w, v_ln2_g, v_ln2_b, v_w_up, v_w_down, v_ln3_g, v_ln3_b):
    args = dict(locals())
    wt = {n: args[n][0] for n in _WEIGHT_ORDER}
    mo = {n: args["m_" + n][0] for n in _WEIGHT_ORDER}
    vo = {n: args["v_" + n][0] for n in _WEIGHT_ORDER}

    me = _slot(*_place())
    me_arr = jnp.reshape(me, (1,)).astype(jnp.int32)
    nothing = jnp.zeros((8, LANE), F32)

    def landing_zones(names):
        shards = [_shard2d(n, wt[n]).astype(F32 if n == "conv_w" else BF16) for n in names]
        zones = [lax.dynamic_update_slice(lax.empty((N_DEV, *s.shape), s.dtype), s[None], (me, 0, 0)) for s in shards]
        return shards, zones

    chip_arr = jnp.reshape(me // 2, (1,)).astype(jnp.int32)
    core_arr = jnp.reshape(lax.axis_index("c"), (1,)).astype(jnp.int32)
    names_of = dict(_GATHER_GROUPS)
    gathers = {}
    prepared = {}

    def gather_near(group, after):
        shards, zones = prepared.pop(group) if group in prepared else landing_zones(names_of[group])
        gathers[group] = _exchange_start("gather_near", shards, zones, after, name="gather_near_" + group)
        return gathers[group][4]

    def gather_next(group, was, now, after):
        _, zones = _exchange_wait(was, gathers[group], after, name=f"{was}_{group}_wait")
        gathers[group] = _exchange_start(now, [], zones, nothing, name=f"{now}_{group}")
        return gathers[group][4]

    def gather_relay(group, after):
        return gather_next(group, "gather_near", "gather_relay", after)

    def gather_pass(group, after):
        return gather_next(group, "gather_relay", "gather_pass", after)

    def gathered(group, after):
        _, zones = _exchange_wait("gather_pass", gathers[group], after, name=f"gather_pass_{group}_wait")
        full = {n: _gathered_to_full(n, z) for n, z in zip(names_of[group], zones)}
        full.update({n: wt[n] for n in _VECTORS})
        return _group_weights(group, full)

    token = gather_near("mixer", nothing)
    x16 = _cast_bf16(x[0], name="cast_x")
    later = {group: landing_zones(names_of[group]) for group in ("attn", "up", "down")}
    token, x16, later = lax.optimization_barrier((token, x16, later))
    prepared.update(later)
    token = gather_pass("mixer", gather_relay("mixer", token))
    token = gather_near("attn", token)

    def weights_of(group, after):
        if group == "mixer":
            return gathered(group, token)
        if group == "ahead_conv":
            return gather_near("up", gather_relay("attn", after))[0:1, 0:1]
        if group == "ahead_scan":
            return gather_pass("attn", after)[0:1, 0:1]
        if group == "attn":
            return gathered(group, gather_near("down", gather_relay("up", after)))
        if group == "ahead_attn":
            return gather_relay("down", gather_pass("up", after))[0:1, 0:1]
        if group == "up":
            return gathered(group, gather_pass("down", after))
        return gathered(group, after)

    scatters = {}
    in_flight = []

    def chip_stage(after):
        group, names, started = in_flight.pop()
        chunks, from_sibling = _exchange_wait("scatter_sibling", started, after, name=f"scatter_sibling_{group}_wait")
        sums, alike = [None] * len(names), {}
        for i, chunk in enumerate(chunks):
            alike.setdefault(chunk.shape, []).append(i)
        for same in alike.values():
            res = _chip_sums([chunks[i] for i in same], [from_sibling[i] for i in same], core_arr,
                             name="chip_sums_" + names[same[0]])
            for i, r in zip(same, res):
                sums[i] = r
        scatters[group] = (names, _exchange_start("scatter_chips", sums, [lax.empty(s.shape, s.dtype) for s in sums],
                                                  nothing, name="scatter_chips_" + group))
        return scatters[group][1][4]

    small_sent = []

    def grads_ready(group, grads):
        if group == "tick":
            return chip_stage(grads["after"])[0:1, 0:1] if in_flight else None
        if group == "small":
            small = _finish_small_grads(grads)
            small["loss"] = 0.5 * grads["sq"][0:1, 0] / D_MODEL
            packed = _pack_small(small)
            zone = lax.empty((N_DEV, *packed.shape), F32)
            small_sent.append(_exchange_start("all_small", [packed], [zone], nothing, name="small_grads_start"))
            return small_sent[0][4][0:1, 0:1]
        names = tuple(grads)
        chunks = [_grad_chunks(n, grads[n]) for n in names]
        token = chip_stage(chunks[0]) if in_flight else nothing
        zones = [lax.empty((N_CHIPS, *c.shape[1:]), c.dtype) for c in chunks]
        started = _exchange_start("scatter_sibling", chunks, zones, token, name="scatter_sibling_" + group)
        in_flight.append((group, names, started))
        if group != "in":
            return started[4][0:1, 0:1]
        return chip_stage(update_group("mlp", started[4], count=1))[0:1, 0:1]

    out = {}
    arrived = {}

    def update_group(group, after, count=None):
        if group not in arrived:
            names, started = scatters.pop(group)
            sums, lands = _exchange_wait("scatter_chips", started, after, name=f"scatter_chips_{group}_wait")
            arrived[group] = list(zip(names, lands, sums))
        todo = arrived[group][:count]
        arrived[group] = arrived[group][len(todo):]
        for n, parts, own in todo:
            res = _adamw_shard(parts, own, chip_arr, _update_view(n, args[n]), _update_view(n, args["m_" + n]),
                               _update_view(n, args["v_" + n]), after=after, name="adamw_" + n)
            out[n] = [_shard_result(n, r, args[n].shape) for r in res]
            after = res[1]
        return after

    sq, grad_x, g = _local_step(x[0], x16, mem[0], loss_target[0], weights_of, grads_ready)

    after = grad_x
    for group in list(arrived) + list(scatters):
        after = update_group(group, after)

    (packed,), (zone,) = _exchange_wait("all_small", small_sent[0], after, name="small_grads_wait")
    quantities, conv_rows, misc_row = _adamw_small(
        zone, packed, me_arr, _pack_small({n: wt[n] for n in _VECTORS}), _pack_small({n: mo[n] for n in _VECTORS}),
        _pack_small({n: vo[n] for n in _VECTORS}))
    cols = conv_w.shape[-1]
    conv_mine = lax.dynamic_slice(conv_rows.reshape(CONV_K, QKV_COLS), (0, me * cols), (CONV_K, cols))[None]
    res = _adamw_shard(conv_mine, conv_mine, jnp.zeros((1,), jnp.int32), wt["conv_w"], mo["conv_w"], vo["conv_w"],
                       name="adamw_conv_w")
    out["conv_w"] = [r.reshape(conv_w.shape) for r in res]
    for n in _VECTORS:
        out[n] = [q[n] for q in quantities]
    loss_at = dict((n, off) for n, off, _ in _MISC)["loss"]

    return (misc_row[0, loss_at], grad_x[None], *[out[n][0] for n in _WEIGHT_ORDER], *[out[n][1] for n in _WEIGHT_ORDER],
            *[out[n][2] for n in _WEIGHT_ORDER], *[out[n][3] for n in _WEIGHT_ORDER])
```
